```python
import math
import jax, jax.numpy as jnp
from jax import lax
import numpy as np

D_MODEL = 1024
BATCH = 8
SEQ = 4096
DEPTH = 2

N_A_LAYERS = DEPTH // 2
N_B_LAYERS = DEPTH - N_A_LAYERS
CONV_WIDTH = 2 * D_MODEL
CONV_KERNEL = 31
N_HEADS = 16
QK_NOPE_DIM = 128
QK_ROPE_DIM = 64
V_HEAD_DIM = 128
KV_LORA_RANK = D_MODEL // 4
Q_LORA_RANK = D_MODEL // 2
ROPE_THETA = 10000.0
Q_BLOCK = 128
LN_EPS = 1e-5
RMS_EPS = 1e-6
MASK_VALUE = -1e30

kernel_name = "yoco_conformer_conv_mla_deepnorm"


def layer_norm(x, g, b):
    xf = x.astype(jnp.float32)
    mu = jnp.mean(xf, axis=-1, keepdims=True)
    var = jnp.mean(jnp.square(xf - mu), axis=-1, keepdims=True)
    return ((xf - mu) * lax.rsqrt(var + LN_EPS)).astype(x.dtype) * g + b


def rms_norm(x, g):
    xf = x.astype(jnp.float32)
    ms = jnp.mean(jnp.square(xf), axis=-1, keepdims=True)
    return (xf * lax.rsqrt(ms + RMS_EPS)).astype(x.dtype) * g


def rope_tables(positions, dtype):
    freqs = ROPE_THETA ** (-jnp.arange(0, QK_ROPE_DIM, 2, dtype=jnp.float32) / QK_ROPE_DIM)
    ang = positions.astype(jnp.float32)[..., None] * freqs
    return jnp.cos(ang).astype(dtype), jnp.sin(ang).astype(dtype)


def apply_rope(x, cos, sin):
    x1, x2 = jnp.split(x, 2, axis=-1)
    return jnp.concatenate([x1 * cos - x2 * sin, x2 * cos + x1 * sin], axis=-1)


def conformer_conv_branch(h, w_in, b_in, conv_w, conv_b, norm_g, norm_b, w_out, b_out):
    proj = h @ w_in + b_in
    val, glu_gate, z = jnp.split(proj, 3, axis=-1)
    u = val * jax.nn.sigmoid(glu_gate)
    u = lax.conv_general_dilated(
        u, conv_w[:, None, :], window_strides=(1,), padding=[(CONV_KERNEL - 1, 0)],
        dimension_numbers=('NWC', 'WIO', 'NWC'), feature_group_count=CONV_WIDTH) + conv_b
    u = jax.nn.silu(layer_norm(u, norm_g, norm_b))
    u = u * jax.nn.silu(z)
    return u @ w_out + b_out


def shared_latent_kv(h, w_down, kv_norm_g, w_uk, w_uv, cos, sin):
    ckv_kr = h @ w_down
    c_kv, k_rope = jnp.split(ckv_kr, [KV_LORA_RANK], axis=-1)
    c_kv = rms_norm(c_kv, kv_norm_g)
    k_nope = jnp.einsum('bsr,rhd->bshd', c_kv, w_uk)
    v = jnp.einsum('bsr,rhd->bshd', c_kv, w_uv)
    k_rope = apply_rope(k_rope, cos, sin)
    return k_nope, k_rope, v


def causal_mla_attention(q_nope, q_rope, k_nope, k_rope, v):
    scale = 1.0 / math.sqrt(QK_NOPE_DIM + QK_ROPE_DIM)
    seq = q_nope.shape[1]
    outs = []
    for blk in range(seq // Q_BLOCK):
        q0 = blk * Q_BLOCK
        kend = q0 + Q_BLOCK
        s = (jnp.einsum('bqhd,bkhd->bhqk', q_nope[:, q0:kend], k_nope[:, :kend])
             + jnp.einsum('bqhr,bkr->bhqk', q_rope[:, q0:kend], k_rope[:, :kend]))
        s = s.astype(jnp.float32) * scale
        mask = jnp.arange(kend)[None, :] <= (q0 + jnp.arange(Q_BLOCK))[:, None]
        s = jnp.where(mask, s, MASK_VALUE)
        p = jax.nn.softmax(s, axis=-1).astype(v.dtype)
        outs.append(jnp.einsum('bhqk,bkhd->bqhd', p, v[:, :kend]))
    return jnp.concatenate(outs, axis=1)


def mla_branch(h, k_nope, k_rope, v, w_in, q_norm_g, w_uq, w_out, cos, sin):
    bsz, seq, _ = h.shape
    proj = h @ w_in
    c_q, z = jnp.split(proj, [Q_LORA_RANK], axis=-1)
    q = jnp.einsum('bsr,rhd->bshd', rms_norm(c_q, q_norm_g), w_uq)
    q_nope, q_rope = jnp.split(q, [QK_NOPE_DIM], axis=-1)
    q_rope = apply_rope(q_rope, cos[:, :, None, :], sin[:, :, None, :])
    o = causal_mla_attention(q_nope, q_rope, k_nope, k_rope, v)
    o = o.reshape(bsz, seq, N_HEADS * V_HEAD_DIM) * jax.nn.silu(z)
    return o @ w_out


def _fwd_setup_inputs(seed: int = 0) -> dict:
    key = jax.random.key(seed)
    ks = jax.random.split(key, 20)
    f32 = jnp.float32
    beta = (8.0 * DEPTH) ** -0.25
    E = CONV_WIDTH
    HV = N_HEADS * V_HEAD_DIM

    def dense(k, shape, fan_in, scale=1.0):
        return jax.random.normal(k, shape, f32) * (scale * fan_in ** -0.5)

    def gain(k, shape):
        return 1.0 + 0.02 * jax.random.normal(k, shape, f32)

    def small(k, shape):
        return 0.02 * jax.random.normal(k, shape, f32)

    return {
        "x": jax.random.normal(ks[0], (BATCH, SEQ, D_MODEL), f32),
        "positions": jnp.broadcast_to(jnp.arange(SEQ, dtype=jnp.int32), (BATCH, SEQ)),
        "ln_g": gain(ks[1], (DEPTH, D_MODEL)),
        "ln_b": small(ks[2], (DEPTH, D_MODEL)),
        "a_w_in": dense(ks[3], (N_A_LAYERS, D_MODEL, 3 * E), D_MODEL),
        "a_b_in": small(ks[4], (N_A_LAYERS, 3 * E)),
        "a_conv_w": dense(ks[5], (N_A_LAYERS, CONV_KERNEL, E), CONV_KERNEL),
        "a_conv_b": small(ks[6], (N_A_LAYERS, E)),
        "a_norm_g": gain(ks[7], (N_A_LAYERS, E)),
        "a_norm_b": small(ks[8], (N_A_LAYERS, E)),
        "a_w_out": dense(ks[9], (N_A_LAYERS, E, D_MODEL), E, beta),
        "a_b_out": small(ks[10], (N_A_LAYERS, D_MODEL)),
        "kv_w_down": dense(ks[11], (D_MODEL, KV_LORA_RANK + QK_ROPE_DIM), D_MODEL),
        "kv_norm_g": gain(ks[12], (KV_LORA_RANK,)),
        "kv_w_uk": dense(ks[13], (KV_LORA_RANK, N_HEADS, QK_NOPE_DIM), KV_LORA_RANK),
        "kv_w_uv": dense(ks[14], (KV_LORA_RANK, N_HEADS, V_HEAD_DIM), KV_LORA_RANK),
        "b_w_in": dense(ks[15], (N_B_LAYERS, D_MODEL, Q_LORA_RANK + HV), D_MODEL),
        "b_q_norm_g": gain(ks[16], (N_B_LAYERS, Q_LORA_RANK)),
        "b_w_uq": dense(ks[17], (N_B_LAYERS, Q_LORA_RANK, N_HEADS, QK_NOPE_DIM + QK_ROPE_DIM), Q_LORA_RANK),
        "b_w_out": dense(ks[18], (N_B_LAYERS, HV, D_MODEL), HV, beta),
    }


def _fwd_reference(x, positions, ln_g, ln_b, a_w_in, a_b_in, a_conv_w, a_conv_b, a_norm_g, a_norm_b,
              a_w_out, a_b_out, kv_w_down, kv_norm_g, kv_w_uk, kv_w_uv, b_w_in, b_q_norm_g,
              b_w_uq, b_w_out):
    alpha = (2.0 * DEPTH) ** 0.25
    cos, sin = rope_tables(positions, x.dtype)
    h = x
    k_nope = k_rope = v = None
    for layer in range(DEPTH):
        if layer < N_A_LAYERS:
            i = layer
            y = conformer_conv_branch(h, a_w_in[i], a_b_in[i], a_conv_w[i], a_conv_b[i],
                                      a_norm_g[i], a_norm_b[i], a_w_out[i], a_b_out[i])
        else:
            if layer == N_A_LAYERS:
                k_nope, k_rope, v = shared_latent_kv(h, kv_w_down, kv_norm_g, kv_w_uk, kv_w_uv, cos, sin)
            j = layer - N_A_LAYERS
            y = mla_branch(h, k_nope, k_rope, v, b_w_in[j], b_q_norm_g[j], b_w_uq[j], b_w_out[j], cos, sin)
        h = layer_norm(alpha * h + y, ln_g[layer], ln_b[layer])
    return h


import jax as _jax
import jax.numpy as _jnp

TWIN_FORMAT = 'train_step'
FWD_PARAMS = ['x', 'positions', 'ln_g', 'ln_b', 'a_w_in', 'a_b_in', 'a_conv_w', 'a_conv_b', 'a_norm_g', 'a_norm_b', 'a_w_out', 'a_b_out', 'kv_w_down', 'kv_norm_g', 'kv_w_uk', 'kv_w_uv', 'b_w_in', 'b_q_norm_g', 'b_w_uq', 'b_w_out']
TWIN_WEIGHTS = ['ln_g', 'ln_b', 'a_w_in', 'a_b_in', 'a_conv_w', 'a_conv_b', 'a_norm_g', 'a_norm_b', 'a_w_out', 'a_b_out', 'kv_w_down', 'kv_norm_g', 'kv_w_uk', 'kv_w_uv', 'b_w_in', 'b_q_norm_g', 'b_w_uq', 'b_w_out']
TWIN_DIFF_INPUT = 'x'
TWIN_INPUTS = ['x', 'positions', 'ln_g', 'ln_b', 'a_w_in', 'a_b_in', 'a_conv_w', 'a_conv_b', 'a_norm_g', 'a_norm_b', 'a_w_out', 'a_b_out', 'kv_w_down', 'kv_norm_g', 'kv_w_uk', 'kv_w_uv', 'b_w_in', 'b_q_norm_g', 'b_w_uq', 'b_w_out', 'loss_target', 'm_ln_g', 'm_ln_b', 'm_a_w_in', 'm_a_b_in', 'm_a_conv_w', 'm_a_conv_b', 'm_a_norm_g', 'm_a_norm_b', 'm_a_w_out', 'm_a_b_out', 'm_kv_w_down', 'm_kv_norm_g', 'm_kv_w_uk', 'm_kv_w_uv', 'm_b_w_in', 'm_b_q_norm_g', 'm_b_w_uq', 'm_b_w_out', 'v_ln_g', 'v_ln_b', 'v_a_w_in', 'v_a_b_in', 'v_a_conv_w', 'v_a_conv_b', 'v_a_norm_g', 'v_a_norm_b', 'v_a_w_out', 'v_a_b_out', 'v_kv_w_down', 'v_kv_norm_g', 'v_kv_w_uk', 'v_kv_w_uv', 'v_b_w_in', 'v_b_q_norm_g', 'v_b_w_uq', 'v_b_w_out']
TWIN_OUTPUTS = ['loss', 'grad_x', 'grad_ln_g', 'grad_ln_b', 'grad_a_w_in', 'grad_a_b_in', 'grad_a_conv_w', 'grad_a_conv_b', 'grad_a_norm_g', 'grad_a_norm_b', 'grad_a_w_out', 'grad_a_b_out', 'grad_kv_w_down', 'grad_kv_norm_g', 'grad_kv_w_uk', 'grad_kv_w_uv', 'grad_b_w_in', 'grad_b_q_norm_g', 'grad_b_w_uq', 'grad_b_w_out', 'delta_ln_g', 'delta_ln_b', 'delta_a_w_in', 'delta_a_b_in', 'delta_a_conv_w', 'delta_a_conv_b', 'delta_a_norm_g', 'delta_a_norm_b', 'delta_a_w_out', 'delta_a_b_out', 'delta_kv_w_down', 'delta_kv_norm_g', 'delta_kv_w_uk', 'delta_kv_w_uv', 'delta_b_w_in', 'delta_b_q_norm_g', 'delta_b_w_uq', 'delta_b_w_out', 'new_m_ln_g', 'new_m_ln_b', 'new_m_a_w_in', 'new_m_a_b_in', 'new_m_a_conv_w', 'new_m_a_conv_b', 'new_m_a_norm_g', 'new_m_a_norm_b', 'new_m_a_w_out', 'new_m_a_b_out', 'new_m_kv_w_down', 'new_m_kv_norm_g', 'new_m_kv_w_uk', 'new_m_kv_w_uv', 'new_m_b_w_in', 'new_m_b_q_norm_g', 'new_m_b_w_uq', 'new_m_b_w_out', 'new_v_ln_g', 'new_v_ln_b', 'new_v_a_w_in', 'new_v_a_b_in', 'new_v_a_conv_w', 'new_v_a_conv_b', 'new_v_a_norm_g', 'new_v_a_norm_b', 'new_v_a_w_out', 'new_v_a_b_out', 'new_v_kv_w_down', 'new_v_kv_norm_g', 'new_v_kv_w_uk', 'new_v_kv_w_uv', 'new_v_b_w_in', 'new_v_b_q_norm_g', 'new_v_b_w_uq', 'new_v_b_w_out']
TWIN_LEAF_KINDS = {'loss': 'loss', 'grad_x': 'grad_x', 'grad_ln_g': 'grad_w', 'grad_ln_b': 'grad_w', 'grad_a_w_in': 'grad_w', 'grad_a_b_in': 'grad_w', 'grad_a_conv_w': 'grad_w', 'grad_a_conv_b': 'grad_w', 'grad_a_norm_g': 'grad_w', 'grad_a_norm_b': 'grad_w', 'grad_a_w_out': 'grad_w', 'grad_a_b_out': 'grad_w', 'grad_kv_w_down': 'grad_w', 'grad_kv_norm_g': 'grad_w', 'grad_kv_w_uk': 'grad_w', 'grad_kv_w_uv': 'grad_w', 'grad_b_w_in': 'grad_w', 'grad_b_q_norm_g': 'grad_w', 'grad_b_w_uq': 'grad_w', 'grad_b_w_out': 'grad_w', 'delta_ln_g': 'delta_w', 'delta_ln_b': 'delta_w', 'delta_a_w_in': 'delta_w', 'delta_a_b_in': 'delta_w', 'delta_a_conv_w': 'delta_w', 'delta_a_conv_b': 'delta_w', 'delta_a_norm_g': 'delta_w', 'delta_a_norm_b': 'delta_w', 'delta_a_w_out': 'delta_w', 'delta_a_b_out': 'delta_w', 'delta_kv_w_down': 'delta_w', 'delta_kv_norm_g': 'delta_w', 'delta_kv_w_uk': 'delta_w', 'delta_kv_w_uv': 'delta_w', 'delta_b_w_in': 'delta_w', 'delta_b_q_norm_g': 'delta_w', 'delta_b_w_uq': 'delta_w', 'delta_b_w_out': 'delta_w', 'new_m_ln_g': 'new_m', 'new_m_ln_b': 'new_m', 'new_m_a_w_in': 'new_m', 'new_m_a_b_in': 'new_m', 'new_m_a_conv_w': 'new_m', 'new_m_a_conv_b': 'new_m', 'new_m_a_norm_g': 'new_m', 'new_m_a_norm_b': 'new_m', 'new_m_a_w_out': 'new_m', 'new_m_a_b_out': 'new_m', 'new_m_kv_w_down': 'new_m', 'new_m_kv_norm_g': 'new_m', 'new_m_kv_w_uk': 'new_m', 'new_m_kv_w_uv': 'new_m', 'new_m_b_w_in': 'new_m', 'new_m_b_q_norm_g': 'new_m', 'new_m_b_w_uq': 'new_m', 'new_m_b_w_out': 'new_m', 'new_v_ln_g': 'new_v', 'new_v_ln_b': 'new_v', 'new_v_a_w_in': 'new_v', 'new_v_a_b_in': 'new_v', 'new_v_a_conv_w': 'new_v', 'new_v_a_conv_b': 'new_v', 'new_v_a_norm_g': 'new_v', 'new_v_a_norm_b': 'new_v', 'new_v_a_w_out': 'new_v', 'new_v_a_b_out': 'new_v', 'new_v_kv_w_down': 'new_v', 'new_v_kv_norm_g': 'new_v', 'new_v_kv_w_uk': 'new_v', 'new_v_kv_w_uv': 'new_v', 'new_v_b_w_in': 'new_v', 'new_v_b_q_norm_g': 'new_v', 'new_v_b_w_uq': 'new_v', 'new_v_b_w_out': 'new_v'}


def _forward(args):
    return _fwd_reference(*[args[k] for k in FWD_PARAMS])


def _output_shape():
    out = _jax.eval_shape(lambda: _forward(_fwd_setup_inputs(0)))
    return out.shape, out.dtype

N_MICROBATCH = 1
ADAM_LR = 0.001
ADAM_B1 = 0.9
ADAM_B2 = 0.999
ADAM_EPS = 1e-08
ADAM_WD = 0.01
ADAM_STEP = 10
PER_EXAMPLE_BATCH_AXIS = {'x': 0, 'positions': 0, 'loss_target': 0}
SHARED_INPUTS = []
_WEIGHT_DTYPES = {'ln_g': _jnp.float32, 'ln_b': _jnp.float32, 'a_w_in': _jnp.float32, 'a_b_in': _jnp.float32, 'a_conv_w': _jnp.float32, 'a_conv_b': _jnp.float32, 'a_norm_g': _jnp.float32, 'a_norm_b': _jnp.float32, 'a_w_out': _jnp.float32, 'a_b_out': _jnp.float32, 'kv_w_down': _jnp.float32, 'kv_norm_g': _jnp.float32, 'kv_w_uk': _jnp.float32, 'kv_w_uv': _jnp.float32, 'b_w_in': _jnp.float32, 'b_q_norm_g': _jnp.float32, 'b_w_uq': _jnp.float32, 'b_w_out': _jnp.float32}
MOMENT_SCALE = {'ln_g': 2.264780e+01, 'ln_b': 7.126686e-01, 'a_w_in': 1.383212e-02, 'a_b_in': 1.543519e-02, 'a_conv_w': 1.632803e-02, 'a_conv_b': 3.557395e-02, 'a_norm_g': 1.884033e-02, 'a_norm_b': 1.784247e-02, 'a_w_out': 4.551515e-02, 'a_b_out': 3.317916e-01, 'kv_w_down': 1.902294e-02, 'kv_norm_g': 1.868549e-02, 'kv_w_uk': 4.011393e-03, 'kv_w_uv': 5.889296e-03, 'b_w_in': 6.949710e-03, 'b_q_norm_g': 9.708793e-03, 'b_w_uq': 3.907421e-03, 'b_w_out': 1.670238e-02}


def _to_microbatches(a, axis):
    t = _jnp.moveaxis(a, axis, 0)
    t = t.reshape((N_MICROBATCH, t.shape[0] // N_MICROBATCH) + t.shape[1:])
    return _jnp.moveaxis(t, 1, axis + 1)


def setup_inputs(seed: int = 0) -> dict:
    inp = _fwd_setup_inputs(seed)
    key = _jax.random.fold_in(_jax.random.key(seed), 7919)
    shape, _ = _output_shape()
    out = dict(inp)
    out["loss_target"] = _jax.random.normal(_jax.random.fold_in(key, 0), shape, _jnp.float32)
    for i, name in enumerate(TWIN_WEIGHTS):
        w = inp[name].astype(_jnp.float32)
        if MOMENT_SCALE is None:
            s = _jnp.sqrt(_jnp.mean(_jnp.square(w)) + 1e-30)
        else:
            s = MOMENT_SCALE[name]
        km, kv = _jax.random.split(_jax.random.fold_in(key, i + 1))
        out[name] = w
        out["m_" + name] = s * _jax.random.normal(km, w.shape, _jnp.float32)
        out["v_" + name] = (s * s) * _jax.random.uniform(kv, w.shape, _jnp.float32, 0.5, 1.5)
    if N_MICROBATCH > 1:
        for name, axis in PER_EXAMPLE_BATCH_AXIS.items():
            out[name] = _to_microbatches(out[name], axis)
    return {'x': out['x'], 'positions': out['positions'], 'ln_g': out['ln_g'], 'ln_b': out['ln_b'], 'a_w_in': out['a_w_in'], 'a_b_in': out['a_b_in'], 'a_conv_w': out['a_conv_w'], 'a_conv_b': out['a_conv_b'], 'a_norm_g': out['a_norm_g'], 'a_norm_b': out['a_norm_b'], 'a_w_out': out['a_w_out'], 'a_b_out': out['a_b_out'], 'kv_w_down': out['kv_w_down'], 'kv_norm_g': out['kv_norm_g'], 'kv_w_uk': out['kv_w_uk'], 'kv_w_uv': out['kv_w_uv'], 'b_w_in': out['b_w_in'], 'b_q_norm_g': out['b_q_norm_g'], 'b_w_uq': out['b_w_uq'], 'b_w_out': out['b_w_out'], 'loss_target': out['loss_target'], 'm_ln_g': out['m_ln_g'], 'm_ln_b': out['m_ln_b'], 'm_a_w_in': out['m_a_w_in'], 'm_a_b_in': out['m_a_b_in'], 'm_a_conv_w': out['m_a_conv_w'], 'm_a_conv_b': out['m_a_conv_b'], 'm_a_norm_g': out['m_a_norm_g'], 'm_a_norm_b': out['m_a_norm_b'], 'm_a_w_out': out['m_a_w_out'], 'm_a_b_out': out['m_a_b_out'], 'm_kv_w_down': out['m_kv_w_down'], 'm_kv_norm_g': out['m_kv_norm_g'], 'm_kv_w_uk': out['m_kv_w_uk'], 'm_kv_w_uv': out['m_kv_w_uv'], 'm_b_w_in': out['m_b_w_in'], 'm_b_q_norm_g': out['m_b_q_norm_g'], 'm_b_w_uq': out['m_b_w_uq'], 'm_b_w_out': out['m_b_w_out'], 'v_ln_g': out['v_ln_g'], 'v_ln_b': out['v_ln_b'], 'v_a_w_in': out['v_a_w_in'], 'v_a_b_in': out['v_a_b_in'], 'v_a_conv_w': out['v_a_conv_w'], 'v_a_conv_b': out['v_a_conv_b'], 'v_a_norm_g': out['v_a_norm_g'], 'v_a_norm_b': out['v_a_norm_b'], 'v_a_w_out': out['v_a_w_out'], 'v_a_b_out': out['v_a_b_out'], 'v_kv_w_down': out['v_kv_w_down'], 'v_kv_norm_g': out['v_kv_norm_g'], 'v_kv_w_uk': out['v_kv_w_uk'], 'v_kv_w_uv': out['v_kv_w_uv'], 'v_b_w_in': out['v_b_w_in'], 'v_b_q_norm_g': out['v_b_q_norm_g'], 'v_b_w_uq': out['v_b_w_uq'], 'v_b_w_out': out['v_b_w_out']}


def _loss(weights, diff, rest, loss_target):
    with _jax.named_scope("forward"):
        args = {**rest, TWIN_DIFF_INPUT: diff, **{k: w.astype(_WEIGHT_DTYPES[k]) for k, w in weights.items()}}
        y = _forward(args)
    with _jax.named_scope("loss_head"):
        err = _jnp.square(y.astype(_jnp.float32) - loss_target)
        return 0.5 * _jnp.sum(_jnp.mean(err, axis=-1)) if err.ndim else 0.5 * err


def _adamw(w, g, m, v):
    m = ADAM_B1 * m + (1.0 - ADAM_B1) * g
    v = ADAM_B2 * v + (1.0 - ADAM_B2) * _jnp.square(g)
    m_hat = m / (1.0 - ADAM_B1 ** ADAM_STEP)
    v_hat = v / (1.0 - ADAM_B2 ** ADAM_STEP)
    delta = -ADAM_LR * (m_hat / (_jnp.sqrt(v_hat) + ADAM_EPS) + ADAM_WD * w)
    return delta, m, v


def reference(x, positions, ln_g, ln_b, a_w_in, a_b_in, a_conv_w, a_conv_b, a_norm_g, a_norm_b, a_w_out, a_b_out, kv_w_down, kv_norm_g, kv_w_uk, kv_w_uv, b_w_in, b_q_norm_g, b_w_uq, b_w_out, loss_target, m_ln_g, m_ln_b, m_a_w_in, m_a_b_in, m_a_conv_w, m_a_conv_b, m_a_norm_g, m_a_norm_b, m_a_w_out, m_a_b_out, m_kv_w_down, m_kv_norm_g, m_kv_w_uk, m_kv_w_uv, m_b_w_in, m_b_q_norm_g, m_b_w_uq, m_b_w_out, v_ln_g, v_ln_b, v_a_w_in, v_a_b_in, v_a_conv_w, v_a_conv_b, v_a_norm_g, v_a_norm_b, v_a_w_out, v_a_b_out, v_kv_w_down, v_kv_norm_g, v_kv_w_uk, v_kv_w_uv, v_b_w_in, v_b_q_norm_g, v_b_w_uq, v_b_w_out):
    given = dict(x=x, positions=positions, ln_g=ln_g, ln_b=ln_b, a_w_in=a_w_in, a_b_in=a_b_in, a_conv_w=a_conv_w, a_conv_b=a_conv_b, a_norm_g=a_norm_g, a_norm_b=a_norm_b, a_w_out=a_w_out, a_b_out=a_b_out, kv_w_down=kv_w_down, kv_norm_g=kv_norm_g, kv_w_uk=kv_w_uk, kv_w_uv=kv_w_uv, b_w_in=b_w_in, b_q_norm_g=b_q_norm_g, b_w_uq=b_w_uq, b_w_out=b_w_out, loss_target=loss_target, m_ln_g=m_ln_g, m_ln_b=m_ln_b, m_a_w_in=m_a_w_in, m_a_b_in=m_a_b_in, m_a_conv_w=m_a_conv_w, m_a_conv_b=m_a_conv_b, m_a_norm_g=m_a_norm_g, m_a_norm_b=m_a_norm_b, m_a_w_out=m_a_w_out, m_a_b_out=m_a_b_out, m_kv_w_down=m_kv_w_down, m_kv_norm_g=m_kv_norm_g, m_kv_w_uk=m_kv_w_uk, m_kv_w_uv=m_kv_w_uv, m_b_w_in=m_b_w_in, m_b_q_norm_g=m_b_q_norm_g, m_b_w_uq=m_b_w_uq, m_b_w_out=m_b_w_out, v_ln_g=v_ln_g, v_ln_b=v_ln_b, v_a_w_in=v_a_w_in, v_a_b_in=v_a_b_in, v_a_conv_w=v_a_conv_w, v_a_conv_b=v_a_conv_b, v_a_norm_g=v_a_norm_g, v_a_norm_b=v_a_norm_b, v_a_w_out=v_a_w_out, v_a_b_out=v_a_b_out, v_kv_w_down=v_kv_w_down, v_kv_norm_g=v_kv_norm_g, v_kv_w_uk=v_kv_w_uk, v_kv_w_uv=v_kv_w_uv, v_b_w_in=v_b_w_in, v_b_q_norm_g=v_b_q_norm_g, v_b_w_uq=v_b_w_uq, v_b_w_out=v_b_w_out)
    weights = {n: given[n] for n in TWIN_WEIGHTS}
    shared = {n: given[n] for n in SHARED_INPUTS}
    per_example = {n: given[n] for n in ['x', 'positions']}
    grad_fn = _jax.value_and_grad(_loss, argnums=(0, 1))

    def one_microbatch(ex, loss_target):
        ex = dict(ex)
        diff = ex.pop(TWIN_DIFF_INPUT)
        return grad_fn(weights, diff, {**shared, **ex}, loss_target)

    if N_MICROBATCH == 1:
        loss, (grad_w, grad_x) = one_microbatch(per_example, given["loss_target"])
    else:
        def body(carry, xs):
            loss_sum, grad_sum = carry
            l_k, (gw_k, gx_k) = one_microbatch(xs[0], xs[1])
            with _jax.named_scope("update"):
                return (loss_sum + l_k, _jax.tree.map(_jnp.add, grad_sum, gw_k)), gx_k

        init = (_jnp.zeros((), _jnp.float32), _jax.tree.map(_jnp.zeros_like, weights))
        (loss, grad_w), grad_x = _jax.lax.scan(body, init, (per_example, given["loss_target"]))
    with _jax.named_scope("update"):
        delta_w, new_m, new_v = {}, {}, {}
        for n in TWIN_WEIGHTS:
            delta_w[n], new_m[n], new_v[n] = _adamw(weights[n], grad_w[n], given["m_" + n], given["v_" + n])
    return (loss, grad_x, *[grad_w[n] for n in TWIN_WEIGHTS], *[delta_w[n] for n in TWIN_WEIGHTS],
            *[new_m[n] for n in TWIN_WEIGHTS], *[new_v[n] for n in TWIN_WEIGHTS])
```

```python
import functools
import math

import jax
import jax.numpy as jnp
from jax import lax
from jax.experimental import pallas as pl
from jax.experimental.pallas import tpu as pltpu

F32 = jnp.float32
BF16 = jnp.bfloat16

D_MODEL = 1024
DEPTH = 2
CONV_WIDTH = 2 * D_MODEL
CONV_KERNEL = 31
N_HEADS = 16
QK_NOPE_DIM = 128
QK_ROPE_DIM = 64
V_HEAD_DIM = 128
KV_LORA_RANK = D_MODEL // 4
Q_LORA_RANK = D_MODEL // 2
ROPE_THETA = 10000.0
LN_EPS = 1e-5
RMS_EPS = 1e-6
MASK_VALUE = -1e30
ALPHA = (2.0 * DEPTH) ** 0.25
ATTN_SCALE = 1.0 / math.sqrt(QK_NOPE_DIM + QK_ROPE_DIM)

ADAM_LR = 0.001
ADAM_B1 = 0.9
ADAM_B2 = 0.999
ADAM_EPS = 1e-08
ADAM_WD = 0.01
ADAM_STEP = 10

N_DEV = 8
LANES = 128
HEAD_PAD = 256
HALO = 32
VMEM_LIMIT = 56 * 1024 * 1024

WEIGHT_NAMES = ['ln_g', 'ln_b', 'a_w_in', 'a_b_in', 'a_conv_w', 'a_conv_b', 'a_norm_g', 'a_norm_b',
                'a_w_out', 'a_b_out', 'kv_w_down', 'kv_norm_g', 'kv_w_uk', 'kv_w_uv', 'b_w_in',
                'b_q_norm_g', 'b_w_uq', 'b_w_out']
REPLICATED = ('ln_g', 'ln_b', 'kv_norm_g', 'b_q_norm_g')
MATMUL_WEIGHTS = ('a_w_in', 'a_w_out', 'kv_w_down', 'kv_w_uk', 'kv_w_uv', 'b_w_in', 'b_w_uq', 'b_w_out')
SMALL_WEIGHTS = ('a_b_in', 'a_conv_w', 'a_conv_b', 'a_norm_g', 'a_norm_b', 'a_b_out')

NN = (((1,), (0,)), ((), ()))
NT = (((1,), (1,)), ((), ()))
TN = (((0,), (0,)), ((), ()))


def _cparams(sem):
    return pltpu.CompilerParams(dimension_semantics=sem, vmem_limit_bytes=VMEM_LIMIT)


def _sigmoid(x):
    return 1.0 / (1.0 + jnp.exp(-x))


def _aligned(v, m):
    return v if isinstance(v, int) else pl.multiple_of(v, m)


def _swap_rope_halves(t):
    lane = lax.broadcasted_iota(jnp.int32, t.shape, 1)
    return jnp.where(lane < QK_ROPE_DIM // 2, pltpu.roll(t, LANES - QK_ROPE_DIM // 2, 1),
                     pltpu.roll(t, QK_ROPE_DIM // 2, 1))


def _matmul(a, b, *, name, bias=None, trans_a=False, trans_b=False, out_dtype=F32, bm=1024, bn=1024, bk=1024):
    if trans_a:
        kdim, m = a.shape
    else:
        m, kdim = a.shape
    if trans_b:
        n, k2 = b.shape
    else:
        k2, n = b.shape
    assert kdim == k2, (a.shape, b.shape)
    bm, bn, bk = min(bm, m), min(bn, n), min(bk, kdim)
    assert m % bm == 0 and n % bn == 0 and kdim % bk == 0, (name, m, n, kdim, bm, bn, bk)
    nk = kdim // bk
    dims = (((0 if trans_a else 1,), (1 if trans_b else 0,)), ((), ()))
    has_bias = bias is not None

    def body(*refs):
        if has_bias:
            a_ref, b_ref, bias_ref, o_ref = refs[:4]
            rest = refs[4:]
        else:
            a_ref, b_ref, o_ref = refs[:3]
            bias_ref = None
            rest = refs[3:]
        prod = lax.dot_general(a_ref[...], b_ref[...], dims, preferred_element_type=F32)

        def finish(acc):
            if has_bias:
                acc = acc + bias_ref[...]
            o_ref[...] = acc.astype(out_dtype)

        if nk == 1:
            finish(prod)
        else:
            acc_ref = rest[0]
            k = pl.program_id(2)

            @pl.when(k == 0)
            def _():
                acc_ref[...] = prod

            @pl.when(k > 0)
            def _():
                acc_ref[...] += prod

            @pl.when(k == nk - 1)
            def _():
                finish(acc_ref[...])

    a_spec = pl.BlockSpec((bk, bm), lambda i, j, k: (k, i)) if trans_a else pl.BlockSpec((bm, bk), lambda i, j, k: (i, k))
    b_spec = pl.BlockSpec((bn, bk), lambda i, j, k: (j, k)) if trans_b else pl.BlockSpec((bk, bn), lambda i, j, k: (k, j))
    in_specs = [a_spec, b_spec]
    args = [a, b]
    if has_bias:
        in_specs.append(pl.BlockSpec((1, bn), lambda i, j, k: (0, j)))
        args.append(bias)
    return pl.pallas_call(
        body, name=name,
        out_shape=jax.ShapeDtypeStruct((m, n), out_dtype),
        grid=(m // bm, n // bn, nk),
        in_specs=in_specs,
        out_specs=pl.BlockSpec((bm, bn), lambda i, j, k: (i, j)),
        scratch_shapes=[pltpu.VMEM((bm, bn), F32)] if nk > 1 else [],
        compiler_params=_cparams(("parallel", "parallel", "arbitrary")),
    )(*args)


def _rope_tables(pos_col, freq_row):
    t = pos_col.shape[0]
    tm = min(t, 512)

    def body(pos_ref, f_ref, c_ref, s_ref):
        ang = pos_ref[...].astype(F32) * f_ref[...]
        lane = lax.broadcasted_iota(jnp.int32, ang.shape, 1)
        cos = jnp.cos(ang)
        sin = jnp.sin(ang)
        c_ref[...] = jnp.where(lane < QK_ROPE_DIM, cos, 0.0)
        s_ref[...] = jnp.where(lane < QK_ROPE_DIM // 2, -sin, jnp.where(lane < QK_ROPE_DIM, sin, 0.0))

    return pl.pallas_call(
        body, name="rope_tables",
        out_shape=(jax.ShapeDtypeStruct((t, LANES), F32), jax.ShapeDtypeStruct((t, LANES), F32)),
        grid=(t // tm,),
        in_specs=[pl.BlockSpec((tm, 1), lambda i: (i, 0)), pl.BlockSpec((1, LANES), lambda i: (0, 0))],
        out_specs=(pl.BlockSpec((tm, LANES), lambda i: (i, 0)), pl.BlockSpec((tm, LANES), lambda i: (i, 0))),
        compiler_params=_cparams(("parallel",)),
    )(pos_col, freq_row)


def _conv_mid_fwd(proj, cw, cb, ng, nb):
    t = proj.shape[0]
    e = CONV_WIDTH
    tm = min(t, 128)
    hb = tm // HALO
    nchunk = e // LANES

    def body(val_ref, gg_ref, z_ref, valh_ref, ggh_ref, cw_ref, cb_ref, ng_ref, nb_ref, u1_ref, u2_ref, ext_ref):
        i = pl.program_id(0)
        u0 = val_ref[...] * _sigmoid(gg_ref[...])
        h0 = valh_ref[...] * _sigmoid(ggh_ref[...])
        ext_ref[0:HALO, :] = jnp.where(i > 0, h0, 0.0)
        ext_ref[HALO:, :] = u0

        def chunk(c, carry):
            col = pl.multiple_of(c * LANES, LANES)
            acc = jnp.zeros((tm, LANES), F32)
            for j in range(CONV_KERNEL):
                off = HALO - (CONV_KERNEL - 1) + j
                acc = acc + cw_ref[j:j + 1, pl.ds(col, LANES)] * ext_ref[off:off + tm, pl.ds(col, LANES)]
            u1_ref[:, pl.ds(col, LANES)] = acc + cb_ref[:, pl.ds(col, LANES)]
            return carry

        lax.fori_loop(0, nchunk, chunk, 0)
        u1 = u1_ref[...]
        mu = jnp.mean(u1, axis=-1, keepdims=True)
        xc = u1 - mu
        var = jnp.mean(xc * xc, axis=-1, keepdims=True)
        n = xc * lax.rsqrt(var + LN_EPS) * ng_ref[...] + nb_ref[...]
        z = z_ref[...]
        u2_ref[...] = ((n * _sigmoid(n)) * (z * _sigmoid(z))).astype(BF16)

    row = lambda c: pl.BlockSpec((tm, e), lambda i: (i, c))
    halo = lambda c: pl.BlockSpec((HALO, e), lambda i: (jnp.maximum(i * hb - 1, 0), c))
    par = lambda r: pl.BlockSpec((r, e), lambda i: (0, 0))
    return pl.pallas_call(
        body, name="conv_mid_fwd",
        out_shape=(jax.ShapeDtypeStruct((t, e), F32), jax.ShapeDtypeStruct((t, e), BF16)),
        grid=(t // tm,),
        in_specs=[row(0), row(1), row(2), halo(0), halo(1), par(HALO), par(1), par(1), par(1)],
        out_specs=(pl.BlockSpec((tm, e), lambda i: (i, 0)), pl.BlockSpec((tm, e), lambda i: (i, 0))),
        scratch_shapes=[pltpu.VMEM((tm + HALO, e), F32)],
        compiler_params=_cparams(("parallel",)),
    )(proj, proj, proj, proj, proj, cw, cb, ng, nb)


def _ln_res_fwd(x, y, g, b):
    t, d = x.shape
    tm = min(t, 256)

    def body(x_ref, y_ref, g_ref, b_ref, h_ref, hb_ref, xh_ref, rs_ref):
        r = ALPHA * x_ref[...] + y_ref[...]
        mu = jnp.mean(r, axis=-1, keepdims=True)
        xc = r - mu
        var = jnp.mean(xc * xc, axis=-1, keepdims=True)
        rstd = lax.rsqrt(var + LN_EPS)
        xh = xc * rstd
        h = xh * g_ref[...] + b_ref[...]
        h_ref[...] = h
        hb_ref[...] = h.astype(BF16)
        xh_ref[...] = xh
        rs_ref[...] = rstd

    row = pl.BlockSpec((tm, d), lambda i: (i, 0))
    par = pl.BlockSpec((1, d), lambda i: (0, 0))
    return pl.pallas_call(
        body, name="ln0_fwd",
        out_shape=(jax.ShapeDtypeStruct((t, d), F32), jax.ShapeDtypeStruct((t, d), BF16),
                   jax.ShapeDtypeStruct((t, d), F32), jax.ShapeDtypeStruct((t, 1), F32)),
        grid=(t // tm,),
        in_specs=[row, row, par, par],
        out_specs=(row, row, row, pl.BlockSpec((tm, 1), lambda i: (i, 0))),
        compiler_params=_cparams(("parallel",)),
    )(x, y, g, b)


def _kv_mid_fwd(ckv, g, rc, rs):
    t = ckv.shape[0]
    tm = min(t, 512)
    r = KV_LORA_RANK

    def body(ckv_ref, g_ref, c_ref, s_ref, ckn_ref, kr_ref):
        c = ckv_ref[:, 0:r]
        ms = jnp.mean(c * c, axis=-1, keepdims=True)
        ckn_ref[...] = (c * lax.rsqrt(ms + RMS_EPS) * g_ref[...]).astype(BF16)
        tr = ckv_ref[:, r:r + LANES]
        kr_ref[...] = (tr * c_ref[...] + _swap_rope_halves(tr) * s_ref[...]).astype(BF16)

    return pl.pallas_call(
        body, name="kv_mid_fwd",
        out_shape=(jax.ShapeDtypeStruct((t, r), BF16), jax.ShapeDtypeStruct((t, LANES), BF16)),
        grid=(t // tm,),
        in_specs=[pl.BlockSpec((tm, r + LANES), lambda i: (i, 0)), pl.BlockSpec((1, r), lambda i: (0, 0)),
                  pl.BlockSpec((tm, LANES), lambda i: (i, 0)), pl.BlockSpec((tm, LANES), lambda i: (i, 0))],
        out_specs=(pl.BlockSpec((tm, r), lambda i: (i, 0)), pl.BlockSpec((tm, LANES), lambda i: (i, 0))),
        compiler_params=_cparams(("parallel",)),
    )(ckv, g, rc, rs)


def _q_norm_fwd(projb, g):
    t = projb.shape[0]
    tm = min(t, 512)
    r = Q_LORA_RANK

    def body(c_ref, g_ref, o_ref):
        c = c_ref[...]
        ms = jnp.mean(c * c, axis=-1, keepdims=True)
        o_ref[...] = (c * lax.rsqrt(ms + RMS_EPS) * g_ref[...]).astype(BF16)

    return pl.pallas_call(
        body, name="q_norm_fwd",
        out_shape=jax.ShapeDtypeStruct((t, r), BF16),
        grid=(t // tm,),
        in_specs=[pl.BlockSpec((tm, r), lambda i: (i, 0)), pl.BlockSpec((1, r), lambda i: (0, 0))],
        out_specs=pl.BlockSpec((tm, r), lambda i: (i, 0)),
        compiler_params=_cparams(("parallel",)),
    )(projb, g)


def _q_rope_fwd(q2, rc, rs):
    t = q2.shape[0]
    tm = min(t, 256)
    w = N_HEADS * HEAD_PAD

    def body(q_ref, c_ref, s_ref, o_ref):
        c = c_ref[...]
        s = s_ref[...]
        for h in range(N_HEADS):
            a = h * HEAD_PAD
            o_ref[:, a:a + LANES] = q_ref[:, a:a + LANES].astype(BF16)
            tr = q_ref[:, a + LANES:a + 2 * LANES]
            o_ref[:, a + LANES:a + 2 * LANES] = (tr * c + _swap_rope_halves(tr) * s).astype(BF16)

    return pl.pallas_call(
        body, name="q_rope_fwd",
        out_shape=jax.ShapeDtypeStruct((t, w), BF16),
        grid=(t // tm,),
        in_specs=[pl.BlockSpec((tm, w), lambda i: (i, 0)), pl.BlockSpec((tm, LANES), lambda i: (i, 0)),
                  pl.BlockSpec((tm, LANES), lambda i: (i, 0))],
        out_specs=pl.BlockSpec((tm, w), lambda i: (i, 0)),
        compiler_params=_cparams(("parallel",)),
    )(q2, rc, rs)


def _flash_fwd(qcat, kv, kr):
    t = qcat.shape[0]
    tq = min(t, 512)
    nq = t // tq

    def body(q_ref, kv_ref, kr_ref, o_ref, lse_ref, kcat_ref, m_ref, l_ref, acc_ref):
        qi = pl.program_id(1)

        @pl.when(qi == 0)
        def _():
            kcat_ref[:, 0:LANES] = kv_ref[:, 0:LANES]
            kcat_ref[:, LANES:] = kr_ref[...]

        m_ref[...] = jnp.full((tq, 1), MASK_VALUE, F32)
        l_ref[...] = jnp.zeros((tq, 1), F32)
        acc_ref[...] = jnp.zeros((tq, LANES), F32)
        q = q_ref[...]

        def step(kb, masked):
            ks = _aligned(kb * tq, tq)
            k = kcat_ref[pl.ds(ks, tq), :]
            s = lax.dot_general(q, k, NT, preferred_element_type=F32) * ATTN_SCALE
            if masked:
                rows = lax.broadcasted_iota(jnp.int32, (tq, tq), 0)
                cols = lax.broadcasted_iota(jnp.int32, (tq, tq), 1)
                s = jnp.where(cols <= rows, s, MASK_VALUE)
            m_old = m_ref[...]
            m_new = jnp.maximum(m_old, jnp.max(s, axis=1, keepdims=True))
            a = jnp.exp(m_old - m_new)
            p = jnp.exp(s - m_new)
            l_ref[...] = a * l_ref[...] + jnp.sum(p, axis=1, keepdims=True)
            v = kv_ref[pl.ds(ks, tq), LANES:]
            acc_ref[...] = a * acc_ref[...] + lax.dot_general(p.astype(BF16), v, NN, preferred_element_type=F32)
            m_ref[...] = m_new

        def loop_body(kb, carry):
            step(kb, False)
            return carry

        lax.fori_loop(0, qi, loop_body, 0)
        step(qi, True)
        l = l_ref[...]
        o_ref[...] = acc_ref[...] / l
        lse_ref[...] = jnp.broadcast_to(m_ref[...] + jnp.log(l), (tq, LANES))

    hw = N_HEADS * LANES
    return pl.pallas_call(
        body, name="flash_fwd",
        out_shape=(jax.ShapeDtypeStruct((t, hw), F32), jax.ShapeDtypeStruct((t, hw), F32)),
        grid=(N_HEADS, nq),
        in_specs=[pl.BlockSpec((tq, HEAD_PAD), lambda h, i: (i, h)),
                  pl.BlockSpec((t, HEAD_PAD), lambda h, i: (0, h)),
                  pl.BlockSpec((t, LANES), lambda h, i: (0, 0))],
        out_specs=(pl.BlockSpec((tq, LANES), lambda h, i: (i, h)), pl.BlockSpec((tq, LANES), lambda h, i: (i, h))),
        scratch_shapes=[pltpu.VMEM((t, HEAD_PAD), BF16), pltpu.VMEM((tq, 1), F32), pltpu.VMEM((tq, 1), F32),
                        pltpu.VMEM((tq, LANES), F32)],
        compiler_params=_cparams(("arbitrary", "arbitrary")),
    )(qcat, kv, kr)


def _gate_fwd(o, projb):
    t, hw = o.shape
    tm = min(t, 512)
    bw = Q_LORA_RANK
    nb = hw // bw

    def body(o_ref, z_ref, o2_ref):
        z = z_ref[...]
        o2_ref[...] = (o_ref[...] * (z * _sigmoid(z))).astype(BF16)

    return pl.pallas_call(
        body, name="gate_fwd",
        out_shape=jax.ShapeDtypeStruct((t, hw), BF16),
        grid=(t // tm, nb),
        in_specs=[pl.BlockSpec((tm, bw), lambda i, j: (i, j)), pl.BlockSpec((tm, bw), lambda i, j: (i, j + 1))],
        out_specs=pl.BlockSpec((tm, bw), lambda i, j: (i, j)),
        compiler_params=_cparams(("parallel", "parallel")),
    )(o, projb)


def _final_ln_loss(h1, yb, g, b, target):
    t, d = h1.shape
    tm = min(t, 256)

    def body(h_ref, y_ref, g_ref, b_ref, t_ref, loss_ref, dr_ref, drb_ref, dg_ref, db_ref):
        i = pl.program_id(0)
        r = ALPHA * h_ref[...] + y_ref[...]
        mu = jnp.mean(r, axis=-1, keepdims=True)
        xc = r - mu
        var = jnp.mean(xc * xc, axis=-1, keepdims=True)
        rstd = lax.rsqrt(var + LN_EPS)
        xh = xc * rstd
        g = g_ref[...]
        diff = xh * g + b_ref[...] - t_ref[...]
        part = 0.5 * jnp.sum(jnp.mean(diff * diff, axis=-1, keepdims=True))
        dh = diff * (1.0 / d)
        dgp = jnp.sum(dh * xh, axis=0, keepdims=True)
        dbp = jnp.sum(dh, axis=0, keepdims=True)

        @pl.when(i == 0)
        def _():
            loss_ref[...] = jnp.zeros_like(loss_ref)
            dg_ref[...] = jnp.zeros_like(dg_ref)
            db_ref[...] = jnp.zeros_like(db_ref)

        loss_ref[...] += jnp.full(loss_ref.shape, part, F32)
        dg_ref[...] += dgp
        db_ref[...] += dbp
        dxh = dh * g
        dr = rstd * (dxh - jnp.mean(dxh, axis=-1, keepdims=True) - xh * jnp.mean(dxh * xh, axis=-1, keepdims=True))
        dr_ref[...] = dr
        drb_ref[...] = dr.astype(BF16)

    row = pl.BlockSpec((tm, d), lambda i: (i, 0))
    par = pl.BlockSpec((1, d), lambda i: (0, 0))
    return pl.pallas_call(
        body, name="final_ln_loss",
        out_shape=(jax.ShapeDtypeStruct((1, LANES), F32), jax.ShapeDtypeStruct((t, d), F32),
                   jax.ShapeDtypeStruct((t, d), BF16), jax.ShapeDtypeStruct((1, d), F32),
                   jax.ShapeDtypeStruct((1, d), F32)),
        grid=(t // tm,),
        in_specs=[row, row, par, par, row],
        out_specs=(pl.BlockSpec((1, LANES), lambda i: (0, 0)), row, row, par, par),
        compiler_params=_cparams(("arbitrary",)),
    )(h1, yb, g, b, target)


def _gate_bwd(do2, o, projb):
    t, hw = o.shape
    tm = min(t, 512)
    bw = Q_LORA_RANK
    nb = hw // bw
    wb = projb.shape[1]

    def body(d_ref, o_ref, z_ref, dob_ref, dz_ref, dl_ref):
        z = z_ref[...]
        sg = _sigmoid(z)
        d2 = d_ref[...]
        o = o_ref[...]
        do = d2 * (z * sg)
        dob_ref[...] = do.astype(BF16)
        dz_ref[...] = (d2 * o * (sg * (1.0 + z * (1.0 - sg)))).astype(BF16)
        prod = do * o
        for hh in range(bw // LANES):
            dsum = jnp.sum(prod[:, hh * LANES:(hh + 1) * LANES], axis=1, keepdims=True)
            dl_ref[:, hh * LANES:(hh + 1) * LANES] = jnp.broadcast_to(dsum, (tm, LANES))

    blk = pl.BlockSpec((tm, bw), lambda i, j: (i, j))
    blk1 = pl.BlockSpec((tm, bw), lambda i, j: (i, j + 1))
    return pl.pallas_call(
        body, name="gate_bwd",
        out_shape=(jax.ShapeDtypeStruct((t, hw), BF16), jax.ShapeDtypeStruct((t, wb), BF16),
                   jax.ShapeDtypeStruct((t, hw), F32)),
        grid=(t // tm, nb),
        in_specs=[blk, blk, blk1],
        out_specs=(blk, blk1, blk),
        compiler_params=_cparams(("parallel", "parallel")),
    )(do2, o, projb)


def _flash_bwd(qcat, kv, kr, dob, lse, dl, rc, rs):
    t = qcat.shape[0]
    tq = min(t, 512)
    nq = t // tq
    once = pl.Buffered(1)

    def body(q_ref, kv_ref, kr_ref, do_ref, lse_ref, dl_ref, c_ref, s_ref,
             dq_ref, dkv_ref, dkr_ref, kcat_ref, dqa_ref, dka_ref, dva_ref):
        h = pl.program_id(0)
        kcat_ref[:, 0:LANES] = kv_ref[:, 0:LANES]
        kcat_ref[:, LANES:] = kr_ref[...]
        dqa_ref[...] = jnp.zeros_like(dqa_ref)

        @pl.when(h == 0)
        def _():
            dkr_ref[...] = jnp.zeros_like(dkr_ref)

        def block(qb, kb, masked):
            qs = _aligned(qb * tq, tq)
            ks = _aligned(kb * tq, tq)
            q = q_ref[pl.ds(qs, tq), :]
            kc = kcat_ref[pl.ds(ks, tq), :]
            s = lax.dot_general(q, kc, NT, preferred_element_type=F32) * ATTN_SCALE
            if masked:
                rows = lax.broadcasted_iota(jnp.int32, (tq, tq), 0)
                cols = lax.broadcasted_iota(jnp.int32, (tq, tq), 1)
                s = jnp.where(cols <= rows, s, MASK_VALUE)
            lse_q = lse_ref[pl.ds(qs, tq), :][:, 0:1]
            d_q = dl_ref[pl.ds(qs, tq), :][:, 0:1]
            p = jnp.exp(s - lse_q)
            do = do_ref[pl.ds(qs, tq), :]
            v = kv_ref[pl.ds(ks, tq), LANES:]
            dp = lax.dot_general(do, v, NT, preferred_element_type=F32)
            ds = (p * (dp - d_q) * ATTN_SCALE).astype(BF16)
            dva_ref[...] += lax.dot_general(p.astype(BF16), do, TN, preferred_element_type=F32)
            dka_ref[...] += lax.dot_general(ds, q, TN, preferred_element_type=F32)
            dqa_ref[pl.ds(qs, tq), :] += lax.dot_general(ds, kc, NN, preferred_element_type=F32)

        for kb in range(nq):
            dka_ref[...] = jnp.zeros_like(dka_ref)
            dva_ref[...] = jnp.zeros_like(dva_ref)
            block(kb, kb, True)
            if kb + 1 < nq:
                def loop_body(qb, carry, kb=kb):
                    block(qb, kb, False)
                    return carry
                lax.fori_loop(kb + 1, nq, loop_body, 0)
            a = kb * tq
            dk = dka_ref[...]
            dkv_ref[a:a + tq, 0:LANES] = dk[:, 0:LANES].astype(BF16)
            dkv_ref[a:a + tq, LANES:] = dva_ref[...].astype(BF16)
            dkr_ref[a:a + tq, :] += dk[:, LANES:]

        dq = dqa_ref[...]
        dq_ref[:, 0:LANES] = dq[:, 0:LANES].astype(BF16)
        d2 = dq[:, LANES:]
        dq_ref[:, LANES:] = (d2 * c_ref[...] - _swap_rope_halves(d2) * s_ref[...]).astype(BF16)

    hp = N_HEADS * HEAD_PAD
    head256 = pl.BlockSpec((t, HEAD_PAD), lambda h: (0, h))
    head128 = pl.BlockSpec((t, LANES), lambda h: (0, h))
    const128 = pl.BlockSpec((t, LANES), lambda h: (0, 0), pipeline_mode=once)
    return pl.pallas_call(
        body, name="flash_bwd",
        out_shape=(jax.ShapeDtypeStruct((t, hp), BF16), jax.ShapeDtypeStruct((t, hp), BF16),
                   jax.ShapeDtypeStruct((t, LANES), F32)),
        grid=(N_HEADS,),
        in_specs=[head256, head256, const128, head128, head128, head128, const128, const128],
        out_specs=(head256, head256, pl.BlockSpec((t, LANES), lambda h: (0, 0))),
        scratch_shapes=[pltpu.VMEM((t, HEAD_PAD), BF16), pltpu.VMEM((t, HEAD_PAD), F32),
                        pltpu.VMEM((tq, HEAD_PAD), F32), pltpu.VMEM((tq, LANES), F32)],
        compiler_params=_cparams(("arbitrary",)),
    )(qcat, kv, kr, dob, lse, dl, rc, rs)


def _q_norm_bwd(dcqn, projb, g, dprojb):
    t = projb.shape[0]
    tm = min(t, 512)
    r = Q_LORA_RANK

    def body(d_ref, c_ref, g_ref, alias_ref, o_ref, dg_ref):
        i = pl.program_id(0)
        c = c_ref[...]
        d = d_ref[...]
        rr = lax.rsqrt(jnp.mean(c * c, axis=-1, keepdims=True) + RMS_EPS)
        xh = c * rr

        @pl.when(i == 0)
        def _():
            dg_ref[...] = jnp.zeros_like(dg_ref)

        dg_ref[...] += jnp.sum(d * xh, axis=0, keepdims=True)
        dxh = d * g_ref[...]
        o_ref[...] = (rr * (dxh - xh * jnp.mean(dxh * xh, axis=-1, keepdims=True))).astype(BF16)

    blk = pl.BlockSpec((tm, r), lambda i: (i, 0))
    par = pl.BlockSpec((1, r), lambda i: (0, 0))
    return pl.pallas_call(
        body, name="q_norm_bwd",
        out_shape=(jax.ShapeDtypeStruct(dprojb.shape, BF16), jax.ShapeDtypeStruct((1, r), F32)),
        grid=(t // tm,),
        in_specs=[blk, blk, par, pl.BlockSpec(memory_space=pl.ANY)],
        out_specs=(blk, par),
        input_output_aliases={3: 0},
        compiler_params=_cparams(("arbitrary",)),
    )(dcqn, projb, g, dprojb)


def _kv_mid_bwd(dckn, ckv, dkr, g, rc, rs):
    t = ckv.shape[0]
    tm = min(t, 512)
    r = KV_LORA_RANK

    def body(d_ref, ckv_ref, dkr_ref, g_ref, c_ref, s_ref, o_ref, dg_ref):
        i = pl.program_id(0)
        c = ckv_ref[:, 0:r]
        d = d_ref[...]
        rr = lax.rsqrt(jnp.mean(c * c, axis=-1, keepdims=True) + RMS_EPS)
        xh = c * rr

        @pl.when(i == 0)
        def _():
            dg_ref[...] = jnp.zeros_like(dg_ref)

        dg_ref[...] += jnp.sum(d * xh, axis=0, keepdims=True)
        dxh = d * g_ref[...]
        o_ref[:, 0:r] = (rr * (dxh - xh * jnp.mean(dxh * xh, axis=-1, keepdims=True))).astype(BF16)
        dk = dkr_ref[...]
        o_ref[:, r:] = (dk * c_ref[...] - _swap_rope_halves(dk) * s_ref[...]).astype(BF16)

    return pl.pallas_call(
        body, name="kv_mid_bwd",
        out_shape=(jax.ShapeDtypeStruct((t, r + LANES), BF16), jax.ShapeDtypeStruct((1, r), F32)),
        grid=(t // tm,),
        in_specs=[pl.BlockSpec((tm, r), lambda i: (i, 0)), pl.BlockSpec((tm, r + LANES), lambda i: (i, 0)),
                  pl.BlockSpec((tm, LANES), lambda i: (i, 0)), pl.BlockSpec((1, r), lambda i: (0, 0)),
                  pl.BlockSpec((tm, LANES), lambda i: (i, 0)), pl.BlockSpec((tm, LANES), lambda i: (i, 0))],
        out_specs=(pl.BlockSpec((tm, r + LANES), lambda i: (i, 0)), pl.BlockSpec((1, r), lambda i: (0, 0))),
        compiler_params=_cparams(("arbitrary",)),
    )(dckn, ckv, dkr, g, rc, rs)


def _ln0_bwd(dr2, t1, t2, xh, rstd, g):
    t, d = dr2.shape
    tm = min(t, 256)

    def body(a_ref, b_ref, c_ref, xh_ref, rs_ref, g_ref, dr_ref, drb_ref, dg_ref, db_ref):
        i = pl.program_id(0)
        dh = ALPHA * a_ref[...] + b_ref[...] + c_ref[...]
        xh = xh_ref[...]

        @pl.when(i == 0)
        def _():
            dg_ref[...] = jnp.zeros_like(dg_ref)
            db_ref[...] = jnp.zeros_like(db_ref)

        dg_ref[...] += jnp.sum(dh * xh, axis=0, keepdims=True)
        db_ref[...] += jnp.sum(dh, axis=0, keepdims=True)
        dxh = dh * g_ref[...]
        dr = rs_ref[...] * (dxh - jnp.mean(dxh, axis=-1, keepdims=True) - xh * jnp.mean(dxh * xh, axis=-1, keepdims=True))
        dr_ref[...] = dr
        drb_ref[...] = dr.astype(BF16)

    row = pl.BlockSpec((tm, d), lambda i: (i, 0))
    par = pl.BlockSpec((1, d), lambda i: (0, 0))
    return pl.pallas_call(
        body, name="ln0_bwd",
        out_shape=(jax.ShapeDtypeStruct((t, d), F32), jax.ShapeDtypeStruct((t, d), BF16),
                   jax.ShapeDtypeStruct((1, d), F32), jax.ShapeDtypeStruct((1, d), F32)),
        grid=(t // tm,),
        in_specs=[row, row, row, row, pl.BlockSpec((tm, 1), lambda i: (i, 0)), par],
        out_specs=(row, row, par, par),
        compiler_params=_cparams(("arbitrary",)),
    )(dr2, t1, t2, xh, rstd, g)


def _conv_mid_bwd_rows(u1, proj, du2, ng, nb):
    t = u1.shape[0]
    e = CONV_WIDTH
    tm = min(t, 256)

    def body(u1_ref, z_ref, d_ref, ng_ref, nb_ref, du1_ref, dz_ref, dng_ref, dnb_ref, dbz_ref):
        i = pl.program_id(0)
        u1 = u1_ref[...]
        mu = jnp.mean(u1, axis=-1, keepdims=True)
        xc = u1 - mu
        rstd = lax.rsqrt(jnp.mean(xc * xc, axis=-1, keepdims=True) + LN_EPS)
        xh = xc * rstd
        g = ng_ref[...]
        n = xh * g + nb_ref[...]
        sn = _sigmoid(n)
        z = z_ref[...]
        sz = _sigmoid(z)
        du2 = d_ref[...]
        dz = du2 * (n * sn) * (sz * (1.0 + z * (1.0 - sz)))
        dn = du2 * (z * sz) * (sn * (1.0 + n * (1.0 - sn)))

        @pl.when(i == 0)
        def _():
            dng_ref[...] = jnp.zeros_like(dng_ref)
            dnb_ref[...] = jnp.zeros_like(dnb_ref)
            dbz_ref[...] = jnp.zeros_like(dbz_ref)

        dng_ref[...] += jnp.sum(dn * xh, axis=0, keepdims=True)
        dnb_ref[...] += jnp.sum(dn, axis=0, keepdims=True)
        dbz_ref[...] += jnp.sum(dz, axis=0, keepdims=True)
        dz_ref[...] = dz.astype(BF16)
        dxh = dn * g
        du1_ref[...] = rstd * (dxh - jnp.mean(dxh, axis=-1, keepdims=True) - xh * jnp.mean(dxh * xh, axis=-1, keepdims=True))

    row = pl.BlockSpec((tm, e), lambda i: (i, 0))
    par = pl.BlockSpec((1, e), lambda i: (0, 0))
    return pl.pallas_call(
        body, name="conv_mid_bwd_rows",
        out_shape=(jax.ShapeDtypeStruct((t, e), F32), jax.ShapeDtypeStruct((t, 3 * e), BF16),
                   jax.ShapeDtypeStruct((1, e), F32), jax.ShapeDtypeStruct((1, e), F32),
                   jax.ShapeDtypeStruct((1, e), F32)),
        grid=(t // tm,),
        in_specs=[row, pl.BlockSpec((tm, e), lambda i: (i, 2)), row, par, par],
        out_specs=(row, pl.BlockSpec((tm, e), lambda i: (i, 2)), par, par, par),
        compiler_params=_cparams(("arbitrary",)),
    )(u1, proj, du2, ng, nb)


def _conv_bwd(du1, proj, cw, dproj):
    t = du1.shape[0]
    e = CONV_WIDTH
    tm = min(t, 128)
    hb = tm // HALO
    nt = t // tm
    nchunk = e // LANES
    last_halo = t // HALO - 1

    def body(d_ref, dn_ref, val_ref, gg_ref, valh_ref, ggh_ref, cw_ref, alias_ref,
             dp_ref, dcw_ref, dcb_ref, dbv_ref, extu_ref, extd_ref, du0_ref, acc_ref):
        i = pl.program_id(0)
        val = val_ref[...]
        sg = _sigmoid(gg_ref[...])
        h0 = valh_ref[...] * _sigmoid(ggh_ref[...])
        extu_ref[0:HALO, :] = jnp.where(i > 0, h0, 0.0)
        extu_ref[HALO:, :] = val * sg
        extd_ref[0:tm, :] = d_ref[...]
        extd_ref[tm:, :] = jnp.where(i < nt - 1, dn_ref[...], 0.0)

        @pl.when(i == 0)
        def _():
            acc_ref[...] = jnp.zeros_like(acc_ref)
            dbv_ref[...] = jnp.zeros_like(dbv_ref)

        def chunk(c, carry):
            col = pl.multiple_of(c * LANES, LANES)
            d = extd_ref[0:tm, pl.ds(col, LANES)]
            du0 = jnp.zeros((tm, LANES), F32)
            for j in range(CONV_KERNEL):
                offd = CONV_KERNEL - 1 - j
                du0 = du0 + cw_ref[j:j + 1, pl.ds(col, LANES)] * extd_ref[offd:offd + tm, pl.ds(col, LANES)]
                offu = HALO - (CONV_KERNEL - 1) + j
                prod = d * extu_ref[offu:offu + tm, pl.ds(col, LANES)]
                acc_ref[j * 8:(j + 1) * 8, pl.ds(col, LANES)] += jnp.sum(prod.reshape(tm // 8, 8, LANES), axis=0)
            acc_ref[CONV_KERNEL * 8:(CONV_KERNEL + 1) * 8, pl.ds(col, LANES)] += jnp.sum(d.reshape(tm // 8, 8, LANES), axis=0)
            du0_ref[:, pl.ds(col, LANES)] = du0
            return carry

        lax.fori_loop(0, nchunk, chunk, 0)
        du0 = du0_ref[...]
        dval = du0 * sg
        dgg = du0 * val * sg * (1.0 - sg)
        dp_ref[:, 0:e] = dval.astype(BF16)
        dp_ref[:, e:] = dgg.astype(BF16)
        dbv_ref[:, 0:e] += jnp.sum(dval, axis=0, keepdims=True)
        dbv_ref[:, e:] += jnp.sum(dgg, axis=0, keepdims=True)

        @pl.when(i == nt - 1)
        def _():
            for j in range(CONV_KERNEL):
                dcw_ref[j:j + 1, :] = jnp.sum(acc_ref[j * 8:(j + 1) * 8, :], axis=0, keepdims=True)
            dcw_ref[CONV_KERNEL:, :] = jnp.zeros((HALO - CONV_KERNEL, e), F32)
            dcb_ref[...] = jnp.sum(acc_ref[CONV_KERNEL * 8:(CONV_KERNEL + 1) * 8, :], axis=0, keepdims=True)

    row = lambda c: pl.BlockSpec((tm, e), lambda i: (i, c))
    halo_prev = lambda c: pl.BlockSpec((HALO, e), lambda i: (jnp.maximum(i * hb - 1, 0), c))
    halo_next = pl.BlockSpec((HALO, e), lambda i: (jnp.minimum((i + 1) * hb, last_halo), 0))
    return pl.pallas_call(
        body, name="conv_bwd",
        out_shape=(jax.ShapeDtypeStruct(dproj.shape, BF16), jax.ShapeDtypeStruct((HALO, e), F32),
                   jax.ShapeDtypeStruct((1, e), F32), jax.ShapeDtypeStruct((1, 2 * e), F32)),
        grid=(nt,),
        in_specs=[row(0), halo_next, row(0), row(1), halo_prev(0), halo_prev(1),
                  pl.BlockSpec((HALO, e), lambda i: (0, 0)), pl.BlockSpec(memory_space=pl.ANY)],
        out_specs=(pl.BlockSpec((tm, 2 * e), lambda i: (i, 0)), pl.BlockSpec((HALO, e), lambda i: (0, 0)),
                   pl.BlockSpec((1, e), lambda i: (0, 0)), pl.BlockSpec((1, 2 * e), lambda i: (0, 0))),
        scratch_shapes=[pltpu.VMEM((tm + HALO, e), F32), pltpu.VMEM((tm + HALO, e), F32),
                        pltpu.VMEM((tm, e), F32), pltpu.VMEM(((CONV_KERNEL + 1) * 8, e), F32)],
        input_output_aliases={7: 0},
        compiler_params=_cparams(("arbitrary",)),
    )(du1, du1, proj, proj, proj, proj, cw, dproj)


def _adamw(gbuf, w, m, v):
    r = w.shape[0]
    tr = min(r, 512)
    assert r % tr == 0
    c1 = 1.0 - ADAM_B1 ** ADAM_STEP
    c2 = 1.0 - ADAM_B2 ** ADAM_STEP

    def body(g_ref, w_ref, m_ref, v_ref, go_ref, d_ref, mo_ref, vo_ref):
        g = g_ref[0]
        for k in range(1, N_DEV):
            g = g + g_ref[k]
        mn = ADAM_B1 * m_ref[...] + (1.0 - ADAM_B1) * g
        vn = ADAM_B2 * v_ref[...] + (1.0 - ADAM_B2) * (g * g)
        m_hat = mn / c1
        v_hat = vn / c2
        go_ref[...] = g
        d_ref[...] = -ADAM_LR * (m_hat / (jnp.sqrt(v_hat) + ADAM_EPS) + ADAM_WD * w_ref[...])
        mo_ref[...] = mn
        vo_ref[...] = vn

    blk = pl.BlockSpec((tr, LANES), lambda i: (i, 0))
    shp = jax.ShapeDtypeStruct((r, LANES), F32)
    return pl.pallas_call(
        body, name="adamw",
        out_shape=(shp, shp, shp, shp),
        grid=(r // tr,),
        in_specs=[pl.BlockSpec((N_DEV, tr, LANES), lambda i: (0, i, 0)), blk, blk, blk],
        out_specs=(blk, blk, blk, blk),
        compiler_params=_cparams(("parallel",)),
    )(gbuf, w, m, v)


def _mesh_peers():
    x, y, c = lax.axis_index("x"), lax.axis_index("y"), lax.axis_index("c")
    me = 4 * x + 2 * y + c
    peers = []
    for k in range(1, N_DEV):
        px = 1 - x if (k >> 2) & 1 else x
        py = 1 - y if (k >> 1) & 1 else y
        pc = 1 - c if k & 1 else c
        peers.append(((px, py, pc), 4 * px + 2 * py + pc))
    return me, peers


def _all_gather_weights(wb, ws):
    def body(wb_ref, ws_ref, ob_ref, os_ref, send_sems, recv_sems, local_sems):
        me, peers = _mesh_peers()
        pairs = ((wb_ref, ob_ref), (ws_ref, os_ref))
        local = [pltpu.make_async_copy(src, dst.at[me], local_sems.at[a]) for a, (src, dst) in enumerate(pairs)]
        for cp in local:
            cp.start()
        sends = []
        for k, (dev, _) in enumerate(peers):
            for a, (src, dst) in enumerate(pairs):
                cp = pltpu.make_async_remote_copy(src_ref=src, dst_ref=dst.at[me], send_sem=send_sems.at[a, k],
                                                  recv_sem=recv_sems.at[a, k], device_id=dev,
                                                  device_id_type=pl.DeviceIdType.MESH)
                cp.start()
                sends.append(cp)
        for k, (dev, pid) in enumerate(peers):
            for a, (src, dst) in enumerate(pairs):
                pltpu.make_async_remote_copy(src_ref=src, dst_ref=dst.at[pid], send_sem=send_sems.at[a, k],
                                             recv_sem=recv_sems.at[a, k], device_id=dev,
                                             device_id_type=pl.DeviceIdType.MESH).wait_recv()
        for cp in sends:
            cp.wait_send()
        for cp in local:
            cp.wait()

    hbm = pl.BlockSpec(memory_space=pl.ANY)
    return pl.pallas_call(
        body, name="all_gather_weights",
        out_shape=(jax.ShapeDtypeStruct((N_DEV,) + wb.shape, wb.dtype), jax.ShapeDtypeStruct((N_DEV,) + ws.shape, ws.dtype)),
        in_specs=[hbm, hbm], out_specs=(hbm, hbm),
        scratch_shapes=[pltpu.SemaphoreType.DMA((2, N_DEV - 1)), pltpu.SemaphoreType.DMA((2, N_DEV - 1)),
                        pltpu.SemaphoreType.DMA((2,))],
    )(wb, ws)


def _exchange_grads(g):
    def body(g_ref, o_ref, send_sems, recv_sems, local_sem):
        me, peers = _mesh_peers()
        local = pltpu.make_async_copy(g_ref.at[me], o_ref.at[me], local_sem)
        local.start()
        sends = []
        for k, (dev, pid) in enumerate(peers):
            cp = pltpu.make_async_remote_copy(src_ref=g_ref.at[pid], dst_ref=o_ref.at[me], send_sem=send_sems.at[k],
                                              recv_sem=recv_sems.at[k], device_id=dev,
                                              device_id_type=pl.DeviceIdType.MESH)
            cp.start()
            sends.append(cp)
        for k, (dev, pid) in enumerate(peers):
            pltpu.make_async_remote_copy(src_ref=g_ref.at[pid], dst_ref=o_ref.at[pid], send_sem=send_sems.at[k],
                                         recv_sem=recv_sems.at[k], device_id=dev,
                                         device_id_type=pl.DeviceIdType.MESH).wait_recv()
        for cp in sends:
            cp.wait_send()
        local.wait()

    hbm = pl.BlockSpec(memory_space=pl.ANY)
    return pl.pallas_call(
        body, name="exchange_grads",
        out_shape=jax.ShapeDtypeStruct(g.shape, g.dtype),
        in_specs=[hbm], out_specs=hbm,
        scratch_shapes=[pltpu.SemaphoreType.DMA((N_DEV - 1,)), pltpu.SemaphoreType.DMA((N_DEV - 1,)),
                        pltpu.SemaphoreType.DMA(())],
    )(g)


def _prep_weights(w):
    e = CONV_WIDTH
    p = {}
    p['a_w_in'] = w['a_w_in'].reshape(D_MODEL, 3 * e)
    p['a_b_in'] = w['a_b_in'].reshape(1, 3 * e)
    p['a_conv_w'] = jnp.pad(w['a_conv_w'].reshape(CONV_KERNEL, e), ((0, HALO - CONV_KERNEL), (0, 0)))
    p['a_conv_b'] = w['a_conv_b'].reshape(1, e)
    p['a_norm_g'] = w['a_norm_g'].reshape(1, e)
    p['a_norm_b'] = w['a_norm_b'].reshape(1, e)
    p['a_w_out'] = w['a_w_out'].reshape(e, D_MODEL)
    p['a_b_out'] = w['a_b_out'].reshape(1, D_MODEL)
    p['kv_w_down'] = jnp.pad(w['kv_w_down'], ((0, 0), (0, KV_LORA_RANK + LANES - w['kv_w_down'].shape[1])))
    p['kv_norm_g'] = w['kv_norm_g'].reshape(1, KV_LORA_RANK)
    p['kv_w_ukv'] = jnp.concatenate([w['kv_w_uk'], w['kv_w_uv']], axis=2).reshape(KV_LORA_RANK, N_HEADS * HEAD_PAD)
    p['b_w_in'] = w['b_w_in'].reshape(D_MODEL, -1)
    p['b_q_norm_g'] = w['b_q_norm_g'].reshape(1, Q_LORA_RANK)
    uq = w['b_w_uq'].reshape(Q_LORA_RANK, N_HEADS, QK_NOPE_DIM + QK_ROPE_DIM)
    p['b_w_uq'] = jnp.pad(uq, ((0, 0), (0, 0), (0, HEAD_PAD - uq.shape[2]))).reshape(Q_LORA_RANK, N_HEADS * HEAD_PAD)
    p['b_w_out'] = w['b_w_out'].reshape(N_HEADS * V_HEAD_DIM, D_MODEL)
    p['ln_g'] = w['ln_g']
    p['ln_b'] = w['ln_b']
    return p


def _local_step(x, positions, target, p):
    t = x.shape[0]
    e = CONV_WIDTH
    freqs = ROPE_THETA ** (-jnp.arange(0, QK_ROPE_DIM, 2, dtype=F32) / QK_ROPE_DIM)
    freq_row = jnp.concatenate([freqs, freqs, jnp.zeros((LANES - QK_ROPE_DIM,), F32)]).reshape(1, LANES)
    rc, rs = _rope_tables(positions.reshape(t, 1), freq_row)
    xb = x.astype(BF16)
    ln_g0, ln_b0 = p['ln_g'][0:1], p['ln_b'][0:1]
    ln_g1, ln_b1 = p['ln_g'][1:2], p['ln_b'][1:2]

    proj = _matmul(xb, p['a_w_in'], bias=p['a_b_in'], name="a_in")
    u1, u2 = _conv_mid_fwd(proj, p['a_conv_w'], p['a_conv_b'], p['a_norm_g'], p['a_norm_b'])
    y = _matmul(u2, p['a_w_out'], bias=p['a_b_out'], name="a_out", bk=2048)
    h1, h1b, xh1, rstd1 = _ln_res_fwd(x, y, ln_g0, ln_b0)
    ckv = _matmul(h1b, p['kv_w_down'], name="kv_down")
    ckn, kr = _kv_mid_fwd(ckv, p['kv_norm_g'], rc, rs)
    kv = _matmul(ckn, p['kv_w_ukv'], name="kv_up", out_dtype=BF16)
    projb = _matmul(h1b, p['b_w_in'], name="b_in", bn=1280)
    cqn = _q_norm_fwd(projb, p['b_q_norm_g'])
    q2 = _matmul(cqn, p['b_w_uq'], name="q_up")
    qcat = _q_rope_fwd(q2, rc, rs)
    o, lse = _flash_fwd(qcat, kv, kr)
    o2 = _gate_fwd(o, projb)
    yb = _matmul(o2, p['b_w_out'], name="b_out", bk=2048)
    loss, dr2, dr2b, dg1, db1 = _final_ln_loss(h1, yb, ln_g1, ln_b1, target)

    g = {}
    g['b_w_out'] = _matmul(o2, dr2b, trans_a=True, name="d_b_out")
    do2 = _matmul(dr2b, p['b_w_out'], trans_b=True, name="d_o2")
    dob, dprojb, dl = _gate_bwd(do2, o, projb)
    dq2, dkv, dkr = _flash_bwd(qcat, kv, kr, dob, lse, dl, rc, rs)
    duq = _matmul(cqn, dq2, trans_a=True, name="d_q_up")
    dcqn = _matmul(dq2, p['b_w_uq'], trans_b=True, name="d_cqn")
    dprojb, dgq = _q_norm_bwd(dcqn, projb, p['b_q_norm_g'], dprojb)
    g['b_w_in'] = _matmul(h1b, dprojb, trans_a=True, name="d_b_in", bn=1280)
    t1 = _matmul(dprojb, p['b_w_in'], trans_b=True, name="d_h1_b", bk=1280)
    dukv = _matmul(ckn, dkv, trans_a=True, name="d_kv_up")
    dckn = _matmul(dkv, p['kv_w_ukv'], trans_b=True, name="d_ckn")
    dckv, dgkv = _kv_mid_bwd(dckn, ckv, dkr, p['kv_norm_g'], rc, rs)
    ddown = _matmul(h1b, dckv, trans_a=True, name="d_kv_down")
    t2 = _matmul(dckv, p['kv_w_down'], trans_b=True, name="d_h1_kv")
    dr1, dr1b, dg0, db0 = _ln0_bwd(dr2, t1, t2, xh1, rstd1, ln_g0)
    g['a_w_out'] = _matmul(u2, dr1b, trans_a=True, name="d_a_out")
    du2 = _matmul(dr1b, p['a_w_out'], trans_b=True, name="d_u2")
    du1, dproj, dng, dnb, dbz = _conv_mid_bwd_rows(u1, proj, du2, p['a_norm_g'], p['a_norm_b'])
    dproj, dcw, dcb, dbv = _conv_bwd(du1, proj, p['a_conv_w'], dproj)
    g['a_w_in'] = _matmul(xb, dproj, trans_a=True, name="d_a_in")
    dxp = _matmul(dproj, p['a_w_in'], trans_b=True, name="d_x_a", bk=2048)
    grad_x, dba_out = _grad_x(dr1, dxp)

    g['ln_g'] = jnp.concatenate([dg0, dg1], axis=0)
    g['ln_b'] = jnp.concatenate([db0, db1], axis=0)
    g['a_b_in'] = jnp.concatenate([dbv, dbz], axis=1)
    g['a_conv_w'] = dcw[:CONV_KERNEL]
    g['a_conv_b'] = dcb
    g['a_norm_g'] = dng
    g['a_norm_b'] = dnb
    g['a_b_out'] = dba_out
    g['kv_w_down'] = ddown[:, :KV_LORA_RANK + QK_ROPE_DIM]
    g['kv_norm_g'] = dgkv
    dukv = dukv.reshape(KV_LORA_RANK, N_HEADS, HEAD_PAD)
    g['kv_w_uk'] = dukv[:, :, :QK_NOPE_DIM]
    g['kv_w_uv'] = dukv[:, :, QK_NOPE_DIM:]
    g['b_q_norm_g'] = dgq
    g['b_w_uq'] = duq.reshape(Q_LORA_RANK, N_HEADS, HEAD_PAD)[:, :, :QK_NOPE_DIM + QK_ROPE_DIM]
    return loss, grad_x, g


def _grad_x(dr1, dxp):
    t, d = dr1.shape
    tm = min(t, 512)

    def body(a_ref, b_ref, o_ref, db_ref):
        i = pl.program_id(0)
        a = a_ref[...]
        o_ref[...] = ALPHA * a + b_ref[...]

        @pl.when(i == 0)
        def _():
            db_ref[...] = jnp.zeros_like(db_ref)

        db_ref[...] += jnp.sum(a, axis=0, keepdims=True)

    row = pl.BlockSpec((tm, d), lambda i: (i, 0))
    return pl.pallas_call(
        body, name="grad_x",
        out_shape=(jax.ShapeDtypeStruct((t, d), F32), jax.ShapeDtypeStruct((1, d), F32)),
        grid=(t // tm,),
        in_specs=[row, row],
        out_specs=(row, pl.BlockSpec((1, d), lambda i: (0, 0))),
        compiler_params=_cparams(("arbitrary",)),
    )(dr1, dxp)


def _shard_to_dev_major(name, gfull):
    e = CONV_WIDTH
    if name in REPLICATED:
        flat = gfull.reshape(1, -1)
        return jnp.broadcast_to(flat, (N_DEV, flat.shape[1]))
    if name == 'a_w_in':
        return gfull.reshape(D_MODEL, N_DEV, -1).transpose(1, 0, 2).reshape(N_DEV, -1)
    if name == 'a_conv_w':
        return gfull.reshape(CONV_KERNEL, N_DEV, -1).transpose(1, 0, 2).reshape(N_DEV, -1)
    if name == 'b_w_in':
        return gfull.reshape(D_MODEL, N_DEV, -1).transpose(1, 0, 2).reshape(N_DEV, -1)
    return gfull.reshape(N_DEV, -1)


def _gathered_to_full(name, gathered, shard_shape):
    blocks = gathered.reshape((N_DEV,) + tuple(shard_shape))
    if name in ('a_w_in', 'b_w_in'):
        return blocks[:, 0].transpose(1, 0, 2).reshape(D_MODEL, -1)
    if name == 'a_conv_w':
        return blocks[:, 0].transpose(1, 0, 2).reshape(CONV_KERNEL, -1)
    if name in ('a_b_in', 'a_conv_b', 'a_norm_g', 'a_norm_b', 'a_b_out'):
        return blocks.reshape(-1)
    if name in ('a_w_out', 'b_w_out'):
        return blocks.reshape(-1, D_MODEL)
    if name == 'b_w_uq':
        return blocks.reshape((-1,) + tuple(shard_shape[2:]))
    return blocks.reshape((-1,) + tuple(shard_shape[1:]))


def _pack_rows(parts, dtype, row_mult):
    flat = jnp.concatenate([a.reshape(-1).astype(dtype) for a in parts])
    n = flat.shape[0]
    rows = -(-n // LANES)
    rows = -(-rows // row_mult) * row_mult
    return jnp.pad(flat, (0, rows * LANES - n)).reshape(rows, LANES)


def kernel(x, positions, ln_g, ln_b, a_w_in, a_b_in, a_conv_w, a_conv_b, a_norm_g, a_norm_b, a_w_out, a_b_out, kv_w_down, kv_norm_g, kv_w_uk, kv_w_uv, b_w_in, b_q_norm_g, b_w_uq, b_w_out, loss_target, m_ln_g, m_ln_b, m_a_w_in, m_a_b_in, m_a_conv_w, m_a_conv_b, m_a_norm_g, m_a_norm_b, m_a_w_out, m_a_b_out, m_kv_w_down, m_kv_norm_g, m_kv_w_uk, m_kv_w_uv, m_b_w_in, m_b_q_norm_g, m_b_w_uq, m_b_w_out, v_ln_g, v_ln_b, v_a_w_in, v_a_b_in, v_a_conv_w, v_a_conv_b, v_a_norm_g, v_a_norm_b, v_a_w_out, v_a_b_out, v_kv_w_down, v_kv_norm_g, v_kv_w_uk, v_kv_w_uv, v_b_w_in, v_b_q_norm_g, v_b_w_uq, v_b_w_out):
    w = dict(ln_g=ln_g, ln_b=ln_b, a_w_in=a_w_in, a_b_in=a_b_in, a_conv_w=a_conv_w, a_conv_b=a_conv_b,
             a_norm_g=a_norm_g, a_norm_b=a_norm_b, a_w_out=a_w_out, a_b_out=a_b_out, kv_w_down=kv_w_down,
             kv_norm_g=kv_norm_g, kv_w_uk=kv_w_uk, kv_w_uv=kv_w_uv, b_w_in=b_w_in, b_q_norm_g=b_q_norm_g,
             b_w_uq=b_w_uq, b_w_out=b_w_out)
    m = dict(ln_g=m_ln_g, ln_b=m_ln_b, a_w_in=m_a_w_in, a_b_in=m_a_b_in, a_conv_w=m_a_conv_w, a_conv_b=m_a_conv_b,
             a_norm_g=m_a_norm_g, a_norm_b=m_a_norm_b, a_w_out=m_a_w_out, a_b_out=m_a_b_out, kv_w_down=m_kv_w_down,
             kv_norm_g=m_kv_norm_g, kv_w_uk=m_kv_w_uk, kv_w_uv=m_kv_w_uv, b_w_in=m_b_w_in, b_q_norm_g=m_b_q_norm_g,
             b_w_uq=m_b_w_uq, b_w_out=m_b_w_out)
    v = dict(ln_g=v_ln_g, ln_b=v_ln_b, a_w_in=v_a_w_in, a_b_in=v_a_b_in, a_conv_w=v_a_conv_w, a_conv_b=v_a_conv_b,
             a_norm_g=v_a_norm_g, a_norm_b=v_a_norm_b, a_w_out=v_a_w_out, a_b_out=v_a_b_out, kv_w_down=v_kv_w_down,
             kv_norm_g=v_kv_norm_g, kv_w_uk=v_kv_w_uk, kv_w_uv=v_kv_w_uv, b_w_in=v_b_w_in, b_q_norm_g=v_b_q_norm_g,
             b_w_uq=v_b_w_uq, b_w_out=v_b_w_out)

    wb = _pack_rows([w[n] for n in MATMUL_WEIGHTS], BF16, 16)
    ws = _pack_rows([w[n] for n in SMALL_WEIGHTS], F32, 8)
    gb, gs = _all_gather_weights(wb, ws)
    gb = gb.reshape(N_DEV, -1)
    gs = gs.reshape(N_DEV, -1)
    full = {}
    off = 0
    for n in MATMUL_WEIGHTS:
        size = math.prod(w[n].shape)
        full[n] = _gathered_to_full(n, gb[:, off:off + size], w[n].shape)
        off += size
    off = 0
    for n in SMALL_WEIGHTS:
        size = math.prod(w[n].shape)
        full[n] = _gathered_to_full(n, gs[:, off:off + size], w[n].shape)
        off += size
    for n in REPLICATED:
        full[n] = w[n]
    p = _prep_weights(full)

    loss_loc, grad_x, g = _local_step(x[0], positions[0], loss_target[0], p)
    loss = lax.psum(loss_loc[0, 0], ("x", "y", "c"))

    gslots = jnp.concatenate([_shard_to_dev_major(n, g[n]) for n in WEIGHT_NAMES], axis=1)
    ntot = gslots.shape[1]
    rows = -(-ntot // LANES)
    rows = -(-rows // 512) * 512
    gslots = jnp.pad(gslots, ((0, 0), (0, rows * LANES - ntot))).reshape(N_DEV, rows, LANES)
    gbuf = _exchange_grads(gslots)
    pack = lambda d: jnp.pad(jnp.concatenate([d[n].reshape(-1) for n in WEIGHT_NAMES]), (0, rows * LANES - ntot)).reshape(rows, LANES)
    go, dl, mo, vo = _adamw(gbuf, pack(w), pack(m), pack(v))

    def unpack(a):
        flat = a.reshape(-1)
        out, off = [], 0
        for n in WEIGHT_NAMES:
            size = math.prod(w[n].shape)
            out.append(flat[off:off + size].reshape(w[n].shape))
            off += size
        return out

    return (loss, grad_x[None], *unpack(go), *unpack(dl), *unpack(mo), *unpack(vo))
```

```python
import functools
import math

import jax
import jax.numpy as jnp
from jax import lax
from jax.experimental import pallas as pl
from jax.experimental.pallas import tpu as pltpu

F32 = jnp.float32
BF16 = jnp.bfloat16

D_MODEL = 1024
DEPTH = 2
CONV_WIDTH = 2 * D_MODEL
CONV_KERNEL = 31
N_HEADS = 16
QK_NOPE_DIM = 128
QK_ROPE_DIM = 64
V_HEAD_DIM = 128
KV_LORA_RANK = D_MODEL // 4
Q_LORA_RANK = D_MODEL // 2
ROPE_THETA = 10000.0
LN_EPS = 1e-5
RMS_EPS = 1e-6
MASK_VALUE = -1e30
ALPHA = (2.0 * DEPTH) ** 0.25
ATTN_SCALE = 1.0 / math.sqrt(QK_NOPE_DIM + QK_ROPE_DIM)

ADAM_LR = 0.001
ADAM_B1 = 0.9
ADAM_B2 = 0.999
ADAM_EPS = 1e-08
ADAM_WD = 0.01
ADAM_STEP = 10

N_DEV = 8
LANES = 128
HEAD_PAD = 256
HALO = 32
VMEM_LIMIT = 56 * 1024 * 1024

WEIGHT_NAMES = ['ln_g', 'ln_b', 'a_w_in', 'a_b_in', 'a_conv_w', 'a_conv_b', 'a_norm_g', 'a_norm_b',
                'a_w_out', 'a_b_out', 'kv_w_down', 'kv_norm_g', 'kv_w_uk', 'kv_w_uv', 'b_w_in',
                'b_q_norm_g', 'b_w_uq', 'b_w_out']
REPLICATED = ('ln_g', 'ln_b', 'kv_norm_g', 'b_q_norm_g')
MATMUL_WEIGHTS = ('a_w_in', 'a_w_out', 'kv_w_down', 'kv_w_uk', 'kv_w_uv', 'b_w_in', 'b_w_uq', 'b_w_out')
SMALL_WEIGHTS = ('a_b_in', 'a_conv_w', 'a_conv_b', 'a_norm_g', 'a_norm_b', 'a_b_out')

NN = (((1,), (0,)), ((), ()))
NT = (((1,), (1,)), ((), ()))
TN = (((0,), (0,)), ((), ()))


def _cparams(sem):
    return pltpu.CompilerParams(dimension_semantics=sem, vmem_limit_bytes=VMEM_LIMIT)


def _sigmoid(x):
    return 1.0 / (1.0 + jnp.exp(-x))


def _aligned(v, m):
    return v if isinstance(v, int) else pl.multiple_of(v, m)


def _swap_rope_halves(t):
    lane = lax.broadcasted_iota(jnp.int32, t.shape, 1)
    return jnp.where(lane < QK_ROPE_DIM // 2, pltpu.roll(t, LANES - QK_ROPE_DIM // 2, 1),
                     pltpu.roll(t, QK_ROPE_DIM // 2, 1))


def _matmul(a, b, *, name, bias=None, trans_a=False, trans_b=False, out_dtype=F32, bm=1024, bn=1024, bk=1024):
    if trans_a:
        kdim, m = a.shape
    else:
        m, kdim = a.shape
    if trans_b:
        n, k2 = b.shape
    else:
        k2, n = b.shape
    assert kdim == k2, (a.shape, b.shape)
    bm, bn, bk = min(bm, m), min(bn, n), min(bk, kdim)
    assert m % bm == 0 and n % bn == 0 and kdim % bk == 0, (name, m, n, kdim, bm, bn, bk)
    nk = kdim // bk
    dims = (((0 if trans_a else 1,), (1 if trans_b else 0,)), ((), ()))
    has_bias = bias is not None

    def body(*refs):
        if has_bias:
            a_ref, b_ref, bias_ref, o_ref = refs[:4]
            rest = refs[4:]
        else:
            a_ref, b_ref, o_ref = refs[:3]
            bias_ref = None
            rest = refs[3:]
        prod = lax.dot_general(a_ref[...], b_ref[...], dims, preferred_element_type=F32)

        def finish(acc):
            if has_bias:
                acc = acc + bias_ref[...]
            o_ref[...] = acc.astype(out_dtype)

        if nk == 1:
            finish(prod)
        else:
            acc_ref = rest[0]
            k = pl.program_id(2)

            @pl.when(k == 0)
            def _():
                acc_ref[...] = prod

            @pl.when(k > 0)
            def _():
                acc_ref[...] += prod

            @pl.when(k == nk - 1)
            def _():
                finish(acc_ref[...])

    a_spec = pl.BlockSpec((bk, bm), lambda i, j, k: (k, i)) if trans_a else pl.BlockSpec((bm, bk), lambda i, j, k: (i, k))
    b_spec = pl.BlockSpec((bn, bk), lambda i, j, k: (j, k)) if trans_b else pl.BlockSpec((bk, bn), lambda i, j, k: (k, j))
    in_specs = [a_spec, b_spec]
    args = [a, b]
    if has_bias:
        in_specs.append(pl.BlockSpec((1, bn), lambda i, j, k: (0, j)))
        args.append(bias)
    return pl.pallas_call(
        body, name=name,
        out_shape=jax.ShapeDtypeStruct((m, n), out_dtype),
        grid=(m // bm, n // bn, nk),
        in_specs=in_specs,
        out_specs=pl.BlockSpec((bm, bn), lambda i, j, k: (i, j)),
        scratch_shapes=[pltpu.VMEM((bm, bn), F32)] if nk > 1 else [],
        compiler_params=_cparams(("parallel", "parallel", "arbitrary")),
    )(*args)


def _rope_tables(pos_col, freq_row):
    t = pos_col.shape[0]
    tm = min(t, 512)

    def body(pos_ref, f_ref, c_ref, s_ref):
        ang = pos_ref[...].astype(F32) * f_ref[...]
        lane = lax.broadcasted_iota(jnp.int32, ang.shape, 1)
        cos = jnp.cos(ang)
        sin = jnp.sin(ang)
        c_ref[...] = jnp.where(lane < QK_ROPE_DIM, cos, 0.0)
        s_ref[...] = jnp.where(lane < QK_ROPE_DIM // 2, -sin, jnp.where(lane < QK_ROPE_DIM, sin, 0.0))

    return pl.pallas_call(
        body, name="rope_tables",
        out_shape=(jax.ShapeDtypeStruct((t, LANES), F32), jax.ShapeDtypeStruct((t, LANES), F32)),
        grid=(t // tm,),
        in_specs=[pl.BlockSpec((tm, 1), lambda i: (i, 0)), pl.BlockSpec((1, LANES), lambda i: (0, 0))],
        out_specs=(pl.BlockSpec((tm, LANES), lambda i: (i, 0)), pl.BlockSpec((tm, LANES), lambda i: (i, 0))),
        compiler_params=_cparams(("parallel",)),
    )(pos_col, freq_row)


def _conv_mid_fwd(proj, cw, cb, ng, nb):
    t = proj.shape[0]
    e = CONV_WIDTH
    tm = min(t, 128)
    hb = tm // HALO
    nchunk = e // LANES

    def body(val_ref, gg_ref, z_ref, valh_ref, ggh_ref, cw_ref, cb_ref, ng_ref, nb_ref, u1_ref, u2_ref, ext_ref):
        i = pl.program_id(0)
        u0 = val_ref[...] * _sigmoid(gg_ref[...])
        h0 = valh_ref[...] * _sigmoid(ggh_ref[...])
        ext_ref[0:HALO, :] = jnp.where(i > 0, h0, 0.0)
        ext_ref[HALO:, :] = u0

        def chunk(c, carry):
            col = pl.multiple_of(c * LANES, LANES)
            acc = jnp.zeros((tm, LANES), F32)
            for j in range(CONV_KERNEL):
                off = HALO - (CONV_KERNEL - 1) + j
                acc = acc + cw_ref[j:j + 1, pl.ds(col, LANES)] * ext_ref[off:off + tm, pl.ds(col, LANES)]
            u1_ref[:, pl.ds(col, LANES)] = acc + cb_ref[:, pl.ds(col, LANES)]
            return carry

        lax.fori_loop(0, nchunk, chunk, 0)
        u1 = u1_ref[...]
        mu = jnp.mean(u1, axis=-1, keepdims=True)
        xc = u1 - mu
        var = jnp.mean(xc * xc, axis=-1, keepdims=True)
        n = xc * lax.rsqrt(var + LN_EPS) * ng_ref[...] + nb_ref[...]
        z = z_ref[...]
        u2_ref[...] = ((n * _sigmoid(n)) * (z * _sigmoid(z))).astype(BF16)

    row = lambda c: pl.BlockSpec((tm, e), lambda i: (i, c))
    halo = lambda c: pl.BlockSpec((HALO, e), lambda i: (jnp.maximum(i * hb - 1, 0), c))
    par = lambda r: pl.BlockSpec((r, e), lambda i: (0, 0))
    return pl.pallas_call(
        body, name="conv_mid_fwd",
        out_shape=(jax.ShapeDtypeStruct((t, e), F32), jax.ShapeDtypeStruct((t, e), BF16)),
        grid=(t // tm,),
        in_specs=[row(0), row(1), row(2), halo(0), halo(1), par(HALO), par(1), par(1), par(1)],
        out_specs=(pl.BlockSpec((tm, e), lambda i: (i, 0)), pl.BlockSpec((tm, e), lambda i: (i, 0))),
        scratch_shapes=[pltpu.VMEM((tm + HALO, e), F32)],
        compiler_params=_cparams(("parallel",)),
    )(proj, proj, proj, proj, proj, cw, cb, ng, nb)


def _ln_res_fwd(x, y, g, b):
    t, d = x.shape
    tm = min(t, 256)

    def body(x_ref, y_ref, g_ref, b_ref, h_ref, hb_ref, xh_ref, rs_ref):
        r = ALPHA * x_ref[...] + y_ref[...]
        mu = jnp.mean(r, axis=-1, keepdims=True)
        xc = r - mu
        var = jnp.mean(xc * xc, axis=-1, keepdims=True)
        rstd = lax.rsqrt(var + LN_EPS)
        xh = xc * rstd
        h = xh * g_ref[...] + b_ref[...]
        h_ref[...] = h
        hb_ref[...] = h.astype(BF16)
        xh_ref[...] = xh
        rs_ref[...] = rstd

    row = pl.BlockSpec((tm, d), lambda i: (i, 0))
    par = pl.BlockSpec((1, d), lambda i: (0, 0))
    return pl.pallas_call(
        body, name="ln0_fwd",
        out_shape=(jax.ShapeDtypeStruct((t, d), F32), jax.ShapeDtypeStruct((t, d), BF16),
                   jax.ShapeDtypeStruct((t, d), F32), jax.ShapeDtypeStruct((t, 1), F32)),
        grid=(t // tm,),
        in_specs=[row, row, par, par],
        out_specs=(row, row, row, pl.BlockSpec((tm, 1), lambda i: (i, 0))),
        compiler_params=_cparams(("parallel",)),
    )(x, y, g, b)


def _kv_mid_fwd(ckv, g, rc, rs):
    t = ckv.shape[0]
    tm = min(t, 512)
    r = KV_LORA_RANK

    def body(ckv_ref, g_ref, c_ref, s_ref, ckn_ref, kr_ref):
        c = ckv_ref[:, 0:r]
        ms = jnp.mean(c * c, axis=-1, keepdims=True)
        ckn_ref[...] = (c * lax.rsqrt(ms + RMS_EPS) * g_ref[...]).astype(BF16)
        tr = ckv_ref[:, r:r + LANES]
        kr_ref[...] = (tr * c_ref[...] + _swap_rope_halves(tr) * s_ref[...]).astype(BF16)

    return pl.pallas_call(
        body, name="kv_mid_fwd",
        out_shape=(jax.ShapeDtypeStruct((t, r), BF16), jax.ShapeDtypeStruct((t, LANES), BF16)),
        grid=(t // tm,),
        in_specs=[pl.BlockSpec((tm, r + LANES), lambda i: (i, 0)), pl.BlockSpec((1, r), lambda i: (0, 0)),
                  pl.BlockSpec((tm, LANES), lambda i: (i, 0)), pl.BlockSpec((tm, LANES), lambda i: (i, 0))],
        out_specs=(pl.BlockSpec((tm, r), lambda i: (i, 0)), pl.BlockSpec((tm, LANES), lambda i: (i, 0))),
        compiler_params=_cparams(("parallel",)),
    )(ckv, g, rc, rs)


def _q_norm_fwd(projb, g):
    t = projb.shape[0]
    tm = min(t, 512)
    r = Q_LORA_RANK

    def body(c_ref, g_ref, o_ref):
        c = c_ref[...]
        ms = jnp.mean(c * c, axis=-1, keepdims=True)
        o_ref[...] = (c * lax.rsqrt(ms + RMS_EPS) * g_ref[...]).astype(BF16)

    return pl.pallas_call(
        body, name="q_norm_fwd",
        out_shape=jax.ShapeDtypeStruct((t, r), BF16),
        grid=(t // tm,),
        in_specs=[pl.BlockSpec((tm, r), lambda i: (i, 0)), pl.BlockSpec((1, r), lambda i: (0, 0))],
        out_specs=pl.BlockSpec((tm, r), lambda i: (i, 0)),
        compiler_params=_cparams(("parallel",)),
    )(projb, g)


def _q_rope_fwd(q2, rc, rs):
    t = q2.shape[0]
    tm = min(t, 256)
    w = N_HEADS * HEAD_PAD

    def body(q_ref, c_ref, s_ref, o_ref):
        c = c_ref[...]
        s = s_ref[...]
        for h in range(N_HEADS):
            a = h * HEAD_PAD
            o_ref[:, a:a + LANES] = q_ref[:, a:a + LANES].astype(BF16)
            tr = q_ref[:, a + LANES:a + 2 * LANES]
            o_ref[:, a + LANES:a + 2 * LANES] = (tr * c + _swap_rope_halves(tr) * s).astype(BF16)

    return pl.pallas_call(
        body, name="q_rope_fwd",
        out_shape=jax.ShapeDtypeStruct((t, w), BF16),
        grid=(t // tm,),
        in_specs=[pl.BlockSpec((tm, w), lambda i: (i, 0)), pl.BlockSpec((tm, LANES), lambda i: (i, 0)),
                  pl.BlockSpec((tm, LANES), lambda i: (i, 0))],
        out_specs=pl.BlockSpec((tm, w), lambda i: (i, 0)),
        compiler_params=_cparams(("parallel",)),
    )(q2, rc, rs)


def _flash_fwd(qcat, kv, kr):
    t = qcat.shape[0]
    tq = min(t, 512)
    nq = t // tq
    exp2_scale = ATTN_SCALE * math.log2(math.e)

    def body(q_ref, kv_ref, kr_ref, o_ref, lse_ref, kcat_ref):
        kcat_ref[:, 0:LANES] = kv_ref[:, 0:LANES]
        kcat_ref[:, LANES:] = kr_ref[...]
        rows = lax.broadcasted_iota(jnp.int32, (tq, tq), 0)
        cols = lax.broadcasted_iota(jnp.int32, (tq, tq), 1)
        for i in range(nq):
            a = i * tq
            q = q_ref[a:a + tq, :]
            sd = lax.dot_general(q, kcat_ref[a:a + tq, :], NT, preferred_element_type=F32)
            sd = jnp.where(cols <= rows, sd, MASK_VALUE)
            m = jnp.max(sd, axis=1, keepdims=True)
            if i > 0:
                sp = lax.dot_general(q, kcat_ref[0:a, :], NT, preferred_element_type=F32)
                m = jnp.maximum(m, jnp.max(sp, axis=1, keepdims=True))
            pd = jnp.exp2((sd - m) * exp2_scale)
            l = jnp.sum(pd, axis=1, keepdims=True)
            acc = lax.dot_general(pd.astype(BF16), kv_ref[a:a + tq, LANES:], NN, preferred_element_type=F32)
            if i > 0:
                pp = jnp.exp2((sp - m) * exp2_scale)
                l = l + jnp.sum(pp, axis=1, keepdims=True)
                acc = acc + lax.dot_general(pp.astype(BF16), kv_ref[0:a, LANES:], NN, preferred_element_type=F32)
            o_ref[a:a + tq, :] = acc / l
            lse_ref[a:a + tq, :] = jnp.broadcast_to(m * ATTN_SCALE + jnp.log(l), (tq, LANES))

    hw = N_HEADS * LANES
    head256 = pl.BlockSpec((t, HEAD_PAD), lambda h: (0, h))
    head128 = pl.BlockSpec((t, LANES), lambda h: (0, h))
    return pl.pallas_call(
        body, name="flash_fwd",
        out_shape=(jax.ShapeDtypeStruct((t, hw), F32), jax.ShapeDtypeStruct((t, hw), F32)),
        grid=(N_HEADS,),
        in_specs=[head256, head256, pl.BlockSpec((t, LANES), lambda h: (0, 0), pipeline_mode=pl.Buffered(1))],
        out_specs=(head128, head128),
        scratch_shapes=[pltpu.VMEM((t, HEAD_PAD), BF16)],
        compiler_params=_cparams(("parallel",)),
    )(qcat, kv, kr)


def _gate_fwd(o, projb):
    t, hw = o.shape
    tm = min(t, 512)
    bw = Q_LORA_RANK
    nb = hw // bw

    def body(o_ref, z_ref, o2_ref):
        z = z_ref[...]
        o2_ref[...] = (o_ref[...] * (z * _sigmoid(z))).astype(BF16)

    return pl.pallas_call(
        body, name="gate_fwd",
        out_shape=jax.ShapeDtypeStruct((t, hw), BF16),
        grid=(t // tm, nb),
        in_specs=[pl.BlockSpec((tm, bw), lambda i, j: (i, j)), pl.BlockSpec((tm, bw), lambda i, j: (i, j + 1))],
        out_specs=pl.BlockSpec((tm, bw), lambda i, j: (i, j)),
        compiler_params=_cparams(("parallel", "parallel")),
    )(o, projb)


def _final_ln_loss(h1, yb, g, b, target):
    t, d = h1.shape
    tm = min(t, 256)

    def body(h_ref, y_ref, g_ref, b_ref, t_ref, loss_ref, dr_ref, drb_ref, dg_ref, db_ref):
        i = pl.program_id(0)
        r = ALPHA * h_ref[...] + y_ref[...]
        mu = jnp.mean(r, axis=-1, keepdims=True)
        xc = r - mu
        var = jnp.mean(xc * xc, axis=-1, keepdims=True)
        rstd = lax.rsqrt(var + LN_EPS)
        xh = xc * rstd
        g = g_ref[...]
        diff = xh * g + b_ref[...] - t_ref[...]
        part = 0.5 * jnp.sum(jnp.mean(diff * diff, axis=-1, keepdims=True))
        dh = diff * (1.0 / d)
        dgp = jnp.sum(dh * xh, axis=0, keepdims=True)
        dbp = jnp.sum(dh, axis=0, keepdims=True)

        @pl.when(i == 0)
        def _():
            loss_ref[...] = jnp.zeros_like(loss_ref)
            dg_ref[...] = jnp.zeros_like(dg_ref)
            db_ref[...] = jnp.zeros_like(db_ref)

        loss_ref[...] += jnp.full(loss_ref.shape, part, F32)
        dg_ref[...] += dgp
        db_ref[...] += dbp
        dxh = dh * g
        dr = rstd * (dxh - jnp.mean(dxh, axis=-1, keepdims=True) - xh * jnp.mean(dxh * xh, axis=-1, keepdims=True))
        dr_ref[...] = dr
        drb_ref[...] = dr.astype(BF16)

    row = pl.BlockSpec((tm, d), lambda i: (i, 0))
    par = pl.BlockSpec((1, d), lambda i: (0, 0))
    return pl.pallas_call(
        body, name="final_ln_loss",
        out_shape=(jax.ShapeDtypeStruct((1, LANES), F32), jax.ShapeDtypeStruct((t, d), F32),
                   jax.ShapeDtypeStruct((t, d), BF16), jax.ShapeDtypeStruct((1, d), F32),
                   jax.ShapeDtypeStruct((1, d), F32)),
        grid=(t // tm,),
        in_specs=[row, row, par, par, row],
        out_specs=(pl.BlockSpec((1, LANES), lambda i: (0, 0)), row, row, par, par),
        compiler_params=_cparams(("arbitrary",)),
    )(h1, yb, g, b, target)


def _gate_bwd(do2, o, projb):
    t, hw = o.shape
    tm = min(t, 512)
    bw = Q_LORA_RANK
    nb = hw // bw
    wb = projb.shape[1]

    def body(d_ref, o_ref, z_ref, dob_ref, dz_ref, dl_ref):
        z = z_ref[...]
        sg = _sigmoid(z)
        d2 = d_ref[...]
        o = o_ref[...]
        do = d2 * (z * sg)
        dob_ref[...] = do.astype(BF16)
        dz_ref[...] = (d2 * o * (sg * (1.0 + z * (1.0 - sg)))).astype(BF16)
        prod = do * o
        for hh in range(bw // LANES):
            dsum = jnp.sum(prod[:, hh * LANES:(hh + 1) * LANES], axis=1, keepdims=True)
            dl_ref[:, hh * LANES:(hh + 1) * LANES] = jnp.broadcast_to(dsum, (tm, LANES))

    blk = pl.BlockSpec((tm, bw), lambda i, j: (i, j))
    blk1 = pl.BlockSpec((tm, bw), lambda i, j: (i, j + 1))
    return pl.pallas_call(
        body, name="gate_bwd",
        out_shape=(jax.ShapeDtypeStruct((t, hw), BF16), jax.ShapeDtypeStruct((t, wb), BF16),
                   jax.ShapeDtypeStruct((t, hw), F32)),
        grid=(t // tm, nb),
        in_specs=[blk, blk, blk1],
        out_specs=(blk, blk1, blk),
        compiler_params=_cparams(("parallel", "parallel")),
    )(do2, o, projb)


def _flash_bwd(qcat, kv, kr, dob, lse, dl, rc, rs):
    t = qcat.shape[0]
    tq = min(t, 512)
    nq = t // tq
    once = pl.Buffered(1)

    def body(q_ref, kv_ref, kr_ref, do_ref, lse_ref, dl_ref, c_ref, s_ref,
             dq_ref, dkv_ref, dkr_ref, kcat_ref, dqa_ref, dka_ref, dva_ref):
        h = pl.program_id(0)
        kcat_ref[:, 0:LANES] = kv_ref[:, 0:LANES]
        kcat_ref[:, LANES:] = kr_ref[...]
        dqa_ref[...] = jnp.zeros_like(dqa_ref)

        @pl.when(h == 0)
        def _():
            dkr_ref[...] = jnp.zeros_like(dkr_ref)

        def block(qb, kb, masked):
            qs = _aligned(qb * tq, tq)
            ks = _aligned(kb * tq, tq)
            q = q_ref[pl.ds(qs, tq), :]
            kc = kcat_ref[pl.ds(ks, tq), :]
            s = lax.dot_general(q, kc, NT, preferred_element_type=F32) * ATTN_SCALE
            if masked:
                rows = lax.broadcasted_iota(jnp.int32, (tq, tq), 0)
                cols = lax.broadcasted_iota(jnp.int32, (tq, tq), 1)
                s = jnp.where(cols <= rows, s, MASK_VALUE)
            lse_q = lse_ref[pl.ds(qs, tq), :][:, 0:1]
            d_q = dl_ref[pl.ds(qs, tq), :][:, 0:1]
            p = jnp.exp(s - lse_q)
            do = do_ref[pl.ds(qs, tq), :]
            v = kv_ref[pl.ds(ks, tq), LANES:]
            dp = lax.dot_general(do, v, NT, preferred_element_type=F32)
            ds = (p * (dp - d_q) * ATTN_SCALE).astype(BF16)
            dva_ref[...] += lax.dot_general(p.astype(BF16), do, TN, preferred_element_type=F32)
            dka_ref[...] += lax.dot_general(ds, q, TN, preferred_element_type=F32)
            dqa_ref[pl.ds(qs, tq), :] += lax.dot_general(ds, kc, NN, preferred_element_type=F32)

        for kb in range(nq):
            dka_ref[...] = jnp.zeros_like(dka_ref)
            dva_ref[...] = jnp.zeros_like(dva_ref)
            block(kb, kb, True)
            if kb + 1 < nq:
                def loop_body(qb, carry, kb=kb):
                    block(qb, kb, False)
                    return carry
                lax.fori_loop(kb + 1, nq, loop_body, 0)
            a = kb * tq
            dk = dka_ref[...]
            dkv_ref[a:a + tq, 0:LANES] = dk[:, 0:LANES].astype(BF16)
            dkv_ref[a:a + tq, LANES:] = dva_ref[...].astype(BF16)
            dkr_ref[a:a + tq, :] += dk[:, LANES:]

        dq = dqa_ref[...]
        dq_ref[:, 0:LANES] = dq[:, 0:LANES].astype(BF16)
        d2 = dq[:, LANES:]
        dq_ref[:, LANES:] = (d2 * c_ref[...] - _swap_rope_halves(d2) * s_ref[...]).astype(BF16)

    hp = N_HEADS * HEAD_PAD
    head256 = pl.BlockSpec((t, HEAD_PAD), lambda h: (0, h))
    head128 = pl.BlockSpec((t, LANES), lambda h: (0, h))
    const128 = pl.BlockSpec((t, LANES), lambda h: (0, 0), pipeline_mode=once)
    return pl.pallas_call(
        body, name="flash_bwd",
        out_shape=(jax.ShapeDtypeStruct((t, hp), BF16), jax.ShapeDtypeStruct((t, hp), BF16),
                   jax.ShapeDtypeStruct((t, LANES), F32)),
        grid=(N_HEADS,),
        in_specs=[head256, head256, const128, head128, head128, head128, const128, const128],
        out_specs=(head256, head256, pl.BlockSpec((t, LANES), lambda h: (0, 0))),
        scratch_shapes=[pltpu.VMEM((t, HEAD_PAD), BF16), pltpu.VMEM((t, HEAD_PAD), F32),
                        pltpu.VMEM((tq, HEAD_PAD), F32), pltpu.VMEM((tq, LANES), F32)],
        compiler_params=_cparams(("arbitrary",)),
    )(qcat, kv, kr, dob, lse, dl, rc, rs)


def _q_norm_bwd(dcqn, projb, g, dprojb):
    t = projb.shape[0]
    tm = min(t, 512)
    r = Q_LORA_RANK

    def body(d_ref, c_ref, g_ref, alias_ref, o_ref, dg_ref):
        i = pl.program_id(0)
        c = c_ref[...]
        d = d_ref[...]
        rr = lax.rsqrt(jnp.mean(c * c, axis=-1, keepdims=True) + RMS_EPS)
        xh = c * rr

        @pl.when(i == 0)
        def _():
            dg_ref[...] = jnp.zeros_like(dg_ref)

        dg_ref[...] += jnp.sum(d * xh, axis=0, keepdims=True)
        dxh = d * g_ref[...]
        o_ref[...] = (rr * (dxh - xh * jnp.mean(dxh * xh, axis=-1, keepdims=True))).astype(BF16)

    blk = pl.BlockSpec((tm, r), lambda i: (i, 0))
    par = pl.BlockSpec((1, r), lambda i: (0, 0))
    return pl.pallas_call(
        body, name="q_norm_bwd",
        out_shape=(jax.ShapeDtypeStruct(dprojb.shape, BF16), jax.ShapeDtypeStruct((1, r), F32)),
        grid=(t // tm,),
        in_specs=[blk, blk, par, pl.BlockSpec(memory_space=pl.ANY)],
        out_specs=(blk, par),
        input_output_aliases={3: 0},
        compiler_params=_cparams(("arbitrary",)),
    )(dcqn, projb, g, dprojb)


def _kv_mid_bwd(dckn, ckv, dkr, g, rc, rs):
    t = ckv.shape[0]
    tm = min(t, 512)
    r = KV_LORA_RANK

    def body(d_ref, ckv_ref, dkr_ref, g_ref, c_ref, s_ref, o_ref, dg_ref):
        i = pl.program_id(0)
        c = ckv_ref[:, 0:r]
        d = d_ref[...]
        rr = lax.rsqrt(jnp.mean(c * c, axis=-1, keepdims=True) + RMS_EPS)
        xh = c * rr

        @pl.when(i == 0)
        def _():
            dg_ref[...] = jnp.zeros_like(dg_ref)

        dg_ref[...] += jnp.sum(d * xh, axis=0, keepdims=True)
        dxh = d * g_ref[...]
        o_ref[:, 0:r] = (rr * (dxh - xh * jnp.mean(dxh * xh, axis=-1, keepdims=True))).astype(BF16)
        dk = dkr_ref[...]
        o_ref[:, r:] = (dk * c_ref[...] - _swap_rope_halves(dk) * s_ref[...]).astype(BF16)

    return pl.pallas_call(
        body, name="kv_mid_bwd",
        out_shape=(jax.ShapeDtypeStruct((t, r + LANES), BF16), jax.ShapeDtypeStruct((1, r), F32)),
        grid=(t // tm,),
        in_specs=[pl.BlockSpec((tm, r), lambda i: (i, 0)), pl.BlockSpec((tm, r + LANES), lambda i: (i, 0)),
                  pl.BlockSpec((tm, LANES), lambda i: (i, 0)), pl.BlockSpec((1, r), lambda i: (0, 0)),
                  pl.BlockSpec((tm, LANES), lambda i: (i, 0)), pl.BlockSpec((tm, LANES), lambda i: (i, 0))],
        out_specs=(pl.BlockSpec((tm, r + LANES), lambda i: (i, 0)), pl.BlockSpec((1, r), lambda i: (0, 0))),
        compiler_params=_cparams(("arbitrary",)),
    )(dckn, ckv, dkr, g, rc, rs)


def _ln0_bwd(dr2, t1, t2, xh, rstd, g):
    t, d = dr2.shape
    tm = min(t, 256)

    def body(a_ref, b_ref, c_ref, xh_ref, rs_ref, g_ref, dr_ref, drb_ref, dg_ref, db_ref):
        i = pl.program_id(0)
        dh = ALPHA * a_ref[...] + b_ref[...] + c_ref[...]
        xh = xh_ref[...]

        @pl.when(i == 0)
        def _():
            dg_ref[...] = jnp.zeros_like(dg_ref)
            db_ref[...] = jnp.zeros_like(db_ref)

        dg_ref[...] += jnp.sum(dh * xh, axis=0, keepdims=True)
        db_ref[...] += jnp.sum(dh, axis=0, keepdims=True)
        dxh = dh * g_ref[...]
        dr = rs_ref[...] * (dxh - jnp.mean(dxh, axis=-1, keepdims=True) - xh * jnp.mean(dxh * xh, axis=-1, keepdims=True))
        dr_ref[...] = dr
        drb_ref[...] = dr.astype(BF16)

    row = pl.BlockSpec((tm, d), lambda i: (i, 0))
    par = pl.BlockSpec((1, d), lambda i: (0, 0))
    return pl.pallas_call(
        body, name="ln0_bwd",
        out_shape=(jax.ShapeDtypeStruct((t, d), F32), jax.ShapeDtypeStruct((t, d), BF16),
                   jax.ShapeDtypeStruct((1, d), F32), jax.ShapeDtypeStruct((1, d), F32)),
        grid=(t // tm,),
        in_specs=[row, row, row, row, pl.BlockSpec((tm, 1), lambda i: (i, 0)), par],
        out_specs=(row, row, par, par),
        compiler_params=_cparams(("arbitrary",)),
    )(dr2, t1, t2, xh, rstd, g)


def _conv_mid_bwd_rows(u1, proj, du2, ng, nb):
    t = u1.shape[0]
    e = CONV_WIDTH
    tm = min(t, 256)

    def body(u1_ref, z_ref, d_ref, ng_ref, nb_ref, du1_ref, dz_ref, dng_ref, dnb_ref, dbz_ref):
        i = pl.program_id(0)
        u1 = u1_ref[...]
        mu = jnp.mean(u1, axis=-1, keepdims=True)
        xc = u1 - mu
        rstd = lax.rsqrt(jnp.mean(xc * xc, axis=-1, keepdims=True) + LN_EPS)
        xh = xc * rstd
        g = ng_ref[...]
        n = xh * g + nb_ref[...]
        sn = _sigmoid(n)
        z = z_ref[...]
        sz = _sigmoid(z)
        du2 = d_ref[...]
        dz = du2 * (n * sn) * (sz * (1.0 + z * (1.0 - sz)))
        dn = du2 * (z * sz) * (sn * (1.0 + n * (1.0 - sn)))

        @pl.when(i == 0)
        def _():
            dng_ref[...] = jnp.zeros_like(dng_ref)
            dnb_ref[...] = jnp.zeros_like(dnb_ref)
            dbz_ref[...] = jnp.zeros_like(dbz_ref)

        dng_ref[...] += jnp.sum(dn * xh, axis=0, keepdims=True)
        dnb_ref[...] += jnp.sum(dn, axis=0, keepdims=True)
        dbz_ref[...] += jnp.sum(dz, axis=0, keepdims=True)
        dz_ref[...] = dz.astype(BF16)
        dxh = dn * g
        du1_ref[...] = rstd * (dxh - jnp.mean(dxh, axis=-1, keepdims=True) - xh * jnp.mean(dxh * xh, axis=-1, keepdims=True))

    row = pl.BlockSpec((tm, e), lambda i: (i, 0))
    par = pl.BlockSpec((1, e), lambda i: (0, 0))
    return pl.pallas_call(
        body, name="conv_mid_bwd_rows",
        out_shape=(jax.ShapeDtypeStruct((t, e), F32), jax.ShapeDtypeStruct((t, 3 * e), BF16),
                   jax.ShapeDtypeStruct((1, e), F32), jax.ShapeDtypeStruct((1, e), F32),
                   jax.ShapeDtypeStruct((1, e), F32)),
        grid=(t // tm,),
        in_specs=[row, pl.BlockSpec((tm, e), lambda i: (i, 2)), row, par, par],
        out_specs=(row, pl.BlockSpec((tm, e), lambda i: (i, 2)), par, par, par),
        compiler_params=_cparams(("arbitrary",)),
    )(u1, proj, du2, ng, nb)


def _conv_bwd(du1, proj, cw, dproj):
    t = du1.shape[0]
    e = CONV_WIDTH
    tm = min(t, 128)
    hb = tm // HALO
    nt = t // tm
    nchunk = e // LANES
    last_halo = t // HALO - 1

    def body(d_ref, dn_ref, val_ref, gg_ref, valh_ref, ggh_ref, cw_ref, alias_ref,
             dp_ref, dcw_ref, dcb_ref, dbv_ref, extu_ref, extd_ref, du0_ref, acc_ref):
        i = pl.program_id(0)
        val = val_ref[...]
        sg = _sigmoid(gg_ref[...])
        h0 = valh_ref[...] * _sigmoid(ggh_ref[...])
        extu_ref[0:HALO, :] = jnp.where(i > 0, h0, 0.0)
        extu_ref[HALO:, :] = val * sg
        extd_ref[0:tm, :] = d_ref[...]
        extd_ref[tm:, :] = jnp.where(i < nt - 1, dn_ref[...], 0.0)

        @pl.when(i == 0)
        def _():
            acc_ref[...] = jnp.zeros_like(acc_ref)
            dbv_ref[...] = jnp.zeros_like(dbv_ref)

        def chunk(c, carry):
            col = pl.multiple_of(c * LANES, LANES)
            d = extd_ref[0:tm, pl.ds(col, LANES)]
            du0 = jnp.zeros((tm, LANES), F32)
            for j in range(CONV_KERNEL):
                offd = CONV_KERNEL - 1 - j
                du0 = du0 + cw_ref[j:j + 1, pl.ds(col, LANES)] * extd_ref[offd:offd + tm, pl.ds(col, LANES)]
                offu = HALO - (CONV_KERNEL - 1) + j
                prod = d * extu_ref[offu:offu + tm, pl.ds(col, LANES)]
                acc_ref[j * 8:(j + 1) * 8, pl.ds(col, LANES)] += jnp.sum(prod.reshape(tm // 8, 8, LANES), axis=0)
            acc_ref[CONV_KERNEL * 8:(CONV_KERNEL + 1) * 8, pl.ds(col, LANES)] += jnp.sum(d.reshape(tm // 8, 8, LANES), axis=0)
            du0_ref[:, pl.ds(col, LANES)] = du0
            return carry

        lax.fori_loop(0, nchunk, chunk, 0)
        du0 = du0_ref[...]
        dval = du0 * sg
        dgg = du0 * val * sg * (1.0 - sg)
        dp_ref[:, 0:e] = dval.astype(BF16)
        dp_ref[:, e:] = dgg.astype(BF16)
        dbv_ref[:, 0:e] += jnp.sum(dval, axis=0, keepdims=True)
        dbv_ref[:, e:] += jnp.sum(dgg, axis=0, keepdims=True)

        @pl.when(i == nt - 1)
        def _():
            for j in range(CONV_KERNEL):
                dcw_ref[j:j + 1, :] = jnp.sum(acc_ref[j * 8:(j + 1) * 8, :], axis=0, keepdims=True)
            dcw_ref[CONV_KERNEL:, :] = jnp.zeros((HALO - CONV_KERNEL, e), F32)
            dcb_ref[...] = jnp.sum(acc_ref[CONV_KERNEL * 8:(CONV_KERNEL + 1) * 8, :], axis=0, keepdims=True)

    row = lambda c: pl.BlockSpec((tm, e), lambda i: (i, c))
    halo_prev = lambda c: pl.BlockSpec((HALO, e), lambda i: (jnp.maximum(i * hb - 1, 0), c))
    halo_next = pl.BlockSpec((HALO, e), lambda i: (jnp.minimum((i + 1) * hb, last_halo), 0))
    return pl.pallas_call(
        body, name="conv_bwd",
        out_shape=(jax.ShapeDtypeStruct(dproj.shape, BF16), jax.ShapeDtypeStruct((HALO, e), F32),
                   jax.ShapeDtypeStruct((1, e), F32), jax.ShapeDtypeStruct((1, 2 * e), F32)),
        grid=(nt,),
        in_specs=[row(0), halo_next, row(0), row(1), halo_prev(0), halo_prev(1),
                  pl.BlockSpec((HALO, e), lambda i: (0, 0)), pl.BlockSpec(memory_space=pl.ANY)],
        out_specs=(pl.BlockSpec((tm, 2 * e), lambda i: (i, 0)), pl.BlockSpec((HALO, e), lambda i: (0, 0)),
                   pl.BlockSpec((1, e), lambda i: (0, 0)), pl.BlockSpec((1, 2 * e), lambda i: (0, 0))),
        scratch_shapes=[pltpu.VMEM((tm + HALO, e), F32), pltpu.VMEM((tm + HALO, e), F32),
                        pltpu.VMEM((tm, e), F32), pltpu.VMEM(((CONV_KERNEL + 1) * 8, e), F32)],
        input_output_aliases={7: 0},
        compiler_params=_cparams(("arbitrary",)),
    )(du1, du1, proj, proj, proj, proj, cw, dproj)


def _adamw(gbuf, w, m, v):
    r = w.shape[0]
    tr = min(r, 512)
    assert r % tr == 0
    c1 = 1.0 - ADAM_B1 ** ADAM_STEP
    c2 = 1.0 - ADAM_B2 ** ADAM_STEP

    def body(g_ref, w_ref, m_ref, v_ref, go_ref, d_ref, mo_ref, vo_ref):
        g = g_ref[0].astype(F32)
        for k in range(1, N_DEV):
            g = g + g_ref[k].astype(F32)
        mn = ADAM_B1 * m_ref[...] + (1.0 - ADAM_B1) * g
        vn = ADAM_B2 * v_ref[...] + (1.0 - ADAM_B2) * (g * g)
        m_hat = mn / c1
        v_hat = vn / c2
        go_ref[...] = g
        d_ref[...] = -ADAM_LR * (m_hat / (jnp.sqrt(v_hat) + ADAM_EPS) + ADAM_WD * w_ref[...])
        mo_ref[...] = mn
        vo_ref[...] = vn

    blk = pl.BlockSpec((tr, LANES), lambda i: (i, 0))
    shp = jax.ShapeDtypeStruct((r, LANES), F32)
    return pl.pallas_call(
        body, name="adamw",
        out_shape=(shp, shp, shp, shp),
        grid=(r // tr,),
        in_specs=[pl.BlockSpec((N_DEV, tr, LANES), lambda i: (0, i, 0)), blk, blk, blk],
        out_specs=(blk, blk, blk, blk),
        compiler_params=_cparams(("parallel",)),
    )(gbuf, w, m, v)


def _mesh_peers():
    x, y, c = lax.axis_index("x"), lax.axis_index("y"), lax.axis_index("c")
    me = 4 * x + 2 * y + c
    peers = []
    for k in range(1, N_DEV):
        px = 1 - x if (k >> 2) & 1 else x
        py = 1 - y if (k >> 1) & 1 else y
        pc = 1 - c if k & 1 else c
        peers.append(((px, py, pc), 4 * px + 2 * py + pc))
    return me, peers


def _all_gather_weights(wb, ws):
    def body(wb_ref, ws_ref, ob_ref, os_ref, send_sems, recv_sems, local_sems):
        me, peers = _mesh_peers()
        pairs = ((wb_ref, ob_ref), (ws_ref, os_ref))
        local = [pltpu.make_async_copy(src, dst.at[me], local_sems.at[a]) for a, (src, dst) in enumerate(pairs)]
        for cp in local:
            cp.start()
        sends = []
        for k, (dev, _) in enumerate(peers):
            for a, (src, dst) in enumerate(pairs):
                cp = pltpu.make_async_remote_copy(src_ref=src, dst_ref=dst.at[me], send_sem=send_sems.at[a, k],
                                                  recv_sem=recv_sems.at[a, k], device_id=dev,
                                                  device_id_type=pl.DeviceIdType.MESH)
                cp.start()
                sends.append(cp)
        for k, (dev, pid) in enumerate(peers):
            for a, (src, dst) in enumerate(pairs):
                pltpu.make_async_remote_copy(src_ref=src, dst_ref=dst.at[pid], send_sem=send_sems.at[a, k],
                                             recv_sem=recv_sems.at[a, k], device_id=dev,
                                             device_id_type=pl.DeviceIdType.MESH).wait_recv()
        for cp in sends:
            cp.wait_send()
        for cp in local:
            cp.wait()

    hbm = pl.BlockSpec(memory_space=pl.ANY)
    return pl.pallas_call(
        body, name="all_gather_weights",
        out_shape=(jax.ShapeDtypeStruct((N_DEV,) + wb.shape, wb.dtype), jax.ShapeDtypeStruct((N_DEV,) + ws.shape, ws.dtype)),
        in_specs=[hbm, hbm], out_specs=(hbm, hbm),
        scratch_shapes=[pltpu.SemaphoreType.DMA((2, N_DEV - 1)), pltpu.SemaphoreType.DMA((2, N_DEV - 1)),
                        pltpu.SemaphoreType.DMA((2,))],
    )(wb, ws)


def _exchange_grads(g):
    def body(g_ref, o_ref, send_sems, recv_sems, local_sem):
        me, peers = _mesh_peers()
        local = pltpu.make_async_copy(g_ref.at[me], o_ref.at[me], local_sem)
        local.start()
        sends = []
        for k, (dev, pid) in enumerate(peers):
            cp = pltpu.make_async_remote_copy(src_ref=g_ref.at[pid], dst_ref=o_ref.at[me], send_sem=send_sems.at[k],
                                              recv_sem=recv_sems.at[k], device_id=dev,
                                              device_id_type=pl.DeviceIdType.MESH)
            cp.start()
            sends.append(cp)
        for k, (dev, pid) in enumerate(peers):
            pltpu.make_async_remote_copy(src_ref=g_ref.at[pid], dst_ref=o_ref.at[pid], send_sem=send_sems.at[k],
                                         recv_sem=recv_sems.at[k], device_id=dev,
                                         device_id_type=pl.DeviceIdType.MESH).wait_recv()
        for cp in sends:
            cp.wait_send()
        local.wait()

    hbm = pl.BlockSpec(memory_space=pl.ANY)
    return pl.pallas_call(
        body, name="exchange_grads",
        out_shape=jax.ShapeDtypeStruct(g.shape, g.dtype),
        in_specs=[hbm], out_specs=hbm,
        scratch_shapes=[pltpu.SemaphoreType.DMA((N_DEV - 1,)), pltpu.SemaphoreType.DMA((N_DEV - 1,)),
                        pltpu.SemaphoreType.DMA(())],
    )(g)


def _prep_weights(w):
    e = CONV_WIDTH
    p = {}
    p['a_w_in'] = w['a_w_in'].reshape(D_MODEL, 3 * e)
    p['a_b_in'] = w['a_b_in'].reshape(1, 3 * e)
    p['a_conv_w'] = jnp.pad(w['a_conv_w'].reshape(CONV_KERNEL, e), ((0, HALO - CONV_KERNEL), (0, 0)))
    p['a_conv_b'] = w['a_conv_b'].reshape(1, e)
    p['a_norm_g'] = w['a_norm_g'].reshape(1, e)
    p['a_norm_b'] = w['a_norm_b'].reshape(1, e)
    p['a_w_out'] = w['a_w_out'].reshape(e, D_MODEL)
    p['a_b_out'] = w['a_b_out'].reshape(1, D_MODEL)
    p['kv_w_down'] = jnp.pad(w['kv_w_down'], ((0, 0), (0, KV_LORA_RANK + LANES - w['kv_w_down'].shape[1])))
    p['kv_norm_g'] = w['kv_norm_g'].reshape(1, KV_LORA_RANK)
    p['kv_w_ukv'] = jnp.concatenate([w['kv_w_uk'], w['kv_w_uv']], axis=2).reshape(KV_LORA_RANK, N_HEADS * HEAD_PAD)
    p['b_w_in'] = w['b_w_in'].reshape(D_MODEL, -1)
    p['b_q_norm_g'] = w['b_q_norm_g'].reshape(1, Q_LORA_RANK)
    uq = w['b_w_uq'].reshape(Q_LORA_RANK, N_HEADS, QK_NOPE_DIM + QK_ROPE_DIM)
    p['b_w_uq'] = jnp.pad(uq, ((0, 0), (0, 0), (0, HEAD_PAD - uq.shape[2]))).reshape(Q_LORA_RANK, N_HEADS * HEAD_PAD)
    p['b_w_out'] = w['b_w_out'].reshape(N_HEADS * V_HEAD_DIM, D_MODEL)
    p['ln_g'] = w['ln_g']
    p['ln_b'] = w['ln_b']
    return p


def _local_step(x, positions, target, p):
    t = x.shape[0]
    e = CONV_WIDTH
    freqs = ROPE_THETA ** (-jnp.arange(0, QK_ROPE_DIM, 2, dtype=F32) / QK_ROPE_DIM)
    freq_row = jnp.concatenate([freqs, freqs, jnp.zeros((LANES - QK_ROPE_DIM,), F32)]).reshape(1, LANES)
    rc, rs = _rope_tables(positions.reshape(t, 1), freq_row)
    xb = x.astype(BF16)
    ln_g0, ln_b0 = p['ln_g'][0:1], p['ln_b'][0:1]
    ln_g1, ln_b1 = p['ln_g'][1:2], p['ln_b'][1:2]

    proj = _matmul(xb, p['a_w_in'], bias=p['a_b_in'], name="a_in")
    u1, u2 = _conv_mid_fwd(proj, p['a_conv_w'], p['a_conv_b'], p['a_norm_g'], p['a_norm_b'])
    y = _matmul(u2, p['a_w_out'], bias=p['a_b_out'], name="a_out", bk=2048)
    h1, h1b, xh1, rstd1 = _ln_res_fwd(x, y, ln_g0, ln_b0)
    ckv = _matmul(h1b, p['kv_w_down'], name="kv_down")
    ckn, kr = _kv_mid_fwd(ckv, p['kv_norm_g'], rc, rs)
    kv = _matmul(ckn, p['kv_w_ukv'], name="kv_up", out_dtype=BF16)
    projb = _matmul(h1b, p['b_w_in'], name="b_in", bn=1280)
    cqn = _q_norm_fwd(projb, p['b_q_norm_g'])
    q2 = _matmul(cqn, p['b_w_uq'], name="q_up")
    qcat = _q_rope_fwd(q2, rc, rs)
    o, lse = _flash_fwd(qcat, kv, kr)
    o2 = _gate_fwd(o, projb)
    yb = _matmul(o2, p['b_w_out'], name="b_out", bk=2048)
    loss, dr2, dr2b, dg1, db1 = _final_ln_loss(h1, yb, ln_g1, ln_b1, target)

    g = {}
    g['b_w_out'] = _matmul(o2, dr2b, trans_a=True, name="d_b_out", out_dtype=BF16)
    do2 = _matmul(dr2b, p['b_w_out'], trans_b=True, name="d_o2")
    dob, dprojb, dl = _gate_bwd(do2, o, projb)
    dq2, dkv, dkr = _flash_bwd(qcat, kv, kr, dob, lse, dl, rc, rs)
    duq = _matmul(cqn, dq2, trans_a=True, name="d_q_up", out_dtype=BF16)
    dcqn = _matmul(dq2, p['b_w_uq'], trans_b=True, name="d_cqn")
    dprojb, dgq = _q_norm_bwd(dcqn, projb, p['b_q_norm_g'], dprojb)
    g['b_w_in'] = _matmul(h1b, dprojb, trans_a=True, name="d_b_in", bn=1280, out_dtype=BF16)
    t1 = _matmul(dprojb, p['b_w_in'], trans_b=True, name="d_h1_b", bk=1280)
    dukv = _matmul(ckn, dkv, trans_a=True, name="d_kv_up", out_dtype=BF16)
    dckn = _matmul(dkv, p['kv_w_ukv'], trans_b=True, name="d_ckn")
    dckv, dgkv = _kv_mid_bwd(dckn, ckv, dkr, p['kv_norm_g'], rc, rs)
    ddown = _matmul(h1b, dckv, trans_a=True, name="d_kv_down", out_dtype=BF16)
    t2 = _matmul(dckv, p['kv_w_down'], trans_b=True, name="d_h1_kv")
    dr1, dr1b, dg0, db0 = _ln0_bwd(dr2, t1, t2, xh1, rstd1, ln_g0)
    g['a_w_out'] = _matmul(u2, dr1b, trans_a=True, name="d_a_out", out_dtype=BF16)
    du2 = _matmul(dr1b, p['a_w_out'], trans_b=True, name="d_u2")
    du1, dproj, dng, dnb, dbz = _conv_mid_bwd_rows(u1, proj, du2, p['a_norm_g'], p['a_norm_b'])
    dproj, dcw, dcb, dbv = _conv_bwd(du1, proj, p['a_conv_w'], dproj)
    g['a_w_in'] = _matmul(xb, dproj, trans_a=True, name="d_a_in", out_dtype=BF16)
    dxp = _matmul(dproj, p['a_w_in'], trans_b=True, name="d_x_a", bk=2048)
    grad_x, dba_out = _grad_x(dr1, dxp)

    g['ln_g'] = jnp.concatenate([dg0, dg1], axis=0)
    g['ln_b'] = jnp.concatenate([db0, db1], axis=0)
    g['a_b_in'] = jnp.concatenate([dbv, dbz], axis=1)
    g['a_conv_w'] = dcw[:CONV_KERNEL]
    g['a_conv_b'] = dcb
    g['a_norm_g'] = dng
    g['a_norm_b'] = dnb
    g['a_b_out'] = dba_out
    g['kv_w_down'] = ddown[:, :KV_LORA_RANK + QK_ROPE_DIM]
    g['kv_norm_g'] = dgkv
    dukv = dukv.reshape(KV_LORA_RANK, N_HEADS, HEAD_PAD)
    g['kv_w_uk'] = dukv[:, :, :QK_NOPE_DIM]
    g['kv_w_uv'] = dukv[:, :, QK_NOPE_DIM:]
    g['b_q_norm_g'] = dgq
    g['b_w_uq'] = duq.reshape(Q_LORA_RANK, N_HEADS, HEAD_PAD)[:, :, :QK_NOPE_DIM + QK_ROPE_DIM]
    return loss, grad_x, g


def _grad_x(dr1, dxp):
    t, d = dr1.shape
    tm = min(t, 512)

    def body(a_ref, b_ref, o_ref, db_ref):
        i = pl.program_id(0)
        a = a_ref[...]
        o_ref[...] = ALPHA * a + b_ref[...]

        @pl.when(i == 0)
        def _():
            db_ref[...] = jnp.zeros_like(db_ref)

        db_ref[...] += jnp.sum(a, axis=0, keepdims=True)

    row = pl.BlockSpec((tm, d), lambda i: (i, 0))
    return pl.pallas_call(
        body, name="grad_x",
        out_shape=(jax.ShapeDtypeStruct((t, d), F32), jax.ShapeDtypeStruct((1, d), F32)),
        grid=(t // tm,),
        in_specs=[row, row],
        out_specs=(row, pl.BlockSpec((1, d), lambda i: (0, 0))),
        compiler_params=_cparams(("arbitrary",)),
    )(dr1, dxp)


def _shard_to_dev_major(name, gfull):
    e = CONV_WIDTH
    if name in REPLICATED:
        flat = gfull.reshape(1, -1)
        return jnp.broadcast_to(flat, (N_DEV, flat.shape[1]))
    if name == 'a_w_in':
        return gfull.reshape(D_MODEL, N_DEV, -1).transpose(1, 0, 2).reshape(N_DEV, -1)
    if name == 'a_conv_w':
        return gfull.reshape(CONV_KERNEL, N_DEV, -1).transpose(1, 0, 2).reshape(N_DEV, -1)
    if name == 'b_w_in':
        return gfull.reshape(D_MODEL, N_DEV, -1).transpose(1, 0, 2).reshape(N_DEV, -1)
    return gfull.reshape(N_DEV, -1)


def _gathered_to_full(name, gathered, shard_shape):
    blocks = gathered.reshape((N_DEV,) + tuple(shard_shape))
    if name in ('a_w_in', 'b_w_in'):
        return blocks[:, 0].transpose(1, 0, 2).reshape(D_MODEL, -1)
    if name == 'a_conv_w':
        return blocks[:, 0].transpose(1, 0, 2).reshape(CONV_KERNEL, -1)
    if name in ('a_b_in', 'a_conv_b', 'a_norm_g', 'a_norm_b', 'a_b_out'):
        return blocks.reshape(-1)
    if name in ('a_w_out', 'b_w_out'):
        return blocks.reshape(-1, D_MODEL)
    if name == 'b_w_uq':
        return blocks.reshape((-1,) + tuple(shard_shape[2:]))
    return blocks.reshape((-1,) + tuple(shard_shape[1:]))


def _pack_rows(parts, dtype, row_mult):
    flat = jnp.concatenate([a.reshape(-1).astype(dtype) for a in parts])
    n = flat.shape[0]
    rows = -(-n // LANES)
    rows = -(-rows // row_mult) * row_mult
    return jnp.pad(flat, (0, rows * LANES - n)).reshape(rows, LANES)


def kernel(x, positions, ln_g, ln_b, a_w_in, a_b_in, a_conv_w, a_conv_b, a_norm_g, a_norm_b, a_w_out, a_b_out, kv_w_down, kv_norm_g, kv_w_uk, kv_w_uv, b_w_in, b_q_norm_g, b_w_uq, b_w_out, loss_target, m_ln_g, m_ln_b, m_a_w_in, m_a_b_in, m_a_conv_w, m_a_conv_b, m_a_norm_g, m_a_norm_b, m_a_w_out, m_a_b_out, m_kv_w_down, m_kv_norm_g, m_kv_w_uk, m_kv_w_uv, m_b_w_in, m_b_q_norm_g, m_b_w_uq, m_b_w_out, v_ln_g, v_ln_b, v_a_w_in, v_a_b_in, v_a_conv_w, v_a_conv_b, v_a_norm_g, v_a_norm_b, v_a_w_out, v_a_b_out, v_kv_w_down, v_kv_norm_g, v_kv_w_uk, v_kv_w_uv, v_b_w_in, v_b_q_norm_g, v_b_w_uq, v_b_w_out):
    w = dict(ln_g=ln_g, ln_b=ln_b, a_w_in=a_w_in, a_b_in=a_b_in, a_conv_w=a_conv_w, a_conv_b=a_conv_b,
             a_norm_g=a_norm_g, a_norm_b=a_norm_b, a_w_out=a_w_out, a_b_out=a_b_out, kv_w_down=kv_w_down,
             kv_norm_g=kv_norm_g, kv_w_uk=kv_w_uk, kv_w_uv=kv_w_uv, b_w_in=b_w_in, b_q_norm_g=b_q_norm_g,
             b_w_uq=b_w_uq, b_w_out=b_w_out)
    m = dict(ln_g=m_ln_g, ln_b=m_ln_b, a_w_in=m_a_w_in, a_b_in=m_a_b_in, a_conv_w=m_a_conv_w, a_conv_b=m_a_conv_b,
             a_norm_g=m_a_norm_g, a_norm_b=m_a_norm_b, a_w_out=m_a_w_out, a_b_out=m_a_b_out, kv_w_down=m_kv_w_down,
             kv_norm_g=m_kv_norm_g, kv_w_uk=m_kv_w_uk, kv_w_uv=m_kv_w_uv, b_w_in=m_b_w_in, b_q_norm_g=m_b_q_norm_g,
             b_w_uq=m_b_w_uq, b_w_out=m_b_w_out)
    v = dict(ln_g=v_ln_g, ln_b=v_ln_b, a_w_in=v_a_w_in, a_b_in=v_a_b_in, a_conv_w=v_a_conv_w, a_conv_b=v_a_conv_b,
             a_norm_g=v_a_norm_g, a_norm_b=v_a_norm_b, a_w_out=v_a_w_out, a_b_out=v_a_b_out, kv_w_down=v_kv_w_down,
             kv_norm_g=v_kv_norm_g, kv_w_uk=v_kv_w_uk, kv_w_uv=v_kv_w_uv, b_w_in=v_b_w_in, b_q_norm_g=v_b_q_norm_g,
             b_w_uq=v_b_w_uq, b_w_out=v_b_w_out)

    wb = _pack_rows([w[n] for n in MATMUL_WEIGHTS], BF16, 16)
    ws = _pack_rows([w[n] for n in SMALL_WEIGHTS], F32, 8)
    gb, gs = _all_gather_weights(wb, ws)
    gb = gb.reshape(N_DEV, -1)
    gs = gs.reshape(N_DEV, -1)
    full = {}
    off = 0
    for n in MATMUL_WEIGHTS:
        size = math.prod(w[n].shape)
        full[n] = _gathered_to_full(n, gb[:, off:off + size], w[n].shape)
        off += size
    off = 0
    for n in SMALL_WEIGHTS:
        size = math.prod(w[n].shape)
        full[n] = _gathered_to_full(n, gs[:, off:off + size], w[n].shape)
        off += size
    for n in REPLICATED:
        full[n] = w[n]
    p = _prep_weights(full)

    loss_loc, grad_x, g = _local_step(x[0], positions[0], loss_target[0], p)
    loss = lax.psum(loss_loc[0, 0], ("x", "y", "c"))

    gslots = jnp.concatenate([_shard_to_dev_major(n, g[n].astype(BF16)) for n in WEIGHT_NAMES], axis=1)
    ntot = gslots.shape[1]
    rows = -(-ntot // LANES)
    rows = -(-rows // 512) * 512
    gslots = jnp.pad(gslots, ((0, 0), (0, rows * LANES - ntot))).reshape(N_DEV, rows, LANES)
    gbuf = _exchange_grads(gslots)
    pack = lambda d: jnp.pad(jnp.concatenate([d[n].reshape(-1) for n in WEIGHT_NAMES]), (0, rows * LANES - ntot)).reshape(rows, LANES)
    go, dl, mo, vo = _adamw(gbuf, pack(w), pack(m), pack(v))

    def unpack(a):
        flat = a.reshape(-1)
        out, off = [], 0
        for n in WEIGHT_NAMES:
            size = math.prod(w[n].shape)
            out.append(flat[off:off + size].reshape(w[n].shape))
            off += size
        return out

    return (loss, grad_x[None], *unpack(go), *unpack(dl), *unpack(mo), *unpack(vo))
```

```python
import functools
import math

import jax
import jax.numpy as jnp
from jax import lax
from jax.experimental import pallas as pl
from jax.experimental.pallas import tpu as pltpu

F32 = jnp.float32
BF16 = jnp.bfloat16

D_MODEL = 1024
DEPTH = 2
CONV_WIDTH = 2 * D_MODEL
CONV_KERNEL = 31
N_HEADS = 16
QK_NOPE_DIM = 128
QK_ROPE_DIM = 64
V_HEAD_DIM = 128
KV_LORA_RANK = D_MODEL // 4
Q_LORA_RANK = D_MODEL // 2
ROPE_THETA = 10000.0
LN_EPS = 1e-5
RMS_EPS = 1e-6
MASK_VALUE = -1e30
ALPHA = (2.0 * DEPTH) ** 0.25
ATTN_SCALE = 1.0 / math.sqrt(QK_NOPE_DIM + QK_ROPE_DIM)

ADAM_LR = 0.001
ADAM_B1 = 0.9
ADAM_B2 = 0.999
ADAM_EPS = 1e-08
ADAM_WD = 0.01
ADAM_STEP = 10

N_DEV = 8
LANES = 128
HEAD_PAD = 256
HALO = 32
VMEM_LIMIT = 56 * 1024 * 1024

WEIGHT_NAMES = ['ln_g', 'ln_b', 'a_w_in', 'a_b_in', 'a_conv_w', 'a_conv_b', 'a_norm_g', 'a_norm_b',
                'a_w_out', 'a_b_out', 'kv_w_down', 'kv_norm_g', 'kv_w_uk', 'kv_w_uv', 'b_w_in',
                'b_q_norm_g', 'b_w_uq', 'b_w_out']
REPLICATED = ('ln_g', 'ln_b', 'kv_norm_g', 'b_q_norm_g')
MATMUL_WEIGHTS = ('a_w_in', 'a_w_out', 'kv_w_down', 'kv_w_uk', 'kv_w_uv', 'b_w_in', 'b_w_uq', 'b_w_out')
SMALL_WEIGHTS = ('a_b_in', 'a_conv_w', 'a_conv_b', 'a_norm_g', 'a_norm_b', 'a_b_out')

NN = (((1,), (0,)), ((), ()))
NT = (((1,), (1,)), ((), ()))
TN = (((0,), (0,)), ((), ()))


def _cparams(sem):
    return pltpu.CompilerParams(dimension_semantics=sem, vmem_limit_bytes=VMEM_LIMIT)


def _sigmoid(x):
    return 1.0 / (1.0 + jnp.exp(-x))


def _aligned(v, m):
    return v if isinstance(v, int) else pl.multiple_of(v, m)


def _swap_rope_halves(t):
    lane = lax.broadcasted_iota(jnp.int32, t.shape, 1)
    return jnp.where(lane < QK_ROPE_DIM // 2, pltpu.roll(t, LANES - QK_ROPE_DIM // 2, 1),
                     pltpu.roll(t, QK_ROPE_DIM // 2, 1))


def _matmul(a, b, *, name, bias=None, trans_a=False, trans_b=False, out_dtype=F32, bm=1024, bn=1024, bk=1024,
            b_blocked=False, out_blocked=False):
    if trans_a:
        kdim, m = a.shape
    else:
        m, kdim = a.shape
    if b_blocked and trans_b:
        n, k2 = b.shape[1], b.shape[0] * b.shape[2]
        bk = b.shape[2]
    elif b_blocked:
        k2, n = b.shape[1], b.shape[0] * b.shape[2]
        bn = b.shape[2]
    elif trans_b:
        n, k2 = b.shape
    else:
        k2, n = b.shape
    assert kdim == k2, (a.shape, b.shape)
    bm, bn, bk = min(bm, m), min(bn, n), min(bk, kdim)
    assert m % bm == 0 and n % bn == 0 and kdim % bk == 0, (name, m, n, kdim, bm, bn, bk)
    nk = kdim // bk
    dims = (((0 if trans_a else 1,), (1 if trans_b else 0,)), ((), ()))
    has_bias = bias is not None

    def body(*refs):
        if has_bias:
            a_ref, b_ref, bias_ref, o_ref = refs[:4]
            rest = refs[4:]
        else:
            a_ref, b_ref, o_ref = refs[:3]
            bias_ref = None
            rest = refs[3:]
        prod = lax.dot_general(a_ref[...], b_ref[...], dims, preferred_element_type=F32)

        def finish(acc):
            if has_bias:
                acc = acc + bias_ref[...]
            o_ref[...] = acc.astype(out_dtype)

        if nk == 1:
            finish(prod)
        else:
            acc_ref = rest[0]
            k = pl.program_id(2)

            @pl.when(k == 0)
            def _():
                acc_ref[...] = prod

            @pl.when(k > 0)
            def _():
                acc_ref[...] += prod

            @pl.when(k == nk - 1)
            def _():
                finish(acc_ref[...])

    a_spec = pl.BlockSpec((bk, bm), lambda i, j, k: (k, i)) if trans_a else pl.BlockSpec((bm, bk), lambda i, j, k: (i, k))
    if b_blocked and trans_b:
        b_spec = pl.BlockSpec((None, bn, bk), lambda i, j, k: (k, j, 0))
    elif b_blocked:
        b_spec = pl.BlockSpec((None, bk, bn), lambda i, j, k: (j, k, 0))
    elif trans_b:
        b_spec = pl.BlockSpec((bn, bk), lambda i, j, k: (j, k))
    else:
        b_spec = pl.BlockSpec((bk, bn), lambda i, j, k: (k, j))
    in_specs = [a_spec, b_spec]
    args = [a, b]
    if has_bias:
        in_specs.append(pl.BlockSpec((1, bn), lambda i, j, k: (0, j)))
        args.append(bias)
    if out_blocked:
        out_struct = jax.ShapeDtypeStruct((n // bn, m, bn), out_dtype)
        out_spec = pl.BlockSpec((None, bm, bn), lambda i, j, k: (j, i, 0))
    else:
        out_struct = jax.ShapeDtypeStruct((m, n), out_dtype)
        out_spec = pl.BlockSpec((bm, bn), lambda i, j, k: (i, j))
    return pl.pallas_call(
        body, name=name,
        out_shape=out_struct,
        grid=(m // bm, n // bn, nk),
        in_specs=in_specs,
        out_specs=out_spec,
        scratch_shapes=[pltpu.VMEM((bm, bn), F32)] if nk > 1 else [],
        compiler_params=_cparams(("parallel", "parallel", "arbitrary")),
    )(*args)


def _rope_tables(pos_col, freq_row):
    t = pos_col.shape[0]
    tm = min(t, 512)

    def body(pos_ref, f_ref, c_ref, s_ref):
        ang = pos_ref[...].astype(F32) * f_ref[...]
        lane = lax.broadcasted_iota(jnp.int32, ang.shape, 1)
        cos = jnp.cos(ang)
        sin = jnp.sin(ang)
        c_ref[...] = jnp.where(lane < QK_ROPE_DIM, cos, 0.0)
        s_ref[...] = jnp.where(lane < QK_ROPE_DIM // 2, -sin, jnp.where(lane < QK_ROPE_DIM, sin, 0.0))

    return pl.pallas_call(
        body, name="rope_tables",
        out_shape=(jax.ShapeDtypeStruct((t, LANES), F32), jax.ShapeDtypeStruct((t, LANES), F32)),
        grid=(t // tm,),
        in_specs=[pl.BlockSpec((tm, 1), lambda i: (i, 0)), pl.BlockSpec((1, LANES), lambda i: (0, 0))],
        out_specs=(pl.BlockSpec((tm, LANES), lambda i: (i, 0)), pl.BlockSpec((tm, LANES), lambda i: (i, 0))),
        compiler_params=_cparams(("parallel",)),
    )(pos_col, freq_row)


def _conv_mid_fwd(proj, cw, cb, ng, nb):
    t = proj.shape[0]
    e = CONV_WIDTH
    tm = min(t, 128)
    hb = tm // HALO
    nchunk = e // LANES

    def body(val_ref, gg_ref, z_ref, valh_ref, ggh_ref, cw_ref, cb_ref, ng_ref, nb_ref, u1_ref, u2_ref, ext_ref):
        i = pl.program_id(0)
        u0 = val_ref[...] * _sigmoid(gg_ref[...])
        h0 = valh_ref[...] * _sigmoid(ggh_ref[...])
        ext_ref[0:HALO, :] = jnp.where(i > 0, h0, 0.0)
        ext_ref[HALO:, :] = u0

        def chunk(c, carry):
            col = pl.multiple_of(c * LANES, LANES)
            acc = jnp.zeros((tm, LANES), F32)
            for j in range(CONV_KERNEL):
                off = HALO - (CONV_KERNEL - 1) + j
                acc = acc + cw_ref[j:j + 1, pl.ds(col, LANES)] * ext_ref[off:off + tm, pl.ds(col, LANES)]
            u1_ref[:, pl.ds(col, LANES)] = acc + cb_ref[:, pl.ds(col, LANES)]
            return carry

        lax.fori_loop(0, nchunk, chunk, 0)
        u1 = u1_ref[...]
        mu = jnp.mean(u1, axis=-1, keepdims=True)
        xc = u1 - mu
        var = jnp.mean(xc * xc, axis=-1, keepdims=True)
        n = xc * lax.rsqrt(var + LN_EPS) * ng_ref[...] + nb_ref[...]
        z = z_ref[...]
        u2_ref[...] = ((n * _sigmoid(n)) * (z * _sigmoid(z))).astype(BF16)

    row = lambda c: pl.BlockSpec((tm, e), lambda i: (i, c))
    halo = lambda c: pl.BlockSpec((HALO, e), lambda i: (jnp.maximum(i * hb - 1, 0), c))
    par = lambda r: pl.BlockSpec((r, e), lambda i: (0, 0))
    return pl.pallas_call(
        body, name="conv_mid_fwd",
        out_shape=(jax.ShapeDtypeStruct((t, e), F32), jax.ShapeDtypeStruct((t, e), BF16)),
        grid=(t // tm,),
        in_specs=[row(0), row(1), row(2), halo(0), halo(1), par(HALO), par(1), par(1), par(1)],
        out_specs=(pl.BlockSpec((tm, e), lambda i: (i, 0)), pl.BlockSpec((tm, e), lambda i: (i, 0))),
        scratch_shapes=[pltpu.VMEM((tm + HALO, e), F32)],
        compiler_params=_cparams(("parallel",)),
    )(proj, proj, proj, proj, proj, cw, cb, ng, nb)


def _ln_res_fwd(x, y, g, b):
    t, d = x.shape
    tm = min(t, 256)

    def body(x_ref, y_ref, g_ref, b_ref, h_ref, hb_ref, xh_ref, rs_ref):
        r = ALPHA * x_ref[...] + y_ref[...]
        mu = jnp.mean(r, axis=-1, keepdims=True)
        xc = r - mu
        var = jnp.mean(xc * xc, axis=-1, keepdims=True)
        rstd = lax.rsqrt(var + LN_EPS)
        xh = xc * rstd
        h = xh * g_ref[...] + b_ref[...]
        h_ref[...] = h
        hb_ref[...] = h.astype(BF16)
        xh_ref[...] = xh
        rs_ref[...] = rstd

    row = pl.BlockSpec((tm, d), lambda i: (i, 0))
    par = pl.BlockSpec((1, d), lambda i: (0, 0))
    return pl.pallas_call(
        body, name="ln0_fwd",
        out_shape=(jax.ShapeDtypeStruct((t, d), F32), jax.ShapeDtypeStruct((t, d), BF16),
                   jax.ShapeDtypeStruct((t, d), F32), jax.ShapeDtypeStruct((t, 1), F32)),
        grid=(t // tm,),
        in_specs=[row, row, par, par],
        out_specs=(row, row, row, pl.BlockSpec((tm, 1), lambda i: (i, 0))),
        compiler_params=_cparams(("parallel",)),
    )(x, y, g, b)


def _kv_mid_fwd(ckv, g, rc, rs):
    t = ckv.shape[0]
    tm = min(t, 512)
    r = KV_LORA_RANK

    def body(ckv_ref, g_ref, c_ref, s_ref, ckn_ref, kr_ref):
        c = ckv_ref[:, 0:r]
        ms = jnp.mean(c * c, axis=-1, keepdims=True)
        ckn_ref[...] = (c * lax.rsqrt(ms + RMS_EPS) * g_ref[...]).astype(BF16)
        tr = ckv_ref[:, r:r + LANES]
        kr_ref[...] = (tr * c_ref[...] + _swap_rope_halves(tr) * s_ref[...]).astype(BF16)

    return pl.pallas_call(
        body, name="kv_mid_fwd",
        out_shape=(jax.ShapeDtypeStruct((t, r), BF16), jax.ShapeDtypeStruct((t, LANES), BF16)),
        grid=(t // tm,),
        in_specs=[pl.BlockSpec((tm, r + LANES), lambda i: (i, 0)), pl.BlockSpec((1, r), lambda i: (0, 0)),
                  pl.BlockSpec((tm, LANES), lambda i: (i, 0)), pl.BlockSpec((tm, LANES), lambda i: (i, 0))],
        out_specs=(pl.BlockSpec((tm, r), lambda i: (i, 0)), pl.BlockSpec((tm, LANES), lambda i: (i, 0))),
        compiler_params=_cparams(("parallel",)),
    )(ckv, g, rc, rs)


def _q_norm_fwd(projb, g):
    t = projb.shape[0]
    tm = min(t, 512)
    r = Q_LORA_RANK

    def body(c_ref, g_ref, o_ref):
        c = c_ref[...]
        ms = jnp.mean(c * c, axis=-1, keepdims=True)
        o_ref[...] = (c * lax.rsqrt(ms + RMS_EPS) * g_ref[...]).astype(BF16)

    return pl.pallas_call(
        body, name="q_norm_fwd",
        out_shape=jax.ShapeDtypeStruct((t, r), BF16),
        grid=(t // tm,),
        in_specs=[pl.BlockSpec((tm, r), lambda i: (i, 0)), pl.BlockSpec((1, r), lambda i: (0, 0))],
        out_specs=pl.BlockSpec((tm, r), lambda i: (i, 0)),
        compiler_params=_cparams(("parallel",)),
    )(projb, g)


def _q_rope_fwd(q2, rc, rs):
    t = q2.shape[0]
    tm = min(t, 256)
    w = N_HEADS * HEAD_PAD

    def body(q_ref, c_ref, s_ref, o_ref):
        c = c_ref[...]
        s = s_ref[...]
        for h in range(N_HEADS):
            a = h * HEAD_PAD
            o_ref[:, a:a + LANES] = q_ref[:, a:a + LANES].astype(BF16)
            tr = q_ref[:, a + LANES:a + 2 * LANES]
            o_ref[:, a + LANES:a + 2 * LANES] = (tr * c + _swap_rope_halves(tr) * s).astype(BF16)

    return pl.pallas_call(
        body, name="q_rope_fwd",
        out_shape=jax.ShapeDtypeStruct((t, w), BF16),
        grid=(t // tm,),
        in_specs=[pl.BlockSpec((tm, w), lambda i: (i, 0)), pl.BlockSpec((tm, LANES), lambda i: (i, 0)),
                  pl.BlockSpec((tm, LANES), lambda i: (i, 0))],
        out_specs=pl.BlockSpec((tm, w), lambda i: (i, 0)),
        compiler_params=_cparams(("parallel",)),
    )(q2, rc, rs)


def _flash_fwd(qcat, kv, kr):
    t = qcat.shape[0]
    tq = min(t, 512)
    nq = t // tq
    exp2_scale = ATTN_SCALE * math.log2(math.e)

    def body(q_ref, kv_ref, kr_ref, o_ref, lse_ref, kcat_ref):
        kcat_ref[:, 0:LANES] = kv_ref[:, 0:LANES]
        kcat_ref[:, LANES:] = kr_ref[...]
        rows = lax.broadcasted_iota(jnp.int32, (tq, tq), 0)
        cols = lax.broadcasted_iota(jnp.int32, (tq, tq), 1)
        for i in range(nq):
            a = i * tq
            q = q_ref[a:a + tq, :]
            sd = lax.dot_general(q, kcat_ref[a:a + tq, :], NT, preferred_element_type=F32)
            sd = jnp.where(cols <= rows, sd, MASK_VALUE)
            m = jnp.max(sd, axis=1, keepdims=True)
            if i > 0:
                sp = lax.dot_general(q, kcat_ref[0:a, :], NT, preferred_element_type=F32)
                m = jnp.maximum(m, jnp.max(sp, axis=1, keepdims=True))
            pd = jnp.exp2((sd - m) * exp2_scale)
            l = jnp.sum(pd, axis=1, keepdims=True)
            acc = lax.dot_general(pd.astype(BF16), kv_ref[a:a + tq, LANES:], NN, preferred_element_type=F32)
            if i > 0:
                pp = jnp.exp2((sp - m) * exp2_scale)
                l = l + jnp.sum(pp, axis=1, keepdims=True)
                acc = acc + lax.dot_general(pp.astype(BF16), kv_ref[0:a, LANES:], NN, preferred_element_type=F32)
            o_ref[a:a + tq, :] = acc / l
            lse_ref[a:a + tq, :] = jnp.broadcast_to(m * ATTN_SCALE + jnp.log(l), (tq, LANES))

    hw = N_HEADS * LANES
    head256 = pl.BlockSpec((t, HEAD_PAD), lambda h: (0, h))
    head128 = pl.BlockSpec((t, LANES), lambda h: (0, h))
    return pl.pallas_call(
        body, name="flash_fwd",
        out_shape=(jax.ShapeDtypeStruct((t, hw), F32), jax.ShapeDtypeStruct((t, hw), F32)),
        grid=(N_HEADS,),
        in_specs=[head256, head256, pl.BlockSpec((t, LANES), lambda h: (0, 0), pipeline_mode=pl.Buffered(1))],
        out_specs=(head128, head128),
        scratch_shapes=[pltpu.VMEM((t, HEAD_PAD), BF16)],
        compiler_params=_cparams(("parallel",)),
    )(qcat, kv, kr)


def _gate_fwd(o, projb):
    t, hw = o.shape
    tm = min(t, 512)
    bw = Q_LORA_RANK
    nb = hw // bw

    def body(o_ref, z_ref, o2_ref):
        z = z_ref[...]
        o2_ref[...] = (o_ref[...] * (z * _sigmoid(z))).astype(BF16)

    return pl.pallas_call(
        body, name="gate_fwd",
        out_shape=jax.ShapeDtypeStruct((t, hw), BF16),
        grid=(t // tm, nb),
        in_specs=[pl.BlockSpec((tm, bw), lambda i, j: (i, j)), pl.BlockSpec((tm, bw), lambda i, j: (i, j + 1))],
        out_specs=pl.BlockSpec((tm, bw), lambda i, j: (i, j)),
        compiler_params=_cparams(("parallel", "parallel")),
    )(o, projb)


def _final_ln_loss(h1, yb, g, b, target):
    t, d = h1.shape
    tm = min(t, 256)

    def body(h_ref, y_ref, g_ref, b_ref, t_ref, loss_ref, dr_ref, drb_ref, dg_ref, db_ref):
        i = pl.program_id(0)
        r = ALPHA * h_ref[...] + y_ref[...]
        mu = jnp.mean(r, axis=-1, keepdims=True)
        xc = r - mu
        var = jnp.mean(xc * xc, axis=-1, keepdims=True)
        rstd = lax.rsqrt(var + LN_EPS)
        xh = xc * rstd
        g = g_ref[...]
        diff = xh * g + b_ref[...] - t_ref[...]
        part = 0.5 * jnp.sum(jnp.mean(diff * diff, axis=-1, keepdims=True))
        dh = diff * (1.0 / d)
        dgp = jnp.sum(dh * xh, axis=0, keepdims=True)
        dbp = jnp.sum(dh, axis=0, keepdims=True)

        @pl.when(i == 0)
        def _():
            loss_ref[...] = jnp.zeros_like(loss_ref)
            dg_ref[...] = jnp.zeros_like(dg_ref)
            db_ref[...] = jnp.zeros_like(db_ref)

        loss_ref[...] += jnp.full(loss_ref.shape, part, F32)
        dg_ref[...] += dgp
        db_ref[...] += dbp
        dxh = dh * g
        dr = rstd * (dxh - jnp.mean(dxh, axis=-1, keepdims=True) - xh * jnp.mean(dxh * xh, axis=-1, keepdims=True))
        dr_ref[...] = dr
        drb_ref[...] = dr.astype(BF16)

    row = pl.BlockSpec((tm, d), lambda i: (i, 0))
    par = pl.BlockSpec((1, d), lambda i: (0, 0))
    return pl.pallas_call(
        body, name="final_ln_loss",
        out_shape=(jax.ShapeDtypeStruct((1, LANES), F32), jax.ShapeDtypeStruct((t, d), F32),
                   jax.ShapeDtypeStruct((t, d), BF16), jax.ShapeDtypeStruct((1, d), F32),
                   jax.ShapeDtypeStruct((1, d), F32)),
        grid=(t // tm,),
        in_specs=[row, row, par, par, row],
        out_specs=(pl.BlockSpec((1, LANES), lambda i: (0, 0)), row, row, par, par),
        compiler_params=_cparams(("arbitrary",)),
    )(h1, yb, g, b, target)


def _gate_bwd(do2, o, projb):
    t, hw = o.shape
    tm = min(t, 512)
    bw = Q_LORA_RANK
    nb = hw // bw
    wb = projb.shape[1]

    def body(d_ref, o_ref, z_ref, dob_ref, dz_ref, dl_ref):
        z = z_ref[...]
        sg = _sigmoid(z)
        d2 = d_ref[...]
        o = o_ref[...]
        do = d2 * (z * sg)
        dob_ref[...] = do.astype(BF16)
        dz_ref[...] = (d2 * o * (sg * (1.0 + z * (1.0 - sg)))).astype(BF16)
        prod = do * o
        for hh in range(bw // LANES):
            dsum = jnp.sum(prod[:, hh * LANES:(hh + 1) * LANES], axis=1, keepdims=True)
            dl_ref[:, hh * LANES:(hh + 1) * LANES] = jnp.broadcast_to(dsum, (tm, LANES))

    blk = pl.BlockSpec((tm, bw), lambda i, j: (i, j))
    blk1 = pl.BlockSpec((tm, bw), lambda i, j: (i, j + 1))
    return pl.pallas_call(
        body, name="gate_bwd",
        out_shape=(jax.ShapeDtypeStruct((t, hw), BF16), jax.ShapeDtypeStruct((t, wb), BF16),
                   jax.ShapeDtypeStruct((t, hw), F32)),
        grid=(t // tm, nb),
        in_specs=[blk, blk, blk1],
        out_specs=(blk, blk1, blk),
        compiler_params=_cparams(("parallel", "parallel")),
    )(do2, o, projb)


def _flash_bwd(qcat, kv, kr, dob, lse, dl, rc, rs):
    t = qcat.shape[0]
    tq = min(t, 512)
    nq = t // tq
    once = pl.Buffered(1)

    def body(q_ref, kv_ref, kr_ref, do_ref, lse_ref, dl_ref, c_ref, s_ref,
             dq_ref, dkv_ref, dkr_ref, kcat_ref, dqa_ref, dka_ref, dva_ref):
        h = pl.program_id(0)
        kcat_ref[:, 0:LANES] = kv_ref[:, 0:LANES]
        kcat_ref[:, LANES:] = kr_ref[...]
        dqa_ref[...] = jnp.zeros_like(dqa_ref)

        @pl.when(h == 0)
        def _():
            dkr_ref[...] = jnp.zeros_like(dkr_ref)

        def block(qb, kb, masked):
            qs = _aligned(qb * tq, tq)
            ks = _aligned(kb * tq, tq)
            q = q_ref[pl.ds(qs, tq), :]
            kc = kcat_ref[pl.ds(ks, tq), :]
            s = lax.dot_general(q, kc, NT, preferred_element_type=F32) * ATTN_SCALE
            if masked:
                rows = lax.broadcasted_iota(jnp.int32, (tq, tq), 0)
                cols = lax.broadcasted_iota(jnp.int32, (tq, tq), 1)
                s = jnp.where(cols <= rows, s, MASK_VALUE)
            lse_q = lse_ref[pl.ds(qs, tq), :][:, 0:1]
            d_q = dl_ref[pl.ds(qs, tq), :][:, 0:1]
            p = jnp.exp(s - lse_q)
            do = do_ref[pl.ds(qs, tq), :]
            v = kv_ref[pl.ds(ks, tq), LANES:]
            dp = lax.dot_general(do, v, NT, preferred_element_type=F32)
            ds = (p * (dp - d_q) * ATTN_SCALE).astype(BF16)
            dva_ref[...] += lax.dot_general(p.astype(BF16), do, TN, preferred_element_type=F32)
            dka_ref[...] += lax.dot_general(ds, q, TN, preferred_element_type=F32)
            dqa_ref[pl.ds(qs, tq), :] += lax.dot_general(ds, kc, NN, preferred_element_type=F32)

        for kb in range(nq):
            dka_ref[...] = jnp.zeros_like(dka_ref)
            dva_ref[...] = jnp.zeros_like(dva_ref)
            block(kb, kb, True)
            if kb + 1 < nq:
                def loop_body(qb, carry, kb=kb):
                    block(qb, kb, False)
                    return carry
                lax.fori_loop(kb + 1, nq, loop_body, 0)
            a = kb * tq
            dk = dka_ref[...]
            dkv_ref[a:a + tq, 0:LANES] = dk[:, 0:LANES].astype(BF16)
            dkv_ref[a:a + tq, LANES:] = dva_ref[...].astype(BF16)
            dkr_ref[a:a + tq, :] += dk[:, LANES:]

        dq = dqa_ref[...]
        dq_ref[:, 0:LANES] = dq[:, 0:LANES].astype(BF16)
        d2 = dq[:, LANES:]
        dq_ref[:, LANES:] = (d2 * c_ref[...] - _swap_rope_halves(d2) * s_ref[...]).astype(BF16)

    hp = N_HEADS * HEAD_PAD
    head256 = pl.BlockSpec((t, HEAD_PAD), lambda h: (0, h))
    head128 = pl.BlockSpec((t, LANES), lambda h: (0, h))
    const128 = pl.BlockSpec((t, LANES), lambda h: (0, 0), pipeline_mode=once)
    return pl.pallas_call(
        body, name="flash_bwd",
        out_shape=(jax.ShapeDtypeStruct((t, hp), BF16), jax.ShapeDtypeStruct((t, hp), BF16),
                   jax.ShapeDtypeStruct((t, LANES), F32)),
        grid=(N_HEADS,),
        in_specs=[head256, head256, const128, head128, head128, head128, const128, const128],
        out_specs=(head256, head256, pl.BlockSpec((t, LANES), lambda h: (0, 0))),
        scratch_shapes=[pltpu.VMEM((t, HEAD_PAD), BF16), pltpu.VMEM((t, HEAD_PAD), F32),
                        pltpu.VMEM((tq, HEAD_PAD), F32), pltpu.VMEM((tq, LANES), F32)],
        compiler_params=_cparams(("arbitrary",)),
    )(qcat, kv, kr, dob, lse, dl, rc, rs)


def _q_norm_bwd(dcqn, projb, g, dprojb):
    t = projb.shape[0]
    tm = min(t, 512)
    r = Q_LORA_RANK

    def body(d_ref, c_ref, g_ref, alias_ref, o_ref, dg_ref):
        i = pl.program_id(0)
        c = c_ref[...]
        d = d_ref[...]
        rr = lax.rsqrt(jnp.mean(c * c, axis=-1, keepdims=True) + RMS_EPS)
        xh = c * rr

        @pl.when(i == 0)
        def _():
            dg_ref[...] = jnp.zeros_like(dg_ref)

        dg_ref[...] += jnp.sum(d * xh, axis=0, keepdims=True)
        dxh = d * g_ref[...]
        o_ref[...] = (rr * (dxh - xh * jnp.mean(dxh * xh, axis=-1, keepdims=True))).astype(BF16)

    blk = pl.BlockSpec((tm, r), lambda i: (i, 0))
    par = pl.BlockSpec((1, r), lambda i: (0, 0))
    return pl.pallas_call(
        body, name="q_norm_bwd",
        out_shape=(jax.ShapeDtypeStruct(dprojb.shape, BF16), jax.ShapeDtypeStruct((1, r), F32)),
        grid=(t // tm,),
        in_specs=[blk, blk, par, pl.BlockSpec(memory_space=pl.ANY)],
        out_specs=(blk, par),
        input_output_aliases={3: 0},
        compiler_params=_cparams(("arbitrary",)),
    )(dcqn, projb, g, dprojb)


def _kv_mid_bwd(dckn, ckv, dkr, g, rc, rs):
    t = ckv.shape[0]
    tm = min(t, 512)
    r = KV_LORA_RANK

    def body(d_ref, ckv_ref, dkr_ref, g_ref, c_ref, s_ref, o_ref, dg_ref):
        i = pl.program_id(0)
        c = ckv_ref[:, 0:r]
        d = d_ref[...]
        rr = lax.rsqrt(jnp.mean(c * c, axis=-1, keepdims=True) + RMS_EPS)
        xh = c * rr

        @pl.when(i == 0)
        def _():
            dg_ref[...] = jnp.zeros_like(dg_ref)

        dg_ref[...] += jnp.sum(d * xh, axis=0, keepdims=True)
        dxh = d * g_ref[...]
        o_ref[:, 0:r] = (rr * (dxh - xh * jnp.mean(dxh * xh, axis=-1, keepdims=True))).astype(BF16)
        dk = dkr_ref[...]
        o_ref[:, r:] = (dk * c_ref[...] - _swap_rope_halves(dk) * s_ref[...]).astype(BF16)

    return pl.pallas_call(
        body, name="kv_mid_bwd",
        out_shape=(jax.ShapeDtypeStruct((t, r + LANES), BF16), jax.ShapeDtypeStruct((1, r), F32)),
        grid=(t // tm,),
        in_specs=[pl.BlockSpec((tm, r), lambda i: (i, 0)), pl.BlockSpec((tm, r + LANES), lambda i: (i, 0)),
                  pl.BlockSpec((tm, LANES), lambda i: (i, 0)), pl.BlockSpec((1, r), lambda i: (0, 0)),
                  pl.BlockSpec((tm, LANES), lambda i: (i, 0)), pl.BlockSpec((tm, LANES), lambda i: (i, 0))],
        out_specs=(pl.BlockSpec((tm, r + LANES), lambda i: (i, 0)), pl.BlockSpec((1, r), lambda i: (0, 0))),
        compiler_params=_cparams(("arbitrary",)),
    )(dckn, ckv, dkr, g, rc, rs)


def _ln0_bwd(dr2, t1, t2, xh, rstd, g):
    t, d = dr2.shape
    tm = min(t, 256)

    def body(a_ref, b_ref, c_ref, xh_ref, rs_ref, g_ref, dr_ref, drb_ref, dg_ref, db_ref):
        i = pl.program_id(0)
        dh = ALPHA * a_ref[...] + b_ref[...] + c_ref[...]
        xh = xh_ref[...]

        @pl.when(i == 0)
        def _():
            dg_ref[...] = jnp.zeros_like(dg_ref)
            db_ref[...] = jnp.zeros_like(db_ref)

        dg_ref[...] += jnp.sum(dh * xh, axis=0, keepdims=True)
        db_ref[...] += jnp.sum(dh, axis=0, keepdims=True)
        dxh = dh * g_ref[...]
        dr = rs_ref[...] * (dxh - jnp.mean(dxh, axis=-1, keepdims=True) - xh * jnp.mean(dxh * xh, axis=-1, keepdims=True))
        dr_ref[...] = dr
        drb_ref[...] = dr.astype(BF16)

    row = pl.BlockSpec((tm, d), lambda i: (i, 0))
    par = pl.BlockSpec((1, d), lambda i: (0, 0))
    return pl.pallas_call(
        body, name="ln0_bwd",
        out_shape=(jax.ShapeDtypeStruct((t, d), F32), jax.ShapeDtypeStruct((t, d), BF16),
                   jax.ShapeDtypeStruct((1, d), F32), jax.ShapeDtypeStruct((1, d), F32)),
        grid=(t // tm,),
        in_specs=[row, row, row, row, pl.BlockSpec((tm, 1), lambda i: (i, 0)), par],
        out_specs=(row, row, par, par),
        compiler_params=_cparams(("arbitrary",)),
    )(dr2, t1, t2, xh, rstd, g)


def _conv_mid_bwd_rows(u1, proj, du2, ng, nb):
    t = u1.shape[0]
    e = CONV_WIDTH
    tm = min(t, 256)

    def body(u1_ref, z_ref, d_ref, ng_ref, nb_ref, du1_ref, dz_ref, dng_ref, dnb_ref, dbz_ref):
        i = pl.program_id(0)
        u1 = u1_ref[...]
        mu = jnp.mean(u1, axis=-1, keepdims=True)
        xc = u1 - mu
        rstd = lax.rsqrt(jnp.mean(xc * xc, axis=-1, keepdims=True) + LN_EPS)
        xh = xc * rstd
        g = ng_ref[...]
        n = xh * g + nb_ref[...]
        sn = _sigmoid(n)
        z = z_ref[...]
        sz = _sigmoid(z)
        du2 = d_ref[...]
        dz = du2 * (n * sn) * (sz * (1.0 + z * (1.0 - sz)))
        dn = du2 * (z * sz) * (sn * (1.0 + n * (1.0 - sn)))

        @pl.when(i == 0)
        def _():
            dng_ref[...] = jnp.zeros_like(dng_ref)
            dnb_ref[...] = jnp.zeros_like(dnb_ref)
            dbz_ref[...] = jnp.zeros_like(dbz_ref)

        dng_ref[...] += jnp.sum(dn * xh, axis=0, keepdims=True)
        dnb_ref[...] += jnp.sum(dn, axis=0, keepdims=True)
        dbz_ref[...] += jnp.sum(dz, axis=0, keepdims=True)
        dz_ref[...] = dz.astype(BF16)
        dxh = dn * g
        du1_ref[...] = rstd * (dxh - jnp.mean(dxh, axis=-1, keepdims=True) - xh * jnp.mean(dxh * xh, axis=-1, keepdims=True))

    row = pl.BlockSpec((tm, e), lambda i: (i, 0))
    par = pl.BlockSpec((1, e), lambda i: (0, 0))
    return pl.pallas_call(
        body, name="conv_mid_bwd_rows",
        out_shape=(jax.ShapeDtypeStruct((t, e), F32), jax.ShapeDtypeStruct((t, 3 * e), BF16),
                   jax.ShapeDtypeStruct((1, e), F32), jax.ShapeDtypeStruct((1, e), F32),
                   jax.ShapeDtypeStruct((1, e), F32)),
        grid=(t // tm,),
        in_specs=[row, pl.BlockSpec((tm, e), lambda i: (i, 2)), row, par, par],
        out_specs=(row, pl.BlockSpec((tm, e), lambda i: (i, 2)), par, par, par),
        compiler_params=_cparams(("arbitrary",)),
    )(u1, proj, du2, ng, nb)


def _conv_bwd(du1, proj, cw, dproj):
    t = du1.shape[0]
    e = CONV_WIDTH
    tm = min(t, 128)
    hb = tm // HALO
    nt = t // tm
    nchunk = e // LANES
    last_halo = t // HALO - 1

    def body(d_ref, dn_ref, val_ref, gg_ref, valh_ref, ggh_ref, cw_ref, alias_ref,
             dp_ref, dcw_ref, dcb_ref, dbv_ref, extu_ref, extd_ref, du0_ref, acc_ref):
        i = pl.program_id(0)
        val = val_ref[...]
        sg = _sigmoid(gg_ref[...])
        h0 = valh_ref[...] * _sigmoid(ggh_ref[...])
        extu_ref[0:HALO, :] = jnp.where(i > 0, h0, 0.0)
        extu_ref[HALO:, :] = val * sg
        extd_ref[0:tm, :] = d_ref[...]
        extd_ref[tm:, :] = jnp.where(i < nt - 1, dn_ref[...], 0.0)

        @pl.when(i == 0)
        def _():
            acc_ref[...] = jnp.zeros_like(acc_ref)
            dbv_ref[...] = jnp.zeros_like(dbv_ref)

        def chunk(c, carry):
            col = pl.multiple_of(c * LANES, LANES)
            d = extd_ref[0:tm, pl.ds(col, LANES)]
            du0 = jnp.zeros((tm, LANES), F32)
            for j in range(CONV_KERNEL):
                offd = CONV_KERNEL - 1 - j
                du0 = du0 + cw_ref[j:j + 1, pl.ds(col, LANES)] * extd_ref[offd:offd + tm, pl.ds(col, LANES)]
                offu = HALO - (CONV_KERNEL - 1) + j
                prod = d * extu_ref[offu:offu + tm, pl.ds(col, LANES)]
                acc_ref[j * 8:(j + 1) * 8, pl.ds(col, LANES)] += jnp.sum(prod.reshape(tm // 8, 8, LANES), axis=0)
            acc_ref[CONV_KERNEL * 8:(CONV_KERNEL + 1) * 8, pl.ds(col, LANES)] += jnp.sum(d.reshape(tm // 8, 8, LANES), axis=0)
            du0_ref[:, pl.ds(col, LANES)] = du0
            return carry

        lax.fori_loop(0, nchunk, chunk, 0)
        du0 = du0_ref[...]
        dval = du0 * sg
        dgg = du0 * val * sg * (1.0 - sg)
        dp_ref[:, 0:e] = dval.astype(BF16)
        dp_ref[:, e:] = dgg.astype(BF16)
        dbv_ref[:, 0:e] += jnp.sum(dval, axis=0, keepdims=True)
        dbv_ref[:, e:] += jnp.sum(dgg, axis=0, keepdims=True)

        @pl.when(i == nt - 1)
        def _():
            for j in range(CONV_KERNEL):
                dcw_ref[j:j + 1, :] = jnp.sum(acc_ref[j * 8:(j + 1) * 8, :], axis=0, keepdims=True)
            dcw_ref[CONV_KERNEL:, :] = jnp.zeros((HALO - CONV_KERNEL, e), F32)
            dcb_ref[...] = jnp.sum(acc_ref[CONV_KERNEL * 8:(CONV_KERNEL + 1) * 8, :], axis=0, keepdims=True)

    row = lambda c: pl.BlockSpec((tm, e), lambda i: (i, c))
    halo_prev = lambda c: pl.BlockSpec((HALO, e), lambda i: (jnp.maximum(i * hb - 1, 0), c))
    halo_next = pl.BlockSpec((HALO, e), lambda i: (jnp.minimum((i + 1) * hb, last_halo), 0))
    return pl.pallas_call(
        body, name="conv_bwd",
        out_shape=(jax.ShapeDtypeStruct(dproj.shape, BF16), jax.ShapeDtypeStruct((HALO, e), F32),
                   jax.ShapeDtypeStruct((1, e), F32), jax.ShapeDtypeStruct((1, 2 * e), F32)),
        grid=(nt,),
        in_specs=[row(0), halo_next, row(0), row(1), halo_prev(0), halo_prev(1),
                  pl.BlockSpec((HALO, e), lambda i: (0, 0)), pl.BlockSpec(memory_space=pl.ANY)],
        out_specs=(pl.BlockSpec((tm, 2 * e), lambda i: (i, 0)), pl.BlockSpec((HALO, e), lambda i: (0, 0)),
                   pl.BlockSpec((1, e), lambda i: (0, 0)), pl.BlockSpec((1, 2 * e), lambda i: (0, 0))),
        scratch_shapes=[pltpu.VMEM((tm + HALO, e), F32), pltpu.VMEM((tm + HALO, e), F32),
                        pltpu.VMEM((tm, e), F32), pltpu.VMEM(((CONV_KERNEL + 1) * 8, e), F32)],
        input_output_aliases={7: 0},
        compiler_params=_cparams(("arbitrary",)),
    )(du1, du1, proj, proj, proj, proj, cw, dproj)


def _adam_update(g, w, m, v):
    c1 = 1.0 - ADAM_B1 ** ADAM_STEP
    c2 = 1.0 - ADAM_B2 ** ADAM_STEP
    mn = ADAM_B1 * m + (1.0 - ADAM_B1) * g
    vn = ADAM_B2 * v + (1.0 - ADAM_B2) * (g * g)
    delta = -ADAM_LR * ((mn / c1) / (jnp.sqrt(vn / c2) + ADAM_EPS) + ADAM_WD * w)
    return delta, mn, vn


def _adamw(name, gbuf, w, m, v):
    r, c = w.shape
    tr = min(r, 256)
    assert r % tr == 0

    def body(g_ref, w_ref, m_ref, v_ref, go_ref, d_ref, mo_ref, vo_ref):
        g = g_ref[0].astype(F32)
        for k in range(1, N_DEV):
            g = g + g_ref[k].astype(F32)
        go_ref[...] = g
        d_ref[...], mo_ref[...], vo_ref[...] = _adam_update(g, w_ref[...], m_ref[...], v_ref[...])

    blk = pl.BlockSpec((tr, c), lambda i: (i, 0))
    shp = jax.ShapeDtypeStruct((r, c), F32)
    return pl.pallas_call(
        body, name="adamw_" + name,
        out_shape=(shp, shp, shp, shp),
        grid=(r // tr,),
        in_specs=[pl.BlockSpec((N_DEV, tr, c), lambda i: (0, i, 0)), blk, blk, blk],
        out_specs=(blk, blk, blk, blk),
        compiler_params=_cparams(("parallel",)),
    )(gbuf, w, m, v)


SMALL_TABLE_COLS = 1024
SMALL_LAYOUT = {
    'ln_g': (0, 2, 1024), 'ln_b': (8, 2, 1024), 'a_b_in': (16, 1, 768), 'a_conv_b': (24, 1, 256),
    'a_norm_g': (32, 1, 256), 'a_norm_b': (40, 1, 256), 'a_b_out': (48, 1, 128), 'kv_norm_g': (56, 1, 256),
    'b_q_norm_g': (64, 1, 512), 'a_conv_w': (72, CONV_KERNEL, 256),
}
SMALL_TABLE_ROWS = 104
SMALL_NAMES = tuple(SMALL_LAYOUT)


def _pack_small_grads(dg0, dg1, db0, db1, dbv, dbz, dcb, dng, dnb, dba, dgkv, dgq, dcw):
    e = CONV_WIDTH
    ce = e // N_DEV

    def body(dg0_ref, dg1_ref, db0_ref, db1_ref, dbv_ref, dbz_ref, dcb_ref, dng_ref, dnb_ref, dba_ref, dgkv_ref,
             dgq_ref, dcw_ref, o_ref):
        o_ref[...] = jnp.zeros_like(o_ref)
        for d in range(N_DEV):
            o_ref[d, 0:1, :] = dg0_ref[...]
            o_ref[d, 1:2, :] = dg1_ref[...]
            o_ref[d, 8:9, :] = db0_ref[...]
            o_ref[d, 9:10, :] = db1_ref[...]
            for q in range(3):
                col = d * 3 * ce + q * ce
                src = dbv_ref[:, col:col + ce] if col < 2 * e else dbz_ref[:, col - 2 * e:col - 2 * e + ce]
                o_ref[d, 16:17, q * ce:(q + 1) * ce] = src
            o_ref[d, 24:25, 0:ce] = dcb_ref[:, d * ce:(d + 1) * ce]
            o_ref[d, 32:33, 0:ce] = dng_ref[:, d * ce:(d + 1) * ce]
            o_ref[d, 40:41, 0:ce] = dnb_ref[:, d * ce:(d + 1) * ce]
            o_ref[d, 48:49, 0:LANES] = dba_ref[:, d * LANES:(d + 1) * LANES]
            o_ref[d, 56:57, 0:KV_LORA_RANK] = dgkv_ref[...]
            o_ref[d, 64:65, 0:Q_LORA_RANK] = dgq_ref[...]
            o_ref[d, 72:72 + HALO, 0:ce] = dcw_ref[:, d * ce:(d + 1) * ce]

    return pl.pallas_call(
        body, name="pack_small_grads",
        out_shape=jax.ShapeDtypeStruct((N_DEV, SMALL_TABLE_ROWS, SMALL_TABLE_COLS), F32),
        compiler_params=pltpu.CompilerParams(vmem_limit_bytes=VMEM_LIMIT),
    )(dg0, dg1, db0, db1, dbv, dbz, dcb, dng, dnb, dba, dgkv, dgq, dcw)


def _adamw_small(gtab, ws, ms, vs):
    n = len(SMALL_NAMES)

    def body(*refs):
        g_ref = refs[0]
        w_refs, m_refs, v_refs = refs[1:1 + n], refs[1 + n:1 + 2 * n], refs[1 + 2 * n:1 + 3 * n]
        outs = refs[1 + 3 * n:]
        for i, name in enumerate(SMALL_NAMES):
            r0, r, c = SMALL_LAYOUT[name]
            g = g_ref[0, r0:r0 + r, 0:c]
            for k in range(1, N_DEV):
                g = g + g_ref[k, r0:r0 + r, 0:c]
            delta, mn, vn = _adam_update(g, w_refs[i][...], m_refs[i][...], v_refs[i][...])
            outs[i][...] = g
            outs[n + i][...] = delta
            outs[2 * n + i][...] = mn
            outs[3 * n + i][...] = vn

    shapes = tuple(jax.ShapeDtypeStruct(w.shape, F32) for w in ws)
    res = pl.pallas_call(
        body, name="adamw_small",
        out_shape=shapes * 4,
        compiler_params=pltpu.CompilerParams(vmem_limit_bytes=VMEM_LIMIT),
    )(gtab, *ws, *ms, *vs)
    return res[:n], res[n:2 * n], res[2 * n:3 * n], res[3 * n:]


def _mesh_peers():
    x, y, c = lax.axis_index("x"), lax.axis_index("y"), lax.axis_index("c")
    me = 4 * x + 2 * y + c
    peers = []
    for k in range(1, N_DEV):
        px = 1 - x if (k >> 2) & 1 else x
        py = 1 - y if (k >> 1) & 1 else y
        pc = 1 - c if k & 1 else c
        peers.append(((px, py, pc), 4 * px + 2 * py + pc))
    return me, peers


def _all_gather(name, shards):
    na = len(shards)

    def body(*refs):
        srcs, dsts = refs[:na], refs[na:2 * na]
        send_sems, recv_sems, local_sems = refs[2 * na:]
        me, peers = _mesh_peers()
        local = [pltpu.make_async_copy(srcs[a], dsts[a].at[me], local_sems.at[a]) for a in range(na)]
        for cp in local:
            cp.start()
        sends = []
        for k, (dev, _) in enumerate(peers):
            for a in range(na):
                cp = pltpu.make_async_remote_copy(src_ref=srcs[a], dst_ref=dsts[a].at[me], send_sem=send_sems.at[a, k],
                                                  recv_sem=recv_sems.at[a, k], device_id=dev,
                                                  device_id_type=pl.DeviceIdType.MESH)
                cp.start()
                sends.append(cp)
        for k, (dev, pid) in enumerate(peers):
            for a in range(na):
                pltpu.make_async_remote_copy(src_ref=srcs[a], dst_ref=dsts[a].at[pid], send_sem=send_sems.at[a, k],
                                             recv_sem=recv_sems.at[a, k], device_id=dev,
                                             device_id_type=pl.DeviceIdType.MESH).wait_recv()
        for cp in sends:
            cp.wait_send()
        for cp in local:
            cp.wait()

    hbm = pl.BlockSpec(memory_space=pl.ANY)
    return pl.pallas_call(
        body, name=name,
        out_shape=tuple(jax.ShapeDtypeStruct((N_DEV,) + s.shape, s.dtype) for s in shards),
        in_specs=[hbm] * na, out_specs=(hbm,) * na,
        scratch_shapes=[pltpu.SemaphoreType.DMA((na, N_DEV - 1)), pltpu.SemaphoreType.DMA((na, N_DEV - 1)),
                        pltpu.SemaphoreType.DMA((na,))],
    )(*shards)


def _exchange(name, slots):
    na = len(slots)

    def body(*refs):
        srcs, dsts = refs[:na], refs[na:2 * na]
        send_sems, recv_sems, local_sems = refs[2 * na:]
        me, peers = _mesh_peers()
        local = [pltpu.make_async_copy(srcs[a].at[me], dsts[a].at[me], local_sems.at[a]) for a in range(na)]
        for cp in local:
            cp.start()
        sends = []
        for k, (dev, pid) in enumerate(peers):
            for a in range(na):
                cp = pltpu.make_async_remote_copy(src_ref=srcs[a].at[pid], dst_ref=dsts[a].at[me],
                                                  send_sem=send_sems.at[a, k], recv_sem=recv_sems.at[a, k],
                                                  device_id=dev, device_id_type=pl.DeviceIdType.MESH)
                cp.start()
                sends.append(cp)
        for k, (dev, pid) in enumerate(peers):
            for a in range(na):
                pltpu.make_async_remote_copy(src_ref=srcs[a].at[pid], dst_ref=dsts[a].at[pid],
                                             send_sem=send_sems.at[a, k], recv_sem=recv_sems.at[a, k],
                                             device_id=dev, device_id_type=pl.DeviceIdType.MESH).wait_recv()
        for cp in sends:
            cp.wait_send()
        for cp in local:
            cp.wait()

    hbm = pl.BlockSpec(memory_space=pl.ANY)
    return pl.pallas_call(
        body, name=name,
        out_shape=tuple(jax.ShapeDtypeStruct(s.shape, s.dtype) for s in slots),
        in_specs=[hbm] * na, out_specs=(hbm,) * na,
        scratch_shapes=[pltpu.SemaphoreType.DMA((na, N_DEV - 1)), pltpu.SemaphoreType.DMA((na, N_DEV - 1)),
                        pltpu.SemaphoreType.DMA((na,))],
    )(*slots)


def _prep_weights(w):
    e = CONV_WIDTH
    p = {}
    if w['a_w_in'].shape == (N_DEV, D_MODEL, 3 * e // N_DEV):
        p['a_w_in3'] = w['a_w_in']
    else:
        p['a_w_in3'] = w['a_w_in'].reshape(D_MODEL, N_DEV, 3 * e // N_DEV).transpose(1, 0, 2)
    p['a_b_in'] = w['a_b_in'].reshape(1, 3 * e)
    p['a_conv_w'] = jnp.pad(w['a_conv_w'].reshape(CONV_KERNEL, e), ((0, HALO - CONV_KERNEL), (0, 0)))
    p['a_conv_b'] = w['a_conv_b'].reshape(1, e)
    p['a_norm_g'] = w['a_norm_g'].reshape(1, e)
    p['a_norm_b'] = w['a_norm_b'].reshape(1, e)
    p['a_w_out'] = w['a_w_out'].reshape(e, D_MODEL)
    p['a_b_out'] = w['a_b_out'].reshape(1, D_MODEL)
    p['kv_w_down'] = jnp.pad(w['kv_w_down'], ((0, 0), (0, KV_LORA_RANK + LANES - w['kv_w_down'].shape[1])))
    p['kv_norm_g'] = w['kv_norm_g'].reshape(1, KV_LORA_RANK)
    p['kv_w_ukv'] = jnp.concatenate([w['kv_w_uk'], w['kv_w_uv']], axis=2).reshape(KV_LORA_RANK, N_HEADS * HEAD_PAD)
    p['b_w_in'] = w['b_w_in'].reshape(D_MODEL, -1)
    p['b_q_norm_g'] = w['b_q_norm_g'].reshape(1, Q_LORA_RANK)
    uq = w['b_w_uq'].reshape(Q_LORA_RANK, N_HEADS, QK_NOPE_DIM + QK_ROPE_DIM)
    p['b_w_uq'] = jnp.pad(uq, ((0, 0), (0, 0), (0, HEAD_PAD - uq.shape[2]))).reshape(Q_LORA_RANK, N_HEADS * HEAD_PAD)
    p['b_w_out'] = w['b_w_out'].reshape(N_HEADS * V_HEAD_DIM, D_MODEL)
    p['ln_g'] = w['ln_g']
    p['ln_b'] = w['ln_b']
    return p


def _local_step(x, positions, target, p):
    t = x.shape[0]
    e = CONV_WIDTH
    freqs = ROPE_THETA ** (-jnp.arange(0, QK_ROPE_DIM, 2, dtype=F32) / QK_ROPE_DIM)
    freq_row = jnp.concatenate([freqs, freqs, jnp.zeros((LANES - QK_ROPE_DIM,), F32)]).reshape(1, LANES)
    rc, rs = _rope_tables(positions.reshape(t, 1), freq_row)
    xb = x.astype(BF16)
    ln_g0, ln_b0 = p['ln_g'][0:1], p['ln_b'][0:1]
    ln_g1, ln_b1 = p['ln_g'][1:2], p['ln_b'][1:2]

    proj = _matmul(xb, p['a_w_in3'], bias=p['a_b_in'], name="a_in", b_blocked=True)
    u1, u2 = _conv_mid_fwd(proj, p['a_conv_w'], p['a_conv_b'], p['a_norm_g'], p['a_norm_b'])
    y = _matmul(u2, p['a_w_out'], bias=p['a_b_out'], name="a_out", bk=2048)
    h1, h1b, xh1, rstd1 = _ln_res_fwd(x, y, ln_g0, ln_b0)
    ckv = _matmul(h1b, p['kv_w_down'], name="kv_down")
    ckn, kr = _kv_mid_fwd(ckv, p['kv_norm_g'], rc, rs)
    kv = _matmul(ckn, p['kv_w_ukv'], name="kv_up", out_dtype=BF16)
    projb = _matmul(h1b, p['b_w_in'], name="b_in", bn=1280)
    cqn = _q_norm_fwd(projb, p['b_q_norm_g'])
    q2 = _matmul(cqn, p['b_w_uq'], name="q_up")
    qcat = _q_rope_fwd(q2, rc, rs)
    o, lse = _flash_fwd(qcat, kv, kr)
    o2 = _gate_fwd(o, projb)
    yb = _matmul(o2, p['b_w_out'], name="b_out", bk=2048)
    loss, dr2, dr2b, dg1, db1 = _final_ln_loss(h1, yb, ln_g1, ln_b1, target)

    g = {}
    g['b_w_out'] = _matmul(o2, dr2b, trans_a=True, name="d_b_out", out_dtype=BF16)
    do2 = _matmul(dr2b, p['b_w_out'], trans_b=True, name="d_o2")
    dob, dprojb, dl = _gate_bwd(do2, o, projb)
    dq2, dkv, dkr = _flash_bwd(qcat, kv, kr, dob, lse, dl, rc, rs)
    duq = _matmul(cqn, dq2, trans_a=True, name="d_q_up", out_dtype=BF16)
    dcqn = _matmul(dq2, p['b_w_uq'], trans_b=True, name="d_cqn")
    dprojb, dgq = _q_norm_bwd(dcqn, projb, p['b_q_norm_g'], dprojb)
    g['b_w_in'] = _matmul(h1b, dprojb, trans_a=True, name="d_b_in", bn=1280, out_dtype=BF16)
    t1 = _matmul(dprojb, p['b_w_in'], trans_b=True, name="d_h1_b", bk=1280)
    dukv = _matmul(ckn, dkv, trans_a=True, name="d_kv_up", out_dtype=BF16)
    dckn = _matmul(dkv, p['kv_w_ukv'], trans_b=True, name="d_ckn")
    dckv, dgkv = _kv_mid_bwd(dckn, ckv, dkr, p['kv_norm_g'], rc, rs)
    ddown = _matmul(h1b, dckv, trans_a=True, name="d_kv_down", out_dtype=BF16)
    t2 = _matmul(dckv, p['kv_w_down'], trans_b=True, name="d_h1_kv")
    dr1, dr1b, dg0, db0 = _ln0_bwd(dr2, t1, t2, xh1, rstd1, ln_g0)
    g['a_w_out'] = _matmul(u2, dr1b, trans_a=True, name="d_a_out", out_dtype=BF16)
    du2 = _matmul(dr1b, p['a_w_out'], trans_b=True, name="d_u2")
    du1, dproj, dng, dnb, dbz = _conv_mid_bwd_rows(u1, proj, du2, p['a_norm_g'], p['a_norm_b'])
    dproj, dcw, dcb, dbv = _conv_bwd(du1, proj, p['a_conv_w'], dproj)
    g['a_w_in'] = _matmul(xb, dproj, trans_a=True, name="d_a_in", out_dtype=BF16, bn=3 * e // N_DEV, out_blocked=True)
    dxp = _matmul(dproj, p['a_w_in3'], trans_b=True, name="d_x_a", b_blocked=True)
    grad_x, dba_out = _grad_x(dr1, dxp)

    g['a_w_out'] = g['a_w_out'].reshape(N_DEV, -1, D_MODEL)
    g['b_w_out'] = g['b_w_out'].reshape(N_DEV, -1, D_MODEL)
    g['b_w_in'] = g['b_w_in'].reshape(D_MODEL, N_DEV, -1).transpose(1, 0, 2)
    g['kv_w_down'] = ddown[:, :KV_LORA_RANK + QK_ROPE_DIM].reshape(N_DEV, -1, KV_LORA_RANK + QK_ROPE_DIM)
    dukv = dukv.reshape(KV_LORA_RANK, N_HEADS, HEAD_PAD)
    g['kv_w_uk'] = dukv[:, :, :QK_NOPE_DIM].reshape(N_DEV, -1, QK_NOPE_DIM)
    g['kv_w_uv'] = dukv[:, :, QK_NOPE_DIM:].reshape(N_DEV, -1, V_HEAD_DIM)
    g['b_w_uq'] = duq.reshape(Q_LORA_RANK, N_HEADS, HEAD_PAD)[:, :, :QK_NOPE_DIM + QK_ROPE_DIM].reshape(
        N_DEV, -1, QK_NOPE_DIM + QK_ROPE_DIM)
    small = (dg0, dg1, db0, db1, dbv, dbz, dcb, dng, dnb, dba_out, dgkv, dgq, dcw)
    return loss, grad_x, g, small


def _grad_x(dr1, dxp):
    t, d = dr1.shape
    tm = min(t, 512)

    def body(a_ref, b_ref, o_ref, db_ref):
        i = pl.program_id(0)
        a = a_ref[...]
        o_ref[...] = ALPHA * a + b_ref[...]

        @pl.when(i == 0)
        def _():
            db_ref[...] = jnp.zeros_like(db_ref)

        db_ref[...] += jnp.sum(a, axis=0, keepdims=True)

    row = pl.BlockSpec((tm, d), lambda i: (i, 0))
    return pl.pallas_call(
        body, name="grad_x",
        out_shape=(jax.ShapeDtypeStruct((t, d), F32), jax.ShapeDtypeStruct((1, d), F32)),
        grid=(t // tm,),
        in_specs=[row, row],
        out_specs=(row, pl.BlockSpec((1, d), lambda i: (0, 0))),
        compiler_params=_cparams(("arbitrary",)),
    )(dr1, dxp)


def _shard_2d(a):
    return a.reshape(-1, a.shape[-1])


def _gathered_to_full(name, blocks):
    if name == 'a_w_in':
        return blocks
    if name == 'b_w_in':
        return blocks.transpose(1, 0, 2).reshape(D_MODEL, -1)
    if name == 'a_conv_w':
        return blocks.transpose(1, 0, 2).reshape(CONV_KERNEL, -1)
    if name in ('a_b_in', 'a_conv_b', 'a_norm_g', 'a_norm_b', 'a_b_out'):
        return blocks.reshape(-1)
    if name in ('kv_w_uk', 'kv_w_uv'):
        return blocks.reshape(KV_LORA_RANK, N_HEADS, -1)
    if name == 'b_w_uq':
        return blocks.reshape(Q_LORA_RANK, N_HEADS, -1)
    return blocks.reshape(-1, blocks.shape[-1])


def kernel(x, positions, ln_g, ln_b, a_w_in, a_b_in, a_conv_w, a_conv_b, a_norm_g, a_norm_b, a_w_out, a_b_out, kv_w_down, kv_norm_g, kv_w_uk, kv_w_uv, b_w_in, b_q_norm_g, b_w_uq, b_w_out, loss_target, m_ln_g, m_ln_b, m_a_w_in, m_a_b_in, m_a_conv_w, m_a_conv_b, m_a_norm_g, m_a_norm_b, m_a_w_out, m_a_b_out, m_kv_w_down, m_kv_norm_g, m_kv_w_uk, m_kv_w_uv, m_b_w_in, m_b_q_norm_g, m_b_w_uq, m_b_w_out, v_ln_g, v_ln_b, v_a_w_in, v_a_b_in, v_a_conv_w, v_a_conv_b, v_a_norm_g, v_a_norm_b, v_a_w_out, v_a_b_out, v_kv_w_down, v_kv_norm_g, v_kv_w_uk, v_kv_w_uv, v_b_w_in, v_b_q_norm_g, v_b_w_uq, v_b_w_out):
    w = dict(ln_g=ln_g, ln_b=ln_b, a_w_in=a_w_in, a_b_in=a_b_in, a_conv_w=a_conv_w, a_conv_b=a_conv_b,
             a_norm_g=a_norm_g, a_norm_b=a_norm_b, a_w_out=a_w_out, a_b_out=a_b_out, kv_w_down=kv_w_down,
             kv_norm_g=kv_norm_g, kv_w_uk=kv_w_uk, kv_w_uv=kv_w_uv, b_w_in=b_w_in, b_q_norm_g=b_q_norm_g,
             b_w_uq=b_w_uq, b_w_out=b_w_out)
    m = dict(ln_g=m_ln_g, ln_b=m_ln_b, a_w_in=m_a_w_in, a_b_in=m_a_b_in, a_conv_w=m_a_conv_w, a_conv_b=m_a_conv_b,
             a_norm_g=m_a_norm_g, a_norm_b=m_a_norm_b, a_w_out=m_a_w_out, a_b_out=m_a_b_out, kv_w_down=m_kv_w_down,
             kv_norm_g=m_kv_norm_g, kv_w_uk=m_kv_w_uk, kv_w_uv=m_kv_w_uv, b_w_in=m_b_w_in, b_q_norm_g=m_b_q_norm_g,
             b_w_uq=m_b_w_uq, b_w_out=m_b_w_out)
    v = dict(ln_g=v_ln_g, ln_b=v_ln_b, a_w_in=v_a_w_in, a_b_in=v_a_b_in, a_conv_w=v_a_conv_w, a_conv_b=v_a_conv_b,
             a_norm_g=v_a_norm_g, a_norm_b=v_a_norm_b, a_w_out=v_a_w_out, a_b_out=v_a_b_out, kv_w_down=v_kv_w_down,
             kv_norm_g=v_kv_norm_g, kv_w_uk=v_kv_w_uk, kv_w_uv=v_kv_w_uv, b_w_in=v_b_w_in, b_q_norm_g=v_b_q_norm_g,
             b_w_uq=v_b_w_uq, b_w_out=v_b_w_out)

    small_flat = jnp.concatenate([w[n].reshape(-1) for n in SMALL_WEIGHTS]).reshape(1, -1)
    gathered = _all_gather("all_gather_weights", [_shard_2d(w[n]).astype(BF16) for n in MATMUL_WEIGHTS] + [small_flat])
    full = {n: _gathered_to_full(n, gathered[i]) for i, n in enumerate(MATMUL_WEIGHTS)}
    gs = gathered[-1].reshape(N_DEV, -1)
    off = 0
    for n in SMALL_WEIGHTS:
        size = math.prod(w[n].shape)
        full[n] = _gathered_to_full(n, gs[:, off:off + size].reshape((N_DEV,) + _shard_2d(w[n]).shape))
        off += size
    for n in REPLICATED:
        full[n] = w[n]
    p = _prep_weights(full)

    loss_loc, grad_x, g, small = _local_step(x[0], positions[0], loss_target[0], p)
    loss = lax.psum(loss_loc[0, 0], ("x", "y", "c"))

    received = _exchange("exchange_grads", [g[n] for n in MATMUL_WEIGHTS] + [_pack_small_grads(*small)])
    res = {}
    for i, n in enumerate(MATMUL_WEIGHTS):
        res[n] = _adamw(n, received[i], _shard_2d(w[n]), _shard_2d(m[n]), _shard_2d(v[n]))
    two_d = lambda d: [_shard_2d(d[n]) if d[n].ndim > 1 else d[n].reshape(1, -1) for n in SMALL_NAMES]
    sg, sd, sm, sv = _adamw_small(received[-1], two_d(w), two_d(m), two_d(v))
    for i, n in enumerate(SMALL_NAMES):
        res[n] = (sg[i], sd[i], sm[i], sv[i])
    outs = [[res[n][j].reshape(w[n].shape) for n in WEIGHT_NAMES] for j in range(4)]
    return (loss, grad_x[None], *outs[0], *outs[1], *outs[2], *outs[3])
```

```python
import functools
import math

import jax
import jax.numpy as jnp
from jax import lax
from jax.experimental import pallas as pl
from jax.experimental.pallas import tpu as pltpu

F32 = jnp.float32
BF16 = jnp.bfloat16

D_MODEL = 1024
DEPTH = 2
CONV_WIDTH = 2 * D_MODEL
CONV_KERNEL = 31
N_HEADS = 16
QK_NOPE_DIM = 128
QK_ROPE_DIM = 64
V_HEAD_DIM = 128
KV_LORA_RANK = D_MODEL // 4
Q_LORA_RANK = D_MODEL // 2
ROPE_THETA = 10000.0
LN_EPS = 1e-5
RMS_EPS = 1e-6
MASK_VALUE = -1e30
ALPHA = (2.0 * DEPTH) ** 0.25
ATTN_SCALE = 1.0 / math.sqrt(QK_NOPE_DIM + QK_ROPE_DIM)

ADAM_LR = 0.001
ADAM_B1 = 0.9
ADAM_B2 = 0.999
ADAM_EPS = 1e-08
ADAM_WD = 0.01
ADAM_STEP = 10

N_DEV = 8
LANES = 128
HEAD_PAD = 256
HALO = 32
VMEM_LIMIT = 56 * 1024 * 1024

WEIGHT_NAMES = ['ln_g', 'ln_b', 'a_w_in', 'a_b_in', 'a_conv_w', 'a_conv_b', 'a_norm_g', 'a_norm_b',
                'a_w_out', 'a_b_out', 'kv_w_down', 'kv_norm_g', 'kv_w_uk', 'kv_w_uv', 'b_w_in',
                'b_q_norm_g', 'b_w_uq', 'b_w_out']
REPLICATED = ('ln_g', 'ln_b', 'kv_norm_g', 'b_q_norm_g')
MATMUL_WEIGHTS = ('a_w_in', 'a_w_out', 'kv_w_down', 'kv_w_uk', 'kv_w_uv', 'b_w_in', 'b_w_uq', 'b_w_out')
SMALL_WEIGHTS = ('a_b_in', 'a_conv_w', 'a_conv_b', 'a_norm_g', 'a_norm_b', 'a_b_out')

NN = (((1,), (0,)), ((), ()))
NT = (((1,), (1,)), ((), ()))
TN = (((0,), (0,)), ((), ()))


def _cparams(sem):
    return pltpu.CompilerParams(dimension_semantics=sem, vmem_limit_bytes=VMEM_LIMIT)


def _sigmoid(x):
    return 1.0 / (1.0 + jnp.exp(-x))


def _aligned(v, m):
    return v if isinstance(v, int) else pl.multiple_of(v, m)


def _swap_rope_halves(t):
    lane = lax.broadcasted_iota(jnp.int32, t.shape, 1)
    return jnp.where(lane < QK_ROPE_DIM // 2, pltpu.roll(t, LANES - QK_ROPE_DIM // 2, 1),
                     pltpu.roll(t, QK_ROPE_DIM // 2, 1))


def _matmul(a, b, *, name, bias=None, trans_a=False, trans_b=False, out_dtype=F32, bm=1024, bn=1024, bk=1024,
            b_blocked=False, out_blocked=False, after=None):
    if trans_a:
        kdim, m = a.shape
    else:
        m, kdim = a.shape
    if b_blocked and trans_b:
        n, k2 = b.shape[1], b.shape[0] * b.shape[2]
        bk = b.shape[2]
    elif b_blocked:
        k2, n = b.shape[1], b.shape[0] * b.shape[2]
        bn = b.shape[2]
    elif trans_b:
        n, k2 = b.shape
    else:
        k2, n = b.shape
    assert kdim == k2, (a.shape, b.shape)
    bm, bn, bk = min(bm, m), min(bn, n), min(bk, kdim)
    assert m % bm == 0 and n % bn == 0 and kdim % bk == 0, (name, m, n, kdim, bm, bn, bk)
    nk = kdim // bk
    dims = (((0 if trans_a else 1,), (1 if trans_b else 0,)), ((), ()))
    has_bias = bias is not None

    def body(*refs):
        refs = list(refs)
        a_ref, b_ref = refs[:2]
        del refs[:2]
        bias_ref = refs.pop(0) if has_bias else None
        if after is not None:
            refs.pop(0)
        o_ref = refs.pop(0)
        rest = refs
        prod = lax.dot_general(a_ref[...], b_ref[...], dims, preferred_element_type=F32)

        def finish(acc):
            if has_bias:
                acc = acc + bias_ref[...]
            o_ref[...] = acc.astype(out_dtype)

        if nk == 1:
            finish(prod)
        else:
            acc_ref = rest[0]
            k = pl.program_id(2)

            @pl.when(k == 0)
            def _():
                acc_ref[...] = prod

            @pl.when(k > 0)
            def _():
                acc_ref[...] += prod

            @pl.when(k == nk - 1)
            def _():
                finish(acc_ref[...])

    a_spec = pl.BlockSpec((bk, bm), lambda i, j, k: (k, i)) if trans_a else pl.BlockSpec((bm, bk), lambda i, j, k: (i, k))
    if b_blocked and trans_b:
        b_spec = pl.BlockSpec((None, bn, bk), lambda i, j, k: (k, j, 0))
    elif b_blocked:
        b_spec = pl.BlockSpec((None, bk, bn), lambda i, j, k: (j, k, 0))
    elif trans_b:
        b_spec = pl.BlockSpec((bn, bk), lambda i, j, k: (j, k))
    else:
        b_spec = pl.BlockSpec((bk, bn), lambda i, j, k: (k, j))
    in_specs = [a_spec, b_spec]
    args = [a, b]
    if has_bias:
        in_specs.append(pl.BlockSpec((1, bn), lambda i, j, k: (0, j)))
        args.append(bias)
    if after is not None:
        in_specs.append(pl.BlockSpec(memory_space=pl.ANY))
        args.append(after)
    if out_blocked:
        out_struct = jax.ShapeDtypeStruct((n // bn, m, bn), out_dtype)
        out_spec = pl.BlockSpec((None, bm, bn), lambda i, j, k: (j, i, 0))
    else:
        out_struct = jax.ShapeDtypeStruct((m, n), out_dtype)
        out_spec = pl.BlockSpec((bm, bn), lambda i, j, k: (i, j))
    return pl.pallas_call(
        body, name=name,
        out_shape=out_struct,
        grid=(m // bm, n // bn, nk),
        in_specs=in_specs,
        out_specs=out_spec,
        scratch_shapes=[pltpu.VMEM((bm, bn), F32)] if nk > 1 else [],
        compiler_params=_cparams(("parallel", "parallel", "arbitrary")),
    )(*args)


def _rope_tables(pos_col, freq_row):
    t = pos_col.shape[0]
    tm = min(t, 512)

    def body(pos_ref, f_ref, c_ref, s_ref):
        ang = pos_ref[...].astype(F32) * f_ref[...]
        lane = lax.broadcasted_iota(jnp.int32, ang.shape, 1)
        cos = jnp.cos(ang)
        sin = jnp.sin(ang)
        c_ref[...] = jnp.where(lane < QK_ROPE_DIM, cos, 0.0)
        s_ref[...] = jnp.where(lane < QK_ROPE_DIM // 2, -sin, jnp.where(lane < QK_ROPE_DIM, sin, 0.0))

    return pl.pallas_call(
        body, name="rope_tables",
        out_shape=(jax.ShapeDtypeStruct((t, LANES), F32), jax.ShapeDtypeStruct((t, LANES), F32)),
        grid=(t // tm,),
        in_specs=[pl.BlockSpec((tm, 1), lambda i: (i, 0)), pl.BlockSpec((1, LANES), lambda i: (0, 0))],
        out_specs=(pl.BlockSpec((tm, LANES), lambda i: (i, 0)), pl.BlockSpec((tm, LANES), lambda i: (i, 0))),
        compiler_params=_cparams(("parallel",)),
    )(pos_col, freq_row)


def _conv_mid_fwd(proj, cw, cb, ng, nb):
    t = proj.shape[0]
    e = CONV_WIDTH
    tm = min(t, 128)
    hb = tm // HALO
    nchunk = e // LANES

    def body(val_ref, gg_ref, z_ref, valh_ref, ggh_ref, cw_ref, cb_ref, ng_ref, nb_ref, u1_ref, u2_ref, ext_ref):
        i = pl.program_id(0)
        u0 = val_ref[...] * _sigmoid(gg_ref[...])
        h0 = valh_ref[...] * _sigmoid(ggh_ref[...])
        ext_ref[0:HALO, :] = jnp.where(i > 0, h0, 0.0)
        ext_ref[HALO:, :] = u0

        def chunk(c, carry):
            col = pl.multiple_of(c * LANES, LANES)
            acc = jnp.zeros((tm, LANES), F32)
            for j in range(CONV_KERNEL):
                off = HALO - (CONV_KERNEL - 1) + j
                acc = acc + cw_ref[j:j + 1, pl.ds(col, LANES)] * ext_ref[off:off + tm, pl.ds(col, LANES)]
            u1_ref[:, pl.ds(col, LANES)] = acc + cb_ref[:, pl.ds(col, LANES)]
            return carry

        lax.fori_loop(0, nchunk, chunk, 0)
        u1 = u1_ref[...]
        mu = jnp.mean(u1, axis=-1, keepdims=True)
        xc = u1 - mu
        var = jnp.mean(xc * xc, axis=-1, keepdims=True)
        n = xc * lax.rsqrt(var + LN_EPS) * ng_ref[...] + nb_ref[...]
        z = z_ref[...]
        u2_ref[...] = ((n * _sigmoid(n)) * (z * _sigmoid(z))).astype(BF16)

    row = lambda c: pl.BlockSpec((tm, e), lambda i: (i, c))
    halo = lambda c: pl.BlockSpec((HALO, e), lambda i: (jnp.maximum(i * hb - 1, 0), c))
    par = lambda r: pl.BlockSpec((r, e), lambda i: (0, 0))
    return pl.pallas_call(
        body, name="conv_mid_fwd",
        out_shape=(jax.ShapeDtypeStruct((t, e), F32), jax.ShapeDtypeStruct((t, e), BF16)),
        grid=(t // tm,),
        in_specs=[row(0), row(1), row(2), halo(0), halo(1), par(HALO), par(1), par(1), par(1)],
        out_specs=(pl.BlockSpec((tm, e), lambda i: (i, 0)), pl.BlockSpec((tm, e), lambda i: (i, 0))),
        scratch_shapes=[pltpu.VMEM((tm + HALO, e), F32)],
        compiler_params=_cparams(("parallel",)),
    )(proj, proj, proj, proj, proj, cw, cb, ng, nb)


def _ln_res_fwd(x, y, g, b):
    t, d = x.shape
    tm = min(t, 256)

    def body(x_ref, y_ref, g_ref, b_ref, h_ref, hb_ref, xh_ref, rs_ref):
        r = ALPHA * x_ref[...] + y_ref[...]
        mu = jnp.mean(r, axis=-1, keepdims=True)
        xc = r - mu
        var = jnp.mean(xc * xc, axis=-1, keepdims=True)
        rstd = lax.rsqrt(var + LN_EPS)
        xh = xc * rstd
        h = xh * g_ref[...] + b_ref[...]
        h_ref[...] = h
        hb_ref[...] = h.astype(BF16)
        xh_ref[...] = xh
        rs_ref[...] = rstd

    row = pl.BlockSpec((tm, d), lambda i: (i, 0))
    par = pl.BlockSpec((1, d), lambda i: (0, 0))
    return pl.pallas_call(
        body, name="ln0_fwd",
        out_shape=(jax.ShapeDtypeStruct((t, d), F32), jax.ShapeDtypeStruct((t, d), BF16),
                   jax.ShapeDtypeStruct((t, d), F32), jax.ShapeDtypeStruct((t, 1), F32)),
        grid=(t // tm,),
        in_specs=[row, row, par, par],
        out_specs=(row, row, row, pl.BlockSpec((tm, 1), lambda i: (i, 0))),
        compiler_params=_cparams(("parallel",)),
    )(x, y, g, b)


def _kv_mid_fwd(ckv, g, rc, rs):
    t = ckv.shape[0]
    tm = min(t, 512)
    r = KV_LORA_RANK

    def body(ckv_ref, g_ref, c_ref, s_ref, ckn_ref, kr_ref):
        c = ckv_ref[:, 0:r]
        ms = jnp.mean(c * c, axis=-1, keepdims=True)
        ckn_ref[...] = (c * lax.rsqrt(ms + RMS_EPS) * g_ref[...]).astype(BF16)
        tr = ckv_ref[:, r:r + LANES]
        kr_ref[...] = (tr * c_ref[...] + _swap_rope_halves(tr) * s_ref[...]).astype(BF16)

    return pl.pallas_call(
        body, name="kv_mid_fwd",
        out_shape=(jax.ShapeDtypeStruct((t, r), BF16), jax.ShapeDtypeStruct((t, LANES), BF16)),
        grid=(t // tm,),
        in_specs=[pl.BlockSpec((tm, r + LANES), lambda i: (i, 0)), pl.BlockSpec((1, r), lambda i: (0, 0)),
                  pl.BlockSpec((tm, LANES), lambda i: (i, 0)), pl.BlockSpec((tm, LANES), lambda i: (i, 0))],
        out_specs=(pl.BlockSpec((tm, r), lambda i: (i, 0)), pl.BlockSpec((tm, LANES), lambda i: (i, 0))),
        compiler_params=_cparams(("parallel",)),
    )(ckv, g, rc, rs)


def _q_norm_fwd(projb, g):
    t = projb.shape[0]
    tm = min(t, 512)
    r = Q_LORA_RANK

    def body(c_ref, g_ref, o_ref):
        c = c_ref[...]
        ms = jnp.mean(c * c, axis=-1, keepdims=True)
        o_ref[...] = (c * lax.rsqrt(ms + RMS_EPS) * g_ref[...]).astype(BF16)

    return pl.pallas_call(
        body, name="q_norm_fwd",
        out_shape=jax.ShapeDtypeStruct((t, r), BF16),
        grid=(t // tm,),
        in_specs=[pl.BlockSpec((tm, r), lambda i: (i, 0)), pl.BlockSpec((1, r), lambda i: (0, 0))],
        out_specs=pl.BlockSpec((tm, r), lambda i: (i, 0)),
        compiler_params=_cparams(("parallel",)),
    )(projb, g)


def _q_rope_fwd(q2, rc, rs):
    t = q2.shape[0]
    tm = min(t, 256)
    w = N_HEADS * HEAD_PAD

    def body(q_ref, c_ref, s_ref, o_ref):
        c = c_ref[...]
        s = s_ref[...]
        for h in range(N_HEADS):
            a = h * HEAD_PAD
            o_ref[:, a:a + LANES] = q_ref[:, a:a + LANES].astype(BF16)
            tr = q_ref[:, a + LANES:a + 2 * LANES]
            o_ref[:, a + LANES:a + 2 * LANES] = (tr * c + _swap_rope_halves(tr) * s).astype(BF16)

    return pl.pallas_call(
        body, name="q_rope_fwd",
        out_shape=jax.ShapeDtypeStruct((t, w), BF16),
        grid=(t // tm,),
        in_specs=[pl.BlockSpec((tm, w), lambda i: (i, 0)), pl.BlockSpec((tm, LANES), lambda i: (i, 0)),
                  pl.BlockSpec((tm, LANES), lambda i: (i, 0))],
        out_specs=pl.BlockSpec((tm, w), lambda i: (i, 0)),
        compiler_params=_cparams(("parallel",)),
    )(q2, rc, rs)


def _flash_fwd(qcat, kv, kr):
    t = qcat.shape[0]
    tq = min(t, 512)
    nq = t // tq
    exp2_scale = ATTN_SCALE * math.log2(math.e)

    def body(q_ref, kv_ref, kr_ref, o_ref, lse_ref, kcat_ref):
        kcat_ref[:, 0:LANES] = kv_ref[:, 0:LANES]
        kcat_ref[:, LANES:] = kr_ref[...]
        rows = lax.broadcasted_iota(jnp.int32, (tq, tq), 0)
        cols = lax.broadcasted_iota(jnp.int32, (tq, tq), 1)
        for i in range(nq):
            a = i * tq
            q = q_ref[a:a + tq, :]
            sd = lax.dot_general(q, kcat_ref[a:a + tq, :], NT, preferred_element_type=F32)
            sd = jnp.where(cols <= rows, sd, MASK_VALUE)
            m = jnp.max(sd, axis=1, keepdims=True)
            if i > 0:
                sp = lax.dot_general(q, kcat_ref[0:a, :], NT, preferred_element_type=F32)
                m = jnp.maximum(m, jnp.max(sp, axis=1, keepdims=True))
            pd = jnp.exp2((sd - m) * exp2_scale)
            l = jnp.sum(pd, axis=1, keepdims=True)
            acc = lax.dot_general(pd.astype(BF16), kv_ref[a:a + tq, LANES:], NN, preferred_element_type=F32)
            if i > 0:
                pp = jnp.exp2((sp - m) * exp2_scale)
                l = l + jnp.sum(pp, axis=1, keepdims=True)
                acc = acc + lax.dot_general(pp.astype(BF16), kv_ref[0:a, LANES:], NN, preferred_element_type=F32)
            o_ref[a:a + tq, :] = acc / l
            lse_ref[a:a + tq, :] = jnp.broadcast_to(m * ATTN_SCALE + jnp.log(l), (tq, LANES))

    hw = N_HEADS * LANES
    head256 = pl.BlockSpec((t, HEAD_PAD), lambda h: (0, h))
    head128 = pl.BlockSpec((t, LANES), lambda h: (0, h))
    return pl.pallas_call(
        body, name="flash_fwd",
        out_shape=(jax.ShapeDtypeStruct((t, hw), F32), jax.ShapeDtypeStruct((t, hw), F32)),
        grid=(N_HEADS,),
        in_specs=[head256, head256, pl.BlockSpec((t, LANES), lambda h: (0, 0), pipeline_mode=pl.Buffered(1))],
        out_specs=(head128, head128),
        scratch_shapes=[pltpu.VMEM((t, HEAD_PAD), BF16)],
        compiler_params=_cparams(("parallel",)),
    )(qcat, kv, kr)


def _gate_fwd(o, projb):
    t, hw = o.shape
    tm = min(t, 512)
    bw = Q_LORA_RANK
    nb = hw // bw

    def body(o_ref, z_ref, o2_ref):
        z = z_ref[...]
        o2_ref[...] = (o_ref[...] * (z * _sigmoid(z))).astype(BF16)

    return pl.pallas_call(
        body, name="gate_fwd",
        out_shape=jax.ShapeDtypeStruct((t, hw), BF16),
        grid=(t // tm, nb),
        in_specs=[pl.BlockSpec((tm, bw), lambda i, j: (i, j)), pl.BlockSpec((tm, bw), lambda i, j: (i, j + 1))],
        out_specs=pl.BlockSpec((tm, bw), lambda i, j: (i, j)),
        compiler_params=_cparams(("parallel", "parallel")),
    )(o, projb)


def _final_ln_loss(h1, yb, g, b, target):
    t, d = h1.shape
    tm = min(t, 256)

    def body(h_ref, y_ref, g_ref, b_ref, t_ref, loss_ref, dr_ref, drb_ref, dg_ref, db_ref):
        i = pl.program_id(0)
        r = ALPHA * h_ref[...] + y_ref[...]
        mu = jnp.mean(r, axis=-1, keepdims=True)
        xc = r - mu
        var = jnp.mean(xc * xc, axis=-1, keepdims=True)
        rstd = lax.rsqrt(var + LN_EPS)
        xh = xc * rstd
        g = g_ref[...]
        diff = xh * g + b_ref[...] - t_ref[...]
        part = 0.5 * jnp.sum(jnp.mean(diff * diff, axis=-1, keepdims=True))
        dh = diff * (1.0 / d)
        dgp = jnp.sum(dh * xh, axis=0, keepdims=True)
        dbp = jnp.sum(dh, axis=0, keepdims=True)

        @pl.when(i == 0)
        def _():
            loss_ref[...] = jnp.zeros_like(loss_ref)
            dg_ref[...] = jnp.zeros_like(dg_ref)
            db_ref[...] = jnp.zeros_like(db_ref)

        loss_ref[...] += jnp.full(loss_ref.shape, part, F32)
        dg_ref[...] += dgp
        db_ref[...] += dbp
        dxh = dh * g
        dr = rstd * (dxh - jnp.mean(dxh, axis=-1, keepdims=True) - xh * jnp.mean(dxh * xh, axis=-1, keepdims=True))
        dr_ref[...] = dr
        drb_ref[...] = dr.astype(BF16)

    row = pl.BlockSpec((tm, d), lambda i: (i, 0))
    par = pl.BlockSpec((1, d), lambda i: (0, 0))
    return pl.pallas_call(
        body, name="final_ln_loss",
        out_shape=(jax.ShapeDtypeStruct((1, LANES), F32), jax.ShapeDtypeStruct((t, d), F32),
                   jax.ShapeDtypeStruct((t, d), BF16), jax.ShapeDtypeStruct((1, d), F32),
                   jax.ShapeDtypeStruct((1, d), F32)),
        grid=(t // tm,),
        in_specs=[row, row, par, par, row],
        out_specs=(pl.BlockSpec((1, LANES), lambda i: (0, 0)), row, row, par, par),
        compiler_params=_cparams(("arbitrary",)),
    )(h1, yb, g, b, target)


def _gate_bwd(do2, o, projb):
    t, hw = o.shape
    tm = min(t, 512)
    bw = Q_LORA_RANK
    nb = hw // bw
    wb = projb.shape[1]

    def body(d_ref, o_ref, z_ref, dob_ref, dz_ref, dl_ref):
        z = z_ref[...]
        sg = _sigmoid(z)
        d2 = d_ref[...]
        o = o_ref[...]
        do = d2 * (z * sg)
        dob_ref[...] = do.astype(BF16)
        dz_ref[...] = (d2 * o * (sg * (1.0 + z * (1.0 - sg)))).astype(BF16)
        prod = do * o
        for hh in range(bw // LANES):
            dsum = jnp.sum(prod[:, hh * LANES:(hh + 1) * LANES], axis=1, keepdims=True)
            dl_ref[:, hh * LANES:(hh + 1) * LANES] = jnp.broadcast_to(dsum, (tm, LANES))

    blk = pl.BlockSpec((tm, bw), lambda i, j: (i, j))
    blk1 = pl.BlockSpec((tm, bw), lambda i, j: (i, j + 1))
    return pl.pallas_call(
        body, name="gate_bwd",
        out_shape=(jax.ShapeDtypeStruct((t, hw), BF16), jax.ShapeDtypeStruct((t, wb), BF16),
                   jax.ShapeDtypeStruct((t, hw), F32)),
        grid=(t // tm, nb),
        in_specs=[blk, blk, blk1],
        out_specs=(blk, blk1, blk),
        compiler_params=_cparams(("parallel", "parallel")),
    )(do2, o, projb)


def _flash_bwd(qcat, kv, kr, dob, lse, dl, rc, rs):
    t = qcat.shape[0]
    tq = min(t, 512)
    nq = t // tq
    once = pl.Buffered(1)

    def body(q_ref, kv_ref, kr_ref, do_ref, lse_ref, dl_ref, c_ref, s_ref,
             dq_ref, dkv_ref, dkr_ref, kcat_ref, dqa_ref, dka_ref, dva_ref):
        h = pl.program_id(0)
        kcat_ref[:, 0:LANES] = kv_ref[:, 0:LANES]
        kcat_ref[:, LANES:] = kr_ref[...]
        dqa_ref[...] = jnp.zeros_like(dqa_ref)

        @pl.when(h == 0)
        def _():
            dkr_ref[...] = jnp.zeros_like(dkr_ref)

        def block(qb, kb, masked):
            qs = _aligned(qb * tq, tq)
            ks = _aligned(kb * tq, tq)
            q = q_ref[pl.ds(qs, tq), :]
            kc = kcat_ref[pl.ds(ks, tq), :]
            s = lax.dot_general(q, kc, NT, preferred_element_type=F32) * ATTN_SCALE
            if masked:
                rows = lax.broadcasted_iota(jnp.int32, (tq, tq), 0)
                cols = lax.broadcasted_iota(jnp.int32, (tq, tq), 1)
                s = jnp.where(cols <= rows, s, MASK_VALUE)
            lse_q = lse_ref[pl.ds(qs, tq), :][:, 0:1]
            d_q = dl_ref[pl.ds(qs, tq), :][:, 0:1]
            p = jnp.exp(s - lse_q)
            do = do_ref[pl.ds(qs, tq), :]
            v = kv_ref[pl.ds(ks, tq), LANES:]
            dp = lax.dot_general(do, v, NT, preferred_element_type=F32)
            ds = (p * (dp - d_q) * ATTN_SCALE).astype(BF16)
            dva_ref[...] += lax.dot_general(p.astype(BF16), do, TN, preferred_element_type=F32)
            dka_ref[...] += lax.dot_general(ds, q, TN, preferred_element_type=F32)
            dqa_ref[pl.ds(qs, tq), :] += lax.dot_general(ds, kc, NN, preferred_element_type=F32)

        for kb in range(nq):
            dka_ref[...] = jnp.zeros_like(dka_ref)
            dva_ref[...] = jnp.zeros_like(dva_ref)
            block(kb, kb, True)
            if kb + 1 < nq:
                def loop_body(qb, carry, kb=kb):
                    block(qb, kb, False)
                    return carry
                lax.fori_loop(kb + 1, nq, loop_body, 0)
            a = kb * tq
            dk = dka_ref[...]
            dkv_ref[a:a + tq, 0:LANES] = dk[:, 0:LANES].astype(BF16)
            dkv_ref[a:a + tq, LANES:] = dva_ref[...].astype(BF16)
            dkr_ref[a:a + tq, :] += dk[:, LANES:]

        dq = dqa_ref[...]
        dq_ref[:, 0:LANES] = dq[:, 0:LANES].astype(BF16)
        d2 = dq[:, LANES:]
        dq_ref[:, LANES:] = (d2 * c_ref[...] - _swap_rope_halves(d2) * s_ref[...]).astype(BF16)

    hp = N_HEADS * HEAD_PAD
    head256 = pl.BlockSpec((t, HEAD_PAD), lambda h: (0, h))
    head128 = pl.BlockSpec((t, LANES), lambda h: (0, h))
    const128 = pl.BlockSpec((t, LANES), lambda h: (0, 0), pipeline_mode=once)
    return pl.pallas_call(
        body, name="flash_bwd",
        out_shape=(jax.ShapeDtypeStruct((t, hp), BF16), jax.ShapeDtypeStruct((t, hp), BF16),
                   jax.ShapeDtypeStruct((t, LANES), F32)),
        grid=(N_HEADS,),
        in_specs=[head256, head256, const128, head128, head128, head128, const128, const128],
        out_specs=(head256, head256, pl.BlockSpec((t, LANES), lambda h: (0, 0))),
        scratch_shapes=[pltpu.VMEM((t, HEAD_PAD), BF16), pltpu.VMEM((t, HEAD_PAD), F32),
                        pltpu.VMEM((tq, HEAD_PAD), F32), pltpu.VMEM((tq, LANES), F32)],
        compiler_params=_cparams(("arbitrary",)),
    )(qcat, kv, kr, dob, lse, dl, rc, rs)


def _q_norm_bwd(dcqn, projb, g, dprojb):
    t = projb.shape[0]
    tm = min(t, 512)
    r = Q_LORA_RANK

    def body(d_ref, c_ref, g_ref, alias_ref, o_ref, dg_ref):
        i = pl.program_id(0)
        c = c_ref[...]
        d = d_ref[...]
        rr = lax.rsqrt(jnp.mean(c * c, axis=-1, keepdims=True) + RMS_EPS)
        xh = c * rr

        @pl.when(i == 0)
        def _():
            dg_ref[...] = jnp.zeros_like(dg_ref)

        dg_ref[...] += jnp.sum(d * xh, axis=0, keepdims=True)
        dxh = d * g_ref[...]
        o_ref[...] = (rr * (dxh - xh * jnp.mean(dxh * xh, axis=-1, keepdims=True))).astype(BF16)

    blk = pl.BlockSpec((tm, r), lambda i: (i, 0))
    par = pl.BlockSpec((1, r), lambda i: (0, 0))
    return pl.pallas_call(
        body, name="q_norm_bwd",
        out_shape=(jax.ShapeDtypeStruct(dprojb.shape, BF16), jax.ShapeDtypeStruct((1, r), F32)),
        grid=(t // tm,),
        in_specs=[blk, blk, par, pl.BlockSpec(memory_space=pl.ANY)],
        out_specs=(blk, par),
        input_output_aliases={3: 0},
        compiler_params=_cparams(("arbitrary",)),
    )(dcqn, projb, g, dprojb)


def _kv_mid_bwd(dckn, ckv, dkr, g, rc, rs):
    t = ckv.shape[0]
    tm = min(t, 512)
    r = KV_LORA_RANK

    def body(d_ref, ckv_ref, dkr_ref, g_ref, c_ref, s_ref, o_ref, dg_ref):
        i = pl.program_id(0)
        c = ckv_ref[:, 0:r]
        d = d_ref[...]
        rr = lax.rsqrt(jnp.mean(c * c, axis=-1, keepdims=True) + RMS_EPS)
        xh = c * rr

        @pl.when(i == 0)
        def _():
            dg_ref[...] = jnp.zeros_like(dg_ref)

        dg_ref[...] += jnp.sum(d * xh, axis=0, keepdims=True)
        dxh = d * g_ref[...]
        o_ref[:, 0:r] = (rr * (dxh - xh * jnp.mean(dxh * xh, axis=-1, keepdims=True))).astype(BF16)
        dk = dkr_ref[...]
        o_ref[:, r:] = (dk * c_ref[...] - _swap_rope_halves(dk) * s_ref[...]).astype(BF16)

    return pl.pallas_call(
        body, name="kv_mid_bwd",
        out_shape=(jax.ShapeDtypeStruct((t, r + LANES), BF16), jax.ShapeDtypeStruct((1, r), F32)),
        grid=(t // tm,),
        in_specs=[pl.BlockSpec((tm, r), lambda i: (i, 0)), pl.BlockSpec((tm, r + LANES), lambda i: (i, 0)),
                  pl.BlockSpec((tm, LANES), lambda i: (i, 0)), pl.BlockSpec((1, r), lambda i: (0, 0)),
                  pl.BlockSpec((tm, LANES), lambda i: (i, 0)), pl.BlockSpec((tm, LANES), lambda i: (i, 0))],
        out_specs=(pl.BlockSpec((tm, r + LANES), lambda i: (i, 0)), pl.BlockSpec((1, r), lambda i: (0, 0))),
        compiler_params=_cparams(("arbitrary",)),
    )(dckn, ckv, dkr, g, rc, rs)


def _ln0_bwd(dr2, t1, t2, xh, rstd, g):
    t, d = dr2.shape
    tm = min(t, 256)

    def body(a_ref, b_ref, c_ref, xh_ref, rs_ref, g_ref, dr_ref, drb_ref, dg_ref, db_ref, dsum_ref):
        i = pl.program_id(0)
        dh = ALPHA * a_ref[...] + b_ref[...] + c_ref[...]
        xh = xh_ref[...]

        @pl.when(i == 0)
        def _():
            dg_ref[...] = jnp.zeros_like(dg_ref)
            db_ref[...] = jnp.zeros_like(db_ref)
            dsum_ref[...] = jnp.zeros_like(dsum_ref)

        dg_ref[...] += jnp.sum(dh * xh, axis=0, keepdims=True)
        db_ref[...] += jnp.sum(dh, axis=0, keepdims=True)
        dxh = dh * g_ref[...]
        dr = rs_ref[...] * (dxh - jnp.mean(dxh, axis=-1, keepdims=True) - xh * jnp.mean(dxh * xh, axis=-1, keepdims=True))
        dr_ref[...] = dr
        drb_ref[...] = dr.astype(BF16)
        dsum_ref[...] += jnp.sum(dr, axis=0, keepdims=True)

    row = pl.BlockSpec((tm, d), lambda i: (i, 0))
    par = pl.BlockSpec((1, d), lambda i: (0, 0))
    return pl.pallas_call(
        body, name="ln0_bwd",
        out_shape=(jax.ShapeDtypeStruct((t, d), F32), jax.ShapeDtypeStruct((t, d), BF16),
                   jax.ShapeDtypeStruct((1, d), F32), jax.ShapeDtypeStruct((1, d), F32),
                   jax.ShapeDtypeStruct((1, d), F32)),
        grid=(t // tm,),
        in_specs=[row, row, row, row, pl.BlockSpec((tm, 1), lambda i: (i, 0)), par],
        out_specs=(row, row, par, par, par),
        compiler_params=_cparams(("arbitrary",)),
    )(dr2, t1, t2, xh, rstd, g)


def _conv_mid_bwd_rows(u1, proj, du2, ng, nb):
    t = u1.shape[0]
    e = CONV_WIDTH
    tm = min(t, 256)

    def body(u1_ref, z_ref, d_ref, ng_ref, nb_ref, du1_ref, dz_ref, dng_ref, dnb_ref, dbz_ref):
        i = pl.program_id(0)
        u1 = u1_ref[...]
        mu = jnp.mean(u1, axis=-1, keepdims=True)
        xc = u1 - mu
        rstd = lax.rsqrt(jnp.mean(xc * xc, axis=-1, keepdims=True) + LN_EPS)
        xh = xc * rstd
        g = ng_ref[...]
        n = xh * g + nb_ref[...]
        sn = _sigmoid(n)
        z = z_ref[...]
        sz = _sigmoid(z)
        du2 = d_ref[...]
        dz = du2 * (n * sn) * (sz * (1.0 + z * (1.0 - sz)))
        dn = du2 * (z * sz) * (sn * (1.0 + n * (1.0 - sn)))

        @pl.when(i == 0)
        def _():
            dng_ref[...] = jnp.zeros_like(dng_ref)
            dnb_ref[...] = jnp.zeros_like(dnb_ref)
            dbz_ref[...] = jnp.zeros_like(dbz_ref)

        dng_ref[...] += jnp.sum(dn * xh, axis=0, keepdims=True)
        dnb_ref[...] += jnp.sum(dn, axis=0, keepdims=True)
        dbz_ref[...] += jnp.sum(dz, axis=0, keepdims=True)
        dz_ref[...] = dz.astype(BF16)
        dxh = dn * g
        du1_ref[...] = rstd * (dxh - jnp.mean(dxh, axis=-1, keepdims=True) - xh * jnp.mean(dxh * xh, axis=-1, keepdims=True))

    row = pl.BlockSpec((tm, e), lambda i: (i, 0))
    par = pl.BlockSpec((1, e), lambda i: (0, 0))
    return pl.pallas_call(
        body, name="conv_mid_bwd_rows",
        out_shape=(jax.ShapeDtypeStruct((t, e), F32), jax.ShapeDtypeStruct((t, 3 * e), BF16),
                   jax.ShapeDtypeStruct((1, e), F32), jax.ShapeDtypeStruct((1, e), F32),
                   jax.ShapeDtypeStruct((1, e), F32)),
        grid=(t // tm,),
        in_specs=[row, pl.BlockSpec((tm, e), lambda i: (i, 2)), row, par, par],
        out_specs=(row, pl.BlockSpec((tm, e), lambda i: (i, 2)), par, par, par),
        compiler_params=_cparams(("arbitrary",)),
    )(u1, proj, du2, ng, nb)


def _conv_bwd(du1, proj, cw, dproj):
    t = du1.shape[0]
    e = CONV_WIDTH
    tm = min(t, 128)
    hb = tm // HALO
    nt = t // tm
    nchunk = e // LANES
    last_halo = t // HALO - 1

    def body(d_ref, dn_ref, val_ref, gg_ref, valh_ref, ggh_ref, cw_ref, alias_ref,
             dp_ref, dcw_ref, dcb_ref, dbv_ref, extu_ref, extd_ref, du0_ref, acc_ref):
        i = pl.program_id(0)
        val = val_ref[...]
        sg = _sigmoid(gg_ref[...])
        h0 = valh_ref[...] * _sigmoid(ggh_ref[...])
        extu_ref[0:HALO, :] = jnp.where(i > 0, h0, 0.0)
        extu_ref[HALO:, :] = val * sg
        extd_ref[0:tm, :] = d_ref[...]
        extd_ref[tm:, :] = jnp.where(i < nt - 1, dn_ref[...], 0.0)

        @pl.when(i == 0)
        def _():
            acc_ref[...] = jnp.zeros_like(acc_ref)
            dbv_ref[...] = jnp.zeros_like(dbv_ref)

        def chunk(c, carry):
            col = pl.multiple_of(c * LANES, LANES)
            d = extd_ref[0:tm, pl.ds(col, LANES)]
            du0 = jnp.zeros((tm, LANES), F32)
            for j in range(CONV_KERNEL):
                offd = CONV_KERNEL - 1 - j
                du0 = du0 + cw_ref[j:j + 1, pl.ds(col, LANES)] * extd_ref[offd:offd + tm, pl.ds(col, LANES)]
                offu = HALO - (CONV_KERNEL - 1) + j
                prod = d * extu_ref[offu:offu + tm, pl.ds(col, LANES)]
                acc_ref[j * 8:(j + 1) * 8, pl.ds(col, LANES)] += jnp.sum(prod.reshape(tm // 8, 8, LANES), axis=0)
            acc_ref[CONV_KERNEL * 8:(CONV_KERNEL + 1) * 8, pl.ds(col, LANES)] += jnp.sum(d.reshape(tm // 8, 8, LANES), axis=0)
            du0_ref[:, pl.ds(col, LANES)] = du0
            return carry

        lax.fori_loop(0, nchunk, chunk, 0)
        du0 = du0_ref[...]
        dval = du0 * sg
        dgg = du0 * val * sg * (1.0 - sg)
        dp_ref[:, 0:e] = dval.astype(BF16)
        dp_ref[:, e:] = dgg.astype(BF16)
        dbv_ref[:, 0:e] += jnp.sum(dval, axis=0, keepdims=True)
        dbv_ref[:, e:] += jnp.sum(dgg, axis=0, keepdims=True)

        @pl.when(i == nt - 1)
        def _():
            for j in range(CONV_KERNEL):
                dcw_ref[j:j + 1, :] = jnp.sum(acc_ref[j * 8:(j + 1) * 8, :], axis=0, keepdims=True)
            dcw_ref[CONV_KERNEL:, :] = jnp.zeros((HALO - CONV_KERNEL, e), F32)
            dcb_ref[...] = jnp.sum(acc_ref[CONV_KERNEL * 8:(CONV_KERNEL + 1) * 8, :], axis=0, keepdims=True)

    row = lambda c: pl.BlockSpec((tm, e), lambda i: (i, c))
    halo_prev = lambda c: pl.BlockSpec((HALO, e), lambda i: (jnp.maximum(i * hb - 1, 0), c))
    halo_next = pl.BlockSpec((HALO, e), lambda i: (jnp.minimum((i + 1) * hb, last_halo), 0))
    return pl.pallas_call(
        body, name="conv_bwd",
        out_shape=(jax.ShapeDtypeStruct(dproj.shape, BF16), jax.ShapeDtypeStruct((HALO, e), F32),
                   jax.ShapeDtypeStruct((1, e), F32), jax.ShapeDtypeStruct((1, 2 * e), F32)),
        grid=(nt,),
        in_specs=[row(0), halo_next, row(0), row(1), halo_prev(0), halo_prev(1),
                  pl.BlockSpec((HALO, e), lambda i: (0, 0)), pl.BlockSpec(memory_space=pl.ANY)],
        out_specs=(pl.BlockSpec((tm, 2 * e), lambda i: (i, 0)), pl.BlockSpec((HALO, e), lambda i: (0, 0)),
                   pl.BlockSpec((1, e), lambda i: (0, 0)), pl.BlockSpec((1, 2 * e), lambda i: (0, 0))),
        scratch_shapes=[pltpu.VMEM((tm + HALO, e), F32), pltpu.VMEM((tm + HALO, e), F32),
                        pltpu.VMEM((tm, e), F32), pltpu.VMEM(((CONV_KERNEL + 1) * 8, e), F32)],
        input_output_aliases={7: 0},
        compiler_params=_cparams(("arbitrary",)),
    )(du1, du1, proj, proj, proj, proj, cw, dproj)


def _adam_update(g, w, m, v):
    c1 = 1.0 - ADAM_B1 ** ADAM_STEP
    c2 = 1.0 - ADAM_B2 ** ADAM_STEP
    mn = ADAM_B1 * m + (1.0 - ADAM_B1) * g
    vn = ADAM_B2 * v + (1.0 - ADAM_B2) * (g * g)
    delta = -ADAM_LR * ((mn / c1) / (jnp.sqrt(vn / c2) + ADAM_EPS) + ADAM_WD * w)
    return delta, mn, vn


def _adamw(name, gbuf, w, m, v):
    r, c = w.shape
    tr = min(r, 256)
    assert r % tr == 0

    def body(g_ref, w_ref, m_ref, v_ref, go_ref, d_ref, mo_ref, vo_ref):
        g = g_ref[0].astype(F32)
        for k in range(1, N_DEV):
            g = g + g_ref[k].astype(F32)
        go_ref[...] = g
        d_ref[...], mo_ref[...], vo_ref[...] = _adam_update(g, w_ref[...], m_ref[...], v_ref[...])

    blk = pl.BlockSpec((tr, c), lambda i: (i, 0))
    shp = jax.ShapeDtypeStruct((r, c), F32)
    return pl.pallas_call(
        body, name="adamw_" + name,
        out_shape=(shp, shp, shp, shp),
        grid=(r // tr,),
        in_specs=[pl.BlockSpec((N_DEV, tr, c), lambda i: (0, i, 0)), blk, blk, blk],
        out_specs=(blk, blk, blk, blk),
        compiler_params=_cparams(("parallel",)),
    )(gbuf, w, m, v)


SMALL_TABLE_COLS = 1024
SMALL_LAYOUT = {
    'ln_g': (0, 2, 1024), 'ln_b': (8, 2, 1024), 'a_b_in': (16, 1, 768), 'a_conv_b': (24, 1, 256),
    'a_norm_g': (32, 1, 256), 'a_norm_b': (40, 1, 256), 'a_b_out': (48, 1, 128), 'kv_norm_g': (56, 1, 256),
    'b_q_norm_g': (64, 1, 512), 'a_conv_w': (72, CONV_KERNEL, 256),
}
SMALL_TABLE_ROWS = 104
SMALL_NAMES = tuple(SMALL_LAYOUT)


def _pack_small_grads(dg0, dg1, db0, db1, dbv, dbz, dcb, dng, dnb, dba, dgkv, dgq, dcw):
    e = CONV_WIDTH
    ce = e // N_DEV

    def body(dg0_ref, dg1_ref, db0_ref, db1_ref, dbv_ref, dbz_ref, dcb_ref, dng_ref, dnb_ref, dba_ref, dgkv_ref,
             dgq_ref, dcw_ref, o_ref):
        o_ref[...] = jnp.zeros_like(o_ref)
        for d in range(N_DEV):
            o_ref[d, 0:1, :] = dg0_ref[...]
            o_ref[d, 1:2, :] = dg1_ref[...]
            o_ref[d, 8:9, :] = db0_ref[...]
            o_ref[d, 9:10, :] = db1_ref[...]
            for q in range(3):
                col = d * 3 * ce + q * ce
                src = dbv_ref[:, col:col + ce] if col < 2 * e else dbz_ref[:, col - 2 * e:col - 2 * e + ce]
                o_ref[d, 16:17, q * ce:(q + 1) * ce] = src
            o_ref[d, 24:25, 0:ce] = dcb_ref[:, d * ce:(d + 1) * ce]
            o_ref[d, 32:33, 0:ce] = dng_ref[:, d * ce:(d + 1) * ce]
            o_ref[d, 40:41, 0:ce] = dnb_ref[:, d * ce:(d + 1) * ce]
            o_ref[d, 48:49, 0:LANES] = dba_ref[:, d * LANES:(d + 1) * LANES]
            o_ref[d, 56:57, 0:KV_LORA_RANK] = dgkv_ref[...]
            o_ref[d, 64:65, 0:Q_LORA_RANK] = dgq_ref[...]
            o_ref[d, 72:72 + HALO, 0:ce] = dcw_ref[:, d * ce:(d + 1) * ce]

    return pl.pallas_call(
        body, name="pack_small_grads",
        out_shape=jax.ShapeDtypeStruct((N_DEV, SMALL_TABLE_ROWS, SMALL_TABLE_COLS), F32),
        compiler_params=pltpu.CompilerParams(vmem_limit_bytes=VMEM_LIMIT),
    )(dg0, dg1, db0, db1, dbv, dbz, dcb, dng, dnb, dba, dgkv, dgq, dcw)


def _adamw_small(gtab, ws, ms, vs):
    n = len(SMALL_NAMES)

    def body(*refs):
        g_ref = refs[0]
        w_refs, m_refs, v_refs = refs[1:1 + n], refs[1 + n:1 + 2 * n], refs[1 + 2 * n:1 + 3 * n]
        outs = refs[1 + 3 * n:]
        for i, name in enumerate(SMALL_NAMES):
            r0, r, c = SMALL_LAYOUT[name]
            g = g_ref[0, r0:r0 + r, 0:c]
            for k in range(1, N_DEV):
                g = g + g_ref[k, r0:r0 + r, 0:c]
            delta, mn, vn = _adam_update(g, w_refs[i][...], m_refs[i][...], v_refs[i][...])
            outs[i][...] = g
            outs[n + i][...] = delta
            outs[2 * n + i][...] = mn
            outs[3 * n + i][...] = vn

    shapes = tuple(jax.ShapeDtypeStruct(w.shape, F32) for w in ws)
    res = pl.pallas_call(
        body, name="adamw_small",
        out_shape=shapes * 4,
        compiler_params=pltpu.CompilerParams(vmem_limit_bytes=VMEM_LIMIT),
    )(gtab, *ws, *ms, *vs)
    return res[:n], res[n:2 * n], res[2 * n:3 * n], res[3 * n:]


def _mesh_peers():
    x, y, c = lax.axis_index("x"), lax.axis_index("y"), lax.axis_index("c")
    me = 4 * x + 2 * y + c
    peers = []
    for k in range(1, N_DEV):
        px = 1 - x if (k >> 2) & 1 else x
        py = 1 - y if (k >> 1) & 1 else y
        pc = 1 - c if k & 1 else c
        peers.append(((px, py, pc), 4 * px + 2 * py + pc))
    return me, peers


def _all_gather(name, shards):
    na = len(shards)

    def body(*refs):
        srcs, dsts = refs[:na], refs[na:2 * na]
        send_sems, recv_sems, local_sems = refs[2 * na:]
        me, peers = _mesh_peers()
        local = [pltpu.make_async_copy(srcs[a], dsts[a].at[me], local_sems.at[a]) for a in range(na)]
        for cp in local:
            cp.start()
        sends = []
        for k, (dev, _) in enumerate(peers):
            for a in range(na):
                cp = pltpu.make_async_remote_copy(src_ref=srcs[a], dst_ref=dsts[a].at[me], send_sem=send_sems.at[a, k],
                                                  recv_sem=recv_sems.at[a, k], device_id=dev,
                                                  device_id_type=pl.DeviceIdType.MESH)
                cp.start()
                sends.append(cp)
        for k, (dev, pid) in enumerate(peers):
            for a in range(na):
                pltpu.make_async_remote_copy(src_ref=srcs[a], dst_ref=dsts[a].at[pid], send_sem=send_sems.at[a, k],
                                             recv_sem=recv_sems.at[a, k], device_id=dev,
                                             device_id_type=pl.DeviceIdType.MESH).wait_recv()
        for cp in sends:
            cp.wait_send()
        for cp in local:
            cp.wait()

    hbm = pl.BlockSpec(memory_space=pl.ANY)
    return pl.pallas_call(
        body, name=name,
        out_shape=tuple(jax.ShapeDtypeStruct((N_DEV,) + s.shape, s.dtype) for s in shards),
        in_specs=[hbm] * na, out_specs=(hbm,) * na,
        scratch_shapes=[pltpu.SemaphoreType.DMA((na, N_DEV - 1)), pltpu.SemaphoreType.DMA((na, N_DEV - 1)),
                        pltpu.SemaphoreType.DMA((na,))],
    )(*shards)


def _exchange(name, slots):
    na = len(slots)

    def body(*refs):
        srcs, dsts = refs[:na], refs[na:2 * na]
        send_sems, recv_sems, local_sems = refs[2 * na:]
        me, peers = _mesh_peers()
        local = [pltpu.make_async_copy(srcs[a].at[me], dsts[a].at[me], local_sems.at[a]) for a in range(na)]
        for cp in local:
            cp.start()
        sends = []
        for k, (dev, pid) in enumerate(peers):
            for a in range(na):
                cp = pltpu.make_async_remote_copy(src_ref=srcs[a].at[pid], dst_ref=dsts[a].at[me],
                                                  send_sem=send_sems.at[a, k], recv_sem=recv_sems.at[a, k],
                                                  device_id=dev, device_id_type=pl.DeviceIdType.MESH)
                cp.start()
                sends.append(cp)
        for k, (dev, pid) in enumerate(peers):
            for a in range(na):
                pltpu.make_async_remote_copy(src_ref=srcs[a].at[pid], dst_ref=dsts[a].at[pid],
                                             send_sem=send_sems.at[a, k], recv_sem=recv_sems.at[a, k],
                                             device_id=dev, device_id_type=pl.DeviceIdType.MESH).wait_recv()
        for cp in sends:
            cp.wait_send()
        for cp in local:
            cp.wait()

    hbm = pl.BlockSpec(memory_space=pl.ANY)
    return pl.pallas_call(
        body, name=name,
        out_shape=tuple(jax.ShapeDtypeStruct(s.shape, s.dtype) for s in slots),
        in_specs=[hbm] * na, out_specs=(hbm,) * na,
        scratch_shapes=[pltpu.SemaphoreType.DMA((na, N_DEV - 1)), pltpu.SemaphoreType.DMA((na, N_DEV - 1)),
                        pltpu.SemaphoreType.DMA((na,))],
    )(*slots)


_HBM_SPEC = pl.BlockSpec(memory_space=pltpu.HBM)
_SEM_SPEC = pl.BlockSpec(memory_space=pltpu.SEMAPHORE)


def _split_copies(srcs, lands, send_sems, recv_sems, local_sems, gather):
    me, peers = _mesh_peers()
    na = len(srcs)
    mine = lambda a, slot: srcs[a] if gather else srcs[a].at[slot]
    local = [pltpu.make_async_copy(mine(a, me), lands[a].at[me], local_sems.at[a]) for a in range(na)]
    sent, received = [], []
    for k, (dev, pid) in enumerate(peers):
        for a in range(na):
            s = a * (N_DEV - 1) + k
            sent.append(pltpu.make_async_remote_copy(
                src_ref=mine(a, pid), dst_ref=lands[a].at[me], send_sem=send_sems.at[s],
                recv_sem=recv_sems.at[s], device_id=dev, device_id_type=pl.DeviceIdType.MESH))
            received.append(pltpu.make_async_remote_copy(
                src_ref=mine(a, pid), dst_ref=lands[a].at[pid], send_sem=send_sems.at[s],
                recv_sem=recv_sems.at[s], device_id=dev, device_id_type=pl.DeviceIdType.MESH))
    return local, sent, received


def _comm_start(name, srcs, gather, after):
    na = len(srcs)
    land_structs = [jax.ShapeDtypeStruct(((N_DEV,) + s.shape) if gather else s.shape, s.dtype) for s in srcs]
    extra = [] if after is None else [after]
    n_in = 2 * na + len(extra)

    def body(*refs):
        srcs_r, lands_r = refs[:na], refs[na:2 * na]
        send_sems, recv_sems, local_sems = refs[n_in:n_in + 3]
        token = refs[-1]
        local, sent, _ = _split_copies(srcs_r, lands_r, send_sems, recv_sems, local_sems, gather)
        for cp in local + sent:
            cp.start()
        token[...] = jnp.zeros_like(token)

    sem = pltpu.SemaphoreType.DMA((na * (N_DEV - 1),))
    outs = pl.pallas_call(
        body, name=name,
        out_shape=(sem, sem, pltpu.SemaphoreType.DMA((na,)),
                   *[pltpu.HBM(s.shape, s.dtype) for s in srcs],
                   *[pltpu.HBM(s.shape, s.dtype) for s in land_structs],
                   jax.ShapeDtypeStruct((8, LANES), F32)),
        in_specs=[_HBM_SPEC] * (2 * na) + [pl.BlockSpec(memory_space=pl.ANY)] * len(extra),
        out_specs=(_SEM_SPEC, _SEM_SPEC, _SEM_SPEC, *([_HBM_SPEC] * (2 * na)), pl.BlockSpec(memory_space=pltpu.VMEM)),
        input_output_aliases={i: 3 + i for i in range(2 * na)},
        compiler_params=pltpu.CompilerParams(has_side_effects=pltpu.SideEffectType.DATAFLOW_SIDE_EFFECTING),
    )(*[pltpu.with_memory_space_constraint(s, pltpu.HBM) for s in srcs],
      *[pltpu.with_memory_space_constraint(lax.empty(s.shape, s.dtype), pltpu.HBM) for s in land_structs],
      *extra)
    return (outs[:3], outs[3:3 + na], outs[3 + na:3 + 2 * na]), outs[-1]


def _comm_wait(name, handles, gather, after):
    (send_sems, recv_sems, local_sems), srcs, lands = handles
    na = len(srcs)

    def body(*refs):
        srcs_r, lands_r = refs[:na], refs[na:2 * na]
        send_r, recv_r, local_r = refs[2 * na:2 * na + 3]
        local, sent, received = _split_copies(srcs_r, lands_r, send_r, recv_r, local_r, gather)
        for cp in sent:
            cp.wait_send()
        for cp in received:
            cp.wait_recv()
        for cp in local:
            cp.wait()

    outs = pl.pallas_call(
        body, name=name,
        out_shape=(*[pltpu.HBM(s.shape, s.dtype) for s in srcs], *[pltpu.HBM(s.shape, s.dtype) for s in lands]),
        in_specs=[_HBM_SPEC] * (2 * na) + [_SEM_SPEC] * 3 + [pl.BlockSpec(memory_space=pl.ANY)],
        out_specs=tuple([_HBM_SPEC] * (2 * na)),
        input_output_aliases={i: i for i in range(2 * na)},
        compiler_params=pltpu.CompilerParams(has_side_effects=pltpu.SideEffectType.DATAFLOW_SIDE_EFFECTING),
    )(*srcs, *lands, send_sems, recv_sems, local_sems, after)
    return outs[na:]


def _prep_weights(w):
    e = CONV_WIDTH
    p = {}
    if 'a_w_in' in w:
        if w['a_w_in'].shape == (N_DEV, D_MODEL, 3 * e // N_DEV):
            p['a_w_in3'] = w['a_w_in']
        else:
            p['a_w_in3'] = w['a_w_in'].reshape(D_MODEL, N_DEV, 3 * e // N_DEV).transpose(1, 0, 2)
        p['a_b_in'] = w['a_b_in'].reshape(1, 3 * e)
        p['a_conv_w'] = jnp.pad(w['a_conv_w'].reshape(CONV_KERNEL, e), ((0, HALO - CONV_KERNEL), (0, 0)))
        p['a_conv_b'] = w['a_conv_b'].reshape(1, e)
        p['a_norm_g'] = w['a_norm_g'].reshape(1, e)
        p['a_norm_b'] = w['a_norm_b'].reshape(1, e)
        p['a_b_out'] = w['a_b_out'].reshape(1, D_MODEL)
        p['kv_norm_g'] = w['kv_norm_g'].reshape(1, KV_LORA_RANK)
        p['b_q_norm_g'] = w['b_q_norm_g'].reshape(1, Q_LORA_RANK)
        p['ln_g'] = w['ln_g']
        p['ln_b'] = w['ln_b']
    if 'a_w_out' in w:
        p['a_w_out'] = w['a_w_out'].reshape(e, D_MODEL)
        p['kv_w_down'] = jnp.pad(w['kv_w_down'], ((0, 0), (0, KV_LORA_RANK + LANES - w['kv_w_down'].shape[1])))
        p['kv_w_ukv'] = jnp.concatenate([w['kv_w_uk'], w['kv_w_uv']], axis=2).reshape(KV_LORA_RANK, N_HEADS * HEAD_PAD)
        p['b_w_in'] = w['b_w_in'].reshape(D_MODEL, -1)
        uq = w['b_w_uq'].reshape(Q_LORA_RANK, N_HEADS, QK_NOPE_DIM + QK_ROPE_DIM)
        p['b_w_uq'] = jnp.pad(uq, ((0, 0), (0, 0), (0, HEAD_PAD - uq.shape[2]))).reshape(Q_LORA_RANK, N_HEADS * HEAD_PAD)
        p['b_w_out'] = w['b_w_out'].reshape(N_HEADS * V_HEAD_DIM, D_MODEL)
    return p


class _NoComm:
    def __init__(self):
        self.grads = {}
        self.small = None

    def first_after(self):
        return None

    def rest_weights(self, after):
        return {}

    def send_group1(self, g):
        self.grads.update(g)
        return None

    def send_group2(self, g, small):
        self.grads.update(g)
        self.small = small
        return None


def _local_step(x, positions, target, p, comm):
    t = x.shape[0]
    e = CONV_WIDTH
    freqs = ROPE_THETA ** (-jnp.arange(0, QK_ROPE_DIM, 2, dtype=F32) / QK_ROPE_DIM)
    freq_row = jnp.concatenate([freqs, freqs, jnp.zeros((LANES - QK_ROPE_DIM,), F32)]).reshape(1, LANES)
    rc, rs = _rope_tables(positions.reshape(t, 1), freq_row)
    xb = x.astype(BF16)
    ln_g0, ln_b0 = p['ln_g'][0:1], p['ln_b'][0:1]
    ln_g1, ln_b1 = p['ln_g'][1:2], p['ln_b'][1:2]

    proj = _matmul(xb, p['a_w_in3'], bias=p['a_b_in'], name="a_in", b_blocked=True, after=comm.first_after())
    u1, u2 = _conv_mid_fwd(proj, p['a_conv_w'], p['a_conv_b'], p['a_norm_g'], p['a_norm_b'])
    p = {**p, **comm.rest_weights(u2)}
    y = _matmul(u2, p['a_w_out'], bias=p['a_b_out'], name="a_out", bk=2048)
    h1, h1b, xh1, rstd1 = _ln_res_fwd(x, y, ln_g0, ln_b0)
    ckv = _matmul(h1b, p['kv_w_down'], name="kv_down")
    ckn, kr = _kv_mid_fwd(ckv, p['kv_norm_g'], rc, rs)
    kv = _matmul(ckn, p['kv_w_ukv'], name="kv_up", out_dtype=BF16)
    projb = _matmul(h1b, p['b_w_in'], name="b_in", bn=1280)
    cqn = _q_norm_fwd(projb, p['b_q_norm_g'])
    q2 = _matmul(cqn, p['b_w_uq'], name="q_up")
    qcat = _q_rope_fwd(q2, rc, rs)
    o, lse = _flash_fwd(qcat, kv, kr)
    o2 = _gate_fwd(o, projb)
    yb = _matmul(o2, p['b_w_out'], name="b_out", bk=2048)
    loss, dr2, dr2b, dg1, db1 = _final_ln_loss(h1, yb, ln_g1, ln_b1, target)

    g = {}
    g['b_w_out'] = _matmul(o2, dr2b, trans_a=True, name="d_b_out", out_dtype=BF16)
    do2 = _matmul(dr2b, p['b_w_out'], trans_b=True, name="d_o2")
    dob, dprojb, dl = _gate_bwd(do2, o, projb)
    dq2, dkv, dkr = _flash_bwd(qcat, kv, kr, dob, lse, dl, rc, rs)
    duq = _matmul(cqn, dq2, trans_a=True, name="d_q_up", out_dtype=BF16)
    dcqn = _matmul(dq2, p['b_w_uq'], trans_b=True, name="d_cqn")
    dprojb, dgq = _q_norm_bwd(dcqn, projb, p['b_q_norm_g'], dprojb)
    g['b_w_in'] = _matmul(h1b, dprojb, trans_a=True, name="d_b_in", bn=1280, out_dtype=BF16)
    t1 = _matmul(dprojb, p['b_w_in'], trans_b=True, name="d_h1_b", bk=1280)
    dukv = _matmul(ckn, dkv, trans_a=True, name="d_kv_up", out_dtype=BF16)
    dckn = _matmul(dkv, p['kv_w_ukv'], trans_b=True, name="d_ckn")
    dckv, dgkv = _kv_mid_bwd(dckn, ckv, dkr, p['kv_norm_g'], rc, rs)
    ddown = _matmul(h1b, dckv, trans_a=True, name="d_kv_down", out_dtype=BF16)
    g['b_w_out'] = g['b_w_out'].reshape(N_DEV, -1, D_MODEL)
    g['b_w_in'] = g['b_w_in'].reshape(D_MODEL, N_DEV, -1).transpose(1, 0, 2)
    g['kv_w_down'] = ddown[:, :KV_LORA_RANK + QK_ROPE_DIM].reshape(N_DEV, -1, KV_LORA_RANK + QK_ROPE_DIM)
    dukv = dukv.reshape(KV_LORA_RANK, N_HEADS, HEAD_PAD)
    g['kv_w_uk'] = dukv[:, :, :QK_NOPE_DIM].reshape(N_DEV, -1, QK_NOPE_DIM)
    g['kv_w_uv'] = dukv[:, :, QK_NOPE_DIM:].reshape(N_DEV, -1, V_HEAD_DIM)
    g['b_w_uq'] = duq.reshape(Q_LORA_RANK, N_HEADS, HEAD_PAD)[:, :, :QK_NOPE_DIM + QK_ROPE_DIM].reshape(
        N_DEV, -1, QK_NOPE_DIM + QK_ROPE_DIM)
    sent1 = comm.send_group1(g)
    t2 = _matmul(dckv, p['kv_w_down'], trans_b=True, name="d_h1_kv", after=sent1)
    dr1, dr1b, dg0, db0, dba_out = _ln0_bwd(dr2, t1, t2, xh1, rstd1, ln_g0)
    dw_out = _matmul(u2, dr1b, trans_a=True, name="d_a_out", out_dtype=BF16).reshape(N_DEV, -1, D_MODEL)
    du2 = _matmul(dr1b, p['a_w_out'], trans_b=True, name="d_u2")
    du1, dproj, dng, dnb, dbz = _conv_mid_bwd_rows(u1, proj, du2, p['a_norm_g'], p['a_norm_b'])
    dproj, dcw, dcb, dbv = _conv_bwd(du1, proj, p['a_conv_w'], dproj)
    dw_in = _matmul(xb, dproj, trans_a=True, name="d_a_in", out_dtype=BF16, bn=3 * e // N_DEV, out_blocked=True)
    small = (dg0, dg1, db0, db1, dbv, dbz, dcb, dng, dnb, dba_out, dgkv, dgq, dcw)
    sent2 = comm.send_group2({'a_w_out': dw_out, 'a_w_in': dw_in}, small)
    dxp = _matmul(dproj, p['a_w_in3'], trans_b=True, name="d_x_a", b_blocked=True, after=sent2)
    grad_x = _grad_x(dr1, dxp)
    return loss, grad_x


def _grad_x(dr1, dxp):
    t, d = dr1.shape
    tm = min(t, 512)

    def body(a_ref, b_ref, o_ref):
        o_ref[...] = ALPHA * a_ref[...] + b_ref[...]

    row = pl.BlockSpec((tm, d), lambda i: (i, 0))
    return pl.pallas_call(
        body, name="grad_x",
        out_shape=jax.ShapeDtypeStruct((t, d), F32),
        grid=(t // tm,),
        in_specs=[row, row],
        out_specs=row,
        compiler_params=_cparams(("parallel",)),
    )(dr1, dxp)


def _shard_2d(a):
    return a.reshape(-1, a.shape[-1])


def _gathered_to_full(name, blocks):
    if name == 'a_w_in':
        return blocks
    if name == 'b_w_in':
        return blocks.transpose(1, 0, 2).reshape(D_MODEL, -1)
    if name == 'a_conv_w':
        return blocks.transpose(1, 0, 2).reshape(CONV_KERNEL, -1)
    if name in ('a_b_in', 'a_conv_b', 'a_norm_g', 'a_norm_b', 'a_b_out'):
        return blocks.reshape(-1)
    if name in ('kv_w_uk', 'kv_w_uv'):
        return blocks.reshape(KV_LORA_RANK, N_HEADS, -1)
    if name == 'b_w_uq':
        return blocks.reshape(Q_LORA_RANK, N_HEADS, -1)
    return blocks.reshape(-1, blocks.shape[-1])


def kernel(x, positions, ln_g, ln_b, a_w_in, a_b_in, a_conv_w, a_conv_b, a_norm_g, a_norm_b, a_w_out, a_b_out, kv_w_down, kv_norm_g, kv_w_uk, kv_w_uv, b_w_in, b_q_norm_g, b_w_uq, b_w_out, loss_target, m_ln_g, m_ln_b, m_a_w_in, m_a_b_in, m_a_conv_w, m_a_conv_b, m_a_norm_g, m_a_norm_b, m_a_w_out, m_a_b_out, m_kv_w_down, m_kv_norm_g, m_kv_w_uk, m_kv_w_uv, m_b_w_in, m_b_q_norm_g, m_b_w_uq, m_b_w_out, v_ln_g, v_ln_b, v_a_w_in, v_a_b_in, v_a_conv_w, v_a_conv_b, v_a_norm_g, v_a_norm_b, v_a_w_out, v_a_b_out, v_kv_w_down, v_kv_norm_g, v_kv_w_uk, v_kv_w_uv, v_b_w_in, v_b_q_norm_g, v_b_w_uq, v_b_w_out):
    w = dict(ln_g=ln_g, ln_b=ln_b, a_w_in=a_w_in, a_b_in=a_b_in, a_conv_w=a_conv_w, a_conv_b=a_conv_b,
             a_norm_g=a_norm_g, a_norm_b=a_norm_b, a_w_out=a_w_out, a_b_out=a_b_out, kv_w_down=kv_w_down,
             kv_norm_g=kv_norm_g, kv_w_uk=kv_w_uk, kv_w_uv=kv_w_uv, b_w_in=b_w_in, b_q_norm_g=b_q_norm_g,
             b_w_uq=b_w_uq, b_w_out=b_w_out)
    m = dict(ln_g=m_ln_g, ln_b=m_ln_b, a_w_in=m_a_w_in, a_b_in=m_a_b_in, a_conv_w=m_a_conv_w, a_conv_b=m_a_conv_b,
             a_norm_g=m_a_norm_g, a_norm_b=m_a_norm_b, a_w_out=m_a_w_out, a_b_out=m_a_b_out, kv_w_down=m_kv_w_down,
             kv_norm_g=m_kv_norm_g, kv_w_uk=m_kv_w_uk, kv_w_uv=m_kv_w_uv, b_w_in=m_b_w_in, b_q_norm_g=m_b_q_norm_g,
             b_w_uq=m_b_w_uq, b_w_out=m_b_w_out)
    v = dict(ln_g=v_ln_g, ln_b=v_ln_b, a_w_in=v_a_w_in, a_b_in=v_a_b_in, a_conv_w=v_a_conv_w, a_conv_b=v_a_conv_b,
             a_norm_g=v_a_norm_g, a_norm_b=v_a_norm_b, a_w_out=v_a_w_out, a_b_out=v_a_b_out, kv_w_down=v_kv_w_down,
             kv_norm_g=v_kv_norm_g, kv_w_uk=v_kv_w_uk, kv_w_uv=v_kv_w_uv, b_w_in=v_b_w_in, b_q_norm_g=v_b_q_norm_g,
             b_w_uq=v_b_w_uq, b_w_out=v_b_w_out)

    small_flat = jnp.concatenate([w[n].reshape(-1) for n in SMALL_WEIGHTS]).reshape(1, -1)
    ga = _all_gather("all_gather_first", [_shard_2d(w['a_w_in']).astype(BF16), small_flat])
    full = {'a_w_in': ga[0]}
    gs = ga[1].reshape(N_DEV, -1)
    off = 0
    for n in SMALL_WEIGHTS:
        size = math.prod(w[n].shape)
        full[n] = _gathered_to_full(n, gs[:, off:off + size].reshape((N_DEV,) + _shard_2d(w[n]).shape))
        off += size
    for n in REPLICATED:
        full[n] = w[n]
    rest_names = [n for n in MATMUL_WEIGHTS if n != 'a_w_in']
    group1 = ('b_w_out', 'b_w_uq', 'b_w_in', 'kv_w_uk', 'kv_w_uv', 'kv_w_down')
    group2 = ('a_w_out', 'a_w_in')

    class Comm:
        def __init__(self):
            self.rest, self.rest_token = _comm_start(
                "gather_rest_start", [_shard_2d(w[n]).astype(BF16) for n in rest_names], True, ga[0])

        def first_after(self):
            return self.rest_token

        def rest_weights(self, after):
            lands = _comm_wait("gather_rest_wait", self.rest, True, after)
            return _prep_weights({n: _gathered_to_full(n, lands[i]) for i, n in enumerate(rest_names)})

        def send_group1(self, g):
            self.g1, token = _comm_start("exchange_g1_start", [g[n] for n in group1], False, None)
            return token

        def send_group2(self, g, small):
            self.g2, token = _comm_start("exchange_g2_start", [g[n] for n in group2] + [_pack_small_grads(*small)],
                                         False, None)
            return token

    comm = Comm()

    loss_loc, grad_x = _local_step(x[0], positions[0], loss_target[0], _prep_weights(full), comm)
    loss = lax.psum(loss_loc[0, 0], ("x", "y", "c"))

    res = {}
    recv1 = _comm_wait("exchange_g1_wait", comm.g1, False, grad_x)
    for i, n in enumerate(group1):
        res[n] = _adamw(n, recv1[i], _shard_2d(w[n]), _shard_2d(m[n]), _shard_2d(v[n]))
    recv2 = _comm_wait("exchange_g2_wait", comm.g2, False, res[group1[-1]][0])
    for i, n in enumerate(group2):
        res[n] = _adamw(n, recv2[i], _shard_2d(w[n]), _shard_2d(m[n]), _shard_2d(v[n]))
    two_d = lambda d: [_shard_2d(d[n]) if d[n].ndim > 1 else d[n].reshape(1, -1) for n in SMALL_NAMES]
    sg, sd, sm, sv = _adamw_small(recv2[-1], two_d(w), two_d(m), two_d(v))
    for i, n in enumerate(SMALL_NAMES):
        res[n] = (sg[i], sd[i], sm[i], sv[i])
    outs = [[res[n][j].reshape(w[n].shape) for n in WEIGHT_NAMES] for j in range(4)]
    return (loss, grad_x[None], *outs[0], *outs[1], *outs[2], *outs[3])
```

```python
import functools
import math

import jax
import jax.numpy as jnp
from jax import lax
from jax.experimental import pallas as pl
from jax.experimental.pallas import tpu as pltpu

F32 = jnp.float32
BF16 = jnp.bfloat16

D_MODEL = 1024
DEPTH = 2
CONV_WIDTH = 2 * D_MODEL
CONV_KERNEL = 31
N_HEADS = 16
QK_NOPE_DIM = 128
QK_ROPE_DIM = 64
V_HEAD_DIM = 128
KV_LORA_RANK = D_MODEL // 4
Q_LORA_RANK = D_MODEL // 2
ROPE_THETA = 10000.0
LN_EPS = 1e-5
RMS_EPS = 1e-6
MASK_VALUE = -1e30
ALPHA = (2.0 * DEPTH) ** 0.25
ATTN_SCALE = 1.0 / math.sqrt(QK_NOPE_DIM + QK_ROPE_DIM)

ADAM_LR = 0.001
ADAM_B1 = 0.9
ADAM_B2 = 0.999
ADAM_EPS = 1e-08
ADAM_WD = 0.01
ADAM_STEP = 10

N_DEV = 8
LANES = 128
HEAD_PAD = 256
HALO = 32
VMEM_LIMIT = 56 * 1024 * 1024

WEIGHT_NAMES = ['ln_g', 'ln_b', 'a_w_in', 'a_b_in', 'a_conv_w', 'a_conv_b', 'a_norm_g', 'a_norm_b',
                'a_w_out', 'a_b_out', 'kv_w_down', 'kv_norm_g', 'kv_w_uk', 'kv_w_uv', 'b_w_in',
                'b_q_norm_g', 'b_w_uq', 'b_w_out']
REPLICATED = ('ln_g', 'ln_b', 'kv_norm_g', 'b_q_norm_g')
MATMUL_WEIGHTS = ('a_w_in', 'a_w_out', 'kv_w_down', 'kv_w_uk', 'kv_w_uv', 'b_w_in', 'b_w_uq', 'b_w_out')
SMALL_WEIGHTS = ('a_b_in', 'a_conv_w', 'a_conv_b', 'a_norm_g', 'a_norm_b', 'a_b_out')

NN = (((1,), (0,)), ((), ()))
NT = (((1,), (1,)), ((), ()))
TN = (((0,), (0,)), ((), ()))


def _cparams(sem):
    return pltpu.CompilerParams(dimension_semantics=sem, vmem_limit_bytes=VMEM_LIMIT)


def _sigmoid(x):
    return 0.5 * jnp.tanh(0.5 * x) + 0.5


def _aligned(v, m):
    return v if isinstance(v, int) else pl.multiple_of(v, m)


def _taps_by_residue(offset_of):
    groups = {}
    for j in range(CONV_KERNEL):
        off = offset_of(j)
        groups.setdefault(off % 8, []).append((j, off // 8))
    return sorted(groups.items())


def _swap_rope_halves(t):
    lane = lax.broadcasted_iota(jnp.int32, t.shape, 1)
    return jnp.where(lane < QK_ROPE_DIM // 2, pltpu.roll(t, LANES - QK_ROPE_DIM // 2, 1),
                     pltpu.roll(t, QK_ROPE_DIM // 2, 1))


def _matmul(a, b, *, name, bias=None, trans_a=False, trans_b=False, out_dtype=F32, bm=1024, bn=1024, bk=1024,
            b_blocked=False, out_blocked=False, after=None):
    if trans_a:
        kdim, m = a.shape
    else:
        m, kdim = a.shape
    if b_blocked and trans_b:
        n, k2 = b.shape[1], b.shape[0] * b.shape[2]
        bk = b.shape[2]
    elif b_blocked:
        k2, n = b.shape[1], b.shape[0] * b.shape[2]
        bn = b.shape[2]
    elif trans_b:
        n, k2 = b.shape
    else:
        k2, n = b.shape
    assert kdim == k2, (a.shape, b.shape)
    bm, bn, bk = min(bm, m), min(bn, n), min(bk, kdim)
    assert m % bm == 0 and n % bn == 0 and kdim % bk == 0, (name, m, n, kdim, bm, bn, bk)
    nk = kdim // bk
    dims = (((0 if trans_a else 1,), (1 if trans_b else 0,)), ((), ()))
    has_bias = bias is not None

    def body(*refs):
        refs = list(refs)
        a_ref, b_ref = refs[:2]
        del refs[:2]
        bias_ref = refs.pop(0) if has_bias else None
        if after is not None:
            refs.pop(0)
        o_ref = refs.pop(0)
        rest = refs
        prod = lax.dot_general(a_ref[...], b_ref[...], dims, preferred_element_type=F32)

        def finish(acc):
            if has_bias:
                acc = acc + bias_ref[...]
            o_ref[...] = acc.astype(out_dtype)

        if nk == 1:
            finish(prod)
        else:
            acc_ref = rest[0]
            k = pl.program_id(2)

            @pl.when(k == 0)
            def _():
                acc_ref[...] = prod

            @pl.when(k > 0)
            def _():
                acc_ref[...] += prod

            @pl.when(k == nk - 1)
            def _():
                finish(acc_ref[...])

    a_spec = pl.BlockSpec((bk, bm), lambda i, j, k: (k, i)) if trans_a else pl.BlockSpec((bm, bk), lambda i, j, k: (i, k))
    if b_blocked and trans_b:
        b_spec = pl.BlockSpec((None, bn, bk), lambda i, j, k: (k, j, 0))
    elif b_blocked:
        b_spec = pl.BlockSpec((None, bk, bn), lambda i, j, k: (j, k, 0))
    elif trans_b:
        b_spec = pl.BlockSpec((bn, bk), lambda i, j, k: (j, k))
    else:
        b_spec = pl.BlockSpec((bk, bn), lambda i, j, k: (k, j))
    in_specs = [a_spec, b_spec]
    args = [a, b]
    if has_bias:
        in_specs.append(pl.BlockSpec((1, bn), lambda i, j, k: (0, j)))
        args.append(bias)
    if after is not None:
        in_specs.append(pl.BlockSpec(memory_space=pl.ANY))
        args.append(after)
    if out_blocked:
        out_struct = jax.ShapeDtypeStruct((n // bn, m, bn), out_dtype)
        out_spec = pl.BlockSpec((None, bm, bn), lambda i, j, k: (j, i, 0))
    else:
        out_struct = jax.ShapeDtypeStruct((m, n), out_dtype)
        out_spec = pl.BlockSpec((bm, bn), lambda i, j, k: (i, j))
    return pl.pallas_call(
        body, name=name,
        out_shape=out_struct,
        grid=(m // bm, n // bn, nk),
        in_specs=in_specs,
        out_specs=out_spec,
        scratch_shapes=[pltpu.VMEM((bm, bn), F32)] if nk > 1 else [],
        compiler_params=_cparams(("parallel", "parallel", "arbitrary")),
    )(*args)


def _rope_tables(pos_col, freq_row):
    t = pos_col.shape[0]
    tm = min(t, 512)

    def body(pos_ref, f_ref, c_ref, s_ref):
        ang = pos_ref[...].astype(F32) * f_ref[...]
        lane = lax.broadcasted_iota(jnp.int32, ang.shape, 1)
        cos = jnp.cos(ang)
        sin = jnp.sin(ang)
        c_ref[...] = jnp.where(lane < QK_ROPE_DIM, cos, 0.0)
        s_ref[...] = jnp.where(lane < QK_ROPE_DIM // 2, -sin, jnp.where(lane < QK_ROPE_DIM, sin, 0.0))

    return pl.pallas_call(
        body, name="rope_tables",
        out_shape=(jax.ShapeDtypeStruct((t, LANES), F32), jax.ShapeDtypeStruct((t, LANES), F32)),
        grid=(t // tm,),
        in_specs=[pl.BlockSpec((tm, 1), lambda i: (i, 0)), pl.BlockSpec((1, LANES), lambda i: (0, 0))],
        out_specs=(pl.BlockSpec((tm, LANES), lambda i: (i, 0)), pl.BlockSpec((tm, LANES), lambda i: (i, 0))),
        compiler_params=_cparams(("parallel",)),
    )(pos_col, freq_row)


def _conv_mid_fwd(proj, cw, cb, ng, nb):
    t = proj.shape[0]
    e = CONV_WIDTH
    tm = min(t, 128)
    hb = tm // HALO
    nchunk = e // LANES

    def body(val_ref, gg_ref, z_ref, valh_ref, ggh_ref, cw_ref, cb_ref, ng_ref, nb_ref, u1_ref, u2_ref, ext_ref, sh_ref):
        i = pl.program_id(0)
        u0 = val_ref[...] * _sigmoid(gg_ref[...])
        h0 = valh_ref[...] * _sigmoid(ggh_ref[...])
        ext_ref[0:HALO, :] = jnp.where(i > 0, h0, 0.0)
        ext_ref[HALO:, :] = u0

        def chunk(c, carry):
            col = pl.multiple_of(c * LANES, LANES)
            acc = jnp.zeros((tm, LANES), F32)
            for res, taps in _taps_by_residue(lambda j: HALO - (CONV_KERNEL - 1) + j):
                span = 8 * max(a for _, a in taps) + tm
                sh_ref[0:span, :] = ext_ref[res:res + span, pl.ds(col, LANES)]
                for j, a in taps:
                    acc = acc + cw_ref[j:j + 1, pl.ds(col, LANES)] * sh_ref[8 * a:8 * a + tm, :]
            u1_ref[:, pl.ds(col, LANES)] = acc + cb_ref[:, pl.ds(col, LANES)]
            return carry

        lax.fori_loop(0, nchunk, chunk, 0)
        u1 = u1_ref[...]
        mu = jnp.mean(u1, axis=-1, keepdims=True)
        xc = u1 - mu
        var = jnp.mean(xc * xc, axis=-1, keepdims=True)
        n = xc * lax.rsqrt(var + LN_EPS) * ng_ref[...] + nb_ref[...]
        z = z_ref[...]
        u2_ref[...] = ((n * _sigmoid(n)) * (z * _sigmoid(z))).astype(BF16)

    row = lambda c: pl.BlockSpec((tm, e), lambda i: (i, c))
    halo = lambda c: pl.BlockSpec((HALO, e), lambda i: (jnp.maximum(i * hb - 1, 0), c))
    par = lambda r: pl.BlockSpec((r, e), lambda i: (0, 0))
    return pl.pallas_call(
        body, name="conv_mid_fwd",
        out_shape=(jax.ShapeDtypeStruct((t, e), F32), jax.ShapeDtypeStruct((t, e), BF16)),
        grid=(t // tm,),
        in_specs=[row(0), row(1), row(2), halo(0), halo(1), par(HALO), par(1), par(1), par(1)],
        out_specs=(pl.BlockSpec((tm, e), lambda i: (i, 0)), pl.BlockSpec((tm, e), lambda i: (i, 0))),
        scratch_shapes=[pltpu.VMEM((tm + HALO, e), F32), pltpu.VMEM((tm + HALO, LANES), F32)],
        compiler_params=_cparams(("parallel",)),
    )(proj, proj, proj, proj, proj, cw, cb, ng, nb)


def _ln_res_fwd(x, y, g, b):
    t, d = x.shape
    tm = min(t, 256)

    def body(x_ref, y_ref, g_ref, b_ref, h_ref, hb_ref, xh_ref, rs_ref):
        r = ALPHA * x_ref[...] + y_ref[...]
        mu = jnp.mean(r, axis=-1, keepdims=True)
        xc = r - mu
        var = jnp.mean(xc * xc, axis=-1, keepdims=True)
        rstd = lax.rsqrt(var + LN_EPS)
        xh = xc * rstd
        h = xh * g_ref[...] + b_ref[...]
        h_ref[...] = h
        hb_ref[...] = h.astype(BF16)
        xh_ref[...] = xh
        rs_ref[...] = rstd

    row = pl.BlockSpec((tm, d), lambda i: (i, 0))
    par = pl.BlockSpec((1, d), lambda i: (0, 0))
    return pl.pallas_call(
        body, name="ln0_fwd",
        out_shape=(jax.ShapeDtypeStruct((t, d), F32), jax.ShapeDtypeStruct((t, d), BF16),
                   jax.ShapeDtypeStruct((t, d), F32), jax.ShapeDtypeStruct((t, 1), F32)),
        grid=(t // tm,),
        in_specs=[row, row, par, par],
        out_specs=(row, row, row, pl.BlockSpec((tm, 1), lambda i: (i, 0))),
        compiler_params=_cparams(("parallel",)),
    )(x, y, g, b)


def _kv_mid_fwd(ckv, g, rc, rs):
    t = ckv.shape[0]
    tm = min(t, 512)
    r = KV_LORA_RANK

    def body(ckv_ref, g_ref, c_ref, s_ref, ckn_ref, kr_ref):
        c = ckv_ref[:, 0:r]
        ms = jnp.mean(c * c, axis=-1, keepdims=True)
        ckn_ref[...] = (c * lax.rsqrt(ms + RMS_EPS) * g_ref[...]).astype(BF16)
        tr = ckv_ref[:, r:r + LANES]
        kr_ref[...] = (tr * c_ref[...] + _swap_rope_halves(tr) * s_ref[...]).astype(BF16)

    return pl.pallas_call(
        body, name="kv_mid_fwd",
        out_shape=(jax.ShapeDtypeStruct((t, r), BF16), jax.ShapeDtypeStruct((t, LANES), BF16)),
        grid=(t // tm,),
        in_specs=[pl.BlockSpec((tm, r + LANES), lambda i: (i, 0)), pl.BlockSpec((1, r), lambda i: (0, 0)),
                  pl.BlockSpec((tm, LANES), lambda i: (i, 0)), pl.BlockSpec((tm, LANES), lambda i: (i, 0))],
        out_specs=(pl.BlockSpec((tm, r), lambda i: (i, 0)), pl.BlockSpec((tm, LANES), lambda i: (i, 0))),
        compiler_params=_cparams(("parallel",)),
    )(ckv, g, rc, rs)


def _q_norm_fwd(projb, g):
    t = projb.shape[0]
    tm = min(t, 512)
    r = Q_LORA_RANK

    def body(c_ref, g_ref, o_ref):
        c = c_ref[...]
        ms = jnp.mean(c * c, axis=-1, keepdims=True)
        o_ref[...] = (c * lax.rsqrt(ms + RMS_EPS) * g_ref[...]).astype(BF16)

    return pl.pallas_call(
        body, name="q_norm_fwd",
        out_shape=jax.ShapeDtypeStruct((t, r), BF16),
        grid=(t // tm,),
        in_specs=[pl.BlockSpec((tm, r), lambda i: (i, 0)), pl.BlockSpec((1, r), lambda i: (0, 0))],
        out_specs=pl.BlockSpec((tm, r), lambda i: (i, 0)),
        compiler_params=_cparams(("parallel",)),
    )(projb, g)


def _q_rope_fwd(q2, rc, rs):
    t = q2.shape[0]
    tm = min(t, 256)
    w = N_HEADS * HEAD_PAD

    def body(q_ref, c_ref, s_ref, o_ref):
        c = c_ref[...]
        s = s_ref[...]
        for h in range(N_HEADS):
            a = h * HEAD_PAD
            o_ref[:, a:a + LANES] = q_ref[:, a:a + LANES].astype(BF16)
            tr = q_ref[:, a + LANES:a + 2 * LANES]
            o_ref[:, a + LANES:a + 2 * LANES] = (tr * c + _swap_rope_halves(tr) * s).astype(BF16)

    return pl.pallas_call(
        body, name="q_rope_fwd",
        out_shape=jax.ShapeDtypeStruct((t, w), BF16),
        grid=(t // tm,),
        in_specs=[pl.BlockSpec((tm, w), lambda i: (i, 0)), pl.BlockSpec((tm, LANES), lambda i: (i, 0)),
                  pl.BlockSpec((tm, LANES), lambda i: (i, 0))],
        out_specs=pl.BlockSpec((tm, w), lambda i: (i, 0)),
        compiler_params=_cparams(("parallel",)),
    )(q2, rc, rs)


def _flash_fwd(qcat, kv, kr):
    t = qcat.shape[0]
    tq = min(t, 512)
    nq = t // tq
    exp2_scale = ATTN_SCALE * math.log2(math.e)

    def body(q_ref, kv_ref, kr_ref, o_ref, lse_ref, kcat_ref):
        kcat_ref[:, 0:LANES] = kv_ref[:, 0:LANES]
        kcat_ref[:, LANES:] = kr_ref[...]
        rows = lax.broadcasted_iota(jnp.int32, (tq, tq), 0)
        cols = lax.broadcasted_iota(jnp.int32, (tq, tq), 1)
        for i in range(nq):
            a = i * tq
            q = q_ref[a:a + tq, :]
            sd = lax.dot_general(q, kcat_ref[a:a + tq, :], NT, preferred_element_type=F32)
            sd = jnp.where(cols <= rows, sd, MASK_VALUE)
            m = jnp.max(sd, axis=1, keepdims=True)
            if i > 0:
                sp = lax.dot_general(q, kcat_ref[0:a, :], NT, preferred_element_type=F32)
                m = jnp.maximum(m, jnp.max(sp, axis=1, keepdims=True))
            pd = jnp.exp2((sd - m) * exp2_scale)
            l = jnp.sum(pd, axis=1, keepdims=True)
            acc = lax.dot_general(pd.astype(BF16), kv_ref[a:a + tq, LANES:], NN, preferred_element_type=F32)
            if i > 0:
                pp = jnp.exp2((sp - m) * exp2_scale)
                l = l + jnp.sum(pp, axis=1, keepdims=True)
                acc = acc + lax.dot_general(pp.astype(BF16), kv_ref[0:a, LANES:], NN, preferred_element_type=F32)
            o_ref[a:a + tq, :] = acc / l
            lse_ref[a:a + tq, :] = jnp.broadcast_to(m * ATTN_SCALE + jnp.log(l), (tq, LANES))

    hw = N_HEADS * LANES
    head256 = pl.BlockSpec((t, HEAD_PAD), lambda h: (0, h))
    head128 = pl.BlockSpec((t, LANES), lambda h: (0, h))
    return pl.pallas_call(
        body, name="flash_fwd",
        out_shape=(jax.ShapeDtypeStruct((t, hw), F32), jax.ShapeDtypeStruct((t, hw), F32)),
        grid=(N_HEADS,),
        in_specs=[head256, head256, pl.BlockSpec((t, LANES), lambda h: (0, 0), pipeline_mode=pl.Buffered(1))],
        out_specs=(head128, head128),
        scratch_shapes=[pltpu.VMEM((t, HEAD_PAD), BF16)],
        compiler_params=_cparams(("parallel",)),
    )(qcat, kv, kr)


def _gate_fwd(o, projb):
    t, hw = o.shape
    tm = min(t, 512)
    bw = Q_LORA_RANK
    nb = hw // bw

    def body(o_ref, z_ref, o2_ref):
        z = z_ref[...]
        o2_ref[...] = (o_ref[...] * (z * _sigmoid(z))).astype(BF16)

    return pl.pallas_call(
        body, name="gate_fwd",
        out_shape=jax.ShapeDtypeStruct((t, hw), BF16),
        grid=(t // tm, nb),
        in_specs=[pl.BlockSpec((tm, bw), lambda i, j: (i, j)), pl.BlockSpec((tm, bw), lambda i, j: (i, j + 1))],
        out_specs=pl.BlockSpec((tm, bw), lambda i, j: (i, j)),
        compiler_params=_cparams(("parallel", "parallel")),
    )(o, projb)


def _final_ln_loss(h1, yb, g, b, target):
    t, d = h1.shape
    tm = min(t, 256)

    def body(h_ref, y_ref, g_ref, b_ref, t_ref, loss_ref, dr_ref, drb_ref, dg_ref, db_ref):
        i = pl.program_id(0)
        r = ALPHA * h_ref[...] + y_ref[...]
        mu = jnp.mean(r, axis=-1, keepdims=True)
        xc = r - mu
        var = jnp.mean(xc * xc, axis=-1, keepdims=True)
        rstd = lax.rsqrt(var + LN_EPS)
        xh = xc * rstd
        g = g_ref[...]
        diff = xh * g + b_ref[...] - t_ref[...]
        part = 0.5 * jnp.sum(jnp.mean(diff * diff, axis=-1, keepdims=True))
        dh = diff * (1.0 / d)
        dgp = jnp.sum(dh * xh, axis=0, keepdims=True)
        dbp = jnp.sum(dh, axis=0, keepdims=True)

        @pl.when(i == 0)
        def _():
            loss_ref[...] = jnp.zeros_like(loss_ref)
            dg_ref[...] = jnp.zeros_like(dg_ref)
            db_ref[...] = jnp.zeros_like(db_ref)

        loss_ref[...] += jnp.full(loss_ref.shape, part, F32)
        dg_ref[...] += dgp
        db_ref[...] += dbp
        dxh = dh * g
        dr = rstd * (dxh - jnp.mean(dxh, axis=-1, keepdims=True) - xh * jnp.mean(dxh * xh, axis=-1, keepdims=True))
        dr_ref[...] = dr
        drb_ref[...] = dr.astype(BF16)

    row = pl.BlockSpec((tm, d), lambda i: (i, 0))
    par = pl.BlockSpec((1, d), lambda i: (0, 0))
    return pl.pallas_call(
        body, name="final_ln_loss",
        out_shape=(jax.ShapeDtypeStruct((1, LANES), F32), jax.ShapeDtypeStruct((t, d), F32),
                   jax.ShapeDtypeStruct((t, d), BF16), jax.ShapeDtypeStruct((1, d), F32),
                   jax.ShapeDtypeStruct((1, d), F32)),
        grid=(t // tm,),
        in_specs=[row, row, par, par, row],
        out_specs=(pl.BlockSpec((1, LANES), lambda i: (0, 0)), row, row, par, par),
        compiler_params=_cparams(("arbitrary",)),
    )(h1, yb, g, b, target)


def _gate_bwd(do2, o, projb):
    t, hw = o.shape
    tm = min(t, 512)
    bw = Q_LORA_RANK
    nb = hw // bw
    wb = projb.shape[1]

    def body(d_ref, o_ref, z_ref, dob_ref, dz_ref, dl_ref):
        z = z_ref[...]
        sg = _sigmoid(z)
        d2 = d_ref[...]
        o = o_ref[...]
        do = d2 * (z * sg)
        dob_ref[...] = do.astype(BF16)
        dz_ref[...] = (d2 * o * (sg * (1.0 + z * (1.0 - sg)))).astype(BF16)
        prod = do * o
        for hh in range(bw // LANES):
            dsum = jnp.sum(prod[:, hh * LANES:(hh + 1) * LANES], axis=1, keepdims=True)
            dl_ref[:, hh * LANES:(hh + 1) * LANES] = jnp.broadcast_to(dsum, (tm, LANES))

    blk = pl.BlockSpec((tm, bw), lambda i, j: (i, j))
    blk1 = pl.BlockSpec((tm, bw), lambda i, j: (i, j + 1))
    return pl.pallas_call(
        body, name="gate_bwd",
        out_shape=(jax.ShapeDtypeStruct((t, hw), BF16), jax.ShapeDtypeStruct((t, wb), BF16),
                   jax.ShapeDtypeStruct((t, hw), F32)),
        grid=(t // tm, nb),
        in_specs=[blk, blk, blk1],
        out_specs=(blk, blk1, blk),
        compiler_params=_cparams(("parallel", "parallel")),
    )(do2, o, projb)


def _flash_bwd(qcat, kv, kr, dob, lse, dl, rc, rs):
    t = qcat.shape[0]
    tq = min(t, 512)
    nq = t // tq
    q_chunk = 2 * tq
    log2e = math.log2(math.e)
    exp2_scale = ATTN_SCALE * log2e
    once = pl.Buffered(1)

    def body(q_ref, kv_ref, kr_ref, do_ref, lse_ref, dl_ref, c_ref, s_ref,
             dq_ref, dkv_ref, dkr_ref, kcat_ref, dqa_ref, dka_ref, dva_ref):
        h = pl.program_id(0)
        kcat_ref[:, 0:LANES] = kv_ref[:, 0:LANES]
        kcat_ref[:, LANES:] = kr_ref[...]
        dqa_ref[...] = jnp.zeros_like(dqa_ref)

        @pl.when(h == 0)
        def _():
            dkr_ref[...] = jnp.zeros_like(dkr_ref)

        rows = lax.broadcasted_iota(jnp.int32, (tq, tq), 0)
        cols = lax.broadcasted_iota(jnp.int32, (tq, tq), 1)

        def block(qs, n, ks, masked):
            q = q_ref[qs:qs + n, :]
            kc = kcat_ref[ks:ks + tq, :]
            s = lax.dot_general(q, kc, NT, preferred_element_type=F32)
            if masked:
                s = jnp.where(cols <= rows, s, MASK_VALUE)
            p = jnp.exp2(s * exp2_scale - lse_ref[qs:qs + n, 0:1] * log2e)
            do = do_ref[qs:qs + n, :]
            dp = lax.dot_general(do, kv_ref[ks:ks + tq, LANES:], NT, preferred_element_type=F32)
            ds = (p * (dp - dl_ref[qs:qs + n, 0:1])).astype(BF16)
            dva_ref[...] += lax.dot_general(p.astype(BF16), do, TN, preferred_element_type=F32)
            dka_ref[...] += lax.dot_general(ds, q, TN, preferred_element_type=F32)
            dqa_ref[qs:qs + n, :] += lax.dot_general(ds, kc, NN, preferred_element_type=F32)

        for kb in range(nq):
            a = kb * tq
            dka_ref[...] = jnp.zeros_like(dka_ref)
            dva_ref[...] = jnp.zeros_like(dva_ref)
            block(a, tq, a, True)
            qs = a + tq
            while qs < t:
                n = min(q_chunk, t - qs)
                block(qs, n, a, False)
                qs += n
            dk = dka_ref[...] * ATTN_SCALE
            dkv_ref[a:a + tq, 0:LANES] = dk[:, 0:LANES].astype(BF16)
            dkv_ref[a:a + tq, LANES:] = dva_ref[...].astype(BF16)
            dkr_ref[a:a + tq, :] += dk[:, LANES:]

        dq = dqa_ref[...] * ATTN_SCALE
        dq_ref[:, 0:LANES] = dq[:, 0:LANES].astype(BF16)
        d2 = dq[:, LANES:]
        dq_ref[:, LANES:] = (d2 * c_ref[...] - _swap_rope_halves(d2) * s_ref[...]).astype(BF16)

    hp = N_HEADS * HEAD_PAD
    head256 = pl.BlockSpec((t, HEAD_PAD), lambda h: (0, h))
    head128 = pl.BlockSpec((t, LANES), lambda h: (0, h))
    const128 = pl.BlockSpec((t, LANES), lambda h: (0, 0), pipeline_mode=once)
    return pl.pallas_call(
        body, name="flash_bwd",
        out_shape=(jax.ShapeDtypeStruct((t, hp), BF16), jax.ShapeDtypeStruct((t, hp), BF16),
                   jax.ShapeDtypeStruct((t, LANES), F32)),
        grid=(N_HEADS,),
        in_specs=[head256, head256, const128, head128, head128, head128, const128, const128],
        out_specs=(head256, head256, pl.BlockSpec((t, LANES), lambda h: (0, 0))),
        scratch_shapes=[pltpu.VMEM((t, HEAD_PAD), BF16), pltpu.VMEM((t, HEAD_PAD), F32),
                        pltpu.VMEM((tq, HEAD_PAD), F32), pltpu.VMEM((tq, LANES), F32)],
        compiler_params=_cparams(("arbitrary",)),
    )(qcat, kv, kr, dob, lse, dl, rc, rs)


def _q_norm_bwd(dcqn, projb, g, dprojb):
    t = projb.shape[0]
    tm = min(t, 512)
    r = Q_LORA_RANK

    def body(d_ref, c_ref, g_ref, alias_ref, o_ref, dg_ref):
        i = pl.program_id(0)
        c = c_ref[...]
        d = d_ref[...]
        rr = lax.rsqrt(jnp.mean(c * c, axis=-1, keepdims=True) + RMS_EPS)
        xh = c * rr

        @pl.when(i == 0)
        def _():
            dg_ref[...] = jnp.zeros_like(dg_ref)

        dg_ref[...] += jnp.sum(d * xh, axis=0, keepdims=True)
        dxh = d * g_ref[...]
        o_ref[...] = (rr * (dxh - xh * jnp.mean(dxh * xh, axis=-1, keepdims=True))).astype(BF16)

    blk = pl.BlockSpec((tm, r), lambda i: (i, 0))
    par = pl.BlockSpec((1, r), lambda i: (0, 0))
    return pl.pallas_call(
        body, name="q_norm_bwd",
        out_shape=(jax.ShapeDtypeStruct(dprojb.shape, BF16), jax.ShapeDtypeStruct((1, r), F32)),
        grid=(t // tm,),
        in_specs=[blk, blk, par, pl.BlockSpec(memory_space=pl.ANY)],
        out_specs=(blk, par),
        input_output_aliases={3: 0},
        compiler_params=_cparams(("arbitrary",)),
    )(dcqn, projb, g, dprojb)


def _kv_mid_bwd(dckn, ckv, dkr, g, rc, rs):
    t = ckv.shape[0]
    tm = min(t, 512)
    r = KV_LORA_RANK

    def body(d_ref, ckv_ref, dkr_ref, g_ref, c_ref, s_ref, o_ref, dg_ref):
        i = pl.program_id(0)
        c = ckv_ref[:, 0:r]
        d = d_ref[...]
        rr = lax.rsqrt(jnp.mean(c * c, axis=-1, keepdims=True) + RMS_EPS)
        xh = c * rr

        @pl.when(i == 0)
        def _():
            dg_ref[...] = jnp.zeros_like(dg_ref)

        dg_ref[...] += jnp.sum(d * xh, axis=0, keepdims=True)
        dxh = d * g_ref[...]
        o_ref[:, 0:r] = (rr * (dxh - xh * jnp.mean(dxh * xh, axis=-1, keepdims=True))).astype(BF16)
        dk = dkr_ref[...]
        o_ref[:, r:] = (dk * c_ref[...] - _swap_rope_halves(dk) * s_ref[...]).astype(BF16)

    return pl.pallas_call(
        body, name="kv_mid_bwd",
        out_shape=(jax.ShapeDtypeStruct((t, r + LANES), BF16), jax.ShapeDtypeStruct((1, r), F32)),
        grid=(t // tm,),
        in_specs=[pl.BlockSpec((tm, r), lambda i: (i, 0)), pl.BlockSpec((tm, r + LANES), lambda i: (i, 0)),
                  pl.BlockSpec((tm, LANES), lambda i: (i, 0)), pl.BlockSpec((1, r), lambda i: (0, 0)),
                  pl.BlockSpec((tm, LANES), lambda i: (i, 0)), pl.BlockSpec((tm, LANES), lambda i: (i, 0))],
        out_specs=(pl.BlockSpec((tm, r + LANES), lambda i: (i, 0)), pl.BlockSpec((1, r), lambda i: (0, 0))),
        compiler_params=_cparams(("arbitrary",)),
    )(dckn, ckv, dkr, g, rc, rs)


def _ln0_bwd(dr2, t1, t2, xh, rstd, g):
    t, d = dr2.shape
    tm = min(t, 256)

    def body(a_ref, b_ref, c_ref, xh_ref, rs_ref, g_ref, dr_ref, drb_ref, dg_ref, db_ref, dsum_ref):
        i = pl.program_id(0)
        dh = ALPHA * a_ref[...] + b_ref[...] + c_ref[...]
        xh = xh_ref[...]

        @pl.when(i == 0)
        def _():
            dg_ref[...] = jnp.zeros_like(dg_ref)
            db_ref[...] = jnp.zeros_like(db_ref)
            dsum_ref[...] = jnp.zeros_like(dsum_ref)

        dg_ref[...] += jnp.sum(dh * xh, axis=0, keepdims=True)
        db_ref[...] += jnp.sum(dh, axis=0, keepdims=True)
        dxh = dh * g_ref[...]
        dr = rs_ref[...] * (dxh - jnp.mean(dxh, axis=-1, keepdims=True) - xh * jnp.mean(dxh * xh, axis=-1, keepdims=True))
        dr_ref[...] = dr
        drb_ref[...] = dr.astype(BF16)
        dsum_ref[...] += jnp.sum(dr, axis=0, keepdims=True)

    row = pl.BlockSpec((tm, d), lambda i: (i, 0))
    par = pl.BlockSpec((1, d), lambda i: (0, 0))
    return pl.pallas_call(
        body, name="ln0_bwd",
        out_shape=(jax.ShapeDtypeStruct((t, d), F32), jax.ShapeDtypeStruct((t, d), BF16),
                   jax.ShapeDtypeStruct((1, d), F32), jax.ShapeDtypeStruct((1, d), F32),
                   jax.ShapeDtypeStruct((1, d), F32)),
        grid=(t // tm,),
        in_specs=[row, row, row, row, pl.BlockSpec((tm, 1), lambda i: (i, 0)), par],
        out_specs=(row, row, par, par, par),
        compiler_params=_cparams(("arbitrary",)),
    )(dr2, t1, t2, xh, rstd, g)


def _conv_mid_bwd_rows(u1, proj, du2, ng, nb):
    t = u1.shape[0]
    e = CONV_WIDTH
    tm = min(t, 256)

    def body(u1_ref, z_ref, d_ref, ng_ref, nb_ref, du1_ref, dz_ref, dng_ref, dnb_ref, dbz_ref):
        i = pl.program_id(0)
        u1 = u1_ref[...]
        mu = jnp.mean(u1, axis=-1, keepdims=True)
        xc = u1 - mu
        rstd = lax.rsqrt(jnp.mean(xc * xc, axis=-1, keepdims=True) + LN_EPS)
        xh = xc * rstd
        g = ng_ref[...]
        n = xh * g + nb_ref[...]
        sn = _sigmoid(n)
        z = z_ref[...]
        sz = _sigmoid(z)
        du2 = d_ref[...]
        dz = du2 * (n * sn) * (sz * (1.0 + z * (1.0 - sz)))
        dn = du2 * (z * sz) * (sn * (1.0 + n * (1.0 - sn)))

        @pl.when(i == 0)
        def _():
            dng_ref[...] = jnp.zeros_like(dng_ref)
            dnb_ref[...] = jnp.zeros_like(dnb_ref)
            dbz_ref[...] = jnp.zeros_like(dbz_ref)

        dng_ref[...] += jnp.sum(dn * xh, axis=0, keepdims=True)
        dnb_ref[...] += jnp.sum(dn, axis=0, keepdims=True)
        dbz_ref[...] += jnp.sum(dz, axis=0, keepdims=True)
        dz_ref[...] = dz.astype(BF16)
        dxh = dn * g
        du1_ref[...] = rstd * (dxh - jnp.mean(dxh, axis=-1, keepdims=True) - xh * jnp.mean(dxh * xh, axis=-1, keepdims=True))

    row = pl.BlockSpec((tm, e), lambda i: (i, 0))
    par = pl.BlockSpec((1, e), lambda i: (0, 0))
    return pl.pallas_call(
        body, name="conv_mid_bwd_rows",
        out_shape=(jax.ShapeDtypeStruct((t, e), F32), jax.ShapeDtypeStruct((t, 3 * e), BF16),
                   jax.ShapeDtypeStruct((1, e), F32), jax.ShapeDtypeStruct((1, e), F32),
                   jax.ShapeDtypeStruct((1, e), F32)),
        grid=(t // tm,),
        in_specs=[row, pl.BlockSpec((tm, e), lambda i: (i, 2)), row, par, par],
        out_specs=(row, pl.BlockSpec((tm, e), lambda i: (i, 2)), par, par, par),
        compiler_params=_cparams(("arbitrary",)),
    )(u1, proj, du2, ng, nb)


def _conv_bwd(du1, proj, cw, dproj):
    t = du1.shape[0]
    e = CONV_WIDTH
    tm = min(t, 128)
    hb = tm // HALO
    nt = t // tm
    nchunk = e // LANES
    last_halo = t // HALO - 1

    def body(d_ref, dn_ref, val_ref, gg_ref, valh_ref, ggh_ref, cw_ref, alias_ref,
             dp_ref, dcw_ref, dcb_ref, dbv_ref, extu_ref, extd_ref, du0_ref, acc_ref, sh_ref):
        i = pl.program_id(0)
        val = val_ref[...]
        sg = _sigmoid(gg_ref[...])
        h0 = valh_ref[...] * _sigmoid(ggh_ref[...])
        extu_ref[0:HALO, :] = jnp.where(i > 0, h0, 0.0)
        extu_ref[HALO:, :] = val * sg
        extd_ref[0:tm, :] = d_ref[...]
        extd_ref[tm:, :] = jnp.where(i < nt - 1, dn_ref[...], 0.0)

        @pl.when(i == 0)
        def _():
            acc_ref[...] = jnp.zeros_like(acc_ref)
            dbv_ref[...] = jnp.zeros_like(dbv_ref)

        def chunk(c, carry):
            col = pl.multiple_of(c * LANES, LANES)
            d = extd_ref[0:tm, pl.ds(col, LANES)]
            du0 = jnp.zeros((tm, LANES), F32)
            for res, taps in _taps_by_residue(lambda j: CONV_KERNEL - 1 - j):
                span = 8 * max(a for _, a in taps) + tm
                sh_ref[0:span, :] = extd_ref[res:res + span, pl.ds(col, LANES)]
                for j, a in taps:
                    du0 = du0 + cw_ref[j:j + 1, pl.ds(col, LANES)] * sh_ref[8 * a:8 * a + tm, :]
            for res, taps in _taps_by_residue(lambda j: HALO - (CONV_KERNEL - 1) + j):
                span = 8 * max(a for _, a in taps) + tm
                sh_ref[0:span, :] = extu_ref[res:res + span, pl.ds(col, LANES)]
                for j, a in taps:
                    prod = d * sh_ref[8 * a:8 * a + tm, :]
                    acc_ref[j * 8:(j + 1) * 8, pl.ds(col, LANES)] += jnp.sum(prod.reshape(tm // 8, 8, LANES), axis=0)
            acc_ref[CONV_KERNEL * 8:(CONV_KERNEL + 1) * 8, pl.ds(col, LANES)] += jnp.sum(d.reshape(tm // 8, 8, LANES), axis=0)
            du0_ref[:, pl.ds(col, LANES)] = du0
            return carry

        lax.fori_loop(0, nchunk, chunk, 0)
        du0 = du0_ref[...]
        dval = du0 * sg
        dgg = du0 * val * sg * (1.0 - sg)
        dp_ref[:, 0:e] = dval.astype(BF16)
        dp_ref[:, e:] = dgg.astype(BF16)
        dbv_ref[:, 0:e] += jnp.sum(dval, axis=0, keepdims=True)
        dbv_ref[:, e:] += jnp.sum(dgg, axis=0, keepdims=True)

        @pl.when(i == nt - 1)
        def _():
            for j in range(CONV_KERNEL):
                dcw_ref[j:j + 1, :] = jnp.sum(acc_ref[j * 8:(j + 1) * 8, :], axis=0, keepdims=True)
            dcw_ref[CONV_KERNEL:, :] = jnp.zeros((HALO - CONV_KERNEL, e), F32)
            dcb_ref[...] = jnp.sum(acc_ref[CONV_KERNEL * 8:(CONV_KERNEL + 1) * 8, :], axis=0, keepdims=True)

    row = lambda c: pl.BlockSpec((tm, e), lambda i: (i, c))
    halo_prev = lambda c: pl.BlockSpec((HALO, e), lambda i: (jnp.maximum(i * hb - 1, 0), c))
    halo_next = pl.BlockSpec((HALO, e), lambda i: (jnp.minimum((i + 1) * hb, last_halo), 0))
    return pl.pallas_call(
        body, name="conv_bwd",
        out_shape=(jax.ShapeDtypeStruct(dproj.shape, BF16), jax.ShapeDtypeStruct((HALO, e), F32),
                   jax.ShapeDtypeStruct((1, e), F32), jax.ShapeDtypeStruct((1, 2 * e), F32)),
        grid=(nt,),
        in_specs=[row(0), halo_next, row(0), row(1), halo_prev(0), halo_prev(1),
                  pl.BlockSpec((HALO, e), lambda i: (0, 0)), pl.BlockSpec(memory_space=pl.ANY)],
        out_specs=(pl.BlockSpec((tm, 2 * e), lambda i: (i, 0)), pl.BlockSpec((HALO, e), lambda i: (0, 0)),
                   pl.BlockSpec((1, e), lambda i: (0, 0)), pl.BlockSpec((1, 2 * e), lambda i: (0, 0))),
        scratch_shapes=[pltpu.VMEM((tm + HALO, e), F32), pltpu.VMEM((tm + HALO, e), F32),
                        pltpu.VMEM((tm, e), F32), pltpu.VMEM(((CONV_KERNEL + 1) * 8, e), F32),
                        pltpu.VMEM((tm + HALO, LANES), F32)],
        input_output_aliases={7: 0},
        compiler_params=_cparams(("arbitrary",)),
    )(du1, du1, proj, proj, proj, proj, cw, dproj)


def _adam_update(g, w, m, v):
    c1 = 1.0 - ADAM_B1 ** ADAM_STEP
    c2 = 1.0 - ADAM_B2 ** ADAM_STEP
    mn = ADAM_B1 * m + (1.0 - ADAM_B1) * g
    vn = ADAM_B2 * v + (1.0 - ADAM_B2) * (g * g)
    delta = -ADAM_LR * ((mn / c1) / (jnp.sqrt(vn / c2) + ADAM_EPS) + ADAM_WD * w)
    return delta, mn, vn


def _adamw(name, gbuf, w, m, v):
    r, c = w.shape
    tr = min(r, 256)
    assert r % tr == 0

    def body(g_ref, w_ref, m_ref, v_ref, go_ref, d_ref, mo_ref, vo_ref):
        g = g_ref[0].astype(F32)
        for k in range(1, N_DEV):
            g = g + g_ref[k].astype(F32)
        go_ref[...] = g
        d_ref[...], mo_ref[...], vo_ref[...] = _adam_update(g, w_ref[...], m_ref[...], v_ref[...])

    blk = pl.BlockSpec((tr, c), lambda i: (i, 0))
    shp = jax.ShapeDtypeStruct((r, c), F32)
    return pl.pallas_call(
        body, name="adamw_" + name,
        out_shape=(shp, shp, shp, shp),
        grid=(r // tr,),
        in_specs=[pl.BlockSpec((N_DEV, tr, c), lambda i: (0, i, 0)), blk, blk, blk],
        out_specs=(blk, blk, blk, blk),
        compiler_params=_cparams(("parallel",)),
    )(gbuf, w, m, v)


SMALL_TABLE_COLS = 1024
SMALL_LAYOUT = {
    'ln_g': (0, 2, 1024), 'ln_b': (8, 2, 1024), 'a_b_in': (16, 1, 768), 'a_conv_b': (24, 1, 256),
    'a_norm_g': (32, 1, 256), 'a_norm_b': (40, 1, 256), 'a_b_out': (48, 1, 128), 'kv_norm_g': (56, 1, 256),
    'b_q_norm_g': (64, 1, 512), 'a_conv_w': (72, CONV_KERNEL, 256),
}
SMALL_TABLE_ROWS = 104
SMALL_NAMES = tuple(SMALL_LAYOUT)


def _pack_small_grads(dg0, dg1, db0, db1, dbv, dbz, dcb, dng, dnb, dba, dgkv, dgq, dcw):
    e = CONV_WIDTH
    ce = e // N_DEV

    def body(dg0_ref, dg1_ref, db0_ref, db1_ref, dbv_ref, dbz_ref, dcb_ref, dng_ref, dnb_ref, dba_ref, dgkv_ref,
             dgq_ref, dcw_ref, o_ref):
        o_ref[...] = jnp.zeros_like(o_ref)
        for d in range(N_DEV):
            o_ref[d, 0:1, :] = dg0_ref[...]
            o_ref[d, 1:2, :] = dg1_ref[...]
            o_ref[d, 8:9, :] = db0_ref[...]
            o_ref[d, 9:10, :] = db1_ref[...]
            for q in range(3):
                col = d * 3 * ce + q * ce
                src = dbv_ref[:, col:col + ce] if col < 2 * e else dbz_ref[:, col - 2 * e:col - 2 * e + ce]
                o_ref[d, 16:17, q * ce:(q + 1) * ce] = src
            o_ref[d, 24:25, 0:ce] = dcb_ref[:, d * ce:(d + 1) * ce]
            o_ref[d, 32:33, 0:ce] = dng_ref[:, d * ce:(d + 1) * ce]
            o_ref[d, 40:41, 0:ce] = dnb_ref[:, d * ce:(d + 1) * ce]
            o_ref[d, 48:49, 0:LANES] = dba_ref[:, d * LANES:(d + 1) * LANES]
            o_ref[d, 56:57, 0:KV_LORA_RANK] = dgkv_ref[...]
            o_ref[d, 64:65, 0:Q_LORA_RANK] = dgq_ref[...]
            o_ref[d, 72:72 + HALO, 0:ce] = dcw_ref[:, d * ce:(d + 1) * ce]

    return pl.pallas_call(
        body, name="pack_small_grads",
        out_shape=jax.ShapeDtypeStruct((N_DEV, SMALL_TABLE_ROWS, SMALL_TABLE_COLS), F32),
        compiler_params=pltpu.CompilerParams(vmem_limit_bytes=VMEM_LIMIT),
    )(dg0, dg1, db0, db1, dbv, dbz, dcb, dng, dnb, dba, dgkv, dgq, dcw)


def _adamw_small(gtab, ws, ms, vs):
    n = len(SMALL_NAMES)

    def body(*refs):
        g_ref = refs[0]
        w_refs, m_refs, v_refs = refs[1:1 + n], refs[1 + n:1 + 2 * n], refs[1 + 2 * n:1 + 3 * n]
        outs = refs[1 + 3 * n:]
        for i, name in enumerate(SMALL_NAMES):
            r0, r, c = SMALL_LAYOUT[name]
            g = g_ref[0, r0:r0 + r, 0:c]
            for k in range(1, N_DEV):
                g = g + g_ref[k, r0:r0 + r, 0:c]
            delta, mn, vn = _adam_update(g, w_refs[i][...], m_refs[i][...], v_refs[i][...])
            outs[i][...] = g
            outs[n + i][...] = delta
            outs[2 * n + i][...] = mn
            outs[3 * n + i][...] = vn

    shapes = tuple(jax.ShapeDtypeStruct(w.shape, F32) for w in ws)
    res = pl.pallas_call(
        body, name="adamw_small",
        out_shape=shapes * 4,
        compiler_params=pltpu.CompilerParams(vmem_limit_bytes=VMEM_LIMIT),
    )(gtab, *ws, *ms, *vs)
    return res[:n], res[n:2 * n], res[2 * n:3 * n], res[3 * n:]


def _mesh_peers():
    x, y, c = lax.axis_index("x"), lax.axis_index("y"), lax.axis_index("c")
    me = 4 * x + 2 * y + c
    peers = []
    for k in range(1, N_DEV):
        px = 1 - x if (k >> 2) & 1 else x
        py = 1 - y if (k >> 1) & 1 else y
        pc = 1 - c if k & 1 else c
        peers.append(((px, py, pc), 4 * px + 2 * py + pc))
    return me, peers


def _all_gather(name, shards):
    na = len(shards)

    def body(*refs):
        srcs, dsts = refs[:na], refs[na:2 * na]
        send_sems, recv_sems, local_sems = refs[2 * na:]
        me, peers = _mesh_peers()
        local = [pltpu.make_async_copy(srcs[a], dsts[a].at[me], local_sems.at[a]) for a in range(na)]
        for cp in local:
            cp.start()
        sends = []
        for k, (dev, _) in enumerate(peers):
            for a in range(na):
                cp = pltpu.make_async_remote_copy(src_ref=srcs[a], dst_ref=dsts[a].at[me], send_sem=send_sems.at[a, k],
                                                  recv_sem=recv_sems.at[a, k], device_id=dev,
                                                  device_id_type=pl.DeviceIdType.MESH)
                cp.start()
                sends.append(cp)
        for k, (dev, pid) in enumerate(peers):
            for a in range(na):
                pltpu.make_async_remote_copy(src_ref=srcs[a], dst_ref=dsts[a].at[pid], send_sem=send_sems.at[a, k],
                                             recv_sem=recv_sems.at[a, k], device_id=dev,
                                             device_id_type=pl.DeviceIdType.MESH).wait_recv()
        for cp in sends:
            cp.wait_send()
        for cp in local:
            cp.wait()

    hbm = pl.BlockSpec(memory_space=pl.ANY)
    return pl.pallas_call(
        body, name=name,
        out_shape=tuple(jax.ShapeDtypeStruct((N_DEV,) + s.shape, s.dtype) for s in shards),
        in_specs=[hbm] * na, out_specs=(hbm,) * na,
        scratch_shapes=[pltpu.SemaphoreType.DMA((na, N_DEV - 1)), pltpu.SemaphoreType.DMA((na, N_DEV - 1)),
                        pltpu.SemaphoreType.DMA((na,))],
    )(*shards)


def _exchange(name, slots):
    na = len(slots)

    def body(*refs):
        srcs, dsts = refs[:na], refs[na:2 * na]
        send_sems, recv_sems, local_sems = refs[2 * na:]
        me, peers = _mesh_peers()
        local = [pltpu.make_async_copy(srcs[a].at[me], dsts[a].at[me], local_sems.at[a]) for a in range(na)]
        for cp in local:
            cp.start()
        sends = []
        for k, (dev, pid) in enumerate(peers):
            for a in range(na):
                cp = pltpu.make_async_remote_copy(src_ref=srcs[a].at[pid], dst_ref=dsts[a].at[me],
                                                  send_sem=send_sems.at[a, k], recv_sem=recv_sems.at[a, k],
                                                  device_id=dev, device_id_type=pl.DeviceIdType.MESH)
                cp.start()
                sends.append(cp)
        for k, (dev, pid) in enumerate(peers):
            for a in range(na):
                pltpu.make_async_remote_copy(src_ref=srcs[a].at[pid], dst_ref=dsts[a].at[pid],
                                             send_sem=send_sems.at[a, k], recv_sem=recv_sems.at[a, k],
                                             device_id=dev, device_id_type=pl.DeviceIdType.MESH).wait_recv()
        for cp in sends:
            cp.wait_send()
        for cp in local:
            cp.wait()

    hbm = pl.BlockSpec(memory_space=pl.ANY)
    return pl.pallas_call(
        body, name=name,
        out_shape=tuple(jax.ShapeDtypeStruct(s.shape, s.dtype) for s in slots),
        in_specs=[hbm] * na, out_specs=(hbm,) * na,
        scratch_shapes=[pltpu.SemaphoreType.DMA((na, N_DEV - 1)), pltpu.SemaphoreType.DMA((na, N_DEV - 1)),
                        pltpu.SemaphoreType.DMA((na,))],
    )(*slots)


_HBM_SPEC = pl.BlockSpec(memory_space=pltpu.HBM)
_SEM_SPEC = pl.BlockSpec(memory_space=pltpu.SEMAPHORE)


def _split_copies(srcs, lands, send_sems, recv_sems, local_sems, gather):
    me, peers = _mesh_peers()
    na = len(srcs)
    mine = lambda a, slot: srcs[a] if gather else srcs[a].at[slot]
    local = [pltpu.make_async_copy(mine(a, me), lands[a].at[me], local_sems.at[a]) for a in range(na)]
    sent, received = [], []
    for k, (dev, pid) in enumerate(peers):
        for a in range(na):
            s = a * (N_DEV - 1) + k
            sent.append(pltpu.make_async_remote_copy(
                src_ref=mine(a, pid), dst_ref=lands[a].at[me], send_sem=send_sems.at[s],
                recv_sem=recv_sems.at[s], device_id=dev, device_id_type=pl.DeviceIdType.MESH))
            received.append(pltpu.make_async_remote_copy(
                src_ref=mine(a, pid), dst_ref=lands[a].at[pid], send_sem=send_sems.at[s],
                recv_sem=recv_sems.at[s], device_id=dev, device_id_type=pl.DeviceIdType.MESH))
    return local, sent, received


def _comm_start(name, srcs, gather, after):
    na = len(srcs)
    land_structs = [jax.ShapeDtypeStruct(((N_DEV,) + s.shape) if gather else s.shape, s.dtype) for s in srcs]
    extra = [] if after is None else [after]
    n_in = 2 * na + len(extra)

    def body(*refs):
        srcs_r, lands_r = refs[:na], refs[na:2 * na]
        send_sems, recv_sems, local_sems = refs[n_in:n_in + 3]
        token = refs[-1]
        local, sent, _ = _split_copies(srcs_r, lands_r, send_sems, recv_sems, local_sems, gather)
        for cp in local + sent:
            cp.start()
        token[...] = jnp.zeros_like(token)

    sem = pltpu.SemaphoreType.DMA((na * (N_DEV - 1),))
    outs = pl.pallas_call(
        body, name=name,
        out_shape=(sem, sem, pltpu.SemaphoreType.DMA((na,)),
                   *[pltpu.HBM(s.shape, s.dtype) for s in srcs],
                   *[pltpu.HBM(s.shape, s.dtype) for s in land_structs],
                   jax.ShapeDtypeStruct((8, LANES), F32)),
        in_specs=[_HBM_SPEC] * (2 * na) + [pl.BlockSpec(memory_space=pl.ANY)] * len(extra),
        out_specs=(_SEM_SPEC, _SEM_SPEC, _SEM_SPEC, *([_HBM_SPEC] * (2 * na)), pl.BlockSpec(memory_space=pltpu.VMEM)),
        input_output_aliases={i: 3 + i for i in range(2 * na)},
        compiler_params=pltpu.CompilerParams(has_side_effects=pltpu.SideEffectType.DATAFLOW_SIDE_EFFECTING),
    )(*[pltpu.with_memory_space_constraint(s, pltpu.HBM) for s in srcs],
      *[pltpu.with_memory_space_constraint(lax.empty(s.shape, s.dtype), pltpu.HBM) for s in land_structs],
      *extra)
    return (outs[:3], outs[3:3 + na], outs[3 + na:3 + 2 * na]), outs[-1]


def _comm_wait(name, handles, gather, after):
    (send_sems, recv_sems, local_sems), srcs, lands = handles
    na = len(srcs)

    def body(*refs):
        srcs_r, lands_r = refs[:na], refs[na:2 * na]
        send_r, recv_r, local_r = refs[2 * na:2 * na + 3]
        local, sent, received = _split_copies(srcs_r, lands_r, send_r, recv_r, local_r, gather)
        for cp in sent:
            cp.wait_send()
        for cp in received:
            cp.wait_recv()
        for cp in local:
            cp.wait()

    outs = pl.pallas_call(
        body, name=name,
        out_shape=(*[pltpu.HBM(s.shape, s.dtype) for s in srcs], *[pltpu.HBM(s.shape, s.dtype) for s in lands]),
        in_specs=[_HBM_SPEC] * (2 * na) + [_SEM_SPEC] * 3 + [pl.BlockSpec(memory_space=pl.ANY)],
        out_specs=tuple([_HBM_SPEC] * (2 * na)),
        input_output_aliases={i: i for i in range(2 * na)},
        compiler_params=pltpu.CompilerParams(has_side_effects=pltpu.SideEffectType.DATAFLOW_SIDE_EFFECTING),
    )(*srcs, *lands, send_sems, recv_sems, local_sems, after)
    return outs[na:]


def _prep_weights(w):
    e = CONV_WIDTH
    p = {}
    if 'a_w_in' in w:
        if w['a_w_in'].shape == (N_DEV, D_MODEL, 3 * e // N_DEV):
            p['a_w_in3'] = w['a_w_in']
        else:
            p['a_w_in3'] = w['a_w_in'].reshape(D_MODEL, N_DEV, 3 * e // N_DEV).transpose(1, 0, 2)
        p['a_b_in'] = w['a_b_in'].reshape(1, 3 * e)
        p['a_conv_w'] = jnp.pad(w['a_conv_w'].reshape(CONV_KERNEL, e), ((0, HALO - CONV_KERNEL), (0, 0)))
        p['a_conv_b'] = w['a_conv_b'].reshape(1, e)
        p['a_norm_g'] = w['a_norm_g'].reshape(1, e)
        p['a_norm_b'] = w['a_norm_b'].reshape(1, e)
        p['a_b_out'] = w['a_b_out'].reshape(1, D_MODEL)
        p['kv_norm_g'] = w['kv_norm_g'].reshape(1, KV_LORA_RANK)
        p['b_q_norm_g'] = w['b_q_norm_g'].reshape(1, Q_LORA_RANK)
        p['ln_g'] = w['ln_g']
        p['ln_b'] = w['ln_b']
    if 'a_w_out' in w:
        p['a_w_out'] = w['a_w_out'].reshape(e, D_MODEL)
        p['kv_w_down'] = jnp.pad(w['kv_w_down'], ((0, 0), (0, KV_LORA_RANK + LANES - w['kv_w_down'].shape[1])))
        p['kv_w_ukv'] = jnp.concatenate([w['kv_w_uk'], w['kv_w_uv']], axis=2).reshape(KV_LORA_RANK, N_HEADS * HEAD_PAD)
        p['b_w_in'] = w['b_w_in'].reshape(D_MODEL, -1)
        uq = w['b_w_uq'].reshape(Q_LORA_RANK, N_HEADS, QK_NOPE_DIM + QK_ROPE_DIM)
        p['b_w_uq'] = jnp.pad(uq, ((0, 0), (0, 0), (0, HEAD_PAD - uq.shape[2]))).reshape(Q_LORA_RANK, N_HEADS * HEAD_PAD)
        p['b_w_out'] = w['b_w_out'].reshape(N_HEADS * V_HEAD_DIM, D_MODEL)
    return p


class _NoComm:
    def __init__(self):
        self.grads = {}
        self.small = None

    def first_after(self):
        return None

    def rest_weights(self, after):
        return {}

    def send_group1(self, g):
        self.grads.update(g)
        return None

    def send_group2(self, g, small):
        self.grads.update(g)
        self.small = small
        return None


def _local_step(x, positions, target, p, comm):
    t = x.shape[0]
    e = CONV_WIDTH
    freqs = ROPE_THETA ** (-jnp.arange(0, QK_ROPE_DIM, 2, dtype=F32) / QK_ROPE_DIM)
    freq_row = jnp.concatenate([freqs, freqs, jnp.zeros((LANES - QK_ROPE_DIM,), F32)]).reshape(1, LANES)
    rc, rs = _rope_tables(positions.reshape(t, 1), freq_row)
    xb = x.astype(BF16)
    ln_g0, ln_b0 = p['ln_g'][0:1], p['ln_b'][0:1]
    ln_g1, ln_b1 = p['ln_g'][1:2], p['ln_b'][1:2]

    proj = _matmul(xb, p['a_w_in3'], bias=p['a_b_in'], name="a_in", b_blocked=True, after=comm.first_after())
    u1, u2 = _conv_mid_fwd(proj, p['a_conv_w'], p['a_conv_b'], p['a_norm_g'], p['a_norm_b'])
    p = {**p, **comm.rest_weights(u2)}
    y = _matmul(u2, p['a_w_out'], bias=p['a_b_out'], name="a_out", bk=2048)
    h1, h1b, xh1, rstd1 = _ln_res_fwd(x, y, ln_g0, ln_b0)
    ckv = _matmul(h1b, p['kv_w_down'], name="kv_down")
    ckn, kr = _kv_mid_fwd(ckv, p['kv_norm_g'], rc, rs)
    kv = _matmul(ckn, p['kv_w_ukv'], name="kv_up", out_dtype=BF16)
    projb = _matmul(h1b, p['b_w_in'], name="b_in", bn=1280)
    cqn = _q_norm_fwd(projb, p['b_q_norm_g'])
    q2 = _matmul(cqn, p['b_w_uq'], name="q_up")
    qcat = _q_rope_fwd(q2, rc, rs)
    o, lse = _flash_fwd(qcat, kv, kr)
    o2 = _gate_fwd(o, projb)
    yb = _matmul(o2, p['b_w_out'], name="b_out", bk=2048)
    loss, dr2, dr2b, dg1, db1 = _final_ln_loss(h1, yb, ln_g1, ln_b1, target)

    g = {}
    g['b_w_out'] = _matmul(o2, dr2b, trans_a=True, name="d_b_out", out_dtype=BF16)
    do2 = _matmul(dr2b, p['b_w_out'], trans_b=True, name="d_o2")
    dob, dprojb, dl = _gate_bwd(do2, o, projb)
    dq2, dkv, dkr = _flash_bwd(qcat, kv, kr, dob, lse, dl, rc, rs)
    duq = _matmul(cqn, dq2, trans_a=True, name="d_q_up", out_dtype=BF16)
    dcqn = _matmul(dq2, p['b_w_uq'], trans_b=True, name="d_cqn")
    dprojb, dgq = _q_norm_bwd(dcqn, projb, p['b_q_norm_g'], dprojb)
    g['b_w_in'] = _matmul(h1b, dprojb, trans_a=True, name="d_b_in", bn=1280, out_dtype=BF16)
    t1 = _matmul(dprojb, p['b_w_in'], trans_b=True, name="d_h1_b", bk=1280)
    dukv = _matmul(ckn, dkv, trans_a=True, name="d_kv_up", out_dtype=BF16)
    dckn = _matmul(dkv, p['kv_w_ukv'], trans_b=True, name="d_ckn")
    dckv, dgkv = _kv_mid_bwd(dckn, ckv, dkr, p['kv_norm_g'], rc, rs)
    ddown = _matmul(h1b, dckv, trans_a=True, name="d_kv_down", out_dtype=BF16)
    g['b_w_out'] = g['b_w_out'].reshape(N_DEV, -1, D_MODEL)
    g['b_w_in'] = g['b_w_in'].reshape(D_MODEL, N_DEV, -1).transpose(1, 0, 2)
    g['kv_w_down'] = ddown[:, :KV_LORA_RANK + QK_ROPE_DIM].reshape(N_DEV, -1, KV_LORA_RANK + QK_ROPE_DIM)
    dukv = dukv.reshape(KV_LORA_RANK, N_HEADS, HEAD_PAD)
    g['kv_w_uk'] = dukv[:, :, :QK_NOPE_DIM].reshape(N_DEV, -1, QK_NOPE_DIM)
    g['kv_w_uv'] = dukv[:, :, QK_NOPE_DIM:].reshape(N_DEV, -1, V_HEAD_DIM)
    g['b_w_uq'] = duq.reshape(Q_LORA_RANK, N_HEADS, HEAD_PAD)[:, :, :QK_NOPE_DIM + QK_ROPE_DIM].reshape(
        N_DEV, -1, QK_NOPE_DIM + QK_ROPE_DIM)
    sent1 = comm.send_group1(g)
    t2 = _matmul(dckv, p['kv_w_down'], trans_b=True, name="d_h1_kv", after=sent1)
    dr1, dr1b, dg0, db0, dba_out = _ln0_bwd(dr2, t1, t2, xh1, rstd1, ln_g0)
    dw_out = _matmul(u2, dr1b, trans_a=True, name="d_a_out", out_dtype=BF16).reshape(N_DEV, -1, D_MODEL)
    du2 = _matmul(dr1b, p['a_w_out'], trans_b=True, name="d_u2")
    du1, dproj, dng, dnb, dbz = _conv_mid_bwd_rows(u1, proj, du2, p['a_norm_g'], p['a_norm_b'])
    dproj, dcw, dcb, dbv = _conv_bwd(du1, proj, p['a_conv_w'], dproj)
    dw_in = _matmul(xb, dproj, trans_a=True, name="d_a_in", out_dtype=BF16, bn=3 * e // N_DEV, out_blocked=True)
    small = (dg0, dg1, db0, db1, dbv, dbz, dcb, dng, dnb, dba_out, dgkv, dgq, dcw)
    sent2 = comm.send_group2({'a_w_out': dw_out, 'a_w_in': dw_in}, small)
    dxp = _matmul(dproj, p['a_w_in3'], trans_b=True, name="d_x_a", b_blocked=True, after=sent2)
    grad_x = _grad_x(dr1, dxp)
    return loss, grad_x


def _grad_x(dr1, dxp):
    t, d = dr1.shape
    tm = min(t, 512)

    def body(a_ref, b_ref, o_ref):
        o_ref[...] = ALPHA * a_ref[...] + b_ref[...]

    row = pl.BlockSpec((tm, d), lambda i: (i, 0))
    return pl.pallas_call(
        body, name="grad_x",
        out_shape=jax.ShapeDtypeStruct((t, d), F32),
        grid=(t // tm,),
        in_specs=[row, row],
        out_specs=row,
        compiler_params=_cparams(("parallel",)),
    )(dr1, dxp)


def _shard_2d(a):
    return a.reshape(-1, a.shape[-1])


def _gathered_to_full(name, blocks):
    if name == 'a_w_in':
        return blocks
    if name == 'b_w_in':
        return blocks.transpose(1, 0, 2).reshape(D_MODEL, -1)
    if name == 'a_conv_w':
        return blocks.transpose(1, 0, 2).reshape(CONV_KERNEL, -1)
    if name in ('a_b_in', 'a_conv_b', 'a_norm_g', 'a_norm_b', 'a_b_out'):
        return blocks.reshape(-1)
    if name in ('kv_w_uk', 'kv_w_uv'):
        return blocks.reshape(KV_LORA_RANK, N_HEADS, -1)
    if name == 'b_w_uq':
        return blocks.reshape(Q_LORA_RANK, N_HEADS, -1)
    return blocks.reshape(-1, blocks.shape[-1])


def kernel(x, positions, ln_g, ln_b, a_w_in, a_b_in, a_conv_w, a_conv_b, a_norm_g, a_norm_b, a_w_out, a_b_out, kv_w_down, kv_norm_g, kv_w_uk, kv_w_uv, b_w_in, b_q_norm_g, b_w_uq, b_w_out, loss_target, m_ln_g, m_ln_b, m_a_w_in, m_a_b_in, m_a_conv_w, m_a_conv_b, m_a_norm_g, m_a_norm_b, m_a_w_out, m_a_b_out, m_kv_w_down, m_kv_norm_g, m_kv_w_uk, m_kv_w_uv, m_b_w_in, m_b_q_norm_g, m_b_w_uq, m_b_w_out, v_ln_g, v_ln_b, v_a_w_in, v_a_b_in, v_a_conv_w, v_a_conv_b, v_a_norm_g, v_a_norm_b, v_a_w_out, v_a_b_out, v_kv_w_down, v_kv_norm_g, v_kv_w_uk, v_kv_w_uv, v_b_w_in, v_b_q_norm_g, v_b_w_uq, v_b_w_out):
    w = dict(ln_g=ln_g, ln_b=ln_b, a_w_in=a_w_in, a_b_in=a_b_in, a_conv_w=a_conv_w, a_conv_b=a_conv_b,
             a_norm_g=a_norm_g, a_norm_b=a_norm_b, a_w_out=a_w_out, a_b_out=a_b_out, kv_w_down=kv_w_down,
             kv_norm_g=kv_norm_g, kv_w_uk=kv_w_uk, kv_w_uv=kv_w_uv, b_w_in=b_w_in, b_q_norm_g=b_q_norm_g,
             b_w_uq=b_w_uq, b_w_out=b_w_out)
    m = dict(ln_g=m_ln_g, ln_b=m_ln_b, a_w_in=m_a_w_in, a_b_in=m_a_b_in, a_conv_w=m_a_conv_w, a_conv_b=m_a_conv_b,
             a_norm_g=m_a_norm_g, a_norm_b=m_a_norm_b, a_w_out=m_a_w_out, a_b_out=m_a_b_out, kv_w_down=m_kv_w_down,
             kv_norm_g=m_kv_norm_g, kv_w_uk=m_kv_w_uk, kv_w_uv=m_kv_w_uv, b_w_in=m_b_w_in, b_q_norm_g=m_b_q_norm_g,
             b_w_uq=m_b_w_uq, b_w_out=m_b_w_out)
    v = dict(ln_g=v_ln_g, ln_b=v_ln_b, a_w_in=v_a_w_in, a_b_in=v_a_b_in, a_conv_w=v_a_conv_w, a_conv_b=v_a_conv_b,
             a_norm_g=v_a_norm_g, a_norm_b=v_a_norm_b, a_w_out=v_a_w_out, a_b_out=v_a_b_out, kv_w_down=v_kv_w_down,
             kv_norm_g=v_kv_norm_g, kv_w_uk=v_kv_w_uk, kv_w_uv=v_kv_w_uv, b_w_in=v_b_w_in, b_q_norm_g=v_b_q_norm_g,
             b_w_uq=v_b_w_uq, b_w_out=v_b_w_out)

    small_flat = jnp.concatenate([w[n].reshape(-1) for n in SMALL_WEIGHTS]).reshape(1, -1)
    ga = _all_gather("all_gather_first", [_shard_2d(w['a_w_in']).astype(BF16), small_flat])
    full = {'a_w_in': ga[0]}
    gs = ga[1].reshape(N_DEV, -1)
    off = 0
    for n in SMALL_WEIGHTS:
        size = math.prod(w[n].shape)
        full[n] = _gathered_to_full(n, gs[:, off:off + size].reshape((N_DEV,) + _shard_2d(w[n]).shape))
        off += size
    for n in REPLICATED:
        full[n] = w[n]
    rest_names = [n for n in MATMUL_WEIGHTS if n != 'a_w_in']
    group1 = ('b_w_out', 'b_w_uq', 'b_w_in', 'kv_w_uk', 'kv_w_uv', 'kv_w_down')
    group2 = ('a_w_out', 'a_w_in')

    class Comm:
        def __init__(self):
            self.rest, self.rest_token = _comm_start(
                "gather_rest_start", [_shard_2d(w[n]).astype(BF16) for n in rest_names], True, ga[0])

        def first_after(self):
            return self.rest_token

        def rest_weights(self, after):
            lands = _comm_wait("gather_rest_wait", self.rest, True, after)
            return _prep_weights({n: _gathered_to_full(n, lands[i]) for i, n in enumerate(rest_names)})

        def send_group1(self, g):
            self.g1, token = _comm_start("exchange_g1_start", [g[n] for n in group1], False, None)
            return token

        def send_group2(self, g, small):
            self.g2, token = _comm_start("exchange_g2_start", [g[n] for n in group2] + [_pack_small_grads(*small)],
                                         False, None)
            return token

    comm = Comm()

    loss_loc, grad_x = _local_step(x[0], positions[0], loss_target[0], _prep_weights(full), comm)
    loss = lax.psum(loss_loc[0, 0], ("x", "y", "c"))

    res = {}
    recv1 = _comm_wait("exchange_g1_wait", comm.g1, False, grad_x)
    for i, n in enumerate(group1):
        res[n] = _adamw(n, recv1[i], _shard_2d(w[n]), _shard_2d(m[n]), _shard_2d(v[n]))
    recv2 = _comm_wait("exchange_g2_wait", comm.g2, False, res[group1[-1]][0])
    for i, n in enumerate(group2):
        res[n] = _adamw(n, recv2[i], _shard_2d(w[n]), _shard_2d(m[n]), _shard_2d(v[n]))
    two_d = lambda d: [_shard_2d(d[n]) if d[n].ndim > 1 else d[n].reshape(1, -1) for n in SMALL_NAMES]
    sg, sd, sm, sv = _adamw_small(recv2[-1], two_d(w), two_d(m), two_d(v))
    for i, n in enumerate(SMALL_NAMES):
        res[n] = (sg[i], sd[i], sm[i], sv[i])
    outs = [[res[n][j].reshape(w[n].shape) for n in WEIGHT_NAMES] for j in range(4)]
    return (loss, grad_x[None], *outs[0], *outs[1], *outs[2], *outs[3])
```

```python
import functools
import math

import jax
import jax.numpy as jnp
from jax import lax
from jax.experimental import pallas as pl
from jax.experimental.pallas import tpu as pltpu

F32 = jnp.float32
BF16 = jnp.bfloat16

D_MODEL = 1024
DEPTH = 2
CONV_WIDTH = 2 * D_MODEL
CONV_KERNEL = 31
N_HEADS = 16
QK_NOPE_DIM = 128
QK_ROPE_DIM = 64
V_HEAD_DIM = 128
KV_LORA_RANK = D_MODEL // 4
Q_LORA_RANK = D_MODEL // 2
ROPE_THETA = 10000.0
LN_EPS = 1e-5
RMS_EPS = 1e-6
MASK_VALUE = -1e30
ALPHA = (2.0 * DEPTH) ** 0.25
ATTN_SCALE = 1.0 / math.sqrt(QK_NOPE_DIM + QK_ROPE_DIM)

ADAM_LR = 0.001
ADAM_B1 = 0.9
ADAM_B2 = 0.999
ADAM_EPS = 1e-08
ADAM_WD = 0.01
ADAM_STEP = 10

N_DEV = 8
LANES = 128
HEAD_PAD = 256
HALO = 32
VMEM_LIMIT = 56 * 1024 * 1024

WEIGHT_NAMES = ['ln_g', 'ln_b', 'a_w_in', 'a_b_in', 'a_conv_w', 'a_conv_b', 'a_norm_g', 'a_norm_b',
                'a_w_out', 'a_b_out', 'kv_w_down', 'kv_norm_g', 'kv_w_uk', 'kv_w_uv', 'b_w_in',
                'b_q_norm_g', 'b_w_uq', 'b_w_out']
REPLICATED = ('ln_g', 'ln_b', 'kv_norm_g', 'b_q_norm_g')
MATMUL_WEIGHTS = ('a_w_in', 'a_w_out', 'kv_w_down', 'kv_w_uk', 'kv_w_uv', 'b_w_in', 'b_w_uq', 'b_w_out')
SMALL_WEIGHTS = ('a_b_in', 'a_conv_w', 'a_conv_b', 'a_norm_g', 'a_norm_b', 'a_b_out')

NN = (((1,), (0,)), ((), ()))
NT = (((1,), (1,)), ((), ()))
TN = (((0,), (0,)), ((), ()))


def _cparams(sem):
    return pltpu.CompilerParams(dimension_semantics=sem, vmem_limit_bytes=VMEM_LIMIT)


def _sigmoid(x):
    return 0.5 * jnp.tanh(0.5 * x) + 0.5


def _aligned(v, m):
    return v if isinstance(v, int) else pl.multiple_of(v, m)


def _taps_by_residue(offset_of):
    groups = {}
    for j in range(CONV_KERNEL):
        off = offset_of(j)
        groups.setdefault(off % 8, []).append((j, off // 8))
    return sorted(groups.items())


def _swap_rope_halves(t):
    lane = lax.broadcasted_iota(jnp.int32, t.shape, 1)
    return jnp.where(lane < QK_ROPE_DIM // 2, pltpu.roll(t, LANES - QK_ROPE_DIM // 2, 1),
                     pltpu.roll(t, QK_ROPE_DIM // 2, 1))


def _matmul(a, b, *, name, bias=None, trans_a=False, trans_b=False, out_dtype=F32, bm=1024, bn=1024, bk=1024,
            b_blocked=False, out_blocked=False, after=None):
    if trans_a:
        kdim, m = a.shape
    else:
        m, kdim = a.shape
    if b_blocked and trans_b:
        n, k2 = b.shape[1], b.shape[0] * b.shape[2]
        bk = b.shape[2]
    elif b_blocked:
        k2, n = b.shape[1], b.shape[0] * b.shape[2]
        bn = b.shape[2]
    elif trans_b:
        n, k2 = b.shape
    else:
        k2, n = b.shape
    assert kdim == k2, (a.shape, b.shape)
    bm, bn, bk = min(bm, m), min(bn, n), min(bk, kdim)
    assert m % bm == 0 and n % bn == 0 and kdim % bk == 0, (name, m, n, kdim, bm, bn, bk)
    nk = kdim // bk
    dims = (((0 if trans_a else 1,), (1 if trans_b else 0,)), ((), ()))
    has_bias = bias is not None

    def body(*refs):
        refs = list(refs)
        a_ref, b_ref = refs[:2]
        del refs[:2]
        bias_ref = refs.pop(0) if has_bias else None
        if after is not None:
            refs.pop(0)
        o_ref = refs.pop(0)
        rest = refs
        prod = lax.dot_general(a_ref[...], b_ref[...], dims, preferred_element_type=F32)

        def finish(acc):
            if has_bias:
                acc = acc + bias_ref[...]
            o_ref[...] = acc.astype(out_dtype)

        if nk == 1:
            finish(prod)
        else:
            acc_ref = rest[0]
            k = pl.program_id(2)

            @pl.when(k == 0)
            def _():
                acc_ref[...] = prod

            @pl.when(k > 0)
            def _():
                acc_ref[...] += prod

            @pl.when(k == nk - 1)
            def _():
                finish(acc_ref[...])

    a_spec = pl.BlockSpec((bk, bm), lambda i, j, k: (k, i)) if trans_a else pl.BlockSpec((bm, bk), lambda i, j, k: (i, k))
    if b_blocked and trans_b:
        b_spec = pl.BlockSpec((None, bn, bk), lambda i, j, k: (k, j, 0))
    elif b_blocked:
        b_spec = pl.BlockSpec((None, bk, bn), lambda i, j, k: (j, k, 0))
    elif trans_b:
        b_spec = pl.BlockSpec((bn, bk), lambda i, j, k: (j, k))
    else:
        b_spec = pl.BlockSpec((bk, bn), lambda i, j, k: (k, j))
    in_specs = [a_spec, b_spec]
    args = [a, b]
    if has_bias:
        in_specs.append(pl.BlockSpec((1, bn), lambda i, j, k: (0, j)))
        args.append(bias)
    if after is not None:
        in_specs.append(pl.BlockSpec(memory_space=pl.ANY))
        args.append(after)
    if out_blocked:
        out_struct = jax.ShapeDtypeStruct((n // bn, m, bn), out_dtype)
        out_spec = pl.BlockSpec((None, bm, bn), lambda i, j, k: (j, i, 0))
    else:
        out_struct = jax.ShapeDtypeStruct((m, n), out_dtype)
        out_spec = pl.BlockSpec((bm, bn), lambda i, j, k: (i, j))
    return pl.pallas_call(
        body, name=name,
        out_shape=out_struct,
        grid=(m // bm, n // bn, nk),
        in_specs=in_specs,
        out_specs=out_spec,
        scratch_shapes=[pltpu.VMEM((bm, bn), F32)] if nk > 1 else [],
        compiler_params=_cparams(("parallel", "parallel", "arbitrary")),
    )(*args)


def _rope_tables(pos_col, freq_row):
    t = pos_col.shape[0]
    tm = min(t, 512)

    def body(pos_ref, f_ref, c_ref, s_ref):
        ang = pos_ref[...].astype(F32) * f_ref[...]
        lane = lax.broadcasted_iota(jnp.int32, ang.shape, 1)
        cos = jnp.cos(ang)
        sin = jnp.sin(ang)
        c_ref[...] = jnp.where(lane < QK_ROPE_DIM, cos, 0.0)
        s_ref[...] = jnp.where(lane < QK_ROPE_DIM // 2, -sin, jnp.where(lane < QK_ROPE_DIM, sin, 0.0))

    return pl.pallas_call(
        body, name="rope_tables",
        out_shape=(jax.ShapeDtypeStruct((t, LANES), F32), jax.ShapeDtypeStruct((t, LANES), F32)),
        grid=(t // tm,),
        in_specs=[pl.BlockSpec((tm, 1), lambda i: (i, 0)), pl.BlockSpec((1, LANES), lambda i: (0, 0))],
        out_specs=(pl.BlockSpec((tm, LANES), lambda i: (i, 0)), pl.BlockSpec((tm, LANES), lambda i: (i, 0))),
        compiler_params=_cparams(("parallel",)),
    )(pos_col, freq_row)


def _conv_mid_fwd(proj, cw, cb, ng, nb):
    t = proj.shape[0]
    e = CONV_WIDTH
    tm = min(t, 128)
    hb = tm // HALO
    nchunk = e // LANES

    def body(val_ref, gg_ref, z_ref, valh_ref, ggh_ref, cw_ref, cb_ref, ng_ref, nb_ref, u1_ref, u2_ref, ext_ref, sh_ref):
        i = pl.program_id(0)
        u0 = val_ref[...] * _sigmoid(gg_ref[...])
        h0 = valh_ref[...] * _sigmoid(ggh_ref[...])
        ext_ref[0:HALO, :] = jnp.where(i > 0, h0, 0.0)
        ext_ref[HALO:, :] = u0

        def chunk(c, carry):
            col = pl.multiple_of(c * LANES, LANES)
            acc = jnp.zeros((tm, LANES), F32)
            for res, taps in _taps_by_residue(lambda j: HALO - (CONV_KERNEL - 1) + j):
                span = 8 * max(a for _, a in taps) + tm
                sh_ref[0:span, :] = ext_ref[res:res + span, pl.ds(col, LANES)]
                for j, a in taps:
                    acc = acc + cw_ref[j:j + 1, pl.ds(col, LANES)] * sh_ref[8 * a:8 * a + tm, :]
            u1_ref[:, pl.ds(col, LANES)] = acc + cb_ref[:, pl.ds(col, LANES)]
            return carry

        lax.fori_loop(0, nchunk, chunk, 0)
        u1 = u1_ref[...]
        mu = jnp.mean(u1, axis=-1, keepdims=True)
        xc = u1 - mu
        var = jnp.mean(xc * xc, axis=-1, keepdims=True)
        n = xc * lax.rsqrt(var + LN_EPS) * ng_ref[...] + nb_ref[...]
        z = z_ref[...]
        u2_ref[...] = ((n * _sigmoid(n)) * (z * _sigmoid(z))).astype(BF16)

    row = lambda c: pl.BlockSpec((tm, e), lambda i: (i, c))
    halo = lambda c: pl.BlockSpec((HALO, e), lambda i: (jnp.maximum(i * hb - 1, 0), c))
    par = lambda r: pl.BlockSpec((r, e), lambda i: (0, 0))
    return pl.pallas_call(
        body, name="conv_mid_fwd",
        out_shape=(jax.ShapeDtypeStruct((t, e), F32), jax.ShapeDtypeStruct((t, e), BF16)),
        grid=(t // tm,),
        in_specs=[row(0), row(1), row(2), halo(0), halo(1), par(HALO), par(1), par(1), par(1)],
        out_specs=(pl.BlockSpec((tm, e), lambda i: (i, 0)), pl.BlockSpec((tm, e), lambda i: (i, 0))),
        scratch_shapes=[pltpu.VMEM((tm + HALO, e), F32), pltpu.VMEM((tm + HALO, LANES), F32)],
        compiler_params=_cparams(("parallel",)),
    )(proj, proj, proj, proj, proj, cw, cb, ng, nb)


def _ln_res_fwd(x, y, g, b):
    t, d = x.shape
    tm = min(t, 256)

    def body(x_ref, y_ref, g_ref, b_ref, h_ref, hb_ref, xh_ref, rs_ref):
        r = ALPHA * x_ref[...] + y_ref[...]
        mu = jnp.mean(r, axis=-1, keepdims=True)
        xc = r - mu
        var = jnp.mean(xc * xc, axis=-1, keepdims=True)
        rstd = lax.rsqrt(var + LN_EPS)
        xh = xc * rstd
        h = xh * g_ref[...] + b_ref[...]
        h_ref[...] = h
        hb_ref[...] = h.astype(BF16)
        xh_ref[...] = xh
        rs_ref[...] = rstd

    row = pl.BlockSpec((tm, d), lambda i: (i, 0))
    par = pl.BlockSpec((1, d), lambda i: (0, 0))
    return pl.pallas_call(
        body, name="ln0_fwd",
        out_shape=(jax.ShapeDtypeStruct((t, d), F32), jax.ShapeDtypeStruct((t, d), BF16),
                   jax.ShapeDtypeStruct((t, d), F32), jax.ShapeDtypeStruct((t, 1), F32)),
        grid=(t // tm,),
        in_specs=[row, row, par, par],
        out_specs=(row, row, row, pl.BlockSpec((tm, 1), lambda i: (i, 0))),
        compiler_params=_cparams(("parallel",)),
    )(x, y, g, b)


def _kv_mid_fwd(ckv, g, rc, rs):
    t = ckv.shape[0]
    tm = min(t, 512)
    r = KV_LORA_RANK

    def body(ckv_ref, g_ref, c_ref, s_ref, ckn_ref, kr_ref):
        c = ckv_ref[:, 0:r]
        ms = jnp.mean(c * c, axis=-1, keepdims=True)
        ckn_ref[...] = (c * lax.rsqrt(ms + RMS_EPS) * g_ref[...]).astype(BF16)
        tr = ckv_ref[:, r:r + LANES]
        kr_ref[...] = (tr * c_ref[...] + _swap_rope_halves(tr) * s_ref[...]).astype(BF16)

    return pl.pallas_call(
        body, name="kv_mid_fwd",
        out_shape=(jax.ShapeDtypeStruct((t, r), BF16), jax.ShapeDtypeStruct((t, LANES), BF16)),
        grid=(t // tm,),
        in_specs=[pl.BlockSpec((tm, r + LANES), lambda i: (i, 0)), pl.BlockSpec((1, r), lambda i: (0, 0)),
                  pl.BlockSpec((tm, LANES), lambda i: (i, 0)), pl.BlockSpec((tm, LANES), lambda i: (i, 0))],
        out_specs=(pl.BlockSpec((tm, r), lambda i: (i, 0)), pl.BlockSpec((tm, LANES), lambda i: (i, 0))),
        compiler_params=_cparams(("parallel",)),
    )(ckv, g, rc, rs)


def _q_norm_fwd(projb, g):
    t = projb.shape[0]
    tm = min(t, 512)
    r = Q_LORA_RANK

    def body(c_ref, g_ref, o_ref):
        c = c_ref[...]
        ms = jnp.mean(c * c, axis=-1, keepdims=True)
        o_ref[...] = (c * lax.rsqrt(ms + RMS_EPS) * g_ref[...]).astype(BF16)

    return pl.pallas_call(
        body, name="q_norm_fwd",
        out_shape=jax.ShapeDtypeStruct((t, r), BF16),
        grid=(t // tm,),
        in_specs=[pl.BlockSpec((tm, r), lambda i: (i, 0)), pl.BlockSpec((1, r), lambda i: (0, 0))],
        out_specs=pl.BlockSpec((tm, r), lambda i: (i, 0)),
        compiler_params=_cparams(("parallel",)),
    )(projb, g)


def _q_up_rope(cqn, wuq, rc, rs):
    t, r = cqn.shape
    w = wuq.shape[1]
    tm = min(t, 1024)
    bn = 4 * HEAD_PAD

    def body(a_ref, b_ref, c_ref, s_ref, o_ref):
        q = lax.dot_general(a_ref[...], b_ref[...], NN, preferred_element_type=F32)
        c = c_ref[...]
        s = s_ref[...]
        for h in range(bn // HEAD_PAD):
            a = h * HEAD_PAD
            o_ref[:, a:a + LANES] = q[:, a:a + LANES].astype(BF16)
            tr = q[:, a + LANES:a + 2 * LANES]
            o_ref[:, a + LANES:a + 2 * LANES] = (tr * c + _swap_rope_halves(tr) * s).astype(BF16)

    return pl.pallas_call(
        body, name="q_up_rope",
        out_shape=jax.ShapeDtypeStruct((t, w), BF16),
        grid=(t // tm, w // bn),
        in_specs=[pl.BlockSpec((tm, r), lambda i, j: (i, 0)), pl.BlockSpec((r, bn), lambda i, j: (0, j)),
                  pl.BlockSpec((tm, LANES), lambda i, j: (i, 0)), pl.BlockSpec((tm, LANES), lambda i, j: (i, 0))],
        out_specs=pl.BlockSpec((tm, bn), lambda i, j: (i, j)),
        compiler_params=_cparams(("parallel", "parallel")),
    )(cqn, wuq, rc, rs)


def _flash_fwd(qcat, kv, kr, projb):
    t = qcat.shape[0]
    tq = min(t, 512)
    nq = t // tq
    exp2_scale = ATTN_SCALE * math.log2(math.e)
    z_first = Q_LORA_RANK // LANES

    def body(q_ref, kv_ref, kr_ref, z_ref, o_ref, o2_ref, lse_ref, kcat_ref):
        kcat_ref[:, 0:LANES] = kv_ref[:, 0:LANES]
        kcat_ref[:, LANES:] = kr_ref[...]
        rows = lax.broadcasted_iota(jnp.int32, (tq, tq), 0)
        cols = lax.broadcasted_iota(jnp.int32, (tq, tq), 1)
        for i in range(nq):
            a = i * tq
            q = q_ref[a:a + tq, :]
            sd = lax.dot_general(q, kcat_ref[a:a + tq, :], NT, preferred_element_type=F32)
            sd = jnp.where(cols <= rows, sd, MASK_VALUE)
            m = jnp.max(sd, axis=1, keepdims=True)
            if i > 0:
                sp = lax.dot_general(q, kcat_ref[0:a, :], NT, preferred_element_type=F32)
                m = jnp.maximum(m, jnp.max(sp, axis=1, keepdims=True))
            pd = jnp.exp2((sd - m) * exp2_scale)
            l = jnp.sum(pd, axis=1, keepdims=True)
            acc = lax.dot_general(pd.astype(BF16), kv_ref[a:a + tq, LANES:], NN, preferred_element_type=F32)
            if i > 0:
                pp = jnp.exp2((sp - m) * exp2_scale)
                l = l + jnp.sum(pp, axis=1, keepdims=True)
                acc = acc + lax.dot_general(pp.astype(BF16), kv_ref[0:a, LANES:], NN, preferred_element_type=F32)
            o = acc / l
            z = z_ref[a:a + tq, :]
            o_ref[a:a + tq, :] = o
            o2_ref[a:a + tq, :] = (o * (z * _sigmoid(z))).astype(BF16)
            lse_ref[a:a + tq, :] = jnp.broadcast_to(m * ATTN_SCALE + jnp.log(l), (tq, LANES))

    hw = N_HEADS * LANES
    head256 = pl.BlockSpec((t, HEAD_PAD), lambda h: (0, h))
    head128 = pl.BlockSpec((t, LANES), lambda h: (0, h))
    return pl.pallas_call(
        body, name="flash_fwd",
        out_shape=(jax.ShapeDtypeStruct((t, hw), F32), jax.ShapeDtypeStruct((t, hw), BF16),
                   jax.ShapeDtypeStruct((t, hw), F32)),
        grid=(N_HEADS,),
        in_specs=[head256, head256, pl.BlockSpec((t, LANES), lambda h: (0, 0), pipeline_mode=pl.Buffered(1)),
                  pl.BlockSpec((t, LANES), lambda h: (0, z_first + h))],
        out_specs=(head128, head128, head128),
        scratch_shapes=[pltpu.VMEM((t, HEAD_PAD), BF16)],
        compiler_params=_cparams(("parallel",)),
    )(qcat, kv, kr, projb)


def _final_ln_loss(h1, yb, g, b, target):
    t, d = h1.shape
    tm = min(t, 256)

    def body(h_ref, y_ref, g_ref, b_ref, t_ref, loss_ref, dr_ref, drb_ref, dg_ref, db_ref):
        i = pl.program_id(0)
        r = ALPHA * h_ref[...] + y_ref[...]
        mu = jnp.mean(r, axis=-1, keepdims=True)
        xc = r - mu
        var = jnp.mean(xc * xc, axis=-1, keepdims=True)
        rstd = lax.rsqrt(var + LN_EPS)
        xh = xc * rstd
        g = g_ref[...]
        diff = xh * g + b_ref[...] - t_ref[...]
        part = 0.5 * jnp.sum(jnp.mean(diff * diff, axis=-1, keepdims=True))
        dh = diff * (1.0 / d)
        dgp = jnp.sum(dh * xh, axis=0, keepdims=True)
        dbp = jnp.sum(dh, axis=0, keepdims=True)

        @pl.when(i == 0)
        def _():
            loss_ref[...] = jnp.zeros_like(loss_ref)
            dg_ref[...] = jnp.zeros_like(dg_ref)
            db_ref[...] = jnp.zeros_like(db_ref)

        loss_ref[...] += jnp.full(loss_ref.shape, part, F32)
        dg_ref[...] += dgp
        db_ref[...] += dbp
        dxh = dh * g
        dr = rstd * (dxh - jnp.mean(dxh, axis=-1, keepdims=True) - xh * jnp.mean(dxh * xh, axis=-1, keepdims=True))
        dr_ref[...] = dr
        drb_ref[...] = dr.astype(BF16)

    row = pl.BlockSpec((tm, d), lambda i: (i, 0))
    par = pl.BlockSpec((1, d), lambda i: (0, 0))
    return pl.pallas_call(
        body, name="final_ln_loss",
        out_shape=(jax.ShapeDtypeStruct((1, LANES), F32), jax.ShapeDtypeStruct((t, d), F32),
                   jax.ShapeDtypeStruct((t, d), BF16), jax.ShapeDtypeStruct((1, d), F32),
                   jax.ShapeDtypeStruct((1, d), F32)),
        grid=(t // tm,),
        in_specs=[row, row, par, par, row],
        out_specs=(pl.BlockSpec((1, LANES), lambda i: (0, 0)), row, row, par, par),
        compiler_params=_cparams(("arbitrary",)),
    )(h1, yb, g, b, target)


def _gate_bwd(do2, o, projb):
    t, hw = o.shape
    tm = min(t, 512)
    bw = Q_LORA_RANK
    nb = hw // bw
    wb = projb.shape[1]

    def body(d_ref, o_ref, z_ref, dob_ref, dz_ref, dl_ref):
        z = z_ref[...]
        sg = _sigmoid(z)
        d2 = d_ref[...]
        o = o_ref[...]
        do = d2 * (z * sg)
        dob_ref[...] = do.astype(BF16)
        dz_ref[...] = (d2 * o * (sg * (1.0 + z * (1.0 - sg)))).astype(BF16)
        prod = do * o
        for hh in range(bw // LANES):
            dsum = jnp.sum(prod[:, hh * LANES:(hh + 1) * LANES], axis=1, keepdims=True)
            dl_ref[:, hh * LANES:(hh + 1) * LANES] = jnp.broadcast_to(dsum, (tm, LANES))

    blk = pl.BlockSpec((tm, bw), lambda i, j: (i, j))
    blk1 = pl.BlockSpec((tm, bw), lambda i, j: (i, j + 1))
    return pl.pallas_call(
        body, name="gate_bwd",
        out_shape=(jax.ShapeDtypeStruct((t, hw), BF16), jax.ShapeDtypeStruct((t, wb), BF16),
                   jax.ShapeDtypeStruct((t, hw), F32)),
        grid=(t // tm, nb),
        in_specs=[blk, blk, blk1],
        out_specs=(blk, blk1, blk),
        compiler_params=_cparams(("parallel", "parallel")),
    )(do2, o, projb)


def _flash_bwd(qcat, kv, kr, dob, lse, dl, rc, rs):
    t = qcat.shape[0]
    tq = min(t, 512)
    nq = t // tq
    q_chunk = 2 * tq
    log2e = math.log2(math.e)
    exp2_scale = ATTN_SCALE * log2e
    once = pl.Buffered(1)

    def body(q_ref, kv_ref, kr_ref, do_ref, lse_ref, dl_ref, c_ref, s_ref,
             dq_ref, dkv_ref, dkr_ref, kcat_ref, dqa_ref, dka_ref, dva_ref):
        h = pl.program_id(0)
        kcat_ref[:, 0:LANES] = kv_ref[:, 0:LANES]
        kcat_ref[:, LANES:] = kr_ref[...]
        dqa_ref[...] = jnp.zeros_like(dqa_ref)

        @pl.when(h == 0)
        def _():
            dkr_ref[...] = jnp.zeros_like(dkr_ref)

        rows = lax.broadcasted_iota(jnp.int32, (tq, tq), 0)
        cols = lax.broadcasted_iota(jnp.int32, (tq, tq), 1)

        def block(qs, n, ks, masked):
            q = q_ref[qs:qs + n, :]
            kc = kcat_ref[ks:ks + tq, :]
            s = lax.dot_general(q, kc, NT, preferred_element_type=F32)
            if masked:
                s = jnp.where(cols <= rows, s, MASK_VALUE)
            p = jnp.exp2(s * exp2_scale - lse_ref[qs:qs + n, 0:1] * log2e)
            do = do_ref[qs:qs + n, :]
            dp = lax.dot_general(do, kv_ref[ks:ks + tq, LANES:], NT, preferred_element_type=F32)
            ds = (p * (dp - dl_ref[qs:qs + n, 0:1])).astype(BF16)
            dva_ref[...] += lax.dot_general(p.astype(BF16), do, TN, preferred_element_type=F32)
            dka_ref[...] += lax.dot_general(ds, q, TN, preferred_element_type=F32)
            dqa_ref[qs:qs + n, :] += lax.dot_general(ds, kc, NN, preferred_element_type=F32)

        for kb in range(nq):
            a = kb * tq
            dka_ref[...] = jnp.zeros_like(dka_ref)
            dva_ref[...] = jnp.zeros_like(dva_ref)
            block(a, tq, a, True)
            qs = a + tq
            while qs < t:
                n = min(q_chunk, t - qs)
                block(qs, n, a, False)
                qs += n
            dk = dka_ref[...] * ATTN_SCALE
            dkv_ref[a:a + tq, 0:LANES] = dk[:, 0:LANES].astype(BF16)
            dkv_ref[a:a + tq, LANES:] = dva_ref[...].astype(BF16)
            dkr_ref[a:a + tq, :] += dk[:, LANES:]

        dq = dqa_ref[...] * ATTN_SCALE
        dq_ref[:, 0:LANES] = dq[:, 0:LANES].astype(BF16)
        d2 = dq[:, LANES:]
        dq_ref[:, LANES:] = (d2 * c_ref[...] - _swap_rope_halves(d2) * s_ref[...]).astype(BF16)

    hp = N_HEADS * HEAD_PAD
    head256 = pl.BlockSpec((t, HEAD_PAD), lambda h: (0, h))
    head128 = pl.BlockSpec((t, LANES), lambda h: (0, h))
    const128 = pl.BlockSpec((t, LANES), lambda h: (0, 0), pipeline_mode=once)
    return pl.pallas_call(
        body, name="flash_bwd",
        out_shape=(jax.ShapeDtypeStruct((t, hp), BF16), jax.ShapeDtypeStruct((t, hp), BF16),
                   jax.ShapeDtypeStruct((t, LANES), F32)),
        grid=(N_HEADS,),
        in_specs=[head256, head256, const128, head128, head128, head128, const128, const128],
        out_specs=(head256, head256, pl.BlockSpec((t, LANES), lambda h: (0, 0))),
        scratch_shapes=[pltpu.VMEM((t, HEAD_PAD), BF16), pltpu.VMEM((t, HEAD_PAD), F32),
                        pltpu.VMEM((tq, HEAD_PAD), F32), pltpu.VMEM((tq, LANES), F32)],
        compiler_params=_cparams(("arbitrary",)),
    )(qcat, kv, kr, dob, lse, dl, rc, rs)


def _q_norm_bwd(dcqn, projb, g, dprojb):
    t = projb.shape[0]
    tm = min(t, 512)
    r = Q_LORA_RANK

    def body(d_ref, c_ref, g_ref, alias_ref, o_ref, dg_ref):
        i = pl.program_id(0)
        c = c_ref[...]
        d = d_ref[...]
        rr = lax.rsqrt(jnp.mean(c * c, axis=-1, keepdims=True) + RMS_EPS)
        xh = c * rr

        @pl.when(i == 0)
        def _():
            dg_ref[...] = jnp.zeros_like(dg_ref)

        dg_ref[...] += jnp.sum(d * xh, axis=0, keepdims=True)
        dxh = d * g_ref[...]
        o_ref[...] = (rr * (dxh - xh * jnp.mean(dxh * xh, axis=-1, keepdims=True))).astype(BF16)

    blk = pl.BlockSpec((tm, r), lambda i: (i, 0))
    par = pl.BlockSpec((1, r), lambda i: (0, 0))
    return pl.pallas_call(
        body, name="q_norm_bwd",
        out_shape=(jax.ShapeDtypeStruct(dprojb.shape, BF16), jax.ShapeDtypeStruct((1, r), F32)),
        grid=(t // tm,),
        in_specs=[blk, blk, par, pl.BlockSpec(memory_space=pl.ANY)],
        out_specs=(blk, par),
        input_output_aliases={3: 0},
        compiler_params=_cparams(("arbitrary",)),
    )(dcqn, projb, g, dprojb)


def _kv_mid_bwd(dckn, ckv, dkr, g, rc, rs):
    t = ckv.shape[0]
    tm = min(t, 512)
    r = KV_LORA_RANK

    def body(d_ref, ckv_ref, dkr_ref, g_ref, c_ref, s_ref, o_ref, dg_ref):
        i = pl.program_id(0)
        c = ckv_ref[:, 0:r]
        d = d_ref[...]
        rr = lax.rsqrt(jnp.mean(c * c, axis=-1, keepdims=True) + RMS_EPS)
        xh = c * rr

        @pl.when(i == 0)
        def _():
            dg_ref[...] = jnp.zeros_like(dg_ref)

        dg_ref[...] += jnp.sum(d * xh, axis=0, keepdims=True)
        dxh = d * g_ref[...]
        o_ref[:, 0:r] = (rr * (dxh - xh * jnp.mean(dxh * xh, axis=-1, keepdims=True))).astype(BF16)
        dk = dkr_ref[...]
        o_ref[:, r:] = (dk * c_ref[...] - _swap_rope_halves(dk) * s_ref[...]).astype(BF16)

    return pl.pallas_call(
        body, name="kv_mid_bwd",
        out_shape=(jax.ShapeDtypeStruct((t, r + LANES), BF16), jax.ShapeDtypeStruct((1, r), F32)),
        grid=(t // tm,),
        in_specs=[pl.BlockSpec((tm, r), lambda i: (i, 0)), pl.BlockSpec((tm, r + LANES), lambda i: (i, 0)),
                  pl.BlockSpec((tm, LANES), lambda i: (i, 0)), pl.BlockSpec((1, r), lambda i: (0, 0)),
                  pl.BlockSpec((tm, LANES), lambda i: (i, 0)), pl.BlockSpec((tm, LANES), lambda i: (i, 0))],
        out_specs=(pl.BlockSpec((tm, r + LANES), lambda i: (i, 0)), pl.BlockSpec((1, r), lambda i: (0, 0))),
        compiler_params=_cparams(("arbitrary",)),
    )(dckn, ckv, dkr, g, rc, rs)


def _ln0_bwd(dr2, t1, t2, xh, rstd, g):
    t, d = dr2.shape
    tm = min(t, 256)

    def body(a_ref, b_ref, c_ref, xh_ref, rs_ref, g_ref, dr_ref, drb_ref, dg_ref, db_ref, dsum_ref):
        i = pl.program_id(0)
        dh = ALPHA * a_ref[...] + b_ref[...] + c_ref[...]
        xh = xh_ref[...]

        @pl.when(i == 0)
        def _():
            dg_ref[...] = jnp.zeros_like(dg_ref)
            db_ref[...] = jnp.zeros_like(db_ref)
            dsum_ref[...] = jnp.zeros_like(dsum_ref)

        dg_ref[...] += jnp.sum(dh * xh, axis=0, keepdims=True)
        db_ref[...] += jnp.sum(dh, axis=0, keepdims=True)
        dxh = dh * g_ref[...]
        dr = rs_ref[...] * (dxh - jnp.mean(dxh, axis=-1, keepdims=True) - xh * jnp.mean(dxh * xh, axis=-1, keepdims=True))
        dr_ref[...] = dr
        drb_ref[...] = dr.astype(BF16)
        dsum_ref[...] += jnp.sum(dr, axis=0, keepdims=True)

    row = pl.BlockSpec((tm, d), lambda i: (i, 0))
    par = pl.BlockSpec((1, d), lambda i: (0, 0))
    return pl.pallas_call(
        body, name="ln0_bwd",
        out_shape=(jax.ShapeDtypeStruct((t, d), F32), jax.ShapeDtypeStruct((t, d), BF16),
                   jax.ShapeDtypeStruct((1, d), F32), jax.ShapeDtypeStruct((1, d), F32),
                   jax.ShapeDtypeStruct((1, d), F32)),
        grid=(t // tm,),
        in_specs=[row, row, row, row, pl.BlockSpec((tm, 1), lambda i: (i, 0)), par],
        out_specs=(row, row, par, par, par),
        compiler_params=_cparams(("arbitrary",)),
    )(dr2, t1, t2, xh, rstd, g)


def _conv_mid_bwd_rows(u1, proj, du2, ng, nb):
    t = u1.shape[0]
    e = CONV_WIDTH
    tm = min(t, 256)

    def body(u1_ref, z_ref, d_ref, ng_ref, nb_ref, du1_ref, dz_ref, dng_ref, dnb_ref, dbz_ref):
        i = pl.program_id(0)
        u1 = u1_ref[...]
        mu = jnp.mean(u1, axis=-1, keepdims=True)
        xc = u1 - mu
        rstd = lax.rsqrt(jnp.mean(xc * xc, axis=-1, keepdims=True) + LN_EPS)
        xh = xc * rstd
        g = ng_ref[...]
        n = xh * g + nb_ref[...]
        sn = _sigmoid(n)
        z = z_ref[...]
        sz = _sigmoid(z)
        du2 = d_ref[...]
        dz = du2 * (n * sn) * (sz * (1.0 + z * (1.0 - sz)))
        dn = du2 * (z * sz) * (sn * (1.0 + n * (1.0 - sn)))

        @pl.when(i == 0)
        def _():
            dng_ref[...] = jnp.zeros_like(dng_ref)
            dnb_ref[...] = jnp.zeros_like(dnb_ref)
            dbz_ref[...] = jnp.zeros_like(dbz_ref)

        dng_ref[...] += jnp.sum(dn * xh, axis=0, keepdims=True)
        dnb_ref[...] += jnp.sum(dn, axis=0, keepdims=True)
        dbz_ref[...] += jnp.sum(dz, axis=0, keepdims=True)
        dz_ref[...] = dz.astype(BF16)
        dxh = dn * g
        du1_ref[...] = rstd * (dxh - jnp.mean(dxh, axis=-1, keepdims=True) - xh * jnp.mean(dxh * xh, axis=-1, keepdims=True))

    row = pl.BlockSpec((tm, e), lambda i: (i, 0))
    par = pl.BlockSpec((1, e), lambda i: (0, 0))
    return pl.pallas_call(
        body, name="conv_mid_bwd_rows",
        out_shape=(jax.ShapeDtypeStruct((t, e), F32), jax.ShapeDtypeStruct((t, 3 * e), BF16),
                   jax.ShapeDtypeStruct((1, e), F32), jax.ShapeDtypeStruct((1, e), F32),
                   jax.ShapeDtypeStruct((1, e), F32)),
        grid=(t // tm,),
        in_specs=[row, pl.BlockSpec((tm, e), lambda i: (i, 2)), row, par, par],
        out_specs=(row, pl.BlockSpec((tm, e), lambda i: (i, 2)), par, par, par),
        compiler_params=_cparams(("arbitrary",)),
    )(u1, proj, du2, ng, nb)


def _conv_bwd(du1, proj, cw, dproj):
    t = du1.shape[0]
    e = CONV_WIDTH
    tm = min(t, 128)
    hb = tm // HALO
    nt = t // tm
    nchunk = e // LANES
    last_halo = t // HALO - 1

    def body(d_ref, dn_ref, val_ref, gg_ref, valh_ref, ggh_ref, cw_ref, alias_ref,
             dp_ref, dcw_ref, dcb_ref, dbv_ref, extu_ref, extd_ref, du0_ref, acc_ref, sh_ref):
        i = pl.program_id(0)
        val = val_ref[...]
        sg = _sigmoid(gg_ref[...])
        h0 = valh_ref[...] * _sigmoid(ggh_ref[...])
        extu_ref[0:HALO, :] = jnp.where(i > 0, h0, 0.0)
        extu_ref[HALO:, :] = val * sg
        extd_ref[0:tm, :] = d_ref[...]
        extd_ref[tm:, :] = jnp.where(i < nt - 1, dn_ref[...], 0.0)

        @pl.when(i == 0)
        def _():
            acc_ref[...] = jnp.zeros_like(acc_ref)
            dbv_ref[...] = jnp.zeros_like(dbv_ref)

        def chunk(c, carry):
            col = pl.multiple_of(c * LANES, LANES)
            d = extd_ref[0:tm, pl.ds(col, LANES)]
            du0 = jnp.zeros((tm, LANES), F32)
            for res, taps in _taps_by_residue(lambda j: CONV_KERNEL - 1 - j):
                span = 8 * max(a for _, a in taps) + tm
                sh_ref[0:span, :] = extd_ref[res:res + span, pl.ds(col, LANES)]
                for j, a in taps:
                    du0 = du0 + cw_ref[j:j + 1, pl.ds(col, LANES)] * sh_ref[8 * a:8 * a + tm, :]
            for res, taps in _taps_by_residue(lambda j: HALO - (CONV_KERNEL - 1) + j):
                span = 8 * max(a for _, a in taps) + tm
                sh_ref[0:span, :] = extu_ref[res:res + span, pl.ds(col, LANES)]
                for j, a in taps:
                    prod = d * sh_ref[8 * a:8 * a + tm, :]
                    acc_ref[j * 8:(j + 1) * 8, pl.ds(col, LANES)] += jnp.sum(prod.reshape(tm // 8, 8, LANES), axis=0)
            acc_ref[CONV_KERNEL * 8:(CONV_KERNEL + 1) * 8, pl.ds(col, LANES)] += jnp.sum(d.reshape(tm // 8, 8, LANES), axis=0)
            du0_ref[:, pl.ds(col, LANES)] = du0
            return carry

        lax.fori_loop(0, nchunk, chunk, 0)
        du0 = du0_ref[...]
        dval = du0 * sg
        dgg = du0 * val * sg * (1.0 - sg)
        dp_ref[:, 0:e] = dval.astype(BF16)
        dp_ref[:, e:] = dgg.astype(BF16)
        dbv_ref[:, 0:e] += jnp.sum(dval, axis=0, keepdims=True)
        dbv_ref[:, e:] += jnp.sum(dgg, axis=0, keepdims=True)

        @pl.when(i == nt - 1)
        def _():
            for j in range(CONV_KERNEL):
                dcw_ref[j:j + 1, :] = jnp.sum(acc_ref[j * 8:(j + 1) * 8, :], axis=0, keepdims=True)
            dcw_ref[CONV_KERNEL:, :] = jnp.zeros((HALO - CONV_KERNEL, e), F32)
            dcb_ref[...] = jnp.sum(acc_ref[CONV_KERNEL * 8:(CONV_KERNEL + 1) * 8, :], axis=0, keepdims=True)

    row = lambda c: pl.BlockSpec((tm, e), lambda i: (i, c))
    halo_prev = lambda c: pl.BlockSpec((HALO, e), lambda i: (jnp.maximum(i * hb - 1, 0), c))
    halo_next = pl.BlockSpec((HALO, e), lambda i: (jnp.minimum((i + 1) * hb, last_halo), 0))
    return pl.pallas_call(
        body, name="conv_bwd",
        out_shape=(jax.ShapeDtypeStruct(dproj.shape, BF16), jax.ShapeDtypeStruct((HALO, e), F32),
                   jax.ShapeDtypeStruct((1, e), F32), jax.ShapeDtypeStruct((1, 2 * e), F32)),
        grid=(nt,),
        in_specs=[row(0), halo_next, row(0), row(1), halo_prev(0), halo_prev(1),
                  pl.BlockSpec((HALO, e), lambda i: (0, 0)), pl.BlockSpec(memory_space=pl.ANY)],
        out_specs=(pl.BlockSpec((tm, 2 * e), lambda i: (i, 0)), pl.BlockSpec((HALO, e), lambda i: (0, 0)),
                   pl.BlockSpec((1, e), lambda i: (0, 0)), pl.BlockSpec((1, 2 * e), lambda i: (0, 0))),
        scratch_shapes=[pltpu.VMEM((tm + HALO, e), F32), pltpu.VMEM((tm + HALO, e), F32),
                        pltpu.VMEM((tm, e), F32), pltpu.VMEM(((CONV_KERNEL + 1) * 8, e), F32),
                        pltpu.VMEM((tm + HALO, LANES), F32)],
        input_output_aliases={7: 0},
        compiler_params=_cparams(("arbitrary",)),
    )(du1, du1, proj, proj, proj, proj, cw, dproj)


def _adam_update(g, w, m, v):
    c1 = 1.0 - ADAM_B1 ** ADAM_STEP
    c2 = 1.0 - ADAM_B2 ** ADAM_STEP
    mn = ADAM_B1 * m + (1.0 - ADAM_B1) * g
    vn = ADAM_B2 * v + (1.0 - ADAM_B2) * (g * g)
    delta = -ADAM_LR * ((mn / c1) / (jnp.sqrt(vn / c2) + ADAM_EPS) + ADAM_WD * w)
    return delta, mn, vn


def _adamw(name, gbuf, w, m, v):
    r, c = w.shape
    tr = min(r, 256)
    assert r % tr == 0

    def body(g_ref, w_ref, m_ref, v_ref, go_ref, d_ref, mo_ref, vo_ref):
        g = g_ref[0].astype(F32)
        for k in range(1, N_DEV):
            g = g + g_ref[k].astype(F32)
        go_ref[...] = g
        d_ref[...], mo_ref[...], vo_ref[...] = _adam_update(g, w_ref[...], m_ref[...], v_ref[...])

    blk = pl.BlockSpec((tr, c), lambda i: (i, 0))
    shp = jax.ShapeDtypeStruct((r, c), F32)
    return pl.pallas_call(
        body, name="adamw_" + name,
        out_shape=(shp, shp, shp, shp),
        grid=(r // tr,),
        in_specs=[pl.BlockSpec((N_DEV, tr, c), lambda i: (0, i, 0)), blk, blk, blk],
        out_specs=(blk, blk, blk, blk),
        compiler_params=_cparams(("parallel",)),
    )(gbuf, w, m, v)


SMALL_TABLE_COLS = 1024
SMALL_LAYOUT = {
    'ln_g': (0, 2, 1024), 'ln_b': (8, 2, 1024), 'a_b_in': (16, 1, 768), 'a_conv_b': (24, 1, 256),
    'a_norm_g': (32, 1, 256), 'a_norm_b': (40, 1, 256), 'a_b_out': (48, 1, 128), 'kv_norm_g': (56, 1, 256),
    'b_q_norm_g': (64, 1, 512), 'a_conv_w': (72, CONV_KERNEL, 256),
}
SMALL_TABLE_ROWS = 104
SMALL_NAMES = tuple(SMALL_LAYOUT)


def _pack_small_grads(dg0, dg1, db0, db1, dbv, dbz, dcb, dng, dnb, dba, dgkv, dgq, dcw):
    e = CONV_WIDTH
    ce = e // N_DEV

    def body(dg0_ref, dg1_ref, db0_ref, db1_ref, dbv_ref, dbz_ref, dcb_ref, dng_ref, dnb_ref, dba_ref, dgkv_ref,
             dgq_ref, dcw_ref, o_ref):
        o_ref[...] = jnp.zeros_like(o_ref)
        for d in range(N_DEV):
            o_ref[d, 0:1, :] = dg0_ref[...]
            o_ref[d, 1:2, :] = dg1_ref[...]
            o_ref[d, 8:9, :] = db0_ref[...]
            o_ref[d, 9:10, :] = db1_ref[...]
            for q in range(3):
                col = d * 3 * ce + q * ce
                src = dbv_ref[:, col:col + ce] if col < 2 * e else dbz_ref[:, col - 2 * e:col - 2 * e + ce]
                o_ref[d, 16:17, q * ce:(q + 1) * ce] = src
            o_ref[d, 24:25, 0:ce] = dcb_ref[:, d * ce:(d + 1) * ce]
            o_ref[d, 32:33, 0:ce] = dng_ref[:, d * ce:(d + 1) * ce]
            o_ref[d, 40:41, 0:ce] = dnb_ref[:, d * ce:(d + 1) * ce]
            o_ref[d, 48:49, 0:LANES] = dba_ref[:, d * LANES:(d + 1) * LANES]
            o_ref[d, 56:57, 0:KV_LORA_RANK] = dgkv_ref[...]
            o_ref[d, 64:65, 0:Q_LORA_RANK] = dgq_ref[...]
            o_ref[d, 72:72 + HALO, 0:ce] = dcw_ref[:, d * ce:(d + 1) * ce]

    return pl.pallas_call(
        body, name="pack_small_grads",
        out_shape=jax.ShapeDtypeStruct((N_DEV, SMALL_TABLE_ROWS, SMALL_TABLE_COLS), F32),
        compiler_params=pltpu.CompilerParams(vmem_limit_bytes=VMEM_LIMIT),
    )(dg0, dg1, db0, db1, dbv, dbz, dcb, dng, dnb, dba, dgkv, dgq, dcw)


def _adamw_small(gtab, ws, ms, vs):
    n = len(SMALL_NAMES)

    def body(*refs):
        g_ref = refs[0]
        w_refs, m_refs, v_refs = refs[1:1 + n], refs[1 + n:1 + 2 * n], refs[1 + 2 * n:1 + 3 * n]
        outs = refs[1 + 3 * n:]
        for i, name in enumerate(SMALL_NAMES):
            r0, r, c = SMALL_LAYOUT[name]
            g = g_ref[0, r0:r0 + r, 0:c]
            for k in range(1, N_DEV):
                g = g + g_ref[k, r0:r0 + r, 0:c]
            delta, mn, vn = _adam_update(g, w_refs[i][...], m_refs[i][...], v_refs[i][...])
            outs[i][...] = g
            outs[n + i][...] = delta
            outs[2 * n + i][...] = mn
            outs[3 * n + i][...] = vn

    shapes = tuple(jax.ShapeDtypeStruct(w.shape, F32) for w in ws)
    res = pl.pallas_call(
        body, name="adamw_small",
        out_shape=shapes * 4,
        compiler_params=pltpu.CompilerParams(vmem_limit_bytes=VMEM_LIMIT),
    )(gtab, *ws, *ms, *vs)
    return res[:n], res[n:2 * n], res[2 * n:3 * n], res[3 * n:]


def _mesh_peers():
    x, y, c = lax.axis_index("x"), lax.axis_index("y"), lax.axis_index("c")
    me = 4 * x + 2 * y + c
    peers = []
    for k in range(1, N_DEV):
        px = 1 - x if (k >> 2) & 1 else x
        py = 1 - y if (k >> 1) & 1 else y
        pc = 1 - c if k & 1 else c
        peers.append(((px, py, pc), 4 * px + 2 * py + pc))
    return me, peers


def _all_gather(name, shards):
    na = len(shards)

    def body(*refs):
        srcs, dsts = refs[:na], refs[na:2 * na]
        send_sems, recv_sems, local_sems = refs[2 * na:]
        x, y, c = lax.axis_index("x"), lax.axis_index("y"), lax.axis_index("c")
        me, sibling = (x, y, c), (x, y, 1 - c)
        chips = [(1 - x, y), (x, 1 - y), (1 - x, 1 - y)]
        slot = lambda dev: 4 * dev[0] + 2 * dev[1] + dev[2]

        def copy(a, k, block, to, own=False):
            return pltpu.make_async_remote_copy(
                src_ref=srcs[a] if own else dsts[a].at[slot(block)], dst_ref=dsts[a].at[slot(block)],
                send_sem=send_sems.at[a, k], recv_sem=recv_sems.at[a, k], device_id=to,
                device_id_type=pl.DeviceIdType.MESH)

        local = [pltpu.make_async_copy(srcs[a], dsts[a].at[slot(me)], local_sems.at[a]) for a in range(na)]
        for cp in local:
            cp.start()
        sends = []
        for a in range(na):
            sends.append(copy(a, 0, me, sibling, own=True))
            sends += [copy(a, 1 + j, me, (*chip, c), own=True) for j, chip in enumerate(chips)]
        for cp in sends:
            cp.start()
        for j, chip in enumerate(chips):
            for a in range(na):
                copy(a, 1 + j, (*chip, c), me).wait_recv()
                fwd = copy(a, 4 + j, (*chip, c), sibling)
                fwd.start()
                sends.append(fwd)
        for a in range(na):
            copy(a, 0, sibling, me).wait_recv()
            for j, chip in enumerate(chips):
                copy(a, 4 + j, (*chip, 1 - c), me).wait_recv()
        for cp in sends:
            cp.wait_send()
        for cp in local:
            cp.wait()

    hbm = pl.BlockSpec(memory_space=pl.ANY)
    return pl.pallas_call(
        body, name=name,
        out_shape=tuple(jax.ShapeDtypeStruct((N_DEV,) + s.shape, s.dtype) for s in shards),
        in_specs=[hbm] * na, out_specs=(hbm,) * na,
        scratch_shapes=[pltpu.SemaphoreType.DMA((na, N_DEV - 1)), pltpu.SemaphoreType.DMA((na, N_DEV - 1)),
                        pltpu.SemaphoreType.DMA((na,))],
    )(*shards)


_HBM_SPEC = pl.BlockSpec(memory_space=pltpu.HBM)
_SEM_SPEC = pl.BlockSpec(memory_space=pltpu.SEMAPHORE)


def _split_copies(srcs, lands, send_sems, recv_sems, local_sems, gather):
    me, peers = _mesh_peers()
    na = len(srcs)
    mine = lambda a, slot: srcs[a] if gather else srcs[a].at[slot]
    local = [pltpu.make_async_copy(mine(a, me), lands[a].at[me], local_sems.at[a]) for a in range(na)]
    sent, received = [], []
    for k, (dev, pid) in enumerate(peers):
        for a in range(na):
            s = a * (N_DEV - 1) + k
            sent.append(pltpu.make_async_remote_copy(
                src_ref=mine(a, pid), dst_ref=lands[a].at[me], send_sem=send_sems.at[s],
                recv_sem=recv_sems.at[s], device_id=dev, device_id_type=pl.DeviceIdType.MESH))
            received.append(pltpu.make_async_remote_copy(
                src_ref=mine(a, pid), dst_ref=lands[a].at[pid], send_sem=send_sems.at[s],
                recv_sem=recv_sems.at[s], device_id=dev, device_id_type=pl.DeviceIdType.MESH))
    return local, sent, received


def _comm_start(name, srcs, gather, after):
    na = len(srcs)
    land_structs = [jax.ShapeDtypeStruct(((N_DEV,) + s.shape) if gather else s.shape, s.dtype) for s in srcs]
    extra = [] if after is None else [after]
    n_in = 2 * na + len(extra)

    def body(*refs):
        srcs_r, lands_r = refs[:na], refs[na:2 * na]
        send_sems, recv_sems, local_sems = refs[n_in:n_in + 3]
        token = refs[-1]
        local, sent, _ = _split_copies(srcs_r, lands_r, send_sems, recv_sems, local_sems, gather)
        for cp in local + sent:
            cp.start()
        token[...] = jnp.zeros_like(token)

    sem = pltpu.SemaphoreType.DMA((na * (N_DEV - 1),))
    outs = pl.pallas_call(
        body, name=name,
        out_shape=(sem, sem, pltpu.SemaphoreType.DMA((na,)),
                   *[pltpu.HBM(s.shape, s.dtype) for s in srcs],
                   *[pltpu.HBM(s.shape, s.dtype) for s in land_structs],
                   jax.ShapeDtypeStruct((8, LANES), F32)),
        in_specs=[_HBM_SPEC] * (2 * na) + [pl.BlockSpec(memory_space=pl.ANY)] * len(extra),
        out_specs=(_SEM_SPEC, _SEM_SPEC, _SEM_SPEC, *([_HBM_SPEC] * (2 * na)), pl.BlockSpec(memory_space=pltpu.VMEM)),
        input_output_aliases={i: 3 + i for i in range(2 * na)},
        compiler_params=pltpu.CompilerParams(has_side_effects=pltpu.SideEffectType.DATAFLOW_SIDE_EFFECTING),
    )(*[pltpu.with_memory_space_constraint(s, pltpu.HBM) for s in srcs],
      *[pltpu.with_memory_space_constraint(lax.empty(s.shape, s.dtype), pltpu.HBM) for s in land_structs],
      *extra)
    return (outs[:3], outs[3:3 + na], outs[3 + na:3 + 2 * na]), outs[-1]


def _comm_wait(name, handles, gather, after):
    (send_sems, recv_sems, local_sems), srcs, lands = handles
    na = len(srcs)

    def body(*refs):
        srcs_r, lands_r = refs[:na], refs[na:2 * na]
        send_r, recv_r, local_r = refs[2 * na:2 * na + 3]
        local, sent, received = _split_copies(srcs_r, lands_r, send_r, recv_r, local_r, gather)
        for cp in sent:
            cp.wait_send()
        for cp in received:
            cp.wait_recv()
        for cp in local:
            cp.wait()

    outs = pl.pallas_call(
        body, name=name,
        out_shape=(*[pltpu.HBM(s.shape, s.dtype) for s in srcs], *[pltpu.HBM(s.shape, s.dtype) for s in lands]),
        in_specs=[_HBM_SPEC] * (2 * na) + [_SEM_SPEC] * 3 + [pl.BlockSpec(memory_space=pl.ANY)],
        out_specs=tuple([_HBM_SPEC] * (2 * na)),
        input_output_aliases={i: i for i in range(2 * na)},
        compiler_params=pltpu.CompilerParams(has_side_effects=pltpu.SideEffectType.DATAFLOW_SIDE_EFFECTING),
    )(*srcs, *lands, send_sems, recv_sems, local_sems, after)
    return outs[na:]


def _prep_weights(w):
    e = CONV_WIDTH
    p = {}
    if 'a_w_in' in w:
        if w['a_w_in'].shape == (N_DEV, D_MODEL, 3 * e // N_DEV):
            p['a_w_in3'] = w['a_w_in']
        else:
            p['a_w_in3'] = w['a_w_in'].reshape(D_MODEL, N_DEV, 3 * e // N_DEV).transpose(1, 0, 2)
        p['a_b_in'] = w['a_b_in'].reshape(1, 3 * e)
        p['a_conv_w'] = jnp.pad(w['a_conv_w'].reshape(CONV_KERNEL, e), ((0, HALO - CONV_KERNEL), (0, 0)))
        p['a_conv_b'] = w['a_conv_b'].reshape(1, e)
        p['a_norm_g'] = w['a_norm_g'].reshape(1, e)
        p['a_norm_b'] = w['a_norm_b'].reshape(1, e)
        p['a_b_out'] = w['a_b_out'].reshape(1, D_MODEL)
        p['kv_norm_g'] = w['kv_norm_g'].reshape(1, KV_LORA_RANK)
        p['b_q_norm_g'] = w['b_q_norm_g'].reshape(1, Q_LORA_RANK)
        p['ln_g'] = w['ln_g']
        p['ln_b'] = w['ln_b']
    if 'a_w_out' in w:
        p['a_w_out'] = w['a_w_out'].reshape(e, D_MODEL)
        p['kv_w_down'] = jnp.pad(w['kv_w_down'], ((0, 0), (0, KV_LORA_RANK + LANES - w['kv_w_down'].shape[1])))
        p['kv_w_ukv'] = jnp.concatenate([w['kv_w_uk'], w['kv_w_uv']], axis=2).reshape(KV_LORA_RANK, N_HEADS * HEAD_PAD)
        p['b_w_in'] = w['b_w_in'].reshape(D_MODEL, -1)
        uq = w['b_w_uq'].reshape(Q_LORA_RANK, N_HEADS, QK_NOPE_DIM + QK_ROPE_DIM)
        p['b_w_uq'] = jnp.pad(uq, ((0, 0), (0, 0), (0, HEAD_PAD - uq.shape[2]))).reshape(Q_LORA_RANK, N_HEADS * HEAD_PAD)
        p['b_w_out'] = w['b_w_out'].reshape(N_HEADS * V_HEAD_DIM, D_MODEL)
    return p


class _NoComm:
    def __init__(self):
        self.grads = {}
        self.small = None

    def first_after(self):
        return None

    def rest_weights(self, after):
        return {}

    def send_group1(self, g):
        self.grads.update(g)
        return None

    def send_group2(self, g, small):
        self.grads.update(g)
        self.small = small
        return None


def _local_step(x, positions, target, p, comm):
    t = x.shape[0]
    e = CONV_WIDTH
    freqs = ROPE_THETA ** (-jnp.arange(0, QK_ROPE_DIM, 2, dtype=F32) / QK_ROPE_DIM)
    freq_row = jnp.concatenate([freqs, freqs, jnp.zeros((LANES - QK_ROPE_DIM,), F32)]).reshape(1, LANES)
    rc, rs = _rope_tables(positions.reshape(t, 1), freq_row)
    xb = x.astype(BF16)
    ln_g0, ln_b0 = p['ln_g'][0:1], p['ln_b'][0:1]
    ln_g1, ln_b1 = p['ln_g'][1:2], p['ln_b'][1:2]

    proj = _matmul(xb, p['a_w_in3'], bias=p['a_b_in'], name="a_in", b_blocked=True, after=comm.first_after())
    u1, u2 = _conv_mid_fwd(proj, p['a_conv_w'], p['a_conv_b'], p['a_norm_g'], p['a_norm_b'])
    p = {**p, **comm.rest_weights(u2)}
    y = _matmul(u2, p['a_w_out'], bias=p['a_b_out'], name="a_out", bk=2048)
    h1, h1b, xh1, rstd1 = _ln_res_fwd(x, y, ln_g0, ln_b0)
    ckv = _matmul(h1b, p['kv_w_down'], name="kv_down")
    ckn, kr = _kv_mid_fwd(ckv, p['kv_norm_g'], rc, rs)
    kv = _matmul(ckn, p['kv_w_ukv'], name="kv_up", out_dtype=BF16)
    projb = _matmul(h1b, p['b_w_in'], name="b_in", bn=1280)
    cqn = _q_norm_fwd(projb, p['b_q_norm_g'])
    qcat = _q_up_rope(cqn, p['b_w_uq'], rc, rs)
    o, o2, lse = _flash_fwd(qcat, kv, kr, projb)
    yb = _matmul(o2, p['b_w_out'], name="b_out", bk=2048)
    loss, dr2, dr2b, dg1, db1 = _final_ln_loss(h1, yb, ln_g1, ln_b1, target)

    g = {}
    g['b_w_out'] = _matmul(o2, dr2b, trans_a=True, name="d_b_out", out_dtype=BF16, bm=512, bk=t)
    do2 = _matmul(dr2b, p['b_w_out'], trans_b=True, name="d_o2")
    dob, dprojb, dl = _gate_bwd(do2, o, projb)
    dq2, dkv, dkr = _flash_bwd(qcat, kv, kr, dob, lse, dl, rc, rs)
    duq = _matmul(cqn, dq2, trans_a=True, name="d_q_up", out_dtype=BF16, bk=t)
    dcqn = _matmul(dq2, p['b_w_uq'], trans_b=True, name="d_cqn", bk=N_HEADS * HEAD_PAD)
    dprojb, dgq = _q_norm_bwd(dcqn, projb, p['b_q_norm_g'], dprojb)
    g['b_w_in'] = _matmul(h1b, dprojb, trans_a=True, name="d_b_in", bn=640, bk=t, out_dtype=BF16)
    t1 = _matmul(dprojb, p['b_w_in'], trans_b=True, name="d_h1_b", bk=dprojb.shape[1])
    dukv = _matmul(ckn, dkv, trans_a=True, name="d_kv_up", out_dtype=BF16, bk=t)
    dckn = _matmul(dkv, p['kv_w_ukv'], trans_b=True, name="d_ckn", bk=N_HEADS * HEAD_PAD)
    dckv, dgkv = _kv_mid_bwd(dckn, ckv, dkr, p['kv_norm_g'], rc, rs)
    ddown = _matmul(h1b, dckv, trans_a=True, name="d_kv_down", out_dtype=BF16, bm=512, bk=t)
    g['b_w_out'] = g['b_w_out'].reshape(N_DEV, -1, D_MODEL)
    g['b_w_in'] = g['b_w_in'].reshape(D_MODEL, N_DEV, -1).transpose(1, 0, 2)
    g['kv_w_down'] = ddown[:, :KV_LORA_RANK + QK_ROPE_DIM].reshape(N_DEV, -1, KV_LORA_RANK + QK_ROPE_DIM)
    dukv = dukv.reshape(KV_LORA_RANK, N_HEADS, HEAD_PAD)
    g['kv_w_uk'] = dukv[:, :, :QK_NOPE_DIM].reshape(N_DEV, -1, QK_NOPE_DIM)
    g['kv_w_uv'] = dukv[:, :, QK_NOPE_DIM:].reshape(N_DEV, -1, V_HEAD_DIM)
    g['b_w_uq'] = duq.reshape(Q_LORA_RANK, N_HEADS, HEAD_PAD)[:, :, :QK_NOPE_DIM + QK_ROPE_DIM].reshape(
        N_DEV, -1, QK_NOPE_DIM + QK_ROPE_DIM)
    sent1 = comm.send_group1(g)
    t2 = _matmul(dckv, p['kv_w_down'], trans_b=True, name="d_h1_kv", after=sent1)
    dr1, dr1b, dg0, db0, dba_out = _ln0_bwd(dr2, t1, t2, xh1, rstd1, ln_g0)
    dw_out = _matmul(u2, dr1b, trans_a=True, name="d_a_out", out_dtype=BF16, bm=512, bk=t).reshape(N_DEV, -1, D_MODEL)
    du2 = _matmul(dr1b, p['a_w_out'], trans_b=True, name="d_u2")
    du1, dproj, dng, dnb, dbz = _conv_mid_bwd_rows(u1, proj, du2, p['a_norm_g'], p['a_norm_b'])
    dproj, dcw, dcb, dbv = _conv_bwd(du1, proj, p['a_conv_w'], dproj)
    dw_in = _matmul(xb, dproj, trans_a=True, name="d_a_in", out_dtype=BF16, bn=3 * e // N_DEV, bk=t, out_blocked=True)
    small = (dg0, dg1, db0, db1, dbv, dbz, dcb, dng, dnb, dba_out, dgkv, dgq, dcw)
    sent2 = comm.send_group2({'a_w_out': dw_out, 'a_w_in': dw_in}, small)
    grad_x = _grad_x(dproj, p['a_w_in3'], dr1, sent2)
    return loss, grad_x


def _grad_x(dproj, w3, dr1, after):
    t, d = dr1.shape
    nblk, _, cb = w3.shape
    tm = min(t, 512)
    extra = [] if after is None else [after]

    def body(a_ref, w_ref, dr_ref, *rest):
        o_ref = rest[-1]
        acc = ALPHA * dr_ref[...]
        for k in range(nblk):
            acc = acc + lax.dot_general(a_ref[:, k * cb:(k + 1) * cb], w_ref[k], NT, preferred_element_type=F32)
        o_ref[...] = acc

    row = pl.BlockSpec((tm, d), lambda i: (i, 0))
    return pl.pallas_call(
        body, name="grad_x",
        out_shape=jax.ShapeDtypeStruct((t, d), F32),
        grid=(t // tm,),
        in_specs=[pl.BlockSpec((tm, nblk * cb), lambda i: (i, 0)),
                  pl.BlockSpec((nblk, d, cb), lambda i: (0, 0, 0), pipeline_mode=pl.Buffered(1)), row]
                 + [pl.BlockSpec(memory_space=pl.ANY)] * len(extra),
        out_specs=row,
        compiler_params=_cparams(("parallel",)),
    )(dproj, w3, dr1, *extra)


def _shard_2d(a):
    return a.reshape(-1, a.shape[-1])


def _gathered_to_full(name, blocks):
    if name == 'a_w_in':
        return blocks
    if name == 'b_w_in':
        return blocks.transpose(1, 0, 2).reshape(D_MODEL, -1)
    if name == 'a_conv_w':
        return blocks.transpose(1, 0, 2).reshape(CONV_KERNEL, -1)
    if name in ('a_b_in', 'a_conv_b', 'a_norm_g', 'a_norm_b', 'a_b_out'):
        return blocks.reshape(-1)
    if name in ('kv_w_uk', 'kv_w_uv'):
        return blocks.reshape(KV_LORA_RANK, N_HEADS, -1)
    if name == 'b_w_uq':
        return blocks.reshape(Q_LORA_RANK, N_HEADS, -1)
    return blocks.reshape(-1, blocks.shape[-1])


def kernel(x, positions, ln_g, ln_b, a_w_in, a_b_in, a_conv_w, a_conv_b, a_norm_g, a_norm_b, a_w_out, a_b_out, kv_w_down, kv_norm_g, kv_w_uk, kv_w_uv, b_w_in, b_q_norm_g, b_w_uq, b_w_out, loss_target, m_ln_g, m_ln_b, m_a_w_in, m_a_b_in, m_a_conv_w, m_a_conv_b, m_a_norm_g, m_a_norm_b, m_a_w_out, m_a_b_out, m_kv_w_down, m_kv_norm_g, m_kv_w_uk, m_kv_w_uv, m_b_w_in, m_b_q_norm_g, m_b_w_uq, m_b_w_out, v_ln_g, v_ln_b, v_a_w_in, v_a_b_in, v_a_conv_w, v_a_conv_b, v_a_norm_g, v_a_norm_b, v_a_w_out, v_a_b_out, v_kv_w_down, v_kv_norm_g, v_kv_w_uk, v_kv_w_uv, v_b_w_in, v_b_q_norm_g, v_b_w_uq, v_b_w_out):
    w = dict(ln_g=ln_g, ln_b=ln_b, a_w_in=a_w_in, a_b_in=a_b_in, a_conv_w=a_conv_w, a_conv_b=a_conv_b,
             a_norm_g=a_norm_g, a_norm_b=a_norm_b, a_w_out=a_w_out, a_b_out=a_b_out, kv_w_down=kv_w_down,
             kv_norm_g=kv_norm_g, kv_w_uk=kv_w_uk, kv_w_uv=kv_w_uv, b_w_in=b_w_in, b_q_norm_g=b_q_norm_g,
             b_w_uq=b_w_uq, b_w_out=b_w_out)
    m = dict(ln_g=m_ln_g, ln_b=m_ln_b, a_w_in=m_a_w_in, a_b_in=m_a_b_in, a_conv_w=m_a_conv_w, a_conv_b=m_a_conv_b,
             a_norm_g=m_a_norm_g, a_norm_b=m_a_norm_b, a_w_out=m_a_w_out, a_b_out=m_a_b_out, kv_w_down=m_kv_w_down,
             kv_norm_g=m_kv_norm_g, kv_w_uk=m_kv_w_uk, kv_w_uv=m_kv_w_uv, b_w_in=m_b_w_in, b_q_norm_g=m_b_q_norm_g,
             b_w_uq=m_b_w_uq, b_w_out=m_b_w_out)
    v = dict(ln_g=v_ln_g, ln_b=v_ln_b, a_w_in=v_a_w_in, a_b_in=v_a_b_in, a_conv_w=v_a_conv_w, a_conv_b=v_a_conv_b,
             a_norm_g=v_a_norm_g, a_norm_b=v_a_norm_b, a_w_out=v_a_w_out, a_b_out=v_a_b_out, kv_w_down=v_kv_w_down,
             kv_norm_g=v_kv_norm_g, kv_w_uk=v_kv_w_uk, kv_w_uv=v_kv_w_uv, b_w_in=v_b_w_in, b_q_norm_g=v_b_q_norm_g,
             b_w_uq=v_b_w_uq, b_w_out=v_b_w_out)

    small_flat = jnp.concatenate([w[n].reshape(-1) for n in SMALL_WEIGHTS]).reshape(1, -1)
    ga = _all_gather("all_gather_first", [_shard_2d(w['a_w_in']).astype(BF16), small_flat])
    full = {'a_w_in': ga[0]}
    gs = ga[1].reshape(N_DEV, -1)
    off = 0
    for n in SMALL_WEIGHTS:
        size = math.prod(w[n].shape)
        full[n] = _gathered_to_full(n, gs[:, off:off + size].reshape((N_DEV,) + _shard_2d(w[n]).shape))
        off += size
    for n in REPLICATED:
        full[n] = w[n]
    rest_names = [n for n in MATMUL_WEIGHTS if n != 'a_w_in']
    group1 = ('b_w_out', 'b_w_uq', 'b_w_in', 'kv_w_uk', 'kv_w_uv', 'kv_w_down')
    group2 = ('a_w_out', 'a_w_in')

    class Comm:
        def __init__(self):
            self.rest, self.rest_token = _comm_start(
                "gather_rest_start", [_shard_2d(w[n]).astype(BF16) for n in rest_names], True, ga[0])

        def first_after(self):
            return self.rest_token

        def rest_weights(self, after):
            lands = _comm_wait("gather_rest_wait", self.rest, True, after)
            return _prep_weights({n: _gathered_to_full(n, lands[i]) for i, n in enumerate(rest_names)})

        def send_group1(self, g):
            self.g1, token = _comm_start("exchange_g1_start", [g[n] for n in group1], False, None)
            return token

        def send_group2(self, g, small):
            self.g2, token = _comm_start("exchange_g2_start", [g[n] for n in group2] + [_pack_small_grads(*small)],
                                         False, None)
            return token

    comm = Comm()

    loss_loc, grad_x = _local_step(x[0], positions[0], loss_target[0], _prep_weights(full), comm)
    loss = lax.psum(loss_loc[0, 0], ("x", "y", "c"))

    res = {}
    recv1 = _comm_wait("exchange_g1_wait", comm.g1, False, grad_x)
    for i, n in enumerate(group1):
        res[n] = _adamw(n, recv1[i], _shard_2d(w[n]), _shard_2d(m[n]), _shard_2d(v[n]))
    recv2 = _comm_wait("exchange_g2_wait", comm.g2, False, res[group1[-1]][0])
    for i, n in enumerate(group2):
        res[n] = _adamw(n, recv2[i], _shard_2d(w[n]), _shard_2d(m[n]), _shard_2d(v[n]))
    two_d = lambda d: [_shard_2d(d[n]) if d[n].ndim > 1 else d[n].reshape(1, -1) for n in SMALL_NAMES]
    sg, sd, sm, sv = _adamw_small(recv2[-1], two_d(w), two_d(m), two_d(v))
    for i, n in enumerate(SMALL_NAMES):
        res[n] = (sg[i], sd[i], sm[i], sv[i])
    outs = [[res[n][j].reshape(w[n].shape) for n in WEIGHT_NAMES] for j in range(4)]
    return (loss, grad_x[None], *outs[0], *outs[1], *outs[2], *outs[3])
```

```python
import functools
import math

import jax
import jax.numpy as jnp
from jax import lax
from jax.experimental import pallas as pl
from jax.experimental.pallas import tpu as pltpu

F32 = jnp.float32
BF16 = jnp.bfloat16

D_MODEL = 1024
DEPTH = 2
CONV_WIDTH = 2 * D_MODEL
CONV_KERNEL = 31
N_HEADS = 16
QK_NOPE_DIM = 128
QK_ROPE_DIM = 64
V_HEAD_DIM = 128
KV_LORA_RANK = D_MODEL // 4
Q_LORA_RANK = D_MODEL // 2
ROPE_THETA = 10000.0
LN_EPS = 1e-5
RMS_EPS = 1e-6
MASK_VALUE = -1e30
ALPHA = (2.0 * DEPTH) ** 0.25
ATTN_SCALE = 1.0 / math.sqrt(QK_NOPE_DIM + QK_ROPE_DIM)

ADAM_LR = 0.001
ADAM_B1 = 0.9
ADAM_B2 = 0.999
ADAM_EPS = 1e-08
ADAM_WD = 0.01
ADAM_STEP = 10

N_DEV = 8
LANES = 128
HEAD_PAD = 256
HALO = 32
VMEM_LIMIT = 56 * 1024 * 1024

WEIGHT_NAMES = ['ln_g', 'ln_b', 'a_w_in', 'a_b_in', 'a_conv_w', 'a_conv_b', 'a_norm_g', 'a_norm_b',
                'a_w_out', 'a_b_out', 'kv_w_down', 'kv_norm_g', 'kv_w_uk', 'kv_w_uv', 'b_w_in',
                'b_q_norm_g', 'b_w_uq', 'b_w_out']
REPLICATED = ('ln_g', 'ln_b', 'kv_norm_g', 'b_q_norm_g')
MATMUL_WEIGHTS = ('a_w_in', 'a_w_out', 'kv_w_down', 'kv_w_uk', 'kv_w_uv', 'b_w_in', 'b_w_uq', 'b_w_out')
SMALL_WEIGHTS = ('a_b_in', 'a_conv_w', 'a_conv_b', 'a_norm_g', 'a_norm_b', 'a_b_out')

NN = (((1,), (0,)), ((), ()))
NT = (((1,), (1,)), ((), ()))
TN = (((0,), (0,)), ((), ()))


def _cparams(sem):
    return pltpu.CompilerParams(dimension_semantics=sem, vmem_limit_bytes=VMEM_LIMIT)


def _sigmoid(x):
    return 0.5 * jnp.tanh(0.5 * x) + 0.5


def _aligned(v, m):
    return v if isinstance(v, int) else pl.multiple_of(v, m)


def _fold_rows(x):
    parts = [x[r:r + 8] for r in range(0, x.shape[0], 8)]
    while len(parts) > 1:
        parts = [parts[i] + parts[i + 1] for i in range(0, len(parts) - 1, 2)] + ([parts[-1]] if len(parts) % 2 else [])
    return parts[0]


def _taps_by_residue(offset_of):
    groups = {}
    for j in range(CONV_KERNEL):
        off = offset_of(j)
        groups.setdefault(off % 8, []).append((j, off // 8))
    return sorted(groups.items())


def _swap_rope_halves(t):
    lane = lax.broadcasted_iota(jnp.int32, t.shape, 1)
    return jnp.where(lane < QK_ROPE_DIM // 2, pltpu.roll(t, LANES - QK_ROPE_DIM // 2, 1),
                     pltpu.roll(t, QK_ROPE_DIM // 2, 1))


def _matmul(a, b, *, name, bias=None, trans_a=False, trans_b=False, out_dtype=F32, bm=1024, bn=1024, bk=1024,
            b_blocked=False, out_blocked=False, after=None):
    if trans_a:
        kdim, m = a.shape
    else:
        m, kdim = a.shape
    if b_blocked and trans_b:
        n, k2 = b.shape[1], b.shape[0] * b.shape[2]
        bk = b.shape[2]
    elif b_blocked:
        k2, n = b.shape[1], b.shape[0] * b.shape[2]
        bn = b.shape[2]
    elif trans_b:
        n, k2 = b.shape
    else:
        k2, n = b.shape
    assert kdim == k2, (a.shape, b.shape)
    bm, bn, bk = min(bm, m), min(bn, n), min(bk, kdim)
    assert m % bm == 0 and n % bn == 0 and kdim % bk == 0, (name, m, n, kdim, bm, bn, bk)
    nk = kdim // bk
    dims = (((0 if trans_a else 1,), (1 if trans_b else 0,)), ((), ()))
    has_bias = bias is not None

    def body(*refs):
        refs = list(refs)
        a_ref, b_ref = refs[:2]
        del refs[:2]
        bias_ref = refs.pop(0) if has_bias else None
        if after is not None:
            refs.pop(0)
        o_ref = refs.pop(0)
        rest = refs
        prod = lax.dot_general(a_ref[...], b_ref[...], dims, preferred_element_type=F32)

        def finish(acc):
            if has_bias:
                acc = acc + bias_ref[...]
            o_ref[...] = acc.astype(out_dtype)

        if nk == 1:
            finish(prod)
        else:
            acc_ref = rest[0]
            k = pl.program_id(2)

            @pl.when(k == 0)
            def _():
                acc_ref[...] = prod

            @pl.when(k > 0)
            def _():
                acc_ref[...] += prod

            @pl.when(k == nk - 1)
            def _():
                finish(acc_ref[...])

    a_spec = pl.BlockSpec((bk, bm), lambda i, j, k: (k, i)) if trans_a else pl.BlockSpec((bm, bk), lambda i, j, k: (i, k))
    if b_blocked and trans_b:
        b_spec = pl.BlockSpec((None, bn, bk), lambda i, j, k: (k, j, 0))
    elif b_blocked:
        b_spec = pl.BlockSpec((None, bk, bn), lambda i, j, k: (j, k, 0))
    elif trans_b:
        b_spec = pl.BlockSpec((bn, bk), lambda i, j, k: (j, k))
    else:
        b_spec = pl.BlockSpec((bk, bn), lambda i, j, k: (k, j))
    in_specs = [a_spec, b_spec]
    args = [a, b]
    if has_bias:
        in_specs.append(pl.BlockSpec((1, bn), lambda i, j, k: (0, j)))
        args.append(bias)
    if after is not None:
        in_specs.append(pl.BlockSpec(memory_space=pl.ANY))
        args.append(after)
    if out_blocked:
        out_struct = jax.ShapeDtypeStruct((n // bn, m, bn), out_dtype)
        out_spec = pl.BlockSpec((None, bm, bn), lambda i, j, k: (j, i, 0))
    else:
        out_struct = jax.ShapeDtypeStruct((m, n), out_dtype)
        out_spec = pl.BlockSpec((bm, bn), lambda i, j, k: (i, j))
    return pl.pallas_call(
        body, name=name,
        out_shape=out_struct,
        grid=(m // bm, n // bn, nk),
        in_specs=in_specs,
        out_specs=out_spec,
        scratch_shapes=[pltpu.VMEM((bm, bn), F32)] if nk > 1 else [],
        compiler_params=_cparams(("parallel", "parallel", "arbitrary")),
    )(*args)


def _rope_tables(pos_col, freq_row):
    t = pos_col.shape[0]
    tm = min(t, 512)

    def body(pos_ref, f_ref, c_ref, s_ref):
        ang = pos_ref[...].astype(F32) * f_ref[...]
        lane = lax.broadcasted_iota(jnp.int32, ang.shape, 1)
        cos = jnp.cos(ang)
        sin = jnp.sin(ang)
        c_ref[...] = jnp.where(lane < QK_ROPE_DIM, cos, 0.0)
        s_ref[...] = jnp.where(lane < QK_ROPE_DIM // 2, -sin, jnp.where(lane < QK_ROPE_DIM, sin, 0.0))

    return pl.pallas_call(
        body, name="rope_tables",
        out_shape=(jax.ShapeDtypeStruct((t, LANES), F32), jax.ShapeDtypeStruct((t, LANES), F32)),
        grid=(t // tm,),
        in_specs=[pl.BlockSpec((tm, 1), lambda i: (i, 0)), pl.BlockSpec((1, LANES), lambda i: (0, 0))],
        out_specs=(pl.BlockSpec((tm, LANES), lambda i: (i, 0)), pl.BlockSpec((tm, LANES), lambda i: (i, 0))),
        compiler_params=_cparams(("parallel",)),
    )(pos_col, freq_row)


def _conv_mid_fwd(proj, cw, cb, ng, nb):
    t = proj.shape[0]
    e = CONV_WIDTH
    tm = min(t, 128)
    hb = tm // HALO
    nchunk = e // LANES

    def body(val_ref, gg_ref, z_ref, valh_ref, ggh_ref, cw_ref, cb_ref, ng_ref, nb_ref, u1_ref, u2_ref, ext_ref, sh_ref):
        i = pl.program_id(0)
        u0 = val_ref[...] * _sigmoid(gg_ref[...])
        h0 = valh_ref[...] * _sigmoid(ggh_ref[...])
        ext_ref[0:HALO, :] = jnp.where(i > 0, h0, 0.0)
        ext_ref[HALO:, :] = u0

        def chunk(c, carry):
            col = pl.multiple_of(c * LANES, LANES)
            acc = jnp.zeros((tm, LANES), F32)
            for res, taps in _taps_by_residue(lambda j: HALO - (CONV_KERNEL - 1) + j):
                span = 8 * max(a for _, a in taps) + tm
                sh_ref[0:span, :] = ext_ref[res:res + span, pl.ds(col, LANES)]
                for j, a in taps:
                    acc = acc + cw_ref[j:j + 1, pl.ds(col, LANES)] * sh_ref[8 * a:8 * a + tm, :]
            u1_ref[:, pl.ds(col, LANES)] = acc + cb_ref[:, pl.ds(col, LANES)]
            return carry

        lax.fori_loop(0, nchunk, chunk, 0)
        u1 = u1_ref[...]
        mu = jnp.mean(u1, axis=-1, keepdims=True)
        xc = u1 - mu
        var = jnp.mean(xc * xc, axis=-1, keepdims=True)
        n = xc * lax.rsqrt(var + LN_EPS) * ng_ref[...] + nb_ref[...]
        z = z_ref[...]
        u2_ref[...] = ((n * _sigmoid(n)) * (z * _sigmoid(z))).astype(BF16)

    row = lambda c: pl.BlockSpec((tm, e), lambda i: (i, c))
    halo = lambda c: pl.BlockSpec((HALO, e), lambda i: (jnp.maximum(i * hb - 1, 0), c))
    par = lambda r: pl.BlockSpec((r, e), lambda i: (0, 0))
    return pl.pallas_call(
        body, name="conv_mid_fwd",
        out_shape=(jax.ShapeDtypeStruct((t, e), F32), jax.ShapeDtypeStruct((t, e), BF16)),
        grid=(t // tm,),
        in_specs=[row(0), row(1), row(2), halo(0), halo(1), par(HALO), par(1), par(1), par(1)],
        out_specs=(pl.BlockSpec((tm, e), lambda i: (i, 0)), pl.BlockSpec((tm, e), lambda i: (i, 0))),
        scratch_shapes=[pltpu.VMEM((tm + HALO, e), F32), pltpu.VMEM((tm + HALO, LANES), F32)],
        compiler_params=_cparams(("parallel",)),
    )(proj, proj, proj, proj, proj, cw, cb, ng, nb)


def _ln_res_fwd(x, y, g, b):
    t, d = x.shape
    tm = min(t, 256)

    def body(x_ref, y_ref, g_ref, b_ref, h_ref, hb_ref, xh_ref, rs_ref):
        r = ALPHA * x_ref[...] + y_ref[...]
        mu = jnp.mean(r, axis=-1, keepdims=True)
        xc = r - mu
        var = jnp.mean(xc * xc, axis=-1, keepdims=True)
        rstd = lax.rsqrt(var + LN_EPS)
        xh = xc * rstd
        h = xh * g_ref[...] + b_ref[...]
        h_ref[...] = h
        hb_ref[...] = h.astype(BF16)
        xh_ref[...] = xh
        rs_ref[...] = rstd

    row = pl.BlockSpec((tm, d), lambda i: (i, 0))
    par = pl.BlockSpec((1, d), lambda i: (0, 0))
    return pl.pallas_call(
        body, name="ln0_fwd",
        out_shape=(jax.ShapeDtypeStruct((t, d), F32), jax.ShapeDtypeStruct((t, d), BF16),
                   jax.ShapeDtypeStruct((t, d), F32), jax.ShapeDtypeStruct((t, 1), F32)),
        grid=(t // tm,),
        in_specs=[row, row, par, par],
        out_specs=(row, row, row, pl.BlockSpec((tm, 1), lambda i: (i, 0))),
        compiler_params=_cparams(("parallel",)),
    )(x, y, g, b)


def _kv_mid_fwd(ckv, g, rc, rs):
    t = ckv.shape[0]
    tm = min(t, 512)
    r = KV_LORA_RANK

    def body(ckv_ref, g_ref, c_ref, s_ref, ckn_ref, kr_ref):
        c = ckv_ref[:, 0:r]
        ms = jnp.mean(c * c, axis=-1, keepdims=True)
        ckn_ref[...] = (c * lax.rsqrt(ms + RMS_EPS) * g_ref[...]).astype(BF16)
        tr = ckv_ref[:, r:r + LANES]
        kr_ref[...] = (tr * c_ref[...] + _swap_rope_halves(tr) * s_ref[...]).astype(BF16)

    return pl.pallas_call(
        body, name="kv_mid_fwd",
        out_shape=(jax.ShapeDtypeStruct((t, r), BF16), jax.ShapeDtypeStruct((t, LANES), BF16)),
        grid=(t // tm,),
        in_specs=[pl.BlockSpec((tm, r + LANES), lambda i: (i, 0)), pl.BlockSpec((1, r), lambda i: (0, 0)),
                  pl.BlockSpec((tm, LANES), lambda i: (i, 0)), pl.BlockSpec((tm, LANES), lambda i: (i, 0))],
        out_specs=(pl.BlockSpec((tm, r), lambda i: (i, 0)), pl.BlockSpec((tm, LANES), lambda i: (i, 0))),
        compiler_params=_cparams(("parallel",)),
    )(ckv, g, rc, rs)


def _q_norm_fwd(projb, g):
    t = projb.shape[0]
    tm = min(t, 512)
    r = Q_LORA_RANK

    def body(c_ref, g_ref, o_ref):
        c = c_ref[...]
        ms = jnp.mean(c * c, axis=-1, keepdims=True)
        o_ref[...] = (c * lax.rsqrt(ms + RMS_EPS) * g_ref[...]).astype(BF16)

    return pl.pallas_call(
        body, name="q_norm_fwd",
        out_shape=jax.ShapeDtypeStruct((t, r), BF16),
        grid=(t // tm,),
        in_specs=[pl.BlockSpec((tm, r), lambda i: (i, 0)), pl.BlockSpec((1, r), lambda i: (0, 0))],
        out_specs=pl.BlockSpec((tm, r), lambda i: (i, 0)),
        compiler_params=_cparams(("parallel",)),
    )(projb, g)


def _q_up_rope(cqn, wuq, rc, rs):
    t, r = cqn.shape
    w = wuq.shape[1]
    tm = min(t, 1024)
    bn = 4 * HEAD_PAD

    def body(a_ref, b_ref, c_ref, s_ref, o_ref):
        q = lax.dot_general(a_ref[...], b_ref[...], NN, preferred_element_type=F32)
        c = c_ref[...]
        s = s_ref[...]
        for h in range(bn // HEAD_PAD):
            a = h * HEAD_PAD
            o_ref[:, a:a + LANES] = q[:, a:a + LANES].astype(BF16)
            tr = q[:, a + LANES:a + 2 * LANES]
            o_ref[:, a + LANES:a + 2 * LANES] = (tr * c + _swap_rope_halves(tr) * s).astype(BF16)

    return pl.pallas_call(
        body, name="q_up_rope",
        out_shape=jax.ShapeDtypeStruct((t, w), BF16),
        grid=(t // tm, w // bn),
        in_specs=[pl.BlockSpec((tm, r), lambda i, j: (i, 0)), pl.BlockSpec((r, bn), lambda i, j: (0, j)),
                  pl.BlockSpec((tm, LANES), lambda i, j: (i, 0)), pl.BlockSpec((tm, LANES), lambda i, j: (i, 0))],
        out_specs=pl.BlockSpec((tm, bn), lambda i, j: (i, j)),
        compiler_params=_cparams(("parallel", "parallel")),
    )(cqn, wuq, rc, rs)


def _flash_fwd(qcat, kv, kr, projb):
    t = qcat.shape[0]
    tq = min(t, 512)
    nq = t // tq
    exp2_scale = ATTN_SCALE * math.log2(math.e)
    z_first = Q_LORA_RANK // LANES

    def body(q_ref, kv_ref, kr_ref, z_ref, o_ref, o2_ref, lse_ref, kcat_ref):
        kcat_ref[:, 0:LANES] = kv_ref[:, 0:LANES]
        kcat_ref[:, LANES:] = kr_ref[...]
        rows = lax.broadcasted_iota(jnp.int32, (tq, tq), 0)
        cols = lax.broadcasted_iota(jnp.int32, (tq, tq), 1)
        for i in range(nq):
            a = i * tq
            q = q_ref[a:a + tq, :]
            sd = lax.dot_general(q, kcat_ref[a:a + tq, :], NT, preferred_element_type=F32)
            sd = jnp.where(cols <= rows, sd, MASK_VALUE)
            m = jnp.max(sd, axis=1, keepdims=True)
            if i > 0:
                sp = lax.dot_general(q, kcat_ref[0:a, :], NT, preferred_element_type=F32)
                m = jnp.maximum(m, jnp.max(sp, axis=1, keepdims=True))
            pd = jnp.exp2((sd - m) * exp2_scale)
            l = jnp.sum(pd, axis=1, keepdims=True)
            acc = lax.dot_general(pd.astype(BF16), kv_ref[a:a + tq, LANES:], NN, preferred_element_type=F32)
            if i > 0:
                pp = jnp.exp2((sp - m) * exp2_scale)
                l = l + jnp.sum(pp, axis=1, keepdims=True)
                acc = acc + lax.dot_general(pp.astype(BF16), kv_ref[0:a, LANES:], NN, preferred_element_type=F32)
            o = acc / l
            z = z_ref[a:a + tq, :]
            o_ref[a:a + tq, :] = o
            o2_ref[a:a + tq, :] = (o * (z * _sigmoid(z))).astype(BF16)
            lse_ref[a:a + tq, :] = jnp.broadcast_to(m * ATTN_SCALE + jnp.log(l), (tq, LANES))

    hw = N_HEADS * LANES
    head256 = pl.BlockSpec((t, HEAD_PAD), lambda h: (0, h))
    head128 = pl.BlockSpec((t, LANES), lambda h: (0, h))
    return pl.pallas_call(
        body, name="flash_fwd",
        out_shape=(jax.ShapeDtypeStruct((t, hw), F32), jax.ShapeDtypeStruct((t, hw), BF16),
                   jax.ShapeDtypeStruct((t, hw), F32)),
        grid=(N_HEADS,),
        in_specs=[head256, head256, pl.BlockSpec((t, LANES), lambda h: (0, 0), pipeline_mode=pl.Buffered(1)),
                  pl.BlockSpec((t, LANES), lambda h: (0, z_first + h))],
        out_specs=(head128, head128, head128),
        scratch_shapes=[pltpu.VMEM((t, HEAD_PAD), BF16)],
        compiler_params=_cparams(("parallel",)),
    )(qcat, kv, kr, projb)


def _final_ln_loss(h1, yb, g, b, target):
    t, d = h1.shape
    tm = min(t, 256)

    def body(h_ref, y_ref, g_ref, b_ref, t_ref, loss_ref, dr_ref, drb_ref, dg_ref, db_ref):
        i = pl.program_id(0)
        r = ALPHA * h_ref[...] + y_ref[...]
        mu = jnp.mean(r, axis=-1, keepdims=True)
        xc = r - mu
        var = jnp.mean(xc * xc, axis=-1, keepdims=True)
        rstd = lax.rsqrt(var + LN_EPS)
        xh = xc * rstd
        g = g_ref[...]
        diff = xh * g + b_ref[...] - t_ref[...]
        part = 0.5 * jnp.sum(jnp.mean(diff * diff, axis=-1, keepdims=True))
        dh = diff * (1.0 / d)
        dgp = jnp.sum(dh * xh, axis=0, keepdims=True)
        dbp = jnp.sum(dh, axis=0, keepdims=True)

        @pl.when(i == 0)
        def _():
            loss_ref[...] = jnp.zeros_like(loss_ref)
            dg_ref[...] = jnp.zeros_like(dg_ref)
            db_ref[...] = jnp.zeros_like(db_ref)

        loss_ref[...] += jnp.full(loss_ref.shape, part, F32)
        dg_ref[...] += dgp
        db_ref[...] += dbp
        dxh = dh * g
        dr = rstd * (dxh - jnp.mean(dxh, axis=-1, keepdims=True) - xh * jnp.mean(dxh * xh, axis=-1, keepdims=True))
        dr_ref[...] = dr
        drb_ref[...] = dr.astype(BF16)

    row = pl.BlockSpec((tm, d), lambda i: (i, 0))
    par = pl.BlockSpec((1, d), lambda i: (0, 0))
    return pl.pallas_call(
        body, name="final_ln_loss",
        out_shape=(jax.ShapeDtypeStruct((1, LANES), F32), jax.ShapeDtypeStruct((t, d), F32),
                   jax.ShapeDtypeStruct((t, d), BF16), jax.ShapeDtypeStruct((1, d), F32),
                   jax.ShapeDtypeStruct((1, d), F32)),
        grid=(t // tm,),
        in_specs=[row, row, par, par, row],
        out_specs=(pl.BlockSpec((1, LANES), lambda i: (0, 0)), row, row, par, par),
        compiler_params=_cparams(("arbitrary",)),
    )(h1, yb, g, b, target)


def _gate_bwd(do2, o, projb):
    t, hw = o.shape
    tm = min(t, 512)
    bw = Q_LORA_RANK
    nb = hw // bw
    wb = projb.shape[1]

    def body(d_ref, o_ref, z_ref, dob_ref, dz_ref, dl_ref):
        z = z_ref[...]
        sg = _sigmoid(z)
        d2 = d_ref[...]
        o = o_ref[...]
        do = d2 * (z * sg)
        dob_ref[...] = do.astype(BF16)
        dz_ref[...] = (d2 * o * (sg * (1.0 + z * (1.0 - sg)))).astype(BF16)
        prod = do * o
        for hh in range(bw // LANES):
            dsum = jnp.sum(prod[:, hh * LANES:(hh + 1) * LANES], axis=1, keepdims=True)
            dl_ref[:, hh * LANES:(hh + 1) * LANES] = jnp.broadcast_to(dsum, (tm, LANES))

    blk = pl.BlockSpec((tm, bw), lambda i, j: (i, j))
    blk1 = pl.BlockSpec((tm, bw), lambda i, j: (i, j + 1))
    return pl.pallas_call(
        body, name="gate_bwd",
        out_shape=(jax.ShapeDtypeStruct((t, hw), BF16), jax.ShapeDtypeStruct((t, wb), BF16),
                   jax.ShapeDtypeStruct((t, hw), F32)),
        grid=(t // tm, nb),
        in_specs=[blk, blk, blk1],
        out_specs=(blk, blk1, blk),
        compiler_params=_cparams(("parallel", "parallel")),
    )(do2, o, projb)


def _flash_bwd(qcat, kv, kr, dob, lse, dl, rc, rs):
    t = qcat.shape[0]
    tq = min(t, 512)
    nq = t // tq
    q_chunk = 2 * tq
    log2e = math.log2(math.e)
    exp2_scale = ATTN_SCALE * log2e
    once = pl.Buffered(1)

    def body(q_ref, kv_ref, kr_ref, do_ref, lse_ref, dl_ref, c_ref, s_ref,
             dq_ref, dkv_ref, dkr_ref, kcat_ref, dqa_ref, dka_ref, dva_ref):
        h = pl.program_id(0)
        kcat_ref[:, 0:LANES] = kv_ref[:, 0:LANES]
        kcat_ref[:, LANES:] = kr_ref[...]
        dqa_ref[...] = jnp.zeros_like(dqa_ref)

        @pl.when(h == 0)
        def _():
            dkr_ref[...] = jnp.zeros_like(dkr_ref)

        rows = lax.broadcasted_iota(jnp.int32, (tq, tq), 0)
        cols = lax.broadcasted_iota(jnp.int32, (tq, tq), 1)

        def block(qs, n, ks, masked):
            q = q_ref[qs:qs + n, :]
            kc = kcat_ref[ks:ks + tq, :]
            s = lax.dot_general(q, kc, NT, preferred_element_type=F32)
            if masked:
                s = jnp.where(cols <= rows, s, MASK_VALUE)
            p = jnp.exp2(s * exp2_scale - lse_ref[qs:qs + n, 0:1] * log2e)
            do = do_ref[qs:qs + n, :]
            dp = lax.dot_general(do, kv_ref[ks:ks + tq, LANES:], NT, preferred_element_type=F32)
            ds = (p * (dp - dl_ref[qs:qs + n, 0:1])).astype(BF16)
            dva_ref[...] += lax.dot_general(p.astype(BF16), do, TN, preferred_element_type=F32)
            dka_ref[...] += lax.dot_general(ds, q, TN, preferred_element_type=F32)
            dqa_ref[qs:qs + n, :] += lax.dot_general(ds, kc, NN, preferred_element_type=F32)

        for kb in range(nq):
            a = kb * tq
            dka_ref[...] = jnp.zeros_like(dka_ref)
            dva_ref[...] = jnp.zeros_like(dva_ref)
            block(a, tq, a, True)
            qs = a + tq
            while qs < t:
                n = min(q_chunk, t - qs)
                block(qs, n, a, False)
                qs += n
            dk = dka_ref[...] * ATTN_SCALE
            dkv_ref[a:a + tq, 0:LANES] = dk[:, 0:LANES].astype(BF16)
            dkv_ref[a:a + tq, LANES:] = dva_ref[...].astype(BF16)
            dkr_ref[a:a + tq, :] += dk[:, LANES:]

        dq = dqa_ref[...] * ATTN_SCALE
        dq_ref[:, 0:LANES] = dq[:, 0:LANES].astype(BF16)
        d2 = dq[:, LANES:]
        dq_ref[:, LANES:] = (d2 * c_ref[...] - _swap_rope_halves(d2) * s_ref[...]).astype(BF16)

    hp = N_HEADS * HEAD_PAD
    head256 = pl.BlockSpec((t, HEAD_PAD), lambda h: (0, h))
    head128 = pl.BlockSpec((t, LANES), lambda h: (0, h))
    const128 = pl.BlockSpec((t, LANES), lambda h: (0, 0), pipeline_mode=once)
    return pl.pallas_call(
        body, name="flash_bwd",
        out_shape=(jax.ShapeDtypeStruct((t, hp), BF16), jax.ShapeDtypeStruct((t, hp), BF16),
                   jax.ShapeDtypeStruct((t, LANES), F32)),
        grid=(N_HEADS,),
        in_specs=[head256, head256, const128, head128, head128, head128, const128, const128],
        out_specs=(head256, head256, pl.BlockSpec((t, LANES), lambda h: (0, 0))),
        scratch_shapes=[pltpu.VMEM((t, HEAD_PAD), BF16), pltpu.VMEM((t, HEAD_PAD), F32),
                        pltpu.VMEM((tq, HEAD_PAD), F32), pltpu.VMEM((tq, LANES), F32)],
        compiler_params=_cparams(("arbitrary",)),
    )(qcat, kv, kr, dob, lse, dl, rc, rs)


def _q_norm_bwd(dcqn, projb, g, dprojb):
    t = projb.shape[0]
    tm = min(t, 512)
    r = Q_LORA_RANK

    def body(d_ref, c_ref, g_ref, alias_ref, o_ref, dg_ref):
        i = pl.program_id(0)
        c = c_ref[...]
        d = d_ref[...]
        rr = lax.rsqrt(jnp.mean(c * c, axis=-1, keepdims=True) + RMS_EPS)
        xh = c * rr

        @pl.when(i == 0)
        def _():
            dg_ref[...] = jnp.zeros_like(dg_ref)

        dg_ref[...] += jnp.sum(d * xh, axis=0, keepdims=True)
        dxh = d * g_ref[...]
        o_ref[...] = (rr * (dxh - xh * jnp.mean(dxh * xh, axis=-1, keepdims=True))).astype(BF16)

    blk = pl.BlockSpec((tm, r), lambda i: (i, 0))
    par = pl.BlockSpec((1, r), lambda i: (0, 0))
    return pl.pallas_call(
        body, name="q_norm_bwd",
        out_shape=(jax.ShapeDtypeStruct(dprojb.shape, BF16), jax.ShapeDtypeStruct((1, r), F32)),
        grid=(t // tm,),
        in_specs=[blk, blk, par, pl.BlockSpec(memory_space=pl.ANY)],
        out_specs=(blk, par),
        input_output_aliases={3: 0},
        compiler_params=_cparams(("arbitrary",)),
    )(dcqn, projb, g, dprojb)


def _kv_mid_bwd(dckn, ckv, dkr, g, rc, rs):
    t = ckv.shape[0]
    tm = min(t, 512)
    r = KV_LORA_RANK

    def body(d_ref, ckv_ref, dkr_ref, g_ref, c_ref, s_ref, o_ref, dg_ref):
        i = pl.program_id(0)
        c = ckv_ref[:, 0:r]
        d = d_ref[...]
        rr = lax.rsqrt(jnp.mean(c * c, axis=-1, keepdims=True) + RMS_EPS)
        xh = c * rr

        @pl.when(i == 0)
        def _():
            dg_ref[...] = jnp.zeros_like(dg_ref)

        dg_ref[...] += jnp.sum(d * xh, axis=0, keepdims=True)
        dxh = d * g_ref[...]
        o_ref[:, 0:r] = (rr * (dxh - xh * jnp.mean(dxh * xh, axis=-1, keepdims=True))).astype(BF16)
        dk = dkr_ref[...]
        o_ref[:, r:] = (dk * c_ref[...] - _swap_rope_halves(dk) * s_ref[...]).astype(BF16)

    return pl.pallas_call(
        body, name="kv_mid_bwd",
        out_shape=(jax.ShapeDtypeStruct((t, r + LANES), BF16), jax.ShapeDtypeStruct((1, r), F32)),
        grid=(t // tm,),
        in_specs=[pl.BlockSpec((tm, r), lambda i: (i, 0)), pl.BlockSpec((tm, r + LANES), lambda i: (i, 0)),
                  pl.BlockSpec((tm, LANES), lambda i: (i, 0)), pl.BlockSpec((1, r), lambda i: (0, 0)),
                  pl.BlockSpec((tm, LANES), lambda i: (i, 0)), pl.BlockSpec((tm, LANES), lambda i: (i, 0))],
        out_specs=(pl.BlockSpec((tm, r + LANES), lambda i: (i, 0)), pl.BlockSpec((1, r), lambda i: (0, 0))),
        compiler_params=_cparams(("arbitrary",)),
    )(dckn, ckv, dkr, g, rc, rs)


def _ln0_bwd(dr2, t1, t2, xh, rstd, g):
    t, d = dr2.shape
    tm = min(t, 256)

    def body(a_ref, b_ref, c_ref, xh_ref, rs_ref, g_ref, dr_ref, drb_ref, dg_ref, db_ref, dsum_ref):
        i = pl.program_id(0)
        dh = ALPHA * a_ref[...] + b_ref[...] + c_ref[...]
        xh = xh_ref[...]

        @pl.when(i == 0)
        def _():
            dg_ref[...] = jnp.zeros_like(dg_ref)
            db_ref[...] = jnp.zeros_like(db_ref)
            dsum_ref[...] = jnp.zeros_like(dsum_ref)

        dg_ref[...] += jnp.sum(dh * xh, axis=0, keepdims=True)
        db_ref[...] += jnp.sum(dh, axis=0, keepdims=True)
        dxh = dh * g_ref[...]
        dr = rs_ref[...] * (dxh - jnp.mean(dxh, axis=-1, keepdims=True) - xh * jnp.mean(dxh * xh, axis=-1, keepdims=True))
        dr_ref[...] = dr
        drb_ref[...] = dr.astype(BF16)
        dsum_ref[...] += jnp.sum(dr, axis=0, keepdims=True)

    row = pl.BlockSpec((tm, d), lambda i: (i, 0))
    par = pl.BlockSpec((1, d), lambda i: (0, 0))
    return pl.pallas_call(
        body, name="ln0_bwd",
        out_shape=(jax.ShapeDtypeStruct((t, d), F32), jax.ShapeDtypeStruct((t, d), BF16),
                   jax.ShapeDtypeStruct((1, d), F32), jax.ShapeDtypeStruct((1, d), F32),
                   jax.ShapeDtypeStruct((1, d), F32)),
        grid=(t // tm,),
        in_specs=[row, row, row, row, pl.BlockSpec((tm, 1), lambda i: (i, 0)), par],
        out_specs=(row, row, par, par, par),
        compiler_params=_cparams(("arbitrary",)),
    )(dr2, t1, t2, xh, rstd, g)


def _conv_mid_bwd_rows(u1, proj, du2, ng, nb):
    t = u1.shape[0]
    e = CONV_WIDTH
    tm = min(t, 256)

    def body(u1_ref, z_ref, d_ref, ng_ref, nb_ref, du1_ref, dz_ref, dng_ref, dnb_ref, dbz_ref):
        i = pl.program_id(0)
        u1 = u1_ref[...]
        mu = jnp.mean(u1, axis=-1, keepdims=True)
        xc = u1 - mu
        rstd = lax.rsqrt(jnp.mean(xc * xc, axis=-1, keepdims=True) + LN_EPS)
        xh = xc * rstd
        g = ng_ref[...]
        n = xh * g + nb_ref[...]
        sn = _sigmoid(n)
        z = z_ref[...]
        sz = _sigmoid(z)
        du2 = d_ref[...]
        dz = du2 * (n * sn) * (sz * (1.0 + z * (1.0 - sz)))
        dn = du2 * (z * sz) * (sn * (1.0 + n * (1.0 - sn)))

        @pl.when(i == 0)
        def _():
            dng_ref[...] = jnp.zeros_like(dng_ref)
            dnb_ref[...] = jnp.zeros_like(dnb_ref)
            dbz_ref[...] = jnp.zeros_like(dbz_ref)

        dng_ref[...] += jnp.sum(dn * xh, axis=0, keepdims=True)
        dnb_ref[...] += jnp.sum(dn, axis=0, keepdims=True)
        dbz_ref[...] += jnp.sum(dz, axis=0, keepdims=True)
        dz_ref[...] = dz.astype(BF16)
        dxh = dn * g
        du1_ref[...] = rstd * (dxh - jnp.mean(dxh, axis=-1, keepdims=True) - xh * jnp.mean(dxh * xh, axis=-1, keepdims=True))

    row = pl.BlockSpec((tm, e), lambda i: (i, 0))
    par = pl.BlockSpec((1, e), lambda i: (0, 0))
    return pl.pallas_call(
        body, name="conv_mid_bwd_rows",
        out_shape=(jax.ShapeDtypeStruct((t, e), F32), jax.ShapeDtypeStruct((t, 3 * e), BF16),
                   jax.ShapeDtypeStruct((1, e), F32), jax.ShapeDtypeStruct((1, e), F32),
                   jax.ShapeDtypeStruct((1, e), F32)),
        grid=(t // tm,),
        in_specs=[row, pl.BlockSpec((tm, e), lambda i: (i, 2)), row, par, par],
        out_specs=(row, pl.BlockSpec((tm, e), lambda i: (i, 2)), par, par, par),
        compiler_params=_cparams(("arbitrary",)),
    )(u1, proj, du2, ng, nb)


def _conv_bwd(du1, proj, cw, dproj):
    t = du1.shape[0]
    e = CONV_WIDTH
    tm = min(t, 128)
    hb = tm // HALO
    nt = t // tm
    nchunk = e // LANES
    last_halo = t // HALO - 1

    def body(d_ref, dn_ref, val_ref, gg_ref, valh_ref, ggh_ref, cw_ref, alias_ref,
             dp_ref, dcw_ref, dcb_ref, dbv_ref, extu_ref, extd_ref, du0_ref, acc_ref, sh_ref):
        i = pl.program_id(0)
        val = val_ref[...]
        sg = _sigmoid(gg_ref[...])
        h0 = valh_ref[...] * _sigmoid(ggh_ref[...])
        extu_ref[0:HALO, :] = jnp.where(i > 0, h0, 0.0)
        extu_ref[HALO:, :] = val * sg
        extd_ref[0:tm, :] = d_ref[...]
        extd_ref[tm:, :] = jnp.where(i < nt - 1, dn_ref[...], 0.0)

        @pl.when(i == 0)
        def _():
            acc_ref[...] = jnp.zeros_like(acc_ref)
            dbv_ref[...] = jnp.zeros_like(dbv_ref)

        def chunk(c, carry):
            col = pl.multiple_of(c * LANES, LANES)
            d = extd_ref[0:tm, pl.ds(col, LANES)]
            du0 = jnp.zeros((tm, LANES), F32)
            for res, taps in _taps_by_residue(lambda j: CONV_KERNEL - 1 - j):
                span = 8 * max(a for _, a in taps) + tm
                sh_ref[0:span, :] = extd_ref[res:res + span, pl.ds(col, LANES)]
                for j, a in taps:
                    du0 = du0 + cw_ref[j:j + 1, pl.ds(col, LANES)] * sh_ref[8 * a:8 * a + tm, :]
            for res, taps in _taps_by_residue(lambda j: HALO - (CONV_KERNEL - 1) + j):
                span = 8 * max(a for _, a in taps) + tm
                sh_ref[0:span, :] = extu_ref[res:res + span, pl.ds(col, LANES)]
                for j, a in taps:
                    prod = d * sh_ref[8 * a:8 * a + tm, :]
                    acc_ref[j * 8:(j + 1) * 8, pl.ds(col, LANES)] += _fold_rows(prod)
            acc_ref[CONV_KERNEL * 8:(CONV_KERNEL + 1) * 8, pl.ds(col, LANES)] += _fold_rows(d)
            du0_ref[:, pl.ds(col, LANES)] = du0
            return carry

        lax.fori_loop(0, nchunk, chunk, 0)
        du0 = du0_ref[...]
        dval = du0 * sg
        dgg = du0 * val * sg * (1.0 - sg)
        dp_ref[:, 0:e] = dval.astype(BF16)
        dp_ref[:, e:] = dgg.astype(BF16)
        dbv_ref[:, 0:e] += jnp.sum(dval, axis=0, keepdims=True)
        dbv_ref[:, e:] += jnp.sum(dgg, axis=0, keepdims=True)

        @pl.when(i == nt - 1)
        def _():
            for j in range(CONV_KERNEL):
                dcw_ref[j:j + 1, :] = jnp.sum(acc_ref[j * 8:(j + 1) * 8, :], axis=0, keepdims=True)
            dcw_ref[CONV_KERNEL:, :] = jnp.zeros((HALO - CONV_KERNEL, e), F32)
            dcb_ref[...] = jnp.sum(acc_ref[CONV_KERNEL * 8:(CONV_KERNEL + 1) * 8, :], axis=0, keepdims=True)

    row = lambda c: pl.BlockSpec((tm, e), lambda i: (i, c))
    halo_prev = lambda c: pl.BlockSpec((HALO, e), lambda i: (jnp.maximum(i * hb - 1, 0), c))
    halo_next = pl.BlockSpec((HALO, e), lambda i: (jnp.minimum((i + 1) * hb, last_halo), 0))
    return pl.pallas_call(
        body, name="conv_bwd",
        out_shape=(jax.ShapeDtypeStruct(dproj.shape, BF16), jax.ShapeDtypeStruct((HALO, e), F32),
                   jax.ShapeDtypeStruct((1, e), F32), jax.ShapeDtypeStruct((1, 2 * e), F32)),
        grid=(nt,),
        in_specs=[row(0), halo_next, row(0), row(1), halo_prev(0), halo_prev(1),
                  pl.BlockSpec((HALO, e), lambda i: (0, 0)), pl.BlockSpec(memory_space=pl.ANY)],
        out_specs=(pl.BlockSpec((tm, 2 * e), lambda i: (i, 0)), pl.BlockSpec((HALO, e), lambda i: (0, 0)),
                   pl.BlockSpec((1, e), lambda i: (0, 0)), pl.BlockSpec((1, 2 * e), lambda i: (0, 0))),
        scratch_shapes=[pltpu.VMEM((tm + HALO, e), F32), pltpu.VMEM((tm + HALO, e), F32),
                        pltpu.VMEM((tm, e), F32), pltpu.VMEM(((CONV_KERNEL + 1) * 8, e), F32),
                        pltpu.VMEM((tm + HALO, LANES), F32)],
        input_output_aliases={7: 0},
        compiler_params=_cparams(("arbitrary",)),
    )(du1, du1, proj, proj, proj, proj, cw, dproj)


def _adam_update(g, w, m, v):
    c1 = 1.0 - ADAM_B1 ** ADAM_STEP
    c2 = 1.0 - ADAM_B2 ** ADAM_STEP
    mn = ADAM_B1 * m + (1.0 - ADAM_B1) * g
    vn = ADAM_B2 * v + (1.0 - ADAM_B2) * (g * g)
    delta = -ADAM_LR * ((mn / c1) / (jnp.sqrt(vn / c2) + ADAM_EPS) + ADAM_WD * w)
    return delta, mn, vn


def _adamw(name, gbuf, w, m, v):
    r, c = w.shape
    tr = min(r, 256)
    assert r % tr == 0

    def body(g_ref, w_ref, m_ref, v_ref, go_ref, d_ref, mo_ref, vo_ref):
        g = g_ref[0].astype(F32)
        for k in range(1, N_DEV):
            g = g + g_ref[k].astype(F32)
        go_ref[...] = g
        d_ref[...], mo_ref[...], vo_ref[...] = _adam_update(g, w_ref[...], m_ref[...], v_ref[...])

    blk = pl.BlockSpec((tr, c), lambda i: (i, 0))
    shp = jax.ShapeDtypeStruct((r, c), F32)
    return pl.pallas_call(
        body, name="adamw_" + name,
        out_shape=(shp, shp, shp, shp),
        grid=(r // tr,),
        in_specs=[pl.BlockSpec((N_DEV, tr, c), lambda i: (0, i, 0)), blk, blk, blk],
        out_specs=(blk, blk, blk, blk),
        compiler_params=_cparams(("parallel",)),
    )(gbuf, w, m, v)


SMALL_TABLE_COLS = 256
SMALL_LAYOUT = {
    'ln_g': (0, 2, 1024), 'ln_b': (8, 2, 1024), 'a_b_in': (16, 1, 768), 'a_conv_b': (24, 1, 256),
    'a_norm_g': (32, 1, 256), 'a_norm_b': (40, 1, 256), 'a_b_out': (48, 1, 128), 'kv_norm_g': (56, 1, 256),
    'b_q_norm_g': (64, 1, 512), 'a_conv_w': (72, CONV_KERNEL, 256),
}
SMALL_TABLE_ROWS = 104
SMALL_NAMES = tuple(SMALL_LAYOUT)


def _small_pieces(name, row):
    r0, _, c = SMALL_LAYOUT[name]
    w = min(c, SMALL_TABLE_COLS)
    per_row = c // w
    return [(r0 + row * per_row + q, q * w, w) for q in range(per_row)]


def _pack_small_grads(dg0, dg1, db0, db1, dbv, dbz, dcb, dng, dnb, dba, dgkv, dgq, dcw):
    e = CONV_WIDTH
    ce = e // N_DEV

    def body(dg0_ref, dg1_ref, db0_ref, db1_ref, dbv_ref, dbz_ref, dcb_ref, dng_ref, dnb_ref, dba_ref, dgkv_ref,
             dgq_ref, dcw_ref, o_ref):
        o_ref[...] = jnp.zeros_like(o_ref)

        def put(d, name, row, src_ref, first_col=0):
            for trow, col, w in _small_pieces(name, row):
                o_ref[d, trow:trow + 1, 0:w] = src_ref[:, first_col + col:first_col + col + w]

        for d in range(N_DEV):
            put(d, 'ln_g', 0, dg0_ref)
            put(d, 'ln_g', 1, dg1_ref)
            put(d, 'ln_b', 0, db0_ref)
            put(d, 'ln_b', 1, db1_ref)
            for trow, col, w in _small_pieces('a_b_in', 0):
                gcol = d * 3 * ce + col
                src = dbv_ref[:, gcol:gcol + w] if gcol < 2 * e else dbz_ref[:, gcol - 2 * e:gcol - 2 * e + w]
                o_ref[d, trow:trow + 1, 0:w] = src
            put(d, 'a_conv_b', 0, dcb_ref, d * ce)
            put(d, 'a_norm_g', 0, dng_ref, d * ce)
            put(d, 'a_norm_b', 0, dnb_ref, d * ce)
            put(d, 'a_b_out', 0, dba_ref, d * LANES)
            put(d, 'kv_norm_g', 0, dgkv_ref)
            put(d, 'b_q_norm_g', 0, dgq_ref)
            r0 = SMALL_LAYOUT['a_conv_w'][0]
            o_ref[d, r0:r0 + HALO, 0:ce] = dcw_ref[:, d * ce:(d + 1) * ce]

    return pl.pallas_call(
        body, name="pack_small_grads",
        out_shape=jax.ShapeDtypeStruct((N_DEV, SMALL_TABLE_ROWS, SMALL_TABLE_COLS), F32),
        compiler_params=pltpu.CompilerParams(vmem_limit_bytes=VMEM_LIMIT),
    )(dg0, dg1, db0, db1, dbv, dbz, dcb, dng, dnb, dba, dgkv, dgq, dcw)


def _adamw_small(gtab, ws, ms, vs):
    n = len(SMALL_NAMES)

    def body(*refs):
        g_ref = refs[0]
        w_refs, m_refs, v_refs = refs[1:1 + n], refs[1 + n:1 + 2 * n], refs[1 + 2 * n:1 + 3 * n]
        outs = refs[1 + 3 * n:]

        def summed(r0, r, c):
            g = g_ref[0, r0:r0 + r, 0:c]
            for k in range(1, N_DEV):
                g = g + g_ref[k, r0:r0 + r, 0:c]
            return g

        for i, name in enumerate(SMALL_NAMES):
            r0, r, c = SMALL_LAYOUT[name]
            if c <= SMALL_TABLE_COLS:
                blocks = [(slice(None), summed(r0, r, c))]
            else:
                blocks = [(slice(row, row + 1), jnp.concatenate([summed(tr, 1, w) for tr, _, w in _small_pieces(name, row)], axis=1))
                          for row in range(r)]
            for rows, g in blocks:
                delta, mn, vn = _adam_update(g, w_refs[i][rows, :], m_refs[i][rows, :], v_refs[i][rows, :])
                outs[i][rows, :] = g
                outs[n + i][rows, :] = delta
                outs[2 * n + i][rows, :] = mn
                outs[3 * n + i][rows, :] = vn

    shapes = tuple(jax.ShapeDtypeStruct(w.shape, F32) for w in ws)
    res = pl.pallas_call(
        body, name="adamw_small",
        out_shape=shapes * 4,
        compiler_params=pltpu.CompilerParams(vmem_limit_bytes=VMEM_LIMIT),
    )(gtab, *ws, *ms, *vs)
    return res[:n], res[n:2 * n], res[2 * n:3 * n], res[3 * n:]


def _mesh_peers():
    x, y, c = lax.axis_index("x"), lax.axis_index("y"), lax.axis_index("c")
    me = 4 * x + 2 * y + c
    peers = []
    for k in range(1, N_DEV):
        px = 1 - x if (k >> 2) & 1 else x
        py = 1 - y if (k >> 1) & 1 else y
        pc = 1 - c if k & 1 else c
        peers.append(((px, py, pc), 4 * px + 2 * py + pc))
    return me, peers


def _all_gather(name, shards):
    na = len(shards)

    def body(*refs):
        srcs, dsts = refs[:na], refs[na:2 * na]
        send_sems, recv_sems, local_sems = refs[2 * na:]
        x, y, c = lax.axis_index("x"), lax.axis_index("y"), lax.axis_index("c")
        me, sibling = (x, y, c), (x, y, 1 - c)
        chips = [(1 - x, y), (x, 1 - y), (1 - x, 1 - y)]
        slot = lambda dev: 4 * dev[0] + 2 * dev[1] + dev[2]

        def copy(a, k, block, to, own=False):
            return pltpu.make_async_remote_copy(
                src_ref=srcs[a] if own else dsts[a].at[slot(block)], dst_ref=dsts[a].at[slot(block)],
                send_sem=send_sems.at[a, k], recv_sem=recv_sems.at[a, k], device_id=to,
                device_id_type=pl.DeviceIdType.MESH)

        local = [pltpu.make_async_copy(srcs[a], dsts[a].at[slot(me)], local_sems.at[a]) for a in range(na)]
        for cp in local:
            cp.start()
        sends = []
        for a in range(na):
            sends.append(copy(a, 0, me, sibling, own=True))
            sends += [copy(a, 1 + j, me, (*chip, c), own=True) for j, chip in enumerate(chips)]
        for cp in sends:
            cp.start()
        for j, chip in enumerate(chips):
            for a in range(na):
                copy(a, 1 + j, (*chip, c), me).wait_recv()
                fwd = copy(a, 4 + j, (*chip, c), sibling)
                fwd.start()
                sends.append(fwd)
        for a in range(na):
            copy(a, 0, sibling, me).wait_recv()
            for j, chip in enumerate(chips):
                copy(a, 4 + j, (*chip, 1 - c), me).wait_recv()
        for cp in sends:
            cp.wait_send()
        for cp in local:
            cp.wait()

    hbm = pl.BlockSpec(memory_space=pl.ANY)
    return pl.pallas_call(
        body, name=name,
        out_shape=tuple(jax.ShapeDtypeStruct((N_DEV,) + s.shape, s.dtype) for s in shards),
        in_specs=[hbm] * na, out_specs=(hbm,) * na,
        scratch_shapes=[pltpu.SemaphoreType.DMA((na, N_DEV - 1)), pltpu.SemaphoreType.DMA((na, N_DEV - 1)),
                        pltpu.SemaphoreType.DMA((na,))],
    )(*shards)


_HBM_SPEC = pl.BlockSpec(memory_space=pltpu.HBM)
_SEM_SPEC = pl.BlockSpec(memory_space=pltpu.SEMAPHORE)


def _split_copies(srcs, lands, send_sems, recv_sems, local_sems, gather):
    me, peers = _mesh_peers()
    na = len(srcs)
    mine = lambda a, slot: srcs[a] if gather else srcs[a].at[slot]
    local = [pltpu.make_async_copy(mine(a, me), lands[a].at[me], local_sems.at[a]) for a in range(na)]
    sent, received = [], []
    for k, (dev, pid) in enumerate(peers):
        for a in range(na):
            s = a * (N_DEV - 1) + k
            sent.append(pltpu.make_async_remote_copy(
                src_ref=mine(a, pid), dst_ref=lands[a].at[me], send_sem=send_sems.at[s],
                recv_sem=recv_sems.at[s], device_id=dev, device_id_type=pl.DeviceIdType.MESH))
            received.append(pltpu.make_async_remote_copy(
                src_ref=mine(a, pid), dst_ref=lands[a].at[pid], send_sem=send_sems.at[s],
                recv_sem=recv_sems.at[s], device_id=dev, device_id_type=pl.DeviceIdType.MESH))
    return local, sent, received


def _comm_start(name, srcs, gather, after):
    na = len(srcs)
    land_structs = [jax.ShapeDtypeStruct(((N_DEV,) + s.shape) if gather else s.shape, s.dtype) for s in srcs]
    extra = [] if after is None else [after]
    n_in = 2 * na + len(extra)

    def body(*refs):
        srcs_r, lands_r = refs[:na], refs[na:2 * na]
        send_sems, recv_sems, local_sems = refs[n_in:n_in + 3]
        token = refs[-1]
        local, sent, _ = _split_copies(srcs_r, lands_r, send_sems, recv_sems, local_sems, gather)
        for cp in local + sent:
            cp.start()
        token[...] = jnp.zeros_like(token)

    sem = pltpu.SemaphoreType.DMA((na * (N_DEV - 1),))
    outs = pl.pallas_call(
        body, name=name,
        out_shape=(sem, sem, pltpu.SemaphoreType.DMA((na,)),
                   *[pltpu.HBM(s.shape, s.dtype) for s in srcs],
                   *[pltpu.HBM(s.shape, s.dtype) for s in land_structs],
                   jax.ShapeDtypeStruct((8, LANES), F32)),
        in_specs=[_HBM_SPEC] * (2 * na) + [pl.BlockSpec(memory_space=pl.ANY)] * len(extra),
        out_specs=(_SEM_SPEC, _SEM_SPEC, _SEM_SPEC, *([_HBM_SPEC] * (2 * na)), pl.BlockSpec(memory_space=pltpu.VMEM)),
        input_output_aliases={i: 3 + i for i in range(2 * na)},
        compiler_params=pltpu.CompilerParams(has_side_effects=pltpu.SideEffectType.DATAFLOW_SIDE_EFFECTING),
    )(*[pltpu.with_memory_space_constraint(s, pltpu.HBM) for s in srcs],
      *[pltpu.with_memory_space_constraint(lax.empty(s.shape, s.dtype), pltpu.HBM) for s in land_structs],
      *extra)
    return (outs[:3], outs[3:3 + na], outs[3 + na:3 + 2 * na]), outs[-1]


def _comm_wait(name, handles, gather, after):
    (send_sems, recv_sems, local_sems), srcs, lands = handles
    na = len(srcs)

    def body(*refs):
        srcs_r, lands_r = refs[:na], refs[na:2 * na]
        send_r, recv_r, local_r = refs[2 * na:2 * na + 3]
        local, sent, received = _split_copies(srcs_r, lands_r, send_r, recv_r, local_r, gather)
        for cp in sent:
            cp.wait_send()
        for cp in received:
            cp.wait_recv()
        for cp in local:
            cp.wait()

    outs = pl.pallas_call(
        body, name=name,
        out_shape=(*[pltpu.HBM(s.shape, s.dtype) for s in srcs], *[pltpu.HBM(s.shape, s.dtype) for s in lands]),
        in_specs=[_HBM_SPEC] * (2 * na) + [_SEM_SPEC] * 3 + [pl.BlockSpec(memory_space=pl.ANY)],
        out_specs=tuple([_HBM_SPEC] * (2 * na)),
        input_output_aliases={i: i for i in range(2 * na)},
        compiler_params=pltpu.CompilerParams(has_side_effects=pltpu.SideEffectType.DATAFLOW_SIDE_EFFECTING),
    )(*srcs, *lands, send_sems, recv_sems, local_sems, after)
    return outs[na:]


def _prep_weights(w):
    e = CONV_WIDTH
    p = {}
    if 'a_w_in' in w:
        if w['a_w_in'].shape == (N_DEV, D_MODEL, 3 * e // N_DEV):
            p['a_w_in3'] = w['a_w_in']
        else:
            p['a_w_in3'] = w['a_w_in'].reshape(D_MODEL, N_DEV, 3 * e // N_DEV).transpose(1, 0, 2)
        p['a_b_in'] = w['a_b_in'].reshape(1, 3 * e)
        p['a_conv_w'] = jnp.pad(w['a_conv_w'].reshape(CONV_KERNEL, e), ((0, HALO - CONV_KERNEL), (0, 0)))
        p['a_conv_b'] = w['a_conv_b'].reshape(1, e)
        p['a_norm_g'] = w['a_norm_g'].reshape(1, e)
        p['a_norm_b'] = w['a_norm_b'].reshape(1, e)
        p['a_b_out'] = w['a_b_out'].reshape(1, D_MODEL)
        p['kv_norm_g'] = w['kv_norm_g'].reshape(1, KV_LORA_RANK)
        p['b_q_norm_g'] = w['b_q_norm_g'].reshape(1, Q_LORA_RANK)
        p['ln_g'] = w['ln_g']
        p['ln_b'] = w['ln_b']
    if 'a_w_out' in w:
        p['a_w_out'] = w['a_w_out'].reshape(e, D_MODEL)
        p['kv_w_down'] = jnp.pad(w['kv_w_down'], ((0, 0), (0, KV_LORA_RANK + LANES - w['kv_w_down'].shape[1])))
        p['kv_w_ukv'] = jnp.concatenate([w['kv_w_uk'], w['kv_w_uv']], axis=2).reshape(KV_LORA_RANK, N_HEADS * HEAD_PAD)
        p['b_w_in'] = w['b_w_in'].reshape(D_MODEL, -1)
        uq = w['b_w_uq'].reshape(Q_LORA_RANK, N_HEADS, QK_NOPE_DIM + QK_ROPE_DIM)
        p['b_w_uq'] = jnp.pad(uq, ((0, 0), (0, 0), (0, HEAD_PAD - uq.shape[2]))).reshape(Q_LORA_RANK, N_HEADS * HEAD_PAD)
        p['b_w_out'] = w['b_w_out'].reshape(N_HEADS * V_HEAD_DIM, D_MODEL)
    return p


class _NoComm:
    def __init__(self):
        self.grads = {}
        self.small = None

    def first_after(self):
        return None

    def rest_weights(self, after):
        return {}

    def send_group1(self, g):
        self.grads.update(g)
        return None

    def send_group1b(self, g):
        self.grads.update(g)
        return None

    def send_group2(self, g, small):
        self.grads.update(g)
        self.small = small
        return None


def _local_step(x, positions, target, p, comm):
    t = x.shape[0]
    e = CONV_WIDTH
    freqs = ROPE_THETA ** (-jnp.arange(0, QK_ROPE_DIM, 2, dtype=F32) / QK_ROPE_DIM)
    freq_row = jnp.concatenate([freqs, freqs, jnp.zeros((LANES - QK_ROPE_DIM,), F32)]).reshape(1, LANES)
    rc, rs = _rope_tables(positions.reshape(t, 1), freq_row)
    xb = x.astype(BF16)
    ln_g0, ln_b0 = p['ln_g'][0:1], p['ln_b'][0:1]
    ln_g1, ln_b1 = p['ln_g'][1:2], p['ln_b'][1:2]

    proj = _matmul(xb, p['a_w_in3'], bias=p['a_b_in'], name="a_in", b_blocked=True, after=comm.first_after())
    u1, u2 = _conv_mid_fwd(proj, p['a_conv_w'], p['a_conv_b'], p['a_norm_g'], p['a_norm_b'])
    p = {**p, **comm.rest_weights(u2)}
    y = _matmul(u2, p['a_w_out'], bias=p['a_b_out'], name="a_out", bk=2048)
    h1, h1b, xh1, rstd1 = _ln_res_fwd(x, y, ln_g0, ln_b0)
    ckv = _matmul(h1b, p['kv_w_down'], name="kv_down")
    ckn, kr = _kv_mid_fwd(ckv, p['kv_norm_g'], rc, rs)
    kv = _matmul(ckn, p['kv_w_ukv'], name="kv_up", out_dtype=BF16)
    projb = _matmul(h1b, p['b_w_in'], name="b_in", bn=1280)
    cqn = _q_norm_fwd(projb, p['b_q_norm_g'])
    qcat = _q_up_rope(cqn, p['b_w_uq'], rc, rs)
    o, o2, lse = _flash_fwd(qcat, kv, kr, projb)
    yb = _matmul(o2, p['b_w_out'], name="b_out", bk=2048)
    loss, dr2, dr2b, dg1, db1 = _final_ln_loss(h1, yb, ln_g1, ln_b1, target)

    g = {}
    g['b_w_out'] = _matmul(o2, dr2b, trans_a=True, name="d_b_out", out_dtype=BF16, bm=512, bk=t)
    do2 = _matmul(dr2b, p['b_w_out'], trans_b=True, name="d_o2")
    dob, dprojb, dl = _gate_bwd(do2, o, projb)
    dq2, dkv, dkr = _flash_bwd(qcat, kv, kr, dob, lse, dl, rc, rs)
    duq = _matmul(cqn, dq2, trans_a=True, name="d_q_up", out_dtype=BF16, bk=t)
    dcqn = _matmul(dq2, p['b_w_uq'], trans_b=True, name="d_cqn", bk=N_HEADS * HEAD_PAD)
    dprojb, dgq = _q_norm_bwd(dcqn, projb, p['b_q_norm_g'], dprojb)
    g['b_w_in'] = _matmul(h1b, dprojb, trans_a=True, name="d_b_in", bn=640, bk=t, out_dtype=BF16)
    t1 = _matmul(dprojb, p['b_w_in'], trans_b=True, name="d_h1_b", bk=dprojb.shape[1])
    dukv = _matmul(ckn, dkv, trans_a=True, name="d_kv_up", out_dtype=BF16, bk=t)
    dckn = _matmul(dkv, p['kv_w_ukv'], trans_b=True, name="d_ckn", bk=N_HEADS * HEAD_PAD)
    dckv, dgkv = _kv_mid_bwd(dckn, ckv, dkr, p['kv_norm_g'], rc, rs)
    ddown = _matmul(h1b, dckv, trans_a=True, name="d_kv_down", out_dtype=BF16, bm=512, bk=t)
    g['b_w_out'] = g['b_w_out'].reshape(N_DEV, -1, D_MODEL)
    g['b_w_in'] = g['b_w_in'].reshape(D_MODEL, N_DEV, -1).transpose(1, 0, 2)
    g['kv_w_down'] = ddown[:, :KV_LORA_RANK + QK_ROPE_DIM].reshape(N_DEV, -1, KV_LORA_RANK + QK_ROPE_DIM)
    dukv = dukv.reshape(KV_LORA_RANK, N_HEADS, HEAD_PAD)
    g['kv_w_uk'] = dukv[:, :, :QK_NOPE_DIM].reshape(N_DEV, -1, QK_NOPE_DIM)
    g['kv_w_uv'] = dukv[:, :, QK_NOPE_DIM:].reshape(N_DEV, -1, V_HEAD_DIM)
    g['b_w_uq'] = duq.reshape(Q_LORA_RANK, N_HEADS, HEAD_PAD)[:, :, :QK_NOPE_DIM + QK_ROPE_DIM].reshape(
        N_DEV, -1, QK_NOPE_DIM + QK_ROPE_DIM)
    sent1 = comm.send_group1(g)
    t2 = _matmul(dckv, p['kv_w_down'], trans_b=True, name="d_h1_kv", after=sent1)
    dr1, dr1b, dg0, db0, dba_out = _ln0_bwd(dr2, t1, t2, xh1, rstd1, ln_g0)
    dw_out = _matmul(u2, dr1b, trans_a=True, name="d_a_out", out_dtype=BF16, bm=512, bk=t).reshape(N_DEV, -1, D_MODEL)
    sent1b = comm.send_group1b({'a_w_out': dw_out})
    du2 = _matmul(dr1b, p['a_w_out'], trans_b=True, name="d_u2", after=sent1b)
    du1, dproj, dng, dnb, dbz = _conv_mid_bwd_rows(u1, proj, du2, p['a_norm_g'], p['a_norm_b'])
    dproj, dcw, dcb, dbv = _conv_bwd(du1, proj, p['a_conv_w'], dproj)
    dw_in = _matmul(xb, dproj, trans_a=True, name="d_a_in", out_dtype=BF16, bn=3 * e // N_DEV, bk=t, out_blocked=True)
    small = (dg0, dg1, db0, db1, dbv, dbz, dcb, dng, dnb, dba_out, dgkv, dgq, dcw)
    sent2 = comm.send_group2({'a_w_in': dw_in}, small)
    grad_x = _grad_x(dproj, p['a_w_in3'], dr1, sent2)
    return loss, grad_x


def _grad_x(dproj, w3, dr1, after):
    t, d = dr1.shape
    nblk, _, cb = w3.shape
    tm = min(t, 512)
    extra = [] if after is None else [after]

    def body(a_ref, w_ref, dr_ref, *rest):
        o_ref = rest[-1]
        acc = ALPHA * dr_ref[...]
        for k in range(nblk):
            acc = acc + lax.dot_general(a_ref[:, k * cb:(k + 1) * cb], w_ref[k], NT, preferred_element_type=F32)
        o_ref[...] = acc

    row = pl.BlockSpec((tm, d), lambda i: (i, 0))
    return pl.pallas_call(
        body, name="grad_x",
        out_shape=jax.ShapeDtypeStruct((t, d), F32),
        grid=(t // tm,),
        in_specs=[pl.BlockSpec((tm, nblk * cb), lambda i: (i, 0)),
                  pl.BlockSpec((nblk, d, cb), lambda i: (0, 0, 0), pipeline_mode=pl.Buffered(1)), row]
                 + [pl.BlockSpec(memory_space=pl.ANY)] * len(extra),
        out_specs=row,
        compiler_params=_cparams(("parallel",)),
    )(dproj, w3, dr1, *extra)


def _shard_2d(a):
    return a.reshape(-1, a.shape[-1])


def _gathered_to_full(name, blocks):
    if name == 'a_w_in':
        return blocks
    if name == 'b_w_in':
        return blocks.transpose(1, 0, 2).reshape(D_MODEL, -1)
    if name == 'a_conv_w':
        return blocks.transpose(1, 0, 2).reshape(CONV_KERNEL, -1)
    if name in ('a_b_in', 'a_conv_b', 'a_norm_g', 'a_norm_b', 'a_b_out'):
        return blocks.reshape(-1)
    if name in ('kv_w_uk', 'kv_w_uv'):
        return blocks.reshape(KV_LORA_RANK, N_HEADS, -1)
    if name == 'b_w_uq':
        return blocks.reshape(Q_LORA_RANK, N_HEADS, -1)
    return blocks.reshape(-1, blocks.shape[-1])


def kernel(x, positions, ln_g, ln_b, a_w_in, a_b_in, a_conv_w, a_conv_b, a_norm_g, a_norm_b, a_w_out, a_b_out, kv_w_down, kv_norm_g, kv_w_uk, kv_w_uv, b_w_in, b_q_norm_g, b_w_uq, b_w_out, loss_target, m_ln_g, m_ln_b, m_a_w_in, m_a_b_in, m_a_conv_w, m_a_conv_b, m_a_norm_g, m_a_norm_b, m_a_w_out, m_a_b_out, m_kv_w_down, m_kv_norm_g, m_kv_w_uk, m_kv_w_uv, m_b_w_in, m_b_q_norm_g, m_b_w_uq, m_b_w_out, v_ln_g, v_ln_b, v_a_w_in, v_a_b_in, v_a_conv_w, v_a_conv_b, v_a_norm_g, v_a_norm_b, v_a_w_out, v_a_b_out, v_kv_w_down, v_kv_norm_g, v_kv_w_uk, v_kv_w_uv, v_b_w_in, v_b_q_norm_g, v_b_w_uq, v_b_w_out):
    w = dict(ln_g=ln_g, ln_b=ln_b, a_w_in=a_w_in, a_b_in=a_b_in, a_conv_w=a_conv_w, a_conv_b=a_conv_b,
             a_norm_g=a_norm_g, a_norm_b=a_norm_b, a_w_out=a_w_out, a_b_out=a_b_out, kv_w_down=kv_w_down,
             kv_norm_g=kv_norm_g, kv_w_uk=kv_w_uk, kv_w_uv=kv_w_uv, b_w_in=b_w_in, b_q_norm_g=b_q_norm_g,
             b_w_uq=b_w_uq, b_w_out=b_w_out)
    m = dict(ln_g=m_ln_g, ln_b=m_ln_b, a_w_in=m_a_w_in, a_b_in=m_a_b_in, a_conv_w=m_a_conv_w, a_conv_b=m_a_conv_b,
             a_norm_g=m_a_norm_g, a_norm_b=m_a_norm_b, a_w_out=m_a_w_out, a_b_out=m_a_b_out, kv_w_down=m_kv_w_down,
             kv_norm_g=m_kv_norm_g, kv_w_uk=m_kv_w_uk, kv_w_uv=m_kv_w_uv, b_w_in=m_b_w_in, b_q_norm_g=m_b_q_norm_g,
             b_w_uq=m_b_w_uq, b_w_out=m_b_w_out)
    v = dict(ln_g=v_ln_g, ln_b=v_ln_b, a_w_in=v_a_w_in, a_b_in=v_a_b_in, a_conv_w=v_a_conv_w, a_conv_b=v_a_conv_b,
             a_norm_g=v_a_norm_g, a_norm_b=v_a_norm_b, a_w_out=v_a_w_out, a_b_out=v_a_b_out, kv_w_down=v_kv_w_down,
             kv_norm_g=v_kv_norm_g, kv_w_uk=v_kv_w_uk, kv_w_uv=v_kv_w_uv, b_w_in=v_b_w_in, b_q_norm_g=v_b_q_norm_g,
             b_w_uq=v_b_w_uq, b_w_out=v_b_w_out)

    small_flat = jnp.concatenate([w[n].reshape(-1) for n in SMALL_WEIGHTS]).reshape(1, -1)
    ga = _all_gather("all_gather_first", [_shard_2d(w['a_w_in']).astype(BF16), small_flat])
    full = {'a_w_in': ga[0]}
    gs = ga[1].reshape(N_DEV, -1)
    off = 0
    for n in SMALL_WEIGHTS:
        size = math.prod(w[n].shape)
        full[n] = _gathered_to_full(n, gs[:, off:off + size].reshape((N_DEV,) + _shard_2d(w[n]).shape))
        off += size
    for n in REPLICATED:
        full[n] = w[n]
    rest_names = [n for n in MATMUL_WEIGHTS if n != 'a_w_in']
    group1 = ('b_w_out', 'b_w_uq', 'b_w_in', 'kv_w_uk', 'kv_w_uv', 'kv_w_down')
    group1b = ('a_w_out',)
    group2 = ('a_w_in',)

    class Comm:
        def __init__(self):
            self.rest, self.rest_token = _comm_start(
                "gather_rest_start", [_shard_2d(w[n]).astype(BF16) for n in rest_names], True, ga[0])

        def first_after(self):
            return self.rest_token

        def rest_weights(self, after):
            lands = _comm_wait("gather_rest_wait", self.rest, True, after)
            return _prep_weights({n: _gathered_to_full(n, lands[i]) for i, n in enumerate(rest_names)})

        def send_group1(self, g):
            self.g1, token = _comm_start("exchange_g1_start", [g[n] for n in group1], False, None)
            return token

        def send_group1b(self, g):
            self.g1b, token = _comm_start("exchange_g1b_start", [g[n] for n in group1b], False, None)
            return token

        def send_group2(self, g, small):
            self.g2, token = _comm_start("exchange_g2_start", [g[n] for n in group2] + [_pack_small_grads(*small)],
                                         False, None)
            return token

    comm = Comm()

    loss_loc, grad_x = _local_step(x[0], positions[0], loss_target[0], _prep_weights(full), comm)
    loss = lax.psum(loss_loc[0, 0], ("x", "y", "c"))

    res = {}
    recv1 = _comm_wait("exchange_g1_wait", comm.g1, False, grad_x)
    for i, n in enumerate(group1):
        res[n] = _adamw(n, recv1[i], _shard_2d(w[n]), _shard_2d(m[n]), _shard_2d(v[n]))
    recv1b = _comm_wait("exchange_g1b_wait", comm.g1b, False, res[group1[-1]][0])
    for i, n in enumerate(group1b):
        res[n] = _adamw(n, recv1b[i], _shard_2d(w[n]), _shard_2d(m[n]), _shard_2d(v[n]))
    recv2 = _comm_wait("exchange_g2_wait", comm.g2, False, res[group1b[-1]][0])
    for i, n in enumerate(group2):
        res[n] = _adamw(n, recv2[i], _shard_2d(w[n]), _shard_2d(m[n]), _shard_2d(v[n]))
    two_d = lambda d: [_shard_2d(d[n]) if d[n].ndim > 1 else d[n].reshape(1, -1) for n in SMALL_NAMES]
    sg, sd, sm, sv = _adamw_small(recv2[-1], two_d(w), two_d(m), two_d(v))
    for i, n in enumerate(SMALL_NAMES):
        res[n] = (sg[i], sd[i], sm[i], sv[i])
    outs = [[res[n][j].reshape(w[n].shape) for n in WEIGHT_NAMES] for j in range(4)]
    return (loss, grad_x[None], *outs[0], *outs[1], *outs[2], *outs[3])
```

```python
import functools
import math

import jax
import jax.numpy as jnp
from jax import lax
from jax.experimental import pallas as pl
from jax.experimental.pallas import tpu as pltpu

F32 = jnp.float32
BF16 = jnp.bfloat16

D_MODEL = 1024
DEPTH = 2
CONV_WIDTH = 2 * D_MODEL
CONV_KERNEL = 31
N_HEADS = 16
QK_NOPE_DIM = 128
QK_ROPE_DIM = 64
V_HEAD_DIM = 128
KV_LORA_RANK = D_MODEL // 4
Q_LORA_RANK = D_MODEL // 2
ROPE_THETA = 10000.0
LN_EPS = 1e-5
RMS_EPS = 1e-6
MASK_VALUE = -1e30
ALPHA = (2.0 * DEPTH) ** 0.25
ATTN_SCALE = 1.0 / math.sqrt(QK_NOPE_DIM + QK_ROPE_DIM)

ADAM_LR = 0.001
ADAM_B1 = 0.9
ADAM_B2 = 0.999
ADAM_EPS = 1e-08
ADAM_WD = 0.01
ADAM_STEP = 10

N_DEV = 8
LANES = 128
HEAD_PAD = 256
HALO = 32
VMEM_LIMIT = 56 * 1024 * 1024

WEIGHT_NAMES = ['ln_g', 'ln_b', 'a_w_in', 'a_b_in', 'a_conv_w', 'a_conv_b', 'a_norm_g', 'a_norm_b',
                'a_w_out', 'a_b_out', 'kv_w_down', 'kv_norm_g', 'kv_w_uk', 'kv_w_uv', 'b_w_in',
                'b_q_norm_g', 'b_w_uq', 'b_w_out']
REPLICATED = ('ln_g', 'ln_b', 'kv_norm_g', 'b_q_norm_g')
MATMUL_WEIGHTS = ('a_w_in', 'a_w_out', 'kv_w_down', 'kv_w_uk', 'kv_w_uv', 'b_w_in', 'b_w_uq', 'b_w_out')
SMALL_WEIGHTS = ('a_b_in', 'a_conv_w', 'a_conv_b', 'a_norm_g', 'a_norm_b', 'a_b_out')

NN = (((1,), (0,)), ((), ()))
NT = (((1,), (1,)), ((), ()))
TN = (((0,), (0,)), ((), ()))


def _cparams(sem):
    return pltpu.CompilerParams(dimension_semantics=sem, vmem_limit_bytes=VMEM_LIMIT)


def _sigmoid(x):
    return 0.5 * jnp.tanh(0.5 * x) + 0.5


def _aligned(v, m):
    return v if isinstance(v, int) else pl.multiple_of(v, m)


def _fold_rows(x):
    parts = [x[r:r + 8] for r in range(0, x.shape[0], 8)]
    while len(parts) > 1:
        parts = [parts[i] + parts[i + 1] for i in range(0, len(parts) - 1, 2)] + ([parts[-1]] if len(parts) % 2 else [])
    return parts[0]


def _taps_by_residue(offset_of):
    groups = {}
    for j in range(CONV_KERNEL):
        off = offset_of(j)
        groups.setdefault(off % 8, []).append((j, off // 8))
    return sorted(groups.items())


def _swap_rope_halves(t):
    lane = lax.broadcasted_iota(jnp.int32, t.shape, 1)
    return jnp.where(lane < QK_ROPE_DIM // 2, pltpu.roll(t, LANES - QK_ROPE_DIM // 2, 1),
                     pltpu.roll(t, QK_ROPE_DIM // 2, 1))


def _matmul(a, b, *, name, bias=None, trans_a=False, trans_b=False, out_dtype=F32, bm=1024, bn=1024, bk=1024,
            b_blocked=False, out_blocked=False, after=None):
    if trans_a:
        kdim, m = a.shape
    else:
        m, kdim = a.shape
    if b_blocked and trans_b:
        n, k2 = b.shape[1], b.shape[0] * b.shape[2]
        bk = b.shape[2]
    elif b_blocked:
        k2, n = b.shape[1], b.shape[0] * b.shape[2]
        bn = b.shape[2]
    elif trans_b:
        n, k2 = b.shape
    else:
        k2, n = b.shape
    assert kdim == k2, (a.shape, b.shape)
    bm, bn, bk = min(bm, m), min(bn, n), min(bk, kdim)
    assert m % bm == 0 and n % bn == 0 and kdim % bk == 0, (name, m, n, kdim, bm, bn, bk)
    nk = kdim // bk
    dims = (((0 if trans_a else 1,), (1 if trans_b else 0,)), ((), ()))
    has_bias = bias is not None

    def body(*refs):
        refs = list(refs)
        a_ref, b_ref = refs[:2]
        del refs[:2]
        bias_ref = refs.pop(0) if has_bias else None
        if after is not None:
            refs.pop(0)
        o_ref = refs.pop(0)
        rest = refs
        prod = lax.dot_general(a_ref[...], b_ref[...], dims, preferred_element_type=F32)

        def finish(acc):
            if has_bias:
                acc = acc + bias_ref[...]
            o_ref[...] = acc.astype(out_dtype)

        if nk == 1:
            finish(prod)
        else:
            acc_ref = rest[0]
            k = pl.program_id(2)

            @pl.when(k == 0)
            def _():
                acc_ref[...] = prod

            @pl.when(k > 0)
            def _():
                acc_ref[...] += prod

            @pl.when(k == nk - 1)
            def _():
                finish(acc_ref[...])

    a_spec = pl.BlockSpec((bk, bm), lambda i, j, k: (k, i)) if trans_a else pl.BlockSpec((bm, bk), lambda i, j, k: (i, k))
    if b_blocked and trans_b:
        b_spec = pl.BlockSpec((None, bn, bk), lambda i, j, k: (k, j, 0))
    elif b_blocked:
        b_spec = pl.BlockSpec((None, bk, bn), lambda i, j, k: (j, k, 0))
    elif trans_b:
        b_spec = pl.BlockSpec((bn, bk), lambda i, j, k: (j, k))
    else:
        b_spec = pl.BlockSpec((bk, bn), lambda i, j, k: (k, j))
    in_specs = [a_spec, b_spec]
    args = [a, b]
    if has_bias:
        in_specs.append(pl.BlockSpec((1, bn), lambda i, j, k: (0, j)))
        args.append(bias)
    if after is not None:
        in_specs.append(pl.BlockSpec(memory_space=pl.ANY))
        args.append(after)
    if out_blocked:
        out_struct = jax.ShapeDtypeStruct((n // bn, m, bn), out_dtype)
        out_spec = pl.BlockSpec((None, bm, bn), lambda i, j, k: (j, i, 0))
    else:
        out_struct = jax.ShapeDtypeStruct((m, n), out_dtype)
        out_spec = pl.BlockSpec((bm, bn), lambda i, j, k: (i, j))
    return pl.pallas_call(
        body, name=name,
        out_shape=out_struct,
        grid=(m // bm, n // bn, nk),
        in_specs=in_specs,
        out_specs=out_spec,
        scratch_shapes=[pltpu.VMEM((bm, bn), F32)] if nk > 1 else [],
        compiler_params=_cparams(("parallel", "parallel", "arbitrary")),
    )(*args)


def _rope_tables(pos_col, freq_row):
    t = pos_col.shape[0]
    tm = min(t, 512)

    def body(pos_ref, f_ref, c_ref, s_ref):
        ang = pos_ref[...].astype(F32) * f_ref[...]
        lane = lax.broadcasted_iota(jnp.int32, ang.shape, 1)
        cos = jnp.cos(ang)
        sin = jnp.sin(ang)
        c_ref[...] = jnp.where(lane < QK_ROPE_DIM, cos, 0.0)
        s_ref[...] = jnp.where(lane < QK_ROPE_DIM // 2, -sin, jnp.where(lane < QK_ROPE_DIM, sin, 0.0))

    return pl.pallas_call(
        body, name="rope_tables",
        out_shape=(jax.ShapeDtypeStruct((t, LANES), F32), jax.ShapeDtypeStruct((t, LANES), F32)),
        grid=(t // tm,),
        in_specs=[pl.BlockSpec((tm, 1), lambda i: (i, 0)), pl.BlockSpec((1, LANES), lambda i: (0, 0))],
        out_specs=(pl.BlockSpec((tm, LANES), lambda i: (i, 0)), pl.BlockSpec((tm, LANES), lambda i: (i, 0))),
        compiler_params=_cparams(("parallel",)),
    )(pos_col, freq_row)


def _conv_mid_fwd(proj, cw, cb, ng, nb):
    t = proj.shape[0]
    e = CONV_WIDTH
    tm = min(t, 128)
    hb = tm // HALO
    nchunk = e // LANES

    def body(val_ref, gg_ref, z_ref, valh_ref, ggh_ref, cw_ref, cb_ref, ng_ref, nb_ref, u1_ref, u2_ref, ext_ref, sh_ref):
        i = pl.program_id(0)
        u0 = val_ref[...] * _sigmoid(gg_ref[...])
        h0 = valh_ref[...] * _sigmoid(ggh_ref[...])
        ext_ref[0:HALO, :] = jnp.where(i > 0, h0, 0.0)
        ext_ref[HALO:, :] = u0

        def chunk(c, carry):
            col = pl.multiple_of(c * LANES, LANES)
            acc = jnp.zeros((tm, LANES), F32)
            for res, taps in _taps_by_residue(lambda j: HALO - (CONV_KERNEL - 1) + j):
                span = 8 * max(a for _, a in taps) + tm
                sh_ref[0:span, :] = ext_ref[res:res + span, pl.ds(col, LANES)]
                for j, a in taps:
                    acc = acc + cw_ref[j:j + 1, pl.ds(col, LANES)] * sh_ref[8 * a:8 * a + tm, :]
            u1_ref[:, pl.ds(col, LANES)] = acc + cb_ref[:, pl.ds(col, LANES)]
            return carry

        lax.fori_loop(0, nchunk, chunk, 0)
        g = ng_ref[...]
        b = nb_ref[...]
        for r in range(0, tm, 16):
            u1 = u1_ref[r:r + 16, :]
            mu = jnp.mean(u1, axis=-1, keepdims=True)
            xc = u1 - mu
            var = jnp.mean(xc * xc, axis=-1, keepdims=True)
            n = xc * lax.rsqrt(var + LN_EPS) * g + b
            z = z_ref[r:r + 16, :]
            u2_ref[r:r + 16, :] = ((n * _sigmoid(n)) * (z * _sigmoid(z))).astype(BF16)

    row = lambda c: pl.BlockSpec((tm, e), lambda i: (i, c))
    halo = lambda c: pl.BlockSpec((HALO, e), lambda i: (jnp.maximum(i * hb - 1, 0), c))
    par = lambda r: pl.BlockSpec((r, e), lambda i: (0, 0))
    return pl.pallas_call(
        body, name="conv_mid_fwd",
        out_shape=(jax.ShapeDtypeStruct((t, e), F32), jax.ShapeDtypeStruct((t, e), BF16)),
        grid=(t // tm,),
        in_specs=[row(0), row(1), row(2), halo(0), halo(1), par(HALO), par(1), par(1), par(1)],
        out_specs=(pl.BlockSpec((tm, e), lambda i: (i, 0)), pl.BlockSpec((tm, e), lambda i: (i, 0))),
        scratch_shapes=[pltpu.VMEM((tm + HALO, e), F32), pltpu.VMEM((tm + HALO, LANES), F32)],
        compiler_params=_cparams(("parallel",)),
    )(proj, proj, proj, proj, proj, cw, cb, ng, nb)


def _ln_res_fwd(x, y, g, b):
    t, d = x.shape
    tm = min(t, 256)

    def body(x_ref, y_ref, g_ref, b_ref, h_ref, hb_ref, xh_ref, rs_ref):
        r = ALPHA * x_ref[...] + y_ref[...]
        mu = jnp.mean(r, axis=-1, keepdims=True)
        xc = r - mu
        var = jnp.mean(xc * xc, axis=-1, keepdims=True)
        rstd = lax.rsqrt(var + LN_EPS)
        xh = xc * rstd
        h = xh * g_ref[...] + b_ref[...]
        h_ref[...] = h
        hb_ref[...] = h.astype(BF16)
        xh_ref[...] = xh
        rs_ref[...] = rstd

    row = pl.BlockSpec((tm, d), lambda i: (i, 0))
    par = pl.BlockSpec((1, d), lambda i: (0, 0))
    return pl.pallas_call(
        body, name="ln0_fwd",
        out_shape=(jax.ShapeDtypeStruct((t, d), F32), jax.ShapeDtypeStruct((t, d), BF16),
                   jax.ShapeDtypeStruct((t, d), F32), jax.ShapeDtypeStruct((t, 1), F32)),
        grid=(t // tm,),
        in_specs=[row, row, par, par],
        out_specs=(row, row, row, pl.BlockSpec((tm, 1), lambda i: (i, 0))),
        compiler_params=_cparams(("parallel",)),
    )(x, y, g, b)


def _kv_mid_fwd(ckv, g, rc, rs):
    t = ckv.shape[0]
    tm = min(t, 512)
    r = KV_LORA_RANK

    def body(ckv_ref, g_ref, c_ref, s_ref, ckn_ref, kr_ref):
        c = ckv_ref[:, 0:r]
        ms = jnp.mean(c * c, axis=-1, keepdims=True)
        ckn_ref[...] = (c * lax.rsqrt(ms + RMS_EPS) * g_ref[...]).astype(BF16)
        tr = ckv_ref[:, r:r + LANES]
        kr_ref[...] = (tr * c_ref[...] + _swap_rope_halves(tr) * s_ref[...]).astype(BF16)

    return pl.pallas_call(
        body, name="kv_mid_fwd",
        out_shape=(jax.ShapeDtypeStruct((t, r), BF16), jax.ShapeDtypeStruct((t, LANES), BF16)),
        grid=(t // tm,),
        in_specs=[pl.BlockSpec((tm, r + LANES), lambda i: (i, 0)), pl.BlockSpec((1, r), lambda i: (0, 0)),
                  pl.BlockSpec((tm, LANES), lambda i: (i, 0)), pl.BlockSpec((tm, LANES), lambda i: (i, 0))],
        out_specs=(pl.BlockSpec((tm, r), lambda i: (i, 0)), pl.BlockSpec((tm, LANES), lambda i: (i, 0))),
        compiler_params=_cparams(("parallel",)),
    )(ckv, g, rc, rs)


def _q_norm_fwd(projb, g):
    t = projb.shape[0]
    tm = min(t, 512)
    r = Q_LORA_RANK

    def body(c_ref, g_ref, o_ref):
        c = c_ref[...]
        ms = jnp.mean(c * c, axis=-1, keepdims=True)
        o_ref[...] = (c * lax.rsqrt(ms + RMS_EPS) * g_ref[...]).astype(BF16)

    return pl.pallas_call(
        body, name="q_norm_fwd",
        out_shape=jax.ShapeDtypeStruct((t, r), BF16),
        grid=(t // tm,),
        in_specs=[pl.BlockSpec((tm, r), lambda i: (i, 0)), pl.BlockSpec((1, r), lambda i: (0, 0))],
        out_specs=pl.BlockSpec((tm, r), lambda i: (i, 0)),
        compiler_params=_cparams(("parallel",)),
    )(projb, g)


def _q_up_rope(cqn, wuq, rc, rs):
    t, r = cqn.shape
    w = wuq.shape[1]
    tm = min(t, 1024)
    bn = 4 * HEAD_PAD

    def body(a_ref, b_ref, c_ref, s_ref, o_ref):
        q = lax.dot_general(a_ref[...], b_ref[...], NN, preferred_element_type=F32)
        c = c_ref[...]
        s = s_ref[...]
        for h in range(bn // HEAD_PAD):
            a = h * HEAD_PAD
            o_ref[:, a:a + LANES] = q[:, a:a + LANES].astype(BF16)
            tr = q[:, a + LANES:a + 2 * LANES]
            o_ref[:, a + LANES:a + 2 * LANES] = (tr * c + _swap_rope_halves(tr) * s).astype(BF16)

    return pl.pallas_call(
        body, name="q_up_rope",
        out_shape=jax.ShapeDtypeStruct((t, w), BF16),
        grid=(t // tm, w // bn),
        in_specs=[pl.BlockSpec((tm, r), lambda i, j: (i, 0)), pl.BlockSpec((r, bn), lambda i, j: (0, j)),
                  pl.BlockSpec((tm, LANES), lambda i, j: (i, 0)), pl.BlockSpec((tm, LANES), lambda i, j: (i, 0))],
        out_specs=pl.BlockSpec((tm, bn), lambda i, j: (i, j)),
        compiler_params=_cparams(("parallel", "parallel")),
    )(cqn, wuq, rc, rs)


def _flash_fwd(qcat, kv, kr, projb):
    t = qcat.shape[0]
    tq = min(t, 512)
    nq = t // tq
    exp2_scale = ATTN_SCALE * math.log2(math.e)
    z_first = Q_LORA_RANK // LANES

    def body(q_ref, kv_ref, kr_ref, z_ref, o_ref, o2_ref, lse_ref, kcat_ref):
        kcat_ref[:, 0:LANES] = kv_ref[:, 0:LANES]
        kcat_ref[:, LANES:] = kr_ref[...]
        rows = lax.broadcasted_iota(jnp.int32, (tq, tq), 0)
        cols = lax.broadcasted_iota(jnp.int32, (tq, tq), 1)
        for i in range(nq):
            a = i * tq
            q = q_ref[a:a + tq, :]
            sd = lax.dot_general(q, kcat_ref[a:a + tq, :], NT, preferred_element_type=F32)
            sd = jnp.where(cols <= rows, sd, MASK_VALUE)
            m = jnp.max(sd, axis=1, keepdims=True)
            if i > 0:
                sp = lax.dot_general(q, kcat_ref[0:a, :], NT, preferred_element_type=F32)
                m = jnp.maximum(m, jnp.max(sp, axis=1, keepdims=True))
            pd = jnp.exp2((sd - m) * exp2_scale)
            l = jnp.sum(pd, axis=1, keepdims=True)
            acc = lax.dot_general(pd.astype(BF16), kv_ref[a:a + tq, LANES:], NN, preferred_element_type=F32)
            if i > 0:
                pp = jnp.exp2((sp - m) * exp2_scale)
                l = l + jnp.sum(pp, axis=1, keepdims=True)
                acc = acc + lax.dot_general(pp.astype(BF16), kv_ref[0:a, LANES:], NN, preferred_element_type=F32)
            o = acc / l
            z = z_ref[a:a + tq, :]
            o_ref[a:a + tq, :] = o
            o2_ref[a:a + tq, :] = (o * (z * _sigmoid(z))).astype(BF16)
            lse_ref[a:a + tq, :] = jnp.broadcast_to(m * ATTN_SCALE + jnp.log(l), (tq, LANES))

    hw = N_HEADS * LANES
    head256 = pl.BlockSpec((t, HEAD_PAD), lambda h: (0, h))
    head128 = pl.BlockSpec((t, LANES), lambda h: (0, h))
    return pl.pallas_call(
        body, name="flash_fwd",
        out_shape=(jax.ShapeDtypeStruct((t, hw), F32), jax.ShapeDtypeStruct((t, hw), BF16),
                   jax.ShapeDtypeStruct((t, hw), F32)),
        grid=(N_HEADS,),
        in_specs=[head256, head256, pl.BlockSpec((t, LANES), lambda h: (0, 0), pipeline_mode=pl.Buffered(1)),
                  pl.BlockSpec((t, LANES), lambda h: (0, z_first + h))],
        out_specs=(head128, head128, head128),
        scratch_shapes=[pltpu.VMEM((t, HEAD_PAD), BF16)],
        compiler_params=_cparams(("parallel",)),
    )(qcat, kv, kr, projb)


def _final_ln_loss(h1, yb, g, b, target):
    t, d = h1.shape
    tm = min(t, 256)

    def body(h_ref, y_ref, g_ref, b_ref, t_ref, loss_ref, dr_ref, drb_ref, dg_ref, db_ref):
        i = pl.program_id(0)
        r = ALPHA * h_ref[...] + y_ref[...]
        mu = jnp.mean(r, axis=-1, keepdims=True)
        xc = r - mu
        var = jnp.mean(xc * xc, axis=-1, keepdims=True)
        rstd = lax.rsqrt(var + LN_EPS)
        xh = xc * rstd
        g = g_ref[...]
        diff = xh * g + b_ref[...] - t_ref[...]
        part = 0.5 * jnp.sum(jnp.mean(diff * diff, axis=-1, keepdims=True))
        dh = diff * (1.0 / d)
        dgp = jnp.sum(dh * xh, axis=0, keepdims=True)
        dbp = jnp.sum(dh, axis=0, keepdims=True)

        @pl.when(i == 0)
        def _():
            loss_ref[...] = jnp.zeros_like(loss_ref)
            dg_ref[...] = jnp.zeros_like(dg_ref)
            db_ref[...] = jnp.zeros_like(db_ref)

        loss_ref[...] += jnp.full(loss_ref.shape, part, F32)
        dg_ref[...] += dgp
        db_ref[...] += dbp
        dxh = dh * g
        dr = rstd * (dxh - jnp.mean(dxh, axis=-1, keepdims=True) - xh * jnp.mean(dxh * xh, axis=-1, keepdims=True))
        dr_ref[...] = dr
        drb_ref[...] = dr.astype(BF16)

    row = pl.BlockSpec((tm, d), lambda i: (i, 0))
    par = pl.BlockSpec((1, d), lambda i: (0, 0))
    return pl.pallas_call(
        body, name="final_ln_loss",
        out_shape=(jax.ShapeDtypeStruct((1, LANES), F32), jax.ShapeDtypeStruct((t, d), F32),
                   jax.ShapeDtypeStruct((t, d), BF16), jax.ShapeDtypeStruct((1, d), F32),
                   jax.ShapeDtypeStruct((1, d), F32)),
        grid=(t // tm,),
        in_specs=[row, row, par, par, row],
        out_specs=(pl.BlockSpec((1, LANES), lambda i: (0, 0)), row, row, par, par),
        compiler_params=_cparams(("arbitrary",)),
    )(h1, yb, g, b, target)


def _gate_bwd(do2, o, projb):
    t, hw = o.shape
    tm = min(t, 512)
    bw = Q_LORA_RANK
    nb = hw // bw
    wb = projb.shape[1]

    def body(d_ref, o_ref, z_ref, dob_ref, dz_ref, dl_ref):
        z = z_ref[...]
        sg = _sigmoid(z)
        d2 = d_ref[...]
        o = o_ref[...]
        do = d2 * (z * sg)
        dob_ref[...] = do.astype(BF16)
        dz_ref[...] = (d2 * o * (sg * (1.0 + z * (1.0 - sg)))).astype(BF16)
        prod = do * o
        for hh in range(bw // LANES):
            dsum = jnp.sum(prod[:, hh * LANES:(hh + 1) * LANES], axis=1, keepdims=True)
            dl_ref[:, hh * LANES:(hh + 1) * LANES] = jnp.broadcast_to(dsum, (tm, LANES))

    blk = pl.BlockSpec((tm, bw), lambda i, j: (i, j))
    blk1 = pl.BlockSpec((tm, bw), lambda i, j: (i, j + 1))
    return pl.pallas_call(
        body, name="gate_bwd",
        out_shape=(jax.ShapeDtypeStruct((t, hw), BF16), jax.ShapeDtypeStruct((t, wb), BF16),
                   jax.ShapeDtypeStruct((t, hw), F32)),
        grid=(t // tm, nb),
        in_specs=[blk, blk, blk1],
        out_specs=(blk, blk1, blk),
        compiler_params=_cparams(("parallel", "parallel")),
    )(do2, o, projb)


def _flash_bwd(qcat, kv, kr, dob, lse, dl, rc, rs):
    t = qcat.shape[0]
    tq = min(t, 512)
    nq = t // tq
    q_chunk = 2 * tq
    log2e = math.log2(math.e)
    exp2_scale = ATTN_SCALE * log2e
    once = pl.Buffered(1)

    def body(q_ref, kv_ref, kr_ref, do_ref, lse_ref, dl_ref, c_ref, s_ref,
             dq_ref, dkv_ref, dkr_ref, kcat_ref, dqa_ref, dka_ref, dva_ref):
        h = pl.program_id(0)
        kcat_ref[:, 0:LANES] = kv_ref[:, 0:LANES]
        kcat_ref[:, LANES:] = kr_ref[...]
        dqa_ref[...] = jnp.zeros_like(dqa_ref)

        @pl.when(h == 0)
        def _():
            dkr_ref[...] = jnp.zeros_like(dkr_ref)

        rows = lax.broadcasted_iota(jnp.int32, (tq, tq), 0)
        cols = lax.broadcasted_iota(jnp.int32, (tq, tq), 1)

        def block(qs, n, ks, masked):
            q = q_ref[qs:qs + n, :]
            kc = kcat_ref[ks:ks + tq, :]
            s = lax.dot_general(q, kc, NT, preferred_element_type=F32)
            if masked:
                s = jnp.where(cols <= rows, s, MASK_VALUE)
            p = jnp.exp2(s * exp2_scale - lse_ref[qs:qs + n, 0:1] * log2e)
            do = do_ref[qs:qs + n, :]
            dp = lax.dot_general(do, kv_ref[ks:ks + tq, LANES:], NT, preferred_element_type=F32)
            ds = (p * (dp - dl_ref[qs:qs + n, 0:1])).astype(BF16)
            dva_ref[...] += lax.dot_general(p.astype(BF16), do, TN, preferred_element_type=F32)
            dka_ref[...] += lax.dot_general(ds, q, TN, preferred_element_type=F32)
            dqa_ref[qs:qs + n, :] += lax.dot_general(ds, kc, NN, preferred_element_type=F32)

        for kb in range(nq):
            a = kb * tq
            dka_ref[...] = jnp.zeros_like(dka_ref)
            dva_ref[...] = jnp.zeros_like(dva_ref)
            block(a, tq, a, True)
            qs = a + tq
            while qs < t:
                n = min(q_chunk, t - qs)
                block(qs, n, a, False)
                qs += n
            dk = dka_ref[...] * ATTN_SCALE
            dkv_ref[a:a + tq, 0:LANES] = dk[:, 0:LANES].astype(BF16)
            dkv_ref[a:a + tq, LANES:] = dva_ref[...].astype(BF16)
            dkr_ref[a:a + tq, :] += dk[:, LANES:]

        dq = dqa_ref[...] * ATTN_SCALE
        dq_ref[:, 0:LANES] = dq[:, 0:LANES].astype(BF16)
        d2 = dq[:, LANES:]
        dq_ref[:, LANES:] = (d2 * c_ref[...] - _swap_rope_halves(d2) * s_ref[...]).astype(BF16)

    hp = N_HEADS * HEAD_PAD
    head256 = pl.BlockSpec((t, HEAD_PAD), lambda h: (0, h))
    head128 = pl.BlockSpec((t, LANES), lambda h: (0, h))
    const128 = pl.BlockSpec((t, LANES), lambda h: (0, 0), pipeline_mode=once)
    return pl.pallas_call(
        body, name="flash_bwd",
        out_shape=(jax.ShapeDtypeStruct((t, hp), BF16), jax.ShapeDtypeStruct((t, hp), BF16),
                   jax.ShapeDtypeStruct((t, LANES), F32)),
        grid=(N_HEADS,),
        in_specs=[head256, head256, const128, head128, head128, head128, const128, const128],
        out_specs=(head256, head256, pl.BlockSpec((t, LANES), lambda h: (0, 0))),
        scratch_shapes=[pltpu.VMEM((t, HEAD_PAD), BF16), pltpu.VMEM((t, HEAD_PAD), F32),
                        pltpu.VMEM((tq, HEAD_PAD), F32), pltpu.VMEM((tq, LANES), F32)],
        compiler_params=_cparams(("arbitrary",)),
    )(qcat, kv, kr, dob, lse, dl, rc, rs)


def _q_norm_bwd(dcqn, projb, g, dprojb):
    t = projb.shape[0]
    tm = min(t, 512)
    r = Q_LORA_RANK

    def body(d_ref, c_ref, g_ref, alias_ref, o_ref, dg_ref):
        i = pl.program_id(0)
        c = c_ref[...]
        d = d_ref[...]
        rr = lax.rsqrt(jnp.mean(c * c, axis=-1, keepdims=True) + RMS_EPS)
        xh = c * rr

        @pl.when(i == 0)
        def _():
            dg_ref[...] = jnp.zeros_like(dg_ref)

        dg_ref[...] += jnp.sum(d * xh, axis=0, keepdims=True)
        dxh = d * g_ref[...]
        o_ref[...] = (rr * (dxh - xh * jnp.mean(dxh * xh, axis=-1, keepdims=True))).astype(BF16)

    blk = pl.BlockSpec((tm, r), lambda i: (i, 0))
    par = pl.BlockSpec((1, r), lambda i: (0, 0))
    return pl.pallas_call(
        body, name="q_norm_bwd",
        out_shape=(jax.ShapeDtypeStruct(dprojb.shape, BF16), jax.ShapeDtypeStruct((1, r), F32)),
        grid=(t // tm,),
        in_specs=[blk, blk, par, pl.BlockSpec(memory_space=pl.ANY)],
        out_specs=(blk, par),
        input_output_aliases={3: 0},
        compiler_params=_cparams(("arbitrary",)),
    )(dcqn, projb, g, dprojb)


def _kv_mid_bwd(dckn, ckv, dkr, g, rc, rs):
    t = ckv.shape[0]
    tm = min(t, 512)
    r = KV_LORA_RANK

    def body(d_ref, ckv_ref, dkr_ref, g_ref, c_ref, s_ref, o_ref, dg_ref):
        i = pl.program_id(0)
        c = ckv_ref[:, 0:r]
        d = d_ref[...]
        rr = lax.rsqrt(jnp.mean(c * c, axis=-1, keepdims=True) + RMS_EPS)
        xh = c * rr

        @pl.when(i == 0)
        def _():
            dg_ref[...] = jnp.zeros_like(dg_ref)

        dg_ref[...] += jnp.sum(d * xh, axis=0, keepdims=True)
        dxh = d * g_ref[...]
        o_ref[:, 0:r] = (rr * (dxh - xh * jnp.mean(dxh * xh, axis=-1, keepdims=True))).astype(BF16)
        dk = dkr_ref[...]
        o_ref[:, r:] = (dk * c_ref[...] - _swap_rope_halves(dk) * s_ref[...]).astype(BF16)

    return pl.pallas_call(
        body, name="kv_mid_bwd",
        out_shape=(jax.ShapeDtypeStruct((t, r + LANES), BF16), jax.ShapeDtypeStruct((1, r), F32)),
        grid=(t // tm,),
        in_specs=[pl.BlockSpec((tm, r), lambda i: (i, 0)), pl.BlockSpec((tm, r + LANES), lambda i: (i, 0)),
                  pl.BlockSpec((tm, LANES), lambda i: (i, 0)), pl.BlockSpec((1, r), lambda i: (0, 0)),
                  pl.BlockSpec((tm, LANES), lambda i: (i, 0)), pl.BlockSpec((tm, LANES), lambda i: (i, 0))],
        out_specs=(pl.BlockSpec((tm, r + LANES), lambda i: (i, 0)), pl.BlockSpec((1, r), lambda i: (0, 0))),
        compiler_params=_cparams(("arbitrary",)),
    )(dckn, ckv, dkr, g, rc, rs)


def _ln0_bwd(dr2, t1, t2, xh, rstd, g):
    t, d = dr2.shape
    tm = min(t, 256)

    def body(a_ref, b_ref, c_ref, xh_ref, rs_ref, g_ref, dr_ref, drb_ref, dg_ref, db_ref, dsum_ref):
        i = pl.program_id(0)
        dh = ALPHA * a_ref[...] + b_ref[...] + c_ref[...]
        xh = xh_ref[...]

        @pl.when(i == 0)
        def _():
            dg_ref[...] = jnp.zeros_like(dg_ref)
            db_ref[...] = jnp.zeros_like(db_ref)
            dsum_ref[...] = jnp.zeros_like(dsum_ref)

        dg_ref[...] += jnp.sum(dh * xh, axis=0, keepdims=True)
        db_ref[...] += jnp.sum(dh, axis=0, keepdims=True)
        dxh = dh * g_ref[...]
        dr = rs_ref[...] * (dxh - jnp.mean(dxh, axis=-1, keepdims=True) - xh * jnp.mean(dxh * xh, axis=-1, keepdims=True))
        dr_ref[...] = dr
        drb_ref[...] = dr.astype(BF16)
        dsum_ref[...] += jnp.sum(dr, axis=0, keepdims=True)

    row = pl.BlockSpec((tm, d), lambda i: (i, 0))
    par = pl.BlockSpec((1, d), lambda i: (0, 0))
    return pl.pallas_call(
        body, name="ln0_bwd",
        out_shape=(jax.ShapeDtypeStruct((t, d), F32), jax.ShapeDtypeStruct((t, d), BF16),
                   jax.ShapeDtypeStruct((1, d), F32), jax.ShapeDtypeStruct((1, d), F32),
                   jax.ShapeDtypeStruct((1, d), F32)),
        grid=(t // tm,),
        in_specs=[row, row, row, row, pl.BlockSpec((tm, 1), lambda i: (i, 0)), par],
        out_specs=(row, row, par, par, par),
        compiler_params=_cparams(("arbitrary",)),
    )(dr2, t1, t2, xh, rstd, g)


def _conv_mid_bwd_rows(u1, proj, du2, ng, nb):
    t = u1.shape[0]
    e = CONV_WIDTH
    tm = min(t, 256)

    rg = 16
    nt = t // tm

    def body(u1_ref, z_ref, d_ref, ng_ref, nb_ref, du1_ref, dz_ref, dng_ref, dnb_ref, dbz_ref, acc_ref):
        i = pl.program_id(0)

        @pl.when(i == 0)
        def _():
            acc_ref[...] = jnp.zeros_like(acc_ref)

        g = ng_ref[...]
        b = nb_ref[...]

        def group(r, carry):
            rows = pl.ds(pl.multiple_of(r * rg, rg), rg)
            u1 = u1_ref[rows, :]
            mu = jnp.mean(u1, axis=-1, keepdims=True)
            xc = u1 - mu
            rstd = lax.rsqrt(jnp.mean(xc * xc, axis=-1, keepdims=True) + LN_EPS)
            xh = xc * rstd
            n = xh * g + b
            sn = _sigmoid(n)
            z = z_ref[rows, :]
            sz = _sigmoid(z)
            du2 = d_ref[rows, :]
            dz = du2 * (n * sn) * (sz * (1.0 + z * (1.0 - sz)))
            dn = du2 * (z * sz) * (sn * (1.0 + n * (1.0 - sn)))
            acc_ref[0:8, :] += _fold_rows(dn * xh)
            acc_ref[8:16, :] += _fold_rows(dn)
            acc_ref[16:24, :] += _fold_rows(dz)
            dz_ref[rows, :] = dz.astype(BF16)
            dxh = dn * g
            du1_ref[rows, :] = rstd * (dxh - jnp.mean(dxh, axis=-1, keepdims=True)
                                       - xh * jnp.mean(dxh * xh, axis=-1, keepdims=True))
            return carry

        lax.fori_loop(0, tm // rg, group, 0, unroll=8)

        @pl.when(i == nt - 1)
        def _():
            dng_ref[...] = jnp.sum(acc_ref[0:8, :], axis=0, keepdims=True)
            dnb_ref[...] = jnp.sum(acc_ref[8:16, :], axis=0, keepdims=True)
            dbz_ref[...] = jnp.sum(acc_ref[16:24, :], axis=0, keepdims=True)

    row = pl.BlockSpec((tm, e), lambda i: (i, 0))
    par = pl.BlockSpec((1, e), lambda i: (0, 0))
    return pl.pallas_call(
        body, name="conv_mid_bwd_rows",
        out_shape=(jax.ShapeDtypeStruct((t, e), F32), jax.ShapeDtypeStruct((t, 3 * e), BF16),
                   jax.ShapeDtypeStruct((1, e), F32), jax.ShapeDtypeStruct((1, e), F32),
                   jax.ShapeDtypeStruct((1, e), F32)),
        grid=(nt,),
        in_specs=[row, pl.BlockSpec((tm, e), lambda i: (i, 2)), row, par, par],
        out_specs=(row, pl.BlockSpec((tm, e), lambda i: (i, 2)), par, par, par),
        scratch_shapes=[pltpu.VMEM((24, e), F32)],
        compiler_params=_cparams(("arbitrary",)),
    )(u1, proj, du2, ng, nb)


def _conv_bwd(du1, proj, cw, dproj):
    t = du1.shape[0]
    e = CONV_WIDTH
    tm = min(t, 128)
    hb = tm // HALO
    nt = t // tm
    nchunk = e // LANES
    last_halo = t // HALO - 1

    def body(d_ref, dn_ref, val_ref, gg_ref, valh_ref, ggh_ref, cw_ref, alias_ref,
             dp_ref, dcw_ref, dcb_ref, dbv_ref, extu_ref, extd_ref, du0_ref, acc_ref, sh_ref):
        i = pl.program_id(0)
        h0 = valh_ref[...] * _sigmoid(ggh_ref[...])
        extu_ref[0:HALO, :] = jnp.where(i > 0, h0, 0.0)
        extu_ref[HALO:, :] = val_ref[...] * _sigmoid(gg_ref[...])
        extd_ref[0:tm, :] = d_ref[...]
        extd_ref[tm:, :] = jnp.where(i < nt - 1, dn_ref[...], 0.0)

        @pl.when(i == 0)
        def _():
            acc_ref[...] = jnp.zeros_like(acc_ref)

        def chunk(c, carry):
            col = pl.multiple_of(c * LANES, LANES)
            d = extd_ref[0:tm, pl.ds(col, LANES)]
            du0 = jnp.zeros((tm, LANES), F32)
            for res, taps in _taps_by_residue(lambda j: CONV_KERNEL - 1 - j):
                span = 8 * max(a for _, a in taps) + tm
                sh_ref[0:span, :] = extd_ref[res:res + span, pl.ds(col, LANES)]
                for j, a in taps:
                    du0 = du0 + cw_ref[j:j + 1, pl.ds(col, LANES)] * sh_ref[8 * a:8 * a + tm, :]
            for res, taps in _taps_by_residue(lambda j: HALO - (CONV_KERNEL - 1) + j):
                span = 8 * max(a for _, a in taps) + tm
                sh_ref[0:span, :] = extu_ref[res:res + span, pl.ds(col, LANES)]
                for j, a in taps:
                    prod = d * sh_ref[8 * a:8 * a + tm, :]
                    acc_ref[j * 8:(j + 1) * 8, pl.ds(col, LANES)] += _fold_rows(prod)
            acc_ref[CONV_KERNEL * 8:(CONV_KERNEL + 1) * 8, pl.ds(col, LANES)] += _fold_rows(d)
            du0_ref[:, pl.ds(col, LANES)] = du0
            return carry

        lax.fori_loop(0, nchunk, chunk, 0)
        kb = CONV_KERNEL * 8
        for r in range(0, tm, 16):
            sg = _sigmoid(gg_ref[r:r + 16, :])
            dval = du0_ref[r:r + 16, :] * sg
            dgg = dval * val_ref[r:r + 16, :] * (1.0 - sg)
            dp_ref[r:r + 16, 0:e] = dval.astype(BF16)
            dp_ref[r:r + 16, e:] = dgg.astype(BF16)
            acc_ref[kb + 8:kb + 16, :] += _fold_rows(dval)
            acc_ref[kb + 16:kb + 24, :] += _fold_rows(dgg)

        @pl.when(i == nt - 1)
        def _():
            for j in range(CONV_KERNEL):
                dcw_ref[j:j + 1, :] = jnp.sum(acc_ref[j * 8:(j + 1) * 8, :], axis=0, keepdims=True)
            dcw_ref[CONV_KERNEL:, :] = jnp.zeros((HALO - CONV_KERNEL, e), F32)
            dcb_ref[...] = jnp.sum(acc_ref[kb:kb + 8, :], axis=0, keepdims=True)
            dbv_ref[:, 0:e] = jnp.sum(acc_ref[kb + 8:kb + 16, :], axis=0, keepdims=True)
            dbv_ref[:, e:] = jnp.sum(acc_ref[kb + 16:kb + 24, :], axis=0, keepdims=True)

    row = lambda c: pl.BlockSpec((tm, e), lambda i: (i, c))
    halo_prev = lambda c: pl.BlockSpec((HALO, e), lambda i: (jnp.maximum(i * hb - 1, 0), c))
    halo_next = pl.BlockSpec((HALO, e), lambda i: (jnp.minimum((i + 1) * hb, last_halo), 0))
    return pl.pallas_call(
        body, name="conv_bwd",
        out_shape=(jax.ShapeDtypeStruct(dproj.shape, BF16), jax.ShapeDtypeStruct((HALO, e), F32),
                   jax.ShapeDtypeStruct((1, e), F32), jax.ShapeDtypeStruct((1, 2 * e), F32)),
        grid=(nt,),
        in_specs=[row(0), halo_next, row(0), row(1), halo_prev(0), halo_prev(1),
                  pl.BlockSpec((HALO, e), lambda i: (0, 0)), pl.BlockSpec(memory_space=pl.ANY)],
        out_specs=(pl.BlockSpec((tm, 2 * e), lambda i: (i, 0)), pl.BlockSpec((HALO, e), lambda i: (0, 0)),
                   pl.BlockSpec((1, e), lambda i: (0, 0)), pl.BlockSpec((1, 2 * e), lambda i: (0, 0))),
        scratch_shapes=[pltpu.VMEM((tm + HALO, e), F32), pltpu.VMEM((tm + HALO, e), F32),
                        pltpu.VMEM((tm, e), F32), pltpu.VMEM(((CONV_KERNEL + 3) * 8, e), F32),
                        pltpu.VMEM((tm + HALO, LANES), F32)],
        input_output_aliases={7: 0},
        compiler_params=_cparams(("arbitrary",)),
    )(du1, du1, proj, proj, proj, proj, cw, dproj)


def _adam_update(g, w, m, v):
    c1 = 1.0 - ADAM_B1 ** ADAM_STEP
    c2 = 1.0 - ADAM_B2 ** ADAM_STEP
    mn = ADAM_B1 * m + (1.0 - ADAM_B1) * g
    vn = ADAM_B2 * v + (1.0 - ADAM_B2) * (g * g)
    delta = -ADAM_LR * ((mn / c1) / (jnp.sqrt(vn / c2) + ADAM_EPS) + ADAM_WD * w)
    return delta, mn, vn


def _adamw(name, gbuf, w, m, v):
    r, c = w.shape
    tr = min(r, 256)
    assert r % tr == 0

    def body(g_ref, w_ref, m_ref, v_ref, go_ref, d_ref, mo_ref, vo_ref):
        g = g_ref[0].astype(F32)
        for k in range(1, N_DEV):
            g = g + g_ref[k].astype(F32)
        go_ref[...] = g
        d_ref[...], mo_ref[...], vo_ref[...] = _adam_update(g, w_ref[...], m_ref[...], v_ref[...])

    blk = pl.BlockSpec((tr, c), lambda i: (i, 0))
    shp = jax.ShapeDtypeStruct((r, c), F32)
    return pl.pallas_call(
        body, name="adamw_" + name,
        out_shape=(shp, shp, shp, shp),
        grid=(r // tr,),
        in_specs=[pl.BlockSpec((N_DEV, tr, c), lambda i: (0, i, 0)), blk, blk, blk],
        out_specs=(blk, blk, blk, blk),
        compiler_params=_cparams(("parallel",)),
    )(gbuf, w, m, v)


SMALL_TABLE_COLS = 256
SMALL_LAYOUT = {
    'ln_g': (0, 2, 1024), 'ln_b': (8, 2, 1024), 'a_b_in': (16, 1, 768), 'a_conv_b': (24, 1, 256),
    'a_norm_g': (32, 1, 256), 'a_norm_b': (40, 1, 256), 'a_b_out': (48, 1, 128), 'kv_norm_g': (56, 1, 256),
    'b_q_norm_g': (64, 1, 512), 'a_conv_w': (72, CONV_KERNEL, 256),
}
SMALL_TABLE_ROWS = 104
SMALL_NAMES = tuple(SMALL_LAYOUT)


def _small_pieces(name, row):
    r0, _, c = SMALL_LAYOUT[name]
    w = min(c, SMALL_TABLE_COLS)
    per_row = c // w
    return [(r0 + row * per_row + q, q * w, w) for q in range(per_row)]


def _pack_small_grads(dg0, dg1, db0, db1, dbv, dbz, dcb, dng, dnb, dba, dgkv, dgq, dcw):
    e = CONV_WIDTH
    ce = e // N_DEV

    def body(dg0_ref, dg1_ref, db0_ref, db1_ref, dbv_ref, dbz_ref, dcb_ref, dng_ref, dnb_ref, dba_ref, dgkv_ref,
             dgq_ref, dcw_ref, o_ref):
        o_ref[...] = jnp.zeros_like(o_ref)

        def put(d, name, row, src_ref, first_col=0):
            for trow, col, w in _small_pieces(name, row):
                o_ref[d, trow:trow + 1, 0:w] = src_ref[:, first_col + col:first_col + col + w]

        for d in range(N_DEV):
            put(d, 'ln_g', 0, dg0_ref)
            put(d, 'ln_g', 1, dg1_ref)
            put(d, 'ln_b', 0, db0_ref)
            put(d, 'ln_b', 1, db1_ref)
            for trow, col, w in _small_pieces('a_b_in', 0):
                gcol = d * 3 * ce + col
                src = dbv_ref[:, gcol:gcol + w] if gcol < 2 * e else dbz_ref[:, gcol - 2 * e:gcol - 2 * e + w]
                o_ref[d, trow:trow + 1, 0:w] = src
            put(d, 'a_conv_b', 0, dcb_ref, d * ce)
            put(d, 'a_norm_g', 0, dng_ref, d * ce)
            put(d, 'a_norm_b', 0, dnb_ref, d * ce)
            put(d, 'a_b_out', 0, dba_ref, d * LANES)
            put(d, 'kv_norm_g', 0, dgkv_ref)
            put(d, 'b_q_norm_g', 0, dgq_ref)
            r0 = SMALL_LAYOUT['a_conv_w'][0]
            o_ref[d, r0:r0 + HALO, 0:ce] = dcw_ref[:, d * ce:(d + 1) * ce]

    return pl.pallas_call(
        body, name="pack_small_grads",
        out_shape=jax.ShapeDtypeStruct((N_DEV, SMALL_TABLE_ROWS, SMALL_TABLE_COLS), F32),
        compiler_params=pltpu.CompilerParams(vmem_limit_bytes=VMEM_LIMIT),
    )(dg0, dg1, db0, db1, dbv, dbz, dcb, dng, dnb, dba, dgkv, dgq, dcw)


def _adamw_small(gtab, ws, ms, vs):
    n = len(SMALL_NAMES)

    def body(*refs):
        g_ref = refs[0]
        w_refs, m_refs, v_refs = refs[1:1 + n], refs[1 + n:1 + 2 * n], refs[1 + 2 * n:1 + 3 * n]
        outs = refs[1 + 3 * n:]

        def summed(r0, r, c):
            g = g_ref[0, r0:r0 + r, 0:c]
            for k in range(1, N_DEV):
                g = g + g_ref[k, r0:r0 + r, 0:c]
            return g

        for i, name in enumerate(SMALL_NAMES):
            r0, r, c = SMALL_LAYOUT[name]
            if c <= SMALL_TABLE_COLS:
                blocks = [(slice(None), summed(r0, r, c))]
            else:
                blocks = [(slice(row, row + 1), jnp.concatenate([summed(tr, 1, w) for tr, _, w in _small_pieces(name, row)], axis=1))
                          for row in range(r)]
            for rows, g in blocks:
                delta, mn, vn = _adam_update(g, w_refs[i][rows, :], m_refs[i][rows, :], v_refs[i][rows, :])
                outs[i][rows, :] = g
                outs[n + i][rows, :] = delta
                outs[2 * n + i][rows, :] = mn
                outs[3 * n + i][rows, :] = vn

    shapes = tuple(jax.ShapeDtypeStruct(w.shape, F32) for w in ws)
    res = pl.pallas_call(
        body, name="adamw_small",
        out_shape=shapes * 4,
        compiler_params=pltpu.CompilerParams(vmem_limit_bytes=VMEM_LIMIT),
    )(gtab, *ws, *ms, *vs)
    return res[:n], res[n:2 * n], res[2 * n:3 * n], res[3 * n:]


def _mesh_peers():
    x, y, c = lax.axis_index("x"), lax.axis_index("y"), lax.axis_index("c")
    me = 4 * x + 2 * y + c
    peers = []
    for k in range(1, N_DEV):
        px = 1 - x if (k >> 2) & 1 else x
        py = 1 - y if (k >> 1) & 1 else y
        pc = 1 - c if k & 1 else c
        peers.append(((px, py, pc), 4 * px + 2 * py + pc))
    return me, peers


def _all_gather(name, shards):
    na = len(shards)

    def body(*refs):
        srcs, dsts = refs[:na], refs[na:2 * na]
        send_sems, recv_sems, local_sems = refs[2 * na:]
        x, y, c = lax.axis_index("x"), lax.axis_index("y"), lax.axis_index("c")
        me, sibling = (x, y, c), (x, y, 1 - c)
        chips = [(1 - x, y), (x, 1 - y), (1 - x, 1 - y)]
        slot = lambda dev: 4 * dev[0] + 2 * dev[1] + dev[2]

        def copy(a, k, block, to, own=False):
            return pltpu.make_async_remote_copy(
                src_ref=srcs[a] if own else dsts[a].at[slot(block)], dst_ref=dsts[a].at[slot(block)],
                send_sem=send_sems.at[a, k], recv_sem=recv_sems.at[a, k], device_id=to,
                device_id_type=pl.DeviceIdType.MESH)

        local = [pltpu.make_async_copy(srcs[a], dsts[a].at[slot(me)], local_sems.at[a]) for a in range(na)]
        for cp in local:
            cp.start()
        sends = []
        for a in range(na):
            sends.append(copy(a, 0, me, sibling, own=True))
            sends += [copy(a, 1 + j, me, (*chip, c), own=True) for j, chip in enumerate(chips)]
        for cp in sends:
            cp.start()
        for j, chip in enumerate(chips):
            for a in range(na):
                copy(a, 1 + j, (*chip, c), me).wait_recv()
                fwd = copy(a, 4 + j, (*chip, c), sibling)
                fwd.start()
                sends.append(fwd)
        for a in range(na):
            copy(a, 0, sibling, me).wait_recv()
            for j, chip in enumerate(chips):
                copy(a, 4 + j, (*chip, 1 - c), me).wait_recv()
        for cp in sends:
            cp.wait_send()
        for cp in local:
            cp.wait()

    hbm = pl.BlockSpec(memory_space=pl.ANY)
    return pl.pallas_call(
        body, name=name,
        out_shape=tuple(jax.ShapeDtypeStruct((N_DEV,) + s.shape, s.dtype) for s in shards),
        in_specs=[hbm] * na, out_specs=(hbm,) * na,
        scratch_shapes=[pltpu.SemaphoreType.DMA((na, N_DEV - 1)), pltpu.SemaphoreType.DMA((na, N_DEV - 1)),
                        pltpu.SemaphoreType.DMA((na,))],
    )(*shards)


_HBM_SPEC = pl.BlockSpec(memory_space=pltpu.HBM)
_SEM_SPEC = pl.BlockSpec(memory_space=pltpu.SEMAPHORE)


def _split_copies(srcs, lands, send_sems, recv_sems, local_sems, gather):
    me, peers = _mesh_peers()
    na = len(srcs)
    mine = lambda a, slot: srcs[a] if gather else srcs[a].at[slot]
    local = [pltpu.make_async_copy(mine(a, me), lands[a].at[me], local_sems.at[a]) for a in range(na)]
    sent, received = [], []
    for k, (dev, pid) in enumerate(peers):
        for a in range(na):
            s = a * (N_DEV - 1) + k
            sent.append(pltpu.make_async_remote_copy(
                src_ref=mine(a, pid), dst_ref=lands[a].at[me], send_sem=send_sems.at[s],
                recv_sem=recv_sems.at[s], device_id=dev, device_id_type=pl.DeviceIdType.MESH))
            received.append(pltpu.make_async_remote_copy(
                src_ref=mine(a, pid), dst_ref=lands[a].at[pid], send_sem=send_sems.at[s],
                recv_sem=recv_sems.at[s], device_id=dev, device_id_type=pl.DeviceIdType.MESH))
    return local, sent, received


def _comm_start(name, srcs, gather, after):
    na = len(srcs)
    land_structs = [jax.ShapeDtypeStruct(((N_DEV,) + s.shape) if gather else s.shape, s.dtype) for s in srcs]
    extra = [] if after is None else [after]
    n_in = 2 * na + len(extra)

    def body(*refs):
        srcs_r, lands_r = refs[:na], refs[na:2 * na]
        send_sems, recv_sems, local_sems = refs[n_in:n_in + 3]
        token = refs[-1]
        local, sent, _ = _split_copies(srcs_r, lands_r, send_sems, recv_sems, local_sems, gather)
        for cp in local + sent:
            cp.start()
        token[...] = jnp.zeros_like(token)

    sem = pltpu.SemaphoreType.DMA((na * (N_DEV - 1),))
    outs = pl.pallas_call(
        body, name=name,
        out_shape=(sem, sem, pltpu.SemaphoreType.DMA((na,)),
                   *[pltpu.HBM(s.shape, s.dtype) for s in srcs],
                   *[pltpu.HBM(s.shape, s.dtype) for s in land_structs],
                   jax.ShapeDtypeStruct((8, LANES), F32)),
        in_specs=[_HBM_SPEC] * (2 * na) + [pl.BlockSpec(memory_space=pl.ANY)] * len(extra),
        out_specs=(_SEM_SPEC, _SEM_SPEC, _SEM_SPEC, *([_HBM_SPEC] * (2 * na)), pl.BlockSpec(memory_space=pltpu.VMEM)),
        input_output_aliases={i: 3 + i for i in range(2 * na)},
        compiler_params=pltpu.CompilerParams(has_side_effects=pltpu.SideEffectType.DATAFLOW_SIDE_EFFECTING),
    )(*[pltpu.with_memory_space_constraint(s, pltpu.HBM) for s in srcs],
      *[pltpu.with_memory_space_constraint(lax.empty(s.shape, s.dtype), pltpu.HBM) for s in land_structs],
      *extra)
    return (outs[:3], outs[3:3 + na], outs[3 + na:3 + 2 * na]), outs[-1]


def _comm_wait(name, handles, gather, after):
    (send_sems, recv_sems, local_sems), srcs, lands = handles
    na = len(srcs)

    def body(*refs):
        srcs_r, lands_r = refs[:na], refs[na:2 * na]
        send_r, recv_r, local_r = refs[2 * na:2 * na + 3]
        local, sent, received = _split_copies(srcs_r, lands_r, send_r, recv_r, local_r, gather)
        for cp in sent:
            cp.wait_send()
        for cp in received:
            cp.wait_recv()
        for cp in local:
            cp.wait()

    outs = pl.pallas_call(
        body, name=name,
        out_shape=(*[pltpu.HBM(s.shape, s.dtype) for s in srcs], *[pltpu.HBM(s.shape, s.dtype) for s in lands]),
        in_specs=[_HBM_SPEC] * (2 * na) + [_SEM_SPEC] * 3 + [pl.BlockSpec(memory_space=pl.ANY)],
        out_specs=tuple([_HBM_SPEC] * (2 * na)),
        input_output_aliases={i: i for i in range(2 * na)},
        compiler_params=pltpu.CompilerParams(has_side_effects=pltpu.SideEffectType.DATAFLOW_SIDE_EFFECTING),
    )(*srcs, *lands, send_sems, recv_sems, local_sems, after)
    return outs[na:]


def _prep_weights(w):
    e = CONV_WIDTH
    p = {}
    if 'a_w_in' in w:
        if w['a_w_in'].shape == (N_DEV, D_MODEL, 3 * e // N_DEV):
            p['a_w_in3'] = w['a_w_in']
        else:
            p['a_w_in3'] = w['a_w_in'].reshape(D_MODEL, N_DEV, 3 * e // N_DEV).transpose(1, 0, 2)
        p['a_b_in'] = w['a_b_in'].reshape(1, 3 * e)
        p['a_conv_w'] = jnp.pad(w['a_conv_w'].reshape(CONV_KERNEL, e), ((0, HALO - CONV_KERNEL), (0, 0)))
        p['a_conv_b'] = w['a_conv_b'].reshape(1, e)
        p['a_norm_g'] = w['a_norm_g'].reshape(1, e)
        p['a_norm_b'] = w['a_norm_b'].reshape(1, e)
        p['a_b_out'] = w['a_b_out'].reshape(1, D_MODEL)
        p['kv_norm_g'] = w['kv_norm_g'].reshape(1, KV_LORA_RANK)
        p['b_q_norm_g'] = w['b_q_norm_g'].reshape(1, Q_LORA_RANK)
        p['ln_g'] = w['ln_g']
        p['ln_b'] = w['ln_b']
    if 'a_w_out' in w:
        p['a_w_out'] = w['a_w_out'].reshape(e, D_MODEL)
        p['kv_w_down'] = jnp.pad(w['kv_w_down'], ((0, 0), (0, KV_LORA_RANK + LANES - w['kv_w_down'].shape[1])))
        p['kv_w_ukv'] = jnp.concatenate([w['kv_w_uk'], w['kv_w_uv']], axis=2).reshape(KV_LORA_RANK, N_HEADS * HEAD_PAD)
        p['b_w_in'] = w['b_w_in'].reshape(D_MODEL, -1)
        uq = w['b_w_uq'].reshape(Q_LORA_RANK, N_HEADS, QK_NOPE_DIM + QK_ROPE_DIM)
        p['b_w_uq'] = jnp.pad(uq, ((0, 0), (0, 0), (0, HEAD_PAD - uq.shape[2]))).reshape(Q_LORA_RANK, N_HEADS * HEAD_PAD)
        p['b_w_out'] = w['b_w_out'].reshape(N_HEADS * V_HEAD_DIM, D_MODEL)
    return p


class _NoComm:
    def __init__(self):
        self.grads = {}
        self.small = None

    def first_after(self):
        return None

    def rest_weights(self, after):
        return {}

    def send_group1(self, g):
        self.grads.update(g)
        return None

    def send_group1b(self, g):
        self.grads.update(g)
        return None

    def send_group2(self, g, small):
        self.grads.update(g)
        self.small = small
        return None


def _local_step(x, positions, target, p, comm):
    t = x.shape[0]
    e = CONV_WIDTH
    freqs = ROPE_THETA ** (-jnp.arange(0, QK_ROPE_DIM, 2, dtype=F32) / QK_ROPE_DIM)
    freq_row = jnp.concatenate([freqs, freqs, jnp.zeros((LANES - QK_ROPE_DIM,), F32)]).reshape(1, LANES)
    rc, rs = _rope_tables(positions.reshape(t, 1), freq_row)
    xb = x.astype(BF16)
    ln_g0, ln_b0 = p['ln_g'][0:1], p['ln_b'][0:1]
    ln_g1, ln_b1 = p['ln_g'][1:2], p['ln_b'][1:2]

    proj = _matmul(xb, p['a_w_in3'], bias=p['a_b_in'], name="a_in", b_blocked=True, after=comm.first_after())
    u1, u2 = _conv_mid_fwd(proj, p['a_conv_w'], p['a_conv_b'], p['a_norm_g'], p['a_norm_b'])
    p = {**p, **comm.rest_weights(u2)}
    y = _matmul(u2, p['a_w_out'], bias=p['a_b_out'], name="a_out", bk=2048)
    h1, h1b, xh1, rstd1 = _ln_res_fwd(x, y, ln_g0, ln_b0)
    ckv = _matmul(h1b, p['kv_w_down'], name="kv_down")
    ckn, kr = _kv_mid_fwd(ckv, p['kv_norm_g'], rc, rs)
    kv = _matmul(ckn, p['kv_w_ukv'], name="kv_up", out_dtype=BF16)
    projb = _matmul(h1b, p['b_w_in'], name="b_in", bn=1280)
    cqn = _q_norm_fwd(projb, p['b_q_norm_g'])
    qcat = _q_up_rope(cqn, p['b_w_uq'], rc, rs)
    o, o2, lse = _flash_fwd(qcat, kv, kr, projb)
    yb = _matmul(o2, p['b_w_out'], name="b_out", bk=2048)
    loss, dr2, dr2b, dg1, db1 = _final_ln_loss(h1, yb, ln_g1, ln_b1, target)

    g = {}
    g['b_w_out'] = _matmul(o2, dr2b, trans_a=True, name="d_b_out", out_dtype=BF16, bm=512, bk=t)
    do2 = _matmul(dr2b, p['b_w_out'], trans_b=True, name="d_o2")
    dob, dprojb, dl = _gate_bwd(do2, o, projb)
    dq2, dkv, dkr = _flash_bwd(qcat, kv, kr, dob, lse, dl, rc, rs)
    duq = _matmul(cqn, dq2, trans_a=True, name="d_q_up", out_dtype=BF16, bk=t)
    dcqn = _matmul(dq2, p['b_w_uq'], trans_b=True, name="d_cqn", bk=N_HEADS * HEAD_PAD)
    dprojb, dgq = _q_norm_bwd(dcqn, projb, p['b_q_norm_g'], dprojb)
    g['b_w_in'] = _matmul(h1b, dprojb, trans_a=True, name="d_b_in", bn=640, bk=t, out_dtype=BF16)
    t1 = _matmul(dprojb, p['b_w_in'], trans_b=True, name="d_h1_b", bk=dprojb.shape[1])
    dukv = _matmul(ckn, dkv, trans_a=True, name="d_kv_up", out_dtype=BF16, bk=t)
    dckn = _matmul(dkv, p['kv_w_ukv'], trans_b=True, name="d_ckn", bk=N_HEADS * HEAD_PAD)
    dckv, dgkv = _kv_mid_bwd(dckn, ckv, dkr, p['kv_norm_g'], rc, rs)
    ddown = _matmul(h1b, dckv, trans_a=True, name="d_kv_down", out_dtype=BF16, bm=512, bk=t)
    g['b_w_out'] = g['b_w_out'].reshape(N_DEV, -1, D_MODEL)
    g['b_w_in'] = g['b_w_in'].reshape(D_MODEL, N_DEV, -1).transpose(1, 0, 2)
    g['kv_w_down'] = ddown[:, :KV_LORA_RANK + QK_ROPE_DIM].reshape(N_DEV, -1, KV_LORA_RANK + QK_ROPE_DIM)
    dukv = dukv.reshape(KV_LORA_RANK, N_HEADS, HEAD_PAD)
    g['kv_w_uk'] = dukv[:, :, :QK_NOPE_DIM].reshape(N_DEV, -1, QK_NOPE_DIM)
    g['kv_w_uv'] = dukv[:, :, QK_NOPE_DIM:].reshape(N_DEV, -1, V_HEAD_DIM)
    g['b_w_uq'] = duq.reshape(Q_LORA_RANK, N_HEADS, HEAD_PAD)[:, :, :QK_NOPE_DIM + QK_ROPE_DIM].reshape(
        N_DEV, -1, QK_NOPE_DIM + QK_ROPE_DIM)
    sent1 = comm.send_group1(g)
    t2 = _matmul(dckv, p['kv_w_down'], trans_b=True, name="d_h1_kv", after=sent1)
    dr1, dr1b, dg0, db0, dba_out = _ln0_bwd(dr2, t1, t2, xh1, rstd1, ln_g0)
    dw_out = _matmul(u2, dr1b, trans_a=True, name="d_a_out", out_dtype=BF16, bm=512, bk=t).reshape(N_DEV, -1, D_MODEL)
    sent1b = comm.send_group1b({'a_w_out': dw_out})
    du2 = _matmul(dr1b, p['a_w_out'], trans_b=True, name="d_u2", after=sent1b)
    du1, dproj, dng, dnb, dbz = _conv_mid_bwd_rows(u1, proj, du2, p['a_norm_g'], p['a_norm_b'])
    dproj, dcw, dcb, dbv = _conv_bwd(du1, proj, p['a_conv_w'], dproj)
    dw_in = _matmul(xb, dproj, trans_a=True, name="d_a_in", out_dtype=BF16, bn=3 * e // N_DEV, bk=t, out_blocked=True)
    small = (dg0, dg1, db0, db1, dbv, dbz, dcb, dng, dnb, dba_out, dgkv, dgq, dcw)
    sent2 = comm.send_group2({'a_w_in': dw_in}, small)
    grad_x = _grad_x(dproj, p['a_w_in3'], dr1, sent2)
    return loss, grad_x


def _grad_x(dproj, w3, dr1, after):
    t, d = dr1.shape
    nblk, _, cb = w3.shape
    tm = min(t, 512)
    extra = [] if after is None else [after]

    def body(a_ref, w_ref, dr_ref, *rest):
        o_ref = rest[-1]
        acc = ALPHA * dr_ref[...]
        for k in range(nblk):
            acc = acc + lax.dot_general(a_ref[:, k * cb:(k + 1) * cb], w_ref[k], NT, preferred_element_type=F32)
        o_ref[...] = acc

    row = pl.BlockSpec((tm, d), lambda i: (i, 0))
    return pl.pallas_call(
        body, name="grad_x",
        out_shape=jax.ShapeDtypeStruct((t, d), F32),
        grid=(t // tm,),
        in_specs=[pl.BlockSpec((tm, nblk * cb), lambda i: (i, 0)),
                  pl.BlockSpec((nblk, d, cb), lambda i: (0, 0, 0), pipeline_mode=pl.Buffered(1)), row]
                 + [pl.BlockSpec(memory_space=pl.ANY)] * len(extra),
        out_specs=row,
        compiler_params=_cparams(("parallel",)),
    )(dproj, w3, dr1, *extra)


def _shard_2d(a):
    return a.reshape(-1, a.shape[-1])


def _gathered_to_full(name, blocks):
    if name == 'a_w_in':
        return blocks
    if name == 'b_w_in':
        return blocks.transpose(1, 0, 2).reshape(D_MODEL, -1)
    if name == 'a_conv_w':
        return blocks.transpose(1, 0, 2).reshape(CONV_KERNEL, -1)
    if name in ('a_b_in', 'a_conv_b', 'a_norm_g', 'a_norm_b', 'a_b_out'):
        return blocks.reshape(-1)
    if name in ('kv_w_uk', 'kv_w_uv'):
        return blocks.reshape(KV_LORA_RANK, N_HEADS, -1)
    if name == 'b_w_uq':
        return blocks.reshape(Q_LORA_RANK, N_HEADS, -1)
    return blocks.reshape(-1, blocks.shape[-1])


def kernel(x, positions, ln_g, ln_b, a_w_in, a_b_in, a_conv_w, a_conv_b, a_norm_g, a_norm_b, a_w_out, a_b_out, kv_w_down, kv_norm_g, kv_w_uk, kv_w_uv, b_w_in, b_q_norm_g, b_w_uq, b_w_out, loss_target, m_ln_g, m_ln_b, m_a_w_in, m_a_b_in, m_a_conv_w, m_a_conv_b, m_a_norm_g, m_a_norm_b, m_a_w_out, m_a_b_out, m_kv_w_down, m_kv_norm_g, m_kv_w_uk, m_kv_w_uv, m_b_w_in, m_b_q_norm_g, m_b_w_uq, m_b_w_out, v_ln_g, v_ln_b, v_a_w_in, v_a_b_in, v_a_conv_w, v_a_conv_b, v_a_norm_g, v_a_norm_b, v_a_w_out, v_a_b_out, v_kv_w_down, v_kv_norm_g, v_kv_w_uk, v_kv_w_uv, v_b_w_in, v_b_q_norm_g, v_b_w_uq, v_b_w_out):
    w = dict(ln_g=ln_g, ln_b=ln_b, a_w_in=a_w_in, a_b_in=a_b_in, a_conv_w=a_conv_w, a_conv_b=a_conv_b,
             a_norm_g=a_norm_g, a_norm_b=a_norm_b, a_w_out=a_w_out, a_b_out=a_b_out, kv_w_down=kv_w_down,
             kv_norm_g=kv_norm_g, kv_w_uk=kv_w_uk, kv_w_uv=kv_w_uv, b_w_in=b_w_in, b_q_norm_g=b_q_norm_g,
             b_w_uq=b_w_uq, b_w_out=b_w_out)
    m = dict(ln_g=m_ln_g, ln_b=m_ln_b, a_w_in=m_a_w_in, a_b_in=m_a_b_in, a_conv_w=m_a_conv_w, a_conv_b=m_a_conv_b,
             a_norm_g=m_a_norm_g, a_norm_b=m_a_norm_b, a_w_out=m_a_w_out, a_b_out=m_a_b_out, kv_w_down=m_kv_w_down,
             kv_norm_g=m_kv_norm_g, kv_w_uk=m_kv_w_uk, kv_w_uv=m_kv_w_uv, b_w_in=m_b_w_in, b_q_norm_g=m_b_q_norm_g,
             b_w_uq=m_b_w_uq, b_w_out=m_b_w_out)
    v = dict(ln_g=v_ln_g, ln_b=v_ln_b, a_w_in=v_a_w_in, a_b_in=v_a_b_in, a_conv_w=v_a_conv_w, a_conv_b=v_a_conv_b,
             a_norm_g=v_a_norm_g, a_norm_b=v_a_norm_b, a_w_out=v_a_w_out, a_b_out=v_a_b_out, kv_w_down=v_kv_w_down,
             kv_norm_g=v_kv_norm_g, kv_w_uk=v_kv_w_uk, kv_w_uv=v_kv_w_uv, b_w_in=v_b_w_in, b_q_norm_g=v_b_q_norm_g,
             b_w_uq=v_b_w_uq, b_w_out=v_b_w_out)

    small_flat = jnp.concatenate([w[n].reshape(-1) for n in SMALL_WEIGHTS]).reshape(1, -1)
    ga = _all_gather("all_gather_first", [_shard_2d(w['a_w_in']).astype(BF16), small_flat])
    full = {'a_w_in': ga[0]}
    gs = ga[1].reshape(N_DEV, -1)
    off = 0
    for n in SMALL_WEIGHTS:
        size = math.prod(w[n].shape)
        full[n] = _gathered_to_full(n, gs[:, off:off + size].reshape((N_DEV,) + _shard_2d(w[n]).shape))
        off += size
    for n in REPLICATED:
        full[n] = w[n]
    rest_names = [n for n in MATMUL_WEIGHTS if n != 'a_w_in']
    group1 = ('b_w_out', 'b_w_uq', 'b_w_in', 'kv_w_uk', 'kv_w_uv', 'kv_w_down')
    group1b = ('a_w_out',)
    group2 = ('a_w_in',)

    class Comm:
        def __init__(self):
            self.rest, self.rest_token = _comm_start(
                "gather_rest_start", [_shard_2d(w[n]).astype(BF16) for n in rest_names], True, ga[0])

        def first_after(self):
            return self.rest_token

        def rest_weights(self, after):
            lands = _comm_wait("gather_rest_wait", self.rest, True, after)
            return _prep_weights({n: _gathered_to_full(n, lands[i]) for i, n in enumerate(rest_names)})

        def send_group1(self, g):
            self.g1, token = _comm_start("exchange_g1_start", [g[n] for n in group1], False, None)
            return token

        def send_group1b(self, g):
            self.g1b, token = _comm_start("exchange_g1b_start", [g[n] for n in group1b], False, None)
            return token

        def send_group2(self, g, small):
            self.g2, token = _comm_start("exchange_g2_start", [g[n] for n in group2] + [_pack_small_grads(*small)],
                                         False, None)
            return token

    comm = Comm()

    loss_loc, grad_x = _local_step(x[0], positions[0], loss_target[0], _prep_weights(full), comm)
    loss = lax.psum(loss_loc[0, 0], ("x", "y", "c"))

    res = {}
    recv1 = _comm_wait("exchange_g1_wait", comm.g1, False, grad_x)
    for i, n in enumerate(group1):
        res[n] = _adamw(n, recv1[i], _shard_2d(w[n]), _shard_2d(m[n]), _shard_2d(v[n]))
    recv1b = _comm_wait("exchange_g1b_wait", comm.g1b, False, res[group1[-1]][0])
    for i, n in enumerate(group1b):
        res[n] = _adamw(n, recv1b[i], _shard_2d(w[n]), _shard_2d(m[n]), _shard_2d(v[n]))
    recv2 = _comm_wait("exchange_g2_wait", comm.g2, False, res[group1b[-1]][0])
    for i, n in enumerate(group2):
        res[n] = _adamw(n, recv2[i], _shard_2d(w[n]), _shard_2d(m[n]), _shard_2d(v[n]))
    two_d = lambda d: [_shard_2d(d[n]) if d[n].ndim > 1 else d[n].reshape(1, -1) for n in SMALL_NAMES]
    sg, sd, sm, sv = _adamw_small(recv2[-1], two_d(w), two_d(m), two_d(v))
    for i, n in enumerate(SMALL_NAMES):
        res[n] = (sg[i], sd[i], sm[i], sv[i])
    outs = [[res[n][j].reshape(w[n].shape) for n in WEIGHT_NAMES] for j in range(4)]
    return (loss, grad_x[None], *outs[0], *outs[1], *outs[2], *outs[3])
```

```python
import functools
import math

import jax
import jax.numpy as jnp
from jax import lax
from jax.experimental import pallas as pl
from jax.experimental.pallas import tpu as pltpu

F32 = jnp.float32
BF16 = jnp.bfloat16

D_MODEL = 1024
DEPTH = 2
CONV_WIDTH = 2 * D_MODEL
CONV_KERNEL = 31
N_HEADS = 16
QK_NOPE_DIM = 128
QK_ROPE_DIM = 64
V_HEAD_DIM = 128
KV_LORA_RANK = D_MODEL // 4
Q_LORA_RANK = D_MODEL // 2
ROPE_THETA = 10000.0
LN_EPS = 1e-5
RMS_EPS = 1e-6
MASK_VALUE = -1e30
ALPHA = (2.0 * DEPTH) ** 0.25
ATTN_SCALE = 1.0 / math.sqrt(QK_NOPE_DIM + QK_ROPE_DIM)

ADAM_LR = 0.001
ADAM_B1 = 0.9
ADAM_B2 = 0.999
ADAM_EPS = 1e-08
ADAM_WD = 0.01
ADAM_STEP = 10

N_DEV = 8
LANES = 128
HEAD_PAD = 256
HALO = 32
VMEM_LIMIT = 56 * 1024 * 1024

WEIGHT_NAMES = ['ln_g', 'ln_b', 'a_w_in', 'a_b_in', 'a_conv_w', 'a_conv_b', 'a_norm_g', 'a_norm_b',
                'a_w_out', 'a_b_out', 'kv_w_down', 'kv_norm_g', 'kv_w_uk', 'kv_w_uv', 'b_w_in',
                'b_q_norm_g', 'b_w_uq', 'b_w_out']
REPLICATED = ('ln_g', 'ln_b', 'kv_norm_g', 'b_q_norm_g')
MATMUL_WEIGHTS = ('a_w_in', 'a_w_out', 'kv_w_down', 'kv_w_uk', 'kv_w_uv', 'b_w_in', 'b_w_uq', 'b_w_out')
SMALL_WEIGHTS = ('a_b_in', 'a_conv_w', 'a_conv_b', 'a_norm_g', 'a_norm_b', 'a_b_out')

NN = (((1,), (0,)), ((), ()))
NT = (((1,), (1,)), ((), ()))
TN = (((0,), (0,)), ((), ()))


def _cparams(sem):
    return pltpu.CompilerParams(dimension_semantics=sem, vmem_limit_bytes=VMEM_LIMIT)


def _sigmoid(x):
    return 0.5 * jnp.tanh(0.5 * x) + 0.5


def _aligned(v, m):
    return v if isinstance(v, int) else pl.multiple_of(v, m)


def _fold_rows(x):
    parts = [x[r:r + 8] for r in range(0, x.shape[0], 8)]
    while len(parts) > 1:
        parts = [parts[i] + parts[i + 1] for i in range(0, len(parts) - 1, 2)] + ([parts[-1]] if len(parts) % 2 else [])
    return parts[0]


def _taps_by_residue(offset_of):
    groups = {}
    for j in range(CONV_KERNEL):
        off = offset_of(j)
        groups.setdefault(off % 8, []).append((j, off // 8))
    return sorted(groups.items())


def _swap_rope_halves(t):
    lane = lax.broadcasted_iota(jnp.int32, t.shape, 1)
    return jnp.where(lane < QK_ROPE_DIM // 2, pltpu.roll(t, LANES - QK_ROPE_DIM // 2, 1),
                     pltpu.roll(t, QK_ROPE_DIM // 2, 1))


def _matmul(a, b, *, name, bias=None, trans_a=False, trans_b=False, out_dtype=F32, bm=1024, bn=1024, bk=1024,
            b_blocked=False, out_blocked=False, after=None):
    if trans_a:
        kdim, m = a.shape
    else:
        m, kdim = a.shape
    if b_blocked and trans_b:
        n, k2 = b.shape[1], b.shape[0] * b.shape[2]
        bk = b.shape[2]
    elif b_blocked:
        k2, n = b.shape[1], b.shape[0] * b.shape[2]
        bn = b.shape[2]
    elif trans_b:
        n, k2 = b.shape
    else:
        k2, n = b.shape
    assert kdim == k2, (a.shape, b.shape)
    bm, bn, bk = min(bm, m), min(bn, n), min(bk, kdim)
    assert m % bm == 0 and n % bn == 0 and kdim % bk == 0, (name, m, n, kdim, bm, bn, bk)
    nk = kdim // bk
    dims = (((0 if trans_a else 1,), (1 if trans_b else 0,)), ((), ()))
    has_bias = bias is not None

    def body(*refs):
        refs = list(refs)
        a_ref, b_ref = refs[:2]
        del refs[:2]
        bias_ref = refs.pop(0) if has_bias else None
        if after is not None:
            refs.pop(0)
        o_ref = refs.pop(0)
        rest = refs
        prod = lax.dot_general(a_ref[...], b_ref[...], dims, preferred_element_type=F32)

        def finish(acc):
            if has_bias:
                acc = acc + bias_ref[...]
            o_ref[...] = acc.astype(out_dtype)

        if nk == 1:
            finish(prod)
        else:
            acc_ref = rest[0]
            k = pl.program_id(2)

            @pl.when(k == 0)
            def _():
                acc_ref[...] = prod

            @pl.when(k > 0)
            def _():
                acc_ref[...] += prod

            @pl.when(k == nk - 1)
            def _():
                finish(acc_ref[...])

    a_spec = pl.BlockSpec((bk, bm), lambda i, j, k: (k, i)) if trans_a else pl.BlockSpec((bm, bk), lambda i, j, k: (i, k))
    if b_blocked and trans_b:
        b_spec = pl.BlockSpec((None, bn, bk), lambda i, j, k: (k, j, 0))
    elif b_blocked:
        b_spec = pl.BlockSpec((None, bk, bn), lambda i, j, k: (j, k, 0))
    elif trans_b:
        b_spec = pl.BlockSpec((bn, bk), lambda i, j, k: (j, k))
    else:
        b_spec = pl.BlockSpec((bk, bn), lambda i, j, k: (k, j))
    in_specs = [a_spec, b_spec]
    args = [a, b]
    if has_bias:
        in_specs.append(pl.BlockSpec((1, bn), lambda i, j, k: (0, j)))
        args.append(bias)
    if after is not None:
        in_specs.append(pl.BlockSpec(memory_space=pl.ANY))
        args.append(after)
    if out_blocked:
        out_struct = jax.ShapeDtypeStruct((n // bn, m, bn), out_dtype)
        out_spec = pl.BlockSpec((None, bm, bn), lambda i, j, k: (j, i, 0))
    else:
        out_struct = jax.ShapeDtypeStruct((m, n), out_dtype)
        out_spec = pl.BlockSpec((bm, bn), lambda i, j, k: (i, j))
    return pl.pallas_call(
        body, name=name,
        out_shape=out_struct,
        grid=(m // bm, n // bn, nk),
        in_specs=in_specs,
        out_specs=out_spec,
        scratch_shapes=[pltpu.VMEM((bm, bn), F32)] if nk > 1 else [],
        compiler_params=_cparams(("parallel", "parallel", "arbitrary")),
    )(*args)


def _rope_tables(pos_col, freq_row):
    t = pos_col.shape[0]
    tm = min(t, 512)

    def body(pos_ref, f_ref, c_ref, s_ref):
        ang = pos_ref[...].astype(F32) * f_ref[...]
        lane = lax.broadcasted_iota(jnp.int32, ang.shape, 1)
        cos = jnp.cos(ang)
        sin = jnp.sin(ang)
        c_ref[...] = jnp.where(lane < QK_ROPE_DIM, cos, 0.0)
        s_ref[...] = jnp.where(lane < QK_ROPE_DIM // 2, -sin, jnp.where(lane < QK_ROPE_DIM, sin, 0.0))

    return pl.pallas_call(
        body, name="rope_tables",
        out_shape=(jax.ShapeDtypeStruct((t, LANES), F32), jax.ShapeDtypeStruct((t, LANES), F32)),
        grid=(t // tm,),
        in_specs=[pl.BlockSpec((tm, 1), lambda i: (i, 0)), pl.BlockSpec((1, LANES), lambda i: (0, 0))],
        out_specs=(pl.BlockSpec((tm, LANES), lambda i: (i, 0)), pl.BlockSpec((tm, LANES), lambda i: (i, 0))),
        compiler_params=_cparams(("parallel",)),
    )(pos_col, freq_row)


def _conv_mid_fwd(proj, cw, cb, ng, nb):
    t = proj.shape[0]
    e = CONV_WIDTH
    tm = min(t, 128)
    hb = tm // HALO
    nchunk = e // LANES

    def body(val_ref, gg_ref, z_ref, valh_ref, ggh_ref, cw_ref, cb_ref, ng_ref, nb_ref, u1_ref, u2_ref, ext_ref, sh_ref):
        i = pl.program_id(0)
        u0 = val_ref[...] * _sigmoid(gg_ref[...])
        h0 = valh_ref[...] * _sigmoid(ggh_ref[...])
        ext_ref[0:HALO, :] = jnp.where(i > 0, h0, 0.0)
        ext_ref[HALO:, :] = u0

        def chunk(c, carry):
            col = pl.multiple_of(c * LANES, LANES)
            acc = jnp.zeros((tm, LANES), F32)
            for res, taps in _taps_by_residue(lambda j: HALO - (CONV_KERNEL - 1) + j):
                span = 8 * max(a for _, a in taps) + tm
                sh_ref[0:span, :] = ext_ref[res:res + span, pl.ds(col, LANES)]
                for j, a in taps:
                    acc = acc + cw_ref[j:j + 1, pl.ds(col, LANES)] * sh_ref[8 * a:8 * a + tm, :]
            u1_ref[:, pl.ds(col, LANES)] = acc + cb_ref[:, pl.ds(col, LANES)]
            return carry

        lax.fori_loop(0, nchunk, chunk, 0)
        g = ng_ref[...]
        b = nb_ref[...]
        for r in range(0, tm, 16):
            u1 = u1_ref[r:r + 16, :]
            mu = jnp.mean(u1, axis=-1, keepdims=True)
            xc = u1 - mu
            var = jnp.mean(xc * xc, axis=-1, keepdims=True)
            n = xc * lax.rsqrt(var + LN_EPS) * g + b
            z = z_ref[r:r + 16, :]
            u2_ref[r:r + 16, :] = ((n * _sigmoid(n)) * (z * _sigmoid(z))).astype(BF16)

    row = lambda c: pl.BlockSpec((tm, e), lambda i: (i, c))
    halo = lambda c: pl.BlockSpec((HALO, e), lambda i: (jnp.maximum(i * hb - 1, 0), c))
    par = lambda r: pl.BlockSpec((r, e), lambda i: (0, 0))
    return pl.pallas_call(
        body, name="conv_mid_fwd",
        out_shape=(jax.ShapeDtypeStruct((t, e), F32), jax.ShapeDtypeStruct((t, e), BF16)),
        grid=(t // tm,),
        in_specs=[row(0), row(1), row(2), halo(0), halo(1), par(HALO), par(1), par(1), par(1)],
        out_specs=(pl.BlockSpec((tm, e), lambda i: (i, 0)), pl.BlockSpec((tm, e), lambda i: (i, 0))),
        scratch_shapes=[pltpu.VMEM((tm + HALO, e), F32), pltpu.VMEM((tm + HALO, LANES), F32)],
        compiler_params=_cparams(("parallel",)),
    )(proj, proj, proj, proj, proj, cw, cb, ng, nb)


def _ln_res_fwd(x, y, g, b):
    t, d = x.shape
    tm = min(t, 256)

    def body(x_ref, y_ref, g_ref, b_ref, h_ref, hb_ref, xh_ref, rs_ref):
        r = ALPHA * x_ref[...] + y_ref[...]
        mu = jnp.mean(r, axis=-1, keepdims=True)
        xc = r - mu
        var = jnp.mean(xc * xc, axis=-1, keepdims=True)
        rstd = lax.rsqrt(var + LN_EPS)
        xh = xc * rstd
        h = xh * g_ref[...] + b_ref[...]
        h_ref[...] = h
        hb_ref[...] = h.astype(BF16)
        xh_ref[...] = xh
        rs_ref[...] = rstd

    row = pl.BlockSpec((tm, d), lambda i: (i, 0))
    par = pl.BlockSpec((1, d), lambda i: (0, 0))
    return pl.pallas_call(
        body, name="ln0_fwd",
        out_shape=(jax.ShapeDtypeStruct((t, d), F32), jax.ShapeDtypeStruct((t, d), BF16),
                   jax.ShapeDtypeStruct((t, d), F32), jax.ShapeDtypeStruct((t, 1), F32)),
        grid=(t // tm,),
        in_specs=[row, row, par, par],
        out_specs=(row, row, row, pl.BlockSpec((tm, 1), lambda i: (i, 0))),
        compiler_params=_cparams(("parallel",)),
    )(x, y, g, b)


def _kv_mid_fwd(ckv, g, rc, rs):
    t = ckv.shape[0]
    tm = min(t, 512)
    r = KV_LORA_RANK

    def body(ckv_ref, g_ref, c_ref, s_ref, ckn_ref, kr_ref):
        c = ckv_ref[:, 0:r]
        ms = jnp.mean(c * c, axis=-1, keepdims=True)
        ckn_ref[...] = (c * lax.rsqrt(ms + RMS_EPS) * g_ref[...]).astype(BF16)
        tr = ckv_ref[:, r:r + LANES]
        kr_ref[...] = (tr * c_ref[...] + _swap_rope_halves(tr) * s_ref[...]).astype(BF16)

    return pl.pallas_call(
        body, name="kv_mid_fwd",
        out_shape=(jax.ShapeDtypeStruct((t, r), BF16), jax.ShapeDtypeStruct((t, LANES), BF16)),
        grid=(t // tm,),
        in_specs=[pl.BlockSpec((tm, r + LANES), lambda i: (i, 0)), pl.BlockSpec((1, r), lambda i: (0, 0)),
                  pl.BlockSpec((tm, LANES), lambda i: (i, 0)), pl.BlockSpec((tm, LANES), lambda i: (i, 0))],
        out_specs=(pl.BlockSpec((tm, r), lambda i: (i, 0)), pl.BlockSpec((tm, LANES), lambda i: (i, 0))),
        compiler_params=_cparams(("parallel",)),
    )(ckv, g, rc, rs)


def _q_norm_fwd(projb, g):
    t = projb.shape[0]
    tm = min(t, 512)
    r = Q_LORA_RANK

    def body(c_ref, g_ref, o_ref):
        c = c_ref[...]
        ms = jnp.mean(c * c, axis=-1, keepdims=True)
        o_ref[...] = (c * lax.rsqrt(ms + RMS_EPS) * g_ref[...]).astype(BF16)

    return pl.pallas_call(
        body, name="q_norm_fwd",
        out_shape=jax.ShapeDtypeStruct((t, r), BF16),
        grid=(t // tm,),
        in_specs=[pl.BlockSpec((tm, r), lambda i: (i, 0)), pl.BlockSpec((1, r), lambda i: (0, 0))],
        out_specs=pl.BlockSpec((tm, r), lambda i: (i, 0)),
        compiler_params=_cparams(("parallel",)),
    )(projb, g)


def _q_up_rope(cqn, wuq, rc, rs):
    t, r = cqn.shape
    w = wuq.shape[1]
    tm = min(t, 1024)
    bn = 4 * HEAD_PAD

    def body(a_ref, b_ref, c_ref, s_ref, o_ref):
        q = lax.dot_general(a_ref[...], b_ref[...], NN, preferred_element_type=F32)
        c = c_ref[...]
        s = s_ref[...]
        for h in range(bn // HEAD_PAD):
            a = h * HEAD_PAD
            o_ref[:, a:a + LANES] = q[:, a:a + LANES].astype(BF16)
            tr = q[:, a + LANES:a + 2 * LANES]
            o_ref[:, a + LANES:a + 2 * LANES] = (tr * c + _swap_rope_halves(tr) * s).astype(BF16)

    return pl.pallas_call(
        body, name="q_up_rope",
        out_shape=jax.ShapeDtypeStruct((t, w), BF16),
        grid=(t // tm, w // bn),
        in_specs=[pl.BlockSpec((tm, r), lambda i, j: (i, 0)), pl.BlockSpec((r, bn), lambda i, j: (0, j)),
                  pl.BlockSpec((tm, LANES), lambda i, j: (i, 0)), pl.BlockSpec((tm, LANES), lambda i, j: (i, 0))],
        out_specs=pl.BlockSpec((tm, bn), lambda i, j: (i, j)),
        compiler_params=_cparams(("parallel", "parallel")),
    )(cqn, wuq, rc, rs)


def _flash_fwd(qcat, kv, kr, projb):
    t = qcat.shape[0]
    tq = min(t, 512)
    nq = t // tq
    exp2_scale = ATTN_SCALE * math.log2(math.e)
    z_first = Q_LORA_RANK // LANES

    def body(q_ref, kv_ref, kr_ref, z_ref, o_ref, o2_ref, lse_ref, kcat_ref):
        kcat_ref[:, 0:LANES] = kv_ref[:, 0:LANES]
        kcat_ref[:, LANES:] = kr_ref[...]
        rows = lax.broadcasted_iota(jnp.int32, (tq, tq), 0)
        cols = lax.broadcasted_iota(jnp.int32, (tq, tq), 1)
        for i in range(nq):
            a = i * tq
            q = q_ref[a:a + tq, :]
            sd = lax.dot_general(q, kcat_ref[a:a + tq, :], NT, preferred_element_type=F32)
            sd = jnp.where(cols <= rows, sd, MASK_VALUE)
            m = jnp.max(sd, axis=1, keepdims=True)
            if i > 0:
                sp = lax.dot_general(q, kcat_ref[0:a, :], NT, preferred_element_type=F32)
                m = jnp.maximum(m, jnp.max(sp, axis=1, keepdims=True))
            pd = jnp.exp2((sd - m) * exp2_scale)
            l = jnp.sum(pd, axis=1, keepdims=True)
            acc = lax.dot_general(pd.astype(BF16), kv_ref[a:a + tq, LANES:], NN, preferred_element_type=F32)
            if i > 0:
                pp = jnp.exp2((sp - m) * exp2_scale)
                l = l + jnp.sum(pp, axis=1, keepdims=True)
                acc = acc + lax.dot_general(pp.astype(BF16), kv_ref[0:a, LANES:], NN, preferred_element_type=F32)
            o = acc / l
            z = z_ref[a:a + tq, :]
            o_ref[a:a + tq, :] = o
            o2_ref[a:a + tq, :] = (o * (z * _sigmoid(z))).astype(BF16)
            lse_ref[a:a + tq, :] = jnp.broadcast_to(m * ATTN_SCALE + jnp.log(l), (tq, LANES))

    hw = N_HEADS * LANES
    head256 = pl.BlockSpec((t, HEAD_PAD), lambda h: (0, h))
    head128 = pl.BlockSpec((t, LANES), lambda h: (0, h))
    return pl.pallas_call(
        body, name="flash_fwd",
        out_shape=(jax.ShapeDtypeStruct((t, hw), F32), jax.ShapeDtypeStruct((t, hw), BF16),
                   jax.ShapeDtypeStruct((t, hw), F32)),
        grid=(N_HEADS,),
        in_specs=[head256, head256, pl.BlockSpec((t, LANES), lambda h: (0, 0), pipeline_mode=pl.Buffered(1)),
                  pl.BlockSpec((t, LANES), lambda h: (0, z_first + h))],
        out_specs=(head128, head128, head128),
        scratch_shapes=[pltpu.VMEM((t, HEAD_PAD), BF16)],
        compiler_params=_cparams(("parallel",)),
    )(qcat, kv, kr, projb)


def _final_ln_loss(h1, yb, g, b, target):
    t, d = h1.shape
    tm = min(t, 256)

    def body(h_ref, y_ref, g_ref, b_ref, t_ref, loss_ref, dr_ref, drb_ref, dg_ref, db_ref):
        i = pl.program_id(0)
        r = ALPHA * h_ref[...] + y_ref[...]
        mu = jnp.mean(r, axis=-1, keepdims=True)
        xc = r - mu
        var = jnp.mean(xc * xc, axis=-1, keepdims=True)
        rstd = lax.rsqrt(var + LN_EPS)
        xh = xc * rstd
        g = g_ref[...]
        diff = xh * g + b_ref[...] - t_ref[...]
        part = 0.5 * jnp.sum(jnp.mean(diff * diff, axis=-1, keepdims=True))
        dh = diff * (1.0 / d)
        dgp = jnp.sum(dh * xh, axis=0, keepdims=True)
        dbp = jnp.sum(dh, axis=0, keepdims=True)

        @pl.when(i == 0)
        def _():
            loss_ref[...] = jnp.zeros_like(loss_ref)
            dg_ref[...] = jnp.zeros_like(dg_ref)
            db_ref[...] = jnp.zeros_like(db_ref)

        loss_ref[...] += jnp.full(loss_ref.shape, part, F32)
        dg_ref[...] += dgp
        db_ref[...] += dbp
        dxh = dh * g
        dr = rstd * (dxh - jnp.mean(dxh, axis=-1, keepdims=True) - xh * jnp.mean(dxh * xh, axis=-1, keepdims=True))
        dr_ref[...] = dr
        drb_ref[...] = dr.astype(BF16)

    row = pl.BlockSpec((tm, d), lambda i: (i, 0))
    par = pl.BlockSpec((1, d), lambda i: (0, 0))
    return pl.pallas_call(
        body, name="final_ln_loss",
        out_shape=(jax.ShapeDtypeStruct((1, LANES), F32), jax.ShapeDtypeStruct((t, d), F32),
                   jax.ShapeDtypeStruct((t, d), BF16), jax.ShapeDtypeStruct((1, d), F32),
                   jax.ShapeDtypeStruct((1, d), F32)),
        grid=(t // tm,),
        in_specs=[row, row, par, par, row],
        out_specs=(pl.BlockSpec((1, LANES), lambda i: (0, 0)), row, row, par, par),
        compiler_params=_cparams(("arbitrary",)),
    )(h1, yb, g, b, target)


def _gate_bwd(do2, o, projb):
    t, hw = o.shape
    tm = min(t, 512)
    bw = Q_LORA_RANK
    nb = hw // bw
    wb = projb.shape[1]

    def body(d_ref, o_ref, z_ref, dob_ref, dz_ref, dl_ref):
        z = z_ref[...]
        sg = _sigmoid(z)
        d2 = d_ref[...]
        o = o_ref[...]
        do = d2 * (z * sg)
        dob_ref[...] = do.astype(BF16)
        dz_ref[...] = (d2 * o * (sg * (1.0 + z * (1.0 - sg)))).astype(BF16)
        prod = do * o
        for hh in range(bw // LANES):
            dsum = jnp.sum(prod[:, hh * LANES:(hh + 1) * LANES], axis=1, keepdims=True)
            dl_ref[:, hh * LANES:(hh + 1) * LANES] = jnp.broadcast_to(dsum, (tm, LANES))

    blk = pl.BlockSpec((tm, bw), lambda i, j: (i, j))
    blk1 = pl.BlockSpec((tm, bw), lambda i, j: (i, j + 1))
    return pl.pallas_call(
        body, name="gate_bwd",
        out_shape=(jax.ShapeDtypeStruct((t, hw), BF16), jax.ShapeDtypeStruct((t, wb), BF16),
                   jax.ShapeDtypeStruct((t, hw), F32)),
        grid=(t // tm, nb),
        in_specs=[blk, blk, blk1],
        out_specs=(blk, blk1, blk),
        compiler_params=_cparams(("parallel", "parallel")),
    )(do2, o, projb)


def _flash_bwd(qcat, kv, kr, dob, lse, dl, rc, rs):
    t = qcat.shape[0]
    tq = min(t, 512)
    nq = t // tq
    q_chunk = 2 * tq
    log2e = math.log2(math.e)
    exp2_scale = ATTN_SCALE * log2e
    once = pl.Buffered(1)

    def body(q_ref, kv_ref, kr_ref, do_ref, lse_ref, dl_ref, c_ref, s_ref,
             dq_ref, dkv_ref, dkr_ref, kcat_ref, dqa_ref, dka_ref, dva_ref):
        h = pl.program_id(0)
        kcat_ref[:, 0:LANES] = kv_ref[:, 0:LANES]
        kcat_ref[:, LANES:] = kr_ref[...]
        dqa_ref[...] = jnp.zeros_like(dqa_ref)

        @pl.when(h == 0)
        def _():
            dkr_ref[...] = jnp.zeros_like(dkr_ref)

        rows = lax.broadcasted_iota(jnp.int32, (tq, tq), 0)
        cols = lax.broadcasted_iota(jnp.int32, (tq, tq), 1)

        def block(qs, n, ks, masked):
            q = q_ref[qs:qs + n, :]
            kc = kcat_ref[ks:ks + tq, :]
            s = lax.dot_general(q, kc, NT, preferred_element_type=F32)
            if masked:
                s = jnp.where(cols <= rows, s, MASK_VALUE)
            p = jnp.exp2(s * exp2_scale - lse_ref[qs:qs + n, 0:1] * log2e)
            do = do_ref[qs:qs + n, :]
            dp = lax.dot_general(do, kv_ref[ks:ks + tq, LANES:], NT, preferred_element_type=F32)
            ds = (p * (dp - dl_ref[qs:qs + n, 0:1])).astype(BF16)
            dva_ref[...] += lax.dot_general(p.astype(BF16), do, TN, preferred_element_type=F32)
            dka_ref[...] += lax.dot_general(ds, q, TN, preferred_element_type=F32)
            dqa_ref[qs:qs + n, :] += lax.dot_general(ds, kc, NN, preferred_element_type=F32)

        for kb in range(nq):
            a = kb * tq
            dka_ref[...] = jnp.zeros_like(dka_ref)
            dva_ref[...] = jnp.zeros_like(dva_ref)
            block(a, tq, a, True)
            qs = a + tq
            while qs < t:
                n = min(q_chunk, t - qs)
                block(qs, n, a, False)
                qs += n
            dk = dka_ref[...] * ATTN_SCALE
            dkv_ref[a:a + tq, 0:LANES] = dk[:, 0:LANES].astype(BF16)
            dkv_ref[a:a + tq, LANES:] = dva_ref[...].astype(BF16)
            dkr_ref[a:a + tq, :] += dk[:, LANES:]

        dq = dqa_ref[...] * ATTN_SCALE
        dq_ref[:, 0:LANES] = dq[:, 0:LANES].astype(BF16)
        d2 = dq[:, LANES:]
        dq_ref[:, LANES:] = (d2 * c_ref[...] - _swap_rope_halves(d2) * s_ref[...]).astype(BF16)

    hp = N_HEADS * HEAD_PAD
    head256 = pl.BlockSpec((t, HEAD_PAD), lambda h: (0, h))
    head128 = pl.BlockSpec((t, LANES), lambda h: (0, h))
    const128 = pl.BlockSpec((t, LANES), lambda h: (0, 0), pipeline_mode=once)
    return pl.pallas_call(
        body, name="flash_bwd",
        out_shape=(jax.ShapeDtypeStruct((t, hp), BF16), jax.ShapeDtypeStruct((t, hp), BF16),
                   jax.ShapeDtypeStruct((t, LANES), F32)),
        grid=(N_HEADS,),
        in_specs=[head256, head256, const128, head128, head128, head128, const128, const128],
        out_specs=(head256, head256, pl.BlockSpec((t, LANES), lambda h: (0, 0))),
        scratch_shapes=[pltpu.VMEM((t, HEAD_PAD), BF16), pltpu.VMEM((t, HEAD_PAD), F32),
                        pltpu.VMEM((tq, HEAD_PAD), F32), pltpu.VMEM((tq, LANES), F32)],
        compiler_params=_cparams(("arbitrary",)),
    )(qcat, kv, kr, dob, lse, dl, rc, rs)


def _q_norm_bwd(dcqn, projb, g, dprojb):
    t = projb.shape[0]
    tm = min(t, 512)
    r = Q_LORA_RANK

    def body(d_ref, c_ref, g_ref, alias_ref, o_ref, dg_ref):
        i = pl.program_id(0)
        c = c_ref[...]
        d = d_ref[...]
        rr = lax.rsqrt(jnp.mean(c * c, axis=-1, keepdims=True) + RMS_EPS)
        xh = c * rr

        @pl.when(i == 0)
        def _():
            dg_ref[...] = jnp.zeros_like(dg_ref)

        dg_ref[...] += jnp.sum(d * xh, axis=0, keepdims=True)
        dxh = d * g_ref[...]
        o_ref[...] = (rr * (dxh - xh * jnp.mean(dxh * xh, axis=-1, keepdims=True))).astype(BF16)

    blk = pl.BlockSpec((tm, r), lambda i: (i, 0))
    par = pl.BlockSpec((1, r), lambda i: (0, 0))
    return pl.pallas_call(
        body, name="q_norm_bwd",
        out_shape=(jax.ShapeDtypeStruct(dprojb.shape, BF16), jax.ShapeDtypeStruct((1, r), F32)),
        grid=(t // tm,),
        in_specs=[blk, blk, par, pl.BlockSpec(memory_space=pl.ANY)],
        out_specs=(blk, par),
        input_output_aliases={3: 0},
        compiler_params=_cparams(("arbitrary",)),
    )(dcqn, projb, g, dprojb)


def _kv_mid_bwd(dckn, ckv, dkr, g, rc, rs):
    t = ckv.shape[0]
    tm = min(t, 512)
    r = KV_LORA_RANK

    def body(d_ref, ckv_ref, dkr_ref, g_ref, c_ref, s_ref, o_ref, dg_ref):
        i = pl.program_id(0)
        c = ckv_ref[:, 0:r]
        d = d_ref[...]
        rr = lax.rsqrt(jnp.mean(c * c, axis=-1, keepdims=True) + RMS_EPS)
        xh = c * rr

        @pl.when(i == 0)
        def _():
            dg_ref[...] = jnp.zeros_like(dg_ref)

        dg_ref[...] += jnp.sum(d * xh, axis=0, keepdims=True)
        dxh = d * g_ref[...]
        o_ref[:, 0:r] = (rr * (dxh - xh * jnp.mean(dxh * xh, axis=-1, keepdims=True))).astype(BF16)
        dk = dkr_ref[...]
        o_ref[:, r:] = (dk * c_ref[...] - _swap_rope_halves(dk) * s_ref[...]).astype(BF16)

    return pl.pallas_call(
        body, name="kv_mid_bwd",
        out_shape=(jax.ShapeDtypeStruct((t, r + LANES), BF16), jax.ShapeDtypeStruct((1, r), F32)),
        grid=(t // tm,),
        in_specs=[pl.BlockSpec((tm, r), lambda i: (i, 0)), pl.BlockSpec((tm, r + LANES), lambda i: (i, 0)),
                  pl.BlockSpec((tm, LANES), lambda i: (i, 0)), pl.BlockSpec((1, r), lambda i: (0, 0)),
                  pl.BlockSpec((tm, LANES), lambda i: (i, 0)), pl.BlockSpec((tm, LANES), lambda i: (i, 0))],
        out_specs=(pl.BlockSpec((tm, r + LANES), lambda i: (i, 0)), pl.BlockSpec((1, r), lambda i: (0, 0))),
        compiler_params=_cparams(("arbitrary",)),
    )(dckn, ckv, dkr, g, rc, rs)


def _ln0_bwd(dr2, t1, t2, xh, rstd, g):
    t, d = dr2.shape
    tm = min(t, 256)

    def body(a_ref, b_ref, c_ref, xh_ref, rs_ref, g_ref, dr_ref, drb_ref, dg_ref, db_ref, dsum_ref):
        i = pl.program_id(0)
        dh = ALPHA * a_ref[...] + b_ref[...] + c_ref[...]
        xh = xh_ref[...]

        @pl.when(i == 0)
        def _():
            dg_ref[...] = jnp.zeros_like(dg_ref)
            db_ref[...] = jnp.zeros_like(db_ref)
            dsum_ref[...] = jnp.zeros_like(dsum_ref)

        dg_ref[...] += jnp.sum(dh * xh, axis=0, keepdims=True)
        db_ref[...] += jnp.sum(dh, axis=0, keepdims=True)
        dxh = dh * g_ref[...]
        dr = rs_ref[...] * (dxh - jnp.mean(dxh, axis=-1, keepdims=True) - xh * jnp.mean(dxh * xh, axis=-1, keepdims=True))
        dr_ref[...] = dr
        drb_ref[...] = dr.astype(BF16)
        dsum_ref[...] += jnp.sum(dr, axis=0, keepdims=True)

    row = pl.BlockSpec((tm, d), lambda i: (i, 0))
    par = pl.BlockSpec((1, d), lambda i: (0, 0))
    return pl.pallas_call(
        body, name="ln0_bwd",
        out_shape=(jax.ShapeDtypeStruct((t, d), F32), jax.ShapeDtypeStruct((t, d), BF16),
                   jax.ShapeDtypeStruct((1, d), F32), jax.ShapeDtypeStruct((1, d), F32),
                   jax.ShapeDtypeStruct((1, d), F32)),
        grid=(t // tm,),
        in_specs=[row, row, row, row, pl.BlockSpec((tm, 1), lambda i: (i, 0)), par],
        out_specs=(row, row, par, par, par),
        compiler_params=_cparams(("arbitrary",)),
    )(dr2, t1, t2, xh, rstd, g)


def _conv_mid_bwd_rows(u1, proj, du2, ng, nb):
    t = u1.shape[0]
    e = CONV_WIDTH
    tm = min(t, 256)

    rg = 16
    nt = t // tm

    def body(u1_ref, z_ref, d_ref, ng_ref, nb_ref, du1_ref, dz_ref, dng_ref, dnb_ref, dbz_ref, acc_ref):
        i = pl.program_id(0)

        @pl.when(i == 0)
        def _():
            acc_ref[...] = jnp.zeros_like(acc_ref)

        g = ng_ref[...]
        b = nb_ref[...]

        def group(r, carry):
            rows = pl.ds(pl.multiple_of(r * rg, rg), rg)
            u1 = u1_ref[rows, :]
            mu = jnp.mean(u1, axis=-1, keepdims=True)
            xc = u1 - mu
            rstd = lax.rsqrt(jnp.mean(xc * xc, axis=-1, keepdims=True) + LN_EPS)
            xh = xc * rstd
            n = xh * g + b
            sn = _sigmoid(n)
            z = z_ref[rows, :]
            sz = _sigmoid(z)
            du2 = d_ref[rows, :]
            dz = du2 * (n * sn) * (sz * (1.0 + z * (1.0 - sz)))
            dn = du2 * (z * sz) * (sn * (1.0 + n * (1.0 - sn)))
            acc_ref[0:8, :] += _fold_rows(dn * xh)
            acc_ref[8:16, :] += _fold_rows(dn)
            acc_ref[16:24, :] += _fold_rows(dz)
            dz_ref[rows, :] = dz.astype(BF16)
            dxh = dn * g
            du1_ref[rows, :] = rstd * (dxh - jnp.mean(dxh, axis=-1, keepdims=True)
                                       - xh * jnp.mean(dxh * xh, axis=-1, keepdims=True))
            return carry

        lax.fori_loop(0, tm // rg, group, 0, unroll=8)

        @pl.when(i == nt - 1)
        def _():
            dng_ref[...] = jnp.sum(acc_ref[0:8, :], axis=0, keepdims=True)
            dnb_ref[...] = jnp.sum(acc_ref[8:16, :], axis=0, keepdims=True)
            dbz_ref[...] = jnp.sum(acc_ref[16:24, :], axis=0, keepdims=True)

    row = pl.BlockSpec((tm, e), lambda i: (i, 0))
    par = pl.BlockSpec((1, e), lambda i: (0, 0))
    return pl.pallas_call(
        body, name="conv_mid_bwd_rows",
        out_shape=(jax.ShapeDtypeStruct((t, e), F32), jax.ShapeDtypeStruct((t, 3 * e), BF16),
                   jax.ShapeDtypeStruct((1, e), F32), jax.ShapeDtypeStruct((1, e), F32),
                   jax.ShapeDtypeStruct((1, e), F32)),
        grid=(nt,),
        in_specs=[row, pl.BlockSpec((tm, e), lambda i: (i, 2)), row, par, par],
        out_specs=(row, pl.BlockSpec((tm, e), lambda i: (i, 2)), par, par, par),
        scratch_shapes=[pltpu.VMEM((24, e), F32)],
        compiler_params=_cparams(("arbitrary",)),
    )(u1, proj, du2, ng, nb)


def _conv_bwd(du1, proj, cw, dproj):
    t = du1.shape[0]
    e = CONV_WIDTH
    tm = min(t, 128)
    hb = tm // HALO
    nt = t // tm
    nchunk = e // LANES
    last_halo = t // HALO - 1

    def body(d_ref, dn_ref, val_ref, gg_ref, valh_ref, ggh_ref, cw_ref, alias_ref,
             dp_ref, dcw_ref, dcb_ref, dbv_ref, extu_ref, extd_ref, du0_ref, acc_ref, sh_ref):
        i = pl.program_id(0)
        h0 = valh_ref[...] * _sigmoid(ggh_ref[...])
        extu_ref[0:HALO, :] = jnp.where(i > 0, h0, 0.0)
        extu_ref[HALO:, :] = val_ref[...] * _sigmoid(gg_ref[...])
        extd_ref[0:tm, :] = d_ref[...]
        extd_ref[tm:, :] = jnp.where(i < nt - 1, dn_ref[...], 0.0)

        @pl.when(i == 0)
        def _():
            acc_ref[...] = jnp.zeros_like(acc_ref)

        def chunk(c, carry):
            col = pl.multiple_of(c * LANES, LANES)
            d = extd_ref[0:tm, pl.ds(col, LANES)]
            du0 = jnp.zeros((tm, LANES), F32)
            for res, taps in _taps_by_residue(lambda j: CONV_KERNEL - 1 - j):
                span = 8 * max(a for _, a in taps) + tm
                sh_ref[0:span, :] = extd_ref[res:res + span, pl.ds(col, LANES)]
                for j, a in taps:
                    du0 = du0 + cw_ref[j:j + 1, pl.ds(col, LANES)] * sh_ref[8 * a:8 * a + tm, :]
            for res, taps in _taps_by_residue(lambda j: HALO - (CONV_KERNEL - 1) + j):
                span = 8 * max(a for _, a in taps) + tm
                sh_ref[0:span, :] = extu_ref[res:res + span, pl.ds(col, LANES)]
                for j, a in taps:
                    prod = d * sh_ref[8 * a:8 * a + tm, :]
                    acc_ref[j * 8:(j + 1) * 8, pl.ds(col, LANES)] += _fold_rows(prod)
            acc_ref[CONV_KERNEL * 8:(CONV_KERNEL + 1) * 8, pl.ds(col, LANES)] += _fold_rows(d)
            du0_ref[:, pl.ds(col, LANES)] = du0
            return carry

        lax.fori_loop(0, nchunk, chunk, 0)
        kb = CONV_KERNEL * 8
        for r in range(0, tm, 16):
            sg = _sigmoid(gg_ref[r:r + 16, :])
            dval = du0_ref[r:r + 16, :] * sg
            dgg = dval * val_ref[r:r + 16, :] * (1.0 - sg)
            dp_ref[r:r + 16, 0:e] = dval.astype(BF16)
            dp_ref[r:r + 16, e:] = dgg.astype(BF16)
            acc_ref[kb + 8:kb + 16, :] += _fold_rows(dval)
            acc_ref[kb + 16:kb + 24, :] += _fold_rows(dgg)

        @pl.when(i == nt - 1)
        def _():
            for j in range(CONV_KERNEL):
                dcw_ref[j:j + 1, :] = jnp.sum(acc_ref[j * 8:(j + 1) * 8, :], axis=0, keepdims=True)
            dcw_ref[CONV_KERNEL:, :] = jnp.zeros((HALO - CONV_KERNEL, e), F32)
            dcb_ref[...] = jnp.sum(acc_ref[kb:kb + 8, :], axis=0, keepdims=True)
            dbv_ref[:, 0:e] = jnp.sum(acc_ref[kb + 8:kb + 16, :], axis=0, keepdims=True)
            dbv_ref[:, e:] = jnp.sum(acc_ref[kb + 16:kb + 24, :], axis=0, keepdims=True)

    row = lambda c: pl.BlockSpec((tm, e), lambda i: (i, c))
    halo_prev = lambda c: pl.BlockSpec((HALO, e), lambda i: (jnp.maximum(i * hb - 1, 0), c))
    halo_next = pl.BlockSpec((HALO, e), lambda i: (jnp.minimum((i + 1) * hb, last_halo), 0))
    return pl.pallas_call(
        body, name="conv_bwd",
        out_shape=(jax.ShapeDtypeStruct(dproj.shape, BF16), jax.ShapeDtypeStruct((HALO, e), F32),
                   jax.ShapeDtypeStruct((1, e), F32), jax.ShapeDtypeStruct((1, 2 * e), F32)),
        grid=(nt,),
        in_specs=[row(0), halo_next, row(0), row(1), halo_prev(0), halo_prev(1),
                  pl.BlockSpec((HALO, e), lambda i: (0, 0)), pl.BlockSpec(memory_space=pl.ANY)],
        out_specs=(pl.BlockSpec((tm, 2 * e), lambda i: (i, 0)), pl.BlockSpec((HALO, e), lambda i: (0, 0)),
                   pl.BlockSpec((1, e), lambda i: (0, 0)), pl.BlockSpec((1, 2 * e), lambda i: (0, 0))),
        scratch_shapes=[pltpu.VMEM((tm + HALO, e), F32), pltpu.VMEM((tm + HALO, e), F32),
                        pltpu.VMEM((tm, e), F32), pltpu.VMEM(((CONV_KERNEL + 3) * 8, e), F32),
                        pltpu.VMEM((tm + HALO, LANES), F32)],
        input_output_aliases={7: 0},
        compiler_params=_cparams(("arbitrary",)),
    )(du1, du1, proj, proj, proj, proj, cw, dproj)


def _adam_update(g, w, m, v):
    c1 = 1.0 - ADAM_B1 ** ADAM_STEP
    c2 = 1.0 - ADAM_B2 ** ADAM_STEP
    mn = ADAM_B1 * m + (1.0 - ADAM_B1) * g
    vn = ADAM_B2 * v + (1.0 - ADAM_B2) * (g * g)
    delta = -ADAM_LR * ((mn / c1) / (jnp.sqrt(vn / c2) + ADAM_EPS) + ADAM_WD * w)
    return delta, mn, vn


def _adamw(name, gbuf, w, m, v):
    parts = list(gbuf) if isinstance(gbuf, (list, tuple)) else [gbuf]
    npart = len(parts)
    r, c = w.shape
    tr = min(r // npart, 256)
    per_part = r // npart // tr
    assert r == npart * per_part * tr and all(p.shape == (N_DEV, r // npart, c) for p in parts)

    def body(*refs):
        g_refs = refs[:npart]
        w_ref, m_ref, v_ref, go_ref, d_ref, mo_ref, vo_ref = refs[npart:]
        i = pl.program_id(0)

        def run(g_ref):
            g = g_ref[0].astype(F32)
            for k in range(1, N_DEV):
                g = g + g_ref[k].astype(F32)
            go_ref[...] = g
            d_ref[...], mo_ref[...], vo_ref[...] = _adam_update(g, w_ref[...], m_ref[...], v_ref[...])

        if npart == 1:
            run(g_refs[0])
        else:
            for q in range(npart):
                pl.when((i >= q * per_part) & (i < (q + 1) * per_part))(functools.partial(run, g_refs[q]))

    blk = pl.BlockSpec((tr, c), lambda i: (i, 0))
    part_spec = lambda q: pl.BlockSpec((N_DEV, tr, c), lambda i: (0, jnp.clip(i - q * per_part, 0, per_part - 1), 0))
    shp = jax.ShapeDtypeStruct((r, c), F32)
    return pl.pallas_call(
        body, name="adamw_" + name,
        out_shape=(shp, shp, shp, shp),
        grid=(r // tr,),
        in_specs=[part_spec(q) for q in range(npart)] + [blk, blk, blk],
        out_specs=(blk, blk, blk, blk),
        compiler_params=_cparams(("parallel",)),
    )(*parts, w, m, v)


SMALL_TABLE_COLS = 256
SMALL_LAYOUT = {
    'ln_g': (0, 2, 1024), 'ln_b': (8, 2, 1024), 'a_b_in': (16, 1, 768), 'a_conv_b': (24, 1, 256),
    'a_norm_g': (32, 1, 256), 'a_norm_b': (40, 1, 256), 'a_b_out': (48, 1, 128), 'kv_norm_g': (56, 1, 256),
    'b_q_norm_g': (64, 1, 512), 'a_conv_w': (72, CONV_KERNEL, 256),
}
SMALL_TABLE_ROWS = 104
SMALL_NAMES = tuple(SMALL_LAYOUT)


def _small_pieces(name, row):
    r0, _, c = SMALL_LAYOUT[name]
    w = min(c, SMALL_TABLE_COLS)
    per_row = c // w
    return [(r0 + row * per_row + q, q * w, w) for q in range(per_row)]


def _pack_small_grads(dg0, dg1, db0, db1, dbv, dbz, dcb, dng, dnb, dba, dgkv, dgq, dcw):
    e = CONV_WIDTH
    ce = e // N_DEV

    def body(dg0_ref, dg1_ref, db0_ref, db1_ref, dbv_ref, dbz_ref, dcb_ref, dng_ref, dnb_ref, dba_ref, dgkv_ref,
             dgq_ref, dcw_ref, o_ref):
        o_ref[...] = jnp.zeros_like(o_ref)

        def put(d, name, row, src_ref, first_col=0):
            for trow, col, w in _small_pieces(name, row):
                o_ref[d, trow:trow + 1, 0:w] = src_ref[:, first_col + col:first_col + col + w]

        for d in range(N_DEV):
            put(d, 'ln_g', 0, dg0_ref)
            put(d, 'ln_g', 1, dg1_ref)
            put(d, 'ln_b', 0, db0_ref)
            put(d, 'ln_b', 1, db1_ref)
            for trow, col, w in _small_pieces('a_b_in', 0):
                gcol = d * 3 * ce + col
                src = dbv_ref[:, gcol:gcol + w] if gcol < 2 * e else dbz_ref[:, gcol - 2 * e:gcol - 2 * e + w]
                o_ref[d, trow:trow + 1, 0:w] = src
            put(d, 'a_conv_b', 0, dcb_ref, d * ce)
            put(d, 'a_norm_g', 0, dng_ref, d * ce)
            put(d, 'a_norm_b', 0, dnb_ref, d * ce)
            put(d, 'a_b_out', 0, dba_ref, d * LANES)
            put(d, 'kv_norm_g', 0, dgkv_ref)
            put(d, 'b_q_norm_g', 0, dgq_ref)
            r0 = SMALL_LAYOUT['a_conv_w'][0]
            o_ref[d, r0:r0 + HALO, 0:ce] = dcw_ref[:, d * ce:(d + 1) * ce]

    return pl.pallas_call(
        body, name="pack_small_grads",
        out_shape=jax.ShapeDtypeStruct((N_DEV, SMALL_TABLE_ROWS, SMALL_TABLE_COLS), F32),
        compiler_params=pltpu.CompilerParams(vmem_limit_bytes=VMEM_LIMIT),
    )(dg0, dg1, db0, db1, dbv, dbz, dcb, dng, dnb, dba, dgkv, dgq, dcw)


def _adamw_small(gtab, ws, ms, vs):
    n = len(SMALL_NAMES)

    def body(*refs):
        g_ref = refs[0]
        w_refs, m_refs, v_refs = refs[1:1 + n], refs[1 + n:1 + 2 * n], refs[1 + 2 * n:1 + 3 * n]
        outs = refs[1 + 3 * n:]

        def summed(r0, r, c):
            g = g_ref[0, r0:r0 + r, 0:c]
            for k in range(1, N_DEV):
                g = g + g_ref[k, r0:r0 + r, 0:c]
            return g

        for i, name in enumerate(SMALL_NAMES):
            r0, r, c = SMALL_LAYOUT[name]
            if c <= SMALL_TABLE_COLS:
                blocks = [(slice(None), summed(r0, r, c))]
            else:
                blocks = [(slice(row, row + 1), jnp.concatenate([summed(tr, 1, w) for tr, _, w in _small_pieces(name, row)], axis=1))
                          for row in range(r)]
            for rows, g in blocks:
                delta, mn, vn = _adam_update(g, w_refs[i][rows, :], m_refs[i][rows, :], v_refs[i][rows, :])
                outs[i][rows, :] = g
                outs[n + i][rows, :] = delta
                outs[2 * n + i][rows, :] = mn
                outs[3 * n + i][rows, :] = vn

    shapes = tuple(jax.ShapeDtypeStruct(w.shape, F32) for w in ws)
    res = pl.pallas_call(
        body, name="adamw_small",
        out_shape=shapes * 4,
        compiler_params=pltpu.CompilerParams(vmem_limit_bytes=VMEM_LIMIT),
    )(gtab, *ws, *ms, *vs)
    return res[:n], res[n:2 * n], res[2 * n:3 * n], res[3 * n:]


def _mesh_peers():
    x, y, c = lax.axis_index("x"), lax.axis_index("y"), lax.axis_index("c")
    me = 4 * x + 2 * y + c
    peers = []
    for k in range(1, N_DEV):
        px = 1 - x if (k >> 2) & 1 else x
        py = 1 - y if (k >> 1) & 1 else y
        pc = 1 - c if k & 1 else c
        peers.append(((px, py, pc), 4 * px + 2 * py + pc))
    return me, peers


def _all_gather(name, shards):
    na = len(shards)

    def body(*refs):
        srcs, dsts = refs[:na], refs[na:2 * na]
        send_sems, recv_sems, local_sems = refs[2 * na:]
        x, y, c = lax.axis_index("x"), lax.axis_index("y"), lax.axis_index("c")
        me, sibling = (x, y, c), (x, y, 1 - c)
        chips = [(1 - x, y), (x, 1 - y), (1 - x, 1 - y)]
        slot = lambda dev: 4 * dev[0] + 2 * dev[1] + dev[2]

        def copy(a, k, block, to, own=False):
            return pltpu.make_async_remote_copy(
                src_ref=srcs[a] if own else dsts[a].at[slot(block)], dst_ref=dsts[a].at[slot(block)],
                send_sem=send_sems.at[a, k], recv_sem=recv_sems.at[a, k], device_id=to,
                device_id_type=pl.DeviceIdType.MESH)

        local = [pltpu.make_async_copy(srcs[a], dsts[a].at[slot(me)], local_sems.at[a]) for a in range(na)]
        for cp in local:
            cp.start()
        sends = []
        for a in range(na):
            sends.append(copy(a, 0, me, sibling, own=True))
            sends += [copy(a, 1 + j, me, (*chip, c), own=True) for j, chip in enumerate(chips)]
        for cp in sends:
            cp.start()
        for j, chip in enumerate(chips):
            for a in range(na):
                copy(a, 1 + j, (*chip, c), me).wait_recv()
                fwd = copy(a, 4 + j, (*chip, c), sibling)
                fwd.start()
                sends.append(fwd)
        for a in range(na):
            copy(a, 0, sibling, me).wait_recv()
            for j, chip in enumerate(chips):
                copy(a, 4 + j, (*chip, 1 - c), me).wait_recv()
        for cp in sends:
            cp.wait_send()
        for cp in local:
            cp.wait()

    hbm = pl.BlockSpec(memory_space=pl.ANY)
    return pl.pallas_call(
        body, name=name,
        out_shape=tuple(jax.ShapeDtypeStruct((N_DEV,) + s.shape, s.dtype) for s in shards),
        in_specs=[hbm] * na, out_specs=(hbm,) * na,
        scratch_shapes=[pltpu.SemaphoreType.DMA((na, N_DEV - 1)), pltpu.SemaphoreType.DMA((na, N_DEV - 1)),
                        pltpu.SemaphoreType.DMA((na,))],
    )(*shards)


_HBM_SPEC = pl.BlockSpec(memory_space=pltpu.HBM)
_SEM_SPEC = pl.BlockSpec(memory_space=pltpu.SEMAPHORE)


def _split_copies(srcs, lands, send_sems, recv_sems, local_sems, gather):
    me, peers = _mesh_peers()
    na = len(srcs)
    mine = lambda a, slot: srcs[a] if gather else srcs[a].at[slot]
    local = [pltpu.make_async_copy(mine(a, me), lands[a].at[me], local_sems.at[a]) for a in range(na)]
    sent, received = [], []
    for k, (dev, pid) in enumerate(peers):
        for a in range(na):
            s = a * (N_DEV - 1) + k
            sent.append(pltpu.make_async_remote_copy(
                src_ref=mine(a, pid), dst_ref=lands[a].at[me], send_sem=send_sems.at[s],
                recv_sem=recv_sems.at[s], device_id=dev, device_id_type=pl.DeviceIdType.MESH))
            received.append(pltpu.make_async_remote_copy(
                src_ref=mine(a, pid), dst_ref=lands[a].at[pid], send_sem=send_sems.at[s],
                recv_sem=recv_sems.at[s], device_id=dev, device_id_type=pl.DeviceIdType.MESH))
    return local, sent, received


def _comm_start(name, srcs, gather, after):
    na = len(srcs)
    land_structs = [jax.ShapeDtypeStruct(((N_DEV,) + s.shape) if gather else s.shape, s.dtype) for s in srcs]
    extra = [] if after is None else [after]
    n_in = 2 * na + len(extra)

    def body(*refs):
        srcs_r, lands_r = refs[:na], refs[na:2 * na]
        send_sems, recv_sems, local_sems = refs[n_in:n_in + 3]
        token = refs[-1]
        local, sent, _ = _split_copies(srcs_r, lands_r, send_sems, recv_sems, local_sems, gather)
        for cp in local + sent:
            cp.start()
        token[...] = jnp.zeros_like(token)

    sem = pltpu.SemaphoreType.DMA((na * (N_DEV - 1),))
    outs = pl.pallas_call(
        body, name=name,
        out_shape=(sem, sem, pltpu.SemaphoreType.DMA((na,)),
                   *[pltpu.HBM(s.shape, s.dtype) for s in srcs],
                   *[pltpu.HBM(s.shape, s.dtype) for s in land_structs],
                   jax.ShapeDtypeStruct((8, LANES), F32)),
        in_specs=[_HBM_SPEC] * (2 * na) + [pl.BlockSpec(memory_space=pl.ANY)] * len(extra),
        out_specs=(_SEM_SPEC, _SEM_SPEC, _SEM_SPEC, *([_HBM_SPEC] * (2 * na)), pl.BlockSpec(memory_space=pltpu.VMEM)),
        input_output_aliases={i: 3 + i for i in range(2 * na)},
        compiler_params=pltpu.CompilerParams(has_side_effects=pltpu.SideEffectType.DATAFLOW_SIDE_EFFECTING),
    )(*[pltpu.with_memory_space_constraint(s, pltpu.HBM) for s in srcs],
      *[pltpu.with_memory_space_constraint(lax.empty(s.shape, s.dtype), pltpu.HBM) for s in land_structs],
      *extra)
    return (outs[:3], outs[3:3 + na], outs[3 + na:3 + 2 * na]), outs[-1]


def _comm_wait(name, handles, gather, after):
    (send_sems, recv_sems, local_sems), srcs, lands = handles
    na = len(srcs)

    def body(*refs):
        srcs_r, lands_r = refs[:na], refs[na:2 * na]
        send_r, recv_r, local_r = refs[2 * na:2 * na + 3]
        local, sent, received = _split_copies(srcs_r, lands_r, send_r, recv_r, local_r, gather)
        for cp in sent:
            cp.wait_send()
        for cp in received:
            cp.wait_recv()
        for cp in local:
            cp.wait()

    outs = pl.pallas_call(
        body, name=name,
        out_shape=(*[pltpu.HBM(s.shape, s.dtype) for s in srcs], *[pltpu.HBM(s.shape, s.dtype) for s in lands]),
        in_specs=[_HBM_SPEC] * (2 * na) + [_SEM_SPEC] * 3 + [pl.BlockSpec(memory_space=pl.ANY)],
        out_specs=tuple([_HBM_SPEC] * (2 * na)),
        input_output_aliases={i: i for i in range(2 * na)},
        compiler_params=pltpu.CompilerParams(has_side_effects=pltpu.SideEffectType.DATAFLOW_SIDE_EFFECTING),
    )(*srcs, *lands, send_sems, recv_sems, local_sems, after)
    return outs[na:]


def _prep_weights(w):
    e = CONV_WIDTH
    p = {}
    if 'a_w_in' in w:
        if w['a_w_in'].shape == (N_DEV, D_MODEL, 3 * e // N_DEV):
            p['a_w_in3'] = w['a_w_in']
        else:
            p['a_w_in3'] = w['a_w_in'].reshape(D_MODEL, N_DEV, 3 * e // N_DEV).transpose(1, 0, 2)
        p['a_b_in'] = w['a_b_in'].reshape(1, 3 * e)
        p['a_conv_w'] = jnp.pad(w['a_conv_w'].reshape(CONV_KERNEL, e), ((0, HALO - CONV_KERNEL), (0, 0)))
        p['a_conv_b'] = w['a_conv_b'].reshape(1, e)
        p['a_norm_g'] = w['a_norm_g'].reshape(1, e)
        p['a_norm_b'] = w['a_norm_b'].reshape(1, e)
        p['a_b_out'] = w['a_b_out'].reshape(1, D_MODEL)
        p['kv_norm_g'] = w['kv_norm_g'].reshape(1, KV_LORA_RANK)
        p['b_q_norm_g'] = w['b_q_norm_g'].reshape(1, Q_LORA_RANK)
        p['ln_g'] = w['ln_g']
        p['ln_b'] = w['ln_b']
    if 'a_w_out' in w:
        p['a_w_out'] = w['a_w_out'].reshape(e, D_MODEL)
        p['kv_w_down'] = jnp.pad(w['kv_w_down'], ((0, 0), (0, KV_LORA_RANK + LANES - w['kv_w_down'].shape[1])))
        p['kv_w_ukv'] = jnp.concatenate([w['kv_w_uk'], w['kv_w_uv']], axis=2).reshape(KV_LORA_RANK, N_HEADS * HEAD_PAD)
        p['b_w_in'] = w['b_w_in'].reshape(D_MODEL, -1)
        uq = w['b_w_uq'].reshape(Q_LORA_RANK, N_HEADS, QK_NOPE_DIM + QK_ROPE_DIM)
        p['b_w_uq'] = jnp.pad(uq, ((0, 0), (0, 0), (0, HEAD_PAD - uq.shape[2]))).reshape(Q_LORA_RANK, N_HEADS * HEAD_PAD)
        p['b_w_out'] = w['b_w_out'].reshape(N_HEADS * V_HEAD_DIM, D_MODEL)
    return p


class _NoComm:
    def __init__(self):
        self.grads = {}
        self.small = None

    def first_after(self):
        return None

    def rest_weights(self, after):
        return {}

    def send_group1(self, g):
        self.grads.update(g)
        return None

    def send_group1b(self, g):
        self.grads.update(g)
        return None

    def send_group2(self, dw_lo, small):
        self.grads['a_w_in_lo'] = dw_lo
        self.small = small
        return None

    def send_group3(self, dw_hi):
        self.grads['a_w_in_hi'] = dw_hi
        return None


def _local_step(x, positions, target, p, comm):
    t = x.shape[0]
    e = CONV_WIDTH
    freqs = ROPE_THETA ** (-jnp.arange(0, QK_ROPE_DIM, 2, dtype=F32) / QK_ROPE_DIM)
    freq_row = jnp.concatenate([freqs, freqs, jnp.zeros((LANES - QK_ROPE_DIM,), F32)]).reshape(1, LANES)
    rc, rs = _rope_tables(positions.reshape(t, 1), freq_row)
    xb = x.astype(BF16)
    ln_g0, ln_b0 = p['ln_g'][0:1], p['ln_b'][0:1]
    ln_g1, ln_b1 = p['ln_g'][1:2], p['ln_b'][1:2]

    proj = _matmul(xb, p['a_w_in3'], bias=p['a_b_in'], name="a_in", b_blocked=True, bm=2048,
                   after=comm.first_after())
    u1, u2 = _conv_mid_fwd(proj, p['a_conv_w'], p['a_conv_b'], p['a_norm_g'], p['a_norm_b'])
    p = {**p, **comm.rest_weights(u2)}
    y = _matmul(u2, p['a_w_out'], bias=p['a_b_out'], name="a_out", bk=2048)
    h1, h1b, xh1, rstd1 = _ln_res_fwd(x, y, ln_g0, ln_b0)
    ckv = _matmul(h1b, p['kv_w_down'], name="kv_down")
    ckn, kr = _kv_mid_fwd(ckv, p['kv_norm_g'], rc, rs)
    kv = _matmul(ckn, p['kv_w_ukv'], name="kv_up", out_dtype=BF16)
    projb = _matmul(h1b, p['b_w_in'], name="b_in", bn=1280)
    cqn = _q_norm_fwd(projb, p['b_q_norm_g'])
    qcat = _q_up_rope(cqn, p['b_w_uq'], rc, rs)
    o, o2, lse = _flash_fwd(qcat, kv, kr, projb)
    yb = _matmul(o2, p['b_w_out'], name="b_out", bk=2048)
    loss, dr2, dr2b, dg1, db1 = _final_ln_loss(h1, yb, ln_g1, ln_b1, target)

    g = {}
    g['b_w_out'] = _matmul(o2, dr2b, trans_a=True, name="d_b_out", out_dtype=BF16, bm=512, bk=t)
    do2 = _matmul(dr2b, p['b_w_out'], trans_b=True, name="d_o2")
    dob, dprojb, dl = _gate_bwd(do2, o, projb)
    dq2, dkv, dkr = _flash_bwd(qcat, kv, kr, dob, lse, dl, rc, rs)
    duq = _matmul(cqn, dq2, trans_a=True, name="d_q_up", out_dtype=BF16, bk=t)
    dcqn = _matmul(dq2, p['b_w_uq'], trans_b=True, name="d_cqn", bk=N_HEADS * HEAD_PAD)
    dprojb, dgq = _q_norm_bwd(dcqn, projb, p['b_q_norm_g'], dprojb)
    g['b_w_in'] = _matmul(h1b, dprojb, trans_a=True, name="d_b_in", bn=640, bk=t, out_dtype=BF16)
    t1 = _matmul(dprojb, p['b_w_in'], trans_b=True, name="d_h1_b", bk=dprojb.shape[1])
    dukv = _matmul(ckn, dkv, trans_a=True, name="d_kv_up", out_dtype=BF16, bk=t)
    dckn = _matmul(dkv, p['kv_w_ukv'], trans_b=True, name="d_ckn", bk=N_HEADS * HEAD_PAD)
    dckv, dgkv = _kv_mid_bwd(dckn, ckv, dkr, p['kv_norm_g'], rc, rs)
    ddown = _matmul(h1b, dckv, trans_a=True, name="d_kv_down", out_dtype=BF16, bm=512, bk=t)
    g['b_w_out'] = g['b_w_out'].reshape(N_DEV, -1, D_MODEL)
    g['b_w_in'] = g['b_w_in'].reshape(D_MODEL, N_DEV, -1).transpose(1, 0, 2)
    g['kv_w_down'] = ddown[:, :KV_LORA_RANK + QK_ROPE_DIM].reshape(N_DEV, -1, KV_LORA_RANK + QK_ROPE_DIM)
    dukv = dukv.reshape(KV_LORA_RANK, N_HEADS, HEAD_PAD)
    g['kv_w_uk'] = dukv[:, :, :QK_NOPE_DIM].reshape(N_DEV, -1, QK_NOPE_DIM)
    g['kv_w_uv'] = dukv[:, :, QK_NOPE_DIM:].reshape(N_DEV, -1, V_HEAD_DIM)
    g['b_w_uq'] = duq.reshape(Q_LORA_RANK, N_HEADS, HEAD_PAD)[:, :, :QK_NOPE_DIM + QK_ROPE_DIM].reshape(
        N_DEV, -1, QK_NOPE_DIM + QK_ROPE_DIM)
    sent1 = comm.send_group1(g)
    t2 = _matmul(dckv, p['kv_w_down'], trans_b=True, name="d_h1_kv", after=sent1)
    dr1, dr1b, dg0, db0, dba_out = _ln0_bwd(dr2, t1, t2, xh1, rstd1, ln_g0)
    dw_out = _matmul(u2, dr1b, trans_a=True, name="d_a_out", out_dtype=BF16, bm=512, bk=t).reshape(N_DEV, -1, D_MODEL)
    sent1b = comm.send_group1b({'a_w_out': dw_out})
    du2 = _matmul(dr1b, p['a_w_out'], trans_b=True, name="d_u2", after=sent1b)
    du1, dproj, dng, dnb, dbz = _conv_mid_bwd_rows(u1, proj, du2, p['a_norm_g'], p['a_norm_b'])
    dproj, dcw, dcb, dbv = _conv_bwd(du1, proj, p['a_conv_w'], dproj)
    half = D_MODEL // 2
    dw_lo = _matmul(xb[:, :half], dproj, trans_a=True, name="d_a_in_lo", out_dtype=BF16, bn=3 * e // N_DEV, bk=t,
                    out_blocked=True)
    small = (dg0, dg1, db0, db1, dbv, dbz, dcb, dng, dnb, dba_out, dgkv, dgq, dcw)
    sent2 = comm.send_group2(dw_lo, small)
    dw_hi = _matmul(xb[:, half:], dproj, trans_a=True, name="d_a_in_hi", out_dtype=BF16, bn=3 * e // N_DEV, bk=t,
                    out_blocked=True, after=sent2)
    sent3 = comm.send_group3(dw_hi)
    grad_x = _grad_x(dproj, p['a_w_in3'], dr1, sent3)
    return loss, grad_x


def _grad_x(dproj, w3, dr1, after):
    t, d = dr1.shape
    nblk, _, cb = w3.shape
    tm = min(t, 512)
    extra = [] if after is None else [after]

    def body(a_ref, w_ref, dr_ref, *rest):
        o_ref = rest[-1]
        acc = ALPHA * dr_ref[...]
        for k in range(nblk):
            acc = acc + lax.dot_general(a_ref[:, k * cb:(k + 1) * cb], w_ref[k], NT, preferred_element_type=F32)
        o_ref[...] = acc

    row = pl.BlockSpec((tm, d), lambda i: (i, 0))
    return pl.pallas_call(
        body, name="grad_x",
        out_shape=jax.ShapeDtypeStruct((t, d), F32),
        grid=(t // tm,),
        in_specs=[pl.BlockSpec((tm, nblk * cb), lambda i: (i, 0)),
                  pl.BlockSpec((nblk, d, cb), lambda i: (0, 0, 0), pipeline_mode=pl.Buffered(1)), row]
                 + [pl.BlockSpec(memory_space=pl.ANY)] * len(extra),
        out_specs=row,
        compiler_params=_cparams(("parallel",)),
    )(dproj, w3, dr1, *extra)


def _shard_2d(a):
    return a.reshape(-1, a.shape[-1])


def _gathered_to_full(name, blocks):
    if name == 'a_w_in':
        return blocks
    if name == 'b_w_in':
        return blocks.transpose(1, 0, 2).reshape(D_MODEL, -1)
    if name == 'a_conv_w':
        return blocks.transpose(1, 0, 2).reshape(CONV_KERNEL, -1)
    if name in ('a_b_in', 'a_conv_b', 'a_norm_g', 'a_norm_b', 'a_b_out'):
        return blocks.reshape(-1)
    if name in ('kv_w_uk', 'kv_w_uv'):
        return blocks.reshape(KV_LORA_RANK, N_HEADS, -1)
    if name == 'b_w_uq':
        return blocks.reshape(Q_LORA_RANK, N_HEADS, -1)
    return blocks.reshape(-1, blocks.shape[-1])


def kernel(x, positions, ln_g, ln_b, a_w_in, a_b_in, a_conv_w, a_conv_b, a_norm_g, a_norm_b, a_w_out, a_b_out, kv_w_down, kv_norm_g, kv_w_uk, kv_w_uv, b_w_in, b_q_norm_g, b_w_uq, b_w_out, loss_target, m_ln_g, m_ln_b, m_a_w_in, m_a_b_in, m_a_conv_w, m_a_conv_b, m_a_norm_g, m_a_norm_b, m_a_w_out, m_a_b_out, m_kv_w_down, m_kv_norm_g, m_kv_w_uk, m_kv_w_uv, m_b_w_in, m_b_q_norm_g, m_b_w_uq, m_b_w_out, v_ln_g, v_ln_b, v_a_w_in, v_a_b_in, v_a_conv_w, v_a_conv_b, v_a_norm_g, v_a_norm_b, v_a_w_out, v_a_b_out, v_kv_w_down, v_kv_norm_g, v_kv_w_uk, v_kv_w_uv, v_b_w_in, v_b_q_norm_g, v_b_w_uq, v_b_w_out):
    w = dict(ln_g=ln_g, ln_b=ln_b, a_w_in=a_w_in, a_b_in=a_b_in, a_conv_w=a_conv_w, a_conv_b=a_conv_b,
             a_norm_g=a_norm_g, a_norm_b=a_norm_b, a_w_out=a_w_out, a_b_out=a_b_out, kv_w_down=kv_w_down,
             kv_norm_g=kv_norm_g, kv_w_uk=kv_w_uk, kv_w_uv=kv_w_uv, b_w_in=b_w_in, b_q_norm_g=b_q_norm_g,
             b_w_uq=b_w_uq, b_w_out=b_w_out)
    m = dict(ln_g=m_ln_g, ln_b=m_ln_b, a_w_in=m_a_w_in, a_b_in=m_a_b_in, a_conv_w=m_a_conv_w, a_conv_b=m_a_conv_b,
             a_norm_g=m_a_norm_g, a_norm_b=m_a_norm_b, a_w_out=m_a_w_out, a_b_out=m_a_b_out, kv_w_down=m_kv_w_down,
             kv_norm_g=m_kv_norm_g, kv_w_uk=m_kv_w_uk, kv_w_uv=m_kv_w_uv, b_w_in=m_b_w_in, b_q_norm_g=m_b_q_norm_g,
             b_w_uq=m_b_w_uq, b_w_out=m_b_w_out)
    v = dict(ln_g=v_ln_g, ln_b=v_ln_b, a_w_in=v_a_w_in, a_b_in=v_a_b_in, a_conv_w=v_a_conv_w, a_conv_b=v_a_conv_b,
             a_norm_g=v_a_norm_g, a_norm_b=v_a_norm_b, a_w_out=v_a_w_out, a_b_out=v_a_b_out, kv_w_down=v_kv_w_down,
             kv_norm_g=v_kv_norm_g, kv_w_uk=v_kv_w_uk, kv_w_uv=v_kv_w_uv, b_w_in=v_b_w_in, b_q_norm_g=v_b_q_norm_g,
             b_w_uq=v_b_w_uq, b_w_out=v_b_w_out)

    small_flat = jnp.concatenate([w[n].reshape(-1) for n in SMALL_WEIGHTS]).reshape(1, -1)
    ga = _all_gather("all_gather_first", [_shard_2d(w['a_w_in']).astype(BF16), small_flat])
    full = {'a_w_in': ga[0]}
    gs = ga[1].reshape(N_DEV, -1)
    off = 0
    for n in SMALL_WEIGHTS:
        size = math.prod(w[n].shape)
        full[n] = _gathered_to_full(n, gs[:, off:off + size].reshape((N_DEV,) + _shard_2d(w[n]).shape))
        off += size
    for n in REPLICATED:
        full[n] = w[n]
    rest_names = [n for n in MATMUL_WEIGHTS if n != 'a_w_in']
    group1 = ('b_w_out', 'b_w_uq', 'b_w_in', 'kv_w_uk', 'kv_w_uv', 'kv_w_down')
    group1b = ('a_w_out',)

    class Comm:
        def __init__(self):
            self.rest, self.rest_token = _comm_start(
                "gather_rest_start", [_shard_2d(w[n]).astype(BF16) for n in rest_names], True, ga[0])

        def first_after(self):
            return self.rest_token

        def rest_weights(self, after):
            lands = _comm_wait("gather_rest_wait", self.rest, True, after)
            return _prep_weights({n: _gathered_to_full(n, lands[i]) for i, n in enumerate(rest_names)})

        def send_group1(self, g):
            self.g1, token = _comm_start("exchange_g1_start", [g[n] for n in group1], False, None)
            return token

        def send_group1b(self, g):
            self.g1b, token = _comm_start("exchange_g1b_start", [g[n] for n in group1b], False, None)
            return token

        def send_group2(self, dw_lo, small):
            self.g2, token = _comm_start("exchange_g2_start", [dw_lo, _pack_small_grads(*small)], False, None)
            return token

        def send_group3(self, dw_hi):
            self.g3, token = _comm_start("exchange_g3_start", [dw_hi], False, None)
            return token

    comm = Comm()

    loss_loc, grad_x = _local_step(x[0], positions[0], loss_target[0], _prep_weights(full), comm)
    loss = lax.psum(loss_loc[0, 0], ("x", "y", "c"))

    res = {}
    recv1 = _comm_wait("exchange_g1_wait", comm.g1, False, grad_x)
    for i, n in enumerate(group1):
        res[n] = _adamw(n, recv1[i], _shard_2d(w[n]), _shard_2d(m[n]), _shard_2d(v[n]))
    recv1b = _comm_wait("exchange_g1b_wait", comm.g1b, False, res[group1[-1]][0])
    for i, n in enumerate(group1b):
        res[n] = _adamw(n, recv1b[i], _shard_2d(w[n]), _shard_2d(m[n]), _shard_2d(v[n]))
    recv2 = _comm_wait("exchange_g2_wait", comm.g2, False, res[group1b[-1]][0])
    recv3 = _comm_wait("exchange_g3_wait", comm.g3, False, recv2[0])
    res['a_w_in'] = _adamw('a_w_in', [recv2[0], recv3[0]], _shard_2d(w['a_w_in']), _shard_2d(m['a_w_in']),
                           _shard_2d(v['a_w_in']))
    two_d = lambda d: [_shard_2d(d[n]) if d[n].ndim > 1 else d[n].reshape(1, -1) for n in SMALL_NAMES]
    sg, sd, sm, sv = _adamw_small(recv2[-1], two_d(w), two_d(m), two_d(v))
    for i, n in enumerate(SMALL_NAMES):
        res[n] = (sg[i], sd[i], sm[i], sv[i])
    outs = [[res[n][j].reshape(w[n].shape) for n in WEIGHT_NAMES] for j in range(4)]
    return (loss, grad_x[None], *outs[0], *outs[1], *outs[2], *outs[3])
```

```python
import functools
import math

import jax
import jax.numpy as jnp
from jax import lax
from jax.experimental import pallas as pl
from jax.experimental.pallas import tpu as pltpu

F32 = jnp.float32
BF16 = jnp.bfloat16

D_MODEL = 1024
DEPTH = 2
CONV_WIDTH = 2 * D_MODEL
CONV_KERNEL = 31
N_HEADS = 16
QK_NOPE_DIM = 128
QK_ROPE_DIM = 64
V_HEAD_DIM = 128
KV_LORA_RANK = D_MODEL // 4
Q_LORA_RANK = D_MODEL // 2
ROPE_THETA = 10000.0
LN_EPS = 1e-5
RMS_EPS = 1e-6
MASK_VALUE = -1e30
ALPHA = (2.0 * DEPTH) ** 0.25
ATTN_SCALE = 1.0 / math.sqrt(QK_NOPE_DIM + QK_ROPE_DIM)

ADAM_LR = 0.001
ADAM_B1 = 0.9
ADAM_B2 = 0.999
ADAM_EPS = 1e-08
ADAM_WD = 0.01
ADAM_STEP = 10

N_DEV = 8
LANES = 128
HEAD_PAD = 256
HALO = 32
VMEM_LIMIT = 56 * 1024 * 1024

WEIGHT_NAMES = ['ln_g', 'ln_b', 'a_w_in', 'a_b_in', 'a_conv_w', 'a_conv_b', 'a_norm_g', 'a_norm_b',
                'a_w_out', 'a_b_out', 'kv_w_down', 'kv_norm_g', 'kv_w_uk', 'kv_w_uv', 'b_w_in',
                'b_q_norm_g', 'b_w_uq', 'b_w_out']
REPLICATED = ('ln_g', 'ln_b', 'kv_norm_g', 'b_q_norm_g')
MATMUL_WEIGHTS = ('a_w_in', 'a_w_out', 'kv_w_down', 'kv_w_uk', 'kv_w_uv', 'b_w_in', 'b_w_uq', 'b_w_out')
SMALL_WEIGHTS = ('a_b_in', 'a_conv_w', 'a_conv_b', 'a_norm_g', 'a_norm_b', 'a_b_out')

NN = (((1,), (0,)), ((), ()))
NT = (((1,), (1,)), ((), ()))
TN = (((0,), (0,)), ((), ()))


def _cparams(sem):
    return pltpu.CompilerParams(dimension_semantics=sem, vmem_limit_bytes=VMEM_LIMIT)


def _sigmoid(x):
    return 0.5 * jnp.tanh(0.5 * x) + 0.5


def _aligned(v, m):
    return v if isinstance(v, int) else pl.multiple_of(v, m)


def _fold_rows(x):
    parts = [x[r:r + 8] for r in range(0, x.shape[0], 8)]
    while len(parts) > 1:
        parts = [parts[i] + parts[i + 1] for i in range(0, len(parts) - 1, 2)] + ([parts[-1]] if len(parts) % 2 else [])
    return parts[0]


def _taps_by_residue(offset_of):
    groups = {}
    for j in range(CONV_KERNEL):
        off = offset_of(j)
        groups.setdefault(off % 8, []).append((j, off // 8))
    return sorted(groups.items())


def _swap_rope_halves(t):
    lane = lax.broadcasted_iota(jnp.int32, t.shape, 1)
    return jnp.where(lane < QK_ROPE_DIM // 2, pltpu.roll(t, LANES - QK_ROPE_DIM // 2, 1),
                     pltpu.roll(t, QK_ROPE_DIM // 2, 1))


def _matmul(a, b, *, name, bias=None, trans_a=False, trans_b=False, out_dtype=F32, bm=1024, bn=1024, bk=1024,
            b_blocked=False, out_blocked=False, after=None, a_cols=None):
    m_off = 0
    if trans_a:
        kdim, m = a.shape
        if a_cols is not None:
            m_off, m = a_cols
    else:
        m, kdim = a.shape
    if b_blocked and trans_b:
        n, k2 = b.shape[1], b.shape[0] * b.shape[2]
        bk = b.shape[2]
    elif b_blocked:
        k2, n = b.shape[1], b.shape[0] * b.shape[2]
        bn = b.shape[2]
    elif trans_b:
        n, k2 = b.shape
    else:
        k2, n = b.shape
    assert kdim == k2, (a.shape, b.shape)
    bm, bn, bk = min(bm, m), min(bn, n), min(bk, kdim)
    assert m % bm == 0 and n % bn == 0 and kdim % bk == 0, (name, m, n, kdim, bm, bn, bk)
    nk = kdim // bk
    dims = (((0 if trans_a else 1,), (1 if trans_b else 0,)), ((), ()))
    has_bias = bias is not None

    def body(*refs):
        refs = list(refs)
        a_ref, b_ref = refs[:2]
        del refs[:2]
        bias_ref = refs.pop(0) if has_bias else None
        if after is not None:
            refs.pop(0)
        o_ref = refs.pop(0)
        rest = refs
        prod = lax.dot_general(a_ref[...], b_ref[...], dims, preferred_element_type=F32)

        def finish(acc):
            if has_bias:
                acc = acc + bias_ref[...]
            o_ref[...] = acc.astype(out_dtype)

        if nk == 1:
            finish(prod)
        else:
            acc_ref = rest[0]
            k = pl.program_id(2)

            @pl.when(k == 0)
            def _():
                acc_ref[...] = prod

            @pl.when(k > 0)
            def _():
                acc_ref[...] += prod

            @pl.when(k == nk - 1)
            def _():
                finish(acc_ref[...])

    assert m_off % bm == 0
    a_first = m_off // bm
    a_spec = (pl.BlockSpec((bk, bm), lambda i, j, k: (k, i + a_first)) if trans_a
              else pl.BlockSpec((bm, bk), lambda i, j, k: (i, k)))
    if b_blocked and trans_b:
        b_spec = pl.BlockSpec((None, bn, bk), lambda i, j, k: (k, j, 0))
    elif b_blocked:
        b_spec = pl.BlockSpec((None, bk, bn), lambda i, j, k: (j, k, 0))
    elif trans_b:
        b_spec = pl.BlockSpec((bn, bk), lambda i, j, k: (j, k))
    else:
        b_spec = pl.BlockSpec((bk, bn), lambda i, j, k: (k, j))
    in_specs = [a_spec, b_spec]
    args = [a, b]
    if has_bias:
        in_specs.append(pl.BlockSpec((1, bn), lambda i, j, k: (0, j)))
        args.append(bias)
    if after is not None:
        in_specs.append(pl.BlockSpec(memory_space=pl.ANY))
        args.append(after)
    if out_blocked:
        out_struct = jax.ShapeDtypeStruct((n // bn, m, bn), out_dtype)
        out_spec = pl.BlockSpec((None, bm, bn), lambda i, j, k: (j, i, 0))
    else:
        out_struct = jax.ShapeDtypeStruct((m, n), out_dtype)
        out_spec = pl.BlockSpec((bm, bn), lambda i, j, k: (i, j))
    return pl.pallas_call(
        body, name=name,
        out_shape=out_struct,
        grid=(m // bm, n // bn, nk),
        in_specs=in_specs,
        out_specs=out_spec,
        scratch_shapes=[pltpu.VMEM((bm, bn), F32)] if nk > 1 else [],
        compiler_params=_cparams(("parallel", "parallel", "arbitrary")),
    )(*args)


def _rope_tables(pos_col, freq_row):
    t = pos_col.shape[0]
    tm = min(t, 512)

    def body(pos_ref, f_ref, c_ref, s_ref):
        ang = pos_ref[...].astype(F32) * f_ref[...]
        lane = lax.broadcasted_iota(jnp.int32, ang.shape, 1)
        cos = jnp.cos(ang)
        sin = jnp.sin(ang)
        c_ref[...] = jnp.where(lane < QK_ROPE_DIM, cos, 0.0)
        s_ref[...] = jnp.where(lane < QK_ROPE_DIM // 2, -sin, jnp.where(lane < QK_ROPE_DIM, sin, 0.0))

    return pl.pallas_call(
        body, name="rope_tables",
        out_shape=(jax.ShapeDtypeStruct((t, LANES), F32), jax.ShapeDtypeStruct((t, LANES), F32)),
        grid=(t // tm,),
        in_specs=[pl.BlockSpec((tm, 1), lambda i: (i, 0)), pl.BlockSpec((1, LANES), lambda i: (0, 0))],
        out_specs=(pl.BlockSpec((tm, LANES), lambda i: (i, 0)), pl.BlockSpec((tm, LANES), lambda i: (i, 0))),
        compiler_params=_cparams(("parallel",)),
    )(pos_col, freq_row)


def _conv_mid_fwd(proj, cw, cb, ng, nb):
    t = proj.shape[0]
    e = CONV_WIDTH
    tm = min(t, 128)
    hb = tm // HALO
    nchunk = e // LANES

    def body(val_ref, gg_ref, z_ref, valh_ref, ggh_ref, cw_ref, cb_ref, ng_ref, nb_ref, u1_ref, u2_ref, ext_ref, sh_ref):
        i = pl.program_id(0)
        u0 = val_ref[...] * _sigmoid(gg_ref[...])
        h0 = valh_ref[...] * _sigmoid(ggh_ref[...])
        ext_ref[0:HALO, :] = jnp.where(i > 0, h0, 0.0)
        ext_ref[HALO:, :] = u0

        def chunk(c, carry):
            col = pl.multiple_of(c * LANES, LANES)
            acc = jnp.zeros((tm, LANES), F32)
            for res, taps in _taps_by_residue(lambda j: HALO - (CONV_KERNEL - 1) + j):
                span = 8 * max(a for _, a in taps) + tm
                sh_ref[0:span, :] = ext_ref[res:res + span, pl.ds(col, LANES)]
                for j, a in taps:
                    acc = acc + cw_ref[j:j + 1, pl.ds(col, LANES)] * sh_ref[8 * a:8 * a + tm, :]
            u1_ref[:, pl.ds(col, LANES)] = acc + cb_ref[:, pl.ds(col, LANES)]
            return carry

        lax.fori_loop(0, nchunk, chunk, 0)
        g = ng_ref[...]
        b = nb_ref[...]
        for r in range(0, tm, 16):
            u1 = u1_ref[r:r + 16, :]
            mu = jnp.mean(u1, axis=-1, keepdims=True)
            xc = u1 - mu
            var = jnp.mean(xc * xc, axis=-1, keepdims=True)
            n = xc * lax.rsqrt(var + LN_EPS) * g + b
            z = z_ref[r:r + 16, :]
            u2_ref[r:r + 16, :] = ((n * _sigmoid(n)) * (z * _sigmoid(z))).astype(BF16)

    row = lambda c: pl.BlockSpec((tm, e), lambda i: (i, c))
    halo = lambda c: pl.BlockSpec((HALO, e), lambda i: (jnp.maximum(i * hb - 1, 0), c))
    par = lambda r: pl.BlockSpec((r, e), lambda i: (0, 0))
    return pl.pallas_call(
        body, name="conv_mid_fwd",
        out_shape=(jax.ShapeDtypeStruct((t, e), F32), jax.ShapeDtypeStruct((t, e), BF16)),
        grid=(t // tm,),
        in_specs=[row(0), row(1), row(2), halo(0), halo(1), par(HALO), par(1), par(1), par(1)],
        out_specs=(pl.BlockSpec((tm, e), lambda i: (i, 0)), pl.BlockSpec((tm, e), lambda i: (i, 0))),
        scratch_shapes=[pltpu.VMEM((tm + HALO, e), F32), pltpu.VMEM((tm + HALO, LANES), F32)],
        compiler_params=_cparams(("parallel",)),
    )(proj, proj, proj, proj, proj, cw, cb, ng, nb)


def _ln_res_fwd(x, y, g, b):
    t, d = x.shape
    tm = min(t, 256)

    def body(x_ref, y_ref, g_ref, b_ref, h_ref, hb_ref, xh_ref, rs_ref):
        r = ALPHA * x_ref[...] + y_ref[...]
        mu = jnp.mean(r, axis=-1, keepdims=True)
        xc = r - mu
        var = jnp.mean(xc * xc, axis=-1, keepdims=True)
        rstd = lax.rsqrt(var + LN_EPS)
        xh = xc * rstd
        h = xh * g_ref[...] + b_ref[...]
        h_ref[...] = h
        hb_ref[...] = h.astype(BF16)
        xh_ref[...] = xh
        rs_ref[...] = rstd

    row = pl.BlockSpec((tm, d), lambda i: (i, 0))
    par = pl.BlockSpec((1, d), lambda i: (0, 0))
    return pl.pallas_call(
        body, name="ln0_fwd",
        out_shape=(jax.ShapeDtypeStruct((t, d), F32), jax.ShapeDtypeStruct((t, d), BF16),
                   jax.ShapeDtypeStruct((t, d), F32), jax.ShapeDtypeStruct((t, 1), F32)),
        grid=(t // tm,),
        in_specs=[row, row, par, par],
        out_specs=(row, row, row, pl.BlockSpec((tm, 1), lambda i: (i, 0))),
        compiler_params=_cparams(("parallel",)),
    )(x, y, g, b)


def _kv_mid_fwd(ckv, g, rc, rs):
    t = ckv.shape[0]
    tm = min(t, 512)
    r = KV_LORA_RANK

    def body(ckv_ref, g_ref, c_ref, s_ref, ckn_ref, kr_ref):
        c = ckv_ref[:, 0:r]
        ms = jnp.mean(c * c, axis=-1, keepdims=True)
        ckn_ref[...] = (c * lax.rsqrt(ms + RMS_EPS) * g_ref[...]).astype(BF16)
        tr = ckv_ref[:, r:r + LANES]
        kr_ref[...] = (tr * c_ref[...] + _swap_rope_halves(tr) * s_ref[...]).astype(BF16)

    return pl.pallas_call(
        body, name="kv_mid_fwd",
        out_shape=(jax.ShapeDtypeStruct((t, r), BF16), jax.ShapeDtypeStruct((t, LANES), BF16)),
        grid=(t // tm,),
        in_specs=[pl.BlockSpec((tm, r + LANES), lambda i: (i, 0)), pl.BlockSpec((1, r), lambda i: (0, 0)),
                  pl.BlockSpec((tm, LANES), lambda i: (i, 0)), pl.BlockSpec((tm, LANES), lambda i: (i, 0))],
        out_specs=(pl.BlockSpec((tm, r), lambda i: (i, 0)), pl.BlockSpec((tm, LANES), lambda i: (i, 0))),
        compiler_params=_cparams(("parallel",)),
    )(ckv, g, rc, rs)


def _q_norm_fwd(projb, g):
    t = projb.shape[0]
    tm = min(t, 512)
    r = Q_LORA_RANK

    def body(c_ref, g_ref, o_ref):
        c = c_ref[...]
        ms = jnp.mean(c * c, axis=-1, keepdims=True)
        o_ref[...] = (c * lax.rsqrt(ms + RMS_EPS) * g_ref[...]).astype(BF16)

    return pl.pallas_call(
        body, name="q_norm_fwd",
        out_shape=jax.ShapeDtypeStruct((t, r), BF16),
        grid=(t // tm,),
        in_specs=[pl.BlockSpec((tm, r), lambda i: (i, 0)), pl.BlockSpec((1, r), lambda i: (0, 0))],
        out_specs=pl.BlockSpec((tm, r), lambda i: (i, 0)),
        compiler_params=_cparams(("parallel",)),
    )(projb, g)


def _q_up_rope(cqn, wuq, rc, rs):
    t, r = cqn.shape
    w = wuq.shape[1]
    tm = min(t, 1024)
    bn = 4 * HEAD_PAD

    def body(a_ref, b_ref, c_ref, s_ref, o_ref):
        q = lax.dot_general(a_ref[...], b_ref[...], NN, preferred_element_type=F32)
        c = c_ref[...]
        s = s_ref[...]
        for h in range(bn // HEAD_PAD):
            a = h * HEAD_PAD
            o_ref[:, a:a + LANES] = q[:, a:a + LANES].astype(BF16)
            tr = q[:, a + LANES:a + 2 * LANES]
            o_ref[:, a + LANES:a + 2 * LANES] = (tr * c + _swap_rope_halves(tr) * s).astype(BF16)

    return pl.pallas_call(
        body, name="q_up_rope",
        out_shape=jax.ShapeDtypeStruct((t, w), BF16),
        grid=(t // tm, w // bn),
        in_specs=[pl.BlockSpec((tm, r), lambda i, j: (i, 0)), pl.BlockSpec((r, bn), lambda i, j: (0, j)),
                  pl.BlockSpec((tm, LANES), lambda i, j: (i, 0)), pl.BlockSpec((tm, LANES), lambda i, j: (i, 0))],
        out_specs=pl.BlockSpec((tm, bn), lambda i, j: (i, j)),
        compiler_params=_cparams(("parallel", "parallel")),
    )(cqn, wuq, rc, rs)


def _flash_fwd(qcat, kv, kr, projb):
    t = qcat.shape[0]
    tq = min(t, 512)
    nq = t // tq
    exp2_scale = ATTN_SCALE * math.log2(math.e)
    z_first = Q_LORA_RANK // LANES

    def body(q_ref, kv_ref, kr_ref, z_ref, o_ref, o2_ref, lse_ref, kcat_ref):
        kcat_ref[:, 0:LANES] = kv_ref[:, 0:LANES]
        kcat_ref[:, LANES:] = kr_ref[...]
        rows = lax.broadcasted_iota(jnp.int32, (tq, tq), 0)
        cols = lax.broadcasted_iota(jnp.int32, (tq, tq), 1)
        for i in range(nq):
            a = i * tq
            q = q_ref[a:a + tq, :]
            sd = lax.dot_general(q, kcat_ref[a:a + tq, :], NT, preferred_element_type=F32)
            sd = jnp.where(cols <= rows, sd, MASK_VALUE)
            m = jnp.max(sd, axis=1, keepdims=True)
            if i > 0:
                sp = lax.dot_general(q, kcat_ref[0:a, :], NT, preferred_element_type=F32)
                m = jnp.maximum(m, jnp.max(sp, axis=1, keepdims=True))
            pd = jnp.exp2((sd - m) * exp2_scale)
            l = jnp.sum(pd, axis=1, keepdims=True)
            acc = lax.dot_general(pd.astype(BF16), kv_ref[a:a + tq, LANES:], NN, preferred_element_type=F32)
            if i > 0:
                pp = jnp.exp2((sp - m) * exp2_scale)
                l = l + jnp.sum(pp, axis=1, keepdims=True)
                acc = acc + lax.dot_general(pp.astype(BF16), kv_ref[0:a, LANES:], NN, preferred_element_type=F32)
            o = acc / l
            z = z_ref[a:a + tq, :]
            o_ref[a:a + tq, :] = o
            o2_ref[a:a + tq, :] = (o * (z * _sigmoid(z))).astype(BF16)
            lse_ref[a:a + tq, :] = jnp.broadcast_to(m * ATTN_SCALE + jnp.log(l), (tq, LANES))

    hw = N_HEADS * LANES
    head256 = pl.BlockSpec((t, HEAD_PAD), lambda h: (0, h))
    head128 = pl.BlockSpec((t, LANES), lambda h: (0, h))
    return pl.pallas_call(
        body, name="flash_fwd",
        out_shape=(jax.ShapeDtypeStruct((t, hw), F32), jax.ShapeDtypeStruct((t, hw), BF16),
                   jax.ShapeDtypeStruct((t, hw), F32)),
        grid=(N_HEADS,),
        in_specs=[head256, head256, pl.BlockSpec((t, LANES), lambda h: (0, 0), pipeline_mode=pl.Buffered(1)),
                  pl.BlockSpec((t, LANES), lambda h: (0, z_first + h))],
        out_specs=(head128, head128, head128),
        scratch_shapes=[pltpu.VMEM((t, HEAD_PAD), BF16)],
        compiler_params=_cparams(("parallel",)),
    )(qcat, kv, kr, projb)


def _final_ln_loss(h1, yb, g, b, target):
    t, d = h1.shape
    tm = min(t, 256)

    def body(h_ref, y_ref, g_ref, b_ref, t_ref, loss_ref, dr_ref, drb_ref, dg_ref, db_ref):
        i = pl.program_id(0)
        r = ALPHA * h_ref[...] + y_ref[...]
        mu = jnp.mean(r, axis=-1, keepdims=True)
        xc = r - mu
        var = jnp.mean(xc * xc, axis=-1, keepdims=True)
        rstd = lax.rsqrt(var + LN_EPS)
        xh = xc * rstd
        g = g_ref[...]
        diff = xh * g + b_ref[...] - t_ref[...]
        part = 0.5 * jnp.sum(jnp.mean(diff * diff, axis=-1, keepdims=True))
        dh = diff * (1.0 / d)
        dgp = jnp.sum(dh * xh, axis=0, keepdims=True)
        dbp = jnp.sum(dh, axis=0, keepdims=True)

        @pl.when(i == 0)
        def _():
            loss_ref[...] = jnp.zeros_like(loss_ref)
            dg_ref[...] = jnp.zeros_like(dg_ref)
            db_ref[...] = jnp.zeros_like(db_ref)

        loss_ref[...] += jnp.full(loss_ref.shape, part, F32)
        dg_ref[...] += dgp
        db_ref[...] += dbp
        dxh = dh * g
        dr = rstd * (dxh - jnp.mean(dxh, axis=-1, keepdims=True) - xh * jnp.mean(dxh * xh, axis=-1, keepdims=True))
        dr_ref[...] = dr
        drb_ref[...] = dr.astype(BF16)

    row = pl.BlockSpec((tm, d), lambda i: (i, 0))
    par = pl.BlockSpec((1, d), lambda i: (0, 0))
    return pl.pallas_call(
        body, name="final_ln_loss",
        out_shape=(jax.ShapeDtypeStruct((1, LANES), F32), jax.ShapeDtypeStruct((t, d), F32),
                   jax.ShapeDtypeStruct((t, d), BF16), jax.ShapeDtypeStruct((1, d), F32),
                   jax.ShapeDtypeStruct((1, d), F32)),
        grid=(t // tm,),
        in_specs=[row, row, par, par, row],
        out_specs=(pl.BlockSpec((1, LANES), lambda i: (0, 0)), row, row, par, par),
        compiler_params=_cparams(("arbitrary",)),
    )(h1, yb, g, b, target)


def _gate_bwd(do2, o, projb):
    t, hw = o.shape
    tm = min(t, 512)
    bw = Q_LORA_RANK
    nb = hw // bw
    wb = projb.shape[1]

    def body(d_ref, o_ref, z_ref, dob_ref, dz_ref, dl_ref):
        z = z_ref[...]
        sg = _sigmoid(z)
        d2 = d_ref[...]
        o = o_ref[...]
        do = d2 * (z * sg)
        dob_ref[...] = do.astype(BF16)
        dz_ref[...] = (d2 * o * (sg * (1.0 + z * (1.0 - sg)))).astype(BF16)
        prod = do * o
        for hh in range(bw // LANES):
            dsum = jnp.sum(prod[:, hh * LANES:(hh + 1) * LANES], axis=1, keepdims=True)
            dl_ref[:, hh * LANES:(hh + 1) * LANES] = jnp.broadcast_to(dsum, (tm, LANES))

    blk = pl.BlockSpec((tm, bw), lambda i, j: (i, j))
    blk1 = pl.BlockSpec((tm, bw), lambda i, j: (i, j + 1))
    return pl.pallas_call(
        body, name="gate_bwd",
        out_shape=(jax.ShapeDtypeStruct((t, hw), BF16), jax.ShapeDtypeStruct((t, wb), BF16),
                   jax.ShapeDtypeStruct((t, hw), F32)),
        grid=(t // tm, nb),
        in_specs=[blk, blk, blk1],
        out_specs=(blk, blk1, blk),
        compiler_params=_cparams(("parallel", "parallel")),
    )(do2, o, projb)


def _flash_bwd(qcat, kv, kr, dob, lse, dl, rc, rs):
    t = qcat.shape[0]
    tq = min(t, 512)
    nq = t // tq
    q_chunk = 2 * tq
    log2e = math.log2(math.e)
    exp2_scale = ATTN_SCALE * log2e
    once = pl.Buffered(1)

    def body(q_ref, kv_ref, kr_ref, do_ref, lse_ref, dl_ref, c_ref, s_ref,
             dq_ref, dkv_ref, dkr_ref, kcat_ref, dqa_ref, dka_ref, dva_ref):
        h = pl.program_id(0)
        kcat_ref[:, 0:LANES] = kv_ref[:, 0:LANES]
        kcat_ref[:, LANES:] = kr_ref[...]
        dqa_ref[...] = jnp.zeros_like(dqa_ref)

        @pl.when(h == 0)
        def _():
            dkr_ref[...] = jnp.zeros_like(dkr_ref)

        rows = lax.broadcasted_iota(jnp.int32, (tq, tq), 0)
        cols = lax.broadcasted_iota(jnp.int32, (tq, tq), 1)

        def block(qs, n, ks, masked):
            q = q_ref[qs:qs + n, :]
            kc = kcat_ref[ks:ks + tq, :]
            s = lax.dot_general(q, kc, NT, preferred_element_type=F32)
            if masked:
                s = jnp.where(cols <= rows, s, MASK_VALUE)
            p = jnp.exp2(s * exp2_scale - lse_ref[qs:qs + n, 0:1] * log2e)
            do = do_ref[qs:qs + n, :]
            dp = lax.dot_general(do, kv_ref[ks:ks + tq, LANES:], NT, preferred_element_type=F32)
            ds = (p * (dp - dl_ref[qs:qs + n, 0:1])).astype(BF16)
            dva_ref[...] += lax.dot_general(p.astype(BF16), do, TN, preferred_element_type=F32)
            dka_ref[...] += lax.dot_general(ds, q, TN, preferred_element_type=F32)
            dqa_ref[qs:qs + n, :] += lax.dot_general(ds, kc, NN, preferred_element_type=F32)

        for kb in range(nq):
            a = kb * tq
            dka_ref[...] = jnp.zeros_like(dka_ref)
            dva_ref[...] = jnp.zeros_like(dva_ref)
            block(a, tq, a, True)
            qs = a + tq
            while qs < t:
                n = min(q_chunk, t - qs)
                block(qs, n, a, False)
                qs += n
            dk = dka_ref[...] * ATTN_SCALE
            dkv_ref[a:a + tq, 0:LANES] = dk[:, 0:LANES].astype(BF16)
            dkv_ref[a:a + tq, LANES:] = dva_ref[...].astype(BF16)
            dkr_ref[a:a + tq, :] += dk[:, LANES:]

        dq = dqa_ref[...] * ATTN_SCALE
        dq_ref[:, 0:LANES] = dq[:, 0:LANES].astype(BF16)
        d2 = dq[:, LANES:]
        dq_ref[:, LANES:] = (d2 * c_ref[...] - _swap_rope_halves(d2) * s_ref[...]).astype(BF16)

    hp = N_HEADS * HEAD_PAD
    head256 = pl.BlockSpec((t, HEAD_PAD), lambda h: (0, h))
    head128 = pl.BlockSpec((t, LANES), lambda h: (0, h))
    const128 = pl.BlockSpec((t, LANES), lambda h: (0, 0), pipeline_mode=once)
    return pl.pallas_call(
        body, name="flash_bwd",
        out_shape=(jax.ShapeDtypeStruct((t, hp), BF16), jax.ShapeDtypeStruct((t, hp), BF16),
                   jax.ShapeDtypeStruct((t, LANES), F32)),
        grid=(N_HEADS,),
        in_specs=[head256, head256, const128, head128, head128, head128, const128, const128],
        out_specs=(head256, head256, pl.BlockSpec((t, LANES), lambda h: (0, 0))),
        scratch_shapes=[pltpu.VMEM((t, HEAD_PAD), BF16), pltpu.VMEM((t, HEAD_PAD), F32),
                        pltpu.VMEM((tq, HEAD_PAD), F32), pltpu.VMEM((tq, LANES), F32)],
        compiler_params=_cparams(("arbitrary",)),
    )(qcat, kv, kr, dob, lse, dl, rc, rs)


def _q_norm_bwd(dcqn, projb, g, dprojb):
    t = projb.shape[0]
    tm = min(t, 512)
    r = Q_LORA_RANK

    def body(d_ref, c_ref, g_ref, alias_ref, o_ref, dg_ref):
        i = pl.program_id(0)
        c = c_ref[...]
        d = d_ref[...]
        rr = lax.rsqrt(jnp.mean(c * c, axis=-1, keepdims=True) + RMS_EPS)
        xh = c * rr

        @pl.when(i == 0)
        def _():
            dg_ref[...] = jnp.zeros_like(dg_ref)

        dg_ref[...] += jnp.sum(d * xh, axis=0, keepdims=True)
        dxh = d * g_ref[...]
        o_ref[...] = (rr * (dxh - xh * jnp.mean(dxh * xh, axis=-1, keepdims=True))).astype(BF16)

    blk = pl.BlockSpec((tm, r), lambda i: (i, 0))
    par = pl.BlockSpec((1, r), lambda i: (0, 0))
    return pl.pallas_call(
        body, name="q_norm_bwd",
        out_shape=(jax.ShapeDtypeStruct(dprojb.shape, BF16), jax.ShapeDtypeStruct((1, r), F32)),
        grid=(t // tm,),
        in_specs=[blk, blk, par, pl.BlockSpec(memory_space=pl.ANY)],
        out_specs=(blk, par),
        input_output_aliases={3: 0},
        compiler_params=_cparams(("arbitrary",)),
    )(dcqn, projb, g, dprojb)


def _kv_mid_bwd(dckn, ckv, dkr, g, rc, rs):
    t = ckv.shape[0]
    tm = min(t, 512)
    r = KV_LORA_RANK

    def body(d_ref, ckv_ref, dkr_ref, g_ref, c_ref, s_ref, o_ref, dg_ref):
        i = pl.program_id(0)
        c = ckv_ref[:, 0:r]
        d = d_ref[...]
        rr = lax.rsqrt(jnp.mean(c * c, axis=-1, keepdims=True) + RMS_EPS)
        xh = c * rr

        @pl.when(i == 0)
        def _():
            dg_ref[...] = jnp.zeros_like(dg_ref)

        dg_ref[...] += jnp.sum(d * xh, axis=0, keepdims=True)
        dxh = d * g_ref[...]
        o_ref[:, 0:r] = (rr * (dxh - xh * jnp.mean(dxh * xh, axis=-1, keepdims=True))).astype(BF16)
        dk = dkr_ref[...]
        o_ref[:, r:] = (dk * c_ref[...] - _swap_rope_halves(dk) * s_ref[...]).astype(BF16)

    return pl.pallas_call(
        body, name="kv_mid_bwd",
        out_shape=(jax.ShapeDtypeStruct((t, r + LANES), BF16), jax.ShapeDtypeStruct((1, r), F32)),
        grid=(t // tm,),
        in_specs=[pl.BlockSpec((tm, r), lambda i: (i, 0)), pl.BlockSpec((tm, r + LANES), lambda i: (i, 0)),
                  pl.BlockSpec((tm, LANES), lambda i: (i, 0)), pl.BlockSpec((1, r), lambda i: (0, 0)),
                  pl.BlockSpec((tm, LANES), lambda i: (i, 0)), pl.BlockSpec((tm, LANES), lambda i: (i, 0))],
        out_specs=(pl.BlockSpec((tm, r + LANES), lambda i: (i, 0)), pl.BlockSpec((1, r), lambda i: (0, 0))),
        compiler_params=_cparams(("arbitrary",)),
    )(dckn, ckv, dkr, g, rc, rs)


def _ln0_bwd(dr2, t1, t2, xh, rstd, g):
    t, d = dr2.shape
    tm = min(t, 256)

    def body(a_ref, b_ref, c_ref, xh_ref, rs_ref, g_ref, dr_ref, drb_ref, dg_ref, db_ref, dsum_ref):
        i = pl.program_id(0)
        dh = ALPHA * a_ref[...] + b_ref[...] + c_ref[...]
        xh = xh_ref[...]

        @pl.when(i == 0)
        def _():
            dg_ref[...] = jnp.zeros_like(dg_ref)
            db_ref[...] = jnp.zeros_like(db_ref)
            dsum_ref[...] = jnp.zeros_like(dsum_ref)

        dg_ref[...] += jnp.sum(dh * xh, axis=0, keepdims=True)
        db_ref[...] += jnp.sum(dh, axis=0, keepdims=True)
        dxh = dh * g_ref[...]
        dr = rs_ref[...] * (dxh - jnp.mean(dxh, axis=-1, keepdims=True) - xh * jnp.mean(dxh * xh, axis=-1, keepdims=True))
        dr_ref[...] = dr
        drb_ref[...] = dr.astype(BF16)
        dsum_ref[...] += jnp.sum(dr, axis=0, keepdims=True)

    row = pl.BlockSpec((tm, d), lambda i: (i, 0))
    par = pl.BlockSpec((1, d), lambda i: (0, 0))
    return pl.pallas_call(
        body, name="ln0_bwd",
        out_shape=(jax.ShapeDtypeStruct((t, d), F32), jax.ShapeDtypeStruct((t, d), BF16),
                   jax.ShapeDtypeStruct((1, d), F32), jax.ShapeDtypeStruct((1, d), F32),
                   jax.ShapeDtypeStruct((1, d), F32)),
        grid=(t // tm,),
        in_specs=[row, row, row, row, pl.BlockSpec((tm, 1), lambda i: (i, 0)), par],
        out_specs=(row, row, par, par, par),
        compiler_params=_cparams(("arbitrary",)),
    )(dr2, t1, t2, xh, rstd, g)


def _conv_mid_bwd_rows(u1, proj, du2, ng, nb):
    t = u1.shape[0]
    e = CONV_WIDTH
    tm = min(t, 256)

    rg = 16
    nt = t // tm

    def body(u1_ref, z_ref, d_ref, ng_ref, nb_ref, du1_ref, dz_ref, dng_ref, dnb_ref, dbz_ref, acc_ref):
        i = pl.program_id(0)

        @pl.when(i == 0)
        def _():
            acc_ref[...] = jnp.zeros_like(acc_ref)

        g = ng_ref[...]
        b = nb_ref[...]

        def group(r, carry):
            rows = pl.ds(pl.multiple_of(r * rg, rg), rg)
            u1 = u1_ref[rows, :]
            mu = jnp.mean(u1, axis=-1, keepdims=True)
            xc = u1 - mu
            rstd = lax.rsqrt(jnp.mean(xc * xc, axis=-1, keepdims=True) + LN_EPS)
            xh = xc * rstd
            n = xh * g + b
            sn = _sigmoid(n)
            z = z_ref[rows, :]
            sz = _sigmoid(z)
            du2 = d_ref[rows, :]
            dz = du2 * (n * sn) * (sz * (1.0 + z * (1.0 - sz)))
            dn = du2 * (z * sz) * (sn * (1.0 + n * (1.0 - sn)))
            acc_ref[0:8, :] += _fold_rows(dn * xh)
            acc_ref[8:16, :] += _fold_rows(dn)
            acc_ref[16:24, :] += _fold_rows(dz)
            dz_ref[rows, :] = dz.astype(BF16)
            dxh = dn * g
            du1_ref[rows, :] = rstd * (dxh - jnp.mean(dxh, axis=-1, keepdims=True)
                                       - xh * jnp.mean(dxh * xh, axis=-1, keepdims=True))
            return carry

        lax.fori_loop(0, tm // rg, group, 0, unroll=8)

        @pl.when(i == nt - 1)
        def _():
            dng_ref[...] = jnp.sum(acc_ref[0:8, :], axis=0, keepdims=True)
            dnb_ref[...] = jnp.sum(acc_ref[8:16, :], axis=0, keepdims=True)
            dbz_ref[...] = jnp.sum(acc_ref[16:24, :], axis=0, keepdims=True)

    row = pl.BlockSpec((tm, e), lambda i: (i, 0))
    par = pl.BlockSpec((1, e), lambda i: (0, 0))
    return pl.pallas_call(
        body, name="conv_mid_bwd_rows",
        out_shape=(jax.ShapeDtypeStruct((t, e), F32), jax.ShapeDtypeStruct((t, 3 * e), BF16),
                   jax.ShapeDtypeStruct((1, e), F32), jax.ShapeDtypeStruct((1, e), F32),
                   jax.ShapeDtypeStruct((1, e), F32)),
        grid=(nt,),
        in_specs=[row, pl.BlockSpec((tm, e), lambda i: (i, 2)), row, par, par],
        out_specs=(row, pl.BlockSpec((tm, e), lambda i: (i, 2)), par, par, par),
        scratch_shapes=[pltpu.VMEM((24, e), F32)],
        compiler_params=_cparams(("arbitrary",)),
    )(u1, proj, du2, ng, nb)


def _conv_bwd(du1, proj, cw, dproj):
    t = du1.shape[0]
    e = CONV_WIDTH
    tm = min(t, 128)
    hb = tm // HALO
    nt = t // tm
    nchunk = e // LANES
    last_halo = t // HALO - 1

    def body(d_ref, dn_ref, val_ref, gg_ref, valh_ref, ggh_ref, cw_ref, alias_ref,
             dp_ref, dcw_ref, dcb_ref, dbv_ref, extu_ref, extd_ref, du0_ref, acc_ref, sh_ref):
        i = pl.program_id(0)
        h0 = valh_ref[...] * _sigmoid(ggh_ref[...])
        extu_ref[0:HALO, :] = jnp.where(i > 0, h0, 0.0)
        extu_ref[HALO:, :] = val_ref[...] * _sigmoid(gg_ref[...])
        extd_ref[0:tm, :] = d_ref[...]
        extd_ref[tm:, :] = jnp.where(i < nt - 1, dn_ref[...], 0.0)

        @pl.when(i == 0)
        def _():
            acc_ref[...] = jnp.zeros_like(acc_ref)

        def chunk(c, carry):
            col = pl.multiple_of(c * LANES, LANES)
            d = extd_ref[0:tm, pl.ds(col, LANES)]
            du0 = jnp.zeros((tm, LANES), F32)
            for res, taps in _taps_by_residue(lambda j: CONV_KERNEL - 1 - j):
                span = 8 * max(a for _, a in taps) + tm
                sh_ref[0:span, :] = extd_ref[res:res + span, pl.ds(col, LANES)]
                for j, a in taps:
                    du0 = du0 + cw_ref[j:j + 1, pl.ds(col, LANES)] * sh_ref[8 * a:8 * a + tm, :]
            for res, taps in _taps_by_residue(lambda j: HALO - (CONV_KERNEL - 1) + j):
                span = 8 * max(a for _, a in taps) + tm
                sh_ref[0:span, :] = extu_ref[res:res + span, pl.ds(col, LANES)]
                for j, a in taps:
                    prod = d * sh_ref[8 * a:8 * a + tm, :]
                    acc_ref[j * 8:(j + 1) * 8, pl.ds(col, LANES)] += _fold_rows(prod)
            acc_ref[CONV_KERNEL * 8:(CONV_KERNEL + 1) * 8, pl.ds(col, LANES)] += _fold_rows(d)
            du0_ref[:, pl.ds(col, LANES)] = du0
            return carry

        lax.fori_loop(0, nchunk, chunk, 0)
        kb = CONV_KERNEL * 8
        for r in range(0, tm, 16):
            sg = _sigmoid(gg_ref[r:r + 16, :])
            dval = du0_ref[r:r + 16, :] * sg
            dgg = dval * val_ref[r:r + 16, :] * (1.0 - sg)
            dp_ref[r:r + 16, 0:e] = dval.astype(BF16)
            dp_ref[r:r + 16, e:] = dgg.astype(BF16)
            acc_ref[kb + 8:kb + 16, :] += _fold_rows(dval)
            acc_ref[kb + 16:kb + 24, :] += _fold_rows(dgg)

        @pl.when(i == nt - 1)
        def _():
            for j in range(CONV_KERNEL):
                dcw_ref[j:j + 1, :] = jnp.sum(acc_ref[j * 8:(j + 1) * 8, :], axis=0, keepdims=True)
            dcw_ref[CONV_KERNEL:, :] = jnp.zeros((HALO - CONV_KERNEL, e), F32)
            dcb_ref[...] = jnp.sum(acc_ref[kb:kb + 8, :], axis=0, keepdims=True)
            dbv_ref[:, 0:e] = jnp.sum(acc_ref[kb + 8:kb + 16, :], axis=0, keepdims=True)
            dbv_ref[:, e:] = jnp.sum(acc_ref[kb + 16:kb + 24, :], axis=0, keepdims=True)

    row = lambda c: pl.BlockSpec((tm, e), lambda i: (i, c))
    halo_prev = lambda c: pl.BlockSpec((HALO, e), lambda i: (jnp.maximum(i * hb - 1, 0), c))
    halo_next = pl.BlockSpec((HALO, e), lambda i: (jnp.minimum((i + 1) * hb, last_halo), 0))
    return pl.pallas_call(
        body, name="conv_bwd",
        out_shape=(jax.ShapeDtypeStruct(dproj.shape, BF16), jax.ShapeDtypeStruct((HALO, e), F32),
                   jax.ShapeDtypeStruct((1, e), F32), jax.ShapeDtypeStruct((1, 2 * e), F32)),
        grid=(nt,),
        in_specs=[row(0), halo_next, row(0), row(1), halo_prev(0), halo_prev(1),
                  pl.BlockSpec((HALO, e), lambda i: (0, 0)), pl.BlockSpec(memory_space=pl.ANY)],
        out_specs=(pl.BlockSpec((tm, 2 * e), lambda i: (i, 0)), pl.BlockSpec((HALO, e), lambda i: (0, 0)),
                   pl.BlockSpec((1, e), lambda i: (0, 0)), pl.BlockSpec((1, 2 * e), lambda i: (0, 0))),
        scratch_shapes=[pltpu.VMEM((tm + HALO, e), F32), pltpu.VMEM((tm + HALO, e), F32),
                        pltpu.VMEM((tm, e), F32), pltpu.VMEM(((CONV_KERNEL + 3) * 8, e), F32),
                        pltpu.VMEM((tm + HALO, LANES), F32)],
        input_output_aliases={7: 0},
        compiler_params=_cparams(("arbitrary",)),
    )(du1, du1, proj, proj, proj, proj, cw, dproj)


def _adam_update(g, w, m, v):
    c1 = 1.0 - ADAM_B1 ** ADAM_STEP
    c2 = 1.0 - ADAM_B2 ** ADAM_STEP
    mn = ADAM_B1 * m + (1.0 - ADAM_B1) * g
    vn = ADAM_B2 * v + (1.0 - ADAM_B2) * (g * g)
    delta = -ADAM_LR * ((mn / c1) / (jnp.sqrt(vn / c2) + ADAM_EPS) + ADAM_WD * w)
    return delta, mn, vn


def _adamw(name, gbuf, w, m, v):
    parts = list(gbuf) if isinstance(gbuf, (list, tuple)) else [gbuf]
    npart = len(parts)
    r, c = w.shape
    tr = min(r // npart, 256)
    per_part = r // npart // tr
    assert r == npart * per_part * tr and all(p.shape == (N_DEV, r // npart, c) for p in parts)

    def body(*refs):
        g_refs = refs[:npart]
        w_ref, m_ref, v_ref, go_ref, d_ref, mo_ref, vo_ref = refs[npart:]
        i = pl.program_id(0)

        def run(g_ref):
            g = g_ref[0].astype(F32)
            for k in range(1, N_DEV):
                g = g + g_ref[k].astype(F32)
            go_ref[...] = g
            d_ref[...], mo_ref[...], vo_ref[...] = _adam_update(g, w_ref[...], m_ref[...], v_ref[...])

        if npart == 1:
            run(g_refs[0])
        else:
            for q in range(npart):
                pl.when((i >= q * per_part) & (i < (q + 1) * per_part))(functools.partial(run, g_refs[q]))

    blk = pl.BlockSpec((tr, c), lambda i: (i, 0))
    part_spec = lambda q: pl.BlockSpec((N_DEV, tr, c), lambda i: (0, jnp.clip(i - q * per_part, 0, per_part - 1), 0))
    shp = jax.ShapeDtypeStruct((r, c), F32)
    return pl.pallas_call(
        body, name="adamw_" + name,
        out_shape=(shp, shp, shp, shp),
        grid=(r // tr,),
        in_specs=[part_spec(q) for q in range(npart)] + [blk, blk, blk],
        out_specs=(blk, blk, blk, blk),
        compiler_params=_cparams(("parallel",)),
    )(*parts, w, m, v)


SMALL_TABLE_COLS = 256
SMALL_LAYOUT = {
    'ln_g': (0, 2, 1024), 'ln_b': (8, 2, 1024), 'a_b_in': (16, 1, 768), 'a_conv_b': (24, 1, 256),
    'a_norm_g': (32, 1, 256), 'a_norm_b': (40, 1, 256), 'a_b_out': (48, 1, 128), 'kv_norm_g': (56, 1, 256),
    'b_q_norm_g': (64, 1, 512), 'a_conv_w': (72, CONV_KERNEL, 256),
}
LOSS_ROW = 104
SMALL_TABLE_ROWS = 112
SMALL_NAMES = tuple(SMALL_LAYOUT)


def _small_pieces(name, row):
    r0, _, c = SMALL_LAYOUT[name]
    w = min(c, SMALL_TABLE_COLS)
    per_row = c // w
    return [(r0 + row * per_row + q, q * w, w) for q in range(per_row)]


def _pack_small_grads(dg0, dg1, db0, db1, dbv, dbz, dcb, dng, dnb, dba, dgkv, dgq, dcw, loss):
    e = CONV_WIDTH
    ce = e // N_DEV

    def body(dg0_ref, dg1_ref, db0_ref, db1_ref, dbv_ref, dbz_ref, dcb_ref, dng_ref, dnb_ref, dba_ref, dgkv_ref,
             dgq_ref, dcw_ref, loss_ref, o_ref):
        o_ref[...] = jnp.zeros_like(o_ref)
        for d in range(N_DEV):
            o_ref[d, LOSS_ROW:LOSS_ROW + 1, 0:LANES] = loss_ref[...]

        def put(d, name, row, src_ref, first_col=0):
            for trow, col, w in _small_pieces(name, row):
                o_ref[d, trow:trow + 1, 0:w] = src_ref[:, first_col + col:first_col + col + w]

        for d in range(N_DEV):
            put(d, 'ln_g', 0, dg0_ref)
            put(d, 'ln_g', 1, dg1_ref)
            put(d, 'ln_b', 0, db0_ref)
            put(d, 'ln_b', 1, db1_ref)
            for trow, col, w in _small_pieces('a_b_in', 0):
                gcol = d * 3 * ce + col
                src = dbv_ref[:, gcol:gcol + w] if gcol < 2 * e else dbz_ref[:, gcol - 2 * e:gcol - 2 * e + w]
                o_ref[d, trow:trow + 1, 0:w] = src
            put(d, 'a_conv_b', 0, dcb_ref, d * ce)
            put(d, 'a_norm_g', 0, dng_ref, d * ce)
            put(d, 'a_norm_b', 0, dnb_ref, d * ce)
            put(d, 'a_b_out', 0, dba_ref, d * LANES)
            put(d, 'kv_norm_g', 0, dgkv_ref)
            put(d, 'b_q_norm_g', 0, dgq_ref)
            r0 = SMALL_LAYOUT['a_conv_w'][0]
            o_ref[d, r0:r0 + HALO, 0:ce] = dcw_ref[:, d * ce:(d + 1) * ce]

    return pl.pallas_call(
        body, name="pack_small_grads",
        out_shape=jax.ShapeDtypeStruct((N_DEV, SMALL_TABLE_ROWS, SMALL_TABLE_COLS), F32),
        compiler_params=pltpu.CompilerParams(vmem_limit_bytes=VMEM_LIMIT),
    )(dg0, dg1, db0, db1, dbv, dbz, dcb, dng, dnb, dba, dgkv, dgq, dcw, loss)


def _adamw_small(gtab, ws, ms, vs):
    n = len(SMALL_NAMES)

    def body(*refs):
        g_ref = refs[0]
        w_refs, m_refs, v_refs = refs[1:1 + n], refs[1 + n:1 + 2 * n], refs[1 + 2 * n:1 + 3 * n]
        outs = refs[1 + 3 * n:]

        def summed(r0, r, c):
            g = g_ref[0, r0:r0 + r, 0:c]
            for k in range(1, N_DEV):
                g = g + g_ref[k, r0:r0 + r, 0:c]
            return g

        for i, name in enumerate(SMALL_NAMES):
            r0, r, c = SMALL_LAYOUT[name]
            if c <= SMALL_TABLE_COLS:
                blocks = [(slice(None), summed(r0, r, c))]
            else:
                blocks = [(slice(row, row + 1), jnp.concatenate([summed(tr, 1, w) for tr, _, w in _small_pieces(name, row)], axis=1))
                          for row in range(r)]
            for rows, g in blocks:
                delta, mn, vn = _adam_update(g, w_refs[i][rows, :], m_refs[i][rows, :], v_refs[i][rows, :])
                outs[i][rows, :] = g
                outs[n + i][rows, :] = delta
                outs[2 * n + i][rows, :] = mn
                outs[3 * n + i][rows, :] = vn
        outs[4 * n][...] = summed(LOSS_ROW, 1, LANES)

    shapes = tuple(jax.ShapeDtypeStruct(w.shape, F32) for w in ws)
    res = pl.pallas_call(
        body, name="adamw_small",
        out_shape=shapes * 4 + (jax.ShapeDtypeStruct((1, LANES), F32),),
        compiler_params=pltpu.CompilerParams(vmem_limit_bytes=VMEM_LIMIT),
    )(gtab, *ws, *ms, *vs)
    return res[:n], res[n:2 * n], res[2 * n:3 * n], res[3 * n:4 * n], res[4 * n]


def _mesh_peers():
    x, y, c = lax.axis_index("x"), lax.axis_index("y"), lax.axis_index("c")
    me = 4 * x + 2 * y + c
    peers = []
    for k in range(1, N_DEV):
        px = 1 - x if (k >> 2) & 1 else x
        py = 1 - y if (k >> 1) & 1 else y
        pc = 1 - c if k & 1 else c
        peers.append(((px, py, pc), 4 * px + 2 * py + pc))
    return me, peers


def _all_gather(name, shards):
    na = len(shards)

    def body(*refs):
        srcs, dsts = refs[:na], refs[na:2 * na]
        send_sems, recv_sems, local_sems = refs[2 * na:]
        x, y, c = lax.axis_index("x"), lax.axis_index("y"), lax.axis_index("c")
        me, sibling = (x, y, c), (x, y, 1 - c)
        chips = [(1 - x, y), (x, 1 - y), (1 - x, 1 - y)]
        slot = lambda dev: 4 * dev[0] + 2 * dev[1] + dev[2]

        def copy(a, k, block, to, own=False):
            return pltpu.make_async_remote_copy(
                src_ref=srcs[a] if own else dsts[a].at[slot(block)], dst_ref=dsts[a].at[slot(block)],
                send_sem=send_sems.at[a, k], recv_sem=recv_sems.at[a, k], device_id=to,
                device_id_type=pl.DeviceIdType.MESH)

        local = [pltpu.make_async_copy(srcs[a], dsts[a].at[slot(me)], local_sems.at[a]) for a in range(na)]
        for cp in local:
            cp.start()
        sends = []
        for a in range(na):
            sends.append(copy(a, 0, me, sibling, own=True))
            sends += [copy(a, 1 + j, me, (*chip, c), own=True) for j, chip in enumerate(chips)]
        for cp in sends:
            cp.start()
        for j, chip in enumerate(chips):
            for a in range(na):
                copy(a, 1 + j, (*chip, c), me).wait_recv()
                fwd = copy(a, 4 + j, (*chip, c), sibling)
                fwd.start()
                sends.append(fwd)
        for a in range(na):
            copy(a, 0, sibling, me).wait_recv()
            for j, chip in enumerate(chips):
                copy(a, 4 + j, (*chip, 1 - c), me).wait_recv()
        for cp in sends:
            cp.wait_send()
        for cp in local:
            cp.wait()

    hbm = pl.BlockSpec(memory_space=pl.ANY)
    return pl.pallas_call(
        body, name=name,
        out_shape=tuple(jax.ShapeDtypeStruct((N_DEV,) + s.shape, s.dtype) for s in shards),
        in_specs=[hbm] * na, out_specs=(hbm,) * na,
        scratch_shapes=[pltpu.SemaphoreType.DMA((na, N_DEV - 1)), pltpu.SemaphoreType.DMA((na, N_DEV - 1)),
                        pltpu.SemaphoreType.DMA((na,))],
    )(*shards)


_HBM_SPEC = pl.BlockSpec(memory_space=pltpu.HBM)
_SEM_SPEC = pl.BlockSpec(memory_space=pltpu.SEMAPHORE)


def _split_copies(srcs, lands, send_sems, recv_sems, local_sems, gather):
    me, peers = _mesh_peers()
    na = len(srcs)
    mine = lambda a, slot: srcs[a] if gather else srcs[a].at[slot]
    local = [pltpu.make_async_copy(mine(a, me), lands[a].at[me], local_sems.at[a]) for a in range(na)]
    sent, received = [], []
    for k, (dev, pid) in enumerate(peers):
        for a in range(na):
            s = a * (N_DEV - 1) + k
            sent.append(pltpu.make_async_remote_copy(
                src_ref=mine(a, pid), dst_ref=lands[a].at[me], send_sem=send_sems.at[s],
                recv_sem=recv_sems.at[s], device_id=dev, device_id_type=pl.DeviceIdType.MESH))
            received.append(pltpu.make_async_remote_copy(
                src_ref=mine(a, pid), dst_ref=lands[a].at[pid], send_sem=send_sems.at[s],
                recv_sem=recv_sems.at[s], device_id=dev, device_id_type=pl.DeviceIdType.MESH))
    return local, sent, received


def _comm_start(name, srcs, gather, after):
    na = len(srcs)
    land_structs = [jax.ShapeDtypeStruct(((N_DEV,) + s.shape) if gather else s.shape, s.dtype) for s in srcs]
    extra = [] if after is None else [after]
    n_in = 2 * na + len(extra)

    def body(*refs):
        srcs_r, lands_r = refs[:na], refs[na:2 * na]
        send_sems, recv_sems, local_sems = refs[n_in:n_in + 3]
        token = refs[-1]
        local, sent, _ = _split_copies(srcs_r, lands_r, send_sems, recv_sems, local_sems, gather)
        for cp in local + sent:
            cp.start()
        token[...] = jnp.zeros_like(token)

    sem = pltpu.SemaphoreType.DMA((na * (N_DEV - 1),))
    outs = pl.pallas_call(
        body, name=name,
        out_shape=(sem, sem, pltpu.SemaphoreType.DMA((na,)),
                   *[pltpu.HBM(s.shape, s.dtype) for s in srcs],
                   *[pltpu.HBM(s.shape, s.dtype) for s in land_structs],
                   jax.ShapeDtypeStruct((8, LANES), F32)),
        in_specs=[_HBM_SPEC] * (2 * na) + [pl.BlockSpec(memory_space=pl.ANY)] * len(extra),
        out_specs=(_SEM_SPEC, _SEM_SPEC, _SEM_SPEC, *([_HBM_SPEC] * (2 * na)), pl.BlockSpec(memory_space=pltpu.VMEM)),
        input_output_aliases={i: 3 + i for i in range(2 * na)},
        compiler_params=pltpu.CompilerParams(has_side_effects=pltpu.SideEffectType.DATAFLOW_SIDE_EFFECTING),
    )(*[pltpu.with_memory_space_constraint(s, pltpu.HBM) for s in srcs],
      *[pltpu.with_memory_space_constraint(lax.empty(s.shape, s.dtype), pltpu.HBM) for s in land_structs],
      *extra)
    return (outs[:3], outs[3:3 + na], outs[3 + na:3 + 2 * na]), outs[-1]


def _comm_wait(name, handles, gather, after):
    (send_sems, recv_sems, local_sems), srcs, lands = handles
    na = len(srcs)

    def body(*refs):
        srcs_r, lands_r = refs[:na], refs[na:2 * na]
        send_r, recv_r, local_r = refs[2 * na:2 * na + 3]
        local, sent, received = _split_copies(srcs_r, lands_r, send_r, recv_r, local_r, gather)
        for cp in sent:
            cp.wait_send()
        for cp in received:
            cp.wait_recv()
        for cp in local:
            cp.wait()

    outs = pl.pallas_call(
        body, name=name,
        out_shape=(*[pltpu.HBM(s.shape, s.dtype) for s in srcs], *[pltpu.HBM(s.shape, s.dtype) for s in lands]),
        in_specs=[_HBM_SPEC] * (2 * na) + [_SEM_SPEC] * 3 + [pl.BlockSpec(memory_space=pl.ANY)],
        out_specs=tuple([_HBM_SPEC] * (2 * na)),
        input_output_aliases={i: i for i in range(2 * na)},
        compiler_params=pltpu.CompilerParams(has_side_effects=pltpu.SideEffectType.DATAFLOW_SIDE_EFFECTING),
    )(*srcs, *lands, send_sems, recv_sems, local_sems, after)
    return outs[na:]


def _prep_weights(w):
    e = CONV_WIDTH
    p = {}
    if 'a_w_in' in w:
        if w['a_w_in'].shape == (N_DEV, D_MODEL, 3 * e // N_DEV):
            p['a_w_in3'] = w['a_w_in']
        else:
            p['a_w_in3'] = w['a_w_in'].reshape(D_MODEL, N_DEV, 3 * e // N_DEV).transpose(1, 0, 2)
        p['a_b_in'] = w['a_b_in'].reshape(1, 3 * e)
        p['a_conv_w'] = jnp.pad(w['a_conv_w'].reshape(CONV_KERNEL, e), ((0, HALO - CONV_KERNEL), (0, 0)))
        p['a_conv_b'] = w['a_conv_b'].reshape(1, e)
        p['a_norm_g'] = w['a_norm_g'].reshape(1, e)
        p['a_norm_b'] = w['a_norm_b'].reshape(1, e)
        p['a_b_out'] = w['a_b_out'].reshape(1, D_MODEL)
        p['kv_norm_g'] = w['kv_norm_g'].reshape(1, KV_LORA_RANK)
        p['b_q_norm_g'] = w['b_q_norm_g'].reshape(1, Q_LORA_RANK)
        p['ln_g'] = w['ln_g']
        p['ln_b'] = w['ln_b']
    if 'a_w_out' in w:
        p['a_w_out'] = w['a_w_out'].reshape(e, D_MODEL)
        p['kv_w_down'] = jnp.pad(w['kv_w_down'], ((0, 0), (0, KV_LORA_RANK + LANES - w['kv_w_down'].shape[1])))
        p['kv_w_ukv'] = jnp.concatenate([w['kv_w_uk'], w['kv_w_uv']], axis=2).reshape(KV_LORA_RANK, N_HEADS * HEAD_PAD)
        p['b_w_in'] = w['b_w_in'].reshape(D_MODEL, -1)
        uq = w['b_w_uq'].reshape(Q_LORA_RANK, N_HEADS, QK_NOPE_DIM + QK_ROPE_DIM)
        p['b_w_uq'] = jnp.pad(uq, ((0, 0), (0, 0), (0, HEAD_PAD - uq.shape[2]))).reshape(Q_LORA_RANK, N_HEADS * HEAD_PAD)
        p['b_w_out'] = w['b_w_out'].reshape(N_HEADS * V_HEAD_DIM, D_MODEL)
    return p


class _NoComm:
    def __init__(self):
        self.grads = {}
        self.small = None

    def first_after(self):
        return None

    def rest_weights(self, after):
        return {}

    def send_group1(self, g):
        self.grads.update(g)
        return None

    def send_group1b(self, g):
        self.grads.update(g)
        return None

    def send_group2(self, dw_lo, small):
        self.grads['a_w_in_lo'] = dw_lo
        self.small = small
        return None

    def send_group3(self, dw_hi):
        self.grads['a_w_in_hi'] = dw_hi
        return None


def _local_step(x, positions, target, p, comm):
    t = x.shape[0]
    e = CONV_WIDTH
    freqs = ROPE_THETA ** (-jnp.arange(0, QK_ROPE_DIM, 2, dtype=F32) / QK_ROPE_DIM)
    freq_row = jnp.concatenate([freqs, freqs, jnp.zeros((LANES - QK_ROPE_DIM,), F32)]).reshape(1, LANES)
    rc, rs = _rope_tables(positions.reshape(t, 1), freq_row)
    xb = x.astype(BF16)
    ln_g0, ln_b0 = p['ln_g'][0:1], p['ln_b'][0:1]
    ln_g1, ln_b1 = p['ln_g'][1:2], p['ln_b'][1:2]

    proj = _matmul(xb, p['a_w_in3'], bias=p['a_b_in'], name="a_in", b_blocked=True, bm=2048,
                   after=comm.first_after())
    u1, u2 = _conv_mid_fwd(proj, p['a_conv_w'], p['a_conv_b'], p['a_norm_g'], p['a_norm_b'])
    p = {**p, **comm.rest_weights(u2)}
    y = _matmul(u2, p['a_w_out'], bias=p['a_b_out'], name="a_out", bk=2048)
    h1, h1b, xh1, rstd1 = _ln_res_fwd(x, y, ln_g0, ln_b0)
    ckv = _matmul(h1b, p['kv_w_down'], name="kv_down")
    ckn, kr = _kv_mid_fwd(ckv, p['kv_norm_g'], rc, rs)
    kv = _matmul(ckn, p['kv_w_ukv'], name="kv_up", out_dtype=BF16)
    projb = _matmul(h1b, p['b_w_in'], name="b_in", bn=1280)
    cqn = _q_norm_fwd(projb, p['b_q_norm_g'])
    qcat = _q_up_rope(cqn, p['b_w_uq'], rc, rs)
    o, o2, lse = _flash_fwd(qcat, kv, kr, projb)
    yb = _matmul(o2, p['b_w_out'], name="b_out", bk=2048)
    loss, dr2, dr2b, dg1, db1 = _final_ln_loss(h1, yb, ln_g1, ln_b1, target)

    g = {}
    g['b_w_out'] = _matmul(o2, dr2b, trans_a=True, name="d_b_out", out_dtype=BF16, bm=512, bk=t)
    do2 = _matmul(dr2b, p['b_w_out'], trans_b=True, name="d_o2")
    dob, dprojb, dl = _gate_bwd(do2, o, projb)
    dq2, dkv, dkr = _flash_bwd(qcat, kv, kr, dob, lse, dl, rc, rs)
    duq = _matmul(cqn, dq2, trans_a=True, name="d_q_up", out_dtype=BF16, bk=t)
    dcqn = _matmul(dq2, p['b_w_uq'], trans_b=True, name="d_cqn", bk=N_HEADS * HEAD_PAD)
    dprojb, dgq = _q_norm_bwd(dcqn, projb, p['b_q_norm_g'], dprojb)
    g['b_w_in'] = _matmul(h1b, dprojb, trans_a=True, name="d_b_in", bn=640, bk=t, out_dtype=BF16)
    t1 = _matmul(dprojb, p['b_w_in'], trans_b=True, name="d_h1_b", bk=dprojb.shape[1])
    dukv = _matmul(ckn, dkv, trans_a=True, name="d_kv_up", out_dtype=BF16, bk=t)
    dckn = _matmul(dkv, p['kv_w_ukv'], trans_b=True, name="d_ckn", bk=N_HEADS * HEAD_PAD)
    dckv, dgkv = _kv_mid_bwd(dckn, ckv, dkr, p['kv_norm_g'], rc, rs)
    ddown = _matmul(h1b, dckv, trans_a=True, name="d_kv_down", out_dtype=BF16, bm=512, bk=t)
    g['b_w_out'] = g['b_w_out'].reshape(N_DEV, -1, D_MODEL)
    g['b_w_in'] = g['b_w_in'].reshape(D_MODEL, N_DEV, -1).transpose(1, 0, 2)
    g['kv_w_down'] = ddown[:, :KV_LORA_RANK + QK_ROPE_DIM].reshape(N_DEV, -1, KV_LORA_RANK + QK_ROPE_DIM)
    dukv = dukv.reshape(KV_LORA_RANK, N_HEADS, HEAD_PAD)
    g['kv_w_uk'] = dukv[:, :, :QK_NOPE_DIM].reshape(N_DEV, -1, QK_NOPE_DIM)
    g['kv_w_uv'] = dukv[:, :, QK_NOPE_DIM:].reshape(N_DEV, -1, V_HEAD_DIM)
    g['b_w_uq'] = duq.reshape(Q_LORA_RANK, N_HEADS, HEAD_PAD)[:, :, :QK_NOPE_DIM + QK_ROPE_DIM].reshape(
        N_DEV, -1, QK_NOPE_DIM + QK_ROPE_DIM)
    sent1 = comm.send_group1(g)
    t2 = _matmul(dckv, p['kv_w_down'], trans_b=True, name="d_h1_kv", after=sent1)
    dr1, dr1b, dg0, db0, dba_out = _ln0_bwd(dr2, t1, t2, xh1, rstd1, ln_g0)
    dw_out = _matmul(u2, dr1b, trans_a=True, name="d_a_out", out_dtype=BF16, bm=512, bk=t).reshape(N_DEV, -1, D_MODEL)
    sent1b = comm.send_group1b({'a_w_out': dw_out})
    du2 = _matmul(dr1b, p['a_w_out'], trans_b=True, name="d_u2", after=sent1b)
    du1, dproj, dng, dnb, dbz = _conv_mid_bwd_rows(u1, proj, du2, p['a_norm_g'], p['a_norm_b'])
    dproj, dcw, dcb, dbv = _conv_bwd(du1, proj, p['a_conv_w'], dproj)
    half = D_MODEL // 2
    dw_lo = _matmul(xb, dproj, trans_a=True, name="d_a_in_lo", out_dtype=BF16, bm=half, bn=3 * e // N_DEV, bk=t,
                    out_blocked=True, a_cols=(0, half))
    small = (dg0, dg1, db0, db1, dbv, dbz, dcb, dng, dnb, dba_out, dgkv, dgq, dcw, loss)
    sent2 = comm.send_group2(dw_lo, small)
    dw_hi = _matmul(xb, dproj, trans_a=True, name="d_a_in_hi", out_dtype=BF16, bm=half, bn=3 * e // N_DEV, bk=t,
                    out_blocked=True, after=sent2, a_cols=(half, half))
    sent3 = comm.send_group3(dw_hi)
    grad_x = _grad_x(dproj, p['a_w_in3'], dr1, sent3)
    return loss, grad_x


def _grad_x(dproj, w3, dr1, after):
    t, d = dr1.shape
    nblk, _, cb = w3.shape
    tm = min(t, 512)
    extra = [] if after is None else [after]

    def body(a_ref, w_ref, dr_ref, *rest):
        o_ref = rest[-1]
        acc = ALPHA * dr_ref[...]
        for k in range(nblk):
            acc = acc + lax.dot_general(a_ref[:, k * cb:(k + 1) * cb], w_ref[k], NT, preferred_element_type=F32)
        o_ref[...] = acc

    row = pl.BlockSpec((tm, d), lambda i: (i, 0))
    return pl.pallas_call(
        body, name="grad_x",
        out_shape=jax.ShapeDtypeStruct((t, d), F32),
        grid=(t // tm,),
        in_specs=[pl.BlockSpec((tm, nblk * cb), lambda i: (i, 0)),
                  pl.BlockSpec((nblk, d, cb), lambda i: (0, 0, 0), pipeline_mode=pl.Buffered(1)), row]
                 + [pl.BlockSpec(memory_space=pl.ANY)] * len(extra),
        out_specs=row,
        compiler_params=_cparams(("parallel",)),
    )(dproj, w3, dr1, *extra)


def _shard_2d(a):
    return a.reshape(-1, a.shape[-1])


def _gathered_to_full(name, blocks):
    if name == 'a_w_in':
        return blocks
    if name == 'b_w_in':
        return blocks.transpose(1, 0, 2).reshape(D_MODEL, -1)
    if name == 'a_conv_w':
        return blocks.transpose(1, 0, 2).reshape(CONV_KERNEL, -1)
    if name in ('a_b_in', 'a_conv_b', 'a_norm_g', 'a_norm_b', 'a_b_out'):
        return blocks.reshape(-1)
    if name in ('kv_w_uk', 'kv_w_uv'):
        return blocks.reshape(KV_LORA_RANK, N_HEADS, -1)
    if name == 'b_w_uq':
        return blocks.reshape(Q_LORA_RANK, N_HEADS, -1)
    return blocks.reshape(-1, blocks.shape[-1])


def kernel(x, positions, ln_g, ln_b, a_w_in, a_b_in, a_conv_w, a_conv_b, a_norm_g, a_norm_b, a_w_out, a_b_out, kv_w_down, kv_norm_g, kv_w_uk, kv_w_uv, b_w_in, b_q_norm_g, b_w_uq, b_w_out, loss_target, m_ln_g, m_ln_b, m_a_w_in, m_a_b_in, m_a_conv_w, m_a_conv_b, m_a_norm_g, m_a_norm_b, m_a_w_out, m_a_b_out, m_kv_w_down, m_kv_norm_g, m_kv_w_uk, m_kv_w_uv, m_b_w_in, m_b_q_norm_g, m_b_w_uq, m_b_w_out, v_ln_g, v_ln_b, v_a_w_in, v_a_b_in, v_a_conv_w, v_a_conv_b, v_a_norm_g, v_a_norm_b, v_a_w_out, v_a_b_out, v_kv_w_down, v_kv_norm_g, v_kv_w_uk, v_kv_w_uv, v_b_w_in, v_b_q_norm_g, v_b_w_uq, v_b_w_out):
    w = dict(ln_g=ln_g, ln_b=ln_b, a_w_in=a_w_in, a_b_in=a_b_in, a_conv_w=a_conv_w, a_conv_b=a_conv_b,
             a_norm_g=a_norm_g, a_norm_b=a_norm_b, a_w_out=a_w_out, a_b_out=a_b_out, kv_w_down=kv_w_down,
             kv_norm_g=kv_norm_g, kv_w_uk=kv_w_uk, kv_w_uv=kv_w_uv, b_w_in=b_w_in, b_q_norm_g=b_q_norm_g,
             b_w_uq=b_w_uq, b_w_out=b_w_out)
    m = dict(ln_g=m_ln_g, ln_b=m_ln_b, a_w_in=m_a_w_in, a_b_in=m_a_b_in, a_conv_w=m_a_conv_w, a_conv_b=m_a_conv_b,
             a_norm_g=m_a_norm_g, a_norm_b=m_a_norm_b, a_w_out=m_a_w_out, a_b_out=m_a_b_out, kv_w_down=m_kv_w_down,
             kv_norm_g=m_kv_norm_g, kv_w_uk=m_kv_w_uk, kv_w_uv=m_kv_w_uv, b_w_in=m_b_w_in, b_q_norm_g=m_b_q_norm_g,
             b_w_uq=m_b_w_uq, b_w_out=m_b_w_out)
    v = dict(ln_g=v_ln_g, ln_b=v_ln_b, a_w_in=v_a_w_in, a_b_in=v_a_b_in, a_conv_w=v_a_conv_w, a_conv_b=v_a_conv_b,
             a_norm_g=v_a_norm_g, a_norm_b=v_a_norm_b, a_w_out=v_a_w_out, a_b_out=v_a_b_out, kv_w_down=v_kv_w_down,
             kv_norm_g=v_kv_norm_g, kv_w_uk=v_kv_w_uk, kv_w_uv=v_kv_w_uv, b_w_in=v_b_w_in, b_q_norm_g=v_b_q_norm_g,
             b_w_uq=v_b_w_uq, b_w_out=v_b_w_out)

    small_flat = jnp.concatenate([w[n].reshape(-1) for n in SMALL_WEIGHTS]).reshape(1, -1)
    ga = _all_gather("all_gather_first", [_shard_2d(w['a_w_in']).astype(BF16), small_flat])
    full = {'a_w_in': ga[0]}
    gs = ga[1].reshape(N_DEV, -1)
    off = 0
    for n in SMALL_WEIGHTS:
        size = math.prod(w[n].shape)
        full[n] = _gathered_to_full(n, gs[:, off:off + size].reshape((N_DEV,) + _shard_2d(w[n]).shape))
        off += size
    for n in REPLICATED:
        full[n] = w[n]
    rest_names = [n for n in MATMUL_WEIGHTS if n != 'a_w_in']
    group1 = ('b_w_out', 'b_w_uq', 'b_w_in', 'kv_w_uk', 'kv_w_uv', 'kv_w_down')
    group1b = ('a_w_out',)

    class Comm:
        def __init__(self):
            dense = lambda a: a if a.shape[-1] % LANES == 0 else a.reshape(-1, LANES)
            self.rest, self.rest_token = _comm_start(
                "gather_rest_start", [dense(_shard_2d(w[n]).astype(BF16)) for n in rest_names], True, ga[0])

        def first_after(self):
            return self.rest_token

        def rest_weights(self, after):
            lands = _comm_wait("gather_rest_wait", self.rest, True, after)
            blocks = {n: lands[i].reshape((N_DEV,) + _shard_2d(w[n]).shape) for i, n in enumerate(rest_names)}
            return _prep_weights({n: _gathered_to_full(n, blocks[n]) for n in rest_names})

        def send_group1(self, g):
            self.g1, token = _comm_start("exchange_g1_start", [g[n] for n in group1], False, None)
            return token

        def send_group1b(self, g):
            self.g1b, token = _comm_start("exchange_g1b_start", [g[n] for n in group1b], False, None)
            return token

        def send_group2(self, dw_lo, small):
            self.g2, token = _comm_start("exchange_g2_start", [dw_lo, _pack_small_grads(*small)], False, None)
            return token

        def send_group3(self, dw_hi):
            self.g3, token = _comm_start("exchange_g3_start", [dw_hi], False, None)
            return token

    comm = Comm()

    _, grad_x = _local_step(x[0], positions[0], loss_target[0], _prep_weights(full), comm)

    res = {}
    recv1 = _comm_wait("exchange_g1_wait", comm.g1, False, grad_x)
    for i, n in enumerate(group1):
        res[n] = _adamw(n, recv1[i], _shard_2d(w[n]), _shard_2d(m[n]), _shard_2d(v[n]))
    recv1b = _comm_wait("exchange_g1b_wait", comm.g1b, False, res[group1[-1]][0])
    for i, n in enumerate(group1b):
        res[n] = _adamw(n, recv1b[i], _shard_2d(w[n]), _shard_2d(m[n]), _shard_2d(v[n]))
    recv2 = _comm_wait("exchange_g2_wait", comm.g2, False, res[group1b[-1]][0])
    recv3 = _comm_wait("exchange_g3_wait", comm.g3, False, recv2[0])
    res['a_w_in'] = _adamw('a_w_in', [recv2[0], recv3[0]], _shard_2d(w['a_w_in']), _shard_2d(m['a_w_in']),
                           _shard_2d(v['a_w_in']))
    two_d = lambda d: [_shard_2d(d[n]) if d[n].ndim > 1 else d[n].reshape(1, -1) for n in SMALL_NAMES]
    sg, sd, sm, sv, loss = _adamw_small(recv2[-1], two_d(w), two_d(m), two_d(v))
    for i, n in enumerate(SMALL_NAMES):
        res[n] = (sg[i], sd[i], sm[i], sv[i])
    outs = [[res[n][j].reshape(w[n].shape) for n in WEIGHT_NAMES] for j in range(4)]
    return (loss[0, 0], grad_x[None], *outs[0], *outs[1], *outs[2], *outs[3])
```

```python
import functools
import math

import jax
import jax.numpy as jnp
from jax import lax
from jax.experimental import pallas as pl
from jax.experimental.pallas import tpu as pltpu

F32 = jnp.float32
BF16 = jnp.bfloat16

D_MODEL = 1024
DEPTH = 2
CONV_WIDTH = 2 * D_MODEL
CONV_KERNEL = 31
N_HEADS = 16
QK_NOPE_DIM = 128
QK_ROPE_DIM = 64
V_HEAD_DIM = 128
KV_LORA_RANK = D_MODEL // 4
Q_LORA_RANK = D_MODEL // 2
ROPE_THETA = 10000.0
LN_EPS = 1e-5
RMS_EPS = 1e-6
MASK_VALUE = -1e30
ALPHA = (2.0 * DEPTH) ** 0.25
ATTN_SCALE = 1.0 / math.sqrt(QK_NOPE_DIM + QK_ROPE_DIM)

ADAM_LR = 0.001
ADAM_B1 = 0.9
ADAM_B2 = 0.999
ADAM_EPS = 1e-08
ADAM_WD = 0.01
ADAM_STEP = 10

N_DEV = 8
LANES = 128
HEAD_PAD = 256
HALO = 32
VMEM_LIMIT = 56 * 1024 * 1024

WEIGHT_NAMES = ['ln_g', 'ln_b', 'a_w_in', 'a_b_in', 'a_conv_w', 'a_conv_b', 'a_norm_g', 'a_norm_b',
                'a_w_out', 'a_b_out', 'kv_w_down', 'kv_norm_g', 'kv_w_uk', 'kv_w_uv', 'b_w_in',
                'b_q_norm_g', 'b_w_uq', 'b_w_out']
REPLICATED = ('ln_g', 'ln_b', 'kv_norm_g', 'b_q_norm_g')
MATMUL_WEIGHTS = ('a_w_in', 'a_w_out', 'kv_w_down', 'kv_w_uk', 'kv_w_uv', 'b_w_in', 'b_w_uq', 'b_w_out')
SMALL_WEIGHTS = ('a_b_in', 'a_conv_w', 'a_conv_b', 'a_norm_g', 'a_norm_b', 'a_b_out')

NN = (((1,), (0,)), ((), ()))
NT = (((1,), (1,)), ((), ()))
TN = (((0,), (0,)), ((), ()))


def _cparams(sem):
    return pltpu.CompilerParams(dimension_semantics=sem, vmem_limit_bytes=VMEM_LIMIT)


def _sigmoid(x):
    return 0.5 * jnp.tanh(0.5 * x) + 0.5


def _fold_rows(x):
    parts = [x[r:r + 8] for r in range(0, x.shape[0], 8)]
    while len(parts) > 1:
        parts = [parts[i] + parts[i + 1] for i in range(0, len(parts) - 1, 2)] + ([parts[-1]] if len(parts) % 2 else [])
    return parts[0]


def _taps_by_residue(offset_of):
    groups = {}
    for j in range(CONV_KERNEL):
        off = offset_of(j)
        groups.setdefault(off % 8, []).append((j, off // 8))
    return sorted(groups.items())


def _swap_rope_halves(t):
    lane = lax.broadcasted_iota(jnp.int32, t.shape, 1)
    return jnp.where(lane < QK_ROPE_DIM // 2, pltpu.roll(t, LANES - QK_ROPE_DIM // 2, 1),
                     pltpu.roll(t, QK_ROPE_DIM // 2, 1))


def _matmul(a, b, *, name, bias=None, trans_a=False, trans_b=False, out_dtype=F32, bm=1024, bn=1024, bk=1024,
            b_blocked=False, out_blocked=False, after=None, a_cols=None):
    m_off = 0
    if trans_a:
        kdim, m = a.shape
        if a_cols is not None:
            m_off, m = a_cols
    else:
        m, kdim = a.shape
    if b_blocked and trans_b:
        n, k2 = b.shape[1], b.shape[0] * b.shape[2]
        bk = b.shape[2]
    elif b_blocked:
        k2, n = b.shape[1], b.shape[0] * b.shape[2]
        bn = b.shape[2]
    elif trans_b:
        n, k2 = b.shape
    else:
        k2, n = b.shape
    assert kdim == k2, (a.shape, b.shape)
    bm, bn, bk = min(bm, m), min(bn, n), min(bk, kdim)
    assert m % bm == 0 and n % bn == 0 and kdim % bk == 0, (name, m, n, kdim, bm, bn, bk)
    nk = kdim // bk
    dims = (((0 if trans_a else 1,), (1 if trans_b else 0,)), ((), ()))
    has_bias = bias is not None

    def body(*refs):
        refs = list(refs)
        a_ref, b_ref = refs[:2]
        del refs[:2]
        bias_ref = refs.pop(0) if has_bias else None
        if after is not None:
            refs.pop(0)
        o_ref = refs.pop(0)
        rest = refs
        prod = lax.dot_general(a_ref[...], b_ref[...], dims, preferred_element_type=F32)

        def finish(acc):
            if has_bias:
                acc = acc + bias_ref[...]
            o_ref[...] = acc.astype(out_dtype)

        if nk == 1:
            finish(prod)
        else:
            acc_ref = rest[0]
            k = pl.program_id(2)

            @pl.when(k == 0)
            def _():
                acc_ref[...] = prod

            @pl.when(k > 0)
            def _():
                acc_ref[...] += prod

            @pl.when(k == nk - 1)
            def _():
                finish(acc_ref[...])

    assert m_off % bm == 0
    a_first = m_off // bm
    a_spec = (pl.BlockSpec((bk, bm), lambda i, j, k: (k, i + a_first)) if trans_a
              else pl.BlockSpec((bm, bk), lambda i, j, k: (i, k)))
    if b_blocked and trans_b:
        b_spec = pl.BlockSpec((None, bn, bk), lambda i, j, k: (k, j, 0))
    elif b_blocked:
        b_spec = pl.BlockSpec((None, bk, bn), lambda i, j, k: (j, k, 0))
    elif trans_b:
        b_spec = pl.BlockSpec((bn, bk), lambda i, j, k: (j, k))
    else:
        b_spec = pl.BlockSpec((bk, bn), lambda i, j, k: (k, j))
    in_specs = [a_spec, b_spec]
    args = [a, b]
    if has_bias:
        in_specs.append(pl.BlockSpec((1, bn), lambda i, j, k: (0, j)))
        args.append(bias)
    if after is not None:
        in_specs.append(pl.BlockSpec(memory_space=pl.ANY))
        args.append(after)
    if out_blocked:
        out_struct = jax.ShapeDtypeStruct((n // bn, m, bn), out_dtype)
        out_spec = pl.BlockSpec((None, bm, bn), lambda i, j, k: (j, i, 0))
    else:
        out_struct = jax.ShapeDtypeStruct((m, n), out_dtype)
        out_spec = pl.BlockSpec((bm, bn), lambda i, j, k: (i, j))
    return pl.pallas_call(
        body, name=name,
        out_shape=out_struct,
        grid=(m // bm, n // bn, nk),
        in_specs=in_specs,
        out_specs=out_spec,
        scratch_shapes=[pltpu.VMEM((bm, bn), F32)] if nk > 1 else [],
        compiler_params=_cparams(("parallel", "parallel", "arbitrary")),
    )(*args)


def _rope_tables(pos_col, freq_row, x, after):
    t, d = x.shape
    tm = min(t, 512)
    extra = [] if after is None else [after]

    def body(pos_ref, f_ref, x_ref, *rest):
        c_ref, s_ref, xb_ref = rest[-3:]
        ang = pos_ref[...].astype(F32) * f_ref[...]
        lane = lax.broadcasted_iota(jnp.int32, ang.shape, 1)
        cos = jnp.cos(ang)
        sin = jnp.sin(ang)
        c_ref[...] = jnp.where(lane < QK_ROPE_DIM, cos, 0.0)
        s_ref[...] = jnp.where(lane < QK_ROPE_DIM // 2, -sin, jnp.where(lane < QK_ROPE_DIM, sin, 0.0))
        xb_ref[...] = x_ref[...].astype(BF16)

    lane_blk = pl.BlockSpec((tm, LANES), lambda i: (i, 0))
    row = pl.BlockSpec((tm, d), lambda i: (i, 0))
    return pl.pallas_call(
        body, name="rope_tables",
        out_shape=(jax.ShapeDtypeStruct((t, LANES), F32), jax.ShapeDtypeStruct((t, LANES), F32),
                   jax.ShapeDtypeStruct((t, d), BF16)),
        grid=(t // tm,),
        in_specs=[pl.BlockSpec((tm, 1), lambda i: (i, 0)), pl.BlockSpec((1, LANES), lambda i: (0, 0)), row]
                 + [pl.BlockSpec(memory_space=pl.ANY)] * len(extra),
        out_specs=(lane_blk, lane_blk, row),
        compiler_params=_cparams(("parallel",)),
    )(pos_col, freq_row, x, *extra)


def _conv_mid_fwd(proj, cw, cb, ng, nb):
    t = proj.shape[0]
    e = CONV_WIDTH
    tm = min(t, 128)
    hb = tm // HALO
    nchunk = e // LANES

    def body(val_ref, gg_ref, z_ref, valh_ref, ggh_ref, cw_ref, cb_ref, ng_ref, nb_ref, u1_ref, u2_ref, ext_ref, sh_ref):
        i = pl.program_id(0)
        u0 = val_ref[...] * _sigmoid(gg_ref[...])
        h0 = valh_ref[...] * _sigmoid(ggh_ref[...])
        ext_ref[0:HALO, :] = jnp.where(i > 0, h0, 0.0)
        ext_ref[HALO:, :] = u0

        def chunk(c, carry):
            col = pl.multiple_of(c * LANES, LANES)
            acc = jnp.zeros((tm, LANES), F32)
            for res, taps in _taps_by_residue(lambda j: HALO - (CONV_KERNEL - 1) + j):
                span = 8 * max(a for _, a in taps) + tm
                sh_ref[0:span, :] = ext_ref[res:res + span, pl.ds(col, LANES)]
                for j, a in taps:
                    acc = acc + cw_ref[j:j + 1, pl.ds(col, LANES)] * sh_ref[8 * a:8 * a + tm, :]
            u1_ref[:, pl.ds(col, LANES)] = acc + cb_ref[:, pl.ds(col, LANES)]
            return carry

        lax.fori_loop(0, nchunk, chunk, 0)
        g = ng_ref[...]
        b = nb_ref[...]
        for r in range(0, tm, 16):
            u1 = u1_ref[r:r + 16, :]
            mu = jnp.mean(u1, axis=-1, keepdims=True)
            xc = u1 - mu
            var = jnp.mean(xc * xc, axis=-1, keepdims=True)
            n = xc * lax.rsqrt(var + LN_EPS) * g + b
            z = z_ref[r:r + 16, :]
            u2_ref[r:r + 16, :] = ((n * _sigmoid(n)) * (z * _sigmoid(z))).astype(BF16)

    row = lambda c: pl.BlockSpec((tm, e), lambda i: (i, c))
    halo = lambda c: pl.BlockSpec((HALO, e), lambda i: (jnp.maximum(i * hb - 1, 0), c))
    par = lambda r: pl.BlockSpec((r, e), lambda i: (0, 0))
    return pl.pallas_call(
        body, name="conv_mid_fwd",
        out_shape=(jax.ShapeDtypeStruct((t, e), F32), jax.ShapeDtypeStruct((t, e), BF16)),
        grid=(t // tm,),
        in_specs=[row(0), row(1), row(2), halo(0), halo(1), par(HALO), par(1), par(1), par(1)],
        out_specs=(pl.BlockSpec((tm, e), lambda i: (i, 0)), pl.BlockSpec((tm, e), lambda i: (i, 0))),
        scratch_shapes=[pltpu.VMEM((tm + HALO, e), F32), pltpu.VMEM((tm + HALO, LANES), F32)],
        compiler_params=_cparams(("parallel",)),
    )(proj, proj, proj, proj, proj, cw, cb, ng, nb)


def _ln_res_fwd(x, y, g, b):
    t, d = x.shape
    tm = min(t, 256)

    def body(x_ref, y_ref, g_ref, b_ref, h_ref, hb_ref, xh_ref, rs_ref):
        r = ALPHA * x_ref[...] + y_ref[...]
        mu = jnp.mean(r, axis=-1, keepdims=True)
        xc = r - mu
        var = jnp.mean(xc * xc, axis=-1, keepdims=True)
        rstd = lax.rsqrt(var + LN_EPS)
        xh = xc * rstd
        h = xh * g_ref[...] + b_ref[...]
        h_ref[...] = h
        hb_ref[...] = h.astype(BF16)
        xh_ref[...] = xh
        rs_ref[...] = rstd

    row = pl.BlockSpec((tm, d), lambda i: (i, 0))
    par = pl.BlockSpec((1, d), lambda i: (0, 0))
    return pl.pallas_call(
        body, name="ln0_fwd",
        out_shape=(jax.ShapeDtypeStruct((t, d), F32), jax.ShapeDtypeStruct((t, d), BF16),
                   jax.ShapeDtypeStruct((t, d), F32), jax.ShapeDtypeStruct((t, 1), F32)),
        grid=(t // tm,),
        in_specs=[row, row, par, par],
        out_specs=(row, row, row, pl.BlockSpec((tm, 1), lambda i: (i, 0))),
        compiler_params=_cparams(("parallel",)),
    )(x, y, g, b)


def _kv_mid_fwd(ckv, g, rc, rs):
    t = ckv.shape[0]
    tm = min(t, 512)
    r = KV_LORA_RANK

    def body(ckv_ref, g_ref, c_ref, s_ref, ckn_ref, kr_ref):
        c = ckv_ref[:, 0:r]
        ms = jnp.mean(c * c, axis=-1, keepdims=True)
        ckn_ref[...] = (c * lax.rsqrt(ms + RMS_EPS) * g_ref[...]).astype(BF16)
        tr = ckv_ref[:, r:r + LANES]
        kr_ref[...] = (tr * c_ref[...] + _swap_rope_halves(tr) * s_ref[...]).astype(BF16)

    return pl.pallas_call(
        body, name="kv_mid_fwd",
        out_shape=(jax.ShapeDtypeStruct((t, r), BF16), jax.ShapeDtypeStruct((t, LANES), BF16)),
        grid=(t // tm,),
        in_specs=[pl.BlockSpec((tm, r + LANES), lambda i: (i, 0)), pl.BlockSpec((1, r), lambda i: (0, 0)),
                  pl.BlockSpec((tm, LANES), lambda i: (i, 0)), pl.BlockSpec((tm, LANES), lambda i: (i, 0))],
        out_specs=(pl.BlockSpec((tm, r), lambda i: (i, 0)), pl.BlockSpec((tm, LANES), lambda i: (i, 0))),
        compiler_params=_cparams(("parallel",)),
    )(ckv, g, rc, rs)


def _q_norm_fwd(projb, g):
    t = projb.shape[0]
    tm = min(t, 512)
    r = Q_LORA_RANK

    def body(c_ref, g_ref, o_ref):
        c = c_ref[...]
        ms = jnp.mean(c * c, axis=-1, keepdims=True)
        o_ref[...] = (c * lax.rsqrt(ms + RMS_EPS) * g_ref[...]).astype(BF16)

    return pl.pallas_call(
        body, name="q_norm_fwd",
        out_shape=jax.ShapeDtypeStruct((t, r), BF16),
        grid=(t // tm,),
        in_specs=[pl.BlockSpec((tm, r), lambda i: (i, 0)), pl.BlockSpec((1, r), lambda i: (0, 0))],
        out_specs=pl.BlockSpec((tm, r), lambda i: (i, 0)),
        compiler_params=_cparams(("parallel",)),
    )(projb, g)


def _q_up_rope(cqn, wuq, rc, rs):
    t, r = cqn.shape
    w = wuq.shape[1]
    tm = min(t, 1024)
    bn = 4 * HEAD_PAD

    def body(a_ref, b_ref, c_ref, s_ref, o_ref):
        q = lax.dot_general(a_ref[...], b_ref[...], NN, preferred_element_type=F32)
        c = c_ref[...]
        s = s_ref[...]
        for h in range(bn // HEAD_PAD):
            a = h * HEAD_PAD
            o_ref[:, a:a + LANES] = q[:, a:a + LANES].astype(BF16)
            tr = q[:, a + LANES:a + 2 * LANES]
            o_ref[:, a + LANES:a + 2 * LANES] = (tr * c + _swap_rope_halves(tr) * s).astype(BF16)

    return pl.pallas_call(
        body, name="q_up_rope",
        out_shape=jax.ShapeDtypeStruct((t, w), BF16),
        grid=(t // tm, w // bn),
        in_specs=[pl.BlockSpec((tm, r), lambda i, j: (i, 0)), pl.BlockSpec((r, bn), lambda i, j: (0, j)),
                  pl.BlockSpec((tm, LANES), lambda i, j: (i, 0)), pl.BlockSpec((tm, LANES), lambda i, j: (i, 0))],
        out_specs=pl.BlockSpec((tm, bn), lambda i, j: (i, j)),
        compiler_params=_cparams(("parallel", "parallel")),
    )(cqn, wuq, rc, rs)


def _flash_fwd(qcat, kv, kr, projb):
    t = qcat.shape[0]
    tq = min(t, 512)
    nq = t // tq
    exp2_scale = ATTN_SCALE * math.log2(math.e)
    z_first = Q_LORA_RANK // LANES

    def body(q_ref, kv_ref, kr_ref, z_ref, o_ref, o2_ref, lse_ref, kcat_ref):
        kcat_ref[:, 0:LANES] = kv_ref[:, 0:LANES]
        kcat_ref[:, LANES:] = kr_ref[...]
        rows = lax.broadcasted_iota(jnp.int32, (tq, tq), 0)
        cols = lax.broadcasted_iota(jnp.int32, (tq, tq), 1)
        for i in range(nq):
            a = i * tq
            q = q_ref[a:a + tq, :]
            sd = lax.dot_general(q, kcat_ref[a:a + tq, :], NT, preferred_element_type=F32)
            sd = jnp.where(cols <= rows, sd, MASK_VALUE)
            m = jnp.max(sd, axis=1, keepdims=True)
            if i > 0:
                sp = lax.dot_general(q, kcat_ref[0:a, :], NT, preferred_element_type=F32)
                m = jnp.maximum(m, jnp.max(sp, axis=1, keepdims=True))
            pd = jnp.exp2((sd - m) * exp2_scale)
            l = jnp.sum(pd, axis=1, keepdims=True)
            acc = lax.dot_general(pd.astype(BF16), kv_ref[a:a + tq, LANES:], NN, preferred_element_type=F32)
            if i > 0:
                pp = jnp.exp2((sp - m) * exp2_scale)
                l = l + jnp.sum(pp, axis=1, keepdims=True)
                acc = acc + lax.dot_general(pp.astype(BF16), kv_ref[0:a, LANES:], NN, preferred_element_type=F32)
            o = acc / l
            z = z_ref[a:a + tq, :]
            o_ref[a:a + tq, :] = o
            o2_ref[a:a + tq, :] = (o * (z * _sigmoid(z))).astype(BF16)
            lse_ref[a:a + tq, :] = jnp.broadcast_to(m * ATTN_SCALE + jnp.log(l), (tq, LANES))

    hw = N_HEADS * LANES
    head256 = pl.BlockSpec((t, HEAD_PAD), lambda h: (0, h))
    head128 = pl.BlockSpec((t, LANES), lambda h: (0, h))
    return pl.pallas_call(
        body, name="flash_fwd",
        out_shape=(jax.ShapeDtypeStruct((t, hw), F32), jax.ShapeDtypeStruct((t, hw), BF16),
                   jax.ShapeDtypeStruct((t, hw), F32)),
        grid=(N_HEADS,),
        in_specs=[head256, head256, pl.BlockSpec((t, LANES), lambda h: (0, 0), pipeline_mode=pl.Buffered(1)),
                  pl.BlockSpec((t, LANES), lambda h: (0, z_first + h))],
        out_specs=(head128, head128, head128),
        scratch_shapes=[pltpu.VMEM((t, HEAD_PAD), BF16)],
        compiler_params=_cparams(("parallel",)),
    )(qcat, kv, kr, projb)


def _final_ln_loss(h1, yb, g, b, target):
    t, d = h1.shape
    tm = min(t, 256)

    def body(h_ref, y_ref, g_ref, b_ref, t_ref, loss_ref, dr_ref, drb_ref, dg_ref, db_ref):
        i = pl.program_id(0)
        r = ALPHA * h_ref[...] + y_ref[...]
        mu = jnp.mean(r, axis=-1, keepdims=True)
        xc = r - mu
        var = jnp.mean(xc * xc, axis=-1, keepdims=True)
        rstd = lax.rsqrt(var + LN_EPS)
        xh = xc * rstd
        g = g_ref[...]
        diff = xh * g + b_ref[...] - t_ref[...]
        part = 0.5 * jnp.sum(jnp.mean(diff * diff, axis=-1, keepdims=True))
        dh = diff * (1.0 / d)
        dgp = jnp.sum(dh * xh, axis=0, keepdims=True)
        dbp = jnp.sum(dh, axis=0, keepdims=True)

        @pl.when(i == 0)
        def _():
            loss_ref[...] = jnp.zeros_like(loss_ref)
            dg_ref[...] = jnp.zeros_like(dg_ref)
            db_ref[...] = jnp.zeros_like(db_ref)

        loss_ref[...] += jnp.full(loss_ref.shape, part, F32)
        dg_ref[...] += dgp
        db_ref[...] += dbp
        dxh = dh * g
        dr = rstd * (dxh - jnp.mean(dxh, axis=-1, keepdims=True) - xh * jnp.mean(dxh * xh, axis=-1, keepdims=True))
        dr_ref[...] = dr
        drb_ref[...] = dr.astype(BF16)

    row = pl.BlockSpec((tm, d), lambda i: (i, 0))
    par = pl.BlockSpec((1, d), lambda i: (0, 0))
    return pl.pallas_call(
        body, name="final_ln_loss",
        out_shape=(jax.ShapeDtypeStruct((1, LANES), F32), jax.ShapeDtypeStruct((t, d), F32),
                   jax.ShapeDtypeStruct((t, d), BF16), jax.ShapeDtypeStruct((1, d), F32),
                   jax.ShapeDtypeStruct((1, d), F32)),
        grid=(t // tm,),
        in_specs=[row, row, par, par, row],
        out_specs=(pl.BlockSpec((1, LANES), lambda i: (0, 0)), row, row, par, par),
        compiler_params=_cparams(("arbitrary",)),
    )(h1, yb, g, b, target)


def _gate_bwd(do2, o, projb):
    t, hw = o.shape
    tm = min(t, 512)
    bw = Q_LORA_RANK
    nb = hw // bw
    wb = projb.shape[1]

    def body(d_ref, o_ref, z_ref, dob_ref, dz_ref, dl_ref):
        z = z_ref[...]
        sg = _sigmoid(z)
        d2 = d_ref[...]
        o = o_ref[...]
        do = d2 * (z * sg)
        dob_ref[...] = do.astype(BF16)
        dz_ref[...] = (d2 * o * (sg * (1.0 + z * (1.0 - sg)))).astype(BF16)
        prod = do * o
        for hh in range(bw // LANES):
            dsum = jnp.sum(prod[:, hh * LANES:(hh + 1) * LANES], axis=1, keepdims=True)
            dl_ref[:, hh * LANES:(hh + 1) * LANES] = jnp.broadcast_to(dsum, (tm, LANES))

    blk = pl.BlockSpec((tm, bw), lambda i, j: (i, j))
    blk1 = pl.BlockSpec((tm, bw), lambda i, j: (i, j + 1))
    return pl.pallas_call(
        body, name="gate_bwd",
        out_shape=(jax.ShapeDtypeStruct((t, hw), BF16), jax.ShapeDtypeStruct((t, wb), BF16),
                   jax.ShapeDtypeStruct((t, hw), F32)),
        grid=(t // tm, nb),
        in_specs=[blk, blk, blk1],
        out_specs=(blk, blk1, blk),
        compiler_params=_cparams(("parallel", "parallel")),
    )(do2, o, projb)


def _flash_bwd(qcat, kv, kr, dob, lse, dl, rc, rs):
    t = qcat.shape[0]
    tq = min(t, 512)
    nq = t // tq
    q_chunk = 2 * tq
    log2e = math.log2(math.e)
    exp2_scale = ATTN_SCALE * log2e
    once = pl.Buffered(1)

    def body(q_ref, kv_ref, kr_ref, do_ref, lse_ref, dl_ref, c_ref, s_ref,
             dq_ref, dkv_ref, dkr_ref, kcat_ref, dqa_ref, dka_ref, dva_ref):
        h = pl.program_id(0)
        kcat_ref[:, 0:LANES] = kv_ref[:, 0:LANES]
        kcat_ref[:, LANES:] = kr_ref[...]
        dqa_ref[...] = jnp.zeros_like(dqa_ref)

        @pl.when(h == 0)
        def _():
            dkr_ref[...] = jnp.zeros_like(dkr_ref)

        rows = lax.broadcasted_iota(jnp.int32, (tq, tq), 0)
        cols = lax.broadcasted_iota(jnp.int32, (tq, tq), 1)

        def block(qs, n, ks, masked):
            q = q_ref[qs:qs + n, :]
            kc = kcat_ref[ks:ks + tq, :]
            s = lax.dot_general(q, kc, NT, preferred_element_type=F32)
            if masked:
                s = jnp.where(cols <= rows, s, MASK_VALUE)
            p = jnp.exp2(s * exp2_scale - lse_ref[qs:qs + n, 0:1] * log2e)
            do = do_ref[qs:qs + n, :]
            dp = lax.dot_general(do, kv_ref[ks:ks + tq, LANES:], NT, preferred_element_type=F32)
            ds = (p * (dp - dl_ref[qs:qs + n, 0:1])).astype(BF16)
            dva_ref[...] += lax.dot_general(p.astype(BF16), do, TN, preferred_element_type=F32)
            dka_ref[...] += lax.dot_general(ds, q, TN, preferred_element_type=F32)
            dqa_ref[qs:qs + n, :] += lax.dot_general(ds, kc, NN, preferred_element_type=F32)

        for kb in range(nq):
            a = kb * tq
            dka_ref[...] = jnp.zeros_like(dka_ref)
            dva_ref[...] = jnp.zeros_like(dva_ref)
            block(a, tq, a, True)
            qs = a + tq
            while qs < t:
                n = min(q_chunk, t - qs)
                block(qs, n, a, False)
                qs += n
            dk = dka_ref[...] * ATTN_SCALE
            dkv_ref[a:a + tq, 0:LANES] = dk[:, 0:LANES].astype(BF16)
            dkv_ref[a:a + tq, LANES:] = dva_ref[...].astype(BF16)
            dkr_ref[a:a + tq, :] += dk[:, LANES:]

        dq = dqa_ref[...] * ATTN_SCALE
        dq_ref[:, 0:LANES] = dq[:, 0:LANES].astype(BF16)
        d2 = dq[:, LANES:]
        dq_ref[:, LANES:] = (d2 * c_ref[...] - _swap_rope_halves(d2) * s_ref[...]).astype(BF16)

    hp = N_HEADS * HEAD_PAD
    head256 = pl.BlockSpec((t, HEAD_PAD), lambda h: (0, h))
    head128 = pl.BlockSpec((t, LANES), lambda h: (0, h))
    const128 = pl.BlockSpec((t, LANES), lambda h: (0, 0), pipeline_mode=once)
    return pl.pallas_call(
        body, name="flash_bwd",
        out_shape=(jax.ShapeDtypeStruct((t, hp), BF16), jax.ShapeDtypeStruct((t, hp), BF16),
                   jax.ShapeDtypeStruct((t, LANES), F32)),
        grid=(N_HEADS,),
        in_specs=[head256, head256, const128, head128, head128, head128, const128, const128],
        out_specs=(head256, head256, pl.BlockSpec((t, LANES), lambda h: (0, 0))),
        scratch_shapes=[pltpu.VMEM((t, HEAD_PAD), BF16), pltpu.VMEM((t, HEAD_PAD), F32),
                        pltpu.VMEM((tq, HEAD_PAD), F32), pltpu.VMEM((tq, LANES), F32)],
        compiler_params=_cparams(("arbitrary",)),
    )(qcat, kv, kr, dob, lse, dl, rc, rs)


def _q_norm_bwd(dcqn, projb, g, dprojb):
    t = projb.shape[0]
    tm = min(t, 512)
    r = Q_LORA_RANK

    def body(d_ref, c_ref, g_ref, alias_ref, o_ref, dg_ref):
        i = pl.program_id(0)
        c = c_ref[...]
        d = d_ref[...]
        rr = lax.rsqrt(jnp.mean(c * c, axis=-1, keepdims=True) + RMS_EPS)
        xh = c * rr

        @pl.when(i == 0)
        def _():
            dg_ref[...] = jnp.zeros_like(dg_ref)

        dg_ref[...] += jnp.sum(d * xh, axis=0, keepdims=True)
        dxh = d * g_ref[...]
        o_ref[...] = (rr * (dxh - xh * jnp.mean(dxh * xh, axis=-1, keepdims=True))).astype(BF16)

    blk = pl.BlockSpec((tm, r), lambda i: (i, 0))
    par = pl.BlockSpec((1, r), lambda i: (0, 0))
    return pl.pallas_call(
        body, name="q_norm_bwd",
        out_shape=(jax.ShapeDtypeStruct(dprojb.shape, BF16), jax.ShapeDtypeStruct((1, r), F32)),
        grid=(t // tm,),
        in_specs=[blk, blk, par, pl.BlockSpec(memory_space=pl.ANY)],
        out_specs=(blk, par),
        input_output_aliases={3: 0},
        compiler_params=_cparams(("arbitrary",)),
    )(dcqn, projb, g, dprojb)


def _kv_mid_bwd(dckn, ckv, dkr, g, rc, rs):
    t = ckv.shape[0]
    tm = min(t, 512)
    r = KV_LORA_RANK

    def body(d_ref, ckv_ref, dkr_ref, g_ref, c_ref, s_ref, o_ref, dg_ref):
        i = pl.program_id(0)
        c = ckv_ref[:, 0:r]
        d = d_ref[...]
        rr = lax.rsqrt(jnp.mean(c * c, axis=-1, keepdims=True) + RMS_EPS)
        xh = c * rr

        @pl.when(i == 0)
        def _():
            dg_ref[...] = jnp.zeros_like(dg_ref)

        dg_ref[...] += jnp.sum(d * xh, axis=0, keepdims=True)
        dxh = d * g_ref[...]
        o_ref[:, 0:r] = (rr * (dxh - xh * jnp.mean(dxh * xh, axis=-1, keepdims=True))).astype(BF16)
        dk = dkr_ref[...]
        o_ref[:, r:] = (dk * c_ref[...] - _swap_rope_halves(dk) * s_ref[...]).astype(BF16)

    return pl.pallas_call(
        body, name="kv_mid_bwd",
        out_shape=(jax.ShapeDtypeStruct((t, r + LANES), BF16), jax.ShapeDtypeStruct((1, r), F32)),
        grid=(t // tm,),
        in_specs=[pl.BlockSpec((tm, r), lambda i: (i, 0)), pl.BlockSpec((tm, r + LANES), lambda i: (i, 0)),
                  pl.BlockSpec((tm, LANES), lambda i: (i, 0)), pl.BlockSpec((1, r), lambda i: (0, 0)),
                  pl.BlockSpec((tm, LANES), lambda i: (i, 0)), pl.BlockSpec((tm, LANES), lambda i: (i, 0))],
        out_specs=(pl.BlockSpec((tm, r + LANES), lambda i: (i, 0)), pl.BlockSpec((1, r), lambda i: (0, 0))),
        compiler_params=_cparams(("arbitrary",)),
    )(dckn, ckv, dkr, g, rc, rs)


def _ln0_bwd(dr2, t1, t2, xh, rstd, g):
    t, d = dr2.shape
    tm = min(t, 256)

    def body(a_ref, b_ref, c_ref, xh_ref, rs_ref, g_ref, dr_ref, drb_ref, dg_ref, db_ref, dsum_ref):
        i = pl.program_id(0)
        dh = ALPHA * a_ref[...] + b_ref[...] + c_ref[...]
        xh = xh_ref[...]

        @pl.when(i == 0)
        def _():
            dg_ref[...] = jnp.zeros_like(dg_ref)
            db_ref[...] = jnp.zeros_like(db_ref)
            dsum_ref[...] = jnp.zeros_like(dsum_ref)

        dg_ref[...] += jnp.sum(dh * xh, axis=0, keepdims=True)
        db_ref[...] += jnp.sum(dh, axis=0, keepdims=True)
        dxh = dh * g_ref[...]
        dr = rs_ref[...] * (dxh - jnp.mean(dxh, axis=-1, keepdims=True) - xh * jnp.mean(dxh * xh, axis=-1, keepdims=True))
        dr_ref[...] = dr
        drb_ref[...] = dr.astype(BF16)
        dsum_ref[...] += jnp.sum(dr, axis=0, keepdims=True)

    row = pl.BlockSpec((tm, d), lambda i: (i, 0))
    par = pl.BlockSpec((1, d), lambda i: (0, 0))
    return pl.pallas_call(
        body, name="ln0_bwd",
        out_shape=(jax.ShapeDtypeStruct((t, d), F32), jax.ShapeDtypeStruct((t, d), BF16),
                   jax.ShapeDtypeStruct((1, d), F32), jax.ShapeDtypeStruct((1, d), F32),
                   jax.ShapeDtypeStruct((1, d), F32)),
        grid=(t // tm,),
        in_specs=[row, row, row, row, pl.BlockSpec((tm, 1), lambda i: (i, 0)), par],
        out_specs=(row, row, par, par, par),
        compiler_params=_cparams(("arbitrary",)),
    )(dr2, t1, t2, xh, rstd, g)


def _conv_mid_bwd_rows(u1, proj, du2, ng, nb):
    t = u1.shape[0]
    e = CONV_WIDTH
    tm = min(t, 256)

    rg = 16
    nt = t // tm

    def body(u1_ref, z_ref, d_ref, ng_ref, nb_ref, du1_ref, dz_ref, dng_ref, dnb_ref, dbz_ref, acc_ref):
        i = pl.program_id(0)

        @pl.when(i == 0)
        def _():
            acc_ref[...] = jnp.zeros_like(acc_ref)

        g = ng_ref[...]
        b = nb_ref[...]

        def group(r, carry):
            rows = pl.ds(pl.multiple_of(r * rg, rg), rg)
            u1 = u1_ref[rows, :]
            mu = jnp.mean(u1, axis=-1, keepdims=True)
            xc = u1 - mu
            rstd = lax.rsqrt(jnp.mean(xc * xc, axis=-1, keepdims=True) + LN_EPS)
            xh = xc * rstd
            n = xh * g + b
            sn = _sigmoid(n)
            z = z_ref[rows, :]
            sz = _sigmoid(z)
            du2 = d_ref[rows, :]
            dz = du2 * (n * sn) * (sz * (1.0 + z * (1.0 - sz)))
            dn = du2 * (z * sz) * (sn * (1.0 + n * (1.0 - sn)))
            acc_ref[0:8, :] += _fold_rows(dn * xh)
            acc_ref[8:16, :] += _fold_rows(dn)
            acc_ref[16:24, :] += _fold_rows(dz)
            dz_ref[rows, :] = dz.astype(BF16)
            dxh = dn * g
            du1_ref[rows, :] = rstd * (dxh - jnp.mean(dxh, axis=-1, keepdims=True)
                                       - xh * jnp.mean(dxh * xh, axis=-1, keepdims=True))
            return carry

        lax.fori_loop(0, tm // rg, group, 0, unroll=8)

        @pl.when(i == nt - 1)
        def _():
            dng_ref[...] = jnp.sum(acc_ref[0:8, :], axis=0, keepdims=True)
            dnb_ref[...] = jnp.sum(acc_ref[8:16, :], axis=0, keepdims=True)
            dbz_ref[...] = jnp.sum(acc_ref[16:24, :], axis=0, keepdims=True)

    row = pl.BlockSpec((tm, e), lambda i: (i, 0))
    par = pl.BlockSpec((1, e), lambda i: (0, 0))
    return pl.pallas_call(
        body, name="conv_mid_bwd_rows",
        out_shape=(jax.ShapeDtypeStruct((t, e), F32), jax.ShapeDtypeStruct((t, 3 * e), BF16),
                   jax.ShapeDtypeStruct((1, e), F32), jax.ShapeDtypeStruct((1, e), F32),
                   jax.ShapeDtypeStruct((1, e), F32)),
        grid=(nt,),
        in_specs=[row, pl.BlockSpec((tm, e), lambda i: (i, 2)), row, par, par],
        out_specs=(row, pl.BlockSpec((tm, e), lambda i: (i, 2)), par, par, par),
        scratch_shapes=[pltpu.VMEM((24, e), F32)],
        compiler_params=_cparams(("arbitrary",)),
    )(u1, proj, du2, ng, nb)


def _conv_bwd(du1, proj, cw, dproj):
    t = du1.shape[0]
    e = CONV_WIDTH
    tm = min(t, 128)
    hb = tm // HALO
    nt = t // tm
    nchunk = e // LANES
    last_halo = t // HALO - 1

    def body(d_ref, dn_ref, val_ref, gg_ref, valh_ref, ggh_ref, cw_ref, alias_ref,
             dp_ref, dcw_ref, dcb_ref, dbv_ref, extu_ref, extd_ref, du0_ref, acc_ref, sh_ref):
        i = pl.program_id(0)
        h0 = valh_ref[...] * _sigmoid(ggh_ref[...])
        extu_ref[0:HALO, :] = jnp.where(i > 0, h0, 0.0)
        extu_ref[HALO:, :] = val_ref[...] * _sigmoid(gg_ref[...])
        extd_ref[0:tm, :] = d_ref[...]
        extd_ref[tm:, :] = jnp.where(i < nt - 1, dn_ref[...], 0.0)

        @pl.when(i == 0)
        def _():
            acc_ref[...] = jnp.zeros_like(acc_ref)

        def chunk(c, carry):
            col = pl.multiple_of(c * LANES, LANES)
            d = extd_ref[0:tm, pl.ds(col, LANES)]
            du0 = jnp.zeros((tm, LANES), F32)
            for res, taps in _taps_by_residue(lambda j: CONV_KERNEL - 1 - j):
                span = 8 * max(a for _, a in taps) + tm
                sh_ref[0:span, :] = extd_ref[res:res + span, pl.ds(col, LANES)]
                for j, a in taps:
                    du0 = du0 + cw_ref[j:j + 1, pl.ds(col, LANES)] * sh_ref[8 * a:8 * a + tm, :]
            for res, taps in _taps_by_residue(lambda j: HALO - (CONV_KERNEL - 1) + j):
                span = 8 * max(a for _, a in taps) + tm
                sh_ref[0:span, :] = extu_ref[res:res + span, pl.ds(col, LANES)]
                for j, a in taps:
                    prod = d * sh_ref[8 * a:8 * a + tm, :]
                    acc_ref[j * 8:(j + 1) * 8, pl.ds(col, LANES)] += _fold_rows(prod)
            acc_ref[CONV_KERNEL * 8:(CONV_KERNEL + 1) * 8, pl.ds(col, LANES)] += _fold_rows(d)
            du0_ref[:, pl.ds(col, LANES)] = du0
            return carry

        lax.fori_loop(0, nchunk, chunk, 0)
        kb = CONV_KERNEL * 8
        for r in range(0, tm, 16):
            sg = _sigmoid(gg_ref[r:r + 16, :])
            dval = du0_ref[r:r + 16, :] * sg
            dgg = dval * val_ref[r:r + 16, :] * (1.0 - sg)
            dp_ref[r:r + 16, 0:e] = dval.astype(BF16)
            dp_ref[r:r + 16, e:] = dgg.astype(BF16)
            acc_ref[kb + 8:kb + 16, :] += _fold_rows(dval)
            acc_ref[kb + 16:kb + 24, :] += _fold_rows(dgg)

        @pl.when(i == nt - 1)
        def _():
            for j in range(CONV_KERNEL):
                dcw_ref[j:j + 1, :] = jnp.sum(acc_ref[j * 8:(j + 1) * 8, :], axis=0, keepdims=True)
            dcw_ref[CONV_KERNEL:, :] = jnp.zeros((HALO - CONV_KERNEL, e), F32)
            dcb_ref[...] = jnp.sum(acc_ref[kb:kb + 8, :], axis=0, keepdims=True)
            dbv_ref[:, 0:e] = jnp.sum(acc_ref[kb + 8:kb + 16, :], axis=0, keepdims=True)
            dbv_ref[:, e:] = jnp.sum(acc_ref[kb + 16:kb + 24, :], axis=0, keepdims=True)

    row = lambda c: pl.BlockSpec((tm, e), lambda i: (i, c))
    halo_prev = lambda c: pl.BlockSpec((HALO, e), lambda i: (jnp.maximum(i * hb - 1, 0), c))
    halo_next = pl.BlockSpec((HALO, e), lambda i: (jnp.minimum((i + 1) * hb, last_halo), 0))
    return pl.pallas_call(
        body, name="conv_bwd",
        out_shape=(jax.ShapeDtypeStruct(dproj.shape, BF16), jax.ShapeDtypeStruct((HALO, e), F32),
                   jax.ShapeDtypeStruct((1, e), F32), jax.ShapeDtypeStruct((1, 2 * e), F32)),
        grid=(nt,),
        in_specs=[row(0), halo_next, row(0), row(1), halo_prev(0), halo_prev(1),
                  pl.BlockSpec((HALO, e), lambda i: (0, 0)), pl.BlockSpec(memory_space=pl.ANY)],
        out_specs=(pl.BlockSpec((tm, 2 * e), lambda i: (i, 0)), pl.BlockSpec((HALO, e), lambda i: (0, 0)),
                   pl.BlockSpec((1, e), lambda i: (0, 0)), pl.BlockSpec((1, 2 * e), lambda i: (0, 0))),
        scratch_shapes=[pltpu.VMEM((tm + HALO, e), F32), pltpu.VMEM((tm + HALO, e), F32),
                        pltpu.VMEM((tm, e), F32), pltpu.VMEM(((CONV_KERNEL + 3) * 8, e), F32),
                        pltpu.VMEM((tm + HALO, LANES), F32)],
        input_output_aliases={7: 0},
        compiler_params=_cparams(("arbitrary",)),
    )(du1, du1, proj, proj, proj, proj, cw, dproj)


def _adam_update(g, w, m, v):
    c1 = 1.0 - ADAM_B1 ** ADAM_STEP
    c2 = 1.0 - ADAM_B2 ** ADAM_STEP
    mn = ADAM_B1 * m + (1.0 - ADAM_B1) * g
    vn = ADAM_B2 * v + (1.0 - ADAM_B2) * (g * g)
    delta = -ADAM_LR * ((mn / c1) / (jnp.sqrt(vn / c2) + ADAM_EPS) + ADAM_WD * w)
    return delta, mn, vn


def _adamw(name, gbuf, w, m, v):
    parts = list(gbuf) if isinstance(gbuf, (list, tuple)) else [gbuf]
    npart = len(parts)
    r, c = w.shape
    tr = min(r // npart, 256)
    per_part = r // npart // tr
    assert r == npart * per_part * tr and all(p.shape == (N_DEV, r // npart, c) for p in parts)

    def body(*refs):
        g_refs = refs[:npart]
        w_ref, m_ref, v_ref, go_ref, d_ref, mo_ref, vo_ref = refs[npart:]
        i = pl.program_id(0)

        def run(g_ref):
            g = g_ref[0].astype(F32)
            for k in range(1, N_DEV):
                g = g + g_ref[k].astype(F32)
            go_ref[...] = g
            d_ref[...], mo_ref[...], vo_ref[...] = _adam_update(g, w_ref[...], m_ref[...], v_ref[...])

        if npart == 1:
            run(g_refs[0])
        else:
            for q in range(npart):
                pl.when((i >= q * per_part) & (i < (q + 1) * per_part))(functools.partial(run, g_refs[q]))

    blk = pl.BlockSpec((tr, c), lambda i: (i, 0))
    part_spec = lambda q: pl.BlockSpec((N_DEV, tr, c), lambda i: (0, jnp.clip(i - q * per_part, 0, per_part - 1), 0))
    shp = jax.ShapeDtypeStruct((r, c), F32)
    return pl.pallas_call(
        body, name="adamw_" + name,
        out_shape=(shp, shp, shp, shp),
        grid=(r // tr,),
        in_specs=[part_spec(q) for q in range(npart)] + [blk, blk, blk],
        out_specs=(blk, blk, blk, blk),
        compiler_params=_cparams(("parallel",)),
    )(*parts, w, m, v)


SMALL_TABLE_COLS = 256
SMALL_LAYOUT = {
    'ln_g': (0, 2, 1024), 'ln_b': (8, 2, 1024), 'a_b_in': (16, 1, 768), 'a_conv_b': (24, 1, 256),
    'a_norm_g': (32, 1, 256), 'a_norm_b': (40, 1, 256), 'a_b_out': (48, 1, 128), 'kv_norm_g': (56, 1, 256),
    'b_q_norm_g': (64, 1, 512), 'a_conv_w': (72, CONV_KERNEL, 256),
}
LOSS_ROW = 104
SMALL_TABLE_ROWS = 112
SMALL_NAMES = tuple(SMALL_LAYOUT)


def _small_pieces(name, row):
    r0, _, c = SMALL_LAYOUT[name]
    w = min(c, SMALL_TABLE_COLS)
    per_row = c // w
    return [(r0 + row * per_row + q, q * w, w) for q in range(per_row)]


def _pack_small_grads(dg0, dg1, db0, db1, dbv, dbz, dcb, dng, dnb, dba, dgkv, dgq, dcw, loss):
    e = CONV_WIDTH
    ce = e // N_DEV

    def body(dg0_ref, dg1_ref, db0_ref, db1_ref, dbv_ref, dbz_ref, dcb_ref, dng_ref, dnb_ref, dba_ref, dgkv_ref,
             dgq_ref, dcw_ref, loss_ref, o_ref):
        o_ref[...] = jnp.zeros_like(o_ref)
        for d in range(N_DEV):
            o_ref[d, LOSS_ROW:LOSS_ROW + 1, 0:LANES] = loss_ref[...]

        def put(d, name, row, src_ref, first_col=0):
            for trow, col, w in _small_pieces(name, row):
                o_ref[d, trow:trow + 1, 0:w] = src_ref[:, first_col + col:first_col + col + w]

        for d in range(N_DEV):
            put(d, 'ln_g', 0, dg0_ref)
            put(d, 'ln_g', 1, dg1_ref)
            put(d, 'ln_b', 0, db0_ref)
            put(d, 'ln_b', 1, db1_ref)
            for trow, col, w in _small_pieces('a_b_in', 0):
                gcol = d * 3 * ce + col
                src = dbv_ref[:, gcol:gcol + w] if gcol < 2 * e else dbz_ref[:, gcol - 2 * e:gcol - 2 * e + w]
                o_ref[d, trow:trow + 1, 0:w] = src
            put(d, 'a_conv_b', 0, dcb_ref, d * ce)
            put(d, 'a_norm_g', 0, dng_ref, d * ce)
            put(d, 'a_norm_b', 0, dnb_ref, d * ce)
            put(d, 'a_b_out', 0, dba_ref, d * LANES)
            put(d, 'kv_norm_g', 0, dgkv_ref)
            put(d, 'b_q_norm_g', 0, dgq_ref)
            r0 = SMALL_LAYOUT['a_conv_w'][0]
            o_ref[d, r0:r0 + HALO, 0:ce] = dcw_ref[:, d * ce:(d + 1) * ce]

    return pl.pallas_call(
        body, name="pack_small_grads",
        out_shape=jax.ShapeDtypeStruct((N_DEV, SMALL_TABLE_ROWS, SMALL_TABLE_COLS), F32),
        compiler_params=pltpu.CompilerParams(vmem_limit_bytes=VMEM_LIMIT),
    )(dg0, dg1, db0, db1, dbv, dbz, dcb, dng, dnb, dba, dgkv, dgq, dcw, loss)


def _adamw_small(gtab, ws, ms, vs):
    n = len(SMALL_NAMES)

    def body(*refs):
        g_ref = refs[0]
        w_refs, m_refs, v_refs = refs[1:1 + n], refs[1 + n:1 + 2 * n], refs[1 + 2 * n:1 + 3 * n]
        outs = refs[1 + 3 * n:]

        def summed(r0, r, c):
            g = g_ref[0, r0:r0 + r, 0:c]
            for k in range(1, N_DEV):
                g = g + g_ref[k, r0:r0 + r, 0:c]
            return g

        for i, name in enumerate(SMALL_NAMES):
            r0, r, c = SMALL_LAYOUT[name]
            if c <= SMALL_TABLE_COLS:
                blocks = [(slice(None), summed(r0, r, c))]
            else:
                blocks = [(slice(row, row + 1), jnp.concatenate([summed(tr, 1, w) for tr, _, w in _small_pieces(name, row)], axis=1))
                          for row in range(r)]
            for rows, g in blocks:
                delta, mn, vn = _adam_update(g, w_refs[i][rows, :], m_refs[i][rows, :], v_refs[i][rows, :])
                outs[i][rows, :] = g
                outs[n + i][rows, :] = delta
                outs[2 * n + i][rows, :] = mn
                outs[3 * n + i][rows, :] = vn
        outs[4 * n][...] = summed(LOSS_ROW, 1, LANES)

    shapes = tuple(jax.ShapeDtypeStruct(w.shape, F32) for w in ws)
    res = pl.pallas_call(
        body, name="adamw_small",
        out_shape=shapes * 4 + (jax.ShapeDtypeStruct((1, LANES), F32),),
        compiler_params=pltpu.CompilerParams(vmem_limit_bytes=VMEM_LIMIT),
    )(gtab, *ws, *ms, *vs)
    return res[:n], res[n:2 * n], res[2 * n:3 * n], res[3 * n:4 * n], res[4 * n]


def _mesh_peers():
    x, y, c = lax.axis_index("x"), lax.axis_index("y"), lax.axis_index("c")
    me = 4 * x + 2 * y + c
    peers = []
    for k in range(1, N_DEV):
        px = 1 - x if (k >> 2) & 1 else x
        py = 1 - y if (k >> 1) & 1 else y
        pc = 1 - c if k & 1 else c
        peers.append(((px, py, pc), 4 * px + 2 * py + pc))
    return me, peers


def _all_gather(name, shards):
    na = len(shards)

    def body(*refs):
        srcs, dsts = refs[:na], refs[na:2 * na]
        send_sems, recv_sems, local_sems = refs[2 * na:]
        x, y, c = lax.axis_index("x"), lax.axis_index("y"), lax.axis_index("c")
        me, sibling = (x, y, c), (x, y, 1 - c)
        chips = [(1 - x, y), (x, 1 - y), (1 - x, 1 - y)]
        slot = lambda dev: 4 * dev[0] + 2 * dev[1] + dev[2]

        def copy(a, k, block, to, own=False):
            return pltpu.make_async_remote_copy(
                src_ref=srcs[a] if own else dsts[a].at[slot(block)], dst_ref=dsts[a].at[slot(block)],
                send_sem=send_sems.at[a, k], recv_sem=recv_sems.at[a, k], device_id=to,
                device_id_type=pl.DeviceIdType.MESH)

        local = [pltpu.make_async_copy(srcs[a], dsts[a].at[slot(me)], local_sems.at[a]) for a in range(na)]
        for cp in local:
            cp.start()
        sends = []
        for a in range(na):
            sends.append(copy(a, 0, me, sibling, own=True))
            sends += [copy(a, 1 + j, me, (*chip, c), own=True) for j, chip in enumerate(chips)]
        for cp in sends:
            cp.start()
        for j, chip in enumerate(chips):
            for a in range(na):
                copy(a, 1 + j, (*chip, c), me).wait_recv()
                fwd = copy(a, 4 + j, (*chip, c), sibling)
                fwd.start()
                sends.append(fwd)
        for a in range(na):
            copy(a, 0, sibling, me).wait_recv()
            for j, chip in enumerate(chips):
                copy(a, 4 + j, (*chip, 1 - c), me).wait_recv()
        for cp in sends:
            cp.wait_send()
        for cp in local:
            cp.wait()

    hbm = pl.BlockSpec(memory_space=pl.ANY)
    return pl.pallas_call(
        body, name=name,
        out_shape=tuple(jax.ShapeDtypeStruct((N_DEV,) + s.shape, s.dtype) for s in shards),
        in_specs=[hbm] * na, out_specs=(hbm,) * na,
        scratch_shapes=[pltpu.SemaphoreType.DMA((na, N_DEV - 1)), pltpu.SemaphoreType.DMA((na, N_DEV - 1)),
                        pltpu.SemaphoreType.DMA((na,))],
    )(*shards)


_HBM_SPEC = pl.BlockSpec(memory_space=pltpu.HBM)
_SEM_SPEC = pl.BlockSpec(memory_space=pltpu.SEMAPHORE)


ALL_PEERS = tuple(range(N_DEV - 1))
CHIP_LEVEL_PEERS = (0, 1, 3, 5)


def _split_copies(srcs, lands, send_sems, recv_sems, local_sems, gather, which=ALL_PEERS):
    me, peers = _mesh_peers()
    na = len(srcs)
    mine = lambda a, slot: srcs[a] if gather else srcs[a].at[slot]
    local = [pltpu.make_async_copy(mine(a, me), lands[a].at[me], local_sems.at[a]) for a in range(na)]
    sent, received = [], []
    for k, (dev, pid) in enumerate(peers):
        if k not in which:
            continue
        for a in range(na):
            s = a * (N_DEV - 1) + k
            sent.append(pltpu.make_async_remote_copy(
                src_ref=mine(a, pid), dst_ref=lands[a].at[me], send_sem=send_sems.at[s],
                recv_sem=recv_sems.at[s], device_id=dev, device_id_type=pl.DeviceIdType.MESH))
            received.append(pltpu.make_async_remote_copy(
                src_ref=mine(a, pid), dst_ref=lands[a].at[pid], send_sem=send_sems.at[s],
                recv_sem=recv_sems.at[s], device_id=dev, device_id_type=pl.DeviceIdType.MESH))
    return local, sent, received


def _comm_start(name, srcs, gather, after, which=ALL_PEERS):
    na = len(srcs)
    land_structs = [jax.ShapeDtypeStruct(((N_DEV,) + s.shape) if gather else s.shape, s.dtype) for s in srcs]
    extra = [] if after is None else [after]
    n_in = 2 * na + len(extra)

    def body(*refs):
        srcs_r, lands_r = refs[:na], refs[na:2 * na]
        send_sems, recv_sems, local_sems = refs[n_in:n_in + 3]
        token = refs[-1]
        local, sent, _ = _split_copies(srcs_r, lands_r, send_sems, recv_sems, local_sems, gather, which)
        for cp in local + sent:
            cp.start()
        token[...] = jnp.zeros_like(token)

    sem = pltpu.SemaphoreType.DMA((na * (N_DEV - 1),))
    outs = pl.pallas_call(
        body, name=name,
        out_shape=(sem, sem, pltpu.SemaphoreType.DMA((na,)),
                   *[pltpu.HBM(s.shape, s.dtype) for s in srcs],
                   *[pltpu.HBM(s.shape, s.dtype) for s in land_structs],
                   jax.ShapeDtypeStruct((8, LANES), F32)),
        in_specs=[_HBM_SPEC] * (2 * na) + [pl.BlockSpec(memory_space=pl.ANY)] * len(extra),
        out_specs=(_SEM_SPEC, _SEM_SPEC, _SEM_SPEC, *([_HBM_SPEC] * (2 * na)), pl.BlockSpec(memory_space=pltpu.VMEM)),
        input_output_aliases={i: 3 + i for i in range(2 * na)},
        compiler_params=pltpu.CompilerParams(has_side_effects=pltpu.SideEffectType.DATAFLOW_SIDE_EFFECTING),
    )(*[pltpu.with_memory_space_constraint(s, pltpu.HBM) for s in srcs],
      *[pltpu.with_memory_space_constraint(lax.empty(s.shape, s.dtype), pltpu.HBM) for s in land_structs],
      *extra)
    return (outs[:3], outs[3:3 + na], outs[3 + na:3 + 2 * na]), outs[-1]


def _comm_wait(name, handles, gather, after, which=ALL_PEERS):
    (send_sems, recv_sems, local_sems), srcs, lands = handles
    na = len(srcs)

    def body(*refs):
        srcs_r, lands_r = refs[:na], refs[na:2 * na]
        send_r, recv_r, local_r = refs[2 * na:2 * na + 3]
        local, sent, received = _split_copies(srcs_r, lands_r, send_r, recv_r, local_r, gather, which)
        for cp in sent:
            cp.wait_send()
        for cp in received:
            cp.wait_recv()
        for cp in local:
            cp.wait()

    outs = pl.pallas_call(
        body, name=name,
        out_shape=(*[pltpu.HBM(s.shape, s.dtype) for s in srcs], *[pltpu.HBM(s.shape, s.dtype) for s in lands]),
        in_specs=[_HBM_SPEC] * (2 * na) + [_SEM_SPEC] * 3 + [pl.BlockSpec(memory_space=pl.ANY)],
        out_specs=tuple([_HBM_SPEC] * (2 * na)),
        input_output_aliases={i: i for i in range(2 * na)},
        compiler_params=pltpu.CompilerParams(has_side_effects=pltpu.SideEffectType.DATAFLOW_SIDE_EFFECTING),
    )(*srcs, *lands, send_sems, recv_sems, local_sems, after)
    return outs[na:]


def _forward_to_sibling(name, lands):
    na = len(lands)

    def body(*refs):
        ins, outs = refs[:na], refs[na:2 * na]
        send_sems, recv_sems = refs[2 * na:]
        x, y, c = lax.axis_index("x"), lax.axis_index("y"), lax.axis_index("c")
        sibling = (x, y, 1 - c)
        chips = [(1 - x, y), (x, 1 - y), (1 - x, 1 - y)]
        slot = lambda cx, cy, cc: 4 * cx + 2 * cy + cc
        sends = []
        for j, (cx, cy) in enumerate(chips):
            for a in range(na):
                cp = pltpu.make_async_remote_copy(
                    src_ref=ins[a].at[slot(cx, cy, c)], dst_ref=outs[a].at[slot(cx, cy, c)],
                    send_sem=send_sems.at[a, j], recv_sem=recv_sems.at[a, j], device_id=sibling,
                    device_id_type=pl.DeviceIdType.MESH)
                cp.start()
                sends.append(cp)
        for j, (cx, cy) in enumerate(chips):
            for a in range(na):
                pltpu.make_async_remote_copy(
                    src_ref=ins[a].at[slot(cx, cy, 1 - c)], dst_ref=outs[a].at[slot(cx, cy, 1 - c)],
                    send_sem=send_sems.at[a, j], recv_sem=recv_sems.at[a, j], device_id=sibling,
                    device_id_type=pl.DeviceIdType.MESH).wait_recv()
        for cp in sends:
            cp.wait_send()

    hbm = pl.BlockSpec(memory_space=pl.ANY)
    return pl.pallas_call(
        body, name=name,
        out_shape=tuple(jax.ShapeDtypeStruct(s.shape, s.dtype) for s in lands),
        in_specs=[hbm] * na, out_specs=(hbm,) * na,
        input_output_aliases={a: a for a in range(na)},
        scratch_shapes=[pltpu.SemaphoreType.DMA((na, 3)), pltpu.SemaphoreType.DMA((na, 3))],
    )(*lands)


def _prep_weights(w):
    e = CONV_WIDTH
    p = {}
    if 'a_w_in' in w:
        if w['a_w_in'].shape == (N_DEV, D_MODEL, 3 * e // N_DEV):
            p['a_w_in3'] = w['a_w_in']
        else:
            p['a_w_in3'] = w['a_w_in'].reshape(D_MODEL, N_DEV, 3 * e // N_DEV).transpose(1, 0, 2)
        p['a_b_in'] = w['a_b_in'].reshape(1, 3 * e)
        p['a_conv_w'] = jnp.pad(w['a_conv_w'].reshape(CONV_KERNEL, e), ((0, HALO - CONV_KERNEL), (0, 0)))
        p['a_conv_b'] = w['a_conv_b'].reshape(1, e)
        p['a_norm_g'] = w['a_norm_g'].reshape(1, e)
        p['a_norm_b'] = w['a_norm_b'].reshape(1, e)
        p['a_b_out'] = w['a_b_out'].reshape(1, D_MODEL)
        p['kv_norm_g'] = w['kv_norm_g'].reshape(1, KV_LORA_RANK)
        p['b_q_norm_g'] = w['b_q_norm_g'].reshape(1, Q_LORA_RANK)
        p['ln_g'] = w['ln_g']
        p['ln_b'] = w['ln_b']
    if 'a_w_out' in w:
        p['a_w_out'] = w['a_w_out'].reshape(e, D_MODEL)
        p['kv_w_down'] = jnp.pad(w['kv_w_down'], ((0, 0), (0, KV_LORA_RANK + LANES - w['kv_w_down'].shape[1])))
        p['kv_w_ukv'] = jnp.concatenate([w['kv_w_uk'], w['kv_w_uv']], axis=2).reshape(KV_LORA_RANK, N_HEADS * HEAD_PAD)
        p['b_w_in'] = w['b_w_in'].reshape(D_MODEL, -1)
        uq = w['b_w_uq'].reshape(Q_LORA_RANK, N_HEADS, QK_NOPE_DIM + QK_ROPE_DIM)
        p['b_w_uq'] = jnp.pad(uq, ((0, 0), (0, 0), (0, HEAD_PAD - uq.shape[2]))).reshape(Q_LORA_RANK, N_HEADS * HEAD_PAD)
        p['b_w_out'] = w['b_w_out'].reshape(N_HEADS * V_HEAD_DIM, D_MODEL)
    return p


def _local_step(x, positions, target, comm):
    t = x.shape[0]
    e = CONV_WIDTH
    freqs = ROPE_THETA ** (-jnp.arange(0, QK_ROPE_DIM, 2, dtype=F32) / QK_ROPE_DIM)
    freq_row = jnp.concatenate([freqs, freqs, jnp.zeros((LANES - QK_ROPE_DIM,), F32)]).reshape(1, LANES)
    rc, rs, xb = _rope_tables(positions.reshape(t, 1), freq_row, x, comm.start_token())
    p = comm.first_weights(rc)
    ln_g0, ln_b0 = p['ln_g'][0:1], p['ln_b'][0:1]
    ln_g1, ln_b1 = p['ln_g'][1:2], p['ln_b'][1:2]

    proj = _matmul(xb, p['a_w_in3'], bias=p['a_b_in'], name="a_in", b_blocked=True, bm=2048,
                   after=comm.first_after())
    u1, u2 = _conv_mid_fwd(proj, p['a_conv_w'], p['a_conv_b'], p['a_norm_g'], p['a_norm_b'])
    p = {**p, **comm.rest_weights(u2)}
    y = _matmul(u2, p['a_w_out'], bias=p['a_b_out'], name="a_out", bk=2048)
    h1, h1b, xh1, rstd1 = _ln_res_fwd(x, y, ln_g0, ln_b0)
    ckv = _matmul(h1b, p['kv_w_down'], name="kv_down")
    ckn, kr = _kv_mid_fwd(ckv, p['kv_norm_g'], rc, rs)
    kv = _matmul(ckn, p['kv_w_ukv'], name="kv_up", out_dtype=BF16)
    projb = _matmul(h1b, p['b_w_in'], name="b_in", bn=1280)
    cqn = _q_norm_fwd(projb, p['b_q_norm_g'])
    qcat = _q_up_rope(cqn, p['b_w_uq'], rc, rs)
    o, o2, lse = _flash_fwd(qcat, kv, kr, projb)
    yb = _matmul(o2, p['b_w_out'], name="b_out", bk=2048)
    loss, dr2, dr2b, dg1, db1 = _final_ln_loss(h1, yb, ln_g1, ln_b1, target)

    g = {}
    g['b_w_out'] = _matmul(o2, dr2b, trans_a=True, name="d_b_out", out_dtype=BF16, bm=512, bk=t)
    do2 = _matmul(dr2b, p['b_w_out'], trans_b=True, name="d_o2")
    dob, dprojb, dl = _gate_bwd(do2, o, projb)
    dq2, dkv, dkr = _flash_bwd(qcat, kv, kr, dob, lse, dl, rc, rs)
    duq = _matmul(cqn, dq2, trans_a=True, name="d_q_up", out_dtype=BF16, bk=t)
    dcqn = _matmul(dq2, p['b_w_uq'], trans_b=True, name="d_cqn", bk=N_HEADS * HEAD_PAD)
    dprojb, dgq = _q_norm_bwd(dcqn, projb, p['b_q_norm_g'], dprojb)
    g['b_w_in'] = _matmul(h1b, dprojb, trans_a=True, name="d_b_in", bn=640, bk=t, out_dtype=BF16)
    t1 = _matmul(dprojb, p['b_w_in'], trans_b=True, name="d_h1_b", bk=dprojb.shape[1])
    dukv = _matmul(ckn, dkv, trans_a=True, name="d_kv_up", out_dtype=BF16, bk=t)
    dckn = _matmul(dkv, p['kv_w_ukv'], trans_b=True, name="d_ckn", bk=N_HEADS * HEAD_PAD)
    dckv, dgkv = _kv_mid_bwd(dckn, ckv, dkr, p['kv_norm_g'], rc, rs)
    ddown = _matmul(h1b, dckv, trans_a=True, name="d_kv_down", out_dtype=BF16, bm=512, bk=t)
    g['b_w_out'] = g['b_w_out'].reshape(N_DEV, -1, D_MODEL)
    g['b_w_in'] = g['b_w_in'].reshape(D_MODEL, N_DEV, -1).transpose(1, 0, 2)
    g['kv_w_down'] = ddown[:, :KV_LORA_RANK + QK_ROPE_DIM].reshape(N_DEV, -1, KV_LORA_RANK + QK_ROPE_DIM)
    dukv = dukv.reshape(KV_LORA_RANK, N_HEADS, HEAD_PAD)
    g['kv_w_uk'] = dukv[:, :, :QK_NOPE_DIM].reshape(N_DEV, -1, QK_NOPE_DIM)
    g['kv_w_uv'] = dukv[:, :, QK_NOPE_DIM:].reshape(N_DEV, -1, V_HEAD_DIM)
    g['b_w_uq'] = duq.reshape(Q_LORA_RANK, N_HEADS, HEAD_PAD)[:, :, :QK_NOPE_DIM + QK_ROPE_DIM].reshape(
        N_DEV, -1, QK_NOPE_DIM + QK_ROPE_DIM)
    sent1 = comm.send_group1(g)
    t2 = _matmul(dckv, p['kv_w_down'], trans_b=True, name="d_h1_kv", after=sent1)
    dr1, dr1b, dg0, db0, dba_out = _ln0_bwd(dr2, t1, t2, xh1, rstd1, ln_g0)
    dw_out = _matmul(u2, dr1b, trans_a=True, name="d_a_out", out_dtype=BF16, bm=512, bk=t).reshape(N_DEV, -1, D_MODEL)
    sent1b = comm.send_group1b({'a_w_out': dw_out})
    du2 = _matmul(dr1b, p['a_w_out'], trans_b=True, name="d_u2", after=sent1b)
    du1, dproj, dng, dnb, dbz = _conv_mid_bwd_rows(u1, proj, du2, p['a_norm_g'], p['a_norm_b'])
    dproj, dcw, dcb, dbv = _conv_bwd(du1, proj, p['a_conv_w'], dproj)
    half = D_MODEL // 2
    dw_lo = _matmul(xb, dproj, trans_a=True, name="d_a_in_lo", out_dtype=BF16, bm=half, bn=3 * e // N_DEV, bk=t,
                    out_blocked=True, a_cols=(0, half))
    small = (dg0, dg1, db0, db1, dbv, dbz, dcb, dng, dnb, dba_out, dgkv, dgq, dcw, loss)
    sent2 = comm.send_group2(dw_lo, small)
    dw_hi = _matmul(xb, dproj, trans_a=True, name="d_a_in_hi", out_dtype=BF16, bm=half, bn=3 * e // N_DEV, bk=t,
                    out_blocked=True, after=sent2, a_cols=(half, half))
    sent3 = comm.send_group3(dw_hi)
    grad_x = _grad_x(dproj, p['a_w_in3'], dr1, sent3)
    return loss, grad_x


def _grad_x(dproj, w3, dr1, after):
    t, d = dr1.shape
    nblk, _, cb = w3.shape
    tm = min(t, 512)
    extra = [] if after is None else [after]

    def body(a_ref, w_ref, dr_ref, *rest):
        o_ref = rest[-1]
        acc = ALPHA * dr_ref[...]
        for k in range(nblk):
            acc = acc + lax.dot_general(a_ref[:, k * cb:(k + 1) * cb], w_ref[k], NT, preferred_element_type=F32)
        o_ref[...] = acc

    row = pl.BlockSpec((tm, d), lambda i: (i, 0))
    return pl.pallas_call(
        body, name="grad_x",
        out_shape=jax.ShapeDtypeStruct((t, d), F32),
        grid=(t // tm,),
        in_specs=[pl.BlockSpec((tm, nblk * cb), lambda i: (i, 0)),
                  pl.BlockSpec((nblk, d, cb), lambda i: (0, 0, 0), pipeline_mode=pl.Buffered(1)), row]
                 + [pl.BlockSpec(memory_space=pl.ANY)] * len(extra),
        out_specs=row,
        compiler_params=_cparams(("parallel",)),
    )(dproj, w3, dr1, *extra)


def _shard_2d(a):
    return a.reshape(-1, a.shape[-1])


def _gathered_to_full(name, blocks):
    if name == 'a_w_in':
        return blocks
    if name == 'b_w_in':
        return blocks.transpose(1, 0, 2).reshape(D_MODEL, -1)
    if name == 'a_conv_w':
        return blocks.transpose(1, 0, 2).reshape(CONV_KERNEL, -1)
    if name in ('a_b_in', 'a_conv_b', 'a_norm_g', 'a_norm_b', 'a_b_out'):
        return blocks.reshape(-1)
    if name in ('kv_w_uk', 'kv_w_uv'):
        return blocks.reshape(KV_LORA_RANK, N_HEADS, -1)
    if name == 'b_w_uq':
        return blocks.reshape(Q_LORA_RANK, N_HEADS, -1)
    return blocks.reshape(-1, blocks.shape[-1])


def kernel(x, positions, ln_g, ln_b, a_w_in, a_b_in, a_conv_w, a_conv_b, a_norm_g, a_norm_b, a_w_out, a_b_out, kv_w_down, kv_norm_g, kv_w_uk, kv_w_uv, b_w_in, b_q_norm_g, b_w_uq, b_w_out, loss_target, m_ln_g, m_ln_b, m_a_w_in, m_a_b_in, m_a_conv_w, m_a_conv_b, m_a_norm_g, m_a_norm_b, m_a_w_out, m_a_b_out, m_kv_w_down, m_kv_norm_g, m_kv_w_uk, m_kv_w_uv, m_b_w_in, m_b_q_norm_g, m_b_w_uq, m_b_w_out, v_ln_g, v_ln_b, v_a_w_in, v_a_b_in, v_a_conv_w, v_a_conv_b, v_a_norm_g, v_a_norm_b, v_a_w_out, v_a_b_out, v_kv_w_down, v_kv_norm_g, v_kv_w_uk, v_kv_w_uv, v_b_w_in, v_b_q_norm_g, v_b_w_uq, v_b_w_out):
    w = dict(ln_g=ln_g, ln_b=ln_b, a_w_in=a_w_in, a_b_in=a_b_in, a_conv_w=a_conv_w, a_conv_b=a_conv_b,
             a_norm_g=a_norm_g, a_norm_b=a_norm_b, a_w_out=a_w_out, a_b_out=a_b_out, kv_w_down=kv_w_down,
             kv_norm_g=kv_norm_g, kv_w_uk=kv_w_uk, kv_w_uv=kv_w_uv, b_w_in=b_w_in, b_q_norm_g=b_q_norm_g,
             b_w_uq=b_w_uq, b_w_out=b_w_out)
    m = dict(ln_g=m_ln_g, ln_b=m_ln_b, a_w_in=m_a_w_in, a_b_in=m_a_b_in, a_conv_w=m_a_conv_w, a_conv_b=m_a_conv_b,
             a_norm_g=m_a_norm_g, a_norm_b=m_a_norm_b, a_w_out=m_a_w_out, a_b_out=m_a_b_out, kv_w_down=m_kv_w_down,
             kv_norm_g=m_kv_norm_g, kv_w_uk=m_kv_w_uk, kv_w_uv=m_kv_w_uv, b_w_in=m_b_w_in, b_q_norm_g=m_b_q_norm_g,
             b_w_uq=m_b_w_uq, b_w_out=m_b_w_out)
    v = dict(ln_g=v_ln_g, ln_b=v_ln_b, a_w_in=v_a_w_in, a_b_in=v_a_b_in, a_conv_w=v_a_conv_w, a_conv_b=v_a_conv_b,
             a_norm_g=v_a_norm_g, a_norm_b=v_a_norm_b, a_w_out=v_a_w_out, a_b_out=v_a_b_out, kv_w_down=v_kv_w_down,
             kv_norm_g=v_kv_norm_g, kv_w_uk=v_kv_w_uk, kv_w_uv=v_kv_w_uv, b_w_in=v_b_w_in, b_q_norm_g=v_b_q_norm_g,
             b_w_uq=v_b_w_uq, b_w_out=v_b_w_out)

    small_flat = jnp.concatenate([w[n].reshape(-1) for n in SMALL_WEIGHTS]).reshape(1, -1)
    rest_names = [n for n in MATMUL_WEIGHTS if n != 'a_w_in']
    group1 = ('b_w_out', 'b_w_uq', 'b_w_in', 'kv_w_uk', 'kv_w_uv', 'kv_w_down')
    group1b = ('a_w_out',)

    class Comm:
        def __init__(self):
            self.first, self.first_token = _comm_start(
                "gather_first_start", [_shard_2d(w['a_w_in']).astype(BF16), small_flat], True, None, CHIP_LEVEL_PEERS)

        def start_token(self):
            return self.first_token

        def first_weights(self, after):
            lands = _comm_wait("gather_first_wait", self.first, True, after, CHIP_LEVEL_PEERS)
            ga = _forward_to_sibling("gather_first_forward", list(lands))
            full = {'a_w_in': ga[0]}
            gs = ga[1].reshape(N_DEV, -1)
            off = 0
            for n in SMALL_WEIGHTS:
                size = math.prod(w[n].shape)
                full[n] = _gathered_to_full(n, gs[:, off:off + size].reshape((N_DEV,) + _shard_2d(w[n]).shape))
                off += size
            for n in REPLICATED:
                full[n] = w[n]
            dense = lambda a: a if a.shape[-1] % LANES == 0 else a.reshape(-1, LANES)
            self.rest, self.rest_token = _comm_start(
                "gather_rest_start", [dense(_shard_2d(w[n]).astype(BF16)) for n in rest_names], True, ga[0])
            return _prep_weights(full)

        def first_after(self):
            return self.rest_token

        def rest_weights(self, after):
            lands = _comm_wait("gather_rest_wait", self.rest, True, after)
            blocks = {n: lands[i].reshape((N_DEV,) + _shard_2d(w[n]).shape) for i, n in enumerate(rest_names)}
            return _prep_weights({n: _gathered_to_full(n, blocks[n]) for n in rest_names})

        def send_group1(self, g):
            self.g1, token = _comm_start("exchange_g1_start", [g[n] for n in group1], False, None)
            return token

        def send_group1b(self, g):
            self.g1b, token = _comm_start("exchange_g1b_start", [g[n] for n in group1b], False, None)
            return token

        def send_group2(self, dw_lo, small):
            self.g2, token = _comm_start("exchange_g2_start", [dw_lo, _pack_small_grads(*small)], False, None)
            return token

        def send_group3(self, dw_hi):
            self.g3, token = _comm_start("exchange_g3_start", [dw_hi], False, None)
            return token

    comm = Comm()

    _, grad_x = _local_step(x[0], positions[0], loss_target[0], comm)

    res = {}
    recv1 = _comm_wait("exchange_g1_wait", comm.g1, False, grad_x)
    for i, n in enumerate(group1):
        res[n] = _adamw(n, recv1[i], _shard_2d(w[n]), _shard_2d(m[n]), _shard_2d(v[n]))
    recv1b = _comm_wait("exchange_g1b_wait", comm.g1b, False, res[group1[-1]][0])
    for i, n in enumerate(group1b):
        res[n] = _adamw(n, recv1b[i], _shard_2d(w[n]), _shard_2d(m[n]), _shard_2d(v[n]))
    recv2 = _comm_wait("exchange_g2_wait", comm.g2, False, res[group1b[-1]][0])
    recv3 = _comm_wait("exchange_g3_wait", comm.g3, False, recv2[0])
    res['a_w_in'] = _adamw('a_w_in', [recv2[0], recv3[0]], _shard_2d(w['a_w_in']), _shard_2d(m['a_w_in']),
                           _shard_2d(v['a_w_in']))
    two_d = lambda d: [_shard_2d(d[n]) if d[n].ndim > 1 else d[n].reshape(1, -1) for n in SMALL_NAMES]
    sg, sd, sm, sv, loss = _adamw_small(recv2[-1], two_d(w), two_d(m), two_d(v))
    for i, n in enumerate(SMALL_NAMES):
        res[n] = (sg[i], sd[i], sm[i], sv[i])
    outs = [[res[n][j].reshape(w[n].shape) for n in WEIGHT_NAMES] for j in range(4)]
    return (loss[0, 0], grad_x[None], *outs[0], *outs[1], *outs[2], *outs[3])
```

```python
import functools
import math

import jax
import jax.numpy as jnp
from jax import lax
from jax.experimental import pallas as pl
from jax.experimental.pallas import tpu as pltpu

F32 = jnp.float32
BF16 = jnp.bfloat16

D_MODEL = 1024
DEPTH = 2
CONV_WIDTH = 2 * D_MODEL
CONV_KERNEL = 31
N_HEADS = 16
QK_NOPE_DIM = 128
QK_ROPE_DIM = 64
V_HEAD_DIM = 128
KV_LORA_RANK = D_MODEL // 4
Q_LORA_RANK = D_MODEL // 2
ROPE_THETA = 10000.0
LN_EPS = 1e-5
RMS_EPS = 1e-6
MASK_VALUE = -1e30
ALPHA = (2.0 * DEPTH) ** 0.25
ATTN_SCALE = 1.0 / math.sqrt(QK_NOPE_DIM + QK_ROPE_DIM)

ADAM_LR = 0.001
ADAM_B1 = 0.9
ADAM_B2 = 0.999
ADAM_EPS = 1e-08
ADAM_WD = 0.01
ADAM_STEP = 10

N_DEV = 8
LANES = 128
HEAD_PAD = 256
HALO = 32
VMEM_LIMIT = 56 * 1024 * 1024

WEIGHT_NAMES = ['ln_g', 'ln_b', 'a_w_in', 'a_b_in', 'a_conv_w', 'a_conv_b', 'a_norm_g', 'a_norm_b',
                'a_w_out', 'a_b_out', 'kv_w_down', 'kv_norm_g', 'kv_w_uk', 'kv_w_uv', 'b_w_in',
                'b_q_norm_g', 'b_w_uq', 'b_w_out']
REPLICATED = ('ln_g', 'ln_b', 'kv_norm_g', 'b_q_norm_g')
MATMUL_WEIGHTS = ('a_w_in', 'a_w_out', 'kv_w_down', 'kv_w_uk', 'kv_w_uv', 'b_w_in', 'b_w_uq', 'b_w_out')
SMALL_WEIGHTS = ('a_b_in', 'a_conv_w', 'a_conv_b', 'a_norm_g', 'a_norm_b', 'a_b_out')

NN = (((1,), (0,)), ((), ()))
NT = (((1,), (1,)), ((), ()))
TN = (((0,), (0,)), ((), ()))


def _cparams(sem):
    return pltpu.CompilerParams(dimension_semantics=sem, vmem_limit_bytes=VMEM_LIMIT)


def _sigmoid(x):
    return 0.5 * jnp.tanh(0.5 * x) + 0.5


def _fold_rows(x):
    parts = [x[r:r + 8] for r in range(0, x.shape[0], 8)]
    while len(parts) > 1:
        parts = [parts[i] + parts[i + 1] for i in range(0, len(parts) - 1, 2)] + ([parts[-1]] if len(parts) % 2 else [])
    return parts[0]


def _taps_by_residue(offset_of):
    groups = {}
    for j in range(CONV_KERNEL):
        off = offset_of(j)
        groups.setdefault(off % 8, []).append((j, off // 8))
    return sorted(groups.items())


def _swap_rope_halves(t):
    lane = lax.broadcasted_iota(jnp.int32, t.shape, 1)
    return jnp.where(lane < QK_ROPE_DIM // 2, pltpu.roll(t, LANES - QK_ROPE_DIM // 2, 1),
                     pltpu.roll(t, QK_ROPE_DIM // 2, 1))


def _matmul(a, b, *, name, bias=None, trans_a=False, trans_b=False, out_dtype=F32, bm=1024, bn=1024, bk=1024,
            b_blocked=False, out_blocked=False, after=None, a_cols=None):
    m_off = 0
    if trans_a:
        kdim, m = a.shape
        if a_cols is not None:
            m_off, m = a_cols
    else:
        m, kdim = a.shape
    if b_blocked and trans_b:
        n, k2 = b.shape[1], b.shape[0] * b.shape[2]
        bk = b.shape[2]
    elif b_blocked:
        k2, n = b.shape[1], b.shape[0] * b.shape[2]
        bn = b.shape[2]
    elif trans_b:
        n, k2 = b.shape
    else:
        k2, n = b.shape
    assert kdim == k2, (a.shape, b.shape)
    bm, bn, bk = min(bm, m), min(bn, n), min(bk, kdim)
    assert m % bm == 0 and n % bn == 0 and kdim % bk == 0, (name, m, n, kdim, bm, bn, bk)
    nk = kdim // bk
    dims = (((0 if trans_a else 1,), (1 if trans_b else 0,)), ((), ()))
    has_bias = bias is not None

    def body(*refs):
        refs = list(refs)
        a_ref, b_ref = refs[:2]
        del refs[:2]
        bias_ref = refs.pop(0) if has_bias else None
        if after is not None:
            refs.pop(0)
        o_ref = refs.pop(0)
        rest = refs
        prod = lax.dot_general(a_ref[...], b_ref[...], dims, preferred_element_type=F32)

        def finish(acc):
            if has_bias:
                acc = acc + bias_ref[...]
            o_ref[...] = acc.astype(out_dtype)

        if nk == 1:
            finish(prod)
        else:
            acc_ref = rest[0]
            k = pl.program_id(2)

            @pl.when(k == 0)
            def _():
                acc_ref[...] = prod

            @pl.when(k > 0)
            def _():
                acc_ref[...] += prod

            @pl.when(k == nk - 1)
            def _():
                finish(acc_ref[...])

    assert m_off % bm == 0
    a_first = m_off // bm
    a_spec = (pl.BlockSpec((bk, bm), lambda i, j, k: (k, i + a_first)) if trans_a
              else pl.BlockSpec((bm, bk), lambda i, j, k: (i, k)))
    if b_blocked and trans_b:
        b_spec = pl.BlockSpec((None, bn, bk), lambda i, j, k: (k, j, 0))
    elif b_blocked:
        b_spec = pl.BlockSpec((None, bk, bn), lambda i, j, k: (j, k, 0))
    elif trans_b:
        b_spec = pl.BlockSpec((bn, bk), lambda i, j, k: (j, k))
    else:
        b_spec = pl.BlockSpec((bk, bn), lambda i, j, k: (k, j))
    in_specs = [a_spec, b_spec]
    args = [a, b]
    if has_bias:
        in_specs.append(pl.BlockSpec((1, bn), lambda i, j, k: (0, j)))
        args.append(bias)
    if after is not None:
        in_specs.append(pl.BlockSpec(memory_space=pl.ANY))
        args.append(after)
    if out_blocked:
        out_struct = jax.ShapeDtypeStruct((n // bn, m, bn), out_dtype)
        out_spec = pl.BlockSpec((None, bm, bn), lambda i, j, k: (j, i, 0))
    else:
        out_struct = jax.ShapeDtypeStruct((m, n), out_dtype)
        out_spec = pl.BlockSpec((bm, bn), lambda i, j, k: (i, j))
    return pl.pallas_call(
        body, name=name,
        out_shape=out_struct,
        grid=(m // bm, n // bn, nk),
        in_specs=in_specs,
        out_specs=out_spec,
        scratch_shapes=[pltpu.VMEM((bm, bn), F32)] if nk > 1 else [],
        compiler_params=_cparams(("parallel", "parallel", "arbitrary")),
    )(*args)


def _rope_tables(pos_col, freq_row, x, after):
    t, d = x.shape
    tm = min(t, 512)
    extra = [] if after is None else [after]

    def body(pos_ref, f_ref, x_ref, *rest):
        c_ref, s_ref, xb_ref = rest[-3:]
        ang = pos_ref[...].astype(F32) * f_ref[...]
        lane = lax.broadcasted_iota(jnp.int32, ang.shape, 1)
        cos = jnp.cos(ang)
        sin = jnp.sin(ang)
        c_ref[...] = jnp.where(lane < QK_ROPE_DIM, cos, 0.0)
        s_ref[...] = jnp.where(lane < QK_ROPE_DIM // 2, -sin, jnp.where(lane < QK_ROPE_DIM, sin, 0.0))
        xb_ref[...] = x_ref[...].astype(BF16)

    lane_blk = pl.BlockSpec((tm, LANES), lambda i: (i, 0))
    row = pl.BlockSpec((tm, d), lambda i: (i, 0))
    return pl.pallas_call(
        body, name="rope_tables",
        out_shape=(jax.ShapeDtypeStruct((t, LANES), F32), jax.ShapeDtypeStruct((t, LANES), F32),
                   jax.ShapeDtypeStruct((t, d), BF16)),
        grid=(t // tm,),
        in_specs=[pl.BlockSpec((tm, 1), lambda i: (i, 0)), pl.BlockSpec((1, LANES), lambda i: (0, 0)), row]
                 + [pl.BlockSpec(memory_space=pl.ANY)] * len(extra),
        out_specs=(lane_blk, lane_blk, row),
        compiler_params=_cparams(("parallel",)),
    )(pos_col, freq_row, x, *extra)


def _conv_mid_fwd(proj, cw, cb, ng, nb):
    t = proj.shape[0]
    e = CONV_WIDTH
    tm = min(t, 128)
    hb = tm // HALO
    nchunk = e // LANES

    def body(val_ref, gg_ref, z_ref, valh_ref, ggh_ref, cw_ref, cb_ref, ng_ref, nb_ref, u1_ref, u2_ref, ext_ref, sh_ref):
        i = pl.program_id(0)
        u0 = val_ref[...] * _sigmoid(gg_ref[...])
        h0 = valh_ref[...] * _sigmoid(ggh_ref[...])
        ext_ref[0:HALO, :] = jnp.where(i > 0, h0, 0.0)
        ext_ref[HALO:, :] = u0

        def chunk(c, carry):
            col = pl.multiple_of(c * LANES, LANES)
            acc = jnp.zeros((tm, LANES), F32)
            for res, taps in _taps_by_residue(lambda j: HALO - (CONV_KERNEL - 1) + j):
                span = 8 * max(a for _, a in taps) + tm
                sh_ref[0:span, :] = ext_ref[res:res + span, pl.ds(col, LANES)]
                for j, a in taps:
                    acc = acc + cw_ref[j:j + 1, pl.ds(col, LANES)] * sh_ref[8 * a:8 * a + tm, :]
            u1_ref[:, pl.ds(col, LANES)] = acc + cb_ref[:, pl.ds(col, LANES)]
            return carry

        lax.fori_loop(0, nchunk, chunk, 0)
        g = ng_ref[...]
        b = nb_ref[...]
        for r in range(0, tm, 16):
            u1 = u1_ref[r:r + 16, :]
            mu = jnp.mean(u1, axis=-1, keepdims=True)
            xc = u1 - mu
            var = jnp.mean(xc * xc, axis=-1, keepdims=True)
            n = xc * lax.rsqrt(var + LN_EPS) * g + b
            z = z_ref[r:r + 16, :]
            u2_ref[r:r + 16, :] = ((n * _sigmoid(n)) * (z * _sigmoid(z))).astype(BF16)

    row = lambda c: pl.BlockSpec((tm, e), lambda i: (i, c))
    halo = lambda c: pl.BlockSpec((HALO, e), lambda i: (jnp.maximum(i * hb - 1, 0), c))
    par = lambda r: pl.BlockSpec((r, e), lambda i: (0, 0))
    return pl.pallas_call(
        body, name="conv_mid_fwd",
        out_shape=(jax.ShapeDtypeStruct((t, e), F32), jax.ShapeDtypeStruct((t, e), BF16)),
        grid=(t // tm,),
        in_specs=[row(0), row(1), row(2), halo(0), halo(1), par(HALO), par(1), par(1), par(1)],
        out_specs=(pl.BlockSpec((tm, e), lambda i: (i, 0)), pl.BlockSpec((tm, e), lambda i: (i, 0))),
        scratch_shapes=[pltpu.VMEM((tm + HALO, e), F32), pltpu.VMEM((tm + HALO, LANES), F32)],
        compiler_params=_cparams(("parallel",)),
    )(proj, proj, proj, proj, proj, cw, cb, ng, nb)


def _ln_res_fwd(x, u, w_out, b_out, g, b):
    t, d = x.shape
    ku = u.shape[1]
    tm = min(t, 512)

    def body(x_ref, u_ref, w_ref, bo_ref, g_ref, b_ref, h_ref, hb_ref, xh_ref, rs_ref):
        y = lax.dot_general(u_ref[...], w_ref[...], NN, preferred_element_type=F32) + bo_ref[...]
        r = ALPHA * x_ref[...] + y
        mu = jnp.mean(r, axis=-1, keepdims=True)
        xc = r - mu
        var = jnp.mean(xc * xc, axis=-1, keepdims=True)
        rstd = lax.rsqrt(var + LN_EPS)
        xh = xc * rstd
        h = xh * g_ref[...] + b_ref[...]
        h_ref[...] = h
        hb_ref[...] = h.astype(BF16)
        xh_ref[...] = xh
        rs_ref[...] = rstd

    row = pl.BlockSpec((tm, d), lambda i: (i, 0))
    par = pl.BlockSpec((1, d), lambda i: (0, 0))
    return pl.pallas_call(
        body, name="ln0_fwd",
        out_shape=(jax.ShapeDtypeStruct((t, d), F32), jax.ShapeDtypeStruct((t, d), BF16),
                   jax.ShapeDtypeStruct((t, d), F32), jax.ShapeDtypeStruct((t, 1), F32)),
        grid=(t // tm,),
        in_specs=[row, pl.BlockSpec((tm, ku), lambda i: (i, 0)),
                  pl.BlockSpec((ku, d), lambda i: (0, 0), pipeline_mode=pl.Buffered(1)), par, par, par],
        out_specs=(row, row, row, pl.BlockSpec((tm, 1), lambda i: (i, 0))),
        compiler_params=_cparams(("parallel",)),
    )(x, u, w_out, b_out, g, b)


def _kv_mid_fwd(ckv, g, rc, rs):
    t = ckv.shape[0]
    tm = min(t, 512)
    r = KV_LORA_RANK

    def body(ckv_ref, g_ref, c_ref, s_ref, ckn_ref, kr_ref):
        c = ckv_ref[:, 0:r]
        ms = jnp.mean(c * c, axis=-1, keepdims=True)
        ckn_ref[...] = (c * lax.rsqrt(ms + RMS_EPS) * g_ref[...]).astype(BF16)
        tr = ckv_ref[:, r:r + LANES]
        kr_ref[...] = (tr * c_ref[...] + _swap_rope_halves(tr) * s_ref[...]).astype(BF16)

    return pl.pallas_call(
        body, name="kv_mid_fwd",
        out_shape=(jax.ShapeDtypeStruct((t, r), BF16), jax.ShapeDtypeStruct((t, LANES), BF16)),
        grid=(t // tm,),
        in_specs=[pl.BlockSpec((tm, r + LANES), lambda i: (i, 0)), pl.BlockSpec((1, r), lambda i: (0, 0)),
                  pl.BlockSpec((tm, LANES), lambda i: (i, 0)), pl.BlockSpec((tm, LANES), lambda i: (i, 0))],
        out_specs=(pl.BlockSpec((tm, r), lambda i: (i, 0)), pl.BlockSpec((tm, LANES), lambda i: (i, 0))),
        compiler_params=_cparams(("parallel",)),
    )(ckv, g, rc, rs)


def _q_norm_fwd(projb, g):
    t = projb.shape[0]
    tm = min(t, 512)
    r = Q_LORA_RANK

    def body(c_ref, g_ref, o_ref):
        c = c_ref[...]
        ms = jnp.mean(c * c, axis=-1, keepdims=True)
        o_ref[...] = (c * lax.rsqrt(ms + RMS_EPS) * g_ref[...]).astype(BF16)

    return pl.pallas_call(
        body, name="q_norm_fwd",
        out_shape=jax.ShapeDtypeStruct((t, r), BF16),
        grid=(t // tm,),
        in_specs=[pl.BlockSpec((tm, r), lambda i: (i, 0)), pl.BlockSpec((1, r), lambda i: (0, 0))],
        out_specs=pl.BlockSpec((tm, r), lambda i: (i, 0)),
        compiler_params=_cparams(("parallel",)),
    )(projb, g)


def _q_up_rope(cqn, wuq, rc, rs):
    t, r = cqn.shape
    w = wuq.shape[1]
    tm = min(t, 1024)
    bn = 4 * HEAD_PAD

    def body(a_ref, b_ref, c_ref, s_ref, o_ref):
        q = lax.dot_general(a_ref[...], b_ref[...], NN, preferred_element_type=F32)
        c = c_ref[...]
        s = s_ref[...]
        for h in range(bn // HEAD_PAD):
            a = h * HEAD_PAD
            o_ref[:, a:a + LANES] = q[:, a:a + LANES].astype(BF16)
            tr = q[:, a + LANES:a + 2 * LANES]
            o_ref[:, a + LANES:a + 2 * LANES] = (tr * c + _swap_rope_halves(tr) * s).astype(BF16)

    return pl.pallas_call(
        body, name="q_up_rope",
        out_shape=jax.ShapeDtypeStruct((t, w), BF16),
        grid=(t // tm, w // bn),
        in_specs=[pl.BlockSpec((tm, r), lambda i, j: (i, 0)), pl.BlockSpec((r, bn), lambda i, j: (0, j)),
                  pl.BlockSpec((tm, LANES), lambda i, j: (i, 0)), pl.BlockSpec((tm, LANES), lambda i, j: (i, 0))],
        out_specs=pl.BlockSpec((tm, bn), lambda i, j: (i, j)),
        compiler_params=_cparams(("parallel", "parallel")),
    )(cqn, wuq, rc, rs)


def _flash_fwd(qcat, kv, kr, projb):
    t = qcat.shape[0]
    tq = min(t, 512)
    nq = t // tq
    exp2_scale = ATTN_SCALE * math.log2(math.e)
    z_first = Q_LORA_RANK // LANES

    def body(q_ref, kv_ref, kr_ref, z_ref, o_ref, o2_ref, lse_ref, kcat_ref):
        kcat_ref[:, 0:LANES] = kv_ref[:, 0:LANES]
        kcat_ref[:, LANES:] = kr_ref[...]
        rows = lax.broadcasted_iota(jnp.int32, (tq, tq), 0)
        cols = lax.broadcasted_iota(jnp.int32, (tq, tq), 1)
        for i in range(nq):
            a = i * tq
            q = q_ref[a:a + tq, :]
            sd = lax.dot_general(q, kcat_ref[a:a + tq, :], NT, preferred_element_type=F32)
            sd = jnp.where(cols <= rows, sd, MASK_VALUE)
            m = jnp.max(sd, axis=1, keepdims=True)
            if i > 0:
                sp = lax.dot_general(q, kcat_ref[0:a, :], NT, preferred_element_type=F32)
                m = jnp.maximum(m, jnp.max(sp, axis=1, keepdims=True))
            pd = jnp.exp2((sd - m) * exp2_scale)
            l = jnp.sum(pd, axis=1, keepdims=True)
            acc = lax.dot_general(pd.astype(BF16), kv_ref[a:a + tq, LANES:], NN, preferred_element_type=F32)
            if i > 0:
                pp = jnp.exp2((sp - m) * exp2_scale)
                l = l + jnp.sum(pp, axis=1, keepdims=True)
                acc = acc + lax.dot_general(pp.astype(BF16), kv_ref[0:a, LANES:], NN, preferred_element_type=F32)
            o = acc / l
            z = z_ref[a:a + tq, :]
            o_ref[a:a + tq, :] = o
            o2_ref[a:a + tq, :] = (o * (z * _sigmoid(z))).astype(BF16)
            lse_ref[a:a + tq, :] = jnp.broadcast_to(m * ATTN_SCALE + jnp.log(l), (tq, LANES))

    hw = N_HEADS * LANES
    head256 = pl.BlockSpec((t, HEAD_PAD), lambda h: (0, h))
    head128 = pl.BlockSpec((t, LANES), lambda h: (0, h))
    return pl.pallas_call(
        body, name="flash_fwd",
        out_shape=(jax.ShapeDtypeStruct((t, hw), F32), jax.ShapeDtypeStruct((t, hw), BF16),
                   jax.ShapeDtypeStruct((t, hw), F32)),
        grid=(N_HEADS,),
        in_specs=[head256, head256, pl.BlockSpec((t, LANES), lambda h: (0, 0), pipeline_mode=pl.Buffered(1)),
                  pl.BlockSpec((t, LANES), lambda h: (0, z_first + h))],
        out_specs=(head128, head128, head128),
        scratch_shapes=[pltpu.VMEM((t, HEAD_PAD), BF16)],
        compiler_params=_cparams(("parallel",)),
    )(qcat, kv, kr, projb)


def _final_ln_loss(h1, o2, w_out, g, b, target):
    t, d = h1.shape
    ku = o2.shape[1]
    tm = min(t, 512)

    def body(h_ref, u_ref, w_ref, g_ref, b_ref, t_ref, loss_ref, dr_ref, drb_ref, dg_ref, db_ref):
        i = pl.program_id(0)
        r = ALPHA * h_ref[...] + lax.dot_general(u_ref[...], w_ref[...], NN, preferred_element_type=F32)
        mu = jnp.mean(r, axis=-1, keepdims=True)
        xc = r - mu
        var = jnp.mean(xc * xc, axis=-1, keepdims=True)
        rstd = lax.rsqrt(var + LN_EPS)
        xh = xc * rstd
        g = g_ref[...]
        diff = xh * g + b_ref[...] - t_ref[...]
        part = 0.5 * jnp.sum(jnp.mean(diff * diff, axis=-1, keepdims=True))
        dh = diff * (1.0 / d)
        dgp = jnp.sum(dh * xh, axis=0, keepdims=True)
        dbp = jnp.sum(dh, axis=0, keepdims=True)

        @pl.when(i == 0)
        def _():
            loss_ref[...] = jnp.zeros_like(loss_ref)
            dg_ref[...] = jnp.zeros_like(dg_ref)
            db_ref[...] = jnp.zeros_like(db_ref)

        loss_ref[...] += jnp.full(loss_ref.shape, part, F32)
        dg_ref[...] += dgp
        db_ref[...] += dbp
        dxh = dh * g
        dr = rstd * (dxh - jnp.mean(dxh, axis=-1, keepdims=True) - xh * jnp.mean(dxh * xh, axis=-1, keepdims=True))
        dr_ref[...] = dr
        drb_ref[...] = dr.astype(BF16)

    row = pl.BlockSpec((tm, d), lambda i: (i, 0))
    par = pl.BlockSpec((1, d), lambda i: (0, 0))
    return pl.pallas_call(
        body, name="final_ln_loss",
        out_shape=(jax.ShapeDtypeStruct((1, LANES), F32), jax.ShapeDtypeStruct((t, d), F32),
                   jax.ShapeDtypeStruct((t, d), BF16), jax.ShapeDtypeStruct((1, d), F32),
                   jax.ShapeDtypeStruct((1, d), F32)),
        grid=(t // tm,),
        in_specs=[row, pl.BlockSpec((tm, ku), lambda i: (i, 0)),
                  pl.BlockSpec((ku, d), lambda i: (0, 0), pipeline_mode=pl.Buffered(1)), par, par, row],
        out_specs=(pl.BlockSpec((1, LANES), lambda i: (0, 0)), row, row, par, par),
        compiler_params=_cparams(("arbitrary",)),
    )(h1, o2, w_out, g, b, target)


def _gate_bwd(do2, o, projb):
    t, hw = o.shape
    tm = min(t, 512)
    bw = Q_LORA_RANK
    nb = hw // bw
    wb = projb.shape[1]

    def body(d_ref, o_ref, z_ref, dob_ref, dz_ref, dl_ref):
        z = z_ref[...]
        sg = _sigmoid(z)
        d2 = d_ref[...]
        o = o_ref[...]
        do = d2 * (z * sg)
        dob_ref[...] = do.astype(BF16)
        dz_ref[...] = (d2 * o * (sg * (1.0 + z * (1.0 - sg)))).astype(BF16)
        prod = do * o
        for hh in range(bw // LANES):
            dsum = jnp.sum(prod[:, hh * LANES:(hh + 1) * LANES], axis=1, keepdims=True)
            dl_ref[:, hh * LANES:(hh + 1) * LANES] = jnp.broadcast_to(dsum, (tm, LANES))

    blk = pl.BlockSpec((tm, bw), lambda i, j: (i, j))
    blk1 = pl.BlockSpec((tm, bw), lambda i, j: (i, j + 1))
    return pl.pallas_call(
        body, name="gate_bwd",
        out_shape=(jax.ShapeDtypeStruct((t, hw), BF16), jax.ShapeDtypeStruct((t, wb), BF16),
                   jax.ShapeDtypeStruct((t, hw), F32)),
        grid=(t // tm, nb),
        in_specs=[blk, blk, blk1],
        out_specs=(blk, blk1, blk),
        compiler_params=_cparams(("parallel", "parallel")),
    )(do2, o, projb)


def _flash_bwd(qcat, kv, kr, dob, lse, dl, rc, rs):
    t = qcat.shape[0]
    tq = min(t, 512)
    nq = t // tq
    q_chunk = 2 * tq
    log2e = math.log2(math.e)
    exp2_scale = ATTN_SCALE * log2e
    once = pl.Buffered(1)

    def body(q_ref, kv_ref, kr_ref, do_ref, lse_ref, dl_ref, c_ref, s_ref,
             dq_ref, dkv_ref, dkr_ref, kcat_ref, dqa_ref, dka_ref, dva_ref):
        h = pl.program_id(0)
        kcat_ref[:, 0:LANES] = kv_ref[:, 0:LANES]
        kcat_ref[:, LANES:] = kr_ref[...]
        dqa_ref[...] = jnp.zeros_like(dqa_ref)

        @pl.when(h == 0)
        def _():
            dkr_ref[...] = jnp.zeros_like(dkr_ref)

        rows = lax.broadcasted_iota(jnp.int32, (tq, tq), 0)
        cols = lax.broadcasted_iota(jnp.int32, (tq, tq), 1)

        def block(qs, n, ks, masked):
            q = q_ref[qs:qs + n, :]
            kc = kcat_ref[ks:ks + tq, :]
            s = lax.dot_general(q, kc, NT, preferred_element_type=F32)
            if masked:
                s = jnp.where(cols <= rows, s, MASK_VALUE)
            p = jnp.exp2(s * exp2_scale - lse_ref[qs:qs + n, 0:1] * log2e)
            do = do_ref[qs:qs + n, :]
            dp = lax.dot_general(do, kv_ref[ks:ks + tq, LANES:], NT, preferred_element_type=F32)
            ds = (p * (dp - dl_ref[qs:qs + n, 0:1])).astype(BF16)
            dva_ref[...] += lax.dot_general(p.astype(BF16), do, TN, preferred_element_type=F32)
            dka_ref[...] += lax.dot_general(ds, q, TN, preferred_element_type=F32)
            dqa_ref[qs:qs + n, :] += lax.dot_general(ds, kc, NN, preferred_element_type=F32)

        for kb in range(nq):
            a = kb * tq
            dka_ref[...] = jnp.zeros_like(dka_ref)
            dva_ref[...] = jnp.zeros_like(dva_ref)
            block(a, tq, a, True)
            qs = a + tq
            while qs < t:
                n = min(q_chunk, t - qs)
                block(qs, n, a, False)
                qs += n
            dk = dka_ref[...] * ATTN_SCALE
            dkv_ref[a:a + tq, 0:LANES] = dk[:, 0:LANES].astype(BF16)
            dkv_ref[a:a + tq, LANES:] = dva_ref[...].astype(BF16)
            dkr_ref[a:a + tq, :] += dk[:, LANES:]

        dq = dqa_ref[...] * ATTN_SCALE
        dq_ref[:, 0:LANES] = dq[:, 0:LANES].astype(BF16)
        d2 = dq[:, LANES:]
        dq_ref[:, LANES:] = (d2 * c_ref[...] - _swap_rope_halves(d2) * s_ref[...]).astype(BF16)

    hp = N_HEADS * HEAD_PAD
    head256 = pl.BlockSpec((t, HEAD_PAD), lambda h: (0, h))
    head128 = pl.BlockSpec((t, LANES), lambda h: (0, h))
    const128 = pl.BlockSpec((t, LANES), lambda h: (0, 0), pipeline_mode=once)
    return pl.pallas_call(
        body, name="flash_bwd",
        out_shape=(jax.ShapeDtypeStruct((t, hp), BF16), jax.ShapeDtypeStruct((t, hp), BF16),
                   jax.ShapeDtypeStruct((t, LANES), F32)),
        grid=(N_HEADS,),
        in_specs=[head256, head256, const128, head128, head128, head128, const128, const128],
        out_specs=(head256, head256, pl.BlockSpec((t, LANES), lambda h: (0, 0))),
        scratch_shapes=[pltpu.VMEM((t, HEAD_PAD), BF16), pltpu.VMEM((t, HEAD_PAD), F32),
                        pltpu.VMEM((tq, HEAD_PAD), F32), pltpu.VMEM((tq, LANES), F32)],
        compiler_params=_cparams(("arbitrary",)),
    )(qcat, kv, kr, dob, lse, dl, rc, rs)


def _q_norm_bwd(dcqn, projb, g, dprojb):
    t = projb.shape[0]
    tm = min(t, 512)
    r = Q_LORA_RANK

    def body(d_ref, c_ref, g_ref, alias_ref, o_ref, dg_ref):
        i = pl.program_id(0)
        c = c_ref[...]
        d = d_ref[...]
        rr = lax.rsqrt(jnp.mean(c * c, axis=-1, keepdims=True) + RMS_EPS)
        xh = c * rr

        @pl.when(i == 0)
        def _():
            dg_ref[...] = jnp.zeros_like(dg_ref)

        dg_ref[...] += jnp.sum(d * xh, axis=0, keepdims=True)
        dxh = d * g_ref[...]
        o_ref[...] = (rr * (dxh - xh * jnp.mean(dxh * xh, axis=-1, keepdims=True))).astype(BF16)

    blk = pl.BlockSpec((tm, r), lambda i: (i, 0))
    par = pl.BlockSpec((1, r), lambda i: (0, 0))
    return pl.pallas_call(
        body, name="q_norm_bwd",
        out_shape=(jax.ShapeDtypeStruct(dprojb.shape, BF16), jax.ShapeDtypeStruct((1, r), F32)),
        grid=(t // tm,),
        in_specs=[blk, blk, par, pl.BlockSpec(memory_space=pl.ANY)],
        out_specs=(blk, par),
        input_output_aliases={3: 0},
        compiler_params=_cparams(("arbitrary",)),
    )(dcqn, projb, g, dprojb)


def _kv_mid_bwd(dckn, ckv, dkr, g, rc, rs):
    t = ckv.shape[0]
    tm = min(t, 512)
    r = KV_LORA_RANK

    def body(d_ref, ckv_ref, dkr_ref, g_ref, c_ref, s_ref, o_ref, dg_ref):
        i = pl.program_id(0)
        c = ckv_ref[:, 0:r]
        d = d_ref[...]
        rr = lax.rsqrt(jnp.mean(c * c, axis=-1, keepdims=True) + RMS_EPS)
        xh = c * rr

        @pl.when(i == 0)
        def _():
            dg_ref[...] = jnp.zeros_like(dg_ref)

        dg_ref[...] += jnp.sum(d * xh, axis=0, keepdims=True)
        dxh = d * g_ref[...]
        o_ref[:, 0:r] = (rr * (dxh - xh * jnp.mean(dxh * xh, axis=-1, keepdims=True))).astype(BF16)
        dk = dkr_ref[...]
        o_ref[:, r:] = (dk * c_ref[...] - _swap_rope_halves(dk) * s_ref[...]).astype(BF16)

    return pl.pallas_call(
        body, name="kv_mid_bwd",
        out_shape=(jax.ShapeDtypeStruct((t, r + LANES), BF16), jax.ShapeDtypeStruct((1, r), F32)),
        grid=(t // tm,),
        in_specs=[pl.BlockSpec((tm, r), lambda i: (i, 0)), pl.BlockSpec((tm, r + LANES), lambda i: (i, 0)),
                  pl.BlockSpec((tm, LANES), lambda i: (i, 0)), pl.BlockSpec((1, r), lambda i: (0, 0)),
                  pl.BlockSpec((tm, LANES), lambda i: (i, 0)), pl.BlockSpec((tm, LANES), lambda i: (i, 0))],
        out_specs=(pl.BlockSpec((tm, r + LANES), lambda i: (i, 0)), pl.BlockSpec((1, r), lambda i: (0, 0))),
        compiler_params=_cparams(("arbitrary",)),
    )(dckn, ckv, dkr, g, rc, rs)


def _ln0_bwd(dr2, t1, t2, xh, rstd, g):
    t, d = dr2.shape
    tm = min(t, 256)

    def body(a_ref, b_ref, c_ref, xh_ref, rs_ref, g_ref, dr_ref, drb_ref, dg_ref, db_ref, dsum_ref):
        i = pl.program_id(0)
        dh = ALPHA * a_ref[...] + b_ref[...] + c_ref[...]
        xh = xh_ref[...]

        @pl.when(i == 0)
        def _():
            dg_ref[...] = jnp.zeros_like(dg_ref)
            db_ref[...] = jnp.zeros_like(db_ref)
            dsum_ref[...] = jnp.zeros_like(dsum_ref)

        dg_ref[...] += jnp.sum(dh * xh, axis=0, keepdims=True)
        db_ref[...] += jnp.sum(dh, axis=0, keepdims=True)
        dxh = dh * g_ref[...]
        dr = rs_ref[...] * (dxh - jnp.mean(dxh, axis=-1, keepdims=True) - xh * jnp.mean(dxh * xh, axis=-1, keepdims=True))
        dr_ref[...] = dr
        drb_ref[...] = dr.astype(BF16)
        dsum_ref[...] += jnp.sum(dr, axis=0, keepdims=True)

    row = pl.BlockSpec((tm, d), lambda i: (i, 0))
    par = pl.BlockSpec((1, d), lambda i: (0, 0))
    return pl.pallas_call(
        body, name="ln0_bwd",
        out_shape=(jax.ShapeDtypeStruct((t, d), F32), jax.ShapeDtypeStruct((t, d), BF16),
                   jax.ShapeDtypeStruct((1, d), F32), jax.ShapeDtypeStruct((1, d), F32),
                   jax.ShapeDtypeStruct((1, d), F32)),
        grid=(t // tm,),
        in_specs=[row, row, row, row, pl.BlockSpec((tm, 1), lambda i: (i, 0)), par],
        out_specs=(row, row, par, par, par),
        compiler_params=_cparams(("arbitrary",)),
    )(dr2, t1, t2, xh, rstd, g)


def _conv_mid_bwd_rows(u1, proj, du2, ng, nb):
    t = u1.shape[0]
    e = CONV_WIDTH
    tm = min(t, 256)

    rg = 16
    nt = t // tm

    def body(u1_ref, z_ref, d_ref, ng_ref, nb_ref, du1_ref, dz_ref, dng_ref, dnb_ref, dbz_ref, acc_ref):
        i = pl.program_id(0)

        @pl.when(i == 0)
        def _():
            acc_ref[...] = jnp.zeros_like(acc_ref)

        g = ng_ref[...]
        b = nb_ref[...]

        def group(r, carry):
            rows = pl.ds(pl.multiple_of(r * rg, rg), rg)
            u1 = u1_ref[rows, :]
            mu = jnp.mean(u1, axis=-1, keepdims=True)
            xc = u1 - mu
            rstd = lax.rsqrt(jnp.mean(xc * xc, axis=-1, keepdims=True) + LN_EPS)
            xh = xc * rstd
            n = xh * g + b
            sn = _sigmoid(n)
            z = z_ref[rows, :]
            sz = _sigmoid(z)
            du2 = d_ref[rows, :]
            dz = du2 * (n * sn) * (sz * (1.0 + z * (1.0 - sz)))
            dn = du2 * (z * sz) * (sn * (1.0 + n * (1.0 - sn)))
            acc_ref[0:8, :] += _fold_rows(dn * xh)
            acc_ref[8:16, :] += _fold_rows(dn)
            acc_ref[16:24, :] += _fold_rows(dz)
            dz_ref[rows, :] = dz.astype(BF16)
            dxh = dn * g
            du1_ref[rows, :] = rstd * (dxh - jnp.mean(dxh, axis=-1, keepdims=True)
                                       - xh * jnp.mean(dxh * xh, axis=-1, keepdims=True))
            return carry

        lax.fori_loop(0, tm // rg, group, 0, unroll=8)

        @pl.when(i == nt - 1)
        def _():
            dng_ref[...] = jnp.sum(acc_ref[0:8, :], axis=0, keepdims=True)
            dnb_ref[...] = jnp.sum(acc_ref[8:16, :], axis=0, keepdims=True)
            dbz_ref[...] = jnp.sum(acc_ref[16:24, :], axis=0, keepdims=True)

    row = pl.BlockSpec((tm, e), lambda i: (i, 0))
    par = pl.BlockSpec((1, e), lambda i: (0, 0))
    return pl.pallas_call(
        body, name="conv_mid_bwd_rows",
        out_shape=(jax.ShapeDtypeStruct((t, e), F32), jax.ShapeDtypeStruct((t, 3 * e), BF16),
                   jax.ShapeDtypeStruct((1, e), F32), jax.ShapeDtypeStruct((1, e), F32),
                   jax.ShapeDtypeStruct((1, e), F32)),
        grid=(nt,),
        in_specs=[row, pl.BlockSpec((tm, e), lambda i: (i, 2)), row, par, par],
        out_specs=(row, pl.BlockSpec((tm, e), lambda i: (i, 2)), par, par, par),
        scratch_shapes=[pltpu.VMEM((24, e), F32)],
        compiler_params=_cparams(("arbitrary",)),
    )(u1, proj, du2, ng, nb)


def _conv_bwd(du1, proj, cw, dproj):
    t = du1.shape[0]
    e = CONV_WIDTH
    tm = min(t, 128)
    hb = tm // HALO
    nt = t // tm
    nchunk = e // LANES
    last_halo = t // HALO - 1

    def body(d_ref, dn_ref, val_ref, gg_ref, valh_ref, ggh_ref, cw_ref, alias_ref,
             dp_ref, dcw_ref, dcb_ref, dbv_ref, extu_ref, extd_ref, du0_ref, acc_ref, sh_ref):
        i = pl.program_id(0)
        h0 = valh_ref[...] * _sigmoid(ggh_ref[...])
        extu_ref[0:HALO, :] = jnp.where(i > 0, h0, 0.0)
        extu_ref[HALO:, :] = val_ref[...] * _sigmoid(gg_ref[...])
        extd_ref[0:tm, :] = d_ref[...]
        extd_ref[tm:, :] = jnp.where(i < nt - 1, dn_ref[...], 0.0)

        @pl.when(i == 0)
        def _():
            acc_ref[...] = jnp.zeros_like(acc_ref)

        def chunk(c, carry):
            col = pl.multiple_of(c * LANES, LANES)
            d = extd_ref[0:tm, pl.ds(col, LANES)]
            du0 = jnp.zeros((tm, LANES), F32)
            for res, taps in _taps_by_residue(lambda j: CONV_KERNEL - 1 - j):
                span = 8 * max(a for _, a in taps) + tm
                sh_ref[0:span, :] = extd_ref[res:res + span, pl.ds(col, LANES)]
                for j, a in taps:
                    du0 = du0 + cw_ref[j:j + 1, pl.ds(col, LANES)] * sh_ref[8 * a:8 * a + tm, :]
            for res, taps in _taps_by_residue(lambda j: HALO - (CONV_KERNEL - 1) + j):
                span = 8 * max(a for _, a in taps) + tm
                sh_ref[0:span, :] = extu_ref[res:res + span, pl.ds(col, LANES)]
                for j, a in taps:
                    prod = d * sh_ref[8 * a:8 * a + tm, :]
                    acc_ref[j * 8:(j + 1) * 8, pl.ds(col, LANES)] += _fold_rows(prod)
            acc_ref[CONV_KERNEL * 8:(CONV_KERNEL + 1) * 8, pl.ds(col, LANES)] += _fold_rows(d)
            du0_ref[:, pl.ds(col, LANES)] = du0
            return carry

        lax.fori_loop(0, nchunk, chunk, 0)
        kb = CONV_KERNEL * 8
        for r in range(0, tm, 16):
            sg = _sigmoid(gg_ref[r:r + 16, :])
            dval = du0_ref[r:r + 16, :] * sg
            dgg = dval * val_ref[r:r + 16, :] * (1.0 - sg)
            dp_ref[r:r + 16, 0:e] = dval.astype(BF16)
            dp_ref[r:r + 16, e:] = dgg.astype(BF16)
            acc_ref[kb + 8:kb + 16, :] += _fold_rows(dval)
            acc_ref[kb + 16:kb + 24, :] += _fold_rows(dgg)

        @pl.when(i == nt - 1)
        def _():
            for j in range(CONV_KERNEL):
                dcw_ref[j:j + 1, :] = jnp.sum(acc_ref[j * 8:(j + 1) * 8, :], axis=0, keepdims=True)
            dcw_ref[CONV_KERNEL:, :] = jnp.zeros((HALO - CONV_KERNEL, e), F32)
            dcb_ref[...] = jnp.sum(acc_ref[kb:kb + 8, :], axis=0, keepdims=True)
            dbv_ref[:, 0:e] = jnp.sum(acc_ref[kb + 8:kb + 16, :], axis=0, keepdims=True)
            dbv_ref[:, e:] = jnp.sum(acc_ref[kb + 16:kb + 24, :], axis=0, keepdims=True)

    row = lambda c: pl.BlockSpec((tm, e), lambda i: (i, c))
    halo_prev = lambda c: pl.BlockSpec((HALO, e), lambda i: (jnp.maximum(i * hb - 1, 0), c))
    halo_next = pl.BlockSpec((HALO, e), lambda i: (jnp.minimum((i + 1) * hb, last_halo), 0))
    return pl.pallas_call(
        body, name="conv_bwd",
        out_shape=(jax.ShapeDtypeStruct(dproj.shape, BF16), jax.ShapeDtypeStruct((HALO, e), F32),
                   jax.ShapeDtypeStruct((1, e), F32), jax.ShapeDtypeStruct((1, 2 * e), F32)),
        grid=(nt,),
        in_specs=[row(0), halo_next, row(0), row(1), halo_prev(0), halo_prev(1),
                  pl.BlockSpec((HALO, e), lambda i: (0, 0)), pl.BlockSpec(memory_space=pl.ANY)],
        out_specs=(pl.BlockSpec((tm, 2 * e), lambda i: (i, 0)), pl.BlockSpec((HALO, e), lambda i: (0, 0)),
                   pl.BlockSpec((1, e), lambda i: (0, 0)), pl.BlockSpec((1, 2 * e), lambda i: (0, 0))),
        scratch_shapes=[pltpu.VMEM((tm + HALO, e), F32), pltpu.VMEM((tm + HALO, e), F32),
                        pltpu.VMEM((tm, e), F32), pltpu.VMEM(((CONV_KERNEL + 3) * 8, e), F32),
                        pltpu.VMEM((tm + HALO, LANES), F32)],
        input_output_aliases={7: 0},
        compiler_params=_cparams(("arbitrary",)),
    )(du1, du1, proj, proj, proj, proj, cw, dproj)


def _adam_update(g, w, m, v):
    c1 = 1.0 - ADAM_B1 ** ADAM_STEP
    c2 = 1.0 - ADAM_B2 ** ADAM_STEP
    mn = ADAM_B1 * m + (1.0 - ADAM_B1) * g
    vn = ADAM_B2 * v + (1.0 - ADAM_B2) * (g * g)
    delta = -ADAM_LR * ((mn / c1) / (jnp.sqrt(vn / c2) + ADAM_EPS) + ADAM_WD * w)
    return delta, mn, vn


def _adamw(name, gbuf, w, m, v):
    parts = list(gbuf) if isinstance(gbuf, (list, tuple)) else [gbuf]
    npart = len(parts)
    r, c = w.shape
    tr = min(r // npart, 256)
    per_part = r // npart // tr
    assert r == npart * per_part * tr and all(p.shape == (N_DEV, r // npart, c) for p in parts)

    def body(*refs):
        g_refs = refs[:npart]
        w_ref, m_ref, v_ref, go_ref, d_ref, mo_ref, vo_ref = refs[npart:]
        i = pl.program_id(0)

        def run(g_ref):
            g = g_ref[0].astype(F32)
            for k in range(1, N_DEV):
                g = g + g_ref[k].astype(F32)
            go_ref[...] = g
            d_ref[...], mo_ref[...], vo_ref[...] = _adam_update(g, w_ref[...], m_ref[...], v_ref[...])

        if npart == 1:
            run(g_refs[0])
        else:
            for q in range(npart):
                pl.when((i >= q * per_part) & (i < (q + 1) * per_part))(functools.partial(run, g_refs[q]))

    blk = pl.BlockSpec((tr, c), lambda i: (i, 0))
    part_spec = lambda q: pl.BlockSpec((N_DEV, tr, c), lambda i: (0, jnp.clip(i - q * per_part, 0, per_part - 1), 0))
    shp = jax.ShapeDtypeStruct((r, c), F32)
    return pl.pallas_call(
        body, name="adamw_" + name,
        out_shape=(shp, shp, shp, shp),
        grid=(r // tr,),
        in_specs=[part_spec(q) for q in range(npart)] + [blk, blk, blk],
        out_specs=(blk, blk, blk, blk),
        compiler_params=_cparams(("parallel",)),
    )(*parts, w, m, v)


SMALL_TABLE_COLS = 256
SMALL_LAYOUT = {
    'ln_g': (0, 2, 1024), 'ln_b': (8, 2, 1024), 'a_b_in': (16, 1, 768), 'a_conv_b': (24, 1, 256),
    'a_norm_g': (32, 1, 256), 'a_norm_b': (40, 1, 256), 'a_b_out': (48, 1, 128), 'kv_norm_g': (56, 1, 256),
    'b_q_norm_g': (64, 1, 512), 'a_conv_w': (72, CONV_KERNEL, 256),
}
LOSS_ROW = 104
SMALL_TABLE_ROWS = 112
SMALL_NAMES = tuple(SMALL_LAYOUT)


def _small_pieces(name, row):
    r0, _, c = SMALL_LAYOUT[name]
    w = min(c, SMALL_TABLE_COLS)
    per_row = c // w
    return [(r0 + row * per_row + q, q * w, w) for q in range(per_row)]


def _pack_small_grads(dg0, dg1, db0, db1, dbv, dbz, dcb, dng, dnb, dba, dgkv, dgq, dcw, loss):
    e = CONV_WIDTH
    ce = e // N_DEV

    def body(dg0_ref, dg1_ref, db0_ref, db1_ref, dbv_ref, dbz_ref, dcb_ref, dng_ref, dnb_ref, dba_ref, dgkv_ref,
             dgq_ref, dcw_ref, loss_ref, o_ref):
        o_ref[...] = jnp.zeros_like(o_ref)
        for d in range(N_DEV):
            o_ref[d, LOSS_ROW:LOSS_ROW + 1, 0:LANES] = loss_ref[...]

        def put(d, name, row, src_ref, first_col=0):
            for trow, col, w in _small_pieces(name, row):
                o_ref[d, trow:trow + 1, 0:w] = src_ref[:, first_col + col:first_col + col + w]

        for d in range(N_DEV):
            put(d, 'ln_g', 0, dg0_ref)
            put(d, 'ln_g', 1, dg1_ref)
            put(d, 'ln_b', 0, db0_ref)
            put(d, 'ln_b', 1, db1_ref)
            for trow, col, w in _small_pieces('a_b_in', 0):
                gcol = d * 3 * ce + col
                src = dbv_ref[:, gcol:gcol + w] if gcol < 2 * e else dbz_ref[:, gcol - 2 * e:gcol - 2 * e + w]
                o_ref[d, trow:trow + 1, 0:w] = src
            put(d, 'a_conv_b', 0, dcb_ref, d * ce)
            put(d, 'a_norm_g', 0, dng_ref, d * ce)
            put(d, 'a_norm_b', 0, dnb_ref, d * ce)
            put(d, 'a_b_out', 0, dba_ref, d * LANES)
            put(d, 'kv_norm_g', 0, dgkv_ref)
            put(d, 'b_q_norm_g', 0, dgq_ref)
            r0 = SMALL_LAYOUT['a_conv_w'][0]
            o_ref[d, r0:r0 + HALO, 0:ce] = dcw_ref[:, d * ce:(d + 1) * ce]

    return pl.pallas_call(
        body, name="pack_small_grads",
        out_shape=jax.ShapeDtypeStruct((N_DEV, SMALL_TABLE_ROWS, SMALL_TABLE_COLS), F32),
        compiler_params=pltpu.CompilerParams(vmem_limit_bytes=VMEM_LIMIT),
    )(dg0, dg1, db0, db1, dbv, dbz, dcb, dng, dnb, dba, dgkv, dgq, dcw, loss)


def _adamw_small(gtab, ws, ms, vs):
    n = len(SMALL_NAMES)

    def body(*refs):
        g_ref = refs[0]
        w_refs, m_refs, v_refs = refs[1:1 + n], refs[1 + n:1 + 2 * n], refs[1 + 2 * n:1 + 3 * n]
        outs = refs[1 + 3 * n:]

        def summed(r0, r, c):
            g = g_ref[0, r0:r0 + r, 0:c]
            for k in range(1, N_DEV):
                g = g + g_ref[k, r0:r0 + r, 0:c]
            return g

        for i, name in enumerate(SMALL_NAMES):
            r0, r, c = SMALL_LAYOUT[name]
            if c <= SMALL_TABLE_COLS:
                blocks = [(slice(None), summed(r0, r, c))]
            else:
                blocks = [(slice(row, row + 1), jnp.concatenate([summed(tr, 1, w) for tr, _, w in _small_pieces(name, row)], axis=1))
                          for row in range(r)]
            for rows, g in blocks:
                delta, mn, vn = _adam_update(g, w_refs[i][rows, :], m_refs[i][rows, :], v_refs[i][rows, :])
                outs[i][rows, :] = g
                outs[n + i][rows, :] = delta
                outs[2 * n + i][rows, :] = mn
                outs[3 * n + i][rows, :] = vn
        outs[4 * n][...] = summed(LOSS_ROW, 1, LANES)

    shapes = tuple(jax.ShapeDtypeStruct(w.shape, F32) for w in ws)
    res = pl.pallas_call(
        body, name="adamw_small",
        out_shape=shapes * 4 + (jax.ShapeDtypeStruct((1, LANES), F32),),
        compiler_params=pltpu.CompilerParams(vmem_limit_bytes=VMEM_LIMIT),
    )(gtab, *ws, *ms, *vs)
    return res[:n], res[n:2 * n], res[2 * n:3 * n], res[3 * n:4 * n], res[4 * n]


def _mesh_peers():
    x, y, c = lax.axis_index("x"), lax.axis_index("y"), lax.axis_index("c")
    me = 4 * x + 2 * y + c
    peers = []
    for k in range(1, N_DEV):
        px = 1 - x if (k >> 2) & 1 else x
        py = 1 - y if (k >> 1) & 1 else y
        pc = 1 - c if k & 1 else c
        peers.append(((px, py, pc), 4 * px + 2 * py + pc))
    return me, peers


_HBM_SPEC = pl.BlockSpec(memory_space=pltpu.HBM)
_SEM_SPEC = pl.BlockSpec(memory_space=pltpu.SEMAPHORE)


ALL_PEERS = tuple(range(N_DEV - 1))
CHIP_LEVEL_PEERS = (0, 1, 3, 5)


def _split_copies(srcs, lands, send_sems, recv_sems, local_sems, gather, which=ALL_PEERS):
    me, peers = _mesh_peers()
    na = len(srcs)
    mine = lambda a, slot: srcs[a] if gather else srcs[a].at[slot]
    local = [pltpu.make_async_copy(mine(a, me), lands[a].at[me], local_sems.at[a]) for a in range(na)]
    sent, received = [], []
    for k, (dev, pid) in enumerate(peers):
        if k not in which:
            continue
        for a in range(na):
            s = a * (N_DEV - 1) + k
            sent.append(pltpu.make_async_remote_copy(
                src_ref=mine(a, pid), dst_ref=lands[a].at[me], send_sem=send_sems.at[s],
                recv_sem=recv_sems.at[s], device_id=dev, device_id_type=pl.DeviceIdType.MESH))
            received.append(pltpu.make_async_remote_copy(
                src_ref=mine(a, pid), dst_ref=lands[a].at[pid], send_sem=send_sems.at[s],
                recv_sem=recv_sems.at[s], device_id=dev, device_id_type=pl.DeviceIdType.MESH))
    return local, sent, received


def _comm_start(name, srcs, gather, after, which=ALL_PEERS):
    na = len(srcs)
    land_structs = [jax.ShapeDtypeStruct(((N_DEV,) + s.shape) if gather else s.shape, s.dtype) for s in srcs]
    extra = [] if after is None else [after]
    n_in = 2 * na + len(extra)

    def body(*refs):
        srcs_r, lands_r = refs[:na], refs[na:2 * na]
        send_sems, recv_sems, local_sems = refs[n_in:n_in + 3]
        token = refs[-1]
        local, sent, _ = _split_copies(srcs_r, lands_r, send_sems, recv_sems, local_sems, gather, which)
        for cp in local + sent:
            cp.start()
        token[...] = jnp.zeros_like(token)

    sem = pltpu.SemaphoreType.DMA((na * (N_DEV - 1),))
    outs = pl.pallas_call(
        body, name=name,
        out_shape=(sem, sem, pltpu.SemaphoreType.DMA((na,)),
                   *[pltpu.HBM(s.shape, s.dtype) for s in srcs],
                   *[pltpu.HBM(s.shape, s.dtype) for s in land_structs],
                   jax.ShapeDtypeStruct((8, LANES), F32)),
        in_specs=[_HBM_SPEC] * (2 * na) + [pl.BlockSpec(memory_space=pl.ANY)] * len(extra),
        out_specs=(_SEM_SPEC, _SEM_SPEC, _SEM_SPEC, *([_HBM_SPEC] * (2 * na)), pl.BlockSpec(memory_space=pltpu.VMEM)),
        input_output_aliases={i: 3 + i for i in range(2 * na)},
        compiler_params=pltpu.CompilerParams(has_side_effects=pltpu.SideEffectType.DATAFLOW_SIDE_EFFECTING),
    )(*[pltpu.with_memory_space_constraint(s, pltpu.HBM) for s in srcs],
      *[pltpu.with_memory_space_constraint(lax.empty(s.shape, s.dtype), pltpu.HBM) for s in land_structs],
      *extra)
    return (outs[:3], outs[3:3 + na], outs[3 + na:3 + 2 * na]), outs[-1]


def _comm_wait(name, handles, gather, after, which=ALL_PEERS):
    (send_sems, recv_sems, local_sems), srcs, lands = handles
    na = len(srcs)

    def body(*refs):
        srcs_r, lands_r = refs[:na], refs[na:2 * na]
        send_r, recv_r, local_r = refs[2 * na:2 * na + 3]
        local, sent, received = _split_copies(srcs_r, lands_r, send_r, recv_r, local_r, gather, which)
        for cp in sent:
            cp.wait_send()
        for cp in received:
            cp.wait_recv()
        for cp in local:
            cp.wait()

    outs = pl.pallas_call(
        body, name=name,
        out_shape=(*[pltpu.HBM(s.shape, s.dtype) for s in srcs], *[pltpu.HBM(s.shape, s.dtype) for s in lands]),
        in_specs=[_HBM_SPEC] * (2 * na) + [_SEM_SPEC] * 3 + [pl.BlockSpec(memory_space=pl.ANY)],
        out_specs=tuple([_HBM_SPEC] * (2 * na)),
        input_output_aliases={i: i for i in range(2 * na)},
        compiler_params=pltpu.CompilerParams(has_side_effects=pltpu.SideEffectType.DATAFLOW_SIDE_EFFECTING),
    )(*srcs, *lands, send_sems, recv_sems, local_sems, after)
    return outs[na:]


def _forward_to_sibling(name, lands):
    na = len(lands)

    def body(*refs):
        ins, outs = refs[:na], refs[na:2 * na]
        send_sems, recv_sems = refs[2 * na:]
        x, y, c = lax.axis_index("x"), lax.axis_index("y"), lax.axis_index("c")
        sibling = (x, y, 1 - c)
        chips = [(1 - x, y), (x, 1 - y), (1 - x, 1 - y)]
        slot = lambda cx, cy, cc: 4 * cx + 2 * cy + cc
        sends = []
        for j, (cx, cy) in enumerate(chips):
            for a in range(na):
                cp = pltpu.make_async_remote_copy(
                    src_ref=ins[a].at[slot(cx, cy, c)], dst_ref=outs[a].at[slot(cx, cy, c)],
                    send_sem=send_sems.at[a, j], recv_sem=recv_sems.at[a, j], device_id=sibling,
                    device_id_type=pl.DeviceIdType.MESH)
                cp.start()
                sends.append(cp)
        for j, (cx, cy) in enumerate(chips):
            for a in range(na):
                pltpu.make_async_remote_copy(
                    src_ref=ins[a].at[slot(cx, cy, 1 - c)], dst_ref=outs[a].at[slot(cx, cy, 1 - c)],
                    send_sem=send_sems.at[a, j], recv_sem=recv_sems.at[a, j], device_id=sibling,
                    device_id_type=pl.DeviceIdType.MESH).wait_recv()
        for cp in sends:
            cp.wait_send()

    hbm = pl.BlockSpec(memory_space=pl.ANY)
    return pl.pallas_call(
        body, name=name,
        out_shape=tuple(jax.ShapeDtypeStruct(s.shape, s.dtype) for s in lands),
        in_specs=[hbm] * na, out_specs=(hbm,) * na,
        input_output_aliases={a: a for a in range(na)},
        scratch_shapes=[pltpu.SemaphoreType.DMA((na, 3)), pltpu.SemaphoreType.DMA((na, 3))],
    )(*lands)


def _prep_weights(w):
    e = CONV_WIDTH
    p = {}
    if 'a_w_in' in w:
        if w['a_w_in'].shape == (N_DEV, D_MODEL, 3 * e // N_DEV):
            p['a_w_in3'] = w['a_w_in']
        else:
            p['a_w_in3'] = w['a_w_in'].reshape(D_MODEL, N_DEV, 3 * e // N_DEV).transpose(1, 0, 2)
        p['a_b_in'] = w['a_b_in'].reshape(1, 3 * e)
        p['a_conv_w'] = jnp.pad(w['a_conv_w'].reshape(CONV_KERNEL, e), ((0, HALO - CONV_KERNEL), (0, 0)))
        p['a_conv_b'] = w['a_conv_b'].reshape(1, e)
        p['a_norm_g'] = w['a_norm_g'].reshape(1, e)
        p['a_norm_b'] = w['a_norm_b'].reshape(1, e)
        p['a_b_out'] = w['a_b_out'].reshape(1, D_MODEL)
        p['kv_norm_g'] = w['kv_norm_g'].reshape(1, KV_LORA_RANK)
        p['b_q_norm_g'] = w['b_q_norm_g'].reshape(1, Q_LORA_RANK)
        p['ln_g'] = w['ln_g']
        p['ln_b'] = w['ln_b']
    if 'a_w_out' in w:
        p['a_w_out'] = w['a_w_out'].reshape(e, D_MODEL)
        p['kv_w_down'] = jnp.pad(w['kv_w_down'], ((0, 0), (0, KV_LORA_RANK + LANES - w['kv_w_down'].shape[1])))
        p['kv_w_ukv'] = jnp.concatenate([w['kv_w_uk'], w['kv_w_uv']], axis=2).reshape(KV_LORA_RANK, N_HEADS * HEAD_PAD)
        p['b_w_in'] = w['b_w_in'].reshape(D_MODEL, -1)
        uq = w['b_w_uq'].reshape(Q_LORA_RANK, N_HEADS, QK_NOPE_DIM + QK_ROPE_DIM)
        p['b_w_uq'] = jnp.pad(uq, ((0, 0), (0, 0), (0, HEAD_PAD - uq.shape[2]))).reshape(Q_LORA_RANK, N_HEADS * HEAD_PAD)
        p['b_w_out'] = w['b_w_out'].reshape(N_HEADS * V_HEAD_DIM, D_MODEL)
    return p


def _local_step(x, positions, target, comm):
    t = x.shape[0]
    e = CONV_WIDTH
    freqs = ROPE_THETA ** (-jnp.arange(0, QK_ROPE_DIM, 2, dtype=F32) / QK_ROPE_DIM)
    freq_row = jnp.concatenate([freqs, freqs, jnp.zeros((LANES - QK_ROPE_DIM,), F32)]).reshape(1, LANES)
    rc, rs, xb = _rope_tables(positions.reshape(t, 1), freq_row, x, comm.start_token())
    p = comm.first_weights(rc)
    ln_g0, ln_b0 = p['ln_g'][0:1], p['ln_b'][0:1]
    ln_g1, ln_b1 = p['ln_g'][1:2], p['ln_b'][1:2]

    proj = _matmul(xb, p['a_w_in3'], bias=p['a_b_in'], name="a_in", b_blocked=True, bm=2048,
                   after=comm.first_after())
    u1, u2 = _conv_mid_fwd(proj, p['a_conv_w'], p['a_conv_b'], p['a_norm_g'], p['a_norm_b'])
    p = {**p, **comm.rest_weights(u2)}
    h1, h1b, xh1, rstd1 = _ln_res_fwd(x, u2, p['a_w_out'], p['a_b_out'], ln_g0, ln_b0)
    ckv = _matmul(h1b, p['kv_w_down'], name="kv_down")
    ckn, kr = _kv_mid_fwd(ckv, p['kv_norm_g'], rc, rs)
    kv = _matmul(ckn, p['kv_w_ukv'], name="kv_up", out_dtype=BF16)
    projb = _matmul(h1b, p['b_w_in'], name="b_in", bn=1280)
    cqn = _q_norm_fwd(projb, p['b_q_norm_g'])
    qcat = _q_up_rope(cqn, p['b_w_uq'], rc, rs)
    o, o2, lse = _flash_fwd(qcat, kv, kr, projb)
    loss, dr2, dr2b, dg1, db1 = _final_ln_loss(h1, o2, p['b_w_out'], ln_g1, ln_b1, target)

    g = {}
    g['b_w_out'] = _matmul(o2, dr2b, trans_a=True, name="d_b_out", out_dtype=BF16, bm=512, bk=t)
    do2 = _matmul(dr2b, p['b_w_out'], trans_b=True, name="d_o2")
    dob, dprojb, dl = _gate_bwd(do2, o, projb)
    dq2, dkv, dkr = _flash_bwd(qcat, kv, kr, dob, lse, dl, rc, rs)
    duq = _matmul(cqn, dq2, trans_a=True, name="d_q_up", out_dtype=BF16, bk=t)
    dcqn = _matmul(dq2, p['b_w_uq'], trans_b=True, name="d_cqn", bk=N_HEADS * HEAD_PAD)
    dprojb, dgq = _q_norm_bwd(dcqn, projb, p['b_q_norm_g'], dprojb)
    g['b_w_in'] = _matmul(h1b, dprojb, trans_a=True, name="d_b_in", bn=640, bk=t, out_dtype=BF16)
    t1 = _matmul(dprojb, p['b_w_in'], trans_b=True, name="d_h1_b", bk=dprojb.shape[1])
    dukv = _matmul(ckn, dkv, trans_a=True, name="d_kv_up", out_dtype=BF16, bk=t)
    dckn = _matmul(dkv, p['kv_w_ukv'], trans_b=True, name="d_ckn", bk=N_HEADS * HEAD_PAD)
    dckv, dgkv = _kv_mid_bwd(dckn, ckv, dkr, p['kv_norm_g'], rc, rs)
    ddown = _matmul(h1b, dckv, trans_a=True, name="d_kv_down", out_dtype=BF16, bm=512, bk=t)
    g['b_w_out'] = g['b_w_out'].reshape(N_DEV, -1, D_MODEL)
    g['b_w_in'] = g['b_w_in'].reshape(D_MODEL, N_DEV, -1).transpose(1, 0, 2)
    g['kv_w_down'] = ddown[:, :KV_LORA_RANK + QK_ROPE_DIM].reshape(N_DEV, -1, KV_LORA_RANK + QK_ROPE_DIM)
    dukv = dukv.reshape(KV_LORA_RANK, N_HEADS, HEAD_PAD)
    g['kv_w_uk'] = dukv[:, :, :QK_NOPE_DIM].reshape(N_DEV, -1, QK_NOPE_DIM)
    g['kv_w_uv'] = dukv[:, :, QK_NOPE_DIM:].reshape(N_DEV, -1, V_HEAD_DIM)
    g['b_w_uq'] = duq.reshape(Q_LORA_RANK, N_HEADS, HEAD_PAD)[:, :, :QK_NOPE_DIM + QK_ROPE_DIM].reshape(
        N_DEV, -1, QK_NOPE_DIM + QK_ROPE_DIM)
    sent1 = comm.send_group1(g)
    t2 = _matmul(dckv, p['kv_w_down'], trans_b=True, name="d_h1_kv", after=sent1)
    dr1, dr1b, dg0, db0, dba_out = _ln0_bwd(dr2, t1, t2, xh1, rstd1, ln_g0)
    dw_out = _matmul(u2, dr1b, trans_a=True, name="d_a_out", out_dtype=BF16, bm=512, bk=t).reshape(N_DEV, -1, D_MODEL)
    sent1b = comm.send_group1b({'a_w_out': dw_out})
    du2 = _matmul(dr1b, p['a_w_out'], trans_b=True, name="d_u2", after=sent1b)
    du1, dproj, dng, dnb, dbz = _conv_mid_bwd_rows(u1, proj, du2, p['a_norm_g'], p['a_norm_b'])
    dproj, dcw, dcb, dbv = _conv_bwd(du1, proj, p['a_conv_w'], dproj)
    half = D_MODEL // 2
    dw_lo = _matmul(xb, dproj, trans_a=True, name="d_a_in_lo", out_dtype=BF16, bm=half, bn=3 * e // N_DEV, bk=t,
                    out_blocked=True, a_cols=(0, half))
    small = (dg0, dg1, db0, db1, dbv, dbz, dcb, dng, dnb, dba_out, dgkv, dgq, dcw, loss)
    sent2 = comm.send_group2(dw_lo, small)
    dw_hi = _matmul(xb, dproj, trans_a=True, name="d_a_in_hi", out_dtype=BF16, bm=half, bn=3 * e // N_DEV, bk=t,
                    out_blocked=True, after=sent2, a_cols=(half, half))
    sent3 = comm.send_group3(dw_hi)
    grad_x = _grad_x(dproj, p['a_w_in3'], dr1, sent3)
    return loss, grad_x


def _grad_x(dproj, w3, dr1, after):
    t, d = dr1.shape
    nblk, _, cb = w3.shape
    tm = min(t, 512)
    extra = [] if after is None else [after]

    def body(a_ref, w_ref, dr_ref, *rest):
        o_ref = rest[-1]
        acc = ALPHA * dr_ref[...]
        for k in range(nblk):
            acc = acc + lax.dot_general(a_ref[:, k * cb:(k + 1) * cb], w_ref[k], NT, preferred_element_type=F32)
        o_ref[...] = acc

    row = pl.BlockSpec((tm, d), lambda i: (i, 0))
    return pl.pallas_call(
        body, name="grad_x",
        out_shape=jax.ShapeDtypeStruct((t, d), F32),
        grid=(t // tm,),
        in_specs=[pl.BlockSpec((tm, nblk * cb), lambda i: (i, 0)),
                  pl.BlockSpec((nblk, d, cb), lambda i: (0, 0, 0), pipeline_mode=pl.Buffered(1)), row]
                 + [pl.BlockSpec(memory_space=pl.ANY)] * len(extra),
        out_specs=row,
        compiler_params=_cparams(("parallel",)),
    )(dproj, w3, dr1, *extra)


def _shard_2d(a):
    return a.reshape(-1, a.shape[-1])


def _gathered_to_full(name, blocks):
    if name == 'a_w_in':
        return blocks
    if name == 'b_w_in':
        return blocks.transpose(1, 0, 2).reshape(D_MODEL, -1)
    if name == 'a_conv_w':
        return blocks.transpose(1, 0, 2).reshape(CONV_KERNEL, -1)
    if name in ('a_b_in', 'a_conv_b', 'a_norm_g', 'a_norm_b', 'a_b_out'):
        return blocks.reshape(-1)
    if name in ('kv_w_uk', 'kv_w_uv'):
        return blocks.reshape(KV_LORA_RANK, N_HEADS, -1)
    if name == 'b_w_uq':
        return blocks.reshape(Q_LORA_RANK, N_HEADS, -1)
    return blocks.reshape(-1, blocks.shape[-1])


def kernel(x, positions, ln_g, ln_b, a_w_in, a_b_in, a_conv_w, a_conv_b, a_norm_g, a_norm_b, a_w_out, a_b_out, kv_w_down, kv_norm_g, kv_w_uk, kv_w_uv, b_w_in, b_q_norm_g, b_w_uq, b_w_out, loss_target, m_ln_g, m_ln_b, m_a_w_in, m_a_b_in, m_a_conv_w, m_a_conv_b, m_a_norm_g, m_a_norm_b, m_a_w_out, m_a_b_out, m_kv_w_down, m_kv_norm_g, m_kv_w_uk, m_kv_w_uv, m_b_w_in, m_b_q_norm_g, m_b_w_uq, m_b_w_out, v_ln_g, v_ln_b, v_a_w_in, v_a_b_in, v_a_conv_w, v_a_conv_b, v_a_norm_g, v_a_norm_b, v_a_w_out, v_a_b_out, v_kv_w_down, v_kv_norm_g, v_kv_w_uk, v_kv_w_uv, v_b_w_in, v_b_q_norm_g, v_b_w_uq, v_b_w_out):
    w = dict(ln_g=ln_g, ln_b=ln_b, a_w_in=a_w_in, a_b_in=a_b_in, a_conv_w=a_conv_w, a_conv_b=a_conv_b,
             a_norm_g=a_norm_g, a_norm_b=a_norm_b, a_w_out=a_w_out, a_b_out=a_b_out, kv_w_down=kv_w_down,
             kv_norm_g=kv_norm_g, kv_w_uk=kv_w_uk, kv_w_uv=kv_w_uv, b_w_in=b_w_in, b_q_norm_g=b_q_norm_g,
             b_w_uq=b_w_uq, b_w_out=b_w_out)
    m = dict(ln_g=m_ln_g, ln_b=m_ln_b, a_w_in=m_a_w_in, a_b_in=m_a_b_in, a_conv_w=m_a_conv_w, a_conv_b=m_a_conv_b,
             a_norm_g=m_a_norm_g, a_norm_b=m_a_norm_b, a_w_out=m_a_w_out, a_b_out=m_a_b_out, kv_w_down=m_kv_w_down,
             kv_norm_g=m_kv_norm_g, kv_w_uk=m_kv_w_uk, kv_w_uv=m_kv_w_uv, b_w_in=m_b_w_in, b_q_norm_g=m_b_q_norm_g,
             b_w_uq=m_b_w_uq, b_w_out=m_b_w_out)
    v = dict(ln_g=v_ln_g, ln_b=v_ln_b, a_w_in=v_a_w_in, a_b_in=v_a_b_in, a_conv_w=v_a_conv_w, a_conv_b=v_a_conv_b,
             a_norm_g=v_a_norm_g, a_norm_b=v_a_norm_b, a_w_out=v_a_w_out, a_b_out=v_a_b_out, kv_w_down=v_kv_w_down,
             kv_norm_g=v_kv_norm_g, kv_w_uk=v_kv_w_uk, kv_w_uv=v_kv_w_uv, b_w_in=v_b_w_in, b_q_norm_g=v_b_q_norm_g,
             b_w_uq=v_b_w_uq, b_w_out=v_b_w_out)

    small_flat = jnp.concatenate([w[n].reshape(-1) for n in SMALL_WEIGHTS]).reshape(1, -1)
    rest_names = [n for n in MATMUL_WEIGHTS if n != 'a_w_in']
    group1 = ('b_w_out', 'b_w_uq', 'b_w_in', 'kv_w_uk', 'kv_w_uv', 'kv_w_down')
    group1b = ('a_w_out',)

    class Comm:
        def __init__(self):
            self.first, self.first_token = _comm_start(
                "gather_first_start", [_shard_2d(w['a_w_in']).astype(BF16), small_flat], True, None, CHIP_LEVEL_PEERS)

        def start_token(self):
            return self.first_token

        def first_weights(self, after):
            lands = _comm_wait("gather_first_wait", self.first, True, after, CHIP_LEVEL_PEERS)
            ga = _forward_to_sibling("gather_first_forward", list(lands))
            full = {'a_w_in': ga[0]}
            gs = ga[1].reshape(N_DEV, -1)
            off = 0
            for n in SMALL_WEIGHTS:
                size = math.prod(w[n].shape)
                full[n] = _gathered_to_full(n, gs[:, off:off + size].reshape((N_DEV,) + _shard_2d(w[n]).shape))
                off += size
            for n in REPLICATED:
                full[n] = w[n]
            dense = lambda a: a if a.shape[-1] % LANES == 0 else a.reshape(-1, LANES)
            self.rest, self.rest_token = _comm_start(
                "gather_rest_start", [dense(_shard_2d(w[n]).astype(BF16)) for n in rest_names], True, ga[0])
            return _prep_weights(full)

        def first_after(self):
            return self.rest_token

        def rest_weights(self, after):
            lands = _comm_wait("gather_rest_wait", self.rest, True, after)
            blocks = {n: lands[i].reshape((N_DEV,) + _shard_2d(w[n]).shape) for i, n in enumerate(rest_names)}
            return _prep_weights({n: _gathered_to_full(n, blocks[n]) for n in rest_names})

        def send_group1(self, g):
            self.g1, token = _comm_start("exchange_g1_start", [g[n] for n in group1], False, None)
            return token

        def send_group1b(self, g):
            self.g1b, token = _comm_start("exchange_g1b_start", [g[n] for n in group1b], False, None)
            return token

        def send_group2(self, dw_lo, small):
            self.g2, token = _comm_start("exchange_g2_start", [dw_lo, _pack_small_grads(*small)], False, None)
            return token

        def send_group3(self, dw_hi):
            self.g3, token = _comm_start("exchange_g3_start", [dw_hi], False, None)
            return token

    comm = Comm()

    _, grad_x = _local_step(x[0], positions[0], loss_target[0], comm)

    res = {}
    recv1 = _comm_wait("exchange_g1_wait", comm.g1, False, grad_x)
    for i, n in enumerate(group1):
        res[n] = _adamw(n, recv1[i], _shard_2d(w[n]), _shard_2d(m[n]), _shard_2d(v[n]))
    recv1b = _comm_wait("exchange_g1b_wait", comm.g1b, False, res[group1[-1]][0])
    for i, n in enumerate(group1b):
        res[n] = _adamw(n, recv1b[i], _shard_2d(w[n]), _shard_2d(m[n]), _shard_2d(v[n]))
    recv2 = _comm_wait("exchange_g2_wait", comm.g2, False, res[group1b[-1]][0])
    recv3 = _comm_wait("exchange_g3_wait", comm.g3, False, recv2[0])
    res['a_w_in'] = _adamw('a_w_in', [recv2[0], recv3[0]], _shard_2d(w['a_w_in']), _shard_2d(m['a_w_in']),
                           _shard_2d(v['a_w_in']))
    two_d = lambda d: [_shard_2d(d[n]) if d[n].ndim > 1 else d[n].reshape(1, -1) for n in SMALL_NAMES]
    sg, sd, sm, sv, loss = _adamw_small(recv2[-1], two_d(w), two_d(m), two_d(v))
    for i, n in enumerate(SMALL_NAMES):
        res[n] = (sg[i], sd[i], sm[i], sv[i])
    outs = [[res[n][j].reshape(w[n].shape) for n in WEIGHT_NAMES] for j in range(4)]
    return (loss[0, 0], grad_x[None], *outs[0], *outs[1], *outs[2], *outs[3])
```

```python
import functools
import math

import jax
import jax.numpy as jnp
from jax import lax
from jax.experimental import pallas as pl
from jax.experimental.pallas import tpu as pltpu

F32 = jnp.float32
BF16 = jnp.bfloat16

D_MODEL = 1024
DEPTH = 2
CONV_WIDTH = 2 * D_MODEL
CONV_KERNEL = 31
N_HEADS = 16
QK_NOPE_DIM = 128
QK_ROPE_DIM = 64
V_HEAD_DIM = 128
KV_LORA_RANK = D_MODEL // 4
Q_LORA_RANK = D_MODEL // 2
ROPE_THETA = 10000.0
LN_EPS = 1e-5
RMS_EPS = 1e-6
MASK_VALUE = -1e30
ALPHA = (2.0 * DEPTH) ** 0.25
ATTN_SCALE = 1.0 / math.sqrt(QK_NOPE_DIM + QK_ROPE_DIM)

ADAM_LR = 0.001
ADAM_B1 = 0.9
ADAM_B2 = 0.999
ADAM_EPS = 1e-08
ADAM_WD = 0.01
ADAM_STEP = 10

N_DEV = 8
LANES = 128
HEAD_PAD = 256
HALO = 32
VMEM_LIMIT = 56 * 1024 * 1024

WEIGHT_NAMES = ['ln_g', 'ln_b', 'a_w_in', 'a_b_in', 'a_conv_w', 'a_conv_b', 'a_norm_g', 'a_norm_b',
                'a_w_out', 'a_b_out', 'kv_w_down', 'kv_norm_g', 'kv_w_uk', 'kv_w_uv', 'b_w_in',
                'b_q_norm_g', 'b_w_uq', 'b_w_out']
REPLICATED = ('ln_g', 'ln_b', 'kv_norm_g', 'b_q_norm_g')
MATMUL_WEIGHTS = ('a_w_in', 'a_w_out', 'kv_w_down', 'kv_w_uk', 'kv_w_uv', 'b_w_in', 'b_w_uq', 'b_w_out')
SMALL_WEIGHTS = ('a_b_in', 'a_conv_w', 'a_conv_b', 'a_norm_g', 'a_norm_b', 'a_b_out')

NN = (((1,), (0,)), ((), ()))
NT = (((1,), (1,)), ((), ()))
TN = (((0,), (0,)), ((), ()))


def _cparams(sem):
    return pltpu.CompilerParams(dimension_semantics=sem, vmem_limit_bytes=VMEM_LIMIT)


def _sigmoid(x):
    return 0.5 * jnp.tanh(0.5 * x) + 0.5


def _fold_rows(x):
    parts = [x[r:r + 8] for r in range(0, x.shape[0], 8)]
    while len(parts) > 1:
        parts = [parts[i] + parts[i + 1] for i in range(0, len(parts) - 1, 2)] + ([parts[-1]] if len(parts) % 2 else [])
    return parts[0]


def _taps_by_residue(offset_of):
    groups = {}
    for j in range(CONV_KERNEL):
        off = offset_of(j)
        groups.setdefault(off % 8, []).append((j, off // 8))
    return sorted(groups.items())


def _swap_rope_halves(t):
    lane = lax.broadcasted_iota(jnp.int32, t.shape, 1)
    return jnp.where(lane < QK_ROPE_DIM // 2, pltpu.roll(t, LANES - QK_ROPE_DIM // 2, 1),
                     pltpu.roll(t, QK_ROPE_DIM // 2, 1))


def _matmul(a, b, *, name, bias=None, trans_a=False, trans_b=False, out_dtype=F32, bm=1024, bn=1024, bk=1024,
            b_blocked=False, out_blocked=False, after=None, a_cols=None):
    m_off = 0
    if trans_a:
        kdim, m = a.shape
        if a_cols is not None:
            m_off, m = a_cols
    else:
        m, kdim = a.shape
    if b_blocked and trans_b:
        n, k2 = b.shape[1], b.shape[0] * b.shape[2]
        bk = b.shape[2]
    elif b_blocked:
        k2, n = b.shape[1], b.shape[0] * b.shape[2]
        bn = b.shape[2]
    elif trans_b:
        n, k2 = b.shape
    else:
        k2, n = b.shape
    assert kdim == k2, (a.shape, b.shape)
    bm, bn, bk = min(bm, m), min(bn, n), min(bk, kdim)
    assert m % bm == 0 and n % bn == 0 and kdim % bk == 0, (name, m, n, kdim, bm, bn, bk)
    nk = kdim // bk
    dims = (((0 if trans_a else 1,), (1 if trans_b else 0,)), ((), ()))
    has_bias = bias is not None

    def body(*refs):
        refs = list(refs)
        a_ref, b_ref = refs[:2]
        del refs[:2]
        bias_ref = refs.pop(0) if has_bias else None
        if after is not None:
            refs.pop(0)
        o_ref = refs.pop(0)
        rest = refs
        prod = lax.dot_general(a_ref[...], b_ref[...], dims, preferred_element_type=F32)

        def finish(acc):
            if has_bias:
                acc = acc + bias_ref[...]
            o_ref[...] = acc.astype(out_dtype)

        if nk == 1:
            finish(prod)
        else:
            acc_ref = rest[0]
            k = pl.program_id(2)

            @pl.when(k == 0)
            def _():
                acc_ref[...] = prod

            @pl.when(k > 0)
            def _():
                acc_ref[...] += prod

            @pl.when(k == nk - 1)
            def _():
                finish(acc_ref[...])

    assert m_off % bm == 0
    a_first = m_off // bm
    a_spec = (pl.BlockSpec((bk, bm), lambda i, j, k: (k, i + a_first)) if trans_a
              else pl.BlockSpec((bm, bk), lambda i, j, k: (i, k)))
    if b_blocked and trans_b:
        b_spec = pl.BlockSpec((None, bn, bk), lambda i, j, k: (k, j, 0))
    elif b_blocked:
        b_spec = pl.BlockSpec((None, bk, bn), lambda i, j, k: (j, k, 0))
    elif trans_b:
        b_spec = pl.BlockSpec((bn, bk), lambda i, j, k: (j, k))
    else:
        b_spec = pl.BlockSpec((bk, bn), lambda i, j, k: (k, j))
    in_specs = [a_spec, b_spec]
    args = [a, b]
    if has_bias:
        in_specs.append(pl.BlockSpec((1, bn), lambda i, j, k: (0, j)))
        args.append(bias)
    if after is not None:
        in_specs.append(pl.BlockSpec(memory_space=pl.ANY))
        args.append(after)
    if out_blocked:
        out_struct = jax.ShapeDtypeStruct((n // bn, m, bn), out_dtype)
        out_spec = pl.BlockSpec((None, bm, bn), lambda i, j, k: (j, i, 0))
    else:
        out_struct = jax.ShapeDtypeStruct((m, n), out_dtype)
        out_spec = pl.BlockSpec((bm, bn), lambda i, j, k: (i, j))
    return pl.pallas_call(
        body, name=name,
        out_shape=out_struct,
        grid=(m // bm, n // bn, nk),
        in_specs=in_specs,
        out_specs=out_spec,
        scratch_shapes=[pltpu.VMEM((bm, bn), F32)] if nk > 1 else [],
        compiler_params=_cparams(("parallel", "parallel", "arbitrary")),
    )(*args)


def _rope_tables(pos_col, freq_row, x, after):
    t, d = x.shape
    tm = min(t, 512)
    extra = [] if after is None else [after]

    def body(pos_ref, f_ref, x_ref, *rest):
        c_ref, s_ref, xb_ref = rest[-3:]
        ang = pos_ref[...].astype(F32) * f_ref[...]
        lane = lax.broadcasted_iota(jnp.int32, ang.shape, 1)
        cos = jnp.cos(ang)
        sin = jnp.sin(ang)
        c_ref[...] = jnp.where(lane < QK_ROPE_DIM, cos, 0.0)
        s_ref[...] = jnp.where(lane < QK_ROPE_DIM // 2, -sin, jnp.where(lane < QK_ROPE_DIM, sin, 0.0))
        xb_ref[...] = x_ref[...].astype(BF16)

    lane_blk = pl.BlockSpec((tm, LANES), lambda i: (i, 0))
    row = pl.BlockSpec((tm, d), lambda i: (i, 0))
    return pl.pallas_call(
        body, name="rope_tables",
        out_shape=(jax.ShapeDtypeStruct((t, LANES), F32), jax.ShapeDtypeStruct((t, LANES), F32),
                   jax.ShapeDtypeStruct((t, d), BF16)),
        grid=(t // tm,),
        in_specs=[pl.BlockSpec((tm, 1), lambda i: (i, 0)), pl.BlockSpec((1, LANES), lambda i: (0, 0)), row]
                 + [pl.BlockSpec(memory_space=pl.ANY)] * len(extra),
        out_specs=(lane_blk, lane_blk, row),
        compiler_params=_cparams(("parallel",)),
    )(pos_col, freq_row, x, *extra)


def _conv_mid_fwd(proj, cw, cb, ng, nb):
    t = proj.shape[0]
    e = CONV_WIDTH
    tm = min(t, 128)
    hb = tm // HALO
    nchunk = e // LANES

    def body(val_ref, gg_ref, z_ref, valh_ref, ggh_ref, cw_ref, cb_ref, ng_ref, nb_ref, u1_ref, u2_ref, ext_ref, sh_ref):
        i = pl.program_id(0)
        u0 = val_ref[...] * _sigmoid(gg_ref[...])
        h0 = valh_ref[...] * _sigmoid(ggh_ref[...])
        ext_ref[0:HALO, :] = jnp.where(i > 0, h0, 0.0)
        ext_ref[HALO:, :] = u0

        def chunk(c, carry):
            col = pl.multiple_of(c * LANES, LANES)
            acc = jnp.zeros((tm, LANES), F32)
            for res, taps in _taps_by_residue(lambda j: HALO - (CONV_KERNEL - 1) + j):
                span = 8 * max(a for _, a in taps) + tm
                sh_ref[0:span, :] = ext_ref[res:res + span, pl.ds(col, LANES)]
                for j, a in taps:
                    acc = acc + cw_ref[j:j + 1, pl.ds(col, LANES)] * sh_ref[8 * a:8 * a + tm, :]
            u1_ref[:, pl.ds(col, LANES)] = acc + cb_ref[:, pl.ds(col, LANES)]
            return carry

        lax.fori_loop(0, nchunk, chunk, 0)
        g = ng_ref[...]
        b = nb_ref[...]
        for r in range(0, tm, 16):
            u1 = u1_ref[r:r + 16, :]
            mu = jnp.mean(u1, axis=-1, keepdims=True)
            xc = u1 - mu
            var = jnp.mean(xc * xc, axis=-1, keepdims=True)
            n = xc * lax.rsqrt(var + LN_EPS) * g + b
            z = z_ref[r:r + 16, :]
            u2_ref[r:r + 16, :] = ((n * _sigmoid(n)) * (z * _sigmoid(z))).astype(BF16)

    row = lambda c: pl.BlockSpec((tm, e), lambda i: (i, c))
    halo = lambda c: pl.BlockSpec((HALO, e), lambda i: (jnp.maximum(i * hb - 1, 0), c))
    par = lambda r: pl.BlockSpec((r, e), lambda i: (0, 0))
    return pl.pallas_call(
        body, name="conv_mid_fwd",
        out_shape=(jax.ShapeDtypeStruct((t, e), F32), jax.ShapeDtypeStruct((t, e), BF16)),
        grid=(t // tm,),
        in_specs=[row(0), row(1), row(2), halo(0), halo(1), par(HALO), par(1), par(1), par(1)],
        out_specs=(pl.BlockSpec((tm, e), lambda i: (i, 0)), pl.BlockSpec((tm, e), lambda i: (i, 0))),
        scratch_shapes=[pltpu.VMEM((tm + HALO, e), F32), pltpu.VMEM((tm + HALO, LANES), F32)],
        compiler_params=_cparams(("parallel",)),
    )(proj, proj, proj, proj, proj, cw, cb, ng, nb)


def _ln_res_fwd(x, u, w_out, b_out, g, b):
    t, d = x.shape
    ku = u.shape[1]
    tm = min(t, 512)

    def body(x_ref, u_ref, w_ref, bo_ref, g_ref, b_ref, h_ref, hb_ref, xh_ref, rs_ref):
        y = lax.dot_general(u_ref[...], w_ref[...], NN, preferred_element_type=F32) + bo_ref[...]
        r = ALPHA * x_ref[...] + y
        mu = jnp.mean(r, axis=-1, keepdims=True)
        xc = r - mu
        var = jnp.mean(xc * xc, axis=-1, keepdims=True)
        rstd = lax.rsqrt(var + LN_EPS)
        xh = xc * rstd
        h = xh * g_ref[...] + b_ref[...]
        h_ref[...] = h
        hb_ref[...] = h.astype(BF16)
        xh_ref[...] = xh
        rs_ref[...] = rstd

    row = pl.BlockSpec((tm, d), lambda i: (i, 0))
    par = pl.BlockSpec((1, d), lambda i: (0, 0))
    return pl.pallas_call(
        body, name="ln0_fwd",
        out_shape=(jax.ShapeDtypeStruct((t, d), F32), jax.ShapeDtypeStruct((t, d), BF16),
                   jax.ShapeDtypeStruct((t, d), F32), jax.ShapeDtypeStruct((t, 1), F32)),
        grid=(t // tm,),
        in_specs=[row, pl.BlockSpec((tm, ku), lambda i: (i, 0)),
                  pl.BlockSpec((ku, d), lambda i: (0, 0), pipeline_mode=pl.Buffered(1)), par, par, par],
        out_specs=(row, row, row, pl.BlockSpec((tm, 1), lambda i: (i, 0))),
        compiler_params=_cparams(("parallel",)),
    )(x, u, w_out, b_out, g, b)


def _kv_mid_fwd(ckv, g, rc, rs):
    t = ckv.shape[0]
    tm = min(t, 512)
    r = KV_LORA_RANK

    def body(ckv_ref, g_ref, c_ref, s_ref, ckn_ref, kr_ref):
        c = ckv_ref[:, 0:r]
        ms = jnp.mean(c * c, axis=-1, keepdims=True)
        ckn_ref[...] = (c * lax.rsqrt(ms + RMS_EPS) * g_ref[...]).astype(BF16)
        tr = ckv_ref[:, r:r + LANES]
        kr_ref[...] = (tr * c_ref[...] + _swap_rope_halves(tr) * s_ref[...]).astype(BF16)

    return pl.pallas_call(
        body, name="kv_mid_fwd",
        out_shape=(jax.ShapeDtypeStruct((t, r), BF16), jax.ShapeDtypeStruct((t, LANES), BF16)),
        grid=(t // tm,),
        in_specs=[pl.BlockSpec((tm, r + LANES), lambda i: (i, 0)), pl.BlockSpec((1, r), lambda i: (0, 0)),
                  pl.BlockSpec((tm, LANES), lambda i: (i, 0)), pl.BlockSpec((tm, LANES), lambda i: (i, 0))],
        out_specs=(pl.BlockSpec((tm, r), lambda i: (i, 0)), pl.BlockSpec((tm, LANES), lambda i: (i, 0))),
        compiler_params=_cparams(("parallel",)),
    )(ckv, g, rc, rs)


def _q_norm_fwd(projb, g):
    t = projb.shape[0]
    tm = min(t, 512)
    r = Q_LORA_RANK

    def body(c_ref, g_ref, o_ref):
        c = c_ref[...]
        ms = jnp.mean(c * c, axis=-1, keepdims=True)
        o_ref[...] = (c * lax.rsqrt(ms + RMS_EPS) * g_ref[...]).astype(BF16)

    return pl.pallas_call(
        body, name="q_norm_fwd",
        out_shape=jax.ShapeDtypeStruct((t, r), BF16),
        grid=(t // tm,),
        in_specs=[pl.BlockSpec((tm, r), lambda i: (i, 0)), pl.BlockSpec((1, r), lambda i: (0, 0))],
        out_specs=pl.BlockSpec((tm, r), lambda i: (i, 0)),
        compiler_params=_cparams(("parallel",)),
    )(projb, g)


def _q_up_rope(cqn, wuq, rc, rs):
    t, r = cqn.shape
    w = wuq.shape[1]
    tm = min(t, 1024)
    bn = 4 * HEAD_PAD

    def body(a_ref, b_ref, c_ref, s_ref, o_ref):
        q = lax.dot_general(a_ref[...], b_ref[...], NN, preferred_element_type=F32)
        c = c_ref[...]
        s = s_ref[...]
        for h in range(bn // HEAD_PAD):
            a = h * HEAD_PAD
            o_ref[:, a:a + LANES] = q[:, a:a + LANES].astype(BF16)
            tr = q[:, a + LANES:a + 2 * LANES]
            o_ref[:, a + LANES:a + 2 * LANES] = (tr * c + _swap_rope_halves(tr) * s).astype(BF16)

    return pl.pallas_call(
        body, name="q_up_rope",
        out_shape=jax.ShapeDtypeStruct((t, w), BF16),
        grid=(t // tm, w // bn),
        in_specs=[pl.BlockSpec((tm, r), lambda i, j: (i, 0)), pl.BlockSpec((r, bn), lambda i, j: (0, j)),
                  pl.BlockSpec((tm, LANES), lambda i, j: (i, 0)), pl.BlockSpec((tm, LANES), lambda i, j: (i, 0))],
        out_specs=pl.BlockSpec((tm, bn), lambda i, j: (i, j)),
        compiler_params=_cparams(("parallel", "parallel")),
    )(cqn, wuq, rc, rs)


def _flash_fwd(qcat, kv, kr, projb):
    t = qcat.shape[0]
    tq = min(t, 512)
    nq = t // tq
    exp2_scale = ATTN_SCALE * math.log2(math.e)
    z_first = Q_LORA_RANK // LANES

    def body(q_ref, kv_ref, kr_ref, z_ref, o_ref, o2_ref, lse_ref, kcat_ref):
        kcat_ref[:, 0:LANES] = kv_ref[:, 0:LANES]
        kcat_ref[:, LANES:] = kr_ref[...]
        rows = lax.broadcasted_iota(jnp.int32, (tq, tq), 0)
        cols = lax.broadcasted_iota(jnp.int32, (tq, tq), 1)
        for i in range(nq):
            a = i * tq
            q = q_ref[a:a + tq, :]
            sd = lax.dot_general(q, kcat_ref[a:a + tq, :], NT, preferred_element_type=F32)
            sd = jnp.where(cols <= rows, sd, MASK_VALUE)
            m = jnp.max(sd, axis=1, keepdims=True)
            if i > 0:
                sp = lax.dot_general(q, kcat_ref[0:a, :], NT, preferred_element_type=F32)
                m = jnp.maximum(m, jnp.max(sp, axis=1, keepdims=True))
            pd = jnp.exp2((sd - m) * exp2_scale)
            l = jnp.sum(pd, axis=1, keepdims=True)
            acc = lax.dot_general(pd.astype(BF16), kv_ref[a:a + tq, LANES:], NN, preferred_element_type=F32)
            if i > 0:
                pp = jnp.exp2((sp - m) * exp2_scale)
                l = l + jnp.sum(pp, axis=1, keepdims=True)
                acc = acc + lax.dot_general(pp.astype(BF16), kv_ref[0:a, LANES:], NN, preferred_element_type=F32)
            o = acc / l
            z = z_ref[a:a + tq, :]
            o_ref[a:a + tq, :] = o
            o2_ref[a:a + tq, :] = (o * (z * _sigmoid(z))).astype(BF16)
            lse_ref[a:a + tq, :] = jnp.broadcast_to(m * ATTN_SCALE + jnp.log(l), (tq, LANES))

    hw = N_HEADS * LANES
    head256 = pl.BlockSpec((t, HEAD_PAD), lambda h: (0, h))
    head128 = pl.BlockSpec((t, LANES), lambda h: (0, h))
    return pl.pallas_call(
        body, name="flash_fwd",
        out_shape=(jax.ShapeDtypeStruct((t, hw), F32), jax.ShapeDtypeStruct((t, hw), BF16),
                   jax.ShapeDtypeStruct((t, hw), F32)),
        grid=(N_HEADS,),
        in_specs=[head256, head256, pl.BlockSpec((t, LANES), lambda h: (0, 0), pipeline_mode=pl.Buffered(1)),
                  pl.BlockSpec((t, LANES), lambda h: (0, z_first + h))],
        out_specs=(head128, head128, head128),
        scratch_shapes=[pltpu.VMEM((t, HEAD_PAD), BF16)],
        compiler_params=_cparams(("parallel",)),
    )(qcat, kv, kr, projb)


def _final_ln_loss(h1, o2, w_out, g, b, target):
    t, d = h1.shape
    ku = o2.shape[1]
    tm = min(t, 512)

    def body(h_ref, u_ref, w_ref, g_ref, b_ref, t_ref, loss_ref, dr_ref, drb_ref, dg_ref, db_ref):
        i = pl.program_id(0)
        r = ALPHA * h_ref[...] + lax.dot_general(u_ref[...], w_ref[...], NN, preferred_element_type=F32)
        mu = jnp.mean(r, axis=-1, keepdims=True)
        xc = r - mu
        var = jnp.mean(xc * xc, axis=-1, keepdims=True)
        rstd = lax.rsqrt(var + LN_EPS)
        xh = xc * rstd
        g = g_ref[...]
        diff = xh * g + b_ref[...] - t_ref[...]
        part = 0.5 * jnp.sum(jnp.mean(diff * diff, axis=-1, keepdims=True))
        dh = diff * (1.0 / d)
        dgp = jnp.sum(dh * xh, axis=0, keepdims=True)
        dbp = jnp.sum(dh, axis=0, keepdims=True)

        @pl.when(i == 0)
        def _():
            loss_ref[...] = jnp.zeros_like(loss_ref)
            dg_ref[...] = jnp.zeros_like(dg_ref)
            db_ref[...] = jnp.zeros_like(db_ref)

        loss_ref[...] += jnp.full(loss_ref.shape, part, F32)
        dg_ref[...] += dgp
        db_ref[...] += dbp
        dxh = dh * g
        dr = rstd * (dxh - jnp.mean(dxh, axis=-1, keepdims=True) - xh * jnp.mean(dxh * xh, axis=-1, keepdims=True))
        dr_ref[...] = dr
        drb_ref[...] = dr.astype(BF16)

    row = pl.BlockSpec((tm, d), lambda i: (i, 0))
    par = pl.BlockSpec((1, d), lambda i: (0, 0))
    return pl.pallas_call(
        body, name="final_ln_loss",
        out_shape=(jax.ShapeDtypeStruct((1, LANES), F32), jax.ShapeDtypeStruct((t, d), F32),
                   jax.ShapeDtypeStruct((t, d), BF16), jax.ShapeDtypeStruct((1, d), F32),
                   jax.ShapeDtypeStruct((1, d), F32)),
        grid=(t // tm,),
        in_specs=[row, pl.BlockSpec((tm, ku), lambda i: (i, 0)),
                  pl.BlockSpec((ku, d), lambda i: (0, 0), pipeline_mode=pl.Buffered(1)), par, par, row],
        out_specs=(pl.BlockSpec((1, LANES), lambda i: (0, 0)), row, row, par, par),
        compiler_params=_cparams(("arbitrary",)),
    )(h1, o2, w_out, g, b, target)


def _gate_bwd(dr2b, w_out, o, projb):
    t, hw = o.shape
    d = dr2b.shape[1]
    tm = min(t, 512)
    bw = Q_LORA_RANK
    nb = hw // bw
    wb = projb.shape[1]

    def body(dr_ref, w_ref, o_ref, z_ref, dob_ref, dz_ref, dl_ref):
        z = z_ref[...]
        sg = _sigmoid(z)
        d2 = lax.dot_general(dr_ref[...], w_ref[...], NT, preferred_element_type=F32)
        o = o_ref[...]
        do = d2 * (z * sg)
        dob_ref[...] = do.astype(BF16)
        dz_ref[...] = (d2 * o * (sg * (1.0 + z * (1.0 - sg)))).astype(BF16)
        prod = do * o
        for hh in range(bw // LANES):
            dsum = jnp.sum(prod[:, hh * LANES:(hh + 1) * LANES], axis=1, keepdims=True)
            dl_ref[:, hh * LANES:(hh + 1) * LANES] = jnp.broadcast_to(dsum, (tm, LANES))

    blk = pl.BlockSpec((tm, bw), lambda i, j: (i, j))
    blk1 = pl.BlockSpec((tm, bw), lambda i, j: (i, j + 1))
    return pl.pallas_call(
        body, name="gate_bwd",
        out_shape=(jax.ShapeDtypeStruct((t, hw), BF16), jax.ShapeDtypeStruct((t, wb), BF16),
                   jax.ShapeDtypeStruct((t, hw), F32)),
        grid=(t // tm, nb),
        in_specs=[pl.BlockSpec((tm, d), lambda i, j: (i, 0)), pl.BlockSpec((bw, d), lambda i, j: (j, 0)), blk, blk1],
        out_specs=(blk, blk1, blk),
        compiler_params=_cparams(("parallel", "parallel")),
    )(dr2b, w_out, o, projb)


def _flash_bwd(qcat, kv, kr, dob, lse, dl, rc, rs):
    t = qcat.shape[0]
    tq = min(t, 512)
    nq = t // tq
    q_chunk = 2 * tq
    log2e = math.log2(math.e)
    exp2_scale = ATTN_SCALE * log2e
    once = pl.Buffered(1)

    def body(q_ref, kv_ref, kr_ref, do_ref, lse_ref, dl_ref, c_ref, s_ref,
             dq_ref, dkv_ref, dkr_ref, kcat_ref, dqa_ref, dka_ref, dva_ref):
        h = pl.program_id(0)
        kcat_ref[:, 0:LANES] = kv_ref[:, 0:LANES]
        kcat_ref[:, LANES:] = kr_ref[...]
        dqa_ref[...] = jnp.zeros_like(dqa_ref)

        @pl.when(h == 0)
        def _():
            dkr_ref[...] = jnp.zeros_like(dkr_ref)

        rows = lax.broadcasted_iota(jnp.int32, (tq, tq), 0)
        cols = lax.broadcasted_iota(jnp.int32, (tq, tq), 1)

        def block(qs, n, ks, masked):
            q = q_ref[qs:qs + n, :]
            kc = kcat_ref[ks:ks + tq, :]
            s = lax.dot_general(q, kc, NT, preferred_element_type=F32)
            if masked:
                s = jnp.where(cols <= rows, s, MASK_VALUE)
            p = jnp.exp2(s * exp2_scale - lse_ref[qs:qs + n, 0:1] * log2e)
            do = do_ref[qs:qs + n, :]
            dp = lax.dot_general(do, kv_ref[ks:ks + tq, LANES:], NT, preferred_element_type=F32)
            ds = (p * (dp - dl_ref[qs:qs + n, 0:1])).astype(BF16)
            dva_ref[...] += lax.dot_general(p.astype(BF16), do, TN, preferred_element_type=F32)
            dka_ref[...] += lax.dot_general(ds, q, TN, preferred_element_type=F32)
            dqa_ref[qs:qs + n, :] += lax.dot_general(ds, kc, NN, preferred_element_type=F32)

        for kb in range(nq):
            a = kb * tq
            dka_ref[...] = jnp.zeros_like(dka_ref)
            dva_ref[...] = jnp.zeros_like(dva_ref)
            block(a, tq, a, True)
            qs = a + tq
            while qs < t:
                n = min(q_chunk, t - qs)
                block(qs, n, a, False)
                qs += n
            dk = dka_ref[...] * ATTN_SCALE
            dkv_ref[a:a + tq, 0:LANES] = dk[:, 0:LANES].astype(BF16)
            dkv_ref[a:a + tq, LANES:] = dva_ref[...].astype(BF16)
            dkr_ref[a:a + tq, :] += dk[:, LANES:]

        dq = dqa_ref[...] * ATTN_SCALE
        dq_ref[:, 0:LANES] = dq[:, 0:LANES].astype(BF16)
        d2 = dq[:, LANES:]
        dq_ref[:, LANES:] = (d2 * c_ref[...] - _swap_rope_halves(d2) * s_ref[...]).astype(BF16)

    hp = N_HEADS * HEAD_PAD
    head256 = pl.BlockSpec((t, HEAD_PAD), lambda h: (0, h))
    head128 = pl.BlockSpec((t, LANES), lambda h: (0, h))
    const128 = pl.BlockSpec((t, LANES), lambda h: (0, 0), pipeline_mode=once)
    return pl.pallas_call(
        body, name="flash_bwd",
        out_shape=(jax.ShapeDtypeStruct((t, hp), BF16), jax.ShapeDtypeStruct((t, hp), BF16),
                   jax.ShapeDtypeStruct((t, LANES), F32)),
        grid=(N_HEADS,),
        in_specs=[head256, head256, const128, head128, head128, head128, const128, const128],
        out_specs=(head256, head256, pl.BlockSpec((t, LANES), lambda h: (0, 0))),
        scratch_shapes=[pltpu.VMEM((t, HEAD_PAD), BF16), pltpu.VMEM((t, HEAD_PAD), F32),
                        pltpu.VMEM((tq, HEAD_PAD), F32), pltpu.VMEM((tq, LANES), F32)],
        compiler_params=_cparams(("arbitrary",)),
    )(qcat, kv, kr, dob, lse, dl, rc, rs)


def _q_norm_bwd(dcqn, projb, g, dprojb):
    t = projb.shape[0]
    tm = min(t, 512)
    r = Q_LORA_RANK

    def body(d_ref, c_ref, g_ref, alias_ref, o_ref, dg_ref):
        i = pl.program_id(0)
        c = c_ref[...]
        d = d_ref[...]
        rr = lax.rsqrt(jnp.mean(c * c, axis=-1, keepdims=True) + RMS_EPS)
        xh = c * rr

        @pl.when(i == 0)
        def _():
            dg_ref[...] = jnp.zeros_like(dg_ref)

        dg_ref[...] += jnp.sum(d * xh, axis=0, keepdims=True)
        dxh = d * g_ref[...]
        o_ref[...] = (rr * (dxh - xh * jnp.mean(dxh * xh, axis=-1, keepdims=True))).astype(BF16)

    blk = pl.BlockSpec((tm, r), lambda i: (i, 0))
    par = pl.BlockSpec((1, r), lambda i: (0, 0))
    return pl.pallas_call(
        body, name="q_norm_bwd",
        out_shape=(jax.ShapeDtypeStruct(dprojb.shape, BF16), jax.ShapeDtypeStruct((1, r), F32)),
        grid=(t // tm,),
        in_specs=[blk, blk, par, pl.BlockSpec(memory_space=pl.ANY)],
        out_specs=(blk, par),
        input_output_aliases={3: 0},
        compiler_params=_cparams(("arbitrary",)),
    )(dcqn, projb, g, dprojb)


def _kv_mid_bwd(dckn, ckv, dkr, g, rc, rs):
    t = ckv.shape[0]
    tm = min(t, 512)
    r = KV_LORA_RANK

    def body(d_ref, ckv_ref, dkr_ref, g_ref, c_ref, s_ref, o_ref, dg_ref):
        i = pl.program_id(0)
        c = ckv_ref[:, 0:r]
        d = d_ref[...]
        rr = lax.rsqrt(jnp.mean(c * c, axis=-1, keepdims=True) + RMS_EPS)
        xh = c * rr

        @pl.when(i == 0)
        def _():
            dg_ref[...] = jnp.zeros_like(dg_ref)

        dg_ref[...] += jnp.sum(d * xh, axis=0, keepdims=True)
        dxh = d * g_ref[...]
        o_ref[:, 0:r] = (rr * (dxh - xh * jnp.mean(dxh * xh, axis=-1, keepdims=True))).astype(BF16)
        dk = dkr_ref[...]
        o_ref[:, r:] = (dk * c_ref[...] - _swap_rope_halves(dk) * s_ref[...]).astype(BF16)

    return pl.pallas_call(
        body, name="kv_mid_bwd",
        out_shape=(jax.ShapeDtypeStruct((t, r + LANES), BF16), jax.ShapeDtypeStruct((1, r), F32)),
        grid=(t // tm,),
        in_specs=[pl.BlockSpec((tm, r), lambda i: (i, 0)), pl.BlockSpec((tm, r + LANES), lambda i: (i, 0)),
                  pl.BlockSpec((tm, LANES), lambda i: (i, 0)), pl.BlockSpec((1, r), lambda i: (0, 0)),
                  pl.BlockSpec((tm, LANES), lambda i: (i, 0)), pl.BlockSpec((tm, LANES), lambda i: (i, 0))],
        out_specs=(pl.BlockSpec((tm, r + LANES), lambda i: (i, 0)), pl.BlockSpec((1, r), lambda i: (0, 0))),
        compiler_params=_cparams(("arbitrary",)),
    )(dckn, ckv, dkr, g, rc, rs)


def _ln0_bwd(dr2, dprojb, w_in, dckv, w_down, xh, rstd, g, after):
    t, d = dr2.shape
    kb, kd = dprojb.shape[1], dckv.shape[1]
    tm = min(t, 256)
    extra = [] if after is None else [after]

    def body(a_ref, pb_ref, wb_ref, pd_ref, wd_ref, xh_ref, rs_ref, g_ref, *rest):
        dr_ref, drb_ref, dg_ref, db_ref, dsum_ref = rest[-5:]
        i = pl.program_id(0)
        dh = (ALPHA * a_ref[...] + lax.dot_general(pb_ref[...], wb_ref[...], NT, preferred_element_type=F32)
              + lax.dot_general(pd_ref[...], wd_ref[...], NT, preferred_element_type=F32))
        xh = xh_ref[...]

        @pl.when(i == 0)
        def _():
            dg_ref[...] = jnp.zeros_like(dg_ref)
            db_ref[...] = jnp.zeros_like(db_ref)
            dsum_ref[...] = jnp.zeros_like(dsum_ref)

        dg_ref[...] += jnp.sum(dh * xh, axis=0, keepdims=True)
        db_ref[...] += jnp.sum(dh, axis=0, keepdims=True)
        dxh = dh * g_ref[...]
        dr = rs_ref[...] * (dxh - jnp.mean(dxh, axis=-1, keepdims=True) - xh * jnp.mean(dxh * xh, axis=-1, keepdims=True))
        dr_ref[...] = dr
        drb_ref[...] = dr.astype(BF16)
        dsum_ref[...] += jnp.sum(dr, axis=0, keepdims=True)

    row = pl.BlockSpec((tm, d), lambda i: (i, 0))
    par = pl.BlockSpec((1, d), lambda i: (0, 0))
    return pl.pallas_call(
        body, name="ln0_bwd",
        out_shape=(jax.ShapeDtypeStruct((t, d), F32), jax.ShapeDtypeStruct((t, d), BF16),
                   jax.ShapeDtypeStruct((1, d), F32), jax.ShapeDtypeStruct((1, d), F32),
                   jax.ShapeDtypeStruct((1, d), F32)),
        grid=(t // tm,),
        in_specs=[row, pl.BlockSpec((tm, kb), lambda i: (i, 0)),
                  pl.BlockSpec((d, kb), lambda i: (0, 0), pipeline_mode=pl.Buffered(1)),
                  pl.BlockSpec((tm, kd), lambda i: (i, 0)),
                  pl.BlockSpec((d, kd), lambda i: (0, 0), pipeline_mode=pl.Buffered(1)),
                  row, pl.BlockSpec((tm, 1), lambda i: (i, 0)), par]
                 + [pl.BlockSpec(memory_space=pl.ANY)] * len(extra),
        out_specs=(row, row, par, par, par),
        compiler_params=_cparams(("arbitrary",)),
    )(dr2, dprojb, w_in, dckv, w_down, xh, rstd, g, *extra)


def _conv_mid_bwd_rows(u1, proj, dr1b, w_out, ng, nb, after):
    t = u1.shape[0]
    e = CONV_WIDTH
    d = dr1b.shape[1]
    tm = min(t, 256)
    extra = [] if after is None else [after]

    rg = 16
    nt = t // tm

    def body(u1_ref, z_ref, dr_ref, w_ref, ng_ref, nb_ref, *rest):
        du1_ref, dz_ref, dng_ref, dnb_ref, dbz_ref, acc_ref, d_ref = rest[-7:]
        i = pl.program_id(0)

        @pl.when(i == 0)
        def _():
            acc_ref[...] = jnp.zeros_like(acc_ref)

        d_ref[...] = lax.dot_general(dr_ref[...], w_ref[...], NT, preferred_element_type=F32)
        g = ng_ref[...]
        b = nb_ref[...]

        def group(r, carry):
            rows = pl.ds(pl.multiple_of(r * rg, rg), rg)
            u1 = u1_ref[rows, :]
            mu = jnp.mean(u1, axis=-1, keepdims=True)
            xc = u1 - mu
            rstd = lax.rsqrt(jnp.mean(xc * xc, axis=-1, keepdims=True) + LN_EPS)
            xh = xc * rstd
            n = xh * g + b
            sn = _sigmoid(n)
            z = z_ref[rows, :]
            sz = _sigmoid(z)
            du2 = d_ref[rows, :]
            dz = du2 * (n * sn) * (sz * (1.0 + z * (1.0 - sz)))
            dn = du2 * (z * sz) * (sn * (1.0 + n * (1.0 - sn)))
            acc_ref[0:8, :] += _fold_rows(dn * xh)
            acc_ref[8:16, :] += _fold_rows(dn)
            acc_ref[16:24, :] += _fold_rows(dz)
            dz_ref[rows, :] = dz.astype(BF16)
            dxh = dn * g
            du1_ref[rows, :] = rstd * (dxh - jnp.mean(dxh, axis=-1, keepdims=True)
                                       - xh * jnp.mean(dxh * xh, axis=-1, keepdims=True))
            return carry

        lax.fori_loop(0, tm // rg, group, 0, unroll=8)

        @pl.when(i == nt - 1)
        def _():
            dng_ref[...] = jnp.sum(acc_ref[0:8, :], axis=0, keepdims=True)
            dnb_ref[...] = jnp.sum(acc_ref[8:16, :], axis=0, keepdims=True)
            dbz_ref[...] = jnp.sum(acc_ref[16:24, :], axis=0, keepdims=True)

    row = pl.BlockSpec((tm, e), lambda i: (i, 0))
    par = pl.BlockSpec((1, e), lambda i: (0, 0))
    return pl.pallas_call(
        body, name="conv_mid_bwd_rows",
        out_shape=(jax.ShapeDtypeStruct((t, e), F32), jax.ShapeDtypeStruct((t, 3 * e), BF16),
                   jax.ShapeDtypeStruct((1, e), F32), jax.ShapeDtypeStruct((1, e), F32),
                   jax.ShapeDtypeStruct((1, e), F32)),
        grid=(nt,),
        in_specs=[row, pl.BlockSpec((tm, e), lambda i: (i, 2)), pl.BlockSpec((tm, d), lambda i: (i, 0)),
                  pl.BlockSpec((e, d), lambda i: (0, 0), pipeline_mode=pl.Buffered(1)), par, par]
                 + [pl.BlockSpec(memory_space=pl.ANY)] * len(extra),
        out_specs=(row, pl.BlockSpec((tm, e), lambda i: (i, 2)), par, par, par),
        scratch_shapes=[pltpu.VMEM((24, e), F32), pltpu.VMEM((tm, e), F32)],
        compiler_params=_cparams(("arbitrary",)),
    )(u1, proj, dr1b, w_out, ng, nb, *extra)


def _conv_bwd(du1, proj, cw, dproj):
    t = du1.shape[0]
    e = CONV_WIDTH
    tm = min(t, 128)
    hb = tm // HALO
    nt = t // tm
    nchunk = e // LANES
    last_halo = t // HALO - 1

    def body(d_ref, dn_ref, val_ref, gg_ref, valh_ref, ggh_ref, cw_ref, alias_ref,
             dp_ref, dcw_ref, dcb_ref, dbv_ref, extu_ref, extd_ref, du0_ref, acc_ref, sh_ref):
        i = pl.program_id(0)
        h0 = valh_ref[...] * _sigmoid(ggh_ref[...])
        extu_ref[0:HALO, :] = jnp.where(i > 0, h0, 0.0)
        extu_ref[HALO:, :] = val_ref[...] * _sigmoid(gg_ref[...])
        extd_ref[0:tm, :] = d_ref[...]
        extd_ref[tm:, :] = jnp.where(i < nt - 1, dn_ref[...], 0.0)

        @pl.when(i == 0)
        def _():
            acc_ref[...] = jnp.zeros_like(acc_ref)

        def chunk(c, carry):
            col = pl.multiple_of(c * LANES, LANES)
            d = extd_ref[0:tm, pl.ds(col, LANES)]
            du0 = jnp.zeros((tm, LANES), F32)
            for res, taps in _taps_by_residue(lambda j: CONV_KERNEL - 1 - j):
                span = 8 * max(a for _, a in taps) + tm
                sh_ref[0:span, :] = extd_ref[res:res + span, pl.ds(col, LANES)]
                for j, a in taps:
                    du0 = du0 + cw_ref[j:j + 1, pl.ds(col, LANES)] * sh_ref[8 * a:8 * a + tm, :]
            for res, taps in _taps_by_residue(lambda j: HALO - (CONV_KERNEL - 1) + j):
                span = 8 * max(a for _, a in taps) + tm
                sh_ref[0:span, :] = extu_ref[res:res + span, pl.ds(col, LANES)]
                for j, a in taps:
                    prod = d * sh_ref[8 * a:8 * a + tm, :]
                    acc_ref[j * 8:(j + 1) * 8, pl.ds(col, LANES)] += _fold_rows(prod)
            acc_ref[CONV_KERNEL * 8:(CONV_KERNEL + 1) * 8, pl.ds(col, LANES)] += _fold_rows(d)
            du0_ref[:, pl.ds(col, LANES)] = du0
            return carry

        lax.fori_loop(0, nchunk, chunk, 0)
        kb = CONV_KERNEL * 8
        for r in range(0, tm, 16):
            sg = _sigmoid(gg_ref[r:r + 16, :])
            dval = du0_ref[r:r + 16, :] * sg
            dgg = dval * val_ref[r:r + 16, :] * (1.0 - sg)
            dp_ref[r:r + 16, 0:e] = dval.astype(BF16)
            dp_ref[r:r + 16, e:] = dgg.astype(BF16)
            acc_ref[kb + 8:kb + 16, :] += _fold_rows(dval)
            acc_ref[kb + 16:kb + 24, :] += _fold_rows(dgg)

        @pl.when(i == nt - 1)
        def _():
            for j in range(CONV_KERNEL):
                dcw_ref[j:j + 1, :] = jnp.sum(acc_ref[j * 8:(j + 1) * 8, :], axis=0, keepdims=True)
            dcw_ref[CONV_KERNEL:, :] = jnp.zeros((HALO - CONV_KERNEL, e), F32)
            dcb_ref[...] = jnp.sum(acc_ref[kb:kb + 8, :], axis=0, keepdims=True)
            dbv_ref[:, 0:e] = jnp.sum(acc_ref[kb + 8:kb + 16, :], axis=0, keepdims=True)
            dbv_ref[:, e:] = jnp.sum(acc_ref[kb + 16:kb + 24, :], axis=0, keepdims=True)

    row = lambda c: pl.BlockSpec((tm, e), lambda i: (i, c))
    halo_prev = lambda c: pl.BlockSpec((HALO, e), lambda i: (jnp.maximum(i * hb - 1, 0), c))
    halo_next = pl.BlockSpec((HALO, e), lambda i: (jnp.minimum((i + 1) * hb, last_halo), 0))
    return pl.pallas_call(
        body, name="conv_bwd",
        out_shape=(jax.ShapeDtypeStruct(dproj.shape, BF16), jax.ShapeDtypeStruct((HALO, e), F32),
                   jax.ShapeDtypeStruct((1, e), F32), jax.ShapeDtypeStruct((1, 2 * e), F32)),
        grid=(nt,),
        in_specs=[row(0), halo_next, row(0), row(1), halo_prev(0), halo_prev(1),
                  pl.BlockSpec((HALO, e), lambda i: (0, 0)), pl.BlockSpec(memory_space=pl.ANY)],
        out_specs=(pl.BlockSpec((tm, 2 * e), lambda i: (i, 0)), pl.BlockSpec((HALO, e), lambda i: (0, 0)),
                   pl.BlockSpec((1, e), lambda i: (0, 0)), pl.BlockSpec((1, 2 * e), lambda i: (0, 0))),
        scratch_shapes=[pltpu.VMEM((tm + HALO, e), F32), pltpu.VMEM((tm + HALO, e), F32),
                        pltpu.VMEM((tm, e), F32), pltpu.VMEM(((CONV_KERNEL + 3) * 8, e), F32),
                        pltpu.VMEM((tm + HALO, LANES), F32)],
        input_output_aliases={7: 0},
        compiler_params=_cparams(("arbitrary",)),
    )(du1, du1, proj, proj, proj, proj, cw, dproj)


def _adam_update(g, w, m, v):
    c1 = 1.0 - ADAM_B1 ** ADAM_STEP
    c2 = 1.0 - ADAM_B2 ** ADAM_STEP
    mn = ADAM_B1 * m + (1.0 - ADAM_B1) * g
    vn = ADAM_B2 * v + (1.0 - ADAM_B2) * (g * g)
    delta = -ADAM_LR * ((mn / c1) / (jnp.sqrt(vn / c2) + ADAM_EPS) + ADAM_WD * w)
    return delta, mn, vn


def _adamw(name, gbuf, w, m, v):
    parts = list(gbuf) if isinstance(gbuf, (list, tuple)) else [gbuf]
    npart = len(parts)
    r, c = w.shape
    tr = min(r // npart, 256)
    per_part = r // npart // tr
    assert r == npart * per_part * tr and all(p.shape == (N_DEV, r // npart, c) for p in parts)

    def body(*refs):
        g_refs = refs[:npart]
        w_ref, m_ref, v_ref, go_ref, d_ref, mo_ref, vo_ref = refs[npart:]
        i = pl.program_id(0)

        def run(g_ref):
            g = g_ref[0].astype(F32)
            for k in range(1, N_DEV):
                g = g + g_ref[k].astype(F32)
            go_ref[...] = g
            d_ref[...], mo_ref[...], vo_ref[...] = _adam_update(g, w_ref[...], m_ref[...], v_ref[...])

        if npart == 1:
            run(g_refs[0])
        else:
            for q in range(npart):
                pl.when((i >= q * per_part) & (i < (q + 1) * per_part))(functools.partial(run, g_refs[q]))

    blk = pl.BlockSpec((tr, c), lambda i: (i, 0))
    part_spec = lambda q: pl.BlockSpec((N_DEV, tr, c), lambda i: (0, jnp.clip(i - q * per_part, 0, per_part - 1), 0))
    shp = jax.ShapeDtypeStruct((r, c), F32)
    return pl.pallas_call(
        body, name="adamw_" + name,
        out_shape=(shp, shp, shp, shp),
        grid=(r // tr,),
        in_specs=[part_spec(q) for q in range(npart)] + [blk, blk, blk],
        out_specs=(blk, blk, blk, blk),
        compiler_params=_cparams(("parallel",)),
    )(*parts, w, m, v)


SMALL_TABLE_COLS = 256
SMALL_LAYOUT = {
    'ln_g': (0, 2, 1024), 'ln_b': (8, 2, 1024), 'a_b_in': (16, 1, 768), 'a_conv_b': (24, 1, 256),
    'a_norm_g': (32, 1, 256), 'a_norm_b': (40, 1, 256), 'a_b_out': (48, 1, 128), 'kv_norm_g': (56, 1, 256),
    'b_q_norm_g': (64, 1, 512), 'a_conv_w': (72, CONV_KERNEL, 256),
}
LOSS_ROW = 104
SMALL_TABLE_ROWS = 112
SMALL_NAMES = tuple(SMALL_LAYOUT)


def _small_pieces(name, row):
    r0, _, c = SMALL_LAYOUT[name]
    w = min(c, SMALL_TABLE_COLS)
    per_row = c // w
    return [(r0 + row * per_row + q, q * w, w) for q in range(per_row)]


def _pack_small_grads(dg0, dg1, db0, db1, dbv, dbz, dcb, dng, dnb, dba, dgkv, dgq, dcw, loss):
    e = CONV_WIDTH
    ce = e // N_DEV

    def body(dg0_ref, dg1_ref, db0_ref, db1_ref, dbv_ref, dbz_ref, dcb_ref, dng_ref, dnb_ref, dba_ref, dgkv_ref,
             dgq_ref, dcw_ref, loss_ref, o_ref):
        o_ref[...] = jnp.zeros_like(o_ref)
        for d in range(N_DEV):
            o_ref[d, LOSS_ROW:LOSS_ROW + 1, 0:LANES] = loss_ref[...]

        def put(d, name, row, src_ref, first_col=0):
            for trow, col, w in _small_pieces(name, row):
                o_ref[d, trow:trow + 1, 0:w] = src_ref[:, first_col + col:first_col + col + w]

        for d in range(N_DEV):
            put(d, 'ln_g', 0, dg0_ref)
            put(d, 'ln_g', 1, dg1_ref)
            put(d, 'ln_b', 0, db0_ref)
            put(d, 'ln_b', 1, db1_ref)
            for trow, col, w in _small_pieces('a_b_in', 0):
                gcol = d * 3 * ce + col
                src = dbv_ref[:, gcol:gcol + w] if gcol < 2 * e else dbz_ref[:, gcol - 2 * e:gcol - 2 * e + w]
                o_ref[d, trow:trow + 1, 0:w] = src
            put(d, 'a_conv_b', 0, dcb_ref, d * ce)
            put(d, 'a_norm_g', 0, dng_ref, d * ce)
            put(d, 'a_norm_b', 0, dnb_ref, d * ce)
            put(d, 'a_b_out', 0, dba_ref, d * LANES)
            put(d, 'kv_norm_g', 0, dgkv_ref)
            put(d, 'b_q_norm_g', 0, dgq_ref)
            r0 = SMALL_LAYOUT['a_conv_w'][0]
            o_ref[d, r0:r0 + HALO, 0:ce] = dcw_ref[:, d * ce:(d + 1) * ce]

    return pl.pallas_call(
        body, name="pack_small_grads",
        out_shape=jax.ShapeDtypeStruct((N_DEV, SMALL_TABLE_ROWS, SMALL_TABLE_COLS), F32),
        compiler_params=pltpu.CompilerParams(vmem_limit_bytes=VMEM_LIMIT),
    )(dg0, dg1, db0, db1, dbv, dbz, dcb, dng, dnb, dba, dgkv, dgq, dcw, loss)


def _adamw_small(gtab, ws, ms, vs):
    n = len(SMALL_NAMES)

    def body(*refs):
        g_ref = refs[0]
        w_refs, m_refs, v_refs = refs[1:1 + n], refs[1 + n:1 + 2 * n], refs[1 + 2 * n:1 + 3 * n]
        outs = refs[1 + 3 * n:]

        def summed(r0, r, c):
            g = g_ref[0, r0:r0 + r, 0:c]
            for k in range(1, N_DEV):
                g = g + g_ref[k, r0:r0 + r, 0:c]
            return g

        for i, name in enumerate(SMALL_NAMES):
            r0, r, c = SMALL_LAYOUT[name]
            if c <= SMALL_TABLE_COLS:
                blocks = [(slice(None), summed(r0, r, c))]
            else:
                blocks = [(slice(row, row + 1), jnp.concatenate([summed(tr, 1, w) for tr, _, w in _small_pieces(name, row)], axis=1))
                          for row in range(r)]
            for rows, g in blocks:
                delta, mn, vn = _adam_update(g, w_refs[i][rows, :], m_refs[i][rows, :], v_refs[i][rows, :])
                outs[i][rows, :] = g
                outs[n + i][rows, :] = delta
                outs[2 * n + i][rows, :] = mn
                outs[3 * n + i][rows, :] = vn
        outs[4 * n][...] = summed(LOSS_ROW, 1, LANES)

    shapes = tuple(jax.ShapeDtypeStruct(w.shape, F32) for w in ws)
    res = pl.pallas_call(
        body, name="adamw_small",
        out_shape=shapes * 4 + (jax.ShapeDtypeStruct((1, LANES), F32),),
        compiler_params=pltpu.CompilerParams(vmem_limit_bytes=VMEM_LIMIT),
    )(gtab, *ws, *ms, *vs)
    return res[:n], res[n:2 * n], res[2 * n:3 * n], res[3 * n:4 * n], res[4 * n]


def _mesh_peers():
    x, y, c = lax.axis_index("x"), lax.axis_index("y"), lax.axis_index("c")
    me = 4 * x + 2 * y + c
    peers = []
    for k in range(1, N_DEV):
        px = 1 - x if (k >> 2) & 1 else x
        py = 1 - y if (k >> 1) & 1 else y
        pc = 1 - c if k & 1 else c
        peers.append(((px, py, pc), 4 * px + 2 * py + pc))
    return me, peers


_HBM_SPEC = pl.BlockSpec(memory_space=pltpu.HBM)
_SEM_SPEC = pl.BlockSpec(memory_space=pltpu.SEMAPHORE)


ALL_PEERS = tuple(range(N_DEV - 1))
CHIP_LEVEL_PEERS = (0, 1, 3, 5)


def _split_copies(srcs, lands, send_sems, recv_sems, local_sems, gather, which=ALL_PEERS):
    me, peers = _mesh_peers()
    na = len(srcs)
    mine = lambda a, slot: srcs[a] if gather else srcs[a].at[slot]
    local = [pltpu.make_async_copy(mine(a, me), lands[a].at[me], local_sems.at[a]) for a in range(na)]
    sent, received = [], []
    for k, (dev, pid) in enumerate(peers):
        if k not in which:
            continue
        for a in range(na):
            s = a * (N_DEV - 1) + k
            sent.append(pltpu.make_async_remote_copy(
                src_ref=mine(a, pid), dst_ref=lands[a].at[me], send_sem=send_sems.at[s],
                recv_sem=recv_sems.at[s], device_id=dev, device_id_type=pl.DeviceIdType.MESH))
            received.append(pltpu.make_async_remote_copy(
                src_ref=mine(a, pid), dst_ref=lands[a].at[pid], send_sem=send_sems.at[s],
                recv_sem=recv_sems.at[s], device_id=dev, device_id_type=pl.DeviceIdType.MESH))
    return local, sent, received


def _comm_start(name, srcs, gather, after, which=ALL_PEERS):
    na = len(srcs)
    land_structs = [jax.ShapeDtypeStruct(((N_DEV,) + s.shape) if gather else s.shape, s.dtype) for s in srcs]
    extra = [] if after is None else [after]
    n_in = 2 * na + len(extra)

    def body(*refs):
        srcs_r, lands_r = refs[:na], refs[na:2 * na]
        send_sems, recv_sems, local_sems = refs[n_in:n_in + 3]
        token = refs[-1]
        local, sent, _ = _split_copies(srcs_r, lands_r, send_sems, recv_sems, local_sems, gather, which)
        for cp in local + sent:
            cp.start()
        token[...] = jnp.zeros_like(token)

    sem = pltpu.SemaphoreType.DMA((na * (N_DEV - 1),))
    outs = pl.pallas_call(
        body, name=name,
        out_shape=(sem, sem, pltpu.SemaphoreType.DMA((na,)),
                   *[pltpu.HBM(s.shape, s.dtype) for s in srcs],
                   *[pltpu.HBM(s.shape, s.dtype) for s in land_structs],
                   jax.ShapeDtypeStruct((8, LANES), F32)),
        in_specs=[_HBM_SPEC] * (2 * na) + [pl.BlockSpec(memory_space=pl.ANY)] * len(extra),
        out_specs=(_SEM_SPEC, _SEM_SPEC, _SEM_SPEC, *([_HBM_SPEC] * (2 * na)), pl.BlockSpec(memory_space=pltpu.VMEM)),
        input_output_aliases={i: 3 + i for i in range(2 * na)},
        compiler_params=pltpu.CompilerParams(has_side_effects=pltpu.SideEffectType.DATAFLOW_SIDE_EFFECTING),
    )(*[pltpu.with_memory_space_constraint(s, pltpu.HBM) for s in srcs],
      *[pltpu.with_memory_space_constraint(lax.empty(s.shape, s.dtype), pltpu.HBM) for s in land_structs],
      *extra)
    return (outs[:3], outs[3:3 + na], outs[3 + na:3 + 2 * na]), outs[-1]


def _comm_wait(name, handles, gather, after, which=ALL_PEERS):
    (send_sems, recv_sems, local_sems), srcs, lands = handles
    na = len(srcs)

    def body(*refs):
        srcs_r, lands_r = refs[:na], refs[na:2 * na]
        send_r, recv_r, local_r = refs[2 * na:2 * na + 3]
        local, sent, received = _split_copies(srcs_r, lands_r, send_r, recv_r, local_r, gather, which)
        for cp in sent:
            cp.wait_send()
        for cp in received:
            cp.wait_recv()
        for cp in local:
            cp.wait()

    outs = pl.pallas_call(
        body, name=name,
        out_shape=(*[pltpu.HBM(s.shape, s.dtype) for s in srcs], *[pltpu.HBM(s.shape, s.dtype) for s in lands]),
        in_specs=[_HBM_SPEC] * (2 * na) + [_SEM_SPEC] * 3 + [pl.BlockSpec(memory_space=pl.ANY)],
        out_specs=tuple([_HBM_SPEC] * (2 * na)),
        input_output_aliases={i: i for i in range(2 * na)},
        compiler_params=pltpu.CompilerParams(has_side_effects=pltpu.SideEffectType.DATAFLOW_SIDE_EFFECTING),
    )(*srcs, *lands, send_sems, recv_sems, local_sems, after)
    return outs[na:]


def _forward_to_sibling(name, lands):
    na = len(lands)

    def body(*refs):
        ins, outs = refs[:na], refs[na:2 * na]
        send_sems, recv_sems = refs[2 * na:]
        x, y, c = lax.axis_index("x"), lax.axis_index("y"), lax.axis_index("c")
        sibling = (x, y, 1 - c)
        chips = [(1 - x, y), (x, 1 - y), (1 - x, 1 - y)]
        slot = lambda cx, cy, cc: 4 * cx + 2 * cy + cc
        sends = []
        for j, (cx, cy) in enumerate(chips):
            for a in range(na):
                cp = pltpu.make_async_remote_copy(
                    src_ref=ins[a].at[slot(cx, cy, c)], dst_ref=outs[a].at[slot(cx, cy, c)],
                    send_sem=send_sems.at[a, j], recv_sem=recv_sems.at[a, j], device_id=sibling,
                    device_id_type=pl.DeviceIdType.MESH)
                cp.start()
                sends.append(cp)
        for j, (cx, cy) in enumerate(chips):
            for a in range(na):
                pltpu.make_async_remote_copy(
                    src_ref=ins[a].at[slot(cx, cy, 1 - c)], dst_ref=outs[a].at[slot(cx, cy, 1 - c)],
                    send_sem=send_sems.at[a, j], recv_sem=recv_sems.at[a, j], device_id=sibling,
                    device_id_type=pl.DeviceIdType.MESH).wait_recv()
        for cp in sends:
            cp.wait_send()

    hbm = pl.BlockSpec(memory_space=pl.ANY)
    return pl.pallas_call(
        body, name=name,
        out_shape=tuple(jax.ShapeDtypeStruct(s.shape, s.dtype) for s in lands),
        in_specs=[hbm] * na, out_specs=(hbm,) * na,
        input_output_aliases={a: a for a in range(na)},
        scratch_shapes=[pltpu.SemaphoreType.DMA((na, 3)), pltpu.SemaphoreType.DMA((na, 3))],
    )(*lands)


def _prep_weights(w):
    e = CONV_WIDTH
    p = {}
    if 'a_w_in' in w:
        if w['a_w_in'].shape == (N_DEV, D_MODEL, 3 * e // N_DEV):
            p['a_w_in3'] = w['a_w_in']
        else:
            p['a_w_in3'] = w['a_w_in'].reshape(D_MODEL, N_DEV, 3 * e // N_DEV).transpose(1, 0, 2)
        p['a_b_in'] = w['a_b_in'].reshape(1, 3 * e)
        p['a_conv_w'] = jnp.pad(w['a_conv_w'].reshape(CONV_KERNEL, e), ((0, HALO - CONV_KERNEL), (0, 0)))
        p['a_conv_b'] = w['a_conv_b'].reshape(1, e)
        p['a_norm_g'] = w['a_norm_g'].reshape(1, e)
        p['a_norm_b'] = w['a_norm_b'].reshape(1, e)
        p['a_b_out'] = w['a_b_out'].reshape(1, D_MODEL)
        p['kv_norm_g'] = w['kv_norm_g'].reshape(1, KV_LORA_RANK)
        p['b_q_norm_g'] = w['b_q_norm_g'].reshape(1, Q_LORA_RANK)
        p['ln_g'] = w['ln_g']
        p['ln_b'] = w['ln_b']
    if 'a_w_out' in w:
        p['a_w_out'] = w['a_w_out'].reshape(e, D_MODEL)
        p['kv_w_down'] = jnp.pad(w['kv_w_down'], ((0, 0), (0, KV_LORA_RANK + LANES - w['kv_w_down'].shape[1])))
        p['kv_w_ukv'] = jnp.concatenate([w['kv_w_uk'], w['kv_w_uv']], axis=2).reshape(KV_LORA_RANK, N_HEADS * HEAD_PAD)
        p['b_w_in'] = w['b_w_in'].reshape(D_MODEL, -1)
        uq = w['b_w_uq'].reshape(Q_LORA_RANK, N_HEADS, QK_NOPE_DIM + QK_ROPE_DIM)
        p['b_w_uq'] = jnp.pad(uq, ((0, 0), (0, 0), (0, HEAD_PAD - uq.shape[2]))).reshape(Q_LORA_RANK, N_HEADS * HEAD_PAD)
        p['b_w_out'] = w['b_w_out'].reshape(N_HEADS * V_HEAD_DIM, D_MODEL)
    return p


def _local_step(x, positions, target, comm):
    t = x.shape[0]
    e = CONV_WIDTH
    freqs = ROPE_THETA ** (-jnp.arange(0, QK_ROPE_DIM, 2, dtype=F32) / QK_ROPE_DIM)
    freq_row = jnp.concatenate([freqs, freqs, jnp.zeros((LANES - QK_ROPE_DIM,), F32)]).reshape(1, LANES)
    rc, rs, xb = _rope_tables(positions.reshape(t, 1), freq_row, x, comm.start_token())
    p = comm.first_weights(rc)
    ln_g0, ln_b0 = p['ln_g'][0:1], p['ln_b'][0:1]
    ln_g1, ln_b1 = p['ln_g'][1:2], p['ln_b'][1:2]

    proj = _matmul(xb, p['a_w_in3'], bias=p['a_b_in'], name="a_in", b_blocked=True, bm=2048,
                   after=comm.first_after())
    u1, u2 = _conv_mid_fwd(proj, p['a_conv_w'], p['a_conv_b'], p['a_norm_g'], p['a_norm_b'])
    p = {**p, **comm.rest_weights(u2)}
    h1, h1b, xh1, rstd1 = _ln_res_fwd(x, u2, p['a_w_out'], p['a_b_out'], ln_g0, ln_b0)
    ckv = _matmul(h1b, p['kv_w_down'], name="kv_down")
    ckn, kr = _kv_mid_fwd(ckv, p['kv_norm_g'], rc, rs)
    kv = _matmul(ckn, p['kv_w_ukv'], name="kv_up", out_dtype=BF16)
    projb = _matmul(h1b, p['b_w_in'], name="b_in", bn=1280)
    cqn = _q_norm_fwd(projb, p['b_q_norm_g'])
    qcat = _q_up_rope(cqn, p['b_w_uq'], rc, rs)
    o, o2, lse = _flash_fwd(qcat, kv, kr, projb)
    loss, dr2, dr2b, dg1, db1 = _final_ln_loss(h1, o2, p['b_w_out'], ln_g1, ln_b1, target)

    g = {}
    g['b_w_out'] = _matmul(o2, dr2b, trans_a=True, name="d_b_out", out_dtype=BF16, bm=512, bk=t)
    dob, dprojb, dl = _gate_bwd(dr2b, p['b_w_out'], o, projb)
    dq2, dkv, dkr = _flash_bwd(qcat, kv, kr, dob, lse, dl, rc, rs)
    duq = _matmul(cqn, dq2, trans_a=True, name="d_q_up", out_dtype=BF16, bk=t)
    dcqn = _matmul(dq2, p['b_w_uq'], trans_b=True, name="d_cqn", bk=N_HEADS * HEAD_PAD)
    dprojb, dgq = _q_norm_bwd(dcqn, projb, p['b_q_norm_g'], dprojb)
    g['b_w_in'] = _matmul(h1b, dprojb, trans_a=True, name="d_b_in", bn=640, bk=t, out_dtype=BF16)
    dukv = _matmul(ckn, dkv, trans_a=True, name="d_kv_up", out_dtype=BF16, bk=t)
    dckn = _matmul(dkv, p['kv_w_ukv'], trans_b=True, name="d_ckn", bk=N_HEADS * HEAD_PAD)
    dckv, dgkv = _kv_mid_bwd(dckn, ckv, dkr, p['kv_norm_g'], rc, rs)
    ddown = _matmul(h1b, dckv, trans_a=True, name="d_kv_down", out_dtype=BF16, bm=512, bk=t)
    g['b_w_out'] = g['b_w_out'].reshape(N_DEV, -1, D_MODEL)
    g['b_w_in'] = g['b_w_in'].reshape(D_MODEL, N_DEV, -1).transpose(1, 0, 2)
    g['kv_w_down'] = ddown[:, :KV_LORA_RANK + QK_ROPE_DIM].reshape(N_DEV, -1, KV_LORA_RANK + QK_ROPE_DIM)
    dukv = dukv.reshape(KV_LORA_RANK, N_HEADS, HEAD_PAD)
    g['kv_w_uk'] = dukv[:, :, :QK_NOPE_DIM].reshape(N_DEV, -1, QK_NOPE_DIM)
    g['kv_w_uv'] = dukv[:, :, QK_NOPE_DIM:].reshape(N_DEV, -1, V_HEAD_DIM)
    g['b_w_uq'] = duq.reshape(Q_LORA_RANK, N_HEADS, HEAD_PAD)[:, :, :QK_NOPE_DIM + QK_ROPE_DIM].reshape(
        N_DEV, -1, QK_NOPE_DIM + QK_ROPE_DIM)
    sent1 = comm.send_group1(g)
    dr1, dr1b, dg0, db0, dba_out = _ln0_bwd(dr2, dprojb, p['b_w_in'], dckv, p['kv_w_down'], xh1, rstd1, ln_g0, sent1)
    dw_out = _matmul(u2, dr1b, trans_a=True, name="d_a_out", out_dtype=BF16, bm=512, bk=t).reshape(N_DEV, -1, D_MODEL)
    sent1b = comm.send_group1b({'a_w_out': dw_out})
    du1, dproj, dng, dnb, dbz = _conv_mid_bwd_rows(u1, proj, dr1b, p['a_w_out'], p['a_norm_g'], p['a_norm_b'], sent1b)
    dproj, dcw, dcb, dbv = _conv_bwd(du1, proj, p['a_conv_w'], dproj)
    half = D_MODEL // 2
    dw_lo = _matmul(xb, dproj, trans_a=True, name="d_a_in_lo", out_dtype=BF16, bm=half, bn=3 * e // N_DEV, bk=t,
                    out_blocked=True, a_cols=(0, half))
    small = (dg0, dg1, db0, db1, dbv, dbz, dcb, dng, dnb, dba_out, dgkv, dgq, dcw, loss)
    sent2 = comm.send_group2(dw_lo, small)
    dw_hi = _matmul(xb, dproj, trans_a=True, name="d_a_in_hi", out_dtype=BF16, bm=half, bn=3 * e // N_DEV, bk=t,
                    out_blocked=True, after=sent2, a_cols=(half, half))
    sent3 = comm.send_group3(dw_hi)
    grad_x = _grad_x(dproj, p['a_w_in3'], dr1, sent3)
    return loss, grad_x


def _grad_x(dproj, w3, dr1, after):
    t, d = dr1.shape
    nblk, _, cb = w3.shape
    tm = min(t, 512)
    extra = [] if after is None else [after]

    def body(a_ref, w_ref, dr_ref, *rest):
        o_ref = rest[-1]
        acc = ALPHA * dr_ref[...]
        for k in range(nblk):
            acc = acc + lax.dot_general(a_ref[:, k * cb:(k + 1) * cb], w_ref[k], NT, preferred_element_type=F32)
        o_ref[...] = acc

    row = pl.BlockSpec((tm, d), lambda i: (i, 0))
    return pl.pallas_call(
        body, name="grad_x",
        out_shape=jax.ShapeDtypeStruct((t, d), F32),
        grid=(t // tm,),
        in_specs=[pl.BlockSpec((tm, nblk * cb), lambda i: (i, 0)),
                  pl.BlockSpec((nblk, d, cb), lambda i: (0, 0, 0), pipeline_mode=pl.Buffered(1)), row]
                 + [pl.BlockSpec(memory_space=pl.ANY)] * len(extra),
        out_specs=row,
        compiler_params=_cparams(("parallel",)),
    )(dproj, w3, dr1, *extra)


def _shard_2d(a):
    return a.reshape(-1, a.shape[-1])


def _gathered_to_full(name, blocks):
    if name == 'a_w_in':
        return blocks
    if name == 'b_w_in':
        return blocks.transpose(1, 0, 2).reshape(D_MODEL, -1)
    if name == 'a_conv_w':
        return blocks.transpose(1, 0, 2).reshape(CONV_KERNEL, -1)
    if name in ('a_b_in', 'a_conv_b', 'a_norm_g', 'a_norm_b', 'a_b_out'):
        return blocks.reshape(-1)
    if name in ('kv_w_uk', 'kv_w_uv'):
        return blocks.reshape(KV_LORA_RANK, N_HEADS, -1)
    if name == 'b_w_uq':
        return blocks.reshape(Q_LORA_RANK, N_HEADS, -1)
    return blocks.reshape(-1, blocks.shape[-1])


def kernel(x, positions, ln_g, ln_b, a_w_in, a_b_in, a_conv_w, a_conv_b, a_norm_g, a_norm_b, a_w_out, a_b_out, kv_w_down, kv_norm_g, kv_w_uk, kv_w_uv, b_w_in, b_q_norm_g, b_w_uq, b_w_out, loss_target, m_ln_g, m_ln_b, m_a_w_in, m_a_b_in, m_a_conv_w, m_a_conv_b, m_a_norm_g, m_a_norm_b, m_a_w_out, m_a_b_out, m_kv_w_down, m_kv_norm_g, m_kv_w_uk, m_kv_w_uv, m_b_w_in, m_b_q_norm_g, m_b_w_uq, m_b_w_out, v_ln_g, v_ln_b, v_a_w_in, v_a_b_in, v_a_conv_w, v_a_conv_b, v_a_norm_g, v_a_norm_b, v_a_w_out, v_a_b_out, v_kv_w_down, v_kv_norm_g, v_kv_w_uk, v_kv_w_uv, v_b_w_in, v_b_q_norm_g, v_b_w_uq, v_b_w_out):
    w = dict(ln_g=ln_g, ln_b=ln_b, a_w_in=a_w_in, a_b_in=a_b_in, a_conv_w=a_conv_w, a_conv_b=a_conv_b,
             a_norm_g=a_norm_g, a_norm_b=a_norm_b, a_w_out=a_w_out, a_b_out=a_b_out, kv_w_down=kv_w_down,
             kv_norm_g=kv_norm_g, kv_w_uk=kv_w_uk, kv_w_uv=kv_w_uv, b_w_in=b_w_in, b_q_norm_g=b_q_norm_g,
             b_w_uq=b_w_uq, b_w_out=b_w_out)
    m = dict(ln_g=m_ln_g, ln_b=m_ln_b, a_w_in=m_a_w_in, a_b_in=m_a_b_in, a_conv_w=m_a_conv_w, a_conv_b=m_a_conv_b,
             a_norm_g=m_a_norm_g, a_norm_b=m_a_norm_b, a_w_out=m_a_w_out, a_b_out=m_a_b_out, kv_w_down=m_kv_w_down,
             kv_norm_g=m_kv_norm_g, kv_w_uk=m_kv_w_uk, kv_w_uv=m_kv_w_uv, b_w_in=m_b_w_in, b_q_norm_g=m_b_q_norm_g,
             b_w_uq=m_b_w_uq, b_w_out=m_b_w_out)
    v = dict(ln_g=v_ln_g, ln_b=v_ln_b, a_w_in=v_a_w_in, a_b_in=v_a_b_in, a_conv_w=v_a_conv_w, a_conv_b=v_a_conv_b,
             a_norm_g=v_a_norm_g, a_norm_b=v_a_norm_b, a_w_out=v_a_w_out, a_b_out=v_a_b_out, kv_w_down=v_kv_w_down,
             kv_norm_g=v_kv_norm_g, kv_w_uk=v_kv_w_uk, kv_w_uv=v_kv_w_uv, b_w_in=v_b_w_in, b_q_norm_g=v_b_q_norm_g,
             b_w_uq=v_b_w_uq, b_w_out=v_b_w_out)

    small_flat = jnp.concatenate([w[n].reshape(-1) for n in SMALL_WEIGHTS]).reshape(1, -1)
    rest_names = [n for n in MATMUL_WEIGHTS if n != 'a_w_in']
    group1 = ('b_w_out', 'b_w_uq', 'b_w_in', 'kv_w_uk', 'kv_w_uv', 'kv_w_down')
    group1b = ('a_w_out',)

    class Comm:
        def __init__(self):
            self.first, self.first_token = _comm_start(
                "gather_first_start", [_shard_2d(w['a_w_in']).astype(BF16), small_flat], True, None, CHIP_LEVEL_PEERS)

        def start_token(self):
            return self.first_token

        def first_weights(self, after):
            lands = _comm_wait("gather_first_wait", self.first, True, after, CHIP_LEVEL_PEERS)
            ga = _forward_to_sibling("gather_first_forward", list(lands))
            full = {'a_w_in': ga[0]}
            gs = ga[1].reshape(N_DEV, -1)
            off = 0
            for n in SMALL_WEIGHTS:
                size = math.prod(w[n].shape)
                full[n] = _gathered_to_full(n, gs[:, off:off + size].reshape((N_DEV,) + _shard_2d(w[n]).shape))
                off += size
            for n in REPLICATED:
                full[n] = w[n]
            dense = lambda a: a if a.shape[-1] % LANES == 0 else a.reshape(-1, LANES)
            self.rest, self.rest_token = _comm_start(
                "gather_rest_start", [dense(_shard_2d(w[n]).astype(BF16)) for n in rest_names], True, ga[0])
            return _prep_weights(full)

        def first_after(self):
            return self.rest_token

        def rest_weights(self, after):
            lands = _comm_wait("gather_rest_wait", self.rest, True, after)
            blocks = {n: lands[i].reshape((N_DEV,) + _shard_2d(w[n]).shape) for i, n in enumerate(rest_names)}
            return _prep_weights({n: _gathered_to_full(n, blocks[n]) for n in rest_names})

        def send_group1(self, g):
            self.g1, token = _comm_start("exchange_g1_start", [g[n] for n in group1], False, None)
            return token

        def send_group1b(self, g):
            self.g1b, token = _comm_start("exchange_g1b_start", [g[n] for n in group1b], False, None)
            return token

        def send_group2(self, dw_lo, small):
            self.g2, token = _comm_start("exchange_g2_start", [dw_lo, _pack_small_grads(*small)], False, None)
            return token

        def send_group3(self, dw_hi):
            self.g3, token = _comm_start("exchange_g3_start", [dw_hi], False, None)
            return token

    comm = Comm()

    _, grad_x = _local_step(x[0], positions[0], loss_target[0], comm)

    res = {}
    recv1 = _comm_wait("exchange_g1_wait", comm.g1, False, grad_x)
    for i, n in enumerate(group1):
        res[n] = _adamw(n, recv1[i], _shard_2d(w[n]), _shard_2d(m[n]), _shard_2d(v[n]))
    recv1b = _comm_wait("exchange_g1b_wait", comm.g1b, False, res[group1[-1]][0])
    for i, n in enumerate(group1b):
        res[n] = _adamw(n, recv1b[i], _shard_2d(w[n]), _shard_2d(m[n]), _shard_2d(v[n]))
    recv2 = _comm_wait("exchange_g2_wait", comm.g2, False, res[group1b[-1]][0])
    recv3 = _comm_wait("exchange_g3_wait", comm.g3, False, recv2[0])
    res['a_w_in'] = _adamw('a_w_in', [recv2[0], recv3[0]], _shard_2d(w['a_w_in']), _shard_2d(m['a_w_in']),
                           _shard_2d(v['a_w_in']))
    two_d = lambda d: [_shard_2d(d[n]) if d[n].ndim > 1 else d[n].reshape(1, -1) for n in SMALL_NAMES]
    sg, sd, sm, sv, loss = _adamw_small(recv2[-1], two_d(w), two_d(m), two_d(v))
    for i, n in enumerate(SMALL_NAMES):
        res[n] = (sg[i], sd[i], sm[i], sv[i])
    outs = [[res[n][j].reshape(w[n].shape) for n in WEIGHT_NAMES] for j in range(4)]
    return (loss[0, 0], grad_x[None], *outs[0], *outs[1], *outs[2], *outs[3])
```

```python
import functools
import math

import jax
import jax.numpy as jnp
from jax import lax
from jax.experimental import pallas as pl
from jax.experimental.pallas import tpu as pltpu

F32 = jnp.float32
BF16 = jnp.bfloat16

D_MODEL = 1024
DEPTH = 2
CONV_WIDTH = 2 * D_MODEL
CONV_KERNEL = 31
N_HEADS = 16
QK_NOPE_DIM = 128
QK_ROPE_DIM = 64
V_HEAD_DIM = 128
KV_LORA_RANK = D_MODEL // 4
Q_LORA_RANK = D_MODEL // 2
ROPE_THETA = 10000.0
LN_EPS = 1e-5
RMS_EPS = 1e-6
MASK_VALUE = -1e30
ALPHA = (2.0 * DEPTH) ** 0.25
ATTN_SCALE = 1.0 / math.sqrt(QK_NOPE_DIM + QK_ROPE_DIM)

ADAM_LR = 0.001
ADAM_B1 = 0.9
ADAM_B2 = 0.999
ADAM_EPS = 1e-08
ADAM_WD = 0.01
ADAM_STEP = 10

N_DEV = 8
LANES = 128
HEAD_PAD = 256
HALO = 32
VMEM_LIMIT = 56 * 1024 * 1024

WEIGHT_NAMES = ['ln_g', 'ln_b', 'a_w_in', 'a_b_in', 'a_conv_w', 'a_conv_b', 'a_norm_g', 'a_norm_b',
                'a_w_out', 'a_b_out', 'kv_w_down', 'kv_norm_g', 'kv_w_uk', 'kv_w_uv', 'b_w_in',
                'b_q_norm_g', 'b_w_uq', 'b_w_out']
REPLICATED = ('ln_g', 'ln_b', 'kv_norm_g', 'b_q_norm_g')
MATMUL_WEIGHTS = ('a_w_in', 'a_w_out', 'kv_w_down', 'kv_w_uk', 'kv_w_uv', 'b_w_in', 'b_w_uq', 'b_w_out')
SMALL_WEIGHTS = ('a_b_in', 'a_conv_w', 'a_conv_b', 'a_norm_g', 'a_norm_b', 'a_b_out')

NN = (((1,), (0,)), ((), ()))
NT = (((1,), (1,)), ((), ()))
TN = (((0,), (0,)), ((), ()))


def _cparams(sem):
    return pltpu.CompilerParams(dimension_semantics=sem, vmem_limit_bytes=VMEM_LIMIT)


def _sigmoid(x):
    return 0.5 * jnp.tanh(0.5 * x) + 0.5


def _fold_rows(x):
    parts = [x[r:r + 8] for r in range(0, x.shape[0], 8)]
    while len(parts) > 1:
        parts = [parts[i] + parts[i + 1] for i in range(0, len(parts) - 1, 2)] + ([parts[-1]] if len(parts) % 2 else [])
    return parts[0]


def _taps_by_residue(offset_of):
    groups = {}
    for j in range(CONV_KERNEL):
        off = offset_of(j)
        groups.setdefault(off % 8, []).append((j, off // 8))
    return sorted(groups.items())


def _swap_rope_halves(t):
    lane = lax.broadcasted_iota(jnp.int32, t.shape, 1)
    return jnp.where(lane < QK_ROPE_DIM // 2, pltpu.roll(t, LANES - QK_ROPE_DIM // 2, 1),
                     pltpu.roll(t, QK_ROPE_DIM // 2, 1))


def _matmul(a, b, *, name, bias=None, trans_a=False, trans_b=False, out_dtype=F32, bm=1024, bn=1024, bk=1024,
            b_blocked=False, out_blocked=False, after=None, a_cols=None):
    m_off = 0
    if trans_a:
        kdim, m = a.shape
        if a_cols is not None:
            m_off, m = a_cols
    else:
        m, kdim = a.shape
    if b_blocked and trans_b:
        n, k2 = b.shape[1], b.shape[0] * b.shape[2]
        bk = b.shape[2]
    elif b_blocked:
        k2, n = b.shape[1], b.shape[0] * b.shape[2]
        bn = b.shape[2]
    elif trans_b:
        n, k2 = b.shape
    else:
        k2, n = b.shape
    assert kdim == k2, (a.shape, b.shape)
    bm, bn, bk = min(bm, m), min(bn, n), min(bk, kdim)
    assert m % bm == 0 and n % bn == 0 and kdim % bk == 0, (name, m, n, kdim, bm, bn, bk)
    nk = kdim // bk
    dims = (((0 if trans_a else 1,), (1 if trans_b else 0,)), ((), ()))
    has_bias = bias is not None

    def body(*refs):
        refs = list(refs)
        a_ref, b_ref = refs[:2]
        del refs[:2]
        bias_ref = refs.pop(0) if has_bias else None
        if after is not None:
            refs.pop(0)
        o_ref = refs.pop(0)
        rest = refs
        prod = lax.dot_general(a_ref[...], b_ref[...], dims, preferred_element_type=F32)

        def finish(acc):
            if has_bias:
                acc = acc + bias_ref[...]
            o_ref[...] = acc.astype(out_dtype)

        if nk == 1:
            finish(prod)
        else:
            acc_ref = rest[0]
            k = pl.program_id(2)

            @pl.when(k == 0)
            def _():
                acc_ref[...] = prod

            @pl.when(k > 0)
            def _():
                acc_ref[...] += prod

            @pl.when(k == nk - 1)
            def _():
                finish(acc_ref[...])

    assert m_off % bm == 0
    a_first = m_off // bm
    a_spec = (pl.BlockSpec((bk, bm), lambda i, j, k: (k, i + a_first)) if trans_a
              else pl.BlockSpec((bm, bk), lambda i, j, k: (i, k)))
    if b_blocked and trans_b:
        b_spec = pl.BlockSpec((None, bn, bk), lambda i, j, k: (k, j, 0))
    elif b_blocked:
        b_spec = pl.BlockSpec((None, bk, bn), lambda i, j, k: (j, k, 0))
    elif trans_b:
        b_spec = pl.BlockSpec((bn, bk), lambda i, j, k: (j, k))
    else:
        b_spec = pl.BlockSpec((bk, bn), lambda i, j, k: (k, j))
    in_specs = [a_spec, b_spec]
    args = [a, b]
    if has_bias:
        in_specs.append(pl.BlockSpec((1, bn), lambda i, j, k: (0, j)))
        args.append(bias)
    if after is not None:
        in_specs.append(pl.BlockSpec(memory_space=pl.ANY))
        args.append(after)
    if out_blocked:
        out_struct = jax.ShapeDtypeStruct((n // bn, m, bn), out_dtype)
        out_spec = pl.BlockSpec((None, bm, bn), lambda i, j, k: (j, i, 0))
    else:
        out_struct = jax.ShapeDtypeStruct((m, n), out_dtype)
        out_spec = pl.BlockSpec((bm, bn), lambda i, j, k: (i, j))
    return pl.pallas_call(
        body, name=name,
        out_shape=out_struct,
        grid=(m // bm, n // bn, nk),
        in_specs=in_specs,
        out_specs=out_spec,
        scratch_shapes=[pltpu.VMEM((bm, bn), F32)] if nk > 1 else [],
        compiler_params=_cparams(("parallel", "parallel", "arbitrary")),
    )(*args)


def _rope_tables(pos_col, freq_row, x, after):
    t, d = x.shape
    tm = min(t, 512)
    extra = [] if after is None else [after]

    def body(pos_ref, f_ref, x_ref, *rest):
        c_ref, s_ref, xb_ref = rest[-3:]
        ang = pos_ref[...].astype(F32) * f_ref[...]
        lane = lax.broadcasted_iota(jnp.int32, ang.shape, 1)
        cos = jnp.cos(ang)
        sin = jnp.sin(ang)
        c_ref[...] = jnp.where(lane < QK_ROPE_DIM, cos, 0.0)
        s_ref[...] = jnp.where(lane < QK_ROPE_DIM // 2, -sin, jnp.where(lane < QK_ROPE_DIM, sin, 0.0))
        xb_ref[...] = x_ref[...].astype(BF16)

    lane_blk = pl.BlockSpec((tm, LANES), lambda i: (i, 0))
    row = pl.BlockSpec((tm, d), lambda i: (i, 0))
    return pl.pallas_call(
        body, name="rope_tables",
        out_shape=(jax.ShapeDtypeStruct((t, LANES), F32), jax.ShapeDtypeStruct((t, LANES), F32),
                   jax.ShapeDtypeStruct((t, d), BF16)),
        grid=(t // tm,),
        in_specs=[pl.BlockSpec((tm, 1), lambda i: (i, 0)), pl.BlockSpec((1, LANES), lambda i: (0, 0)), row]
                 + [pl.BlockSpec(memory_space=pl.ANY)] * len(extra),
        out_specs=(lane_blk, lane_blk, row),
        compiler_params=_cparams(("parallel",)),
    )(pos_col, freq_row, x, *extra)


def _conv_mid_fwd(proj, cw, cb, ng, nb):
    t = proj.shape[0]
    e = CONV_WIDTH
    tm = min(t, 128)
    hb = tm // HALO
    nchunk = e // LANES

    def body(val_ref, gg_ref, z_ref, valh_ref, ggh_ref, cw_ref, cb_ref, ng_ref, nb_ref, u1_ref, u2_ref, ext_ref, sh_ref):
        i = pl.program_id(0)
        u0 = val_ref[...] * _sigmoid(gg_ref[...])
        h0 = valh_ref[...] * _sigmoid(ggh_ref[...])
        ext_ref[0:HALO, :] = jnp.where(i > 0, h0, 0.0)
        ext_ref[HALO:, :] = u0

        def chunk(c, carry):
            col = pl.multiple_of(c * LANES, LANES)
            acc = jnp.zeros((tm, LANES), F32)
            for res, taps in _taps_by_residue(lambda j: HALO - (CONV_KERNEL - 1) + j):
                span = 8 * max(a for _, a in taps) + tm
                sh_ref[0:span, :] = ext_ref[res:res + span, pl.ds(col, LANES)]
                for j, a in taps:
                    acc = acc + cw_ref[j:j + 1, pl.ds(col, LANES)] * sh_ref[8 * a:8 * a + tm, :]
            u1_ref[:, pl.ds(col, LANES)] = acc + cb_ref[:, pl.ds(col, LANES)]
            return carry

        lax.fori_loop(0, nchunk, chunk, 0)
        g = ng_ref[...]
        b = nb_ref[...]
        for r in range(0, tm, 16):
            u1 = u1_ref[r:r + 16, :]
            mu = jnp.mean(u1, axis=-1, keepdims=True)
            xc = u1 - mu
            var = jnp.mean(xc * xc, axis=-1, keepdims=True)
            n = xc * lax.rsqrt(var + LN_EPS) * g + b
            z = z_ref[r:r + 16, :]
            u2_ref[r:r + 16, :] = ((n * _sigmoid(n)) * (z * _sigmoid(z))).astype(BF16)

    row = lambda c: pl.BlockSpec((tm, e), lambda i: (i, c))
    halo = lambda c: pl.BlockSpec((HALO, e), lambda i: (jnp.maximum(i * hb - 1, 0), c))
    par = lambda r: pl.BlockSpec((r, e), lambda i: (0, 0))
    return pl.pallas_call(
        body, name="conv_mid_fwd",
        out_shape=(jax.ShapeDtypeStruct((t, e), F32), jax.ShapeDtypeStruct((t, e), BF16)),
        grid=(t // tm,),
        in_specs=[row(0), row(1), row(2), halo(0), halo(1), par(HALO), par(1), par(1), par(1)],
        out_specs=(pl.BlockSpec((tm, e), lambda i: (i, 0)), pl.BlockSpec((tm, e), lambda i: (i, 0))),
        scratch_shapes=[pltpu.VMEM((tm + HALO, e), F32), pltpu.VMEM((tm + HALO, LANES), F32)],
        compiler_params=_cparams(("parallel",)),
    )(proj, proj, proj, proj, proj, cw, cb, ng, nb)


def _ln_res_fwd(x, u, w_out, b_out, g, b):
    t, d = x.shape
    ku = u.shape[1]
    tm = min(t, 512)

    def body(x_ref, u_ref, w_ref, bo_ref, g_ref, b_ref, h_ref, hb_ref, xh_ref, rs_ref):
        y = lax.dot_general(u_ref[...], w_ref[...], NN, preferred_element_type=F32) + bo_ref[...]
        r = ALPHA * x_ref[...] + y
        mu = jnp.mean(r, axis=-1, keepdims=True)
        xc = r - mu
        var = jnp.mean(xc * xc, axis=-1, keepdims=True)
        rstd = lax.rsqrt(var + LN_EPS)
        xh = xc * rstd
        h = xh * g_ref[...] + b_ref[...]
        h_ref[...] = h
        hb_ref[...] = h.astype(BF16)
        xh_ref[...] = xh
        rs_ref[...] = rstd

    row = pl.BlockSpec((tm, d), lambda i: (i, 0))
    par = pl.BlockSpec((1, d), lambda i: (0, 0))
    return pl.pallas_call(
        body, name="ln0_fwd",
        out_shape=(jax.ShapeDtypeStruct((t, d), F32), jax.ShapeDtypeStruct((t, d), BF16),
                   jax.ShapeDtypeStruct((t, d), F32), jax.ShapeDtypeStruct((t, 1), F32)),
        grid=(t // tm,),
        in_specs=[row, pl.BlockSpec((tm, ku), lambda i: (i, 0)),
                  pl.BlockSpec((ku, d), lambda i: (0, 0), pipeline_mode=pl.Buffered(1)), par, par, par],
        out_specs=(row, row, row, pl.BlockSpec((tm, 1), lambda i: (i, 0))),
        compiler_params=_cparams(("parallel",)),
    )(x, u, w_out, b_out, g, b)


def _kv_mid_fwd(h1b, w_down, g, rc, rs):
    t, d = h1b.shape
    tm = min(t, 512)
    r = KV_LORA_RANK

    def body(h_ref, w_ref, g_ref, c_ref, s_ref, ckv_ref, ckn_ref, kr_ref):
        ckv = lax.dot_general(h_ref[...], w_ref[...], NN, preferred_element_type=F32)
        ckv_ref[...] = ckv
        c = ckv[:, 0:r]
        ms = jnp.mean(c * c, axis=-1, keepdims=True)
        ckn_ref[...] = (c * lax.rsqrt(ms + RMS_EPS) * g_ref[...]).astype(BF16)
        tr = ckv[:, r:r + LANES]
        kr_ref[...] = (tr * c_ref[...] + _swap_rope_halves(tr) * s_ref[...]).astype(BF16)

    lane_blk = pl.BlockSpec((tm, LANES), lambda i: (i, 0))
    return pl.pallas_call(
        body, name="kv_mid_fwd",
        out_shape=(jax.ShapeDtypeStruct((t, r + LANES), F32), jax.ShapeDtypeStruct((t, r), BF16),
                   jax.ShapeDtypeStruct((t, LANES), BF16)),
        grid=(t // tm,),
        in_specs=[pl.BlockSpec((tm, d), lambda i: (i, 0)),
                  pl.BlockSpec((d, r + LANES), lambda i: (0, 0), pipeline_mode=pl.Buffered(1)),
                  pl.BlockSpec((1, r), lambda i: (0, 0)), lane_blk, lane_blk],
        out_specs=(pl.BlockSpec((tm, r + LANES), lambda i: (i, 0)), pl.BlockSpec((tm, r), lambda i: (i, 0)), lane_blk),
        compiler_params=_cparams(("parallel",)),
    )(h1b, w_down, g, rc, rs)


def _q_norm_fwd(projb, g):
    t = projb.shape[0]
    tm = min(t, 512)
    r = Q_LORA_RANK

    def body(c_ref, g_ref, o_ref):
        c = c_ref[...]
        ms = jnp.mean(c * c, axis=-1, keepdims=True)
        o_ref[...] = (c * lax.rsqrt(ms + RMS_EPS) * g_ref[...]).astype(BF16)

    return pl.pallas_call(
        body, name="q_norm_fwd",
        out_shape=jax.ShapeDtypeStruct((t, r), BF16),
        grid=(t // tm,),
        in_specs=[pl.BlockSpec((tm, r), lambda i: (i, 0)), pl.BlockSpec((1, r), lambda i: (0, 0))],
        out_specs=pl.BlockSpec((tm, r), lambda i: (i, 0)),
        compiler_params=_cparams(("parallel",)),
    )(projb, g)


def _q_up_rope(cqn, wuq, rc, rs):
    t, r = cqn.shape
    w = wuq.shape[1]
    tm = min(t, 1024)
    bn = 4 * HEAD_PAD

    def body(a_ref, b_ref, c_ref, s_ref, o_ref):
        q = lax.dot_general(a_ref[...], b_ref[...], NN, preferred_element_type=F32)
        c = c_ref[...]
        s = s_ref[...]
        for h in range(bn // HEAD_PAD):
            a = h * HEAD_PAD
            o_ref[:, a:a + LANES] = q[:, a:a + LANES].astype(BF16)
            tr = q[:, a + LANES:a + 2 * LANES]
            o_ref[:, a + LANES:a + 2 * LANES] = (tr * c + _swap_rope_halves(tr) * s).astype(BF16)

    return pl.pallas_call(
        body, name="q_up_rope",
        out_shape=jax.ShapeDtypeStruct((t, w), BF16),
        grid=(t // tm, w // bn),
        in_specs=[pl.BlockSpec((tm, r), lambda i, j: (i, 0)), pl.BlockSpec((r, bn), lambda i, j: (0, j)),
                  pl.BlockSpec((tm, LANES), lambda i, j: (i, 0)), pl.BlockSpec((tm, LANES), lambda i, j: (i, 0))],
        out_specs=pl.BlockSpec((tm, bn), lambda i, j: (i, j)),
        compiler_params=_cparams(("parallel", "parallel")),
    )(cqn, wuq, rc, rs)


def _flash_fwd(qcat, kv, kr, projb):
    t = qcat.shape[0]
    tq = min(t, 512)
    nq = t // tq
    exp2_scale = ATTN_SCALE * math.log2(math.e)
    z_first = Q_LORA_RANK // LANES

    def body(q_ref, kv_ref, kr_ref, z_ref, o_ref, o2_ref, lse_ref, kcat_ref):
        kcat_ref[:, 0:LANES] = kv_ref[:, 0:LANES]
        kcat_ref[:, LANES:] = kr_ref[...]
        rows = lax.broadcasted_iota(jnp.int32, (tq, tq), 0)
        cols = lax.broadcasted_iota(jnp.int32, (tq, tq), 1)
        for i in range(nq):
            a = i * tq
            q = q_ref[a:a + tq, :]
            sd = lax.dot_general(q, kcat_ref[a:a + tq, :], NT, preferred_element_type=F32)
            sd = jnp.where(cols <= rows, sd, MASK_VALUE)
            m = jnp.max(sd, axis=1, keepdims=True)
            if i > 0:
                sp = lax.dot_general(q, kcat_ref[0:a, :], NT, preferred_element_type=F32)
                m = jnp.maximum(m, jnp.max(sp, axis=1, keepdims=True))
            pd = jnp.exp2((sd - m) * exp2_scale)
            l = jnp.sum(pd, axis=1, keepdims=True)
            acc = lax.dot_general(pd.astype(BF16), kv_ref[a:a + tq, LANES:], NN, preferred_element_type=F32)
            if i > 0:
                pp = jnp.exp2((sp - m) * exp2_scale)
                l = l + jnp.sum(pp, axis=1, keepdims=True)
                acc = acc + lax.dot_general(pp.astype(BF16), kv_ref[0:a, LANES:], NN, preferred_element_type=F32)
            o = acc / l
            z = z_ref[a:a + tq, :]
            o_ref[a:a + tq, :] = o
            o2_ref[a:a + tq, :] = (o * (z * _sigmoid(z))).astype(BF16)
            lse_ref[a:a + tq, :] = jnp.broadcast_to(m * ATTN_SCALE + jnp.log(l), (tq, LANES))

    hw = N_HEADS * LANES
    head256 = pl.BlockSpec((t, HEAD_PAD), lambda h: (0, h))
    head128 = pl.BlockSpec((t, LANES), lambda h: (0, h))
    return pl.pallas_call(
        body, name="flash_fwd",
        out_shape=(jax.ShapeDtypeStruct((t, hw), F32), jax.ShapeDtypeStruct((t, hw), BF16),
                   jax.ShapeDtypeStruct((t, hw), F32)),
        grid=(N_HEADS,),
        in_specs=[head256, head256, pl.BlockSpec((t, LANES), lambda h: (0, 0), pipeline_mode=pl.Buffered(1)),
                  pl.BlockSpec((t, LANES), lambda h: (0, z_first + h))],
        out_specs=(head128, head128, head128),
        scratch_shapes=[pltpu.VMEM((t, HEAD_PAD), BF16)],
        compiler_params=_cparams(("parallel",)),
    )(qcat, kv, kr, projb)


def _final_ln_loss(h1, o2, w_out, g, b, target):
    t, d = h1.shape
    ku = o2.shape[1]
    tm = min(t, 512)

    def body(h_ref, u_ref, w_ref, g_ref, b_ref, t_ref, loss_ref, dr_ref, drb_ref, dg_ref, db_ref):
        i = pl.program_id(0)
        r = ALPHA * h_ref[...] + lax.dot_general(u_ref[...], w_ref[...], NN, preferred_element_type=F32)
        mu = jnp.mean(r, axis=-1, keepdims=True)
        xc = r - mu
        var = jnp.mean(xc * xc, axis=-1, keepdims=True)
        rstd = lax.rsqrt(var + LN_EPS)
        xh = xc * rstd
        g = g_ref[...]
        diff = xh * g + b_ref[...] - t_ref[...]
        part = 0.5 * jnp.sum(jnp.mean(diff * diff, axis=-1, keepdims=True))
        dh = diff * (1.0 / d)
        dgp = jnp.sum(dh * xh, axis=0, keepdims=True)
        dbp = jnp.sum(dh, axis=0, keepdims=True)

        @pl.when(i == 0)
        def _():
            loss_ref[...] = jnp.zeros_like(loss_ref)
            dg_ref[...] = jnp.zeros_like(dg_ref)
            db_ref[...] = jnp.zeros_like(db_ref)

        loss_ref[...] += jnp.full(loss_ref.shape, part, F32)
        dg_ref[...] += dgp
        db_ref[...] += dbp
        dxh = dh * g
        dr = rstd * (dxh - jnp.mean(dxh, axis=-1, keepdims=True) - xh * jnp.mean(dxh * xh, axis=-1, keepdims=True))
        dr_ref[...] = dr
        drb_ref[...] = dr.astype(BF16)

    row = pl.BlockSpec((tm, d), lambda i: (i, 0))
    par = pl.BlockSpec((1, d), lambda i: (0, 0))
    return pl.pallas_call(
        body, name="final_ln_loss",
        out_shape=(jax.ShapeDtypeStruct((1, LANES), F32), jax.ShapeDtypeStruct((t, d), F32),
                   jax.ShapeDtypeStruct((t, d), BF16), jax.ShapeDtypeStruct((1, d), F32),
                   jax.ShapeDtypeStruct((1, d), F32)),
        grid=(t // tm,),
        in_specs=[row, pl.BlockSpec((tm, ku), lambda i: (i, 0)),
                  pl.BlockSpec((ku, d), lambda i: (0, 0), pipeline_mode=pl.Buffered(1)), par, par, row],
        out_specs=(pl.BlockSpec((1, LANES), lambda i: (0, 0)), row, row, par, par),
        compiler_params=_cparams(("arbitrary",)),
    )(h1, o2, w_out, g, b, target)


def _gate_bwd(dr2b, w_out, o, projb):
    t, hw = o.shape
    d = dr2b.shape[1]
    tm = min(t, 512)
    bw = Q_LORA_RANK
    nb = hw // bw
    wb = projb.shape[1]

    def body(dr_ref, w_ref, o_ref, z_ref, dob_ref, dz_ref, dl_ref):
        z = z_ref[...]
        sg = _sigmoid(z)
        d2 = lax.dot_general(dr_ref[...], w_ref[...], NT, preferred_element_type=F32)
        o = o_ref[...]
        do = d2 * (z * sg)
        dob_ref[...] = do.astype(BF16)
        dz_ref[...] = (d2 * o * (sg * (1.0 + z * (1.0 - sg)))).astype(BF16)
        prod = do * o
        for hh in range(bw // LANES):
            dsum = jnp.sum(prod[:, hh * LANES:(hh + 1) * LANES], axis=1, keepdims=True)
            dl_ref[:, hh * LANES:(hh + 1) * LANES] = jnp.broadcast_to(dsum, (tm, LANES))

    blk = pl.BlockSpec((tm, bw), lambda i, j: (i, j))
    blk1 = pl.BlockSpec((tm, bw), lambda i, j: (i, j + 1))
    return pl.pallas_call(
        body, name="gate_bwd",
        out_shape=(jax.ShapeDtypeStruct((t, hw), BF16), jax.ShapeDtypeStruct((t, wb), BF16),
                   jax.ShapeDtypeStruct((t, hw), F32)),
        grid=(t // tm, nb),
        in_specs=[pl.BlockSpec((tm, d), lambda i, j: (i, 0)), pl.BlockSpec((bw, d), lambda i, j: (j, 0)), blk, blk1],
        out_specs=(blk, blk1, blk),
        compiler_params=_cparams(("parallel", "parallel")),
    )(dr2b, w_out, o, projb)


def _flash_bwd(qcat, kv, kr, dob, lse, dl, rc, rs):
    t = qcat.shape[0]
    tq = min(t, 512)
    nq = t // tq
    q_chunk = 2 * tq
    log2e = math.log2(math.e)
    exp2_scale = ATTN_SCALE * log2e
    once = pl.Buffered(1)

    def body(q_ref, kv_ref, kr_ref, do_ref, lse_ref, dl_ref, c_ref, s_ref,
             dq_ref, dkv_ref, dkr_ref, kcat_ref, dqa_ref, dka_ref, dva_ref):
        h = pl.program_id(0)
        kcat_ref[:, 0:LANES] = kv_ref[:, 0:LANES]
        kcat_ref[:, LANES:] = kr_ref[...]
        dqa_ref[...] = jnp.zeros_like(dqa_ref)

        @pl.when(h == 0)
        def _():
            dkr_ref[...] = jnp.zeros_like(dkr_ref)

        rows = lax.broadcasted_iota(jnp.int32, (tq, tq), 0)
        cols = lax.broadcasted_iota(jnp.int32, (tq, tq), 1)

        def block(qs, n, ks, masked):
            q = q_ref[qs:qs + n, :]
            kc = kcat_ref[ks:ks + tq, :]
            s = lax.dot_general(q, kc, NT, preferred_element_type=F32)
            if masked:
                s = jnp.where(cols <= rows, s, MASK_VALUE)
            p = jnp.exp2(s * exp2_scale - lse_ref[qs:qs + n, 0:1] * log2e)
            do = do_ref[qs:qs + n, :]
            dp = lax.dot_general(do, kv_ref[ks:ks + tq, LANES:], NT, preferred_element_type=F32)
            ds = (p * (dp - dl_ref[qs:qs + n, 0:1])).astype(BF16)
            dva_ref[...] += lax.dot_general(p.astype(BF16), do, TN, preferred_element_type=F32)
            dka_ref[...] += lax.dot_general(ds, q, TN, preferred_element_type=F32)
            dqa_ref[qs:qs + n, :] += lax.dot_general(ds, kc, NN, preferred_element_type=F32)

        for kb in range(nq):
            a = kb * tq
            dka_ref[...] = jnp.zeros_like(dka_ref)
            dva_ref[...] = jnp.zeros_like(dva_ref)
            block(a, tq, a, True)
            qs = a + tq
            while qs < t:
                n = min(q_chunk, t - qs)
                block(qs, n, a, False)
                qs += n
            dk = dka_ref[...] * ATTN_SCALE
            dkv_ref[a:a + tq, 0:LANES] = dk[:, 0:LANES].astype(BF16)
            dkv_ref[a:a + tq, LANES:] = dva_ref[...].astype(BF16)
            dkr_ref[a:a + tq, :] += dk[:, LANES:]

        dq = dqa_ref[...] * ATTN_SCALE
        dq_ref[:, 0:LANES] = dq[:, 0:LANES].astype(BF16)
        d2 = dq[:, LANES:]
        dq_ref[:, LANES:] = (d2 * c_ref[...] - _swap_rope_halves(d2) * s_ref[...]).astype(BF16)

    hp = N_HEADS * HEAD_PAD
    head256 = pl.BlockSpec((t, HEAD_PAD), lambda h: (0, h))
    head128 = pl.BlockSpec((t, LANES), lambda h: (0, h))
    const128 = pl.BlockSpec((t, LANES), lambda h: (0, 0), pipeline_mode=once)
    return pl.pallas_call(
        body, name="flash_bwd",
        out_shape=(jax.ShapeDtypeStruct((t, hp), BF16), jax.ShapeDtypeStruct((t, hp), BF16),
                   jax.ShapeDtypeStruct((t, LANES), F32)),
        grid=(N_HEADS,),
        in_specs=[head256, head256, const128, head128, head128, head128, const128, const128],
        out_specs=(head256, head256, pl.BlockSpec((t, LANES), lambda h: (0, 0))),
        scratch_shapes=[pltpu.VMEM((t, HEAD_PAD), BF16), pltpu.VMEM((t, HEAD_PAD), F32),
                        pltpu.VMEM((tq, HEAD_PAD), F32), pltpu.VMEM((tq, LANES), F32)],
        compiler_params=_cparams(("arbitrary",)),
    )(qcat, kv, kr, dob, lse, dl, rc, rs)


def _q_norm_bwd(dq2, w_uq, projb, g, dprojb):
    t = projb.shape[0]
    kq = dq2.shape[1]
    tm = min(t, 512)
    r = Q_LORA_RANK

    def body(dq_ref, w_ref, c_ref, g_ref, alias_ref, o_ref, dg_ref):
        i = pl.program_id(0)
        c = c_ref[...]
        d = lax.dot_general(dq_ref[...], w_ref[...], NT, preferred_element_type=F32)
        rr = lax.rsqrt(jnp.mean(c * c, axis=-1, keepdims=True) + RMS_EPS)
        xh = c * rr

        @pl.when(i == 0)
        def _():
            dg_ref[...] = jnp.zeros_like(dg_ref)

        dg_ref[...] += jnp.sum(d * xh, axis=0, keepdims=True)
        dxh = d * g_ref[...]
        o_ref[...] = (rr * (dxh - xh * jnp.mean(dxh * xh, axis=-1, keepdims=True))).astype(BF16)

    blk = pl.BlockSpec((tm, r), lambda i: (i, 0))
    par = pl.BlockSpec((1, r), lambda i: (0, 0))
    return pl.pallas_call(
        body, name="q_norm_bwd",
        out_shape=(jax.ShapeDtypeStruct(dprojb.shape, BF16), jax.ShapeDtypeStruct((1, r), F32)),
        grid=(t // tm,),
        in_specs=[pl.BlockSpec((tm, kq), lambda i: (i, 0)),
                  pl.BlockSpec((r, kq), lambda i: (0, 0), pipeline_mode=pl.Buffered(1)),
                  blk, par, pl.BlockSpec(memory_space=pl.ANY)],
        out_specs=(blk, par),
        input_output_aliases={4: 0},
        compiler_params=_cparams(("arbitrary",)),
    )(dq2, w_uq, projb, g, dprojb)


def _kv_mid_bwd(dkv, w_ukv, ckv, dkr, g, rc, rs):
    t = ckv.shape[0]
    kk = dkv.shape[1]
    tm = min(t, 512)
    r = KV_LORA_RANK

    def body(dkv_ref, w_ref, ckv_ref, dkr_ref, g_ref, c_ref, s_ref, o_ref, dg_ref):
        i = pl.program_id(0)
        c = ckv_ref[:, 0:r]
        d = lax.dot_general(dkv_ref[...], w_ref[...], NT, preferred_element_type=F32)
        rr = lax.rsqrt(jnp.mean(c * c, axis=-1, keepdims=True) + RMS_EPS)
        xh = c * rr

        @pl.when(i == 0)
        def _():
            dg_ref[...] = jnp.zeros_like(dg_ref)

        dg_ref[...] += jnp.sum(d * xh, axis=0, keepdims=True)
        dxh = d * g_ref[...]
        o_ref[:, 0:r] = (rr * (dxh - xh * jnp.mean(dxh * xh, axis=-1, keepdims=True))).astype(BF16)
        dk = dkr_ref[...]
        o_ref[:, r:] = (dk * c_ref[...] - _swap_rope_halves(dk) * s_ref[...]).astype(BF16)

    return pl.pallas_call(
        body, name="kv_mid_bwd",
        out_shape=(jax.ShapeDtypeStruct((t, r + LANES), BF16), jax.ShapeDtypeStruct((1, r), F32)),
        grid=(t // tm,),
        in_specs=[pl.BlockSpec((tm, kk), lambda i: (i, 0)),
                  pl.BlockSpec((r, kk), lambda i: (0, 0), pipeline_mode=pl.Buffered(1)),
                  pl.BlockSpec((tm, r + LANES), lambda i: (i, 0)),
                  pl.BlockSpec((tm, LANES), lambda i: (i, 0)), pl.BlockSpec((1, r), lambda i: (0, 0)),
                  pl.BlockSpec((tm, LANES), lambda i: (i, 0)), pl.BlockSpec((tm, LANES), lambda i: (i, 0))],
        out_specs=(pl.BlockSpec((tm, r + LANES), lambda i: (i, 0)), pl.BlockSpec((1, r), lambda i: (0, 0))),
        compiler_params=_cparams(("arbitrary",)),
    )(dkv, w_ukv, ckv, dkr, g, rc, rs)


def _ln0_bwd(dr2, dprojb, w_in, dckv, w_down, xh, rstd, g, after):
    t, d = dr2.shape
    kb, kd = dprojb.shape[1], dckv.shape[1]
    tm = min(t, 256)
    extra = [] if after is None else [after]

    def body(a_ref, pb_ref, wb_ref, pd_ref, wd_ref, xh_ref, rs_ref, g_ref, *rest):
        dr_ref, drb_ref, dg_ref, db_ref, dsum_ref = rest[-5:]
        i = pl.program_id(0)
        dh = (ALPHA * a_ref[...] + lax.dot_general(pb_ref[...], wb_ref[...], NT, preferred_element_type=F32)
              + lax.dot_general(pd_ref[...], wd_ref[...], NT, preferred_element_type=F32))
        xh = xh_ref[...]

        @pl.when(i == 0)
        def _():
            dg_ref[...] = jnp.zeros_like(dg_ref)
            db_ref[...] = jnp.zeros_like(db_ref)
            dsum_ref[...] = jnp.zeros_like(dsum_ref)

        dg_ref[...] += jnp.sum(dh * xh, axis=0, keepdims=True)
        db_ref[...] += jnp.sum(dh, axis=0, keepdims=True)
        dxh = dh * g_ref[...]
        dr = rs_ref[...] * (dxh - jnp.mean(dxh, axis=-1, keepdims=True) - xh * jnp.mean(dxh * xh, axis=-1, keepdims=True))
        dr_ref[...] = dr
        drb_ref[...] = dr.astype(BF16)
        dsum_ref[...] += jnp.sum(dr, axis=0, keepdims=True)

    row = pl.BlockSpec((tm, d), lambda i: (i, 0))
    par = pl.BlockSpec((1, d), lambda i: (0, 0))
    return pl.pallas_call(
        body, name="ln0_bwd",
        out_shape=(jax.ShapeDtypeStruct((t, d), F32), jax.ShapeDtypeStruct((t, d), BF16),
                   jax.ShapeDtypeStruct((1, d), F32), jax.ShapeDtypeStruct((1, d), F32),
                   jax.ShapeDtypeStruct((1, d), F32)),
        grid=(t // tm,),
        in_specs=[row, pl.BlockSpec((tm, kb), lambda i: (i, 0)),
                  pl.BlockSpec((d, kb), lambda i: (0, 0), pipeline_mode=pl.Buffered(1)),
                  pl.BlockSpec((tm, kd), lambda i: (i, 0)),
                  pl.BlockSpec((d, kd), lambda i: (0, 0), pipeline_mode=pl.Buffered(1)),
                  row, pl.BlockSpec((tm, 1), lambda i: (i, 0)), par]
                 + [pl.BlockSpec(memory_space=pl.ANY)] * len(extra),
        out_specs=(row, row, par, par, par),
        compiler_params=_cparams(("arbitrary",)),
    )(dr2, dprojb, w_in, dckv, w_down, xh, rstd, g, *extra)


def _conv_mid_bwd_rows(u1, proj, dr1b, w_out, ng, nb, after):
    t = u1.shape[0]
    e = CONV_WIDTH
    d = dr1b.shape[1]
    tm = min(t, 256)
    extra = [] if after is None else [after]

    rg = 16
    nt = t // tm

    def body(u1_ref, z_ref, dr_ref, w_ref, ng_ref, nb_ref, *rest):
        du1_ref, dz_ref, dng_ref, dnb_ref, dbz_ref, acc_ref, d_ref = rest[-7:]
        i = pl.program_id(0)

        @pl.when(i == 0)
        def _():
            acc_ref[...] = jnp.zeros_like(acc_ref)

        d_ref[...] = lax.dot_general(dr_ref[...], w_ref[...], NT, preferred_element_type=F32)
        g = ng_ref[...]
        b = nb_ref[...]

        def group(r, carry):
            rows = pl.ds(pl.multiple_of(r * rg, rg), rg)
            u1 = u1_ref[rows, :]
            mu = jnp.mean(u1, axis=-1, keepdims=True)
            xc = u1 - mu
            rstd = lax.rsqrt(jnp.mean(xc * xc, axis=-1, keepdims=True) + LN_EPS)
            xh = xc * rstd
            n = xh * g + b
            sn = _sigmoid(n)
            z = z_ref[rows, :]
            sz = _sigmoid(z)
            du2 = d_ref[rows, :]
            dz = du2 * (n * sn) * (sz * (1.0 + z * (1.0 - sz)))
            dn = du2 * (z * sz) * (sn * (1.0 + n * (1.0 - sn)))
            acc_ref[0:8, :] += _fold_rows(dn * xh)
            acc_ref[8:16, :] += _fold_rows(dn)
            acc_ref[16:24, :] += _fold_rows(dz)
            dz_ref[rows, :] = dz.astype(BF16)
            dxh = dn * g
            du1_ref[rows, :] = rstd * (dxh - jnp.mean(dxh, axis=-1, keepdims=True)
                                       - xh * jnp.mean(dxh * xh, axis=-1, keepdims=True))
            return carry

        lax.fori_loop(0, tm // rg, group, 0, unroll=8)

        @pl.when(i == nt - 1)
        def _():
            dng_ref[...] = jnp.sum(acc_ref[0:8, :], axis=0, keepdims=True)
            dnb_ref[...] = jnp.sum(acc_ref[8:16, :], axis=0, keepdims=True)
            dbz_ref[...] = jnp.sum(acc_ref[16:24, :], axis=0, keepdims=True)

    row = pl.BlockSpec((tm, e), lambda i: (i, 0))
    par = pl.BlockSpec((1, e), lambda i: (0, 0))
    return pl.pallas_call(
        body, name="conv_mid_bwd_rows",
        out_shape=(jax.ShapeDtypeStruct((t, e), F32), jax.ShapeDtypeStruct((t, 3 * e), BF16),
                   jax.ShapeDtypeStruct((1, e), F32), jax.ShapeDtypeStruct((1, e), F32),
                   jax.ShapeDtypeStruct((1, e), F32)),
        grid=(nt,),
        in_specs=[row, pl.BlockSpec((tm, e), lambda i: (i, 2)), pl.BlockSpec((tm, d), lambda i: (i, 0)),
                  pl.BlockSpec((e, d), lambda i: (0, 0), pipeline_mode=pl.Buffered(1)), par, par]
                 + [pl.BlockSpec(memory_space=pl.ANY)] * len(extra),
        out_specs=(row, pl.BlockSpec((tm, e), lambda i: (i, 2)), par, par, par),
        scratch_shapes=[pltpu.VMEM((24, e), F32), pltpu.VMEM((tm, e), F32)],
        compiler_params=_cparams(("arbitrary",)),
    )(u1, proj, dr1b, w_out, ng, nb, *extra)


def _conv_bwd(du1, proj, cw, dproj):
    t = du1.shape[0]
    e = CONV_WIDTH
    tm = min(t, 128)
    hb = tm // HALO
    nt = t // tm
    nchunk = e // LANES
    last_halo = t // HALO - 1

    def body(d_ref, dn_ref, val_ref, gg_ref, valh_ref, ggh_ref, cw_ref, alias_ref,
             dp_ref, dcw_ref, dcb_ref, dbv_ref, extu_ref, extd_ref, du0_ref, acc_ref, sh_ref):
        i = pl.program_id(0)
        h0 = valh_ref[...] * _sigmoid(ggh_ref[...])
        extu_ref[0:HALO, :] = jnp.where(i > 0, h0, 0.0)
        extu_ref[HALO:, :] = val_ref[...] * _sigmoid(gg_ref[...])
        extd_ref[0:tm, :] = d_ref[...]
        extd_ref[tm:, :] = jnp.where(i < nt - 1, dn_ref[...], 0.0)

        @pl.when(i == 0)
        def _():
            acc_ref[...] = jnp.zeros_like(acc_ref)

        def chunk(c, carry):
            col = pl.multiple_of(c * LANES, LANES)
            d = extd_ref[0:tm, pl.ds(col, LANES)]
            du0 = jnp.zeros((tm, LANES), F32)
            for res, taps in _taps_by_residue(lambda j: CONV_KERNEL - 1 - j):
                span = 8 * max(a for _, a in taps) + tm
                sh_ref[0:span, :] = extd_ref[res:res + span, pl.ds(col, LANES)]
                for j, a in taps:
                    du0 = du0 + cw_ref[j:j + 1, pl.ds(col, LANES)] * sh_ref[8 * a:8 * a + tm, :]
            for res, taps in _taps_by_residue(lambda j: HALO - (CONV_KERNEL - 1) + j):
                span = 8 * max(a for _, a in taps) + tm
                sh_ref[0:span, :] = extu_ref[res:res + span, pl.ds(col, LANES)]
                for j, a in taps:
                    prod = d * sh_ref[8 * a:8 * a + tm, :]
                    acc_ref[j * 8:(j + 1) * 8, pl.ds(col, LANES)] += _fold_rows(prod)
            acc_ref[CONV_KERNEL * 8:(CONV_KERNEL + 1) * 8, pl.ds(col, LANES)] += _fold_rows(d)
            du0_ref[:, pl.ds(col, LANES)] = du0
            return carry

        lax.fori_loop(0, nchunk, chunk, 0)
        kb = CONV_KERNEL * 8
        for r in range(0, tm, 16):
            sg = _sigmoid(gg_ref[r:r + 16, :])
            dval = du0_ref[r:r + 16, :] * sg
            dgg = dval * val_ref[r:r + 16, :] * (1.0 - sg)
            dp_ref[r:r + 16, 0:e] = dval.astype(BF16)
            dp_ref[r:r + 16, e:] = dgg.astype(BF16)
            acc_ref[kb + 8:kb + 16, :] += _fold_rows(dval)
            acc_ref[kb + 16:kb + 24, :] += _fold_rows(dgg)

        @pl.when(i == nt - 1)
        def _():
            for j in range(CONV_KERNEL):
                dcw_ref[j:j + 1, :] = jnp.sum(acc_ref[j * 8:(j + 1) * 8, :], axis=0, keepdims=True)
            dcw_ref[CONV_KERNEL:, :] = jnp.zeros((HALO - CONV_KERNEL, e), F32)
            dcb_ref[...] = jnp.sum(acc_ref[kb:kb + 8, :], axis=0, keepdims=True)
            dbv_ref[:, 0:e] = jnp.sum(acc_ref[kb + 8:kb + 16, :], axis=0, keepdims=True)
            dbv_ref[:, e:] = jnp.sum(acc_ref[kb + 16:kb + 24, :], axis=0, keepdims=True)

    row = lambda c: pl.BlockSpec((tm, e), lambda i: (i, c))
    halo_prev = lambda c: pl.BlockSpec((HALO, e), lambda i: (jnp.maximum(i * hb - 1, 0), c))
    halo_next = pl.BlockSpec((HALO, e), lambda i: (jnp.minimum((i + 1) * hb, last_halo), 0))
    return pl.pallas_call(
        body, name="conv_bwd",
        out_shape=(jax.ShapeDtypeStruct(dproj.shape, BF16), jax.ShapeDtypeStruct((HALO, e), F32),
                   jax.ShapeDtypeStruct((1, e), F32), jax.ShapeDtypeStruct((1, 2 * e), F32)),
        grid=(nt,),
        in_specs=[row(0), halo_next, row(0), row(1), halo_prev(0), halo_prev(1),
                  pl.BlockSpec((HALO, e), lambda i: (0, 0)), pl.BlockSpec(memory_space=pl.ANY)],
        out_specs=(pl.BlockSpec((tm, 2 * e), lambda i: (i, 0)), pl.BlockSpec((HALO, e), lambda i: (0, 0)),
                   pl.BlockSpec((1, e), lambda i: (0, 0)), pl.BlockSpec((1, 2 * e), lambda i: (0, 0))),
        scratch_shapes=[pltpu.VMEM((tm + HALO, e), F32), pltpu.VMEM((tm + HALO, e), F32),
                        pltpu.VMEM((tm, e), F32), pltpu.VMEM(((CONV_KERNEL + 3) * 8, e), F32),
                        pltpu.VMEM((tm + HALO, LANES), F32)],
        input_output_aliases={7: 0},
        compiler_params=_cparams(("arbitrary",)),
    )(du1, du1, proj, proj, proj, proj, cw, dproj)


def _adam_update(g, w, m, v):
    c1 = 1.0 - ADAM_B1 ** ADAM_STEP
    c2 = 1.0 - ADAM_B2 ** ADAM_STEP
    mn = ADAM_B1 * m + (1.0 - ADAM_B1) * g
    vn = ADAM_B2 * v + (1.0 - ADAM_B2) * (g * g)
    delta = -ADAM_LR * ((mn / c1) / (jnp.sqrt(vn / c2) + ADAM_EPS) + ADAM_WD * w)
    return delta, mn, vn


def _adamw(name, gbuf, w, m, v):
    parts = list(gbuf) if isinstance(gbuf, (list, tuple)) else [gbuf]
    npart = len(parts)
    r, c = w.shape
    tr = min(r // npart, 256)
    per_part = r // npart // tr
    assert r == npart * per_part * tr and all(p.shape == (N_DEV, r // npart, c) for p in parts)

    def body(*refs):
        g_refs = refs[:npart]
        w_ref, m_ref, v_ref, go_ref, d_ref, mo_ref, vo_ref = refs[npart:]
        i = pl.program_id(0)

        def run(g_ref):
            g = g_ref[0].astype(F32)
            for k in range(1, N_DEV):
                g = g + g_ref[k].astype(F32)
            go_ref[...] = g
            d_ref[...], mo_ref[...], vo_ref[...] = _adam_update(g, w_ref[...], m_ref[...], v_ref[...])

        if npart == 1:
            run(g_refs[0])
        else:
            for q in range(npart):
                pl.when((i >= q * per_part) & (i < (q + 1) * per_part))(functools.partial(run, g_refs[q]))

    blk = pl.BlockSpec((tr, c), lambda i: (i, 0))
    part_spec = lambda q: pl.BlockSpec((N_DEV, tr, c), lambda i: (0, jnp.clip(i - q * per_part, 0, per_part - 1), 0))
    shp = jax.ShapeDtypeStruct((r, c), F32)
    return pl.pallas_call(
        body, name="adamw_" + name,
        out_shape=(shp, shp, shp, shp),
        grid=(r // tr,),
        in_specs=[part_spec(q) for q in range(npart)] + [blk, blk, blk],
        out_specs=(blk, blk, blk, blk),
        compiler_params=_cparams(("parallel",)),
    )(*parts, w, m, v)


SMALL_TABLE_COLS = 256
SMALL_LAYOUT = {
    'ln_g': (0, 2, 1024), 'ln_b': (8, 2, 1024), 'a_b_in': (16, 1, 768), 'a_conv_b': (24, 1, 256),
    'a_norm_g': (32, 1, 256), 'a_norm_b': (40, 1, 256), 'a_b_out': (48, 1, 128), 'kv_norm_g': (56, 1, 256),
    'b_q_norm_g': (64, 1, 512), 'a_conv_w': (72, CONV_KERNEL, 256),
}
LOSS_ROW = 104
SMALL_TABLE_ROWS = 112
SMALL_NAMES = tuple(SMALL_LAYOUT)


def _small_pieces(name, row):
    r0, _, c = SMALL_LAYOUT[name]
    w = min(c, SMALL_TABLE_COLS)
    per_row = c // w
    return [(r0 + row * per_row + q, q * w, w) for q in range(per_row)]


def _pack_small_grads(dg0, dg1, db0, db1, dbv, dbz, dcb, dng, dnb, dba, dgkv, dgq, dcw, loss):
    e = CONV_WIDTH
    ce = e // N_DEV

    def body(dg0_ref, dg1_ref, db0_ref, db1_ref, dbv_ref, dbz_ref, dcb_ref, dng_ref, dnb_ref, dba_ref, dgkv_ref,
             dgq_ref, dcw_ref, loss_ref, o_ref):
        o_ref[...] = jnp.zeros_like(o_ref)
        for d in range(N_DEV):
            o_ref[d, LOSS_ROW:LOSS_ROW + 1, 0:LANES] = loss_ref[...]

        def put(d, name, row, src_ref, first_col=0):
            for trow, col, w in _small_pieces(name, row):
                o_ref[d, trow:trow + 1, 0:w] = src_ref[:, first_col + col:first_col + col + w]

        for d in range(N_DEV):
            put(d, 'ln_g', 0, dg0_ref)
            put(d, 'ln_g', 1, dg1_ref)
            put(d, 'ln_b', 0, db0_ref)
            put(d, 'ln_b', 1, db1_ref)
            for trow, col, w in _small_pieces('a_b_in', 0):
                gcol = d * 3 * ce + col
                src = dbv_ref[:, gcol:gcol + w] if gcol < 2 * e else dbz_ref[:, gcol - 2 * e:gcol - 2 * e + w]
                o_ref[d, trow:trow + 1, 0:w] = src
            put(d, 'a_conv_b', 0, dcb_ref, d * ce)
            put(d, 'a_norm_g', 0, dng_ref, d * ce)
            put(d, 'a_norm_b', 0, dnb_ref, d * ce)
            put(d, 'a_b_out', 0, dba_ref, d * LANES)
            put(d, 'kv_norm_g', 0, dgkv_ref)
            put(d, 'b_q_norm_g', 0, dgq_ref)
            r0 = SMALL_LAYOUT['a_conv_w'][0]
            o_ref[d, r0:r0 + HALO, 0:ce] = dcw_ref[:, d * ce:(d + 1) * ce]

    return pl.pallas_call(
        body, name="pack_small_grads",
        out_shape=jax.ShapeDtypeStruct((N_DEV, SMALL_TABLE_ROWS, SMALL_TABLE_COLS), F32),
        compiler_params=pltpu.CompilerParams(vmem_limit_bytes=VMEM_LIMIT),
    )(dg0, dg1, db0, db1, dbv, dbz, dcb, dng, dnb, dba, dgkv, dgq, dcw, loss)


def _adamw_small(gtab, ws, ms, vs):
    n = len(SMALL_NAMES)

    def body(*refs):
        g_ref = refs[0]
        w_refs, m_refs, v_refs = refs[1:1 + n], refs[1 + n:1 + 2 * n], refs[1 + 2 * n:1 + 3 * n]
        outs = refs[1 + 3 * n:]

        def summed(r0, r, c):
            g = g_ref[0, r0:r0 + r, 0:c]
            for k in range(1, N_DEV):
                g = g + g_ref[k, r0:r0 + r, 0:c]
            return g

        for i, name in enumerate(SMALL_NAMES):
            r0, r, c = SMALL_LAYOUT[name]
            if c <= SMALL_TABLE_COLS:
                blocks = [(slice(None), summed(r0, r, c))]
            else:
                blocks = [(slice(row, row + 1), jnp.concatenate([summed(tr, 1, w) for tr, _, w in _small_pieces(name, row)], axis=1))
                          for row in range(r)]
            for rows, g in blocks:
                delta, mn, vn = _adam_update(g, w_refs[i][rows, :], m_refs[i][rows, :], v_refs[i][rows, :])
                outs[i][rows, :] = g
                outs[n + i][rows, :] = delta
                outs[2 * n + i][rows, :] = mn
                outs[3 * n + i][rows, :] = vn
        outs[4 * n][...] = summed(LOSS_ROW, 1, LANES)

    shapes = tuple(jax.ShapeDtypeStruct(w.shape, F32) for w in ws)
    res = pl.pallas_call(
        body, name="adamw_small",
        out_shape=shapes * 4 + (jax.ShapeDtypeStruct((1, LANES), F32),),
        compiler_params=pltpu.CompilerParams(vmem_limit_bytes=VMEM_LIMIT),
    )(gtab, *ws, *ms, *vs)
    return res[:n], res[n:2 * n], res[2 * n:3 * n], res[3 * n:4 * n], res[4 * n]


def _mesh_peers():
    x, y, c = lax.axis_index("x"), lax.axis_index("y"), lax.axis_index("c")
    me = 4 * x + 2 * y + c
    peers = []
    for k in range(1, N_DEV):
        px = 1 - x if (k >> 2) & 1 else x
        py = 1 - y if (k >> 1) & 1 else y
        pc = 1 - c if k & 1 else c
        peers.append(((px, py, pc), 4 * px + 2 * py + pc))
    return me, peers


_HBM_SPEC = pl.BlockSpec(memory_space=pltpu.HBM)
_SEM_SPEC = pl.BlockSpec(memory_space=pltpu.SEMAPHORE)


ALL_PEERS = tuple(range(N_DEV - 1))
CHIP_LEVEL_PEERS = (0, 1, 3, 5)


def _split_copies(srcs, lands, send_sems, recv_sems, local_sems, gather, which=ALL_PEERS):
    me, peers = _mesh_peers()
    na = len(srcs)
    mine = lambda a, slot: srcs[a] if gather else srcs[a].at[slot]
    local = [pltpu.make_async_copy(mine(a, me), lands[a].at[me], local_sems.at[a]) for a in range(na)]
    sent, received = [], []
    for k, (dev, pid) in enumerate(peers):
        if k not in which:
            continue
        for a in range(na):
            s = a * (N_DEV - 1) + k
            sent.append(pltpu.make_async_remote_copy(
                src_ref=mine(a, pid), dst_ref=lands[a].at[me], send_sem=send_sems.at[s],
                recv_sem=recv_sems.at[s], device_id=dev, device_id_type=pl.DeviceIdType.MESH))
            received.append(pltpu.make_async_remote_copy(
                src_ref=mine(a, pid), dst_ref=lands[a].at[pid], send_sem=send_sems.at[s],
                recv_sem=recv_sems.at[s], device_id=dev, device_id_type=pl.DeviceIdType.MESH))
    return local, sent, received


def _comm_start(name, srcs, gather, after, which=ALL_PEERS):
    na = len(srcs)
    land_structs = [jax.ShapeDtypeStruct(((N_DEV,) + s.shape) if gather else s.shape, s.dtype) for s in srcs]
    extra = [] if after is None else [after]
    n_in = 2 * na + len(extra)

    def body(*refs):
        srcs_r, lands_r = refs[:na], refs[na:2 * na]
        send_sems, recv_sems, local_sems = refs[n_in:n_in + 3]
        token = refs[-1]
        local, sent, _ = _split_copies(srcs_r, lands_r, send_sems, recv_sems, local_sems, gather, which)
        for cp in local + sent:
            cp.start()
        token[...] = jnp.zeros_like(token)

    sem = pltpu.SemaphoreType.DMA((na * (N_DEV - 1),))
    outs = pl.pallas_call(
        body, name=name,
        out_shape=(sem, sem, pltpu.SemaphoreType.DMA((na,)),
                   *[pltpu.HBM(s.shape, s.dtype) for s in srcs],
                   *[pltpu.HBM(s.shape, s.dtype) for s in land_structs],
                   jax.ShapeDtypeStruct((8, LANES), F32)),
        in_specs=[_HBM_SPEC] * (2 * na) + [pl.BlockSpec(memory_space=pl.ANY)] * len(extra),
        out_specs=(_SEM_SPEC, _SEM_SPEC, _SEM_SPEC, *([_HBM_SPEC] * (2 * na)), pl.BlockSpec(memory_space=pltpu.VMEM)),
        input_output_aliases={i: 3 + i for i in range(2 * na)},
        compiler_params=pltpu.CompilerParams(has_side_effects=pltpu.SideEffectType.DATAFLOW_SIDE_EFFECTING),
    )(*[pltpu.with_memory_space_constraint(s, pltpu.HBM) for s in srcs],
      *[pltpu.with_memory_space_constraint(lax.empty(s.shape, s.dtype), pltpu.HBM) for s in land_structs],
      *extra)
    return (outs[:3], outs[3:3 + na], outs[3 + na:3 + 2 * na]), outs[-1]


def _comm_wait(name, handles, gather, after, which=ALL_PEERS):
    (send_sems, recv_sems, local_sems), srcs, lands = handles
    na = len(srcs)

    def body(*refs):
        srcs_r, lands_r = refs[:na], refs[na:2 * na]
        send_r, recv_r, local_r = refs[2 * na:2 * na + 3]
        local, sent, received = _split_copies(srcs_r, lands_r, send_r, recv_r, local_r, gather, which)
        for cp in sent:
            cp.wait_send()
        for cp in received:
            cp.wait_recv()
        for cp in local:
            cp.wait()

    outs = pl.pallas_call(
        body, name=name,
        out_shape=(*[pltpu.HBM(s.shape, s.dtype) for s in srcs], *[pltpu.HBM(s.shape, s.dtype) for s in lands]),
        in_specs=[_HBM_SPEC] * (2 * na) + [_SEM_SPEC] * 3 + [pl.BlockSpec(memory_space=pl.ANY)],
        out_specs=tuple([_HBM_SPEC] * (2 * na)),
        input_output_aliases={i: i for i in range(2 * na)},
        compiler_params=pltpu.CompilerParams(has_side_effects=pltpu.SideEffectType.DATAFLOW_SIDE_EFFECTING),
    )(*srcs, *lands, send_sems, recv_sems, local_sems, after)
    return outs[na:]


def _forward_to_sibling(name, lands):
    na = len(lands)

    def body(*refs):
        ins, outs = refs[:na], refs[na:2 * na]
        send_sems, recv_sems = refs[2 * na:]
        x, y, c = lax.axis_index("x"), lax.axis_index("y"), lax.axis_index("c")
        sibling = (x, y, 1 - c)
        chips = [(1 - x, y), (x, 1 - y), (1 - x, 1 - y)]
        slot = lambda cx, cy, cc: 4 * cx + 2 * cy + cc
        sends = []
        for j, (cx, cy) in enumerate(chips):
            for a in range(na):
                cp = pltpu.make_async_remote_copy(
                    src_ref=ins[a].at[slot(cx, cy, c)], dst_ref=outs[a].at[slot(cx, cy, c)],
                    send_sem=send_sems.at[a, j], recv_sem=recv_sems.at[a, j], device_id=sibling,
                    device_id_type=pl.DeviceIdType.MESH)
                cp.start()
                sends.append(cp)
        for j, (cx, cy) in enumerate(chips):
            for a in range(na):
                pltpu.make_async_remote_copy(
                    src_ref=ins[a].at[slot(cx, cy, 1 - c)], dst_ref=outs[a].at[slot(cx, cy, 1 - c)],
                    send_sem=send_sems.at[a, j], recv_sem=recv_sems.at[a, j], device_id=sibling,
                    device_id_type=pl.DeviceIdType.MESH).wait_recv()
        for cp in sends:
            cp.wait_send()

    hbm = pl.BlockSpec(memory_space=pl.ANY)
    return pl.pallas_call(
        body, name=name,
        out_shape=tuple(jax.ShapeDtypeStruct(s.shape, s.dtype) for s in lands),
        in_specs=[hbm] * na, out_specs=(hbm,) * na,
        input_output_aliases={a: a for a in range(na)},
        scratch_shapes=[pltpu.SemaphoreType.DMA((na, 3)), pltpu.SemaphoreType.DMA((na, 3))],
    )(*lands)


def _prep_weights(w):
    e = CONV_WIDTH
    p = {}
    if 'a_w_in' in w:
        if w['a_w_in'].shape == (N_DEV, D_MODEL, 3 * e // N_DEV):
            p['a_w_in3'] = w['a_w_in']
        else:
            p['a_w_in3'] = w['a_w_in'].reshape(D_MODEL, N_DEV, 3 * e // N_DEV).transpose(1, 0, 2)
        p['a_b_in'] = w['a_b_in'].reshape(1, 3 * e)
        p['a_conv_w'] = jnp.pad(w['a_conv_w'].reshape(CONV_KERNEL, e), ((0, HALO - CONV_KERNEL), (0, 0)))
        p['a_conv_b'] = w['a_conv_b'].reshape(1, e)
        p['a_norm_g'] = w['a_norm_g'].reshape(1, e)
        p['a_norm_b'] = w['a_norm_b'].reshape(1, e)
        p['a_b_out'] = w['a_b_out'].reshape(1, D_MODEL)
        p['kv_norm_g'] = w['kv_norm_g'].reshape(1, KV_LORA_RANK)
        p['b_q_norm_g'] = w['b_q_norm_g'].reshape(1, Q_LORA_RANK)
        p['ln_g'] = w['ln_g']
        p['ln_b'] = w['ln_b']
    if 'a_w_out' in w:
        p['a_w_out'] = w['a_w_out'].reshape(e, D_MODEL)
        p['kv_w_down'] = jnp.pad(w['kv_w_down'], ((0, 0), (0, KV_LORA_RANK + LANES - w['kv_w_down'].shape[1])))
        p['kv_w_ukv'] = jnp.concatenate([w['kv_w_uk'], w['kv_w_uv']], axis=2).reshape(KV_LORA_RANK, N_HEADS * HEAD_PAD)
        p['b_w_in'] = w['b_w_in'].reshape(D_MODEL, -1)
        uq = w['b_w_uq'].reshape(Q_LORA_RANK, N_HEADS, QK_NOPE_DIM + QK_ROPE_DIM)
        p['b_w_uq'] = jnp.pad(uq, ((0, 0), (0, 0), (0, HEAD_PAD - uq.shape[2]))).reshape(Q_LORA_RANK, N_HEADS * HEAD_PAD)
        p['b_w_out'] = w['b_w_out'].reshape(N_HEADS * V_HEAD_DIM, D_MODEL)
    return p


def _local_step(x, positions, target, comm):
    t = x.shape[0]
    e = CONV_WIDTH
    freqs = ROPE_THETA ** (-jnp.arange(0, QK_ROPE_DIM, 2, dtype=F32) / QK_ROPE_DIM)
    freq_row = jnp.concatenate([freqs, freqs, jnp.zeros((LANES - QK_ROPE_DIM,), F32)]).reshape(1, LANES)
    rc, rs, xb = _rope_tables(positions.reshape(t, 1), freq_row, x, comm.start_token())
    p = comm.first_weights(rc)
    ln_g0, ln_b0 = p['ln_g'][0:1], p['ln_b'][0:1]
    ln_g1, ln_b1 = p['ln_g'][1:2], p['ln_b'][1:2]

    proj = _matmul(xb, p['a_w_in3'], bias=p['a_b_in'], name="a_in", b_blocked=True, bm=2048,
                   after=comm.first_after())
    u1, u2 = _conv_mid_fwd(proj, p['a_conv_w'], p['a_conv_b'], p['a_norm_g'], p['a_norm_b'])
    p = {**p, **comm.rest_weights(u2)}
    h1, h1b, xh1, rstd1 = _ln_res_fwd(x, u2, p['a_w_out'], p['a_b_out'], ln_g0, ln_b0)
    ckv, ckn, kr = _kv_mid_fwd(h1b, p['kv_w_down'], p['kv_norm_g'], rc, rs)
    kv = _matmul(ckn, p['kv_w_ukv'], name="kv_up", out_dtype=BF16)
    projb = _matmul(h1b, p['b_w_in'], name="b_in", bn=1280)
    cqn = _q_norm_fwd(projb, p['b_q_norm_g'])
    qcat = _q_up_rope(cqn, p['b_w_uq'], rc, rs)
    o, o2, lse = _flash_fwd(qcat, kv, kr, projb)
    loss, dr2, dr2b, dg1, db1 = _final_ln_loss(h1, o2, p['b_w_out'], ln_g1, ln_b1, target)

    g = {}
    g['b_w_out'] = _matmul(o2, dr2b, trans_a=True, name="d_b_out", out_dtype=BF16, bm=512, bk=t)
    dob, dprojb, dl = _gate_bwd(dr2b, p['b_w_out'], o, projb)
    dq2, dkv, dkr = _flash_bwd(qcat, kv, kr, dob, lse, dl, rc, rs)
    duq = _matmul(cqn, dq2, trans_a=True, name="d_q_up", out_dtype=BF16, bk=t)
    dprojb, dgq = _q_norm_bwd(dq2, p['b_w_uq'], projb, p['b_q_norm_g'], dprojb)
    g['b_w_in'] = _matmul(h1b, dprojb, trans_a=True, name="d_b_in", bn=640, bk=t, out_dtype=BF16)
    dukv = _matmul(ckn, dkv, trans_a=True, name="d_kv_up", out_dtype=BF16, bk=t)
    dckv, dgkv = _kv_mid_bwd(dkv, p['kv_w_ukv'], ckv, dkr, p['kv_norm_g'], rc, rs)
    ddown = _matmul(h1b, dckv, trans_a=True, name="d_kv_down", out_dtype=BF16, bm=512, bk=t)
    g['b_w_out'] = g['b_w_out'].reshape(N_DEV, -1, D_MODEL)
    g['b_w_in'] = g['b_w_in'].reshape(D_MODEL, N_DEV, -1).transpose(1, 0, 2)
    g['kv_w_down'] = ddown[:, :KV_LORA_RANK + QK_ROPE_DIM].reshape(N_DEV, -1, KV_LORA_RANK + QK_ROPE_DIM)
    dukv = dukv.reshape(KV_LORA_RANK, N_HEADS, HEAD_PAD)
    g['kv_w_uk'] = dukv[:, :, :QK_NOPE_DIM].reshape(N_DEV, -1, QK_NOPE_DIM)
    g['kv_w_uv'] = dukv[:, :, QK_NOPE_DIM:].reshape(N_DEV, -1, V_HEAD_DIM)
    g['b_w_uq'] = duq.reshape(Q_LORA_RANK, N_HEADS, HEAD_PAD)[:, :, :QK_NOPE_DIM + QK_ROPE_DIM].reshape(
        N_DEV, -1, QK_NOPE_DIM + QK_ROPE_DIM)
    sent1 = comm.send_group1(g)
    dr1, dr1b, dg0, db0, dba_out = _ln0_bwd(dr2, dprojb, p['b_w_in'], dckv, p['kv_w_down'], xh1, rstd1, ln_g0, sent1)
    dw_out = _matmul(u2, dr1b, trans_a=True, name="d_a_out", out_dtype=BF16, bm=512, bk=t).reshape(N_DEV, -1, D_MODEL)
    sent1b = comm.send_group1b({'a_w_out': dw_out})
    du1, dproj, dng, dnb, dbz = _conv_mid_bwd_rows(u1, proj, dr1b, p['a_w_out'], p['a_norm_g'], p['a_norm_b'], sent1b)
    dproj, dcw, dcb, dbv = _conv_bwd(du1, proj, p['a_conv_w'], dproj)
    half = D_MODEL // 2
    dw_lo = _matmul(xb, dproj, trans_a=True, name="d_a_in_lo", out_dtype=BF16, bm=half, bn=3 * e // N_DEV, bk=t,
                    out_blocked=True, a_cols=(0, half))
    small = (dg0, dg1, db0, db1, dbv, dbz, dcb, dng, dnb, dba_out, dgkv, dgq, dcw, loss)
    sent2 = comm.send_group2(dw_lo, small)
    dw_hi = _matmul(xb, dproj, trans_a=True, name="d_a_in_hi", out_dtype=BF16, bm=half, bn=3 * e // N_DEV, bk=t,
                    out_blocked=True, after=sent2, a_cols=(half, half))
    sent3 = comm.send_group3(dw_hi)
    grad_x = _grad_x(dproj, p['a_w_in3'], dr1, sent3)
    return loss, grad_x


def _grad_x(dproj, w3, dr1, after):
    t, d = dr1.shape
    nblk, _, cb = w3.shape
    tm = min(t, 512)
    extra = [] if after is None else [after]

    def body(a_ref, w_ref, dr_ref, *rest):
        o_ref = rest[-1]
        acc = ALPHA * dr_ref[...]
        for k in range(nblk):
            acc = acc + lax.dot_general(a_ref[:, k * cb:(k + 1) * cb], w_ref[k], NT, preferred_element_type=F32)
        o_ref[...] = acc

    row = pl.BlockSpec((tm, d), lambda i: (i, 0))
    return pl.pallas_call(
        body, name="grad_x",
        out_shape=jax.ShapeDtypeStruct((t, d), F32),
        grid=(t // tm,),
        in_specs=[pl.BlockSpec((tm, nblk * cb), lambda i: (i, 0)),
                  pl.BlockSpec((nblk, d, cb), lambda i: (0, 0, 0), pipeline_mode=pl.Buffered(1)), row]
                 + [pl.BlockSpec(memory_space=pl.ANY)] * len(extra),
        out_specs=row,
        compiler_params=_cparams(("parallel",)),
    )(dproj, w3, dr1, *extra)


def _shard_2d(a):
    return a.reshape(-1, a.shape[-1])


def _gathered_to_full(name, blocks):
    if name == 'a_w_in':
        return blocks
    if name == 'b_w_in':
        return blocks.transpose(1, 0, 2).reshape(D_MODEL, -1)
    if name == 'a_conv_w':
        return blocks.transpose(1, 0, 2).reshape(CONV_KERNEL, -1)
    if name in ('a_b_in', 'a_conv_b', 'a_norm_g', 'a_norm_b', 'a_b_out'):
        return blocks.reshape(-1)
    if name in ('kv_w_uk', 'kv_w_uv'):
        return blocks.reshape(KV_LORA_RANK, N_HEADS, -1)
    if name == 'b_w_uq':
        return blocks.reshape(Q_LORA_RANK, N_HEADS, -1)
    return blocks.reshape(-1, blocks.shape[-1])


def kernel(x, positions, ln_g, ln_b, a_w_in, a_b_in, a_conv_w, a_conv_b, a_norm_g, a_norm_b, a_w_out, a_b_out, kv_w_down, kv_norm_g, kv_w_uk, kv_w_uv, b_w_in, b_q_norm_g, b_w_uq, b_w_out, loss_target, m_ln_g, m_ln_b, m_a_w_in, m_a_b_in, m_a_conv_w, m_a_conv_b, m_a_norm_g, m_a_norm_b, m_a_w_out, m_a_b_out, m_kv_w_down, m_kv_norm_g, m_kv_w_uk, m_kv_w_uv, m_b_w_in, m_b_q_norm_g, m_b_w_uq, m_b_w_out, v_ln_g, v_ln_b, v_a_w_in, v_a_b_in, v_a_conv_w, v_a_conv_b, v_a_norm_g, v_a_norm_b, v_a_w_out, v_a_b_out, v_kv_w_down, v_kv_norm_g, v_kv_w_uk, v_kv_w_uv, v_b_w_in, v_b_q_norm_g, v_b_w_uq, v_b_w_out):
    w = dict(ln_g=ln_g, ln_b=ln_b, a_w_in=a_w_in, a_b_in=a_b_in, a_conv_w=a_conv_w, a_conv_b=a_conv_b,
             a_norm_g=a_norm_g, a_norm_b=a_norm_b, a_w_out=a_w_out, a_b_out=a_b_out, kv_w_down=kv_w_down,
             kv_norm_g=kv_norm_g, kv_w_uk=kv_w_uk, kv_w_uv=kv_w_uv, b_w_in=b_w_in, b_q_norm_g=b_q_norm_g,
             b_w_uq=b_w_uq, b_w_out=b_w_out)
    m = dict(ln_g=m_ln_g, ln_b=m_ln_b, a_w_in=m_a_w_in, a_b_in=m_a_b_in, a_conv_w=m_a_conv_w, a_conv_b=m_a_conv_b,
             a_norm_g=m_a_norm_g, a_norm_b=m_a_norm_b, a_w_out=m_a_w_out, a_b_out=m_a_b_out, kv_w_down=m_kv_w_down,
             kv_norm_g=m_kv_norm_g, kv_w_uk=m_kv_w_uk, kv_w_uv=m_kv_w_uv, b_w_in=m_b_w_in, b_q_norm_g=m_b_q_norm_g,
             b_w_uq=m_b_w_uq, b_w_out=m_b_w_out)
    v = dict(ln_g=v_ln_g, ln_b=v_ln_b, a_w_in=v_a_w_in, a_b_in=v_a_b_in, a_conv_w=v_a_conv_w, a_conv_b=v_a_conv_b,
             a_norm_g=v_a_norm_g, a_norm_b=v_a_norm_b, a_w_out=v_a_w_out, a_b_out=v_a_b_out, kv_w_down=v_kv_w_down,
             kv_norm_g=v_kv_norm_g, kv_w_uk=v_kv_w_uk, kv_w_uv=v_kv_w_uv, b_w_in=v_b_w_in, b_q_norm_g=v_b_q_norm_g,
             b_w_uq=v_b_w_uq, b_w_out=v_b_w_out)

    small_flat = jnp.concatenate([w[n].reshape(-1) for n in SMALL_WEIGHTS]).reshape(1, -1)
    rest_names = [n for n in MATMUL_WEIGHTS if n != 'a_w_in']
    group1 = ('b_w_out', 'b_w_uq', 'b_w_in', 'kv_w_uk', 'kv_w_uv', 'kv_w_down')
    group1b = ('a_w_out',)

    class Comm:
        def __init__(self):
            self.first, self.first_token = _comm_start(
                "gather_first_start", [_shard_2d(w['a_w_in']).astype(BF16), small_flat], True, None, CHIP_LEVEL_PEERS)

        def start_token(self):
            return self.first_token

        def first_weights(self, after):
            lands = _comm_wait("gather_first_wait", self.first, True, after, CHIP_LEVEL_PEERS)
            ga = _forward_to_sibling("gather_first_forward", list(lands))
            full = {'a_w_in': ga[0]}
            gs = ga[1].reshape(N_DEV, -1)
            off = 0
            for n in SMALL_WEIGHTS:
                size = math.prod(w[n].shape)
                full[n] = _gathered_to_full(n, gs[:, off:off + size].reshape((N_DEV,) + _shard_2d(w[n]).shape))
                off += size
            for n in REPLICATED:
                full[n] = w[n]
            dense = lambda a: a if a.shape[-1] % LANES == 0 else a.reshape(-1, LANES)
            self.rest, self.rest_token = _comm_start(
                "gather_rest_start", [dense(_shard_2d(w[n]).astype(BF16)) for n in rest_names], True, ga[0])
            return _prep_weights(full)

        def first_after(self):
            return self.rest_token

        def rest_weights(self, after):
            lands = _comm_wait("gather_rest_wait", self.rest, True, after)
            blocks = {n: lands[i].reshape((N_DEV,) + _shard_2d(w[n]).shape) for i, n in enumerate(rest_names)}
            return _prep_weights({n: _gathered_to_full(n, blocks[n]) for n in rest_names})

        def send_group1(self, g):
            self.g1, token = _comm_start("exchange_g1_start", [g[n] for n in group1], False, None)
            return token

        def send_group1b(self, g):
            self.g1b, token = _comm_start("exchange_g1b_start", [g[n] for n in group1b], False, None)
            return token

        def send_group2(self, dw_lo, small):
            self.g2, token = _comm_start("exchange_g2_start", [dw_lo, _pack_small_grads(*small)], False, None)
            return token

        def send_group3(self, dw_hi):
            self.g3, token = _comm_start("exchange_g3_start", [dw_hi], False, None)
            return token

    comm = Comm()

    _, grad_x = _local_step(x[0], positions[0], loss_target[0], comm)

    res = {}
    recv1 = _comm_wait("exchange_g1_wait", comm.g1, False, grad_x)
    for i, n in enumerate(group1):
        res[n] = _adamw(n, recv1[i], _shard_2d(w[n]), _shard_2d(m[n]), _shard_2d(v[n]))
    recv1b = _comm_wait("exchange_g1b_wait", comm.g1b, False, res[group1[-1]][0])
    for i, n in enumerate(group1b):
        res[n] = _adamw(n, recv1b[i], _shard_2d(w[n]), _shard_2d(m[n]), _shard_2d(v[n]))
    recv2 = _comm_wait("exchange_g2_wait", comm.g2, False, res[group1b[-1]][0])
    recv3 = _comm_wait("exchange_g3_wait", comm.g3, False, recv2[0])
    res['a_w_in'] = _adamw('a_w_in', [recv2[0], recv3[0]], _shard_2d(w['a_w_in']), _shard_2d(m['a_w_in']),
                           _shard_2d(v['a_w_in']))
    two_d = lambda d: [_shard_2d(d[n]) if d[n].ndim > 1 else d[n].reshape(1, -1) for n in SMALL_NAMES]
    sg, sd, sm, sv, loss = _adamw_small(recv2[-1], two_d(w), two_d(m), two_d(v))
    for i, n in enumerate(SMALL_NAMES):
        res[n] = (sg[i], sd[i], sm[i], sv[i])
    outs = [[res[n][j].reshape(w[n].shape) for n in WEIGHT_NAMES] for j in range(4)]
    return (loss[0, 0], grad_x[None], *outs[0], *outs[1], *outs[2], *outs[3])
```

```python
import functools
import math

import jax
import jax.numpy as jnp
from jax import lax
from jax.experimental import pallas as pl
from jax.experimental.pallas import tpu as pltpu

F32 = jnp.float32
BF16 = jnp.bfloat16

D_MODEL = 1024
DEPTH = 2
CONV_WIDTH = 2 * D_MODEL
CONV_KERNEL = 31
N_HEADS = 16
QK_NOPE_DIM = 128
QK_ROPE_DIM = 64
V_HEAD_DIM = 128
KV_LORA_RANK = D_MODEL // 4
Q_LORA_RANK = D_MODEL // 2
ROPE_THETA = 10000.0
LN_EPS = 1e-5
RMS_EPS = 1e-6
MASK_VALUE = -1e30
ALPHA = (2.0 * DEPTH) ** 0.25
ATTN_SCALE = 1.0 / math.sqrt(QK_NOPE_DIM + QK_ROPE_DIM)

ADAM_LR = 0.001
ADAM_B1 = 0.9
ADAM_B2 = 0.999
ADAM_EPS = 1e-08
ADAM_WD = 0.01
ADAM_STEP = 10

N_DEV = 8
LANES = 128
HEAD_PAD = 256
HALO = 32
VMEM_LIMIT = 56 * 1024 * 1024

WEIGHT_NAMES = ['ln_g', 'ln_b', 'a_w_in', 'a_b_in', 'a_conv_w', 'a_conv_b', 'a_norm_g', 'a_norm_b',
                'a_w_out', 'a_b_out', 'kv_w_down', 'kv_norm_g', 'kv_w_uk', 'kv_w_uv', 'b_w_in',
                'b_q_norm_g', 'b_w_uq', 'b_w_out']
REPLICATED = ('ln_g', 'ln_b', 'kv_norm_g', 'b_q_norm_g')
MATMUL_WEIGHTS = ('a_w_in', 'a_w_out', 'kv_w_down', 'kv_w_uk', 'kv_w_uv', 'b_w_in', 'b_w_uq', 'b_w_out')
SMALL_WEIGHTS = ('a_b_in', 'a_conv_w', 'a_conv_b', 'a_norm_g', 'a_norm_b', 'a_b_out')

NN = (((1,), (0,)), ((), ()))
NT = (((1,), (1,)), ((), ()))
TN = (((0,), (0,)), ((), ()))


def _cparams(sem):
    return pltpu.CompilerParams(dimension_semantics=sem, vmem_limit_bytes=VMEM_LIMIT)


def _sigmoid(x):
    return 0.5 * jnp.tanh(0.5 * x) + 0.5


def _fold_rows(x):
    parts = [x[r:r + 8] for r in range(0, x.shape[0], 8)]
    while len(parts) > 1:
        parts = [parts[i] + parts[i + 1] for i in range(0, len(parts) - 1, 2)] + ([parts[-1]] if len(parts) % 2 else [])
    return parts[0]


def _taps_by_residue(offset_of):
    groups = {}
    for j in range(CONV_KERNEL):
        off = offset_of(j)
        groups.setdefault(off % 8, []).append((j, off // 8))
    return sorted(groups.items())


def _swap_rope_halves(t):
    lane = lax.broadcasted_iota(jnp.int32, t.shape, 1)
    return jnp.where(lane < QK_ROPE_DIM // 2, pltpu.roll(t, LANES - QK_ROPE_DIM // 2, 1),
                     pltpu.roll(t, QK_ROPE_DIM // 2, 1))


def _matmul(a, b, *, name, bias=None, trans_a=False, trans_b=False, out_dtype=F32, bm=1024, bn=1024, bk=1024,
            b_blocked=False, out_blocked=False, after=None, a_cols=None):
    m_off = 0
    if trans_a:
        kdim, m = a.shape
        if a_cols is not None:
            m_off, m = a_cols
    else:
        m, kdim = a.shape
    if b_blocked and trans_b:
        n, k2 = b.shape[1], b.shape[0] * b.shape[2]
        bk = b.shape[2]
    elif b_blocked:
        k2, n = b.shape[1], b.shape[0] * b.shape[2]
        bn = b.shape[2]
    elif trans_b:
        n, k2 = b.shape
    else:
        k2, n = b.shape
    assert kdim == k2, (a.shape, b.shape)
    bm, bn, bk = min(bm, m), min(bn, n), min(bk, kdim)
    assert m % bm == 0 and n % bn == 0 and kdim % bk == 0, (name, m, n, kdim, bm, bn, bk)
    nk = kdim // bk
    dims = (((0 if trans_a else 1,), (1 if trans_b else 0,)), ((), ()))
    has_bias = bias is not None

    def body(*refs):
        refs = list(refs)
        a_ref, b_ref = refs[:2]
        del refs[:2]
        bias_ref = refs.pop(0) if has_bias else None
        if after is not None:
            refs.pop(0)
        o_ref = refs.pop(0)
        rest = refs
        prod = lax.dot_general(a_ref[...], b_ref[...], dims, preferred_element_type=F32)

        def finish(acc):
            if has_bias:
                acc = acc + bias_ref[...]
            o_ref[...] = acc.astype(out_dtype)

        if nk == 1:
            finish(prod)
        else:
            acc_ref = rest[0]
            k = pl.program_id(2)

            @pl.when(k == 0)
            def _():
                acc_ref[...] = prod

            @pl.when(k > 0)
            def _():
                acc_ref[...] += prod

            @pl.when(k == nk - 1)
            def _():
                finish(acc_ref[...])

    assert m_off % bm == 0
    a_first = m_off // bm
    a_spec = (pl.BlockSpec((bk, bm), lambda i, j, k: (k, i + a_first)) if trans_a
              else pl.BlockSpec((bm, bk), lambda i, j, k: (i, k)))
    if b_blocked and trans_b:
        b_spec = pl.BlockSpec((None, bn, bk), lambda i, j, k: (k, j, 0))
    elif b_blocked:
        b_spec = pl.BlockSpec((None, bk, bn), lambda i, j, k: (j, k, 0))
    elif trans_b:
        b_spec = pl.BlockSpec((bn, bk), lambda i, j, k: (j, k))
    else:
        b_spec = pl.BlockSpec((bk, bn), lambda i, j, k: (k, j))
    in_specs = [a_spec, b_spec]
    args = [a, b]
    if has_bias:
        in_specs.append(pl.BlockSpec((1, bn), lambda i, j, k: (0, j)))
        args.append(bias)
    if after is not None:
        in_specs.append(pl.BlockSpec(memory_space=pl.ANY))
        args.append(after)
    if out_blocked:
        out_struct = jax.ShapeDtypeStruct((n // bn, m, bn), out_dtype)
        out_spec = pl.BlockSpec((None, bm, bn), lambda i, j, k: (j, i, 0))
    else:
        out_struct = jax.ShapeDtypeStruct((m, n), out_dtype)
        out_spec = pl.BlockSpec((bm, bn), lambda i, j, k: (i, j))
    return pl.pallas_call(
        body, name=name,
        out_shape=out_struct,
        grid=(m // bm, n // bn, nk),
        in_specs=in_specs,
        out_specs=out_spec,
        scratch_shapes=[pltpu.VMEM((bm, bn), F32)] if nk > 1 else [],
        compiler_params=_cparams(("parallel", "parallel", "arbitrary")),
    )(*args)


def _rope_tables(pos_col, freq_row, x, after):
    t, d = x.shape
    tm = min(t, 512)
    extra = [] if after is None else [after]

    def body(pos_ref, f_ref, x_ref, *rest):
        c_ref, s_ref, xb_ref = rest[-3:]
        ang = pos_ref[...].astype(F32) * f_ref[...]
        lane = lax.broadcasted_iota(jnp.int32, ang.shape, 1)
        cos = jnp.cos(ang)
        sin = jnp.sin(ang)
        c_ref[...] = jnp.where(lane < QK_ROPE_DIM, cos, 0.0)
        s_ref[...] = jnp.where(lane < QK_ROPE_DIM // 2, -sin, jnp.where(lane < QK_ROPE_DIM, sin, 0.0))
        xb_ref[...] = x_ref[...].astype(BF16)

    lane_blk = pl.BlockSpec((tm, LANES), lambda i: (i, 0))
    row = pl.BlockSpec((tm, d), lambda i: (i, 0))
    return pl.pallas_call(
        body, name="rope_tables",
        out_shape=(jax.ShapeDtypeStruct((t, LANES), F32), jax.ShapeDtypeStruct((t, LANES), F32),
                   jax.ShapeDtypeStruct((t, d), BF16)),
        grid=(t // tm,),
        in_specs=[pl.BlockSpec((tm, 1), lambda i: (i, 0)), pl.BlockSpec((1, LANES), lambda i: (0, 0)), row]
                 + [pl.BlockSpec(memory_space=pl.ANY)] * len(extra),
        out_specs=(lane_blk, lane_blk, row),
        compiler_params=_cparams(("parallel",)),
    )(pos_col, freq_row, x, *extra)


def _conv_mid_fwd(proj, cw, cb, ng, nb):
    t = proj.shape[0]
    e = CONV_WIDTH
    tm = min(t, 128)
    hb = tm // HALO
    nchunk = e // LANES

    def body(val_ref, gg_ref, z_ref, valh_ref, ggh_ref, cw_ref, cb_ref, ng_ref, nb_ref, u1_ref, u2_ref, ext_ref, sh_ref):
        i = pl.program_id(0)
        u0 = val_ref[...] * _sigmoid(gg_ref[...])
        h0 = valh_ref[...] * _sigmoid(ggh_ref[...])
        ext_ref[0:HALO, :] = jnp.where(i > 0, h0, 0.0)
        ext_ref[HALO:, :] = u0

        def chunk(c, carry):
            col = pl.multiple_of(c * LANES, LANES)
            acc = jnp.zeros((tm, LANES), F32)
            for res, taps in _taps_by_residue(lambda j: HALO - (CONV_KERNEL - 1) + j):
                span = 8 * max(a for _, a in taps) + tm
                sh_ref[0:span, :] = ext_ref[res:res + span, pl.ds(col, LANES)]
                for j, a in taps:
                    acc = acc + cw_ref[j:j + 1, pl.ds(col, LANES)] * sh_ref[8 * a:8 * a + tm, :]
            u1_ref[:, pl.ds(col, LANES)] = acc + cb_ref[:, pl.ds(col, LANES)]
            return carry

        lax.fori_loop(0, nchunk, chunk, 0)
        g = ng_ref[...]
        b = nb_ref[...]
        for r in range(0, tm, 16):
            u1 = u1_ref[r:r + 16, :]
            mu = jnp.mean(u1, axis=-1, keepdims=True)
            xc = u1 - mu
            var = jnp.mean(xc * xc, axis=-1, keepdims=True)
            n = xc * lax.rsqrt(var + LN_EPS) * g + b
            z = z_ref[r:r + 16, :]
            u2_ref[r:r + 16, :] = ((n * _sigmoid(n)) * (z * _sigmoid(z))).astype(BF16)

    row = lambda c: pl.BlockSpec((tm, e), lambda i: (i, c))
    halo = lambda c: pl.BlockSpec((HALO, e), lambda i: (jnp.maximum(i * hb - 1, 0), c))
    par = lambda r: pl.BlockSpec((r, e), lambda i: (0, 0))
    return pl.pallas_call(
        body, name="conv_mid_fwd",
        out_shape=(jax.ShapeDtypeStruct((t, e), F32), jax.ShapeDtypeStruct((t, e), BF16)),
        grid=(t // tm,),
        in_specs=[row(0), row(1), row(2), halo(0), halo(1), par(HALO), par(1), par(1), par(1)],
        out_specs=(pl.BlockSpec((tm, e), lambda i: (i, 0)), pl.BlockSpec((tm, e), lambda i: (i, 0))),
        scratch_shapes=[pltpu.VMEM((tm + HALO, e), F32), pltpu.VMEM((tm + HALO, LANES), F32)],
        compiler_params=_cparams(("parallel",)),
    )(proj, proj, proj, proj, proj, cw, cb, ng, nb)


def _ln_res_fwd(x, u, w_out, b_out, g, b):
    t, d = x.shape
    ku = u.shape[1]
    tm = min(t, 512)

    def body(x_ref, u_ref, w_ref, bo_ref, g_ref, b_ref, h_ref, hb_ref, xh_ref, rs_ref):
        y = lax.dot_general(u_ref[...], w_ref[...], NN, preferred_element_type=F32) + bo_ref[...]
        r = ALPHA * x_ref[...] + y
        mu = jnp.mean(r, axis=-1, keepdims=True)
        xc = r - mu
        var = jnp.mean(xc * xc, axis=-1, keepdims=True)
        rstd = lax.rsqrt(var + LN_EPS)
        xh = xc * rstd
        h = xh * g_ref[...] + b_ref[...]
        h_ref[...] = h
        hb_ref[...] = h.astype(BF16)
        xh_ref[...] = xh
        rs_ref[...] = rstd

    row = pl.BlockSpec((tm, d), lambda i: (i, 0))
    par = pl.BlockSpec((1, d), lambda i: (0, 0))
    return pl.pallas_call(
        body, name="ln0_fwd",
        out_shape=(jax.ShapeDtypeStruct((t, d), F32), jax.ShapeDtypeStruct((t, d), BF16),
                   jax.ShapeDtypeStruct((t, d), F32), jax.ShapeDtypeStruct((t, 1), F32)),
        grid=(t // tm,),
        in_specs=[row, pl.BlockSpec((tm, ku), lambda i: (i, 0)),
                  pl.BlockSpec((ku, d), lambda i: (0, 0), pipeline_mode=pl.Buffered(1)), par, par, par],
        out_specs=(row, row, row, pl.BlockSpec((tm, 1), lambda i: (i, 0))),
        compiler_params=_cparams(("parallel",)),
    )(x, u, w_out, b_out, g, b)


def _kv_mid_fwd(h1b, w_down, g, rc, rs):
    t, d = h1b.shape
    tm = min(t, 512)
    r = KV_LORA_RANK

    def body(h_ref, w_ref, g_ref, c_ref, s_ref, ckv_ref, ckn_ref, kr_ref):
        ckv = lax.dot_general(h_ref[...], w_ref[...], NN, preferred_element_type=F32)
        ckv_ref[...] = ckv
        c = ckv[:, 0:r]
        ms = jnp.mean(c * c, axis=-1, keepdims=True)
        ckn_ref[...] = (c * lax.rsqrt(ms + RMS_EPS) * g_ref[...]).astype(BF16)
        tr = ckv[:, r:r + LANES]
        kr_ref[...] = (tr * c_ref[...] + _swap_rope_halves(tr) * s_ref[...]).astype(BF16)

    lane_blk = pl.BlockSpec((tm, LANES), lambda i: (i, 0))
    return pl.pallas_call(
        body, name="kv_mid_fwd",
        out_shape=(jax.ShapeDtypeStruct((t, r + LANES), F32), jax.ShapeDtypeStruct((t, r), BF16),
                   jax.ShapeDtypeStruct((t, LANES), BF16)),
        grid=(t // tm,),
        in_specs=[pl.BlockSpec((tm, d), lambda i: (i, 0)),
                  pl.BlockSpec((d, r + LANES), lambda i: (0, 0), pipeline_mode=pl.Buffered(1)),
                  pl.BlockSpec((1, r), lambda i: (0, 0)), lane_blk, lane_blk],
        out_specs=(pl.BlockSpec((tm, r + LANES), lambda i: (i, 0)), pl.BlockSpec((tm, r), lambda i: (i, 0)), lane_blk),
        compiler_params=_cparams(("parallel",)),
    )(h1b, w_down, g, rc, rs)


def _b_in_qnorm(h1b, w_in, g):
    t, d = h1b.shape
    n = w_in.shape[1]
    r = Q_LORA_RANK
    tm = min(t, 1024)
    bn = n // 2
    assert bn >= r and bn % LANES == 0

    def body(a_ref, w_ref, g_ref, o_ref, cqn_ref):
        acc = lax.dot_general(a_ref[...], w_ref[...], NN, preferred_element_type=F32)
        o_ref[...] = acc

        @pl.when(pl.program_id(1) == 0)
        def _():
            c = acc[:, 0:r]
            ms = jnp.mean(c * c, axis=-1, keepdims=True)
            cqn_ref[...] = (c * lax.rsqrt(ms + RMS_EPS) * g_ref[...]).astype(BF16)

    return pl.pallas_call(
        body, name="b_in",
        out_shape=(jax.ShapeDtypeStruct((t, n), F32), jax.ShapeDtypeStruct((t, r), BF16)),
        grid=(t // tm, 2),
        in_specs=[pl.BlockSpec((tm, d), lambda i, j: (i, 0)), pl.BlockSpec((d, bn), lambda i, j: (0, j)),
                  pl.BlockSpec((1, r), lambda i, j: (0, 0))],
        out_specs=(pl.BlockSpec((tm, bn), lambda i, j: (i, j)), pl.BlockSpec((tm, r), lambda i, j: (i, 0))),
        compiler_params=_cparams(("parallel", "arbitrary")),
    )(h1b, w_in, g)


def _q_up_rope(cqn, wuq, rc, rs):
    t, r = cqn.shape
    w = wuq.shape[1]
    tm = min(t, 1024)
    bn = 4 * HEAD_PAD

    def body(a_ref, b_ref, c_ref, s_ref, o_ref):
        q = lax.dot_general(a_ref[...], b_ref[...], NN, preferred_element_type=F32)
        c = c_ref[...]
        s = s_ref[...]
        for h in range(bn // HEAD_PAD):
            a = h * HEAD_PAD
            o_ref[:, a:a + LANES] = q[:, a:a + LANES].astype(BF16)
            tr = q[:, a + LANES:a + 2 * LANES]
            o_ref[:, a + LANES:a + 2 * LANES] = (tr * c + _swap_rope_halves(tr) * s).astype(BF16)

    return pl.pallas_call(
        body, name="q_up_rope",
        out_shape=jax.ShapeDtypeStruct((t, w), BF16),
        grid=(t // tm, w // bn),
        in_specs=[pl.BlockSpec((tm, r), lambda i, j: (i, 0)), pl.BlockSpec((r, bn), lambda i, j: (0, j)),
                  pl.BlockSpec((tm, LANES), lambda i, j: (i, 0)), pl.BlockSpec((tm, LANES), lambda i, j: (i, 0))],
        out_specs=pl.BlockSpec((tm, bn), lambda i, j: (i, j)),
        compiler_params=_cparams(("parallel", "parallel")),
    )(cqn, wuq, rc, rs)


def _flash_fwd(qcat, kv, kr, projb):
    t = qcat.shape[0]
    tq = min(t, 512)
    nq = t // tq
    exp2_scale = ATTN_SCALE * math.log2(math.e)
    z_first = Q_LORA_RANK // LANES

    def body(q_ref, kv_ref, kr_ref, z_ref, o_ref, o2_ref, lse_ref, kcat_ref):
        kcat_ref[:, 0:LANES] = kv_ref[:, 0:LANES]
        kcat_ref[:, LANES:] = kr_ref[...]
        rows = lax.broadcasted_iota(jnp.int32, (tq, tq), 0)
        cols = lax.broadcasted_iota(jnp.int32, (tq, tq), 1)
        for i in range(nq):
            a = i * tq
            q = q_ref[a:a + tq, :]
            sd = lax.dot_general(q, kcat_ref[a:a + tq, :], NT, preferred_element_type=F32)
            sd = jnp.where(cols <= rows, sd, MASK_VALUE)
            m = jnp.max(sd, axis=1, keepdims=True)
            if i > 0:
                sp = lax.dot_general(q, kcat_ref[0:a, :], NT, preferred_element_type=F32)
                m = jnp.maximum(m, jnp.max(sp, axis=1, keepdims=True))
            pd = jnp.exp2((sd - m) * exp2_scale)
            l = jnp.sum(pd, axis=1, keepdims=True)
            acc = lax.dot_general(pd.astype(BF16), kv_ref[a:a + tq, LANES:], NN, preferred_element_type=F32)
            if i > 0:
                pp = jnp.exp2((sp - m) * exp2_scale)
                l = l + jnp.sum(pp, axis=1, keepdims=True)
                acc = acc + lax.dot_general(pp.astype(BF16), kv_ref[0:a, LANES:], NN, preferred_element_type=F32)
            o = acc / l
            z = z_ref[a:a + tq, :]
            o_ref[a:a + tq, :] = o
            o2_ref[a:a + tq, :] = (o * (z * _sigmoid(z))).astype(BF16)
            lse_ref[a:a + tq, :] = jnp.broadcast_to(m * ATTN_SCALE + jnp.log(l), (tq, LANES))

    hw = N_HEADS * LANES
    head256 = pl.BlockSpec((t, HEAD_PAD), lambda h: (0, h))
    head128 = pl.BlockSpec((t, LANES), lambda h: (0, h))
    return pl.pallas_call(
        body, name="flash_fwd",
        out_shape=(jax.ShapeDtypeStruct((t, hw), F32), jax.ShapeDtypeStruct((t, hw), BF16),
                   jax.ShapeDtypeStruct((t, hw), F32)),
        grid=(N_HEADS,),
        in_specs=[head256, head256, pl.BlockSpec((t, LANES), lambda h: (0, 0), pipeline_mode=pl.Buffered(1)),
                  pl.BlockSpec((t, LANES), lambda h: (0, z_first + h))],
        out_specs=(head128, head128, head128),
        scratch_shapes=[pltpu.VMEM((t, HEAD_PAD), BF16)],
        compiler_params=_cparams(("parallel",)),
    )(qcat, kv, kr, projb)


def _final_ln_loss(h1, o2, w_out, g, b, target):
    t, d = h1.shape
    ku = o2.shape[1]
    tm = min(t, 512)

    def body(h_ref, u_ref, w_ref, g_ref, b_ref, t_ref, loss_ref, dr_ref, drb_ref, dg_ref, db_ref):
        i = pl.program_id(0)
        r = ALPHA * h_ref[...] + lax.dot_general(u_ref[...], w_ref[...], NN, preferred_element_type=F32)
        mu = jnp.mean(r, axis=-1, keepdims=True)
        xc = r - mu
        var = jnp.mean(xc * xc, axis=-1, keepdims=True)
        rstd = lax.rsqrt(var + LN_EPS)
        xh = xc * rstd
        g = g_ref[...]
        diff = xh * g + b_ref[...] - t_ref[...]
        part = 0.5 * jnp.sum(jnp.mean(diff * diff, axis=-1, keepdims=True))
        dh = diff * (1.0 / d)
        dgp = jnp.sum(dh * xh, axis=0, keepdims=True)
        dbp = jnp.sum(dh, axis=0, keepdims=True)

        @pl.when(i == 0)
        def _():
            loss_ref[...] = jnp.zeros_like(loss_ref)
            dg_ref[...] = jnp.zeros_like(dg_ref)
            db_ref[...] = jnp.zeros_like(db_ref)

        loss_ref[...] += jnp.full(loss_ref.shape, part, F32)
        dg_ref[...] += dgp
        db_ref[...] += dbp
        dxh = dh * g
        dr = rstd * (dxh - jnp.mean(dxh, axis=-1, keepdims=True) - xh * jnp.mean(dxh * xh, axis=-1, keepdims=True))
        dr_ref[...] = dr
        drb_ref[...] = dr.astype(BF16)

    row = pl.BlockSpec((tm, d), lambda i: (i, 0))
    par = pl.BlockSpec((1, d), lambda i: (0, 0))
    return pl.pallas_call(
        body, name="final_ln_loss",
        out_shape=(jax.ShapeDtypeStruct((1, LANES), F32), jax.ShapeDtypeStruct((t, d), F32),
                   jax.ShapeDtypeStruct((t, d), BF16), jax.ShapeDtypeStruct((1, d), F32),
                   jax.ShapeDtypeStruct((1, d), F32)),
        grid=(t // tm,),
        in_specs=[row, pl.BlockSpec((tm, ku), lambda i: (i, 0)),
                  pl.BlockSpec((ku, d), lambda i: (0, 0), pipeline_mode=pl.Buffered(1)), par, par, row],
        out_specs=(pl.BlockSpec((1, LANES), lambda i: (0, 0)), row, row, par, par),
        compiler_params=_cparams(("arbitrary",)),
    )(h1, o2, w_out, g, b, target)


def _gate_bwd(dr2b, w_out, o, projb):
    t, hw = o.shape
    d = dr2b.shape[1]
    tm = min(t, 512)
    bw = Q_LORA_RANK
    nb = hw // bw
    wb = projb.shape[1]

    def body(dr_ref, w_ref, o_ref, z_ref, dob_ref, dz_ref):
        z = z_ref[...]
        sg = _sigmoid(z)
        d2 = lax.dot_general(dr_ref[...], w_ref[...], NT, preferred_element_type=F32)
        dob_ref[...] = (d2 * (z * sg)).astype(BF16)
        dz_ref[...] = (d2 * o_ref[...] * (sg * (1.0 + z * (1.0 - sg)))).astype(BF16)

    blk = pl.BlockSpec((tm, bw), lambda i, j: (i, j))
    blk1 = pl.BlockSpec((tm, bw), lambda i, j: (i, j + 1))
    return pl.pallas_call(
        body, name="gate_bwd",
        out_shape=(jax.ShapeDtypeStruct((t, hw), BF16), jax.ShapeDtypeStruct((t, wb), BF16)),
        grid=(t // tm, nb),
        in_specs=[pl.BlockSpec((tm, d), lambda i, j: (i, 0)), pl.BlockSpec((bw, d), lambda i, j: (j, 0)), blk, blk1],
        out_specs=(blk, blk1),
        compiler_params=_cparams(("parallel", "parallel")),
    )(dr2b, w_out, o, projb)


def _flash_bwd(qcat, kv, kr, dob, lse, o, rc, rs):
    t = qcat.shape[0]
    tq = min(t, 512)
    nq = t // tq
    q_chunk = 2 * tq
    log2e = math.log2(math.e)
    exp2_scale = ATTN_SCALE * log2e
    once = pl.Buffered(1)

    def body(q_ref, kv_ref, kr_ref, do_ref, lse_ref, o_ref, c_ref, s_ref,
             dq_ref, dkv_ref, dkr_ref, kcat_ref, dqa_ref, dka_ref, dva_ref, dl_ref):
        h = pl.program_id(0)
        kcat_ref[:, 0:LANES] = kv_ref[:, 0:LANES]
        kcat_ref[:, LANES:] = kr_ref[...]
        dqa_ref[...] = jnp.zeros_like(dqa_ref)
        dl_ref[...] = jnp.sum(do_ref[...].astype(F32) * o_ref[...], axis=1, keepdims=True)

        @pl.when(h == 0)
        def _():
            dkr_ref[...] = jnp.zeros_like(dkr_ref)

        rows = lax.broadcasted_iota(jnp.int32, (tq, tq), 0)
        cols = lax.broadcasted_iota(jnp.int32, (tq, tq), 1)

        def block(qs, n, ks, masked):
            q = q_ref[qs:qs + n, :]
            kc = kcat_ref[ks:ks + tq, :]
            s = lax.dot_general(q, kc, NT, preferred_element_type=F32)
            if masked:
                s = jnp.where(cols <= rows, s, MASK_VALUE)
            p = jnp.exp2(s * exp2_scale - lse_ref[qs:qs + n, 0:1] * log2e)
            do = do_ref[qs:qs + n, :]
            dp = lax.dot_general(do, kv_ref[ks:ks + tq, LANES:], NT, preferred_element_type=F32)
            ds = (p * (dp - dl_ref[qs:qs + n, :])).astype(BF16)
            dva_ref[...] += lax.dot_general(p.astype(BF16), do, TN, preferred_element_type=F32)
            dka_ref[...] += lax.dot_general(ds, q, TN, preferred_element_type=F32)
            dqa_ref[qs:qs + n, :] += lax.dot_general(ds, kc, NN, preferred_element_type=F32)

        for kb in range(nq):
            a = kb * tq
            dka_ref[...] = jnp.zeros_like(dka_ref)
            dva_ref[...] = jnp.zeros_like(dva_ref)
            block(a, tq, a, True)
            qs = a + tq
            while qs < t:
                n = min(q_chunk, t - qs)
                block(qs, n, a, False)
                qs += n
            dk = dka_ref[...] * ATTN_SCALE
            dkv_ref[a:a + tq, 0:LANES] = dk[:, 0:LANES].astype(BF16)
            dkv_ref[a:a + tq, LANES:] = dva_ref[...].astype(BF16)
            dkr_ref[a:a + tq, :] += dk[:, LANES:]

        dq = dqa_ref[...] * ATTN_SCALE
        dq_ref[:, 0:LANES] = dq[:, 0:LANES].astype(BF16)
        d2 = dq[:, LANES:]
        dq_ref[:, LANES:] = (d2 * c_ref[...] - _swap_rope_halves(d2) * s_ref[...]).astype(BF16)

    hp = N_HEADS * HEAD_PAD
    head256 = pl.BlockSpec((t, HEAD_PAD), lambda h: (0, h))
    head128 = pl.BlockSpec((t, LANES), lambda h: (0, h))
    const128 = pl.BlockSpec((t, LANES), lambda h: (0, 0), pipeline_mode=once)
    return pl.pallas_call(
        body, name="flash_bwd",
        out_shape=(jax.ShapeDtypeStruct((t, hp), BF16), jax.ShapeDtypeStruct((t, hp), BF16),
                   jax.ShapeDtypeStruct((t, LANES), F32)),
        grid=(N_HEADS,),
        in_specs=[head256, head256, const128, head128, head128, head128, const128, const128],
        out_specs=(head256, head256, pl.BlockSpec((t, LANES), lambda h: (0, 0))),
        scratch_shapes=[pltpu.VMEM((t, HEAD_PAD), BF16), pltpu.VMEM((t, HEAD_PAD), F32),
                        pltpu.VMEM((tq, HEAD_PAD), F32), pltpu.VMEM((tq, LANES), F32), pltpu.VMEM((t, 1), F32)],
        compiler_params=_cparams(("arbitrary",)),
    )(qcat, kv, kr, dob, lse, o, rc, rs)


def _q_norm_bwd(dq2, w_uq, projb, g, dprojb):
    t = projb.shape[0]
    kq = dq2.shape[1]
    tm = min(t, 512)
    r = Q_LORA_RANK

    def body(dq_ref, w_ref, c_ref, g_ref, alias_ref, o_ref, dg_ref):
        i = pl.program_id(0)
        c = c_ref[...]
        d = lax.dot_general(dq_ref[...], w_ref[...], NT, preferred_element_type=F32)
        rr = lax.rsqrt(jnp.mean(c * c, axis=-1, keepdims=True) + RMS_EPS)
        xh = c * rr

        @pl.when(i == 0)
        def _():
            dg_ref[...] = jnp.zeros_like(dg_ref)

        dg_ref[...] += jnp.sum(d * xh, axis=0, keepdims=True)
        dxh = d * g_ref[...]
        o_ref[...] = (rr * (dxh - xh * jnp.mean(dxh * xh, axis=-1, keepdims=True))).astype(BF16)

    blk = pl.BlockSpec((tm, r), lambda i: (i, 0))
    par = pl.BlockSpec((1, r), lambda i: (0, 0))
    return pl.pallas_call(
        body, name="q_norm_bwd",
        out_shape=(jax.ShapeDtypeStruct(dprojb.shape, BF16), jax.ShapeDtypeStruct((1, r), F32)),
        grid=(t // tm,),
        in_specs=[pl.BlockSpec((tm, kq), lambda i: (i, 0)),
                  pl.BlockSpec((r, kq), lambda i: (0, 0), pipeline_mode=pl.Buffered(1)),
                  blk, par, pl.BlockSpec(memory_space=pl.ANY)],
        out_specs=(blk, par),
        input_output_aliases={4: 0},
        compiler_params=_cparams(("arbitrary",)),
    )(dq2, w_uq, projb, g, dprojb)


def _kv_mid_bwd(dkv, w_ukv, ckv, dkr, g, rc, rs):
    t = ckv.shape[0]
    kk = dkv.shape[1]
    tm = min(t, 512)
    r = KV_LORA_RANK

    def body(dkv_ref, w_ref, ckv_ref, dkr_ref, g_ref, c_ref, s_ref, o_ref, dg_ref):
        i = pl.program_id(0)
        c = ckv_ref[:, 0:r]
        d = lax.dot_general(dkv_ref[...], w_ref[...], NT, preferred_element_type=F32)
        rr = lax.rsqrt(jnp.mean(c * c, axis=-1, keepdims=True) + RMS_EPS)
        xh = c * rr

        @pl.when(i == 0)
        def _():
            dg_ref[...] = jnp.zeros_like(dg_ref)

        dg_ref[...] += jnp.sum(d * xh, axis=0, keepdims=True)
        dxh = d * g_ref[...]
        o_ref[:, 0:r] = (rr * (dxh - xh * jnp.mean(dxh * xh, axis=-1, keepdims=True))).astype(BF16)
        dk = dkr_ref[...]
        o_ref[:, r:] = (dk * c_ref[...] - _swap_rope_halves(dk) * s_ref[...]).astype(BF16)

    return pl.pallas_call(
        body, name="kv_mid_bwd",
        out_shape=(jax.ShapeDtypeStruct((t, r + LANES), BF16), jax.ShapeDtypeStruct((1, r), F32)),
        grid=(t // tm,),
        in_specs=[pl.BlockSpec((tm, kk), lambda i: (i, 0)),
                  pl.BlockSpec((r, kk), lambda i: (0, 0), pipeline_mode=pl.Buffered(1)),
                  pl.BlockSpec((tm, r + LANES), lambda i: (i, 0)),
                  pl.BlockSpec((tm, LANES), lambda i: (i, 0)), pl.BlockSpec((1, r), lambda i: (0, 0)),
                  pl.BlockSpec((tm, LANES), lambda i: (i, 0)), pl.BlockSpec((tm, LANES), lambda i: (i, 0))],
        out_specs=(pl.BlockSpec((tm, r + LANES), lambda i: (i, 0)), pl.BlockSpec((1, r), lambda i: (0, 0))),
        compiler_params=_cparams(("arbitrary",)),
    )(dkv, w_ukv, ckv, dkr, g, rc, rs)


def _ln0_bwd(dr2, dprojb, w_in, dckv, w_down, xh, rstd, g, after):
    t, d = dr2.shape
    kb, kd = dprojb.shape[1], dckv.shape[1]
    tm = min(t, 256)
    extra = [] if after is None else [after]

    def body(a_ref, pb_ref, wb_ref, pd_ref, wd_ref, xh_ref, rs_ref, g_ref, *rest):
        dr_ref, drb_ref, dg_ref, db_ref, dsum_ref = rest[-5:]
        i = pl.program_id(0)
        dh = (ALPHA * a_ref[...] + lax.dot_general(pb_ref[...], wb_ref[...], NT, preferred_element_type=F32)
              + lax.dot_general(pd_ref[...], wd_ref[...], NT, preferred_element_type=F32))
        xh = xh_ref[...]

        @pl.when(i == 0)
        def _():
            dg_ref[...] = jnp.zeros_like(dg_ref)
            db_ref[...] = jnp.zeros_like(db_ref)
            dsum_ref[...] = jnp.zeros_like(dsum_ref)

        dg_ref[...] += jnp.sum(dh * xh, axis=0, keepdims=True)
        db_ref[...] += jnp.sum(dh, axis=0, keepdims=True)
        dxh = dh * g_ref[...]
        dr = rs_ref[...] * (dxh - jnp.mean(dxh, axis=-1, keepdims=True) - xh * jnp.mean(dxh * xh, axis=-1, keepdims=True))
        dr_ref[...] = dr
        drb_ref[...] = dr.astype(BF16)
        dsum_ref[...] += jnp.sum(dr, axis=0, keepdims=True)

    row = pl.BlockSpec((tm, d), lambda i: (i, 0))
    par = pl.BlockSpec((1, d), lambda i: (0, 0))
    return pl.pallas_call(
        body, name="ln0_bwd",
        out_shape=(jax.ShapeDtypeStruct((t, d), F32), jax.ShapeDtypeStruct((t, d), BF16),
                   jax.ShapeDtypeStruct((1, d), F32), jax.ShapeDtypeStruct((1, d), F32),
                   jax.ShapeDtypeStruct((1, d), F32)),
        grid=(t // tm,),
        in_specs=[row, pl.BlockSpec((tm, kb), lambda i: (i, 0)),
                  pl.BlockSpec((d, kb), lambda i: (0, 0), pipeline_mode=pl.Buffered(1)),
                  pl.BlockSpec((tm, kd), lambda i: (i, 0)),
                  pl.BlockSpec((d, kd), lambda i: (0, 0), pipeline_mode=pl.Buffered(1)),
                  row, pl.BlockSpec((tm, 1), lambda i: (i, 0)), par]
                 + [pl.BlockSpec(memory_space=pl.ANY)] * len(extra),
        out_specs=(row, row, par, par, par),
        compiler_params=_cparams(("arbitrary",)),
    )(dr2, dprojb, w_in, dckv, w_down, xh, rstd, g, *extra)


def _conv_mid_bwd_rows(u1, proj, dr1b, w_out, ng, nb, after):
    t = u1.shape[0]
    e = CONV_WIDTH
    d = dr1b.shape[1]
    tm = min(t, 256)
    extra = [] if after is None else [after]

    rg = 16
    nt = t // tm

    def body(u1_ref, z_ref, dr_ref, w_ref, ng_ref, nb_ref, *rest):
        du1_ref, dz_ref, dng_ref, dnb_ref, dbz_ref, acc_ref, d_ref = rest[-7:]
        i = pl.program_id(0)

        @pl.when(i == 0)
        def _():
            acc_ref[...] = jnp.zeros_like(acc_ref)

        d_ref[...] = lax.dot_general(dr_ref[...], w_ref[...], NT, preferred_element_type=F32)
        g = ng_ref[...]
        b = nb_ref[...]

        def group(r, carry):
            rows = pl.ds(pl.multiple_of(r * rg, rg), rg)
            u1 = u1_ref[rows, :]
            mu = jnp.mean(u1, axis=-1, keepdims=True)
            xc = u1 - mu
            rstd = lax.rsqrt(jnp.mean(xc * xc, axis=-1, keepdims=True) + LN_EPS)
            xh = xc * rstd
            n = xh * g + b
            sn = _sigmoid(n)
            z = z_ref[rows, :]
            sz = _sigmoid(z)
            du2 = d_ref[rows, :]
            dz = du2 * (n * sn) * (sz * (1.0 + z * (1.0 - sz)))
            dn = du2 * (z * sz) * (sn * (1.0 + n * (1.0 - sn)))
            acc_ref[0:8, :] += _fold_rows(dn * xh)
            acc_ref[8:16, :] += _fold_rows(dn)
            acc_ref[16:24, :] += _fold_rows(dz)
            dz_ref[rows, :] = dz.astype(BF16)
            dxh = dn * g
            du1_ref[rows, :] = rstd * (dxh - jnp.mean(dxh, axis=-1, keepdims=True)
                                       - xh * jnp.mean(dxh * xh, axis=-1, keepdims=True))
            return carry

        lax.fori_loop(0, tm // rg, group, 0, unroll=8)

        @pl.when(i == nt - 1)
        def _():
            dng_ref[...] = jnp.sum(acc_ref[0:8, :], axis=0, keepdims=True)
            dnb_ref[...] = jnp.sum(acc_ref[8:16, :], axis=0, keepdims=True)
            dbz_ref[...] = jnp.sum(acc_ref[16:24, :], axis=0, keepdims=True)

    row = pl.BlockSpec((tm, e), lambda i: (i, 0))
    par = pl.BlockSpec((1, e), lambda i: (0, 0))
    return pl.pallas_call(
        body, name="conv_mid_bwd_rows",
        out_shape=(jax.ShapeDtypeStruct((t, e), F32), jax.ShapeDtypeStruct((t, 3 * e), BF16),
                   jax.ShapeDtypeStruct((1, e), F32), jax.ShapeDtypeStruct((1, e), F32),
                   jax.ShapeDtypeStruct((1, e), F32)),
        grid=(nt,),
        in_specs=[row, pl.BlockSpec((tm, e), lambda i: (i, 2)), pl.BlockSpec((tm, d), lambda i: (i, 0)),
                  pl.BlockSpec((e, d), lambda i: (0, 0), pipeline_mode=pl.Buffered(1)), par, par]
                 + [pl.BlockSpec(memory_space=pl.ANY)] * len(extra),
        out_specs=(row, pl.BlockSpec((tm, e), lambda i: (i, 2)), par, par, par),
        scratch_shapes=[pltpu.VMEM((24, e), F32), pltpu.VMEM((tm, e), F32)],
        compiler_params=_cparams(("arbitrary",)),
    )(u1, proj, dr1b, w_out, ng, nb, *extra)


def _conv_bwd(du1, proj, cw, dproj):
    t = du1.shape[0]
    e = CONV_WIDTH
    tm = min(t, 128)
    hb = tm // HALO
    nt = t // tm
    nchunk = e // LANES
    last_halo = t // HALO - 1

    def body(d_ref, dn_ref, val_ref, gg_ref, valh_ref, ggh_ref, cw_ref, alias_ref,
             dp_ref, dcw_ref, dcb_ref, dbv_ref, extu_ref, extd_ref, du0_ref, acc_ref, sh_ref):
        i = pl.program_id(0)
        h0 = valh_ref[...] * _sigmoid(ggh_ref[...])
        extu_ref[0:HALO, :] = jnp.where(i > 0, h0, 0.0)
        extu_ref[HALO:, :] = val_ref[...] * _sigmoid(gg_ref[...])
        extd_ref[0:tm, :] = d_ref[...]
        extd_ref[tm:, :] = jnp.where(i < nt - 1, dn_ref[...], 0.0)

        @pl.when(i == 0)
        def _():
            acc_ref[...] = jnp.zeros_like(acc_ref)

        def chunk(c, carry):
            col = pl.multiple_of(c * LANES, LANES)
            d = extd_ref[0:tm, pl.ds(col, LANES)]
            du0 = jnp.zeros((tm, LANES), F32)
            for res, taps in _taps_by_residue(lambda j: CONV_KERNEL - 1 - j):
                span = 8 * max(a for _, a in taps) + tm
                sh_ref[0:span, :] = extd_ref[res:res + span, pl.ds(col, LANES)]
                for j, a in taps:
                    du0 = du0 + cw_ref[j:j + 1, pl.ds(col, LANES)] * sh_ref[8 * a:8 * a + tm, :]
            for res, taps in _taps_by_residue(lambda j: HALO - (CONV_KERNEL - 1) + j):
                span = 8 * max(a for _, a in taps) + tm
                sh_ref[0:span, :] = extu_ref[res:res + span, pl.ds(col, LANES)]
                for j, a in taps:
                    prod = d * sh_ref[8 * a:8 * a + tm, :]
                    acc_ref[j * 8:(j + 1) * 8, pl.ds(col, LANES)] += _fold_rows(prod)
            acc_ref[CONV_KERNEL * 8:(CONV_KERNEL + 1) * 8, pl.ds(col, LANES)] += _fold_rows(d)
            du0_ref[:, pl.ds(col, LANES)] = du0
            return carry

        lax.fori_loop(0, nchunk, chunk, 0)
        kb = CONV_KERNEL * 8
        for r in range(0, tm, 16):
            sg = _sigmoid(gg_ref[r:r + 16, :])
            dval = du0_ref[r:r + 16, :] * sg
            dgg = dval * val_ref[r:r + 16, :] * (1.0 - sg)
            dp_ref[r:r + 16, 0:e] = dval.astype(BF16)
            dp_ref[r:r + 16, e:] = dgg.astype(BF16)
            acc_ref[kb + 8:kb + 16, :] += _fold_rows(dval)
            acc_ref[kb + 16:kb + 24, :] += _fold_rows(dgg)

        @pl.when(i == nt - 1)
        def _():
            for j in range(CONV_KERNEL):
                dcw_ref[j:j + 1, :] = jnp.sum(acc_ref[j * 8:(j + 1) * 8, :], axis=0, keepdims=True)
            dcw_ref[CONV_KERNEL:, :] = jnp.zeros((HALO - CONV_KERNEL, e), F32)
            dcb_ref[...] = jnp.sum(acc_ref[kb:kb + 8, :], axis=0, keepdims=True)
            dbv_ref[:, 0:e] = jnp.sum(acc_ref[kb + 8:kb + 16, :], axis=0, keepdims=True)
            dbv_ref[:, e:] = jnp.sum(acc_ref[kb + 16:kb + 24, :], axis=0, keepdims=True)

    row = lambda c: pl.BlockSpec((tm, e), lambda i: (i, c))
    halo_prev = lambda c: pl.BlockSpec((HALO, e), lambda i: (jnp.maximum(i * hb - 1, 0), c))
    halo_next = pl.BlockSpec((HALO, e), lambda i: (jnp.minimum((i + 1) * hb, last_halo), 0))
    return pl.pallas_call(
        body, name="conv_bwd",
        out_shape=(jax.ShapeDtypeStruct(dproj.shape, BF16), jax.ShapeDtypeStruct((HALO, e), F32),
                   jax.ShapeDtypeStruct((1, e), F32), jax.ShapeDtypeStruct((1, 2 * e), F32)),
        grid=(nt,),
        in_specs=[row(0), halo_next, row(0), row(1), halo_prev(0), halo_prev(1),
                  pl.BlockSpec((HALO, e), lambda i: (0, 0)), pl.BlockSpec(memory_space=pl.ANY)],
        out_specs=(pl.BlockSpec((tm, 2 * e), lambda i: (i, 0)), pl.BlockSpec((HALO, e), lambda i: (0, 0)),
                   pl.BlockSpec((1, e), lambda i: (0, 0)), pl.BlockSpec((1, 2 * e), lambda i: (0, 0))),
        scratch_shapes=[pltpu.VMEM((tm + HALO, e), F32), pltpu.VMEM((tm + HALO, e), F32),
                        pltpu.VMEM((tm, e), F32), pltpu.VMEM(((CONV_KERNEL + 3) * 8, e), F32),
                        pltpu.VMEM((tm + HALO, LANES), F32)],
        input_output_aliases={7: 0},
        compiler_params=_cparams(("arbitrary",)),
    )(du1, du1, proj, proj, proj, proj, cw, dproj)


def _adam_update(g, w, m, v):
    c1 = 1.0 - ADAM_B1 ** ADAM_STEP
    c2 = 1.0 - ADAM_B2 ** ADAM_STEP
    mn = ADAM_B1 * m + (1.0 - ADAM_B1) * g
    vn = ADAM_B2 * v + (1.0 - ADAM_B2) * (g * g)
    delta = -ADAM_LR * ((mn / c1) / (jnp.sqrt(vn / c2) + ADAM_EPS) + ADAM_WD * w)
    return delta, mn, vn


def _adamw(name, gbuf, w, m, v):
    parts = list(gbuf) if isinstance(gbuf, (list, tuple)) else [gbuf]
    npart = len(parts)
    r, c = w.shape
    tr = min(r // npart, 256)
    per_part = r // npart // tr
    assert r == npart * per_part * tr and all(p.shape == (N_DEV, r // npart, c) for p in parts)

    def body(*refs):
        g_refs = refs[:npart]
        w_ref, m_ref, v_ref, go_ref, d_ref, mo_ref, vo_ref = refs[npart:]
        i = pl.program_id(0)

        def run(g_ref):
            g = g_ref[0].astype(F32)
            for k in range(1, N_DEV):
                g = g + g_ref[k].astype(F32)
            go_ref[...] = g
            d_ref[...], mo_ref[...], vo_ref[...] = _adam_update(g, w_ref[...], m_ref[...], v_ref[...])

        if npart == 1:
            run(g_refs[0])
        else:
            for q in range(npart):
                pl.when((i >= q * per_part) & (i < (q + 1) * per_part))(functools.partial(run, g_refs[q]))

    blk = pl.BlockSpec((tr, c), lambda i: (i, 0))
    part_spec = lambda q: pl.BlockSpec((N_DEV, tr, c), lambda i: (0, jnp.clip(i - q * per_part, 0, per_part - 1), 0))
    shp = jax.ShapeDtypeStruct((r, c), F32)
    return pl.pallas_call(
        body, name="adamw_" + name,
        out_shape=(shp, shp, shp, shp),
        grid=(r // tr,),
        in_specs=[part_spec(q) for q in range(npart)] + [blk, blk, blk],
        out_specs=(blk, blk, blk, blk),
        compiler_params=_cparams(("parallel",)),
    )(*parts, w, m, v)


SMALL_TABLE_COLS = 256
SMALL_LAYOUT = {
    'ln_g': (0, 2, 1024), 'ln_b': (8, 2, 1024), 'a_b_in': (16, 1, 768), 'a_conv_b': (24, 1, 256),
    'a_norm_g': (32, 1, 256), 'a_norm_b': (40, 1, 256), 'a_b_out': (48, 1, 128), 'kv_norm_g': (56, 1, 256),
    'b_q_norm_g': (64, 1, 512), 'a_conv_w': (72, CONV_KERNEL, 256),
}
LOSS_ROW = 104
SMALL_TABLE_ROWS = 112
SMALL_NAMES = tuple(SMALL_LAYOUT)


def _small_pieces(name, row):
    r0, _, c = SMALL_LAYOUT[name]
    w = min(c, SMALL_TABLE_COLS)
    per_row = c // w
    return [(r0 + row * per_row + q, q * w, w) for q in range(per_row)]


def _pack_small_grads(dg0, dg1, db0, db1, dbv, dbz, dcb, dng, dnb, dba, dgkv, dgq, dcw, loss):
    e = CONV_WIDTH
    ce = e // N_DEV

    def body(dg0_ref, dg1_ref, db0_ref, db1_ref, dbv_ref, dbz_ref, dcb_ref, dng_ref, dnb_ref, dba_ref, dgkv_ref,
             dgq_ref, dcw_ref, loss_ref, o_ref):
        o_ref[...] = jnp.zeros_like(o_ref)
        for d in range(N_DEV):
            o_ref[d, LOSS_ROW:LOSS_ROW + 1, 0:LANES] = loss_ref[...]

        def put(d, name, row, src_ref, first_col=0):
            for trow, col, w in _small_pieces(name, row):
                o_ref[d, trow:trow + 1, 0:w] = src_ref[:, first_col + col:first_col + col + w]

        for d in range(N_DEV):
            put(d, 'ln_g', 0, dg0_ref)
            put(d, 'ln_g', 1, dg1_ref)
            put(d, 'ln_b', 0, db0_ref)
            put(d, 'ln_b', 1, db1_ref)
            for trow, col, w in _small_pieces('a_b_in', 0):
                gcol = d * 3 * ce + col
                src = dbv_ref[:, gcol:gcol + w] if gcol < 2 * e else dbz_ref[:, gcol - 2 * e:gcol - 2 * e + w]
                o_ref[d, trow:trow + 1, 0:w] = src
            put(d, 'a_conv_b', 0, dcb_ref, d * ce)
            put(d, 'a_norm_g', 0, dng_ref, d * ce)
            put(d, 'a_norm_b', 0, dnb_ref, d * ce)
            put(d, 'a_b_out', 0, dba_ref, d * LANES)
            put(d, 'kv_norm_g', 0, dgkv_ref)
            put(d, 'b_q_norm_g', 0, dgq_ref)
            r0 = SMALL_LAYOUT['a_conv_w'][0]
            o_ref[d, r0:r0 + HALO, 0:ce] = dcw_ref[:, d * ce:(d + 1) * ce]

    return pl.pallas_call(
        body, name="pack_small_grads",
        out_shape=jax.ShapeDtypeStruct((N_DEV, SMALL_TABLE_ROWS, SMALL_TABLE_COLS), F32),
        compiler_params=pltpu.CompilerParams(vmem_limit_bytes=VMEM_LIMIT),
    )(dg0, dg1, db0, db1, dbv, dbz, dcb, dng, dnb, dba, dgkv, dgq, dcw, loss)


def _adamw_small(gtab, ws, ms, vs):
    n = len(SMALL_NAMES)

    def body(*refs):
        g_ref = refs[0]
        w_refs, m_refs, v_refs = refs[1:1 + n], refs[1 + n:1 + 2 * n], refs[1 + 2 * n:1 + 3 * n]
        outs = refs[1 + 3 * n:]

        def summed(r0, r, c):
            g = g_ref[0, r0:r0 + r, 0:c]
            for k in range(1, N_DEV):
                g = g + g_ref[k, r0:r0 + r, 0:c]
            return g

        for i, name in enumerate(SMALL_NAMES):
            r0, r, c = SMALL_LAYOUT[name]
            if c <= SMALL_TABLE_COLS:
                blocks = [(slice(None), summed(r0, r, c))]
            else:
                blocks = [(slice(row, row + 1), jnp.concatenate([summed(tr, 1, w) for tr, _, w in _small_pieces(name, row)], axis=1))
                          for row in range(r)]
            for rows, g in blocks:
                delta, mn, vn = _adam_update(g, w_refs[i][rows, :], m_refs[i][rows, :], v_refs[i][rows, :])
                outs[i][rows, :] = g
                outs[n + i][rows, :] = delta
                outs[2 * n + i][rows, :] = mn
                outs[3 * n + i][rows, :] = vn
        outs[4 * n][...] = summed(LOSS_ROW, 1, LANES)

    shapes = tuple(jax.ShapeDtypeStruct(w.shape, F32) for w in ws)
    res = pl.pallas_call(
        body, name="adamw_small",
        out_shape=shapes * 4 + (jax.ShapeDtypeStruct((1, LANES), F32),),
        compiler_params=pltpu.CompilerParams(vmem_limit_bytes=VMEM_LIMIT),
    )(gtab, *ws, *ms, *vs)
    return res[:n], res[n:2 * n], res[2 * n:3 * n], res[3 * n:4 * n], res[4 * n]


def _mesh_peers():
    x, y, c = lax.axis_index("x"), lax.axis_index("y"), lax.axis_index("c")
    me = 4 * x + 2 * y + c
    peers = []
    for k in range(1, N_DEV):
        px = 1 - x if (k >> 2) & 1 else x
        py = 1 - y if (k >> 1) & 1 else y
        pc = 1 - c if k & 1 else c
        peers.append(((px, py, pc), 4 * px + 2 * py + pc))
    return me, peers


_HBM_SPEC = pl.BlockSpec(memory_space=pltpu.HBM)
_SEM_SPEC = pl.BlockSpec(memory_space=pltpu.SEMAPHORE)


ALL_PEERS = tuple(range(N_DEV - 1))
CHIP_LEVEL_PEERS = (0, 1, 3, 5)


def _split_copies(srcs, lands, send_sems, recv_sems, local_sems, gather, which=ALL_PEERS):
    me, peers = _mesh_peers()
    na = len(srcs)
    mine = lambda a, slot: srcs[a] if gather else srcs[a].at[slot]
    local = [pltpu.make_async_copy(mine(a, me), lands[a].at[me], local_sems.at[a]) for a in range(na)]
    sent, received = [], []
    for k, (dev, pid) in enumerate(peers):
        if k not in which:
            continue
        for a in range(na):
            s = a * (N_DEV - 1) + k
            sent.append(pltpu.make_async_remote_copy(
                src_ref=mine(a, pid), dst_ref=lands[a].at[me], send_sem=send_sems.at[s],
                recv_sem=recv_sems.at[s], device_id=dev, device_id_type=pl.DeviceIdType.MESH))
            received.append(pltpu.make_async_remote_copy(
                src_ref=mine(a, pid), dst_ref=lands[a].at[pid], send_sem=send_sems.at[s],
                recv_sem=recv_sems.at[s], device_id=dev, device_id_type=pl.DeviceIdType.MESH))
    return local, sent, received


def _comm_start(name, srcs, gather, after, which=ALL_PEERS):
    na = len(srcs)
    land_structs = [jax.ShapeDtypeStruct(((N_DEV,) + s.shape) if gather else s.shape, s.dtype) for s in srcs]
    extra = [] if after is None else [after]
    n_in = 2 * na + len(extra)

    def body(*refs):
        srcs_r, lands_r = refs[:na], refs[na:2 * na]
        send_sems, recv_sems, local_sems = refs[n_in:n_in + 3]
        token = refs[-1]
        local, sent, _ = _split_copies(srcs_r, lands_r, send_sems, recv_sems, local_sems, gather, which)
        for cp in local + sent:
            cp.start()
        token[...] = jnp.zeros_like(token)

    sem = pltpu.SemaphoreType.DMA((na * (N_DEV - 1),))
    outs = pl.pallas_call(
        body, name=name,
        out_shape=(sem, sem, pltpu.SemaphoreType.DMA((na,)),
                   *[pltpu.HBM(s.shape, s.dtype) for s in srcs],
                   *[pltpu.HBM(s.shape, s.dtype) for s in land_structs],
                   jax.ShapeDtypeStruct((8, LANES), F32)),
        in_specs=[_HBM_SPEC] * (2 * na) + [pl.BlockSpec(memory_space=pl.ANY)] * len(extra),
        out_specs=(_SEM_SPEC, _SEM_SPEC, _SEM_SPEC, *([_HBM_SPEC] * (2 * na)), pl.BlockSpec(memory_space=pltpu.VMEM)),
        input_output_aliases={i: 3 + i for i in range(2 * na)},
        compiler_params=pltpu.CompilerParams(has_side_effects=pltpu.SideEffectType.DATAFLOW_SIDE_EFFECTING),
    )(*[pltpu.with_memory_space_constraint(s, pltpu.HBM) for s in srcs],
      *[pltpu.with_memory_space_constraint(lax.empty(s.shape, s.dtype), pltpu.HBM) for s in land_structs],
      *extra)
    return (outs[:3], outs[3:3 + na], outs[3 + na:3 + 2 * na]), outs[-1]


def _comm_wait(name, handles, gather, after, which=ALL_PEERS):
    (send_sems, recv_sems, local_sems), srcs, lands = handles
    na = len(srcs)

    def body(*refs):
        srcs_r, lands_r = refs[:na], refs[na:2 * na]
        send_r, recv_r, local_r = refs[2 * na:2 * na + 3]
        local, sent, received = _split_copies(srcs_r, lands_r, send_r, recv_r, local_r, gather, which)
        for cp in sent:
            cp.wait_send()
        for cp in received:
            cp.wait_recv()
        for cp in local:
            cp.wait()

    outs = pl.pallas_call(
        body, name=name,
        out_shape=(*[pltpu.HBM(s.shape, s.dtype) for s in srcs], *[pltpu.HBM(s.shape, s.dtype) for s in lands]),
        in_specs=[_HBM_SPEC] * (2 * na) + [_SEM_SPEC] * 3 + [pl.BlockSpec(memory_space=pl.ANY)],
        out_specs=tuple([_HBM_SPEC] * (2 * na)),
        input_output_aliases={i: i for i in range(2 * na)},
        compiler_params=pltpu.CompilerParams(has_side_effects=pltpu.SideEffectType.DATAFLOW_SIDE_EFFECTING),
    )(*srcs, *lands, send_sems, recv_sems, local_sems, after)
    return outs[na:]


def _forward_to_sibling(name, lands):
    na = len(lands)

    def body(*refs):
        ins, outs = refs[:na], refs[na:2 * na]
        send_sems, recv_sems = refs[2 * na:]
        x, y, c = lax.axis_index("x"), lax.axis_index("y"), lax.axis_index("c")
        sibling = (x, y, 1 - c)
        chips = [(1 - x, y), (x, 1 - y), (1 - x, 1 - y)]
        slot = lambda cx, cy, cc: 4 * cx + 2 * cy + cc
        sends = []
        for j, (cx, cy) in enumerate(chips):
            for a in range(na):
                cp = pltpu.make_async_remote_copy(
                    src_ref=ins[a].at[slot(cx, cy, c)], dst_ref=outs[a].at[slot(cx, cy, c)],
                    send_sem=send_sems.at[a, j], recv_sem=recv_sems.at[a, j], device_id=sibling,
                    device_id_type=pl.DeviceIdType.MESH)
                cp.start()
                sends.append(cp)
        for j, (cx, cy) in enumerate(chips):
            for a in range(na):
                pltpu.make_async_remote_copy(
                    src_ref=ins[a].at[slot(cx, cy, 1 - c)], dst_ref=outs[a].at[slot(cx, cy, 1 - c)],
                    send_sem=send_sems.at[a, j], recv_sem=recv_sems.at[a, j], device_id=sibling,
                    device_id_type=pl.DeviceIdType.MESH).wait_recv()
        for cp in sends:
            cp.wait_send()

    hbm = pl.BlockSpec(memory_space=pl.ANY)
    return pl.pallas_call(
        body, name=name,
        out_shape=tuple(jax.ShapeDtypeStruct(s.shape, s.dtype) for s in lands),
        in_specs=[hbm] * na, out_specs=(hbm,) * na,
        input_output_aliases={a: a for a in range(na)},
        scratch_shapes=[pltpu.SemaphoreType.DMA((na, 3)), pltpu.SemaphoreType.DMA((na, 3))],
    )(*lands)


def _prep_weights(w):
    e = CONV_WIDTH
    p = {}
    if 'a_w_in' in w:
        if w['a_w_in'].shape == (N_DEV, D_MODEL, 3 * e // N_DEV):
            p['a_w_in3'] = w['a_w_in']
        else:
            p['a_w_in3'] = w['a_w_in'].reshape(D_MODEL, N_DEV, 3 * e // N_DEV).transpose(1, 0, 2)
        p['a_b_in'] = w['a_b_in'].reshape(1, 3 * e)
        p['a_conv_w'] = jnp.pad(w['a_conv_w'].reshape(CONV_KERNEL, e), ((0, HALO - CONV_KERNEL), (0, 0)))
        p['a_conv_b'] = w['a_conv_b'].reshape(1, e)
        p['a_norm_g'] = w['a_norm_g'].reshape(1, e)
        p['a_norm_b'] = w['a_norm_b'].reshape(1, e)
        p['a_b_out'] = w['a_b_out'].reshape(1, D_MODEL)
        p['kv_norm_g'] = w['kv_norm_g'].reshape(1, KV_LORA_RANK)
        p['b_q_norm_g'] = w['b_q_norm_g'].reshape(1, Q_LORA_RANK)
        p['ln_g'] = w['ln_g']
        p['ln_b'] = w['ln_b']
    if 'a_w_out' in w:
        p['a_w_out'] = w['a_w_out'].reshape(e, D_MODEL)
        p['kv_w_down'] = jnp.pad(w['kv_w_down'], ((0, 0), (0, KV_LORA_RANK + LANES - w['kv_w_down'].shape[1])))
        p['kv_w_ukv'] = jnp.concatenate([w['kv_w_uk'], w['kv_w_uv']], axis=2).reshape(KV_LORA_RANK, N_HEADS * HEAD_PAD)
        p['b_w_in'] = w['b_w_in'].reshape(D_MODEL, -1)
        uq = w['b_w_uq'].reshape(Q_LORA_RANK, N_HEADS, QK_NOPE_DIM + QK_ROPE_DIM)
        p['b_w_uq'] = jnp.pad(uq, ((0, 0), (0, 0), (0, HEAD_PAD - uq.shape[2]))).reshape(Q_LORA_RANK, N_HEADS * HEAD_PAD)
        p['b_w_out'] = w['b_w_out'].reshape(N_HEADS * V_HEAD_DIM, D_MODEL)
    return p


def _local_step(x, positions, target, comm):
    t = x.shape[0]
    e = CONV_WIDTH
    freqs = ROPE_THETA ** (-jnp.arange(0, QK_ROPE_DIM, 2, dtype=F32) / QK_ROPE_DIM)
    freq_row = jnp.concatenate([freqs, freqs, jnp.zeros((LANES - QK_ROPE_DIM,), F32)]).reshape(1, LANES)
    rc, rs, xb = _rope_tables(positions.reshape(t, 1), freq_row, x, comm.start_token())
    p = comm.first_weights(rc)
    ln_g0, ln_b0 = p['ln_g'][0:1], p['ln_b'][0:1]
    ln_g1, ln_b1 = p['ln_g'][1:2], p['ln_b'][1:2]

    proj = _matmul(xb, p['a_w_in3'], bias=p['a_b_in'], name="a_in", b_blocked=True, bm=2048,
                   after=comm.first_after())
    u1, u2 = _conv_mid_fwd(proj, p['a_conv_w'], p['a_conv_b'], p['a_norm_g'], p['a_norm_b'])
    p = {**p, **comm.rest_weights(u2)}
    h1, h1b, xh1, rstd1 = _ln_res_fwd(x, u2, p['a_w_out'], p['a_b_out'], ln_g0, ln_b0)
    ckv, ckn, kr = _kv_mid_fwd(h1b, p['kv_w_down'], p['kv_norm_g'], rc, rs)
    kv = _matmul(ckn, p['kv_w_ukv'], name="kv_up", out_dtype=BF16)
    projb, cqn = _b_in_qnorm(h1b, p['b_w_in'], p['b_q_norm_g'])
    qcat = _q_up_rope(cqn, p['b_w_uq'], rc, rs)
    o, o2, lse = _flash_fwd(qcat, kv, kr, projb)
    loss, dr2, dr2b, dg1, db1 = _final_ln_loss(h1, o2, p['b_w_out'], ln_g1, ln_b1, target)

    g = {}
    g['b_w_out'] = _matmul(o2, dr2b, trans_a=True, name="d_b_out", out_dtype=BF16, bm=512, bk=t)
    dob, dprojb = _gate_bwd(dr2b, p['b_w_out'], o, projb)
    dq2, dkv, dkr = _flash_bwd(qcat, kv, kr, dob, lse, o, rc, rs)
    duq = _matmul(cqn, dq2, trans_a=True, name="d_q_up", out_dtype=BF16, bk=t)
    dprojb, dgq = _q_norm_bwd(dq2, p['b_w_uq'], projb, p['b_q_norm_g'], dprojb)
    g['b_w_in'] = _matmul(h1b, dprojb, trans_a=True, name="d_b_in", bn=640, bk=t, out_dtype=BF16)
    dukv = _matmul(ckn, dkv, trans_a=True, name="d_kv_up", out_dtype=BF16, bk=t)
    dckv, dgkv = _kv_mid_bwd(dkv, p['kv_w_ukv'], ckv, dkr, p['kv_norm_g'], rc, rs)
    ddown = _matmul(h1b, dckv, trans_a=True, name="d_kv_down", out_dtype=BF16, bm=512, bk=t)
    g['b_w_out'] = g['b_w_out'].reshape(N_DEV, -1, D_MODEL)
    g['b_w_in'] = g['b_w_in'].reshape(D_MODEL, N_DEV, -1).transpose(1, 0, 2)
    g['kv_w_down'] = ddown[:, :KV_LORA_RANK + QK_ROPE_DIM].reshape(N_DEV, -1, KV_LORA_RANK + QK_ROPE_DIM)
    dukv = dukv.reshape(KV_LORA_RANK, N_HEADS, HEAD_PAD)
    g['kv_w_uk'] = dukv[:, :, :QK_NOPE_DIM].reshape(N_DEV, -1, QK_NOPE_DIM)
    g['kv_w_uv'] = dukv[:, :, QK_NOPE_DIM:].reshape(N_DEV, -1, V_HEAD_DIM)
    g['b_w_uq'] = duq.reshape(Q_LORA_RANK, N_HEADS, HEAD_PAD)[:, :, :QK_NOPE_DIM + QK_ROPE_DIM].reshape(
        N_DEV, -1, QK_NOPE_DIM + QK_ROPE_DIM)
    sent1 = comm.send_group1(g)
    dr1, dr1b, dg0, db0, dba_out = _ln0_bwd(dr2, dprojb, p['b_w_in'], dckv, p['kv_w_down'], xh1, rstd1, ln_g0, sent1)
    dw_out = _matmul(u2, dr1b, trans_a=True, name="d_a_out", out_dtype=BF16, bm=512, bk=t).reshape(N_DEV, -1, D_MODEL)
    sent1b = comm.send_group1b({'a_w_out': dw_out})
    du1, dproj, dng, dnb, dbz = _conv_mid_bwd_rows(u1, proj, dr1b, p['a_w_out'], p['a_norm_g'], p['a_norm_b'], sent1b)
    dproj, dcw, dcb, dbv = _conv_bwd(du1, proj, p['a_conv_w'], dproj)
    half = D_MODEL // 2
    dw_lo = _matmul(xb, dproj, trans_a=True, name="d_a_in_lo", out_dtype=BF16, bm=half, bn=3 * e // N_DEV, bk=t,
                    out_blocked=True, a_cols=(0, half))
    small = (dg0, dg1, db0, db1, dbv, dbz, dcb, dng, dnb, dba_out, dgkv, dgq, dcw, loss)
    sent2 = comm.send_group2(dw_lo, small)
    dw_hi = _matmul(xb, dproj, trans_a=True, name="d_a_in_hi", out_dtype=BF16, bm=half, bn=3 * e // N_DEV, bk=t,
                    out_blocked=True, after=sent2, a_cols=(half, half))
    sent3 = comm.send_group3(dw_hi)
    grad_x = _grad_x(dproj, p['a_w_in3'], dr1, sent3)
    return loss, grad_x


def _grad_x(dproj, w3, dr1, after):
    t, d = dr1.shape
    nblk, _, cb = w3.shape
    tm = min(t, 512)
    extra = [] if after is None else [after]

    def body(a_ref, w_ref, dr_ref, *rest):
        o_ref = rest[-1]
        acc = ALPHA * dr_ref[...]
        for k in range(nblk):
            acc = acc + lax.dot_general(a_ref[:, k * cb:(k + 1) * cb], w_ref[k], NT, preferred_element_type=F32)
        o_ref[...] = acc

    row = pl.BlockSpec((tm, d), lambda i: (i, 0))
    return pl.pallas_call(
        body, name="grad_x",
        out_shape=jax.ShapeDtypeStruct((t, d), F32),
        grid=(t // tm,),
        in_specs=[pl.BlockSpec((tm, nblk * cb), lambda i: (i, 0)),
                  pl.BlockSpec((nblk, d, cb), lambda i: (0, 0, 0), pipeline_mode=pl.Buffered(1)), row]
                 + [pl.BlockSpec(memory_space=pl.ANY)] * len(extra),
        out_specs=row,
        compiler_params=_cparams(("parallel",)),
    )(dproj, w3, dr1, *extra)


def _shard_2d(a):
    return a.reshape(-1, a.shape[-1])


def _gathered_to_full(name, blocks):
    if name == 'a_w_in':
        return blocks
    if name == 'b_w_in':
        return blocks.transpose(1, 0, 2).reshape(D_MODEL, -1)
    if name == 'a_conv_w':
        return blocks.transpose(1, 0, 2).reshape(CONV_KERNEL, -1)
    if name in ('a_b_in', 'a_conv_b', 'a_norm_g', 'a_norm_b', 'a_b_out'):
        return blocks.reshape(-1)
    if name in ('kv_w_uk', 'kv_w_uv'):
        return blocks.reshape(KV_LORA_RANK, N_HEADS, -1)
    if name == 'b_w_uq':
        return blocks.reshape(Q_LORA_RANK, N_HEADS, -1)
    return blocks.reshape(-1, blocks.shape[-1])


def kernel(x, positions, ln_g, ln_b, a_w_in, a_b_in, a_conv_w, a_conv_b, a_norm_g, a_norm_b, a_w_out, a_b_out, kv_w_down, kv_norm_g, kv_w_uk, kv_w_uv, b_w_in, b_q_norm_g, b_w_uq, b_w_out, loss_target, m_ln_g, m_ln_b, m_a_w_in, m_a_b_in, m_a_conv_w, m_a_conv_b, m_a_norm_g, m_a_norm_b, m_a_w_out, m_a_b_out, m_kv_w_down, m_kv_norm_g, m_kv_w_uk, m_kv_w_uv, m_b_w_in, m_b_q_norm_g, m_b_w_uq, m_b_w_out, v_ln_g, v_ln_b, v_a_w_in, v_a_b_in, v_a_conv_w, v_a_conv_b, v_a_norm_g, v_a_norm_b, v_a_w_out, v_a_b_out, v_kv_w_down, v_kv_norm_g, v_kv_w_uk, v_kv_w_uv, v_b_w_in, v_b_q_norm_g, v_b_w_uq, v_b_w_out):
    w = dict(ln_g=ln_g, ln_b=ln_b, a_w_in=a_w_in, a_b_in=a_b_in, a_conv_w=a_conv_w, a_conv_b=a_conv_b,
             a_norm_g=a_norm_g, a_norm_b=a_norm_b, a_w_out=a_w_out, a_b_out=a_b_out, kv_w_down=kv_w_down,
             kv_norm_g=kv_norm_g, kv_w_uk=kv_w_uk, kv_w_uv=kv_w_uv, b_w_in=b_w_in, b_q_norm_g=b_q_norm_g,
             b_w_uq=b_w_uq, b_w_out=b_w_out)
    m = dict(ln_g=m_ln_g, ln_b=m_ln_b, a_w_in=m_a_w_in, a_b_in=m_a_b_in, a_conv_w=m_a_conv_w, a_conv_b=m_a_conv_b,
             a_norm_g=m_a_norm_g, a_norm_b=m_a_norm_b, a_w_out=m_a_w_out, a_b_out=m_a_b_out, kv_w_down=m_kv_w_down,
             kv_norm_g=m_kv_norm_g, kv_w_uk=m_kv_w_uk, kv_w_uv=m_kv_w_uv, b_w_in=m_b_w_in, b_q_norm_g=m_b_q_norm_g,
             b_w_uq=m_b_w_uq, b_w_out=m_b_w_out)
    v = dict(ln_g=v_ln_g, ln_b=v_ln_b, a_w_in=v_a_w_in, a_b_in=v_a_b_in, a_conv_w=v_a_conv_w, a_conv_b=v_a_conv_b,
             a_norm_g=v_a_norm_g, a_norm_b=v_a_norm_b, a_w_out=v_a_w_out, a_b_out=v_a_b_out, kv_w_down=v_kv_w_down,
             kv_norm_g=v_kv_norm_g, kv_w_uk=v_kv_w_uk, kv_w_uv=v_kv_w_uv, b_w_in=v_b_w_in, b_q_norm_g=v_b_q_norm_g,
             b_w_uq=v_b_w_uq, b_w_out=v_b_w_out)

    small_flat = jnp.concatenate([w[n].reshape(-1) for n in SMALL_WEIGHTS]).reshape(1, -1)
    rest_names = [n for n in MATMUL_WEIGHTS if n != 'a_w_in']
    group1 = ('b_w_out', 'b_w_uq', 'b_w_in', 'kv_w_uk', 'kv_w_uv', 'kv_w_down')
    group1b = ('a_w_out',)

    class Comm:
        def __init__(self):
            self.first, self.first_token = _comm_start(
                "gather_first_start", [_shard_2d(w['a_w_in']).astype(BF16), small_flat], True, None, CHIP_LEVEL_PEERS)

        def start_token(self):
            return self.first_token

        def first_weights(self, after):
            lands = _comm_wait("gather_first_wait", self.first, True, after, CHIP_LEVEL_PEERS)
            ga = _forward_to_sibling("gather_first_forward", list(lands))
            full = {'a_w_in': ga[0]}
            gs = ga[1].reshape(N_DEV, -1)
            off = 0
            for n in SMALL_WEIGHTS:
                size = math.prod(w[n].shape)
                full[n] = _gathered_to_full(n, gs[:, off:off + size].reshape((N_DEV,) + _shard_2d(w[n]).shape))
                off += size
            for n in REPLICATED:
                full[n] = w[n]
            dense = lambda a: a if a.shape[-1] % LANES == 0 else a.reshape(-1, LANES)
            self.rest, self.rest_token = _comm_start(
                "gather_rest_start", [dense(_shard_2d(w[n]).astype(BF16)) for n in rest_names], True, ga[0])
            return _prep_weights(full)

        def first_after(self):
            return self.rest_token

        def rest_weights(self, after):
            lands = _comm_wait("gather_rest_wait", self.rest, True, after)
            blocks = {n: lands[i].reshape((N_DEV,) + _shard_2d(w[n]).shape) for i, n in enumerate(rest_names)}
            return _prep_weights({n: _gathered_to_full(n, blocks[n]) for n in rest_names})

        def send_group1(self, g):
            self.g1, token = _comm_start("exchange_g1_start", [g[n] for n in group1], False, None)
            return token

        def send_group1b(self, g):
            self.g1b, token = _comm_start("exchange_g1b_start", [g[n] for n in group1b], False, None)
            return token

        def send_group2(self, dw_lo, small):
            self.g2, token = _comm_start("exchange_g2_start", [dw_lo, _pack_small_grads(*small)], False, None)
            return token

        def send_group3(self, dw_hi):
            self.g3, token = _comm_start("exchange_g3_start", [dw_hi], False, None)
            return token

    comm = Comm()

    _, grad_x = _local_step(x[0], positions[0], loss_target[0], comm)

    res = {}
    recv1 = _comm_wait("exchange_g1_wait", comm.g1, False, grad_x)
    for i, n in enumerate(group1):
        res[n] = _adamw(n, recv1[i], _shard_2d(w[n]), _shard_2d(m[n]), _shard_2d(v[n]))
    recv1b = _comm_wait("exchange_g1b_wait", comm.g1b, False, res[group1[-1]][0])
    for i, n in enumerate(group1b):
        res[n] = _adamw(n, recv1b[i], _shard_2d(w[n]), _shard_2d(m[n]), _shard_2d(v[n]))
    recv2 = _comm_wait("exchange_g2_wait", comm.g2, False, res[group1b[-1]][0])
    recv3 = _comm_wait("exchange_g3_wait", comm.g3, False, recv2[0])
    res['a_w_in'] = _adamw('a_w_in', [recv2[0], recv3[0]], _shard_2d(w['a_w_in']), _shard_2d(m['a_w_in']),
                           _shard_2d(v['a_w_in']))
    two_d = lambda d: [_shard_2d(d[n]) if d[n].ndim > 1 else d[n].reshape(1, -1) for n in SMALL_NAMES]
    sg, sd, sm, sv, loss = _adamw_small(recv2[-1], two_d(w), two_d(m), two_d(v))
    for i, n in enumerate(SMALL_NAMES):
        res[n] = (sg[i], sd[i], sm[i], sv[i])
    outs = [[res[n][j].reshape(w[n].shape) for n in WEIGHT_NAMES] for j in range(4)]
    return (loss[0, 0], grad_x[None], *outs[0], *outs[1], *outs[2], *outs[3])
```

```python
import functools
import math

import jax
import jax.numpy as jnp
from jax import lax
from jax.experimental import pallas as pl
from jax.experimental.pallas import tpu as pltpu

F32 = jnp.float32
BF16 = jnp.bfloat16

D_MODEL = 1024
DEPTH = 2
CONV_WIDTH = 2 * D_MODEL
CONV_KERNEL = 31
N_HEADS = 16
QK_NOPE_DIM = 128
QK_ROPE_DIM = 64
V_HEAD_DIM = 128
KV_LORA_RANK = D_MODEL // 4
Q_LORA_RANK = D_MODEL // 2
ROPE_THETA = 10000.0
LN_EPS = 1e-5
RMS_EPS = 1e-6
MASK_VALUE = -1e30
ALPHA = (2.0 * DEPTH) ** 0.25
ATTN_SCALE = 1.0 / math.sqrt(QK_NOPE_DIM + QK_ROPE_DIM)

ADAM_LR = 0.001
ADAM_B1 = 0.9
ADAM_B2 = 0.999
ADAM_EPS = 1e-08
ADAM_WD = 0.01
ADAM_STEP = 10

N_DEV = 8
LANES = 128
HEAD_PAD = 256
HALO = 32
VMEM_LIMIT = 56 * 1024 * 1024

WEIGHT_NAMES = ['ln_g', 'ln_b', 'a_w_in', 'a_b_in', 'a_conv_w', 'a_conv_b', 'a_norm_g', 'a_norm_b',
                'a_w_out', 'a_b_out', 'kv_w_down', 'kv_norm_g', 'kv_w_uk', 'kv_w_uv', 'b_w_in',
                'b_q_norm_g', 'b_w_uq', 'b_w_out']
REPLICATED = ('ln_g', 'ln_b', 'kv_norm_g', 'b_q_norm_g')
MATMUL_WEIGHTS = ('a_w_in', 'a_w_out', 'kv_w_down', 'kv_w_uk', 'kv_w_uv', 'b_w_in', 'b_w_uq', 'b_w_out')
SMALL_WEIGHTS = ('a_b_in', 'a_conv_w', 'a_conv_b', 'a_norm_g', 'a_norm_b', 'a_b_out')

NN = (((1,), (0,)), ((), ()))
NT = (((1,), (1,)), ((), ()))
TN = (((0,), (0,)), ((), ()))


def _cparams(sem):
    return pltpu.CompilerParams(dimension_semantics=sem, vmem_limit_bytes=VMEM_LIMIT)


def _sigmoid(x):
    return 0.5 * jnp.tanh(0.5 * x) + 0.5


def _fold_rows(x):
    parts = [x[r:r + 8] for r in range(0, x.shape[0], 8)]
    while len(parts) > 1:
        parts = [parts[i] + parts[i + 1] for i in range(0, len(parts) - 1, 2)] + ([parts[-1]] if len(parts) % 2 else [])
    return parts[0]


def _taps_by_residue(offset_of):
    groups = {}
    for j in range(CONV_KERNEL):
        off = offset_of(j)
        groups.setdefault(off % 8, []).append((j, off // 8))
    return sorted(groups.items())


def _swap_rope_halves(t):
    lane = lax.broadcasted_iota(jnp.int32, t.shape, 1)
    return jnp.where(lane < QK_ROPE_DIM // 2, pltpu.roll(t, LANES - QK_ROPE_DIM // 2, 1),
                     pltpu.roll(t, QK_ROPE_DIM // 2, 1))


def _matmul(a, b, *, name, bias=None, trans_a=False, trans_b=False, out_dtype=F32, bm=1024, bn=1024, bk=1024,
            b_blocked=False, out_blocked=False, after=None, a_cols=None):
    m_off = 0
    if trans_a:
        kdim, m = a.shape
        if a_cols is not None:
            m_off, m = a_cols
    else:
        m, kdim = a.shape
    if b_blocked and trans_b:
        n, k2 = b.shape[1], b.shape[0] * b.shape[2]
        bk = b.shape[2]
    elif b_blocked:
        k2, n = b.shape[1], b.shape[0] * b.shape[2]
        bn = b.shape[2]
    elif trans_b:
        n, k2 = b.shape
    else:
        k2, n = b.shape
    assert kdim == k2, (a.shape, b.shape)
    bm, bn, bk = min(bm, m), min(bn, n), min(bk, kdim)
    assert m % bm == 0 and n % bn == 0 and kdim % bk == 0, (name, m, n, kdim, bm, bn, bk)
    nk = kdim // bk
    dims = (((0 if trans_a else 1,), (1 if trans_b else 0,)), ((), ()))
    has_bias = bias is not None

    def body(*refs):
        refs = list(refs)
        a_ref, b_ref = refs[:2]
        del refs[:2]
        bias_ref = refs.pop(0) if has_bias else None
        if after is not None:
            refs.pop(0)
        o_ref = refs.pop(0)
        rest = refs
        prod = lax.dot_general(a_ref[...], b_ref[...], dims, preferred_element_type=F32)

        def finish(acc):
            if has_bias:
                acc = acc + bias_ref[...]
            o_ref[...] = acc.astype(out_dtype)

        if nk == 1:
            finish(prod)
        else:
            acc_ref = rest[0]
            k = pl.program_id(2)

            @pl.when(k == 0)
            def _():
                acc_ref[...] = prod

            @pl.when(k > 0)
            def _():
                acc_ref[...] += prod

            @pl.when(k == nk - 1)
            def _():
                finish(acc_ref[...])

    assert m_off % bm == 0
    a_first = m_off // bm
    a_spec = (pl.BlockSpec((bk, bm), lambda i, j, k: (k, i + a_first)) if trans_a
              else pl.BlockSpec((bm, bk), lambda i, j, k: (i, k)))
    if b_blocked and trans_b:
        b_spec = pl.BlockSpec((None, bn, bk), lambda i, j, k: (k, j, 0))
    elif b_blocked:
        b_spec = pl.BlockSpec((None, bk, bn), lambda i, j, k: (j, k, 0))
    elif trans_b:
        b_spec = pl.BlockSpec((bn, bk), lambda i, j, k: (j, k))
    else:
        b_spec = pl.BlockSpec((bk, bn), lambda i, j, k: (k, j))
    in_specs = [a_spec, b_spec]
    args = [a, b]
    if has_bias:
        in_specs.append(pl.BlockSpec((1, bn), lambda i, j, k: (0, j)))
        args.append(bias)
    if after is not None:
        in_specs.append(pl.BlockSpec(memory_space=pl.ANY))
        args.append(after)
    if out_blocked:
        out_struct = jax.ShapeDtypeStruct((n // bn, m, bn), out_dtype)
        out_spec = pl.BlockSpec((None, bm, bn), lambda i, j, k: (j, i, 0))
    else:
        out_struct = jax.ShapeDtypeStruct((m, n), out_dtype)
        out_spec = pl.BlockSpec((bm, bn), lambda i, j, k: (i, j))
    return pl.pallas_call(
        body, name=name,
        out_shape=out_struct,
        grid=(m // bm, n // bn, nk),
        in_specs=in_specs,
        out_specs=out_spec,
        scratch_shapes=[pltpu.VMEM((bm, bn), F32)] if nk > 1 else [],
        compiler_params=_cparams(("parallel", "parallel", "arbitrary")),
    )(*args)


def _rope_tables(pos_col, freq_row, x, after):
    t, d = x.shape
    tm = min(t, 512)
    extra = [] if after is None else [after]

    def body(pos_ref, f_ref, x_ref, *rest):
        c_ref, s_ref, xb_ref = rest[-3:]
        ang = pos_ref[...].astype(F32) * f_ref[...]
        lane = lax.broadcasted_iota(jnp.int32, ang.shape, 1)
        cos = jnp.cos(ang)
        sin = jnp.sin(ang)
        c_ref[...] = jnp.where(lane < QK_ROPE_DIM, cos, 0.0)
        s_ref[...] = jnp.where(lane < QK_ROPE_DIM // 2, -sin, jnp.where(lane < QK_ROPE_DIM, sin, 0.0))
        xb_ref[...] = x_ref[...].astype(BF16)

    lane_blk = pl.BlockSpec((tm, LANES), lambda i: (i, 0))
    row = pl.BlockSpec((tm, d), lambda i: (i, 0))
    return pl.pallas_call(
        body, name="rope_tables",
        out_shape=(jax.ShapeDtypeStruct((t, LANES), F32), jax.ShapeDtypeStruct((t, LANES), F32),
                   jax.ShapeDtypeStruct((t, d), BF16)),
        grid=(t // tm,),
        in_specs=[pl.BlockSpec((tm, 1), lambda i: (i, 0)), pl.BlockSpec((1, LANES), lambda i: (0, 0)), row]
                 + [pl.BlockSpec(memory_space=pl.ANY)] * len(extra),
        out_specs=(lane_blk, lane_blk, row),
        compiler_params=_cparams(("parallel",)),
    )(pos_col, freq_row, x, *extra)


def _conv_mid_fwd(proj, cw, cb, ng, nb):
    t = proj.shape[0]
    e = CONV_WIDTH
    tm = min(t, 128)
    hb = tm // HALO
    nchunk = e // LANES

    def body(val_ref, gg_ref, z_ref, valh_ref, ggh_ref, cw_ref, cb_ref, ng_ref, nb_ref, u1_ref, u2_ref, ext_ref, sh_ref):
        i = pl.program_id(0)
        u0 = val_ref[...] * _sigmoid(gg_ref[...])
        h0 = valh_ref[...] * _sigmoid(ggh_ref[...])
        ext_ref[0:HALO, :] = jnp.where(i > 0, h0, 0.0)
        ext_ref[HALO:, :] = u0

        def chunk(c, carry):
            col = pl.multiple_of(c * LANES, LANES)
            acc = jnp.zeros((tm, LANES), F32)
            for res, taps in _taps_by_residue(lambda j: HALO - (CONV_KERNEL - 1) + j):
                span = 8 * max(a for _, a in taps) + tm
                sh_ref[0:span, :] = ext_ref[res:res + span, pl.ds(col, LANES)]
                for j, a in taps:
                    acc = acc + cw_ref[j:j + 1, pl.ds(col, LANES)] * sh_ref[8 * a:8 * a + tm, :]
            u1_ref[:, pl.ds(col, LANES)] = acc + cb_ref[:, pl.ds(col, LANES)]
            return carry

        lax.fori_loop(0, nchunk, chunk, 0)
        g = ng_ref[...]
        b = nb_ref[...]
        for r in range(0, tm, 16):
            u1 = u1_ref[r:r + 16, :]
            mu = jnp.mean(u1, axis=-1, keepdims=True)
            xc = u1 - mu
            var = jnp.mean(xc * xc, axis=-1, keepdims=True)
            n = xc * lax.rsqrt(var + LN_EPS) * g + b
            z = z_ref[r:r + 16, :]
            u2_ref[r:r + 16, :] = ((n * _sigmoid(n)) * (z * _sigmoid(z))).astype(BF16)

    row = lambda c: pl.BlockSpec((tm, e), lambda i: (i, c))
    halo = lambda c: pl.BlockSpec((HALO, e), lambda i: (jnp.maximum(i * hb - 1, 0), c))
    par = lambda r: pl.BlockSpec((r, e), lambda i: (0, 0))
    return pl.pallas_call(
        body, name="conv_mid_fwd",
        out_shape=(jax.ShapeDtypeStruct((t, e), F32), jax.ShapeDtypeStruct((t, e), BF16)),
        grid=(t // tm,),
        in_specs=[row(0), row(1), row(2), halo(0), halo(1), par(HALO), par(1), par(1), par(1)],
        out_specs=(pl.BlockSpec((tm, e), lambda i: (i, 0)), pl.BlockSpec((tm, e), lambda i: (i, 0))),
        scratch_shapes=[pltpu.VMEM((tm + HALO, e), F32), pltpu.VMEM((tm + HALO, LANES), F32)],
        compiler_params=_cparams(("parallel",)),
    )(proj, proj, proj, proj, proj, cw, cb, ng, nb)


def _ln_res_fwd(x, u, w_out, b_out, g, b):
    t, d = x.shape
    ku = u.shape[1]
    tm = min(t, 512)

    def body(x_ref, u_ref, w_ref, bo_ref, g_ref, b_ref, h_ref, hb_ref, xh_ref, rs_ref):
        y = lax.dot_general(u_ref[...], w_ref[...], NN, preferred_element_type=F32) + bo_ref[...]
        r = ALPHA * x_ref[...] + y
        mu = jnp.mean(r, axis=-1, keepdims=True)
        xc = r - mu
        var = jnp.mean(xc * xc, axis=-1, keepdims=True)
        rstd = lax.rsqrt(var + LN_EPS)
        xh = xc * rstd
        h = xh * g_ref[...] + b_ref[...]
        h_ref[...] = h
        hb_ref[...] = h.astype(BF16)
        xh_ref[...] = xh
        rs_ref[...] = rstd

    row = pl.BlockSpec((tm, d), lambda i: (i, 0))
    par = pl.BlockSpec((1, d), lambda i: (0, 0))
    return pl.pallas_call(
        body, name="ln0_fwd",
        out_shape=(jax.ShapeDtypeStruct((t, d), F32), jax.ShapeDtypeStruct((t, d), BF16),
                   jax.ShapeDtypeStruct((t, d), F32), jax.ShapeDtypeStruct((t, 1), F32)),
        grid=(t // tm,),
        in_specs=[row, pl.BlockSpec((tm, ku), lambda i: (i, 0)),
                  pl.BlockSpec((ku, d), lambda i: (0, 0), pipeline_mode=pl.Buffered(1)), par, par, par],
        out_specs=(row, row, row, pl.BlockSpec((tm, 1), lambda i: (i, 0))),
        compiler_params=_cparams(("parallel",)),
    )(x, u, w_out, b_out, g, b)


def _kv_mid_fwd(h1b, w_down, w_ukv, g, rc, rs):
    t, d = h1b.shape
    nkv = w_ukv.shape[1]
    tm = min(t, 512)
    r = KV_LORA_RANK

    def body(h_ref, w_ref, wu_ref, g_ref, c_ref, s_ref, ckv_ref, ckn_ref, kr_ref, kv_ref):
        ckv = lax.dot_general(h_ref[...], w_ref[...], NN, preferred_element_type=F32)
        ckv_ref[...] = ckv
        c = ckv[:, 0:r]
        ms = jnp.mean(c * c, axis=-1, keepdims=True)
        ckn = (c * lax.rsqrt(ms + RMS_EPS) * g_ref[...]).astype(BF16)
        ckn_ref[...] = ckn
        tr = ckv[:, r:r + LANES]
        kr_ref[...] = (tr * c_ref[...] + _swap_rope_halves(tr) * s_ref[...]).astype(BF16)
        kv_ref[...] = lax.dot_general(ckn, wu_ref[...], NN, preferred_element_type=F32).astype(BF16)

    lane_blk = pl.BlockSpec((tm, LANES), lambda i: (i, 0))
    return pl.pallas_call(
        body, name="kv_mid_fwd",
        out_shape=(jax.ShapeDtypeStruct((t, r + LANES), F32), jax.ShapeDtypeStruct((t, r), BF16),
                   jax.ShapeDtypeStruct((t, LANES), BF16), jax.ShapeDtypeStruct((t, nkv), BF16)),
        grid=(t // tm,),
        in_specs=[pl.BlockSpec((tm, d), lambda i: (i, 0)),
                  pl.BlockSpec((d, r + LANES), lambda i: (0, 0), pipeline_mode=pl.Buffered(1)),
                  pl.BlockSpec((r, nkv), lambda i: (0, 0), pipeline_mode=pl.Buffered(1)),
                  pl.BlockSpec((1, r), lambda i: (0, 0)), lane_blk, lane_blk],
        out_specs=(pl.BlockSpec((tm, r + LANES), lambda i: (i, 0)), pl.BlockSpec((tm, r), lambda i: (i, 0)), lane_blk,
                   pl.BlockSpec((tm, nkv), lambda i: (i, 0))),
        compiler_params=_cparams(("parallel",)),
    )(h1b, w_down, w_ukv, g, rc, rs)


def _b_in_qnorm(h1b, w_in, g):
    t, d = h1b.shape
    n = w_in.shape[1]
    r = Q_LORA_RANK
    tm = min(t, 1024)
    bn = n // 2
    assert bn >= r and bn % LANES == 0

    def body(a_ref, w_ref, g_ref, o_ref, cqn_ref):
        acc = lax.dot_general(a_ref[...], w_ref[...], NN, preferred_element_type=F32)
        o_ref[...] = acc

        @pl.when(pl.program_id(1) == 0)
        def _():
            c = acc[:, 0:r]
            ms = jnp.mean(c * c, axis=-1, keepdims=True)
            cqn_ref[...] = (c * lax.rsqrt(ms + RMS_EPS) * g_ref[...]).astype(BF16)

    return pl.pallas_call(
        body, name="b_in",
        out_shape=(jax.ShapeDtypeStruct((t, n), F32), jax.ShapeDtypeStruct((t, r), BF16)),
        grid=(t // tm, 2),
        in_specs=[pl.BlockSpec((tm, d), lambda i, j: (i, 0)), pl.BlockSpec((d, bn), lambda i, j: (0, j)),
                  pl.BlockSpec((1, r), lambda i, j: (0, 0))],
        out_specs=(pl.BlockSpec((tm, bn), lambda i, j: (i, j)), pl.BlockSpec((tm, r), lambda i, j: (i, 0))),
        compiler_params=_cparams(("parallel", "arbitrary")),
    )(h1b, w_in, g)


def _q_up_rope(cqn, wuq, rc, rs):
    t, r = cqn.shape
    w = wuq.shape[1]
    tm = min(t, 1024)
    bn = 4 * HEAD_PAD

    def body(a_ref, b_ref, c_ref, s_ref, o_ref):
        q = lax.dot_general(a_ref[...], b_ref[...], NN, preferred_element_type=F32)
        c = c_ref[...]
        s = s_ref[...]
        for h in range(bn // HEAD_PAD):
            a = h * HEAD_PAD
            o_ref[:, a:a + LANES] = q[:, a:a + LANES].astype(BF16)
            tr = q[:, a + LANES:a + 2 * LANES]
            o_ref[:, a + LANES:a + 2 * LANES] = (tr * c + _swap_rope_halves(tr) * s).astype(BF16)

    return pl.pallas_call(
        body, name="q_up_rope",
        out_shape=jax.ShapeDtypeStruct((t, w), BF16),
        grid=(t // tm, w // bn),
        in_specs=[pl.BlockSpec((tm, r), lambda i, j: (i, 0)), pl.BlockSpec((r, bn), lambda i, j: (0, j)),
                  pl.BlockSpec((tm, LANES), lambda i, j: (i, 0)), pl.BlockSpec((tm, LANES), lambda i, j: (i, 0))],
        out_specs=pl.BlockSpec((tm, bn), lambda i, j: (i, j)),
        compiler_params=_cparams(("parallel", "parallel")),
    )(cqn, wuq, rc, rs)


def _flash_fwd(qcat, kv, kr, projb):
    t = qcat.shape[0]
    tq = min(t, 512)
    nq = t // tq
    exp2_scale = ATTN_SCALE * math.log2(math.e)
    z_first = Q_LORA_RANK // LANES

    def body(q_ref, kv_ref, kr_ref, z_ref, o_ref, o2_ref, lse_ref, kcat_ref):
        kcat_ref[:, 0:LANES] = kv_ref[:, 0:LANES]
        kcat_ref[:, LANES:] = kr_ref[...]
        rows = lax.broadcasted_iota(jnp.int32, (tq, tq), 0)
        cols = lax.broadcasted_iota(jnp.int32, (tq, tq), 1)
        for i in range(nq):
            a = i * tq
            q = q_ref[a:a + tq, :]
            sd = lax.dot_general(q, kcat_ref[a:a + tq, :], NT, preferred_element_type=F32)
            sd = jnp.where(cols <= rows, sd, MASK_VALUE)
            m = jnp.max(sd, axis=1, keepdims=True)
            if i > 0:
                sp = lax.dot_general(q, kcat_ref[0:a, :], NT, preferred_element_type=F32)
                m = jnp.maximum(m, jnp.max(sp, axis=1, keepdims=True))
            pd = jnp.exp2((sd - m) * exp2_scale)
            l = jnp.sum(pd, axis=1, keepdims=True)
            acc = lax.dot_general(pd.astype(BF16), kv_ref[a:a + tq, LANES:], NN, preferred_element_type=F32)
            if i > 0:
                pp = jnp.exp2((sp - m) * exp2_scale)
                l = l + jnp.sum(pp, axis=1, keepdims=True)
                acc = acc + lax.dot_general(pp.astype(BF16), kv_ref[0:a, LANES:], NN, preferred_element_type=F32)
            o = acc / l
            z = z_ref[a:a + tq, :]
            o_ref[a:a + tq, :] = o
            o2_ref[a:a + tq, :] = (o * (z * _sigmoid(z))).astype(BF16)
            lse_ref[a:a + tq, :] = jnp.broadcast_to(m * ATTN_SCALE + jnp.log(l), (tq, LANES))

    hw = N_HEADS * LANES
    head256 = pl.BlockSpec((t, HEAD_PAD), lambda h: (0, h))
    head128 = pl.BlockSpec((t, LANES), lambda h: (0, h))
    return pl.pallas_call(
        body, name="flash_fwd",
        out_shape=(jax.ShapeDtypeStruct((t, hw), F32), jax.ShapeDtypeStruct((t, hw), BF16),
                   jax.ShapeDtypeStruct((t, hw), F32)),
        grid=(N_HEADS,),
        in_specs=[head256, head256, pl.BlockSpec((t, LANES), lambda h: (0, 0), pipeline_mode=pl.Buffered(1)),
                  pl.BlockSpec((t, LANES), lambda h: (0, z_first + h))],
        out_specs=(head128, head128, head128),
        scratch_shapes=[pltpu.VMEM((t, HEAD_PAD), BF16)],
        compiler_params=_cparams(("parallel",)),
    )(qcat, kv, kr, projb)


def _final_ln_loss(h1, o2, w_out, g, b, target):
    t, d = h1.shape
    ku = o2.shape[1]
    tm = min(t, 512)

    def body(h_ref, u_ref, w_ref, g_ref, b_ref, t_ref, loss_ref, dr_ref, drb_ref, dg_ref, db_ref):
        i = pl.program_id(0)
        r = ALPHA * h_ref[...] + lax.dot_general(u_ref[...], w_ref[...], NN, preferred_element_type=F32)
        mu = jnp.mean(r, axis=-1, keepdims=True)
        xc = r - mu
        var = jnp.mean(xc * xc, axis=-1, keepdims=True)
        rstd = lax.rsqrt(var + LN_EPS)
        xh = xc * rstd
        g = g_ref[...]
        diff = xh * g + b_ref[...] - t_ref[...]
        part = 0.5 * jnp.sum(jnp.mean(diff * diff, axis=-1, keepdims=True))
        dh = diff * (1.0 / d)
        dgp = jnp.sum(dh * xh, axis=0, keepdims=True)
        dbp = jnp.sum(dh, axis=0, keepdims=True)

        @pl.when(i == 0)
        def _():
            loss_ref[...] = jnp.zeros_like(loss_ref)
            dg_ref[...] = jnp.zeros_like(dg_ref)
            db_ref[...] = jnp.zeros_like(db_ref)

        loss_ref[...] += jnp.full(loss_ref.shape, part, F32)
        dg_ref[...] += dgp
        db_ref[...] += dbp
        dxh = dh * g
        dr = rstd * (dxh - jnp.mean(dxh, axis=-1, keepdims=True) - xh * jnp.mean(dxh * xh, axis=-1, keepdims=True))
        dr_ref[...] = dr
        drb_ref[...] = dr.astype(BF16)

    row = pl.BlockSpec((tm, d), lambda i: (i, 0))
    par = pl.BlockSpec((1, d), lambda i: (0, 0))
    return pl.pallas_call(
        body, name="final_ln_loss",
        out_shape=(jax.ShapeDtypeStruct((1, LANES), F32), jax.ShapeDtypeStruct((t, d), F32),
                   jax.ShapeDtypeStruct((t, d), BF16), jax.ShapeDtypeStruct((1, d), F32),
                   jax.ShapeDtypeStruct((1, d), F32)),
        grid=(t // tm,),
        in_specs=[row, pl.BlockSpec((tm, ku), lambda i: (i, 0)),
                  pl.BlockSpec((ku, d), lambda i: (0, 0), pipeline_mode=pl.Buffered(1)), par, par, row],
        out_specs=(pl.BlockSpec((1, LANES), lambda i: (0, 0)), row, row, par, par),
        compiler_params=_cparams(("arbitrary",)),
    )(h1, o2, w_out, g, b, target)


def _gate_bwd(dr2b, w_out, o, projb):
    t, hw = o.shape
    d = dr2b.shape[1]
    tm = min(t, 512)
    bw = Q_LORA_RANK
    nb = hw // bw
    wb = projb.shape[1]

    def body(dr_ref, w_ref, o_ref, z_ref, dob_ref, dz_ref):
        z = z_ref[...]
        sg = _sigmoid(z)
        d2 = lax.dot_general(dr_ref[...], w_ref[...], NT, preferred_element_type=F32)
        dob_ref[...] = (d2 * (z * sg)).astype(BF16)
        dz_ref[...] = (d2 * o_ref[...] * (sg * (1.0 + z * (1.0 - sg)))).astype(BF16)

    blk = pl.BlockSpec((tm, bw), lambda i, j: (i, j))
    blk1 = pl.BlockSpec((tm, bw), lambda i, j: (i, j + 1))
    return pl.pallas_call(
        body, name="gate_bwd",
        out_shape=(jax.ShapeDtypeStruct((t, hw), BF16), jax.ShapeDtypeStruct((t, wb), BF16)),
        grid=(t // tm, nb),
        in_specs=[pl.BlockSpec((tm, d), lambda i, j: (i, 0)), pl.BlockSpec((bw, d), lambda i, j: (j, 0)), blk, blk1],
        out_specs=(blk, blk1),
        compiler_params=_cparams(("parallel", "parallel")),
    )(dr2b, w_out, o, projb)


def _flash_bwd(qcat, kv, kr, dob, lse, o, rc, rs):
    t = qcat.shape[0]
    tq = min(t, 512)
    nq = t // tq
    q_chunk = 2 * tq
    log2e = math.log2(math.e)
    exp2_scale = ATTN_SCALE * log2e
    once = pl.Buffered(1)

    def body(q_ref, kv_ref, kr_ref, do_ref, lse_ref, o_ref, c_ref, s_ref,
             dq_ref, dkv_ref, dkr_ref, kcat_ref, dqa_ref, dka_ref, dva_ref, dl_ref):
        h = pl.program_id(0)
        kcat_ref[:, 0:LANES] = kv_ref[:, 0:LANES]
        kcat_ref[:, LANES:] = kr_ref[...]
        dqa_ref[...] = jnp.zeros_like(dqa_ref)
        dl_ref[...] = jnp.sum(do_ref[...].astype(F32) * o_ref[...], axis=1, keepdims=True)

        @pl.when(h == 0)
        def _():
            dkr_ref[...] = jnp.zeros_like(dkr_ref)

        rows = lax.broadcasted_iota(jnp.int32, (tq, tq), 0)
        cols = lax.broadcasted_iota(jnp.int32, (tq, tq), 1)

        def block(qs, n, ks, masked):
            q = q_ref[qs:qs + n, :]
            kc = kcat_ref[ks:ks + tq, :]
            s = lax.dot_general(q, kc, NT, preferred_element_type=F32)
            if masked:
                s = jnp.where(cols <= rows, s, MASK_VALUE)
            p = jnp.exp2(s * exp2_scale - lse_ref[qs:qs + n, 0:1] * log2e)
            do = do_ref[qs:qs + n, :]
            dp = lax.dot_general(do, kv_ref[ks:ks + tq, LANES:], NT, preferred_element_type=F32)
            ds = (p * (dp - dl_ref[qs:qs + n, :])).astype(BF16)
            dva_ref[...] += lax.dot_general(p.astype(BF16), do, TN, preferred_element_type=F32)
            dka_ref[...] += lax.dot_general(ds, q, TN, preferred_element_type=F32)
            dqa_ref[qs:qs + n, :] += lax.dot_general(ds, kc, NN, preferred_element_type=F32)

        for kb in range(nq):
            a = kb * tq
            dka_ref[...] = jnp.zeros_like(dka_ref)
            dva_ref[...] = jnp.zeros_like(dva_ref)
            block(a, tq, a, True)
            qs = a + tq
            while qs < t:
                n = min(q_chunk, t - qs)
                block(qs, n, a, False)
                qs += n
            dk = dka_ref[...] * ATTN_SCALE
            dkv_ref[a:a + tq, 0:LANES] = dk[:, 0:LANES].astype(BF16)
            dkv_ref[a:a + tq, LANES:] = dva_ref[...].astype(BF16)
            dkr_ref[a:a + tq, :] += dk[:, LANES:]

        dq = dqa_ref[...] * ATTN_SCALE
        dq_ref[:, 0:LANES] = dq[:, 0:LANES].astype(BF16)
        d2 = dq[:, LANES:]
        dq_ref[:, LANES:] = (d2 * c_ref[...] - _swap_rope_halves(d2) * s_ref[...]).astype(BF16)

    hp = N_HEADS * HEAD_PAD
    head256 = pl.BlockSpec((t, HEAD_PAD), lambda h: (0, h))
    head128 = pl.BlockSpec((t, LANES), lambda h: (0, h))
    const128 = pl.BlockSpec((t, LANES), lambda h: (0, 0), pipeline_mode=once)
    return pl.pallas_call(
        body, name="flash_bwd",
        out_shape=(jax.ShapeDtypeStruct((t, hp), BF16), jax.ShapeDtypeStruct((t, hp), BF16),
                   jax.ShapeDtypeStruct((t, LANES), F32)),
        grid=(N_HEADS,),
        in_specs=[head256, head256, const128, head128, head128, head128, const128, const128],
        out_specs=(head256, head256, pl.BlockSpec((t, LANES), lambda h: (0, 0))),
        scratch_shapes=[pltpu.VMEM((t, HEAD_PAD), BF16), pltpu.VMEM((t, HEAD_PAD), F32),
                        pltpu.VMEM((tq, HEAD_PAD), F32), pltpu.VMEM((tq, LANES), F32), pltpu.VMEM((t, 1), F32)],
        compiler_params=_cparams(("arbitrary",)),
    )(qcat, kv, kr, dob, lse, o, rc, rs)


def _q_norm_bwd(dq2, w_uq, projb, g, dprojb):
    t = projb.shape[0]
    kq = dq2.shape[1]
    tm = min(t, 512)
    r = Q_LORA_RANK

    def body(dq_ref, w_ref, c_ref, g_ref, alias_ref, o_ref, dg_ref):
        i = pl.program_id(0)
        c = c_ref[...]
        d = lax.dot_general(dq_ref[...], w_ref[...], NT, preferred_element_type=F32)
        rr = lax.rsqrt(jnp.mean(c * c, axis=-1, keepdims=True) + RMS_EPS)
        xh = c * rr

        @pl.when(i == 0)
        def _():
            dg_ref[...] = jnp.zeros_like(dg_ref)

        dg_ref[...] += jnp.sum(d * xh, axis=0, keepdims=True)
        dxh = d * g_ref[...]
        o_ref[...] = (rr * (dxh - xh * jnp.mean(dxh * xh, axis=-1, keepdims=True))).astype(BF16)

    blk = pl.BlockSpec((tm, r), lambda i: (i, 0))
    par = pl.BlockSpec((1, r), lambda i: (0, 0))
    return pl.pallas_call(
        body, name="q_norm_bwd",
        out_shape=(jax.ShapeDtypeStruct(dprojb.shape, BF16), jax.ShapeDtypeStruct((1, r), F32)),
        grid=(t // tm,),
        in_specs=[pl.BlockSpec((tm, kq), lambda i: (i, 0)),
                  pl.BlockSpec((r, kq), lambda i: (0, 0), pipeline_mode=pl.Buffered(1)),
                  blk, par, pl.BlockSpec(memory_space=pl.ANY)],
        out_specs=(blk, par),
        input_output_aliases={4: 0},
        compiler_params=_cparams(("arbitrary",)),
    )(dq2, w_uq, projb, g, dprojb)


def _kv_mid_bwd(dkv, w_ukv, ckv, dkr, g, rc, rs):
    t = ckv.shape[0]
    kk = dkv.shape[1]
    tm = min(t, 512)
    r = KV_LORA_RANK

    def body(dkv_ref, w_ref, ckv_ref, dkr_ref, g_ref, c_ref, s_ref, o_ref, dg_ref):
        i = pl.program_id(0)
        c = ckv_ref[:, 0:r]
        d = lax.dot_general(dkv_ref[...], w_ref[...], NT, preferred_element_type=F32)
        rr = lax.rsqrt(jnp.mean(c * c, axis=-1, keepdims=True) + RMS_EPS)
        xh = c * rr

        @pl.when(i == 0)
        def _():
            dg_ref[...] = jnp.zeros_like(dg_ref)

        dg_ref[...] += jnp.sum(d * xh, axis=0, keepdims=True)
        dxh = d * g_ref[...]
        o_ref[:, 0:r] = (rr * (dxh - xh * jnp.mean(dxh * xh, axis=-1, keepdims=True))).astype(BF16)
        dk = dkr_ref[...]
        o_ref[:, r:] = (dk * c_ref[...] - _swap_rope_halves(dk) * s_ref[...]).astype(BF16)

    return pl.pallas_call(
        body, name="kv_mid_bwd",
        out_shape=(jax.ShapeDtypeStruct((t, r + LANES), BF16), jax.ShapeDtypeStruct((1, r), F32)),
        grid=(t // tm,),
        in_specs=[pl.BlockSpec((tm, kk), lambda i: (i, 0)),
                  pl.BlockSpec((r, kk), lambda i: (0, 0), pipeline_mode=pl.Buffered(1)),
                  pl.BlockSpec((tm, r + LANES), lambda i: (i, 0)),
                  pl.BlockSpec((tm, LANES), lambda i: (i, 0)), pl.BlockSpec((1, r), lambda i: (0, 0)),
                  pl.BlockSpec((tm, LANES), lambda i: (i, 0)), pl.BlockSpec((tm, LANES), lambda i: (i, 0))],
        out_specs=(pl.BlockSpec((tm, r + LANES), lambda i: (i, 0)), pl.BlockSpec((1, r), lambda i: (0, 0))),
        compiler_params=_cparams(("arbitrary",)),
    )(dkv, w_ukv, ckv, dkr, g, rc, rs)


def _ln0_bwd(dr2, dprojb, w_in, dckv, w_down, xh, rstd, g, after):
    t, d = dr2.shape
    kb, kd = dprojb.shape[1], dckv.shape[1]
    tm = min(t, 256)
    extra = [] if after is None else [after]

    def body(a_ref, pb_ref, wb_ref, pd_ref, wd_ref, xh_ref, rs_ref, g_ref, *rest):
        dr_ref, drb_ref, dg_ref, db_ref, dsum_ref = rest[-5:]
        i = pl.program_id(0)
        dh = (ALPHA * a_ref[...] + lax.dot_general(pb_ref[...], wb_ref[...], NT, preferred_element_type=F32)
              + lax.dot_general(pd_ref[...], wd_ref[...], NT, preferred_element_type=F32))
        xh = xh_ref[...]

        @pl.when(i == 0)
        def _():
            dg_ref[...] = jnp.zeros_like(dg_ref)
            db_ref[...] = jnp.zeros_like(db_ref)
            dsum_ref[...] = jnp.zeros_like(dsum_ref)

        dg_ref[...] += jnp.sum(dh * xh, axis=0, keepdims=True)
        db_ref[...] += jnp.sum(dh, axis=0, keepdims=True)
        dxh = dh * g_ref[...]
        dr = rs_ref[...] * (dxh - jnp.mean(dxh, axis=-1, keepdims=True) - xh * jnp.mean(dxh * xh, axis=-1, keepdims=True))
        dr_ref[...] = dr
        drb_ref[...] = dr.astype(BF16)
        dsum_ref[...] += jnp.sum(dr, axis=0, keepdims=True)

    row = pl.BlockSpec((tm, d), lambda i: (i, 0))
    par = pl.BlockSpec((1, d), lambda i: (0, 0))
    return pl.pallas_call(
        body, name="ln0_bwd",
        out_shape=(jax.ShapeDtypeStruct((t, d), F32), jax.ShapeDtypeStruct((t, d), BF16),
                   jax.ShapeDtypeStruct((1, d), F32), jax.ShapeDtypeStruct((1, d), F32),
                   jax.ShapeDtypeStruct((1, d), F32)),
        grid=(t // tm,),
        in_specs=[row, pl.BlockSpec((tm, kb), lambda i: (i, 0)),
                  pl.BlockSpec((d, kb), lambda i: (0, 0), pipeline_mode=pl.Buffered(1)),
                  pl.BlockSpec((tm, kd), lambda i: (i, 0)),
                  pl.BlockSpec((d, kd), lambda i: (0, 0), pipeline_mode=pl.Buffered(1)),
                  row, pl.BlockSpec((tm, 1), lambda i: (i, 0)), par]
                 + [pl.BlockSpec(memory_space=pl.ANY)] * len(extra),
        out_specs=(row, row, par, par, par),
        compiler_params=_cparams(("arbitrary",)),
    )(dr2, dprojb, w_in, dckv, w_down, xh, rstd, g, *extra)


def _conv_mid_bwd_rows(u1, proj, dr1b, w_out, ng, nb, after):
    t = u1.shape[0]
    e = CONV_WIDTH
    d = dr1b.shape[1]
    tm = min(t, 256)
    extra = [] if after is None else [after]

    rg = 16
    nt = t // tm

    def body(u1_ref, z_ref, dr_ref, w_ref, ng_ref, nb_ref, *rest):
        du1_ref, dz_ref, dng_ref, dnb_ref, dbz_ref, acc_ref, d_ref = rest[-7:]
        i = pl.program_id(0)

        @pl.when(i == 0)
        def _():
            acc_ref[...] = jnp.zeros_like(acc_ref)

        d_ref[...] = lax.dot_general(dr_ref[...], w_ref[...], NT, preferred_element_type=F32)
        g = ng_ref[...]
        b = nb_ref[...]

        def group(r, carry):
            rows = pl.ds(pl.multiple_of(r * rg, rg), rg)
            u1 = u1_ref[rows, :]
            mu = jnp.mean(u1, axis=-1, keepdims=True)
            xc = u1 - mu
            rstd = lax.rsqrt(jnp.mean(xc * xc, axis=-1, keepdims=True) + LN_EPS)
            xh = xc * rstd
            n = xh * g + b
            sn = _sigmoid(n)
            z = z_ref[rows, :]
            sz = _sigmoid(z)
            du2 = d_ref[rows, :]
            dz = du2 * (n * sn) * (sz * (1.0 + z * (1.0 - sz)))
            dn = du2 * (z * sz) * (sn * (1.0 + n * (1.0 - sn)))
            acc_ref[0:8, :] += _fold_rows(dn * xh)
            acc_ref[8:16, :] += _fold_rows(dn)
            acc_ref[16:24, :] += _fold_rows(dz)
            dz_ref[rows, :] = dz.astype(BF16)
            dxh = dn * g
            du1_ref[rows, :] = rstd * (dxh - jnp.mean(dxh, axis=-1, keepdims=True)
                                       - xh * jnp.mean(dxh * xh, axis=-1, keepdims=True))
            return carry

        lax.fori_loop(0, tm // rg, group, 0, unroll=8)

        @pl.when(i == nt - 1)
        def _():
            dng_ref[...] = jnp.sum(acc_ref[0:8, :], axis=0, keepdims=True)
            dnb_ref[...] = jnp.sum(acc_ref[8:16, :], axis=0, keepdims=True)
            dbz_ref[...] = jnp.sum(acc_ref[16:24, :], axis=0, keepdims=True)

    row = pl.BlockSpec((tm, e), lambda i: (i, 0))
    par = pl.BlockSpec((1, e), lambda i: (0, 0))
    return pl.pallas_call(
        body, name="conv_mid_bwd_rows",
        out_shape=(jax.ShapeDtypeStruct((t, e), F32), jax.ShapeDtypeStruct((t, 3 * e), BF16),
                   jax.ShapeDtypeStruct((1, e), F32), jax.ShapeDtypeStruct((1, e), F32),
                   jax.ShapeDtypeStruct((1, e), F32)),
        grid=(nt,),
        in_specs=[row, pl.BlockSpec((tm, e), lambda i: (i, 2)), pl.BlockSpec((tm, d), lambda i: (i, 0)),
                  pl.BlockSpec((e, d), lambda i: (0, 0), pipeline_mode=pl.Buffered(1)), par, par]
                 + [pl.BlockSpec(memory_space=pl.ANY)] * len(extra),
        out_specs=(row, pl.BlockSpec((tm, e), lambda i: (i, 2)), par, par, par),
        scratch_shapes=[pltpu.VMEM((24, e), F32), pltpu.VMEM((tm, e), F32)],
        compiler_params=_cparams(("arbitrary",)),
    )(u1, proj, dr1b, w_out, ng, nb, *extra)


def _conv_bwd(du1, proj, cw, dproj):
    t = du1.shape[0]
    e = CONV_WIDTH
    tm = min(t, 128)
    hb = tm // HALO
    nt = t // tm
    nchunk = e // LANES
    last_halo = t // HALO - 1

    def body(d_ref, dn_ref, val_ref, gg_ref, valh_ref, ggh_ref, cw_ref, alias_ref,
             dp_ref, dcw_ref, dcb_ref, dbv_ref, extu_ref, extd_ref, du0_ref, acc_ref, sh_ref):
        i = pl.program_id(0)
        h0 = valh_ref[...] * _sigmoid(ggh_ref[...])
        extu_ref[0:HALO, :] = jnp.where(i > 0, h0, 0.0)
        extu_ref[HALO:, :] = val_ref[...] * _sigmoid(gg_ref[...])
        extd_ref[0:tm, :] = d_ref[...]
        extd_ref[tm:, :] = jnp.where(i < nt - 1, dn_ref[...], 0.0)

        @pl.when(i == 0)
        def _():
            acc_ref[...] = jnp.zeros_like(acc_ref)

        def chunk(c, carry):
            col = pl.multiple_of(c * LANES, LANES)
            d = extd_ref[0:tm, pl.ds(col, LANES)]
            du0 = jnp.zeros((tm, LANES), F32)
            for res, taps in _taps_by_residue(lambda j: CONV_KERNEL - 1 - j):
                span = 8 * max(a for _, a in taps) + tm
                sh_ref[0:span, :] = extd_ref[res:res + span, pl.ds(col, LANES)]
                for j, a in taps:
                    du0 = du0 + cw_ref[j:j + 1, pl.ds(col, LANES)] * sh_ref[8 * a:8 * a + tm, :]
            for res, taps in _taps_by_residue(lambda j: HALO - (CONV_KERNEL - 1) + j):
                span = 8 * max(a for _, a in taps) + tm
                sh_ref[0:span, :] = extu_ref[res:res + span, pl.ds(col, LANES)]
                for j, a in taps:
                    prod = d * sh_ref[8 * a:8 * a + tm, :]
                    acc_ref[j * 8:(j + 1) * 8, pl.ds(col, LANES)] += _fold_rows(prod)
            acc_ref[CONV_KERNEL * 8:(CONV_KERNEL + 1) * 8, pl.ds(col, LANES)] += _fold_rows(d)
            du0_ref[:, pl.ds(col, LANES)] = du0
            return carry

        lax.fori_loop(0, nchunk, chunk, 0)
        kb = CONV_KERNEL * 8
        for r in range(0, tm, 16):
            sg = _sigmoid(gg_ref[r:r + 16, :])
            dval = du0_ref[r:r + 16, :] * sg
            dgg = dval * val_ref[r:r + 16, :] * (1.0 - sg)
            dp_ref[r:r + 16, 0:e] = dval.astype(BF16)
            dp_ref[r:r + 16, e:] = dgg.astype(BF16)
            acc_ref[kb + 8:kb + 16, :] += _fold_rows(dval)
            acc_ref[kb + 16:kb + 24, :] += _fold_rows(dgg)

        @pl.when(i == nt - 1)
        def _():
            for j in range(CONV_KERNEL):
                dcw_ref[j:j + 1, :] = jnp.sum(acc_ref[j * 8:(j + 1) * 8, :], axis=0, keepdims=True)
            dcw_ref[CONV_KERNEL:, :] = jnp.zeros((HALO - CONV_KERNEL, e), F32)
            dcb_ref[...] = jnp.sum(acc_ref[kb:kb + 8, :], axis=0, keepdims=True)
            dbv_ref[:, 0:e] = jnp.sum(acc_ref[kb + 8:kb + 16, :], axis=0, keepdims=True)
            dbv_ref[:, e:] = jnp.sum(acc_ref[kb + 16:kb + 24, :], axis=0, keepdims=True)

    row = lambda c: pl.BlockSpec((tm, e), lambda i: (i, c))
    halo_prev = lambda c: pl.BlockSpec((HALO, e), lambda i: (jnp.maximum(i * hb - 1, 0), c))
    halo_next = pl.BlockSpec((HALO, e), lambda i: (jnp.minimum((i + 1) * hb, last_halo), 0))
    return pl.pallas_call(
        body, name="conv_bwd",
        out_shape=(jax.ShapeDtypeStruct(dproj.shape, BF16), jax.ShapeDtypeStruct((HALO, e), F32),
                   jax.ShapeDtypeStruct((1, e), F32), jax.ShapeDtypeStruct((1, 2 * e), F32)),
        grid=(nt,),
        in_specs=[row(0), halo_next, row(0), row(1), halo_prev(0), halo_prev(1),
                  pl.BlockSpec((HALO, e), lambda i: (0, 0)), pl.BlockSpec(memory_space=pl.ANY)],
        out_specs=(pl.BlockSpec((tm, 2 * e), lambda i: (i, 0)), pl.BlockSpec((HALO, e), lambda i: (0, 0)),
                   pl.BlockSpec((1, e), lambda i: (0, 0)), pl.BlockSpec((1, 2 * e), lambda i: (0, 0))),
        scratch_shapes=[pltpu.VMEM((tm + HALO, e), F32), pltpu.VMEM((tm + HALO, e), F32),
                        pltpu.VMEM((tm, e), F32), pltpu.VMEM(((CONV_KERNEL + 3) * 8, e), F32),
                        pltpu.VMEM((tm + HALO, LANES), F32)],
        input_output_aliases={7: 0},
        compiler_params=_cparams(("arbitrary",)),
    )(du1, du1, proj, proj, proj, proj, cw, dproj)


def _adam_update(g, w, m, v):
    c1 = 1.0 - ADAM_B1 ** ADAM_STEP
    c2 = 1.0 - ADAM_B2 ** ADAM_STEP
    mn = ADAM_B1 * m + (1.0 - ADAM_B1) * g
    vn = ADAM_B2 * v + (1.0 - ADAM_B2) * (g * g)
    delta = -ADAM_LR * ((mn / c1) / (jnp.sqrt(vn / c2) + ADAM_EPS) + ADAM_WD * w)
    return delta, mn, vn


def _adamw(name, gbuf, w, m, v):
    parts = list(gbuf) if isinstance(gbuf, (list, tuple)) else [gbuf]
    npart = len(parts)
    r, c = w.shape
    tr = min(r // npart, 256)
    per_part = r // npart // tr
    assert r == npart * per_part * tr and all(p.shape == (N_DEV, r // npart, c) for p in parts)

    def body(*refs):
        g_refs = refs[:npart]
        w_ref, m_ref, v_ref, go_ref, d_ref, mo_ref, vo_ref = refs[npart:]
        i = pl.program_id(0)

        def run(g_ref):
            g = g_ref[0].astype(F32)
            for k in range(1, N_DEV):
                g = g + g_ref[k].astype(F32)
            go_ref[...] = g
            d_ref[...], mo_ref[...], vo_ref[...] = _adam_update(g, w_ref[...], m_ref[...], v_ref[...])

        if npart == 1:
            run(g_refs[0])
        else:
            for q in range(npart):
                pl.when((i >= q * per_part) & (i < (q + 1) * per_part))(functools.partial(run, g_refs[q]))

    blk = pl.BlockSpec((tr, c), lambda i: (i, 0))
    part_spec = lambda q: pl.BlockSpec((N_DEV, tr, c), lambda i: (0, jnp.clip(i - q * per_part, 0, per_part - 1), 0))
    shp = jax.ShapeDtypeStruct((r, c), F32)
    return pl.pallas_call(
        body, name="adamw_" + name,
        out_shape=(shp, shp, shp, shp),
        grid=(r // tr,),
        in_specs=[part_spec(q) for q in range(npart)] + [blk, blk, blk],
        out_specs=(blk, blk, blk, blk),
        compiler_params=_cparams(("parallel",)),
    )(*parts, w, m, v)


SMALL_TABLE_COLS = 256
SMALL_LAYOUT = {
    'ln_g': (0, 2, 1024), 'ln_b': (8, 2, 1024), 'a_b_in': (16, 1, 768), 'a_conv_b': (24, 1, 256),
    'a_norm_g': (32, 1, 256), 'a_norm_b': (40, 1, 256), 'a_b_out': (48, 1, 128), 'kv_norm_g': (56, 1, 256),
    'b_q_norm_g': (64, 1, 512), 'a_conv_w': (72, CONV_KERNEL, 256),
}
LOSS_ROW = 104
SMALL_TABLE_ROWS = 112
SMALL_NAMES = tuple(SMALL_LAYOUT)


def _small_pieces(name, row):
    r0, _, c = SMALL_LAYOUT[name]
    w = min(c, SMALL_TABLE_COLS)
    per_row = c // w
    return [(r0 + row * per_row + q, q * w, w) for q in range(per_row)]


def _pack_small_grads(dg0, dg1, db0, db1, dbv, dbz, dcb, dng, dnb, dba, dgkv, dgq, dcw, loss):
    e = CONV_WIDTH
    ce = e // N_DEV

    def body(dg0_ref, dg1_ref, db0_ref, db1_ref, dbv_ref, dbz_ref, dcb_ref, dng_ref, dnb_ref, dba_ref, dgkv_ref,
             dgq_ref, dcw_ref, loss_ref, o_ref):
        o_ref[...] = jnp.zeros_like(o_ref)
        for d in range(N_DEV):
            o_ref[d, LOSS_ROW:LOSS_ROW + 1, 0:LANES] = loss_ref[...]

        def put(d, name, row, src_ref, first_col=0):
            for trow, col, w in _small_pieces(name, row):
                o_ref[d, trow:trow + 1, 0:w] = src_ref[:, first_col + col:first_col + col + w]

        for d in range(N_DEV):
            put(d, 'ln_g', 0, dg0_ref)
            put(d, 'ln_g', 1, dg1_ref)
            put(d, 'ln_b', 0, db0_ref)
            put(d, 'ln_b', 1, db1_ref)
            for trow, col, w in _small_pieces('a_b_in', 0):
                gcol = d * 3 * ce + col
                src = dbv_ref[:, gcol:gcol + w] if gcol < 2 * e else dbz_ref[:, gcol - 2 * e:gcol - 2 * e + w]
                o_ref[d, trow:trow + 1, 0:w] = src
            put(d, 'a_conv_b', 0, dcb_ref, d * ce)
            put(d, 'a_norm_g', 0, dng_ref, d * ce)
            put(d, 'a_norm_b', 0, dnb_ref, d * ce)
            put(d, 'a_b_out', 0, dba_ref, d * LANES)
            put(d, 'kv_norm_g', 0, dgkv_ref)
            put(d, 'b_q_norm_g', 0, dgq_ref)
            r0 = SMALL_LAYOUT['a_conv_w'][0]
            o_ref[d, r0:r0 + HALO, 0:ce] = dcw_ref[:, d * ce:(d + 1) * ce]

    return pl.pallas_call(
        body, name="pack_small_grads",
        out_shape=jax.ShapeDtypeStruct((N_DEV, SMALL_TABLE_ROWS, SMALL_TABLE_COLS), F32),
        compiler_params=pltpu.CompilerParams(vmem_limit_bytes=VMEM_LIMIT),
    )(dg0, dg1, db0, db1, dbv, dbz, dcb, dng, dnb, dba, dgkv, dgq, dcw, loss)


def _adamw_small(gtab, ws, ms, vs):
    n = len(SMALL_NAMES)

    def body(*refs):
        g_ref = refs[0]
        w_refs, m_refs, v_refs = refs[1:1 + n], refs[1 + n:1 + 2 * n], refs[1 + 2 * n:1 + 3 * n]
        outs = refs[1 + 3 * n:]

        def summed(r0, r, c):
            g = g_ref[0, r0:r0 + r, 0:c]
            for k in range(1, N_DEV):
                g = g + g_ref[k, r0:r0 + r, 0:c]
            return g

        for i, name in enumerate(SMALL_NAMES):
            r0, r, c = SMALL_LAYOUT[name]
            if c <= SMALL_TABLE_COLS:
                blocks = [(slice(None), summed(r0, r, c))]
            else:
                blocks = [(slice(row, row + 1), jnp.concatenate([summed(tr, 1, w) for tr, _, w in _small_pieces(name, row)], axis=1))
                          for row in range(r)]
            for rows, g in blocks:
                delta, mn, vn = _adam_update(g, w_refs[i][rows, :], m_refs[i][rows, :], v_refs[i][rows, :])
                outs[i][rows, :] = g
                outs[n + i][rows, :] = delta
                outs[2 * n + i][rows, :] = mn
                outs[3 * n + i][rows, :] = vn
        outs[4 * n][...] = summed(LOSS_ROW, 1, LANES)

    shapes = tuple(jax.ShapeDtypeStruct(w.shape, F32) for w in ws)
    res = pl.pallas_call(
        body, name="adamw_small",
        out_shape=shapes * 4 + (jax.ShapeDtypeStruct((1, LANES), F32),),
        compiler_params=pltpu.CompilerParams(vmem_limit_bytes=VMEM_LIMIT),
    )(gtab, *ws, *ms, *vs)
    return res[:n], res[n:2 * n], res[2 * n:3 * n], res[3 * n:4 * n], res[4 * n]


def _mesh_peers():
    x, y, c = lax.axis_index("x"), lax.axis_index("y"), lax.axis_index("c")
    me = 4 * x + 2 * y + c
    peers = []
    for k in range(1, N_DEV):
        px = 1 - x if (k >> 2) & 1 else x
        py = 1 - y if (k >> 1) & 1 else y
        pc = 1 - c if k & 1 else c
        peers.append(((px, py, pc), 4 * px + 2 * py + pc))
    return me, peers


_HBM_SPEC = pl.BlockSpec(memory_space=pltpu.HBM)
_SEM_SPEC = pl.BlockSpec(memory_space=pltpu.SEMAPHORE)


ALL_PEERS = tuple(range(N_DEV - 1))
CHIP_LEVEL_PEERS = (0, 1, 3, 5)


def _split_copies(srcs, lands, send_sems, recv_sems, local_sems, gather, which=ALL_PEERS):
    me, peers = _mesh_peers()
    na = len(srcs)
    mine = lambda a, slot: srcs[a] if gather else srcs[a].at[slot]
    local = [pltpu.make_async_copy(mine(a, me), lands[a].at[me], local_sems.at[a]) for a in range(na)]
    sent, received = [], []
    for k, (dev, pid) in enumerate(peers):
        if k not in which:
            continue
        for a in range(na):
            s = a * (N_DEV - 1) + k
            sent.append(pltpu.make_async_remote_copy(
                src_ref=mine(a, pid), dst_ref=lands[a].at[me], send_sem=send_sems.at[s],
                recv_sem=recv_sems.at[s], device_id=dev, device_id_type=pl.DeviceIdType.MESH))
            received.append(pltpu.make_async_remote_copy(
                src_ref=mine(a, pid), dst_ref=lands[a].at[pid], send_sem=send_sems.at[s],
                recv_sem=recv_sems.at[s], device_id=dev, device_id_type=pl.DeviceIdType.MESH))
    return local, sent, received


def _comm_start(name, srcs, gather, after, which=ALL_PEERS):
    na = len(srcs)
    land_structs = [jax.ShapeDtypeStruct(((N_DEV,) + s.shape) if gather else s.shape, s.dtype) for s in srcs]
    extra = [] if after is None else [after]
    n_in = 2 * na + len(extra)

    def body(*refs):
        srcs_r, lands_r = refs[:na], refs[na:2 * na]
        send_sems, recv_sems, local_sems = refs[n_in:n_in + 3]
        token = refs[-1]
        local, sent, _ = _split_copies(srcs_r, lands_r, send_sems, recv_sems, local_sems, gather, which)
        for cp in local + sent:
            cp.start()
        token[...] = jnp.zeros_like(token)

    sem = pltpu.SemaphoreType.DMA((na * (N_DEV - 1),))
    outs = pl.pallas_call(
        body, name=name,
        out_shape=(sem, sem, pltpu.SemaphoreType.DMA((na,)),
                   *[pltpu.HBM(s.shape, s.dtype) for s in srcs],
                   *[pltpu.HBM(s.shape, s.dtype) for s in land_structs],
                   jax.ShapeDtypeStruct((8, LANES), F32)),
        in_specs=[_HBM_SPEC] * (2 * na) + [pl.BlockSpec(memory_space=pl.ANY)] * len(extra),
        out_specs=(_SEM_SPEC, _SEM_SPEC, _SEM_SPEC, *([_HBM_SPEC] * (2 * na)), pl.BlockSpec(memory_space=pltpu.VMEM)),
        input_output_aliases={i: 3 + i for i in range(2 * na)},
        compiler_params=pltpu.CompilerParams(has_side_effects=pltpu.SideEffectType.DATAFLOW_SIDE_EFFECTING),
    )(*[pltpu.with_memory_space_constraint(s, pltpu.HBM) for s in srcs],
      *[pltpu.with_memory_space_constraint(lax.empty(s.shape, s.dtype), pltpu.HBM) for s in land_structs],
      *extra)
    return (outs[:3], outs[3:3 + na], outs[3 + na:3 + 2 * na]), outs[-1]


def _comm_wait(name, handles, gather, after, which=ALL_PEERS):
    (send_sems, recv_sems, local_sems), srcs, lands = handles
    na = len(srcs)

    def body(*refs):
        srcs_r, lands_r = refs[:na], refs[na:2 * na]
        send_r, recv_r, local_r = refs[2 * na:2 * na + 3]
        local, sent, received = _split_copies(srcs_r, lands_r, send_r, recv_r, local_r, gather, which)
        for cp in sent:
            cp.wait_send()
        for cp in received:
            cp.wait_recv()
        for cp in local:
            cp.wait()

    outs = pl.pallas_call(
        body, name=name,
        out_shape=(*[pltpu.HBM(s.shape, s.dtype) for s in srcs], *[pltpu.HBM(s.shape, s.dtype) for s in lands]),
        in_specs=[_HBM_SPEC] * (2 * na) + [_SEM_SPEC] * 3 + [pl.BlockSpec(memory_space=pl.ANY)],
        out_specs=tuple([_HBM_SPEC] * (2 * na)),
        input_output_aliases={i: i for i in range(2 * na)},
        compiler_params=pltpu.CompilerParams(has_side_effects=pltpu.SideEffectType.DATAFLOW_SIDE_EFFECTING),
    )(*srcs, *lands, send_sems, recv_sems, local_sems, after)
    return outs[na:]


def _forward_to_sibling(name, lands):
    na = len(lands)

    def body(*refs):
        ins, outs = refs[:na], refs[na:2 * na]
        send_sems, recv_sems = refs[2 * na:]
        x, y, c = lax.axis_index("x"), lax.axis_index("y"), lax.axis_index("c")
        sibling = (x, y, 1 - c)
        chips = [(1 - x, y), (x, 1 - y), (1 - x, 1 - y)]
        slot = lambda cx, cy, cc: 4 * cx + 2 * cy + cc
        sends = []
        for j, (cx, cy) in enumerate(chips):
            for a in range(na):
                cp = pltpu.make_async_remote_copy(
                    src_ref=ins[a].at[slot(cx, cy, c)], dst_ref=outs[a].at[slot(cx, cy, c)],
                    send_sem=send_sems.at[a, j], recv_sem=recv_sems.at[a, j], device_id=sibling,
                    device_id_type=pl.DeviceIdType.MESH)
                cp.start()
                sends.append(cp)
        for j, (cx, cy) in enumerate(chips):
            for a in range(na):
                pltpu.make_async_remote_copy(
                    src_ref=ins[a].at[slot(cx, cy, 1 - c)], dst_ref=outs[a].at[slot(cx, cy, 1 - c)],
                    send_sem=send_sems.at[a, j], recv_sem=recv_sems.at[a, j], device_id=sibling,
                    device_id_type=pl.DeviceIdType.MESH).wait_recv()
        for cp in sends:
            cp.wait_send()

    hbm = pl.BlockSpec(memory_space=pl.ANY)
    return pl.pallas_call(
        body, name=name,
        out_shape=tuple(jax.ShapeDtypeStruct(s.shape, s.dtype) for s in lands),
        in_specs=[hbm] * na, out_specs=(hbm,) * na,
        input_output_aliases={a: a for a in range(na)},
        scratch_shapes=[pltpu.SemaphoreType.DMA((na, 3)), pltpu.SemaphoreType.DMA((na, 3))],
    )(*lands)


def _prep_weights(w):
    e = CONV_WIDTH
    p = {}
    if 'a_w_in' in w:
        if w['a_w_in'].shape == (N_DEV, D_MODEL, 3 * e // N_DEV):
            p['a_w_in3'] = w['a_w_in']
        else:
            p['a_w_in3'] = w['a_w_in'].reshape(D_MODEL, N_DEV, 3 * e // N_DEV).transpose(1, 0, 2)
        p['a_b_in'] = w['a_b_in'].reshape(1, 3 * e)
        p['a_conv_w'] = jnp.pad(w['a_conv_w'].reshape(CONV_KERNEL, e), ((0, HALO - CONV_KERNEL), (0, 0)))
        p['a_conv_b'] = w['a_conv_b'].reshape(1, e)
        p['a_norm_g'] = w['a_norm_g'].reshape(1, e)
        p['a_norm_b'] = w['a_norm_b'].reshape(1, e)
        p['a_b_out'] = w['a_b_out'].reshape(1, D_MODEL)
        p['kv_norm_g'] = w['kv_norm_g'].reshape(1, KV_LORA_RANK)
        p['b_q_norm_g'] = w['b_q_norm_g'].reshape(1, Q_LORA_RANK)
        p['ln_g'] = w['ln_g']
        p['ln_b'] = w['ln_b']
    if 'a_w_out' in w:
        p['a_w_out'] = w['a_w_out'].reshape(e, D_MODEL)
        p['kv_w_down'] = jnp.pad(w['kv_w_down'], ((0, 0), (0, KV_LORA_RANK + LANES - w['kv_w_down'].shape[1])))
        p['kv_w_ukv'] = jnp.concatenate([w['kv_w_uk'], w['kv_w_uv']], axis=2).reshape(KV_LORA_RANK, N_HEADS * HEAD_PAD)
        p['b_w_in'] = w['b_w_in'].reshape(D_MODEL, -1)
        uq = w['b_w_uq'].reshape(Q_LORA_RANK, N_HEADS, QK_NOPE_DIM + QK_ROPE_DIM)
        p['b_w_uq'] = jnp.pad(uq, ((0, 0), (0, 0), (0, HEAD_PAD - uq.shape[2]))).reshape(Q_LORA_RANK, N_HEADS * HEAD_PAD)
        p['b_w_out'] = w['b_w_out'].reshape(N_HEADS * V_HEAD_DIM, D_MODEL)
    return p


def _local_step(x, positions, target, comm):
    t = x.shape[0]
    e = CONV_WIDTH
    freqs = ROPE_THETA ** (-jnp.arange(0, QK_ROPE_DIM, 2, dtype=F32) / QK_ROPE_DIM)
    freq_row = jnp.concatenate([freqs, freqs, jnp.zeros((LANES - QK_ROPE_DIM,), F32)]).reshape(1, LANES)
    rc, rs, xb = _rope_tables(positions.reshape(t, 1), freq_row, x, comm.start_token())
    p = comm.first_weights(rc)
    ln_g0, ln_b0 = p['ln_g'][0:1], p['ln_b'][0:1]
    ln_g1, ln_b1 = p['ln_g'][1:2], p['ln_b'][1:2]

    proj = _matmul(xb, p['a_w_in3'], bias=p['a_b_in'], name="a_in", b_blocked=True, bm=2048,
                   after=comm.first_after())
    u1, u2 = _conv_mid_fwd(proj, p['a_conv_w'], p['a_conv_b'], p['a_norm_g'], p['a_norm_b'])
    p = {**p, **comm.rest_weights(u2)}
    h1, h1b, xh1, rstd1 = _ln_res_fwd(x, u2, p['a_w_out'], p['a_b_out'], ln_g0, ln_b0)
    ckv, ckn, kr, kv = _kv_mid_fwd(h1b, p['kv_w_down'], p['kv_w_ukv'], p['kv_norm_g'], rc, rs)
    projb, cqn = _b_in_qnorm(h1b, p['b_w_in'], p['b_q_norm_g'])
    qcat = _q_up_rope(cqn, p['b_w_uq'], rc, rs)
    o, o2, lse = _flash_fwd(qcat, kv, kr, projb)
    loss, dr2, dr2b, dg1, db1 = _final_ln_loss(h1, o2, p['b_w_out'], ln_g1, ln_b1, target)

    g = {}
    g['b_w_out'] = _matmul(o2, dr2b, trans_a=True, name="d_b_out", out_dtype=BF16, bm=512, bk=t)
    dob, dprojb = _gate_bwd(dr2b, p['b_w_out'], o, projb)
    dq2, dkv, dkr = _flash_bwd(qcat, kv, kr, dob, lse, o, rc, rs)
    duq = _matmul(cqn, dq2, trans_a=True, name="d_q_up", out_dtype=BF16, bk=t)
    dprojb, dgq = _q_norm_bwd(dq2, p['b_w_uq'], projb, p['b_q_norm_g'], dprojb)
    g['b_w_in'] = _matmul(h1b, dprojb, trans_a=True, name="d_b_in", bn=640, bk=t, out_dtype=BF16)
    dukv = _matmul(ckn, dkv, trans_a=True, name="d_kv_up", out_dtype=BF16, bk=t)
    dckv, dgkv = _kv_mid_bwd(dkv, p['kv_w_ukv'], ckv, dkr, p['kv_norm_g'], rc, rs)
    ddown = _matmul(h1b, dckv, trans_a=True, name="d_kv_down", out_dtype=BF16, bm=512, bk=t)
    g['b_w_out'] = g['b_w_out'].reshape(N_DEV, -1, D_MODEL)
    g['b_w_in'] = g['b_w_in'].reshape(D_MODEL, N_DEV, -1).transpose(1, 0, 2)
    g['kv_w_down'] = ddown[:, :KV_LORA_RANK + QK_ROPE_DIM].reshape(N_DEV, -1, KV_LORA_RANK + QK_ROPE_DIM)
    dukv = dukv.reshape(KV_LORA_RANK, N_HEADS, HEAD_PAD)
    g['kv_w_uk'] = dukv[:, :, :QK_NOPE_DIM].reshape(N_DEV, -1, QK_NOPE_DIM)
    g['kv_w_uv'] = dukv[:, :, QK_NOPE_DIM:].reshape(N_DEV, -1, V_HEAD_DIM)
    g['b_w_uq'] = duq.reshape(Q_LORA_RANK, N_HEADS, HEAD_PAD)[:, :, :QK_NOPE_DIM + QK_ROPE_DIM].reshape(
        N_DEV, -1, QK_NOPE_DIM + QK_ROPE_DIM)
    sent1 = comm.send_group1(g)
    dr1, dr1b, dg0, db0, dba_out = _ln0_bwd(dr2, dprojb, p['b_w_in'], dckv, p['kv_w_down'], xh1, rstd1, ln_g0, sent1)
    dw_out = _matmul(u2, dr1b, trans_a=True, name="d_a_out", out_dtype=BF16, bm=512, bk=t).reshape(N_DEV, -1, D_MODEL)
    sent1b = comm.send_group1b({'a_w_out': dw_out})
    du1, dproj, dng, dnb, dbz = _conv_mid_bwd_rows(u1, proj, dr1b, p['a_w_out'], p['a_norm_g'], p['a_norm_b'], sent1b)
    dproj, dcw, dcb, dbv = _conv_bwd(du1, proj, p['a_conv_w'], dproj)
    half = D_MODEL // 2
    dw_lo = _matmul(xb, dproj, trans_a=True, name="d_a_in_lo", out_dtype=BF16, bm=half, bn=3 * e // N_DEV, bk=t,
                    out_blocked=True, a_cols=(0, half))
    small = (dg0, dg1, db0, db1, dbv, dbz, dcb, dng, dnb, dba_out, dgkv, dgq, dcw, loss)
    sent2 = comm.send_group2(dw_lo, small)
    dw_hi = _matmul(xb, dproj, trans_a=True, name="d_a_in_hi", out_dtype=BF16, bm=half, bn=3 * e // N_DEV, bk=t,
                    out_blocked=True, after=sent2, a_cols=(half, half))
    sent3 = comm.send_group3(dw_hi)
    grad_x = _grad_x(dproj, p['a_w_in3'], dr1, sent3)
    return loss, grad_x


def _grad_x(dproj, w3, dr1, after):
    t, d = dr1.shape
    nblk, _, cb = w3.shape
    tm = min(t, 512)
    extra = [] if after is None else [after]

    def body(a_ref, w_ref, dr_ref, *rest):
        o_ref = rest[-1]
        acc = ALPHA * dr_ref[...]
        for k in range(nblk):
            acc = acc + lax.dot_general(a_ref[:, k * cb:(k + 1) * cb], w_ref[k], NT, preferred_element_type=F32)
        o_ref[...] = acc

    row = pl.BlockSpec((tm, d), lambda i: (i, 0))
    return pl.pallas_call(
        body, name="grad_x",
        out_shape=jax.ShapeDtypeStruct((t, d), F32),
        grid=(t // tm,),
        in_specs=[pl.BlockSpec((tm, nblk * cb), lambda i: (i, 0)),
                  pl.BlockSpec((nblk, d, cb), lambda i: (0, 0, 0), pipeline_mode=pl.Buffered(1)), row]
                 + [pl.BlockSpec(memory_space=pl.ANY)] * len(extra),
        out_specs=row,
        compiler_params=_cparams(("parallel",)),
    )(dproj, w3, dr1, *extra)


def _shard_2d(a):
    return a.reshape(-1, a.shape[-1])


def _gathered_to_full(name, blocks):
    if name == 'a_w_in':
        return blocks
    if name == 'b_w_in':
        return blocks.transpose(1, 0, 2).reshape(D_MODEL, -1)
    if name == 'a_conv_w':
        return blocks.transpose(1, 0, 2).reshape(CONV_KERNEL, -1)
    if name in ('a_b_in', 'a_conv_b', 'a_norm_g', 'a_norm_b', 'a_b_out'):
        return blocks.reshape(-1)
    if name in ('kv_w_uk', 'kv_w_uv'):
        return blocks.reshape(KV_LORA_RANK, N_HEADS, -1)
    if name == 'b_w_uq':
        return blocks.reshape(Q_LORA_RANK, N_HEADS, -1)
    return blocks.reshape(-1, blocks.shape[-1])


def kernel(x, positions, ln_g, ln_b, a_w_in, a_b_in, a_conv_w, a_conv_b, a_norm_g, a_norm_b, a_w_out, a_b_out, kv_w_down, kv_norm_g, kv_w_uk, kv_w_uv, b_w_in, b_q_norm_g, b_w_uq, b_w_out, loss_target, m_ln_g, m_ln_b, m_a_w_in, m_a_b_in, m_a_conv_w, m_a_conv_b, m_a_norm_g, m_a_norm_b, m_a_w_out, m_a_b_out, m_kv_w_down, m_kv_norm_g, m_kv_w_uk, m_kv_w_uv, m_b_w_in, m_b_q_norm_g, m_b_w_uq, m_b_w_out, v_ln_g, v_ln_b, v_a_w_in, v_a_b_in, v_a_conv_w, v_a_conv_b, v_a_norm_g, v_a_norm_b, v_a_w_out, v_a_b_out, v_kv_w_down, v_kv_norm_g, v_kv_w_uk, v_kv_w_uv, v_b_w_in, v_b_q_norm_g, v_b_w_uq, v_b_w_out):
    w = dict(ln_g=ln_g, ln_b=ln_b, a_w_in=a_w_in, a_b_in=a_b_in, a_conv_w=a_conv_w, a_conv_b=a_conv_b,
             a_norm_g=a_norm_g, a_norm_b=a_norm_b, a_w_out=a_w_out, a_b_out=a_b_out, kv_w_down=kv_w_down,
             kv_norm_g=kv_norm_g, kv_w_uk=kv_w_uk, kv_w_uv=kv_w_uv, b_w_in=b_w_in, b_q_norm_g=b_q_norm_g,
             b_w_uq=b_w_uq, b_w_out=b_w_out)
    m = dict(ln_g=m_ln_g, ln_b=m_ln_b, a_w_in=m_a_w_in, a_b_in=m_a_b_in, a_conv_w=m_a_conv_w, a_conv_b=m_a_conv_b,
             a_norm_g=m_a_norm_g, a_norm_b=m_a_norm_b, a_w_out=m_a_w_out, a_b_out=m_a_b_out, kv_w_down=m_kv_w_down,
             kv_norm_g=m_kv_norm_g, kv_w_uk=m_kv_w_uk, kv_w_uv=m_kv_w_uv, b_w_in=m_b_w_in, b_q_norm_g=m_b_q_norm_g,
             b_w_uq=m_b_w_uq, b_w_out=m_b_w_out)
    v = dict(ln_g=v_ln_g, ln_b=v_ln_b, a_w_in=v_a_w_in, a_b_in=v_a_b_in, a_conv_w=v_a_conv_w, a_conv_b=v_a_conv_b,
             a_norm_g=v_a_norm_g, a_norm_b=v_a_norm_b, a_w_out=v_a_w_out, a_b_out=v_a_b_out, kv_w_down=v_kv_w_down,
             kv_norm_g=v_kv_norm_g, kv_w_uk=v_kv_w_uk, kv_w_uv=v_kv_w_uv, b_w_in=v_b_w_in, b_q_norm_g=v_b_q_norm_g,
             b_w_uq=v_b_w_uq, b_w_out=v_b_w_out)

    small_flat = jnp.concatenate([w[n].reshape(-1) for n in SMALL_WEIGHTS]).reshape(1, -1)
    rest_names = [n for n in MATMUL_WEIGHTS if n != 'a_w_in']
    group1 = ('b_w_out', 'b_w_uq', 'b_w_in', 'kv_w_uk', 'kv_w_uv', 'kv_w_down')
    group1b = ('a_w_out',)

    class Comm:
        def __init__(self):
            self.first, self.first_token = _comm_start(
                "gather_first_start", [_shard_2d(w['a_w_in']).astype(BF16), small_flat], True, None, CHIP_LEVEL_PEERS)

        def start_token(self):
            return self.first_token

        def first_weights(self, after):
            lands = _comm_wait("gather_first_wait", self.first, True, after, CHIP_LEVEL_PEERS)
            ga = _forward_to_sibling("gather_first_forward", list(lands))
            full = {'a_w_in': ga[0]}
            gs = ga[1].reshape(N_DEV, -1)
            off = 0
            for n in SMALL_WEIGHTS:
                size = math.prod(w[n].shape)
                full[n] = _gathered_to_full(n, gs[:, off:off + size].reshape((N_DEV,) + _shard_2d(w[n]).shape))
                off += size
            for n in REPLICATED:
                full[n] = w[n]
            dense = lambda a: a if a.shape[-1] % LANES == 0 else a.reshape(-1, LANES)
            self.rest, self.rest_token = _comm_start(
                "gather_rest_start", [dense(_shard_2d(w[n]).astype(BF16)) for n in rest_names], True, ga[0])
            return _prep_weights(full)

        def first_after(self):
            return self.rest_token

        def rest_weights(self, after):
            lands = _comm_wait("gather_rest_wait", self.rest, True, after)
            blocks = {n: lands[i].reshape((N_DEV,) + _shard_2d(w[n]).shape) for i, n in enumerate(rest_names)}
            return _prep_weights({n: _gathered_to_full(n, blocks[n]) for n in rest_names})

        def send_group1(self, g):
            self.g1, token = _comm_start("exchange_g1_start", [g[n] for n in group1], False, None)
            return token

        def send_group1b(self, g):
            self.g1b, token = _comm_start("exchange_g1b_start", [g[n] for n in group1b], False, None)
            return token

        def send_group2(self, dw_lo, small):
            self.g2, token = _comm_start("exchange_g2_start", [dw_lo, _pack_small_grads(*small)], False, None)
            return token

        def send_group3(self, dw_hi):
            self.g3, token = _comm_start("exchange_g3_start", [dw_hi], False, None)
            return token

    comm = Comm()

    _, grad_x = _local_step(x[0], positions[0], loss_target[0], comm)

    res = {}
    recv1 = _comm_wait("exchange_g1_wait", comm.g1, False, grad_x)
    for i, n in enumerate(group1):
        res[n] = _adamw(n, recv1[i], _shard_2d(w[n]), _shard_2d(m[n]), _shard_2d(v[n]))
    recv1b = _comm_wait("exchange_g1b_wait", comm.g1b, False, res[group1[-1]][0])
    for i, n in enumerate(group1b):
        res[n] = _adamw(n, recv1b[i], _shard_2d(w[n]), _shard_2d(m[n]), _shard_2d(v[n]))
    recv2 = _comm_wait("exchange_g2_wait", comm.g2, False, res[group1b[-1]][0])
    recv3 = _comm_wait("exchange_g3_wait", comm.g3, False, recv2[0])
    res['a_w_in'] = _adamw('a_w_in', [recv2[0], recv3[0]], _shard_2d(w['a_w_in']), _shard_2d(m['a_w_in']),
                           _shard_2d(v['a_w_in']))
    two_d = lambda d: [_shard_2d(d[n]) if d[n].ndim > 1 else d[n].reshape(1, -1) for n in SMALL_NAMES]
    sg, sd, sm, sv, loss = _adamw_small(recv2[-1], two_d(w), two_d(m), two_d(v))
    for i, n in enumerate(SMALL_NAMES):
        res[n] = (sg[i], sd[i], sm[i], sv[i])
    outs = [[res[n][j].reshape(w[n].shape) for n in WEIGHT_NAMES] for j in range(4)]
    return (loss[0, 0], grad_x[None], *outs[0], *outs[1], *outs[2], *outs[3])
```

```python
import functools
import math

import jax
import jax.numpy as jnp
from jax import lax
from jax.experimental import pallas as pl
from jax.experimental.pallas import tpu as pltpu

F32 = jnp.float32
BF16 = jnp.bfloat16

D_MODEL = 1024
DEPTH = 2
CONV_WIDTH = 2 * D_MODEL
CONV_KERNEL = 31
N_HEADS = 16
QK_NOPE_DIM = 128
QK_ROPE_DIM = 64
V_HEAD_DIM = 128
KV_LORA_RANK = D_MODEL // 4
Q_LORA_RANK = D_MODEL // 2
ROPE_THETA = 10000.0
LN_EPS = 1e-5
RMS_EPS = 1e-6
MASK_VALUE = -1e30
ALPHA = (2.0 * DEPTH) ** 0.25
ATTN_SCALE = 1.0 / math.sqrt(QK_NOPE_DIM + QK_ROPE_DIM)

ADAM_LR = 0.001
ADAM_B1 = 0.9
ADAM_B2 = 0.999
ADAM_EPS = 1e-08
ADAM_WD = 0.01
ADAM_STEP = 10

N_DEV = 8
LANES = 128
HEAD_PAD = 256
HALO = 32
VMEM_LIMIT = 56 * 1024 * 1024

WEIGHT_NAMES = ['ln_g', 'ln_b', 'a_w_in', 'a_b_in', 'a_conv_w', 'a_conv_b', 'a_norm_g', 'a_norm_b',
                'a_w_out', 'a_b_out', 'kv_w_down', 'kv_norm_g', 'kv_w_uk', 'kv_w_uv', 'b_w_in',
                'b_q_norm_g', 'b_w_uq', 'b_w_out']
REPLICATED = ('ln_g', 'ln_b', 'kv_norm_g', 'b_q_norm_g')
MATMUL_WEIGHTS = ('a_w_in', 'a_w_out', 'kv_w_down', 'kv_w_uk', 'kv_w_uv', 'b_w_in', 'b_w_uq', 'b_w_out')
SMALL_WEIGHTS = ('a_b_in', 'a_conv_w', 'a_conv_b', 'a_norm_g', 'a_norm_b', 'a_b_out')

NN = (((1,), (0,)), ((), ()))
NT = (((1,), (1,)), ((), ()))
TN = (((0,), (0,)), ((), ()))


def _cparams(sem):
    return pltpu.CompilerParams(dimension_semantics=sem, vmem_limit_bytes=VMEM_LIMIT)


def _sigmoid(x):
    return 0.5 * jnp.tanh(0.5 * x) + 0.5


def _fold_rows(x):
    parts = [x[r:r + 8] for r in range(0, x.shape[0], 8)]
    while len(parts) > 1:
        parts = [parts[i] + parts[i + 1] for i in range(0, len(parts) - 1, 2)] + ([parts[-1]] if len(parts) % 2 else [])
    return parts[0]


def _taps_by_residue(offset_of):
    groups = {}
    for j in range(CONV_KERNEL):
        off = offset_of(j)
        groups.setdefault(off % 8, []).append((j, off // 8))
    return sorted(groups.items())


def _swap_rope_halves(t):
    lane = lax.broadcasted_iota(jnp.int32, t.shape, 1)
    return jnp.where(lane < QK_ROPE_DIM // 2, pltpu.roll(t, LANES - QK_ROPE_DIM // 2, 1),
                     pltpu.roll(t, QK_ROPE_DIM // 2, 1))


def _matmul(a, b, *, name, bias=None, trans_a=False, trans_b=False, out_dtype=F32, bm=1024, bn=1024, bk=1024,
            b_blocked=False, out_blocked=False, after=None, a_cols=None):
    m_off = 0
    if trans_a:
        kdim, m = a.shape
        if a_cols is not None:
            m_off, m = a_cols
    else:
        m, kdim = a.shape
    if b_blocked and trans_b:
        n, k2 = b.shape[1], b.shape[0] * b.shape[2]
        bk = b.shape[2]
    elif b_blocked:
        k2, n = b.shape[1], b.shape[0] * b.shape[2]
        bn = b.shape[2]
    elif trans_b:
        n, k2 = b.shape
    else:
        k2, n = b.shape
    assert kdim == k2, (a.shape, b.shape)
    bm, bn, bk = min(bm, m), min(bn, n), min(bk, kdim)
    assert m % bm == 0 and n % bn == 0 and kdim % bk == 0, (name, m, n, kdim, bm, bn, bk)
    nk = kdim // bk
    dims = (((0 if trans_a else 1,), (1 if trans_b else 0,)), ((), ()))
    has_bias = bias is not None

    def body(*refs):
        refs = list(refs)
        a_ref, b_ref = refs[:2]
        del refs[:2]
        bias_ref = refs.pop(0) if has_bias else None
        if after is not None:
            refs.pop(0)
        o_ref = refs.pop(0)
        rest = refs
        prod = lax.dot_general(a_ref[...], b_ref[...], dims, preferred_element_type=F32)

        def finish(acc):
            if has_bias:
                acc = acc + bias_ref[...]
            o_ref[...] = acc.astype(out_dtype)

        if nk == 1:
            finish(prod)
        else:
            acc_ref = rest[0]
            k = pl.program_id(2)

            @pl.when(k == 0)
            def _():
                acc_ref[...] = prod

            @pl.when(k > 0)
            def _():
                acc_ref[...] += prod

            @pl.when(k == nk - 1)
            def _():
                finish(acc_ref[...])

    assert m_off % bm == 0
    a_first = m_off // bm
    a_spec = (pl.BlockSpec((bk, bm), lambda i, j, k: (k, i + a_first)) if trans_a
              else pl.BlockSpec((bm, bk), lambda i, j, k: (i, k)))
    if b_blocked and trans_b:
        b_spec = pl.BlockSpec((None, bn, bk), lambda i, j, k: (k, j, 0))
    elif b_blocked:
        b_spec = pl.BlockSpec((None, bk, bn), lambda i, j, k: (j, k, 0))
    elif trans_b:
        b_spec = pl.BlockSpec((bn, bk), lambda i, j, k: (j, k))
    else:
        b_spec = pl.BlockSpec((bk, bn), lambda i, j, k: (k, j))
    in_specs = [a_spec, b_spec]
    args = [a, b]
    if has_bias:
        in_specs.append(pl.BlockSpec((1, bn), lambda i, j, k: (0, j)))
        args.append(bias)
    if after is not None:
        in_specs.append(pl.BlockSpec(memory_space=pl.ANY))
        args.append(after)
    if out_blocked:
        out_struct = jax.ShapeDtypeStruct((n // bn, m, bn), out_dtype)
        out_spec = pl.BlockSpec((None, bm, bn), lambda i, j, k: (j, i, 0))
    else:
        out_struct = jax.ShapeDtypeStruct((m, n), out_dtype)
        out_spec = pl.BlockSpec((bm, bn), lambda i, j, k: (i, j))
    return pl.pallas_call(
        body, name=name,
        out_shape=out_struct,
        grid=(m // bm, n // bn, nk),
        in_specs=in_specs,
        out_specs=out_spec,
        scratch_shapes=[pltpu.VMEM((bm, bn), F32)] if nk > 1 else [],
        compiler_params=_cparams(("parallel", "parallel", "arbitrary")),
    )(*args)


def _rope_tables(pos_col, freq_row, x, after):
    t, d = x.shape
    tm = min(t, 512)
    extra = [] if after is None else [after]

    def body(pos_ref, f_ref, x_ref, *rest):
        c_ref, s_ref, xb_ref = rest[-3:]
        ang = pos_ref[...].astype(F32) * f_ref[...]
        lane = lax.broadcasted_iota(jnp.int32, ang.shape, 1)
        cos = jnp.cos(ang)
        sin = jnp.sin(ang)
        c_ref[...] = jnp.where(lane < QK_ROPE_DIM, cos, 0.0)
        s_ref[...] = jnp.where(lane < QK_ROPE_DIM // 2, -sin, jnp.where(lane < QK_ROPE_DIM, sin, 0.0))
        xb_ref[...] = x_ref[...].astype(BF16)

    lane_blk = pl.BlockSpec((tm, LANES), lambda i: (i, 0))
    row = pl.BlockSpec((tm, d), lambda i: (i, 0))
    return pl.pallas_call(
        body, name="rope_tables",
        out_shape=(jax.ShapeDtypeStruct((t, LANES), F32), jax.ShapeDtypeStruct((t, LANES), F32),
                   jax.ShapeDtypeStruct((t, d), BF16)),
        grid=(t // tm,),
        in_specs=[pl.BlockSpec((tm, 1), lambda i: (i, 0)), pl.BlockSpec((1, LANES), lambda i: (0, 0)), row]
                 + [pl.BlockSpec(memory_space=pl.ANY)] * len(extra),
        out_specs=(lane_blk, lane_blk, row),
        compiler_params=_cparams(("parallel",)),
    )(pos_col, freq_row, x, *extra)


def _conv_mid_fwd(proj, cw, cb, ng, nb):
    t = proj.shape[0]
    e = CONV_WIDTH
    tm = min(t, 128)
    hb = tm // HALO
    nchunk = e // LANES

    def body(val_ref, gg_ref, z_ref, valh_ref, ggh_ref, cw_ref, cb_ref, ng_ref, nb_ref, u1_ref, u2_ref, ext_ref, sh_ref):
        i = pl.program_id(0)
        u0 = val_ref[...] * _sigmoid(gg_ref[...])
        h0 = valh_ref[...] * _sigmoid(ggh_ref[...])
        ext_ref[0:HALO, :] = jnp.where(i > 0, h0, 0.0)
        ext_ref[HALO:, :] = u0

        def chunk(c, carry):
            col = pl.multiple_of(c * LANES, LANES)
            acc = jnp.zeros((tm, LANES), F32)
            for res, taps in _taps_by_residue(lambda j: HALO - (CONV_KERNEL - 1) + j):
                span = 8 * max(a for _, a in taps) + tm
                sh_ref[0:span, :] = ext_ref[res:res + span, pl.ds(col, LANES)]
                for j, a in taps:
                    acc = acc + cw_ref[j:j + 1, pl.ds(col, LANES)] * sh_ref[8 * a:8 * a + tm, :]
            u1_ref[:, pl.ds(col, LANES)] = acc + cb_ref[:, pl.ds(col, LANES)]
            return carry

        lax.fori_loop(0, nchunk, chunk, 0)
        g = ng_ref[...]
        b = nb_ref[...]
        for r in range(0, tm, 16):
            u1 = u1_ref[r:r + 16, :]
            mu = jnp.mean(u1, axis=-1, keepdims=True)
            xc = u1 - mu
            var = jnp.mean(xc * xc, axis=-1, keepdims=True)
            n = xc * lax.rsqrt(var + LN_EPS) * g + b
            z = z_ref[r:r + 16, :]
            u2_ref[r:r + 16, :] = ((n * _sigmoid(n)) * (z * _sigmoid(z))).astype(BF16)

    row = lambda c: pl.BlockSpec((tm, e), lambda i: (i, c))
    halo = lambda c: pl.BlockSpec((HALO, e), lambda i: (jnp.maximum(i * hb - 1, 0), c))
    par = lambda r: pl.BlockSpec((r, e), lambda i: (0, 0))
    return pl.pallas_call(
        body, name="conv_mid_fwd",
        out_shape=(jax.ShapeDtypeStruct((t, e), F32), jax.ShapeDtypeStruct((t, e), BF16)),
        grid=(t // tm,),
        in_specs=[row(0), row(1), row(2), halo(0), halo(1), par(HALO), par(1), par(1), par(1)],
        out_specs=(pl.BlockSpec((tm, e), lambda i: (i, 0)), pl.BlockSpec((tm, e), lambda i: (i, 0))),
        scratch_shapes=[pltpu.VMEM((tm + HALO, e), F32), pltpu.VMEM((tm + HALO, LANES), F32)],
        compiler_params=_cparams(("parallel",)),
    )(proj, proj, proj, proj, proj, cw, cb, ng, nb)


def _ln_res_fwd(x, u, w_out, b_out, g, b):
    t, d = x.shape
    ku = u.shape[1]
    tm = min(t, 512)

    def body(x_ref, u_ref, w_ref, bo_ref, g_ref, b_ref, h_ref, hb_ref, xh_ref, rs_ref):
        y = lax.dot_general(u_ref[...], w_ref[...], NN, preferred_element_type=F32) + bo_ref[...]
        r = ALPHA * x_ref[...] + y
        mu = jnp.mean(r, axis=-1, keepdims=True)
        xc = r - mu
        var = jnp.mean(xc * xc, axis=-1, keepdims=True)
        rstd = lax.rsqrt(var + LN_EPS)
        xh = xc * rstd
        h = xh * g_ref[...] + b_ref[...]
        h_ref[...] = h
        hb_ref[...] = h.astype(BF16)
        xh_ref[...] = xh
        rs_ref[...] = rstd

    row = pl.BlockSpec((tm, d), lambda i: (i, 0))
    par = pl.BlockSpec((1, d), lambda i: (0, 0))
    return pl.pallas_call(
        body, name="ln0_fwd",
        out_shape=(jax.ShapeDtypeStruct((t, d), F32), jax.ShapeDtypeStruct((t, d), BF16),
                   jax.ShapeDtypeStruct((t, d), F32), jax.ShapeDtypeStruct((t, 1), F32)),
        grid=(t // tm,),
        in_specs=[row, pl.BlockSpec((tm, ku), lambda i: (i, 0)),
                  pl.BlockSpec((ku, d), lambda i: (0, 0), pipeline_mode=pl.Buffered(1)), par, par, par],
        out_specs=(row, row, row, pl.BlockSpec((tm, 1), lambda i: (i, 0))),
        compiler_params=_cparams(("parallel",)),
    )(x, u, w_out, b_out, g, b)


def _kv_mid_fwd(h1b, w_down, w_ukv, g, rc, rs):
    t, d = h1b.shape
    nkv = w_ukv.shape[1]
    tm = min(t, 512)
    r = KV_LORA_RANK

    def body(h_ref, w_ref, wu_ref, g_ref, c_ref, s_ref, ckv_ref, ckn_ref, kr_ref, kv_ref):
        ckv = lax.dot_general(h_ref[...], w_ref[...], NN, preferred_element_type=F32)
        ckv_ref[...] = ckv
        c = ckv[:, 0:r]
        ms = jnp.mean(c * c, axis=-1, keepdims=True)
        ckn = (c * lax.rsqrt(ms + RMS_EPS) * g_ref[...]).astype(BF16)
        ckn_ref[...] = ckn
        tr = ckv[:, r:r + LANES]
        kr_ref[...] = (tr * c_ref[...] + _swap_rope_halves(tr) * s_ref[...]).astype(BF16)
        kv_ref[...] = lax.dot_general(ckn, wu_ref[...], NN, preferred_element_type=F32).astype(BF16)

    lane_blk = pl.BlockSpec((tm, LANES), lambda i: (i, 0))
    return pl.pallas_call(
        body, name="kv_mid_fwd",
        out_shape=(jax.ShapeDtypeStruct((t, r + LANES), F32), jax.ShapeDtypeStruct((t, r), BF16),
                   jax.ShapeDtypeStruct((t, LANES), BF16), jax.ShapeDtypeStruct((t, nkv), BF16)),
        grid=(t // tm,),
        in_specs=[pl.BlockSpec((tm, d), lambda i: (i, 0)),
                  pl.BlockSpec((d, r + LANES), lambda i: (0, 0), pipeline_mode=pl.Buffered(1)),
                  pl.BlockSpec((r, nkv), lambda i: (0, 0), pipeline_mode=pl.Buffered(1)),
                  pl.BlockSpec((1, r), lambda i: (0, 0)), lane_blk, lane_blk],
        out_specs=(pl.BlockSpec((tm, r + LANES), lambda i: (i, 0)), pl.BlockSpec((tm, r), lambda i: (i, 0)), lane_blk,
                   pl.BlockSpec((tm, nkv), lambda i: (i, 0))),
        compiler_params=_cparams(("parallel",)),
    )(h1b, w_down, w_ukv, g, rc, rs)


def _b_in_qnorm(h1b, w_in, g):
    t, d = h1b.shape
    n = w_in.shape[1]
    r = Q_LORA_RANK
    tm = min(t, 512)

    def body(a_ref, w_ref, g_ref, o_ref, cqn_ref):
        acc = lax.dot_general(a_ref[...], w_ref[...], NN, preferred_element_type=F32)
        o_ref[...] = acc
        c = acc[:, 0:r]
        ms = jnp.mean(c * c, axis=-1, keepdims=True)
        cqn_ref[...] = (c * lax.rsqrt(ms + RMS_EPS) * g_ref[...]).astype(BF16)

    return pl.pallas_call(
        body, name="b_in",
        out_shape=(jax.ShapeDtypeStruct((t, n), F32), jax.ShapeDtypeStruct((t, r), BF16)),
        grid=(t // tm,),
        in_specs=[pl.BlockSpec((tm, d), lambda i: (i, 0)),
                  pl.BlockSpec((d, n), lambda i: (0, 0), pipeline_mode=pl.Buffered(1)),
                  pl.BlockSpec((1, r), lambda i: (0, 0))],
        out_specs=(pl.BlockSpec((tm, n), lambda i: (i, 0)), pl.BlockSpec((tm, r), lambda i: (i, 0))),
        compiler_params=_cparams(("parallel",)),
    )(h1b, w_in, g)


def _q_up_rope(cqn, wuq, rc, rs):
    t, r = cqn.shape
    w = wuq.shape[1]
    tm = min(t, 1024)
    bn = 4 * HEAD_PAD

    def body(a_ref, b_ref, c_ref, s_ref, o_ref):
        q = lax.dot_general(a_ref[...], b_ref[...], NN, preferred_element_type=F32)
        c = c_ref[...]
        s = s_ref[...]
        for h in range(bn // HEAD_PAD):
            a = h * HEAD_PAD
            o_ref[:, a:a + LANES] = q[:, a:a + LANES].astype(BF16)
            tr = q[:, a + LANES:a + 2 * LANES]
            o_ref[:, a + LANES:a + 2 * LANES] = (tr * c + _swap_rope_halves(tr) * s).astype(BF16)

    return pl.pallas_call(
        body, name="q_up_rope",
        out_shape=jax.ShapeDtypeStruct((t, w), BF16),
        grid=(t // tm, w // bn),
        in_specs=[pl.BlockSpec((tm, r), lambda i, j: (i, 0)), pl.BlockSpec((r, bn), lambda i, j: (0, j)),
                  pl.BlockSpec((tm, LANES), lambda i, j: (i, 0)), pl.BlockSpec((tm, LANES), lambda i, j: (i, 0))],
        out_specs=pl.BlockSpec((tm, bn), lambda i, j: (i, j)),
        compiler_params=_cparams(("parallel", "parallel")),
    )(cqn, wuq, rc, rs)


def _flash_fwd(qcat, kv, kr, projb):
    t = qcat.shape[0]
    tq = min(t, 512)
    nq = t // tq
    exp2_scale = ATTN_SCALE * math.log2(math.e)
    z_first = Q_LORA_RANK // LANES

    def body(q_ref, kv_ref, kr_ref, z_ref, o_ref, o2_ref, lse_ref, kcat_ref):
        kcat_ref[:, 0:LANES] = kv_ref[:, 0:LANES]
        kcat_ref[:, LANES:] = kr_ref[...]
        rows = lax.broadcasted_iota(jnp.int32, (tq, tq), 0)
        cols = lax.broadcasted_iota(jnp.int32, (tq, tq), 1)
        for i in range(nq):
            a = i * tq
            q = q_ref[a:a + tq, :]
            sd = lax.dot_general(q, kcat_ref[a:a + tq, :], NT, preferred_element_type=F32)
            sd = jnp.where(cols <= rows, sd, MASK_VALUE)
            m = jnp.max(sd, axis=1, keepdims=True)
            if i > 0:
                sp = lax.dot_general(q, kcat_ref[0:a, :], NT, preferred_element_type=F32)
                m = jnp.maximum(m, jnp.max(sp, axis=1, keepdims=True))
            pd = jnp.exp2((sd - m) * exp2_scale)
            l = jnp.sum(pd, axis=1, keepdims=True)
            acc = lax.dot_general(pd.astype(BF16), kv_ref[a:a + tq, LANES:], NN, preferred_element_type=F32)
            if i > 0:
                pp = jnp.exp2((sp - m) * exp2_scale)
                l = l + jnp.sum(pp, axis=1, keepdims=True)
                acc = acc + lax.dot_general(pp.astype(BF16), kv_ref[0:a, LANES:], NN, preferred_element_type=F32)
            o = acc / l
            z = z_ref[a:a + tq, :]
            o_ref[a:a + tq, :] = o
            o2_ref[a:a + tq, :] = (o * (z * _sigmoid(z))).astype(BF16)
            lse_ref[a:a + tq, :] = jnp.broadcast_to(m * ATTN_SCALE + jnp.log(l), (tq, LANES))

    hw = N_HEADS * LANES
    head256 = pl.BlockSpec((t, HEAD_PAD), lambda h: (0, h))
    head128 = pl.BlockSpec((t, LANES), lambda h: (0, h))
    return pl.pallas_call(
        body, name="flash_fwd",
        out_shape=(jax.ShapeDtypeStruct((t, hw), F32), jax.ShapeDtypeStruct((t, hw), BF16),
                   jax.ShapeDtypeStruct((t, hw), F32)),
        grid=(N_HEADS,),
        in_specs=[head256, head256, pl.BlockSpec((t, LANES), lambda h: (0, 0), pipeline_mode=pl.Buffered(1)),
                  pl.BlockSpec((t, LANES), lambda h: (0, z_first + h))],
        out_specs=(head128, head128, head128),
        scratch_shapes=[pltpu.VMEM((t, HEAD_PAD), BF16)],
        compiler_params=_cparams(("parallel",)),
    )(qcat, kv, kr, projb)


def _final_ln_loss(h1, o2, w_out, g, b, target):
    t, d = h1.shape
    ku = o2.shape[1]
    tm = min(t, 512)

    def body(h_ref, u_ref, w_ref, g_ref, b_ref, t_ref, loss_ref, dr_ref, drb_ref, dg_ref, db_ref):
        i = pl.program_id(0)
        r = ALPHA * h_ref[...] + lax.dot_general(u_ref[...], w_ref[...], NN, preferred_element_type=F32)
        mu = jnp.mean(r, axis=-1, keepdims=True)
        xc = r - mu
        var = jnp.mean(xc * xc, axis=-1, keepdims=True)
        rstd = lax.rsqrt(var + LN_EPS)
        xh = xc * rstd
        g = g_ref[...]
        diff = xh * g + b_ref[...] - t_ref[...]
        part = 0.5 * jnp.sum(jnp.mean(diff * diff, axis=-1, keepdims=True))
        dh = diff * (1.0 / d)
        dgp = jnp.sum(dh * xh, axis=0, keepdims=True)
        dbp = jnp.sum(dh, axis=0, keepdims=True)

        @pl.when(i == 0)
        def _():
            loss_ref[...] = jnp.zeros_like(loss_ref)
            dg_ref[...] = jnp.zeros_like(dg_ref)
            db_ref[...] = jnp.zeros_like(db_ref)

        loss_ref[...] += jnp.full(loss_ref.shape, part, F32)
        dg_ref[...] += dgp
        db_ref[...] += dbp
        dxh = dh * g
        dr = rstd * (dxh - jnp.mean(dxh, axis=-1, keepdims=True) - xh * jnp.mean(dxh * xh, axis=-1, keepdims=True))
        dr_ref[...] = dr
        drb_ref[...] = dr.astype(BF16)

    row = pl.BlockSpec((tm, d), lambda i: (i, 0))
    par = pl.BlockSpec((1, d), lambda i: (0, 0))
    return pl.pallas_call(
        body, name="final_ln_loss",
        out_shape=(jax.ShapeDtypeStruct((1, LANES), F32), jax.ShapeDtypeStruct((t, d), F32),
                   jax.ShapeDtypeStruct((t, d), BF16), jax.ShapeDtypeStruct((1, d), F32),
                   jax.ShapeDtypeStruct((1, d), F32)),
        grid=(t // tm,),
        in_specs=[row, pl.BlockSpec((tm, ku), lambda i: (i, 0)),
                  pl.BlockSpec((ku, d), lambda i: (0, 0), pipeline_mode=pl.Buffered(1)), par, par, row],
        out_specs=(pl.BlockSpec((1, LANES), lambda i: (0, 0)), row, row, par, par),
        compiler_params=_cparams(("arbitrary",)),
    )(h1, o2, w_out, g, b, target)


def _gate_bwd(dr2b, w_out, o, projb):
    t, hw = o.shape
    d = dr2b.shape[1]
    tm = min(t, 1024)
    bw = Q_LORA_RANK
    nb = hw // bw
    wb = projb.shape[1]

    def body(dr_ref, w_ref, o_ref, z_ref, dob_ref, dz_ref):
        z = z_ref[...]
        sg = _sigmoid(z)
        w = w_ref[pl.ds(pl.multiple_of(pl.program_id(1) * bw, bw), bw), :]
        d2 = lax.dot_general(dr_ref[...], w, NT, preferred_element_type=F32)
        dob_ref[...] = (d2 * (z * sg)).astype(BF16)
        dz_ref[...] = (d2 * o_ref[...] * (sg * (1.0 + z * (1.0 - sg)))).astype(BF16)

    blk = pl.BlockSpec((tm, bw), lambda i, j: (i, j))
    blk1 = pl.BlockSpec((tm, bw), lambda i, j: (i, j + 1))
    return pl.pallas_call(
        body, name="gate_bwd",
        out_shape=(jax.ShapeDtypeStruct((t, hw), BF16), jax.ShapeDtypeStruct((t, wb), BF16)),
        grid=(t // tm, nb),
        in_specs=[pl.BlockSpec((tm, d), lambda i, j: (i, 0)),
                  pl.BlockSpec((hw, d), lambda i, j: (0, 0), pipeline_mode=pl.Buffered(1)), blk, blk1],
        out_specs=(blk, blk1),
        compiler_params=_cparams(("parallel", "arbitrary")),
    )(dr2b, w_out, o, projb)


def _flash_bwd(qcat, kv, kr, dob, lse, o, rc, rs):
    t = qcat.shape[0]
    tq = min(t, 512)
    nq = t // tq
    q_chunk = 2 * tq
    log2e = math.log2(math.e)
    exp2_scale = ATTN_SCALE * log2e
    once = pl.Buffered(1)

    def body(q_ref, kv_ref, kr_ref, do_ref, lse_ref, o_ref, c_ref, s_ref,
             dq_ref, dkv_ref, dkr_ref, kcat_ref, dqa_ref, dka_ref, dva_ref, dl_ref):
        h = pl.program_id(0)
        kcat_ref[:, 0:LANES] = kv_ref[:, 0:LANES]
        kcat_ref[:, LANES:] = kr_ref[...]
        dqa_ref[...] = jnp.zeros_like(dqa_ref)
        dl_ref[...] = jnp.sum(do_ref[...].astype(F32) * o_ref[...], axis=1, keepdims=True)

        @pl.when(h == 0)
        def _():
            dkr_ref[...] = jnp.zeros_like(dkr_ref)

        rows = lax.broadcasted_iota(jnp.int32, (tq, tq), 0)
        cols = lax.broadcasted_iota(jnp.int32, (tq, tq), 1)

        def block(qs, n, ks, masked):
            q = q_ref[qs:qs + n, :]
            kc = kcat_ref[ks:ks + tq, :]
            s = lax.dot_general(q, kc, NT, preferred_element_type=F32)
            if masked:
                s = jnp.where(cols <= rows, s, MASK_VALUE)
            p = jnp.exp2(s * exp2_scale - lse_ref[qs:qs + n, 0:1] * log2e)
            do = do_ref[qs:qs + n, :]
            dp = lax.dot_general(do, kv_ref[ks:ks + tq, LANES:], NT, preferred_element_type=F32)
            ds = (p * (dp - dl_ref[qs:qs + n, :])).astype(BF16)
            dva_ref[...] += lax.dot_general(p.astype(BF16), do, TN, preferred_element_type=F32)
            dka_ref[...] += lax.dot_general(ds, q, TN, preferred_element_type=F32)
            dqa_ref[qs:qs + n, :] += lax.dot_general(ds, kc, NN, preferred_element_type=F32)

        for kb in range(nq):
            a = kb * tq
            dka_ref[...] = jnp.zeros_like(dka_ref)
            dva_ref[...] = jnp.zeros_like(dva_ref)
            block(a, tq, a, True)
            qs = a + tq
            while qs < t:
                n = min(q_chunk, t - qs)
                block(qs, n, a, False)
                qs += n
            dk = dka_ref[...] * ATTN_SCALE
            dkv_ref[a:a + tq, 0:LANES] = dk[:, 0:LANES].astype(BF16)
            dkv_ref[a:a + tq, LANES:] = dva_ref[...].astype(BF16)
            dkr_ref[a:a + tq, :] += dk[:, LANES:]

        dq = dqa_ref[...] * ATTN_SCALE
        dq_ref[:, 0:LANES] = dq[:, 0:LANES].astype(BF16)
        d2 = dq[:, LANES:]
        dq_ref[:, LANES:] = (d2 * c_ref[...] - _swap_rope_halves(d2) * s_ref[...]).astype(BF16)

    hp = N_HEADS * HEAD_PAD
    head256 = pl.BlockSpec((t, HEAD_PAD), lambda h: (0, h))
    head128 = pl.BlockSpec((t, LANES), lambda h: (0, h))
    const128 = pl.BlockSpec((t, LANES), lambda h: (0, 0), pipeline_mode=once)
    return pl.pallas_call(
        body, name="flash_bwd",
        out_shape=(jax.ShapeDtypeStruct((t, hp), BF16), jax.ShapeDtypeStruct((t, hp), BF16),
                   jax.ShapeDtypeStruct((t, LANES), F32)),
        grid=(N_HEADS,),
        in_specs=[head256, head256, const128, head128, head128, head128, const128, const128],
        out_specs=(head256, head256, pl.BlockSpec((t, LANES), lambda h: (0, 0))),
        scratch_shapes=[pltpu.VMEM((t, HEAD_PAD), BF16), pltpu.VMEM((t, HEAD_PAD), F32),
                        pltpu.VMEM((tq, HEAD_PAD), F32), pltpu.VMEM((tq, LANES), F32), pltpu.VMEM((t, 1), F32)],
        compiler_params=_cparams(("arbitrary",)),
    )(qcat, kv, kr, dob, lse, o, rc, rs)


def _q_norm_bwd(dq2, w_uq, projb, g, dprojb):
    t = projb.shape[0]
    kq = dq2.shape[1]
    tm = min(t, 512)
    r = Q_LORA_RANK

    def body(dq_ref, w_ref, c_ref, g_ref, alias_ref, o_ref, dg_ref):
        i = pl.program_id(0)
        c = c_ref[...]
        d = lax.dot_general(dq_ref[...], w_ref[...], NT, preferred_element_type=F32)
        rr = lax.rsqrt(jnp.mean(c * c, axis=-1, keepdims=True) + RMS_EPS)
        xh = c * rr

        @pl.when(i == 0)
        def _():
            dg_ref[...] = jnp.zeros_like(dg_ref)

        dg_ref[...] += jnp.sum(d * xh, axis=0, keepdims=True)
        dxh = d * g_ref[...]
        o_ref[...] = (rr * (dxh - xh * jnp.mean(dxh * xh, axis=-1, keepdims=True))).astype(BF16)

    blk = pl.BlockSpec((tm, r), lambda i: (i, 0))
    par = pl.BlockSpec((1, r), lambda i: (0, 0))
    return pl.pallas_call(
        body, name="q_norm_bwd",
        out_shape=(jax.ShapeDtypeStruct(dprojb.shape, BF16), jax.ShapeDtypeStruct((1, r), F32)),
        grid=(t // tm,),
        in_specs=[pl.BlockSpec((tm, kq), lambda i: (i, 0)),
                  pl.BlockSpec((r, kq), lambda i: (0, 0), pipeline_mode=pl.Buffered(1)),
                  blk, par, pl.BlockSpec(memory_space=pl.ANY)],
        out_specs=(blk, par),
        input_output_aliases={4: 0},
        compiler_params=_cparams(("arbitrary",)),
    )(dq2, w_uq, projb, g, dprojb)


def _kv_mid_bwd(dkv, w_ukv, ckv, dkr, g, rc, rs):
    t = ckv.shape[0]
    kk = dkv.shape[1]
    tm = min(t, 512)
    r = KV_LORA_RANK

    def body(dkv_ref, w_ref, ckv_ref, dkr_ref, g_ref, c_ref, s_ref, o_ref, dg_ref):
        i = pl.program_id(0)
        c = ckv_ref[:, 0:r]
        d = lax.dot_general(dkv_ref[...], w_ref[...], NT, preferred_element_type=F32)
        rr = lax.rsqrt(jnp.mean(c * c, axis=-1, keepdims=True) + RMS_EPS)
        xh = c * rr

        @pl.when(i == 0)
        def _():
            dg_ref[...] = jnp.zeros_like(dg_ref)

        dg_ref[...] += jnp.sum(d * xh, axis=0, keepdims=True)
        dxh = d * g_ref[...]
        o_ref[:, 0:r] = (rr * (dxh - xh * jnp.mean(dxh * xh, axis=-1, keepdims=True))).astype(BF16)
        dk = dkr_ref[...]
        o_ref[:, r:] = (dk * c_ref[...] - _swap_rope_halves(dk) * s_ref[...]).astype(BF16)

    return pl.pallas_call(
        body, name="kv_mid_bwd",
        out_shape=(jax.ShapeDtypeStruct((t, r + LANES), BF16), jax.ShapeDtypeStruct((1, r), F32)),
        grid=(t // tm,),
        in_specs=[pl.BlockSpec((tm, kk), lambda i: (i, 0)),
                  pl.BlockSpec((r, kk), lambda i: (0, 0), pipeline_mode=pl.Buffered(1)),
                  pl.BlockSpec((tm, r + LANES), lambda i: (i, 0)),
                  pl.BlockSpec((tm, LANES), lambda i: (i, 0)), pl.BlockSpec((1, r), lambda i: (0, 0)),
                  pl.BlockSpec((tm, LANES), lambda i: (i, 0)), pl.BlockSpec((tm, LANES), lambda i: (i, 0))],
        out_specs=(pl.BlockSpec((tm, r + LANES), lambda i: (i, 0)), pl.BlockSpec((1, r), lambda i: (0, 0))),
        compiler_params=_cparams(("arbitrary",)),
    )(dkv, w_ukv, ckv, dkr, g, rc, rs)


def _ln0_bwd(dr2, dprojb, w_in, dckv, w_down, xh, rstd, g, after):
    t, d = dr2.shape
    kb, kd = dprojb.shape[1], dckv.shape[1]
    tm = min(t, 256)
    extra = [] if after is None else [after]

    def body(a_ref, pb_ref, wb_ref, pd_ref, wd_ref, xh_ref, rs_ref, g_ref, *rest):
        dr_ref, drb_ref, dg_ref, db_ref, dsum_ref = rest[-5:]
        i = pl.program_id(0)
        dh = (ALPHA * a_ref[...] + lax.dot_general(pb_ref[...], wb_ref[...], NT, preferred_element_type=F32)
              + lax.dot_general(pd_ref[...], wd_ref[...], NT, preferred_element_type=F32))
        xh = xh_ref[...]

        @pl.when(i == 0)
        def _():
            dg_ref[...] = jnp.zeros_like(dg_ref)
            db_ref[...] = jnp.zeros_like(db_ref)
            dsum_ref[...] = jnp.zeros_like(dsum_ref)

        dg_ref[...] += jnp.sum(dh * xh, axis=0, keepdims=True)
        db_ref[...] += jnp.sum(dh, axis=0, keepdims=True)
        dxh = dh * g_ref[...]
        dr = rs_ref[...] * (dxh - jnp.mean(dxh, axis=-1, keepdims=True) - xh * jnp.mean(dxh * xh, axis=-1, keepdims=True))
        dr_ref[...] = dr
        drb_ref[...] = dr.astype(BF16)
        dsum_ref[...] += jnp.sum(dr, axis=0, keepdims=True)

    row = pl.BlockSpec((tm, d), lambda i: (i, 0))
    par = pl.BlockSpec((1, d), lambda i: (0, 0))
    return pl.pallas_call(
        body, name="ln0_bwd",
        out_shape=(jax.ShapeDtypeStruct((t, d), F32), jax.ShapeDtypeStruct((t, d), BF16),
                   jax.ShapeDtypeStruct((1, d), F32), jax.ShapeDtypeStruct((1, d), F32),
                   jax.ShapeDtypeStruct((1, d), F32)),
        grid=(t // tm,),
        in_specs=[row, pl.BlockSpec((tm, kb), lambda i: (i, 0)),
                  pl.BlockSpec((d, kb), lambda i: (0, 0), pipeline_mode=pl.Buffered(1)),
                  pl.BlockSpec((tm, kd), lambda i: (i, 0)),
                  pl.BlockSpec((d, kd), lambda i: (0, 0), pipeline_mode=pl.Buffered(1)),
                  row, pl.BlockSpec((tm, 1), lambda i: (i, 0)), par]
                 + [pl.BlockSpec(memory_space=pl.ANY)] * len(extra),
        out_specs=(row, row, par, par, par),
        compiler_params=_cparams(("arbitrary",)),
    )(dr2, dprojb, w_in, dckv, w_down, xh, rstd, g, *extra)


def _conv_mid_bwd_rows(u1, proj, dr1b, w_out, ng, nb, after):
    t = u1.shape[0]
    e = CONV_WIDTH
    d = dr1b.shape[1]
    tm = min(t, 256)
    extra = [] if after is None else [after]

    rg = 16
    nt = t // tm

    def body(u1_ref, z_ref, dr_ref, w_ref, ng_ref, nb_ref, *rest):
        du1_ref, dz_ref, dng_ref, dnb_ref, dbz_ref, acc_ref, d_ref = rest[-7:]
        i = pl.program_id(0)

        @pl.when(i == 0)
        def _():
            acc_ref[...] = jnp.zeros_like(acc_ref)

        d_ref[...] = lax.dot_general(dr_ref[...], w_ref[...], NT, preferred_element_type=F32)
        g = ng_ref[...]
        b = nb_ref[...]

        def group(r, carry):
            rows = pl.ds(pl.multiple_of(r * rg, rg), rg)
            u1 = u1_ref[rows, :]
            mu = jnp.mean(u1, axis=-1, keepdims=True)
            xc = u1 - mu
            rstd = lax.rsqrt(jnp.mean(xc * xc, axis=-1, keepdims=True) + LN_EPS)
            xh = xc * rstd
            n = xh * g + b
            sn = _sigmoid(n)
            z = z_ref[rows, :]
            sz = _sigmoid(z)
            du2 = d_ref[rows, :]
            dz = du2 * (n * sn) * (sz * (1.0 + z * (1.0 - sz)))
            dn = du2 * (z * sz) * (sn * (1.0 + n * (1.0 - sn)))
            acc_ref[0:8, :] += _fold_rows(dn * xh)
            acc_ref[8:16, :] += _fold_rows(dn)
            acc_ref[16:24, :] += _fold_rows(dz)
            dz_ref[rows, :] = dz.astype(BF16)
            dxh = dn * g
            du1_ref[rows, :] = rstd * (dxh - jnp.mean(dxh, axis=-1, keepdims=True)
                                       - xh * jnp.mean(dxh * xh, axis=-1, keepdims=True))
            return carry

        lax.fori_loop(0, tm // rg, group, 0, unroll=8)

        @pl.when(i == nt - 1)
        def _():
            dng_ref[...] = jnp.sum(acc_ref[0:8, :], axis=0, keepdims=True)
            dnb_ref[...] = jnp.sum(acc_ref[8:16, :], axis=0, keepdims=True)
            dbz_ref[...] = jnp.sum(acc_ref[16:24, :], axis=0, keepdims=True)

    row = pl.BlockSpec((tm, e), lambda i: (i, 0))
    par = pl.BlockSpec((1, e), lambda i: (0, 0))
    return pl.pallas_call(
        body, name="conv_mid_bwd_rows",
        out_shape=(jax.ShapeDtypeStruct((t, e), F32), jax.ShapeDtypeStruct((t, 3 * e), BF16),
                   jax.ShapeDtypeStruct((1, e), F32), jax.ShapeDtypeStruct((1, e), F32),
                   jax.ShapeDtypeStruct((1, e), F32)),
        grid=(nt,),
        in_specs=[row, pl.BlockSpec((tm, e), lambda i: (i, 2)), pl.BlockSpec((tm, d), lambda i: (i, 0)),
                  pl.BlockSpec((e, d), lambda i: (0, 0), pipeline_mode=pl.Buffered(1)), par, par]
                 + [pl.BlockSpec(memory_space=pl.ANY)] * len(extra),
        out_specs=(row, pl.BlockSpec((tm, e), lambda i: (i, 2)), par, par, par),
        scratch_shapes=[pltpu.VMEM((24, e), F32), pltpu.VMEM((tm, e), F32)],
        compiler_params=_cparams(("arbitrary",)),
    )(u1, proj, dr1b, w_out, ng, nb, *extra)


def _conv_bwd(du1, proj, cw, dproj):
    t = du1.shape[0]
    e = CONV_WIDTH
    tm = min(t, 128)
    hb = tm // HALO
    nt = t // tm
    nchunk = e // LANES
    last_halo = t // HALO - 1

    def body(d_ref, dn_ref, val_ref, gg_ref, valh_ref, ggh_ref, cw_ref, alias_ref,
             dp_ref, dcw_ref, dcb_ref, dbv_ref, extu_ref, extd_ref, du0_ref, acc_ref, sh_ref):
        i = pl.program_id(0)
        h0 = valh_ref[...] * _sigmoid(ggh_ref[...])
        extu_ref[0:HALO, :] = jnp.where(i > 0, h0, 0.0)
        extu_ref[HALO:, :] = val_ref[...] * _sigmoid(gg_ref[...])
        extd_ref[0:tm, :] = d_ref[...]
        extd_ref[tm:, :] = jnp.where(i < nt - 1, dn_ref[...], 0.0)

        @pl.when(i == 0)
        def _():
            acc_ref[...] = jnp.zeros_like(acc_ref)

        def chunk(c, carry):
            col = pl.multiple_of(c * LANES, LANES)
            d = extd_ref[0:tm, pl.ds(col, LANES)]
            du0 = jnp.zeros((tm, LANES), F32)
            for res, taps in _taps_by_residue(lambda j: CONV_KERNEL - 1 - j):
                span = 8 * max(a for _, a in taps) + tm
                sh_ref[0:span, :] = extd_ref[res:res + span, pl.ds(col, LANES)]
                for j, a in taps:
                    du0 = du0 + cw_ref[j:j + 1, pl.ds(col, LANES)] * sh_ref[8 * a:8 * a + tm, :]
            for res, taps in _taps_by_residue(lambda j: HALO - (CONV_KERNEL - 1) + j):
                span = 8 * max(a for _, a in taps) + tm
                sh_ref[0:span, :] = extu_ref[res:res + span, pl.ds(col, LANES)]
                for j, a in taps:
                    prod = d * sh_ref[8 * a:8 * a + tm, :]
                    acc_ref[j * 8:(j + 1) * 8, pl.ds(col, LANES)] += _fold_rows(prod)
            acc_ref[CONV_KERNEL * 8:(CONV_KERNEL + 1) * 8, pl.ds(col, LANES)] += _fold_rows(d)
            du0_ref[:, pl.ds(col, LANES)] = du0
            return carry

        lax.fori_loop(0, nchunk, chunk, 0)
        kb = CONV_KERNEL * 8
        for r in range(0, tm, 16):
            sg = _sigmoid(gg_ref[r:r + 16, :])
            dval = du0_ref[r:r + 16, :] * sg
            dgg = dval * val_ref[r:r + 16, :] * (1.0 - sg)
            dp_ref[r:r + 16, 0:e] = dval.astype(BF16)
            dp_ref[r:r + 16, e:] = dgg.astype(BF16)
            acc_ref[kb + 8:kb + 16, :] += _fold_rows(dval)
            acc_ref[kb + 16:kb + 24, :] += _fold_rows(dgg)

        @pl.when(i == nt - 1)
        def _():
            for j in range(CONV_KERNEL):
                dcw_ref[j:j + 1, :] = jnp.sum(acc_ref[j * 8:(j + 1) * 8, :], axis=0, keepdims=True)
            dcw_ref[CONV_KERNEL:, :] = jnp.zeros((HALO - CONV_KERNEL, e), F32)
            dcb_ref[...] = jnp.sum(acc_ref[kb:kb + 8, :], axis=0, keepdims=True)
            dbv_ref[:, 0:e] = jnp.sum(acc_ref[kb + 8:kb + 16, :], axis=0, keepdims=True)
            dbv_ref[:, e:] = jnp.sum(acc_ref[kb + 16:kb + 24, :], axis=0, keepdims=True)

    row = lambda c: pl.BlockSpec((tm, e), lambda i: (i, c))
    halo_prev = lambda c: pl.BlockSpec((HALO, e), lambda i: (jnp.maximum(i * hb - 1, 0), c))
    halo_next = pl.BlockSpec((HALO, e), lambda i: (jnp.minimum((i + 1) * hb, last_halo), 0))
    return pl.pallas_call(
        body, name="conv_bwd",
        out_shape=(jax.ShapeDtypeStruct(dproj.shape, BF16), jax.ShapeDtypeStruct((HALO, e), F32),
                   jax.ShapeDtypeStruct((1, e), F32), jax.ShapeDtypeStruct((1, 2 * e), F32)),
        grid=(nt,),
        in_specs=[row(0), halo_next, row(0), row(1), halo_prev(0), halo_prev(1),
                  pl.BlockSpec((HALO, e), lambda i: (0, 0)), pl.BlockSpec(memory_space=pl.ANY)],
        out_specs=(pl.BlockSpec((tm, 2 * e), lambda i: (i, 0)), pl.BlockSpec((HALO, e), lambda i: (0, 0)),
                   pl.BlockSpec((1, e), lambda i: (0, 0)), pl.BlockSpec((1, 2 * e), lambda i: (0, 0))),
        scratch_shapes=[pltpu.VMEM((tm + HALO, e), F32), pltpu.VMEM((tm + HALO, e), F32),
                        pltpu.VMEM((tm, e), F32), pltpu.VMEM(((CONV_KERNEL + 3) * 8, e), F32),
                        pltpu.VMEM((tm + HALO, LANES), F32)],
        input_output_aliases={7: 0},
        compiler_params=_cparams(("arbitrary",)),
    )(du1, du1, proj, proj, proj, proj, cw, dproj)


def _adam_update(g, w, m, v):
    c1 = 1.0 - ADAM_B1 ** ADAM_STEP
    c2 = 1.0 - ADAM_B2 ** ADAM_STEP
    mn = ADAM_B1 * m + (1.0 - ADAM_B1) * g
    vn = ADAM_B2 * v + (1.0 - ADAM_B2) * (g * g)
    delta = -ADAM_LR * ((mn / c1) / (jnp.sqrt(vn / c2) + ADAM_EPS) + ADAM_WD * w)
    return delta, mn, vn


def _adamw(name, gbuf, w, m, v):
    parts = list(gbuf) if isinstance(gbuf, (list, tuple)) else [gbuf]
    npart = len(parts)
    r, c = w.shape
    tr = min(r // npart, 256)
    per_part = r // npart // tr
    assert r == npart * per_part * tr and all(p.shape == (N_DEV, r // npart, c) for p in parts)

    def body(*refs):
        g_refs = refs[:npart]
        w_ref, m_ref, v_ref, go_ref, d_ref, mo_ref, vo_ref = refs[npart:]
        i = pl.program_id(0)

        def run(g_ref):
            g = g_ref[0].astype(F32)
            for k in range(1, N_DEV):
                g = g + g_ref[k].astype(F32)
            go_ref[...] = g
            d_ref[...], mo_ref[...], vo_ref[...] = _adam_update(g, w_ref[...], m_ref[...], v_ref[...])

        if npart == 1:
            run(g_refs[0])
        else:
            for q in range(npart):
                pl.when((i >= q * per_part) & (i < (q + 1) * per_part))(functools.partial(run, g_refs[q]))

    blk = pl.BlockSpec((tr, c), lambda i: (i, 0))
    part_spec = lambda q: pl.BlockSpec((N_DEV, tr, c), lambda i: (0, jnp.clip(i - q * per_part, 0, per_part - 1), 0))
    shp = jax.ShapeDtypeStruct((r, c), F32)
    return pl.pallas_call(
        body, name="adamw_" + name,
        out_shape=(shp, shp, shp, shp),
        grid=(r // tr,),
        in_specs=[part_spec(q) for q in range(npart)] + [blk, blk, blk],
        out_specs=(blk, blk, blk, blk),
        compiler_params=_cparams(("parallel",)),
    )(*parts, w, m, v)


SMALL_TABLE_COLS = 256
SMALL_LAYOUT = {
    'ln_g': (0, 2, 1024), 'ln_b': (8, 2, 1024), 'a_b_in': (16, 1, 768), 'a_conv_b': (24, 1, 256),
    'a_norm_g': (32, 1, 256), 'a_norm_b': (40, 1, 256), 'a_b_out': (48, 1, 128), 'kv_norm_g': (56, 1, 256),
    'b_q_norm_g': (64, 1, 512), 'a_conv_w': (72, CONV_KERNEL, 256),
}
LOSS_ROW = 104
SMALL_TABLE_ROWS = 112
SMALL_NAMES = tuple(SMALL_LAYOUT)


def _small_pieces(name, row):
    r0, _, c = SMALL_LAYOUT[name]
    w = min(c, SMALL_TABLE_COLS)
    per_row = c // w
    return [(r0 + row * per_row + q, q * w, w) for q in range(per_row)]


def _pack_small_grads(dg0, dg1, db0, db1, dbv, dbz, dcb, dng, dnb, dba, dgkv, dgq, dcw, loss):
    e = CONV_WIDTH
    ce = e // N_DEV

    def body(dg0_ref, dg1_ref, db0_ref, db1_ref, dbv_ref, dbz_ref, dcb_ref, dng_ref, dnb_ref, dba_ref, dgkv_ref,
             dgq_ref, dcw_ref, loss_ref, o_ref):
        o_ref[...] = jnp.zeros_like(o_ref)
        for d in range(N_DEV):
            o_ref[d, LOSS_ROW:LOSS_ROW + 1, 0:LANES] = loss_ref[...]

        def put(d, name, row, src_ref, first_col=0):
            for trow, col, w in _small_pieces(name, row):
                o_ref[d, trow:trow + 1, 0:w] = src_ref[:, first_col + col:first_col + col + w]

        for d in range(N_DEV):
            put(d, 'ln_g', 0, dg0_ref)
            put(d, 'ln_g', 1, dg1_ref)
            put(d, 'ln_b', 0, db0_ref)
            put(d, 'ln_b', 1, db1_ref)
            for trow, col, w in _small_pieces('a_b_in', 0):
                gcol = d * 3 * ce + col
                src = dbv_ref[:, gcol:gcol + w] if gcol < 2 * e else dbz_ref[:, gcol - 2 * e:gcol - 2 * e + w]
                o_ref[d, trow:trow + 1, 0:w] = src
            put(d, 'a_conv_b', 0, dcb_ref, d * ce)
            put(d, 'a_norm_g', 0, dng_ref, d * ce)
            put(d, 'a_norm_b', 0, dnb_ref, d * ce)
            put(d, 'a_b_out', 0, dba_ref, d * LANES)
            put(d, 'kv_norm_g', 0, dgkv_ref)
            put(d, 'b_q_norm_g', 0, dgq_ref)
            r0 = SMALL_LAYOUT['a_conv_w'][0]
            o_ref[d, r0:r0 + HALO, 0:ce] = dcw_ref[:, d * ce:(d + 1) * ce]

    return pl.pallas_call(
        body, name="pack_small_grads",
        out_shape=jax.ShapeDtypeStruct((N_DEV, SMALL_TABLE_ROWS, SMALL_TABLE_COLS), F32),
        compiler_params=pltpu.CompilerParams(vmem_limit_bytes=VMEM_LIMIT),
    )(dg0, dg1, db0, db1, dbv, dbz, dcb, dng, dnb, dba, dgkv, dgq, dcw, loss)


def _adamw_small(gtab, ws, ms, vs):
    n = len(SMALL_NAMES)

    def body(*refs):
        g_ref = refs[0]
        w_refs, m_refs, v_refs = refs[1:1 + n], refs[1 + n:1 + 2 * n], refs[1 + 2 * n:1 + 3 * n]
        outs = refs[1 + 3 * n:]

        def summed(r0, r, c):
            g = g_ref[0, r0:r0 + r, 0:c]
            for k in range(1, N_DEV):
                g = g + g_ref[k, r0:r0 + r, 0:c]
            return g

        for i, name in enumerate(SMALL_NAMES):
            r0, r, c = SMALL_LAYOUT[name]
            if c <= SMALL_TABLE_COLS:
                blocks = [(slice(None), summed(r0, r, c))]
            else:
                blocks = [(slice(row, row + 1), jnp.concatenate([summed(tr, 1, w) for tr, _, w in _small_pieces(name, row)], axis=1))
                          for row in range(r)]
            for rows, g in blocks:
                delta, mn, vn = _adam_update(g, w_refs[i][rows, :], m_refs[i][rows, :], v_refs[i][rows, :])
                outs[i][rows, :] = g
                outs[n + i][rows, :] = delta
                outs[2 * n + i][rows, :] = mn
                outs[3 * n + i][rows, :] = vn
        outs[4 * n][...] = summed(LOSS_ROW, 1, LANES)

    shapes = tuple(jax.ShapeDtypeStruct(w.shape, F32) for w in ws)
    res = pl.pallas_call(
        body, name="adamw_small",
        out_shape=shapes * 4 + (jax.ShapeDtypeStruct((1, LANES), F32),),
        compiler_params=pltpu.CompilerParams(vmem_limit_bytes=VMEM_LIMIT),
    )(gtab, *ws, *ms, *vs)
    return res[:n], res[n:2 * n], res[2 * n:3 * n], res[3 * n:4 * n], res[4 * n]


def _mesh_peers():
    x, y, c = lax.axis_index("x"), lax.axis_index("y"), lax.axis_index("c")
    me = 4 * x + 2 * y + c
    peers = []
    for k in range(1, N_DEV):
        px = 1 - x if (k >> 2) & 1 else x
        py = 1 - y if (k >> 1) & 1 else y
        pc = 1 - c if k & 1 else c
        peers.append(((px, py, pc), 4 * px + 2 * py + pc))
    return me, peers


_HBM_SPEC = pl.BlockSpec(memory_space=pltpu.HBM)
_SEM_SPEC = pl.BlockSpec(memory_space=pltpu.SEMAPHORE)


ALL_PEERS = tuple(range(N_DEV - 1))
CHIP_LEVEL_PEERS = (0, 1, 3, 5)


def _split_copies(srcs, lands, send_sems, recv_sems, local_sems, gather, which=ALL_PEERS):
    me, peers = _mesh_peers()
    na = len(srcs)
    mine = lambda a, slot: srcs[a] if gather else srcs[a].at[slot]
    local = [pltpu.make_async_copy(mine(a, me), lands[a].at[me], local_sems.at[a]) for a in range(na)]
    sent, received = [], []
    for k, (dev, pid) in enumerate(peers):
        if k not in which:
            continue
        for a in range(na):
            s = a * (N_DEV - 1) + k
            sent.append(pltpu.make_async_remote_copy(
                src_ref=mine(a, pid), dst_ref=lands[a].at[me], send_sem=send_sems.at[s],
                recv_sem=recv_sems.at[s], device_id=dev, device_id_type=pl.DeviceIdType.MESH))
            received.append(pltpu.make_async_remote_copy(
                src_ref=mine(a, pid), dst_ref=lands[a].at[pid], send_sem=send_sems.at[s],
                recv_sem=recv_sems.at[s], device_id=dev, device_id_type=pl.DeviceIdType.MESH))
    return local, sent, received


def _comm_start(name, srcs, gather, after, which=ALL_PEERS):
    na = len(srcs)
    land_structs = [jax.ShapeDtypeStruct(((N_DEV,) + s.shape) if gather else s.shape, s.dtype) for s in srcs]
    extra = [] if after is None else [after]
    n_in = 2 * na + len(extra)

    def body(*refs):
        srcs_r, lands_r = refs[:na], refs[na:2 * na]
        send_sems, recv_sems, local_sems = refs[n_in:n_in + 3]
        token = refs[-1]
        local, sent, _ = _split_copies(srcs_r, lands_r, send_sems, recv_sems, local_sems, gather, which)
        for cp in local + sent:
            cp.start()
        token[...] = jnp.zeros_like(token)

    sem = pltpu.SemaphoreType.DMA((na * (N_DEV - 1),))
    outs = pl.pallas_call(
        body, name=name,
        out_shape=(sem, sem, pltpu.SemaphoreType.DMA((na,)),
                   *[pltpu.HBM(s.shape, s.dtype) for s in srcs],
                   *[pltpu.HBM(s.shape, s.dtype) for s in land_structs],
                   jax.ShapeDtypeStruct((8, LANES), F32)),
        in_specs=[_HBM_SPEC] * (2 * na) + [pl.BlockSpec(memory_space=pl.ANY)] * len(extra),
        out_specs=(_SEM_SPEC, _SEM_SPEC, _SEM_SPEC, *([_HBM_SPEC] * (2 * na)), pl.BlockSpec(memory_space=pltpu.VMEM)),
        input_output_aliases={i: 3 + i for i in range(2 * na)},
        compiler_params=pltpu.CompilerParams(has_side_effects=pltpu.SideEffectType.DATAFLOW_SIDE_EFFECTING),
    )(*[pltpu.with_memory_space_constraint(s, pltpu.HBM) for s in srcs],
      *[pltpu.with_memory_space_constraint(lax.empty(s.shape, s.dtype), pltpu.HBM) for s in land_structs],
      *extra)
    return (outs[:3], outs[3:3 + na], outs[3 + na:3 + 2 * na]), outs[-1]


def _comm_wait(name, handles, gather, after, which=ALL_PEERS):
    (send_sems, recv_sems, local_sems), srcs, lands = handles
    na = len(srcs)

    def body(*refs):
        srcs_r, lands_r = refs[:na], refs[na:2 * na]
        send_r, recv_r, local_r = refs[2 * na:2 * na + 3]
        local, sent, received = _split_copies(srcs_r, lands_r, send_r, recv_r, local_r, gather, which)
        for cp in sent:
            cp.wait_send()
        for cp in received:
            cp.wait_recv()
        for cp in local:
            cp.wait()

    outs = pl.pallas_call(
        body, name=name,
        out_shape=(*[pltpu.HBM(s.shape, s.dtype) for s in srcs], *[pltpu.HBM(s.shape, s.dtype) for s in lands]),
        in_specs=[_HBM_SPEC] * (2 * na) + [_SEM_SPEC] * 3 + [pl.BlockSpec(memory_space=pl.ANY)],
        out_specs=tuple([_HBM_SPEC] * (2 * na)),
        input_output_aliases={i: i for i in range(2 * na)},
        compiler_params=pltpu.CompilerParams(has_side_effects=pltpu.SideEffectType.DATAFLOW_SIDE_EFFECTING),
    )(*srcs, *lands, send_sems, recv_sems, local_sems, after)
    return outs[na:]


def _forward_to_sibling(name, lands):
    na = len(lands)

    def body(*refs):
        ins, outs = refs[:na], refs[na:2 * na]
        send_sems, recv_sems = refs[2 * na:]
        x, y, c = lax.axis_index("x"), lax.axis_index("y"), lax.axis_index("c")
        sibling = (x, y, 1 - c)
        chips = [(1 - x, y), (x, 1 - y), (1 - x, 1 - y)]
        slot = lambda cx, cy, cc: 4 * cx + 2 * cy + cc
        sends = []
        for j, (cx, cy) in enumerate(chips):
            for a in range(na):
                cp = pltpu.make_async_remote_copy(
                    src_ref=ins[a].at[slot(cx, cy, c)], dst_ref=outs[a].at[slot(cx, cy, c)],
                    send_sem=send_sems.at[a, j], recv_sem=recv_sems.at[a, j], device_id=sibling,
                    device_id_type=pl.DeviceIdType.MESH)
                cp.start()
                sends.append(cp)
        for j, (cx, cy) in enumerate(chips):
            for a in range(na):
                pltpu.make_async_remote_copy(
                    src_ref=ins[a].at[slot(cx, cy, 1 - c)], dst_ref=outs[a].at[slot(cx, cy, 1 - c)],
                    send_sem=send_sems.at[a, j], recv_sem=recv_sems.at[a, j], device_id=sibling,
                    device_id_type=pl.DeviceIdType.MESH).wait_recv()
        for cp in sends:
            cp.wait_send()

    hbm = pl.BlockSpec(memory_space=pl.ANY)
    return pl.pallas_call(
        body, name=name,
        out_shape=tuple(jax.ShapeDtypeStruct(s.shape, s.dtype) for s in lands),
        in_specs=[hbm] * na, out_specs=(hbm,) * na,
        input_output_aliases={a: a for a in range(na)},
        scratch_shapes=[pltpu.SemaphoreType.DMA((na, 3)), pltpu.SemaphoreType.DMA((na, 3))],
    )(*lands)


def _prep_weights(w):
    e = CONV_WIDTH
    p = {}
    if 'a_w_in' in w:
        if w['a_w_in'].shape == (N_DEV, D_MODEL, 3 * e // N_DEV):
            p['a_w_in3'] = w['a_w_in']
        else:
            p['a_w_in3'] = w['a_w_in'].reshape(D_MODEL, N_DEV, 3 * e // N_DEV).transpose(1, 0, 2)
        p['a_b_in'] = w['a_b_in'].reshape(1, 3 * e)
        p['a_conv_w'] = jnp.pad(w['a_conv_w'].reshape(CONV_KERNEL, e), ((0, HALO - CONV_KERNEL), (0, 0)))
        p['a_conv_b'] = w['a_conv_b'].reshape(1, e)
        p['a_norm_g'] = w['a_norm_g'].reshape(1, e)
        p['a_norm_b'] = w['a_norm_b'].reshape(1, e)
        p['a_b_out'] = w['a_b_out'].reshape(1, D_MODEL)
        p['kv_norm_g'] = w['kv_norm_g'].reshape(1, KV_LORA_RANK)
        p['b_q_norm_g'] = w['b_q_norm_g'].reshape(1, Q_LORA_RANK)
        p['ln_g'] = w['ln_g']
        p['ln_b'] = w['ln_b']
    if 'a_w_out' in w:
        p['a_w_out'] = w['a_w_out'].reshape(e, D_MODEL)
        p['kv_w_down'] = jnp.pad(w['kv_w_down'], ((0, 0), (0, KV_LORA_RANK + LANES - w['kv_w_down'].shape[1])))
        p['kv_w_ukv'] = jnp.concatenate([w['kv_w_uk'], w['kv_w_uv']], axis=2).reshape(KV_LORA_RANK, N_HEADS * HEAD_PAD)
        p['b_w_in'] = w['b_w_in'].reshape(D_MODEL, -1)
        uq = w['b_w_uq'].reshape(Q_LORA_RANK, N_HEADS, QK_NOPE_DIM + QK_ROPE_DIM)
        p['b_w_uq'] = jnp.pad(uq, ((0, 0), (0, 0), (0, HEAD_PAD - uq.shape[2]))).reshape(Q_LORA_RANK, N_HEADS * HEAD_PAD)
        p['b_w_out'] = w['b_w_out'].reshape(N_HEADS * V_HEAD_DIM, D_MODEL)
    return p


def _local_step(x, positions, target, comm):
    t = x.shape[0]
    e = CONV_WIDTH
    freqs = ROPE_THETA ** (-jnp.arange(0, QK_ROPE_DIM, 2, dtype=F32) / QK_ROPE_DIM)
    freq_row = jnp.concatenate([freqs, freqs, jnp.zeros((LANES - QK_ROPE_DIM,), F32)]).reshape(1, LANES)
    rc, rs, xb = _rope_tables(positions.reshape(t, 1), freq_row, x, comm.start_token())
    p = comm.first_weights(rc)
    ln_g0, ln_b0 = p['ln_g'][0:1], p['ln_b'][0:1]
    ln_g1, ln_b1 = p['ln_g'][1:2], p['ln_b'][1:2]

    proj = _matmul(xb, p['a_w_in3'], bias=p['a_b_in'], name="a_in", b_blocked=True, bm=2048,
                   after=comm.first_after())
    u1, u2 = _conv_mid_fwd(proj, p['a_conv_w'], p['a_conv_b'], p['a_norm_g'], p['a_norm_b'])
    p = {**p, **comm.rest_weights(u2)}
    h1, h1b, xh1, rstd1 = _ln_res_fwd(x, u2, p['a_w_out'], p['a_b_out'], ln_g0, ln_b0)
    ckv, ckn, kr, kv = _kv_mid_fwd(h1b, p['kv_w_down'], p['kv_w_ukv'], p['kv_norm_g'], rc, rs)
    projb, cqn = _b_in_qnorm(h1b, p['b_w_in'], p['b_q_norm_g'])
    qcat = _q_up_rope(cqn, p['b_w_uq'], rc, rs)
    o, o2, lse = _flash_fwd(qcat, kv, kr, projb)
    loss, dr2, dr2b, dg1, db1 = _final_ln_loss(h1, o2, p['b_w_out'], ln_g1, ln_b1, target)

    g = {}
    g['b_w_out'] = _matmul(o2, dr2b, trans_a=True, name="d_b_out", out_dtype=BF16, bm=512, bk=t)
    dob, dprojb = _gate_bwd(dr2b, p['b_w_out'], o, projb)
    dq2, dkv, dkr = _flash_bwd(qcat, kv, kr, dob, lse, o, rc, rs)
    duq = _matmul(cqn, dq2, trans_a=True, name="d_q_up", out_dtype=BF16, bk=t)
    dprojb, dgq = _q_norm_bwd(dq2, p['b_w_uq'], projb, p['b_q_norm_g'], dprojb)
    g['b_w_in'] = _matmul(h1b, dprojb, trans_a=True, name="d_b_in", bn=640, bk=t, out_dtype=BF16)
    dukv = _matmul(ckn, dkv, trans_a=True, name="d_kv_up", out_dtype=BF16, bk=t)
    dckv, dgkv = _kv_mid_bwd(dkv, p['kv_w_ukv'], ckv, dkr, p['kv_norm_g'], rc, rs)
    ddown = _matmul(h1b, dckv, trans_a=True, name="d_kv_down", out_dtype=BF16, bm=512, bk=t)
    g['b_w_out'] = g['b_w_out'].reshape(N_DEV, -1, D_MODEL)
    g['b_w_in'] = g['b_w_in'].reshape(D_MODEL, N_DEV, -1).transpose(1, 0, 2)
    g['kv_w_down'] = ddown[:, :KV_LORA_RANK + QK_ROPE_DIM].reshape(N_DEV, -1, KV_LORA_RANK + QK_ROPE_DIM)
    dukv = dukv.reshape(KV_LORA_RANK, N_HEADS, HEAD_PAD)
    g['kv_w_uk'] = dukv[:, :, :QK_NOPE_DIM].reshape(N_DEV, -1, QK_NOPE_DIM)
    g['kv_w_uv'] = dukv[:, :, QK_NOPE_DIM:].reshape(N_DEV, -1, V_HEAD_DIM)
    g['b_w_uq'] = duq.reshape(Q_LORA_RANK, N_HEADS, HEAD_PAD)[:, :, :QK_NOPE_DIM + QK_ROPE_DIM].reshape(
        N_DEV, -1, QK_NOPE_DIM + QK_ROPE_DIM)
    sent1 = comm.send_group1(g)
    dr1, dr1b, dg0, db0, dba_out = _ln0_bwd(dr2, dprojb, p['b_w_in'], dckv, p['kv_w_down'], xh1, rstd1, ln_g0, sent1)
    dw_out = _matmul(u2, dr1b, trans_a=True, name="d_a_out", out_dtype=BF16, bm=512, bk=t).reshape(N_DEV, -1, D_MODEL)
    sent1b = comm.send_group1b({'a_w_out': dw_out})
    du1, dproj, dng, dnb, dbz = _conv_mid_bwd_rows(u1, proj, dr1b, p['a_w_out'], p['a_norm_g'], p['a_norm_b'], sent1b)
    dproj, dcw, dcb, dbv = _conv_bwd(du1, proj, p['a_conv_w'], dproj)
    half = D_MODEL // 2
    dw_lo = _matmul(xb, dproj, trans_a=True, name="d_a_in_lo", out_dtype=BF16, bm=half, bn=3 * e // N_DEV, bk=t,
                    out_blocked=True, a_cols=(0, half))
    small = (dg0, dg1, db0, db1, dbv, dbz, dcb, dng, dnb, dba_out, dgkv, dgq, dcw, loss)
    sent2 = comm.send_group2(dw_lo, small)
    dw_hi = _matmul(xb, dproj, trans_a=True, name="d_a_in_hi", out_dtype=BF16, bm=half, bn=3 * e // N_DEV, bk=t,
                    out_blocked=True, after=sent2, a_cols=(half, half))
    sent3 = comm.send_group3(dw_hi)
    grad_x = _grad_x(dproj, p['a_w_in3'], dr1, sent3)
    return loss, grad_x


def _grad_x(dproj, w3, dr1, after):
    t, d = dr1.shape
    nblk, _, cb = w3.shape
    tm = min(t, 512)
    extra = [] if after is None else [after]

    def body(a_ref, w_ref, dr_ref, *rest):
        o_ref = rest[-1]
        acc = ALPHA * dr_ref[...]
        for k in range(nblk):
            acc = acc + lax.dot_general(a_ref[:, k * cb:(k + 1) * cb], w_ref[k], NT, preferred_element_type=F32)
        o_ref[...] = acc

    row = pl.BlockSpec((tm, d), lambda i: (i, 0))
    return pl.pallas_call(
        body, name="grad_x",
        out_shape=jax.ShapeDtypeStruct((t, d), F32),
        grid=(t // tm,),
        in_specs=[pl.BlockSpec((tm, nblk * cb), lambda i: (i, 0)),
                  pl.BlockSpec((nblk, d, cb), lambda i: (0, 0, 0), pipeline_mode=pl.Buffered(1)), row]
                 + [pl.BlockSpec(memory_space=pl.ANY)] * len(extra),
        out_specs=row,
        compiler_params=_cparams(("parallel",)),
    )(dproj, w3, dr1, *extra)


def _shard_2d(a):
    return a.reshape(-1, a.shape[-1])


def _gathered_to_full(name, blocks):
    if name == 'a_w_in':
        return blocks
    if name == 'b_w_in':
        return blocks.transpose(1, 0, 2).reshape(D_MODEL, -1)
    if name == 'a_conv_w':
        return blocks.transpose(1, 0, 2).reshape(CONV_KERNEL, -1)
    if name in ('a_b_in', 'a_conv_b', 'a_norm_g', 'a_norm_b', 'a_b_out'):
        return blocks.reshape(-1)
    if name in ('kv_w_uk', 'kv_w_uv'):
        return blocks.reshape(KV_LORA_RANK, N_HEADS, -1)
    if name == 'b_w_uq':
        return blocks.reshape(Q_LORA_RANK, N_HEADS, -1)
    return blocks.reshape(-1, blocks.shape[-1])


def kernel(x, positions, ln_g, ln_b, a_w_in, a_b_in, a_conv_w, a_conv_b, a_norm_g, a_norm_b, a_w_out, a_b_out, kv_w_down, kv_norm_g, kv_w_uk, kv_w_uv, b_w_in, b_q_norm_g, b_w_uq, b_w_out, loss_target, m_ln_g, m_ln_b, m_a_w_in, m_a_b_in, m_a_conv_w, m_a_conv_b, m_a_norm_g, m_a_norm_b, m_a_w_out, m_a_b_out, m_kv_w_down, m_kv_norm_g, m_kv_w_uk, m_kv_w_uv, m_b_w_in, m_b_q_norm_g, m_b_w_uq, m_b_w_out, v_ln_g, v_ln_b, v_a_w_in, v_a_b_in, v_a_conv_w, v_a_conv_b, v_a_norm_g, v_a_norm_b, v_a_w_out, v_a_b_out, v_kv_w_down, v_kv_norm_g, v_kv_w_uk, v_kv_w_uv, v_b_w_in, v_b_q_norm_g, v_b_w_uq, v_b_w_out):
    w = dict(ln_g=ln_g, ln_b=ln_b, a_w_in=a_w_in, a_b_in=a_b_in, a_conv_w=a_conv_w, a_conv_b=a_conv_b,
             a_norm_g=a_norm_g, a_norm_b=a_norm_b, a_w_out=a_w_out, a_b_out=a_b_out, kv_w_down=kv_w_down,
             kv_norm_g=kv_norm_g, kv_w_uk=kv_w_uk, kv_w_uv=kv_w_uv, b_w_in=b_w_in, b_q_norm_g=b_q_norm_g,
             b_w_uq=b_w_uq, b_w_out=b_w_out)
    m = dict(ln_g=m_ln_g, ln_b=m_ln_b, a_w_in=m_a_w_in, a_b_in=m_a_b_in, a_conv_w=m_a_conv_w, a_conv_b=m_a_conv_b,
             a_norm_g=m_a_norm_g, a_norm_b=m_a_norm_b, a_w_out=m_a_w_out, a_b_out=m_a_b_out, kv_w_down=m_kv_w_down,
             kv_norm_g=m_kv_norm_g, kv_w_uk=m_kv_w_uk, kv_w_uv=m_kv_w_uv, b_w_in=m_b_w_in, b_q_norm_g=m_b_q_norm_g,
             b_w_uq=m_b_w_uq, b_w_out=m_b_w_out)
    v = dict(ln_g=v_ln_g, ln_b=v_ln_b, a_w_in=v_a_w_in, a_b_in=v_a_b_in, a_conv_w=v_a_conv_w, a_conv_b=v_a_conv_b,
             a_norm_g=v_a_norm_g, a_norm_b=v_a_norm_b, a_w_out=v_a_w_out, a_b_out=v_a_b_out, kv_w_down=v_kv_w_down,
             kv_norm_g=v_kv_norm_g, kv_w_uk=v_kv_w_uk, kv_w_uv=v_kv_w_uv, b_w_in=v_b_w_in, b_q_norm_g=v_b_q_norm_g,
             b_w_uq=v_b_w_uq, b_w_out=v_b_w_out)

    small_flat = jnp.concatenate([w[n].reshape(-1) for n in SMALL_WEIGHTS]).reshape(1, -1)
    rest_names = [n for n in MATMUL_WEIGHTS if n != 'a_w_in']
    group1 = ('b_w_out', 'b_w_uq', 'b_w_in', 'kv_w_uk', 'kv_w_uv', 'kv_w_down')
    group1b = ('a_w_out',)

    class Comm:
        def __init__(self):
            self.first, self.first_token = _comm_start(
                "gather_first_start", [_shard_2d(w['a_w_in']).astype(BF16), small_flat], True, None, CHIP_LEVEL_PEERS)

        def start_token(self):
            return self.first_token

        def first_weights(self, after):
            lands = _comm_wait("gather_first_wait", self.first, True, after, CHIP_LEVEL_PEERS)
            ga = _forward_to_sibling("gather_first_forward", list(lands))
            full = {'a_w_in': ga[0]}
            gs = ga[1].reshape(N_DEV, -1)
            off = 0
            for n in SMALL_WEIGHTS:
                size = math.prod(w[n].shape)
                full[n] = _gathered_to_full(n, gs[:, off:off + size].reshape((N_DEV,) + _shard_2d(w[n]).shape))
                off += size
            for n in REPLICATED:
                full[n] = w[n]
            dense = lambda a: a if a.shape[-1] % LANES == 0 else a.reshape(-1, LANES)
            self.rest, self.rest_token = _comm_start(
                "gather_rest_start", [dense(_shard_2d(w[n]).astype(BF16)) for n in rest_names], True, ga[0])
            return _prep_weights(full)

        def first_after(self):
            return self.rest_token

        def rest_weights(self, after):
            lands = _comm_wait("gather_rest_wait", self.rest, True, after)
            blocks = {n: lands[i].reshape((N_DEV,) + _shard_2d(w[n]).shape) for i, n in enumerate(rest_names)}
            return _prep_weights({n: _gathered_to_full(n, blocks[n]) for n in rest_names})

        def send_group1(self, g):
            self.g1, token = _comm_start("exchange_g1_start", [g[n] for n in group1], False, None)
            return token

        def send_group1b(self, g):
            self.g1b, token = _comm_start("exchange_g1b_start", [g[n] for n in group1b], False, None)
            return token

        def send_group2(self, dw_lo, small):
            self.g2, token = _comm_start("exchange_g2_start", [dw_lo, _pack_small_grads(*small)], False, None)
            return token

        def send_group3(self, dw_hi):
            self.g3, token = _comm_start("exchange_g3_start", [dw_hi], False, None)
            return token

    comm = Comm()

    _, grad_x = _local_step(x[0], positions[0], loss_target[0], comm)

    res = {}
    recv1 = _comm_wait("exchange_g1_wait", comm.g1, False, grad_x)
    for i, n in enumerate(group1):
        res[n] = _adamw(n, recv1[i], _shard_2d(w[n]), _shard_2d(m[n]), _shard_2d(v[n]))
    recv1b = _comm_wait("exchange_g1b_wait", comm.g1b, False, res[group1[-1]][0])
    for i, n in enumerate(group1b):
        res[n] = _adamw(n, recv1b[i], _shard_2d(w[n]), _shard_2d(m[n]), _shard_2d(v[n]))
    recv2 = _comm_wait("exchange_g2_wait", comm.g2, False, res[group1b[-1]][0])
    recv3 = _comm_wait("exchange_g3_wait", comm.g3, False, recv2[0])
    res['a_w_in'] = _adamw('a_w_in', [recv2[0], recv3[0]], _shard_2d(w['a_w_in']), _shard_2d(m['a_w_in']),
                           _shard_2d(v['a_w_in']))
    two_d = lambda d: [_shard_2d(d[n]) if d[n].ndim > 1 else d[n].reshape(1, -1) for n in SMALL_NAMES]
    sg, sd, sm, sv, loss = _adamw_small(recv2[-1], two_d(w), two_d(m), two_d(v))
    for i, n in enumerate(SMALL_NAMES):
        res[n] = (sg[i], sd[i], sm[i], sv[i])
    outs = [[res[n][j].reshape(w[n].shape) for n in WEIGHT_NAMES] for j in range(4)]
    return (loss[0, 0], grad_x[None], *outs[0], *outs[1], *outs[2], *outs[3])
```

```python
import functools
import math

import jax
import jax.numpy as jnp
from jax import lax
from jax.experimental import pallas as pl
from jax.experimental.pallas import tpu as pltpu

F32 = jnp.float32
BF16 = jnp.bfloat16

D_MODEL = 1024
DEPTH = 2
CONV_WIDTH = 2 * D_MODEL
CONV_KERNEL = 31
N_HEADS = 16
QK_NOPE_DIM = 128
QK_ROPE_DIM = 64
V_HEAD_DIM = 128
KV_LORA_RANK = D_MODEL // 4
Q_LORA_RANK = D_MODEL // 2
ROPE_THETA = 10000.0
LN_EPS = 1e-5
RMS_EPS = 1e-6
MASK_VALUE = -1e30
ALPHA = (2.0 * DEPTH) ** 0.25
ATTN_SCALE = 1.0 / math.sqrt(QK_NOPE_DIM + QK_ROPE_DIM)

ADAM_LR = 0.001
ADAM_B1 = 0.9
ADAM_B2 = 0.999
ADAM_EPS = 1e-08
ADAM_WD = 0.01
ADAM_STEP = 10

N_DEV = 8
LANES = 128
HEAD_PAD = 256
HALO = 32
VMEM_LIMIT = 56 * 1024 * 1024

WEIGHT_NAMES = ['ln_g', 'ln_b', 'a_w_in', 'a_b_in', 'a_conv_w', 'a_conv_b', 'a_norm_g', 'a_norm_b',
                'a_w_out', 'a_b_out', 'kv_w_down', 'kv_norm_g', 'kv_w_uk', 'kv_w_uv', 'b_w_in',
                'b_q_norm_g', 'b_w_uq', 'b_w_out']
REPLICATED = ('ln_g', 'ln_b', 'kv_norm_g', 'b_q_norm_g')
MATMUL_WEIGHTS = ('a_w_in', 'a_w_out', 'kv_w_down', 'kv_w_uk', 'kv_w_uv', 'b_w_in', 'b_w_uq', 'b_w_out')
SMALL_WEIGHTS = ('a_b_in', 'a_conv_w', 'a_conv_b', 'a_norm_g', 'a_norm_b', 'a_b_out')

NN = (((1,), (0,)), ((), ()))
NT = (((1,), (1,)), ((), ()))
TN = (((0,), (0,)), ((), ()))


def _cparams(sem):
    return pltpu.CompilerParams(dimension_semantics=sem, vmem_limit_bytes=VMEM_LIMIT)


def _sigmoid(x):
    return 0.5 * jnp.tanh(0.5 * x) + 0.5


def _fold_rows(x):
    parts = [x[r:r + 8] for r in range(0, x.shape[0], 8)]
    while len(parts) > 1:
        parts = [parts[i] + parts[i + 1] for i in range(0, len(parts) - 1, 2)] + ([parts[-1]] if len(parts) % 2 else [])
    return parts[0]


def _taps_by_residue(offset_of):
    groups = {}
    for j in range(CONV_KERNEL):
        off = offset_of(j)
        groups.setdefault(off % 8, []).append((j, off // 8))
    return sorted(groups.items())


def _swap_rope_halves(t):
    lane = lax.broadcasted_iota(jnp.int32, t.shape, 1)
    return jnp.where(lane < QK_ROPE_DIM // 2, pltpu.roll(t, LANES - QK_ROPE_DIM // 2, 1),
                     pltpu.roll(t, QK_ROPE_DIM // 2, 1))


def _matmul(a, b, *, name, bias=None, trans_a=False, trans_b=False, out_dtype=F32, bm=1024, bn=1024, bk=1024,
            b_blocked=False, out_blocked=False, out_parts=1, after=None, a_cols=None):
    m_off = 0
    if trans_a:
        kdim, m = a.shape
        if a_cols is not None:
            m_off, m = a_cols
    else:
        m, kdim = a.shape
    if b_blocked and trans_b:
        n, k2 = b.shape[1], b.shape[0] * b.shape[2]
        bk = b.shape[2]
    elif b_blocked:
        k2, n = b.shape[1], b.shape[0] * b.shape[2]
        bn = b.shape[2]
    elif trans_b:
        n, k2 = b.shape
    else:
        k2, n = b.shape
    assert kdim == k2, (a.shape, b.shape)
    bm, bn, bk = min(bm, m), min(bn, n), min(bk, kdim)
    assert m % bm == 0 and n % bn == 0 and kdim % bk == 0, (name, m, n, kdim, bm, bn, bk)
    nk = kdim // bk
    assert out_parts == 1 or (out_blocked and bn % out_parts == 0)
    pw = bn // out_parts
    dims = (((0 if trans_a else 1,), (1 if trans_b else 0,)), ((), ()))
    has_bias = bias is not None

    def body(*refs):
        refs = list(refs)
        a_ref, b_ref = refs[:2]
        del refs[:2]
        bias_ref = refs.pop(0) if has_bias else None
        if after is not None:
            refs.pop(0)
        o_ref = refs.pop(0)
        rest = refs
        prod = lax.dot_general(a_ref[...], b_ref[...], dims, preferred_element_type=F32)

        def finish(acc):
            if has_bias:
                acc = acc + bias_ref[...]
            if out_parts == 1:
                o_ref[...] = acc.astype(out_dtype)
            else:
                for q in range(out_parts):
                    o_ref[q] = acc[:, q * pw:(q + 1) * pw].astype(out_dtype)

        if nk == 1:
            finish(prod)
        else:
            acc_ref = rest[0]
            k = pl.program_id(2)

            @pl.when(k == 0)
            def _():
                acc_ref[...] = prod

            @pl.when(k > 0)
            def _():
                acc_ref[...] += prod

            @pl.when(k == nk - 1)
            def _():
                finish(acc_ref[...])

    assert m_off % bm == 0
    a_first = m_off // bm
    a_spec = (pl.BlockSpec((bk, bm), lambda i, j, k: (k, i + a_first)) if trans_a
              else pl.BlockSpec((bm, bk), lambda i, j, k: (i, k)))
    if b_blocked and trans_b:
        b_spec = pl.BlockSpec((None, bn, bk), lambda i, j, k: (k, j, 0))
    elif b_blocked:
        b_spec = pl.BlockSpec((None, bk, bn), lambda i, j, k: (j, k, 0))
    elif trans_b:
        b_spec = pl.BlockSpec((bn, bk), lambda i, j, k: (j, k))
    else:
        b_spec = pl.BlockSpec((bk, bn), lambda i, j, k: (k, j))
    in_specs = [a_spec, b_spec]
    args = [a, b]
    if has_bias:
        in_specs.append(pl.BlockSpec((1, bn), lambda i, j, k: (0, j)))
        args.append(bias)
    if after is not None:
        in_specs.append(pl.BlockSpec(memory_space=pl.ANY))
        args.append(after)
    if out_blocked and out_parts > 1:
        out_struct = jax.ShapeDtypeStruct((n // pw, m, pw), out_dtype)
        out_spec = pl.BlockSpec((out_parts, bm, pw), lambda i, j, k: (j, i, 0))
    elif out_blocked:
        out_struct = jax.ShapeDtypeStruct((n // bn, m, bn), out_dtype)
        out_spec = pl.BlockSpec((None, bm, bn), lambda i, j, k: (j, i, 0))
    else:
        out_struct = jax.ShapeDtypeStruct((m, n), out_dtype)
        out_spec = pl.BlockSpec((bm, bn), lambda i, j, k: (i, j))
    return pl.pallas_call(
        body, name=name,
        out_shape=out_struct,
        grid=(m // bm, n // bn, nk),
        in_specs=in_specs,
        out_specs=out_spec,
        scratch_shapes=[pltpu.VMEM((bm, bn), F32)] if nk > 1 else [],
        compiler_params=_cparams(("parallel", "parallel", "arbitrary")),
    )(*args)


def _rope_tables(pos_col, freq_row, x, after):
    t, d = x.shape
    tm = min(t, 512)
    extra = [] if after is None else [after]

    def body(pos_ref, f_ref, x_ref, *rest):
        c_ref, s_ref, xb_ref = rest[-3:]
        ang = pos_ref[...].astype(F32) * f_ref[...]
        lane = lax.broadcasted_iota(jnp.int32, ang.shape, 1)
        cos = jnp.cos(ang)
        sin = jnp.sin(ang)
        c_ref[...] = jnp.where(lane < QK_ROPE_DIM, cos, 0.0)
        s_ref[...] = jnp.where(lane < QK_ROPE_DIM // 2, -sin, jnp.where(lane < QK_ROPE_DIM, sin, 0.0))
        xb_ref[...] = x_ref[...].astype(BF16)

    lane_blk = pl.BlockSpec((tm, LANES), lambda i: (i, 0))
    row = pl.BlockSpec((tm, d), lambda i: (i, 0))
    return pl.pallas_call(
        body, name="rope_tables",
        out_shape=(jax.ShapeDtypeStruct((t, LANES), F32), jax.ShapeDtypeStruct((t, LANES), F32),
                   jax.ShapeDtypeStruct((t, d), BF16)),
        grid=(t // tm,),
        in_specs=[pl.BlockSpec((tm, 1), lambda i: (i, 0)), pl.BlockSpec((1, LANES), lambda i: (0, 0)), row]
                 + [pl.BlockSpec(memory_space=pl.ANY)] * len(extra),
        out_specs=(lane_blk, lane_blk, row),
        compiler_params=_cparams(("parallel",)),
    )(pos_col, freq_row, x, *extra)


def _conv_mid_fwd(proj, cw, cb, ng, nb):
    t = proj.shape[0]
    e = CONV_WIDTH
    tm = min(t, 128)
    hb = tm // HALO
    nchunk = e // LANES

    def body(val_ref, gg_ref, z_ref, valh_ref, ggh_ref, cw_ref, cb_ref, ng_ref, nb_ref, u1_ref, u2_ref, ext_ref, sh_ref):
        i = pl.program_id(0)
        u0 = val_ref[...] * _sigmoid(gg_ref[...])
        h0 = valh_ref[...] * _sigmoid(ggh_ref[...])
        ext_ref[0:HALO, :] = jnp.where(i > 0, h0, 0.0)
        ext_ref[HALO:, :] = u0

        def chunk(c, carry):
            col = pl.multiple_of(c * LANES, LANES)
            acc = jnp.zeros((tm, LANES), F32)
            for res, taps in _taps_by_residue(lambda j: HALO - (CONV_KERNEL - 1) + j):
                span = 8 * max(a for _, a in taps) + tm
                sh_ref[0:span, :] = ext_ref[res:res + span, pl.ds(col, LANES)]
                for j, a in taps:
                    acc = acc + cw_ref[j:j + 1, pl.ds(col, LANES)] * sh_ref[8 * a:8 * a + tm, :]
            u1_ref[:, pl.ds(col, LANES)] = acc + cb_ref[:, pl.ds(col, LANES)]
            return carry

        lax.fori_loop(0, nchunk, chunk, 0)
        g = ng_ref[...]
        b = nb_ref[...]
        for r in range(0, tm, 16):
            u1 = u1_ref[r:r + 16, :]
            mu = jnp.mean(u1, axis=-1, keepdims=True)
            xc = u1 - mu
            var = jnp.mean(xc * xc, axis=-1, keepdims=True)
            n = xc * lax.rsqrt(var + LN_EPS) * g + b
            z = z_ref[r:r + 16, :]
            u2_ref[r:r + 16, :] = ((n * _sigmoid(n)) * (z * _sigmoid(z))).astype(BF16)

    row = lambda c: pl.BlockSpec((tm, e), lambda i: (i, c))
    halo = lambda c: pl.BlockSpec((HALO, e), lambda i: (jnp.maximum(i * hb - 1, 0), c))
    par = lambda r: pl.BlockSpec((r, e), lambda i: (0, 0))
    return pl.pallas_call(
        body, name="conv_mid_fwd",
        out_shape=(jax.ShapeDtypeStruct((t, e), F32), jax.ShapeDtypeStruct((t, e), BF16)),
        grid=(t // tm,),
        in_specs=[row(0), row(1), row(2), halo(0), halo(1), par(HALO), par(1), par(1), par(1)],
        out_specs=(pl.BlockSpec((tm, e), lambda i: (i, 0)), pl.BlockSpec((tm, e), lambda i: (i, 0))),
        scratch_shapes=[pltpu.VMEM((tm + HALO, e), F32), pltpu.VMEM((tm + HALO, LANES), F32)],
        compiler_params=_cparams(("parallel",)),
    )(proj, proj, proj, proj, proj, cw, cb, ng, nb)


def _ln_res_fwd(x, u, w_out, b_out, g, b):
    t, d = x.shape
    ku = u.shape[1]
    tm = min(t, 512)

    def body(x_ref, u_ref, w_ref, bo_ref, g_ref, b_ref, h_ref, hb_ref, xh_ref, rs_ref):
        y = lax.dot_general(u_ref[...], w_ref[...], NN, preferred_element_type=F32) + bo_ref[...]
        r = ALPHA * x_ref[...] + y
        mu = jnp.mean(r, axis=-1, keepdims=True)
        xc = r - mu
        var = jnp.mean(xc * xc, axis=-1, keepdims=True)
        rstd = lax.rsqrt(var + LN_EPS)
        xh = xc * rstd
        h = xh * g_ref[...] + b_ref[...]
        h_ref[...] = h
        hb_ref[...] = h.astype(BF16)
        xh_ref[...] = xh
        rs_ref[...] = rstd

    row = pl.BlockSpec((tm, d), lambda i: (i, 0))
    par = pl.BlockSpec((1, d), lambda i: (0, 0))
    return pl.pallas_call(
        body, name="ln0_fwd",
        out_shape=(jax.ShapeDtypeStruct((t, d), F32), jax.ShapeDtypeStruct((t, d), BF16),
                   jax.ShapeDtypeStruct((t, d), F32), jax.ShapeDtypeStruct((t, 1), F32)),
        grid=(t // tm,),
        in_specs=[row, pl.BlockSpec((tm, ku), lambda i: (i, 0)),
                  pl.BlockSpec((ku, d), lambda i: (0, 0), pipeline_mode=pl.Buffered(1)), par, par, par],
        out_specs=(row, row, row, pl.BlockSpec((tm, 1), lambda i: (i, 0))),
        compiler_params=_cparams(("parallel",)),
    )(x, u, w_out, b_out, g, b)


def _kv_mid_fwd(h1b, w_down, w_ukv, g, rc, rs):
    t, d = h1b.shape
    nkv = w_ukv.shape[1]
    tm = min(t, 512)
    r = KV_LORA_RANK

    def body(h_ref, w_ref, wu_ref, g_ref, c_ref, s_ref, ckv_ref, ckn_ref, kr_ref, kv_ref):
        ckv = lax.dot_general(h_ref[...], w_ref[...], NN, preferred_element_type=F32)
        ckv_ref[...] = ckv
        c = ckv[:, 0:r]
        ms = jnp.mean(c * c, axis=-1, keepdims=True)
        ckn = (c * lax.rsqrt(ms + RMS_EPS) * g_ref[...]).astype(BF16)
        ckn_ref[...] = ckn
        tr = ckv[:, r:r + LANES]
        kr_ref[...] = (tr * c_ref[...] + _swap_rope_halves(tr) * s_ref[...]).astype(BF16)
        kv_ref[...] = lax.dot_general(ckn, wu_ref[...], NN, preferred_element_type=F32).astype(BF16)

    lane_blk = pl.BlockSpec((tm, LANES), lambda i: (i, 0))
    return pl.pallas_call(
        body, name="kv_mid_fwd",
        out_shape=(jax.ShapeDtypeStruct((t, r + LANES), F32), jax.ShapeDtypeStruct((t, r), BF16),
                   jax.ShapeDtypeStruct((t, LANES), BF16), jax.ShapeDtypeStruct((t, nkv), BF16)),
        grid=(t // tm,),
        in_specs=[pl.BlockSpec((tm, d), lambda i: (i, 0)),
                  pl.BlockSpec((d, r + LANES), lambda i: (0, 0), pipeline_mode=pl.Buffered(1)),
                  pl.BlockSpec((r, nkv), lambda i: (0, 0), pipeline_mode=pl.Buffered(1)),
                  pl.BlockSpec((1, r), lambda i: (0, 0)), lane_blk, lane_blk],
        out_specs=(pl.BlockSpec((tm, r + LANES), lambda i: (i, 0)), pl.BlockSpec((tm, r), lambda i: (i, 0)), lane_blk,
                   pl.BlockSpec((tm, nkv), lambda i: (i, 0))),
        compiler_params=_cparams(("parallel",)),
    )(h1b, w_down, w_ukv, g, rc, rs)


def _b_in_qnorm(h1b, w_blocks, g):
    t, d = h1b.shape
    nblk, _, cb = w_blocks.shape
    n = nblk * cb
    r = Q_LORA_RANK
    tm = min(t, 512)

    def body(a_ref, wb_ref, g_ref, o_ref, cqn_ref, w_ref):
        @pl.when(pl.program_id(0) == 0)
        def _():
            for j in range(nblk):
                w_ref[:, j * cb:(j + 1) * cb] = wb_ref[j]

        acc = lax.dot_general(a_ref[...], w_ref[...], NN, preferred_element_type=F32)
        o_ref[...] = acc
        c = acc[:, 0:r]
        ms = jnp.mean(c * c, axis=-1, keepdims=True)
        cqn_ref[...] = (c * lax.rsqrt(ms + RMS_EPS) * g_ref[...]).astype(BF16)

    return pl.pallas_call(
        body, name="b_in",
        out_shape=(jax.ShapeDtypeStruct((t, n), F32), jax.ShapeDtypeStruct((t, r), BF16),
                   jax.ShapeDtypeStruct((d, n), BF16)),
        grid=(t // tm,),
        in_specs=[pl.BlockSpec((tm, d), lambda i: (i, 0)),
                  pl.BlockSpec((nblk, d, cb), lambda i: (0, 0, 0), pipeline_mode=pl.Buffered(1)),
                  pl.BlockSpec((1, r), lambda i: (0, 0))],
        out_specs=(pl.BlockSpec((tm, n), lambda i: (i, 0)), pl.BlockSpec((tm, r), lambda i: (i, 0)),
                   pl.BlockSpec((d, n), lambda i: (0, 0))),
        compiler_params=_cparams(("arbitrary",)),
    )(h1b, w_blocks, g)


def _q_up_rope(cqn, wuq, rc, rs):
    t, r = cqn.shape
    w = wuq.shape[1]
    tm = min(t, 1024)
    bn = 4 * HEAD_PAD

    def body(a_ref, b_ref, c_ref, s_ref, o_ref):
        q = lax.dot_general(a_ref[...], b_ref[...], NN, preferred_element_type=F32)
        c = c_ref[...]
        s = s_ref[...]
        for h in range(bn // HEAD_PAD):
            a = h * HEAD_PAD
            o_ref[:, a:a + LANES] = q[:, a:a + LANES].astype(BF16)
            tr = q[:, a + LANES:a + 2 * LANES]
            o_ref[:, a + LANES:a + 2 * LANES] = (tr * c + _swap_rope_halves(tr) * s).astype(BF16)

    return pl.pallas_call(
        body, name="q_up_rope",
        out_shape=jax.ShapeDtypeStruct((t, w), BF16),
        grid=(t // tm, w // bn),
        in_specs=[pl.BlockSpec((tm, r), lambda i, j: (i, 0)), pl.BlockSpec((r, bn), lambda i, j: (0, j)),
                  pl.BlockSpec((tm, LANES), lambda i, j: (i, 0)), pl.BlockSpec((tm, LANES), lambda i, j: (i, 0))],
        out_specs=pl.BlockSpec((tm, bn), lambda i, j: (i, j)),
        compiler_params=_cparams(("parallel", "parallel")),
    )(cqn, wuq, rc, rs)


def _flash_fwd(qcat, kv, kr, projb):
    t = qcat.shape[0]
    tq = min(t, 512)
    nq = t // tq
    exp2_scale = ATTN_SCALE * math.log2(math.e)
    z_first = Q_LORA_RANK // LANES

    def body(q_ref, kv_ref, kr_ref, z_ref, o_ref, o2_ref, lse_ref, kcat_ref):
        kcat_ref[:, 0:LANES] = kv_ref[:, 0:LANES]
        kcat_ref[:, LANES:] = kr_ref[...]
        rows = lax.broadcasted_iota(jnp.int32, (tq, tq), 0)
        cols = lax.broadcasted_iota(jnp.int32, (tq, tq), 1)
        for i in range(nq):
            a = i * tq
            q = q_ref[a:a + tq, :]
            sd = lax.dot_general(q, kcat_ref[a:a + tq, :], NT, preferred_element_type=F32)
            sd = jnp.where(cols <= rows, sd, MASK_VALUE)
            m = jnp.max(sd, axis=1, keepdims=True)
            if i > 0:
                sp = lax.dot_general(q, kcat_ref[0:a, :], NT, preferred_element_type=F32)
                m = jnp.maximum(m, jnp.max(sp, axis=1, keepdims=True))
            pd = jnp.exp2((sd - m) * exp2_scale)
            l = jnp.sum(pd, axis=1, keepdims=True)
            acc = lax.dot_general(pd.astype(BF16), kv_ref[a:a + tq, LANES:], NN, preferred_element_type=F32)
            if i > 0:
                pp = jnp.exp2((sp - m) * exp2_scale)
                l = l + jnp.sum(pp, axis=1, keepdims=True)
                acc = acc + lax.dot_general(pp.astype(BF16), kv_ref[0:a, LANES:], NN, preferred_element_type=F32)
            o = acc / l
            z = z_ref[a:a + tq, :]
            o_ref[a:a + tq, :] = o
            o2_ref[a:a + tq, :] = (o * (z * _sigmoid(z))).astype(BF16)
            lse_ref[a:a + tq, :] = jnp.broadcast_to(m * ATTN_SCALE + jnp.log(l), (tq, LANES))

    hw = N_HEADS * LANES
    head256 = pl.BlockSpec((t, HEAD_PAD), lambda h: (0, h))
    head128 = pl.BlockSpec((t, LANES), lambda h: (0, h))
    return pl.pallas_call(
        body, name="flash_fwd",
        out_shape=(jax.ShapeDtypeStruct((t, hw), F32), jax.ShapeDtypeStruct((t, hw), BF16),
                   jax.ShapeDtypeStruct((t, hw), F32)),
        grid=(N_HEADS,),
        in_specs=[head256, head256, pl.BlockSpec((t, LANES), lambda h: (0, 0), pipeline_mode=pl.Buffered(1)),
                  pl.BlockSpec((t, LANES), lambda h: (0, z_first + h))],
        out_specs=(head128, head128, head128),
        scratch_shapes=[pltpu.VMEM((t, HEAD_PAD), BF16)],
        compiler_params=_cparams(("parallel",)),
    )(qcat, kv, kr, projb)


def _final_ln_loss(h1, o2, w_out, g, b, target):
    t, d = h1.shape
    ku = o2.shape[1]
    tm = min(t, 512)

    def body(h_ref, u_ref, w_ref, g_ref, b_ref, t_ref, loss_ref, dr_ref, drb_ref, dg_ref, db_ref):
        i = pl.program_id(0)
        r = ALPHA * h_ref[...] + lax.dot_general(u_ref[...], w_ref[...], NN, preferred_element_type=F32)
        mu = jnp.mean(r, axis=-1, keepdims=True)
        xc = r - mu
        var = jnp.mean(xc * xc, axis=-1, keepdims=True)
        rstd = lax.rsqrt(var + LN_EPS)
        xh = xc * rstd
        g = g_ref[...]
        diff = xh * g + b_ref[...] - t_ref[...]
        part = 0.5 * jnp.sum(jnp.mean(diff * diff, axis=-1, keepdims=True))
        dh = diff * (1.0 / d)
        dgp = jnp.sum(dh * xh, axis=0, keepdims=True)
        dbp = jnp.sum(dh, axis=0, keepdims=True)

        @pl.when(i == 0)
        def _():
            loss_ref[...] = jnp.zeros_like(loss_ref)
            dg_ref[...] = jnp.zeros_like(dg_ref)
            db_ref[...] = jnp.zeros_like(db_ref)

        loss_ref[...] += jnp.full(loss_ref.shape, part, F32)
        dg_ref[...] += dgp
        db_ref[...] += dbp
        dxh = dh * g
        dr = rstd * (dxh - jnp.mean(dxh, axis=-1, keepdims=True) - xh * jnp.mean(dxh * xh, axis=-1, keepdims=True))
        dr_ref[...] = dr
        drb_ref[...] = dr.astype(BF16)

    row = pl.BlockSpec((tm, d), lambda i: (i, 0))
    par = pl.BlockSpec((1, d), lambda i: (0, 0))
    return pl.pallas_call(
        body, name="final_ln_loss",
        out_shape=(jax.ShapeDtypeStruct((1, LANES), F32), jax.ShapeDtypeStruct((t, d), F32),
                   jax.ShapeDtypeStruct((t, d), BF16), jax.ShapeDtypeStruct((1, d), F32),
                   jax.ShapeDtypeStruct((1, d), F32)),
        grid=(t // tm,),
        in_specs=[row, pl.BlockSpec((tm, ku), lambda i: (i, 0)),
                  pl.BlockSpec((ku, d), lambda i: (0, 0), pipeline_mode=pl.Buffered(1)), par, par, row],
        out_specs=(pl.BlockSpec((1, LANES), lambda i: (0, 0)), row, row, par, par),
        compiler_params=_cparams(("arbitrary",)),
    )(h1, o2, w_out, g, b, target)


def _gate_bwd(dr2b, w_out, o, projb):
    t, hw = o.shape
    d = dr2b.shape[1]
    tm = min(t, 1024)
    bw = Q_LORA_RANK
    nb = hw // bw
    wb = projb.shape[1]

    def body(dr_ref, w_ref, o_ref, z_ref, dob_ref, dz_ref):
        z = z_ref[...]
        sg = _sigmoid(z)
        w = w_ref[pl.ds(pl.multiple_of(pl.program_id(1) * bw, bw), bw), :]
        d2 = lax.dot_general(dr_ref[...], w, NT, preferred_element_type=F32)
        dob_ref[...] = (d2 * (z * sg)).astype(BF16)
        dz_ref[...] = (d2 * o_ref[...] * (sg * (1.0 + z * (1.0 - sg)))).astype(BF16)

    blk = pl.BlockSpec((tm, bw), lambda i, j: (i, j))
    blk1 = pl.BlockSpec((tm, bw), lambda i, j: (i, j + 1))
    return pl.pallas_call(
        body, name="gate_bwd",
        out_shape=(jax.ShapeDtypeStruct((t, hw), BF16), jax.ShapeDtypeStruct((t, wb), BF16)),
        grid=(t // tm, nb),
        in_specs=[pl.BlockSpec((tm, d), lambda i, j: (i, 0)),
                  pl.BlockSpec((hw, d), lambda i, j: (0, 0), pipeline_mode=pl.Buffered(1)), blk, blk1],
        out_specs=(blk, blk1),
        compiler_params=_cparams(("parallel", "arbitrary")),
    )(dr2b, w_out, o, projb)


def _flash_bwd(qcat, kv, kr, dob, lse, o, rc, rs):
    t = qcat.shape[0]
    tq = min(t, 512)
    nq = t // tq
    q_chunk = 2 * tq
    log2e = math.log2(math.e)
    exp2_scale = ATTN_SCALE * log2e
    once = pl.Buffered(1)

    def body(q_ref, kv_ref, kr_ref, do_ref, lse_ref, o_ref, c_ref, s_ref,
             dq_ref, dkv_ref, dkr_ref, kcat_ref, dqa_ref, dka_ref, dva_ref, dl_ref):
        h = pl.program_id(0)
        kcat_ref[:, 0:LANES] = kv_ref[:, 0:LANES]
        kcat_ref[:, LANES:] = kr_ref[...]
        dqa_ref[...] = jnp.zeros_like(dqa_ref)
        dl_ref[...] = jnp.sum(do_ref[...].astype(F32) * o_ref[...], axis=1, keepdims=True)

        @pl.when(h == 0)
        def _():
            dkr_ref[...] = jnp.zeros_like(dkr_ref)

        rows = lax.broadcasted_iota(jnp.int32, (tq, tq), 0)
        cols = lax.broadcasted_iota(jnp.int32, (tq, tq), 1)

        def block(qs, n, ks, masked):
            q = q_ref[qs:qs + n, :]
            kc = kcat_ref[ks:ks + tq, :]
            s = lax.dot_general(q, kc, NT, preferred_element_type=F32)
            if masked:
                s = jnp.where(cols <= rows, s, MASK_VALUE)
            p = jnp.exp2(s * exp2_scale - lse_ref[qs:qs + n, 0:1] * log2e)
            do = do_ref[qs:qs + n, :]
            dp = lax.dot_general(do, kv_ref[ks:ks + tq, LANES:], NT, preferred_element_type=F32)
            ds = (p * (dp - dl_ref[qs:qs + n, :])).astype(BF16)
            dva_ref[...] += lax.dot_general(p.astype(BF16), do, TN, preferred_element_type=F32)
            dka_ref[...] += lax.dot_general(ds, q, TN, preferred_element_type=F32)
            dqa_ref[qs:qs + n, :] += lax.dot_general(ds, kc, NN, preferred_element_type=F32)

        for kb in range(nq):
            a = kb * tq
            dka_ref[...] = jnp.zeros_like(dka_ref)
            dva_ref[...] = jnp.zeros_like(dva_ref)
            block(a, tq, a, True)
            qs = a + tq
            while qs < t:
                n = min(q_chunk, t - qs)
                block(qs, n, a, False)
                qs += n
            dk = dka_ref[...] * ATTN_SCALE
            dkv_ref[a:a + tq, 0:LANES] = dk[:, 0:LANES].astype(BF16)
            dkv_ref[a:a + tq, LANES:] = dva_ref[...].astype(BF16)
            dkr_ref[a:a + tq, :] += dk[:, LANES:]

        dq = dqa_ref[...] * ATTN_SCALE
        dq_ref[:, 0:LANES] = dq[:, 0:LANES].astype(BF16)
        d2 = dq[:, LANES:]
        dq_ref[:, LANES:] = (d2 * c_ref[...] - _swap_rope_halves(d2) * s_ref[...]).astype(BF16)

    hp = N_HEADS * HEAD_PAD
    head256 = pl.BlockSpec((t, HEAD_PAD), lambda h: (0, h))
    head128 = pl.BlockSpec((t, LANES), lambda h: (0, h))
    const128 = pl.BlockSpec((t, LANES), lambda h: (0, 0), pipeline_mode=once)
    return pl.pallas_call(
        body, name="flash_bwd",
        out_shape=(jax.ShapeDtypeStruct((t, hp), BF16), jax.ShapeDtypeStruct((t, hp), BF16),
                   jax.ShapeDtypeStruct((t, LANES), F32)),
        grid=(N_HEADS,),
        in_specs=[head256, head256, const128, head128, head128, head128, const128, const128],
        out_specs=(head256, head256, pl.BlockSpec((t, LANES), lambda h: (0, 0))),
        scratch_shapes=[pltpu.VMEM((t, HEAD_PAD), BF16), pltpu.VMEM((t, HEAD_PAD), F32),
                        pltpu.VMEM((tq, HEAD_PAD), F32), pltpu.VMEM((tq, LANES), F32), pltpu.VMEM((t, 1), F32)],
        compiler_params=_cparams(("arbitrary",)),
    )(qcat, kv, kr, dob, lse, o, rc, rs)


def _q_norm_bwd(dq2, w_uq, projb, g, dprojb):
    t = projb.shape[0]
    kq = dq2.shape[1]
    tm = min(t, 512)
    r = Q_LORA_RANK

    def body(dq_ref, w_ref, c_ref, g_ref, alias_ref, o_ref, dg_ref):
        i = pl.program_id(0)
        c = c_ref[...]
        d = lax.dot_general(dq_ref[...], w_ref[...], NT, preferred_element_type=F32)
        rr = lax.rsqrt(jnp.mean(c * c, axis=-1, keepdims=True) + RMS_EPS)
        xh = c * rr

        @pl.when(i == 0)
        def _():
            dg_ref[...] = jnp.zeros_like(dg_ref)

        dg_ref[...] += jnp.sum(d * xh, axis=0, keepdims=True)
        dxh = d * g_ref[...]
        o_ref[...] = (rr * (dxh - xh * jnp.mean(dxh * xh, axis=-1, keepdims=True))).astype(BF16)

    blk = pl.BlockSpec((tm, r), lambda i: (i, 0))
    par = pl.BlockSpec((1, r), lambda i: (0, 0))
    return pl.pallas_call(
        body, name="q_norm_bwd",
        out_shape=(jax.ShapeDtypeStruct(dprojb.shape, BF16), jax.ShapeDtypeStruct((1, r), F32)),
        grid=(t // tm,),
        in_specs=[pl.BlockSpec((tm, kq), lambda i: (i, 0)),
                  pl.BlockSpec((r, kq), lambda i: (0, 0), pipeline_mode=pl.Buffered(1)),
                  blk, par, pl.BlockSpec(memory_space=pl.ANY)],
        out_specs=(blk, par),
        input_output_aliases={4: 0},
        compiler_params=_cparams(("arbitrary",)),
    )(dq2, w_uq, projb, g, dprojb)


def _kv_mid_bwd(dkv, w_ukv, ckv, dkr, g, rc, rs):
    t = ckv.shape[0]
    kk = dkv.shape[1]
    tm = min(t, 512)
    r = KV_LORA_RANK

    def body(dkv_ref, w_ref, ckv_ref, dkr_ref, g_ref, c_ref, s_ref, o_ref, dg_ref):
        i = pl.program_id(0)
        c = ckv_ref[:, 0:r]
        d = lax.dot_general(dkv_ref[...], w_ref[...], NT, preferred_element_type=F32)
        rr = lax.rsqrt(jnp.mean(c * c, axis=-1, keepdims=True) + RMS_EPS)
        xh = c * rr

        @pl.when(i == 0)
        def _():
            dg_ref[...] = jnp.zeros_like(dg_ref)

        dg_ref[...] += jnp.sum(d * xh, axis=0, keepdims=True)
        dxh = d * g_ref[...]
        o_ref[:, 0:r] = (rr * (dxh - xh * jnp.mean(dxh * xh, axis=-1, keepdims=True))).astype(BF16)
        dk = dkr_ref[...]
        o_ref[:, r:] = (dk * c_ref[...] - _swap_rope_halves(dk) * s_ref[...]).astype(BF16)

    return pl.pallas_call(
        body, name="kv_mid_bwd",
        out_shape=(jax.ShapeDtypeStruct((t, r + LANES), BF16), jax.ShapeDtypeStruct((1, r), F32)),
        grid=(t // tm,),
        in_specs=[pl.BlockSpec((tm, kk), lambda i: (i, 0)),
                  pl.BlockSpec((r, kk), lambda i: (0, 0), pipeline_mode=pl.Buffered(1)),
                  pl.BlockSpec((tm, r + LANES), lambda i: (i, 0)),
                  pl.BlockSpec((tm, LANES), lambda i: (i, 0)), pl.BlockSpec((1, r), lambda i: (0, 0)),
                  pl.BlockSpec((tm, LANES), lambda i: (i, 0)), pl.BlockSpec((tm, LANES), lambda i: (i, 0))],
        out_specs=(pl.BlockSpec((tm, r + LANES), lambda i: (i, 0)), pl.BlockSpec((1, r), lambda i: (0, 0))),
        compiler_params=_cparams(("arbitrary",)),
    )(dkv, w_ukv, ckv, dkr, g, rc, rs)


def _ln0_bwd(dr2, dprojb, w_in, dckv, w_down, xh, rstd, g, after):
    t, d = dr2.shape
    kb, kd = dprojb.shape[1], dckv.shape[1]
    tm = min(t, 256)
    extra = [] if after is None else [after]

    def body(a_ref, pb_ref, wb_ref, pd_ref, wd_ref, xh_ref, rs_ref, g_ref, *rest):
        dr_ref, drb_ref, dg_ref, db_ref, dsum_ref = rest[-5:]
        i = pl.program_id(0)
        dh = (ALPHA * a_ref[...] + lax.dot_general(pb_ref[...], wb_ref[...], NT, preferred_element_type=F32)
              + lax.dot_general(pd_ref[...], wd_ref[...], NT, preferred_element_type=F32))
        xh = xh_ref[...]

        @pl.when(i == 0)
        def _():
            dg_ref[...] = jnp.zeros_like(dg_ref)
            db_ref[...] = jnp.zeros_like(db_ref)
            dsum_ref[...] = jnp.zeros_like(dsum_ref)

        dg_ref[...] += jnp.sum(dh * xh, axis=0, keepdims=True)
        db_ref[...] += jnp.sum(dh, axis=0, keepdims=True)
        dxh = dh * g_ref[...]
        dr = rs_ref[...] * (dxh - jnp.mean(dxh, axis=-1, keepdims=True) - xh * jnp.mean(dxh * xh, axis=-1, keepdims=True))
        dr_ref[...] = dr
        drb_ref[...] = dr.astype(BF16)
        dsum_ref[...] += jnp.sum(dr, axis=0, keepdims=True)

    row = pl.BlockSpec((tm, d), lambda i: (i, 0))
    par = pl.BlockSpec((1, d), lambda i: (0, 0))
    return pl.pallas_call(
        body, name="ln0_bwd",
        out_shape=(jax.ShapeDtypeStruct((t, d), F32), jax.ShapeDtypeStruct((t, d), BF16),
                   jax.ShapeDtypeStruct((1, d), F32), jax.ShapeDtypeStruct((1, d), F32),
                   jax.ShapeDtypeStruct((1, d), F32)),
        grid=(t // tm,),
        in_specs=[row, pl.BlockSpec((tm, kb), lambda i: (i, 0)),
                  pl.BlockSpec((d, kb), lambda i: (0, 0), pipeline_mode=pl.Buffered(1)),
                  pl.BlockSpec((tm, kd), lambda i: (i, 0)),
                  pl.BlockSpec((d, kd), lambda i: (0, 0), pipeline_mode=pl.Buffered(1)),
                  row, pl.BlockSpec((tm, 1), lambda i: (i, 0)), par]
                 + [pl.BlockSpec(memory_space=pl.ANY)] * len(extra),
        out_specs=(row, row, par, par, par),
        compiler_params=_cparams(("arbitrary",)),
    )(dr2, dprojb, w_in, dckv, w_down, xh, rstd, g, *extra)


def _conv_mid_bwd_rows(u1, proj, dr1b, w_out, ng, nb, after):
    t = u1.shape[0]
    e = CONV_WIDTH
    d = dr1b.shape[1]
    tm = min(t, 256)
    extra = [] if after is None else [after]

    rg = 16
    nt = t // tm

    def body(u1_ref, z_ref, dr_ref, w_ref, ng_ref, nb_ref, *rest):
        du1_ref, dz_ref, dng_ref, dnb_ref, dbz_ref, acc_ref, d_ref = rest[-7:]
        i = pl.program_id(0)

        @pl.when(i == 0)
        def _():
            acc_ref[...] = jnp.zeros_like(acc_ref)

        d_ref[...] = lax.dot_general(dr_ref[...], w_ref[...], NT, preferred_element_type=F32)
        g = ng_ref[...]
        b = nb_ref[...]

        def group(r, carry):
            rows = pl.ds(pl.multiple_of(r * rg, rg), rg)
            u1 = u1_ref[rows, :]
            mu = jnp.mean(u1, axis=-1, keepdims=True)
            xc = u1 - mu
            rstd = lax.rsqrt(jnp.mean(xc * xc, axis=-1, keepdims=True) + LN_EPS)
            xh = xc * rstd
            n = xh * g + b
            sn = _sigmoid(n)
            z = z_ref[rows, :]
            sz = _sigmoid(z)
            du2 = d_ref[rows, :]
            dz = du2 * (n * sn) * (sz * (1.0 + z * (1.0 - sz)))
            dn = du2 * (z * sz) * (sn * (1.0 + n * (1.0 - sn)))
            acc_ref[0:8, :] += _fold_rows(dn * xh)
            acc_ref[8:16, :] += _fold_rows(dn)
            acc_ref[16:24, :] += _fold_rows(dz)
            dz_ref[rows, :] = dz.astype(BF16)
            dxh = dn * g
            du1_ref[rows, :] = rstd * (dxh - jnp.mean(dxh, axis=-1, keepdims=True)
                                       - xh * jnp.mean(dxh * xh, axis=-1, keepdims=True))
            return carry

        lax.fori_loop(0, tm // rg, group, 0, unroll=8)

        @pl.when(i == nt - 1)
        def _():
            dng_ref[...] = jnp.sum(acc_ref[0:8, :], axis=0, keepdims=True)
            dnb_ref[...] = jnp.sum(acc_ref[8:16, :], axis=0, keepdims=True)
            dbz_ref[...] = jnp.sum(acc_ref[16:24, :], axis=0, keepdims=True)

    row = pl.BlockSpec((tm, e), lambda i: (i, 0))
    par = pl.BlockSpec((1, e), lambda i: (0, 0))
    return pl.pallas_call(
        body, name="conv_mid_bwd_rows",
        out_shape=(jax.ShapeDtypeStruct((t, e), F32), jax.ShapeDtypeStruct((t, 3 * e), BF16),
                   jax.ShapeDtypeStruct((1, e), F32), jax.ShapeDtypeStruct((1, e), F32),
                   jax.ShapeDtypeStruct((1, e), F32)),
        grid=(nt,),
        in_specs=[row, pl.BlockSpec((tm, e), lambda i: (i, 2)), pl.BlockSpec((tm, d), lambda i: (i, 0)),
                  pl.BlockSpec((e, d), lambda i: (0, 0), pipeline_mode=pl.Buffered(1)), par, par]
                 + [pl.BlockSpec(memory_space=pl.ANY)] * len(extra),
        out_specs=(row, pl.BlockSpec((tm, e), lambda i: (i, 2)), par, par, par),
        scratch_shapes=[pltpu.VMEM((24, e), F32), pltpu.VMEM((tm, e), F32)],
        compiler_params=_cparams(("arbitrary",)),
    )(u1, proj, dr1b, w_out, ng, nb, *extra)


def _conv_bwd(du1, proj, cw, dproj):
    t = du1.shape[0]
    e = CONV_WIDTH
    tm = min(t, 128)
    hb = tm // HALO
    nt = t // tm
    nchunk = e // LANES
    last_halo = t // HALO - 1

    def body(d_ref, dn_ref, val_ref, gg_ref, valh_ref, ggh_ref, cw_ref, alias_ref,
             dp_ref, dcw_ref, dcb_ref, dbv_ref, extu_ref, extd_ref, du0_ref, acc_ref, sh_ref):
        i = pl.program_id(0)
        h0 = valh_ref[...] * _sigmoid(ggh_ref[...])
        extu_ref[0:HALO, :] = jnp.where(i > 0, h0, 0.0)
        extu_ref[HALO:, :] = val_ref[...] * _sigmoid(gg_ref[...])
        extd_ref[0:tm, :] = d_ref[...]
        extd_ref[tm:, :] = jnp.where(i < nt - 1, dn_ref[...], 0.0)

        @pl.when(i == 0)
        def _():
            acc_ref[...] = jnp.zeros_like(acc_ref)

        def chunk(c, carry):
            col = pl.multiple_of(c * LANES, LANES)
            d = extd_ref[0:tm, pl.ds(col, LANES)]
            du0 = jnp.zeros((tm, LANES), F32)
            for res, taps in _taps_by_residue(lambda j: CONV_KERNEL - 1 - j):
                span = 8 * max(a for _, a in taps) + tm
                sh_ref[0:span, :] = extd_ref[res:res + span, pl.ds(col, LANES)]
                for j, a in taps:
                    du0 = du0 + cw_ref[j:j + 1, pl.ds(col, LANES)] * sh_ref[8 * a:8 * a + tm, :]
            for res, taps in _taps_by_residue(lambda j: HALO - (CONV_KERNEL - 1) + j):
                span = 8 * max(a for _, a in taps) + tm
                sh_ref[0:span, :] = extu_ref[res:res + span, pl.ds(col, LANES)]
                for j, a in taps:
                    prod = d * sh_ref[8 * a:8 * a + tm, :]
                    acc_ref[j * 8:(j + 1) * 8, pl.ds(col, LANES)] += _fold_rows(prod)
            acc_ref[CONV_KERNEL * 8:(CONV_KERNEL + 1) * 8, pl.ds(col, LANES)] += _fold_rows(d)
            du0_ref[:, pl.ds(col, LANES)] = du0
            return carry

        lax.fori_loop(0, nchunk, chunk, 0)
        kb = CONV_KERNEL * 8
        for r in range(0, tm, 16):
            sg = _sigmoid(gg_ref[r:r + 16, :])
            dval = du0_ref[r:r + 16, :] * sg
            dgg = dval * val_ref[r:r + 16, :] * (1.0 - sg)
            dp_ref[r:r + 16, 0:e] = dval.astype(BF16)
            dp_ref[r:r + 16, e:] = dgg.astype(BF16)
            acc_ref[kb + 8:kb + 16, :] += _fold_rows(dval)
            acc_ref[kb + 16:kb + 24, :] += _fold_rows(dgg)

        @pl.when(i == nt - 1)
        def _():
            for j in range(CONV_KERNEL):
                dcw_ref[j:j + 1, :] = jnp.sum(acc_ref[j * 8:(j + 1) * 8, :], axis=0, keepdims=True)
            dcw_ref[CONV_KERNEL:, :] = jnp.zeros((HALO - CONV_KERNEL, e), F32)
            dcb_ref[...] = jnp.sum(acc_ref[kb:kb + 8, :], axis=0, keepdims=True)
            dbv_ref[:, 0:e] = jnp.sum(acc_ref[kb + 8:kb + 16, :], axis=0, keepdims=True)
            dbv_ref[:, e:] = jnp.sum(acc_ref[kb + 16:kb + 24, :], axis=0, keepdims=True)

    row = lambda c: pl.BlockSpec((tm, e), lambda i: (i, c))
    halo_prev = lambda c: pl.BlockSpec((HALO, e), lambda i: (jnp.maximum(i * hb - 1, 0), c))
    halo_next = pl.BlockSpec((HALO, e), lambda i: (jnp.minimum((i + 1) * hb, last_halo), 0))
    return pl.pallas_call(
        body, name="conv_bwd",
        out_shape=(jax.ShapeDtypeStruct(dproj.shape, BF16), jax.ShapeDtypeStruct((HALO, e), F32),
                   jax.ShapeDtypeStruct((1, e), F32), jax.ShapeDtypeStruct((1, 2 * e), F32)),
        grid=(nt,),
        in_specs=[row(0), halo_next, row(0), row(1), halo_prev(0), halo_prev(1),
                  pl.BlockSpec((HALO, e), lambda i: (0, 0)), pl.BlockSpec(memory_space=pl.ANY)],
        out_specs=(pl.BlockSpec((tm, 2 * e), lambda i: (i, 0)), pl.BlockSpec((HALO, e), lambda i: (0, 0)),
                   pl.BlockSpec((1, e), lambda i: (0, 0)), pl.BlockSpec((1, 2 * e), lambda i: (0, 0))),
        scratch_shapes=[pltpu.VMEM((tm + HALO, e), F32), pltpu.VMEM((tm + HALO, e), F32),
                        pltpu.VMEM((tm, e), F32), pltpu.VMEM(((CONV_KERNEL + 3) * 8, e), F32),
                        pltpu.VMEM((tm + HALO, LANES), F32)],
        input_output_aliases={7: 0},
        compiler_params=_cparams(("arbitrary",)),
    )(du1, du1, proj, proj, proj, proj, cw, dproj)


def _adam_update(g, w, m, v):
    c1 = 1.0 - ADAM_B1 ** ADAM_STEP
    c2 = 1.0 - ADAM_B2 ** ADAM_STEP
    mn = ADAM_B1 * m + (1.0 - ADAM_B1) * g
    vn = ADAM_B2 * v + (1.0 - ADAM_B2) * (g * g)
    delta = -ADAM_LR * ((mn / c1) / (jnp.sqrt(vn / c2) + ADAM_EPS) + ADAM_WD * w)
    return delta, mn, vn


def _adamw(name, gbuf, w, m, v):
    parts = list(gbuf) if isinstance(gbuf, (list, tuple)) else [gbuf]
    npart = len(parts)
    r, c = w.shape
    tr = min(r // npart, 256)
    per_part = r // npart // tr
    assert r == npart * per_part * tr and all(p.shape == (N_DEV, r // npart, c) for p in parts)

    def body(*refs):
        g_refs = refs[:npart]
        w_ref, m_ref, v_ref, go_ref, d_ref, mo_ref, vo_ref = refs[npart:]
        i = pl.program_id(0)

        def run(g_ref):
            g = g_ref[0].astype(F32)
            for k in range(1, N_DEV):
                g = g + g_ref[k].astype(F32)
            go_ref[...] = g
            d_ref[...], mo_ref[...], vo_ref[...] = _adam_update(g, w_ref[...], m_ref[...], v_ref[...])

        if npart == 1:
            run(g_refs[0])
        else:
            for q in range(npart):
                pl.when((i >= q * per_part) & (i < (q + 1) * per_part))(functools.partial(run, g_refs[q]))

    blk = pl.BlockSpec((tr, c), lambda i: (i, 0))
    part_spec = lambda q: pl.BlockSpec((N_DEV, tr, c), lambda i: (0, jnp.clip(i - q * per_part, 0, per_part - 1), 0))
    shp = jax.ShapeDtypeStruct((r, c), F32)
    return pl.pallas_call(
        body, name="adamw_" + name,
        out_shape=(shp, shp, shp, shp),
        grid=(r // tr,),
        in_specs=[part_spec(q) for q in range(npart)] + [blk, blk, blk],
        out_specs=(blk, blk, blk, blk),
        compiler_params=_cparams(("parallel",)),
    )(*parts, w, m, v)


SMALL_TABLE_COLS = 256
SMALL_LAYOUT = {
    'ln_g': (0, 2, 1024), 'ln_b': (8, 2, 1024), 'a_b_in': (16, 1, 768), 'a_conv_b': (24, 1, 256),
    'a_norm_g': (32, 1, 256), 'a_norm_b': (40, 1, 256), 'a_b_out': (48, 1, 128), 'kv_norm_g': (56, 1, 256),
    'b_q_norm_g': (64, 1, 512), 'a_conv_w': (72, CONV_KERNEL, 256),
}
LOSS_ROW = 104
SMALL_TABLE_ROWS = 112
SMALL_NAMES = tuple(SMALL_LAYOUT)


def _small_pieces(name, row):
    r0, _, c = SMALL_LAYOUT[name]
    w = min(c, SMALL_TABLE_COLS)
    per_row = c // w
    return [(r0 + row * per_row + q, q * w, w) for q in range(per_row)]


def _pack_small_grads(dg0, dg1, db0, db1, dbv, dbz, dcb, dng, dnb, dba, dgkv, dgq, dcw, loss):
    e = CONV_WIDTH
    ce = e // N_DEV

    def body(dg0_ref, dg1_ref, db0_ref, db1_ref, dbv_ref, dbz_ref, dcb_ref, dng_ref, dnb_ref, dba_ref, dgkv_ref,
             dgq_ref, dcw_ref, loss_ref, o_ref):
        o_ref[...] = jnp.zeros_like(o_ref)
        for d in range(N_DEV):
            o_ref[d, LOSS_ROW:LOSS_ROW + 1, 0:LANES] = loss_ref[...]

        def put(d, name, row, src_ref, first_col=0):
            for trow, col, w in _small_pieces(name, row):
                o_ref[d, trow:trow + 1, 0:w] = src_ref[:, first_col + col:first_col + col + w]

        for d in range(N_DEV):
            put(d, 'ln_g', 0, dg0_ref)
            put(d, 'ln_g', 1, dg1_ref)
            put(d, 'ln_b', 0, db0_ref)
            put(d, 'ln_b', 1, db1_ref)
            for trow, col, w in _small_pieces('a_b_in', 0):
                gcol = d * 3 * ce + col
                src = dbv_ref[:, gcol:gcol + w] if gcol < 2 * e else dbz_ref[:, gcol - 2 * e:gcol - 2 * e + w]
                o_ref[d, trow:trow + 1, 0:w] = src
            put(d, 'a_conv_b', 0, dcb_ref, d * ce)
            put(d, 'a_norm_g', 0, dng_ref, d * ce)
            put(d, 'a_norm_b', 0, dnb_ref, d * ce)
            put(d, 'a_b_out', 0, dba_ref, d * LANES)
            put(d, 'kv_norm_g', 0, dgkv_ref)
            put(d, 'b_q_norm_g', 0, dgq_ref)
            r0 = SMALL_LAYOUT['a_conv_w'][0]
            o_ref[d, r0:r0 + HALO, 0:ce] = dcw_ref[:, d * ce:(d + 1) * ce]

    return pl.pallas_call(
        body, name="pack_small_grads",
        out_shape=jax.ShapeDtypeStruct((N_DEV, SMALL_TABLE_ROWS, SMALL_TABLE_COLS), F32),
        compiler_params=pltpu.CompilerParams(vmem_limit_bytes=VMEM_LIMIT),
    )(dg0, dg1, db0, db1, dbv, dbz, dcb, dng, dnb, dba, dgkv, dgq, dcw, loss)


def _adamw_small(gtab, ws, ms, vs):
    n = len(SMALL_NAMES)

    def body(*refs):
        g_ref = refs[0]
        w_refs, m_refs, v_refs = refs[1:1 + n], refs[1 + n:1 + 2 * n], refs[1 + 2 * n:1 + 3 * n]
        outs = refs[1 + 3 * n:]

        def summed(r0, r, c):
            g = g_ref[0, r0:r0 + r, 0:c]
            for k in range(1, N_DEV):
                g = g + g_ref[k, r0:r0 + r, 0:c]
            return g

        for i, name in enumerate(SMALL_NAMES):
            r0, r, c = SMALL_LAYOUT[name]
            if c <= SMALL_TABLE_COLS:
                blocks = [(slice(None), summed(r0, r, c))]
            else:
                blocks = [(slice(row, row + 1), jnp.concatenate([summed(tr, 1, w) for tr, _, w in _small_pieces(name, row)], axis=1))
                          for row in range(r)]
            for rows, g in blocks:
                delta, mn, vn = _adam_update(g, w_refs[i][rows, :], m_refs[i][rows, :], v_refs[i][rows, :])
                outs[i][rows, :] = g
                outs[n + i][rows, :] = delta
                outs[2 * n + i][rows, :] = mn
                outs[3 * n + i][rows, :] = vn
        outs[4 * n][...] = summed(LOSS_ROW, 1, LANES)

    shapes = tuple(jax.ShapeDtypeStruct(w.shape, F32) for w in ws)
    res = pl.pallas_call(
        body, name="adamw_small",
        out_shape=shapes * 4 + (jax.ShapeDtypeStruct((1, LANES), F32),),
        compiler_params=pltpu.CompilerParams(vmem_limit_bytes=VMEM_LIMIT),
    )(gtab, *ws, *ms, *vs)
    return res[:n], res[n:2 * n], res[2 * n:3 * n], res[3 * n:4 * n], res[4 * n]


def _mesh_peers():
    x, y, c = lax.axis_index("x"), lax.axis_index("y"), lax.axis_index("c")
    me = 4 * x + 2 * y + c
    peers = []
    for k in range(1, N_DEV):
        px = 1 - x if (k >> 2) & 1 else x
        py = 1 - y if (k >> 1) & 1 else y
        pc = 1 - c if k & 1 else c
        peers.append(((px, py, pc), 4 * px + 2 * py + pc))
    return me, peers


_HBM_SPEC = pl.BlockSpec(memory_space=pltpu.HBM)
_SEM_SPEC = pl.BlockSpec(memory_space=pltpu.SEMAPHORE)


ALL_PEERS = tuple(range(N_DEV - 1))
CHIP_LEVEL_PEERS = (0, 1, 3, 5)


def _split_copies(srcs, lands, send_sems, recv_sems, local_sems, gather, which=ALL_PEERS):
    me, peers = _mesh_peers()
    na = len(srcs)
    mine = lambda a, slot: srcs[a] if gather else srcs[a].at[slot]
    local = [pltpu.make_async_copy(mine(a, me), lands[a].at[me], local_sems.at[a]) for a in range(na)]
    sent, received = [], []
    for k, (dev, pid) in enumerate(peers):
        if k not in which:
            continue
        for a in range(na):
            s = a * (N_DEV - 1) + k
            sent.append(pltpu.make_async_remote_copy(
                src_ref=mine(a, pid), dst_ref=lands[a].at[me], send_sem=send_sems.at[s],
                recv_sem=recv_sems.at[s], device_id=dev, device_id_type=pl.DeviceIdType.MESH))
            received.append(pltpu.make_async_remote_copy(
                src_ref=mine(a, pid), dst_ref=lands[a].at[pid], send_sem=send_sems.at[s],
                recv_sem=recv_sems.at[s], device_id=dev, device_id_type=pl.DeviceIdType.MESH))
    return local, sent, received


def _comm_start(name, srcs, gather, after, which=ALL_PEERS):
    na = len(srcs)
    land_structs = [jax.ShapeDtypeStruct(((N_DEV,) + s.shape) if gather else s.shape, s.dtype) for s in srcs]
    extra = [] if after is None else [after]
    n_in = 2 * na + len(extra)

    def body(*refs):
        srcs_r, lands_r = refs[:na], refs[na:2 * na]
        send_sems, recv_sems, local_sems = refs[n_in:n_in + 3]
        token = refs[-1]
        local, sent, _ = _split_copies(srcs_r, lands_r, send_sems, recv_sems, local_sems, gather, which)
        for cp in local + sent:
            cp.start()
        token[...] = jnp.zeros_like(token)

    sem = pltpu.SemaphoreType.DMA((na * (N_DEV - 1),))
    outs = pl.pallas_call(
        body, name=name,
        out_shape=(sem, sem, pltpu.SemaphoreType.DMA((na,)),
                   *[pltpu.HBM(s.shape, s.dtype) for s in srcs],
                   *[pltpu.HBM(s.shape, s.dtype) for s in land_structs],
                   jax.ShapeDtypeStruct((8, LANES), F32)),
        in_specs=[_HBM_SPEC] * (2 * na) + [pl.BlockSpec(memory_space=pl.ANY)] * len(extra),
        out_specs=(_SEM_SPEC, _SEM_SPEC, _SEM_SPEC, *([_HBM_SPEC] * (2 * na)), pl.BlockSpec(memory_space=pltpu.VMEM)),
        input_output_aliases={i: 3 + i for i in range(2 * na)},
        compiler_params=pltpu.CompilerParams(has_side_effects=pltpu.SideEffectType.DATAFLOW_SIDE_EFFECTING),
    )(*[pltpu.with_memory_space_constraint(s, pltpu.HBM) for s in srcs],
      *[pltpu.with_memory_space_constraint(lax.empty(s.shape, s.dtype), pltpu.HBM) for s in land_structs],
      *extra)
    return (outs[:3], outs[3:3 + na], outs[3 + na:3 + 2 * na]), outs[-1]


def _comm_wait(name, handles, gather, after, which=ALL_PEERS):
    (send_sems, recv_sems, local_sems), srcs, lands = handles
    na = len(srcs)

    def body(*refs):
        srcs_r, lands_r = refs[:na], refs[na:2 * na]
        send_r, recv_r, local_r = refs[2 * na:2 * na + 3]
        local, sent, received = _split_copies(srcs_r, lands_r, send_r, recv_r, local_r, gather, which)
        for cp in sent:
            cp.wait_send()
        for cp in received:
            cp.wait_recv()
        for cp in local:
            cp.wait()

    outs = pl.pallas_call(
        body, name=name,
        out_shape=(*[pltpu.HBM(s.shape, s.dtype) for s in srcs], *[pltpu.HBM(s.shape, s.dtype) for s in lands]),
        in_specs=[_HBM_SPEC] * (2 * na) + [_SEM_SPEC] * 3 + [pl.BlockSpec(memory_space=pl.ANY)],
        out_specs=tuple([_HBM_SPEC] * (2 * na)),
        input_output_aliases={i: i for i in range(2 * na)},
        compiler_params=pltpu.CompilerParams(has_side_effects=pltpu.SideEffectType.DATAFLOW_SIDE_EFFECTING),
    )(*srcs, *lands, send_sems, recv_sems, local_sems, after)
    return outs[na:]


def _forward_to_sibling(name, lands):
    na = len(lands)

    def body(*refs):
        ins, outs = refs[:na], refs[na:2 * na]
        send_sems, recv_sems = refs[2 * na:]
        x, y, c = lax.axis_index("x"), lax.axis_index("y"), lax.axis_index("c")
        sibling = (x, y, 1 - c)
        chips = [(1 - x, y), (x, 1 - y), (1 - x, 1 - y)]
        slot = lambda cx, cy, cc: 4 * cx + 2 * cy + cc
        sends = []
        for j, (cx, cy) in enumerate(chips):
            for a in range(na):
                cp = pltpu.make_async_remote_copy(
                    src_ref=ins[a].at[slot(cx, cy, c)], dst_ref=outs[a].at[slot(cx, cy, c)],
                    send_sem=send_sems.at[a, j], recv_sem=recv_sems.at[a, j], device_id=sibling,
                    device_id_type=pl.DeviceIdType.MESH)
                cp.start()
                sends.append(cp)
        for j, (cx, cy) in enumerate(chips):
            for a in range(na):
                pltpu.make_async_remote_copy(
                    src_ref=ins[a].at[slot(cx, cy, 1 - c)], dst_ref=outs[a].at[slot(cx, cy, 1 - c)],
                    send_sem=send_sems.at[a, j], recv_sem=recv_sems.at[a, j], device_id=sibling,
                    device_id_type=pl.DeviceIdType.MESH).wait_recv()
        for cp in sends:
            cp.wait_send()

    hbm = pl.BlockSpec(memory_space=pl.ANY)
    return pl.pallas_call(
        body, name=name,
        out_shape=tuple(jax.ShapeDtypeStruct(s.shape, s.dtype) for s in lands),
        in_specs=[hbm] * na, out_specs=(hbm,) * na,
        input_output_aliases={a: a for a in range(na)},
        scratch_shapes=[pltpu.SemaphoreType.DMA((na, 3)), pltpu.SemaphoreType.DMA((na, 3))],
    )(*lands)


def _prep_weights(w):
    e = CONV_WIDTH
    p = {}
    if 'a_w_in' in w:
        if w['a_w_in'].shape == (N_DEV, D_MODEL, 3 * e // N_DEV):
            p['a_w_in3'] = w['a_w_in']
        else:
            p['a_w_in3'] = w['a_w_in'].reshape(D_MODEL, N_DEV, 3 * e // N_DEV).transpose(1, 0, 2)
        p['a_b_in'] = w['a_b_in'].reshape(1, 3 * e)
        p['a_conv_w'] = jnp.pad(w['a_conv_w'].reshape(CONV_KERNEL, e), ((0, HALO - CONV_KERNEL), (0, 0)))
        p['a_conv_b'] = w['a_conv_b'].reshape(1, e)
        p['a_norm_g'] = w['a_norm_g'].reshape(1, e)
        p['a_norm_b'] = w['a_norm_b'].reshape(1, e)
        p['a_b_out'] = w['a_b_out'].reshape(1, D_MODEL)
        p['kv_norm_g'] = w['kv_norm_g'].reshape(1, KV_LORA_RANK)
        p['b_q_norm_g'] = w['b_q_norm_g'].reshape(1, Q_LORA_RANK)
        p['ln_g'] = w['ln_g']
        p['ln_b'] = w['ln_b']
    if 'a_w_out' in w:
        p['a_w_out'] = w['a_w_out'].reshape(e, D_MODEL)
        p['kv_w_down'] = jnp.pad(w['kv_w_down'], ((0, 0), (0, KV_LORA_RANK + LANES - w['kv_w_down'].shape[1])))
        p['kv_w_ukv'] = jnp.concatenate([w['kv_w_uk'], w['kv_w_uv']], axis=2).reshape(KV_LORA_RANK, N_HEADS * HEAD_PAD)
        if w['b_w_in'].ndim == 3 and w['b_w_in'].shape[:2] == (N_DEV, D_MODEL):
            p['b_w_in3'] = w['b_w_in']
        else:
            p['b_w_in3'] = w['b_w_in'].reshape(D_MODEL, N_DEV, -1).transpose(1, 0, 2)
        uq = w['b_w_uq'].reshape(Q_LORA_RANK, N_HEADS, QK_NOPE_DIM + QK_ROPE_DIM)
        p['b_w_uq'] = jnp.pad(uq, ((0, 0), (0, 0), (0, HEAD_PAD - uq.shape[2]))).reshape(Q_LORA_RANK, N_HEADS * HEAD_PAD)
        p['b_w_out'] = w['b_w_out'].reshape(N_HEADS * V_HEAD_DIM, D_MODEL)
    return p


def _local_step(x, positions, target, comm):
    t = x.shape[0]
    e = CONV_WIDTH
    freqs = ROPE_THETA ** (-jnp.arange(0, QK_ROPE_DIM, 2, dtype=F32) / QK_ROPE_DIM)
    freq_row = jnp.concatenate([freqs, freqs, jnp.zeros((LANES - QK_ROPE_DIM,), F32)]).reshape(1, LANES)
    rc, rs, xb = _rope_tables(positions.reshape(t, 1), freq_row, x, comm.start_token())
    p = comm.first_weights(rc)
    ln_g0, ln_b0 = p['ln_g'][0:1], p['ln_b'][0:1]
    ln_g1, ln_b1 = p['ln_g'][1:2], p['ln_b'][1:2]

    proj = _matmul(xb, p['a_w_in3'], bias=p['a_b_in'], name="a_in", b_blocked=True, bm=2048,
                   after=comm.first_after())
    u1, u2 = _conv_mid_fwd(proj, p['a_conv_w'], p['a_conv_b'], p['a_norm_g'], p['a_norm_b'])
    p = {**p, **comm.rest_weights(u2)}
    h1, h1b, xh1, rstd1 = _ln_res_fwd(x, u2, p['a_w_out'], p['a_b_out'], ln_g0, ln_b0)
    ckv, ckn, kr, kv = _kv_mid_fwd(h1b, p['kv_w_down'], p['kv_w_ukv'], p['kv_norm_g'], rc, rs)
    projb, cqn, b_w_in = _b_in_qnorm(h1b, p['b_w_in3'], p['b_q_norm_g'])
    qcat = _q_up_rope(cqn, p['b_w_uq'], rc, rs)
    o, o2, lse = _flash_fwd(qcat, kv, kr, projb)
    loss, dr2, dr2b, dg1, db1 = _final_ln_loss(h1, o2, p['b_w_out'], ln_g1, ln_b1, target)

    g = {}
    g['b_w_out'] = _matmul(o2, dr2b, trans_a=True, name="d_b_out", out_dtype=BF16, bm=512, bk=t)
    dob, dprojb = _gate_bwd(dr2b, p['b_w_out'], o, projb)
    dq2, dkv, dkr = _flash_bwd(qcat, kv, kr, dob, lse, o, rc, rs)
    duq = _matmul(cqn, dq2, trans_a=True, name="d_q_up", out_dtype=BF16, bk=t)
    dprojb, dgq = _q_norm_bwd(dq2, p['b_w_uq'], projb, p['b_q_norm_g'], dprojb)
    g['b_w_in'] = _matmul(h1b, dprojb, trans_a=True, name="d_b_in", bn=2 * dprojb.shape[1] // N_DEV, bk=t, out_dtype=BF16,
                          out_blocked=True, out_parts=2)
    dukv = _matmul(ckn, dkv, trans_a=True, name="d_kv_up", out_dtype=BF16, bk=t)
    dckv, dgkv = _kv_mid_bwd(dkv, p['kv_w_ukv'], ckv, dkr, p['kv_norm_g'], rc, rs)
    ddown = _matmul(h1b, dckv, trans_a=True, name="d_kv_down", out_dtype=BF16, bm=512, bk=t)
    g['b_w_out'] = g['b_w_out'].reshape(N_DEV, -1, D_MODEL)
    g['kv_w_down'] = ddown[:, :KV_LORA_RANK + QK_ROPE_DIM].reshape(N_DEV, -1, KV_LORA_RANK + QK_ROPE_DIM)
    dukv = dukv.reshape(KV_LORA_RANK, N_HEADS, HEAD_PAD)
    g['kv_w_uk'] = dukv[:, :, :QK_NOPE_DIM].reshape(N_DEV, -1, QK_NOPE_DIM)
    g['kv_w_uv'] = dukv[:, :, QK_NOPE_DIM:].reshape(N_DEV, -1, V_HEAD_DIM)
    g['b_w_uq'] = duq.reshape(Q_LORA_RANK, N_HEADS, HEAD_PAD)[:, :, :QK_NOPE_DIM + QK_ROPE_DIM].reshape(
        N_DEV, -1, QK_NOPE_DIM + QK_ROPE_DIM)
    sent1 = comm.send_group1(g)
    dr1, dr1b, dg0, db0, dba_out = _ln0_bwd(dr2, dprojb, b_w_in, dckv, p['kv_w_down'], xh1, rstd1, ln_g0, sent1)
    dw_out = _matmul(u2, dr1b, trans_a=True, name="d_a_out", out_dtype=BF16, bm=512, bk=t).reshape(N_DEV, -1, D_MODEL)
    sent1b = comm.send_group1b({'a_w_out': dw_out})
    du1, dproj, dng, dnb, dbz = _conv_mid_bwd_rows(u1, proj, dr1b, p['a_w_out'], p['a_norm_g'], p['a_norm_b'], sent1b)
    dproj, dcw, dcb, dbv = _conv_bwd(du1, proj, p['a_conv_w'], dproj)
    half = D_MODEL // 2
    dw_lo = _matmul(xb, dproj, trans_a=True, name="d_a_in_lo", out_dtype=BF16, bm=half, bn=3 * e // N_DEV, bk=t,
                    out_blocked=True, a_cols=(0, half))
    small = (dg0, dg1, db0, db1, dbv, dbz, dcb, dng, dnb, dba_out, dgkv, dgq, dcw, loss)
    sent2 = comm.send_group2(dw_lo, small)
    dw_hi = _matmul(xb, dproj, trans_a=True, name="d_a_in_hi", out_dtype=BF16, bm=half, bn=3 * e // N_DEV, bk=t,
                    out_blocked=True, after=sent2, a_cols=(half, half))
    sent3 = comm.send_group3(dw_hi)
    grad_x = _grad_x(dproj, p['a_w_in3'], dr1, sent3)
    return loss, grad_x


def _grad_x(dproj, w3, dr1, after):
    t, d = dr1.shape
    nblk, _, cb = w3.shape
    tm = min(t, 512)
    extra = [] if after is None else [after]

    def body(a_ref, w_ref, dr_ref, *rest):
        o_ref = rest[-1]
        acc = ALPHA * dr_ref[...]
        for k in range(nblk):
            acc = acc + lax.dot_general(a_ref[:, k * cb:(k + 1) * cb], w_ref[k], NT, preferred_element_type=F32)
        o_ref[...] = acc

    row = pl.BlockSpec((tm, d), lambda i: (i, 0))
    return pl.pallas_call(
        body, name="grad_x",
        out_shape=jax.ShapeDtypeStruct((t, d), F32),
        grid=(t // tm,),
        in_specs=[pl.BlockSpec((tm, nblk * cb), lambda i: (i, 0)),
                  pl.BlockSpec((nblk, d, cb), lambda i: (0, 0, 0), pipeline_mode=pl.Buffered(1)), row]
                 + [pl.BlockSpec(memory_space=pl.ANY)] * len(extra),
        out_specs=row,
        compiler_params=_cparams(("parallel",)),
    )(dproj, w3, dr1, *extra)


def _shard_2d(a):
    return a.reshape(-1, a.shape[-1])


def _gathered_to_full(name, blocks):
    if name in ('a_w_in', 'b_w_in'):
        return blocks
    if name == 'a_conv_w':
        return blocks.transpose(1, 0, 2).reshape(CONV_KERNEL, -1)
    if name in ('a_b_in', 'a_conv_b', 'a_norm_g', 'a_norm_b', 'a_b_out'):
        return blocks.reshape(-1)
    if name in ('kv_w_uk', 'kv_w_uv'):
        return blocks.reshape(KV_LORA_RANK, N_HEADS, -1)
    if name == 'b_w_uq':
        return blocks.reshape(Q_LORA_RANK, N_HEADS, -1)
    return blocks.reshape(-1, blocks.shape[-1])


def kernel(x, positions, ln_g, ln_b, a_w_in, a_b_in, a_conv_w, a_conv_b, a_norm_g, a_norm_b, a_w_out, a_b_out, kv_w_down, kv_norm_g, kv_w_uk, kv_w_uv, b_w_in, b_q_norm_g, b_w_uq, b_w_out, loss_target, m_ln_g, m_ln_b, m_a_w_in, m_a_b_in, m_a_conv_w, m_a_conv_b, m_a_norm_g, m_a_norm_b, m_a_w_out, m_a_b_out, m_kv_w_down, m_kv_norm_g, m_kv_w_uk, m_kv_w_uv, m_b_w_in, m_b_q_norm_g, m_b_w_uq, m_b_w_out, v_ln_g, v_ln_b, v_a_w_in, v_a_b_in, v_a_conv_w, v_a_conv_b, v_a_norm_g, v_a_norm_b, v_a_w_out, v_a_b_out, v_kv_w_down, v_kv_norm_g, v_kv_w_uk, v_kv_w_uv, v_b_w_in, v_b_q_norm_g, v_b_w_uq, v_b_w_out):
    w = dict(ln_g=ln_g, ln_b=ln_b, a_w_in=a_w_in, a_b_in=a_b_in, a_conv_w=a_conv_w, a_conv_b=a_conv_b,
             a_norm_g=a_norm_g, a_norm_b=a_norm_b, a_w_out=a_w_out, a_b_out=a_b_out, kv_w_down=kv_w_down,
             kv_norm_g=kv_norm_g, kv_w_uk=kv_w_uk, kv_w_uv=kv_w_uv, b_w_in=b_w_in, b_q_norm_g=b_q_norm_g,
             b_w_uq=b_w_uq, b_w_out=b_w_out)
    m = dict(ln_g=m_ln_g, ln_b=m_ln_b, a_w_in=m_a_w_in, a_b_in=m_a_b_in, a_conv_w=m_a_conv_w, a_conv_b=m_a_conv_b,
             a_norm_g=m_a_norm_g, a_norm_b=m_a_norm_b, a_w_out=m_a_w_out, a_b_out=m_a_b_out, kv_w_down=m_kv_w_down,
             kv_norm_g=m_kv_norm_g, kv_w_uk=m_kv_w_uk, kv_w_uv=m_kv_w_uv, b_w_in=m_b_w_in, b_q_norm_g=m_b_q_norm_g,
             b_w_uq=m_b_w_uq, b_w_out=m_b_w_out)
    v = dict(ln_g=v_ln_g, ln_b=v_ln_b, a_w_in=v_a_w_in, a_b_in=v_a_b_in, a_conv_w=v_a_conv_w, a_conv_b=v_a_conv_b,
             a_norm_g=v_a_norm_g, a_norm_b=v_a_norm_b, a_w_out=v_a_w_out, a_b_out=v_a_b_out, kv_w_down=v_kv_w_down,
             kv_norm_g=v_kv_norm_g, kv_w_uk=v_kv_w_uk, kv_w_uv=v_kv_w_uv, b_w_in=v_b_w_in, b_q_norm_g=v_b_q_norm_g,
             b_w_uq=v_b_w_uq, b_w_out=v_b_w_out)

    small_flat = jnp.concatenate([w[n].reshape(-1) for n in SMALL_WEIGHTS]).reshape(1, -1)
    rest_names = [n for n in MATMUL_WEIGHTS if n != 'a_w_in']
    group1 = ('b_w_out', 'b_w_uq', 'b_w_in', 'kv_w_uk', 'kv_w_uv', 'kv_w_down')
    group1b = ('a_w_out',)

    class Comm:
        def __init__(self):
            self.first, self.first_token = _comm_start(
                "gather_first_start", [_shard_2d(w['a_w_in']).astype(BF16), small_flat], True, None, CHIP_LEVEL_PEERS)

        def start_token(self):
            return self.first_token

        def first_weights(self, after):
            lands = _comm_wait("gather_first_wait", self.first, True, after, CHIP_LEVEL_PEERS)
            ga = _forward_to_sibling("gather_first_forward", list(lands))
            full = {'a_w_in': ga[0]}
            gs = ga[1].reshape(N_DEV, -1)
            off = 0
            for n in SMALL_WEIGHTS:
                size = math.prod(w[n].shape)
                full[n] = _gathered_to_full(n, gs[:, off:off + size].reshape((N_DEV,) + _shard_2d(w[n]).shape))
                off += size
            for n in REPLICATED:
                full[n] = w[n]
            dense = lambda a: a if a.shape[-1] % LANES == 0 else a.reshape(-1, LANES)
            self.rest, self.rest_token = _comm_start(
                "gather_rest_start", [dense(_shard_2d(w[n]).astype(BF16)) for n in rest_names], True, ga[0])
            return _prep_weights(full)

        def first_after(self):
            return self.rest_token

        def rest_weights(self, after):
            lands = _comm_wait("gather_rest_wait", self.rest, True, after)
            blocks = {n: lands[i].reshape((N_DEV,) + _shard_2d(w[n]).shape) for i, n in enumerate(rest_names)}
            return _prep_weights({n: _gathered_to_full(n, blocks[n]) for n in rest_names})

        def send_group1(self, g):
            self.g1, token = _comm_start("exchange_g1_start", [g[n] for n in group1], False, None)
            return token

        def send_group1b(self, g):
            self.g1b, token = _comm_start("exchange_g1b_start", [g[n] for n in group1b], False, None)
            return token

        def send_group2(self, dw_lo, small):
            self.g2, token = _comm_start("exchange_g2_start", [dw_lo, _pack_small_grads(*small)], False, None)
            return token

        def send_group3(self, dw_hi):
            self.g3, token = _comm_start("exchange_g3_start", [dw_hi], False, None)
            return token

    comm = Comm()

    _, grad_x = _local_step(x[0], positions[0], loss_target[0], comm)

    res = {}
    recv1 = _comm_wait("exchange_g1_wait", comm.g1, False, grad_x)
    for i, n in enumerate(group1):
        res[n] = _adamw(n, recv1[i], _shard_2d(w[n]), _shard_2d(m[n]), _shard_2d(v[n]))
    recv1b = _comm_wait("exchange_g1b_wait", comm.g1b, False, res[group1[-1]][0])
    for i, n in enumerate(group1b):
        res[n] = _adamw(n, recv1b[i], _shard_2d(w[n]), _shard_2d(m[n]), _shard_2d(v[n]))
    recv2 = _comm_wait("exchange_g2_wait", comm.g2, False, res[group1b[-1]][0])
    recv3 = _comm_wait("exchange_g3_wait", comm.g3, False, recv2[0])
    res['a_w_in'] = _adamw('a_w_in', [recv2[0], recv3[0]], _shard_2d(w['a_w_in']), _shard_2d(m['a_w_in']),
                           _shard_2d(v['a_w_in']))
    two_d = lambda d: [_shard_2d(d[n]) if d[n].ndim > 1 else d[n].reshape(1, -1) for n in SMALL_NAMES]
    sg, sd, sm, sv, loss = _adamw_small(recv2[-1], two_d(w), two_d(m), two_d(v))
    for i, n in enumerate(SMALL_NAMES):
        res[n] = (sg[i], sd[i], sm[i], sv[i])
    outs = [[res[n][j].reshape(w[n].shape) for n in WEIGHT_NAMES] for j in range(4)]
    return (loss[0, 0], grad_x[None], *outs[0], *outs[1], *outs[2], *outs[3])
```

```python
import functools
import math

import jax
import jax.numpy as jnp
from jax import lax
from jax.experimental import pallas as pl
from jax.experimental.pallas import tpu as pltpu

F32 = jnp.float32
BF16 = jnp.bfloat16

D_MODEL = 1024
DEPTH = 2
CONV_WIDTH = 2 * D_MODEL
CONV_KERNEL = 31
N_HEADS = 16
QK_NOPE_DIM = 128
QK_ROPE_DIM = 64
V_HEAD_DIM = 128
KV_LORA_RANK = D_MODEL // 4
Q_LORA_RANK = D_MODEL // 2
ROPE_THETA = 10000.0
LN_EPS = 1e-5
RMS_EPS = 1e-6
MASK_VALUE = -1e30
ALPHA = (2.0 * DEPTH) ** 0.25
ATTN_SCALE = 1.0 / math.sqrt(QK_NOPE_DIM + QK_ROPE_DIM)

ADAM_LR = 0.001
ADAM_B1 = 0.9
ADAM_B2 = 0.999
ADAM_EPS = 1e-08
ADAM_WD = 0.01
ADAM_STEP = 10

N_DEV = 8
LANES = 128
HEAD_PAD = 256
HALO = 32
VMEM_LIMIT = 56 * 1024 * 1024

WEIGHT_NAMES = ['ln_g', 'ln_b', 'a_w_in', 'a_b_in', 'a_conv_w', 'a_conv_b', 'a_norm_g', 'a_norm_b',
                'a_w_out', 'a_b_out', 'kv_w_down', 'kv_norm_g', 'kv_w_uk', 'kv_w_uv', 'b_w_in',
                'b_q_norm_g', 'b_w_uq', 'b_w_out']
REPLICATED = ('ln_g', 'ln_b', 'kv_norm_g', 'b_q_norm_g')
MATMUL_WEIGHTS = ('a_w_in', 'a_w_out', 'kv_w_down', 'kv_w_uk', 'kv_w_uv', 'b_w_in', 'b_w_uq', 'b_w_out')
SMALL_WEIGHTS = ('a_b_in', 'a_conv_w', 'a_conv_b', 'a_norm_g', 'a_norm_b', 'a_b_out')

NN = (((1,), (0,)), ((), ()))
NT = (((1,), (1,)), ((), ()))
TN = (((0,), (0,)), ((), ()))


def _cparams(sem):
    return pltpu.CompilerParams(dimension_semantics=sem, vmem_limit_bytes=VMEM_LIMIT)


def _sigmoid(x):
    return 0.5 * jnp.tanh(0.5 * x) + 0.5


def _fold_rows(x):
    parts = [x[r:r + 8] for r in range(0, x.shape[0], 8)]
    while len(parts) > 1:
        parts = [parts[i] + parts[i + 1] for i in range(0, len(parts) - 1, 2)] + ([parts[-1]] if len(parts) % 2 else [])
    return parts[0]


def _taps_by_residue(offset_of):
    groups = {}
    for j in range(CONV_KERNEL):
        off = offset_of(j)
        groups.setdefault(off % 8, []).append((j, off // 8))
    return sorted(groups.items())


def _swap_rope_halves(t):
    lane = lax.broadcasted_iota(jnp.int32, t.shape, 1)
    return jnp.where(lane < QK_ROPE_DIM // 2, pltpu.roll(t, LANES - QK_ROPE_DIM // 2, 1),
                     pltpu.roll(t, QK_ROPE_DIM // 2, 1))


def _matmul(a, b, *, name, bias=None, trans_a=False, trans_b=False, out_dtype=F32, bm=1024, bn=1024, bk=1024,
            b_blocked=False, out_blocked=False, out_parts=1, after=None, a_cols=None):
    m_off = 0
    if trans_a:
        kdim, m = a.shape
        if a_cols is not None:
            m_off, m = a_cols
    else:
        m, kdim = a.shape
    if b_blocked and trans_b:
        n, k2 = b.shape[1], b.shape[0] * b.shape[2]
        bk = b.shape[2]
    elif b_blocked:
        k2, n = b.shape[1], b.shape[0] * b.shape[2]
        bn = b.shape[2]
    elif trans_b:
        n, k2 = b.shape
    else:
        k2, n = b.shape
    assert kdim == k2, (a.shape, b.shape)
    bm, bn, bk = min(bm, m), min(bn, n), min(bk, kdim)
    assert m % bm == 0 and n % bn == 0 and kdim % bk == 0, (name, m, n, kdim, bm, bn, bk)
    nk = kdim // bk
    assert out_parts == 1 or (out_blocked and bn % out_parts == 0)
    pw = bn // out_parts
    dims = (((0 if trans_a else 1,), (1 if trans_b else 0,)), ((), ()))
    has_bias = bias is not None

    def body(*refs):
        refs = list(refs)
        a_ref, b_ref = refs[:2]
        del refs[:2]
        bias_ref = refs.pop(0) if has_bias else None
        if after is not None:
            refs.pop(0)
        o_ref = refs.pop(0)
        rest = refs
        prod = lax.dot_general(a_ref[...], b_ref[...], dims, preferred_element_type=F32)

        def finish(acc):
            if has_bias:
                acc = acc + bias_ref[...]
            if out_parts == 1:
                o_ref[...] = acc.astype(out_dtype)
            else:
                for q in range(out_parts):
                    o_ref[q] = acc[:, q * pw:(q + 1) * pw].astype(out_dtype)

        if nk == 1:
            finish(prod)
        else:
            acc_ref = rest[0]
            k = pl.program_id(2)

            @pl.when(k == 0)
            def _():
                acc_ref[...] = prod

            @pl.when(k > 0)
            def _():
                acc_ref[...] += prod

            @pl.when(k == nk - 1)
            def _():
                finish(acc_ref[...])

    assert m_off % bm == 0
    a_first = m_off // bm
    a_spec = (pl.BlockSpec((bk, bm), lambda i, j, k: (k, i + a_first)) if trans_a
              else pl.BlockSpec((bm, bk), lambda i, j, k: (i, k)))
    if b_blocked and trans_b:
        b_spec = pl.BlockSpec((None, bn, bk), lambda i, j, k: (k, j, 0))
    elif b_blocked:
        b_spec = pl.BlockSpec((None, bk, bn), lambda i, j, k: (j, k, 0))
    elif trans_b:
        b_spec = pl.BlockSpec((bn, bk), lambda i, j, k: (j, k))
    else:
        b_spec = pl.BlockSpec((bk, bn), lambda i, j, k: (k, j))
    in_specs = [a_spec, b_spec]
    args = [a, b]
    if has_bias:
        in_specs.append(pl.BlockSpec((1, bn), lambda i, j, k: (0, j)))
        args.append(bias)
    if after is not None:
        in_specs.append(pl.BlockSpec(memory_space=pl.ANY))
        args.append(after)
    if out_blocked and out_parts > 1:
        out_struct = jax.ShapeDtypeStruct((n // pw, m, pw), out_dtype)
        out_spec = pl.BlockSpec((out_parts, bm, pw), lambda i, j, k: (j, i, 0))
    elif out_blocked:
        out_struct = jax.ShapeDtypeStruct((n // bn, m, bn), out_dtype)
        out_spec = pl.BlockSpec((None, bm, bn), lambda i, j, k: (j, i, 0))
    else:
        out_struct = jax.ShapeDtypeStruct((m, n), out_dtype)
        out_spec = pl.BlockSpec((bm, bn), lambda i, j, k: (i, j))
    return pl.pallas_call(
        body, name=name,
        out_shape=out_struct,
        grid=(m // bm, n // bn, nk),
        in_specs=in_specs,
        out_specs=out_spec,
        scratch_shapes=[pltpu.VMEM((bm, bn), F32)] if nk > 1 else [],
        compiler_params=_cparams(("parallel", "parallel", "arbitrary")),
    )(*args)


def _rope_tables(pos_col, freq_row, x, after):
    t, d = x.shape
    tm = min(t, 512)
    extra = [] if after is None else [after]

    def body(pos_ref, f_ref, x_ref, *rest):
        c_ref, s_ref, xb_ref = rest[-3:]
        ang = pos_ref[...].astype(F32) * f_ref[...]
        lane = lax.broadcasted_iota(jnp.int32, ang.shape, 1)
        cos = jnp.cos(ang)
        sin = jnp.sin(ang)
        c_ref[...] = jnp.where(lane < QK_ROPE_DIM, cos, 0.0)
        s_ref[...] = jnp.where(lane < QK_ROPE_DIM // 2, -sin, jnp.where(lane < QK_ROPE_DIM, sin, 0.0))
        xb_ref[...] = x_ref[...].astype(BF16)

    lane_blk = pl.BlockSpec((tm, LANES), lambda i: (i, 0))
    row = pl.BlockSpec((tm, d), lambda i: (i, 0))
    return pl.pallas_call(
        body, name="rope_tables",
        out_shape=(jax.ShapeDtypeStruct((t, LANES), F32), jax.ShapeDtypeStruct((t, LANES), F32),
                   jax.ShapeDtypeStruct((t, d), BF16)),
        grid=(t // tm,),
        in_specs=[pl.BlockSpec((tm, 1), lambda i: (i, 0)), pl.BlockSpec((1, LANES), lambda i: (0, 0)), row]
                 + [pl.BlockSpec(memory_space=pl.ANY)] * len(extra),
        out_specs=(lane_blk, lane_blk, row),
        compiler_params=_cparams(("parallel",)),
    )(pos_col, freq_row, x, *extra)


def _conv_mid_fwd(proj, cw, cb, ng, nb):
    t = proj.shape[0]
    e = CONV_WIDTH
    tm = min(t, 128)
    hb = tm // HALO
    nchunk = e // LANES

    def body(val_ref, gg_ref, z_ref, valh_ref, ggh_ref, cw_ref, cb_ref, ng_ref, nb_ref, u1_ref, u2_ref, ext_ref, sh_ref):
        i = pl.program_id(0)
        u0 = val_ref[...] * _sigmoid(gg_ref[...])
        h0 = valh_ref[...] * _sigmoid(ggh_ref[...])
        ext_ref[0:HALO, :] = jnp.where(i > 0, h0, 0.0)
        ext_ref[HALO:, :] = u0

        def chunk(c, carry):
            col = pl.multiple_of(c * LANES, LANES)
            acc = jnp.zeros((tm, LANES), F32)
            for res, taps in _taps_by_residue(lambda j: HALO - (CONV_KERNEL - 1) + j):
                span = 8 * max(a for _, a in taps) + tm
                sh_ref[0:span, :] = ext_ref[res:res + span, pl.ds(col, LANES)]
                for j, a in taps:
                    acc = acc + cw_ref[j:j + 1, pl.ds(col, LANES)] * sh_ref[8 * a:8 * a + tm, :]
            u1_ref[:, pl.ds(col, LANES)] = acc + cb_ref[:, pl.ds(col, LANES)]
            return carry

        lax.fori_loop(0, nchunk, chunk, 0)
        g = ng_ref[...]
        b = nb_ref[...]
        for r in range(0, tm, 16):
            u1 = u1_ref[r:r + 16, :]
            mu = jnp.mean(u1, axis=-1, keepdims=True)
            xc = u1 - mu
            var = jnp.mean(xc * xc, axis=-1, keepdims=True)
            n = xc * lax.rsqrt(var + LN_EPS) * g + b
            z = z_ref[r:r + 16, :]
            u2_ref[r:r + 16, :] = ((n * _sigmoid(n)) * (z * _sigmoid(z))).astype(BF16)

    row = lambda c: pl.BlockSpec((tm, e), lambda i: (i, c))
    halo = lambda c: pl.BlockSpec((HALO, e), lambda i: (jnp.maximum(i * hb - 1, 0), c))
    par = lambda r: pl.BlockSpec((r, e), lambda i: (0, 0))
    return pl.pallas_call(
        body, name="conv_mid_fwd",
        out_shape=(jax.ShapeDtypeStruct((t, e), F32), jax.ShapeDtypeStruct((t, e), BF16)),
        grid=(t // tm,),
        in_specs=[row(0), row(1), row(2), halo(0), halo(1), par(HALO), par(1), par(1), par(1)],
        out_specs=(pl.BlockSpec((tm, e), lambda i: (i, 0)), pl.BlockSpec((tm, e), lambda i: (i, 0))),
        scratch_shapes=[pltpu.VMEM((tm + HALO, e), F32), pltpu.VMEM((tm + HALO, LANES), F32)],
        compiler_params=_cparams(("parallel",)),
    )(proj, proj, proj, proj, proj, cw, cb, ng, nb)


def _ln_res_fwd(x, u, w_out, b_out, g, b):
    t, d = x.shape
    ku = u.shape[1]
    tm = min(t, 512)

    def body(x_ref, u_ref, w_ref, bo_ref, g_ref, b_ref, h_ref, hb_ref, xh_ref, rs_ref):
        y = lax.dot_general(u_ref[...], w_ref[...], NN, preferred_element_type=F32) + bo_ref[...]
        r = ALPHA * x_ref[...] + y
        mu = jnp.mean(r, axis=-1, keepdims=True)
        xc = r - mu
        var = jnp.mean(xc * xc, axis=-1, keepdims=True)
        rstd = lax.rsqrt(var + LN_EPS)
        xh = xc * rstd
        h = xh * g_ref[...] + b_ref[...]
        h_ref[...] = h
        hb_ref[...] = h.astype(BF16)
        xh_ref[...] = xh
        rs_ref[...] = rstd

    row = pl.BlockSpec((tm, d), lambda i: (i, 0))
    par = pl.BlockSpec((1, d), lambda i: (0, 0))
    return pl.pallas_call(
        body, name="ln0_fwd",
        out_shape=(jax.ShapeDtypeStruct((t, d), F32), jax.ShapeDtypeStruct((t, d), BF16),
                   jax.ShapeDtypeStruct((t, d), F32), jax.ShapeDtypeStruct((t, 1), F32)),
        grid=(t // tm,),
        in_specs=[row, pl.BlockSpec((tm, ku), lambda i: (i, 0)),
                  pl.BlockSpec((ku, d), lambda i: (0, 0), pipeline_mode=pl.Buffered(1)), par, par, par],
        out_specs=(row, row, row, pl.BlockSpec((tm, 1), lambda i: (i, 0))),
        compiler_params=_cparams(("parallel",)),
    )(x, u, w_out, b_out, g, b)


def _kv_mid_fwd(h1b, w_down, w_ukv, g, rc, rs):
    t, d = h1b.shape
    nkv = w_ukv.shape[1]
    tm = min(t, 512)
    r = KV_LORA_RANK

    def body(h_ref, w_ref, wu_ref, g_ref, c_ref, s_ref, ckv_ref, ckn_ref, kr_ref, kv_ref):
        ckv = lax.dot_general(h_ref[...], w_ref[...], NN, preferred_element_type=F32)
        ckv_ref[...] = ckv
        c = ckv[:, 0:r]
        ms = jnp.mean(c * c, axis=-1, keepdims=True)
        ckn = (c * lax.rsqrt(ms + RMS_EPS) * g_ref[...]).astype(BF16)
        ckn_ref[...] = ckn
        tr = ckv[:, r:r + LANES]
        kr_ref[...] = (tr * c_ref[...] + _swap_rope_halves(tr) * s_ref[...]).astype(BF16)
        kv_ref[...] = lax.dot_general(ckn, wu_ref[...], NN, preferred_element_type=F32).astype(BF16)

    lane_blk = pl.BlockSpec((tm, LANES), lambda i: (i, 0))
    return pl.pallas_call(
        body, name="kv_mid_fwd",
        out_shape=(jax.ShapeDtypeStruct((t, r + LANES), F32), jax.ShapeDtypeStruct((t, r), BF16),
                   jax.ShapeDtypeStruct((t, LANES), BF16), jax.ShapeDtypeStruct((t, nkv), BF16)),
        grid=(t // tm,),
        in_specs=[pl.BlockSpec((tm, d), lambda i: (i, 0)),
                  pl.BlockSpec((d, r + LANES), lambda i: (0, 0), pipeline_mode=pl.Buffered(1)),
                  pl.BlockSpec((r, nkv), lambda i: (0, 0), pipeline_mode=pl.Buffered(1)),
                  pl.BlockSpec((1, r), lambda i: (0, 0)), lane_blk, lane_blk],
        out_specs=(pl.BlockSpec((tm, r + LANES), lambda i: (i, 0)), pl.BlockSpec((tm, r), lambda i: (i, 0)), lane_blk,
                   pl.BlockSpec((tm, nkv), lambda i: (i, 0))),
        compiler_params=_cparams(("parallel",)),
    )(h1b, w_down, w_ukv, g, rc, rs)


def _b_in_qnorm(h1b, w_blocks, g):
    t, d = h1b.shape
    nblk, _, cb = w_blocks.shape
    n = nblk * cb
    r = Q_LORA_RANK
    tm = min(t, 512)

    def body(a_ref, wb_ref, g_ref, o_ref, cqn_ref, w_ref):
        @pl.when(pl.program_id(0) == 0)
        def _():
            for j in range(nblk):
                w_ref[:, j * cb:(j + 1) * cb] = wb_ref[j]

        acc = lax.dot_general(a_ref[...], w_ref[...], NN, preferred_element_type=F32)
        o_ref[...] = acc
        c = acc[:, 0:r]
        ms = jnp.mean(c * c, axis=-1, keepdims=True)
        cqn_ref[...] = (c * lax.rsqrt(ms + RMS_EPS) * g_ref[...]).astype(BF16)

    return pl.pallas_call(
        body, name="b_in",
        out_shape=(jax.ShapeDtypeStruct((t, n), F32), jax.ShapeDtypeStruct((t, r), BF16),
                   jax.ShapeDtypeStruct((d, n), BF16)),
        grid=(t // tm,),
        in_specs=[pl.BlockSpec((tm, d), lambda i: (i, 0)),
                  pl.BlockSpec((nblk, d, cb), lambda i: (0, 0, 0), pipeline_mode=pl.Buffered(1)),
                  pl.BlockSpec((1, r), lambda i: (0, 0))],
        out_specs=(pl.BlockSpec((tm, n), lambda i: (i, 0)), pl.BlockSpec((tm, r), lambda i: (i, 0)),
                   pl.BlockSpec((d, n), lambda i: (0, 0))),
        compiler_params=_cparams(("arbitrary",)),
    )(h1b, w_blocks, g)


def _q_up_rope(cqn, wuq, rc, rs):
    t, r = cqn.shape
    w = wuq.shape[1]
    tm = min(t, 1024)
    bn = 4 * HEAD_PAD

    def body(a_ref, b_ref, c_ref, s_ref, o_ref):
        q = lax.dot_general(a_ref[...], b_ref[...], NN, preferred_element_type=F32)
        c = c_ref[...]
        s = s_ref[...]
        for h in range(bn // HEAD_PAD):
            a = h * HEAD_PAD
            o_ref[:, a:a + LANES] = q[:, a:a + LANES].astype(BF16)
            tr = q[:, a + LANES:a + 2 * LANES]
            o_ref[:, a + LANES:a + 2 * LANES] = (tr * c + _swap_rope_halves(tr) * s).astype(BF16)

    return pl.pallas_call(
        body, name="q_up_rope",
        out_shape=jax.ShapeDtypeStruct((t, w), BF16),
        grid=(t // tm, w // bn),
        in_specs=[pl.BlockSpec((tm, r), lambda i, j: (i, 0)), pl.BlockSpec((r, bn), lambda i, j: (0, j)),
                  pl.BlockSpec((tm, LANES), lambda i, j: (i, 0)), pl.BlockSpec((tm, LANES), lambda i, j: (i, 0))],
        out_specs=pl.BlockSpec((tm, bn), lambda i, j: (i, j)),
        compiler_params=_cparams(("parallel", "parallel")),
    )(cqn, wuq, rc, rs)


def _flash_fwd(qcat, kv, kr, projb):
    t = qcat.shape[0]
    tq = min(t, 512)
    nq = t // tq
    exp2_scale = ATTN_SCALE * math.log2(math.e)
    z_first = Q_LORA_RANK // LANES

    def body(q_ref, kv_ref, kr_ref, z_ref, o_ref, o2_ref, lse_ref, kcat_ref):
        kcat_ref[:, 0:LANES] = kv_ref[:, 0:LANES]
        kcat_ref[:, LANES:] = kr_ref[...]
        rows = lax.broadcasted_iota(jnp.int32, (tq, tq), 0)
        cols = lax.broadcasted_iota(jnp.int32, (tq, tq), 1)
        for i in range(nq):
            a = i * tq
            q = q_ref[a:a + tq, :]
            sd = lax.dot_general(q, kcat_ref[a:a + tq, :], NT, preferred_element_type=F32)
            sd = jnp.where(cols <= rows, sd, MASK_VALUE)
            m = jnp.max(sd, axis=1, keepdims=True)
            if i > 0:
                sp = lax.dot_general(q, kcat_ref[0:a, :], NT, preferred_element_type=F32)
                m = jnp.maximum(m, jnp.max(sp, axis=1, keepdims=True))
            pd = jnp.exp2((sd - m) * exp2_scale)
            l = jnp.sum(pd, axis=1, keepdims=True)
            acc = lax.dot_general(pd.astype(BF16), kv_ref[a:a + tq, LANES:], NN, preferred_element_type=F32)
            if i > 0:
                pp = jnp.exp2((sp - m) * exp2_scale)
                l = l + jnp.sum(pp, axis=1, keepdims=True)
                acc = acc + lax.dot_general(pp.astype(BF16), kv_ref[0:a, LANES:], NN, preferred_element_type=F32)
            o = acc / l
            z = z_ref[a:a + tq, :]
            o_ref[a:a + tq, :] = o
            o2_ref[a:a + tq, :] = (o * (z * _sigmoid(z))).astype(BF16)
            lse_ref[a:a + tq, :] = jnp.broadcast_to(m * ATTN_SCALE + jnp.log(l), (tq, LANES))

    hw = N_HEADS * LANES
    head256 = pl.BlockSpec((t, HEAD_PAD), lambda h: (0, h))
    head128 = pl.BlockSpec((t, LANES), lambda h: (0, h))
    return pl.pallas_call(
        body, name="flash_fwd",
        out_shape=(jax.ShapeDtypeStruct((t, hw), F32), jax.ShapeDtypeStruct((t, hw), BF16),
                   jax.ShapeDtypeStruct((t, hw), F32)),
        grid=(N_HEADS,),
        in_specs=[head256, head256, pl.BlockSpec((t, LANES), lambda h: (0, 0), pipeline_mode=pl.Buffered(1)),
                  pl.BlockSpec((t, LANES), lambda h: (0, z_first + h))],
        out_specs=(head128, head128, head128),
        scratch_shapes=[pltpu.VMEM((t, HEAD_PAD), BF16)],
        compiler_params=_cparams(("parallel",)),
    )(qcat, kv, kr, projb)


def _final_ln_loss(h1, o2, w_out, g, b, target):
    t, d = h1.shape
    ku = o2.shape[1]
    tm = min(t, 512)
    nblk = t // tm

    def step(y_new, y_old, h_ref, u_ref, w_ref, g_ref, b_ref, t_ref, loss_ref, dr_ref, drb_ref, dg_ref, db_ref):
        i = pl.program_id(0)
        live = i > 0
        y_new[...] = lax.dot_general(u_ref[...], w_ref[...], NN, preferred_element_type=F32)
        r = ALPHA * h_ref[...] + y_old[...]
        mu = jnp.mean(r, axis=-1, keepdims=True)
        xc = r - mu
        var = jnp.mean(xc * xc, axis=-1, keepdims=True)
        rstd = lax.rsqrt(var + LN_EPS)
        xh = xc * rstd
        g = g_ref[...]
        diff = xh * g + b_ref[...] - t_ref[...]
        part = 0.5 * jnp.sum(jnp.mean(diff * diff, axis=-1, keepdims=True))
        dh = diff * (1.0 / d)
        dgp = jnp.sum(dh * xh, axis=0, keepdims=True)
        dbp = jnp.sum(dh, axis=0, keepdims=True)

        loss_ref[...] += jnp.where(live, jnp.full(loss_ref.shape, part, F32), 0.0)
        dg_ref[...] += jnp.where(live, dgp, 0.0)
        db_ref[...] += jnp.where(live, dbp, 0.0)
        dxh = dh * g
        dr = rstd * (dxh - jnp.mean(dxh, axis=-1, keepdims=True) - xh * jnp.mean(dxh * xh, axis=-1, keepdims=True))
        dr_ref[...] = dr
        drb_ref[...] = dr.astype(BF16)

    def body(*refs):
        io, (ya_ref, yb_ref) = refs[:-2], refs[-2:]
        loss_ref, dg_ref, db_ref = io[6], io[9], io[10]
        i = pl.program_id(0)

        @pl.when(i == 0)
        def _():
            loss_ref[...] = jnp.zeros_like(loss_ref)
            dg_ref[...] = jnp.zeros_like(dg_ref)
            db_ref[...] = jnp.zeros_like(db_ref)
            yb_ref[...] = jnp.zeros_like(yb_ref)

        pl.when(i % 2 == 0)(functools.partial(step, ya_ref, yb_ref, *io))
        pl.when(i % 2 == 1)(functools.partial(step, yb_ref, ya_ref, *io))

    behind = pl.BlockSpec((tm, d), lambda i: (jnp.maximum(i - 1, 0), 0))
    par = pl.BlockSpec((1, d), lambda i: (0, 0))
    return pl.pallas_call(
        body, name="final_ln_loss",
        out_shape=(jax.ShapeDtypeStruct((1, LANES), F32), jax.ShapeDtypeStruct((t, d), F32),
                   jax.ShapeDtypeStruct((t, d), BF16), jax.ShapeDtypeStruct((1, d), F32),
                   jax.ShapeDtypeStruct((1, d), F32)),
        grid=(nblk + 1,),
        in_specs=[behind, pl.BlockSpec((tm, ku), lambda i: (jnp.minimum(i, nblk - 1), 0)),
                  pl.BlockSpec((ku, d), lambda i: (0, 0), pipeline_mode=pl.Buffered(1)), par, par, behind],
        out_specs=(pl.BlockSpec((1, LANES), lambda i: (0, 0)), behind, behind, par, par),
        scratch_shapes=[pltpu.VMEM((tm, d), F32), pltpu.VMEM((tm, d), F32)],
        compiler_params=_cparams(("arbitrary",)),
    )(h1, o2, w_out, g, b, target)


def _gate_bwd(dr2b, w_out, o, projb):
    t, hw = o.shape
    d = dr2b.shape[1]
    tm = min(t, 1024)
    bw = Q_LORA_RANK
    nb = hw // bw
    wb = projb.shape[1]

    def body(dr_ref, w_ref, o_ref, z_ref, dob_ref, dz_ref):
        z = z_ref[...]
        sg = _sigmoid(z)
        w = w_ref[pl.ds(pl.multiple_of(pl.program_id(1) * bw, bw), bw), :]
        d2 = lax.dot_general(dr_ref[...], w, NT, preferred_element_type=F32)
        dob_ref[...] = (d2 * (z * sg)).astype(BF16)
        dz_ref[...] = (d2 * o_ref[...] * (sg * (1.0 + z * (1.0 - sg)))).astype(BF16)

    blk = pl.BlockSpec((tm, bw), lambda i, j: (i, j))
    blk1 = pl.BlockSpec((tm, bw), lambda i, j: (i, j + 1))
    return pl.pallas_call(
        body, name="gate_bwd",
        out_shape=(jax.ShapeDtypeStruct((t, hw), BF16), jax.ShapeDtypeStruct((t, wb), BF16)),
        grid=(t // tm, nb),
        in_specs=[pl.BlockSpec((tm, d), lambda i, j: (i, 0)),
                  pl.BlockSpec((hw, d), lambda i, j: (0, 0), pipeline_mode=pl.Buffered(1)), blk, blk1],
        out_specs=(blk, blk1),
        compiler_params=_cparams(("parallel", "arbitrary")),
    )(dr2b, w_out, o, projb)


def _flash_bwd(qcat, kv, kr, dob, lse, o, rc, rs):
    t = qcat.shape[0]
    tq = min(t, 512)
    nq = t // tq
    q_chunk = 2 * tq
    log2e = math.log2(math.e)
    exp2_scale = ATTN_SCALE * log2e
    once = pl.Buffered(1)

    def body(q_ref, kv_ref, kr_ref, do_ref, lse_ref, o_ref, c_ref, s_ref,
             dq_ref, dkv_ref, dkr_ref, kcat_ref, dqa_ref, dka_ref, dva_ref, dl_ref):
        h = pl.program_id(0)
        kcat_ref[:, 0:LANES] = kv_ref[:, 0:LANES]
        kcat_ref[:, LANES:] = kr_ref[...]
        dqa_ref[...] = jnp.zeros_like(dqa_ref)
        dl_ref[...] = jnp.sum(do_ref[...].astype(F32) * o_ref[...], axis=1, keepdims=True)

        @pl.when(h == 0)
        def _():
            dkr_ref[...] = jnp.zeros_like(dkr_ref)

        rows = lax.broadcasted_iota(jnp.int32, (tq, tq), 0)
        cols = lax.broadcasted_iota(jnp.int32, (tq, tq), 1)

        def block(qs, n, ks, masked):
            q = q_ref[qs:qs + n, :]
            kc = kcat_ref[ks:ks + tq, :]
            s = lax.dot_general(q, kc, NT, preferred_element_type=F32)
            if masked:
                s = jnp.where(cols <= rows, s, MASK_VALUE)
            p = jnp.exp2(s * exp2_scale - lse_ref[qs:qs + n, 0:1] * log2e)
            do = do_ref[qs:qs + n, :]
            dp = lax.dot_general(do, kv_ref[ks:ks + tq, LANES:], NT, preferred_element_type=F32)
            ds = (p * (dp - dl_ref[qs:qs + n, :])).astype(BF16)
            dva_ref[...] += lax.dot_general(p.astype(BF16), do, TN, preferred_element_type=F32)
            dka_ref[...] += lax.dot_general(ds, q, TN, preferred_element_type=F32)
            dqa_ref[qs:qs + n, :] += lax.dot_general(ds, kc, NN, preferred_element_type=F32)

        for kb in range(nq):
            a = kb * tq
            dka_ref[...] = jnp.zeros_like(dka_ref)
            dva_ref[...] = jnp.zeros_like(dva_ref)
            block(a, tq, a, True)
            qs = a + tq
            while qs < t:
                n = min(q_chunk, t - qs)
                block(qs, n, a, False)
                qs += n
            dk = dka_ref[...] * ATTN_SCALE
            dkv_ref[a:a + tq, 0:LANES] = dk[:, 0:LANES].astype(BF16)
            dkv_ref[a:a + tq, LANES:] = dva_ref[...].astype(BF16)
            dkr_ref[a:a + tq, :] += dk[:, LANES:]

        dq = dqa_ref[...] * ATTN_SCALE
        dq_ref[:, 0:LANES] = dq[:, 0:LANES].astype(BF16)
        d2 = dq[:, LANES:]
        dq_ref[:, LANES:] = (d2 * c_ref[...] - _swap_rope_halves(d2) * s_ref[...]).astype(BF16)

    hp = N_HEADS * HEAD_PAD
    head256 = pl.BlockSpec((t, HEAD_PAD), lambda h: (0, h))
    head128 = pl.BlockSpec((t, LANES), lambda h: (0, h))
    const128 = pl.BlockSpec((t, LANES), lambda h: (0, 0), pipeline_mode=once)
    return pl.pallas_call(
        body, name="flash_bwd",
        out_shape=(jax.ShapeDtypeStruct((t, hp), BF16), jax.ShapeDtypeStruct((t, hp), BF16),
                   jax.ShapeDtypeStruct((t, LANES), F32)),
        grid=(N_HEADS,),
        in_specs=[head256, head256, const128, head128, head128, head128, const128, const128],
        out_specs=(head256, head256, pl.BlockSpec((t, LANES), lambda h: (0, 0))),
        scratch_shapes=[pltpu.VMEM((t, HEAD_PAD), BF16), pltpu.VMEM((t, HEAD_PAD), F32),
                        pltpu.VMEM((tq, HEAD_PAD), F32), pltpu.VMEM((tq, LANES), F32), pltpu.VMEM((t, 1), F32)],
        compiler_params=_cparams(("arbitrary",)),
    )(qcat, kv, kr, dob, lse, o, rc, rs)


def _q_norm_bwd(dq2, w_uq, projb, g, dprojb):
    t = projb.shape[0]
    kq = dq2.shape[1]
    tm = min(t, 512)
    r = Q_LORA_RANK

    def body(dq_ref, w_ref, c_ref, g_ref, alias_ref, o_ref, dg_ref):
        i = pl.program_id(0)
        c = c_ref[...]
        d = lax.dot_general(dq_ref[...], w_ref[...], NT, preferred_element_type=F32)
        rr = lax.rsqrt(jnp.mean(c * c, axis=-1, keepdims=True) + RMS_EPS)
        xh = c * rr

        @pl.when(i == 0)
        def _():
            dg_ref[...] = jnp.zeros_like(dg_ref)

        dg_ref[...] += jnp.sum(d * xh, axis=0, keepdims=True)
        dxh = d * g_ref[...]
        o_ref[...] = (rr * (dxh - xh * jnp.mean(dxh * xh, axis=-1, keepdims=True))).astype(BF16)

    blk = pl.BlockSpec((tm, r), lambda i: (i, 0))
    par = pl.BlockSpec((1, r), lambda i: (0, 0))
    return pl.pallas_call(
        body, name="q_norm_bwd",
        out_shape=(jax.ShapeDtypeStruct(dprojb.shape, BF16), jax.ShapeDtypeStruct((1, r), F32)),
        grid=(t // tm,),
        in_specs=[pl.BlockSpec((tm, kq), lambda i: (i, 0)),
                  pl.BlockSpec((r, kq), lambda i: (0, 0), pipeline_mode=pl.Buffered(1)),
                  blk, par, pl.BlockSpec(memory_space=pl.ANY)],
        out_specs=(blk, par),
        input_output_aliases={4: 0},
        compiler_params=_cparams(("arbitrary",)),
    )(dq2, w_uq, projb, g, dprojb)


def _kv_mid_bwd(dkv, w_ukv, ckv, dkr, g, rc, rs):
    t = ckv.shape[0]
    kk = dkv.shape[1]
    tm = min(t, 512)
    r = KV_LORA_RANK

    def body(dkv_ref, w_ref, ckv_ref, dkr_ref, g_ref, c_ref, s_ref, o_ref, dg_ref):
        i = pl.program_id(0)
        c = ckv_ref[:, 0:r]
        d = lax.dot_general(dkv_ref[...], w_ref[...], NT, preferred_element_type=F32)
        rr = lax.rsqrt(jnp.mean(c * c, axis=-1, keepdims=True) + RMS_EPS)
        xh = c * rr

        @pl.when(i == 0)
        def _():
            dg_ref[...] = jnp.zeros_like(dg_ref)

        dg_ref[...] += jnp.sum(d * xh, axis=0, keepdims=True)
        dxh = d * g_ref[...]
        o_ref[:, 0:r] = (rr * (dxh - xh * jnp.mean(dxh * xh, axis=-1, keepdims=True))).astype(BF16)
        dk = dkr_ref[...]
        o_ref[:, r:] = (dk * c_ref[...] - _swap_rope_halves(dk) * s_ref[...]).astype(BF16)

    return pl.pallas_call(
        body, name="kv_mid_bwd",
        out_shape=(jax.ShapeDtypeStruct((t, r + LANES), BF16), jax.ShapeDtypeStruct((1, r), F32)),
        grid=(t // tm,),
        in_specs=[pl.BlockSpec((tm, kk), lambda i: (i, 0)),
                  pl.BlockSpec((r, kk), lambda i: (0, 0), pipeline_mode=pl.Buffered(1)),
                  pl.BlockSpec((tm, r + LANES), lambda i: (i, 0)),
                  pl.BlockSpec((tm, LANES), lambda i: (i, 0)), pl.BlockSpec((1, r), lambda i: (0, 0)),
                  pl.BlockSpec((tm, LANES), lambda i: (i, 0)), pl.BlockSpec((tm, LANES), lambda i: (i, 0))],
        out_specs=(pl.BlockSpec((tm, r + LANES), lambda i: (i, 0)), pl.BlockSpec((1, r), lambda i: (0, 0))),
        compiler_params=_cparams(("arbitrary",)),
    )(dkv, w_ukv, ckv, dkr, g, rc, rs)


def _ln0_bwd(dr2, dprojb, w_in, dckv, w_down, xh, rstd, g, after):
    t, d = dr2.shape
    kb, kd = dprojb.shape[1], dckv.shape[1]
    tm = min(t, 256)
    extra = [] if after is None else [after]

    def body(a_ref, pb_ref, wb_ref, pd_ref, wd_ref, xh_ref, rs_ref, g_ref, *rest):
        dr_ref, drb_ref, dg_ref, db_ref, dsum_ref = rest[-5:]
        i = pl.program_id(0)
        dh = (ALPHA * a_ref[...] + lax.dot_general(pb_ref[...], wb_ref[...], NT, preferred_element_type=F32)
              + lax.dot_general(pd_ref[...], wd_ref[...], NT, preferred_element_type=F32))
        xh = xh_ref[...]

        @pl.when(i == 0)
        def _():
            dg_ref[...] = jnp.zeros_like(dg_ref)
            db_ref[...] = jnp.zeros_like(db_ref)
            dsum_ref[...] = jnp.zeros_like(dsum_ref)

        dg_ref[...] += jnp.sum(dh * xh, axis=0, keepdims=True)
        db_ref[...] += jnp.sum(dh, axis=0, keepdims=True)
        dxh = dh * g_ref[...]
        dr = rs_ref[...] * (dxh - jnp.mean(dxh, axis=-1, keepdims=True) - xh * jnp.mean(dxh * xh, axis=-1, keepdims=True))
        dr_ref[...] = dr
        drb_ref[...] = dr.astype(BF16)
        dsum_ref[...] += jnp.sum(dr, axis=0, keepdims=True)

    row = pl.BlockSpec((tm, d), lambda i: (i, 0))
    par = pl.BlockSpec((1, d), lambda i: (0, 0))
    return pl.pallas_call(
        body, name="ln0_bwd",
        out_shape=(jax.ShapeDtypeStruct((t, d), F32), jax.ShapeDtypeStruct((t, d), BF16),
                   jax.ShapeDtypeStruct((1, d), F32), jax.ShapeDtypeStruct((1, d), F32),
                   jax.ShapeDtypeStruct((1, d), F32)),
        grid=(t // tm,),
        in_specs=[row, pl.BlockSpec((tm, kb), lambda i: (i, 0)),
                  pl.BlockSpec((d, kb), lambda i: (0, 0), pipeline_mode=pl.Buffered(1)),
                  pl.BlockSpec((tm, kd), lambda i: (i, 0)),
                  pl.BlockSpec((d, kd), lambda i: (0, 0), pipeline_mode=pl.Buffered(1)),
                  row, pl.BlockSpec((tm, 1), lambda i: (i, 0)), par]
                 + [pl.BlockSpec(memory_space=pl.ANY)] * len(extra),
        out_specs=(row, row, par, par, par),
        compiler_params=_cparams(("arbitrary",)),
    )(dr2, dprojb, w_in, dckv, w_down, xh, rstd, g, *extra)


def _conv_mid_bwd_rows(u1, proj, dr1b, w_out, ng, nb, after):
    t = u1.shape[0]
    e = CONV_WIDTH
    d = dr1b.shape[1]
    tm = min(t, 256)
    extra = [] if after is None else [after]

    rg = 16
    nt = t // tm

    def body(u1_ref, z_ref, dr_ref, w_ref, ng_ref, nb_ref, *rest):
        du1_ref, dz_ref, dng_ref, dnb_ref, dbz_ref, acc_ref, d_ref = rest[-7:]
        i = pl.program_id(0)

        @pl.when(i == 0)
        def _():
            acc_ref[...] = jnp.zeros_like(acc_ref)

        d_ref[...] = lax.dot_general(dr_ref[...], w_ref[...], NT, preferred_element_type=F32)
        g = ng_ref[...]
        b = nb_ref[...]

        def group(r, carry):
            rows = pl.ds(pl.multiple_of(r * rg, rg), rg)
            u1 = u1_ref[rows, :]
            mu = jnp.mean(u1, axis=-1, keepdims=True)
            xc = u1 - mu
            rstd = lax.rsqrt(jnp.mean(xc * xc, axis=-1, keepdims=True) + LN_EPS)
            xh = xc * rstd
            n = xh * g + b
            sn = _sigmoid(n)
            z = z_ref[rows, :]
            sz = _sigmoid(z)
            du2 = d_ref[rows, :]
            dz = du2 * (n * sn) * (sz * (1.0 + z * (1.0 - sz)))
            dn = du2 * (z * sz) * (sn * (1.0 + n * (1.0 - sn)))
            acc_ref[0:8, :] += _fold_rows(dn * xh)
            acc_ref[8:16, :] += _fold_rows(dn)
            acc_ref[16:24, :] += _fold_rows(dz)
            dz_ref[rows, :] = dz.astype(BF16)
            dxh = dn * g
            du1_ref[rows, :] = rstd * (dxh - jnp.mean(dxh, axis=-1, keepdims=True)
                                       - xh * jnp.mean(dxh * xh, axis=-1, keepdims=True))
            return carry

        lax.fori_loop(0, tm // rg, group, 0, unroll=8)

        @pl.when(i == nt - 1)
        def _():
            dng_ref[...] = jnp.sum(acc_ref[0:8, :], axis=0, keepdims=True)
            dnb_ref[...] = jnp.sum(acc_ref[8:16, :], axis=0, keepdims=True)
            dbz_ref[...] = jnp.sum(acc_ref[16:24, :], axis=0, keepdims=True)

    row = pl.BlockSpec((tm, e), lambda i: (i, 0))
    par = pl.BlockSpec((1, e), lambda i: (0, 0))
    return pl.pallas_call(
        body, name="conv_mid_bwd_rows",
        out_shape=(jax.ShapeDtypeStruct((t, e), F32), jax.ShapeDtypeStruct((t, 3 * e), BF16),
                   jax.ShapeDtypeStruct((1, e), F32), jax.ShapeDtypeStruct((1, e), F32),
                   jax.ShapeDtypeStruct((1, e), F32)),
        grid=(nt,),
        in_specs=[row, pl.BlockSpec((tm, e), lambda i: (i, 2)), pl.BlockSpec((tm, d), lambda i: (i, 0)),
                  pl.BlockSpec((e, d), lambda i: (0, 0), pipeline_mode=pl.Buffered(1)), par, par]
                 + [pl.BlockSpec(memory_space=pl.ANY)] * len(extra),
        out_specs=(row, pl.BlockSpec((tm, e), lambda i: (i, 2)), par, par, par),
        scratch_shapes=[pltpu.VMEM((24, e), F32), pltpu.VMEM((tm, e), F32)],
        compiler_params=_cparams(("arbitrary",)),
    )(u1, proj, dr1b, w_out, ng, nb, *extra)


def _conv_bwd(du1, proj, cw, dproj):
    t = du1.shape[0]
    e = CONV_WIDTH
    tm = min(t, 128)
    hb = tm // HALO
    nt = t // tm
    nchunk = e // LANES
    last_halo = t // HALO - 1

    def body(d_ref, dn_ref, val_ref, gg_ref, valh_ref, ggh_ref, cw_ref, alias_ref,
             dp_ref, dcw_ref, dcb_ref, dbv_ref, extu_ref, extd_ref, du0_ref, acc_ref, sh_ref):
        i = pl.program_id(0)
        h0 = valh_ref[...] * _sigmoid(ggh_ref[...])
        extu_ref[0:HALO, :] = jnp.where(i > 0, h0, 0.0)
        extu_ref[HALO:, :] = val_ref[...] * _sigmoid(gg_ref[...])
        extd_ref[0:tm, :] = d_ref[...]
        extd_ref[tm:, :] = jnp.where(i < nt - 1, dn_ref[...], 0.0)

        @pl.when(i == 0)
        def _():
            acc_ref[...] = jnp.zeros_like(acc_ref)

        def chunk(c, carry):
            col = pl.multiple_of(c * LANES, LANES)
            d = extd_ref[0:tm, pl.ds(col, LANES)]
            du0 = jnp.zeros((tm, LANES), F32)
            for res, taps in _taps_by_residue(lambda j: CONV_KERNEL - 1 - j):
                span = 8 * max(a for _, a in taps) + tm
                sh_ref[0:span, :] = extd_ref[res:res + span, pl.ds(col, LANES)]
                for j, a in taps:
                    du0 = du0 + cw_ref[j:j + 1, pl.ds(col, LANES)] * sh_ref[8 * a:8 * a + tm, :]
            for res, taps in _taps_by_residue(lambda j: HALO - (CONV_KERNEL - 1) + j):
                span = 8 * max(a for _, a in taps) + tm
                sh_ref[0:span, :] = extu_ref[res:res + span, pl.ds(col, LANES)]
                for j, a in taps:
                    prod = d * sh_ref[8 * a:8 * a + tm, :]
                    acc_ref[j * 8:(j + 1) * 8, pl.ds(col, LANES)] += _fold_rows(prod)
            acc_ref[CONV_KERNEL * 8:(CONV_KERNEL + 1) * 8, pl.ds(col, LANES)] += _fold_rows(d)
            du0_ref[:, pl.ds(col, LANES)] = du0
            return carry

        lax.fori_loop(0, nchunk, chunk, 0)
        kb = CONV_KERNEL * 8
        for r in range(0, tm, 16):
            sg = _sigmoid(gg_ref[r:r + 16, :])
            dval = du0_ref[r:r + 16, :] * sg
            dgg = dval * val_ref[r:r + 16, :] * (1.0 - sg)
            dp_ref[r:r + 16, 0:e] = dval.astype(BF16)
            dp_ref[r:r + 16, e:] = dgg.astype(BF16)
            acc_ref[kb + 8:kb + 16, :] += _fold_rows(dval)
            acc_ref[kb + 16:kb + 24, :] += _fold_rows(dgg)

        @pl.when(i == nt - 1)
        def _():
            for j in range(CONV_KERNEL):
                dcw_ref[j:j + 1, :] = jnp.sum(acc_ref[j * 8:(j + 1) * 8, :], axis=0, keepdims=True)
            dcw_ref[CONV_KERNEL:, :] = jnp.zeros((HALO - CONV_KERNEL, e), F32)
            dcb_ref[...] = jnp.sum(acc_ref[kb:kb + 8, :], axis=0, keepdims=True)
            dbv_ref[:, 0:e] = jnp.sum(acc_ref[kb + 8:kb + 16, :], axis=0, keepdims=True)
            dbv_ref[:, e:] = jnp.sum(acc_ref[kb + 16:kb + 24, :], axis=0, keepdims=True)

    row = lambda c: pl.BlockSpec((tm, e), lambda i: (i, c))
    halo_prev = lambda c: pl.BlockSpec((HALO, e), lambda i: (jnp.maximum(i * hb - 1, 0), c))
    halo_next = pl.BlockSpec((HALO, e), lambda i: (jnp.minimum((i + 1) * hb, last_halo), 0))
    return pl.pallas_call(
        body, name="conv_bwd",
        out_shape=(jax.ShapeDtypeStruct(dproj.shape, BF16), jax.ShapeDtypeStruct((HALO, e), F32),
                   jax.ShapeDtypeStruct((1, e), F32), jax.ShapeDtypeStruct((1, 2 * e), F32)),
        grid=(nt,),
        in_specs=[row(0), halo_next, row(0), row(1), halo_prev(0), halo_prev(1),
                  pl.BlockSpec((HALO, e), lambda i: (0, 0)), pl.BlockSpec(memory_space=pl.ANY)],
        out_specs=(pl.BlockSpec((tm, 2 * e), lambda i: (i, 0)), pl.BlockSpec((HALO, e), lambda i: (0, 0)),
                   pl.BlockSpec((1, e), lambda i: (0, 0)), pl.BlockSpec((1, 2 * e), lambda i: (0, 0))),
        scratch_shapes=[pltpu.VMEM((tm + HALO, e), F32), pltpu.VMEM((tm + HALO, e), F32),
                        pltpu.VMEM((tm, e), F32), pltpu.VMEM(((CONV_KERNEL + 3) * 8, e), F32),
                        pltpu.VMEM((tm + HALO, LANES), F32)],
        input_output_aliases={7: 0},
        compiler_params=_cparams(("arbitrary",)),
    )(du1, du1, proj, proj, proj, proj, cw, dproj)


def _adam_update(g, w, m, v):
    c1 = 1.0 - ADAM_B1 ** ADAM_STEP
    c2 = 1.0 - ADAM_B2 ** ADAM_STEP
    mn = ADAM_B1 * m + (1.0 - ADAM_B1) * g
    vn = ADAM_B2 * v + (1.0 - ADAM_B2) * (g * g)
    delta = -ADAM_LR * ((mn / c1) / (jnp.sqrt(vn / c2) + ADAM_EPS) + ADAM_WD * w)
    return delta, mn, vn


def _adamw(name, gbuf, w, m, v):
    parts = list(gbuf) if isinstance(gbuf, (list, tuple)) else [gbuf]
    npart = len(parts)
    r, c = w.shape
    tr = min(r // npart, 256)
    per_part = r // npart // tr
    assert r == npart * per_part * tr and all(p.shape == (N_DEV, r // npart, c) for p in parts)

    def body(*refs):
        g_refs = refs[:npart]
        w_ref, m_ref, v_ref, go_ref, d_ref, mo_ref, vo_ref = refs[npart:]
        i = pl.program_id(0)

        def run(g_ref):
            g = g_ref[0].astype(F32)
            for k in range(1, N_DEV):
                g = g + g_ref[k].astype(F32)
            go_ref[...] = g
            d_ref[...], mo_ref[...], vo_ref[...] = _adam_update(g, w_ref[...], m_ref[...], v_ref[...])

        if npart == 1:
            run(g_refs[0])
        else:
            for q in range(npart):
                pl.when((i >= q * per_part) & (i < (q + 1) * per_part))(functools.partial(run, g_refs[q]))

    blk = pl.BlockSpec((tr, c), lambda i: (i, 0))
    part_spec = lambda q: pl.BlockSpec((N_DEV, tr, c), lambda i: (0, jnp.clip(i - q * per_part, 0, per_part - 1), 0))
    shp = jax.ShapeDtypeStruct((r, c), F32)
    return pl.pallas_call(
        body, name="adamw_" + name,
        out_shape=(shp, shp, shp, shp),
        grid=(r // tr,),
        in_specs=[part_spec(q) for q in range(npart)] + [blk, blk, blk],
        out_specs=(blk, blk, blk, blk),
        compiler_params=_cparams(("parallel",)),
    )(*parts, w, m, v)


SMALL_TABLE_COLS = 256
SMALL_LAYOUT = {
    'ln_g': (0, 2, 1024), 'ln_b': (8, 2, 1024), 'a_b_in': (16, 1, 768), 'a_conv_b': (24, 1, 256),
    'a_norm_g': (32, 1, 256), 'a_norm_b': (40, 1, 256), 'a_b_out': (48, 1, 128), 'kv_norm_g': (56, 1, 256),
    'b_q_norm_g': (64, 1, 512), 'a_conv_w': (72, CONV_KERNEL, 256),
}
LOSS_ROW = 104
SMALL_TABLE_ROWS = 112
SMALL_NAMES = tuple(SMALL_LAYOUT)


def _small_pieces(name, row):
    r0, _, c = SMALL_LAYOUT[name]
    w = min(c, SMALL_TABLE_COLS)
    per_row = c // w
    return [(r0 + row * per_row + q, q * w, w) for q in range(per_row)]


def _pack_small_grads(dg0, dg1, db0, db1, dbv, dbz, dcb, dng, dnb, dba, dgkv, dgq, dcw, loss):
    e = CONV_WIDTH
    ce = e // N_DEV

    def body(dg0_ref, dg1_ref, db0_ref, db1_ref, dbv_ref, dbz_ref, dcb_ref, dng_ref, dnb_ref, dba_ref, dgkv_ref,
             dgq_ref, dcw_ref, loss_ref, o_ref):
        o_ref[...] = jnp.zeros_like(o_ref)
        for d in range(N_DEV):
            o_ref[d, LOSS_ROW:LOSS_ROW + 1, 0:LANES] = loss_ref[...]

        def put(d, name, row, src_ref, first_col=0):
            for trow, col, w in _small_pieces(name, row):
                o_ref[d, trow:trow + 1, 0:w] = src_ref[:, first_col + col:first_col + col + w]

        for d in range(N_DEV):
            put(d, 'ln_g', 0, dg0_ref)
            put(d, 'ln_g', 1, dg1_ref)
            put(d, 'ln_b', 0, db0_ref)
            put(d, 'ln_b', 1, db1_ref)
            for trow, col, w in _small_pieces('a_b_in', 0):
                gcol = d * 3 * ce + col
                src = dbv_ref[:, gcol:gcol + w] if gcol < 2 * e else dbz_ref[:, gcol - 2 * e:gcol - 2 * e + w]
                o_ref[d, trow:trow + 1, 0:w] = src
            put(d, 'a_conv_b', 0, dcb_ref, d * ce)
            put(d, 'a_norm_g', 0, dng_ref, d * ce)
            put(d, 'a_norm_b', 0, dnb_ref, d * ce)
            put(d, 'a_b_out', 0, dba_ref, d * LANES)
            put(d, 'kv_norm_g', 0, dgkv_ref)
            put(d, 'b_q_norm_g', 0, dgq_ref)
            r0 = SMALL_LAYOUT['a_conv_w'][0]
            o_ref[d, r0:r0 + HALO, 0:ce] = dcw_ref[:, d * ce:(d + 1) * ce]

    return pl.pallas_call(
        body, name="pack_small_grads",
        out_shape=jax.ShapeDtypeStruct((N_DEV, SMALL_TABLE_ROWS, SMALL_TABLE_COLS), F32),
        compiler_params=pltpu.CompilerParams(vmem_limit_bytes=VMEM_LIMIT),
    )(dg0, dg1, db0, db1, dbv, dbz, dcb, dng, dnb, dba, dgkv, dgq, dcw, loss)


def _adamw_small(gtab, ws, ms, vs):
    n = len(SMALL_NAMES)

    def body(*refs):
        g_ref = refs[0]
        w_refs, m_refs, v_refs = refs[1:1 + n], refs[1 + n:1 + 2 * n], refs[1 + 2 * n:1 + 3 * n]
        outs = refs[1 + 3 * n:]

        def summed(r0, r, c):
            g = g_ref[0, r0:r0 + r, 0:c]
            for k in range(1, N_DEV):
                g = g + g_ref[k, r0:r0 + r, 0:c]
            return g

        for i, name in enumerate(SMALL_NAMES):
            r0, r, c = SMALL_LAYOUT[name]
            if c <= SMALL_TABLE_COLS:
                blocks = [(slice(None), summed(r0, r, c))]
            else:
                blocks = [(slice(row, row + 1), jnp.concatenate([summed(tr, 1, w) for tr, _, w in _small_pieces(name, row)], axis=1))
                          for row in range(r)]
            for rows, g in blocks:
                delta, mn, vn = _adam_update(g, w_refs[i][rows, :], m_refs[i][rows, :], v_refs[i][rows, :])
                outs[i][rows, :] = g
                outs[n + i][rows, :] = delta
                outs[2 * n + i][rows, :] = mn
                outs[3 * n + i][rows, :] = vn
        outs[4 * n][...] = summed(LOSS_ROW, 1, LANES)

    shapes = tuple(jax.ShapeDtypeStruct(w.shape, F32) for w in ws)
    res = pl.pallas_call(
        body, name="adamw_small",
        out_shape=shapes * 4 + (jax.ShapeDtypeStruct((1, LANES), F32),),
        compiler_params=pltpu.CompilerParams(vmem_limit_bytes=VMEM_LIMIT),
    )(gtab, *ws, *ms, *vs)
    return res[:n], res[n:2 * n], res[2 * n:3 * n], res[3 * n:4 * n], res[4 * n]


def _mesh_peers():
    x, y, c = lax.axis_index("x"), lax.axis_index("y"), lax.axis_index("c")
    me = 4 * x + 2 * y + c
    peers = []
    for k in range(1, N_DEV):
        px = 1 - x if (k >> 2) & 1 else x
        py = 1 - y if (k >> 1) & 1 else y
        pc = 1 - c if k & 1 else c
        peers.append(((px, py, pc), 4 * px + 2 * py + pc))
    return me, peers


_HBM_SPEC = pl.BlockSpec(memory_space=pltpu.HBM)
_SEM_SPEC = pl.BlockSpec(memory_space=pltpu.SEMAPHORE)


ALL_PEERS = tuple(range(N_DEV - 1))
CHIP_LEVEL_PEERS = (0, 1, 3, 5)


def _split_copies(srcs, lands, send_sems, recv_sems, local_sems, gather, which=ALL_PEERS):
    me, peers = _mesh_peers()
    na = len(srcs)
    mine = lambda a, slot: srcs[a] if gather else srcs[a].at[slot]
    local = [pltpu.make_async_copy(mine(a, me), lands[a].at[me], local_sems.at[a]) for a in range(na)]
    sent, received = [], []
    for k, (dev, pid) in enumerate(peers):
        if k not in which:
            continue
        for a in range(na):
            s = a * (N_DEV - 1) + k
            sent.append(pltpu.make_async_remote_copy(
                src_ref=mine(a, pid), dst_ref=lands[a].at[me], send_sem=send_sems.at[s],
                recv_sem=recv_sems.at[s], device_id=dev, device_id_type=pl.DeviceIdType.MESH))
            received.append(pltpu.make_async_remote_copy(
                src_ref=mine(a, pid), dst_ref=lands[a].at[pid], send_sem=send_sems.at[s],
                recv_sem=recv_sems.at[s], device_id=dev, device_id_type=pl.DeviceIdType.MESH))
    return local, sent, received


def _comm_start(name, srcs, gather, after, which=ALL_PEERS):
    na = len(srcs)
    land_structs = [jax.ShapeDtypeStruct(((N_DEV,) + s.shape) if gather else s.shape, s.dtype) for s in srcs]
    extra = [] if after is None else [after]
    n_in = 2 * na + len(extra)

    def body(*refs):
        srcs_r, lands_r = refs[:na], refs[na:2 * na]
        send_sems, recv_sems, local_sems = refs[n_in:n_in + 3]
        token = refs[-1]
        local, sent, _ = _split_copies(srcs_r, lands_r, send_sems, recv_sems, local_sems, gather, which)
        for cp in local + sent:
            cp.start()
        token[...] = jnp.zeros_like(token)

    sem = pltpu.SemaphoreType.DMA((na * (N_DEV - 1),))
    outs = pl.pallas_call(
        body, name=name,
        out_shape=(sem, sem, pltpu.SemaphoreType.DMA((na,)),
                   *[pltpu.HBM(s.shape, s.dtype) for s in srcs],
                   *[pltpu.HBM(s.shape, s.dtype) for s in land_structs],
                   jax.ShapeDtypeStruct((8, LANES), F32)),
        in_specs=[_HBM_SPEC] * (2 * na) + [pl.BlockSpec(memory_space=pl.ANY)] * len(extra),
        out_specs=(_SEM_SPEC, _SEM_SPEC, _SEM_SPEC, *([_HBM_SPEC] * (2 * na)), pl.BlockSpec(memory_space=pltpu.VMEM)),
        input_output_aliases={i: 3 + i for i in range(2 * na)},
        compiler_params=pltpu.CompilerParams(has_side_effects=pltpu.SideEffectType.DATAFLOW_SIDE_EFFECTING),
    )(*[pltpu.with_memory_space_constraint(s, pltpu.HBM) for s in srcs],
      *[pltpu.with_memory_space_constraint(lax.empty(s.shape, s.dtype), pltpu.HBM) for s in land_structs],
      *extra)
    return (outs[:3], outs[3:3 + na], outs[3 + na:3 + 2 * na]), outs[-1]


def _comm_wait(name, handles, gather, after, which=ALL_PEERS):
    (send_sems, recv_sems, local_sems), srcs, lands = handles
    na = len(srcs)

    def body(*refs):
        srcs_r, lands_r = refs[:na], refs[na:2 * na]
        send_r, recv_r, local_r = refs[2 * na:2 * na + 3]
        local, sent, received = _split_copies(srcs_r, lands_r, send_r, recv_r, local_r, gather, which)
        for cp in sent:
            cp.wait_send()
        for cp in received:
            cp.wait_recv()
        for cp in local:
            cp.wait()

    outs = pl.pallas_call(
        body, name=name,
        out_shape=(*[pltpu.HBM(s.shape, s.dtype) for s in srcs], *[pltpu.HBM(s.shape, s.dtype) for s in lands]),
        in_specs=[_HBM_SPEC] * (2 * na) + [_SEM_SPEC] * 3 + [pl.BlockSpec(memory_space=pl.ANY)],
        out_specs=tuple([_HBM_SPEC] * (2 * na)),
        input_output_aliases={i: i for i in range(2 * na)},
        compiler_params=pltpu.CompilerParams(has_side_effects=pltpu.SideEffectType.DATAFLOW_SIDE_EFFECTING),
    )(*srcs, *lands, send_sems, recv_sems, local_sems, after)
    return outs[na:]


def _forward_to_sibling(name, lands):
    na = len(lands)

    def body(*refs):
        ins, outs = refs[:na], refs[na:2 * na]
        send_sems, recv_sems = refs[2 * na:]
        x, y, c = lax.axis_index("x"), lax.axis_index("y"), lax.axis_index("c")
        sibling = (x, y, 1 - c)
        chips = [(1 - x, y), (x, 1 - y), (1 - x, 1 - y)]
        slot = lambda cx, cy, cc: 4 * cx + 2 * cy + cc
        sends = []
        for j, (cx, cy) in enumerate(chips):
            for a in range(na):
                cp = pltpu.make_async_remote_copy(
                    src_ref=ins[a].at[slot(cx, cy, c)], dst_ref=outs[a].at[slot(cx, cy, c)],
                    send_sem=send_sems.at[a, j], recv_sem=recv_sems.at[a, j], device_id=sibling,
                    device_id_type=pl.DeviceIdType.MESH)
                cp.start()
                sends.append(cp)
        for j, (cx, cy) in enumerate(chips):
            for a in range(na):
                pltpu.make_async_remote_copy(
                    src_ref=ins[a].at[slot(cx, cy, 1 - c)], dst_ref=outs[a].at[slot(cx, cy, 1 - c)],
                    send_sem=send_sems.at[a, j], recv_sem=recv_sems.at[a, j], device_id=sibling,
                    device_id_type=pl.DeviceIdType.MESH).wait_recv()
        for cp in sends:
            cp.wait_send()

    hbm = pl.BlockSpec(memory_space=pl.ANY)
    return pl.pallas_call(
        body, name=name,
        out_shape=tuple(jax.ShapeDtypeStruct(s.shape, s.dtype) for s in lands),
        in_specs=[hbm] * na, out_specs=(hbm,) * na,
        input_output_aliases={a: a for a in range(na)},
        scratch_shapes=[pltpu.SemaphoreType.DMA((na, 3)), pltpu.SemaphoreType.DMA((na, 3))],
    )(*lands)


def _prep_weights(w):
    e = CONV_WIDTH
    p = {}
    if 'a_w_in' in w:
        if w['a_w_in'].shape == (N_DEV, D_MODEL, 3 * e // N_DEV):
            p['a_w_in3'] = w['a_w_in']
        else:
            p['a_w_in3'] = w['a_w_in'].reshape(D_MODEL, N_DEV, 3 * e // N_DEV).transpose(1, 0, 2)
        p['a_b_in'] = w['a_b_in'].reshape(1, 3 * e)
        p['a_conv_w'] = jnp.pad(w['a_conv_w'].reshape(CONV_KERNEL, e), ((0, HALO - CONV_KERNEL), (0, 0)))
        p['a_conv_b'] = w['a_conv_b'].reshape(1, e)
        p['a_norm_g'] = w['a_norm_g'].reshape(1, e)
        p['a_norm_b'] = w['a_norm_b'].reshape(1, e)
        p['a_b_out'] = w['a_b_out'].reshape(1, D_MODEL)
        p['kv_norm_g'] = w['kv_norm_g'].reshape(1, KV_LORA_RANK)
        p['b_q_norm_g'] = w['b_q_norm_g'].reshape(1, Q_LORA_RANK)
        p['ln_g'] = w['ln_g']
        p['ln_b'] = w['ln_b']
    if 'a_w_out' in w:
        p['a_w_out'] = w['a_w_out'].reshape(e, D_MODEL)
        p['kv_w_down'] = jnp.pad(w['kv_w_down'], ((0, 0), (0, KV_LORA_RANK + LANES - w['kv_w_down'].shape[1])))
        p['kv_w_ukv'] = jnp.concatenate([w['kv_w_uk'], w['kv_w_uv']], axis=2).reshape(KV_LORA_RANK, N_HEADS * HEAD_PAD)
        if w['b_w_in'].ndim == 3 and w['b_w_in'].shape[:2] == (N_DEV, D_MODEL):
            p['b_w_in3'] = w['b_w_in']
        else:
            p['b_w_in3'] = w['b_w_in'].reshape(D_MODEL, N_DEV, -1).transpose(1, 0, 2)
        uq = w['b_w_uq'].reshape(Q_LORA_RANK, N_HEADS, QK_NOPE_DIM + QK_ROPE_DIM)
        p['b_w_uq'] = jnp.pad(uq, ((0, 0), (0, 0), (0, HEAD_PAD - uq.shape[2]))).reshape(Q_LORA_RANK, N_HEADS * HEAD_PAD)
        p['b_w_out'] = w['b_w_out'].reshape(N_HEADS * V_HEAD_DIM, D_MODEL)
    return p


def _local_step(x, positions, target, comm):
    t = x.shape[0]
    e = CONV_WIDTH
    freqs = ROPE_THETA ** (-jnp.arange(0, QK_ROPE_DIM, 2, dtype=F32) / QK_ROPE_DIM)
    freq_row = jnp.concatenate([freqs, freqs, jnp.zeros((LANES - QK_ROPE_DIM,), F32)]).reshape(1, LANES)
    rc, rs, xb = _rope_tables(positions.reshape(t, 1), freq_row, x, comm.start_token())
    p = comm.first_weights(rc)
    ln_g0, ln_b0 = p['ln_g'][0:1], p['ln_b'][0:1]
    ln_g1, ln_b1 = p['ln_g'][1:2], p['ln_b'][1:2]

    proj = _matmul(xb, p['a_w_in3'], bias=p['a_b_in'], name="a_in", b_blocked=True, bm=2048,
                   after=comm.first_after())
    u1, u2 = _conv_mid_fwd(proj, p['a_conv_w'], p['a_conv_b'], p['a_norm_g'], p['a_norm_b'])
    p = {**p, **comm.rest_weights(u2)}
    h1, h1b, xh1, rstd1 = _ln_res_fwd(x, u2, p['a_w_out'], p['a_b_out'], ln_g0, ln_b0)
    ckv, ckn, kr, kv = _kv_mid_fwd(h1b, p['kv_w_down'], p['kv_w_ukv'], p['kv_norm_g'], rc, rs)
    projb, cqn, b_w_in = _b_in_qnorm(h1b, p['b_w_in3'], p['b_q_norm_g'])
    qcat = _q_up_rope(cqn, p['b_w_uq'], rc, rs)
    o, o2, lse = _flash_fwd(qcat, kv, kr, projb)
    loss, dr2, dr2b, dg1, db1 = _final_ln_loss(h1, o2, p['b_w_out'], ln_g1, ln_b1, target)

    g = {}
    g['b_w_out'] = _matmul(o2, dr2b, trans_a=True, name="d_b_out", out_dtype=BF16, bm=512, bk=t)
    dob, dprojb = _gate_bwd(dr2b, p['b_w_out'], o, projb)
    dq2, dkv, dkr = _flash_bwd(qcat, kv, kr, dob, lse, o, rc, rs)
    duq = _matmul(cqn, dq2, trans_a=True, name="d_q_up", out_dtype=BF16, bk=t)
    dprojb, dgq = _q_norm_bwd(dq2, p['b_w_uq'], projb, p['b_q_norm_g'], dprojb)
    g['b_w_in'] = _matmul(h1b, dprojb, trans_a=True, name="d_b_in", bn=2 * dprojb.shape[1] // N_DEV, bk=t, out_dtype=BF16,
                          out_blocked=True, out_parts=2)
    dukv = _matmul(ckn, dkv, trans_a=True, name="d_kv_up", out_dtype=BF16, bk=t)
    dckv, dgkv = _kv_mid_bwd(dkv, p['kv_w_ukv'], ckv, dkr, p['kv_norm_g'], rc, rs)
    ddown = _matmul(h1b, dckv, trans_a=True, name="d_kv_down", out_dtype=BF16, bm=512, bk=t)
    g['b_w_out'] = g['b_w_out'].reshape(N_DEV, -1, D_MODEL)
    g['kv_w_down'] = ddown[:, :KV_LORA_RANK + QK_ROPE_DIM].reshape(N_DEV, -1, KV_LORA_RANK + QK_ROPE_DIM)
    dukv = dukv.reshape(KV_LORA_RANK, N_HEADS, HEAD_PAD)
    g['kv_w_uk'] = dukv[:, :, :QK_NOPE_DIM].reshape(N_DEV, -1, QK_NOPE_DIM)
    g['kv_w_uv'] = dukv[:, :, QK_NOPE_DIM:].reshape(N_DEV, -1, V_HEAD_DIM)
    g['b_w_uq'] = duq.reshape(Q_LORA_RANK, N_HEADS, HEAD_PAD)[:, :, :QK_NOPE_DIM + QK_ROPE_DIM].reshape(
        N_DEV, -1, QK_NOPE_DIM + QK_ROPE_DIM)
    sent1 = comm.send_group1(g)
    dr1, dr1b, dg0, db0, dba_out = _ln0_bwd(dr2, dprojb, b_w_in, dckv, p['kv_w_down'], xh1, rstd1, ln_g0, sent1)
    dw_out = _matmul(u2, dr1b, trans_a=True, name="d_a_out", out_dtype=BF16, bm=512, bk=t).reshape(N_DEV, -1, D_MODEL)
    sent1b = comm.send_group1b({'a_w_out': dw_out})
    du1, dproj, dng, dnb, dbz = _conv_mid_bwd_rows(u1, proj, dr1b, p['a_w_out'], p['a_norm_g'], p['a_norm_b'], sent1b)
    dproj, dcw, dcb, dbv = _conv_bwd(du1, proj, p['a_conv_w'], dproj)
    half = D_MODEL // 2
    dw_lo = _matmul(xb, dproj, trans_a=True, name="d_a_in_lo", out_dtype=BF16, bm=half, bn=3 * e // N_DEV, bk=t,
                    out_blocked=True, a_cols=(0, half))
    small = (dg0, dg1, db0, db1, dbv, dbz, dcb, dng, dnb, dba_out, dgkv, dgq, dcw, loss)
    sent2 = comm.send_group2(dw_lo, small)
    dw_hi = _matmul(xb, dproj, trans_a=True, name="d_a_in_hi", out_dtype=BF16, bm=half, bn=3 * e // N_DEV, bk=t,
                    out_blocked=True, after=sent2, a_cols=(half, half))
    sent3 = comm.send_group3(dw_hi)
    grad_x = _grad_x(dproj, p['a_w_in3'], dr1, sent3)
    return loss, grad_x


def _grad_x(dproj, w3, dr1, after):
    t, d = dr1.shape
    nblk, _, cb = w3.shape
    tm = min(t, 512)
    extra = [] if after is None else [after]

    def body(a_ref, w_ref, dr_ref, *rest):
        o_ref = rest[-1]
        acc = ALPHA * dr_ref[...]
        for k in range(nblk):
            acc = acc + lax.dot_general(a_ref[:, k * cb:(k + 1) * cb], w_ref[k], NT, preferred_element_type=F32)
        o_ref[...] = acc

    row = pl.BlockSpec((tm, d), lambda i: (i, 0))
    return pl.pallas_call(
        body, name="grad_x",
        out_shape=jax.ShapeDtypeStruct((t, d), F32),
        grid=(t // tm,),
        in_specs=[pl.BlockSpec((tm, nblk * cb), lambda i: (i, 0)),
                  pl.BlockSpec((nblk, d, cb), lambda i: (0, 0, 0), pipeline_mode=pl.Buffered(1)), row]
                 + [pl.BlockSpec(memory_space=pl.ANY)] * len(extra),
        out_specs=row,
        compiler_params=_cparams(("parallel",)),
    )(dproj, w3, dr1, *extra)


def _shard_2d(a):
    return a.reshape(-1, a.shape[-1])


def _gathered_to_full(name, blocks):
    if name in ('a_w_in', 'b_w_in'):
        return blocks
    if name == 'a_conv_w':
        return blocks.transpose(1, 0, 2).reshape(CONV_KERNEL, -1)
    if name in ('a_b_in', 'a_conv_b', 'a_norm_g', 'a_norm_b', 'a_b_out'):
        return blocks.reshape(-1)
    if name in ('kv_w_uk', 'kv_w_uv'):
        return blocks.reshape(KV_LORA_RANK, N_HEADS, -1)
    if name == 'b_w_uq':
        return blocks.reshape(Q_LORA_RANK, N_HEADS, -1)
    return blocks.reshape(-1, blocks.shape[-1])


def kernel(x, positions, ln_g, ln_b, a_w_in, a_b_in, a_conv_w, a_conv_b, a_norm_g, a_norm_b, a_w_out, a_b_out, kv_w_down, kv_norm_g, kv_w_uk, kv_w_uv, b_w_in, b_q_norm_g, b_w_uq, b_w_out, loss_target, m_ln_g, m_ln_b, m_a_w_in, m_a_b_in, m_a_conv_w, m_a_conv_b, m_a_norm_g, m_a_norm_b, m_a_w_out, m_a_b_out, m_kv_w_down, m_kv_norm_g, m_kv_w_uk, m_kv_w_uv, m_b_w_in, m_b_q_norm_g, m_b_w_uq, m_b_w_out, v_ln_g, v_ln_b, v_a_w_in, v_a_b_in, v_a_conv_w, v_a_conv_b, v_a_norm_g, v_a_norm_b, v_a_w_out, v_a_b_out, v_kv_w_down, v_kv_norm_g, v_kv_w_uk, v_kv_w_uv, v_b_w_in, v_b_q_norm_g, v_b_w_uq, v_b_w_out):
    w = dict(ln_g=ln_g, ln_b=ln_b, a_w_in=a_w_in, a_b_in=a_b_in, a_conv_w=a_conv_w, a_conv_b=a_conv_b,
             a_norm_g=a_norm_g, a_norm_b=a_norm_b, a_w_out=a_w_out, a_b_out=a_b_out, kv_w_down=kv_w_down,
             kv_norm_g=kv_norm_g, kv_w_uk=kv_w_uk, kv_w_uv=kv_w_uv, b_w_in=b_w_in, b_q_norm_g=b_q_norm_g,
             b_w_uq=b_w_uq, b_w_out=b_w_out)
    m = dict(ln_g=m_ln_g, ln_b=m_ln_b, a_w_in=m_a_w_in, a_b_in=m_a_b_in, a_conv_w=m_a_conv_w, a_conv_b=m_a_conv_b,
             a_norm_g=m_a_norm_g, a_norm_b=m_a_norm_b, a_w_out=m_a_w_out, a_b_out=m_a_b_out, kv_w_down=m_kv_w_down,
             kv_norm_g=m_kv_norm_g, kv_w_uk=m_kv_w_uk, kv_w_uv=m_kv_w_uv, b_w_in=m_b_w_in, b_q_norm_g=m_b_q_norm_g,
             b_w_uq=m_b_w_uq, b_w_out=m_b_w_out)
    v = dict(ln_g=v_ln_g, ln_b=v_ln_b, a_w_in=v_a_w_in, a_b_in=v_a_b_in, a_conv_w=v_a_conv_w, a_conv_b=v_a_conv_b,
             a_norm_g=v_a_norm_g, a_norm_b=v_a_norm_b, a_w_out=v_a_w_out, a_b_out=v_a_b_out, kv_w_down=v_kv_w_down,
             kv_norm_g=v_kv_norm_g, kv_w_uk=v_kv_w_uk, kv_w_uv=v_kv_w_uv, b_w_in=v_b_w_in, b_q_norm_g=v_b_q_norm_g,
             b_w_uq=v_b_w_uq, b_w_out=v_b_w_out)

    small_flat = jnp.concatenate([w[n].reshape(-1) for n in SMALL_WEIGHTS]).reshape(1, -1)
    rest_names = [n for n in MATMUL_WEIGHTS if n != 'a_w_in']
    group1 = ('b_w_out', 'b_w_uq', 'b_w_in', 'kv_w_uk', 'kv_w_uv', 'kv_w_down')
    group1b = ('a_w_out',)

    class Comm:
        def __init__(self):
            self.first, self.first_token = _comm_start(
                "gather_first_start", [_shard_2d(w['a_w_in']).astype(BF16), small_flat], True, None, CHIP_LEVEL_PEERS)

        def start_token(self):
            return self.first_token

        def first_weights(self, after):
            lands = _comm_wait("gather_first_wait", self.first, True, after, CHIP_LEVEL_PEERS)
            ga = _forward_to_sibling("gather_first_forward", list(lands))
            full = {'a_w_in': ga[0]}
            gs = ga[1].reshape(N_DEV, -1)
            off = 0
            for n in SMALL_WEIGHTS:
                size = math.prod(w[n].shape)
                full[n] = _gathered_to_full(n, gs[:, off:off + size].reshape((N_DEV,) + _shard_2d(w[n]).shape))
                off += size
            for n in REPLICATED:
                full[n] = w[n]
            dense = lambda a: a if a.shape[-1] % LANES == 0 else a.reshape(-1, LANES)
            self.rest, self.rest_token = _comm_start(
                "gather_rest_start", [dense(_shard_2d(w[n]).astype(BF16)) for n in rest_names], True, ga[0])
            return _prep_weights(full)

        def first_after(self):
            return self.rest_token

        def rest_weights(self, after):
            lands = _comm_wait("gather_rest_wait", self.rest, True, after)
            blocks = {n: lands[i].reshape((N_DEV,) + _shard_2d(w[n]).shape) for i, n in enumerate(rest_names)}
            return _prep_weights({n: _gathered_to_full(n, blocks[n]) for n in rest_names})

        def send_group1(self, g):
            self.g1, token = _comm_start("exchange_g1_start", [g[n] for n in group1], False, None)
            return token

        def send_group1b(self, g):
            self.g1b, token = _comm_start("exchange_g1b_start", [g[n] for n in group1b], False, None)
            return token

        def send_group2(self, dw_lo, small):
            self.g2, token = _comm_start("exchange_g2_start", [dw_lo, _pack_small_grads(*small)], False, None)
            return token

        def send_group3(self, dw_hi):
            self.g3, token = _comm_start("exchange_g3_start", [dw_hi], False, None)
            return token

    comm = Comm()

    _, grad_x = _local_step(x[0], positions[0], loss_target[0], comm)

    res = {}
    recv1 = _comm_wait("exchange_g1_wait", comm.g1, False, grad_x)
    for i, n in enumerate(group1):
        res[n] = _adamw(n, recv1[i], _shard_2d(w[n]), _shard_2d(m[n]), _shard_2d(v[n]))
    recv1b = _comm_wait("exchange_g1b_wait", comm.g1b, False, res[group1[-1]][0])
    for i, n in enumerate(group1b):
        res[n] = _adamw(n, recv1b[i], _shard_2d(w[n]), _shard_2d(m[n]), _shard_2d(v[n]))
    recv2 = _comm_wait("exchange_g2_wait", comm.g2, False, res[group1b[-1]][0])
    recv3 = _comm_wait("exchange_g3_wait", comm.g3, False, recv2[0])
    res['a_w_in'] = _adamw('a_w_in', [recv2[0], recv3[0]], _shard_2d(w['a_w_in']), _shard_2d(m['a_w_in']),
                           _shard_2d(v['a_w_in']))
    two_d = lambda d: [_shard_2d(d[n]) if d[n].ndim > 1 else d[n].reshape(1, -1) for n in SMALL_NAMES]
    sg, sd, sm, sv, loss = _adamw_small(recv2[-1], two_d(w), two_d(m), two_d(v))
    for i, n in enumerate(SMALL_NAMES):
        res[n] = (sg[i], sd[i], sm[i], sv[i])
    outs = [[res[n][j].reshape(w[n].shape) for n in WEIGHT_NAMES] for j in range(4)]
    return (loss[0, 0], grad_x[None], *outs[0], *outs[1], *outs[2], *outs[3])
```

```python
import functools
import math

import jax
import jax.numpy as jnp
from jax import lax
from jax.experimental import pallas as pl
from jax.experimental.pallas import tpu as pltpu

F32 = jnp.float32
BF16 = jnp.bfloat16

D_MODEL = 1024
DEPTH = 2
CONV_WIDTH = 2 * D_MODEL
CONV_KERNEL = 31
N_HEADS = 16
QK_NOPE_DIM = 128
QK_ROPE_DIM = 64
V_HEAD_DIM = 128
KV_LORA_RANK = D_MODEL // 4
Q_LORA_RANK = D_MODEL // 2
ROPE_THETA = 10000.0
LN_EPS = 1e-5
RMS_EPS = 1e-6
MASK_VALUE = -1e30
ALPHA = (2.0 * DEPTH) ** 0.25
ATTN_SCALE = 1.0 / math.sqrt(QK_NOPE_DIM + QK_ROPE_DIM)

ADAM_LR = 0.001
ADAM_B1 = 0.9
ADAM_B2 = 0.999
ADAM_EPS = 1e-08
ADAM_WD = 0.01
ADAM_STEP = 10

N_DEV = 8
LANES = 128
HEAD_PAD = 256
HALO = 32
VMEM_LIMIT = 56 * 1024 * 1024

WEIGHT_NAMES = ['ln_g', 'ln_b', 'a_w_in', 'a_b_in', 'a_conv_w', 'a_conv_b', 'a_norm_g', 'a_norm_b',
                'a_w_out', 'a_b_out', 'kv_w_down', 'kv_norm_g', 'kv_w_uk', 'kv_w_uv', 'b_w_in',
                'b_q_norm_g', 'b_w_uq', 'b_w_out']
REPLICATED = ('ln_g', 'ln_b', 'kv_norm_g', 'b_q_norm_g')
MATMUL_WEIGHTS = ('a_w_in', 'a_w_out', 'kv_w_down', 'kv_w_uk', 'kv_w_uv', 'b_w_in', 'b_w_uq', 'b_w_out')
SMALL_WEIGHTS = ('a_b_in', 'a_conv_w', 'a_conv_b', 'a_norm_g', 'a_norm_b', 'a_b_out')

NN = (((1,), (0,)), ((), ()))
NT = (((1,), (1,)), ((), ()))
TN = (((0,), (0,)), ((), ()))


def _cparams(sem):
    return pltpu.CompilerParams(dimension_semantics=sem, vmem_limit_bytes=VMEM_LIMIT)


def _sigmoid(x):
    return 0.5 * jnp.tanh(0.5 * x) + 0.5


def _fold_rows(x):
    parts = [x[r:r + 8] for r in range(0, x.shape[0], 8)]
    while len(parts) > 1:
        parts = [parts[i] + parts[i + 1] for i in range(0, len(parts) - 1, 2)] + ([parts[-1]] if len(parts) % 2 else [])
    return parts[0]


def _taps_by_residue(offset_of):
    groups = {}
    for j in range(CONV_KERNEL):
        off = offset_of(j)
        groups.setdefault(off % 8, []).append((j, off // 8))
    return sorted(groups.items())


def _swap_rope_halves(t):
    lane = lax.broadcasted_iota(jnp.int32, t.shape, 1)
    return jnp.where(lane < QK_ROPE_DIM // 2, pltpu.roll(t, LANES - QK_ROPE_DIM // 2, 1),
                     pltpu.roll(t, QK_ROPE_DIM // 2, 1))


def _matmul(a, b, *, name, bias=None, trans_a=False, trans_b=False, out_dtype=F32, bm=1024, bn=1024, bk=1024,
            b_blocked=False, out_blocked=False, out_parts=1, after=None, a_cols=None):
    m_off = 0
    if trans_a:
        kdim, m = a.shape
        if a_cols is not None:
            m_off, m = a_cols
    else:
        m, kdim = a.shape
    if b_blocked and trans_b:
        n, k2 = b.shape[1], b.shape[0] * b.shape[2]
        bk = b.shape[2]
    elif b_blocked:
        k2, n = b.shape[1], b.shape[0] * b.shape[2]
        bn = b.shape[2]
    elif trans_b:
        n, k2 = b.shape
    else:
        k2, n = b.shape
    assert kdim == k2, (a.shape, b.shape)
    bm, bn, bk = min(bm, m), min(bn, n), min(bk, kdim)
    assert m % bm == 0 and n % bn == 0 and kdim % bk == 0, (name, m, n, kdim, bm, bn, bk)
    nk = kdim // bk
    assert out_parts == 1 or (out_blocked and bn % out_parts == 0)
    pw = bn // out_parts
    dims = (((0 if trans_a else 1,), (1 if trans_b else 0,)), ((), ()))
    has_bias = bias is not None

    def body(*refs):
        refs = list(refs)
        a_ref, b_ref = refs[:2]
        del refs[:2]
        bias_ref = refs.pop(0) if has_bias else None
        if after is not None:
            refs.pop(0)
        o_ref = refs.pop(0)
        rest = refs
        prod = lax.dot_general(a_ref[...], b_ref[...], dims, preferred_element_type=F32)

        def finish(acc):
            if has_bias:
                acc = acc + bias_ref[...]
            if out_parts == 1:
                o_ref[...] = acc.astype(out_dtype)
            else:
                for q in range(out_parts):
                    o_ref[q] = acc[:, q * pw:(q + 1) * pw].astype(out_dtype)

        if nk == 1:
            finish(prod)
        else:
            acc_ref = rest[0]
            k = pl.program_id(2)

            @pl.when(k == 0)
            def _():
                acc_ref[...] = prod

            @pl.when(k > 0)
            def _():
                acc_ref[...] += prod

            @pl.when(k == nk - 1)
            def _():
                finish(acc_ref[...])

    assert m_off % bm == 0
    a_first = m_off // bm
    a_spec = (pl.BlockSpec((bk, bm), lambda i, j, k: (k, i + a_first)) if trans_a
              else pl.BlockSpec((bm, bk), lambda i, j, k: (i, k)))
    if b_blocked and trans_b:
        b_spec = pl.BlockSpec((None, bn, bk), lambda i, j, k: (k, j, 0))
    elif b_blocked:
        b_spec = pl.BlockSpec((None, bk, bn), lambda i, j, k: (j, k, 0))
    elif trans_b:
        b_spec = pl.BlockSpec((bn, bk), lambda i, j, k: (j, k))
    else:
        b_spec = pl.BlockSpec((bk, bn), lambda i, j, k: (k, j))
    in_specs = [a_spec, b_spec]
    args = [a, b]
    if has_bias:
        in_specs.append(pl.BlockSpec((1, bn), lambda i, j, k: (0, j)))
        args.append(bias)
    if after is not None:
        in_specs.append(pl.BlockSpec(memory_space=pl.ANY))
        args.append(after)
    if out_blocked and out_parts > 1:
        out_struct = jax.ShapeDtypeStruct((n // pw, m, pw), out_dtype)
        out_spec = pl.BlockSpec((out_parts, bm, pw), lambda i, j, k: (j, i, 0))
    elif out_blocked:
        out_struct = jax.ShapeDtypeStruct((n // bn, m, bn), out_dtype)
        out_spec = pl.BlockSpec((None, bm, bn), lambda i, j, k: (j, i, 0))
    else:
        out_struct = jax.ShapeDtypeStruct((m, n), out_dtype)
        out_spec = pl.BlockSpec((bm, bn), lambda i, j, k: (i, j))
    return pl.pallas_call(
        body, name=name,
        out_shape=out_struct,
        grid=(m // bm, n // bn, nk),
        in_specs=in_specs,
        out_specs=out_spec,
        scratch_shapes=[pltpu.VMEM((bm, bn), F32)] if nk > 1 else [],
        compiler_params=_cparams(("parallel", "parallel", "arbitrary")),
    )(*args)


def _rope_tables(pos_col, freq_row, x, after):
    t, d = x.shape
    tm = min(t, 512)
    extra = [] if after is None else [after]

    def body(pos_ref, f_ref, x_ref, *rest):
        c_ref, s_ref, xb_ref = rest[-3:]
        ang = pos_ref[...].astype(F32) * f_ref[...]
        lane = lax.broadcasted_iota(jnp.int32, ang.shape, 1)
        cos = jnp.cos(ang)
        sin = jnp.sin(ang)
        c_ref[...] = jnp.where(lane < QK_ROPE_DIM, cos, 0.0)
        s_ref[...] = jnp.where(lane < QK_ROPE_DIM // 2, -sin, jnp.where(lane < QK_ROPE_DIM, sin, 0.0))
        xb_ref[...] = x_ref[...].astype(BF16)

    lane_blk = pl.BlockSpec((tm, LANES), lambda i: (i, 0))
    row = pl.BlockSpec((tm, d), lambda i: (i, 0))
    return pl.pallas_call(
        body, name="rope_tables",
        out_shape=(jax.ShapeDtypeStruct((t, LANES), F32), jax.ShapeDtypeStruct((t, LANES), F32),
                   jax.ShapeDtypeStruct((t, d), BF16)),
        grid=(t // tm,),
        in_specs=[pl.BlockSpec((tm, 1), lambda i: (i, 0)), pl.BlockSpec((1, LANES), lambda i: (0, 0)), row]
                 + [pl.BlockSpec(memory_space=pl.ANY)] * len(extra),
        out_specs=(lane_blk, lane_blk, row),
        compiler_params=_cparams(("parallel",)),
    )(pos_col, freq_row, x, *extra)


def _conv_mid_fwd(proj, cw, cb, ng, nb):
    t = proj.shape[0]
    e = CONV_WIDTH
    tm = min(t, 128)
    hb = tm // HALO
    nchunk = e // LANES

    def body(val_ref, gg_ref, z_ref, valh_ref, ggh_ref, cw_ref, cb_ref, ng_ref, nb_ref, u1_ref, u2_ref, ext_ref, sh_ref):
        i = pl.program_id(0)
        u0 = val_ref[...] * _sigmoid(gg_ref[...])
        h0 = valh_ref[...] * _sigmoid(ggh_ref[...])
        ext_ref[0:HALO, :] = jnp.where(i > 0, h0, 0.0)
        ext_ref[HALO:, :] = u0

        def chunk(c, carry):
            col = pl.multiple_of(c * LANES, LANES)
            acc = jnp.zeros((tm, LANES), F32)
            for res, taps in _taps_by_residue(lambda j: HALO - (CONV_KERNEL - 1) + j):
                span = 8 * max(a for _, a in taps) + tm
                sh_ref[0:span, :] = ext_ref[res:res + span, pl.ds(col, LANES)]
                for j, a in taps:
                    acc = acc + cw_ref[j:j + 1, pl.ds(col, LANES)] * sh_ref[8 * a:8 * a + tm, :]
            u1_ref[:, pl.ds(col, LANES)] = acc + cb_ref[:, pl.ds(col, LANES)]
            return carry

        lax.fori_loop(0, nchunk, chunk, 0)
        g = ng_ref[...]
        b = nb_ref[...]
        for r in range(0, tm, 16):
            u1 = u1_ref[r:r + 16, :]
            mu = jnp.mean(u1, axis=-1, keepdims=True)
            xc = u1 - mu
            var = jnp.mean(xc * xc, axis=-1, keepdims=True)
            n = xc * lax.rsqrt(var + LN_EPS) * g + b
            z = z_ref[r:r + 16, :]
            u2_ref[r:r + 16, :] = ((n * _sigmoid(n)) * (z * _sigmoid(z))).astype(BF16)

    row = lambda c: pl.BlockSpec((tm, e), lambda i: (i, c))
    halo = lambda c: pl.BlockSpec((HALO, e), lambda i: (jnp.maximum(i * hb - 1, 0), c))
    par = lambda r: pl.BlockSpec((r, e), lambda i: (0, 0))
    return pl.pallas_call(
        body, name="conv_mid_fwd",
        out_shape=(jax.ShapeDtypeStruct((t, e), F32), jax.ShapeDtypeStruct((t, e), BF16)),
        grid=(t // tm,),
        in_specs=[row(0), row(1), row(2), halo(0), halo(1), par(HALO), par(1), par(1), par(1)],
        out_specs=(pl.BlockSpec((tm, e), lambda i: (i, 0)), pl.BlockSpec((tm, e), lambda i: (i, 0))),
        scratch_shapes=[pltpu.VMEM((tm + HALO, e), F32), pltpu.VMEM((tm + HALO, LANES), F32)],
        compiler_params=_cparams(("parallel",)),
    )(proj, proj, proj, proj, proj, cw, cb, ng, nb)


def _ln_res_fwd(x, u, w_out, b_out, g, b):
    t, d = x.shape
    ku = u.shape[1]
    tm = min(t, 512)

    def body(x_ref, u_ref, w_ref, bo_ref, g_ref, b_ref, h_ref, hb_ref, xh_ref, rs_ref):
        y = lax.dot_general(u_ref[...], w_ref[...], NN, preferred_element_type=F32) + bo_ref[...]
        r = ALPHA * x_ref[...] + y
        mu = jnp.mean(r, axis=-1, keepdims=True)
        xc = r - mu
        var = jnp.mean(xc * xc, axis=-1, keepdims=True)
        rstd = lax.rsqrt(var + LN_EPS)
        xh = xc * rstd
        h = xh * g_ref[...] + b_ref[...]
        h_ref[...] = h
        hb_ref[...] = h.astype(BF16)
        xh_ref[...] = xh
        rs_ref[...] = rstd

    row = pl.BlockSpec((tm, d), lambda i: (i, 0))
    par = pl.BlockSpec((1, d), lambda i: (0, 0))
    return pl.pallas_call(
        body, name="ln0_fwd",
        out_shape=(jax.ShapeDtypeStruct((t, d), F32), jax.ShapeDtypeStruct((t, d), BF16),
                   jax.ShapeDtypeStruct((t, d), F32), jax.ShapeDtypeStruct((t, 1), F32)),
        grid=(t // tm,),
        in_specs=[row, pl.BlockSpec((tm, ku), lambda i: (i, 0)),
                  pl.BlockSpec((ku, d), lambda i: (0, 0), pipeline_mode=pl.Buffered(1)), par, par, par],
        out_specs=(row, row, row, pl.BlockSpec((tm, 1), lambda i: (i, 0))),
        compiler_params=_cparams(("parallel",)),
    )(x, u, w_out, b_out, g, b)


def _kv_mid_fwd(h1b, w_down, w_ukv, g, rc, rs):
    t, d = h1b.shape
    nkv = w_ukv.shape[1]
    tm = min(t, 512)
    r = KV_LORA_RANK

    def body(h_ref, w_ref, wu_ref, g_ref, c_ref, s_ref, ckv_ref, ckn_ref, kr_ref, kv_ref):
        ckv = lax.dot_general(h_ref[...], w_ref[...], NN, preferred_element_type=F32)
        ckv_ref[...] = ckv
        c = ckv[:, 0:r]
        ms = jnp.mean(c * c, axis=-1, keepdims=True)
        ckn = (c * lax.rsqrt(ms + RMS_EPS) * g_ref[...]).astype(BF16)
        ckn_ref[...] = ckn
        tr = ckv[:, r:r + LANES]
        kr_ref[...] = (tr * c_ref[...] + _swap_rope_halves(tr) * s_ref[...]).astype(BF16)
        kv_ref[...] = lax.dot_general(ckn, wu_ref[...], NN, preferred_element_type=F32).astype(BF16)

    lane_blk = pl.BlockSpec((tm, LANES), lambda i: (i, 0))
    return pl.pallas_call(
        body, name="kv_mid_fwd",
        out_shape=(jax.ShapeDtypeStruct((t, r + LANES), F32), jax.ShapeDtypeStruct((t, r), BF16),
                   jax.ShapeDtypeStruct((t, LANES), BF16), jax.ShapeDtypeStruct((t, nkv), BF16)),
        grid=(t // tm,),
        in_specs=[pl.BlockSpec((tm, d), lambda i: (i, 0)),
                  pl.BlockSpec((d, r + LANES), lambda i: (0, 0), pipeline_mode=pl.Buffered(1)),
                  pl.BlockSpec((r, nkv), lambda i: (0, 0), pipeline_mode=pl.Buffered(1)),
                  pl.BlockSpec((1, r), lambda i: (0, 0)), lane_blk, lane_blk],
        out_specs=(pl.BlockSpec((tm, r + LANES), lambda i: (i, 0)), pl.BlockSpec((tm, r), lambda i: (i, 0)), lane_blk,
                   pl.BlockSpec((tm, nkv), lambda i: (i, 0))),
        compiler_params=_cparams(("parallel",)),
    )(h1b, w_down, w_ukv, g, rc, rs)


def _b_in_qnorm(h1b, w_blocks, g):
    t, d = h1b.shape
    nblk, _, cb = w_blocks.shape
    n = nblk * cb
    r = Q_LORA_RANK
    tm = min(t, 512)

    def body(a_ref, wb_ref, g_ref, o_ref, cqn_ref, w_ref):
        @pl.when(pl.program_id(0) == 0)
        def _():
            for j in range(nblk):
                w_ref[:, j * cb:(j + 1) * cb] = wb_ref[j]

        acc = lax.dot_general(a_ref[...], w_ref[...], NN, preferred_element_type=F32)
        o_ref[...] = acc
        c = acc[:, 0:r]
        ms = jnp.mean(c * c, axis=-1, keepdims=True)
        cqn_ref[...] = (c * lax.rsqrt(ms + RMS_EPS) * g_ref[...]).astype(BF16)

    return pl.pallas_call(
        body, name="b_in",
        out_shape=(jax.ShapeDtypeStruct((t, n), F32), jax.ShapeDtypeStruct((t, r), BF16),
                   jax.ShapeDtypeStruct((d, n), BF16)),
        grid=(t // tm,),
        in_specs=[pl.BlockSpec((tm, d), lambda i: (i, 0)),
                  pl.BlockSpec((nblk, d, cb), lambda i: (0, 0, 0), pipeline_mode=pl.Buffered(1)),
                  pl.BlockSpec((1, r), lambda i: (0, 0))],
        out_specs=(pl.BlockSpec((tm, n), lambda i: (i, 0)), pl.BlockSpec((tm, r), lambda i: (i, 0)),
                   pl.BlockSpec((d, n), lambda i: (0, 0))),
        compiler_params=_cparams(("arbitrary",)),
    )(h1b, w_blocks, g)


def _q_up_rope(cqn, wuq, rc, rs):
    t, r = cqn.shape
    w = wuq.shape[1]
    tm = min(t, 1024)
    bn = 4 * HEAD_PAD

    def body(a_ref, b_ref, c_ref, s_ref, o_ref):
        q = lax.dot_general(a_ref[...], b_ref[...], NN, preferred_element_type=F32)
        c = c_ref[...]
        s = s_ref[...]
        for h in range(bn // HEAD_PAD):
            a = h * HEAD_PAD
            o_ref[:, a:a + LANES] = q[:, a:a + LANES].astype(BF16)
            tr = q[:, a + LANES:a + 2 * LANES]
            o_ref[:, a + LANES:a + 2 * LANES] = (tr * c + _swap_rope_halves(tr) * s).astype(BF16)

    return pl.pallas_call(
        body, name="q_up_rope",
        out_shape=jax.ShapeDtypeStruct((t, w), BF16),
        grid=(t // tm, w // bn),
        in_specs=[pl.BlockSpec((tm, r), lambda i, j: (i, 0)), pl.BlockSpec((r, bn), lambda i, j: (0, j)),
                  pl.BlockSpec((tm, LANES), lambda i, j: (i, 0)), pl.BlockSpec((tm, LANES), lambda i, j: (i, 0))],
        out_specs=pl.BlockSpec((tm, bn), lambda i, j: (i, j)),
        compiler_params=_cparams(("parallel", "parallel")),
    )(cqn, wuq, rc, rs)


def _flash_fwd(qcat, kv, kr, projb):
    t = qcat.shape[0]
    tq = min(t, 512)
    nq = t // tq
    exp2_scale = ATTN_SCALE * math.log2(math.e)
    z_first = Q_LORA_RANK // LANES

    def body(q_ref, kv_ref, kr_ref, z_ref, o_ref, o2_ref, lse_ref, kcat_ref):
        kcat_ref[:, 0:LANES] = kv_ref[:, 0:LANES]
        kcat_ref[:, LANES:] = kr_ref[...]
        rows = lax.broadcasted_iota(jnp.int32, (tq, tq), 0)
        cols = lax.broadcasted_iota(jnp.int32, (tq, tq), 1)
        for i in range(nq):
            a = i * tq
            q = q_ref[a:a + tq, :]
            sd = lax.dot_general(q, kcat_ref[a:a + tq, :], NT, preferred_element_type=F32)
            sd = jnp.where(cols <= rows, sd, MASK_VALUE)
            m = jnp.max(sd, axis=1, keepdims=True)
            if i > 0:
                sp = lax.dot_general(q, kcat_ref[0:a, :], NT, preferred_element_type=F32)
                m = jnp.maximum(m, jnp.max(sp, axis=1, keepdims=True))
            pd = jnp.exp2((sd - m) * exp2_scale)
            l = jnp.sum(pd, axis=1, keepdims=True)
            acc = lax.dot_general(pd.astype(BF16), kv_ref[a:a + tq, LANES:], NN, preferred_element_type=F32)
            if i > 0:
                pp = jnp.exp2((sp - m) * exp2_scale)
                l = l + jnp.sum(pp, axis=1, keepdims=True)
                acc = acc + lax.dot_general(pp.astype(BF16), kv_ref[0:a, LANES:], NN, preferred_element_type=F32)
            o = acc / l
            z = z_ref[a:a + tq, :]
            o_ref[a:a + tq, :] = o
            o2_ref[a:a + tq, :] = (o * (z * _sigmoid(z))).astype(BF16)
            lse_ref[a:a + tq, :] = jnp.broadcast_to(m * ATTN_SCALE + jnp.log(l), (tq, LANES))

    hw = N_HEADS * LANES
    head256 = pl.BlockSpec((t, HEAD_PAD), lambda h: (0, h))
    head128 = pl.BlockSpec((t, LANES), lambda h: (0, h))
    return pl.pallas_call(
        body, name="flash_fwd",
        out_shape=(jax.ShapeDtypeStruct((t, hw), F32), jax.ShapeDtypeStruct((t, hw), BF16),
                   jax.ShapeDtypeStruct((t, hw), F32)),
        grid=(N_HEADS,),
        in_specs=[head256, head256, pl.BlockSpec((t, LANES), lambda h: (0, 0), pipeline_mode=pl.Buffered(1)),
                  pl.BlockSpec((t, LANES), lambda h: (0, z_first + h))],
        out_specs=(head128, head128, head128),
        scratch_shapes=[pltpu.VMEM((t, HEAD_PAD), BF16)],
        compiler_params=_cparams(("parallel",)),
    )(qcat, kv, kr, projb)


def _final_ln_loss(h1, o2, w_out, g, b, target):
    t, d = h1.shape
    ku = o2.shape[1]
    tm = min(t, 512)

    def body(h_ref, u_ref, w_ref, g_ref, b_ref, t_ref, loss_ref, dr_ref, drb_ref, dg_ref, db_ref):
        i = pl.program_id(0)
        r = ALPHA * h_ref[...] + lax.dot_general(u_ref[...], w_ref[...], NN, preferred_element_type=F32)
        mu = jnp.mean(r, axis=-1, keepdims=True)
        xc = r - mu
        var = jnp.mean(xc * xc, axis=-1, keepdims=True)
        rstd = lax.rsqrt(var + LN_EPS)
        xh = xc * rstd
        g = g_ref[...]
        diff = xh * g + b_ref[...] - t_ref[...]
        part = 0.5 * jnp.sum(jnp.mean(diff * diff, axis=-1, keepdims=True))
        dh = diff * (1.0 / d)
        dgp = jnp.sum(dh * xh, axis=0, keepdims=True)
        dbp = jnp.sum(dh, axis=0, keepdims=True)

        @pl.when(i == 0)
        def _():
            loss_ref[...] = jnp.zeros_like(loss_ref)
            dg_ref[...] = jnp.zeros_like(dg_ref)
            db_ref[...] = jnp.zeros_like(db_ref)

        loss_ref[...] += jnp.full(loss_ref.shape, part, F32)
        dg_ref[...] += dgp
        db_ref[...] += dbp
        dxh = dh * g
        dr = rstd * (dxh - jnp.mean(dxh, axis=-1, keepdims=True) - xh * jnp.mean(dxh * xh, axis=-1, keepdims=True))
        dr_ref[...] = dr
        drb_ref[...] = dr.astype(BF16)

    row = pl.BlockSpec((tm, d), lambda i: (i, 0))
    par = pl.BlockSpec((1, d), lambda i: (0, 0))
    return pl.pallas_call(
        body, name="final_ln_loss",
        out_shape=(jax.ShapeDtypeStruct((1, LANES), F32), jax.ShapeDtypeStruct((t, d), F32),
                   jax.ShapeDtypeStruct((t, d), BF16), jax.ShapeDtypeStruct((1, d), F32),
                   jax.ShapeDtypeStruct((1, d), F32)),
        grid=(t // tm,),
        in_specs=[row, pl.BlockSpec((tm, ku), lambda i: (i, 0)),
                  pl.BlockSpec((ku, d), lambda i: (0, 0), pipeline_mode=pl.Buffered(1)), par, par, row],
        out_specs=(pl.BlockSpec((1, LANES), lambda i: (0, 0)), row, row, par, par),
        compiler_params=_cparams(("arbitrary",)),
    )(h1, o2, w_out, g, b, target)


def _gate_bwd(dr2b, w_out, o, projb):
    t, hw = o.shape
    d = dr2b.shape[1]
    tm = min(t, 1024)
    bw = Q_LORA_RANK
    nb = hw // bw
    wb = projb.shape[1]
    steps = (t // tm) * nb
    ring = 3
    assert steps >= ring

    def body(dr_ref, w_ref, o_hbm, p_hbm, dob_ref, dz_ref, obuf, zbuf, sems):
        s = pl.program_id(0) * nb + pl.program_id(1)

        def fetches(step):
            rows = pl.ds(pl.multiple_of((step // nb) * tm, tm), tm)
            col = (step % nb) * bw
            slot = step % ring
            return (pltpu.make_async_copy(o_hbm.at[rows, pl.ds(pl.multiple_of(col, bw), bw)], obuf.at[slot],
                                          sems.at[slot]),
                    pltpu.make_async_copy(p_hbm.at[rows, pl.ds(pl.multiple_of(col + bw, bw), bw)], zbuf.at[slot],
                                          sems.at[ring + slot]))

        @pl.when(s == 0)
        def _():
            for ahead in range(ring - 1):
                for cp in fetches(s + ahead):
                    cp.start()

        @pl.when(s + ring - 1 < steps)
        def _():
            for cp in fetches(s + ring - 1):
                cp.start()

        for cp in fetches(s):
            cp.wait()
        slot = s % ring
        z = zbuf[slot]
        sg = _sigmoid(z)
        w = w_ref[pl.ds(pl.multiple_of(pl.program_id(1) * bw, bw), bw), :]
        d2 = lax.dot_general(dr_ref[...], w, NT, preferred_element_type=F32)
        dob_ref[...] = (d2 * (z * sg)).astype(BF16)
        dz_ref[...] = (d2 * obuf[slot] * (sg * (1.0 + z * (1.0 - sg)))).astype(BF16)

    blk = pl.BlockSpec((tm, bw), lambda i, j: (i, j))
    blk1 = pl.BlockSpec((tm, bw), lambda i, j: (i, j + 1))
    hbm = pl.BlockSpec(memory_space=pl.ANY)
    return pl.pallas_call(
        body, name="gate_bwd",
        out_shape=(jax.ShapeDtypeStruct((t, hw), BF16), jax.ShapeDtypeStruct((t, wb), BF16)),
        grid=(t // tm, nb),
        in_specs=[pl.BlockSpec((tm, d), lambda i, j: (i, 0)),
                  pl.BlockSpec((hw, d), lambda i, j: (0, 0), pipeline_mode=pl.Buffered(1)), hbm, hbm],
        out_specs=(blk, blk1),
        scratch_shapes=[pltpu.VMEM((ring, tm, bw), F32), pltpu.VMEM((ring, tm, bw), F32),
                        pltpu.SemaphoreType.DMA((2 * ring,))],
        compiler_params=_cparams(("arbitrary", "arbitrary")),
    )(dr2b, w_out, o, projb)


def _flash_bwd(qcat, kv, kr, dob, lse, o, rc, rs):
    t = qcat.shape[0]
    tq = min(t, 512)
    nq = t // tq
    q_chunk = 2 * tq
    log2e = math.log2(math.e)
    exp2_scale = ATTN_SCALE * log2e
    once = pl.Buffered(1)

    def body(q_ref, kv_ref, kr_ref, do_ref, lse_ref, o_ref, c_ref, s_ref,
             dq_ref, dkv_ref, dkr_ref, kcat_ref, dqa_ref, dka_ref, dva_ref, dl_ref):
        h = pl.program_id(0)
        kcat_ref[:, 0:LANES] = kv_ref[:, 0:LANES]
        kcat_ref[:, LANES:] = kr_ref[...]
        dqa_ref[...] = jnp.zeros_like(dqa_ref)
        dl_ref[...] = jnp.sum(do_ref[...].astype(F32) * o_ref[...], axis=1, keepdims=True)

        @pl.when(h == 0)
        def _():
            dkr_ref[...] = jnp.zeros_like(dkr_ref)

        rows = lax.broadcasted_iota(jnp.int32, (tq, tq), 0)
        cols = lax.broadcasted_iota(jnp.int32, (tq, tq), 1)

        def block(qs, n, ks, masked):
            q = q_ref[qs:qs + n, :]
            kc = kcat_ref[ks:ks + tq, :]
            s = lax.dot_general(q, kc, NT, preferred_element_type=F32)
            if masked:
                s = jnp.where(cols <= rows, s, MASK_VALUE)
            p = jnp.exp2(s * exp2_scale - lse_ref[qs:qs + n, 0:1] * log2e)
            do = do_ref[qs:qs + n, :]
            dp = lax.dot_general(do, kv_ref[ks:ks + tq, LANES:], NT, preferred_element_type=F32)
            ds = (p * (dp - dl_ref[qs:qs + n, :])).astype(BF16)
            dva_ref[...] += lax.dot_general(p.astype(BF16), do, TN, preferred_element_type=F32)
            dka_ref[...] += lax.dot_general(ds, q, TN, preferred_element_type=F32)
            dqa_ref[qs:qs + n, :] += lax.dot_general(ds, kc, NN, preferred_element_type=F32)

        for kb in range(nq):
            a = kb * tq
            dka_ref[...] = jnp.zeros_like(dka_ref)
            dva_ref[...] = jnp.zeros_like(dva_ref)
            block(a, tq, a, True)
            qs = a + tq
            while qs < t:
                n = min(q_chunk, t - qs)
                block(qs, n, a, False)
                qs += n
            dk = dka_ref[...] * ATTN_SCALE
            dkv_ref[a:a + tq, 0:LANES] = dk[:, 0:LANES].astype(BF16)
            dkv_ref[a:a + tq, LANES:] = dva_ref[...].astype(BF16)
            dkr_ref[a:a + tq, :] += dk[:, LANES:]

        dq = dqa_ref[...] * ATTN_SCALE
        dq_ref[:, 0:LANES] = dq[:, 0:LANES].astype(BF16)
        d2 = dq[:, LANES:]
        dq_ref[:, LANES:] = (d2 * c_ref[...] - _swap_rope_halves(d2) * s_ref[...]).astype(BF16)

    hp = N_HEADS * HEAD_PAD
    head256 = pl.BlockSpec((t, HEAD_PAD), lambda h: (0, h))
    head128 = pl.BlockSpec((t, LANES), lambda h: (0, h))
    const128 = pl.BlockSpec((t, LANES), lambda h: (0, 0), pipeline_mode=once)
    return pl.pallas_call(
        body, name="flash_bwd",
        out_shape=(jax.ShapeDtypeStruct((t, hp), BF16), jax.ShapeDtypeStruct((t, hp), BF16),
                   jax.ShapeDtypeStruct((t, LANES), F32)),
        grid=(N_HEADS,),
        in_specs=[head256, head256, const128, head128, head128, head128, const128, const128],
        out_specs=(head256, head256, pl.BlockSpec((t, LANES), lambda h: (0, 0))),
        scratch_shapes=[pltpu.VMEM((t, HEAD_PAD), BF16), pltpu.VMEM((t, HEAD_PAD), F32),
                        pltpu.VMEM((tq, HEAD_PAD), F32), pltpu.VMEM((tq, LANES), F32), pltpu.VMEM((t, 1), F32)],
        compiler_params=_cparams(("arbitrary",)),
    )(qcat, kv, kr, dob, lse, o, rc, rs)


def _q_norm_bwd(dq2, w_uq, projb, g, dprojb):
    t = projb.shape[0]
    kq = dq2.shape[1]
    tm = min(t, 512)
    r = Q_LORA_RANK

    def body(dq_ref, w_ref, c_ref, g_ref, alias_ref, o_ref, dg_ref):
        i = pl.program_id(0)
        c = c_ref[...]
        d = lax.dot_general(dq_ref[...], w_ref[...], NT, preferred_element_type=F32)
        rr = lax.rsqrt(jnp.mean(c * c, axis=-1, keepdims=True) + RMS_EPS)
        xh = c * rr

        @pl.when(i == 0)
        def _():
            dg_ref[...] = jnp.zeros_like(dg_ref)

        dg_ref[...] += jnp.sum(d * xh, axis=0, keepdims=True)
        dxh = d * g_ref[...]
        o_ref[...] = (rr * (dxh - xh * jnp.mean(dxh * xh, axis=-1, keepdims=True))).astype(BF16)

    blk = pl.BlockSpec((tm, r), lambda i: (i, 0))
    par = pl.BlockSpec((1, r), lambda i: (0, 0))
    return pl.pallas_call(
        body, name="q_norm_bwd",
        out_shape=(jax.ShapeDtypeStruct(dprojb.shape, BF16), jax.ShapeDtypeStruct((1, r), F32)),
        grid=(t // tm,),
        in_specs=[pl.BlockSpec((tm, kq), lambda i: (i, 0)),
                  pl.BlockSpec((r, kq), lambda i: (0, 0), pipeline_mode=pl.Buffered(1)),
                  blk, par, pl.BlockSpec(memory_space=pl.ANY)],
        out_specs=(blk, par),
        input_output_aliases={4: 0},
        compiler_params=_cparams(("arbitrary",)),
    )(dq2, w_uq, projb, g, dprojb)


def _kv_mid_bwd(dkv, w_ukv, ckv, dkr, g, rc, rs):
    t = ckv.shape[0]
    kk = dkv.shape[1]
    tm = min(t, 512)
    r = KV_LORA_RANK

    def body(dkv_ref, w_ref, ckv_ref, dkr_ref, g_ref, c_ref, s_ref, o_ref, dg_ref):
        i = pl.program_id(0)
        c = ckv_ref[:, 0:r]
        d = lax.dot_general(dkv_ref[...], w_ref[...], NT, preferred_element_type=F32)
        rr = lax.rsqrt(jnp.mean(c * c, axis=-1, keepdims=True) + RMS_EPS)
        xh = c * rr

        @pl.when(i == 0)
        def _():
            dg_ref[...] = jnp.zeros_like(dg_ref)

        dg_ref[...] += jnp.sum(d * xh, axis=0, keepdims=True)
        dxh = d * g_ref[...]
        o_ref[:, 0:r] = (rr * (dxh - xh * jnp.mean(dxh * xh, axis=-1, keepdims=True))).astype(BF16)
        dk = dkr_ref[...]
        o_ref[:, r:] = (dk * c_ref[...] - _swap_rope_halves(dk) * s_ref[...]).astype(BF16)

    return pl.pallas_call(
        body, name="kv_mid_bwd",
        out_shape=(jax.ShapeDtypeStruct((t, r + LANES), BF16), jax.ShapeDtypeStruct((1, r), F32)),
        grid=(t // tm,),
        in_specs=[pl.BlockSpec((tm, kk), lambda i: (i, 0)),
                  pl.BlockSpec((r, kk), lambda i: (0, 0), pipeline_mode=pl.Buffered(1)),
                  pl.BlockSpec((tm, r + LANES), lambda i: (i, 0)),
                  pl.BlockSpec((tm, LANES), lambda i: (i, 0)), pl.BlockSpec((1, r), lambda i: (0, 0)),
                  pl.BlockSpec((tm, LANES), lambda i: (i, 0)), pl.BlockSpec((tm, LANES), lambda i: (i, 0))],
        out_specs=(pl.BlockSpec((tm, r + LANES), lambda i: (i, 0)), pl.BlockSpec((1, r), lambda i: (0, 0))),
        compiler_params=_cparams(("arbitrary",)),
    )(dkv, w_ukv, ckv, dkr, g, rc, rs)


def _ln0_bwd(dr2, dprojb, w_in, dckv, w_down, xh, rstd, g, after):
    t, d = dr2.shape
    kb, kd = dprojb.shape[1], dckv.shape[1]
    tm = min(t, 256)
    extra = [] if after is None else [after]

    def body(a_ref, pb_ref, wb_ref, pd_ref, wd_ref, xh_ref, rs_ref, g_ref, *rest):
        dr_ref, drb_ref, dg_ref, db_ref, dsum_ref = rest[-5:]
        i = pl.program_id(0)
        dh = (ALPHA * a_ref[...] + lax.dot_general(pb_ref[...], wb_ref[...], NT, preferred_element_type=F32)
              + lax.dot_general(pd_ref[...], wd_ref[...], NT, preferred_element_type=F32))
        xh = xh_ref[...]

        @pl.when(i == 0)
        def _():
            dg_ref[...] = jnp.zeros_like(dg_ref)
            db_ref[...] = jnp.zeros_like(db_ref)
            dsum_ref[...] = jnp.zeros_like(dsum_ref)

        dg_ref[...] += jnp.sum(dh * xh, axis=0, keepdims=True)
        db_ref[...] += jnp.sum(dh, axis=0, keepdims=True)
        dxh = dh * g_ref[...]
        dr = rs_ref[...] * (dxh - jnp.mean(dxh, axis=-1, keepdims=True) - xh * jnp.mean(dxh * xh, axis=-1, keepdims=True))
        dr_ref[...] = dr
        drb_ref[...] = dr.astype(BF16)
        dsum_ref[...] += jnp.sum(dr, axis=0, keepdims=True)

    row = pl.BlockSpec((tm, d), lambda i: (i, 0))
    par = pl.BlockSpec((1, d), lambda i: (0, 0))
    return pl.pallas_call(
        body, name="ln0_bwd",
        out_shape=(jax.ShapeDtypeStruct((t, d), F32), jax.ShapeDtypeStruct((t, d), BF16),
                   jax.ShapeDtypeStruct((1, d), F32), jax.ShapeDtypeStruct((1, d), F32),
                   jax.ShapeDtypeStruct((1, d), F32)),
        grid=(t // tm,),
        in_specs=[row, pl.BlockSpec((tm, kb), lambda i: (i, 0)),
                  pl.BlockSpec((d, kb), lambda i: (0, 0), pipeline_mode=pl.Buffered(1)),
                  pl.BlockSpec((tm, kd), lambda i: (i, 0)),
                  pl.BlockSpec((d, kd), lambda i: (0, 0), pipeline_mode=pl.Buffered(1)),
                  row, pl.BlockSpec((tm, 1), lambda i: (i, 0)), par]
                 + [pl.BlockSpec(memory_space=pl.ANY)] * len(extra),
        out_specs=(row, row, par, par, par),
        compiler_params=_cparams(("arbitrary",)),
    )(dr2, dprojb, w_in, dckv, w_down, xh, rstd, g, *extra)


def _conv_mid_bwd_rows(u1, proj, dr1b, w_out, ng, nb, after):
    t = u1.shape[0]
    e = CONV_WIDTH
    d = dr1b.shape[1]
    tm = min(t, 256)
    extra = [] if after is None else [after]

    rg = 16
    nt = t // tm

    def body(u1_ref, z_ref, dr_ref, w_ref, ng_ref, nb_ref, *rest):
        du1_ref, dz_ref, dng_ref, dnb_ref, dbz_ref, acc_ref, d_ref = rest[-7:]
        i = pl.program_id(0)

        @pl.when(i == 0)
        def _():
            acc_ref[...] = jnp.zeros_like(acc_ref)

        d_ref[...] = lax.dot_general(dr_ref[...], w_ref[...], NT, preferred_element_type=F32)
        g = ng_ref[...]
        b = nb_ref[...]

        def group(r, carry):
            rows = pl.ds(pl.multiple_of(r * rg, rg), rg)
            u1 = u1_ref[rows, :]
            mu = jnp.mean(u1, axis=-1, keepdims=True)
            xc = u1 - mu
            rstd = lax.rsqrt(jnp.mean(xc * xc, axis=-1, keepdims=True) + LN_EPS)
            xh = xc * rstd
            n = xh * g + b
            sn = _sigmoid(n)
            z = z_ref[rows, :]
            sz = _sigmoid(z)
            du2 = d_ref[rows, :]
            dz = du2 * (n * sn) * (sz * (1.0 + z * (1.0 - sz)))
            dn = du2 * (z * sz) * (sn * (1.0 + n * (1.0 - sn)))
            acc_ref[0:8, :] += _fold_rows(dn * xh)
            acc_ref[8:16, :] += _fold_rows(dn)
            acc_ref[16:24, :] += _fold_rows(dz)
            dz_ref[rows, :] = dz.astype(BF16)
            dxh = dn * g
            du1_ref[rows, :] = rstd * (dxh - jnp.mean(dxh, axis=-1, keepdims=True)
                                       - xh * jnp.mean(dxh * xh, axis=-1, keepdims=True))
            return carry

        lax.fori_loop(0, tm // rg, group, 0, unroll=8)

        @pl.when(i == nt - 1)
        def _():
            dng_ref[...] = jnp.sum(acc_ref[0:8, :], axis=0, keepdims=True)
            dnb_ref[...] = jnp.sum(acc_ref[8:16, :], axis=0, keepdims=True)
            dbz_ref[...] = jnp.sum(acc_ref[16:24, :], axis=0, keepdims=True)

    row = pl.BlockSpec((tm, e), lambda i: (i, 0))
    par = pl.BlockSpec((1, e), lambda i: (0, 0))
    return pl.pallas_call(
        body, name="conv_mid_bwd_rows",
        out_shape=(jax.ShapeDtypeStruct((t, e), F32), jax.ShapeDtypeStruct((t, 3 * e), BF16),
                   jax.ShapeDtypeStruct((1, e), F32), jax.ShapeDtypeStruct((1, e), F32),
                   jax.ShapeDtypeStruct((1, e), F32)),
        grid=(nt,),
        in_specs=[row, pl.BlockSpec((tm, e), lambda i: (i, 2)), pl.BlockSpec((tm, d), lambda i: (i, 0)),
                  pl.BlockSpec((e, d), lambda i: (0, 0), pipeline_mode=pl.Buffered(1)), par, par]
                 + [pl.BlockSpec(memory_space=pl.ANY)] * len(extra),
        out_specs=(row, pl.BlockSpec((tm, e), lambda i: (i, 2)), par, par, par),
        scratch_shapes=[pltpu.VMEM((24, e), F32), pltpu.VMEM((tm, e), F32)],
        compiler_params=_cparams(("arbitrary",)),
    )(u1, proj, dr1b, w_out, ng, nb, *extra)


def _conv_bwd(du1, proj, cw, dproj):
    t = du1.shape[0]
    e = CONV_WIDTH
    tm = min(t, 128)
    hb = tm // HALO
    nt = t // tm
    nchunk = e // LANES
    last_halo = t // HALO - 1

    def body(d_ref, dn_ref, val_ref, gg_ref, valh_ref, ggh_ref, cw_ref, alias_ref,
             dp_ref, dcw_ref, dcb_ref, dbv_ref, extu_ref, extd_ref, du0_ref, acc_ref, sh_ref):
        i = pl.program_id(0)
        h0 = valh_ref[...] * _sigmoid(ggh_ref[...])
        extu_ref[0:HALO, :] = jnp.where(i > 0, h0, 0.0)
        extu_ref[HALO:, :] = val_ref[...] * _sigmoid(gg_ref[...])
        extd_ref[0:tm, :] = d_ref[...]
        extd_ref[tm:, :] = jnp.where(i < nt - 1, dn_ref[...], 0.0)

        @pl.when(i == 0)
        def _():
            acc_ref[...] = jnp.zeros_like(acc_ref)

        def chunk(c, carry):
            col = pl.multiple_of(c * LANES, LANES)
            d = extd_ref[0:tm, pl.ds(col, LANES)]
            du0 = jnp.zeros((tm, LANES), F32)
            for res, taps in _taps_by_residue(lambda j: CONV_KERNEL - 1 - j):
                span = 8 * max(a for _, a in taps) + tm
                sh_ref[0:span, :] = extd_ref[res:res + span, pl.ds(col, LANES)]
                for j, a in taps:
                    du0 = du0 + cw_ref[j:j + 1, pl.ds(col, LANES)] * sh_ref[8 * a:8 * a + tm, :]
            for res, taps in _taps_by_residue(lambda j: HALO - (CONV_KERNEL - 1) + j):
                span = 8 * max(a for _, a in taps) + tm
                sh_ref[0:span, :] = extu_ref[res:res + span, pl.ds(col, LANES)]
                for j, a in taps:
                    prod = d * sh_ref[8 * a:8 * a + tm, :]
                    acc_ref[j * 8:(j + 1) * 8, pl.ds(col, LANES)] += _fold_rows(prod)
            acc_ref[CONV_KERNEL * 8:(CONV_KERNEL + 1) * 8, pl.ds(col, LANES)] += _fold_rows(d)
            du0_ref[:, pl.ds(col, LANES)] = du0
            return carry

        lax.fori_loop(0, nchunk, chunk, 0)
        kb = CONV_KERNEL * 8
        for r in range(0, tm, 16):
            sg = _sigmoid(gg_ref[r:r + 16, :])
            dval = du0_ref[r:r + 16, :] * sg
            dgg = dval * val_ref[r:r + 16, :] * (1.0 - sg)
            dp_ref[r:r + 16, 0:e] = dval.astype(BF16)
            dp_ref[r:r + 16, e:] = dgg.astype(BF16)
            acc_ref[kb + 8:kb + 16, :] += _fold_rows(dval)
            acc_ref[kb + 16:kb + 24, :] += _fold_rows(dgg)

        @pl.when(i == nt - 1)
        def _():
            for j in range(CONV_KERNEL):
                dcw_ref[j:j + 1, :] = jnp.sum(acc_ref[j * 8:(j + 1) * 8, :], axis=0, keepdims=True)
            dcw_ref[CONV_KERNEL:, :] = jnp.zeros((HALO - CONV_KERNEL, e), F32)
            dcb_ref[...] = jnp.sum(acc_ref[kb:kb + 8, :], axis=0, keepdims=True)
            dbv_ref[:, 0:e] = jnp.sum(acc_ref[kb + 8:kb + 16, :], axis=0, keepdims=True)
            dbv_ref[:, e:] = jnp.sum(acc_ref[kb + 16:kb + 24, :], axis=0, keepdims=True)

    row = lambda c: pl.BlockSpec((tm, e), lambda i: (i, c))
    halo_prev = lambda c: pl.BlockSpec((HALO, e), lambda i: (jnp.maximum(i * hb - 1, 0), c))
    halo_next = pl.BlockSpec((HALO, e), lambda i: (jnp.minimum((i + 1) * hb, last_halo), 0))
    return pl.pallas_call(
        body, name="conv_bwd",
        out_shape=(jax.ShapeDtypeStruct(dproj.shape, BF16), jax.ShapeDtypeStruct((HALO, e), F32),
                   jax.ShapeDtypeStruct((1, e), F32), jax.ShapeDtypeStruct((1, 2 * e), F32)),
        grid=(nt,),
        in_specs=[row(0), halo_next, row(0), row(1), halo_prev(0), halo_prev(1),
                  pl.BlockSpec((HALO, e), lambda i: (0, 0)), pl.BlockSpec(memory_space=pl.ANY)],
        out_specs=(pl.BlockSpec((tm, 2 * e), lambda i: (i, 0)), pl.BlockSpec((HALO, e), lambda i: (0, 0)),
                   pl.BlockSpec((1, e), lambda i: (0, 0)), pl.BlockSpec((1, 2 * e), lambda i: (0, 0))),
        scratch_shapes=[pltpu.VMEM((tm + HALO, e), F32), pltpu.VMEM((tm + HALO, e), F32),
                        pltpu.VMEM((tm, e), F32), pltpu.VMEM(((CONV_KERNEL + 3) * 8, e), F32),
                        pltpu.VMEM((tm + HALO, LANES), F32)],
        input_output_aliases={7: 0},
        compiler_params=_cparams(("arbitrary",)),
    )(du1, du1, proj, proj, proj, proj, cw, dproj)


def _adam_update(g, w, m, v):
    c1 = 1.0 - ADAM_B1 ** ADAM_STEP
    c2 = 1.0 - ADAM_B2 ** ADAM_STEP
    mn = ADAM_B1 * m + (1.0 - ADAM_B1) * g
    vn = ADAM_B2 * v + (1.0 - ADAM_B2) * (g * g)
    delta = -ADAM_LR * ((mn / c1) / (jnp.sqrt(vn / c2) + ADAM_EPS) + ADAM_WD * w)
    return delta, mn, vn


def _adamw(name, gbuf, w, m, v):
    parts = list(gbuf) if isinstance(gbuf, (list, tuple)) else [gbuf]
    npart = len(parts)
    r, c = w.shape
    tr = min(r // npart, 256)
    per_part = r // npart // tr
    assert r == npart * per_part * tr and all(p.shape == (N_DEV, r // npart, c) for p in parts)

    def body(*refs):
        g_refs = refs[:npart]
        w_ref, m_ref, v_ref, go_ref, d_ref, mo_ref, vo_ref = refs[npart:]
        i = pl.program_id(0)

        def run(g_ref):
            g = g_ref[0].astype(F32)
            for k in range(1, N_DEV):
                g = g + g_ref[k].astype(F32)
            go_ref[...] = g
            d_ref[...], mo_ref[...], vo_ref[...] = _adam_update(g, w_ref[...], m_ref[...], v_ref[...])

        if npart == 1:
            run(g_refs[0])
        else:
            for q in range(npart):
                pl.when((i >= q * per_part) & (i < (q + 1) * per_part))(functools.partial(run, g_refs[q]))

    blk = pl.BlockSpec((tr, c), lambda i: (i, 0))
    part_spec = lambda q: pl.BlockSpec((N_DEV, tr, c), lambda i: (0, jnp.clip(i - q * per_part, 0, per_part - 1), 0))
    shp = jax.ShapeDtypeStruct((r, c), F32)
    return pl.pallas_call(
        body, name="adamw_" + name,
        out_shape=(shp, shp, shp, shp),
        grid=(r // tr,),
        in_specs=[part_spec(q) for q in range(npart)] + [blk, blk, blk],
        out_specs=(blk, blk, blk, blk),
        compiler_params=_cparams(("parallel",)),
    )(*parts, w, m, v)


SMALL_TABLE_COLS = 256
SMALL_LAYOUT = {
    'ln_g': (0, 2, 1024), 'ln_b': (8, 2, 1024), 'a_b_in': (16, 1, 768), 'a_conv_b': (24, 1, 256),
    'a_norm_g': (32, 1, 256), 'a_norm_b': (40, 1, 256), 'a_b_out': (48, 1, 128), 'kv_norm_g': (56, 1, 256),
    'b_q_norm_g': (64, 1, 512), 'a_conv_w': (72, CONV_KERNEL, 256),
}
LOSS_ROW = 104
SMALL_TABLE_ROWS = 112
SMALL_NAMES = tuple(SMALL_LAYOUT)


def _small_pieces(name, row):
    r0, _, c = SMALL_LAYOUT[name]
    w = min(c, SMALL_TABLE_COLS)
    per_row = c // w
    return [(r0 + row * per_row + q, q * w, w) for q in range(per_row)]


def _pack_small_grads(dg0, dg1, db0, db1, dbv, dbz, dcb, dng, dnb, dba, dgkv, dgq, dcw, loss):
    e = CONV_WIDTH
    ce = e // N_DEV

    def body(dg0_ref, dg1_ref, db0_ref, db1_ref, dbv_ref, dbz_ref, dcb_ref, dng_ref, dnb_ref, dba_ref, dgkv_ref,
             dgq_ref, dcw_ref, loss_ref, o_ref):
        o_ref[...] = jnp.zeros_like(o_ref)
        for d in range(N_DEV):
            o_ref[d, LOSS_ROW:LOSS_ROW + 1, 0:LANES] = loss_ref[...]

        def put(d, name, row, src_ref, first_col=0):
            for trow, col, w in _small_pieces(name, row):
                o_ref[d, trow:trow + 1, 0:w] = src_ref[:, first_col + col:first_col + col + w]

        for d in range(N_DEV):
            put(d, 'ln_g', 0, dg0_ref)
            put(d, 'ln_g', 1, dg1_ref)
            put(d, 'ln_b', 0, db0_ref)
            put(d, 'ln_b', 1, db1_ref)
            for trow, col, w in _small_pieces('a_b_in', 0):
                gcol = d * 3 * ce + col
                src = dbv_ref[:, gcol:gcol + w] if gcol < 2 * e else dbz_ref[:, gcol - 2 * e:gcol - 2 * e + w]
                o_ref[d, trow:trow + 1, 0:w] = src
            put(d, 'a_conv_b', 0, dcb_ref, d * ce)
            put(d, 'a_norm_g', 0, dng_ref, d * ce)
            put(d, 'a_norm_b', 0, dnb_ref, d * ce)
            put(d, 'a_b_out', 0, dba_ref, d * LANES)
            put(d, 'kv_norm_g', 0, dgkv_ref)
            put(d, 'b_q_norm_g', 0, dgq_ref)
            r0 = SMALL_LAYOUT['a_conv_w'][0]
            o_ref[d, r0:r0 + HALO, 0:ce] = dcw_ref[:, d * ce:(d + 1) * ce]

    return pl.pallas_call(
        body, name="pack_small_grads",
        out_shape=jax.ShapeDtypeStruct((N_DEV, SMALL_TABLE_ROWS, SMALL_TABLE_COLS), F32),
        compiler_params=pltpu.CompilerParams(vmem_limit_bytes=VMEM_LIMIT),
    )(dg0, dg1, db0, db1, dbv, dbz, dcb, dng, dnb, dba, dgkv, dgq, dcw, loss)


def _adamw_small(gtab, ws, ms, vs):
    n = len(SMALL_NAMES)

    def body(*refs):
        g_ref = refs[0]
        w_refs, m_refs, v_refs = refs[1:1 + n], refs[1 + n:1 + 2 * n], refs[1 + 2 * n:1 + 3 * n]
        outs = refs[1 + 3 * n:]

        def summed(r0, r, c):
            g = g_ref[0, r0:r0 + r, 0:c]
            for k in range(1, N_DEV):
                g = g + g_ref[k, r0:r0 + r, 0:c]
            return g

        for i, name in enumerate(SMALL_NAMES):
            r0, r, c = SMALL_LAYOUT[name]
            if c <= SMALL_TABLE_COLS:
                blocks = [(slice(None), summed(r0, r, c))]
            else:
                blocks = [(slice(row, row + 1), jnp.concatenate([summed(tr, 1, w) for tr, _, w in _small_pieces(name, row)], axis=1))
                          for row in range(r)]
            for rows, g in blocks:
                delta, mn, vn = _adam_update(g, w_refs[i][rows, :], m_refs[i][rows, :], v_refs[i][rows, :])
                outs[i][rows, :] = g
                outs[n + i][rows, :] = delta
                outs[2 * n + i][rows, :] = mn
                outs[3 * n + i][rows, :] = vn
        outs[4 * n][...] = summed(LOSS_ROW, 1, LANES)

    shapes = tuple(jax.ShapeDtypeStruct(w.shape, F32) for w in ws)
    res = pl.pallas_call(
        body, name="adamw_small",
        out_shape=shapes * 4 + (jax.ShapeDtypeStruct((1, LANES), F32),),
        compiler_params=pltpu.CompilerParams(vmem_limit_bytes=VMEM_LIMIT),
    )(gtab, *ws, *ms, *vs)
    return res[:n], res[n:2 * n], res[2 * n:3 * n], res[3 * n:4 * n], res[4 * n]


def _mesh_peers():
    x, y, c = lax.axis_index("x"), lax.axis_index("y"), lax.axis_index("c")
    me = 4 * x + 2 * y + c
    peers = []
    for k in range(1, N_DEV):
        px = 1 - x if (k >> 2) & 1 else x
        py = 1 - y if (k >> 1) & 1 else y
        pc = 1 - c if k & 1 else c
        peers.append(((px, py, pc), 4 * px + 2 * py + pc))
    return me, peers


_HBM_SPEC = pl.BlockSpec(memory_space=pltpu.HBM)
_SEM_SPEC = pl.BlockSpec(memory_space=pltpu.SEMAPHORE)


ALL_PEERS = tuple(range(N_DEV - 1))
CHIP_LEVEL_PEERS = (0, 1, 3, 5)


def _split_copies(srcs, lands, send_sems, recv_sems, local_sems, gather, which=ALL_PEERS):
    me, peers = _mesh_peers()
    na = len(srcs)
    mine = lambda a, slot: srcs[a] if gather else srcs[a].at[slot]
    local = [pltpu.make_async_copy(mine(a, me), lands[a].at[me], local_sems.at[a]) for a in range(na)]
    sent, received = [], []
    for k, (dev, pid) in enumerate(peers):
        if k not in which:
            continue
        for a in range(na):
            s = a * (N_DEV - 1) + k
            sent.append(pltpu.make_async_remote_copy(
                src_ref=mine(a, pid), dst_ref=lands[a].at[me], send_sem=send_sems.at[s],
                recv_sem=recv_sems.at[s], device_id=dev, device_id_type=pl.DeviceIdType.MESH))
            received.append(pltpu.make_async_remote_copy(
                src_ref=mine(a, pid), dst_ref=lands[a].at[pid], send_sem=send_sems.at[s],
                recv_sem=recv_sems.at[s], device_id=dev, device_id_type=pl.DeviceIdType.MESH))
    return local, sent, received


def _comm_start(name, srcs, gather, after, which=ALL_PEERS):
    na = len(srcs)
    land_structs = [jax.ShapeDtypeStruct(((N_DEV,) + s.shape) if gather else s.shape, s.dtype) for s in srcs]
    extra = [] if after is None else [after]
    n_in = 2 * na + len(extra)

    def body(*refs):
        srcs_r, lands_r = refs[:na], refs[na:2 * na]
        send_sems, recv_sems, local_sems = refs[n_in:n_in + 3]
        token = refs[-1]
        local, sent, _ = _split_copies(srcs_r, lands_r, send_sems, recv_sems, local_sems, gather, which)
        for cp in local + sent:
            cp.start()
        token[...] = jnp.zeros_like(token)

    sem = pltpu.SemaphoreType.DMA((na * (N_DEV - 1),))
    outs = pl.pallas_call(
        body, name=name,
        out_shape=(sem, sem, pltpu.SemaphoreType.DMA((na,)),
                   *[pltpu.HBM(s.shape, s.dtype) for s in srcs],
                   *[pltpu.HBM(s.shape, s.dtype) for s in land_structs],
                   jax.ShapeDtypeStruct((8, LANES), F32)),
        in_specs=[_HBM_SPEC] * (2 * na) + [pl.BlockSpec(memory_space=pl.ANY)] * len(extra),
        out_specs=(_SEM_SPEC, _SEM_SPEC, _SEM_SPEC, *([_HBM_SPEC] * (2 * na)), pl.BlockSpec(memory_space=pltpu.VMEM)),
        input_output_aliases={i: 3 + i for i in range(2 * na)},
        compiler_params=pltpu.CompilerParams(has_side_effects=pltpu.SideEffectType.DATAFLOW_SIDE_EFFECTING),
    )(*[pltpu.with_memory_space_constraint(s, pltpu.HBM) for s in srcs],
      *[pltpu.with_memory_space_constraint(lax.empty(s.shape, s.dtype), pltpu.HBM) for s in land_structs],
      *extra)
    return (outs[:3], outs[3:3 + na], outs[3 + na:3 + 2 * na]), outs[-1]


def _comm_wait(name, handles, gather, after, which=ALL_PEERS):
    (send_sems, recv_sems, local_sems), srcs, lands = handles
    na = len(srcs)

    def body(*refs):
        srcs_r, lands_r = refs[:na], refs[na:2 * na]
        send_r, recv_r, local_r = refs[2 * na:2 * na + 3]
        local, sent, received = _split_copies(srcs_r, lands_r, send_r, recv_r, local_r, gather, which)
        for cp in sent:
            cp.wait_send()
        for cp in received:
            cp.wait_recv()
        for cp in local:
            cp.wait()

    outs = pl.pallas_call(
        body, name=name,
        out_shape=(*[pltpu.HBM(s.shape, s.dtype) for s in srcs], *[pltpu.HBM(s.shape, s.dtype) for s in lands]),
        in_specs=[_HBM_SPEC] * (2 * na) + [_SEM_SPEC] * 3 + [pl.BlockSpec(memory_space=pl.ANY)],
        out_specs=tuple([_HBM_SPEC] * (2 * na)),
        input_output_aliases={i: i for i in range(2 * na)},
        compiler_params=pltpu.CompilerParams(has_side_effects=pltpu.SideEffectType.DATAFLOW_SIDE_EFFECTING),
    )(*srcs, *lands, send_sems, recv_sems, local_sems, after)
    return outs[na:]


def _forward_to_sibling(name, lands):
    na = len(lands)

    def body(*refs):
        ins, outs = refs[:na], refs[na:2 * na]
        send_sems, recv_sems = refs[2 * na:]
        x, y, c = lax.axis_index("x"), lax.axis_index("y"), lax.axis_index("c")
        sibling = (x, y, 1 - c)
        chips = [(1 - x, y), (x, 1 - y), (1 - x, 1 - y)]
        slot = lambda cx, cy, cc: 4 * cx + 2 * cy + cc
        sends = []
        for j, (cx, cy) in enumerate(chips):
            for a in range(na):
                cp = pltpu.make_async_remote_copy(
                    src_ref=ins[a].at[slot(cx, cy, c)], dst_ref=outs[a].at[slot(cx, cy, c)],
                    send_sem=send_sems.at[a, j], recv_sem=recv_sems.at[a, j], device_id=sibling,
                    device_id_type=pl.DeviceIdType.MESH)
                cp.start()
                sends.append(cp)
        for j, (cx, cy) in enumerate(chips):
            for a in range(na):
                pltpu.make_async_remote_copy(
                    src_ref=ins[a].at[slot(cx, cy, 1 - c)], dst_ref=outs[a].at[slot(cx, cy, 1 - c)],
                    send_sem=send_sems.at[a, j], recv_sem=recv_sems.at[a, j], device_id=sibling,
                    device_id_type=pl.DeviceIdType.MESH).wait_recv()
        for cp in sends:
            cp.wait_send()

    hbm = pl.BlockSpec(memory_space=pl.ANY)
    return pl.pallas_call(
        body, name=name,
        out_shape=tuple(jax.ShapeDtypeStruct(s.shape, s.dtype) for s in lands),
        in_specs=[hbm] * na, out_specs=(hbm,) * na,
        input_output_aliases={a: a for a in range(na)},
        scratch_shapes=[pltpu.SemaphoreType.DMA((na, 3)), pltpu.SemaphoreType.DMA((na, 3))],
    )(*lands)


def _prep_weights(w):
    e = CONV_WIDTH
    p = {}
    if 'a_w_in' in w:
        if w['a_w_in'].shape == (N_DEV, D_MODEL, 3 * e // N_DEV):
            p['a_w_in3'] = w['a_w_in']
        else:
            p['a_w_in3'] = w['a_w_in'].reshape(D_MODEL, N_DEV, 3 * e // N_DEV).transpose(1, 0, 2)
        p['a_b_in'] = w['a_b_in'].reshape(1, 3 * e)
        p['a_conv_w'] = jnp.pad(w['a_conv_w'].reshape(CONV_KERNEL, e), ((0, HALO - CONV_KERNEL), (0, 0)))
        p['a_conv_b'] = w['a_conv_b'].reshape(1, e)
        p['a_norm_g'] = w['a_norm_g'].reshape(1, e)
        p['a_norm_b'] = w['a_norm_b'].reshape(1, e)
        p['a_b_out'] = w['a_b_out'].reshape(1, D_MODEL)
        p['kv_norm_g'] = w['kv_norm_g'].reshape(1, KV_LORA_RANK)
        p['b_q_norm_g'] = w['b_q_norm_g'].reshape(1, Q_LORA_RANK)
        p['ln_g'] = w['ln_g']
        p['ln_b'] = w['ln_b']
    if 'a_w_out' in w:
        p['a_w_out'] = w['a_w_out'].reshape(e, D_MODEL)
        p['kv_w_down'] = jnp.pad(w['kv_w_down'], ((0, 0), (0, KV_LORA_RANK + LANES - w['kv_w_down'].shape[1])))
        p['kv_w_ukv'] = jnp.concatenate([w['kv_w_uk'], w['kv_w_uv']], axis=2).reshape(KV_LORA_RANK, N_HEADS * HEAD_PAD)
        if w['b_w_in'].ndim == 3 and w['b_w_in'].shape[:2] == (N_DEV, D_MODEL):
            p['b_w_in3'] = w['b_w_in']
        else:
            p['b_w_in3'] = w['b_w_in'].reshape(D_MODEL, N_DEV, -1).transpose(1, 0, 2)
        uq = w['b_w_uq'].reshape(Q_LORA_RANK, N_HEADS, QK_NOPE_DIM + QK_ROPE_DIM)
        p['b_w_uq'] = jnp.pad(uq, ((0, 0), (0, 0), (0, HEAD_PAD - uq.shape[2]))).reshape(Q_LORA_RANK, N_HEADS * HEAD_PAD)
        p['b_w_out'] = w['b_w_out'].reshape(N_HEADS * V_HEAD_DIM, D_MODEL)
    return p


def _local_step(x, positions, target, comm):
    t = x.shape[0]
    e = CONV_WIDTH
    freqs = ROPE_THETA ** (-jnp.arange(0, QK_ROPE_DIM, 2, dtype=F32) / QK_ROPE_DIM)
    freq_row = jnp.concatenate([freqs, freqs, jnp.zeros((LANES - QK_ROPE_DIM,), F32)]).reshape(1, LANES)
    rc, rs, xb = _rope_tables(positions.reshape(t, 1), freq_row, x, comm.start_token())
    p = comm.first_weights(rc)
    ln_g0, ln_b0 = p['ln_g'][0:1], p['ln_b'][0:1]
    ln_g1, ln_b1 = p['ln_g'][1:2], p['ln_b'][1:2]

    proj = _matmul(xb, p['a_w_in3'], bias=p['a_b_in'], name="a_in", b_blocked=True, bm=2048,
                   after=comm.first_after())
    u1, u2 = _conv_mid_fwd(proj, p['a_conv_w'], p['a_conv_b'], p['a_norm_g'], p['a_norm_b'])
    p = {**p, **comm.rest_weights(u2)}
    h1, h1b, xh1, rstd1 = _ln_res_fwd(x, u2, p['a_w_out'], p['a_b_out'], ln_g0, ln_b0)
    ckv, ckn, kr, kv = _kv_mid_fwd(h1b, p['kv_w_down'], p['kv_w_ukv'], p['kv_norm_g'], rc, rs)
    projb, cqn, b_w_in = _b_in_qnorm(h1b, p['b_w_in3'], p['b_q_norm_g'])
    qcat = _q_up_rope(cqn, p['b_w_uq'], rc, rs)
    o, o2, lse = _flash_fwd(qcat, kv, kr, projb)
    loss, dr2, dr2b, dg1, db1 = _final_ln_loss(h1, o2, p['b_w_out'], ln_g1, ln_b1, target)

    g = {}
    g['b_w_out'] = _matmul(o2, dr2b, trans_a=True, name="d_b_out", out_dtype=BF16, bm=512, bk=t)
    dob, dprojb = _gate_bwd(dr2b, p['b_w_out'], o, projb)
    dq2, dkv, dkr = _flash_bwd(qcat, kv, kr, dob, lse, o, rc, rs)
    duq = _matmul(cqn, dq2, trans_a=True, name="d_q_up", out_dtype=BF16, bk=t)
    dprojb, dgq = _q_norm_bwd(dq2, p['b_w_uq'], projb, p['b_q_norm_g'], dprojb)
    g['b_w_in'] = _matmul(h1b, dprojb, trans_a=True, name="d_b_in", bn=2 * dprojb.shape[1] // N_DEV, bk=t, out_dtype=BF16,
                          out_blocked=True, out_parts=2)
    dukv = _matmul(ckn, dkv, trans_a=True, name="d_kv_up", out_dtype=BF16, bk=t)
    dckv, dgkv = _kv_mid_bwd(dkv, p['kv_w_ukv'], ckv, dkr, p['kv_norm_g'], rc, rs)
    ddown = _matmul(h1b, dckv, trans_a=True, name="d_kv_down", out_dtype=BF16, bm=512, bk=t)
    g['b_w_out'] = g['b_w_out'].reshape(N_DEV, -1, D_MODEL)
    g['kv_w_down'] = ddown[:, :KV_LORA_RANK + QK_ROPE_DIM].reshape(N_DEV, -1, KV_LORA_RANK + QK_ROPE_DIM)
    dukv = dukv.reshape(KV_LORA_RANK, N_HEADS, HEAD_PAD)
    g['kv_w_uk'] = dukv[:, :, :QK_NOPE_DIM].reshape(N_DEV, -1, QK_NOPE_DIM)
    g['kv_w_uv'] = dukv[:, :, QK_NOPE_DIM:].reshape(N_DEV, -1, V_HEAD_DIM)
    g['b_w_uq'] = duq.reshape(Q_LORA_RANK, N_HEADS, HEAD_PAD)[:, :, :QK_NOPE_DIM + QK_ROPE_DIM].reshape(
        N_DEV, -1, QK_NOPE_DIM + QK_ROPE_DIM)
    sent1 = comm.send_group1(g)
    dr1, dr1b, dg0, db0, dba_out = _ln0_bwd(dr2, dprojb, b_w_in, dckv, p['kv_w_down'], xh1, rstd1, ln_g0, sent1)
    dw_out = _matmul(u2, dr1b, trans_a=True, name="d_a_out", out_dtype=BF16, bm=512, bk=t).reshape(N_DEV, -1, D_MODEL)
    sent1b = comm.send_group1b({'a_w_out': dw_out})
    du1, dproj, dng, dnb, dbz = _conv_mid_bwd_rows(u1, proj, dr1b, p['a_w_out'], p['a_norm_g'], p['a_norm_b'], sent1b)
    dproj, dcw, dcb, dbv = _conv_bwd(du1, proj, p['a_conv_w'], dproj)
    half = D_MODEL // 2
    dw_lo = _matmul(xb, dproj, trans_a=True, name="d_a_in_lo", out_dtype=BF16, bm=half, bn=3 * e // N_DEV, bk=t,
                    out_blocked=True, a_cols=(0, half))
    small = (dg0, dg1, db0, db1, dbv, dbz, dcb, dng, dnb, dba_out, dgkv, dgq, dcw, loss)
    sent2 = comm.send_group2(dw_lo, small)
    dw_hi = _matmul(xb, dproj, trans_a=True, name="d_a_in_hi", out_dtype=BF16, bm=half, bn=3 * e // N_DEV, bk=t,
                    out_blocked=True, after=sent2, a_cols=(half, half))
    sent3 = comm.send_group3(dw_hi)
    grad_x = _grad_x(dproj, p['a_w_in3'], dr1, sent3)
    return loss, grad_x


def _grad_x(dproj, w3, dr1, after):
    t, d = dr1.shape
    nblk, _, cb = w3.shape
    tm = min(t, 512)
    extra = [] if after is None else [after]

    def body(a_ref, w_ref, dr_ref, *rest):
        o_ref = rest[-1]
        acc = ALPHA * dr_ref[...]
        for k in range(nblk):
            acc = acc + lax.dot_general(a_ref[:, k * cb:(k + 1) * cb], w_ref[k], NT, preferred_element_type=F32)
        o_ref[...] = acc

    row = pl.BlockSpec((tm, d), lambda i: (i, 0))
    return pl.pallas_call(
        body, name="grad_x",
        out_shape=jax.ShapeDtypeStruct((t, d), F32),
        grid=(t // tm,),
        in_specs=[pl.BlockSpec((tm, nblk * cb), lambda i: (i, 0)),
                  pl.BlockSpec((nblk, d, cb), lambda i: (0, 0, 0), pipeline_mode=pl.Buffered(1)), row]
                 + [pl.BlockSpec(memory_space=pl.ANY)] * len(extra),
        out_specs=row,
        compiler_params=_cparams(("parallel",)),
    )(dproj, w3, dr1, *extra)


def _shard_2d(a):
    return a.reshape(-1, a.shape[-1])


def _gathered_to_full(name, blocks):
    if name in ('a_w_in', 'b_w_in'):
        return blocks
    if name == 'a_conv_w':
        return blocks.transpose(1, 0, 2).reshape(CONV_KERNEL, -1)
    if name in ('a_b_in', 'a_conv_b', 'a_norm_g', 'a_norm_b', 'a_b_out'):
        return blocks.reshape(-1)
    if name in ('kv_w_uk', 'kv_w_uv'):
        return blocks.reshape(KV_LORA_RANK, N_HEADS, -1)
    if name == 'b_w_uq':
        return blocks.reshape(Q_LORA_RANK, N_HEADS, -1)
    return blocks.reshape(-1, blocks.shape[-1])


def kernel(x, positions, ln_g, ln_b, a_w_in, a_b_in, a_conv_w, a_conv_b, a_norm_g, a_norm_b, a_w_out, a_b_out, kv_w_down, kv_norm_g, kv_w_uk, kv_w_uv, b_w_in, b_q_norm_g, b_w_uq, b_w_out, loss_target, m_ln_g, m_ln_b, m_a_w_in, m_a_b_in, m_a_conv_w, m_a_conv_b, m_a_norm_g, m_a_norm_b, m_a_w_out, m_a_b_out, m_kv_w_down, m_kv_norm_g, m_kv_w_uk, m_kv_w_uv, m_b_w_in, m_b_q_norm_g, m_b_w_uq, m_b_w_out, v_ln_g, v_ln_b, v_a_w_in, v_a_b_in, v_a_conv_w, v_a_conv_b, v_a_norm_g, v_a_norm_b, v_a_w_out, v_a_b_out, v_kv_w_down, v_kv_norm_g, v_kv_w_uk, v_kv_w_uv, v_b_w_in, v_b_q_norm_g, v_b_w_uq, v_b_w_out):
    w = dict(ln_g=ln_g, ln_b=ln_b, a_w_in=a_w_in, a_b_in=a_b_in, a_conv_w=a_conv_w, a_conv_b=a_conv_b,
             a_norm_g=a_norm_g, a_norm_b=a_norm_b, a_w_out=a_w_out, a_b_out=a_b_out, kv_w_down=kv_w_down,
             kv_norm_g=kv_norm_g, kv_w_uk=kv_w_uk, kv_w_uv=kv_w_uv, b_w_in=b_w_in, b_q_norm_g=b_q_norm_g,
             b_w_uq=b_w_uq, b_w_out=b_w_out)
    m = dict(ln_g=m_ln_g, ln_b=m_ln_b, a_w_in=m_a_w_in, a_b_in=m_a_b_in, a_conv_w=m_a_conv_w, a_conv_b=m_a_conv_b,
             a_norm_g=m_a_norm_g, a_norm_b=m_a_norm_b, a_w_out=m_a_w_out, a_b_out=m_a_b_out, kv_w_down=m_kv_w_down,
             kv_norm_g=m_kv_norm_g, kv_w_uk=m_kv_w_uk, kv_w_uv=m_kv_w_uv, b_w_in=m_b_w_in, b_q_norm_g=m_b_q_norm_g,
             b_w_uq=m_b_w_uq, b_w_out=m_b_w_out)
    v = dict(ln_g=v_ln_g, ln_b=v_ln_b, a_w_in=v_a_w_in, a_b_in=v_a_b_in, a_conv_w=v_a_conv_w, a_conv_b=v_a_conv_b,
             a_norm_g=v_a_norm_g, a_norm_b=v_a_norm_b, a_w_out=v_a_w_out, a_b_out=v_a_b_out, kv_w_down=v_kv_w_down,
             kv_norm_g=v_kv_norm_g, kv_w_uk=v_kv_w_uk, kv_w_uv=v_kv_w_uv, b_w_in=v_b_w_in, b_q_norm_g=v_b_q_norm_g,
             b_w_uq=v_b_w_uq, b_w_out=v_b_w_out)

    small_flat = jnp.concatenate([w[n].reshape(-1) for n in SMALL_WEIGHTS]).reshape(1, -1)
    rest_names = [n for n in MATMUL_WEIGHTS if n != 'a_w_in']
    group1 = ('b_w_out', 'b_w_uq', 'b_w_in', 'kv_w_uk', 'kv_w_uv', 'kv_w_down')
    group1b = ('a_w_out',)

    class Comm:
        def __init__(self):
            self.first, self.first_token = _comm_start(
                "gather_first_start", [_shard_2d(w['a_w_in']).astype(BF16), small_flat], True, None, CHIP_LEVEL_PEERS)

        def start_token(self):
            return self.first_token

        def first_weights(self, after):
            lands = _comm_wait("gather_first_wait", self.first, True, after, CHIP_LEVEL_PEERS)
            ga = _forward_to_sibling("gather_first_forward", list(lands))
            full = {'a_w_in': ga[0]}
            gs = ga[1].reshape(N_DEV, -1)
            off = 0
            for n in SMALL_WEIGHTS:
                size = math.prod(w[n].shape)
                full[n] = _gathered_to_full(n, gs[:, off:off + size].reshape((N_DEV,) + _shard_2d(w[n]).shape))
                off += size
            for n in REPLICATED:
                full[n] = w[n]
            dense = lambda a: a if a.shape[-1] % LANES == 0 else a.reshape(-1, LANES)
            self.rest, self.rest_token = _comm_start(
                "gather_rest_start", [dense(_shard_2d(w[n]).astype(BF16)) for n in rest_names], True, ga[0])
            return _prep_weights(full)

        def first_after(self):
            return self.rest_token

        def rest_weights(self, after):
            lands = _comm_wait("gather_rest_wait", self.rest, True, after)
            blocks = {n: lands[i].reshape((N_DEV,) + _shard_2d(w[n]).shape) for i, n in enumerate(rest_names)}
            return _prep_weights({n: _gathered_to_full(n, blocks[n]) for n in rest_names})

        def send_group1(self, g):
            self.g1, token = _comm_start("exchange_g1_start", [g[n] for n in group1], False, None)
            return token

        def send_group1b(self, g):
            self.g1b, token = _comm_start("exchange_g1b_start", [g[n] for n in group1b], False, None)
            return token

        def send_group2(self, dw_lo, small):
            self.g2, token = _comm_start("exchange_g2_start", [dw_lo, _pack_small_grads(*small)], False, None)
            return token

        def send_group3(self, dw_hi):
            self.g3, token = _comm_start("exchange_g3_start", [dw_hi], False, None)
            return token

    comm = Comm()

    _, grad_x = _local_step(x[0], positions[0], loss_target[0], comm)

    res = {}
    recv1 = _comm_wait("exchange_g1_wait", comm.g1, False, grad_x)
    for i, n in enumerate(group1):
        res[n] = _adamw(n, recv1[i], _shard_2d(w[n]), _shard_2d(m[n]), _shard_2d(v[n]))
    recv1b = _comm_wait("exchange_g1b_wait", comm.g1b, False, res[group1[-1]][0])
    for i, n in enumerate(group1b):
        res[n] = _adamw(n, recv1b[i], _shard_2d(w[n]), _shard_2d(m[n]), _shard_2d(v[n]))
    recv2 = _comm_wait("exchange_g2_wait", comm.g2, False, res[group1b[-1]][0])
    recv3 = _comm_wait("exchange_g3_wait", comm.g3, False, recv2[0])
    res['a_w_in'] = _adamw('a_w_in', [recv2[0], recv3[0]], _shard_2d(w['a_w_in']), _shard_2d(m['a_w_in']),
                           _shard_2d(v['a_w_in']))
    two_d = lambda d: [_shard_2d(d[n]) if d[n].ndim > 1 else d[n].reshape(1, -1) for n in SMALL_NAMES]
    sg, sd, sm, sv, loss = _adamw_small(recv2[-1], two_d(w), two_d(m), two_d(v))
    for i, n in enumerate(SMALL_NAMES):
        res[n] = (sg[i], sd[i], sm[i], sv[i])
    outs = [[res[n][j].reshape(w[n].shape) for n in WEIGHT_NAMES] for j in range(4)]
    return (loss[0, 0], grad_x[None], *outs[0], *outs[1], *outs[2], *outs[3])
```

```python
import functools
import math

import jax
import jax.numpy as jnp
from jax import lax
from jax.experimental import pallas as pl
from jax.experimental.pallas import tpu as pltpu

F32 = jnp.float32
BF16 = jnp.bfloat16

D_MODEL = 1024
DEPTH = 2
CONV_WIDTH = 2 * D_MODEL
CONV_KERNEL = 31
N_HEADS = 16
QK_NOPE_DIM = 128
QK_ROPE_DIM = 64
V_HEAD_DIM = 128
KV_LORA_RANK = D_MODEL // 4
Q_LORA_RANK = D_MODEL // 2
ROPE_THETA = 10000.0
LN_EPS = 1e-5
RMS_EPS = 1e-6
MASK_VALUE = -1e30
ALPHA = (2.0 * DEPTH) ** 0.25
ATTN_SCALE = 1.0 / math.sqrt(QK_NOPE_DIM + QK_ROPE_DIM)

ADAM_LR = 0.001
ADAM_B1 = 0.9
ADAM_B2 = 0.999
ADAM_EPS = 1e-08
ADAM_WD = 0.01
ADAM_STEP = 10

N_DEV = 8
LANES = 128
HEAD_PAD = 256
HALO = 32
VMEM_LIMIT = 56 * 1024 * 1024

WEIGHT_NAMES = ['ln_g', 'ln_b', 'a_w_in', 'a_b_in', 'a_conv_w', 'a_conv_b', 'a_norm_g', 'a_norm_b',
                'a_w_out', 'a_b_out', 'kv_w_down', 'kv_norm_g', 'kv_w_uk', 'kv_w_uv', 'b_w_in',
                'b_q_norm_g', 'b_w_uq', 'b_w_out']
REPLICATED = ('ln_g', 'ln_b', 'kv_norm_g', 'b_q_norm_g')
MATMUL_WEIGHTS = ('a_w_in', 'a_w_out', 'kv_w_down', 'kv_w_uk', 'kv_w_uv', 'b_w_in', 'b_w_uq', 'b_w_out')
SMALL_WEIGHTS = ('a_b_in', 'a_conv_w', 'a_conv_b', 'a_norm_g', 'a_norm_b', 'a_b_out')

NN = (((1,), (0,)), ((), ()))
NT = (((1,), (1,)), ((), ()))
TN = (((0,), (0,)), ((), ()))


def _cparams(sem):
    return pltpu.CompilerParams(dimension_semantics=sem, vmem_limit_bytes=VMEM_LIMIT)


def _sigmoid(x):
    return 0.5 * jnp.tanh(0.5 * x) + 0.5


def _fold_rows(x):
    parts = [x[r:r + 8] for r in range(0, x.shape[0], 8)]
    while len(parts) > 1:
        parts = [parts[i] + parts[i + 1] for i in range(0, len(parts) - 1, 2)] + ([parts[-1]] if len(parts) % 2 else [])
    return parts[0]


def _taps_by_residue(offset_of):
    groups = {}
    for j in range(CONV_KERNEL):
        off = offset_of(j)
        groups.setdefault(off % 8, []).append((j, off // 8))
    return sorted(groups.items())


def _swap_rope_halves(t):
    lane = lax.broadcasted_iota(jnp.int32, t.shape, 1)
    return jnp.where(lane < QK_ROPE_DIM // 2, pltpu.roll(t, LANES - QK_ROPE_DIM // 2, 1),
                     pltpu.roll(t, QK_ROPE_DIM // 2, 1))


def _matmul(a, b, *, name, bias=None, trans_a=False, trans_b=False, out_dtype=F32, bm=1024, bn=1024, bk=1024,
            b_blocked=False, out_blocked=False, out_parts=1, after=None, a_cols=None, b_ring=0):
    m_off = 0
    if trans_a:
        kdim, m = a.shape
        if a_cols is not None:
            m_off, m = a_cols
    else:
        m, kdim = a.shape
    if b_blocked and trans_b:
        n, k2 = b.shape[1], b.shape[0] * b.shape[2]
        bk = b.shape[2]
    elif b_blocked:
        k2, n = b.shape[1], b.shape[0] * b.shape[2]
        bn = b.shape[2]
    elif trans_b:
        n, k2 = b.shape
    else:
        k2, n = b.shape
    assert kdim == k2, (a.shape, b.shape)
    bm, bn, bk = min(bm, m), min(bn, n), min(bk, kdim)
    assert m % bm == 0 and n % bn == 0 and kdim % bk == 0, (name, m, n, kdim, bm, bn, bk)
    nk = kdim // bk
    assert out_parts == 1 or (out_blocked and bn % out_parts == 0)
    pw = bn // out_parts
    dims = (((0 if trans_a else 1,), (1 if trans_b else 0,)), ((), ()))
    has_bias = bias is not None

    def body(*refs):
        refs = list(refs)
        a_ref, b_ref = refs[:2]
        del refs[:2]
        bias_ref = refs.pop(0) if has_bias else None
        if after is not None:
            refs.pop(0)
        o_ref = refs.pop(0)
        rest = refs
        if b_ring:
            buf, sems = rest[-2:]
            s = pl.program_id(1)

            def fetch(step):
                slot = step % b_ring
                return pltpu.make_async_copy(b_ref.at[:, pl.ds(pl.multiple_of(step * bn, LANES), bn)], buf.at[slot],
                                             sems.at[slot])

            @pl.when(s == 0)
            def _():
                for ahead in range(b_ring - 1):
                    fetch(s + ahead).start()

            @pl.when(s + b_ring - 1 < n // bn)
            def _():
                fetch(s + b_ring - 1).start()

            fetch(s).wait()
            b_val = buf[s % b_ring]
        else:
            b_val = b_ref[...]
        prod = lax.dot_general(a_ref[...], b_val, dims, preferred_element_type=F32)

        def finish(acc):
            if has_bias:
                acc = acc + bias_ref[...]
            if out_parts == 1:
                o_ref[...] = acc.astype(out_dtype)
            else:
                for q in range(out_parts):
                    o_ref[q] = acc[:, q * pw:(q + 1) * pw].astype(out_dtype)

        if nk == 1:
            finish(prod)
        else:
            acc_ref = rest[0]
            k = pl.program_id(2)

            @pl.when(k == 0)
            def _():
                acc_ref[...] = prod

            @pl.when(k > 0)
            def _():
                acc_ref[...] += prod

            @pl.when(k == nk - 1)
            def _():
                finish(acc_ref[...])

    assert m_off % bm == 0
    a_first = m_off // bm
    a_spec = (pl.BlockSpec((bk, bm), lambda i, j, k: (k, i + a_first)) if trans_a
              else pl.BlockSpec((bm, bk), lambda i, j, k: (i, k)))
    if b_ring:
        assert not (b_blocked or trans_b) and nk == 1 and m == bm and bn % LANES == 0 and n // bn >= b_ring
        b_spec = pl.BlockSpec(memory_space=pl.ANY)
    elif b_blocked and trans_b:
        b_spec = pl.BlockSpec((None, bn, bk), lambda i, j, k: (k, j, 0))
    elif b_blocked:
        b_spec = pl.BlockSpec((None, bk, bn), lambda i, j, k: (j, k, 0))
    elif trans_b:
        b_spec = pl.BlockSpec((bn, bk), lambda i, j, k: (j, k))
    else:
        b_spec = pl.BlockSpec((bk, bn), lambda i, j, k: (k, j))
    in_specs = [a_spec, b_spec]
    args = [a, b]
    if has_bias:
        in_specs.append(pl.BlockSpec((1, bn), lambda i, j, k: (0, j)))
        args.append(bias)
    if after is not None:
        in_specs.append(pl.BlockSpec(memory_space=pl.ANY))
        args.append(after)
    if out_blocked and out_parts > 1:
        out_struct = jax.ShapeDtypeStruct((n // pw, m, pw), out_dtype)
        out_spec = pl.BlockSpec((out_parts, bm, pw), lambda i, j, k: (j, i, 0))
    elif out_blocked:
        out_struct = jax.ShapeDtypeStruct((n // bn, m, bn), out_dtype)
        out_spec = pl.BlockSpec((None, bm, bn), lambda i, j, k: (j, i, 0))
    else:
        out_struct = jax.ShapeDtypeStruct((m, n), out_dtype)
        out_spec = pl.BlockSpec((bm, bn), lambda i, j, k: (i, j))
    return pl.pallas_call(
        body, name=name,
        out_shape=out_struct,
        grid=(m // bm, n // bn, nk),
        in_specs=in_specs,
        out_specs=out_spec,
        scratch_shapes=([pltpu.VMEM((bm, bn), F32)] if nk > 1 else [])
                       + ([pltpu.VMEM((b_ring, bk, bn), b.dtype), pltpu.SemaphoreType.DMA((b_ring,))] if b_ring else []),
        compiler_params=_cparams(("arbitrary",) * 3 if b_ring else ("parallel", "parallel", "arbitrary")),
    )(*args)


def _rope_tables(pos_col, freq_row, x, after):
    t, d = x.shape
    tm = min(t, 512)
    extra = [] if after is None else [after]

    def body(pos_ref, f_ref, x_ref, *rest):
        c_ref, s_ref, xb_ref = rest[-3:]
        ang = pos_ref[...].astype(F32) * f_ref[...]
        lane = lax.broadcasted_iota(jnp.int32, ang.shape, 1)
        cos = jnp.cos(ang)
        sin = jnp.sin(ang)
        c_ref[...] = jnp.where(lane < QK_ROPE_DIM, cos, 0.0)
        s_ref[...] = jnp.where(lane < QK_ROPE_DIM // 2, -sin, jnp.where(lane < QK_ROPE_DIM, sin, 0.0))
        xb_ref[...] = x_ref[...].astype(BF16)

    lane_blk = pl.BlockSpec((tm, LANES), lambda i: (i, 0))
    row = pl.BlockSpec((tm, d), lambda i: (i, 0))
    return pl.pallas_call(
        body, name="rope_tables",
        out_shape=(jax.ShapeDtypeStruct((t, LANES), F32), jax.ShapeDtypeStruct((t, LANES), F32),
                   jax.ShapeDtypeStruct((t, d), BF16)),
        grid=(t // tm,),
        in_specs=[pl.BlockSpec((tm, 1), lambda i: (i, 0)), pl.BlockSpec((1, LANES), lambda i: (0, 0)), row]
                 + [pl.BlockSpec(memory_space=pl.ANY)] * len(extra),
        out_specs=(lane_blk, lane_blk, row),
        compiler_params=_cparams(("parallel",)),
    )(pos_col, freq_row, x, *extra)


def _conv_mid_fwd(proj, cw, cb, ng, nb):
    t = proj.shape[0]
    e = CONV_WIDTH
    tm = min(t, 128)
    hb = tm // HALO
    nchunk = e // LANES

    def body(val_ref, gg_ref, z_ref, valh_ref, ggh_ref, cw_ref, cb_ref, ng_ref, nb_ref, u1_ref, u2_ref, ext_ref, sh_ref):
        i = pl.program_id(0)
        u0 = val_ref[...] * _sigmoid(gg_ref[...])
        h0 = valh_ref[...] * _sigmoid(ggh_ref[...])
        ext_ref[0:HALO, :] = jnp.where(i > 0, h0, 0.0)
        ext_ref[HALO:, :] = u0

        def chunk(c, carry):
            col = pl.multiple_of(c * LANES, LANES)
            acc = jnp.zeros((tm, LANES), F32)
            for res, taps in _taps_by_residue(lambda j: HALO - (CONV_KERNEL - 1) + j):
                span = 8 * max(a for _, a in taps) + tm
                sh_ref[0:span, :] = ext_ref[res:res + span, pl.ds(col, LANES)]
                for j, a in taps:
                    acc = acc + cw_ref[j:j + 1, pl.ds(col, LANES)] * sh_ref[8 * a:8 * a + tm, :]
            u1_ref[:, pl.ds(col, LANES)] = acc + cb_ref[:, pl.ds(col, LANES)]
            return carry

        lax.fori_loop(0, nchunk, chunk, 0)
        g = ng_ref[...]
        b = nb_ref[...]
        for r in range(0, tm, 16):
            u1 = u1_ref[r:r + 16, :]
            mu = jnp.mean(u1, axis=-1, keepdims=True)
            xc = u1 - mu
            var = jnp.mean(xc * xc, axis=-1, keepdims=True)
            n = xc * lax.rsqrt(var + LN_EPS) * g + b
            z = z_ref[r:r + 16, :]
            u2_ref[r:r + 16, :] = ((n * _sigmoid(n)) * (z * _sigmoid(z))).astype(BF16)

    row = lambda c: pl.BlockSpec((tm, e), lambda i: (i, c))
    halo = lambda c: pl.BlockSpec((HALO, e), lambda i: (jnp.maximum(i * hb - 1, 0), c))
    par = lambda r: pl.BlockSpec((r, e), lambda i: (0, 0))
    return pl.pallas_call(
        body, name="conv_mid_fwd",
        out_shape=(jax.ShapeDtypeStruct((t, e), F32), jax.ShapeDtypeStruct((t, e), BF16)),
        grid=(t // tm,),
        in_specs=[row(0), row(1), row(2), halo(0), halo(1), par(HALO), par(1), par(1), par(1)],
        out_specs=(pl.BlockSpec((tm, e), lambda i: (i, 0)), pl.BlockSpec((tm, e), lambda i: (i, 0))),
        scratch_shapes=[pltpu.VMEM((tm + HALO, e), F32), pltpu.VMEM((tm + HALO, LANES), F32)],
        compiler_params=_cparams(("parallel",)),
    )(proj, proj, proj, proj, proj, cw, cb, ng, nb)


def _ln_res_fwd(x, u, w_out, b_out, g, b):
    t, d = x.shape
    ku = u.shape[1]
    tm = min(t, 512)

    def body(x_ref, u_ref, w_ref, bo_ref, g_ref, b_ref, h_ref, hb_ref, xh_ref, rs_ref):
        y = lax.dot_general(u_ref[...], w_ref[...], NN, preferred_element_type=F32) + bo_ref[...]
        r = ALPHA * x_ref[...] + y
        mu = jnp.mean(r, axis=-1, keepdims=True)
        xc = r - mu
        var = jnp.mean(xc * xc, axis=-1, keepdims=True)
        rstd = lax.rsqrt(var + LN_EPS)
        xh = xc * rstd
        h = xh * g_ref[...] + b_ref[...]
        h_ref[...] = h
        hb_ref[...] = h.astype(BF16)
        xh_ref[...] = xh
        rs_ref[...] = rstd

    row = pl.BlockSpec((tm, d), lambda i: (i, 0))
    par = pl.BlockSpec((1, d), lambda i: (0, 0))
    return pl.pallas_call(
        body, name="ln0_fwd",
        out_shape=(jax.ShapeDtypeStruct((t, d), F32), jax.ShapeDtypeStruct((t, d), BF16),
                   jax.ShapeDtypeStruct((t, d), F32), jax.ShapeDtypeStruct((t, 1), F32)),
        grid=(t // tm,),
        in_specs=[row, pl.BlockSpec((tm, ku), lambda i: (i, 0)),
                  pl.BlockSpec((ku, d), lambda i: (0, 0), pipeline_mode=pl.Buffered(1)), par, par, par],
        out_specs=(row, row, row, pl.BlockSpec((tm, 1), lambda i: (i, 0))),
        compiler_params=_cparams(("parallel",)),
    )(x, u, w_out, b_out, g, b)


def _kv_mid_fwd(h1b, w_down, w_ukv, g, rc, rs):
    t, d = h1b.shape
    nkv = w_ukv.shape[1]
    tm = min(t, 512)
    r = KV_LORA_RANK

    def body(h_ref, w_ref, wu_ref, g_ref, c_ref, s_ref, ckv_ref, ckn_ref, kr_ref, kv_ref):
        ckv = lax.dot_general(h_ref[...], w_ref[...], NN, preferred_element_type=F32)
        ckv_ref[...] = ckv
        c = ckv[:, 0:r]
        ms = jnp.mean(c * c, axis=-1, keepdims=True)
        ckn = (c * lax.rsqrt(ms + RMS_EPS) * g_ref[...]).astype(BF16)
        ckn_ref[...] = ckn
        tr = ckv[:, r:r + LANES]
        kr_ref[...] = (tr * c_ref[...] + _swap_rope_halves(tr) * s_ref[...]).astype(BF16)
        kv_ref[...] = lax.dot_general(ckn, wu_ref[...], NN, preferred_element_type=F32).astype(BF16)

    lane_blk = pl.BlockSpec((tm, LANES), lambda i: (i, 0))
    return pl.pallas_call(
        body, name="kv_mid_fwd",
        out_shape=(jax.ShapeDtypeStruct((t, r + LANES), F32), jax.ShapeDtypeStruct((t, r), BF16),
                   jax.ShapeDtypeStruct((t, LANES), BF16), jax.ShapeDtypeStruct((t, nkv), BF16)),
        grid=(t // tm,),
        in_specs=[pl.BlockSpec((tm, d), lambda i: (i, 0)),
                  pl.BlockSpec((d, r + LANES), lambda i: (0, 0), pipeline_mode=pl.Buffered(1)),
                  pl.BlockSpec((r, nkv), lambda i: (0, 0), pipeline_mode=pl.Buffered(1)),
                  pl.BlockSpec((1, r), lambda i: (0, 0)), lane_blk, lane_blk],
        out_specs=(pl.BlockSpec((tm, r + LANES), lambda i: (i, 0)), pl.BlockSpec((tm, r), lambda i: (i, 0)), lane_blk,
                   pl.BlockSpec((tm, nkv), lambda i: (i, 0))),
        compiler_params=_cparams(("parallel",)),
    )(h1b, w_down, w_ukv, g, rc, rs)


def _b_in_qnorm(h1b, w_blocks, g):
    t, d = h1b.shape
    nblk, _, cb = w_blocks.shape
    n = nblk * cb
    r = Q_LORA_RANK
    tm = min(t, 512)

    def body(a_ref, wb_ref, g_ref, o_ref, cqn_ref, w_ref):
        @pl.when(pl.program_id(0) == 0)
        def _():
            for j in range(nblk):
                w_ref[:, j * cb:(j + 1) * cb] = wb_ref[j]

        acc = lax.dot_general(a_ref[...], w_ref[...], NN, preferred_element_type=F32)
        o_ref[...] = acc
        c = acc[:, 0:r]
        ms = jnp.mean(c * c, axis=-1, keepdims=True)
        cqn_ref[...] = (c * lax.rsqrt(ms + RMS_EPS) * g_ref[...]).astype(BF16)

    return pl.pallas_call(
        body, name="b_in",
        out_shape=(jax.ShapeDtypeStruct((t, n), F32), jax.ShapeDtypeStruct((t, r), BF16),
                   jax.ShapeDtypeStruct((d, n), BF16)),
        grid=(t // tm,),
        in_specs=[pl.BlockSpec((tm, d), lambda i: (i, 0)),
                  pl.BlockSpec((nblk, d, cb), lambda i: (0, 0, 0), pipeline_mode=pl.Buffered(1)),
                  pl.BlockSpec((1, r), lambda i: (0, 0))],
        out_specs=(pl.BlockSpec((tm, n), lambda i: (i, 0)), pl.BlockSpec((tm, r), lambda i: (i, 0)),
                   pl.BlockSpec((d, n), lambda i: (0, 0))),
        compiler_params=_cparams(("arbitrary",)),
    )(h1b, w_blocks, g)


def _q_up_rope(cqn, wuq, rc, rs):
    t, r = cqn.shape
    w = wuq.shape[1]
    tm = min(t, 1024)
    bn = 4 * HEAD_PAD

    def body(a_ref, b_ref, c_ref, s_ref, o_ref):
        q = lax.dot_general(a_ref[...], b_ref[...], NN, preferred_element_type=F32)
        c = c_ref[...]
        s = s_ref[...]
        for h in range(bn // HEAD_PAD):
            a = h * HEAD_PAD
            o_ref[:, a:a + LANES] = q[:, a:a + LANES].astype(BF16)
            tr = q[:, a + LANES:a + 2 * LANES]
            o_ref[:, a + LANES:a + 2 * LANES] = (tr * c + _swap_rope_halves(tr) * s).astype(BF16)

    return pl.pallas_call(
        body, name="q_up_rope",
        out_shape=jax.ShapeDtypeStruct((t, w), BF16),
        grid=(t // tm, w // bn),
        in_specs=[pl.BlockSpec((tm, r), lambda i, j: (i, 0)), pl.BlockSpec((r, bn), lambda i, j: (0, j)),
                  pl.BlockSpec((tm, LANES), lambda i, j: (i, 0)), pl.BlockSpec((tm, LANES), lambda i, j: (i, 0))],
        out_specs=pl.BlockSpec((tm, bn), lambda i, j: (i, j)),
        compiler_params=_cparams(("parallel", "parallel")),
    )(cqn, wuq, rc, rs)


def _flash_fwd(qcat, kv, kr, projb):
    t = qcat.shape[0]
    tq = min(t, 512)
    nq = t // tq
    exp2_scale = ATTN_SCALE * math.log2(math.e)
    z_first = Q_LORA_RANK // LANES

    def body(q_ref, kv_ref, kr_ref, z_ref, o_ref, o2_ref, lse_ref, kcat_ref):
        kcat_ref[:, 0:LANES] = kv_ref[:, 0:LANES]
        kcat_ref[:, LANES:] = kr_ref[...]
        rows = lax.broadcasted_iota(jnp.int32, (tq, tq), 0)
        cols = lax.broadcasted_iota(jnp.int32, (tq, tq), 1)
        for i in range(nq):
            a = i * tq
            q = q_ref[a:a + tq, :]
            sd = lax.dot_general(q, kcat_ref[a:a + tq, :], NT, preferred_element_type=F32)
            sd = jnp.where(cols <= rows, sd, MASK_VALUE)
            m = jnp.max(sd, axis=1, keepdims=True)
            if i > 0:
                sp = lax.dot_general(q, kcat_ref[0:a, :], NT, preferred_element_type=F32)
                m = jnp.maximum(m, jnp.max(sp, axis=1, keepdims=True))
            pd = jnp.exp2((sd - m) * exp2_scale)
            l = jnp.sum(pd, axis=1, keepdims=True)
            acc = lax.dot_general(pd.astype(BF16), kv_ref[a:a + tq, LANES:], NN, preferred_element_type=F32)
            if i > 0:
                pp = jnp.exp2((sp - m) * exp2_scale)
                l = l + jnp.sum(pp, axis=1, keepdims=True)
                acc = acc + lax.dot_general(pp.astype(BF16), kv_ref[0:a, LANES:], NN, preferred_element_type=F32)
            o = acc / l
            z = z_ref[a:a + tq, :]
            o_ref[a:a + tq, :] = o
            o2_ref[a:a + tq, :] = (o * (z * _sigmoid(z))).astype(BF16)
            lse_ref[a:a + tq, :] = jnp.broadcast_to(m * ATTN_SCALE + jnp.log(l), (tq, LANES))

    hw = N_HEADS * LANES
    head256 = pl.BlockSpec((t, HEAD_PAD), lambda h: (0, h))
    head128 = pl.BlockSpec((t, LANES), lambda h: (0, h))
    return pl.pallas_call(
        body, name="flash_fwd",
        out_shape=(jax.ShapeDtypeStruct((t, hw), F32), jax.ShapeDtypeStruct((t, hw), BF16),
                   jax.ShapeDtypeStruct((t, hw), F32)),
        grid=(N_HEADS,),
        in_specs=[head256, head256, pl.BlockSpec((t, LANES), lambda h: (0, 0), pipeline_mode=pl.Buffered(1)),
                  pl.BlockSpec((t, LANES), lambda h: (0, z_first + h))],
        out_specs=(head128, head128, head128),
        scratch_shapes=[pltpu.VMEM((t, HEAD_PAD), BF16)],
        compiler_params=_cparams(("parallel",)),
    )(qcat, kv, kr, projb)


def _final_ln_loss(h1, o2, w_out, g, b, target):
    t, d = h1.shape
    ku = o2.shape[1]
    tm = min(t, 512)

    def body(h_ref, u_ref, w_ref, g_ref, b_ref, t_ref, loss_ref, dr_ref, drb_ref, dg_ref, db_ref):
        i = pl.program_id(0)
        r = ALPHA * h_ref[...] + lax.dot_general(u_ref[...], w_ref[...], NN, preferred_element_type=F32)
        mu = jnp.mean(r, axis=-1, keepdims=True)
        xc = r - mu
        var = jnp.mean(xc * xc, axis=-1, keepdims=True)
        rstd = lax.rsqrt(var + LN_EPS)
        xh = xc * rstd
        g = g_ref[...]
        diff = xh * g + b_ref[...] - t_ref[...]
        part = 0.5 * jnp.sum(jnp.mean(diff * diff, axis=-1, keepdims=True))
        dh = diff * (1.0 / d)
        dgp = jnp.sum(dh * xh, axis=0, keepdims=True)
        dbp = jnp.sum(dh, axis=0, keepdims=True)

        @pl.when(i == 0)
        def _():
            loss_ref[...] = jnp.zeros_like(loss_ref)
            dg_ref[...] = jnp.zeros_like(dg_ref)
            db_ref[...] = jnp.zeros_like(db_ref)

        loss_ref[...] += jnp.full(loss_ref.shape, part, F32)
        dg_ref[...] += dgp
        db_ref[...] += dbp
        dxh = dh * g
        dr = rstd * (dxh - jnp.mean(dxh, axis=-1, keepdims=True) - xh * jnp.mean(dxh * xh, axis=-1, keepdims=True))
        dr_ref[...] = dr
        drb_ref[...] = dr.astype(BF16)

    row = pl.BlockSpec((tm, d), lambda i: (i, 0))
    par = pl.BlockSpec((1, d), lambda i: (0, 0))
    return pl.pallas_call(
        body, name="final_ln_loss",
        out_shape=(jax.ShapeDtypeStruct((1, LANES), F32), jax.ShapeDtypeStruct((t, d), F32),
                   jax.ShapeDtypeStruct((t, d), BF16), jax.ShapeDtypeStruct((1, d), F32),
                   jax.ShapeDtypeStruct((1, d), F32)),
        grid=(t // tm,),
        in_specs=[row, pl.BlockSpec((tm, ku), lambda i: (i, 0)),
                  pl.BlockSpec((ku, d), lambda i: (0, 0), pipeline_mode=pl.Buffered(1)), par, par, row],
        out_specs=(pl.BlockSpec((1, LANES), lambda i: (0, 0)), row, row, par, par),
        compiler_params=_cparams(("arbitrary",)),
    )(h1, o2, w_out, g, b, target)


def _gate_bwd(dr2b, w_out, o, projb):
    t, hw = o.shape
    d = dr2b.shape[1]
    tm = min(t, 1024)
    bw = Q_LORA_RANK
    nb = hw // bw
    wb = projb.shape[1]
    steps = (t // tm) * nb
    ring = 3
    assert steps >= ring

    def body(dr_ref, w_ref, o_hbm, p_hbm, dob_ref, dz_ref, obuf, zbuf, sems):
        s = pl.program_id(0) * nb + pl.program_id(1)

        def fetches(step):
            rows = pl.ds(pl.multiple_of((step // nb) * tm, tm), tm)
            col = (step % nb) * bw
            slot = step % ring
            return (pltpu.make_async_copy(o_hbm.at[rows, pl.ds(pl.multiple_of(col, bw), bw)], obuf.at[slot],
                                          sems.at[slot]),
                    pltpu.make_async_copy(p_hbm.at[rows, pl.ds(pl.multiple_of(col + bw, bw), bw)], zbuf.at[slot],
                                          sems.at[ring + slot]))

        @pl.when(s == 0)
        def _():
            for ahead in range(ring - 1):
                for cp in fetches(s + ahead):
                    cp.start()

        @pl.when(s + ring - 1 < steps)
        def _():
            for cp in fetches(s + ring - 1):
                cp.start()

        for cp in fetches(s):
            cp.wait()
        slot = s % ring
        z = zbuf[slot]
        sg = _sigmoid(z)
        w = w_ref[pl.ds(pl.multiple_of(pl.program_id(1) * bw, bw), bw), :]
        d2 = lax.dot_general(dr_ref[...], w, NT, preferred_element_type=F32)
        dob_ref[...] = (d2 * (z * sg)).astype(BF16)
        dz_ref[...] = (d2 * obuf[slot] * (sg * (1.0 + z * (1.0 - sg)))).astype(BF16)

    blk = pl.BlockSpec((tm, bw), lambda i, j: (i, j))
    blk1 = pl.BlockSpec((tm, bw), lambda i, j: (i, j + 1))
    hbm = pl.BlockSpec(memory_space=pl.ANY)
    return pl.pallas_call(
        body, name="gate_bwd",
        out_shape=(jax.ShapeDtypeStruct((t, hw), BF16), jax.ShapeDtypeStruct((t, wb), BF16)),
        grid=(t // tm, nb),
        in_specs=[pl.BlockSpec((tm, d), lambda i, j: (i, 0)),
                  pl.BlockSpec((hw, d), lambda i, j: (0, 0), pipeline_mode=pl.Buffered(1)), hbm, hbm],
        out_specs=(blk, blk1),
        scratch_shapes=[pltpu.VMEM((ring, tm, bw), F32), pltpu.VMEM((ring, tm, bw), F32),
                        pltpu.SemaphoreType.DMA((2 * ring,))],
        compiler_params=_cparams(("arbitrary", "arbitrary")),
    )(dr2b, w_out, o, projb)


def _flash_bwd(qcat, kv, kr, dob, lse, o, rc, rs):
    t = qcat.shape[0]
    tq = min(t, 512)
    nq = t // tq
    q_chunk = 2 * tq
    log2e = math.log2(math.e)
    exp2_scale = ATTN_SCALE * log2e
    once = pl.Buffered(1)

    def body(q_ref, kv_ref, kr_ref, do_ref, lse_ref, o_ref, c_ref, s_ref,
             dq_ref, dkv_ref, dkr_ref, kcat_ref, dqa_ref, dka_ref, dva_ref, dl_ref):
        h = pl.program_id(0)
        kcat_ref[:, 0:LANES] = kv_ref[:, 0:LANES]
        kcat_ref[:, LANES:] = kr_ref[...]
        dqa_ref[...] = jnp.zeros_like(dqa_ref)
        dl_ref[...] = jnp.sum(do_ref[...].astype(F32) * o_ref[...], axis=1, keepdims=True)

        @pl.when(h == 0)
        def _():
            dkr_ref[...] = jnp.zeros_like(dkr_ref)

        rows = lax.broadcasted_iota(jnp.int32, (tq, tq), 0)
        cols = lax.broadcasted_iota(jnp.int32, (tq, tq), 1)

        def block(qs, n, ks, masked):
            q = q_ref[qs:qs + n, :]
            kc = kcat_ref[ks:ks + tq, :]
            s = lax.dot_general(q, kc, NT, preferred_element_type=F32)
            if masked:
                s = jnp.where(cols <= rows, s, MASK_VALUE)
            p = jnp.exp2(s * exp2_scale - lse_ref[qs:qs + n, 0:1] * log2e)
            do = do_ref[qs:qs + n, :]
            dp = lax.dot_general(do, kv_ref[ks:ks + tq, LANES:], NT, preferred_element_type=F32)
            ds = (p * (dp - dl_ref[qs:qs + n, :])).astype(BF16)
            dva_ref[...] += lax.dot_general(p.astype(BF16), do, TN, preferred_element_type=F32)
            dka_ref[...] += lax.dot_general(ds, q, TN, preferred_element_type=F32)
            dqa_ref[qs:qs + n, :] += lax.dot_general(ds, kc, NN, preferred_element_type=F32)

        for kb in range(nq):
            a = kb * tq
            dka_ref[...] = jnp.zeros_like(dka_ref)
            dva_ref[...] = jnp.zeros_like(dva_ref)
            block(a, tq, a, True)
            qs = a + tq
            while qs < t:
                n = min(q_chunk, t - qs)
                block(qs, n, a, False)
                qs += n
            dk = dka_ref[...] * ATTN_SCALE
            dkv_ref[a:a + tq, 0:LANES] = dk[:, 0:LANES].astype(BF16)
            dkv_ref[a:a + tq, LANES:] = dva_ref[...].astype(BF16)
            dkr_ref[a:a + tq, :] += dk[:, LANES:]

        dq = dqa_ref[...] * ATTN_SCALE
        dq_ref[:, 0:LANES] = dq[:, 0:LANES].astype(BF16)
        d2 = dq[:, LANES:]
        dq_ref[:, LANES:] = (d2 * c_ref[...] - _swap_rope_halves(d2) * s_ref[...]).astype(BF16)

    hp = N_HEADS * HEAD_PAD
    head256 = pl.BlockSpec((t, HEAD_PAD), lambda h: (0, h))
    head128 = pl.BlockSpec((t, LANES), lambda h: (0, h))
    const128 = pl.BlockSpec((t, LANES), lambda h: (0, 0), pipeline_mode=once)
    return pl.pallas_call(
        body, name="flash_bwd",
        out_shape=(jax.ShapeDtypeStruct((t, hp), BF16), jax.ShapeDtypeStruct((t, hp), BF16),
                   jax.ShapeDtypeStruct((t, LANES), F32)),
        grid=(N_HEADS,),
        in_specs=[head256, head256, const128, head128, head128, head128, const128, const128],
        out_specs=(head256, head256, pl.BlockSpec((t, LANES), lambda h: (0, 0))),
        scratch_shapes=[pltpu.VMEM((t, HEAD_PAD), BF16), pltpu.VMEM((t, HEAD_PAD), F32),
                        pltpu.VMEM((tq, HEAD_PAD), F32), pltpu.VMEM((tq, LANES), F32), pltpu.VMEM((t, 1), F32)],
        compiler_params=_cparams(("arbitrary",)),
    )(qcat, kv, kr, dob, lse, o, rc, rs)


def _q_norm_bwd(dq2, w_uq, projb, g, dprojb):
    t = projb.shape[0]
    kq = dq2.shape[1]
    tm = min(t, 512)
    r = Q_LORA_RANK

    def body(dq_ref, w_ref, c_ref, g_ref, alias_ref, o_ref, dg_ref):
        i = pl.program_id(0)
        c = c_ref[...]
        d = lax.dot_general(dq_ref[...], w_ref[...], NT, preferred_element_type=F32)
        rr = lax.rsqrt(jnp.mean(c * c, axis=-1, keepdims=True) + RMS_EPS)
        xh = c * rr

        @pl.when(i == 0)
        def _():
            dg_ref[...] = jnp.zeros_like(dg_ref)

        dg_ref[...] += jnp.sum(d * xh, axis=0, keepdims=True)
        dxh = d * g_ref[...]
        o_ref[...] = (rr * (dxh - xh * jnp.mean(dxh * xh, axis=-1, keepdims=True))).astype(BF16)

    blk = pl.BlockSpec((tm, r), lambda i: (i, 0))
    par = pl.BlockSpec((1, r), lambda i: (0, 0))
    return pl.pallas_call(
        body, name="q_norm_bwd",
        out_shape=(jax.ShapeDtypeStruct(dprojb.shape, BF16), jax.ShapeDtypeStruct((1, r), F32)),
        grid=(t // tm,),
        in_specs=[pl.BlockSpec((tm, kq), lambda i: (i, 0)),
                  pl.BlockSpec((r, kq), lambda i: (0, 0), pipeline_mode=pl.Buffered(1)),
                  blk, par, pl.BlockSpec(memory_space=pl.ANY)],
        out_specs=(blk, par),
        input_output_aliases={4: 0},
        compiler_params=_cparams(("arbitrary",)),
    )(dq2, w_uq, projb, g, dprojb)


def _kv_mid_bwd(dkv, w_ukv, ckv, dkr, g, rc, rs):
    t = ckv.shape[0]
    kk = dkv.shape[1]
    tm = min(t, 512)
    r = KV_LORA_RANK

    def body(dkv_ref, w_ref, ckv_ref, dkr_ref, g_ref, c_ref, s_ref, o_ref, dg_ref):
        i = pl.program_id(0)
        c = ckv_ref[:, 0:r]
        d = lax.dot_general(dkv_ref[...], w_ref[...], NT, preferred_element_type=F32)
        rr = lax.rsqrt(jnp.mean(c * c, axis=-1, keepdims=True) + RMS_EPS)
        xh = c * rr

        @pl.when(i == 0)
        def _():
            dg_ref[...] = jnp.zeros_like(dg_ref)

        dg_ref[...] += jnp.sum(d * xh, axis=0, keepdims=True)
        dxh = d * g_ref[...]
        o_ref[:, 0:r] = (rr * (dxh - xh * jnp.mean(dxh * xh, axis=-1, keepdims=True))).astype(BF16)
        dk = dkr_ref[...]
        o_ref[:, r:] = (dk * c_ref[...] - _swap_rope_halves(dk) * s_ref[...]).astype(BF16)

    return pl.pallas_call(
        body, name="kv_mid_bwd",
        out_shape=(jax.ShapeDtypeStruct((t, r + LANES), BF16), jax.ShapeDtypeStruct((1, r), F32)),
        grid=(t // tm,),
        in_specs=[pl.BlockSpec((tm, kk), lambda i: (i, 0)),
                  pl.BlockSpec((r, kk), lambda i: (0, 0), pipeline_mode=pl.Buffered(1)),
                  pl.BlockSpec((tm, r + LANES), lambda i: (i, 0)),
                  pl.BlockSpec((tm, LANES), lambda i: (i, 0)), pl.BlockSpec((1, r), lambda i: (0, 0)),
                  pl.BlockSpec((tm, LANES), lambda i: (i, 0)), pl.BlockSpec((tm, LANES), lambda i: (i, 0))],
        out_specs=(pl.BlockSpec((tm, r + LANES), lambda i: (i, 0)), pl.BlockSpec((1, r), lambda i: (0, 0))),
        compiler_params=_cparams(("arbitrary",)),
    )(dkv, w_ukv, ckv, dkr, g, rc, rs)


def _ln0_bwd(dr2, dprojb, w_in, dckv, w_down, xh, rstd, g, after):
    t, d = dr2.shape
    kb, kd = dprojb.shape[1], dckv.shape[1]
    tm = min(t, 256)
    extra = [] if after is None else [after]

    def body(a_ref, pb_ref, wb_ref, pd_ref, wd_ref, xh_ref, rs_ref, g_ref, *rest):
        dr_ref, drb_ref, dg_ref, db_ref, dsum_ref = rest[-5:]
        i = pl.program_id(0)
        dh = (ALPHA * a_ref[...] + lax.dot_general(pb_ref[...], wb_ref[...], NT, preferred_element_type=F32)
              + lax.dot_general(pd_ref[...], wd_ref[...], NT, preferred_element_type=F32))
        xh = xh_ref[...]

        @pl.when(i == 0)
        def _():
            dg_ref[...] = jnp.zeros_like(dg_ref)
            db_ref[...] = jnp.zeros_like(db_ref)
            dsum_ref[...] = jnp.zeros_like(dsum_ref)

        dg_ref[...] += jnp.sum(dh * xh, axis=0, keepdims=True)
        db_ref[...] += jnp.sum(dh, axis=0, keepdims=True)
        dxh = dh * g_ref[...]
        dr = rs_ref[...] * (dxh - jnp.mean(dxh, axis=-1, keepdims=True) - xh * jnp.mean(dxh * xh, axis=-1, keepdims=True))
        dr_ref[...] = dr
        drb_ref[...] = dr.astype(BF16)
        dsum_ref[...] += jnp.sum(dr, axis=0, keepdims=True)

    row = pl.BlockSpec((tm, d), lambda i: (i, 0))
    par = pl.BlockSpec((1, d), lambda i: (0, 0))
    return pl.pallas_call(
        body, name="ln0_bwd",
        out_shape=(jax.ShapeDtypeStruct((t, d), F32), jax.ShapeDtypeStruct((t, d), BF16),
                   jax.ShapeDtypeStruct((1, d), F32), jax.ShapeDtypeStruct((1, d), F32),
                   jax.ShapeDtypeStruct((1, d), F32)),
        grid=(t // tm,),
        in_specs=[row, pl.BlockSpec((tm, kb), lambda i: (i, 0)),
                  pl.BlockSpec((d, kb), lambda i: (0, 0), pipeline_mode=pl.Buffered(1)),
                  pl.BlockSpec((tm, kd), lambda i: (i, 0)),
                  pl.BlockSpec((d, kd), lambda i: (0, 0), pipeline_mode=pl.Buffered(1)),
                  row, pl.BlockSpec((tm, 1), lambda i: (i, 0)), par]
                 + [pl.BlockSpec(memory_space=pl.ANY)] * len(extra),
        out_specs=(row, row, par, par, par),
        compiler_params=_cparams(("arbitrary",)),
    )(dr2, dprojb, w_in, dckv, w_down, xh, rstd, g, *extra)


def _conv_mid_bwd_rows(u1, proj, dr1b, w_out, ng, nb, after):
    t = u1.shape[0]
    e = CONV_WIDTH
    d = dr1b.shape[1]
    tm = min(t, 256)
    extra = [] if after is None else [after]

    rg = 16
    nt = t // tm

    def body(u1_ref, z_ref, dr_ref, w_ref, ng_ref, nb_ref, *rest):
        du1_ref, dz_ref, dng_ref, dnb_ref, dbz_ref, acc_ref, d_ref = rest[-7:]
        i = pl.program_id(0)

        @pl.when(i == 0)
        def _():
            acc_ref[...] = jnp.zeros_like(acc_ref)

        d_ref[...] = lax.dot_general(dr_ref[...], w_ref[...], NT, preferred_element_type=F32)
        g = ng_ref[...]
        b = nb_ref[...]

        def group(r, carry):
            rows = pl.ds(pl.multiple_of(r * rg, rg), rg)
            u1 = u1_ref[rows, :]
            mu = jnp.mean(u1, axis=-1, keepdims=True)
            xc = u1 - mu
            rstd = lax.rsqrt(jnp.mean(xc * xc, axis=-1, keepdims=True) + LN_EPS)
            xh = xc * rstd
            n = xh * g + b
            sn = _sigmoid(n)
            z = z_ref[rows, :]
            sz = _sigmoid(z)
            du2 = d_ref[rows, :]
            dz = du2 * (n * sn) * (sz * (1.0 + z * (1.0 - sz)))
            dn = du2 * (z * sz) * (sn * (1.0 + n * (1.0 - sn)))
            acc_ref[0:8, :] += _fold_rows(dn * xh)
            acc_ref[8:16, :] += _fold_rows(dn)
            acc_ref[16:24, :] += _fold_rows(dz)
            dz_ref[rows, :] = dz.astype(BF16)
            dxh = dn * g
            du1_ref[rows, :] = rstd * (dxh - jnp.mean(dxh, axis=-1, keepdims=True)
                                       - xh * jnp.mean(dxh * xh, axis=-1, keepdims=True))
            return carry

        lax.fori_loop(0, tm // rg, group, 0, unroll=8)

        @pl.when(i == nt - 1)
        def _():
            dng_ref[...] = jnp.sum(acc_ref[0:8, :], axis=0, keepdims=True)
            dnb_ref[...] = jnp.sum(acc_ref[8:16, :], axis=0, keepdims=True)
            dbz_ref[...] = jnp.sum(acc_ref[16:24, :], axis=0, keepdims=True)

    row = pl.BlockSpec((tm, e), lambda i: (i, 0))
    par = pl.BlockSpec((1, e), lambda i: (0, 0))
    return pl.pallas_call(
        body, name="conv_mid_bwd_rows",
        out_shape=(jax.ShapeDtypeStruct((t, e), F32), jax.ShapeDtypeStruct((t, 3 * e), BF16),
                   jax.ShapeDtypeStruct((1, e), F32), jax.ShapeDtypeStruct((1, e), F32),
                   jax.ShapeDtypeStruct((1, e), F32)),
        grid=(nt,),
        in_specs=[row, pl.BlockSpec((tm, e), lambda i: (i, 2)), pl.BlockSpec((tm, d), lambda i: (i, 0)),
                  pl.BlockSpec((e, d), lambda i: (0, 0), pipeline_mode=pl.Buffered(1)), par, par]
                 + [pl.BlockSpec(memory_space=pl.ANY)] * len(extra),
        out_specs=(row, pl.BlockSpec((tm, e), lambda i: (i, 2)), par, par, par),
        scratch_shapes=[pltpu.VMEM((24, e), F32), pltpu.VMEM((tm, e), F32)],
        compiler_params=_cparams(("arbitrary",)),
    )(u1, proj, dr1b, w_out, ng, nb, *extra)


def _conv_bwd(du1, proj, cw, dproj):
    t = du1.shape[0]
    e = CONV_WIDTH
    tm = min(t, 128)
    hb = tm // HALO
    nt = t // tm
    nchunk = e // LANES
    last_halo = t // HALO - 1

    def body(d_ref, dn_ref, val_ref, gg_ref, valh_ref, ggh_ref, cw_ref, alias_ref,
             dp_ref, dcw_ref, dcb_ref, dbv_ref, extu_ref, extd_ref, du0_ref, acc_ref, sh_ref):
        i = pl.program_id(0)
        h0 = valh_ref[...] * _sigmoid(ggh_ref[...])
        extu_ref[0:HALO, :] = jnp.where(i > 0, h0, 0.0)
        extu_ref[HALO:, :] = val_ref[...] * _sigmoid(gg_ref[...])
        extd_ref[0:tm, :] = d_ref[...]
        extd_ref[tm:, :] = jnp.where(i < nt - 1, dn_ref[...], 0.0)

        @pl.when(i == 0)
        def _():
            acc_ref[...] = jnp.zeros_like(acc_ref)

        def chunk(c, carry):
            col = pl.multiple_of(c * LANES, LANES)
            d = extd_ref[0:tm, pl.ds(col, LANES)]
            du0 = jnp.zeros((tm, LANES), F32)
            for res, taps in _taps_by_residue(lambda j: CONV_KERNEL - 1 - j):
                span = 8 * max(a for _, a in taps) + tm
                sh_ref[0:span, :] = extd_ref[res:res + span, pl.ds(col, LANES)]
                for j, a in taps:
                    du0 = du0 + cw_ref[j:j + 1, pl.ds(col, LANES)] * sh_ref[8 * a:8 * a + tm, :]
            for res, taps in _taps_by_residue(lambda j: HALO - (CONV_KERNEL - 1) + j):
                span = 8 * max(a for _, a in taps) + tm
                sh_ref[0:span, :] = extu_ref[res:res + span, pl.ds(col, LANES)]
                for j, a in taps:
                    prod = d * sh_ref[8 * a:8 * a + tm, :]
                    acc_ref[j * 8:(j + 1) * 8, pl.ds(col, LANES)] += _fold_rows(prod)
            acc_ref[CONV_KERNEL * 8:(CONV_KERNEL + 1) * 8, pl.ds(col, LANES)] += _fold_rows(d)
            du0_ref[:, pl.ds(col, LANES)] = du0
            return carry

        lax.fori_loop(0, nchunk, chunk, 0)
        kb = CONV_KERNEL * 8
        for r in range(0, tm, 16):
            sg = _sigmoid(gg_ref[r:r + 16, :])
            dval = du0_ref[r:r + 16, :] * sg
            dgg = dval * val_ref[r:r + 16, :] * (1.0 - sg)
            dp_ref[r:r + 16, 0:e] = dval.astype(BF16)
            dp_ref[r:r + 16, e:] = dgg.astype(BF16)
            acc_ref[kb + 8:kb + 16, :] += _fold_rows(dval)
            acc_ref[kb + 16:kb + 24, :] += _fold_rows(dgg)

        @pl.when(i == nt - 1)
        def _():
            for j in range(CONV_KERNEL):
                dcw_ref[j:j + 1, :] = jnp.sum(acc_ref[j * 8:(j + 1) * 8, :], axis=0, keepdims=True)
            dcw_ref[CONV_KERNEL:, :] = jnp.zeros((HALO - CONV_KERNEL, e), F32)
            dcb_ref[...] = jnp.sum(acc_ref[kb:kb + 8, :], axis=0, keepdims=True)
            dbv_ref[:, 0:e] = jnp.sum(acc_ref[kb + 8:kb + 16, :], axis=0, keepdims=True)
            dbv_ref[:, e:] = jnp.sum(acc_ref[kb + 16:kb + 24, :], axis=0, keepdims=True)

    row = lambda c: pl.BlockSpec((tm, e), lambda i: (i, c))
    halo_prev = lambda c: pl.BlockSpec((HALO, e), lambda i: (jnp.maximum(i * hb - 1, 0), c))
    halo_next = pl.BlockSpec((HALO, e), lambda i: (jnp.minimum((i + 1) * hb, last_halo), 0))
    return pl.pallas_call(
        body, name="conv_bwd",
        out_shape=(jax.ShapeDtypeStruct(dproj.shape, BF16), jax.ShapeDtypeStruct((HALO, e), F32),
                   jax.ShapeDtypeStruct((1, e), F32), jax.ShapeDtypeStruct((1, 2 * e), F32)),
        grid=(nt,),
        in_specs=[row(0), halo_next, row(0), row(1), halo_prev(0), halo_prev(1),
                  pl.BlockSpec((HALO, e), lambda i: (0, 0)), pl.BlockSpec(memory_space=pl.ANY)],
        out_specs=(pl.BlockSpec((tm, 2 * e), lambda i: (i, 0)), pl.BlockSpec((HALO, e), lambda i: (0, 0)),
                   pl.BlockSpec((1, e), lambda i: (0, 0)), pl.BlockSpec((1, 2 * e), lambda i: (0, 0))),
        scratch_shapes=[pltpu.VMEM((tm + HALO, e), F32), pltpu.VMEM((tm + HALO, e), F32),
                        pltpu.VMEM((tm, e), F32), pltpu.VMEM(((CONV_KERNEL + 3) * 8, e), F32),
                        pltpu.VMEM((tm + HALO, LANES), F32)],
        input_output_aliases={7: 0},
        compiler_params=_cparams(("arbitrary",)),
    )(du1, du1, proj, proj, proj, proj, cw, dproj)


def _adam_update(g, w, m, v):
    c1 = 1.0 - ADAM_B1 ** ADAM_STEP
    c2 = 1.0 - ADAM_B2 ** ADAM_STEP
    mn = ADAM_B1 * m + (1.0 - ADAM_B1) * g
    vn = ADAM_B2 * v + (1.0 - ADAM_B2) * (g * g)
    delta = -ADAM_LR * ((mn / c1) / (jnp.sqrt(vn / c2) + ADAM_EPS) + ADAM_WD * w)
    return delta, mn, vn


def _adamw(name, gbuf, w, m, v):
    parts = list(gbuf) if isinstance(gbuf, (list, tuple)) else [gbuf]
    npart = len(parts)
    r, c = w.shape
    tr = min(r // npart, 256)
    per_part = r // npart // tr
    assert r == npart * per_part * tr and all(p.shape == (N_DEV, r // npart, c) for p in parts)

    def body(*refs):
        g_refs = refs[:npart]
        w_ref, m_ref, v_ref, go_ref, d_ref, mo_ref, vo_ref = refs[npart:]
        i = pl.program_id(0)

        def run(g_ref):
            g = g_ref[0].astype(F32)
            for k in range(1, N_DEV):
                g = g + g_ref[k].astype(F32)
            go_ref[...] = g
            d_ref[...], mo_ref[...], vo_ref[...] = _adam_update(g, w_ref[...], m_ref[...], v_ref[...])

        if npart == 1:
            run(g_refs[0])
        else:
            for q in range(npart):
                pl.when((i >= q * per_part) & (i < (q + 1) * per_part))(functools.partial(run, g_refs[q]))

    blk = pl.BlockSpec((tr, c), lambda i: (i, 0))
    part_spec = lambda q: pl.BlockSpec((N_DEV, tr, c), lambda i: (0, jnp.clip(i - q * per_part, 0, per_part - 1), 0))
    shp = jax.ShapeDtypeStruct((r, c), F32)
    return pl.pallas_call(
        body, name="adamw_" + name,
        out_shape=(shp, shp, shp, shp),
        grid=(r // tr,),
        in_specs=[part_spec(q) for q in range(npart)] + [blk, blk, blk],
        out_specs=(blk, blk, blk, blk),
        compiler_params=_cparams(("parallel",)),
    )(*parts, w, m, v)


SMALL_TABLE_COLS = 256
SMALL_LAYOUT = {
    'ln_g': (0, 2, 1024), 'ln_b': (8, 2, 1024), 'a_b_in': (16, 1, 768), 'a_conv_b': (24, 1, 256),
    'a_norm_g': (32, 1, 256), 'a_norm_b': (40, 1, 256), 'a_b_out': (48, 1, 128), 'kv_norm_g': (56, 1, 256),
    'b_q_norm_g': (64, 1, 512), 'a_conv_w': (72, CONV_KERNEL, 256),
}
LOSS_ROW = 104
SMALL_TABLE_ROWS = 112
SMALL_NAMES = tuple(SMALL_LAYOUT)


def _small_pieces(name, row):
    r0, _, c = SMALL_LAYOUT[name]
    w = min(c, SMALL_TABLE_COLS)
    per_row = c // w
    return [(r0 + row * per_row + q, q * w, w) for q in range(per_row)]


def _pack_small_grads(dg0, dg1, db0, db1, dbv, dbz, dcb, dng, dnb, dba, dgkv, dgq, dcw, loss):
    e = CONV_WIDTH
    ce = e // N_DEV

    def body(dg0_ref, dg1_ref, db0_ref, db1_ref, dbv_ref, dbz_ref, dcb_ref, dng_ref, dnb_ref, dba_ref, dgkv_ref,
             dgq_ref, dcw_ref, loss_ref, o_ref):
        o_ref[...] = jnp.zeros_like(o_ref)
        for d in range(N_DEV):
            o_ref[d, LOSS_ROW:LOSS_ROW + 1, 0:LANES] = loss_ref[...]

        def put(d, name, row, src_ref, first_col=0):
            for trow, col, w in _small_pieces(name, row):
                o_ref[d, trow:trow + 1, 0:w] = src_ref[:, first_col + col:first_col + col + w]

        for d in range(N_DEV):
            put(d, 'ln_g', 0, dg0_ref)
            put(d, 'ln_g', 1, dg1_ref)
            put(d, 'ln_b', 0, db0_ref)
            put(d, 'ln_b', 1, db1_ref)
            for trow, col, w in _small_pieces('a_b_in', 0):
                gcol = d * 3 * ce + col
                src = dbv_ref[:, gcol:gcol + w] if gcol < 2 * e else dbz_ref[:, gcol - 2 * e:gcol - 2 * e + w]
                o_ref[d, trow:trow + 1, 0:w] = src
            put(d, 'a_conv_b', 0, dcb_ref, d * ce)
            put(d, 'a_norm_g', 0, dng_ref, d * ce)
            put(d, 'a_norm_b', 0, dnb_ref, d * ce)
            put(d, 'a_b_out', 0, dba_ref, d * LANES)
            put(d, 'kv_norm_g', 0, dgkv_ref)
            put(d, 'b_q_norm_g', 0, dgq_ref)
            r0 = SMALL_LAYOUT['a_conv_w'][0]
            o_ref[d, r0:r0 + HALO, 0:ce] = dcw_ref[:, d * ce:(d + 1) * ce]

    return pl.pallas_call(
        body, name="pack_small_grads",
        out_shape=jax.ShapeDtypeStruct((N_DEV, SMALL_TABLE_ROWS, SMALL_TABLE_COLS), F32),
        compiler_params=pltpu.CompilerParams(vmem_limit_bytes=VMEM_LIMIT),
    )(dg0, dg1, db0, db1, dbv, dbz, dcb, dng, dnb, dba, dgkv, dgq, dcw, loss)


def _adamw_small(gtab, ws, ms, vs):
    n = len(SMALL_NAMES)

    def body(*refs):
        g_ref = refs[0]
        w_refs, m_refs, v_refs = refs[1:1 + n], refs[1 + n:1 + 2 * n], refs[1 + 2 * n:1 + 3 * n]
        outs = refs[1 + 3 * n:]

        def summed(r0, r, c):
            g = g_ref[0, r0:r0 + r, 0:c]
            for k in range(1, N_DEV):
                g = g + g_ref[k, r0:r0 + r, 0:c]
            return g

        for i, name in enumerate(SMALL_NAMES):
            r0, r, c = SMALL_LAYOUT[name]
            if c <= SMALL_TABLE_COLS:
                blocks = [(slice(None), summed(r0, r, c))]
            else:
                blocks = [(slice(row, row + 1), jnp.concatenate([summed(tr, 1, w) for tr, _, w in _small_pieces(name, row)], axis=1))
                          for row in range(r)]
            for rows, g in blocks:
                delta, mn, vn = _adam_update(g, w_refs[i][rows, :], m_refs[i][rows, :], v_refs[i][rows, :])
                outs[i][rows, :] = g
                outs[n + i][rows, :] = delta
                outs[2 * n + i][rows, :] = mn
                outs[3 * n + i][rows, :] = vn
        outs[4 * n][...] = summed(LOSS_ROW, 1, LANES)

    shapes = tuple(jax.ShapeDtypeStruct(w.shape, F32) for w in ws)
    res = pl.pallas_call(
        body, name="adamw_small",
        out_shape=shapes * 4 + (jax.ShapeDtypeStruct((1, LANES), F32),),
        compiler_params=pltpu.CompilerParams(vmem_limit_bytes=VMEM_LIMIT),
    )(gtab, *ws, *ms, *vs)
    return res[:n], res[n:2 * n], res[2 * n:3 * n], res[3 * n:4 * n], res[4 * n]


def _mesh_peers():
    x, y, c = lax.axis_index("x"), lax.axis_index("y"), lax.axis_index("c")
    me = 4 * x + 2 * y + c
    peers = []
    for k in range(1, N_DEV):
        px = 1 - x if (k >> 2) & 1 else x
        py = 1 - y if (k >> 1) & 1 else y
        pc = 1 - c if k & 1 else c
        peers.append(((px, py, pc), 4 * px + 2 * py + pc))
    return me, peers


_HBM_SPEC = pl.BlockSpec(memory_space=pltpu.HBM)
_SEM_SPEC = pl.BlockSpec(memory_space=pltpu.SEMAPHORE)


ALL_PEERS = tuple(range(N_DEV - 1))
CHIP_LEVEL_PEERS = (0, 1, 3, 5)


def _split_copies(srcs, lands, send_sems, recv_sems, local_sems, gather, which=ALL_PEERS):
    me, peers = _mesh_peers()
    na = len(srcs)
    mine = lambda a, slot: srcs[a] if gather else srcs[a].at[slot]
    local = [pltpu.make_async_copy(mine(a, me), lands[a].at[me], local_sems.at[a]) for a in range(na)]
    sent, received = [], []
    for k, (dev, pid) in enumerate(peers):
        if k not in which:
            continue
        for a in range(na):
            s = a * (N_DEV - 1) + k
            sent.append(pltpu.make_async_remote_copy(
                src_ref=mine(a, pid), dst_ref=lands[a].at[me], send_sem=send_sems.at[s],
                recv_sem=recv_sems.at[s], device_id=dev, device_id_type=pl.DeviceIdType.MESH))
            received.append(pltpu.make_async_remote_copy(
                src_ref=mine(a, pid), dst_ref=lands[a].at[pid], send_sem=send_sems.at[s],
                recv_sem=recv_sems.at[s], device_id=dev, device_id_type=pl.DeviceIdType.MESH))
    return local, sent, received


def _comm_start(name, srcs, gather, after, which=ALL_PEERS):
    na = len(srcs)
    land_structs = [jax.ShapeDtypeStruct(((N_DEV,) + s.shape) if gather else s.shape, s.dtype) for s in srcs]
    extra = [] if after is None else [after]
    n_in = 2 * na + len(extra)

    def body(*refs):
        srcs_r, lands_r = refs[:na], refs[na:2 * na]
        send_sems, recv_sems, local_sems = refs[n_in:n_in + 3]
        token = refs[-1]
        local, sent, _ = _split_copies(srcs_r, lands_r, send_sems, recv_sems, local_sems, gather, which)
        for cp in local + sent:
            cp.start()
        token[...] = jnp.zeros_like(token)

    sem = pltpu.SemaphoreType.DMA((na * (N_DEV - 1),))
    outs = pl.pallas_call(
        body, name=name,
        out_shape=(sem, sem, pltpu.SemaphoreType.DMA((na,)),
                   *[pltpu.HBM(s.shape, s.dtype) for s in srcs],
                   *[pltpu.HBM(s.shape, s.dtype) for s in land_structs],
                   jax.ShapeDtypeStruct((8, LANES), F32)),
        in_specs=[_HBM_SPEC] * (2 * na) + [pl.BlockSpec(memory_space=pl.ANY)] * len(extra),
        out_specs=(_SEM_SPEC, _SEM_SPEC, _SEM_SPEC, *([_HBM_SPEC] * (2 * na)), pl.BlockSpec(memory_space=pltpu.VMEM)),
        input_output_aliases={i: 3 + i for i in range(2 * na)},
        compiler_params=pltpu.CompilerParams(has_side_effects=pltpu.SideEffectType.DATAFLOW_SIDE_EFFECTING),
    )(*[pltpu.with_memory_space_constraint(s, pltpu.HBM) for s in srcs],
      *[pltpu.with_memory_space_constraint(lax.empty(s.shape, s.dtype), pltpu.HBM) for s in land_structs],
      *extra)
    return (outs[:3], outs[3:3 + na], outs[3 + na:3 + 2 * na]), outs[-1]


def _comm_wait(name, handles, gather, after, which=ALL_PEERS):
    (send_sems, recv_sems, local_sems), srcs, lands = handles
    na = len(srcs)

    def body(*refs):
        srcs_r, lands_r = refs[:na], refs[na:2 * na]
        send_r, recv_r, local_r = refs[2 * na:2 * na + 3]
        local, sent, received = _split_copies(srcs_r, lands_r, send_r, recv_r, local_r, gather, which)
        for cp in sent:
            cp.wait_send()
        for cp in received:
            cp.wait_recv()
        for cp in local:
            cp.wait()

    outs = pl.pallas_call(
        body, name=name,
        out_shape=(*[pltpu.HBM(s.shape, s.dtype) for s in srcs], *[pltpu.HBM(s.shape, s.dtype) for s in lands]),
        in_specs=[_HBM_SPEC] * (2 * na) + [_SEM_SPEC] * 3 + [pl.BlockSpec(memory_space=pl.ANY)],
        out_specs=tuple([_HBM_SPEC] * (2 * na)),
        input_output_aliases={i: i for i in range(2 * na)},
        compiler_params=pltpu.CompilerParams(has_side_effects=pltpu.SideEffectType.DATAFLOW_SIDE_EFFECTING),
    )(*srcs, *lands, send_sems, recv_sems, local_sems, after)
    return outs[na:]


def _forward_to_sibling(name, lands):
    na = len(lands)

    def body(*refs):
        ins, outs = refs[:na], refs[na:2 * na]
        send_sems, recv_sems = refs[2 * na:]
        x, y, c = lax.axis_index("x"), lax.axis_index("y"), lax.axis_index("c")
        sibling = (x, y, 1 - c)
        chips = [(1 - x, y), (x, 1 - y), (1 - x, 1 - y)]
        slot = lambda cx, cy, cc: 4 * cx + 2 * cy + cc
        sends = []
        for j, (cx, cy) in enumerate(chips):
            for a in range(na):
                cp = pltpu.make_async_remote_copy(
                    src_ref=ins[a].at[slot(cx, cy, c)], dst_ref=outs[a].at[slot(cx, cy, c)],
                    send_sem=send_sems.at[a, j], recv_sem=recv_sems.at[a, j], device_id=sibling,
                    device_id_type=pl.DeviceIdType.MESH)
                cp.start()
                sends.append(cp)
        for j, (cx, cy) in enumerate(chips):
            for a in range(na):
                pltpu.make_async_remote_copy(
                    src_ref=ins[a].at[slot(cx, cy, 1 - c)], dst_ref=outs[a].at[slot(cx, cy, 1 - c)],
                    send_sem=send_sems.at[a, j], recv_sem=recv_sems.at[a, j], device_id=sibling,
                    device_id_type=pl.DeviceIdType.MESH).wait_recv()
        for cp in sends:
            cp.wait_send()

    hbm = pl.BlockSpec(memory_space=pl.ANY)
    return pl.pallas_call(
        body, name=name,
        out_shape=tuple(jax.ShapeDtypeStruct(s.shape, s.dtype) for s in lands),
        in_specs=[hbm] * na, out_specs=(hbm,) * na,
        input_output_aliases={a: a for a in range(na)},
        scratch_shapes=[pltpu.SemaphoreType.DMA((na, 3)), pltpu.SemaphoreType.DMA((na, 3))],
    )(*lands)


def _prep_weights(w):
    e = CONV_WIDTH
    p = {}
    if 'a_w_in' in w:
        if w['a_w_in'].shape == (N_DEV, D_MODEL, 3 * e // N_DEV):
            p['a_w_in3'] = w['a_w_in']
        else:
            p['a_w_in3'] = w['a_w_in'].reshape(D_MODEL, N_DEV, 3 * e // N_DEV).transpose(1, 0, 2)
        p['a_b_in'] = w['a_b_in'].reshape(1, 3 * e)
        p['a_conv_w'] = jnp.pad(w['a_conv_w'].reshape(CONV_KERNEL, e), ((0, HALO - CONV_KERNEL), (0, 0)))
        p['a_conv_b'] = w['a_conv_b'].reshape(1, e)
        p['a_norm_g'] = w['a_norm_g'].reshape(1, e)
        p['a_norm_b'] = w['a_norm_b'].reshape(1, e)
        p['a_b_out'] = w['a_b_out'].reshape(1, D_MODEL)
        p['kv_norm_g'] = w['kv_norm_g'].reshape(1, KV_LORA_RANK)
        p['b_q_norm_g'] = w['b_q_norm_g'].reshape(1, Q_LORA_RANK)
        p['ln_g'] = w['ln_g']
        p['ln_b'] = w['ln_b']
    if 'a_w_out' in w:
        p['a_w_out'] = w['a_w_out'].reshape(e, D_MODEL)
        p['kv_w_down'] = jnp.pad(w['kv_w_down'], ((0, 0), (0, KV_LORA_RANK + LANES - w['kv_w_down'].shape[1])))
        p['kv_w_ukv'] = jnp.concatenate([w['kv_w_uk'], w['kv_w_uv']], axis=2).reshape(KV_LORA_RANK, N_HEADS * HEAD_PAD)
        if w['b_w_in'].ndim == 3 and w['b_w_in'].shape[:2] == (N_DEV, D_MODEL):
            p['b_w_in3'] = w['b_w_in']
        else:
            p['b_w_in3'] = w['b_w_in'].reshape(D_MODEL, N_DEV, -1).transpose(1, 0, 2)
        uq = w['b_w_uq'].reshape(Q_LORA_RANK, N_HEADS, QK_NOPE_DIM + QK_ROPE_DIM)
        p['b_w_uq'] = jnp.pad(uq, ((0, 0), (0, 0), (0, HEAD_PAD - uq.shape[2]))).reshape(Q_LORA_RANK, N_HEADS * HEAD_PAD)
        p['b_w_out'] = w['b_w_out'].reshape(N_HEADS * V_HEAD_DIM, D_MODEL)
    return p


def _local_step(x, positions, target, comm):
    t = x.shape[0]
    e = CONV_WIDTH
    freqs = ROPE_THETA ** (-jnp.arange(0, QK_ROPE_DIM, 2, dtype=F32) / QK_ROPE_DIM)
    freq_row = jnp.concatenate([freqs, freqs, jnp.zeros((LANES - QK_ROPE_DIM,), F32)]).reshape(1, LANES)
    rc, rs, xb = _rope_tables(positions.reshape(t, 1), freq_row, x, comm.start_token())
    p = comm.first_weights(rc)
    ln_g0, ln_b0 = p['ln_g'][0:1], p['ln_b'][0:1]
    ln_g1, ln_b1 = p['ln_g'][1:2], p['ln_b'][1:2]

    proj = _matmul(xb, p['a_w_in3'], bias=p['a_b_in'], name="a_in", b_blocked=True, bm=2048,
                   after=comm.first_after())
    u1, u2 = _conv_mid_fwd(proj, p['a_conv_w'], p['a_conv_b'], p['a_norm_g'], p['a_norm_b'])
    p = {**p, **comm.rest_weights(u2)}
    h1, h1b, xh1, rstd1 = _ln_res_fwd(x, u2, p['a_w_out'], p['a_b_out'], ln_g0, ln_b0)
    ckv, ckn, kr, kv = _kv_mid_fwd(h1b, p['kv_w_down'], p['kv_w_ukv'], p['kv_norm_g'], rc, rs)
    projb, cqn, b_w_in = _b_in_qnorm(h1b, p['b_w_in3'], p['b_q_norm_g'])
    qcat = _q_up_rope(cqn, p['b_w_uq'], rc, rs)
    o, o2, lse = _flash_fwd(qcat, kv, kr, projb)
    loss, dr2, dr2b, dg1, db1 = _final_ln_loss(h1, o2, p['b_w_out'], ln_g1, ln_b1, target)

    g = {}
    g['b_w_out'] = _matmul(o2, dr2b, trans_a=True, name="d_b_out", out_dtype=BF16, bm=512, bk=t)
    dob, dprojb = _gate_bwd(dr2b, p['b_w_out'], o, projb)
    dq2, dkv, dkr = _flash_bwd(qcat, kv, kr, dob, lse, o, rc, rs)
    duq = _matmul(cqn, dq2, trans_a=True, name="d_q_up", out_dtype=BF16, bk=t)
    dprojb, dgq = _q_norm_bwd(dq2, p['b_w_uq'], projb, p['b_q_norm_g'], dprojb)
    g['b_w_in'] = _matmul(h1b, dprojb, trans_a=True, name="d_b_in", bn=2 * dprojb.shape[1] // N_DEV, bk=t, out_dtype=BF16,
                          out_blocked=True, out_parts=2, b_ring=3)
    dukv = _matmul(ckn, dkv, trans_a=True, name="d_kv_up", out_dtype=BF16, bk=t)
    dckv, dgkv = _kv_mid_bwd(dkv, p['kv_w_ukv'], ckv, dkr, p['kv_norm_g'], rc, rs)
    ddown = _matmul(h1b, dckv, trans_a=True, name="d_kv_down", out_dtype=BF16, bm=512, bk=t)
    g['b_w_out'] = g['b_w_out'].reshape(N_DEV, -1, D_MODEL)
    g['kv_w_down'] = ddown[:, :KV_LORA_RANK + QK_ROPE_DIM].reshape(N_DEV, -1, KV_LORA_RANK + QK_ROPE_DIM)
    dukv = dukv.reshape(KV_LORA_RANK, N_HEADS, HEAD_PAD)
    g['kv_w_uk'] = dukv[:, :, :QK_NOPE_DIM].reshape(N_DEV, -1, QK_NOPE_DIM)
    g['kv_w_uv'] = dukv[:, :, QK_NOPE_DIM:].reshape(N_DEV, -1, V_HEAD_DIM)
    g['b_w_uq'] = duq.reshape(Q_LORA_RANK, N_HEADS, HEAD_PAD)[:, :, :QK_NOPE_DIM + QK_ROPE_DIM].reshape(
        N_DEV, -1, QK_NOPE_DIM + QK_ROPE_DIM)
    sent1 = comm.send_group1(g)
    dr1, dr1b, dg0, db0, dba_out = _ln0_bwd(dr2, dprojb, b_w_in, dckv, p['kv_w_down'], xh1, rstd1, ln_g0, sent1)
    dw_out = _matmul(u2, dr1b, trans_a=True, name="d_a_out", out_dtype=BF16, bm=512, bk=t).reshape(N_DEV, -1, D_MODEL)
    sent1b = comm.send_group1b({'a_w_out': dw_out})
    du1, dproj, dng, dnb, dbz = _conv_mid_bwd_rows(u1, proj, dr1b, p['a_w_out'], p['a_norm_g'], p['a_norm_b'], sent1b)
    dproj, dcw, dcb, dbv = _conv_bwd(du1, proj, p['a_conv_w'], dproj)
    half = D_MODEL // 2
    dw_lo = _matmul(xb, dproj, trans_a=True, name="d_a_in_lo", out_dtype=BF16, bm=half, bn=3 * e // N_DEV, bk=t,
                    out_blocked=True, a_cols=(0, half), b_ring=3)
    small = (dg0, dg1, db0, db1, dbv, dbz, dcb, dng, dnb, dba_out, dgkv, dgq, dcw, loss)
    sent2 = comm.send_group2(dw_lo, small)
    dw_hi = _matmul(xb, dproj, trans_a=True, name="d_a_in_hi", out_dtype=BF16, bm=half, bn=3 * e // N_DEV, bk=t,
                    out_blocked=True, after=sent2, a_cols=(half, half), b_ring=3)
    sent3 = comm.send_group3(dw_hi)
    grad_x = _grad_x(dproj, p['a_w_in3'], dr1, sent3)
    return loss, grad_x


def _grad_x(dproj, w3, dr1, after):
    t, d = dr1.shape
    nblk, _, cb = w3.shape
    tm = min(t, 512)
    extra = [] if after is None else [after]

    def body(a_ref, w_ref, dr_ref, *rest):
        o_ref = rest[-1]
        acc = ALPHA * dr_ref[...]
        for k in range(nblk):
            acc = acc + lax.dot_general(a_ref[:, k * cb:(k + 1) * cb], w_ref[k], NT, preferred_element_type=F32)
        o_ref[...] = acc

    row = pl.BlockSpec((tm, d), lambda i: (i, 0))
    return pl.pallas_call(
        body, name="grad_x",
        out_shape=jax.ShapeDtypeStruct((t, d), F32),
        grid=(t // tm,),
        in_specs=[pl.BlockSpec((tm, nblk * cb), lambda i: (i, 0)),
                  pl.BlockSpec((nblk, d, cb), lambda i: (0, 0, 0), pipeline_mode=pl.Buffered(1)), row]
                 + [pl.BlockSpec(memory_space=pl.ANY)] * len(extra),
        out_specs=row,
        compiler_params=_cparams(("parallel",)),
    )(dproj, w3, dr1, *extra)


def _shard_2d(a):
    return a.reshape(-1, a.shape[-1])


def _gathered_to_full(name, blocks):
    if name in ('a_w_in', 'b_w_in'):
        return blocks
    if name == 'a_conv_w':
        return blocks.transpose(1, 0, 2).reshape(CONV_KERNEL, -1)
    if name in ('a_b_in', 'a_conv_b', 'a_norm_g', 'a_norm_b', 'a_b_out'):
        return blocks.reshape(-1)
    if name in ('kv_w_uk', 'kv_w_uv'):
        return blocks.reshape(KV_LORA_RANK, N_HEADS, -1)
    if name == 'b_w_uq':
        return blocks.reshape(Q_LORA_RANK, N_HEADS, -1)
    return blocks.reshape(-1, blocks.shape[-1])


def kernel(x, positions, ln_g, ln_b, a_w_in, a_b_in, a_conv_w, a_conv_b, a_norm_g, a_norm_b, a_w_out, a_b_out, kv_w_down, kv_norm_g, kv_w_uk, kv_w_uv, b_w_in, b_q_norm_g, b_w_uq, b_w_out, loss_target, m_ln_g, m_ln_b, m_a_w_in, m_a_b_in, m_a_conv_w, m_a_conv_b, m_a_norm_g, m_a_norm_b, m_a_w_out, m_a_b_out, m_kv_w_down, m_kv_norm_g, m_kv_w_uk, m_kv_w_uv, m_b_w_in, m_b_q_norm_g, m_b_w_uq, m_b_w_out, v_ln_g, v_ln_b, v_a_w_in, v_a_b_in, v_a_conv_w, v_a_conv_b, v_a_norm_g, v_a_norm_b, v_a_w_out, v_a_b_out, v_kv_w_down, v_kv_norm_g, v_kv_w_uk, v_kv_w_uv, v_b_w_in, v_b_q_norm_g, v_b_w_uq, v_b_w_out):
    w = dict(ln_g=ln_g, ln_b=ln_b, a_w_in=a_w_in, a_b_in=a_b_in, a_conv_w=a_conv_w, a_conv_b=a_conv_b,
             a_norm_g=a_norm_g, a_norm_b=a_norm_b, a_w_out=a_w_out, a_b_out=a_b_out, kv_w_down=kv_w_down,
             kv_norm_g=kv_norm_g, kv_w_uk=kv_w_uk, kv_w_uv=kv_w_uv, b_w_in=b_w_in, b_q_norm_g=b_q_norm_g,
             b_w_uq=b_w_uq, b_w_out=b_w_out)
    m = dict(ln_g=m_ln_g, ln_b=m_ln_b, a_w_in=m_a_w_in, a_b_in=m_a_b_in, a_conv_w=m_a_conv_w, a_conv_b=m_a_conv_b,
             a_norm_g=m_a_norm_g, a_norm_b=m_a_norm_b, a_w_out=m_a_w_out, a_b_out=m_a_b_out, kv_w_down=m_kv_w_down,
             kv_norm_g=m_kv_norm_g, kv_w_uk=m_kv_w_uk, kv_w_uv=m_kv_w_uv, b_w_in=m_b_w_in, b_q_norm_g=m_b_q_norm_g,
             b_w_uq=m_b_w_uq, b_w_out=m_b_w_out)
    v = dict(ln_g=v_ln_g, ln_b=v_ln_b, a_w_in=v_a_w_in, a_b_in=v_a_b_in, a_conv_w=v_a_conv_w, a_conv_b=v_a_conv_b,
             a_norm_g=v_a_norm_g, a_norm_b=v_a_norm_b, a_w_out=v_a_w_out, a_b_out=v_a_b_out, kv_w_down=v_kv_w_down,
             kv_norm_g=v_kv_norm_g, kv_w_uk=v_kv_w_uk, kv_w_uv=v_kv_w_uv, b_w_in=v_b_w_in, b_q_norm_g=v_b_q_norm_g,
             b_w_uq=v_b_w_uq, b_w_out=v_b_w_out)

    small_flat = jnp.concatenate([w[n].reshape(-1) for n in SMALL_WEIGHTS]).reshape(1, -1)
    rest_names = [n for n in MATMUL_WEIGHTS if n != 'a_w_in']
    group1 = ('b_w_out', 'b_w_uq', 'b_w_in', 'kv_w_uk', 'kv_w_uv', 'kv_w_down')
    group1b = ('a_w_out',)

    class Comm:
        def __init__(self):
            self.first, self.first_token = _comm_start(
                "gather_first_start", [_shard_2d(w['a_w_in']).astype(BF16), small_flat], True, None, CHIP_LEVEL_PEERS)

        def start_token(self):
            return self.first_token

        def first_weights(self, after):
            lands = _comm_wait("gather_first_wait", self.first, True, after, CHIP_LEVEL_PEERS)
            ga = _forward_to_sibling("gather_first_forward", list(lands))
            full = {'a_w_in': ga[0]}
            gs = ga[1].reshape(N_DEV, -1)
            off = 0
            for n in SMALL_WEIGHTS:
                size = math.prod(w[n].shape)
                full[n] = _gathered_to_full(n, gs[:, off:off + size].reshape((N_DEV,) + _shard_2d(w[n]).shape))
                off += size
            for n in REPLICATED:
                full[n] = w[n]
            dense = lambda a: a if a.shape[-1] % LANES == 0 else a.reshape(-1, LANES)
            self.rest, self.rest_token = _comm_start(
                "gather_rest_start", [dense(_shard_2d(w[n]).astype(BF16)) for n in rest_names], True, ga[0])
            return _prep_weights(full)

        def first_after(self):
            return self.rest_token

        def rest_weights(self, after):
            lands = _comm_wait("gather_rest_wait", self.rest, True, after)
            blocks = {n: lands[i].reshape((N_DEV,) + _shard_2d(w[n]).shape) for i, n in enumerate(rest_names)}
            return _prep_weights({n: _gathered_to_full(n, blocks[n]) for n in rest_names})

        def send_group1(self, g):
            self.g1, token = _comm_start("exchange_g1_start", [g[n] for n in group1], False, None)
            return token

        def send_group1b(self, g):
            self.g1b, token = _comm_start("exchange_g1b_start", [g[n] for n in group1b], False, None)
            return token

        def send_group2(self, dw_lo, small):
            self.g2, token = _comm_start("exchange_g2_start", [dw_lo, _pack_small_grads(*small)], False, None)
            return token

        def send_group3(self, dw_hi):
            self.g3, token = _comm_start("exchange_g3_start", [dw_hi], False, None)
            return token

    comm = Comm()

    _, grad_x = _local_step(x[0], positions[0], loss_target[0], comm)

    res = {}
    recv1 = _comm_wait("exchange_g1_wait", comm.g1, False, grad_x)
    for i, n in enumerate(group1):
        res[n] = _adamw(n, recv1[i], _shard_2d(w[n]), _shard_2d(m[n]), _shard_2d(v[n]))
    recv1b = _comm_wait("exchange_g1b_wait", comm.g1b, False, res[group1[-1]][0])
    for i, n in enumerate(group1b):
        res[n] = _adamw(n, recv1b[i], _shard_2d(w[n]), _shard_2d(m[n]), _shard_2d(v[n]))
    recv2 = _comm_wait("exchange_g2_wait", comm.g2, False, res[group1b[-1]][0])
    recv3 = _comm_wait("exchange_g3_wait", comm.g3, False, recv2[0])
    res['a_w_in'] = _adamw('a_w_in', [recv2[0], recv3[0]], _shard_2d(w['a_w_in']), _shard_2d(m['a_w_in']),
                           _shard_2d(v['a_w_in']))
    two_d = lambda d: [_shard_2d(d[n]) if d[n].ndim > 1 else d[n].reshape(1, -1) for n in SMALL_NAMES]
    sg, sd, sm, sv, loss = _adamw_small(recv2[-1], two_d(w), two_d(m), two_d(v))
    for i, n in enumerate(SMALL_NAMES):
        res[n] = (sg[i], sd[i], sm[i], sv[i])
    outs = [[res[n][j].reshape(w[n].shape) for n in WEIGHT_NAMES] for j in range(4)]
    return (loss[0, 0], grad_x[None], *outs[0], *outs[1], *outs[2], *outs[3])
```

```python
import functools
import math

import jax
import jax.numpy as jnp
from jax import lax
from jax.experimental import pallas as pl
from jax.experimental.pallas import tpu as pltpu

F32 = jnp.float32
BF16 = jnp.bfloat16

D_MODEL = 1024
DEPTH = 2
CONV_WIDTH = 2 * D_MODEL
CONV_KERNEL = 31
N_HEADS = 16
QK_NOPE_DIM = 128
QK_ROPE_DIM = 64
V_HEAD_DIM = 128
KV_LORA_RANK = D_MODEL // 4
Q_LORA_RANK = D_MODEL // 2
ROPE_THETA = 10000.0
LN_EPS = 1e-5
RMS_EPS = 1e-6
MASK_VALUE = -1e30
ALPHA = (2.0 * DEPTH) ** 0.25
ATTN_SCALE = 1.0 / math.sqrt(QK_NOPE_DIM + QK_ROPE_DIM)

ADAM_LR = 0.001
ADAM_B1 = 0.9
ADAM_B2 = 0.999
ADAM_EPS = 1e-08
ADAM_WD = 0.01
ADAM_STEP = 10

N_DEV = 8
LANES = 128
HEAD_PAD = 256
HALO = 32
VMEM_LIMIT = 56 * 1024 * 1024

WEIGHT_NAMES = ['ln_g', 'ln_b', 'a_w_in', 'a_b_in', 'a_conv_w', 'a_conv_b', 'a_norm_g', 'a_norm_b',
                'a_w_out', 'a_b_out', 'kv_w_down', 'kv_norm_g', 'kv_w_uk', 'kv_w_uv', 'b_w_in',
                'b_q_norm_g', 'b_w_uq', 'b_w_out']
REPLICATED = ('ln_g', 'ln_b', 'kv_norm_g', 'b_q_norm_g')
MATMUL_WEIGHTS = ('a_w_in', 'a_w_out', 'kv_w_down', 'kv_w_uk', 'kv_w_uv', 'b_w_in', 'b_w_uq', 'b_w_out')
SMALL_WEIGHTS = ('a_b_in', 'a_conv_w', 'a_conv_b', 'a_norm_g', 'a_norm_b', 'a_b_out')

NN = (((1,), (0,)), ((), ()))
NT = (((1,), (1,)), ((), ()))
TN = (((0,), (0,)), ((), ()))


def _cparams(sem):
    return pltpu.CompilerParams(dimension_semantics=sem, vmem_limit_bytes=VMEM_LIMIT)


def _sigmoid(x):
    return 0.5 * jnp.tanh(0.5 * x) + 0.5


def _fold_rows(x):
    parts = [x[r:r + 8] for r in range(0, x.shape[0], 8)]
    while len(parts) > 1:
        parts = [parts[i] + parts[i + 1] for i in range(0, len(parts) - 1, 2)] + ([parts[-1]] if len(parts) % 2 else [])
    return parts[0]


def _taps_by_residue(offset_of):
    groups = {}
    for j in range(CONV_KERNEL):
        off = offset_of(j)
        groups.setdefault(off % 8, []).append((j, off // 8))
    return sorted(groups.items())


def _swap_rope_halves(t):
    lane = lax.broadcasted_iota(jnp.int32, t.shape, 1)
    return jnp.where(lane < QK_ROPE_DIM // 2, pltpu.roll(t, LANES - QK_ROPE_DIM // 2, 1),
                     pltpu.roll(t, QK_ROPE_DIM // 2, 1))


def _matmul(a, b, *, name, bias=None, trans_a=False, trans_b=False, out_dtype=F32, bm=1024, bn=1024, bk=1024,
            b_blocked=False, out_blocked=False, out_parts=1, after=None, a_cols=None):
    m_off = 0
    if trans_a:
        kdim, m = a.shape
        if a_cols is not None:
            m_off, m = a_cols
    else:
        m, kdim = a.shape
    if b_blocked and trans_b:
        n, k2 = b.shape[1], b.shape[0] * b.shape[2]
        bk = b.shape[2]
    elif b_blocked:
        k2, n = b.shape[1], b.shape[0] * b.shape[2]
        bn = b.shape[2]
    elif trans_b:
        n, k2 = b.shape
    else:
        k2, n = b.shape
    assert kdim == k2, (a.shape, b.shape)
    bm, bn, bk = min(bm, m), min(bn, n), min(bk, kdim)
    assert m % bm == 0 and n % bn == 0 and kdim % bk == 0, (name, m, n, kdim, bm, bn, bk)
    nk = kdim // bk
    assert out_parts == 1 or (out_blocked and bn % out_parts == 0)
    pw = bn // out_parts
    dims = (((0 if trans_a else 1,), (1 if trans_b else 0,)), ((), ()))
    has_bias = bias is not None

    def body(*refs):
        refs = list(refs)
        a_ref, b_ref = refs[:2]
        del refs[:2]
        bias_ref = refs.pop(0) if has_bias else None
        if after is not None:
            refs.pop(0)
        o_ref = refs.pop(0)
        rest = refs
        prod = lax.dot_general(a_ref[...], b_ref[...], dims, preferred_element_type=F32)

        def finish(acc):
            if has_bias:
                acc = acc + bias_ref[...]
            if out_parts == 1:
                o_ref[...] = acc.astype(out_dtype)
            else:
                for q in range(out_parts):
                    o_ref[q] = acc[:, q * pw:(q + 1) * pw].astype(out_dtype)

        if nk == 1:
            finish(prod)
        else:
            acc_ref = rest[0]
            k = pl.program_id(2)

            @pl.when(k == 0)
            def _():
                acc_ref[...] = prod

            @pl.when(k > 0)
            def _():
                acc_ref[...] += prod

            @pl.when(k == nk - 1)
            def _():
                finish(acc_ref[...])

    assert m_off % bm == 0
    a_first = m_off // bm
    a_spec = (pl.BlockSpec((bk, bm), lambda i, j, k: (k, i + a_first)) if trans_a
              else pl.BlockSpec((bm, bk), lambda i, j, k: (i, k)))
    if b_blocked and trans_b:
        b_spec = pl.BlockSpec((None, bn, bk), lambda i, j, k: (k, j, 0))
    elif b_blocked:
        b_spec = pl.BlockSpec((None, bk, bn), lambda i, j, k: (j, k, 0))
    elif trans_b:
        b_spec = pl.BlockSpec((bn, bk), lambda i, j, k: (j, k))
    else:
        b_spec = pl.BlockSpec((bk, bn), lambda i, j, k: (k, j))
    in_specs = [a_spec, b_spec]
    args = [a, b]
    if has_bias:
        in_specs.append(pl.BlockSpec((1, bn), lambda i, j, k: (0, j)))
        args.append(bias)
    if after is not None:
        in_specs.append(pl.BlockSpec(memory_space=pl.ANY))
        args.append(after)
    if out_blocked and out_parts > 1:
        out_struct = jax.ShapeDtypeStruct((n // pw, m, pw), out_dtype)
        out_spec = pl.BlockSpec((out_parts, bm, pw), lambda i, j, k: (j, i, 0))
    elif out_blocked:
        out_struct = jax.ShapeDtypeStruct((n // bn, m, bn), out_dtype)
        out_spec = pl.BlockSpec((None, bm, bn), lambda i, j, k: (j, i, 0))
    else:
        out_struct = jax.ShapeDtypeStruct((m, n), out_dtype)
        out_spec = pl.BlockSpec((bm, bn), lambda i, j, k: (i, j))
    return pl.pallas_call(
        body, name=name,
        out_shape=out_struct,
        grid=(m // bm, n // bn, nk),
        in_specs=in_specs,
        out_specs=out_spec,
        scratch_shapes=[pltpu.VMEM((bm, bn), F32)] if nk > 1 else [],
        compiler_params=_cparams(("parallel", "parallel", "arbitrary")),
    )(*args)


def _rope_tables(pos_col, freq_row, x, after):
    t, d = x.shape
    tm = min(t, 512)
    extra = [] if after is None else [after]

    def body(pos_ref, f_ref, x_ref, *rest):
        c_ref, s_ref, xb_ref = rest[-3:]
        ang = pos_ref[...].astype(F32) * f_ref[...]
        lane = lax.broadcasted_iota(jnp.int32, ang.shape, 1)
        cos = jnp.cos(ang)
        sin = jnp.sin(ang)
        c_ref[...] = jnp.where(lane < QK_ROPE_DIM, cos, 0.0)
        s_ref[...] = jnp.where(lane < QK_ROPE_DIM // 2, -sin, jnp.where(lane < QK_ROPE_DIM, sin, 0.0))
        xb_ref[...] = x_ref[...].astype(BF16)

    lane_blk = pl.BlockSpec((tm, LANES), lambda i: (i, 0))
    row = pl.BlockSpec((tm, d), lambda i: (i, 0))
    return pl.pallas_call(
        body, name="rope_tables",
        out_shape=(jax.ShapeDtypeStruct((t, LANES), F32), jax.ShapeDtypeStruct((t, LANES), F32),
                   jax.ShapeDtypeStruct((t, d), BF16)),
        grid=(t // tm,),
        in_specs=[pl.BlockSpec((tm, 1), lambda i: (i, 0)), pl.BlockSpec((1, LANES), lambda i: (0, 0)), row]
                 + [pl.BlockSpec(memory_space=pl.ANY)] * len(extra),
        out_specs=(lane_blk, lane_blk, row),
        compiler_params=_cparams(("parallel",)),
    )(pos_col, freq_row, x, *extra)


def _conv_mid_fwd(proj, cw, cb, ng, nb):
    t = proj.shape[0]
    e = CONV_WIDTH
    tm = min(t, 128)
    hb = tm // HALO
    nchunk = e // LANES

    def body(val_ref, gg_ref, z_ref, valh_ref, ggh_ref, cw_ref, cb_ref, ng_ref, nb_ref, u1_ref, u2_ref, ext_ref, sh_ref):
        i = pl.program_id(0)
        u0 = val_ref[...] * _sigmoid(gg_ref[...])
        h0 = valh_ref[...] * _sigmoid(ggh_ref[...])
        ext_ref[0:HALO, :] = jnp.where(i > 0, h0, 0.0)
        ext_ref[HALO:, :] = u0

        def chunk(c, carry):
            col = pl.multiple_of(c * LANES, LANES)
            acc = jnp.zeros((tm, LANES), F32)
            for res, taps in _taps_by_residue(lambda j: HALO - (CONV_KERNEL - 1) + j):
                span = 8 * max(a for _, a in taps) + tm
                sh_ref[0:span, :] = ext_ref[res:res + span, pl.ds(col, LANES)]
                for j, a in taps:
                    acc = acc + cw_ref[j:j + 1, pl.ds(col, LANES)] * sh_ref[8 * a:8 * a + tm, :]
            u1_ref[:, pl.ds(col, LANES)] = acc + cb_ref[:, pl.ds(col, LANES)]
            return carry

        lax.fori_loop(0, nchunk, chunk, 0)
        g = ng_ref[...]
        b = nb_ref[...]
        for r in range(0, tm, 16):
            u1 = u1_ref[r:r + 16, :]
            mu = jnp.mean(u1, axis=-1, keepdims=True)
            xc = u1 - mu
            var = jnp.mean(xc * xc, axis=-1, keepdims=True)
            n = xc * lax.rsqrt(var + LN_EPS) * g + b
            z = z_ref[r:r + 16, :]
            u2_ref[r:r + 16, :] = ((n * _sigmoid(n)) * (z * _sigmoid(z))).astype(BF16)

    row = lambda c: pl.BlockSpec((tm, e), lambda i: (i, c))
    halo = lambda c: pl.BlockSpec((HALO, e), lambda i: (jnp.maximum(i * hb - 1, 0), c))
    par = lambda r: pl.BlockSpec((r, e), lambda i: (0, 0))
    return pl.pallas_call(
        body, name="conv_mid_fwd",
        out_shape=(jax.ShapeDtypeStruct((t, e), F32), jax.ShapeDtypeStruct((t, e), BF16)),
        grid=(t // tm,),
        in_specs=[row(0), row(1), row(2), halo(0), halo(1), par(HALO), par(1), par(1), par(1)],
        out_specs=(pl.BlockSpec((tm, e), lambda i: (i, 0)), pl.BlockSpec((tm, e), lambda i: (i, 0))),
        scratch_shapes=[pltpu.VMEM((tm + HALO, e), F32), pltpu.VMEM((tm + HALO, LANES), F32)],
        compiler_params=_cparams(("parallel",)),
    )(proj, proj, proj, proj, proj, cw, cb, ng, nb)


def _ln_res_fwd(x, u, w_out, b_out, g, b):
    t, d = x.shape
    ku = u.shape[1]
    tm = min(t, 512)
    steps = t // tm
    ring = 3
    assert steps >= ring

    def body(x_hbm, u_hbm, w_ref, bo_ref, g_ref, b_ref, h_ref, hb_ref, xh_ref, rs_ref, xbuf, ubuf, sems):
        s = pl.program_id(0)

        def fetches(step):
            rows = pl.ds(pl.multiple_of(step * tm, tm), tm)
            slot = step % ring
            return (pltpu.make_async_copy(x_hbm.at[rows], xbuf.at[slot], sems.at[slot]),
                    pltpu.make_async_copy(u_hbm.at[rows], ubuf.at[slot], sems.at[ring + slot]))

        @pl.when(s == 0)
        def _():
            for ahead in range(ring - 1):
                for cp in fetches(s + ahead):
                    cp.start()

        @pl.when(s + ring - 1 < steps)
        def _():
            for cp in fetches(s + ring - 1):
                cp.start()

        for cp in fetches(s):
            cp.wait()
        slot = s % ring
        y = lax.dot_general(ubuf[slot], w_ref[...], NN, preferred_element_type=F32) + bo_ref[...]
        r = ALPHA * xbuf[slot] + y
        mu = jnp.mean(r, axis=-1, keepdims=True)
        xc = r - mu
        var = jnp.mean(xc * xc, axis=-1, keepdims=True)
        rstd = lax.rsqrt(var + LN_EPS)
        xh = xc * rstd
        h = xh * g_ref[...] + b_ref[...]
        h_ref[...] = h
        hb_ref[...] = h.astype(BF16)
        xh_ref[...] = xh
        rs_ref[...] = rstd

    row = pl.BlockSpec((tm, d), lambda i: (i, 0))
    par = pl.BlockSpec((1, d), lambda i: (0, 0))
    return pl.pallas_call(
        body, name="ln0_fwd",
        out_shape=(jax.ShapeDtypeStruct((t, d), F32), jax.ShapeDtypeStruct((t, d), BF16),
                   jax.ShapeDtypeStruct((t, d), F32), jax.ShapeDtypeStruct((t, 1), F32)),
        grid=(steps,),
        in_specs=[pl.BlockSpec(memory_space=pl.ANY), pl.BlockSpec(memory_space=pl.ANY),
                  pl.BlockSpec((ku, d), lambda i: (0, 0), pipeline_mode=pl.Buffered(1)), par, par, par],
        out_specs=(row, row, row, pl.BlockSpec((tm, 1), lambda i: (i, 0))),
        scratch_shapes=[pltpu.VMEM((ring, tm, d), F32), pltpu.VMEM((ring, tm, ku), u.dtype),
                        pltpu.SemaphoreType.DMA((2 * ring,))],
        compiler_params=_cparams(("arbitrary",)),
    )(x, u, w_out, b_out, g, b)


def _kv_mid_fwd(h1b, w_down, w_ukv, g, rc, rs):
    t, d = h1b.shape
    nkv = w_ukv.shape[1]
    tm = min(t, 512)
    r = KV_LORA_RANK

    def body(h_ref, w_ref, wu_ref, g_ref, c_ref, s_ref, ckv_ref, ckn_ref, kr_ref, kv_ref):
        ckv = lax.dot_general(h_ref[...], w_ref[...], NN, preferred_element_type=F32)
        ckv_ref[...] = ckv
        c = ckv[:, 0:r]
        ms = jnp.mean(c * c, axis=-1, keepdims=True)
        ckn = (c * lax.rsqrt(ms + RMS_EPS) * g_ref[...]).astype(BF16)
        ckn_ref[...] = ckn
        tr = ckv[:, r:r + LANES]
        kr_ref[...] = (tr * c_ref[...] + _swap_rope_halves(tr) * s_ref[...]).astype(BF16)
        kv_ref[...] = lax.dot_general(ckn, wu_ref[...], NN, preferred_element_type=F32).astype(BF16)

    lane_blk = pl.BlockSpec((tm, LANES), lambda i: (i, 0))
    return pl.pallas_call(
        body, name="kv_mid_fwd",
        out_shape=(jax.ShapeDtypeStruct((t, r + LANES), F32), jax.ShapeDtypeStruct((t, r), BF16),
                   jax.ShapeDtypeStruct((t, LANES), BF16), jax.ShapeDtypeStruct((t, nkv), BF16)),
        grid=(t // tm,),
        in_specs=[pl.BlockSpec((tm, d), lambda i: (i, 0)),
                  pl.BlockSpec((d, r + LANES), lambda i: (0, 0), pipeline_mode=pl.Buffered(1)),
                  pl.BlockSpec((r, nkv), lambda i: (0, 0), pipeline_mode=pl.Buffered(1)),
                  pl.BlockSpec((1, r), lambda i: (0, 0)), lane_blk, lane_blk],
        out_specs=(pl.BlockSpec((tm, r + LANES), lambda i: (i, 0)), pl.BlockSpec((tm, r), lambda i: (i, 0)), lane_blk,
                   pl.BlockSpec((tm, nkv), lambda i: (i, 0))),
        compiler_params=_cparams(("parallel",)),
    )(h1b, w_down, w_ukv, g, rc, rs)


def _b_in_qnorm(h1b, w_blocks, g):
    t, d = h1b.shape
    nblk, _, cb = w_blocks.shape
    n = nblk * cb
    r = Q_LORA_RANK
    tm = min(t, 512)

    def body(a_ref, wb_ref, g_ref, o_ref, cqn_ref, w_ref):
        @pl.when(pl.program_id(0) == 0)
        def _():
            for j in range(nblk):
                w_ref[:, j * cb:(j + 1) * cb] = wb_ref[j]

        acc = lax.dot_general(a_ref[...], w_ref[...], NN, preferred_element_type=F32)
        o_ref[...] = acc
        c = acc[:, 0:r]
        ms = jnp.mean(c * c, axis=-1, keepdims=True)
        cqn_ref[...] = (c * lax.rsqrt(ms + RMS_EPS) * g_ref[...]).astype(BF16)

    return pl.pallas_call(
        body, name="b_in",
        out_shape=(jax.ShapeDtypeStruct((t, n), F32), jax.ShapeDtypeStruct((t, r), BF16),
                   jax.ShapeDtypeStruct((d, n), BF16)),
        grid=(t // tm,),
        in_specs=[pl.BlockSpec((tm, d), lambda i: (i, 0)),
                  pl.BlockSpec((nblk, d, cb), lambda i: (0, 0, 0), pipeline_mode=pl.Buffered(1)),
                  pl.BlockSpec((1, r), lambda i: (0, 0))],
        out_specs=(pl.BlockSpec((tm, n), lambda i: (i, 0)), pl.BlockSpec((tm, r), lambda i: (i, 0)),
                   pl.BlockSpec((d, n), lambda i: (0, 0))),
        compiler_params=_cparams(("arbitrary",)),
    )(h1b, w_blocks, g)


def _q_up_rope(cqn, wuq, rc, rs):
    t, r = cqn.shape
    w = wuq.shape[1]
    tm = min(t, 1024)
    bn = 4 * HEAD_PAD

    def body(a_ref, b_ref, c_ref, s_ref, o_ref):
        q = lax.dot_general(a_ref[...], b_ref[...], NN, preferred_element_type=F32)
        c = c_ref[...]
        s = s_ref[...]
        for h in range(bn // HEAD_PAD):
            a = h * HEAD_PAD
            o_ref[:, a:a + LANES] = q[:, a:a + LANES].astype(BF16)
            tr = q[:, a + LANES:a + 2 * LANES]
            o_ref[:, a + LANES:a + 2 * LANES] = (tr * c + _swap_rope_halves(tr) * s).astype(BF16)

    return pl.pallas_call(
        body, name="q_up_rope",
        out_shape=jax.ShapeDtypeStruct((t, w), BF16),
        grid=(t // tm, w // bn),
        in_specs=[pl.BlockSpec((tm, r), lambda i, j: (i, 0)), pl.BlockSpec((r, bn), lambda i, j: (0, j)),
                  pl.BlockSpec((tm, LANES), lambda i, j: (i, 0)), pl.BlockSpec((tm, LANES), lambda i, j: (i, 0))],
        out_specs=pl.BlockSpec((tm, bn), lambda i, j: (i, j)),
        compiler_params=_cparams(("parallel", "parallel")),
    )(cqn, wuq, rc, rs)


def _flash_fwd(qcat, kv, kr, projb):
    t = qcat.shape[0]
    tq = min(t, 512)
    nq = t // tq
    exp2_scale = ATTN_SCALE * math.log2(math.e)
    z_first = Q_LORA_RANK // LANES

    def body(q_ref, kv_ref, kr_ref, z_ref, o_ref, o2_ref, lse_ref, kcat_ref):
        kcat_ref[:, 0:LANES] = kv_ref[:, 0:LANES]
        kcat_ref[:, LANES:] = kr_ref[...]
        rows = lax.broadcasted_iota(jnp.int32, (tq, tq), 0)
        cols = lax.broadcasted_iota(jnp.int32, (tq, tq), 1)
        for i in range(nq):
            a = i * tq
            q = q_ref[a:a + tq, :]
            sd = lax.dot_general(q, kcat_ref[a:a + tq, :], NT, preferred_element_type=F32)
            sd = jnp.where(cols <= rows, sd, MASK_VALUE)
            m = jnp.max(sd, axis=1, keepdims=True)
            if i > 0:
                sp = lax.dot_general(q, kcat_ref[0:a, :], NT, preferred_element_type=F32)
                m = jnp.maximum(m, jnp.max(sp, axis=1, keepdims=True))
            pd = jnp.exp2((sd - m) * exp2_scale)
            l = jnp.sum(pd, axis=1, keepdims=True)
            acc = lax.dot_general(pd.astype(BF16), kv_ref[a:a + tq, LANES:], NN, preferred_element_type=F32)
            if i > 0:
                pp = jnp.exp2((sp - m) * exp2_scale)
                l = l + jnp.sum(pp, axis=1, keepdims=True)
                acc = acc + lax.dot_general(pp.astype(BF16), kv_ref[0:a, LANES:], NN, preferred_element_type=F32)
            o = acc / l
            z = z_ref[a:a + tq, :]
            o_ref[a:a + tq, :] = o
            o2_ref[a:a + tq, :] = (o * (z * _sigmoid(z))).astype(BF16)
            lse_ref[a:a + tq, :] = jnp.broadcast_to(m * ATTN_SCALE + jnp.log(l), (tq, LANES))

    hw = N_HEADS * LANES
    head256 = pl.BlockSpec((t, HEAD_PAD), lambda h: (0, h))
    head128 = pl.BlockSpec((t, LANES), lambda h: (0, h))
    return pl.pallas_call(
        body, name="flash_fwd",
        out_shape=(jax.ShapeDtypeStruct((t, hw), F32), jax.ShapeDtypeStruct((t, hw), BF16),
                   jax.ShapeDtypeStruct((t, hw), F32)),
        grid=(N_HEADS,),
        in_specs=[head256, head256, pl.BlockSpec((t, LANES), lambda h: (0, 0), pipeline_mode=pl.Buffered(1)),
                  pl.BlockSpec((t, LANES), lambda h: (0, z_first + h))],
        out_specs=(head128, head128, head128),
        scratch_shapes=[pltpu.VMEM((t, HEAD_PAD), BF16)],
        compiler_params=_cparams(("parallel",)),
    )(qcat, kv, kr, projb)


def _final_ln_loss(h1, o2, w_out, g, b, target):
    t, d = h1.shape
    ku = o2.shape[1]
    tm = min(t, 512)

    def body(h_ref, u_ref, w_ref, g_ref, b_ref, t_ref, loss_ref, dr_ref, drb_ref, dg_ref, db_ref):
        i = pl.program_id(0)
        r = ALPHA * h_ref[...] + lax.dot_general(u_ref[...], w_ref[...], NN, preferred_element_type=F32)
        mu = jnp.mean(r, axis=-1, keepdims=True)
        xc = r - mu
        var = jnp.mean(xc * xc, axis=-1, keepdims=True)
        rstd = lax.rsqrt(var + LN_EPS)
        xh = xc * rstd
        g = g_ref[...]
        diff = xh * g + b_ref[...] - t_ref[...]
        part = 0.5 * jnp.sum(jnp.mean(diff * diff, axis=-1, keepdims=True))
        dh = diff * (1.0 / d)
        dgp = jnp.sum(dh * xh, axis=0, keepdims=True)
        dbp = jnp.sum(dh, axis=0, keepdims=True)

        @pl.when(i == 0)
        def _():
            loss_ref[...] = jnp.zeros_like(loss_ref)
            dg_ref[...] = jnp.zeros_like(dg_ref)
            db_ref[...] = jnp.zeros_like(db_ref)

        loss_ref[...] += jnp.full(loss_ref.shape, part, F32)
        dg_ref[...] += dgp
        db_ref[...] += dbp
        dxh = dh * g
        dr = rstd * (dxh - jnp.mean(dxh, axis=-1, keepdims=True) - xh * jnp.mean(dxh * xh, axis=-1, keepdims=True))
        dr_ref[...] = dr
        drb_ref[...] = dr.astype(BF16)

    row = pl.BlockSpec((tm, d), lambda i: (i, 0))
    par = pl.BlockSpec((1, d), lambda i: (0, 0))
    return pl.pallas_call(
        body, name="final_ln_loss",
        out_shape=(jax.ShapeDtypeStruct((1, LANES), F32), jax.ShapeDtypeStruct((t, d), F32),
                   jax.ShapeDtypeStruct((t, d), BF16), jax.ShapeDtypeStruct((1, d), F32),
                   jax.ShapeDtypeStruct((1, d), F32)),
        grid=(t // tm,),
        in_specs=[row, pl.BlockSpec((tm, ku), lambda i: (i, 0)),
                  pl.BlockSpec((ku, d), lambda i: (0, 0), pipeline_mode=pl.Buffered(1)), par, par, row],
        out_specs=(pl.BlockSpec((1, LANES), lambda i: (0, 0)), row, row, par, par),
        compiler_params=_cparams(("arbitrary",)),
    )(h1, o2, w_out, g, b, target)


def _gate_bwd(dr2b, w_out, o, projb):
    t, hw = o.shape
    d = dr2b.shape[1]
    tm = min(t, 1024)
    bw = Q_LORA_RANK
    nb = hw // bw
    wb = projb.shape[1]
    steps = (t // tm) * nb
    ring = 3
    assert steps >= ring

    def body(dr_ref, w_ref, o_hbm, p_hbm, dob_ref, dz_ref, obuf, zbuf, sems):
        s = pl.program_id(0) * nb + pl.program_id(1)

        def fetches(step):
            rows = pl.ds(pl.multiple_of((step // nb) * tm, tm), tm)
            col = (step % nb) * bw
            slot = step % ring
            return (pltpu.make_async_copy(o_hbm.at[rows, pl.ds(pl.multiple_of(col, bw), bw)], obuf.at[slot],
                                          sems.at[slot]),
                    pltpu.make_async_copy(p_hbm.at[rows, pl.ds(pl.multiple_of(col + bw, bw), bw)], zbuf.at[slot],
                                          sems.at[ring + slot]))

        @pl.when(s == 0)
        def _():
            for ahead in range(ring - 1):
                for cp in fetches(s + ahead):
                    cp.start()

        @pl.when(s + ring - 1 < steps)
        def _():
            for cp in fetches(s + ring - 1):
                cp.start()

        for cp in fetches(s):
            cp.wait()
        slot = s % ring
        z = zbuf[slot]
        sg = _sigmoid(z)
        w = w_ref[pl.ds(pl.multiple_of(pl.program_id(1) * bw, bw), bw), :]
        d2 = lax.dot_general(dr_ref[...], w, NT, preferred_element_type=F32)
        dob_ref[...] = (d2 * (z * sg)).astype(BF16)
        dz_ref[...] = (d2 * obuf[slot] * (sg * (1.0 + z * (1.0 - sg)))).astype(BF16)

    blk = pl.BlockSpec((tm, bw), lambda i, j: (i, j))
    blk1 = pl.BlockSpec((tm, bw), lambda i, j: (i, j + 1))
    hbm = pl.BlockSpec(memory_space=pl.ANY)
    return pl.pallas_call(
        body, name="gate_bwd",
        out_shape=(jax.ShapeDtypeStruct((t, hw), BF16), jax.ShapeDtypeStruct((t, wb), BF16)),
        grid=(t // tm, nb),
        in_specs=[pl.BlockSpec((tm, d), lambda i, j: (i, 0)),
                  pl.BlockSpec((hw, d), lambda i, j: (0, 0), pipeline_mode=pl.Buffered(1)), hbm, hbm],
        out_specs=(blk, blk1),
        scratch_shapes=[pltpu.VMEM((ring, tm, bw), F32), pltpu.VMEM((ring, tm, bw), F32),
                        pltpu.SemaphoreType.DMA((2 * ring,))],
        compiler_params=_cparams(("arbitrary", "arbitrary")),
    )(dr2b, w_out, o, projb)


def _flash_bwd(qcat, kv, kr, dob, lse, o, rc, rs):
    t = qcat.shape[0]
    tq = min(t, 512)
    nq = t // tq
    q_chunk = 2 * tq
    log2e = math.log2(math.e)
    exp2_scale = ATTN_SCALE * log2e
    once = pl.Buffered(1)

    def body(q_ref, kv_ref, kr_ref, do_ref, lse_ref, o_ref, c_ref, s_ref,
             dq_ref, dkv_ref, dkr_ref, kcat_ref, dqa_ref, dka_ref, dva_ref, dl_ref):
        h = pl.program_id(0)
        kcat_ref[:, 0:LANES] = kv_ref[:, 0:LANES]
        kcat_ref[:, LANES:] = kr_ref[...]
        dqa_ref[...] = jnp.zeros_like(dqa_ref)
        dl_ref[...] = jnp.sum(do_ref[...].astype(F32) * o_ref[...], axis=1, keepdims=True)

        @pl.when(h == 0)
        def _():
            dkr_ref[...] = jnp.zeros_like(dkr_ref)

        rows = lax.broadcasted_iota(jnp.int32, (tq, tq), 0)
        cols = lax.broadcasted_iota(jnp.int32, (tq, tq), 1)

        def block(qs, n, ks, masked):
            q = q_ref[qs:qs + n, :]
            kc = kcat_ref[ks:ks + tq, :]
            s = lax.dot_general(q, kc, NT, preferred_element_type=F32)
            if masked:
                s = jnp.where(cols <= rows, s, MASK_VALUE)
            p = jnp.exp2(s * exp2_scale - lse_ref[qs:qs + n, 0:1] * log2e)
            do = do_ref[qs:qs + n, :]
            dp = lax.dot_general(do, kv_ref[ks:ks + tq, LANES:], NT, preferred_element_type=F32)
            ds = (p * (dp - dl_ref[qs:qs + n, :])).astype(BF16)
            dva_ref[...] += lax.dot_general(p.astype(BF16), do, TN, preferred_element_type=F32)
            dka_ref[...] += lax.dot_general(ds, q, TN, preferred_element_type=F32)
            dqa_ref[qs:qs + n, :] += lax.dot_general(ds, kc, NN, preferred_element_type=F32)

        for kb in range(nq):
            a = kb * tq
            dka_ref[...] = jnp.zeros_like(dka_ref)
            dva_ref[...] = jnp.zeros_like(dva_ref)
            block(a, tq, a, True)
            qs = a + tq
            while qs < t:
                n = min(q_chunk, t - qs)
                block(qs, n, a, False)
                qs += n
            dk = dka_ref[...] * ATTN_SCALE
            dkv_ref[a:a + tq, 0:LANES] = dk[:, 0:LANES].astype(BF16)
            dkv_ref[a:a + tq, LANES:] = dva_ref[...].astype(BF16)
            dkr_ref[a:a + tq, :] += dk[:, LANES:]

        dq = dqa_ref[...] * ATTN_SCALE
        dq_ref[:, 0:LANES] = dq[:, 0:LANES].astype(BF16)
        d2 = dq[:, LANES:]
        dq_ref[:, LANES:] = (d2 * c_ref[...] - _swap_rope_halves(d2) * s_ref[...]).astype(BF16)

    hp = N_HEADS * HEAD_PAD
    head256 = pl.BlockSpec((t, HEAD_PAD), lambda h: (0, h))
    head128 = pl.BlockSpec((t, LANES), lambda h: (0, h))
    const128 = pl.BlockSpec((t, LANES), lambda h: (0, 0), pipeline_mode=once)
    return pl.pallas_call(
        body, name="flash_bwd",
        out_shape=(jax.ShapeDtypeStruct((t, hp), BF16), jax.ShapeDtypeStruct((t, hp), BF16),
                   jax.ShapeDtypeStruct((t, LANES), F32)),
        grid=(N_HEADS,),
        in_specs=[head256, head256, const128, head128, head128, head128, const128, const128],
        out_specs=(head256, head256, pl.BlockSpec((t, LANES), lambda h: (0, 0))),
        scratch_shapes=[pltpu.VMEM((t, HEAD_PAD), BF16), pltpu.VMEM((t, HEAD_PAD), F32),
                        pltpu.VMEM((tq, HEAD_PAD), F32), pltpu.VMEM((tq, LANES), F32), pltpu.VMEM((t, 1), F32)],
        compiler_params=_cparams(("arbitrary",)),
    )(qcat, kv, kr, dob, lse, o, rc, rs)


def _q_norm_bwd(dq2, w_uq, projb, g, dprojb):
    t = projb.shape[0]
    kq = dq2.shape[1]
    tm = min(t, 512)
    r = Q_LORA_RANK

    def body(dq_ref, w_ref, c_ref, g_ref, alias_ref, o_ref, dg_ref):
        i = pl.program_id(0)
        c = c_ref[...]
        d = lax.dot_general(dq_ref[...], w_ref[...], NT, preferred_element_type=F32)
        rr = lax.rsqrt(jnp.mean(c * c, axis=-1, keepdims=True) + RMS_EPS)
        xh = c * rr

        @pl.when(i == 0)
        def _():
            dg_ref[...] = jnp.zeros_like(dg_ref)

        dg_ref[...] += jnp.sum(d * xh, axis=0, keepdims=True)
        dxh = d * g_ref[...]
        o_ref[...] = (rr * (dxh - xh * jnp.mean(dxh * xh, axis=-1, keepdims=True))).astype(BF16)

    blk = pl.BlockSpec((tm, r), lambda i: (i, 0))
    par = pl.BlockSpec((1, r), lambda i: (0, 0))
    return pl.pallas_call(
        body, name="q_norm_bwd",
        out_shape=(jax.ShapeDtypeStruct(dprojb.shape, BF16), jax.ShapeDtypeStruct((1, r), F32)),
        grid=(t // tm,),
        in_specs=[pl.BlockSpec((tm, kq), lambda i: (i, 0)),
                  pl.BlockSpec((r, kq), lambda i: (0, 0), pipeline_mode=pl.Buffered(1)),
                  blk, par, pl.BlockSpec(memory_space=pl.ANY)],
        out_specs=(blk, par),
        input_output_aliases={4: 0},
        compiler_params=_cparams(("arbitrary",)),
    )(dq2, w_uq, projb, g, dprojb)


def _kv_mid_bwd(dkv, w_ukv, ckv, dkr, g, rc, rs):
    t = ckv.shape[0]
    kk = dkv.shape[1]
    tm = min(t, 512)
    r = KV_LORA_RANK

    def body(dkv_ref, w_ref, ckv_ref, dkr_ref, g_ref, c_ref, s_ref, o_ref, dg_ref):
        i = pl.program_id(0)
        c = ckv_ref[:, 0:r]
        d = lax.dot_general(dkv_ref[...], w_ref[...], NT, preferred_element_type=F32)
        rr = lax.rsqrt(jnp.mean(c * c, axis=-1, keepdims=True) + RMS_EPS)
        xh = c * rr

        @pl.when(i == 0)
        def _():
            dg_ref[...] = jnp.zeros_like(dg_ref)

        dg_ref[...] += jnp.sum(d * xh, axis=0, keepdims=True)
        dxh = d * g_ref[...]
        o_ref[:, 0:r] = (rr * (dxh - xh * jnp.mean(dxh * xh, axis=-1, keepdims=True))).astype(BF16)
        dk = dkr_ref[...]
        o_ref[:, r:] = (dk * c_ref[...] - _swap_rope_halves(dk) * s_ref[...]).astype(BF16)

    return pl.pallas_call(
        body, name="kv_mid_bwd",
        out_shape=(jax.ShapeDtypeStruct((t, r + LANES), BF16), jax.ShapeDtypeStruct((1, r), F32)),
        grid=(t // tm,),
        in_specs=[pl.BlockSpec((tm, kk), lambda i: (i, 0)),
                  pl.BlockSpec((r, kk), lambda i: (0, 0), pipeline_mode=pl.Buffered(1)),
                  pl.BlockSpec((tm, r + LANES), lambda i: (i, 0)),
                  pl.BlockSpec((tm, LANES), lambda i: (i, 0)), pl.BlockSpec((1, r), lambda i: (0, 0)),
                  pl.BlockSpec((tm, LANES), lambda i: (i, 0)), pl.BlockSpec((tm, LANES), lambda i: (i, 0))],
        out_specs=(pl.BlockSpec((tm, r + LANES), lambda i: (i, 0)), pl.BlockSpec((1, r), lambda i: (0, 0))),
        compiler_params=_cparams(("arbitrary",)),
    )(dkv, w_ukv, ckv, dkr, g, rc, rs)


def _ln0_bwd(dr2, dprojb, w_in, dckv, w_down, xh, rstd, g, after):
    t, d = dr2.shape
    kb, kd = dprojb.shape[1], dckv.shape[1]
    tm = min(t, 256)
    extra = [] if after is None else [after]

    def body(a_ref, pb_ref, wb_ref, pd_ref, wd_ref, xh_ref, rs_ref, g_ref, *rest):
        dr_ref, drb_ref, dg_ref, db_ref, dsum_ref = rest[-5:]
        i = pl.program_id(0)
        dh = (ALPHA * a_ref[...] + lax.dot_general(pb_ref[...], wb_ref[...], NT, preferred_element_type=F32)
              + lax.dot_general(pd_ref[...], wd_ref[...], NT, preferred_element_type=F32))
        xh = xh_ref[...]

        @pl.when(i == 0)
        def _():
            dg_ref[...] = jnp.zeros_like(dg_ref)
            db_ref[...] = jnp.zeros_like(db_ref)
            dsum_ref[...] = jnp.zeros_like(dsum_ref)

        dg_ref[...] += jnp.sum(dh * xh, axis=0, keepdims=True)
        db_ref[...] += jnp.sum(dh, axis=0, keepdims=True)
        dxh = dh * g_ref[...]
        dr = rs_ref[...] * (dxh - jnp.mean(dxh, axis=-1, keepdims=True) - xh * jnp.mean(dxh * xh, axis=-1, keepdims=True))
        dr_ref[...] = dr
        drb_ref[...] = dr.astype(BF16)
        dsum_ref[...] += jnp.sum(dr, axis=0, keepdims=True)

    row = pl.BlockSpec((tm, d), lambda i: (i, 0))
    par = pl.BlockSpec((1, d), lambda i: (0, 0))
    return pl.pallas_call(
        body, name="ln0_bwd",
        out_shape=(jax.ShapeDtypeStruct((t, d), F32), jax.ShapeDtypeStruct((t, d), BF16),
                   jax.ShapeDtypeStruct((1, d), F32), jax.ShapeDtypeStruct((1, d), F32),
                   jax.ShapeDtypeStruct((1, d), F32)),
        grid=(t // tm,),
        in_specs=[row, pl.BlockSpec((tm, kb), lambda i: (i, 0)),
                  pl.BlockSpec((d, kb), lambda i: (0, 0), pipeline_mode=pl.Buffered(1)),
                  pl.BlockSpec((tm, kd), lambda i: (i, 0)),
                  pl.BlockSpec((d, kd), lambda i: (0, 0), pipeline_mode=pl.Buffered(1)),
                  row, pl.BlockSpec((tm, 1), lambda i: (i, 0)), par]
                 + [pl.BlockSpec(memory_space=pl.ANY)] * len(extra),
        out_specs=(row, row, par, par, par),
        compiler_params=_cparams(("arbitrary",)),
    )(dr2, dprojb, w_in, dckv, w_down, xh, rstd, g, *extra)


def _conv_mid_bwd_rows(u1, proj, dr1b, w_out, ng, nb, after):
    t = u1.shape[0]
    e = CONV_WIDTH
    d = dr1b.shape[1]
    tm = min(t, 256)
    extra = [] if after is None else [after]

    rg = 16
    nt = t // tm

    def body(u1_ref, z_ref, dr_ref, w_ref, ng_ref, nb_ref, *rest):
        du1_ref, dz_ref, dng_ref, dnb_ref, dbz_ref, acc_ref, d_ref = rest[-7:]
        i = pl.program_id(0)

        @pl.when(i == 0)
        def _():
            acc_ref[...] = jnp.zeros_like(acc_ref)

        d_ref[...] = lax.dot_general(dr_ref[...], w_ref[...], NT, preferred_element_type=F32)
        g = ng_ref[...]
        b = nb_ref[...]

        def group(r, carry):
            rows = pl.ds(pl.multiple_of(r * rg, rg), rg)
            u1 = u1_ref[rows, :]
            mu = jnp.mean(u1, axis=-1, keepdims=True)
            xc = u1 - mu
            rstd = lax.rsqrt(jnp.mean(xc * xc, axis=-1, keepdims=True) + LN_EPS)
            xh = xc * rstd
            n = xh * g + b
            sn = _sigmoid(n)
            z = z_ref[rows, :]
            sz = _sigmoid(z)
            du2 = d_ref[rows, :]
            dz = du2 * (n * sn) * (sz * (1.0 + z * (1.0 - sz)))
            dn = du2 * (z * sz) * (sn * (1.0 + n * (1.0 - sn)))
            acc_ref[0:8, :] += _fold_rows(dn * xh)
            acc_ref[8:16, :] += _fold_rows(dn)
            acc_ref[16:24, :] += _fold_rows(dz)
            dz_ref[rows, :] = dz.astype(BF16)
            dxh = dn * g
            du1_ref[rows, :] = rstd * (dxh - jnp.mean(dxh, axis=-1, keepdims=True)
                                       - xh * jnp.mean(dxh * xh, axis=-1, keepdims=True))
            return carry

        lax.fori_loop(0, tm // rg, group, 0, unroll=8)

        @pl.when(i == nt - 1)
        def _():
            dng_ref[...] = jnp.sum(acc_ref[0:8, :], axis=0, keepdims=True)
            dnb_ref[...] = jnp.sum(acc_ref[8:16, :], axis=0, keepdims=True)
            dbz_ref[...] = jnp.sum(acc_ref[16:24, :], axis=0, keepdims=True)

    row = pl.BlockSpec((tm, e), lambda i: (i, 0))
    par = pl.BlockSpec((1, e), lambda i: (0, 0))
    return pl.pallas_call(
        body, name="conv_mid_bwd_rows",
        out_shape=(jax.ShapeDtypeStruct((t, e), F32), jax.ShapeDtypeStruct((t, 3 * e), BF16),
                   jax.ShapeDtypeStruct((1, e), F32), jax.ShapeDtypeStruct((1, e), F32),
                   jax.ShapeDtypeStruct((1, e), F32)),
        grid=(nt,),
        in_specs=[row, pl.BlockSpec((tm, e), lambda i: (i, 2)), pl.BlockSpec((tm, d), lambda i: (i, 0)),
                  pl.BlockSpec((e, d), lambda i: (0, 0), pipeline_mode=pl.Buffered(1)), par, par]
                 + [pl.BlockSpec(memory_space=pl.ANY)] * len(extra),
        out_specs=(row, pl.BlockSpec((tm, e), lambda i: (i, 2)), par, par, par),
        scratch_shapes=[pltpu.VMEM((24, e), F32), pltpu.VMEM((tm, e), F32)],
        compiler_params=_cparams(("arbitrary",)),
    )(u1, proj, dr1b, w_out, ng, nb, *extra)


def _conv_bwd(du1, proj, cw, dproj):
    t = du1.shape[0]
    e = CONV_WIDTH
    tm = min(t, 128)
    hb = tm // HALO
    nt = t // tm
    nchunk = e // LANES
    last_halo = t // HALO - 1

    def body(d_ref, dn_ref, val_ref, gg_ref, valh_ref, ggh_ref, cw_ref, alias_ref,
             dp_ref, dcw_ref, dcb_ref, dbv_ref, extu_ref, extd_ref, du0_ref, acc_ref, sh_ref):
        i = pl.program_id(0)
        h0 = valh_ref[...] * _sigmoid(ggh_ref[...])
        extu_ref[0:HALO, :] = jnp.where(i > 0, h0, 0.0)
        extu_ref[HALO:, :] = val_ref[...] * _sigmoid(gg_ref[...])
        extd_ref[0:tm, :] = d_ref[...]
        extd_ref[tm:, :] = jnp.where(i < nt - 1, dn_ref[...], 0.0)

        @pl.when(i == 0)
        def _():
            acc_ref[...] = jnp.zeros_like(acc_ref)

        def chunk(c, carry):
            col = pl.multiple_of(c * LANES, LANES)
            d = extd_ref[0:tm, pl.ds(col, LANES)]
            du0 = jnp.zeros((tm, LANES), F32)
            for res, taps in _taps_by_residue(lambda j: CONV_KERNEL - 1 - j):
                span = 8 * max(a for _, a in taps) + tm
                sh_ref[0:span, :] = extd_ref[res:res + span, pl.ds(col, LANES)]
                for j, a in taps:
                    du0 = du0 + cw_ref[j:j + 1, pl.ds(col, LANES)] * sh_ref[8 * a:8 * a + tm, :]
            for res, taps in _taps_by_residue(lambda j: HALO - (CONV_KERNEL - 1) + j):
                span = 8 * max(a for _, a in taps) + tm
                sh_ref[0:span, :] = extu_ref[res:res + span, pl.ds(col, LANES)]
                for j, a in taps:
                    prod = d * sh_ref[8 * a:8 * a + tm, :]
                    acc_ref[j * 8:(j + 1) * 8, pl.ds(col, LANES)] += _fold_rows(prod)
            acc_ref[CONV_KERNEL * 8:(CONV_KERNEL + 1) * 8, pl.ds(col, LANES)] += _fold_rows(d)
            du0_ref[:, pl.ds(col, LANES)] = du0
            return carry

        lax.fori_loop(0, nchunk, chunk, 0)
        kb = CONV_KERNEL * 8
        for r in range(0, tm, 16):
            sg = _sigmoid(gg_ref[r:r + 16, :])
            dval = du0_ref[r:r + 16, :] * sg
            dgg = dval * val_ref[r:r + 16, :] * (1.0 - sg)
            dp_ref[r:r + 16, 0:e] = dval.astype(BF16)
            dp_ref[r:r + 16, e:] = dgg.astype(BF16)
            acc_ref[kb + 8:kb + 16, :] += _fold_rows(dval)
            acc_ref[kb + 16:kb + 24, :] += _fold_rows(dgg)

        @pl.when(i == nt - 1)
        def _():
            for j in range(CONV_KERNEL):
                dcw_ref[j:j + 1, :] = jnp.sum(acc_ref[j * 8:(j + 1) * 8, :], axis=0, keepdims=True)
            dcw_ref[CONV_KERNEL:, :] = jnp.zeros((HALO - CONV_KERNEL, e), F32)
            dcb_ref[...] = jnp.sum(acc_ref[kb:kb + 8, :], axis=0, keepdims=True)
            dbv_ref[:, 0:e] = jnp.sum(acc_ref[kb + 8:kb + 16, :], axis=0, keepdims=True)
            dbv_ref[:, e:] = jnp.sum(acc_ref[kb + 16:kb + 24, :], axis=0, keepdims=True)

    row = lambda c: pl.BlockSpec((tm, e), lambda i: (i, c))
    halo_prev = lambda c: pl.BlockSpec((HALO, e), lambda i: (jnp.maximum(i * hb - 1, 0), c))
    halo_next = pl.BlockSpec((HALO, e), lambda i: (jnp.minimum((i + 1) * hb, last_halo), 0))
    return pl.pallas_call(
        body, name="conv_bwd",
        out_shape=(jax.ShapeDtypeStruct(dproj.shape, BF16), jax.ShapeDtypeStruct((HALO, e), F32),
                   jax.ShapeDtypeStruct((1, e), F32), jax.ShapeDtypeStruct((1, 2 * e), F32)),
        grid=(nt,),
        in_specs=[row(0), halo_next, row(0), row(1), halo_prev(0), halo_prev(1),
                  pl.BlockSpec((HALO, e), lambda i: (0, 0)), pl.BlockSpec(memory_space=pl.ANY)],
        out_specs=(pl.BlockSpec((tm, 2 * e), lambda i: (i, 0)), pl.BlockSpec((HALO, e), lambda i: (0, 0)),
                   pl.BlockSpec((1, e), lambda i: (0, 0)), pl.BlockSpec((1, 2 * e), lambda i: (0, 0))),
        scratch_shapes=[pltpu.VMEM((tm + HALO, e), F32), pltpu.VMEM((tm + HALO, e), F32),
                        pltpu.VMEM((tm, e), F32), pltpu.VMEM(((CONV_KERNEL + 3) * 8, e), F32),
                        pltpu.VMEM((tm + HALO, LANES), F32)],
        input_output_aliases={7: 0},
        compiler_params=_cparams(("arbitrary",)),
    )(du1, du1, proj, proj, proj, proj, cw, dproj)


def _adam_update(g, w, m, v):
    c1 = 1.0 - ADAM_B1 ** ADAM_STEP
    c2 = 1.0 - ADAM_B2 ** ADAM_STEP
    mn = ADAM_B1 * m + (1.0 - ADAM_B1) * g
    vn = ADAM_B2 * v + (1.0 - ADAM_B2) * (g * g)
    delta = -ADAM_LR * ((mn / c1) / (jnp.sqrt(vn / c2) + ADAM_EPS) + ADAM_WD * w)
    return delta, mn, vn


def _adamw(name, gbuf, w, m, v):
    parts = list(gbuf) if isinstance(gbuf, (list, tuple)) else [gbuf]
    npart = len(parts)
    r, c = w.shape
    tr = min(r // npart, 256)
    per_part = r // npart // tr
    assert r == npart * per_part * tr and all(p.shape == (N_DEV, r // npart, c) for p in parts)

    def body(*refs):
        g_refs = refs[:npart]
        w_ref, m_ref, v_ref, go_ref, d_ref, mo_ref, vo_ref = refs[npart:]
        i = pl.program_id(0)

        def run(g_ref):
            g = g_ref[0].astype(F32)
            for k in range(1, N_DEV):
                g = g + g_ref[k].astype(F32)
            go_ref[...] = g
            d_ref[...], mo_ref[...], vo_ref[...] = _adam_update(g, w_ref[...], m_ref[...], v_ref[...])

        if npart == 1:
            run(g_refs[0])
        else:
            for q in range(npart):
                pl.when((i >= q * per_part) & (i < (q + 1) * per_part))(functools.partial(run, g_refs[q]))

    blk = pl.BlockSpec((tr, c), lambda i: (i, 0))
    part_spec = lambda q: pl.BlockSpec((N_DEV, tr, c), lambda i: (0, jnp.clip(i - q * per_part, 0, per_part - 1), 0))
    shp = jax.ShapeDtypeStruct((r, c), F32)
    return pl.pallas_call(
        body, name="adamw_" + name,
        out_shape=(shp, shp, shp, shp),
        grid=(r // tr,),
        in_specs=[part_spec(q) for q in range(npart)] + [blk, blk, blk],
        out_specs=(blk, blk, blk, blk),
        compiler_params=_cparams(("parallel",)),
    )(*parts, w, m, v)


SMALL_TABLE_COLS = 256
SMALL_LAYOUT = {
    'ln_g': (0, 2, 1024), 'ln_b': (8, 2, 1024), 'a_b_in': (16, 1, 768), 'a_conv_b': (24, 1, 256),
    'a_norm_g': (32, 1, 256), 'a_norm_b': (40, 1, 256), 'a_b_out': (48, 1, 128), 'kv_norm_g': (56, 1, 256),
    'b_q_norm_g': (64, 1, 512), 'a_conv_w': (72, CONV_KERNEL, 256),
}
LOSS_ROW = 104
SMALL_TABLE_ROWS = 112
SMALL_NAMES = tuple(SMALL_LAYOUT)


def _small_pieces(name, row):
    r0, _, c = SMALL_LAYOUT[name]
    w = min(c, SMALL_TABLE_COLS)
    per_row = c // w
    return [(r0 + row * per_row + q, q * w, w) for q in range(per_row)]


def _pack_small_grads(dg0, dg1, db0, db1, dbv, dbz, dcb, dng, dnb, dba, dgkv, dgq, dcw, loss):
    e = CONV_WIDTH
    ce = e // N_DEV

    def body(dg0_ref, dg1_ref, db0_ref, db1_ref, dbv_ref, dbz_ref, dcb_ref, dng_ref, dnb_ref, dba_ref, dgkv_ref,
             dgq_ref, dcw_ref, loss_ref, o_ref):
        o_ref[...] = jnp.zeros_like(o_ref)
        for d in range(N_DEV):
            o_ref[d, LOSS_ROW:LOSS_ROW + 1, 0:LANES] = loss_ref[...]

        def put(d, name, row, src_ref, first_col=0):
            for trow, col, w in _small_pieces(name, row):
                o_ref[d, trow:trow + 1, 0:w] = src_ref[:, first_col + col:first_col + col + w]

        for d in range(N_DEV):
            put(d, 'ln_g', 0, dg0_ref)
            put(d, 'ln_g', 1, dg1_ref)
            put(d, 'ln_b', 0, db0_ref)
            put(d, 'ln_b', 1, db1_ref)
            for trow, col, w in _small_pieces('a_b_in', 0):
                gcol = d * 3 * ce + col
                src = dbv_ref[:, gcol:gcol + w] if gcol < 2 * e else dbz_ref[:, gcol - 2 * e:gcol - 2 * e + w]
                o_ref[d, trow:trow + 1, 0:w] = src
            put(d, 'a_conv_b', 0, dcb_ref, d * ce)
            put(d, 'a_norm_g', 0, dng_ref, d * ce)
            put(d, 'a_norm_b', 0, dnb_ref, d * ce)
            put(d, 'a_b_out', 0, dba_ref, d * LANES)
            put(d, 'kv_norm_g', 0, dgkv_ref)
            put(d, 'b_q_norm_g', 0, dgq_ref)
            r0 = SMALL_LAYOUT['a_conv_w'][0]
            o_ref[d, r0:r0 + HALO, 0:ce] = dcw_ref[:, d * ce:(d + 1) * ce]

    return pl.pallas_call(
        body, name="pack_small_grads",
        out_shape=jax.ShapeDtypeStruct((N_DEV, SMALL_TABLE_ROWS, SMALL_TABLE_COLS), F32),
        compiler_params=pltpu.CompilerParams(vmem_limit_bytes=VMEM_LIMIT),
    )(dg0, dg1, db0, db1, dbv, dbz, dcb, dng, dnb, dba, dgkv, dgq, dcw, loss)


def _adamw_small(gtab, ws, ms, vs):
    n = len(SMALL_NAMES)

    def body(*refs):
        g_ref = refs[0]
        w_refs, m_refs, v_refs = refs[1:1 + n], refs[1 + n:1 + 2 * n], refs[1 + 2 * n:1 + 3 * n]
        outs = refs[1 + 3 * n:]

        def summed(r0, r, c):
            g = g_ref[0, r0:r0 + r, 0:c]
            for k in range(1, N_DEV):
                g = g + g_ref[k, r0:r0 + r, 0:c]
            return g

        for i, name in enumerate(SMALL_NAMES):
            r0, r, c = SMALL_LAYOUT[name]
            if c <= SMALL_TABLE_COLS:
                blocks = [(slice(None), summed(r0, r, c))]
            else:
                blocks = [(slice(row, row + 1), jnp.concatenate([summed(tr, 1, w) for tr, _, w in _small_pieces(name, row)], axis=1))
                          for row in range(r)]
            for rows, g in blocks:
                delta, mn, vn = _adam_update(g, w_refs[i][rows, :], m_refs[i][rows, :], v_refs[i][rows, :])
                outs[i][rows, :] = g
                outs[n + i][rows, :] = delta
                outs[2 * n + i][rows, :] = mn
                outs[3 * n + i][rows, :] = vn
        outs[4 * n][...] = summed(LOSS_ROW, 1, LANES)

    shapes = tuple(jax.ShapeDtypeStruct(w.shape, F32) for w in ws)
    res = pl.pallas_call(
        body, name="adamw_small",
        out_shape=shapes * 4 + (jax.ShapeDtypeStruct((1, LANES), F32),),
        compiler_params=pltpu.CompilerParams(vmem_limit_bytes=VMEM_LIMIT),
    )(gtab, *ws, *ms, *vs)
    return res[:n], res[n:2 * n], res[2 * n:3 * n], res[3 * n:4 * n], res[4 * n]


def _mesh_peers():
    x, y, c = lax.axis_index("x"), lax.axis_index("y"), lax.axis_index("c")
    me = 4 * x + 2 * y + c
    peers = []
    for k in range(1, N_DEV):
        px = 1 - x if (k >> 2) & 1 else x
        py = 1 - y if (k >> 1) & 1 else y
        pc = 1 - c if k & 1 else c
        peers.append(((px, py, pc), 4 * px + 2 * py + pc))
    return me, peers


_HBM_SPEC = pl.BlockSpec(memory_space=pltpu.HBM)
_SEM_SPEC = pl.BlockSpec(memory_space=pltpu.SEMAPHORE)


ALL_PEERS = tuple(range(N_DEV - 1))
CHIP_LEVEL_PEERS = (0, 1, 3, 5)


def _split_copies(srcs, lands, send_sems, recv_sems, local_sems, gather, which=ALL_PEERS):
    me, peers = _mesh_peers()
    na = len(srcs)
    mine = lambda a, slot: srcs[a] if gather else srcs[a].at[slot]
    local = [pltpu.make_async_copy(mine(a, me), lands[a].at[me], local_sems.at[a]) for a in range(na)]
    sent, received = [], []
    for k, (dev, pid) in enumerate(peers):
        if k not in which:
            continue
        for a in range(na):
            s = a * (N_DEV - 1) + k
            sent.append(pltpu.make_async_remote_copy(
                src_ref=mine(a, pid), dst_ref=lands[a].at[me], send_sem=send_sems.at[s],
                recv_sem=recv_sems.at[s], device_id=dev, device_id_type=pl.DeviceIdType.MESH))
            received.append(pltpu.make_async_remote_copy(
                src_ref=mine(a, pid), dst_ref=lands[a].at[pid], send_sem=send_sems.at[s],
                recv_sem=recv_sems.at[s], device_id=dev, device_id_type=pl.DeviceIdType.MESH))
    return local, sent, received


def _comm_start(name, srcs, gather, after, which=ALL_PEERS):
    na = len(srcs)
    land_structs = [jax.ShapeDtypeStruct(((N_DEV,) + s.shape) if gather else s.shape, s.dtype) for s in srcs]
    extra = [] if after is None else [after]
    n_in = 2 * na + len(extra)

    def body(*refs):
        srcs_r, lands_r = refs[:na], refs[na:2 * na]
        send_sems, recv_sems, local_sems = refs[n_in:n_in + 3]
        token = refs[-1]
        local, sent, _ = _split_copies(srcs_r, lands_r, send_sems, recv_sems, local_sems, gather, which)
        for cp in local + sent:
            cp.start()
        token[...] = jnp.zeros_like(token)

    sem = pltpu.SemaphoreType.DMA((na * (N_DEV - 1),))
    outs = pl.pallas_call(
        body, name=name,
        out_shape=(sem, sem, pltpu.SemaphoreType.DMA((na,)),
                   *[pltpu.HBM(s.shape, s.dtype) for s in srcs],
                   *[pltpu.HBM(s.shape, s.dtype) for s in land_structs],
                   jax.ShapeDtypeStruct((8, LANES), F32)),
        in_specs=[_HBM_SPEC] * (2 * na) + [pl.BlockSpec(memory_space=pl.ANY)] * len(extra),
        out_specs=(_SEM_SPEC, _SEM_SPEC, _SEM_SPEC, *([_HBM_SPEC] * (2 * na)), pl.BlockSpec(memory_space=pltpu.VMEM)),
        input_output_aliases={i: 3 + i for i in range(2 * na)},
        compiler_params=pltpu.CompilerParams(has_side_effects=pltpu.SideEffectType.DATAFLOW_SIDE_EFFECTING),
    )(*[pltpu.with_memory_space_constraint(s, pltpu.HBM) for s in srcs],
      *[pltpu.with_memory_space_constraint(lax.empty(s.shape, s.dtype), pltpu.HBM) for s in land_structs],
      *extra)
    return (outs[:3], outs[3:3 + na], outs[3 + na:3 + 2 * na]), outs[-1]


def _comm_wait(name, handles, gather, after, which=ALL_PEERS):
    (send_sems, recv_sems, local_sems), srcs, lands = handles
    na = len(srcs)

    def body(*refs):
        srcs_r, lands_r = refs[:na], refs[na:2 * na]
        send_r, recv_r, local_r = refs[2 * na:2 * na + 3]
        local, sent, received = _split_copies(srcs_r, lands_r, send_r, recv_r, local_r, gather, which)
        for cp in sent:
            cp.wait_send()
        for cp in received:
            cp.wait_recv()
        for cp in local:
            cp.wait()

    outs = pl.pallas_call(
        body, name=name,
        out_shape=(*[pltpu.HBM(s.shape, s.dtype) for s in srcs], *[pltpu.HBM(s.shape, s.dtype) for s in lands]),
        in_specs=[_HBM_SPEC] * (2 * na) + [_SEM_SPEC] * 3 + [pl.BlockSpec(memory_space=pl.ANY)],
        out_specs=tuple([_HBM_SPEC] * (2 * na)),
        input_output_aliases={i: i for i in range(2 * na)},
        compiler_params=pltpu.CompilerParams(has_side_effects=pltpu.SideEffectType.DATAFLOW_SIDE_EFFECTING),
    )(*srcs, *lands, send_sems, recv_sems, local_sems, after)
    return outs[na:]


def _forward_to_sibling(name, lands):
    na = len(lands)

    def body(*refs):
        ins, outs = refs[:na], refs[na:2 * na]
        send_sems, recv_sems = refs[2 * na:]
        x, y, c = lax.axis_index("x"), lax.axis_index("y"), lax.axis_index("c")
        sibling = (x, y, 1 - c)
        chips = [(1 - x, y), (x, 1 - y), (1 - x, 1 - y)]
        slot = lambda cx, cy, cc: 4 * cx + 2 * cy + cc
        sends = []
        for j, (cx, cy) in enumerate(chips):
            for a in range(na):
                cp = pltpu.make_async_remote_copy(
                    src_ref=ins[a].at[slot(cx, cy, c)], dst_ref=outs[a].at[slot(cx, cy, c)],
                    send_sem=send_sems.at[a, j], recv_sem=recv_sems.at[a, j], device_id=sibling,
                    device_id_type=pl.DeviceIdType.MESH)
                cp.start()
                sends.append(cp)
        for j, (cx, cy) in enumerate(chips):
            for a in range(na):
                pltpu.make_async_remote_copy(
                    src_ref=ins[a].at[slot(cx, cy, 1 - c)], dst_ref=outs[a].at[slot(cx, cy, 1 - c)],
                    send_sem=send_sems.at[a, j], recv_sem=recv_sems.at[a, j], device_id=sibling,
                    device_id_type=pl.DeviceIdType.MESH).wait_recv()
        for cp in sends:
            cp.wait_send()

    hbm = pl.BlockSpec(memory_space=pl.ANY)
    return pl.pallas_call(
        body, name=name,
        out_shape=tuple(jax.ShapeDtypeStruct(s.shape, s.dtype) for s in lands),
        in_specs=[hbm] * na, out_specs=(hbm,) * na,
        input_output_aliases={a: a for a in range(na)},
        scratch_shapes=[pltpu.SemaphoreType.DMA((na, 3)), pltpu.SemaphoreType.DMA((na, 3))],
    )(*lands)


def _prep_weights(w):
    e = CONV_WIDTH
    p = {}
    if 'a_w_in' in w:
        if w['a_w_in'].shape == (N_DEV, D_MODEL, 3 * e // N_DEV):
            p['a_w_in3'] = w['a_w_in']
        else:
            p['a_w_in3'] = w['a_w_in'].reshape(D_MODEL, N_DEV, 3 * e // N_DEV).transpose(1, 0, 2)
        p['a_b_in'] = w['a_b_in'].reshape(1, 3 * e)
        p['a_conv_w'] = jnp.pad(w['a_conv_w'].reshape(CONV_KERNEL, e), ((0, HALO - CONV_KERNEL), (0, 0)))
        p['a_conv_b'] = w['a_conv_b'].reshape(1, e)
        p['a_norm_g'] = w['a_norm_g'].reshape(1, e)
        p['a_norm_b'] = w['a_norm_b'].reshape(1, e)
        p['a_b_out'] = w['a_b_out'].reshape(1, D_MODEL)
        p['kv_norm_g'] = w['kv_norm_g'].reshape(1, KV_LORA_RANK)
        p['b_q_norm_g'] = w['b_q_norm_g'].reshape(1, Q_LORA_RANK)
        p['ln_g'] = w['ln_g']
        p['ln_b'] = w['ln_b']
    if 'a_w_out' in w:
        p['a_w_out'] = w['a_w_out'].reshape(e, D_MODEL)
        p['kv_w_down'] = jnp.pad(w['kv_w_down'], ((0, 0), (0, KV_LORA_RANK + LANES - w['kv_w_down'].shape[1])))
        p['kv_w_ukv'] = jnp.concatenate([w['kv_w_uk'], w['kv_w_uv']], axis=2).reshape(KV_LORA_RANK, N_HEADS * HEAD_PAD)
        if w['b_w_in'].ndim == 3 and w['b_w_in'].shape[:2] == (N_DEV, D_MODEL):
            p['b_w_in3'] = w['b_w_in']
        else:
            p['b_w_in3'] = w['b_w_in'].reshape(D_MODEL, N_DEV, -1).transpose(1, 0, 2)
        uq = w['b_w_uq'].reshape(Q_LORA_RANK, N_HEADS, QK_NOPE_DIM + QK_ROPE_DIM)
        p['b_w_uq'] = jnp.pad(uq, ((0, 0), (0, 0), (0, HEAD_PAD - uq.shape[2]))).reshape(Q_LORA_RANK, N_HEADS * HEAD_PAD)
        p['b_w_out'] = w['b_w_out'].reshape(N_HEADS * V_HEAD_DIM, D_MODEL)
    return p


def _local_step(x, positions, target, comm):
    t = x.shape[0]
    e = CONV_WIDTH
    freqs = ROPE_THETA ** (-jnp.arange(0, QK_ROPE_DIM, 2, dtype=F32) / QK_ROPE_DIM)
    freq_row = jnp.concatenate([freqs, freqs, jnp.zeros((LANES - QK_ROPE_DIM,), F32)]).reshape(1, LANES)
    rc, rs, xb = _rope_tables(positions.reshape(t, 1), freq_row, x, comm.start_token())
    p = comm.first_weights(rc)
    ln_g0, ln_b0 = p['ln_g'][0:1], p['ln_b'][0:1]
    ln_g1, ln_b1 = p['ln_g'][1:2], p['ln_b'][1:2]

    proj = _matmul(xb, p['a_w_in3'], bias=p['a_b_in'], name="a_in", b_blocked=True, bm=2048,
                   after=comm.first_after())
    u1, u2 = _conv_mid_fwd(proj, p['a_conv_w'], p['a_conv_b'], p['a_norm_g'], p['a_norm_b'])
    p = {**p, **comm.rest_weights(u2)}
    h1, h1b, xh1, rstd1 = _ln_res_fwd(x, u2, p['a_w_out'], p['a_b_out'], ln_g0, ln_b0)
    ckv, ckn, kr, kv = _kv_mid_fwd(h1b, p['kv_w_down'], p['kv_w_ukv'], p['kv_norm_g'], rc, rs)
    projb, cqn, b_w_in = _b_in_qnorm(h1b, p['b_w_in3'], p['b_q_norm_g'])
    qcat = _q_up_rope(cqn, p['b_w_uq'], rc, rs)
    o, o2, lse = _flash_fwd(qcat, kv, kr, projb)
    loss, dr2, dr2b, dg1, db1 = _final_ln_loss(h1, o2, p['b_w_out'], ln_g1, ln_b1, target)

    g = {}
    g['b_w_out'] = _matmul(o2, dr2b, trans_a=True, name="d_b_out", out_dtype=BF16, bm=512, bk=t)
    dob, dprojb = _gate_bwd(dr2b, p['b_w_out'], o, projb)
    dq2, dkv, dkr = _flash_bwd(qcat, kv, kr, dob, lse, o, rc, rs)
    duq = _matmul(cqn, dq2, trans_a=True, name="d_q_up", out_dtype=BF16, bk=t)
    dprojb, dgq = _q_norm_bwd(dq2, p['b_w_uq'], projb, p['b_q_norm_g'], dprojb)
    g['b_w_in'] = _matmul(h1b, dprojb, trans_a=True, name="d_b_in", bn=2 * dprojb.shape[1] // N_DEV, bk=t, out_dtype=BF16,
                          out_blocked=True, out_parts=2)
    dukv = _matmul(ckn, dkv, trans_a=True, name="d_kv_up", out_dtype=BF16, bk=t)
    dckv, dgkv = _kv_mid_bwd(dkv, p['kv_w_ukv'], ckv, dkr, p['kv_norm_g'], rc, rs)
    ddown = _matmul(h1b, dckv, trans_a=True, name="d_kv_down", out_dtype=BF16, bm=512, bk=t)
    g['b_w_out'] = g['b_w_out'].reshape(N_DEV, -1, D_MODEL)
    g['kv_w_down'] = ddown[:, :KV_LORA_RANK + QK_ROPE_DIM].reshape(N_DEV, -1, KV_LORA_RANK + QK_ROPE_DIM)
    dukv = dukv.reshape(KV_LORA_RANK, N_HEADS, HEAD_PAD)
    g['kv_w_uk'] = dukv[:, :, :QK_NOPE_DIM].reshape(N_DEV, -1, QK_NOPE_DIM)
    g['kv_w_uv'] = dukv[:, :, QK_NOPE_DIM:].reshape(N_DEV, -1, V_HEAD_DIM)
    g['b_w_uq'] = duq.reshape(Q_LORA_RANK, N_HEADS, HEAD_PAD)[:, :, :QK_NOPE_DIM + QK_ROPE_DIM].reshape(
        N_DEV, -1, QK_NOPE_DIM + QK_ROPE_DIM)
    sent1 = comm.send_group1(g)
    dr1, dr1b, dg0, db0, dba_out = _ln0_bwd(dr2, dprojb, b_w_in, dckv, p['kv_w_down'], xh1, rstd1, ln_g0, sent1)
    dw_out = _matmul(u2, dr1b, trans_a=True, name="d_a_out", out_dtype=BF16, bm=512, bk=t).reshape(N_DEV, -1, D_MODEL)
    sent1b = comm.send_group1b({'a_w_out': dw_out})
    du1, dproj, dng, dnb, dbz = _conv_mid_bwd_rows(u1, proj, dr1b, p['a_w_out'], p['a_norm_g'], p['a_norm_b'], sent1b)
    dproj, dcw, dcb, dbv = _conv_bwd(du1, proj, p['a_conv_w'], dproj)
    half = D_MODEL // 2
    dw_lo = _matmul(xb, dproj, trans_a=True, name="d_a_in_lo", out_dtype=BF16, bm=half, bn=3 * e // N_DEV, bk=t,
                    out_blocked=True, a_cols=(0, half))
    small = (dg0, dg1, db0, db1, dbv, dbz, dcb, dng, dnb, dba_out, dgkv, dgq, dcw, loss)
    sent2 = comm.send_group2(dw_lo, small)
    dw_hi = _matmul(xb, dproj, trans_a=True, name="d_a_in_hi", out_dtype=BF16, bm=half, bn=3 * e // N_DEV, bk=t,
                    out_blocked=True, after=sent2, a_cols=(half, half))
    sent3 = comm.send_group3(dw_hi)
    grad_x = _grad_x(dproj, p['a_w_in3'], dr1, sent3)
    return loss, grad_x


def _grad_x(dproj, w3, dr1, after):
    t, d = dr1.shape
    nblk, _, cb = w3.shape
    tm = min(t, 512)
    extra = [] if after is None else [after]

    def body(a_ref, w_ref, dr_ref, *rest):
        o_ref = rest[-1]
        acc = ALPHA * dr_ref[...]
        for k in range(nblk):
            acc = acc + lax.dot_general(a_ref[:, k * cb:(k + 1) * cb], w_ref[k], NT, preferred_element_type=F32)
        o_ref[...] = acc

    row = pl.BlockSpec((tm, d), lambda i: (i, 0))
    return pl.pallas_call(
        body, name="grad_x",
        out_shape=jax.ShapeDtypeStruct((t, d), F32),
        grid=(t // tm,),
        in_specs=[pl.BlockSpec((tm, nblk * cb), lambda i: (i, 0)),
                  pl.BlockSpec((nblk, d, cb), lambda i: (0, 0, 0), pipeline_mode=pl.Buffered(1)), row]
                 + [pl.BlockSpec(memory_space=pl.ANY)] * len(extra),
        out_specs=row,
        compiler_params=_cparams(("parallel",)),
    )(dproj, w3, dr1, *extra)


def _shard_2d(a):
    return a.reshape(-1, a.shape[-1])


def _gathered_to_full(name, blocks):
    if name in ('a_w_in', 'b_w_in'):
        return blocks
    if name == 'a_conv_w':
        return blocks.transpose(1, 0, 2).reshape(CONV_KERNEL, -1)
    if name in ('a_b_in', 'a_conv_b', 'a_norm_g', 'a_norm_b', 'a_b_out'):
        return blocks.reshape(-1)
    if name in ('kv_w_uk', 'kv_w_uv'):
        return blocks.reshape(KV_LORA_RANK, N_HEADS, -1)
    if name == 'b_w_uq':
        return blocks.reshape(Q_LORA_RANK, N_HEADS, -1)
    return blocks.reshape(-1, blocks.shape[-1])


def kernel(x, positions, ln_g, ln_b, a_w_in, a_b_in, a_conv_w, a_conv_b, a_norm_g, a_norm_b, a_w_out, a_b_out, kv_w_down, kv_norm_g, kv_w_uk, kv_w_uv, b_w_in, b_q_norm_g, b_w_uq, b_w_out, loss_target, m_ln_g, m_ln_b, m_a_w_in, m_a_b_in, m_a_conv_w, m_a_conv_b, m_a_norm_g, m_a_norm_b, m_a_w_out, m_a_b_out, m_kv_w_down, m_kv_norm_g, m_kv_w_uk, m_kv_w_uv, m_b_w_in, m_b_q_norm_g, m_b_w_uq, m_b_w_out, v_ln_g, v_ln_b, v_a_w_in, v_a_b_in, v_a_conv_w, v_a_conv_b, v_a_norm_g, v_a_norm_b, v_a_w_out, v_a_b_out, v_kv_w_down, v_kv_norm_g, v_kv_w_uk, v_kv_w_uv, v_b_w_in, v_b_q_norm_g, v_b_w_uq, v_b_w_out):
    w = dict(ln_g=ln_g, ln_b=ln_b, a_w_in=a_w_in, a_b_in=a_b_in, a_conv_w=a_conv_w, a_conv_b=a_conv_b,
             a_norm_g=a_norm_g, a_norm_b=a_norm_b, a_w_out=a_w_out, a_b_out=a_b_out, kv_w_down=kv_w_down,
             kv_norm_g=kv_norm_g, kv_w_uk=kv_w_uk, kv_w_uv=kv_w_uv, b_w_in=b_w_in, b_q_norm_g=b_q_norm_g,
             b_w_uq=b_w_uq, b_w_out=b_w_out)
    m = dict(ln_g=m_ln_g, ln_b=m_ln_b, a_w_in=m_a_w_in, a_b_in=m_a_b_in, a_conv_w=m_a_conv_w, a_conv_b=m_a_conv_b,
             a_norm_g=m_a_norm_g, a_norm_b=m_a_norm_b, a_w_out=m_a_w_out, a_b_out=m_a_b_out, kv_w_down=m_kv_w_down,
             kv_norm_g=m_kv_norm_g, kv_w_uk=m_kv_w_uk, kv_w_uv=m_kv_w_uv, b_w_in=m_b_w_in, b_q_norm_g=m_b_q_norm_g,
             b_w_uq=m_b_w_uq, b_w_out=m_b_w_out)
    v = dict(ln_g=v_ln_g, ln_b=v_ln_b, a_w_in=v_a_w_in, a_b_in=v_a_b_in, a_conv_w=v_a_conv_w, a_conv_b=v_a_conv_b,
             a_norm_g=v_a_norm_g, a_norm_b=v_a_norm_b, a_w_out=v_a_w_out, a_b_out=v_a_b_out, kv_w_down=v_kv_w_down,
             kv_norm_g=v_kv_norm_g, kv_w_uk=v_kv_w_uk, kv_w_uv=v_kv_w_uv, b_w_in=v_b_w_in, b_q_norm_g=v_b_q_norm_g,
             b_w_uq=v_b_w_uq, b_w_out=v_b_w_out)

    small_flat = jnp.concatenate([w[n].reshape(-1) for n in SMALL_WEIGHTS]).reshape(1, -1)
    rest_names = [n for n in MATMUL_WEIGHTS if n != 'a_w_in']
    group1 = ('b_w_out', 'b_w_uq', 'b_w_in', 'kv_w_uk', 'kv_w_uv', 'kv_w_down')
    group1b = ('a_w_out',)

    class Comm:
        def __init__(self):
            self.first, self.first_token = _comm_start(
                "gather_first_start", [_shard_2d(w['a_w_in']).astype(BF16), small_flat], True, None, CHIP_LEVEL_PEERS)

        def start_token(self):
            return self.first_token

        def first_weights(self, after):
            lands = _comm_wait("gather_first_wait", self.first, True, after, CHIP_LEVEL_PEERS)
            ga = _forward_to_sibling("gather_first_forward", list(lands))
            full = {'a_w_in': ga[0]}
            gs = ga[1].reshape(N_DEV, -1)
            off = 0
            for n in SMALL_WEIGHTS:
                size = math.prod(w[n].shape)
                full[n] = _gathered_to_full(n, gs[:, off:off + size].reshape((N_DEV,) + _shard_2d(w[n]).shape))
                off += size
            for n in REPLICATED:
                full[n] = w[n]
            dense = lambda a: a if a.shape[-1] % LANES == 0 else a.reshape(-1, LANES)
            self.rest, self.rest_token = _comm_start(
                "gather_rest_start", [dense(_shard_2d(w[n]).astype(BF16)) for n in rest_names], True, ga[0])
            return _prep_weights(full)

        def first_after(self):
            return self.rest_token

        def rest_weights(self, after):
            lands = _comm_wait("gather_rest_wait", self.rest, True, after)
            blocks = {n: lands[i].reshape((N_DEV,) + _shard_2d(w[n]).shape) for i, n in enumerate(rest_names)}
            return _prep_weights({n: _gathered_to_full(n, blocks[n]) for n in rest_names})

        def send_group1(self, g):
            self.g1, token = _comm_start("exchange_g1_start", [g[n] for n in group1], False, None)
            return token

        def send_group1b(self, g):
            self.g1b, token = _comm_start("exchange_g1b_start", [g[n] for n in group1b], False, None)
            return token

        def send_group2(self, dw_lo, small):
            self.g2, token = _comm_start("exchange_g2_start", [dw_lo, _pack_small_grads(*small)], False, None)
            return token

        def send_group3(self, dw_hi):
            self.g3, token = _comm_start("exchange_g3_start", [dw_hi], False, None)
            return token

    comm = Comm()

    _, grad_x = _local_step(x[0], positions[0], loss_target[0], comm)

    res = {}
    recv1 = _comm_wait("exchange_g1_wait", comm.g1, False, grad_x)
    for i, n in enumerate(group1):
        res[n] = _adamw(n, recv1[i], _shard_2d(w[n]), _shard_2d(m[n]), _shard_2d(v[n]))
    recv1b = _comm_wait("exchange_g1b_wait", comm.g1b, False, res[group1[-1]][0])
    for i, n in enumerate(group1b):
        res[n] = _adamw(n, recv1b[i], _shard_2d(w[n]), _shard_2d(m[n]), _shard_2d(v[n]))
    recv2 = _comm_wait("exchange_g2_wait", comm.g2, False, res[group1b[-1]][0])
    recv3 = _comm_wait("exchange_g3_wait", comm.g3, False, recv2[0])
    res['a_w_in'] = _adamw('a_w_in', [recv2[0], recv3[0]], _shard_2d(w['a_w_in']), _shard_2d(m['a_w_in']),
                           _shard_2d(v['a_w_in']))
    two_d = lambda d: [_shard_2d(d[n]) if d[n].ndim > 1 else d[n].reshape(1, -1) for n in SMALL_NAMES]
    sg, sd, sm, sv, loss = _adamw_small(recv2[-1], two_d(w), two_d(m), two_d(v))
    for i, n in enumerate(SMALL_NAMES):
        res[n] = (sg[i], sd[i], sm[i], sv[i])
    outs = [[res[n][j].reshape(w[n].shape) for n in WEIGHT_NAMES] for j in range(4)]
    return (loss[0, 0], grad_x[None], *outs[0], *outs[1], *outs[2], *outs[3])
```
